```python
import jax, jax.numpy as jnp
from jax import lax
import numpy as np

D_MODEL = 1024
BATCH = 8
SEQ = 8192
DEPTH = 1

N_META = 16
D_MIX = D_MODEL
GLA_HEADS = 4
GLA_DK = D_MIX // 4 // GLA_HEADS
GLA_DV = D_MIX // 2 // GLA_HEADS
GLA_RANK = 16
GLA_TAU = 16.0
GLA_CHUNK = 64
GLA_PAD = GLA_CHUNK - N_META
SWA_HEADS = 8
SWA_KV_HEADS = 2
SWA_GROUP = SWA_HEADS // SWA_KV_HEADS
SWA_HD = D_MIX // 2 // SWA_HEADS
SWA_WINDOW = 128
SWA_BLOCK = 128
ROPE_DIM = SWA_HD // 4
ROPE_THETA = 500000.0
D_FF = 4 * D_MODEL
EPS = 1e-5

IN_SIZES = (GLA_HEADS * GLA_DK,
            GLA_HEADS * GLA_DK,
            GLA_HEADS * GLA_DV,
            GLA_HEADS * GLA_DV,
            GLA_RANK,
            SWA_HEADS * SWA_HD,
            SWA_KV_HEADS * SWA_HD,
            SWA_KV_HEADS * SWA_HD)
D_IN = sum(IN_SIZES)

kernel_name = "hybrid_gla_swa_sink_meta_layer"


def rmsnorm(x, w):
    xf = x.astype(jnp.float32)
    y = xf * lax.rsqrt(jnp.mean(jnp.square(xf), axis=-1, keepdims=True) + EPS)
    return (y * w.astype(jnp.float32)).astype(x.dtype)


def partial_rope(x, pos):
    inv_freq = 1.0 / (ROPE_THETA ** (jnp.arange(0, ROPE_DIM, 2, dtype=jnp.float32) / ROPE_DIM))
    ang = pos.astype(jnp.float32)[:, None] * inv_freq[None, :]
    ang = jnp.concatenate([ang, ang], axis=-1)[:, None, :]
    cos, sin = jnp.cos(ang), jnp.sin(ang)
    xr = x[..., :ROPE_DIM].astype(jnp.float32)
    half = ROPE_DIM // 2
    rot = jnp.concatenate([-xr[..., half:], xr[..., :half]], axis=-1)
    xr = (xr * cos + rot * sin).astype(x.dtype)
    return jnp.concatenate([xr, x[..., ROPE_DIM:]], axis=-1)


def gla_chunk_step(S, inp):
    q, k, v, g = inp
    b = jnp.cumsum(g, axis=2)
    causal = jnp.tril(jnp.ones((GLA_CHUNK, GLA_CHUNK), dtype=bool))
    diff = b[:, :, :, None, :] - b[:, :, None, :, :]
    decay = jnp.exp(jnp.where(causal[None, None, :, :, None], diff, -jnp.inf))
    A = jnp.einsum('bhid,bhjd,bhijd->bhij', q, k, decay)
    o = (jnp.einsum('bhij,bhjv->bhiv', A, v)
         + jnp.einsum('bhid,bhdv->bhiv', q * jnp.exp(b), S))
    b_last = b[:, :, -1:, :]
    S = (jnp.exp(b_last[:, :, 0, :])[..., None] * S
         + jnp.einsum('bhjd,bhjv->bhdv', k * jnp.exp(b_last - b), v))
    return S, o


def gla_mixer(q, k, v, r, lr, w_gate_up, b_gate, gla_norm_w):
    B, L, _ = q.shape
    dtype = q.dtype
    g = jax.nn.log_sigmoid((lr @ w_gate_up + b_gate).astype(jnp.float32)) / GLA_TAU
    q = q.astype(jnp.float32).reshape(B, L, GLA_HEADS, GLA_DK) * (GLA_DK ** -0.5)
    k = k.astype(jnp.float32).reshape(B, L, GLA_HEADS, GLA_DK)
    v = v.astype(jnp.float32).reshape(B, L, GLA_HEADS, GLA_DV)
    g = g.reshape(B, L, GLA_HEADS, GLA_DK)
    pad = ((0, 0), (GLA_PAD, 0), (0, 0), (0, 0))
    Lp = L + GLA_PAD
    n_chunks = Lp // GLA_CHUNK

    def to_chunks(t):
        t = jnp.pad(t, pad).reshape(B, n_chunks, GLA_CHUNK, GLA_HEADS, t.shape[-1])
        return t.transpose(1, 0, 3, 2, 4)

    S0 = jnp.zeros((B, GLA_HEADS, GLA_DK, GLA_DV), jnp.float32)
    _, o = lax.scan(gla_chunk_step, S0, (to_chunks(q), to_chunks(k), to_chunks(v), to_chunks(g)))
    o = o.transpose(1, 0, 3, 2, 4).reshape(B, Lp, GLA_HEADS, GLA_DV)[:, GLA_PAD:]
    o = rmsnorm(o.astype(dtype), gla_norm_w)
    o = o * jax.nn.silu(r.reshape(B, L, GLA_HEADS, GLA_DV))
    return o.reshape(B, L, GLA_HEADS * GLA_DV)


def sink_softmax(scores, sink):
    sink_b = jnp.broadcast_to(sink, scores.shape[:-1] + (1,))
    p = jax.nn.softmax(jnp.concatenate([scores, sink_b], axis=-1), axis=-1)
    return p[..., :-1]


def swa_mixer(q, k, v, sinks, pos):
    B, L, _ = q.shape
    dtype = q.dtype
    q = partial_rope(q.reshape(B, L, SWA_HEADS, SWA_HD), pos) * (SWA_HD ** -0.5)
    k = partial_rope(k.reshape(B, L, SWA_KV_HEADS, SWA_HD), pos)
    v = v.reshape(B, L, SWA_KV_HEADS, SWA_HD)
    q = q.reshape(B, L, SWA_KV_HEADS, SWA_GROUP, SWA_HD)
    qm, qr = q[:, :N_META], q[:, N_META:]
    km, kr = k[:, :N_META], k[:, N_META:]
    vm, vr = v[:, :N_META], v[:, N_META:]
    sink = sinks.astype(jnp.float32).reshape(SWA_KV_HEADS, SWA_GROUP)

    sm = jnp.einsum('bqkgd,bjkd->bkgqj', qm, km).astype(jnp.float32)
    mmask = jnp.tril(jnp.ones((N_META, N_META), dtype=bool))
    pm = sink_softmax(jnp.where(mmask, sm, -jnp.inf), sink[None, :, :, None, None])
    om = jnp.einsum('bkgqj,bjkd->bqkgd', pm.astype(dtype), vm).reshape(B, N_META, SWA_HEADS * SWA_HD)

    S = L - N_META
    nb = S // SWA_BLOCK
    qb = qr.reshape(B, nb, SWA_BLOCK, SWA_KV_HEADS, SWA_GROUP, SWA_HD)

    def band(t):
        cur = t.reshape(B, nb, SWA_BLOCK, SWA_KV_HEADS, SWA_HD)
        prev = jnp.pad(t, ((0, 0), (SWA_BLOCK, 0), (0, 0), (0, 0)))[:, :S]
        prev = prev.reshape(B, nb, SWA_BLOCK, SWA_KV_HEADS, SWA_HD)
        return jnp.concatenate([prev, cur], axis=2)

    kw, vw = band(kr), band(vr)
    s_meta = jnp.einsum('bnqkgd,bjkd->bnkgqj', qb, km).astype(jnp.float32)
    s_win = jnp.einsum('bnqkgd,bnjkd->bnkgqj', qb, kw).astype(jnp.float32)
    rr = jnp.arange(SWA_BLOCK)[:, None]
    jj = jnp.arange(2 * SWA_BLOCK)[None, :]
    dist = SWA_BLOCK + rr - jj
    in_band = (dist >= 0) & (dist < SWA_WINDOW)
    blk = jnp.arange(nb)[:, None, None]
    wmask = in_band[None] & ((blk > 0) | (jj[None] >= SWA_BLOCK))
    s_win = jnp.where(wmask[None, :, None, None], s_win, -jnp.inf)
    p = sink_softmax(jnp.concatenate([s_meta, s_win], axis=-1),
                     sink[None, None, :, :, None, None]).astype(dtype)
    orr = (jnp.einsum('bnkgqj,bjkd->bnqkgd', p[..., :N_META], vm)
           + jnp.einsum('bnkgqj,bnjkd->bnqkgd', p[..., N_META:], vw))
    orr = orr.reshape(B, S, SWA_HEADS * SWA_HD)
    return jnp.concatenate([om, orr], axis=1)


def hybrid_layer(h, pos, norm_mix_w, w_in, w_gate_up, b_gate, gla_norm_w, sinks,
                 w_out, norm_ff_w, w_ff1, w_ff2):
    u = rmsnorm(h, norm_mix_w)
    proj = u @ w_in
    split_points = np.cumsum(IN_SIZES)[:-1].tolist()
    gq, gk, gv, gr, glr, sq, sk, sv = jnp.split(proj, split_points, axis=-1)
    o_gla = gla_mixer(gq, gk, gv, gr, glr, w_gate_up, b_gate, gla_norm_w)
    o_swa = swa_mixer(sq, sk, sv, sinks, pos)
    h = h + jnp.concatenate([o_gla, o_swa], axis=-1) @ w_out
    f = rmsnorm(h, norm_ff_w)
    return h + jnp.square(jax.nn.relu(f @ w_ff1)) @ w_ff2


def _fwd_setup_inputs(seed: int = 0) -> dict:
    key = jax.random.key(seed)
    ks = jax.random.split(key, 14)
    f32 = jnp.float32
    nrm = lambda k, shape, s: jax.random.normal(k, shape, f32) * s
    return {
        "x": nrm(ks[0], (BATCH, SEQ, D_MODEL), 1.0),
        "meta_tokens": nrm(ks[1], (N_META, D_MODEL), 1.0),
        "norm_mix_w": 1.0 + nrm(ks[2], (DEPTH, D_MODEL), 0.02),
        "w_in": nrm(ks[3], (DEPTH, D_MODEL, D_IN), D_MODEL ** -0.5),
        "w_gate_up": nrm(ks[4], (DEPTH, GLA_RANK, GLA_HEADS * GLA_DK), GLA_RANK ** -0.5),
        "b_gate": nrm(ks[5], (DEPTH, GLA_HEADS * GLA_DK), 0.1),
        "gla_norm_w": 1.0 + nrm(ks[6], (DEPTH, GLA_DV), 0.02),
        "sinks": nrm(ks[7], (DEPTH, SWA_HEADS), 1.0),
        "w_out": nrm(ks[8], (DEPTH, D_MIX, D_MODEL), D_MIX ** -0.5),
        "norm_ff_w": 1.0 + nrm(ks[9], (DEPTH, D_MODEL), 0.02),
        "w_ff1": nrm(ks[10], (DEPTH, D_MODEL, D_FF), D_MODEL ** -0.5),
        "w_ff2": nrm(ks[11], (DEPTH, D_FF, D_MODEL), D_FF ** -0.5),
        "final_norm_w": 1.0 + nrm(ks[12], (D_MODEL,), 0.02),
    }


def _fwd_reference(x, meta_tokens, norm_mix_w, w_in, w_gate_up, b_gate, gla_norm_w, sinks,
              w_out, norm_ff_w, w_ff1, w_ff2, final_norm_w):
    B = x.shape[0]
    meta = jnp.broadcast_to(meta_tokens[None].astype(x.dtype), (B, N_META, D_MODEL))
    h = jnp.concatenate([meta, x], axis=1)
    pos = jnp.arange(h.shape[1], dtype=jnp.int32)
    for layer in range(DEPTH):
        h = hybrid_layer(h, pos, norm_mix_w[layer], w_in[layer], w_gate_up[layer], b_gate[layer],
                         gla_norm_w[layer], sinks[layer], w_out[layer], norm_ff_w[layer],
                         w_ff1[layer], w_ff2[layer])
    return rmsnorm(h, final_norm_w)[:, N_META:]


import jax as _jax
import jax.numpy as _jnp

TWIN_FORMAT = 'train_step'
FWD_PARAMS = ['x', 'meta_tokens', 'norm_mix_w', 'w_in', 'w_gate_up', 'b_gate', 'gla_norm_w', 'sinks', 'w_out', 'norm_ff_w', 'w_ff1', 'w_ff2', 'final_norm_w']
TWIN_WEIGHTS = ['meta_tokens', 'norm_mix_w', 'w_in', 'w_gate_up', 'b_gate', 'gla_norm_w', 'sinks', 'w_out', 'norm_ff_w', 'w_ff1', 'w_ff2', 'final_norm_w']
TWIN_DIFF_INPUT = 'x'
TWIN_INPUTS = ['x', 'meta_tokens', 'norm_mix_w', 'w_in', 'w_gate_up', 'b_gate', 'gla_norm_w', 'sinks', 'w_out', 'norm_ff_w', 'w_ff1', 'w_ff2', 'final_norm_w', 'loss_target', 'm_meta_tokens', 'm_norm_mix_w', 'm_w_in', 'm_w_gate_up', 'm_b_gate', 'm_gla_norm_w', 'm_sinks', 'm_w_out', 'm_norm_ff_w', 'm_w_ff1', 'm_w_ff2', 'm_final_norm_w', 'v_meta_tokens', 'v_norm_mix_w', 'v_w_in', 'v_w_gate_up', 'v_b_gate', 'v_gla_norm_w', 'v_sinks', 'v_w_out', 'v_norm_ff_w', 'v_w_ff1', 'v_w_ff2', 'v_final_norm_w']
TWIN_OUTPUTS = ['loss', 'grad_x', 'grad_meta_tokens', 'grad_norm_mix_w', 'grad_w_in', 'grad_w_gate_up', 'grad_b_gate', 'grad_gla_norm_w', 'grad_sinks', 'grad_w_out', 'grad_norm_ff_w', 'grad_w_ff1', 'grad_w_ff2', 'grad_final_norm_w', 'delta_meta_tokens', 'delta_norm_mix_w', 'delta_w_in', 'delta_w_gate_up', 'delta_b_gate', 'delta_gla_norm_w', 'delta_sinks', 'delta_w_out', 'delta_norm_ff_w', 'delta_w_ff1', 'delta_w_ff2', 'delta_final_norm_w', 'new_m_meta_tokens', 'new_m_norm_mix_w', 'new_m_w_in', 'new_m_w_gate_up', 'new_m_b_gate', 'new_m_gla_norm_w', 'new_m_sinks', 'new_m_w_out', 'new_m_norm_ff_w', 'new_m_w_ff1', 'new_m_w_ff2', 'new_m_final_norm_w', 'new_v_meta_tokens', 'new_v_norm_mix_w', 'new_v_w_in', 'new_v_w_gate_up', 'new_v_b_gate', 'new_v_gla_norm_w', 'new_v_sinks', 'new_v_w_out', 'new_v_norm_ff_w', 'new_v_w_ff1', 'new_v_w_ff2', 'new_v_final_norm_w']
TWIN_LEAF_KINDS = {'loss': 'loss', 'grad_x': 'grad_x', 'grad_meta_tokens': 'grad_w', 'grad_norm_mix_w': 'grad_w', 'grad_w_in': 'grad_w', 'grad_w_gate_up': 'grad_w', 'grad_b_gate': 'grad_w', 'grad_gla_norm_w': 'grad_w', 'grad_sinks': 'grad_w', 'grad_w_out': 'grad_w', 'grad_norm_ff_w': 'grad_w', 'grad_w_ff1': 'grad_w', 'grad_w_ff2': 'grad_w', 'grad_final_norm_w': 'grad_w', 'delta_meta_tokens': 'delta_w', 'delta_norm_mix_w': 'delta_w', 'delta_w_in': 'delta_w', 'delta_w_gate_up': 'delta_w', 'delta_b_gate': 'delta_w', 'delta_gla_norm_w': 'delta_w', 'delta_sinks': 'delta_w', 'delta_w_out': 'delta_w', 'delta_norm_ff_w': 'delta_w', 'delta_w_ff1': 'delta_w', 'delta_w_ff2': 'delta_w', 'delta_final_norm_w': 'delta_w', 'new_m_meta_tokens': 'new_m', 'new_m_norm_mix_w': 'new_m', 'new_m_w_in': 'new_m', 'new_m_w_gate_up': 'new_m', 'new_m_b_gate': 'new_m', 'new_m_gla_norm_w': 'new_m', 'new_m_sinks': 'new_m', 'new_m_w_out': 'new_m', 'new_m_norm_ff_w': 'new_m', 'new_m_w_ff1': 'new_m', 'new_m_w_ff2': 'new_m', 'new_m_final_norm_w': 'new_m', 'new_v_meta_tokens': 'new_v', 'new_v_norm_mix_w': 'new_v', 'new_v_w_in': 'new_v', 'new_v_w_gate_up': 'new_v', 'new_v_b_gate': 'new_v', 'new_v_gla_norm_w': 'new_v', 'new_v_sinks': 'new_v', 'new_v_w_out': 'new_v', 'new_v_norm_ff_w': 'new_v', 'new_v_w_ff1': 'new_v', 'new_v_w_ff2': 'new_v', 'new_v_final_norm_w': 'new_v'}


def _forward(args):
    return _fwd_reference(*[args[k] for k in FWD_PARAMS])


def _output_shape():
    def fwd():
        inp = _fwd_setup_inputs(0)
        return _fwd_reference(*[inp[k] for k in FWD_PARAMS])
    out = _jax.eval_shape(fwd)
    return out.shape, out.dtype

N_MICROBATCH = 1
ADAM_LR = 0.001
ADAM_B1 = 0.9
ADAM_B2 = 0.999
ADAM_EPS = 1e-08
ADAM_WD = 0.01
ADAM_STEP = 10
PER_EXAMPLE_BATCH_AXIS = {'x': 0, 'loss_target': 0}
SHARED_INPUTS = []
_WEIGHT_DTYPES = {'meta_tokens': _jnp.float32, 'norm_mix_w': _jnp.float32, 'w_in': _jnp.float32, 'w_gate_up': _jnp.float32, 'b_gate': _jnp.float32, 'gla_norm_w': _jnp.float32, 'sinks': _jnp.float32, 'w_out': _jnp.float32, 'norm_ff_w': _jnp.float32, 'w_ff1': _jnp.float32, 'w_ff2': _jnp.float32, 'final_norm_w': _jnp.float32}
MOMENT_SCALE = {'meta_tokens': 8.977068e-03, 'norm_mix_w': 2.327164e-01, 'w_in': 1.480693e-01, 'w_gate_up': 2.466490e-02, 'b_gate': 1.008869e-01, 'gla_norm_w': 2.923689e-01, 'sinks': 5.625274e-03, 'w_out': 1.099082e-01, 'norm_ff_w': 2.350713e-01, 'w_ff1': 1.088209e-01, 'w_ff2': 2.286894e-01, 'final_norm_w': 6.465999e+01}


def _to_microbatches(a, axis):
    t = _jnp.moveaxis(a, axis, 0)
    t = t.reshape((N_MICROBATCH, t.shape[0] // N_MICROBATCH) + t.shape[1:])
    return _jnp.moveaxis(t, 1, axis + 1)


def setup_inputs(seed: int = 0) -> dict:
    inp = _fwd_setup_inputs(seed)
    key = _jax.random.fold_in(_jax.random.key(seed), 7919)
    shape, _ = _output_shape()
    out = dict(inp)
    out["loss_target"] = _jax.random.normal(_jax.random.fold_in(key, 0), shape, _jnp.float32)
    for i, name in enumerate(TWIN_WEIGHTS):
        w = inp[name].astype(_jnp.float32)
        if MOMENT_SCALE is None:
            s = _jnp.sqrt(_jnp.mean(_jnp.square(w)) + 1e-30)
        else:
            s = MOMENT_SCALE[name]
        km, kv = _jax.random.split(_jax.random.fold_in(key, i + 1))
        out[name] = w
        out["m_" + name] = s * _jax.random.normal(km, w.shape, _jnp.float32)
        out["v_" + name] = (s * s) * _jax.random.uniform(kv, w.shape, _jnp.float32, 0.5, 1.5)
    if N_MICROBATCH > 1:
        for name, axis in PER_EXAMPLE_BATCH_AXIS.items():
            out[name] = _to_microbatches(out[name], axis)
    return {'x': out['x'], 'meta_tokens': out['meta_tokens'], 'norm_mix_w': out['norm_mix_w'], 'w_in': out['w_in'], 'w_gate_up': out['w_gate_up'], 'b_gate': out['b_gate'], 'gla_norm_w': out['gla_norm_w'], 'sinks': out['sinks'], 'w_out': out['w_out'], 'norm_ff_w': out['norm_ff_w'], 'w_ff1': out['w_ff1'], 'w_ff2': out['w_ff2'], 'final_norm_w': out['final_norm_w'], 'loss_target': out['loss_target'], 'm_meta_tokens': out['m_meta_tokens'], 'm_norm_mix_w': out['m_norm_mix_w'], 'm_w_in': out['m_w_in'], 'm_w_gate_up': out['m_w_gate_up'], 'm_b_gate': out['m_b_gate'], 'm_gla_norm_w': out['m_gla_norm_w'], 'm_sinks': out['m_sinks'], 'm_w_out': out['m_w_out'], 'm_norm_ff_w': out['m_norm_ff_w'], 'm_w_ff1': out['m_w_ff1'], 'm_w_ff2': out['m_w_ff2'], 'm_final_norm_w': out['m_final_norm_w'], 'v_meta_tokens': out['v_meta_tokens'], 'v_norm_mix_w': out['v_norm_mix_w'], 'v_w_in': out['v_w_in'], 'v_w_gate_up': out['v_w_gate_up'], 'v_b_gate': out['v_b_gate'], 'v_gla_norm_w': out['v_gla_norm_w'], 'v_sinks': out['v_sinks'], 'v_w_out': out['v_w_out'], 'v_norm_ff_w': out['v_norm_ff_w'], 'v_w_ff1': out['v_w_ff1'], 'v_w_ff2': out['v_w_ff2'], 'v_final_norm_w': out['v_final_norm_w']}


def _loss(weights, diff, rest, loss_target):
    with _jax.named_scope("forward"):
        args = {**rest, TWIN_DIFF_INPUT: diff, **{k: w.astype(_WEIGHT_DTYPES[k]) for k, w in weights.items()}}
        y = _forward(args)
    with _jax.named_scope("loss_head"):
        err = _jnp.square(y.astype(_jnp.float32) - loss_target)
        return 0.5 * _jnp.sum(_jnp.mean(err, axis=-1)) if err.ndim else 0.5 * err


def _adamw(w, g, m, v):
    m = ADAM_B1 * m + (1.0 - ADAM_B1) * g
    v = ADAM_B2 * v + (1.0 - ADAM_B2) * _jnp.square(g)
    m_hat = m / (1.0 - ADAM_B1 ** ADAM_STEP)
    v_hat = v / (1.0 - ADAM_B2 ** ADAM_STEP)
    delta = -ADAM_LR * (m_hat / (_jnp.sqrt(v_hat) + ADAM_EPS) + ADAM_WD * w)
    return delta, m, v


def reference(x, meta_tokens, norm_mix_w, w_in, w_gate_up, b_gate, gla_norm_w, sinks, w_out, norm_ff_w, w_ff1, w_ff2, final_norm_w, loss_target, m_meta_tokens, m_norm_mix_w, m_w_in, m_w_gate_up, m_b_gate, m_gla_norm_w, m_sinks, m_w_out, m_norm_ff_w, m_w_ff1, m_w_ff2, m_final_norm_w, v_meta_tokens, v_norm_mix_w, v_w_in, v_w_gate_up, v_b_gate, v_gla_norm_w, v_sinks, v_w_out, v_norm_ff_w, v_w_ff1, v_w_ff2, v_final_norm_w):
    given = dict(x=x, meta_tokens=meta_tokens, norm_mix_w=norm_mix_w, w_in=w_in, w_gate_up=w_gate_up, b_gate=b_gate, gla_norm_w=gla_norm_w, sinks=sinks, w_out=w_out, norm_ff_w=norm_ff_w, w_ff1=w_ff1, w_ff2=w_ff2, final_norm_w=final_norm_w, loss_target=loss_target, m_meta_tokens=m_meta_tokens, m_norm_mix_w=m_norm_mix_w, m_w_in=m_w_in, m_w_gate_up=m_w_gate_up, m_b_gate=m_b_gate, m_gla_norm_w=m_gla_norm_w, m_sinks=m_sinks, m_w_out=m_w_out, m_norm_ff_w=m_norm_ff_w, m_w_ff1=m_w_ff1, m_w_ff2=m_w_ff2, m_final_norm_w=m_final_norm_w, v_meta_tokens=v_meta_tokens, v_norm_mix_w=v_norm_mix_w, v_w_in=v_w_in, v_w_gate_up=v_w_gate_up, v_b_gate=v_b_gate, v_gla_norm_w=v_gla_norm_w, v_sinks=v_sinks, v_w_out=v_w_out, v_norm_ff_w=v_norm_ff_w, v_w_ff1=v_w_ff1, v_w_ff2=v_w_ff2, v_final_norm_w=v_final_norm_w)
    weights = {n: given[n] for n in TWIN_WEIGHTS}
    shared = {n: given[n] for n in SHARED_INPUTS}
    per_example = {n: given[n] for n in ['x']}
    grad_fn = _jax.value_and_grad(_loss, argnums=(0, 1))

    def one_microbatch(ex, loss_target):
        ex = dict(ex)
        diff = ex.pop(TWIN_DIFF_INPUT)
        return grad_fn(weights, diff, {**shared, **ex}, loss_target)

    if N_MICROBATCH == 1:
        loss, (grad_w, grad_x) = one_microbatch(per_example, given["loss_target"])
    else:
        def body(carry, xs):
            loss_sum, grad_sum = carry
            l_k, (gw_k, gx_k) = one_microbatch(xs[0], xs[1])
            with _jax.named_scope("update"):
                return (loss_sum + l_k, _jax.tree.map(_jnp.add, grad_sum, gw_k)), gx_k

        init = (_jnp.zeros((), _jnp.float32), _jax.tree.map(_jnp.zeros_like, weights))
        (loss, grad_w), grad_x = _jax.lax.scan(body, init, (per_example, given["loss_target"]))
    with _jax.named_scope("update"):
        delta_w, new_m, new_v = {}, {}, {}
        for n in TWIN_WEIGHTS:
            delta_w[n], new_m[n], new_v[n] = _adamw(weights[n], grad_w[n], given["m_" + n], given["v_" + n])
    return (loss, grad_x, *[grad_w[n] for n in TWIN_WEIGHTS], *[delta_w[n] for n in TWIN_WEIGHTS],
            *[new_m[n] for n in TWIN_WEIGHTS], *[new_v[n] for n in TWIN_WEIGHTS])
```

```python
import jax
import jax.numpy as jnp
from jax import lax
from jax.experimental import pallas as pl
from jax.experimental.pallas import tpu as pltpu

F32 = jnp.float32
MXU_DTYPE = jnp.bfloat16
ACT_DTYPE = jnp.bfloat16
WIRE_DTYPE = jnp.bfloat16

D = 1024
N_META = 16
LEAD = 128
META0 = LEAD - N_META
EPS = 1e-5
GLA_HEADS, GLA_DK, GLA_DV, GLA_RANK, GLA_CHUNK = 4, 64, 128, 16, 64
GLA_TAU = 16.0
SWA_HEADS, SWA_KV, SWA_GROUP, SWA_HD, SWA_BLOCK = 8, 2, 4, 64, 128
ROPE_DIM, ROPE_THETA = 16, 500000.0
D_FF = 4096
N_DEV = 8
FF_TILE = D_FF // N_DEV
NEG = -1e30

C_GV, C_GR, C_SQ, C_GQ, C_GK, C_SK, C_SV, C_LR = 0, 512, 1024, 1536, 1792, 2048, 2176, 2304
DINP = 2432
DIN = 2320
O_GQ, O_GK, O_GV, O_GR, O_LR, O_SQ, O_SK, O_SV = (0, 256), (256, 512), (512, 1024), (1024, 1536), (1536, 1552), (1552, 2064), (2064, 2192), (2192, 2320)

ADAM_LR, ADAM_B1, ADAM_B2, ADAM_EPS, ADAM_WD, ADAM_STEP = 0.001, 0.9, 0.999, 1e-08, 0.01, 10

MESH = pl.DeviceIdType.MESH
ANY = pl.BlockSpec(memory_space=pl.ANY)
HIGHEST = lax.Precision.HIGHEST


def _cp(sem=None, vmem_mb=None):
    kw = {}
    if sem is not None:
        kw["dimension_semantics"] = sem
    if vmem_mb is not None:
        kw["vmem_limit_bytes"] = vmem_mb << 20
    return pltpu.CompilerParams(**kw)


def _mm(a, b):
    return jnp.dot(a.astype(MXU_DTYPE), b.astype(MXU_DTYPE), preferred_element_type=F32)


def _mm_nt(a, b):
    return lax.dot_general(a.astype(MXU_DTYPE), b.astype(MXU_DTYPE), (((1,), (1,)), ((), ())),
                           preferred_element_type=F32)


def _mm_tn(a, b):
    return lax.dot_general(a.astype(MXU_DTYPE), b.astype(MXU_DTYPE), (((0,), (0,)), ((), ())),
                           preferred_element_type=F32)


def _logsigmoid(z):
    return jnp.minimum(z, 0.0) - jnp.log(1.0 + jnp.exp(-jnp.abs(z)))


def _sigmoid(z):
    return 1.0 / (1.0 + jnp.exp(-z))


def _row_tile(rows):
    return 640 if rows % 640 == 0 else 128


def _mesh_pos():
    return lax.axis_index("x"), lax.axis_index("y"), lax.axis_index("c")


def _all_gather(shards):
    n = len(shards)

    def body(*refs):
        ins, outs = refs[:n], refs[n:2 * n]
        send_sems, recv_sems, local_sems = refs[2 * n:]
        x, y, c = _mesh_pos()
        me, sibling = (x, y, c), (x, y, 1 - c)
        chips = [(1 - x, y), (x, 1 - y), (1 - x, 1 - y)]

        def copy(a, k, block, to, src=None):
            dst = outs[a].at[4 * block[0] + 2 * block[1] + block[2]]
            return pltpu.make_async_remote_copy(
                src_ref=dst if src is None else src, dst_ref=dst,
                send_sem=send_sems.at[a * 7 + k], recv_sem=recv_sems.at[a * 7 + k],
                device_id=to, device_id_type=MESH)

        mine = [pltpu.make_async_copy(ins[a], outs[a].at[4 * x + 2 * y + c], local_sems.at[a]) for a in range(n)]
        for cp in mine:
            cp.start()
        first = []
        for a in range(n):
            first.append(copy(a, 0, me, sibling, src=ins[a]))
            first += [copy(a, 1 + j, me, (*chip, c), src=ins[a]) for j, chip in enumerate(chips)]
        for cp in first:
            cp.start()
        passed = []
        for j, chip in enumerate(chips):
            for a in range(n):
                copy(a, 1 + j, (*chip, c), me).wait_recv()
                fwd = copy(a, 4 + j, (*chip, c), sibling)
                fwd.start()
                passed.append(fwd)
        for a in range(n):
            copy(a, 0, sibling, me).wait_recv()
            for j, chip in enumerate(chips):
                copy(a, 4 + j, (*chip, 1 - c), me).wait_recv()
        for cp in first + passed:
            cp.wait_send()
        for cp in mine:
            cp.wait()

    return pl.pallas_call(
        body, name="all_gather_weights",
        out_shape=[jax.ShapeDtypeStruct((N_DEV,) + s.shape, s.dtype) for s in shards],
        in_specs=[ANY] * n, out_specs=[ANY] * n,
        scratch_shapes=[pltpu.SemaphoreType.DMA((7 * n,)), pltpu.SemaphoreType.DMA((7 * n,)),
                        pltpu.SemaphoreType.DMA((n,))],
    )(*shards)


def _rs_sibling(gs):
    n = len(gs)

    def body(*refs):
        ins, keep, land = refs[:n], refs[n:2 * n], refs[2 * n:3 * n]
        send_sems, recv_sems, local_sems = refs[3 * n:]
        x, y, c = _mesh_pos()
        local = [pltpu.make_async_copy(ins[a].at[c], keep[a], local_sems.at[a]) for a in range(n)]
        remote = [pltpu.make_async_remote_copy(
            src_ref=ins[a].at[1 - c], dst_ref=land[a], send_sem=send_sems.at[a], recv_sem=recv_sems.at[a],
            device_id=(x, y, 1 - c), device_id_type=MESH) for a in range(n)]
        for cp in remote + local:
            cp.start()
        for cp in remote:
            cp.wait_recv()
        for cp in remote:
            cp.wait_send()
        for cp in local:
            cp.wait()

    half = [jax.ShapeDtypeStruct(g.shape[1:], g.dtype) for g in gs]
    outs = pl.pallas_call(
        body, name="reduce_scatter_sibling",
        out_shape=half + half, in_specs=[ANY] * n, out_specs=[ANY] * (2 * n),
        scratch_shapes=[pltpu.SemaphoreType.DMA((n,)), pltpu.SemaphoreType.DMA((n,)), pltpu.SemaphoreType.DMA((n,))],
    )(*gs)
    return outs[:n], outs[n:]


def _rs_chips(ps):
    n = len(ps)

    def body(*refs):
        ins, land = refs[:n], refs[n:2 * n]
        send_sems, recv_sems, local_sems = refs[2 * n:]
        x, y, c = _mesh_pos()
        chips = [(1 - x, y), (x, 1 - y), (1 - x, 1 - y)]
        local = [pltpu.make_async_copy(ins[a].at[2 * x + y], land[a].at[3], local_sems.at[a]) for a in range(n)]
        remote = []
        for a in range(n):
            for j, chip in enumerate(chips):
                remote.append(pltpu.make_async_remote_copy(
                    src_ref=ins[a].at[2 * chip[0] + chip[1]], dst_ref=land[a].at[j],
                    send_sem=send_sems.at[3 * a + j], recv_sem=recv_sems.at[3 * a + j],
                    device_id=(*chip, c), device_id_type=MESH))
        for cp in remote + local:
            cp.start()
        for cp in remote:
            cp.wait_recv()
        for cp in remote:
            cp.wait_send()
        for cp in local:
            cp.wait()

    return pl.pallas_call(
        body, name="reduce_scatter_chips",
        out_shape=[jax.ShapeDtypeStruct(p.shape, p.dtype) for p in ps],
        in_specs=[ANY] * n, out_specs=[ANY] * n,
        scratch_shapes=[pltpu.SemaphoreType.DMA((3 * n,)), pltpu.SemaphoreType.DMA((3 * n,)),
                        pltpu.SemaphoreType.DMA((n,))],
    )(*ps)


def _all_reduce_small(pack):
    rows = pack.shape[0]

    def body(p_ref, out_ref, land, send_sems, recv_sems):
        x, y, c = _mesh_pos()
        me = 4 * x + 2 * y + c
        land[me] = p_ref[...]
        copies = []
        for k in range(1, N_DEV):
            bx, by, bc = (k >> 2) & 1, (k >> 1) & 1, k & 1
            peer = (1 - x if bx else x, 1 - y if by else y, 1 - c if bc else c)
            copies.append(pltpu.make_async_remote_copy(
                src_ref=p_ref, dst_ref=land.at[me], send_sem=send_sems.at[k - 1], recv_sem=recv_sems.at[k - 1],
                device_id=peer, device_id_type=MESH))
        for cp in copies:
            cp.start()
        for cp in copies:
            cp.wait_recv()
        for cp in copies:
            cp.wait_send()
        acc = land[0]
        for d in range(1, N_DEV):
            acc = acc + land[d]
        out_ref[...] = acc

    return pl.pallas_call(
        body, name="all_reduce_small",
        out_shape=jax.ShapeDtypeStruct(pack.shape, F32),
        in_specs=[pl.BlockSpec(memory_space=pltpu.VMEM)], out_specs=pl.BlockSpec(memory_space=pltpu.VMEM),
        scratch_shapes=[pltpu.VMEM((N_DEV, rows, 128), F32), pltpu.SemaphoreType.DMA((7,)),
                        pltpu.SemaphoreType.DMA((7,))],
    )(pack)


def _in_proj(h0, nw, win_p, tm):
    rows = h0.shape[0]

    def body(h_ref, nw_ref, w_ref, o_ref):
        h = h_ref[...]
        rstd = lax.rsqrt(jnp.mean(h * h, axis=-1, keepdims=True) + EPS)
        u = (h * rstd * nw_ref[...]).astype(MXU_DTYPE)
        o_ref[...] = jnp.dot(u, w_ref[...].astype(MXU_DTYPE), preferred_element_type=F32)

    return pl.pallas_call(
        body, name="in_proj", grid=(rows // tm,),
        in_specs=[pl.BlockSpec((tm, D), lambda i: (i, 0)), pl.BlockSpec((1, D), lambda i: (0, 0)),
                  pl.BlockSpec((D, DINP), lambda i: (0, 0))],
        out_specs=pl.BlockSpec((tm, DINP), lambda i: (i, 0)),
        out_shape=jax.ShapeDtypeStruct((rows, DINP), F32),
        compiler_params=_cp(("arbitrary",), 56),
    )(h0, nw, win_p)


def _rope_tables(rows):
    pos = (jnp.arange(rows, dtype=jnp.int32) - META0).astype(F32)
    inv_freq = 1.0 / (ROPE_THETA ** (jnp.arange(0, ROPE_DIM, 2, dtype=F32) / ROPE_DIM))
    ang = pos[:, None] * inv_freq[None, :]
    cos, sin = jnp.cos(ang), jnp.sin(ang)
    zeros = jnp.zeros((rows, SWA_HD - ROPE_DIM), F32)
    zeros8 = jnp.zeros((rows, 8), F32)
    c_head = jnp.concatenate([cos, cos, zeros + 1.0], axis=1)
    sa_head = jnp.concatenate([-sin, zeros8, zeros], axis=1)
    sb_head = jnp.concatenate([zeros8, sin, zeros], axis=1)
    two = lambda t: jnp.concatenate([t, t], axis=1)
    return two(c_head), two(sa_head), two(sb_head)


def _rope(xv, cos, sa, sb):
    width = xv.shape[1]
    reps = width // 128
    if reps > 1:
        cos, sa, sb = (jnp.tile(t, (1, reps)) for t in (cos, sa, sb))
    return xv * cos + pltpu.roll(xv, width - 8, 1) * sa + pltpu.roll(xv, 8, 1) * sb


def _unrope(dy, cos, sa, sb):
    width = dy.shape[1]
    reps = width // 128
    if reps > 1:
        cos, sa, sb = (jnp.tile(t, (1, reps)) for t in (cos, sa, sb))
    return dy * cos + pltpu.roll(dy * sa, 8, 1) + pltpu.roll(dy * sb, width - 8, 1)


def _swa_prep(proj, tabs, tm):
    rows = proj.shape[0]

    def body(q_ref, k_ref, v_ref, c_ref, sa_ref, sb_ref, qo_ref, ko_ref, vo_ref):
        cos, sa, sb = c_ref[...], sa_ref[...], sb_ref[...]
        qo_ref[...] = (_rope(q_ref[...], cos, sa, sb) * (SWA_HD ** -0.5)).astype(ACT_DTYPE)
        ko_ref[...] = _rope(k_ref[...], cos, sa, sb).astype(ACT_DTYPE)
        vo_ref[...] = v_ref[...].astype(ACT_DTYPE)

    tab_spec = pl.BlockSpec((tm, 128), lambda i: (i, 0))
    return pl.pallas_call(
        body, name="swa_prep", grid=(rows // tm,),
        in_specs=[pl.BlockSpec((tm, 512), lambda i: (i, C_SQ // 512)),
                  pl.BlockSpec((tm, 128), lambda i: (i, C_SK // 128)),
                  pl.BlockSpec((tm, 128), lambda i: (i, C_SV // 128)), tab_spec, tab_spec, tab_spec],
        out_specs=[pl.BlockSpec((tm, 512), lambda i: (i, 0)), tab_spec, tab_spec],
        out_shape=[jax.ShapeDtypeStruct((rows, 512), ACT_DTYPE), jax.ShapeDtypeStruct((rows, 128), ACT_DTYPE),
                   jax.ShapeDtypeStruct((rows, 128), ACT_DTYPE)],
        compiler_params=_cp(("arbitrary",)),
    )(proj, proj, proj, *tabs)


def _gla_gates(lr, wg, bg, chunk):
    zg = _mm(lr, wg) + bg
    row = chunk * GLA_CHUNK + lax.broadcasted_iota(jnp.int32, (GLA_CHUNK, 1), 0)
    live = row >= META0
    g = jnp.where(live, _logsigmoid(zg) * (1.0 / GLA_TAU), 0.0)
    ii = lax.broadcasted_iota(jnp.int32, (GLA_CHUNK, GLA_CHUNK), 0)
    jj = lax.broadcasted_iota(jnp.int32, (GLA_CHUNK, GLA_CHUNK), 1)
    tril = jj <= ii
    b = jnp.dot(tril.astype(F32), g, precision=HIGHEST, preferred_element_type=F32)
    return zg, live, tril, b


def _gla_fwd(proj, wg_p, bg, gnw):
    rows = proj.shape[0]
    nc = rows // GLA_CHUNK

    def body(q_ref, k_ref, v_ref, r_ref, lr_ref, wg_ref, bg_ref, gnw_ref, oraw_ref, og_ref, st_ref, state):
        c = pl.program_id(0)

        @pl.when(c == 0)
        def _():
            state[...] = jnp.zeros_like(state)

        zg, live, tril, b = _gla_gates(lr_ref[...], wg_ref[...], bg_ref[...], c)
        eb = jnp.exp(b)
        gq = q_ref[...] * (GLA_DK ** -0.5) * eb
        gk = k_ref[...] * jnp.exp(-b)
        ebl = eb[GLA_CHUNK - 1:GLA_CHUNK, :]
        v = v_ref[...]
        r = r_ref[...]
        gnw_v = gnw_ref[...]
        oraw, og = [], []
        for h in range(GLA_HEADS):
            s64 = slice(h * GLA_DK, (h + 1) * GLA_DK)
            s128 = slice(h * GLA_DV, (h + 1) * GLA_DV)
            qh, kh, vh, eblh = gq[:, s64], gk[:, s64], v[:, s128], ebl[:, s64]
            st = state[h]
            st_ref[0, h] = st
            a = jnp.where(tril, _mm_nt(qh, kh), 0.0)
            o = _mm(a, vh) + _mm_nt(qh, st)
            state[h] = st * eblh + _mm_tn(vh, kh * eblh)
            oraw.append(o)
            rstd = lax.rsqrt(jnp.mean(o * o, axis=-1, keepdims=True) + EPS)
            rh = r[:, s128]
            og.append(o * rstd * gnw_v * (rh * _sigmoid(rh)))
        oraw_ref[...] = jnp.concatenate(oraw, axis=1)
        og_ref[...] = jnp.concatenate(og, axis=1).astype(ACT_DTYPE)

    nb = lambda w, col: pl.BlockSpec((GLA_CHUNK, w), lambda c: (c, col // w))
    const = lambda shape: pl.BlockSpec(shape, lambda c: (0,) * len(shape))
    return pl.pallas_call(
        body, name="gla_fwd", grid=(nc,),
        in_specs=[nb(256, C_GQ), nb(256, C_GK), nb(512, C_GV), nb(512, C_GR), nb(128, C_LR),
                  const((128, 256)), const((1, 256)), const((1, 128))],
        out_specs=[pl.BlockSpec((GLA_CHUNK, 512), lambda c: (c, 0)), pl.BlockSpec((GLA_CHUNK, 512), lambda c: (c, 0)),
                   pl.BlockSpec((1, GLA_HEADS, GLA_DV, GLA_DK), lambda c: (c, 0, 0, 0))],
        out_shape=[jax.ShapeDtypeStruct((rows, 512), F32), jax.ShapeDtypeStruct((rows, 512), ACT_DTYPE),
                   jax.ShapeDtypeStruct((nc, GLA_HEADS, GLA_DV, GLA_DK), F32)],
        scratch_shapes=[pltpu.VMEM((GLA_HEADS, GLA_DV, GLA_DK), F32)],
        compiler_params=_cp(("arbitrary",)),
    )(proj, proj, proj, proj, proj, wg_p, bg, gnw)


def _swa_mask(n):
    qi = lax.broadcasted_iota(jnp.int32, (SWA_BLOCK, 3 * SWA_BLOCK), 0)
    jj = lax.broadcasted_iota(jnp.int32, (SWA_BLOCK, 3 * SWA_BLOCK), 1)
    meta = (jj < SWA_BLOCK) & (jj >= META0) & ((n > 0) | (jj <= qi))
    prev = (jj >= SWA_BLOCK) & (jj < 2 * SWA_BLOCK) & (n >= 2) & (jj - SWA_BLOCK > qi)
    cur = (jj >= 2 * SWA_BLOCK) & (n >= 1) & (jj - 2 * SWA_BLOCK <= qi)
    return meta | prev | cur


def _swa_specs():
    blk = lambda w: pl.BlockSpec((SWA_BLOCK, w), lambda n: (n, 0))
    first = pl.BlockSpec((SWA_BLOCK, 128), lambda n: (0, 0))
    prev = pl.BlockSpec((SWA_BLOCK, 128), lambda n: (jnp.maximum(n - 1, 0), 0))
    return blk, first, prev


def _swa_fwd(qr, kr, vr, sinks):
    rows = qr.shape[0]
    nblk = rows // SWA_BLOCK

    def body(q_ref, k0, kp, kc, v0, vp, vc, sink_ref, o_ref):
        n = pl.program_id(0)
        q = q_ref[...]
        kall = jnp.concatenate([k0[...], kp[...], kc[...]], axis=0)
        vall = jnp.concatenate([v0[...], vp[...], vc[...]], axis=0)
        mask = _swa_mask(n)
        outs = []
        for head in range(SWA_HEADS):
            kv = slice((head // SWA_GROUP) * SWA_HD, (head // SWA_GROUP + 1) * SWA_HD)
            s = jnp.where(mask, _mm_nt(q[:, head * SWA_HD:(head + 1) * SWA_HD], kall[:, kv]), NEG)
            sink = sink_ref[0, head]
            m = jnp.maximum(jnp.max(s, axis=-1, keepdims=True), sink)
            p = jnp.exp(s - m)
            den = jnp.sum(p, axis=-1, keepdims=True) + jnp.exp(sink - m)
            outs.append(_mm(p, vall[:, kv]) / den)
        o_ref[...] = jnp.concatenate(outs, axis=1).astype(ACT_DTYPE)

    blk, first, prev = _swa_specs()
    return pl.pallas_call(
        body, name="swa_fwd", grid=(nblk,),
        in_specs=[blk(512), first, prev, blk(128), first, prev, blk(128),
                  pl.BlockSpec(memory_space=pltpu.SMEM)],
        out_specs=blk(512),
        out_shape=jax.ShapeDtypeStruct((rows, 512), ACT_DTYPE),
        compiler_params=_cp(("arbitrary",)),
    )(qr, kr, kr, kr, vr, vr, vr, sinks)


def _out_proj(h0, og, osw, wout, nfw, tm):
    rows = h0.shape[0]

    def body(h_ref, og_ref, os_ref, w_ref, nw_ref, h1_ref, f_ref):
        h1 = h_ref[...] + _mm(og_ref[...], w_ref[0:512, :]) + _mm(os_ref[...], w_ref[512:1024, :])
        h1_ref[...] = h1
        rstd = lax.rsqrt(jnp.mean(h1 * h1, axis=-1, keepdims=True) + EPS)
        f_ref[...] = (h1 * rstd * nw_ref[...]).astype(ACT_DTYPE)

    row = lambda w: pl.BlockSpec((tm, w), lambda i: (i, 0))
    return pl.pallas_call(
        body, name="out_proj", grid=(rows // tm,),
        in_specs=[row(D), row(512), row(512), pl.BlockSpec((D, D), lambda i: (0, 0)), pl.BlockSpec((1, D), lambda i: (0, 0))],
        out_specs=[row(D), row(D)],
        out_shape=[jax.ShapeDtypeStruct((rows, D), F32), jax.ShapeDtypeStruct((rows, D), ACT_DTYPE)],
        compiler_params=_cp(("arbitrary",), 48),
    )(h0, og, osw, wout, nfw)


def _ffn_fwd(f, h1, w1g, w2, tgt, fnw, tm):
    rows = f.shape[0]
    nj = N_DEV

    def body(f_ref, h1_ref, w1_ref, w2_ref, t_ref, nw_ref, a_ref, dh2_ref, loss_ref, gfn_ref, acc):
        i, j = pl.program_id(0), pl.program_id(1)

        @pl.when((i == 0) & (j == 0))
        def _():
            loss_ref[...] = jnp.zeros_like(loss_ref)
            gfn_ref[...] = jnp.zeros_like(gfn_ref)

        @pl.when(j == 0)
        def _():
            acc[...] = jnp.zeros_like(acc)

        a = _mm(f_ref[...], w1_ref[...])
        a_ref[...] = a.astype(ACT_DTYPE)
        z = jnp.square(jnp.maximum(a, 0.0))
        acc[...] += _mm(z, w2_ref[...])

        @pl.when(j == nj - 1)
        def _():
            h2 = h1_ref[...] + acc[...]
            rstd = lax.rsqrt(jnp.mean(h2 * h2, axis=-1, keepdims=True) + EPS)
            hn = h2 * rstd
            nw = nw_ref[...]
            row = i * tm + lax.broadcasted_iota(jnp.int32, (tm, 1), 0)
            err = jnp.where(row >= LEAD, hn * nw - t_ref[...], 0.0)
            row_loss = jnp.sum(err * err, axis=-1, keepdims=True) * (1.0 / D)
            loss_ref[...] += jnp.broadcast_to(0.5 * jnp.sum(row_loss, axis=0, keepdims=True), loss_ref.shape)
            dy = err * (1.0 / D)
            gfn_ref[...] += jnp.broadcast_to(jnp.sum(dy * hn, axis=0, keepdims=True), gfn_ref.shape)
            dhn = dy * nw
            dh2_ref[...] = rstd * (dhn - hn * jnp.mean(dhn * hn, axis=-1, keepdims=True))

    return pl.pallas_call(
        body, name="ffn_fwd", grid=(rows // tm, nj),
        in_specs=[pl.BlockSpec((tm, D), lambda i, j: (i, 0)), pl.BlockSpec((tm, D), lambda i, j: (i, 0)),
                  pl.BlockSpec((None, D, FF_TILE), lambda i, j: (j, 0, 0)),
                  pl.BlockSpec((FF_TILE, D), lambda i, j: (j, 0)),
                  pl.BlockSpec((tm, D), lambda i, j: (i, 0)), pl.BlockSpec((1, D), lambda i, j: (0, 0))],
        out_specs=[pl.BlockSpec((tm, FF_TILE), lambda i, j: (i, j)), pl.BlockSpec((tm, D), lambda i, j: (i, 0)),
                   pl.BlockSpec((8, 128), lambda i, j: (0, 0)), pl.BlockSpec((8, D), lambda i, j: (0, 0))],
        out_shape=[jax.ShapeDtypeStruct((rows, D_FF), ACT_DTYPE), jax.ShapeDtypeStruct((rows, D), F32),
                   jax.ShapeDtypeStruct((8, 128), F32), jax.ShapeDtypeStruct((8, D), F32)],
        scratch_shapes=[pltpu.VMEM((tm, D), F32)],
        compiler_params=_cp(("arbitrary", "arbitrary"), 48),
    )(f, h1, w1g, w2, tgt, fnw)


def _ffn_bwd_act(dh2, a, w1g, w2, h1, nfw, tm):
    rows = dh2.shape[0]
    nj = N_DEV

    def body(dh2_ref, a_ref, w1_ref, w2_ref, h1_ref, nw_ref, da_ref, dh1_ref, gnf_ref, acc):
        i, j = pl.program_id(0), pl.program_id(1)

        @pl.when((i == 0) & (j == 0))
        def _():
            gnf_ref[...] = jnp.zeros_like(gnf_ref)

        @pl.when(j == 0)
        def _():
            acc[...] = jnp.zeros_like(acc)

        dz = _mm_nt(dh2_ref[...], w2_ref[...])
        da = dz * (2.0 * jnp.maximum(a_ref[...].astype(F32), 0.0))
        da_ref[...] = da.astype(ACT_DTYPE)
        acc[...] += _mm_nt(da, w1_ref[...])

        @pl.when(j == nj - 1)
        def _():
            h1 = h1_ref[...]
            rstd = lax.rsqrt(jnp.mean(h1 * h1, axis=-1, keepdims=True) + EPS)
            hn = h1 * rstd
            df = acc[...]
            gnf_ref[...] += jnp.broadcast_to(jnp.sum(df * hn, axis=0, keepdims=True), gnf_ref.shape)
            dfn = df * nw_ref[...]
            dh1_ref[...] = dh2_ref[...] + rstd * (dfn - hn * jnp.mean(dfn * hn, axis=-1, keepdims=True))

    return pl.pallas_call(
        body, name="ffn_bwd_act", grid=(rows // tm, nj),
        in_specs=[pl.BlockSpec((tm, D), lambda i, j: (i, 0)), pl.BlockSpec((tm, FF_TILE), lambda i, j: (i, j)),
                  pl.BlockSpec((None, D, FF_TILE), lambda i, j: (j, 0, 0)),
                  pl.BlockSpec((FF_TILE, D), lambda i, j: (j, 0)),
                  pl.BlockSpec((tm, D), lambda i, j: (i, 0)), pl.BlockSpec((1, D), lambda i, j: (0, 0))],
        out_specs=[pl.BlockSpec((tm, FF_TILE), lambda i, j: (i, j)), pl.BlockSpec((tm, D), lambda i, j: (i, 0)),
                   pl.BlockSpec((8, D), lambda i, j: (0, 0))],
        out_shape=[jax.ShapeDtypeStruct((rows, D_FF), ACT_DTYPE), jax.ShapeDtypeStruct((rows, D), F32),
                   jax.ShapeDtypeStruct((8, D), F32)],
        scratch_shapes=[pltpu.VMEM((tm, D), F32)],
        compiler_params=_cp(("arbitrary", "arbitrary"), 48),
    )(dh2, a, w1g, w2, h1, nfw)


def _ffn_bwd_weights(f, a, da, dh2, tm):
    rows = f.shape[0]

    def body(f_ref, a_ref, da_ref, dh2_ref, dw1_ref, dw2_ref):
        i = pl.program_id(1)

        @pl.when(i == 0)
        def _():
            dw1_ref[...] = jnp.zeros_like(dw1_ref)
            dw2_ref[...] = jnp.zeros_like(dw2_ref)

        z = jnp.square(jnp.maximum(a_ref[...].astype(F32), 0.0))
        dw1_ref[...] += _mm_tn(f_ref[...], da_ref[...])
        dw2_ref[...] += _mm_tn(z, dh2_ref[...])

    return pl.pallas_call(
        body, name="ffn_bwd_weights", grid=(N_DEV, rows // tm),
        in_specs=[pl.BlockSpec((tm, D), lambda j, i: (i, 0)), pl.BlockSpec((tm, FF_TILE), lambda j, i: (i, j)),
                  pl.BlockSpec((tm, FF_TILE), lambda j, i: (i, j)), pl.BlockSpec((tm, D), lambda j, i: (i, 0))],
        out_specs=[pl.BlockSpec((None, None, D, FF_TILE), lambda j, i: (j % 2, j // 2, 0, 0)),
                   pl.BlockSpec((None, None, FF_TILE, D), lambda j, i: (j % 2, j // 2, 0, 0))],
        out_shape=[jax.ShapeDtypeStruct((2, 4, D, FF_TILE), F32), jax.ShapeDtypeStruct((2, 4, FF_TILE, D), F32)],
        compiler_params=_cp(("arbitrary", "arbitrary"), 48),
    )(f, a, da, dh2)


def _out_proj_bwd(dh1, og, osw, wout, tm):
    rows = dh1.shape[0]

    def body(dh1_ref, og_ref, os_ref, w_ref, dog_ref, dos_ref, dw_ref):
        i = pl.program_id(0)

        @pl.when(i == 0)
        def _():
            dw_ref[...] = jnp.zeros_like(dw_ref)

        dh1 = dh1_ref[...].astype(MXU_DTYPE)
        dog_ref[...] = _mm_nt(dh1, w_ref[0:512, :])
        dos_ref[...] = _mm_nt(dh1, w_ref[512:1024, :])
        for half, ref in enumerate((og_ref, os_ref)):
            dw = _mm_tn(ref[...], dh1)
            for blk in range(4):
                shard = half * 4 + blk
                dw_ref[shard % 2, shard // 2] += dw[blk * 128:(blk + 1) * 128, :]

    row = lambda w: pl.BlockSpec((tm, w), lambda i: (i, 0))
    return pl.pallas_call(
        body, name="out_proj_bwd", grid=(rows // tm,),
        in_specs=[row(D), row(512), row(512), pl.BlockSpec((D, D), lambda i: (0, 0))],
        out_specs=[row(512), row(512), pl.BlockSpec((2, 4, 128, D), lambda i: (0, 0, 0, 0))],
        out_shape=[jax.ShapeDtypeStruct((rows, 512), F32), jax.ShapeDtypeStruct((rows, 512), F32),
                   jax.ShapeDtypeStruct((2, 4, 128, D), F32)],
        compiler_params=_cp(("arbitrary",), 48),
    )(dh1, og, osw, wout)


def _swa_bwd(qr, kr, vr, osw, dos, sinks):
    rows = qr.shape[0]
    nblk = rows // SWA_BLOCK

    def body(q_ref, k0, kp, kc, v0, vp, vc, o_ref, do_ref, sink_ref, dq_ref, dk_ref, dv_ref, dsink_ref):
        n = pl.program_id(0)

        @pl.when(n == 0)
        def _():
            dk_ref[...] = jnp.zeros_like(dk_ref)
            dv_ref[...] = jnp.zeros_like(dv_ref)
            dsink_ref[...] = jnp.zeros_like(dsink_ref)

        q = q_ref[...]
        kall = jnp.concatenate([k0[...], kp[...], kc[...]], axis=0)
        vall = jnp.concatenate([v0[...], vp[...], vc[...]], axis=0)
        mask = _swa_mask(n)
        do_all = do_ref[...]
        o_all = o_ref[...].astype(F32)
        dq, dk, dv = [], [], []
        for kvh in range(SWA_KV):
            kv = slice(kvh * SWA_HD, (kvh + 1) * SWA_HD)
            dk_h = jnp.zeros((3 * SWA_BLOCK, SWA_HD), F32)
            dv_h = jnp.zeros((3 * SWA_BLOCK, SWA_HD), F32)
            for g in range(SWA_GROUP):
                head = kvh * SWA_GROUP + g
                hs = slice(head * SWA_HD, (head + 1) * SWA_HD)
                qh, doh = q[:, hs], do_all[:, hs]
                s = jnp.where(mask, _mm_nt(qh, kall[:, kv]), NEG)
                sink = sink_ref[0, head]
                m = jnp.maximum(jnp.max(s, axis=-1, keepdims=True), sink)
                e = jnp.exp(s - m)
                inv = 1.0 / (jnp.sum(e, axis=-1, keepdims=True) + jnp.exp(sink - m))
                p = e * inv
                delta = jnp.sum(doh * o_all[:, hs], axis=-1, keepdims=True)
                ds = p * (_mm_nt(doh, vall[:, kv]) - delta)
                dq.append(_mm(ds, kall[:, kv]))
                dk_h = dk_h + _mm_tn(ds, qh)
                dv_h = dv_h + _mm_tn(p, doh)
                dsink = -jnp.sum(jnp.exp(sink - m) * inv * delta, axis=0, keepdims=True)
                dsink_ref[head:head + 1, :] += jnp.broadcast_to(dsink, (1, 128))
            dk.append(dk_h)
            dv.append(dv_h)
        dq_ref[...] = jnp.concatenate(dq, axis=1)
        dk_all = jnp.concatenate(dk, axis=1)
        dv_all = jnp.concatenate(dv, axis=1)
        prev0 = pl.multiple_of(jnp.maximum(n - 1, 0) * SWA_BLOCK, SWA_BLOCK)
        cur0 = pl.multiple_of(n * SWA_BLOCK, SWA_BLOCK)
        for ref, val in ((dk_ref, dk_all), (dv_ref, dv_all)):
            ref[0:SWA_BLOCK, :] += val[0:SWA_BLOCK]
            ref[pl.ds(prev0, SWA_BLOCK), :] += val[SWA_BLOCK:2 * SWA_BLOCK]
            ref[pl.ds(cur0, SWA_BLOCK), :] += val[2 * SWA_BLOCK:]

    blk, first, prev = _swa_specs()
    whole = pl.BlockSpec((rows, 128), lambda n: (0, 0))
    return pl.pallas_call(
        body, name="swa_bwd", grid=(nblk,),
        in_specs=[blk(512), first, prev, blk(128), first, prev, blk(128), blk(512), blk(512),
                  pl.BlockSpec(memory_space=pltpu.SMEM)],
        out_specs=[blk(512), whole, whole, pl.BlockSpec((8, 128), lambda n: (0, 0))],
        out_shape=[jax.ShapeDtypeStruct((rows, 512), F32), jax.ShapeDtypeStruct((rows, 128), F32),
                   jax.ShapeDtypeStruct((rows, 128), F32), jax.ShapeDtypeStruct((8, 128), F32)],
        compiler_params=_cp(("arbitrary",), 48),
    )(qr, kr, kr, kr, vr, vr, vr, osw, dos, sinks)


def _gla_bwd(proj, oraw, states, dog, wg_p, bg, gnw):
    rows = proj.shape[0]
    nc = rows // GLA_CHUNK

    def body(q_ref, k_ref, v_ref, r_ref, lr_ref, oraw_ref, st_ref, dog_ref, wg_ref, bg_ref, gnw_ref,
             dq_ref, dk_ref, dv_ref, dr_ref, dlr_ref, dwg_ref, dbg_ref, dgnw_ref, dstate):
        t = pl.program_id(0)
        c = nc - 1 - t

        @pl.when(t == 0)
        def _():
            dstate[...] = jnp.zeros_like(dstate)
            dwg_ref[...] = jnp.zeros_like(dwg_ref)
            dbg_ref[...] = jnp.zeros_like(dbg_ref)
            dgnw_ref[...] = jnp.zeros_like(dgnw_ref)

        lr, wg = lr_ref[...], wg_ref[...]
        zg, live, tril, b = _gla_gates(lr, wg, bg_ref[...], c)
        eb, enb = jnp.exp(b), jnp.exp(-b)
        scale = GLA_DK ** -0.5
        gq = q_ref[...] * scale * eb
        gk = k_ref[...] * enb
        ebl = eb[GLA_CHUNK - 1:GLA_CHUNK, :]
        v, r, oraw_v, dog_v = v_ref[...], r_ref[...], oraw_ref[...], dog_ref[...]
        gnw_v = gnw_ref[...]
        is_last = lax.broadcasted_iota(jnp.int32, (GLA_CHUNK, 1), 0) == GLA_CHUNK - 1
        dq, dk, dv, dr, db = [], [], [], [], []
        dgnw = jnp.zeros((1, GLA_DV), F32)
        for h in range(GLA_HEADS):
            s64 = slice(h * GLA_DK, (h + 1) * GLA_DK)
            s128 = slice(h * GLA_DV, (h + 1) * GLA_DV)
            qh, kh, vh, eblh = gq[:, s64], gk[:, s64], v[:, s128], ebl[:, s64]
            klh = kh * eblh
            st = st_ref[0, h]
            o, rh, dout = oraw_v[:, s128], r[:, s128], dog_v[:, s128]
            rstd = lax.rsqrt(jnp.mean(o * o, axis=-1, keepdims=True) + EPS)
            on = o * rstd
            sg = _sigmoid(rh)
            dr.append(dout * (on * gnw_v) * (sg * (1.0 + rh * (1.0 - sg))))
            dy = dout * (rh * sg)
            dgnw = dgnw + jnp.sum(dy * on, axis=0, keepdims=True)
            don = dy * gnw_v
            do = rstd * (don - on * jnp.mean(don * on, axis=-1, keepdims=True))
            a = jnp.where(tril, _mm_nt(qh, kh), 0.0)
            da = jnp.where(tril, _mm_nt(do, vh), 0.0)
            dsp = dstate[h]
            dkl = _mm(vh, dsp)
            dv.append(_mm_tn(a, do) + _mm_nt(klh, dsp))
            debl = jnp.sum(dsp * st, axis=0, keepdims=True)
            dgq = _mm(da, kh) + _mm(do, st)
            dgk = _mm_tn(da, qh)
            dstate[h] = dsp * eblh + _mm_tn(do, qh)
            dq.append(dgq * (scale * eb[:, s64]))
            dk.append((dgk + dkl * eblh) * enb[:, s64])
            last = debl * eblh + jnp.sum(dkl * klh, axis=0, keepdims=True)
            db.append(dgq * qh - dgk * kh - dkl * klh + jnp.where(is_last, last, 0.0))
        dq_ref[...] = jnp.concatenate(dq, axis=1).astype(ACT_DTYPE)
        dk_ref[...] = jnp.concatenate(dk, axis=1).astype(ACT_DTYPE)
        dv_ref[...] = jnp.concatenate(dv, axis=1).astype(ACT_DTYPE)
        dr_ref[...] = jnp.concatenate(dr, axis=1).astype(ACT_DTYPE)
        triu = jnp.logical_not(tril) | (lax.broadcasted_iota(jnp.int32, (GLA_CHUNK, GLA_CHUNK), 0)
                                        == lax.broadcasted_iota(jnp.int32, (GLA_CHUNK, GLA_CHUNK), 1))
        dg = jnp.dot(triu.astype(F32), jnp.concatenate(db, axis=1), precision=HIGHEST, preferred_element_type=F32)
        dzg = jnp.where(live, dg * _sigmoid(-zg) * (1.0 / GLA_TAU), 0.0)
        dlr_ref[...] = _mm_nt(dzg, wg).astype(ACT_DTYPE)
        dwg_ref[...] += _mm_tn(lr, dzg)
        dbg_ref[...] += jnp.broadcast_to(jnp.sum(dzg, axis=0, keepdims=True), dbg_ref.shape)
        dgnw_ref[...] += jnp.broadcast_to(dgnw, dgnw_ref.shape)

    nb = lambda w, col: pl.BlockSpec((GLA_CHUNK, w), lambda t: (nc - 1 - t, col // w))
    const = lambda shape: pl.BlockSpec(shape, lambda t: (0,) * len(shape))
    return pl.pallas_call(
        body, name="gla_bwd", grid=(nc,),
        in_specs=[nb(256, C_GQ), nb(256, C_GK), nb(512, C_GV), nb(512, C_GR), nb(128, C_LR), nb(512, 0),
                  pl.BlockSpec((1, GLA_HEADS, GLA_DV, GLA_DK), lambda t: (nc - 1 - t, 0, 0, 0)), nb(512, 0),
                  const((128, 256)), const((1, 256)), const((1, 128))],
        out_specs=[nb(256, 0), nb(256, 0), nb(512, 0), nb(512, 0), nb(128, 0),
                   const((128, 256)), const((8, 256)), const((8, 128))],
        out_shape=[jax.ShapeDtypeStruct((rows, 256), ACT_DTYPE), jax.ShapeDtypeStruct((rows, 256), ACT_DTYPE),
                   jax.ShapeDtypeStruct((rows, 512), ACT_DTYPE), jax.ShapeDtypeStruct((rows, 512), ACT_DTYPE),
                   jax.ShapeDtypeStruct((rows, 128), ACT_DTYPE), jax.ShapeDtypeStruct((128, 256), F32),
                   jax.ShapeDtypeStruct((8, 256), F32), jax.ShapeDtypeStruct((8, 128), F32)],
        scratch_shapes=[pltpu.VMEM((GLA_HEADS, GLA_DV, GLA_DK), F32)],
        compiler_params=_cp(("arbitrary",)),
    )(proj, proj, proj, proj, proj, oraw, states, dog, wg_p, bg, gnw)


def _in_proj_bwd(h0, dh1, nw, win_p, dgv, dgr, dsq, dgq, dgk, dsk, dsv, dlr, tabs, tm):
    rows = h0.shape[0]

    def body(h_ref, dh1_ref, nw_ref, w_ref, dgv_ref, dgr_ref, dsq_ref, dgq_ref, dgk_ref, dsk_ref, dsv_ref, dlr_ref,
             c_ref, sa_ref, sb_ref, dh0_ref, dw_ref, gnm_ref):
        i = pl.program_id(0)

        @pl.when(i == 0)
        def _():
            dw_ref[...] = jnp.zeros_like(dw_ref)
            gnm_ref[...] = jnp.zeros_like(gnm_ref)

        cos, sa, sb = c_ref[...], sa_ref[...], sb_ref[...]
        dsq_v = (_unrope(dsq_ref[...], cos, sa, sb) * (SWA_HD ** -0.5)).astype(MXU_DTYPE)
        dsk_v = _unrope(dsk_ref[...], cos, sa, sb).astype(MXU_DTYPE)
        dproj = jnp.concatenate(
            [dgv_ref[...].astype(MXU_DTYPE), dgr_ref[...].astype(MXU_DTYPE), dsq_v, dgq_ref[...].astype(MXU_DTYPE),
             dgk_ref[...].astype(MXU_DTYPE), dsk_v, dsv_ref[...].astype(MXU_DTYPE), dlr_ref[...].astype(MXU_DTYPE)],
            axis=1)
        h = h_ref[...]
        rstd = lax.rsqrt(jnp.mean(h * h, axis=-1, keepdims=True) + EPS)
        hn = h * rstd
        nw_v = nw_ref[...]
        u = (hn * nw_v).astype(MXU_DTYPE)
        du = _mm_nt(dproj, w_ref[...])
        dw_ref[...] += _mm_tn(u, dproj)
        gnm_ref[...] += jnp.broadcast_to(jnp.sum(du * hn, axis=0, keepdims=True), gnm_ref.shape)
        dun = du * nw_v
        dh0_ref[...] = dh1_ref[...] + rstd * (dun - hn * jnp.mean(dun * hn, axis=-1, keepdims=True))

    row = lambda w: pl.BlockSpec((tm, w), lambda i: (i, 0))
    return pl.pallas_call(
        body, name="in_proj_bwd", grid=(rows // tm,),
        in_specs=[row(D), row(D), pl.BlockSpec((1, D), lambda i: (0, 0)), pl.BlockSpec((D, DINP), lambda i: (0, 0)),
                  row(512), row(512), row(512), row(256), row(256), row(128), row(128), row(128),
                  row(128), row(128), row(128)],
        out_specs=[row(D), pl.BlockSpec((D, DINP), lambda i: (0, 0)), pl.BlockSpec((8, D), lambda i: (0, 0))],
        out_shape=[jax.ShapeDtypeStruct((rows, D), F32), jax.ShapeDtypeStruct((D, DINP), F32),
                   jax.ShapeDtypeStruct((8, D), F32)],
        compiler_params=_cp(("arbitrary",), 56),
    )(h0, dh1, nw, win_p, dgv, dgr, dsq, dgq, dgk, dsk, dsv, dlr, *tabs)


def _adamw(w, g, m, v):
    m = ADAM_B1 * m + (1.0 - ADAM_B1) * g
    v = ADAM_B2 * v + (1.0 - ADAM_B2) * jnp.square(g)
    m_hat = m / (1.0 - ADAM_B1 ** ADAM_STEP)
    v_hat = v / (1.0 - ADAM_B2 ** ADAM_STEP)
    delta = -ADAM_LR * (m_hat / (jnp.sqrt(v_hat) + ADAM_EPS) + ADAM_WD * w)
    return delta, m, v


def _adamw_shard(parts, w, m, v, name):
    r, cdim = w.shape
    tr = 128 if r % 128 == 0 else r

    def body(p_ref, w_ref, m_ref, v_ref, g_ref, d_ref, nm_ref, nv_ref):
        g = ((p_ref[0] + p_ref[1]) + p_ref[2]) + p_ref[3]
        g_ref[...] = g
        d_ref[...], nm_ref[...], nv_ref[...] = _adamw(w_ref[...], g, m_ref[...], v_ref[...])

    spec = pl.BlockSpec((tr, cdim), lambda i: (i, 0))
    shape = jax.ShapeDtypeStruct((r, cdim), F32)
    return pl.pallas_call(
        body, name=name, grid=(r // tr,),
        in_specs=[pl.BlockSpec((4, tr, cdim), lambda i: (0, i, 0)), spec, spec, spec],
        out_specs=[spec] * 4, out_shape=[shape] * 4,
        compiler_params=_cp(("arbitrary",)),
    )(parts, w, m, v)


def _adamw_small(w, g, m, v):
    def body(w_ref, g_ref, m_ref, v_ref, d_ref, nm_ref, nv_ref):
        d_ref[...], nm_ref[...], nv_ref[...] = _adamw(w_ref[...], g_ref[...], m_ref[...], v_ref[...])

    vm = pl.BlockSpec(memory_space=pltpu.VMEM)
    shape = jax.ShapeDtypeStruct(w.shape, F32)
    return pl.pallas_call(body, name="adamw_small", in_specs=[vm] * 4, out_specs=[vm] * 3,
                          out_shape=[shape] * 3)(w, g, m, v)


def _add(a, b, name):
    _, r, cdim = a.shape
    tr = 128 if r % 128 == 0 else r

    def body(a_ref, b_ref, o_ref):
        o_ref[...] = a_ref[...] + b_ref[...]

    spec = pl.BlockSpec((4, tr, cdim), lambda i: (0, i, 0))
    return pl.pallas_call(body, name=name, grid=(r // tr,), in_specs=[spec, spec], out_specs=spec,
                          out_shape=jax.ShapeDtypeStruct(a.shape, F32), compiler_params=_cp(("arbitrary",)))(a, b)


def _to_rows128(a):
    return a.reshape(-1, 128)


def _pad_rows128(vec):
    flat = vec.reshape(-1)
    n = -(-flat.shape[0] // 128)
    return jnp.pad(flat, (0, n * 128 - flat.shape[0])).reshape(n, 128)


def kernel(x, meta_tokens, norm_mix_w, w_in, w_gate_up, b_gate, gla_norm_w, sinks, w_out, norm_ff_w, w_ff1, w_ff2, final_norm_w, loss_target, m_meta_tokens, m_norm_mix_w, m_w_in, m_w_gate_up, m_b_gate, m_gla_norm_w, m_sinks, m_w_out, m_norm_ff_w, m_w_ff1, m_w_ff2, m_final_norm_w, v_meta_tokens, v_norm_mix_w, v_w_in, v_w_gate_up, v_b_gate, v_gla_norm_w, v_sinks, v_w_out, v_norm_ff_w, v_w_ff1, v_w_ff2, v_final_norm_w):
    seq = x.shape[1]
    rows = LEAD + seq
    tm = _row_tile(rows)
    tm_small = tm // 2 if tm == 640 else tm
    dev = 4 * lax.axis_index("x") + 2 * lax.axis_index("y") + lax.axis_index("c")

    small_shard = jnp.concatenate([meta_tokens, w_gate_up[0], jnp.zeros((N_META, 96), F32)], axis=1)
    g_in, g_out, g_w1, g_w2, g_small = _all_gather(
        [w_in[0].astype(WIRE_DTYPE), w_out[0].astype(WIRE_DTYPE), w_ff1[0].astype(WIRE_DTYPE),
         w_ff2[0].astype(WIRE_DTYPE), small_shard])
    win_full = jnp.transpose(g_in, (1, 0, 2)).reshape(D, DIN)
    cols = lambda r: win_full[:, r[0]:r[1]]
    win_p = jnp.concatenate([cols(O_GV), cols(O_GR), cols(O_SQ), cols(O_GQ), cols(O_GK), cols(O_SK), cols(O_SV),
                             cols(O_LR), jnp.zeros((D, 128 - GLA_RANK), WIRE_DTYPE)], axis=1)
    wout_full = g_out.reshape(D, D)
    w2_full = g_w2.reshape(D_FF, D)
    meta_full = jnp.transpose(g_small[:, :, 0:128], (1, 0, 2)).reshape(N_META, D)
    wg_full = jnp.transpose(g_small[:, :, 128:160], (1, 0, 2)).reshape(GLA_RANK, GLA_HEADS * GLA_DK)
    wg_p = jnp.concatenate([wg_full, jnp.zeros((128 - GLA_RANK, 256), F32)], axis=0)

    h0 = jnp.concatenate([jnp.zeros((META0, D), F32), meta_full, x[0]], axis=0)
    tgt = jnp.concatenate([jnp.zeros((LEAD, D), F32), loss_target[0]], axis=0)
    tabs = _rope_tables(rows)
    proj = _in_proj(h0, norm_mix_w, win_p, tm)
    oraw, og, states = _gla_fwd(proj, wg_p, b_gate, gla_norm_w)
    qr, kr, vr = _swa_prep(proj, tabs, tm)
    osw = _swa_fwd(qr, kr, vr, sinks)
    h1, f = _out_proj(h0, og, osw, wout_full, norm_ff_w, tm)
    a, dh2, loss_p, gfn_p = _ffn_fwd(f, h1, g_w1, w2_full, tgt, final_norm_w.reshape(1, D), tm)

    da, dh1, gnf_p = _ffn_bwd_act(dh2, a, g_w1, w2_full, h1, norm_ff_w, tm)
    dw1, dw2 = _ffn_bwd_weights(f, a, da, dh2, tm)
    dog, dos, dwout = _out_proj_bwd(dh1, og, osw, wout_full, tm)
    dsq, dsk, dsv, dsink_p = _swa_bwd(qr, kr, vr, osw, dos, sinks)
    dgq, dgk, dgv, dgr, dlr, dwg_p, dbg_p, dgnw_p = _gla_bwd(proj, oraw, states, dog, wg_p, b_gate, gla_norm_w)
    dh0, dwin_p, gnm_p = _in_proj_bwd(h0, dh1, norm_mix_w, win_p, dgv, dgr, dsq, dgq, dgk, dsk, dsv, dlr, tabs, tm_small)
    grad_x = dh0[LEAD:][None]

    pcols = lambda c0, r: dwin_p[:, c0:c0 + (r[1] - r[0])]
    dwin = jnp.concatenate([pcols(C_GQ, O_GQ), pcols(C_GK, O_GK), pcols(C_GV, O_GV), pcols(C_GR, O_GR),
                            pcols(C_LR, O_LR), pcols(C_SQ, O_SQ), pcols(C_SK, O_SK), pcols(C_SV, O_SV)], axis=1)
    dwin = jnp.transpose(dwin.reshape(D, 4, 2, DIN // N_DEV), (2, 1, 0, 3))
    mine, theirs = _rs_sibling([dwin, dwout, dw1, dw2])
    chip_sums = [_add(p, q, "reduce_pair_%d" % k) for k, (p, q) in enumerate(zip(mine, theirs))]
    parts = _rs_chips(chip_sums)

    small = [dh0[META0:LEAD], dwg_p[0:GLA_RANK], gnm_p[0:1], dbg_p[0:1], dgnw_p[0:1], dsink_p[:, 0], gnf_p[0:1],
             gfn_p[0:1], loss_p[0:1, 0:1]]
    sizes = [-(-s.size // 128) for s in small]
    pack = jnp.concatenate([_pad_rows128(s) for s in small], axis=0)
    pad_rows = -pack.shape[0] % 8
    pack = jnp.pad(pack, ((0, pad_rows), (0, 0)))
    total = _all_reduce_small(pack)
    offs = [sum(sizes[:k]) for k in range(len(sizes))]
    take = lambda k, shape: total[offs[k]:offs[k] + sizes[k]].reshape(-1)[:small[k].size].reshape(shape)
    g_meta_full = take(0, (N_META, D))
    g_wg_full = take(1, (GLA_RANK, 256))
    g_meta = lax.dynamic_slice_in_dim(g_meta_full, dev * 128, 128, axis=1)
    g_wg = lax.dynamic_slice_in_dim(g_wg_full, dev * 32, 32, axis=1)[None]
    g_norm_mix, g_b_gate, g_gla_norm = take(2, (1, D)), take(3, (1, 256)), take(4, (1, 128))
    g_sinks, g_norm_ff, g_final_norm = take(5, (1, 8)), take(6, (1, D)), take(7, (D,))
    loss = take(8, ())

    g_win, d_win, nm_win, nv_win = _adamw_shard(parts[0], w_in[0], m_w_in[0], v_w_in[0], "adamw_w_in")
    g_wout, d_wout, nm_wout, nv_wout = _adamw_shard(parts[1], w_out[0], m_w_out[0], v_w_out[0], "adamw_w_out")
    g_w1s, d_w1, nm_w1, nv_w1 = _adamw_shard(parts[2], w_ff1[0], m_w_ff1[0], v_w_ff1[0], "adamw_w_ff1")
    g_w2s, d_w2, nm_w2, nv_w2 = _adamw_shard(parts[3], w_ff2[0], m_w_ff2[0], v_w_ff2[0], "adamw_w_ff2")

    names = ["meta", "wg", "norm_mix", "b_gate", "gla_norm", "sinks", "norm_ff", "final_norm"]
    ws = [meta_tokens, w_gate_up, norm_mix_w, b_gate, gla_norm_w, sinks, norm_ff_w, final_norm_w]
    gs = [g_meta, g_wg, g_norm_mix, g_b_gate, g_gla_norm, g_sinks, g_norm_ff, g_final_norm]
    ms = [m_meta_tokens, m_w_gate_up, m_norm_mix_w, m_b_gate, m_gla_norm_w, m_sinks, m_norm_ff_w, m_final_norm_w]
    vs = [v_meta_tokens, v_w_gate_up, v_norm_mix_w, v_b_gate, v_gla_norm_w, v_sinks, v_norm_ff_w, v_final_norm_w]
    ssz = [-(-w.size // 128) for w in ws]
    packed = []
    for group in (ws, gs, ms, vs):
        p = jnp.concatenate([_pad_rows128(t) for t in group], axis=0)
        packed.append(jnp.pad(p, ((0, -p.shape[0] % 8), (0, 0))))
    d_s, nm_s, nv_s = _adamw_small(*packed)
    soffs = [sum(ssz[:k]) for k in range(len(ssz))]
    unpack = lambda t, k: t[soffs[k]:soffs[k] + ssz[k]].reshape(-1)[:ws[k].size].reshape(ws[k].shape)
    d_small = {n: unpack(d_s, k) for k, n in enumerate(names)}
    nm_small = {n: unpack(nm_s, k) for k, n in enumerate(names)}
    nv_small = {n: unpack(nv_s, k) for k, n in enumerate(names)}
    g_small_d = dict(zip(names, gs))

    def ordered(big, small_d):
        win_v, wout_v, w1_v, w2_v = big
        return (small_d["meta"], small_d["norm_mix"], win_v[None], small_d["wg"], small_d["b_gate"],
                small_d["gla_norm"], small_d["sinks"], wout_v[None], small_d["norm_ff"], w1_v[None], w2_v[None],
                small_d["final_norm"])

    return (loss, grad_x,
            *ordered((g_win, g_wout, g_w1s, g_w2s), g_small_d),
            *ordered((d_win, d_wout, d_w1, d_w2), d_small),
            *ordered((nm_win, nm_wout, nm_w1, nm_w2), nm_small),
            *ordered((nv_win, nv_wout, nv_w1, nv_w2), nv_small))
```

```python
import jax
import jax.numpy as jnp
from jax import lax
from jax.experimental import pallas as pl
from jax.experimental.pallas import tpu as pltpu

F32 = jnp.float32
MXU_DTYPE = jnp.bfloat16
ACT_DTYPE = jnp.bfloat16
WIRE_DTYPE = jnp.bfloat16

D = 1024
N_META = 16
LEAD = 128
META0 = LEAD - N_META
EPS = 1e-5
GLA_HEADS, GLA_DK, GLA_DV, GLA_RANK, GLA_CHUNK = 4, 64, 128, 16, 64
GLA_TAU = 16.0
SWA_HEADS, SWA_KV, SWA_GROUP, SWA_HD, SWA_BLOCK = 8, 2, 4, 64, 128
ROPE_DIM, ROPE_THETA = 16, 500000.0
D_FF = 4096
N_DEV = 8
FF_TILE = D_FF // N_DEV
NEG = -1e30

C_GV, C_GR, C_SQ, C_GQ, C_GK, C_SK, C_SV, C_LR = 0, 512, 1024, 1536, 1792, 2048, 2176, 2304
DINP = 2432
DIN = 2320
O_GQ, O_GK, O_GV, O_GR, O_LR, O_SQ, O_SK, O_SV = (0, 256), (256, 512), (512, 1024), (1024, 1536), (1536, 1552), (1552, 2064), (2064, 2192), (2192, 2320)

ADAM_LR, ADAM_B1, ADAM_B2, ADAM_EPS, ADAM_WD, ADAM_STEP = 0.001, 0.9, 0.999, 1e-08, 0.01, 10

MESH = pl.DeviceIdType.MESH
ANY = pl.BlockSpec(memory_space=pl.ANY)
HIGHEST = lax.Precision.HIGHEST


def _cp(sem=None, vmem_mb=None):
    kw = {}
    if sem is not None:
        kw["dimension_semantics"] = sem
    if vmem_mb is not None:
        kw["vmem_limit_bytes"] = vmem_mb << 20
    return pltpu.CompilerParams(**kw)


def _mm(a, b):
    return jnp.dot(a.astype(MXU_DTYPE), b.astype(MXU_DTYPE), preferred_element_type=F32)


def _mm_nt(a, b):
    return lax.dot_general(a.astype(MXU_DTYPE), b.astype(MXU_DTYPE), (((1,), (1,)), ((), ())),
                           preferred_element_type=F32)


def _mm_tn(a, b):
    return lax.dot_general(a.astype(MXU_DTYPE), b.astype(MXU_DTYPE), (((0,), (0,)), ((), ())),
                           preferred_element_type=F32)


def _logsigmoid(z):
    return jnp.minimum(z, 0.0) - jnp.log(1.0 + jnp.exp(-jnp.abs(z)))


def _sigmoid(z):
    return 1.0 / (1.0 + jnp.exp(-z))


def _row_tile(rows):
    return 640 if rows % 640 == 0 else 128


def _mesh_pos():
    return lax.axis_index("x"), lax.axis_index("y"), lax.axis_index("c")


def _all_gather(shards):
    n = len(shards)

    def body(*refs):
        ins, outs = refs[:n], refs[n:2 * n]
        send_sems, recv_sems, local_sems = refs[2 * n:]
        x, y, c = _mesh_pos()
        me, sibling = (x, y, c), (x, y, 1 - c)
        chips = [(1 - x, y), (x, 1 - y), (1 - x, 1 - y)]

        def copy(a, k, block, to, src=None):
            dst = outs[a].at[4 * block[0] + 2 * block[1] + block[2]]
            return pltpu.make_async_remote_copy(
                src_ref=dst if src is None else src, dst_ref=dst,
                send_sem=send_sems.at[a * 7 + k], recv_sem=recv_sems.at[a * 7 + k],
                device_id=to, device_id_type=MESH)

        mine = [pltpu.make_async_copy(ins[a], outs[a].at[4 * x + 2 * y + c], local_sems.at[a]) for a in range(n)]
        for cp in mine:
            cp.start()
        first = []
        for a in range(n):
            first.append(copy(a, 0, me, sibling, src=ins[a]))
            first += [copy(a, 1 + j, me, (*chip, c), src=ins[a]) for j, chip in enumerate(chips)]
        for cp in first:
            cp.start()
        passed = []
        for j, chip in enumerate(chips):
            for a in range(n):
                copy(a, 1 + j, (*chip, c), me).wait_recv()
                fwd = copy(a, 4 + j, (*chip, c), sibling)
                fwd.start()
                passed.append(fwd)
        for a in range(n):
            copy(a, 0, sibling, me).wait_recv()
            for j, chip in enumerate(chips):
                copy(a, 4 + j, (*chip, 1 - c), me).wait_recv()
        for cp in first + passed:
            cp.wait_send()
        for cp in mine:
            cp.wait()

    return pl.pallas_call(
        body, name="all_gather_weights",
        out_shape=[jax.ShapeDtypeStruct((N_DEV,) + s.shape, s.dtype) for s in shards],
        in_specs=[ANY] * n, out_specs=[ANY] * n,
        scratch_shapes=[pltpu.SemaphoreType.DMA((7 * n,)), pltpu.SemaphoreType.DMA((7 * n,)),
                        pltpu.SemaphoreType.DMA((n,))],
    )(*shards)


def _rs_sibling(gs):
    n = len(gs)

    def body(*refs):
        ins, keep, land = refs[:n], refs[n:2 * n], refs[2 * n:3 * n]
        send_sems, recv_sems, local_sems = refs[3 * n:]
        x, y, c = _mesh_pos()
        local, remote = [], []
        for a in range(n):
            for chip in range(4):
                k = 4 * a + chip
                local.append(pltpu.make_async_copy(ins[a].at[c, chip], keep[a].at[chip], local_sems.at[k]))
                remote.append(pltpu.make_async_remote_copy(
                    src_ref=ins[a].at[1 - c, chip], dst_ref=land[a].at[chip], send_sem=send_sems.at[k],
                    recv_sem=recv_sems.at[k], device_id=(x, y, 1 - c), device_id_type=MESH))
        for cp in remote + local:
            cp.start()
        for cp in remote:
            cp.wait_recv()
        for cp in remote:
            cp.wait_send()
        for cp in local:
            cp.wait()

    half = [jax.ShapeDtypeStruct(g.shape[1:], g.dtype) for g in gs]
    outs = pl.pallas_call(
        body, name="reduce_scatter_sibling",
        out_shape=half + half, in_specs=[ANY] * n, out_specs=[ANY] * (2 * n),
        scratch_shapes=[pltpu.SemaphoreType.DMA((4 * n,)), pltpu.SemaphoreType.DMA((4 * n,)),
                        pltpu.SemaphoreType.DMA((4 * n,))],
    )(*gs)
    return outs[:n], outs[n:]


def _rs_chips(ps):
    n = len(ps)

    def body(*refs):
        ins, land = refs[:n], refs[n:2 * n]
        send_sems, recv_sems, local_sems = refs[2 * n:]
        x, y, c = _mesh_pos()
        chips = [(1 - x, y), (x, 1 - y), (1 - x, 1 - y)]
        local = [pltpu.make_async_copy(ins[a].at[2 * x + y], land[a].at[3], local_sems.at[a]) for a in range(n)]
        remote = []
        for a in range(n):
            for j, chip in enumerate(chips):
                remote.append(pltpu.make_async_remote_copy(
                    src_ref=ins[a].at[2 * chip[0] + chip[1]], dst_ref=land[a].at[j],
                    send_sem=send_sems.at[3 * a + j], recv_sem=recv_sems.at[3 * a + j],
                    device_id=(*chip, c), device_id_type=MESH))
        for cp in remote + local:
            cp.start()
        for cp in remote:
            cp.wait_recv()
        for cp in remote:
            cp.wait_send()
        for cp in local:
            cp.wait()

    return pl.pallas_call(
        body, name="reduce_scatter_chips",
        out_shape=[jax.ShapeDtypeStruct(p.shape, p.dtype) for p in ps],
        in_specs=[ANY] * n, out_specs=[ANY] * n,
        scratch_shapes=[pltpu.SemaphoreType.DMA((3 * n,)), pltpu.SemaphoreType.DMA((3 * n,)),
                        pltpu.SemaphoreType.DMA((n,))],
    )(*ps)


def _all_reduce_small(pack):
    rows = pack.shape[0]

    def body(p_ref, out_ref, land, send_sems, recv_sems):
        x, y, c = _mesh_pos()
        me = 4 * x + 2 * y + c
        land[me] = p_ref[...]
        copies = []
        for k in range(1, N_DEV):
            bx, by, bc = (k >> 2) & 1, (k >> 1) & 1, k & 1
            peer = (1 - x if bx else x, 1 - y if by else y, 1 - c if bc else c)
            copies.append(pltpu.make_async_remote_copy(
                src_ref=p_ref, dst_ref=land.at[me], send_sem=send_sems.at[k - 1], recv_sem=recv_sems.at[k - 1],
                device_id=peer, device_id_type=MESH))
        for cp in copies:
            cp.start()
        for cp in copies:
            cp.wait_recv()
        for cp in copies:
            cp.wait_send()
        acc = land[0]
        for d in range(1, N_DEV):
            acc = acc + land[d]
        out_ref[...] = acc

    return pl.pallas_call(
        body, name="all_reduce_small",
        out_shape=jax.ShapeDtypeStruct(pack.shape, F32),
        in_specs=[pl.BlockSpec(memory_space=pltpu.VMEM)], out_specs=pl.BlockSpec(memory_space=pltpu.VMEM),
        scratch_shapes=[pltpu.VMEM((N_DEV, rows, 128), F32), pltpu.SemaphoreType.DMA((7,)),
                        pltpu.SemaphoreType.DMA((7,))],
    )(pack)


def _in_proj(h0, nw, win_p, tm):
    rows = h0.shape[0]

    def body(h_ref, nw_ref, w_ref, o_ref):
        h = h_ref[...]
        rstd = lax.rsqrt(jnp.mean(h * h, axis=-1, keepdims=True) + EPS)
        u = (h * rstd * nw_ref[...]).astype(MXU_DTYPE)
        o_ref[...] = jnp.dot(u, w_ref[...].astype(MXU_DTYPE), preferred_element_type=F32)

    return pl.pallas_call(
        body, name="in_proj", grid=(rows // tm,),
        in_specs=[pl.BlockSpec((tm, D), lambda i: (i, 0)), pl.BlockSpec((1, D), lambda i: (0, 0)),
                  pl.BlockSpec((D, DINP), lambda i: (0, 0))],
        out_specs=pl.BlockSpec((tm, DINP), lambda i: (i, 0)),
        out_shape=jax.ShapeDtypeStruct((rows, DINP), F32),
        compiler_params=_cp(("arbitrary",), 56),
    )(h0, nw, win_p)


def _rope_tables(rows):
    pos = (jnp.arange(rows, dtype=jnp.int32) - META0).astype(F32)
    inv_freq = 1.0 / (ROPE_THETA ** (jnp.arange(0, ROPE_DIM, 2, dtype=F32) / ROPE_DIM))
    ang = pos[:, None] * inv_freq[None, :]
    cos, sin = jnp.cos(ang), jnp.sin(ang)
    zeros = jnp.zeros((rows, SWA_HD - ROPE_DIM), F32)
    zeros8 = jnp.zeros((rows, 8), F32)
    c_head = jnp.concatenate([cos, cos, zeros + 1.0], axis=1)
    sa_head = jnp.concatenate([-sin, zeros8, zeros], axis=1)
    sb_head = jnp.concatenate([zeros8, sin, zeros], axis=1)
    two = lambda t: jnp.concatenate([t, t], axis=1)
    return two(c_head), two(sa_head), two(sb_head)


def _rope(xv, cos, sa, sb):
    width = xv.shape[1]
    reps = width // 128
    if reps > 1:
        cos, sa, sb = (jnp.tile(t, (1, reps)) for t in (cos, sa, sb))
    return xv * cos + pltpu.roll(xv, width - 8, 1) * sa + pltpu.roll(xv, 8, 1) * sb


def _unrope(dy, cos, sa, sb):
    width = dy.shape[1]
    reps = width // 128
    if reps > 1:
        cos, sa, sb = (jnp.tile(t, (1, reps)) for t in (cos, sa, sb))
    return dy * cos + pltpu.roll(dy * sa, 8, 1) + pltpu.roll(dy * sb, width - 8, 1)


def _swa_prep(proj, tabs, tm):
    rows = proj.shape[0]

    def body(q_ref, k_ref, v_ref, c_ref, sa_ref, sb_ref, qo_ref, ko_ref, vo_ref):
        cos, sa, sb = c_ref[...], sa_ref[...], sb_ref[...]
        qo_ref[...] = (_rope(q_ref[...], cos, sa, sb) * (SWA_HD ** -0.5)).astype(ACT_DTYPE)
        ko_ref[...] = _rope(k_ref[...], cos, sa, sb).astype(ACT_DTYPE)
        vo_ref[...] = v_ref[...].astype(ACT_DTYPE)

    tab_spec = pl.BlockSpec((tm, 128), lambda i: (i, 0))
    return pl.pallas_call(
        body, name="swa_prep", grid=(rows // tm,),
        in_specs=[pl.BlockSpec((tm, 512), lambda i: (i, C_SQ // 512)),
                  pl.BlockSpec((tm, 128), lambda i: (i, C_SK // 128)),
                  pl.BlockSpec((tm, 128), lambda i: (i, C_SV // 128)), tab_spec, tab_spec, tab_spec],
        out_specs=[pl.BlockSpec((tm, 512), lambda i: (i, 0)), tab_spec, tab_spec],
        out_shape=[jax.ShapeDtypeStruct((rows, 512), ACT_DTYPE), jax.ShapeDtypeStruct((rows, 128), ACT_DTYPE),
                   jax.ShapeDtypeStruct((rows, 128), ACT_DTYPE)],
        compiler_params=_cp(("arbitrary",)),
    )(proj, proj, proj, *tabs)


def _gla_gates(lr, wg, bg, chunk):
    zg = _mm(lr, wg) + bg
    row = chunk * GLA_CHUNK + lax.broadcasted_iota(jnp.int32, (GLA_CHUNK, 1), 0)
    live = row >= META0
    g = jnp.where(live, _logsigmoid(zg) * (1.0 / GLA_TAU), 0.0)
    ii = lax.broadcasted_iota(jnp.int32, (GLA_CHUNK, GLA_CHUNK), 0)
    jj = lax.broadcasted_iota(jnp.int32, (GLA_CHUNK, GLA_CHUNK), 1)
    tril = jj <= ii
    b = jnp.dot(tril.astype(F32), g, precision=HIGHEST, preferred_element_type=F32)
    return zg, live, tril, b


def _gla_fwd(proj, wg_p, bg, gnw):
    rows = proj.shape[0]
    nc = rows // GLA_CHUNK

    def body(q_ref, k_ref, v_ref, r_ref, lr_ref, wg_ref, bg_ref, gnw_ref, oraw_ref, og_ref, st_ref, state):
        c = pl.program_id(0)

        @pl.when(c == 0)
        def _():
            state[...] = jnp.zeros_like(state)

        zg, live, tril, b = _gla_gates(lr_ref[...], wg_ref[...], bg_ref[...], c)
        eb = jnp.exp(b)
        gq = q_ref[...] * (GLA_DK ** -0.5) * eb
        gk = k_ref[...] * jnp.exp(-b)
        ebl = eb[GLA_CHUNK - 1:GLA_CHUNK, :]
        v = v_ref[...]
        r = r_ref[...]
        gnw_v = gnw_ref[...]
        oraw, og = [], []
        for h in range(GLA_HEADS):
            s64 = slice(h * GLA_DK, (h + 1) * GLA_DK)
            s128 = slice(h * GLA_DV, (h + 1) * GLA_DV)
            qh, kh, vh, eblh = gq[:, s64], gk[:, s64], v[:, s128], ebl[:, s64]
            st = state[h]
            st_ref[0, h] = st
            a = jnp.where(tril, _mm_nt(qh, kh), 0.0)
            o = _mm(a, vh) + _mm_nt(qh, st)
            state[h] = st * eblh + _mm_tn(vh, kh * eblh)
            oraw.append(o)
            rstd = lax.rsqrt(jnp.mean(o * o, axis=-1, keepdims=True) + EPS)
            rh = r[:, s128]
            og.append(o * rstd * gnw_v * (rh * _sigmoid(rh)))
        oraw_ref[...] = jnp.concatenate(oraw, axis=1)
        og_ref[...] = jnp.concatenate(og, axis=1).astype(ACT_DTYPE)

    nb = lambda w, col: pl.BlockSpec((GLA_CHUNK, w), lambda c: (c, col // w))
    const = lambda shape: pl.BlockSpec(shape, lambda c: (0,) * len(shape))
    return pl.pallas_call(
        body, name="gla_fwd", grid=(nc,),
        in_specs=[nb(256, C_GQ), nb(256, C_GK), nb(512, C_GV), nb(512, C_GR), nb(128, C_LR),
                  const((128, 256)), const((1, 256)), const((1, 128))],
        out_specs=[pl.BlockSpec((GLA_CHUNK, 512), lambda c: (c, 0)), pl.BlockSpec((GLA_CHUNK, 512), lambda c: (c, 0)),
                   pl.BlockSpec((1, GLA_HEADS, GLA_DV, GLA_DK), lambda c: (c, 0, 0, 0))],
        out_shape=[jax.ShapeDtypeStruct((rows, 512), F32), jax.ShapeDtypeStruct((rows, 512), ACT_DTYPE),
                   jax.ShapeDtypeStruct((nc, GLA_HEADS, GLA_DV, GLA_DK), F32)],
        scratch_shapes=[pltpu.VMEM((GLA_HEADS, GLA_DV, GLA_DK), F32)],
        compiler_params=_cp(("arbitrary",)),
    )(proj, proj, proj, proj, proj, wg_p, bg, gnw)


def _swa_mask(n):
    qi = lax.broadcasted_iota(jnp.int32, (SWA_BLOCK, 3 * SWA_BLOCK), 0)
    jj = lax.broadcasted_iota(jnp.int32, (SWA_BLOCK, 3 * SWA_BLOCK), 1)
    meta = (jj < SWA_BLOCK) & (jj >= META0) & ((n > 0) | (jj <= qi))
    prev = (jj >= SWA_BLOCK) & (jj < 2 * SWA_BLOCK) & (n >= 2) & (jj - SWA_BLOCK > qi)
    cur = (jj >= 2 * SWA_BLOCK) & (n >= 1) & (jj - 2 * SWA_BLOCK <= qi)
    return meta | prev | cur


def _swa_specs():
    blk = lambda w: pl.BlockSpec((SWA_BLOCK, w), lambda n: (n, 0))
    first = pl.BlockSpec((SWA_BLOCK, 128), lambda n: (0, 0))
    prev = pl.BlockSpec((SWA_BLOCK, 128), lambda n: (jnp.maximum(n - 1, 0), 0))
    return blk, first, prev


def _swa_fwd(qr, kr, vr, sinks):
    rows = qr.shape[0]
    nblk = rows // SWA_BLOCK

    def body(q_ref, k0, kp, kc, v0, vp, vc, sink_ref, o_ref):
        n = pl.program_id(0)
        q = q_ref[...]
        kall = jnp.concatenate([k0[...], kp[...], kc[...]], axis=0)
        vall = jnp.concatenate([v0[...], vp[...], vc[...]], axis=0)
        mask = _swa_mask(n)
        outs = []
        for head in range(SWA_HEADS):
            kv = slice((head // SWA_GROUP) * SWA_HD, (head // SWA_GROUP + 1) * SWA_HD)
            s = jnp.where(mask, _mm_nt(q[:, head * SWA_HD:(head + 1) * SWA_HD], kall[:, kv]), NEG)
            sink = sink_ref[0, head]
            m = jnp.maximum(jnp.max(s, axis=-1, keepdims=True), sink)
            p = jnp.exp(s - m)
            den = jnp.sum(p, axis=-1, keepdims=True) + jnp.exp(sink - m)
            outs.append(_mm(p, vall[:, kv]) / den)
        o_ref[...] = jnp.concatenate(outs, axis=1).astype(ACT_DTYPE)

    blk, first, prev = _swa_specs()
    return pl.pallas_call(
        body, name="swa_fwd", grid=(nblk,),
        in_specs=[blk(512), first, prev, blk(128), first, prev, blk(128),
                  pl.BlockSpec(memory_space=pltpu.SMEM)],
        out_specs=blk(512),
        out_shape=jax.ShapeDtypeStruct((rows, 512), ACT_DTYPE),
        compiler_params=_cp(("arbitrary",)),
    )(qr, kr, kr, kr, vr, vr, vr, sinks)


def _out_proj(h0, og, osw, wout, nfw, tm):
    rows = h0.shape[0]

    def body(h_ref, og_ref, os_ref, w_ref, nw_ref, h1_ref, f_ref):
        h1 = h_ref[...] + _mm(og_ref[...], w_ref[0:512, :]) + _mm(os_ref[...], w_ref[512:1024, :])
        h1_ref[...] = h1
        rstd = lax.rsqrt(jnp.mean(h1 * h1, axis=-1, keepdims=True) + EPS)
        f_ref[...] = (h1 * rstd * nw_ref[...]).astype(ACT_DTYPE)

    row = lambda w: pl.BlockSpec((tm, w), lambda i: (i, 0))
    return pl.pallas_call(
        body, name="out_proj", grid=(rows // tm,),
        in_specs=[row(D), row(512), row(512), pl.BlockSpec((D, D), lambda i: (0, 0)), pl.BlockSpec((1, D), lambda i: (0, 0))],
        out_specs=[row(D), row(D)],
        out_shape=[jax.ShapeDtypeStruct((rows, D), F32), jax.ShapeDtypeStruct((rows, D), ACT_DTYPE)],
        compiler_params=_cp(("arbitrary",), 48),
    )(h0, og, osw, wout, nfw)


def _ffn_fwd(f, h1, w1g, w2, tgt, fnw, tm):
    rows = f.shape[0]
    nj = N_DEV

    def body(f_ref, h1_ref, w1_ref, w2_ref, t_ref, nw_ref, a_ref, dh2_ref, loss_ref, gfn_ref, acc):
        i, j = pl.program_id(0), pl.program_id(1)

        @pl.when((i == 0) & (j == 0))
        def _():
            loss_ref[...] = jnp.zeros_like(loss_ref)
            gfn_ref[...] = jnp.zeros_like(gfn_ref)

        @pl.when(j == 0)
        def _():
            acc[...] = jnp.zeros_like(acc)

        a = _mm(f_ref[...], w1_ref[...])
        a_ref[...] = a.astype(ACT_DTYPE)
        z = jnp.square(jnp.maximum(a, 0.0))
        acc[...] += _mm(z, w2_ref[...])

        @pl.when(j == nj - 1)
        def _():
            h2 = h1_ref[...] + acc[...]
            rstd = lax.rsqrt(jnp.mean(h2 * h2, axis=-1, keepdims=True) + EPS)
            hn = h2 * rstd
            nw = nw_ref[...]
            row = i * tm + lax.broadcasted_iota(jnp.int32, (tm, 1), 0)
            err = jnp.where(row >= LEAD, hn * nw - t_ref[...], 0.0)
            row_loss = jnp.sum(err * err, axis=-1, keepdims=True) * (1.0 / D)
            loss_ref[...] += jnp.broadcast_to(0.5 * jnp.sum(row_loss, axis=0, keepdims=True), loss_ref.shape)
            dy = err * (1.0 / D)
            gfn_ref[...] += jnp.broadcast_to(jnp.sum(dy * hn, axis=0, keepdims=True), gfn_ref.shape)
            dhn = dy * nw
            dh2_ref[...] = rstd * (dhn - hn * jnp.mean(dhn * hn, axis=-1, keepdims=True))

    return pl.pallas_call(
        body, name="ffn_fwd", grid=(rows // tm, nj),
        in_specs=[pl.BlockSpec((tm, D), lambda i, j: (i, 0)), pl.BlockSpec((tm, D), lambda i, j: (i, 0)),
                  pl.BlockSpec((None, D, FF_TILE), lambda i, j: (j, 0, 0)),
                  pl.BlockSpec((FF_TILE, D), lambda i, j: (j, 0)),
                  pl.BlockSpec((tm, D), lambda i, j: (i, 0)), pl.BlockSpec((1, D), lambda i, j: (0, 0))],
        out_specs=[pl.BlockSpec((tm, FF_TILE), lambda i, j: (i, j)), pl.BlockSpec((tm, D), lambda i, j: (i, 0)),
                   pl.BlockSpec((8, 128), lambda i, j: (0, 0)), pl.BlockSpec((8, D), lambda i, j: (0, 0))],
        out_shape=[jax.ShapeDtypeStruct((rows, D_FF), ACT_DTYPE), jax.ShapeDtypeStruct((rows, D), F32),
                   jax.ShapeDtypeStruct((8, 128), F32), jax.ShapeDtypeStruct((8, D), F32)],
        scratch_shapes=[pltpu.VMEM((tm, D), F32)],
        compiler_params=_cp(("arbitrary", "arbitrary"), 48),
    )(f, h1, w1g, w2, tgt, fnw)


def _ffn_bwd_act(dh2, a, w1g, w2, h1, nfw, tm):
    rows = dh2.shape[0]
    nj = N_DEV

    def body(dh2_ref, a_ref, w1_ref, w2_ref, h1_ref, nw_ref, da_ref, dh1_ref, gnf_ref, acc):
        i, j = pl.program_id(0), pl.program_id(1)

        @pl.when((i == 0) & (j == 0))
        def _():
            gnf_ref[...] = jnp.zeros_like(gnf_ref)

        @pl.when(j == 0)
        def _():
            acc[...] = jnp.zeros_like(acc)

        dz = _mm_nt(dh2_ref[...], w2_ref[...])
        da = dz * (2.0 * jnp.maximum(a_ref[...].astype(F32), 0.0))
        da_ref[...] = da.astype(ACT_DTYPE)
        acc[...] += _mm_nt(da, w1_ref[...])

        @pl.when(j == nj - 1)
        def _():
            h1 = h1_ref[...]
            rstd = lax.rsqrt(jnp.mean(h1 * h1, axis=-1, keepdims=True) + EPS)
            hn = h1 * rstd
            df = acc[...]
            gnf_ref[...] += jnp.broadcast_to(jnp.sum(df * hn, axis=0, keepdims=True), gnf_ref.shape)
            dfn = df * nw_ref[...]
            dh1_ref[...] = dh2_ref[...] + rstd * (dfn - hn * jnp.mean(dfn * hn, axis=-1, keepdims=True))

    return pl.pallas_call(
        body, name="ffn_bwd_act", grid=(rows // tm, nj),
        in_specs=[pl.BlockSpec((tm, D), lambda i, j: (i, 0)), pl.BlockSpec((tm, FF_TILE), lambda i, j: (i, j)),
                  pl.BlockSpec((None, D, FF_TILE), lambda i, j: (j, 0, 0)),
                  pl.BlockSpec((FF_TILE, D), lambda i, j: (j, 0)),
                  pl.BlockSpec((tm, D), lambda i, j: (i, 0)), pl.BlockSpec((1, D), lambda i, j: (0, 0))],
        out_specs=[pl.BlockSpec((tm, FF_TILE), lambda i, j: (i, j)), pl.BlockSpec((tm, D), lambda i, j: (i, 0)),
                   pl.BlockSpec((8, D), lambda i, j: (0, 0))],
        out_shape=[jax.ShapeDtypeStruct((rows, D_FF), ACT_DTYPE), jax.ShapeDtypeStruct((rows, D), F32),
                   jax.ShapeDtypeStruct((8, D), F32)],
        scratch_shapes=[pltpu.VMEM((tm, D), F32)],
        compiler_params=_cp(("arbitrary", "arbitrary"), 48),
    )(dh2, a, w1g, w2, h1, nfw)


def _ffn_bwd_weights(f, a, da, dh2, tm):
    rows = f.shape[0]

    def body(f_ref, a_ref, da_ref, dh2_ref, dw1_ref, dw2_ref):
        i = pl.program_id(1)

        @pl.when(i == 0)
        def _():
            dw1_ref[...] = jnp.zeros_like(dw1_ref)
            dw2_ref[...] = jnp.zeros_like(dw2_ref)

        z = jnp.square(jnp.maximum(a_ref[...].astype(F32), 0.0))
        dw1_ref[...] += _mm_tn(f_ref[...], da_ref[...])
        dw2_ref[...] += _mm_tn(z, dh2_ref[...])

    return pl.pallas_call(
        body, name="ffn_bwd_weights", grid=(N_DEV, rows // tm),
        in_specs=[pl.BlockSpec((tm, D), lambda j, i: (i, 0)), pl.BlockSpec((tm, FF_TILE), lambda j, i: (i, j)),
                  pl.BlockSpec((tm, FF_TILE), lambda j, i: (i, j)), pl.BlockSpec((tm, D), lambda j, i: (i, 0))],
        out_specs=[pl.BlockSpec((None, None, D, FF_TILE), lambda j, i: (j % 2, j // 2, 0, 0)),
                   pl.BlockSpec((None, None, FF_TILE, D), lambda j, i: (j % 2, j // 2, 0, 0))],
        out_shape=[jax.ShapeDtypeStruct((2, 4, D, FF_TILE), F32), jax.ShapeDtypeStruct((2, 4, FF_TILE, D), F32)],
        compiler_params=_cp(("arbitrary", "arbitrary"), 48),
    )(f, a, da, dh2)


def _out_proj_bwd(dh1, og, osw, wout, tm):
    rows = dh1.shape[0]

    def body(dh1_ref, og_ref, os_ref, w_ref, dog_ref, dos_ref, dw_ref):
        i = pl.program_id(0)

        @pl.when(i == 0)
        def _():
            dw_ref[...] = jnp.zeros_like(dw_ref)

        dh1 = dh1_ref[...].astype(MXU_DTYPE)
        dog_ref[...] = _mm_nt(dh1, w_ref[0:512, :])
        dos_ref[...] = _mm_nt(dh1, w_ref[512:1024, :])
        for half, ref in enumerate((og_ref, os_ref)):
            dw = _mm_tn(ref[...], dh1)
            for blk in range(4):
                shard = half * 4 + blk
                dw_ref[shard % 2, shard // 2] += dw[blk * 128:(blk + 1) * 128, :]

    row = lambda w: pl.BlockSpec((tm, w), lambda i: (i, 0))
    return pl.pallas_call(
        body, name="out_proj_bwd", grid=(rows // tm,),
        in_specs=[row(D), row(512), row(512), pl.BlockSpec((D, D), lambda i: (0, 0))],
        out_specs=[row(512), row(512), pl.BlockSpec((2, 4, 128, D), lambda i: (0, 0, 0, 0))],
        out_shape=[jax.ShapeDtypeStruct((rows, 512), F32), jax.ShapeDtypeStruct((rows, 512), F32),
                   jax.ShapeDtypeStruct((2, 4, 128, D), F32)],
        compiler_params=_cp(("arbitrary",), 48),
    )(dh1, og, osw, wout)


def _swa_bwd(qr, kr, vr, osw, dos, sinks):
    rows = qr.shape[0]
    nblk = rows // SWA_BLOCK

    def body(q_ref, k0, kp, kc, v0, vp, vc, o_ref, do_ref, sink_ref, dq_ref, dk_ref, dv_ref, dsink_ref):
        n = pl.program_id(0)

        @pl.when(n == 0)
        def _():
            dk_ref[...] = jnp.zeros_like(dk_ref)
            dv_ref[...] = jnp.zeros_like(dv_ref)
            dsink_ref[...] = jnp.zeros_like(dsink_ref)

        q = q_ref[...]
        kall = jnp.concatenate([k0[...], kp[...], kc[...]], axis=0)
        vall = jnp.concatenate([v0[...], vp[...], vc[...]], axis=0)
        mask = _swa_mask(n)
        do_all = do_ref[...]
        o_all = o_ref[...].astype(F32)
        dq, dk, dv = [], [], []
        for kvh in range(SWA_KV):
            kv = slice(kvh * SWA_HD, (kvh + 1) * SWA_HD)
            dk_h = jnp.zeros((3 * SWA_BLOCK, SWA_HD), F32)
            dv_h = jnp.zeros((3 * SWA_BLOCK, SWA_HD), F32)
            for g in range(SWA_GROUP):
                head = kvh * SWA_GROUP + g
                hs = slice(head * SWA_HD, (head + 1) * SWA_HD)
                qh, doh = q[:, hs], do_all[:, hs]
                s = jnp.where(mask, _mm_nt(qh, kall[:, kv]), NEG)
                sink = sink_ref[0, head]
                m = jnp.maximum(jnp.max(s, axis=-1, keepdims=True), sink)
                e = jnp.exp(s - m)
                inv = 1.0 / (jnp.sum(e, axis=-1, keepdims=True) + jnp.exp(sink - m))
                p = e * inv
                delta = jnp.sum(doh * o_all[:, hs], axis=-1, keepdims=True)
                ds = p * (_mm_nt(doh, vall[:, kv]) - delta)
                dq.append(_mm(ds, kall[:, kv]))
                dk_h = dk_h + _mm_tn(ds, qh)
                dv_h = dv_h + _mm_tn(p, doh)
                dsink = -jnp.sum(jnp.exp(sink - m) * inv * delta, axis=0, keepdims=True)
                dsink_ref[head:head + 1, :] += jnp.broadcast_to(dsink, (1, 128))
            dk.append(dk_h)
            dv.append(dv_h)
        dq_ref[...] = jnp.concatenate(dq, axis=1)
        dk_all = jnp.concatenate(dk, axis=1)
        dv_all = jnp.concatenate(dv, axis=1)
        prev0 = pl.multiple_of(jnp.maximum(n - 1, 0) * SWA_BLOCK, SWA_BLOCK)
        cur0 = pl.multiple_of(n * SWA_BLOCK, SWA_BLOCK)
        for ref, val in ((dk_ref, dk_all), (dv_ref, dv_all)):
            ref[0:SWA_BLOCK, :] += val[0:SWA_BLOCK]
            ref[pl.ds(prev0, SWA_BLOCK), :] += val[SWA_BLOCK:2 * SWA_BLOCK]
            ref[pl.ds(cur0, SWA_BLOCK), :] += val[2 * SWA_BLOCK:]

    blk, first, prev = _swa_specs()
    whole = pl.BlockSpec((rows, 128), lambda n: (0, 0))
    return pl.pallas_call(
        body, name="swa_bwd", grid=(nblk,),
        in_specs=[blk(512), first, prev, blk(128), first, prev, blk(128), blk(512), blk(512),
                  pl.BlockSpec(memory_space=pltpu.SMEM)],
        out_specs=[blk(512), whole, whole, pl.BlockSpec((8, 128), lambda n: (0, 0))],
        out_shape=[jax.ShapeDtypeStruct((rows, 512), F32), jax.ShapeDtypeStruct((rows, 128), F32),
                   jax.ShapeDtypeStruct((rows, 128), F32), jax.ShapeDtypeStruct((8, 128), F32)],
        compiler_params=_cp(("arbitrary",), 48),
    )(qr, kr, kr, kr, vr, vr, vr, osw, dos, sinks)


def _gla_bwd(proj, oraw, states, dog, wg_p, bg, gnw):
    rows = proj.shape[0]
    nc = rows // GLA_CHUNK

    def body(q_ref, k_ref, v_ref, r_ref, lr_ref, oraw_ref, st_ref, dog_ref, wg_ref, bg_ref, gnw_ref,
             dq_ref, dk_ref, dv_ref, dr_ref, dlr_ref, dwg_ref, dbg_ref, dgnw_ref, dstate):
        t = pl.program_id(0)
        c = nc - 1 - t

        @pl.when(t == 0)
        def _():
            dstate[...] = jnp.zeros_like(dstate)
            dwg_ref[...] = jnp.zeros_like(dwg_ref)
            dbg_ref[...] = jnp.zeros_like(dbg_ref)
            dgnw_ref[...] = jnp.zeros_like(dgnw_ref)

        lr, wg = lr_ref[...], wg_ref[...]
        zg, live, tril, b = _gla_gates(lr, wg, bg_ref[...], c)
        eb, enb = jnp.exp(b), jnp.exp(-b)
        scale = GLA_DK ** -0.5
        gq = q_ref[...] * scale * eb
        gk = k_ref[...] * enb
        ebl = eb[GLA_CHUNK - 1:GLA_CHUNK, :]
        v, r, oraw_v, dog_v = v_ref[...], r_ref[...], oraw_ref[...], dog_ref[...]
        gnw_v = gnw_ref[...]
        is_last = lax.broadcasted_iota(jnp.int32, (GLA_CHUNK, 1), 0) == GLA_CHUNK - 1
        dq, dk, dv, dr, db = [], [], [], [], []
        dgnw = jnp.zeros((1, GLA_DV), F32)
        for h in range(GLA_HEADS):
            s64 = slice(h * GLA_DK, (h + 1) * GLA_DK)
            s128 = slice(h * GLA_DV, (h + 1) * GLA_DV)
            qh, kh, vh, eblh = gq[:, s64], gk[:, s64], v[:, s128], ebl[:, s64]
            klh = kh * eblh
            st = st_ref[0, h]
            o, rh, dout = oraw_v[:, s128], r[:, s128], dog_v[:, s128]
            rstd = lax.rsqrt(jnp.mean(o * o, axis=-1, keepdims=True) + EPS)
            on = o * rstd
            sg = _sigmoid(rh)
            dr.append(dout * (on * gnw_v) * (sg * (1.0 + rh * (1.0 - sg))))
            dy = dout * (rh * sg)
            dgnw = dgnw + jnp.sum(dy * on, axis=0, keepdims=True)
            don = dy * gnw_v
            do = rstd * (don - on * jnp.mean(don * on, axis=-1, keepdims=True))
            a = jnp.where(tril, _mm_nt(qh, kh), 0.0)
            da = jnp.where(tril, _mm_nt(do, vh), 0.0)
            dsp = dstate[h]
            dkl = _mm(vh, dsp)
            dv.append(_mm_tn(a, do) + _mm_nt(klh, dsp))
            debl = jnp.sum(dsp * st, axis=0, keepdims=True)
            dgq = _mm(da, kh) + _mm(do, st)
            dgk = _mm_tn(da, qh)
            dstate[h] = dsp * eblh + _mm_tn(do, qh)
            dq.append(dgq * (scale * eb[:, s64]))
            dk.append((dgk + dkl * eblh) * enb[:, s64])
            last = debl * eblh + jnp.sum(dkl * klh, axis=0, keepdims=True)
            db.append(dgq * qh - dgk * kh - dkl * klh + jnp.where(is_last, last, 0.0))
        dq_ref[...] = jnp.concatenate(dq, axis=1).astype(ACT_DTYPE)
        dk_ref[...] = jnp.concatenate(dk, axis=1).astype(ACT_DTYPE)
        dv_ref[...] = jnp.concatenate(dv, axis=1).astype(ACT_DTYPE)
        dr_ref[...] = jnp.concatenate(dr, axis=1).astype(ACT_DTYPE)
        triu = jnp.logical_not(tril) | (lax.broadcasted_iota(jnp.int32, (GLA_CHUNK, GLA_CHUNK), 0)
                                        == lax.broadcasted_iota(jnp.int32, (GLA_CHUNK, GLA_CHUNK), 1))
        dg = jnp.dot(triu.astype(F32), jnp.concatenate(db, axis=1), precision=HIGHEST, preferred_element_type=F32)
        dzg = jnp.where(live, dg * _sigmoid(-zg) * (1.0 / GLA_TAU), 0.0)
        dlr_ref[...] = _mm_nt(dzg, wg).astype(ACT_DTYPE)
        dwg_ref[...] += _mm_tn(lr, dzg)
        dbg_ref[...] += jnp.broadcast_to(jnp.sum(dzg, axis=0, keepdims=True), dbg_ref.shape)
        dgnw_ref[...] += jnp.broadcast_to(dgnw, dgnw_ref.shape)

    nb = lambda w, col: pl.BlockSpec((GLA_CHUNK, w), lambda t: (nc - 1 - t, col // w))
    const = lambda shape: pl.BlockSpec(shape, lambda t: (0,) * len(shape))
    return pl.pallas_call(
        body, name="gla_bwd", grid=(nc,),
        in_specs=[nb(256, C_GQ), nb(256, C_GK), nb(512, C_GV), nb(512, C_GR), nb(128, C_LR), nb(512, 0),
                  pl.BlockSpec((1, GLA_HEADS, GLA_DV, GLA_DK), lambda t: (nc - 1 - t, 0, 0, 0)), nb(512, 0),
                  const((128, 256)), const((1, 256)), const((1, 128))],
        out_specs=[nb(256, 0), nb(256, 0), nb(512, 0), nb(512, 0), nb(128, 0),
                   const((128, 256)), const((8, 256)), const((8, 128))],
        out_shape=[jax.ShapeDtypeStruct((rows, 256), ACT_DTYPE), jax.ShapeDtypeStruct((rows, 256), ACT_DTYPE),
                   jax.ShapeDtypeStruct((rows, 512), ACT_DTYPE), jax.ShapeDtypeStruct((rows, 512), ACT_DTYPE),
                   jax.ShapeDtypeStruct((rows, 128), ACT_DTYPE), jax.ShapeDtypeStruct((128, 256), F32),
                   jax.ShapeDtypeStruct((8, 256), F32), jax.ShapeDtypeStruct((8, 128), F32)],
        scratch_shapes=[pltpu.VMEM((GLA_HEADS, GLA_DV, GLA_DK), F32)],
        compiler_params=_cp(("arbitrary",)),
    )(proj, proj, proj, proj, proj, oraw, states, dog, wg_p, bg, gnw)


def _in_proj_bwd(h0, dh1, nw, win_p, dgv, dgr, dsq, dgq, dgk, dsk, dsv, dlr, tabs, tm):
    rows = h0.shape[0]

    def body(h_ref, dh1_ref, nw_ref, w_ref, dgv_ref, dgr_ref, dsq_ref, dgq_ref, dgk_ref, dsk_ref, dsv_ref, dlr_ref,
             c_ref, sa_ref, sb_ref, dh0_ref, dw_ref, gnm_ref):
        i = pl.program_id(0)

        @pl.when(i == 0)
        def _():
            dw_ref[...] = jnp.zeros_like(dw_ref)
            gnm_ref[...] = jnp.zeros_like(gnm_ref)

        cos, sa, sb = c_ref[...], sa_ref[...], sb_ref[...]
        dsq_v = (_unrope(dsq_ref[...], cos, sa, sb) * (SWA_HD ** -0.5)).astype(MXU_DTYPE)
        dsk_v = _unrope(dsk_ref[...], cos, sa, sb).astype(MXU_DTYPE)
        dproj = jnp.concatenate(
            [dgv_ref[...].astype(MXU_DTYPE), dgr_ref[...].astype(MXU_DTYPE), dsq_v, dgq_ref[...].astype(MXU_DTYPE),
             dgk_ref[...].astype(MXU_DTYPE), dsk_v, dsv_ref[...].astype(MXU_DTYPE), dlr_ref[...].astype(MXU_DTYPE)],
            axis=1)
        h = h_ref[...]
        rstd = lax.rsqrt(jnp.mean(h * h, axis=-1, keepdims=True) + EPS)
        hn = h * rstd
        nw_v = nw_ref[...]
        u = (hn * nw_v).astype(MXU_DTYPE)
        du = _mm_nt(dproj, w_ref[...])
        dw_ref[...] += _mm_tn(u, dproj)
        gnm_ref[...] += jnp.broadcast_to(jnp.sum(du * hn, axis=0, keepdims=True), gnm_ref.shape)
        dun = du * nw_v
        dh0_ref[...] = dh1_ref[...] + rstd * (dun - hn * jnp.mean(dun * hn, axis=-1, keepdims=True))

    row = lambda w: pl.BlockSpec((tm, w), lambda i: (i, 0))
    return pl.pallas_call(
        body, name="in_proj_bwd", grid=(rows // tm,),
        in_specs=[row(D), row(D), pl.BlockSpec((1, D), lambda i: (0, 0)), pl.BlockSpec((D, DINP), lambda i: (0, 0)),
                  row(512), row(512), row(512), row(256), row(256), row(128), row(128), row(128),
                  row(128), row(128), row(128)],
        out_specs=[row(D), pl.BlockSpec((D, DINP), lambda i: (0, 0)), pl.BlockSpec((8, D), lambda i: (0, 0))],
        out_shape=[jax.ShapeDtypeStruct((rows, D), F32), jax.ShapeDtypeStruct((D, DINP), F32),
                   jax.ShapeDtypeStruct((8, D), F32)],
        compiler_params=_cp(("arbitrary",), 56),
    )(h0, dh1, nw, win_p, dgv, dgr, dsq, dgq, dgk, dsk, dsv, dlr, *tabs)


def _adamw(w, g, m, v):
    m = ADAM_B1 * m + (1.0 - ADAM_B1) * g
    v = ADAM_B2 * v + (1.0 - ADAM_B2) * jnp.square(g)
    m_hat = m / (1.0 - ADAM_B1 ** ADAM_STEP)
    v_hat = v / (1.0 - ADAM_B2 ** ADAM_STEP)
    delta = -ADAM_LR * (m_hat / (jnp.sqrt(v_hat) + ADAM_EPS) + ADAM_WD * w)
    return delta, m, v


def _adamw_shard(parts, w, m, v, name):
    r, cdim = w.shape
    tr = 128 if r % 128 == 0 else r

    def body(p_ref, w_ref, m_ref, v_ref, g_ref, d_ref, nm_ref, nv_ref):
        g = ((p_ref[0] + p_ref[1]) + p_ref[2]) + p_ref[3]
        g_ref[...] = g
        d_ref[...], nm_ref[...], nv_ref[...] = _adamw(w_ref[...], g, m_ref[...], v_ref[...])

    spec = pl.BlockSpec((tr, cdim), lambda i: (i, 0))
    shape = jax.ShapeDtypeStruct((r, cdim), F32)
    return pl.pallas_call(
        body, name=name, grid=(r // tr,),
        in_specs=[pl.BlockSpec((4, tr, cdim), lambda i: (0, i, 0)), spec, spec, spec],
        out_specs=[spec] * 4, out_shape=[shape] * 4,
        compiler_params=_cp(("arbitrary",)),
    )(parts, w, m, v)


def _adamw_small(w, g, m, v):
    def body(w_ref, g_ref, m_ref, v_ref, d_ref, nm_ref, nv_ref):
        d_ref[...], nm_ref[...], nv_ref[...] = _adamw(w_ref[...], g_ref[...], m_ref[...], v_ref[...])

    vm = pl.BlockSpec(memory_space=pltpu.VMEM)
    shape = jax.ShapeDtypeStruct(w.shape, F32)
    return pl.pallas_call(body, name="adamw_small", in_specs=[vm] * 4, out_specs=[vm] * 3,
                          out_shape=[shape] * 3)(w, g, m, v)


def _add(a, b, name):
    _, r, cdim = a.shape
    tr = 128 if r % 128 == 0 else r

    def body(a_ref, b_ref, o_ref):
        o_ref[...] = a_ref[...] + b_ref[...]

    spec = pl.BlockSpec((4, tr, cdim), lambda i: (0, i, 0))
    return pl.pallas_call(body, name=name, grid=(r // tr,), in_specs=[spec, spec], out_specs=spec,
                          out_shape=jax.ShapeDtypeStruct(a.shape, F32), compiler_params=_cp(("arbitrary",)))(a, b)


def _to_rows128(a):
    return a.reshape(-1, 128)


def _pad_rows128(vec):
    flat = vec.reshape(-1)
    n = -(-flat.shape[0] // 128)
    return jnp.pad(flat, (0, n * 128 - flat.shape[0])).reshape(n, 128)


def kernel(x, meta_tokens, norm_mix_w, w_in, w_gate_up, b_gate, gla_norm_w, sinks, w_out, norm_ff_w, w_ff1, w_ff2, final_norm_w, loss_target, m_meta_tokens, m_norm_mix_w, m_w_in, m_w_gate_up, m_b_gate, m_gla_norm_w, m_sinks, m_w_out, m_norm_ff_w, m_w_ff1, m_w_ff2, m_final_norm_w, v_meta_tokens, v_norm_mix_w, v_w_in, v_w_gate_up, v_b_gate, v_gla_norm_w, v_sinks, v_w_out, v_norm_ff_w, v_w_ff1, v_w_ff2, v_final_norm_w):
    seq = x.shape[1]
    rows = LEAD + seq
    tm = _row_tile(rows)
    tm_small = tm // 2 if tm == 640 else tm
    dev = 4 * lax.axis_index("x") + 2 * lax.axis_index("y") + lax.axis_index("c")

    small_shard = jnp.concatenate([meta_tokens, w_gate_up[0], jnp.zeros((N_META, 96), F32)], axis=1)
    g_in, g_out, g_w1, g_w2, g_small = _all_gather(
        [w_in[0].astype(WIRE_DTYPE), w_out[0].astype(WIRE_DTYPE), w_ff1[0].astype(WIRE_DTYPE),
         w_ff2[0].astype(WIRE_DTYPE), small_shard])
    win_full = jnp.transpose(g_in, (1, 0, 2)).reshape(D, DIN)
    cols = lambda r: win_full[:, r[0]:r[1]]
    win_p = jnp.concatenate([cols(O_GV), cols(O_GR), cols(O_SQ), cols(O_GQ), cols(O_GK), cols(O_SK), cols(O_SV),
                             cols(O_LR), jnp.zeros((D, 128 - GLA_RANK), WIRE_DTYPE)], axis=1)
    wout_full = g_out.reshape(D, D)
    w2_full = g_w2.reshape(D_FF, D)
    meta_full = jnp.transpose(g_small[:, :, 0:128], (1, 0, 2)).reshape(N_META, D)
    wg_full = jnp.transpose(g_small[:, :, 128:160], (1, 0, 2)).reshape(GLA_RANK, GLA_HEADS * GLA_DK)
    wg_p = jnp.concatenate([wg_full, jnp.zeros((128 - GLA_RANK, 256), F32)], axis=0)

    h0 = jnp.concatenate([jnp.zeros((META0, D), F32), meta_full, x[0]], axis=0)
    tgt = jnp.concatenate([jnp.zeros((LEAD, D), F32), loss_target[0]], axis=0)
    tabs = _rope_tables(rows)
    proj = _in_proj(h0, norm_mix_w, win_p, tm)
    oraw, og, states = _gla_fwd(proj, wg_p, b_gate, gla_norm_w)
    qr, kr, vr = _swa_prep(proj, tabs, tm)
    osw = _swa_fwd(qr, kr, vr, sinks)
    h1, f = _out_proj(h0, og, osw, wout_full, norm_ff_w, tm)
    a, dh2, loss_p, gfn_p = _ffn_fwd(f, h1, g_w1, w2_full, tgt, final_norm_w.reshape(1, D), tm)

    da, dh1, gnf_p = _ffn_bwd_act(dh2, a, g_w1, w2_full, h1, norm_ff_w, tm)
    dw1, dw2 = _ffn_bwd_weights(f, a, da, dh2, tm)
    dog, dos, dwout = _out_proj_bwd(dh1, og, osw, wout_full, tm)
    dsq, dsk, dsv, dsink_p = _swa_bwd(qr, kr, vr, osw, dos, sinks)
    dgq, dgk, dgv, dgr, dlr, dwg_p, dbg_p, dgnw_p = _gla_bwd(proj, oraw, states, dog, wg_p, b_gate, gla_norm_w)
    dh0, dwin_p, gnm_p = _in_proj_bwd(h0, dh1, norm_mix_w, win_p, dgv, dgr, dsq, dgq, dgk, dsk, dsv, dlr, tabs, tm_small)
    grad_x = dh0[LEAD:][None]

    pcols = lambda c0, r: dwin_p[:, c0:c0 + (r[1] - r[0])]
    dwin = jnp.concatenate([pcols(C_GQ, O_GQ), pcols(C_GK, O_GK), pcols(C_GV, O_GV), pcols(C_GR, O_GR),
                            pcols(C_LR, O_LR), pcols(C_SQ, O_SQ), pcols(C_SK, O_SK), pcols(C_SV, O_SV)], axis=1)
    dwin = jnp.transpose(dwin.reshape(D, 4, 2, DIN // N_DEV), (2, 1, 0, 3))
    mine, theirs = _rs_sibling([dwin, dwout, dw1, dw2])
    chip_sums = [_add(p, q, "reduce_pair_%d" % k) for k, (p, q) in enumerate(zip(mine, theirs))]
    parts = _rs_chips(chip_sums)

    small = [dh0[META0:LEAD], dwg_p[0:GLA_RANK], gnm_p[0:1], dbg_p[0:1], dgnw_p[0:1], dsink_p[:, 0], gnf_p[0:1],
             gfn_p[0:1], loss_p[0:1, 0:1]]
    sizes = [-(-s.size // 128) for s in small]
    pack = jnp.concatenate([_pad_rows128(s) for s in small], axis=0)
    pad_rows = -pack.shape[0] % 8
    pack = jnp.pad(pack, ((0, pad_rows), (0, 0)))
    total = _all_reduce_small(pack)
    offs = [sum(sizes[:k]) for k in range(len(sizes))]
    take = lambda k, shape: total[offs[k]:offs[k] + sizes[k]].reshape(-1)[:small[k].size].reshape(shape)
    g_meta_full = take(0, (N_META, D))
    g_wg_full = take(1, (GLA_RANK, 256))
    g_meta = lax.dynamic_slice_in_dim(g_meta_full, dev * 128, 128, axis=1)
    g_wg = lax.dynamic_slice_in_dim(g_wg_full, dev * 32, 32, axis=1)[None]
    g_norm_mix, g_b_gate, g_gla_norm = take(2, (1, D)), take(3, (1, 256)), take(4, (1, 128))
    g_sinks, g_norm_ff, g_final_norm = take(5, (1, 8)), take(6, (1, D)), take(7, (D,))
    loss = take(8, ())

    g_win, d_win, nm_win, nv_win = _adamw_shard(parts[0], w_in[0], m_w_in[0], v_w_in[0], "adamw_w_in")
    g_wout, d_wout, nm_wout, nv_wout = _adamw_shard(parts[1], w_out[0], m_w_out[0], v_w_out[0], "adamw_w_out")
    g_w1s, d_w1, nm_w1, nv_w1 = _adamw_shard(parts[2], w_ff1[0], m_w_ff1[0], v_w_ff1[0], "adamw_w_ff1")
    g_w2s, d_w2, nm_w2, nv_w2 = _adamw_shard(parts[3], w_ff2[0], m_w_ff2[0], v_w_ff2[0], "adamw_w_ff2")

    names = ["meta", "wg", "norm_mix", "b_gate", "gla_norm", "sinks", "norm_ff", "final_norm"]
    ws = [meta_tokens, w_gate_up, norm_mix_w, b_gate, gla_norm_w, sinks, norm_ff_w, final_norm_w]
    gs = [g_meta, g_wg, g_norm_mix, g_b_gate, g_gla_norm, g_sinks, g_norm_ff, g_final_norm]
    ms = [m_meta_tokens, m_w_gate_up, m_norm_mix_w, m_b_gate, m_gla_norm_w, m_sinks, m_norm_ff_w, m_final_norm_w]
    vs = [v_meta_tokens, v_w_gate_up, v_norm_mix_w, v_b_gate, v_gla_norm_w, v_sinks, v_norm_ff_w, v_final_norm_w]
    ssz = [-(-w.size // 128) for w in ws]
    packed = []
    for group in (ws, gs, ms, vs):
        p = jnp.concatenate([_pad_rows128(t) for t in group], axis=0)
        packed.append(jnp.pad(p, ((0, -p.shape[0] % 8), (0, 0))))
    d_s, nm_s, nv_s = _adamw_small(*packed)
    soffs = [sum(ssz[:k]) for k in range(len(ssz))]
    unpack = lambda t, k: t[soffs[k]:soffs[k] + ssz[k]].reshape(-1)[:ws[k].size].reshape(ws[k].shape)
    d_small = {n: unpack(d_s, k) for k, n in enumerate(names)}
    nm_small = {n: unpack(nm_s, k) for k, n in enumerate(names)}
    nv_small = {n: unpack(nv_s, k) for k, n in enumerate(names)}
    g_small_d = dict(zip(names, gs))

    def ordered(big, small_d):
        win_v, wout_v, w1_v, w2_v = big
        return (small_d["meta"], small_d["norm_mix"], win_v[None], small_d["wg"], small_d["b_gate"],
                small_d["gla_norm"], small_d["sinks"], wout_v[None], small_d["norm_ff"], w1_v[None], w2_v[None],
                small_d["final_norm"])

    return (loss, grad_x,
            *ordered((g_win, g_wout, g_w1s, g_w2s), g_small_d),
            *ordered((d_win, d_wout, d_w1, d_w2), d_small),
            *ordered((nm_win, nm_wout, nm_w1, nm_w2), nm_small),
            *ordered((nv_win, nv_wout, nv_w1, nv_w2), nv_small))
```

```python
import jax
import jax.numpy as jnp
from jax import lax
from jax.experimental import pallas as pl
from jax.experimental.pallas import tpu as pltpu

F32 = jnp.float32
MXU_DTYPE = jnp.bfloat16
ACT_DTYPE = jnp.bfloat16
WIRE_DTYPE = jnp.bfloat16

D = 1024
N_META = 16
LEAD = 128
META0 = LEAD - N_META
EPS = 1e-5
GLA_HEADS, GLA_DK, GLA_DV, GLA_RANK, GLA_CHUNK = 4, 64, 128, 16, 64
GLA_TAU = 16.0
SWA_HEADS, SWA_KV, SWA_GROUP, SWA_HD, SWA_BLOCK = 8, 2, 4, 64, 128
ROPE_DIM, ROPE_THETA = 16, 500000.0
D_FF = 4096
N_DEV = 8
FF_TILE = D_FF // N_DEV
NEG = -1e30

C_GV, C_GR, C_SQ, C_GQ, C_GK, C_SK, C_SV, C_LR = 0, 512, 1024, 1536, 1792, 2048, 2176, 2304
DINP = 2432
DIN = 2320
O_GQ, O_GK, O_GV, O_GR, O_LR, O_SQ, O_SK, O_SV = (0, 256), (256, 512), (512, 1024), (1024, 1536), (1536, 1552), (1552, 2064), (2064, 2192), (2192, 2320)

ADAM_LR, ADAM_B1, ADAM_B2, ADAM_EPS, ADAM_WD, ADAM_STEP = 0.001, 0.9, 0.999, 1e-08, 0.01, 10

MESH = pl.DeviceIdType.MESH
ANY = pl.BlockSpec(memory_space=pl.ANY)
HIGHEST = lax.Precision.HIGHEST


def _cp(sem=None, vmem_mb=None):
    kw = {}
    if sem is not None:
        kw["dimension_semantics"] = sem
    if vmem_mb is not None:
        kw["vmem_limit_bytes"] = vmem_mb << 20
    return pltpu.CompilerParams(**kw)


def _mm(a, b):
    return jnp.dot(a.astype(MXU_DTYPE), b.astype(MXU_DTYPE), preferred_element_type=F32)


def _mm_nt(a, b):
    return lax.dot_general(a.astype(MXU_DTYPE), b.astype(MXU_DTYPE), (((1,), (1,)), ((), ())),
                           preferred_element_type=F32)


def _mm_tn(a, b):
    return lax.dot_general(a.astype(MXU_DTYPE), b.astype(MXU_DTYPE), (((0,), (0,)), ((), ())),
                           preferred_element_type=F32)


def _logsigmoid(z):
    return jnp.minimum(z, 0.0) - jnp.log(1.0 + jnp.exp(-jnp.abs(z)))


def _sigmoid(z):
    return 1.0 / (1.0 + jnp.exp(-z))


def _row_tile(rows):
    return 640 if rows % 640 == 0 else 128


def _mesh_pos():
    return lax.axis_index("x"), lax.axis_index("y"), lax.axis_index("c")


def _all_gather(shards):
    n = len(shards)

    def body(*refs):
        ins, outs = refs[:n], refs[n:2 * n]
        send_sems, recv_sems, local_sems = refs[2 * n:]
        x, y, c = _mesh_pos()
        me, sibling = (x, y, c), (x, y, 1 - c)
        chips = [(1 - x, y), (x, 1 - y), (1 - x, 1 - y)]

        def copy(a, k, block, to, src=None):
            dst = outs[a].at[4 * block[0] + 2 * block[1] + block[2]]
            return pltpu.make_async_remote_copy(
                src_ref=dst if src is None else src, dst_ref=dst,
                send_sem=send_sems.at[a * 7 + k], recv_sem=recv_sems.at[a * 7 + k],
                device_id=to, device_id_type=MESH)

        mine = [pltpu.make_async_copy(ins[a], outs[a].at[4 * x + 2 * y + c], local_sems.at[a]) for a in range(n)]
        for cp in mine:
            cp.start()
        first = []
        for a in range(n):
            first.append(copy(a, 0, me, sibling, src=ins[a]))
            first += [copy(a, 1 + j, me, (*chip, c), src=ins[a]) for j, chip in enumerate(chips)]
        for cp in first:
            cp.start()
        passed = []
        for j, chip in enumerate(chips):
            for a in range(n):
                copy(a, 1 + j, (*chip, c), me).wait_recv()
                fwd = copy(a, 4 + j, (*chip, c), sibling)
                fwd.start()
                passed.append(fwd)
        for a in range(n):
            copy(a, 0, sibling, me).wait_recv()
            for j, chip in enumerate(chips):
                copy(a, 4 + j, (*chip, 1 - c), me).wait_recv()
        for cp in first + passed:
            cp.wait_send()
        for cp in mine:
            cp.wait()

    return pl.pallas_call(
        body, name="all_gather_weights",
        out_shape=[jax.ShapeDtypeStruct((N_DEV,) + s.shape, s.dtype) for s in shards],
        in_specs=[ANY] * n, out_specs=[ANY] * n,
        scratch_shapes=[pltpu.SemaphoreType.DMA((7 * n,)), pltpu.SemaphoreType.DMA((7 * n,)),
                        pltpu.SemaphoreType.DMA((n,))],
    )(*shards)


def _rs_sibling(gs):
    n = len(gs)

    def body(*refs):
        ins, land = refs[:n], refs[n:2 * n]
        send_sems, recv_sems = refs[2 * n:]
        x, y, c = _mesh_pos()
        remote = [pltpu.make_async_remote_copy(
            src_ref=ins[a].at[1 - c], dst_ref=land[a], send_sem=send_sems.at[a], recv_sem=recv_sems.at[a],
            device_id=(x, y, 1 - c), device_id_type=MESH) for a in range(n)]
        for cp in remote:
            cp.start()
        for cp in remote:
            cp.wait_recv()
        for cp in remote:
            cp.wait_send()

    return pl.pallas_call(
        body, name="reduce_scatter_sibling",
        out_shape=[jax.ShapeDtypeStruct(g.shape[1:], g.dtype) for g in gs], in_specs=[ANY] * n, out_specs=[ANY] * n,
        scratch_shapes=[pltpu.SemaphoreType.DMA((n,)), pltpu.SemaphoreType.DMA((n,))],
    )(*gs)


def _rs_chips(ps):
    n = len(ps)

    def body(*refs):
        ins, land = refs[:n], refs[n:2 * n]
        send_sems, recv_sems = refs[2 * n:]
        x, y, c = _mesh_pos()
        chips = [(1 - x, y), (x, 1 - y), (1 - x, 1 - y)]
        remote = []
        for a in range(n):
            for j, chip in enumerate(chips):
                remote.append(pltpu.make_async_remote_copy(
                    src_ref=ins[a].at[2 * chip[0] + chip[1]], dst_ref=land[a].at[j],
                    send_sem=send_sems.at[3 * a + j], recv_sem=recv_sems.at[3 * a + j],
                    device_id=(*chip, c), device_id_type=MESH))
        for cp in remote:
            cp.start()
        for cp in remote:
            cp.wait_recv()
        for cp in remote:
            cp.wait_send()

    return pl.pallas_call(
        body, name="reduce_scatter_chips",
        out_shape=[jax.ShapeDtypeStruct((3,) + p.shape[1:], p.dtype) for p in ps],
        in_specs=[ANY] * n, out_specs=[ANY] * n,
        scratch_shapes=[pltpu.SemaphoreType.DMA((3 * n,)), pltpu.SemaphoreType.DMA((3 * n,))],
    )(*ps)


def _all_reduce_small(pack):
    rows = pack.shape[0]

    def body(p_ref, out_ref, land, send_sems, recv_sems):
        x, y, c = _mesh_pos()
        me = 4 * x + 2 * y + c
        land[me] = p_ref[...]
        copies = []
        for k in range(1, N_DEV):
            bx, by, bc = (k >> 2) & 1, (k >> 1) & 1, k & 1
            peer = (1 - x if bx else x, 1 - y if by else y, 1 - c if bc else c)
            copies.append(pltpu.make_async_remote_copy(
                src_ref=p_ref, dst_ref=land.at[me], send_sem=send_sems.at[k - 1], recv_sem=recv_sems.at[k - 1],
                device_id=peer, device_id_type=MESH))
        for cp in copies:
            cp.start()
        for cp in copies:
            cp.wait_recv()
        for cp in copies:
            cp.wait_send()
        acc = land[0]
        for d in range(1, N_DEV):
            acc = acc + land[d]
        out_ref[...] = acc

    return pl.pallas_call(
        body, name="all_reduce_small",
        out_shape=jax.ShapeDtypeStruct(pack.shape, F32),
        in_specs=[pl.BlockSpec(memory_space=pltpu.VMEM)], out_specs=pl.BlockSpec(memory_space=pltpu.VMEM),
        scratch_shapes=[pltpu.VMEM((N_DEV, rows, 128), F32), pltpu.SemaphoreType.DMA((7,)),
                        pltpu.SemaphoreType.DMA((7,))],
    )(pack)


def _in_proj(h0, nw, win_p, tm):
    rows = h0.shape[0]

    def body(h_ref, nw_ref, w_ref, o_ref):
        h = h_ref[...]
        rstd = lax.rsqrt(jnp.mean(h * h, axis=-1, keepdims=True) + EPS)
        u = (h * rstd * nw_ref[...]).astype(MXU_DTYPE)
        o_ref[...] = jnp.dot(u, w_ref[...].astype(MXU_DTYPE), preferred_element_type=F32)

    return pl.pallas_call(
        body, name="in_proj", grid=(rows // tm,),
        in_specs=[pl.BlockSpec((tm, D), lambda i: (i, 0)), pl.BlockSpec((1, D), lambda i: (0, 0)),
                  pl.BlockSpec((D, DINP), lambda i: (0, 0))],
        out_specs=pl.BlockSpec((tm, DINP), lambda i: (i, 0)),
        out_shape=jax.ShapeDtypeStruct((rows, DINP), F32),
        compiler_params=_cp(("arbitrary",), 56),
    )(h0, nw, win_p)


def _rope_tables(rows):
    pos = (jnp.arange(rows, dtype=jnp.int32) - META0).astype(F32)
    inv_freq = 1.0 / (ROPE_THETA ** (jnp.arange(0, ROPE_DIM, 2, dtype=F32) / ROPE_DIM))
    ang = pos[:, None] * inv_freq[None, :]
    cos, sin = jnp.cos(ang), jnp.sin(ang)
    zeros = jnp.zeros((rows, SWA_HD - ROPE_DIM), F32)
    zeros8 = jnp.zeros((rows, 8), F32)
    c_head = jnp.concatenate([cos, cos, zeros + 1.0], axis=1)
    sa_head = jnp.concatenate([-sin, zeros8, zeros], axis=1)
    sb_head = jnp.concatenate([zeros8, sin, zeros], axis=1)
    two = lambda t: jnp.concatenate([t, t], axis=1)
    return two(c_head), two(sa_head), two(sb_head)


def _rope(xv, cos, sa, sb):
    width = xv.shape[1]
    reps = width // 128
    if reps > 1:
        cos, sa, sb = (jnp.tile(t, (1, reps)) for t in (cos, sa, sb))
    return xv * cos + pltpu.roll(xv, width - 8, 1) * sa + pltpu.roll(xv, 8, 1) * sb


def _unrope(dy, cos, sa, sb):
    width = dy.shape[1]
    reps = width // 128
    if reps > 1:
        cos, sa, sb = (jnp.tile(t, (1, reps)) for t in (cos, sa, sb))
    return dy * cos + pltpu.roll(dy * sa, 8, 1) + pltpu.roll(dy * sb, width - 8, 1)


def _swa_prep(proj, tabs, tm):
    rows = proj.shape[0]

    def body(q_ref, k_ref, v_ref, c_ref, sa_ref, sb_ref, qo_ref, ko_ref, vo_ref):
        cos, sa, sb = c_ref[...], sa_ref[...], sb_ref[...]
        qo_ref[...] = (_rope(q_ref[...], cos, sa, sb) * (SWA_HD ** -0.5)).astype(ACT_DTYPE)
        ko_ref[...] = _rope(k_ref[...], cos, sa, sb).astype(ACT_DTYPE)
        vo_ref[...] = v_ref[...].astype(ACT_DTYPE)

    tab_spec = pl.BlockSpec((tm, 128), lambda i: (i, 0))
    return pl.pallas_call(
        body, name="swa_prep", grid=(rows // tm,),
        in_specs=[pl.BlockSpec((tm, 512), lambda i: (i, C_SQ // 512)),
                  pl.BlockSpec((tm, 128), lambda i: (i, C_SK // 128)),
                  pl.BlockSpec((tm, 128), lambda i: (i, C_SV // 128)), tab_spec, tab_spec, tab_spec],
        out_specs=[pl.BlockSpec((tm, 512), lambda i: (i, 0)), tab_spec, tab_spec],
        out_shape=[jax.ShapeDtypeStruct((rows, 512), ACT_DTYPE), jax.ShapeDtypeStruct((rows, 128), ACT_DTYPE),
                   jax.ShapeDtypeStruct((rows, 128), ACT_DTYPE)],
        compiler_params=_cp(("arbitrary",)),
    )(proj, proj, proj, *tabs)


def _gla_gates(lr, wg, bg, chunk):
    zg = _mm(lr, wg) + bg
    row = chunk * GLA_CHUNK + lax.broadcasted_iota(jnp.int32, (GLA_CHUNK, 1), 0)
    live = row >= META0
    g = jnp.where(live, _logsigmoid(zg) * (1.0 / GLA_TAU), 0.0)
    ii = lax.broadcasted_iota(jnp.int32, (GLA_CHUNK, GLA_CHUNK), 0)
    jj = lax.broadcasted_iota(jnp.int32, (GLA_CHUNK, GLA_CHUNK), 1)
    tril = jj <= ii
    b = jnp.dot(tril.astype(F32), g, precision=HIGHEST, preferred_element_type=F32)
    return zg, live, tril, b


def _gla_fwd(proj, wg_p, bg, gnw):
    rows = proj.shape[0]
    nc = rows // GLA_CHUNK

    def body(q_ref, k_ref, v_ref, r_ref, lr_ref, wg_ref, bg_ref, gnw_ref, oraw_ref, og_ref, st_ref, state):
        c = pl.program_id(0)

        @pl.when(c == 0)
        def _():
            state[...] = jnp.zeros_like(state)

        zg, live, tril, b = _gla_gates(lr_ref[...], wg_ref[...], bg_ref[...], c)
        eb = jnp.exp(b)
        gq = q_ref[...] * (GLA_DK ** -0.5) * eb
        gk = k_ref[...] * jnp.exp(-b)
        ebl = eb[GLA_CHUNK - 1:GLA_CHUNK, :]
        v = v_ref[...]
        r = r_ref[...]
        gnw_v = gnw_ref[...]
        oraw, og = [], []
        for h in range(GLA_HEADS):
            s64 = slice(h * GLA_DK, (h + 1) * GLA_DK)
            s128 = slice(h * GLA_DV, (h + 1) * GLA_DV)
            qh, kh, vh, eblh = gq[:, s64], gk[:, s64], v[:, s128], ebl[:, s64]
            st = state[h]
            st_ref[0, h] = st
            a = jnp.where(tril, _mm_nt(qh, kh), 0.0)
            o = _mm(a, vh) + _mm_nt(qh, st)
            state[h] = st * eblh + _mm_tn(vh, kh * eblh)
            oraw.append(o)
            rstd = lax.rsqrt(jnp.mean(o * o, axis=-1, keepdims=True) + EPS)
            rh = r[:, s128]
            og.append(o * rstd * gnw_v * (rh * _sigmoid(rh)))
        oraw_ref[...] = jnp.concatenate(oraw, axis=1)
        og_ref[...] = jnp.concatenate(og, axis=1).astype(ACT_DTYPE)

    nb = lambda w, col: pl.BlockSpec((GLA_CHUNK, w), lambda c: (c, col // w))
    const = lambda shape: pl.BlockSpec(shape, lambda c: (0,) * len(shape))
    return pl.pallas_call(
        body, name="gla_fwd", grid=(nc,),
        in_specs=[nb(256, C_GQ), nb(256, C_GK), nb(512, C_GV), nb(512, C_GR), nb(128, C_LR),
                  const((128, 256)), const((1, 256)), const((1, 128))],
        out_specs=[pl.BlockSpec((GLA_CHUNK, 512), lambda c: (c, 0)), pl.BlockSpec((GLA_CHUNK, 512), lambda c: (c, 0)),
                   pl.BlockSpec((1, GLA_HEADS, GLA_DV, GLA_DK), lambda c: (c, 0, 0, 0))],
        out_shape=[jax.ShapeDtypeStruct((rows, 512), F32), jax.ShapeDtypeStruct((rows, 512), ACT_DTYPE),
                   jax.ShapeDtypeStruct((nc, GLA_HEADS, GLA_DV, GLA_DK), F32)],
        scratch_shapes=[pltpu.VMEM((GLA_HEADS, GLA_DV, GLA_DK), F32)],
        compiler_params=_cp(("arbitrary",)),
    )(proj, proj, proj, proj, proj, wg_p, bg, gnw)


def _swa_mask(n):
    qi = lax.broadcasted_iota(jnp.int32, (SWA_BLOCK, 3 * SWA_BLOCK), 0)
    jj = lax.broadcasted_iota(jnp.int32, (SWA_BLOCK, 3 * SWA_BLOCK), 1)
    meta = (jj < SWA_BLOCK) & (jj >= META0) & ((n > 0) | (jj <= qi))
    prev = (jj >= SWA_BLOCK) & (jj < 2 * SWA_BLOCK) & (n >= 2) & (jj - SWA_BLOCK > qi)
    cur = (jj >= 2 * SWA_BLOCK) & (n >= 1) & (jj - 2 * SWA_BLOCK <= qi)
    return meta | prev | cur


def _swa_specs():
    blk = lambda w: pl.BlockSpec((SWA_BLOCK, w), lambda n: (n, 0))
    first = pl.BlockSpec((SWA_BLOCK, 128), lambda n: (0, 0))
    prev = pl.BlockSpec((SWA_BLOCK, 128), lambda n: (jnp.maximum(n - 1, 0), 0))
    return blk, first, prev


def _swa_fwd(qr, kr, vr, sinks):
    rows = qr.shape[0]
    nblk = rows // SWA_BLOCK

    def body(q_ref, k0, kp, kc, v0, vp, vc, sink_ref, o_ref):
        n = pl.program_id(0)
        q = q_ref[...]
        kall = jnp.concatenate([k0[...], kp[...], kc[...]], axis=0)
        vall = jnp.concatenate([v0[...], vp[...], vc[...]], axis=0)
        mask = _swa_mask(n)
        outs = []
        for head in range(SWA_HEADS):
            kv = slice((head // SWA_GROUP) * SWA_HD, (head // SWA_GROUP + 1) * SWA_HD)
            s = jnp.where(mask, _mm_nt(q[:, head * SWA_HD:(head + 1) * SWA_HD], kall[:, kv]), NEG)
            sink = sink_ref[0, head]
            m = jnp.maximum(jnp.max(s, axis=-1, keepdims=True), sink)
            p = jnp.exp(s - m)
            den = jnp.sum(p, axis=-1, keepdims=True) + jnp.exp(sink - m)
            outs.append(_mm(p, vall[:, kv]) / den)
        o_ref[...] = jnp.concatenate(outs, axis=1).astype(ACT_DTYPE)

    blk, first, prev = _swa_specs()
    return pl.pallas_call(
        body, name="swa_fwd", grid=(nblk,),
        in_specs=[blk(512), first, prev, blk(128), first, prev, blk(128),
                  pl.BlockSpec(memory_space=pltpu.SMEM)],
        out_specs=blk(512),
        out_shape=jax.ShapeDtypeStruct((rows, 512), ACT_DTYPE),
        compiler_params=_cp(("arbitrary",)),
    )(qr, kr, kr, kr, vr, vr, vr, sinks)


def _out_proj(h0, og, osw, wout, nfw, tm):
    rows = h0.shape[0]

    def body(h_ref, og_ref, os_ref, w_ref, nw_ref, h1_ref, f_ref):
        h1 = h_ref[...] + _mm(og_ref[...], w_ref[0:512, :]) + _mm(os_ref[...], w_ref[512:1024, :])
        h1_ref[...] = h1
        rstd = lax.rsqrt(jnp.mean(h1 * h1, axis=-1, keepdims=True) + EPS)
        f_ref[...] = (h1 * rstd * nw_ref[...]).astype(ACT_DTYPE)

    row = lambda w: pl.BlockSpec((tm, w), lambda i: (i, 0))
    return pl.pallas_call(
        body, name="out_proj", grid=(rows // tm,),
        in_specs=[row(D), row(512), row(512), pl.BlockSpec((D, D), lambda i: (0, 0)), pl.BlockSpec((1, D), lambda i: (0, 0))],
        out_specs=[row(D), row(D)],
        out_shape=[jax.ShapeDtypeStruct((rows, D), F32), jax.ShapeDtypeStruct((rows, D), ACT_DTYPE)],
        compiler_params=_cp(("arbitrary",), 48),
    )(h0, og, osw, wout, nfw)


def _ffn_fwd(f, h1, w1g, w2, tgt, fnw, tm):
    rows = f.shape[0]
    nj = N_DEV

    def body(f_ref, h1_ref, w1_ref, w2_ref, t_ref, nw_ref, a_ref, dh2_ref, loss_ref, gfn_ref, acc):
        i, j = pl.program_id(0), pl.program_id(1)

        @pl.when((i == 0) & (j == 0))
        def _():
            loss_ref[...] = jnp.zeros_like(loss_ref)
            gfn_ref[...] = jnp.zeros_like(gfn_ref)

        @pl.when(j == 0)
        def _():
            acc[...] = jnp.zeros_like(acc)

        a = _mm(f_ref[...], w1_ref[...])
        a_ref[...] = a.astype(ACT_DTYPE)
        z = jnp.square(jnp.maximum(a, 0.0))
        acc[...] += _mm(z, w2_ref[...])

        @pl.when(j == nj - 1)
        def _():
            h2 = h1_ref[...] + acc[...]
            rstd = lax.rsqrt(jnp.mean(h2 * h2, axis=-1, keepdims=True) + EPS)
            hn = h2 * rstd
            nw = nw_ref[...]
            row = i * tm + lax.broadcasted_iota(jnp.int32, (tm, 1), 0)
            err = jnp.where(row >= LEAD, hn * nw - t_ref[...], 0.0)
            row_loss = jnp.sum(err * err, axis=-1, keepdims=True) * (1.0 / D)
            loss_ref[...] += jnp.broadcast_to(0.5 * jnp.sum(row_loss, axis=0, keepdims=True), loss_ref.shape)
            dy = err * (1.0 / D)
            gfn_ref[...] += jnp.broadcast_to(jnp.sum(dy * hn, axis=0, keepdims=True), gfn_ref.shape)
            dhn = dy * nw
            dh2_ref[...] = rstd * (dhn - hn * jnp.mean(dhn * hn, axis=-1, keepdims=True))

    return pl.pallas_call(
        body, name="ffn_fwd", grid=(rows // tm, nj),
        in_specs=[pl.BlockSpec((tm, D), lambda i, j: (i, 0)), pl.BlockSpec((tm, D), lambda i, j: (i, 0)),
                  pl.BlockSpec((None, D, FF_TILE), lambda i, j: (j, 0, 0)),
                  pl.BlockSpec((FF_TILE, D), lambda i, j: (j, 0)),
                  pl.BlockSpec((tm, D), lambda i, j: (i, 0)), pl.BlockSpec((1, D), lambda i, j: (0, 0))],
        out_specs=[pl.BlockSpec((tm, FF_TILE), lambda i, j: (i, j)), pl.BlockSpec((tm, D), lambda i, j: (i, 0)),
                   pl.BlockSpec((8, 128), lambda i, j: (0, 0)), pl.BlockSpec((8, D), lambda i, j: (0, 0))],
        out_shape=[jax.ShapeDtypeStruct((rows, D_FF), ACT_DTYPE), jax.ShapeDtypeStruct((rows, D), F32),
                   jax.ShapeDtypeStruct((8, 128), F32), jax.ShapeDtypeStruct((8, D), F32)],
        scratch_shapes=[pltpu.VMEM((tm, D), F32)],
        compiler_params=_cp(("arbitrary", "arbitrary"), 48),
    )(f, h1, w1g, w2, tgt, fnw)


def _ffn_bwd_act(dh2, a, w1g, w2, h1, nfw, tm):
    rows = dh2.shape[0]
    nj = N_DEV

    def body(dh2_ref, a_ref, w1_ref, w2_ref, h1_ref, nw_ref, da_ref, dh1_ref, gnf_ref, acc):
        i, j = pl.program_id(0), pl.program_id(1)

        @pl.when((i == 0) & (j == 0))
        def _():
            gnf_ref[...] = jnp.zeros_like(gnf_ref)

        @pl.when(j == 0)
        def _():
            acc[...] = jnp.zeros_like(acc)

        dz = _mm_nt(dh2_ref[...], w2_ref[...])
        da = dz * (2.0 * jnp.maximum(a_ref[...].astype(F32), 0.0))
        da_ref[...] = da.astype(ACT_DTYPE)
        acc[...] += _mm_nt(da, w1_ref[...])

        @pl.when(j == nj - 1)
        def _():
            h1 = h1_ref[...]
            rstd = lax.rsqrt(jnp.mean(h1 * h1, axis=-1, keepdims=True) + EPS)
            hn = h1 * rstd
            df = acc[...]
            gnf_ref[...] += jnp.broadcast_to(jnp.sum(df * hn, axis=0, keepdims=True), gnf_ref.shape)
            dfn = df * nw_ref[...]
            dh1_ref[...] = dh2_ref[...] + rstd * (dfn - hn * jnp.mean(dfn * hn, axis=-1, keepdims=True))

    return pl.pallas_call(
        body, name="ffn_bwd_act", grid=(rows // tm, nj),
        in_specs=[pl.BlockSpec((tm, D), lambda i, j: (i, 0)), pl.BlockSpec((tm, FF_TILE), lambda i, j: (i, j)),
                  pl.BlockSpec((None, D, FF_TILE), lambda i, j: (j, 0, 0)),
                  pl.BlockSpec((FF_TILE, D), lambda i, j: (j, 0)),
                  pl.BlockSpec((tm, D), lambda i, j: (i, 0)), pl.BlockSpec((1, D), lambda i, j: (0, 0))],
        out_specs=[pl.BlockSpec((tm, FF_TILE), lambda i, j: (i, j)), pl.BlockSpec((tm, D), lambda i, j: (i, 0)),
                   pl.BlockSpec((8, D), lambda i, j: (0, 0))],
        out_shape=[jax.ShapeDtypeStruct((rows, D_FF), ACT_DTYPE), jax.ShapeDtypeStruct((rows, D), F32),
                   jax.ShapeDtypeStruct((8, D), F32)],
        scratch_shapes=[pltpu.VMEM((tm, D), F32)],
        compiler_params=_cp(("arbitrary", "arbitrary"), 48),
    )(dh2, a, w1g, w2, h1, nfw)


def _ffn_bwd_weights(f, a, da, dh2, tm):
    rows = f.shape[0]

    def body(f_ref, a_ref, da_ref, dh2_ref, dw1_ref, dw2_ref):
        i = pl.program_id(1)

        @pl.when(i == 0)
        def _():
            dw1_ref[...] = jnp.zeros_like(dw1_ref)
            dw2_ref[...] = jnp.zeros_like(dw2_ref)

        z = jnp.square(jnp.maximum(a_ref[...].astype(F32), 0.0))
        dw1_ref[...] += _mm_tn(f_ref[...], da_ref[...])
        dw2_ref[...] += _mm_tn(z, dh2_ref[...])

    return pl.pallas_call(
        body, name="ffn_bwd_weights", grid=(N_DEV, rows // tm),
        in_specs=[pl.BlockSpec((tm, D), lambda j, i: (i, 0)), pl.BlockSpec((tm, FF_TILE), lambda j, i: (i, j)),
                  pl.BlockSpec((tm, FF_TILE), lambda j, i: (i, j)), pl.BlockSpec((tm, D), lambda j, i: (i, 0))],
        out_specs=[pl.BlockSpec((None, None, D, FF_TILE), lambda j, i: (j % 2, j // 2, 0, 0)),
                   pl.BlockSpec((None, None, FF_TILE, D), lambda j, i: (j % 2, j // 2, 0, 0))],
        out_shape=[jax.ShapeDtypeStruct((2, 4, D, FF_TILE), F32), jax.ShapeDtypeStruct((2, 4, FF_TILE, D), F32)],
        compiler_params=_cp(("arbitrary", "arbitrary"), 48),
    )(f, a, da, dh2)


def _out_proj_bwd(dh1, og, osw, wout, tm):
    rows = dh1.shape[0]

    def body(dh1_ref, og_ref, os_ref, w_ref, dog_ref, dos_ref, dw_ref):
        i = pl.program_id(0)

        @pl.when(i == 0)
        def _():
            dw_ref[...] = jnp.zeros_like(dw_ref)

        dh1 = dh1_ref[...].astype(MXU_DTYPE)
        dog_ref[...] = _mm_nt(dh1, w_ref[0:512, :])
        dos_ref[...] = _mm_nt(dh1, w_ref[512:1024, :])
        for half, ref in enumerate((og_ref, os_ref)):
            dw = _mm_tn(ref[...], dh1)
            for blk in range(4):
                shard = half * 4 + blk
                dw_ref[shard % 2, shard // 2] += dw[blk * 128:(blk + 1) * 128, :]

    row = lambda w: pl.BlockSpec((tm, w), lambda i: (i, 0))
    return pl.pallas_call(
        body, name="out_proj_bwd", grid=(rows // tm,),
        in_specs=[row(D), row(512), row(512), pl.BlockSpec((D, D), lambda i: (0, 0))],
        out_specs=[row(512), row(512), pl.BlockSpec((2, 4, 128, D), lambda i: (0, 0, 0, 0))],
        out_shape=[jax.ShapeDtypeStruct((rows, 512), F32), jax.ShapeDtypeStruct((rows, 512), F32),
                   jax.ShapeDtypeStruct((2, 4, 128, D), F32)],
        compiler_params=_cp(("arbitrary",), 48),
    )(dh1, og, osw, wout)


def _swa_bwd(qr, kr, vr, osw, dos, sinks):
    rows = qr.shape[0]
    nblk = rows // SWA_BLOCK

    def body(q_ref, k0, kp, kc, v0, vp, vc, o_ref, do_ref, sink_ref, dq_ref, dk_ref, dv_ref, dsink_ref):
        n = pl.program_id(0)

        @pl.when(n == 0)
        def _():
            dk_ref[...] = jnp.zeros_like(dk_ref)
            dv_ref[...] = jnp.zeros_like(dv_ref)
            dsink_ref[...] = jnp.zeros_like(dsink_ref)

        q = q_ref[...]
        kall = jnp.concatenate([k0[...], kp[...], kc[...]], axis=0)
        vall = jnp.concatenate([v0[...], vp[...], vc[...]], axis=0)
        mask = _swa_mask(n)
        do_all = do_ref[...]
        o_all = o_ref[...].astype(F32)
        dq, dk, dv = [], [], []
        for kvh in range(SWA_KV):
            kv = slice(kvh * SWA_HD, (kvh + 1) * SWA_HD)
            dk_h = jnp.zeros((3 * SWA_BLOCK, SWA_HD), F32)
            dv_h = jnp.zeros((3 * SWA_BLOCK, SWA_HD), F32)
            for g in range(SWA_GROUP):
                head = kvh * SWA_GROUP + g
                hs = slice(head * SWA_HD, (head + 1) * SWA_HD)
                qh, doh = q[:, hs], do_all[:, hs]
                s = jnp.where(mask, _mm_nt(qh, kall[:, kv]), NEG)
                sink = sink_ref[0, head]
                m = jnp.maximum(jnp.max(s, axis=-1, keepdims=True), sink)
                e = jnp.exp(s - m)
                inv = 1.0 / (jnp.sum(e, axis=-1, keepdims=True) + jnp.exp(sink - m))
                p = e * inv
                delta = jnp.sum(doh * o_all[:, hs], axis=-1, keepdims=True)
                ds = p * (_mm_nt(doh, vall[:, kv]) - delta)
                dq.append(_mm(ds, kall[:, kv]))
                dk_h = dk_h + _mm_tn(ds, qh)
                dv_h = dv_h + _mm_tn(p, doh)
                dsink = -jnp.sum(jnp.exp(sink - m) * inv * delta, axis=0, keepdims=True)
                dsink_ref[head:head + 1, :] += jnp.broadcast_to(dsink, (1, 128))
            dk.append(dk_h)
            dv.append(dv_h)
        dq_ref[...] = jnp.concatenate(dq, axis=1)
        dk_all = jnp.concatenate(dk, axis=1)
        dv_all = jnp.concatenate(dv, axis=1)
        prev0 = pl.multiple_of(jnp.maximum(n - 1, 0) * SWA_BLOCK, SWA_BLOCK)
        cur0 = pl.multiple_of(n * SWA_BLOCK, SWA_BLOCK)
        for ref, val in ((dk_ref, dk_all), (dv_ref, dv_all)):
            ref[0:SWA_BLOCK, :] += val[0:SWA_BLOCK]
            ref[pl.ds(prev0, SWA_BLOCK), :] += val[SWA_BLOCK:2 * SWA_BLOCK]
            ref[pl.ds(cur0, SWA_BLOCK), :] += val[2 * SWA_BLOCK:]

    blk, first, prev = _swa_specs()
    whole = pl.BlockSpec((rows, 128), lambda n: (0, 0))
    return pl.pallas_call(
        body, name="swa_bwd", grid=(nblk,),
        in_specs=[blk(512), first, prev, blk(128), first, prev, blk(128), blk(512), blk(512),
                  pl.BlockSpec(memory_space=pltpu.SMEM)],
        out_specs=[blk(512), whole, whole, pl.BlockSpec((8, 128), lambda n: (0, 0))],
        out_shape=[jax.ShapeDtypeStruct((rows, 512), F32), jax.ShapeDtypeStruct((rows, 128), F32),
                   jax.ShapeDtypeStruct((rows, 128), F32), jax.ShapeDtypeStruct((8, 128), F32)],
        compiler_params=_cp(("arbitrary",), 48),
    )(qr, kr, kr, kr, vr, vr, vr, osw, dos, sinks)


def _gla_bwd(proj, oraw, states, dog, wg_p, bg, gnw):
    rows = proj.shape[0]
    nc = rows // GLA_CHUNK

    def body(q_ref, k_ref, v_ref, r_ref, lr_ref, oraw_ref, st_ref, dog_ref, wg_ref, bg_ref, gnw_ref,
             dq_ref, dk_ref, dv_ref, dr_ref, dlr_ref, dwg_ref, dbg_ref, dgnw_ref, dstate):
        t = pl.program_id(0)
        c = nc - 1 - t

        @pl.when(t == 0)
        def _():
            dstate[...] = jnp.zeros_like(dstate)
            dwg_ref[...] = jnp.zeros_like(dwg_ref)
            dbg_ref[...] = jnp.zeros_like(dbg_ref)
            dgnw_ref[...] = jnp.zeros_like(dgnw_ref)

        lr, wg = lr_ref[...], wg_ref[...]
        zg, live, tril, b = _gla_gates(lr, wg, bg_ref[...], c)
        eb, enb = jnp.exp(b), jnp.exp(-b)
        scale = GLA_DK ** -0.5
        gq = q_ref[...] * scale * eb
        gk = k_ref[...] * enb
        ebl = eb[GLA_CHUNK - 1:GLA_CHUNK, :]
        v, r, oraw_v, dog_v = v_ref[...], r_ref[...], oraw_ref[...], dog_ref[...]
        gnw_v = gnw_ref[...]
        is_last = lax.broadcasted_iota(jnp.int32, (GLA_CHUNK, 1), 0) == GLA_CHUNK - 1
        dq, dk, dv, dr, db = [], [], [], [], []
        dgnw = jnp.zeros((1, GLA_DV), F32)
        for h in range(GLA_HEADS):
            s64 = slice(h * GLA_DK, (h + 1) * GLA_DK)
            s128 = slice(h * GLA_DV, (h + 1) * GLA_DV)
            qh, kh, vh, eblh = gq[:, s64], gk[:, s64], v[:, s128], ebl[:, s64]
            klh = kh * eblh
            st = st_ref[0, h]
            o, rh, dout = oraw_v[:, s128], r[:, s128], dog_v[:, s128]
            rstd = lax.rsqrt(jnp.mean(o * o, axis=-1, keepdims=True) + EPS)
            on = o * rstd
            sg = _sigmoid(rh)
            dr.append(dout * (on * gnw_v) * (sg * (1.0 + rh * (1.0 - sg))))
            dy = dout * (rh * sg)
            dgnw = dgnw + jnp.sum(dy * on, axis=0, keepdims=True)
            don = dy * gnw_v
            do = rstd * (don - on * jnp.mean(don * on, axis=-1, keepdims=True))
            a = jnp.where(tril, _mm_nt(qh, kh), 0.0)
            da = jnp.where(tril, _mm_nt(do, vh), 0.0)
            dsp = dstate[h]
            dkl = _mm(vh, dsp)
            dv.append(_mm_tn(a, do) + _mm_nt(klh, dsp))
            debl = jnp.sum(dsp * st, axis=0, keepdims=True)
            dgq = _mm(da, kh) + _mm(do, st)
            dgk = _mm_tn(da, qh)
            dstate[h] = dsp * eblh + _mm_tn(do, qh)
            dq.append(dgq * (scale * eb[:, s64]))
            dk.append((dgk + dkl * eblh) * enb[:, s64])
            last = debl * eblh + jnp.sum(dkl * klh, axis=0, keepdims=True)
            db.append(dgq * qh - dgk * kh - dkl * klh + jnp.where(is_last, last, 0.0))
        dq_ref[...] = jnp.concatenate(dq, axis=1).astype(ACT_DTYPE)
        dk_ref[...] = jnp.concatenate(dk, axis=1).astype(ACT_DTYPE)
        dv_ref[...] = jnp.concatenate(dv, axis=1).astype(ACT_DTYPE)
        dr_ref[...] = jnp.concatenate(dr, axis=1).astype(ACT_DTYPE)
        triu = jnp.logical_not(tril) | (lax.broadcasted_iota(jnp.int32, (GLA_CHUNK, GLA_CHUNK), 0)
                                        == lax.broadcasted_iota(jnp.int32, (GLA_CHUNK, GLA_CHUNK), 1))
        dg = jnp.dot(triu.astype(F32), jnp.concatenate(db, axis=1), precision=HIGHEST, preferred_element_type=F32)
        dzg = jnp.where(live, dg * _sigmoid(-zg) * (1.0 / GLA_TAU), 0.0)
        dlr_ref[...] = _mm_nt(dzg, wg).astype(ACT_DTYPE)
        dwg_ref[...] += _mm_tn(lr, dzg)
        dbg_ref[...] += jnp.broadcast_to(jnp.sum(dzg, axis=0, keepdims=True), dbg_ref.shape)
        dgnw_ref[...] += jnp.broadcast_to(dgnw, dgnw_ref.shape)

    nb = lambda w, col: pl.BlockSpec((GLA_CHUNK, w), lambda t: (nc - 1 - t, col // w))
    const = lambda shape: pl.BlockSpec(shape, lambda t: (0,) * len(shape))
    return pl.pallas_call(
        body, name="gla_bwd", grid=(nc,),
        in_specs=[nb(256, C_GQ), nb(256, C_GK), nb(512, C_GV), nb(512, C_GR), nb(128, C_LR), nb(512, 0),
                  pl.BlockSpec((1, GLA_HEADS, GLA_DV, GLA_DK), lambda t: (nc - 1 - t, 0, 0, 0)), nb(512, 0),
                  const((128, 256)), const((1, 256)), const((1, 128))],
        out_specs=[nb(256, 0), nb(256, 0), nb(512, 0), nb(512, 0), nb(128, 0),
                   const((128, 256)), const((8, 256)), const((8, 128))],
        out_shape=[jax.ShapeDtypeStruct((rows, 256), ACT_DTYPE), jax.ShapeDtypeStruct((rows, 256), ACT_DTYPE),
                   jax.ShapeDtypeStruct((rows, 512), ACT_DTYPE), jax.ShapeDtypeStruct((rows, 512), ACT_DTYPE),
                   jax.ShapeDtypeStruct((rows, 128), ACT_DTYPE), jax.ShapeDtypeStruct((128, 256), F32),
                   jax.ShapeDtypeStruct((8, 256), F32), jax.ShapeDtypeStruct((8, 128), F32)],
        scratch_shapes=[pltpu.VMEM((GLA_HEADS, GLA_DV, GLA_DK), F32)],
        compiler_params=_cp(("arbitrary",)),
    )(proj, proj, proj, proj, proj, oraw, states, dog, wg_p, bg, gnw)


def _in_proj_bwd(h0, dh1, nw, win_p, dgv, dgr, dsq, dgq, dgk, dsk, dsv, dlr, tabs, tm):
    rows = h0.shape[0]

    def body(h_ref, dh1_ref, nw_ref, w_ref, dgv_ref, dgr_ref, dsq_ref, dgq_ref, dgk_ref, dsk_ref, dsv_ref, dlr_ref,
             c_ref, sa_ref, sb_ref, dh0_ref, dw_ref, gnm_ref):
        i = pl.program_id(0)

        @pl.when(i == 0)
        def _():
            dw_ref[...] = jnp.zeros_like(dw_ref)
            gnm_ref[...] = jnp.zeros_like(gnm_ref)

        cos, sa, sb = c_ref[...], sa_ref[...], sb_ref[...]
        dsq_v = (_unrope(dsq_ref[...], cos, sa, sb) * (SWA_HD ** -0.5)).astype(MXU_DTYPE)
        dsk_v = _unrope(dsk_ref[...], cos, sa, sb).astype(MXU_DTYPE)
        dproj = jnp.concatenate(
            [dgv_ref[...].astype(MXU_DTYPE), dgr_ref[...].astype(MXU_DTYPE), dsq_v, dgq_ref[...].astype(MXU_DTYPE),
             dgk_ref[...].astype(MXU_DTYPE), dsk_v, dsv_ref[...].astype(MXU_DTYPE), dlr_ref[...].astype(MXU_DTYPE)],
            axis=1)
        h = h_ref[...]
        rstd = lax.rsqrt(jnp.mean(h * h, axis=-1, keepdims=True) + EPS)
        hn = h * rstd
        nw_v = nw_ref[...]
        u = (hn * nw_v).astype(MXU_DTYPE)
        du = _mm_nt(dproj, w_ref[...])
        dw_ref[...] += _mm_tn(u, dproj)
        gnm_ref[...] += jnp.broadcast_to(jnp.sum(du * hn, axis=0, keepdims=True), gnm_ref.shape)
        dun = du * nw_v
        dh0_ref[...] = dh1_ref[...] + rstd * (dun - hn * jnp.mean(dun * hn, axis=-1, keepdims=True))

    row = lambda w: pl.BlockSpec((tm, w), lambda i: (i, 0))
    return pl.pallas_call(
        body, name="in_proj_bwd", grid=(rows // tm,),
        in_specs=[row(D), row(D), pl.BlockSpec((1, D), lambda i: (0, 0)), pl.BlockSpec((D, DINP), lambda i: (0, 0)),
                  row(512), row(512), row(512), row(256), row(256), row(128), row(128), row(128),
                  row(128), row(128), row(128)],
        out_specs=[row(D), pl.BlockSpec((D, DINP), lambda i: (0, 0)), pl.BlockSpec((8, D), lambda i: (0, 0))],
        out_shape=[jax.ShapeDtypeStruct((rows, D), F32), jax.ShapeDtypeStruct((D, DINP), F32),
                   jax.ShapeDtypeStruct((8, D), F32)],
        compiler_params=_cp(("arbitrary",), 56),
    )(h0, dh1, nw, win_p, dgv, dgr, dsq, dgq, dgk, dsk, dsv, dlr, *tabs)


def _adamw(w, g, m, v):
    m = ADAM_B1 * m + (1.0 - ADAM_B1) * g
    v = ADAM_B2 * v + (1.0 - ADAM_B2) * jnp.square(g)
    m_hat = m / (1.0 - ADAM_B1 ** ADAM_STEP)
    v_hat = v / (1.0 - ADAM_B2 ** ADAM_STEP)
    delta = -ADAM_LR * (m_hat / (jnp.sqrt(v_hat) + ADAM_EPS) + ADAM_WD * w)
    return delta, m, v


def _adamw_shard(where, parts, own, w, m, v, name):
    r, cdim = w.shape
    tr = 128 if r % 128 == 0 else r

    def body(where_ref, p_ref, own_ref, w_ref, m_ref, v_ref, g_ref, d_ref, nm_ref, nv_ref):
        g = ((p_ref[0] + p_ref[1]) + p_ref[2]) + own_ref[...]
        g_ref[...] = g
        d_ref[...], nm_ref[...], nv_ref[...] = _adamw(w_ref[...], g, m_ref[...], v_ref[...])

    spec = pl.BlockSpec((tr, cdim), lambda i, s: (i, 0))
    shape = jax.ShapeDtypeStruct((r, cdim), F32)
    return pl.pallas_call(
        body, name=name,
        grid_spec=pltpu.PrefetchScalarGridSpec(
            num_scalar_prefetch=1, grid=(r // tr,),
            in_specs=[pl.BlockSpec((3, tr, cdim), lambda i, s: (0, i, 0)),
                      pl.BlockSpec((None, tr, cdim), lambda i, s: (s[1], i, 0)), spec, spec, spec],
            out_specs=[spec] * 4),
        out_shape=[shape] * 4,
        compiler_params=_cp(("arbitrary",)),
    )(where, parts, own, w, m, v)


def _adamw_small(w, g, m, v):
    def body(w_ref, g_ref, m_ref, v_ref, d_ref, nm_ref, nv_ref):
        d_ref[...], nm_ref[...], nv_ref[...] = _adamw(w_ref[...], g_ref[...], m_ref[...], v_ref[...])

    vm = pl.BlockSpec(memory_space=pltpu.VMEM)
    shape = jax.ShapeDtypeStruct(w.shape, F32)
    return pl.pallas_call(body, name="adamw_small", in_specs=[vm] * 4, out_specs=[vm] * 3,
                          out_shape=[shape] * 3)(w, g, m, v)


def _add_own_half(where, full, theirs, name):
    _, _, r, cdim = full.shape
    tr = 128 if r % 128 == 0 else r

    def body(where_ref, a_ref, b_ref, o_ref):
        o_ref[...] = a_ref[...] + b_ref[...]

    spec = pl.BlockSpec((4, tr, cdim), lambda i, s: (0, i, 0))
    return pl.pallas_call(
        body, name=name,
        grid_spec=pltpu.PrefetchScalarGridSpec(
            num_scalar_prefetch=1, grid=(r // tr,),
            in_specs=[pl.BlockSpec((None, 4, tr, cdim), lambda i, s: (s[0], 0, i, 0)), spec], out_specs=spec),
        out_shape=jax.ShapeDtypeStruct(theirs.shape, F32), compiler_params=_cp(("arbitrary",)))(where, full, theirs)


def _to_rows128(a):
    return a.reshape(-1, 128)


def _pad_rows128(vec):
    flat = vec.reshape(-1)
    n = -(-flat.shape[0] // 128)
    return jnp.pad(flat, (0, n * 128 - flat.shape[0])).reshape(n, 128)


def kernel(x, meta_tokens, norm_mix_w, w_in, w_gate_up, b_gate, gla_norm_w, sinks, w_out, norm_ff_w, w_ff1, w_ff2, final_norm_w, loss_target, m_meta_tokens, m_norm_mix_w, m_w_in, m_w_gate_up, m_b_gate, m_gla_norm_w, m_sinks, m_w_out, m_norm_ff_w, m_w_ff1, m_w_ff2, m_final_norm_w, v_meta_tokens, v_norm_mix_w, v_w_in, v_w_gate_up, v_b_gate, v_gla_norm_w, v_sinks, v_w_out, v_norm_ff_w, v_w_ff1, v_w_ff2, v_final_norm_w):
    seq = x.shape[1]
    rows = LEAD + seq
    tm = _row_tile(rows)
    tm_small = tm // 2 if tm == 640 else tm
    dev = 4 * lax.axis_index("x") + 2 * lax.axis_index("y") + lax.axis_index("c")

    small_shard = jnp.concatenate([meta_tokens, w_gate_up[0], jnp.zeros((N_META, 96), F32)], axis=1)
    g_in, g_out, g_w1, g_w2, g_small = _all_gather(
        [w_in[0].astype(WIRE_DTYPE), w_out[0].astype(WIRE_DTYPE), w_ff1[0].astype(WIRE_DTYPE),
         w_ff2[0].astype(WIRE_DTYPE), small_shard])
    win_full = jnp.transpose(g_in, (1, 0, 2)).reshape(D, DIN)
    cols = lambda r: win_full[:, r[0]:r[1]]
    win_p = jnp.concatenate([cols(O_GV), cols(O_GR), cols(O_SQ), cols(O_GQ), cols(O_GK), cols(O_SK), cols(O_SV),
                             cols(O_LR), jnp.zeros((D, 128 - GLA_RANK), WIRE_DTYPE)], axis=1)
    wout_full = g_out.reshape(D, D)
    w2_full = g_w2.reshape(D_FF, D)
    meta_full = jnp.transpose(g_small[:, :, 0:128], (1, 0, 2)).reshape(N_META, D)
    wg_full = jnp.transpose(g_small[:, :, 128:160], (1, 0, 2)).reshape(GLA_RANK, GLA_HEADS * GLA_DK)
    wg_p = jnp.concatenate([wg_full, jnp.zeros((128 - GLA_RANK, 256), F32)], axis=0)

    h0 = jnp.concatenate([jnp.zeros((META0, D), F32), meta_full, x[0]], axis=0)
    tgt = jnp.concatenate([jnp.zeros((LEAD, D), F32), loss_target[0]], axis=0)
    tabs = _rope_tables(rows)
    proj = _in_proj(h0, norm_mix_w, win_p, tm)
    oraw, og, states = _gla_fwd(proj, wg_p, b_gate, gla_norm_w)
    qr, kr, vr = _swa_prep(proj, tabs, tm)
    osw = _swa_fwd(qr, kr, vr, sinks)
    h1, f = _out_proj(h0, og, osw, wout_full, norm_ff_w, tm)
    a, dh2, loss_p, gfn_p = _ffn_fwd(f, h1, g_w1, w2_full, tgt, final_norm_w.reshape(1, D), tm)

    da, dh1, gnf_p = _ffn_bwd_act(dh2, a, g_w1, w2_full, h1, norm_ff_w, tm)
    dw1, dw2 = _ffn_bwd_weights(f, a, da, dh2, tm)
    dog, dos, dwout = _out_proj_bwd(dh1, og, osw, wout_full, tm)
    dsq, dsk, dsv, dsink_p = _swa_bwd(qr, kr, vr, osw, dos, sinks)
    dgq, dgk, dgv, dgr, dlr, dwg_p, dbg_p, dgnw_p = _gla_bwd(proj, oraw, states, dog, wg_p, b_gate, gla_norm_w)
    dh0, dwin_p, gnm_p = _in_proj_bwd(h0, dh1, norm_mix_w, win_p, dgv, dgr, dsq, dgq, dgk, dsk, dsv, dlr, tabs, tm_small)
    grad_x = dh0[LEAD:][None]

    pcols = lambda c0, r: dwin_p[:, c0:c0 + (r[1] - r[0])]
    dwin = jnp.concatenate([pcols(C_GQ, O_GQ), pcols(C_GK, O_GK), pcols(C_GV, O_GV), pcols(C_GR, O_GR),
                            pcols(C_LR, O_LR), pcols(C_SQ, O_SQ), pcols(C_SK, O_SK), pcols(C_SV, O_SV)], axis=1)
    dwin = jnp.transpose(dwin.reshape(D, 4, 2, DIN // N_DEV), (2, 1, 0, 3))
    where = jnp.stack([lax.axis_index("c"), 2 * lax.axis_index("x") + lax.axis_index("y")]).astype(jnp.int32)
    partials = [dwin, dwout, dw1, dw2]
    theirs = _rs_sibling(partials)
    chip_sums = [_add_own_half(where, p, q, "reduce_pair_%d" % k) for k, (p, q) in enumerate(zip(partials, theirs))]
    parts = _rs_chips(chip_sums)

    small = [dh0[META0:LEAD], dwg_p[0:GLA_RANK], gnm_p[0:1], dbg_p[0:1], dgnw_p[0:1], dsink_p[:, 0], gnf_p[0:1],
             gfn_p[0:1], loss_p[0:1, 0:1]]
    sizes = [-(-s.size // 128) for s in small]
    pack = jnp.concatenate([_pad_rows128(s) for s in small], axis=0)
    pad_rows = -pack.shape[0] % 8
    pack = jnp.pad(pack, ((0, pad_rows), (0, 0)))
    total = _all_reduce_small(pack)
    offs = [sum(sizes[:k]) for k in range(len(sizes))]
    take = lambda k, shape: total[offs[k]:offs[k] + sizes[k]].reshape(-1)[:small[k].size].reshape(shape)
    g_meta_full = take(0, (N_META, D))
    g_wg_full = take(1, (GLA_RANK, 256))
    g_meta = lax.dynamic_slice_in_dim(g_meta_full, dev * 128, 128, axis=1)
    g_wg = lax.dynamic_slice_in_dim(g_wg_full, dev * 32, 32, axis=1)[None]
    g_norm_mix, g_b_gate, g_gla_norm = take(2, (1, D)), take(3, (1, 256)), take(4, (1, 128))
    g_sinks, g_norm_ff, g_final_norm = take(5, (1, 8)), take(6, (1, D)), take(7, (D,))
    loss = take(8, ())

    g_win, d_win, nm_win, nv_win = _adamw_shard(where, parts[0], chip_sums[0], w_in[0], m_w_in[0], v_w_in[0], "adamw_w_in")
    g_wout, d_wout, nm_wout, nv_wout = _adamw_shard(where, parts[1], chip_sums[1], w_out[0], m_w_out[0], v_w_out[0], "adamw_w_out")
    g_w1s, d_w1, nm_w1, nv_w1 = _adamw_shard(where, parts[2], chip_sums[2], w_ff1[0], m_w_ff1[0], v_w_ff1[0], "adamw_w_ff1")
    g_w2s, d_w2, nm_w2, nv_w2 = _adamw_shard(where, parts[3], chip_sums[3], w_ff2[0], m_w_ff2[0], v_w_ff2[0], "adamw_w_ff2")

    names = ["meta", "wg", "norm_mix", "b_gate", "gla_norm", "sinks", "norm_ff", "final_norm"]
    ws = [meta_tokens, w_gate_up, norm_mix_w, b_gate, gla_norm_w, sinks, norm_ff_w, final_norm_w]
    gs = [g_meta, g_wg, g_norm_mix, g_b_gate, g_gla_norm, g_sinks, g_norm_ff, g_final_norm]
    ms = [m_meta_tokens, m_w_gate_up, m_norm_mix_w, m_b_gate, m_gla_norm_w, m_sinks, m_norm_ff_w, m_final_norm_w]
    vs = [v_meta_tokens, v_w_gate_up, v_norm_mix_w, v_b_gate, v_gla_norm_w, v_sinks, v_norm_ff_w, v_final_norm_w]
    ssz = [-(-w.size // 128) for w in ws]
    packed = []
    for group in (ws, gs, ms, vs):
        p = jnp.concatenate([_pad_rows128(t) for t in group], axis=0)
        packed.append(jnp.pad(p, ((0, -p.shape[0] % 8), (0, 0))))
    d_s, nm_s, nv_s = _adamw_small(*packed)
    soffs = [sum(ssz[:k]) for k in range(len(ssz))]
    unpack = lambda t, k: t[soffs[k]:soffs[k] + ssz[k]].reshape(-1)[:ws[k].size].reshape(ws[k].shape)
    d_small = {n: unpack(d_s, k) for k, n in enumerate(names)}
    nm_small = {n: unpack(nm_s, k) for k, n in enumerate(names)}
    nv_small = {n: unpack(nv_s, k) for k, n in enumerate(names)}
    g_small_d = dict(zip(names, gs))

    def ordered(big, small_d):
        win_v, wout_v, w1_v, w2_v = big
        return (small_d["meta"], small_d["norm_mix"], win_v[None], small_d["wg"], small_d["b_gate"],
                small_d["gla_norm"], small_d["sinks"], wout_v[None], small_d["norm_ff"], w1_v[None], w2_v[None],
                small_d["final_norm"])

    return (loss, grad_x,
            *ordered((g_win, g_wout, g_w1s, g_w2s), g_small_d),
            *ordered((d_win, d_wout, d_w1, d_w2), d_small),
            *ordered((nm_win, nm_wout, nm_w1, nm_w2), nm_small),
            *ordered((nv_win, nv_wout, nv_w1, nv_w2), nv_small))
```

```python
import jax
import jax.numpy as jnp
from jax import lax
from jax.experimental import pallas as pl
from jax.experimental.pallas import tpu as pltpu

F32 = jnp.float32
MXU_DTYPE = jnp.bfloat16
ACT_DTYPE = jnp.bfloat16
WIRE_DTYPE = jnp.bfloat16

D = 1024
N_META = 16
LEAD = 128
META0 = LEAD - N_META
EPS = 1e-5
GLA_HEADS, GLA_DK, GLA_DV, GLA_RANK, GLA_CHUNK = 4, 64, 128, 16, 64
GLA_TAU = 16.0
SWA_HEADS, SWA_KV, SWA_GROUP, SWA_HD, SWA_BLOCK = 8, 2, 4, 64, 128
ROPE_DIM, ROPE_THETA = 16, 500000.0
D_FF = 4096
N_DEV = 8
FF_TILE = D_FF // N_DEV
NEG = -1e30

C_GV, C_GR, C_SQ, C_GQ, C_GK, C_SK, C_SV, C_LR = 0, 512, 1024, 1536, 1792, 2048, 2176, 2304
DINP = 2432
DIN = 2320
O_GQ, O_GK, O_GV, O_GR, O_LR, O_SQ, O_SK, O_SV = (0, 256), (256, 512), (512, 1024), (1024, 1536), (1536, 1552), (1552, 2064), (2064, 2192), (2192, 2320)

ADAM_LR, ADAM_B1, ADAM_B2, ADAM_EPS, ADAM_WD, ADAM_STEP = 0.001, 0.9, 0.999, 1e-08, 0.01, 10

MESH = pl.DeviceIdType.MESH
ANY = pl.BlockSpec(memory_space=pl.ANY)
HIGHEST = lax.Precision.HIGHEST


def _cp(sem=None, vmem_mb=None):
    kw = {}
    if sem is not None:
        kw["dimension_semantics"] = sem
    if vmem_mb is not None:
        kw["vmem_limit_bytes"] = vmem_mb << 20
    return pltpu.CompilerParams(**kw)


def _mm(a, b):
    return jnp.dot(a.astype(MXU_DTYPE), b.astype(MXU_DTYPE), preferred_element_type=F32)


def _mm_nt(a, b):
    return lax.dot_general(a.astype(MXU_DTYPE), b.astype(MXU_DTYPE), (((1,), (1,)), ((), ())),
                           preferred_element_type=F32)


def _mm_tn(a, b):
    return lax.dot_general(a.astype(MXU_DTYPE), b.astype(MXU_DTYPE), (((0,), (0,)), ((), ())),
                           preferred_element_type=F32)


def _logsigmoid(z):
    return jnp.minimum(z, 0.0) - jnp.log(1.0 + jnp.exp(-jnp.abs(z)))


def _sigmoid(z):
    return 1.0 / (1.0 + jnp.exp(-z))


def _row_tile(rows):
    return 640 if rows % 640 == 0 else 128


def _mesh_pos():
    return lax.axis_index("x"), lax.axis_index("y"), lax.axis_index("c")


def _all_gather(shards):
    n = len(shards)

    def body(*refs):
        ins, outs = refs[:n], refs[n:2 * n]
        send_sems, recv_sems = refs[2 * n:]
        x, y, c = _mesh_pos()
        me, sibling = (x, y, c), (x, y, 1 - c)
        chips = [(1 - x, y), (x, 1 - y), (1 - x, 1 - y)]

        def copy(a, k, block, to, src=None):
            dst = outs[a].at[4 * block[0] + 2 * block[1] + block[2]]
            return pltpu.make_async_remote_copy(
                src_ref=dst if src is None else src, dst_ref=dst,
                send_sem=send_sems.at[a * 7 + k], recv_sem=recv_sems.at[a * 7 + k],
                device_id=to, device_id_type=MESH)

        first = []
        for a in range(n):
            first.append(copy(a, 0, me, sibling, src=ins[a]))
            first += [copy(a, 1 + j, me, (*chip, c), src=ins[a]) for j, chip in enumerate(chips)]
        for cp in first:
            cp.start()
        passed = []
        for j, chip in enumerate(chips):
            for a in range(n):
                copy(a, 1 + j, (*chip, c), me).wait_recv()
                fwd = copy(a, 4 + j, (*chip, c), sibling)
                fwd.start()
                passed.append(fwd)
        for a in range(n):
            copy(a, 0, sibling, me).wait_recv()
            for j, chip in enumerate(chips):
                copy(a, 4 + j, (*chip, 1 - c), me).wait_recv()
        for cp in first + passed:
            cp.wait_send()

    gathered = pl.pallas_call(
        body, name="all_gather_weights",
        out_shape=[jax.ShapeDtypeStruct((N_DEV,) + s.shape, s.dtype) for s in shards],
        in_specs=[ANY] * n, out_specs=[ANY] * n,
        scratch_shapes=[pltpu.SemaphoreType.DMA((7 * n,)), pltpu.SemaphoreType.DMA((7 * n,))],
    )(*shards)
    dev = 4 * lax.axis_index("x") + 2 * lax.axis_index("y") + lax.axis_index("c")
    return [lax.dynamic_update_index_in_dim(g, s, dev, 0) for g, s in zip(gathered, shards)]


def _rs_sibling(gs):
    n = len(gs)

    def body(*refs):
        ins, land = refs[:n], refs[n:2 * n]
        send_sems, recv_sems = refs[2 * n:]
        x, y, c = _mesh_pos()
        remote = [pltpu.make_async_remote_copy(
            src_ref=ins[a].at[1 - c], dst_ref=land[a], send_sem=send_sems.at[a], recv_sem=recv_sems.at[a],
            device_id=(x, y, 1 - c), device_id_type=MESH) for a in range(n)]
        for cp in remote:
            cp.start()
        for cp in remote:
            cp.wait_recv()
        for cp in remote:
            cp.wait_send()

    return pl.pallas_call(
        body, name="reduce_scatter_sibling",
        out_shape=[jax.ShapeDtypeStruct(g.shape[1:], g.dtype) for g in gs], in_specs=[ANY] * n, out_specs=[ANY] * n,
        scratch_shapes=[pltpu.SemaphoreType.DMA((n,)), pltpu.SemaphoreType.DMA((n,))],
    )(*gs)


def _rs_chips(ps):
    n = len(ps)

    def body(*refs):
        ins, land = refs[:n], refs[n:2 * n]
        send_sems, recv_sems = refs[2 * n:]
        x, y, c = _mesh_pos()
        chips = [(1 - x, y), (x, 1 - y), (1 - x, 1 - y)]
        remote = []
        for a in range(n):
            for j, chip in enumerate(chips):
                remote.append(pltpu.make_async_remote_copy(
                    src_ref=ins[a].at[2 * chip[0] + chip[1]], dst_ref=land[a].at[j],
                    send_sem=send_sems.at[3 * a + j], recv_sem=recv_sems.at[3 * a + j],
                    device_id=(*chip, c), device_id_type=MESH))
        for cp in remote:
            cp.start()
        for cp in remote:
            cp.wait_recv()
        for cp in remote:
            cp.wait_send()

    return pl.pallas_call(
        body, name="reduce_scatter_chips",
        out_shape=[jax.ShapeDtypeStruct((3,) + p.shape[1:], p.dtype) for p in ps],
        in_specs=[ANY] * n, out_specs=[ANY] * n,
        scratch_shapes=[pltpu.SemaphoreType.DMA((3 * n,)), pltpu.SemaphoreType.DMA((3 * n,))],
    )(*ps)


def _all_reduce_small(pack):
    rows = pack.shape[0]

    def body(p_ref, out_ref, land, send_sems, recv_sems):
        x, y, c = _mesh_pos()
        me = 4 * x + 2 * y + c
        land[me] = p_ref[...]
        copies = []
        for k in range(1, N_DEV):
            bx, by, bc = (k >> 2) & 1, (k >> 1) & 1, k & 1
            peer = (1 - x if bx else x, 1 - y if by else y, 1 - c if bc else c)
            copies.append(pltpu.make_async_remote_copy(
                src_ref=p_ref, dst_ref=land.at[me], send_sem=send_sems.at[k - 1], recv_sem=recv_sems.at[k - 1],
                device_id=peer, device_id_type=MESH))
        for cp in copies:
            cp.start()
        for cp in copies:
            cp.wait_recv()
        for cp in copies:
            cp.wait_send()
        acc = land[0]
        for d in range(1, N_DEV):
            acc = acc + land[d]
        out_ref[...] = acc

    return pl.pallas_call(
        body, name="all_reduce_small",
        out_shape=jax.ShapeDtypeStruct(pack.shape, F32),
        in_specs=[pl.BlockSpec(memory_space=pltpu.VMEM)], out_specs=pl.BlockSpec(memory_space=pltpu.VMEM),
        scratch_shapes=[pltpu.VMEM((N_DEV, rows, 128), F32), pltpu.SemaphoreType.DMA((7,)),
                        pltpu.SemaphoreType.DMA((7,))],
    )(pack)


def _in_proj(h0, nw, win_p, tm):
    rows = h0.shape[0]

    def body(h_ref, nw_ref, w_ref, o_ref):
        h = h_ref[...]
        rstd = lax.rsqrt(jnp.mean(h * h, axis=-1, keepdims=True) + EPS)
        u = (h * rstd * nw_ref[...]).astype(MXU_DTYPE)
        o_ref[...] = jnp.dot(u, w_ref[...].astype(MXU_DTYPE), preferred_element_type=F32)

    return pl.pallas_call(
        body, name="in_proj", grid=(rows // tm,),
        in_specs=[pl.BlockSpec((tm, D), lambda i: (i, 0)), pl.BlockSpec((1, D), lambda i: (0, 0)),
                  pl.BlockSpec((D, DINP), lambda i: (0, 0))],
        out_specs=pl.BlockSpec((tm, DINP), lambda i: (i, 0)),
        out_shape=jax.ShapeDtypeStruct((rows, DINP), F32),
        compiler_params=_cp(("arbitrary",), 56),
    )(h0, nw, win_p)


def _rope_tables(rows):
    pos = (jnp.arange(rows, dtype=jnp.int32) - META0).astype(F32)
    inv_freq = 1.0 / (ROPE_THETA ** (jnp.arange(0, ROPE_DIM, 2, dtype=F32) / ROPE_DIM))
    ang = pos[:, None] * jnp.tile(inv_freq, 128 // (ROPE_DIM // 2))[None, :]
    in_head = jnp.arange(128, dtype=jnp.int32)[None, :] % SWA_HD
    cos, sin = jnp.cos(ang), jnp.sin(ang)
    c_tab = jnp.where(in_head < ROPE_DIM, cos, 1.0)
    sa_tab = jnp.where(in_head < ROPE_DIM // 2, -sin, 0.0)
    sb_tab = jnp.where((in_head >= ROPE_DIM // 2) & (in_head < ROPE_DIM), sin, 0.0)
    return c_tab, sa_tab, sb_tab


def _rope(xv, cos, sa, sb):
    width = xv.shape[1]
    reps = width // 128
    if reps > 1:
        cos, sa, sb = (jnp.tile(t, (1, reps)) for t in (cos, sa, sb))
    return xv * cos + pltpu.roll(xv, width - 8, 1) * sa + pltpu.roll(xv, 8, 1) * sb


def _unrope(dy, cos, sa, sb):
    width = dy.shape[1]
    reps = width // 128
    if reps > 1:
        cos, sa, sb = (jnp.tile(t, (1, reps)) for t in (cos, sa, sb))
    return dy * cos + pltpu.roll(dy * sa, 8, 1) + pltpu.roll(dy * sb, width - 8, 1)


def _swa_prep(proj, tabs, tm):
    rows = proj.shape[0]

    def body(q_ref, k_ref, v_ref, c_ref, sa_ref, sb_ref, qo_ref, ko_ref, vo_ref):
        cos, sa, sb = c_ref[...], sa_ref[...], sb_ref[...]
        qo_ref[...] = (_rope(q_ref[...], cos, sa, sb) * (SWA_HD ** -0.5)).astype(ACT_DTYPE)
        ko_ref[...] = _rope(k_ref[...], cos, sa, sb).astype(ACT_DTYPE)
        vo_ref[...] = v_ref[...].astype(ACT_DTYPE)

    tab_spec = pl.BlockSpec((tm, 128), lambda i: (i, 0))
    return pl.pallas_call(
        body, name="swa_prep", grid=(rows // tm,),
        in_specs=[pl.BlockSpec((tm, 512), lambda i: (i, C_SQ // 512)),
                  pl.BlockSpec((tm, 128), lambda i: (i, C_SK // 128)),
                  pl.BlockSpec((tm, 128), lambda i: (i, C_SV // 128)), tab_spec, tab_spec, tab_spec],
        out_specs=[pl.BlockSpec((tm, 512), lambda i: (i, 0)), tab_spec, tab_spec],
        out_shape=[jax.ShapeDtypeStruct((rows, 512), ACT_DTYPE), jax.ShapeDtypeStruct((rows, 128), ACT_DTYPE),
                   jax.ShapeDtypeStruct((rows, 128), ACT_DTYPE)],
        compiler_params=_cp(("arbitrary",)),
    )(proj, proj, proj, *tabs)


def _gla_gates(lr, wg, bg, chunk):
    zg = _mm(lr, wg) + bg
    row = chunk * GLA_CHUNK + lax.broadcasted_iota(jnp.int32, (GLA_CHUNK, 1), 0)
    live = row >= META0
    g = jnp.where(live, _logsigmoid(zg) * (1.0 / GLA_TAU), 0.0)
    ii = lax.broadcasted_iota(jnp.int32, (GLA_CHUNK, GLA_CHUNK), 0)
    jj = lax.broadcasted_iota(jnp.int32, (GLA_CHUNK, GLA_CHUNK), 1)
    tril = jj <= ii
    b = jnp.dot(tril.astype(F32), g, precision=HIGHEST, preferred_element_type=F32)
    return zg, live, tril, b


def _gla_fwd(proj, wg_p, bg, gnw):
    rows = proj.shape[0]
    nc = rows // GLA_CHUNK

    def body(q_ref, k_ref, v_ref, r_ref, lr_ref, wg_ref, bg_ref, gnw_ref, oraw_ref, og_ref, st_ref, state):
        c = pl.program_id(0)

        @pl.when(c == 0)
        def _():
            state[...] = jnp.zeros_like(state)

        zg, live, tril, b = _gla_gates(lr_ref[...], wg_ref[...], bg_ref[...], c)
        eb = jnp.exp(b)
        gq = q_ref[...] * (GLA_DK ** -0.5) * eb
        gk = k_ref[...] * jnp.exp(-b)
        ebl = eb[GLA_CHUNK - 1:GLA_CHUNK, :]
        v = v_ref[...]
        r = r_ref[...]
        gnw_v = gnw_ref[...]
        oraw, og = [], []
        for h in range(GLA_HEADS):
            s64 = slice(h * GLA_DK, (h + 1) * GLA_DK)
            s128 = slice(h * GLA_DV, (h + 1) * GLA_DV)
            qh, kh, vh, eblh = gq[:, s64], gk[:, s64], v[:, s128], ebl[:, s64]
            st = state[h]
            st_ref[0, h] = st
            a = jnp.where(tril, _mm_nt(qh, kh), 0.0)
            o = _mm(a, vh) + _mm_nt(qh, st)
            state[h] = st * eblh + _mm_tn(vh, kh * eblh)
            oraw.append(o)
            rstd = lax.rsqrt(jnp.mean(o * o, axis=-1, keepdims=True) + EPS)
            rh = r[:, s128]
            og.append(o * rstd * gnw_v * (rh * _sigmoid(rh)))
        oraw_ref[...] = jnp.concatenate(oraw, axis=1)
        og_ref[...] = jnp.concatenate(og, axis=1).astype(ACT_DTYPE)

    nb = lambda w, col: pl.BlockSpec((GLA_CHUNK, w), lambda c: (c, col // w))
    const = lambda shape: pl.BlockSpec(shape, lambda c: (0,) * len(shape))
    return pl.pallas_call(
        body, name="gla_fwd", grid=(nc,),
        in_specs=[nb(256, C_GQ), nb(256, C_GK), nb(512, C_GV), nb(512, C_GR), nb(128, C_LR),
                  const((128, 256)), const((1, 256)), const((1, 128))],
        out_specs=[pl.BlockSpec((GLA_CHUNK, 512), lambda c: (c, 0)), pl.BlockSpec((GLA_CHUNK, 512), lambda c: (c, 0)),
                   pl.BlockSpec((1, GLA_HEADS, GLA_DV, GLA_DK), lambda c: (c, 0, 0, 0))],
        out_shape=[jax.ShapeDtypeStruct((rows, 512), F32), jax.ShapeDtypeStruct((rows, 512), ACT_DTYPE),
                   jax.ShapeDtypeStruct((nc, GLA_HEADS, GLA_DV, GLA_DK), F32)],
        scratch_shapes=[pltpu.VMEM((GLA_HEADS, GLA_DV, GLA_DK), F32)],
        compiler_params=_cp(("arbitrary",)),
    )(proj, proj, proj, proj, proj, wg_p, bg, gnw)


def _swa_mask(n):
    shape = (SWA_GROUP * SWA_BLOCK, 3 * SWA_BLOCK)
    qi = lax.broadcasted_iota(jnp.int32, shape, 0) & (SWA_BLOCK - 1)
    jj = lax.broadcasted_iota(jnp.int32, shape, 1)
    meta = (jj < SWA_BLOCK) & (jj >= META0) & ((n > 0) | (jj <= qi))
    prev = (jj >= SWA_BLOCK) & (jj < 2 * SWA_BLOCK) & (n >= 2) & (jj - SWA_BLOCK > qi)
    cur = (jj >= 2 * SWA_BLOCK) & (n >= 1) & (jj - 2 * SWA_BLOCK <= qi)
    return meta | prev | cur


def _stack_heads(t, kvh):
    return jnp.concatenate([t[:, (kvh * SWA_GROUP + g) * SWA_HD:(kvh * SWA_GROUP + g + 1) * SWA_HD]
                            for g in range(SWA_GROUP)], axis=0)


def _stack_sinks(sink_ref, kvh):
    return jnp.concatenate([jnp.full((SWA_BLOCK, 1), sink_ref[0, kvh * SWA_GROUP + g], F32)
                            for g in range(SWA_GROUP)], axis=0)


def _swa_specs():
    blk = lambda w: pl.BlockSpec((SWA_BLOCK, w), lambda n: (n, 0))
    first = pl.BlockSpec((SWA_BLOCK, 128), lambda n: (0, 0))
    prev = pl.BlockSpec((SWA_BLOCK, 128), lambda n: (jnp.maximum(n - 1, 0), 0))
    return blk, first, prev


def _swa_fwd(qr, kr, vr, sinks):
    rows = qr.shape[0]
    nblk = rows // SWA_BLOCK

    def body(q_ref, k0, kp, kc, v0, vp, vc, sink_ref, o_ref):
        n = pl.program_id(0)
        q = q_ref[...]
        kall = jnp.concatenate([k0[...], kp[...], kc[...]], axis=0)
        vall = jnp.concatenate([v0[...], vp[...], vc[...]], axis=0)
        mask = _swa_mask(n)
        outs = []
        for kvh in range(SWA_KV):
            kv = slice(kvh * SWA_HD, (kvh + 1) * SWA_HD)
            q4, sink4 = _stack_heads(q, kvh), _stack_sinks(sink_ref, kvh)
            s = jnp.where(mask, _mm_nt(q4, kall[:, kv]), NEG)
            m = jnp.maximum(jnp.max(s, axis=-1, keepdims=True), sink4)
            p = jnp.exp(s - m)
            den = jnp.sum(p, axis=-1, keepdims=True) + jnp.exp(sink4 - m)
            o4 = _mm(p, vall[:, kv]) / den
            outs += [o4[g * SWA_BLOCK:(g + 1) * SWA_BLOCK] for g in range(SWA_GROUP)]
        o_ref[...] = jnp.concatenate(outs, axis=1).astype(ACT_DTYPE)

    blk, first, prev = _swa_specs()
    return pl.pallas_call(
        body, name="swa_fwd", grid=(nblk,),
        in_specs=[blk(512), first, prev, blk(128), first, prev, blk(128),
                  pl.BlockSpec(memory_space=pltpu.SMEM)],
        out_specs=blk(512),
        out_shape=jax.ShapeDtypeStruct((rows, 512), ACT_DTYPE),
        compiler_params=_cp(("arbitrary",)),
    )(qr, kr, kr, kr, vr, vr, vr, sinks)


def _out_proj(h0, og, osw, wout, nfw, tm):
    rows = h0.shape[0]

    def body(h_ref, og_ref, os_ref, w_ref, nw_ref, h1_ref, f_ref):
        h1 = h_ref[...] + _mm(og_ref[...], w_ref[0:512, :]) + _mm(os_ref[...], w_ref[512:1024, :])
        h1_ref[...] = h1
        rstd = lax.rsqrt(jnp.mean(h1 * h1, axis=-1, keepdims=True) + EPS)
        f_ref[...] = (h1 * rstd * nw_ref[...]).astype(ACT_DTYPE)

    row = lambda w: pl.BlockSpec((tm, w), lambda i: (i, 0))
    return pl.pallas_call(
        body, name="out_proj", grid=(rows // tm,),
        in_specs=[row(D), row(512), row(512), pl.BlockSpec((D, D), lambda i: (0, 0)), pl.BlockSpec((1, D), lambda i: (0, 0))],
        out_specs=[row(D), row(D)],
        out_shape=[jax.ShapeDtypeStruct((rows, D), F32), jax.ShapeDtypeStruct((rows, D), ACT_DTYPE)],
        compiler_params=_cp(("arbitrary",), 48),
    )(h0, og, osw, wout, nfw)


def _ffn_fwd(f, h1, w1g, w2, tgt, fnw, tm):
    rows = f.shape[0]
    nj = N_DEV

    def body(f_ref, h1_ref, w1_ref, w2_ref, t_ref, nw_ref, a_ref, dh2_ref, loss_ref, gfn_ref, acc):
        i, j = pl.program_id(0), pl.program_id(1)

        @pl.when((i == 0) & (j == 0))
        def _():
            loss_ref[...] = jnp.zeros_like(loss_ref)
            gfn_ref[...] = jnp.zeros_like(gfn_ref)

        @pl.when(j == 0)
        def _():
            acc[...] = jnp.zeros_like(acc)

        a = _mm(f_ref[...], w1_ref[...])
        a_ref[...] = a.astype(ACT_DTYPE)
        z = jnp.square(jnp.maximum(a, 0.0))
        acc[...] += _mm(z, w2_ref[...])

        @pl.when(j == nj - 1)
        def _():
            h2 = h1_ref[...] + acc[...]
            rstd = lax.rsqrt(jnp.mean(h2 * h2, axis=-1, keepdims=True) + EPS)
            hn = h2 * rstd
            nw = nw_ref[...]
            row = i * tm + lax.broadcasted_iota(jnp.int32, (tm, 1), 0)
            err = jnp.where(row >= LEAD, hn * nw - t_ref[...], 0.0)
            row_loss = jnp.sum(err * err, axis=-1, keepdims=True) * (1.0 / D)
            loss_ref[...] += jnp.broadcast_to(0.5 * jnp.sum(row_loss, axis=0, keepdims=True), loss_ref.shape)
            dy = err * (1.0 / D)
            gfn_ref[...] += jnp.broadcast_to(jnp.sum(dy * hn, axis=0, keepdims=True), gfn_ref.shape)
            dhn = dy * nw
            dh2_ref[...] = rstd * (dhn - hn * jnp.mean(dhn * hn, axis=-1, keepdims=True))

    return pl.pallas_call(
        body, name="ffn_fwd", grid=(rows // tm, nj),
        in_specs=[pl.BlockSpec((tm, D), lambda i, j: (i, 0)), pl.BlockSpec((tm, D), lambda i, j: (i, 0)),
                  pl.BlockSpec((None, D, FF_TILE), lambda i, j: (j, 0, 0)),
                  pl.BlockSpec((FF_TILE, D), lambda i, j: (j, 0)),
                  pl.BlockSpec((tm, D), lambda i, j: (i, 0)), pl.BlockSpec((1, D), lambda i, j: (0, 0))],
        out_specs=[pl.BlockSpec((tm, FF_TILE), lambda i, j: (i, j)), pl.BlockSpec((tm, D), lambda i, j: (i, 0)),
                   pl.BlockSpec((8, 128), lambda i, j: (0, 0)), pl.BlockSpec((8, D), lambda i, j: (0, 0))],
        out_shape=[jax.ShapeDtypeStruct((rows, D_FF), ACT_DTYPE), jax.ShapeDtypeStruct((rows, D), F32),
                   jax.ShapeDtypeStruct((8, 128), F32), jax.ShapeDtypeStruct((8, D), F32)],
        scratch_shapes=[pltpu.VMEM((tm, D), F32)],
        compiler_params=_cp(("arbitrary", "arbitrary"), 48),
    )(f, h1, w1g, w2, tgt, fnw)


def _ffn_bwd_act(dh2, a, w1g, w2, h1, nfw, tm):
    rows = dh2.shape[0]
    nj = N_DEV

    def body(dh2_ref, a_ref, w1_ref, w2_ref, h1_ref, nw_ref, da_ref, dh1_ref, gnf_ref, acc):
        i, j = pl.program_id(0), pl.program_id(1)

        @pl.when((i == 0) & (j == 0))
        def _():
            gnf_ref[...] = jnp.zeros_like(gnf_ref)

        @pl.when(j == 0)
        def _():
            acc[...] = jnp.zeros_like(acc)

        dz = _mm_nt(dh2_ref[...], w2_ref[...])
        da = dz * (2.0 * jnp.maximum(a_ref[...].astype(F32), 0.0))
        da_ref[...] = da.astype(ACT_DTYPE)
        acc[...] += _mm_nt(da, w1_ref[...])

        @pl.when(j == nj - 1)
        def _():
            h1 = h1_ref[...]
            rstd = lax.rsqrt(jnp.mean(h1 * h1, axis=-1, keepdims=True) + EPS)
            hn = h1 * rstd
            df = acc[...]
            gnf_ref[...] += jnp.broadcast_to(jnp.sum(df * hn, axis=0, keepdims=True), gnf_ref.shape)
            dfn = df * nw_ref[...]
            dh1_ref[...] = dh2_ref[...] + rstd * (dfn - hn * jnp.mean(dfn * hn, axis=-1, keepdims=True))

    return pl.pallas_call(
        body, name="ffn_bwd_act", grid=(rows // tm, nj),
        in_specs=[pl.BlockSpec((tm, D), lambda i, j: (i, 0)), pl.BlockSpec((tm, FF_TILE), lambda i, j: (i, j)),
                  pl.BlockSpec((None, D, FF_TILE), lambda i, j: (j, 0, 0)),
                  pl.BlockSpec((FF_TILE, D), lambda i, j: (j, 0)),
                  pl.BlockSpec((tm, D), lambda i, j: (i, 0)), pl.BlockSpec((1, D), lambda i, j: (0, 0))],
        out_specs=[pl.BlockSpec((tm, FF_TILE), lambda i, j: (i, j)), pl.BlockSpec((tm, D), lambda i, j: (i, 0)),
                   pl.BlockSpec((8, D), lambda i, j: (0, 0))],
        out_shape=[jax.ShapeDtypeStruct((rows, D_FF), ACT_DTYPE), jax.ShapeDtypeStruct((rows, D), F32),
                   jax.ShapeDtypeStruct((8, D), F32)],
        scratch_shapes=[pltpu.VMEM((tm, D), F32)],
        compiler_params=_cp(("arbitrary", "arbitrary"), 48),
    )(dh2, a, w1g, w2, h1, nfw)


def _ffn_bwd_weights(f, a, da, dh2, tm):
    rows = f.shape[0]

    def body(f_ref, a_ref, da_ref, dh2_ref, dw1_ref, dw2_ref):
        i = pl.program_id(1)

        @pl.when(i == 0)
        def _():
            dw1_ref[...] = jnp.zeros_like(dw1_ref)
            dw2_ref[...] = jnp.zeros_like(dw2_ref)

        z = jnp.square(jnp.maximum(a_ref[...].astype(F32), 0.0))
        dw1_ref[...] += _mm_tn(f_ref[...], da_ref[...])
        dw2_ref[...] += _mm_tn(z, dh2_ref[...])

    return pl.pallas_call(
        body, name="ffn_bwd_weights", grid=(N_DEV, rows // tm),
        in_specs=[pl.BlockSpec((tm, D), lambda j, i: (i, 0)), pl.BlockSpec((tm, FF_TILE), lambda j, i: (i, j)),
                  pl.BlockSpec((tm, FF_TILE), lambda j, i: (i, j)), pl.BlockSpec((tm, D), lambda j, i: (i, 0))],
        out_specs=[pl.BlockSpec((None, None, D, FF_TILE), lambda j, i: (j % 2, j // 2, 0, 0)),
                   pl.BlockSpec((None, None, FF_TILE, D), lambda j, i: (j % 2, j // 2, 0, 0))],
        out_shape=[jax.ShapeDtypeStruct((2, 4, D, FF_TILE), F32), jax.ShapeDtypeStruct((2, 4, FF_TILE, D), F32)],
        compiler_params=_cp(("arbitrary", "arbitrary"), 48),
    )(f, a, da, dh2)


def _out_proj_bwd(dh1, og, osw, wout, tm):
    rows = dh1.shape[0]

    def body(dh1_ref, og_ref, os_ref, w_ref, dog_ref, dos_ref, dw_ref):
        i = pl.program_id(0)

        @pl.when(i == 0)
        def _():
            dw_ref[...] = jnp.zeros_like(dw_ref)

        dh1 = dh1_ref[...].astype(MXU_DTYPE)
        dog_ref[...] = _mm_nt(dh1, w_ref[0:512, :])
        dos_ref[...] = _mm_nt(dh1, w_ref[512:1024, :])
        for half, ref in enumerate((og_ref, os_ref)):
            dw = _mm_tn(ref[...], dh1)
            for blk in range(4):
                shard = half * 4 + blk
                dw_ref[shard % 2, shard // 2] += dw[blk * 128:(blk + 1) * 128, :]

    row = lambda w: pl.BlockSpec((tm, w), lambda i: (i, 0))
    return pl.pallas_call(
        body, name="out_proj_bwd", grid=(rows // tm,),
        in_specs=[row(D), row(512), row(512), pl.BlockSpec((D, D), lambda i: (0, 0))],
        out_specs=[row(512), row(512), pl.BlockSpec((2, 4, 128, D), lambda i: (0, 0, 0, 0))],
        out_shape=[jax.ShapeDtypeStruct((rows, 512), F32), jax.ShapeDtypeStruct((rows, 512), F32),
                   jax.ShapeDtypeStruct((2, 4, 128, D), F32)],
        compiler_params=_cp(("arbitrary",), 48),
    )(dh1, og, osw, wout)


def _swa_bwd(qr, kr, vr, osw, dos, sinks):
    rows = qr.shape[0]
    nblk = rows // SWA_BLOCK

    def body(q_ref, k0, kp, kc, v0, vp, vc, o_ref, do_ref, sink_ref, dq_ref, dk_ref, dv_ref, dsink_ref):
        n = pl.program_id(0)

        @pl.when(n == 0)
        def _():
            dk_ref[...] = jnp.zeros_like(dk_ref)
            dv_ref[...] = jnp.zeros_like(dv_ref)
            dsink_ref[...] = jnp.zeros_like(dsink_ref)

        q = q_ref[...]
        kall = jnp.concatenate([k0[...], kp[...], kc[...]], axis=0)
        vall = jnp.concatenate([v0[...], vp[...], vc[...]], axis=0)
        mask = _swa_mask(n)
        do_all = do_ref[...]
        o_all = o_ref[...].astype(F32)
        dq, dk, dv = [], [], []
        for kvh in range(SWA_KV):
            kv = slice(kvh * SWA_HD, (kvh + 1) * SWA_HD)
            q4, do4, o4 = _stack_heads(q, kvh), _stack_heads(do_all, kvh), _stack_heads(o_all, kvh)
            sink4 = _stack_sinks(sink_ref, kvh)
            s = jnp.where(mask, _mm_nt(q4, kall[:, kv]), NEG)
            m = jnp.maximum(jnp.max(s, axis=-1, keepdims=True), sink4)
            e = jnp.exp(s - m)
            inv = 1.0 / (jnp.sum(e, axis=-1, keepdims=True) + jnp.exp(sink4 - m))
            p = e * inv
            delta = jnp.sum(do4 * o4, axis=-1, keepdims=True)
            ds = p * (_mm_nt(do4, vall[:, kv]) - delta)
            dq4 = _mm(ds, kall[:, kv])
            dq += [dq4[g * SWA_BLOCK:(g + 1) * SWA_BLOCK] for g in range(SWA_GROUP)]
            dk.append(_mm_tn(ds, q4))
            dv.append(_mm_tn(p, do4))
            sink_term = jnp.exp(sink4 - m) * inv * delta
            for g in range(SWA_GROUP):
                head = kvh * SWA_GROUP + g
                dsink = -jnp.sum(sink_term[g * SWA_BLOCK:(g + 1) * SWA_BLOCK], axis=0, keepdims=True)
                dsink_ref[head:head + 1, :] += jnp.broadcast_to(dsink, (1, 128))
        dq_ref[...] = jnp.concatenate(dq, axis=1)
        dk_all = jnp.concatenate(dk, axis=1)
        dv_all = jnp.concatenate(dv, axis=1)
        prev0 = pl.multiple_of(jnp.maximum(n - 1, 0) * SWA_BLOCK, SWA_BLOCK)
        cur0 = pl.multiple_of(n * SWA_BLOCK, SWA_BLOCK)
        for ref, val in ((dk_ref, dk_all), (dv_ref, dv_all)):
            ref[0:SWA_BLOCK, :] += val[0:SWA_BLOCK]
            ref[pl.ds(prev0, SWA_BLOCK), :] += val[SWA_BLOCK:2 * SWA_BLOCK]
            ref[pl.ds(cur0, SWA_BLOCK), :] += val[2 * SWA_BLOCK:]

    blk, first, prev = _swa_specs()
    whole = pl.BlockSpec((rows, 128), lambda n: (0, 0))
    return pl.pallas_call(
        body, name="swa_bwd", grid=(nblk,),
        in_specs=[blk(512), first, prev, blk(128), first, prev, blk(128), blk(512), blk(512),
                  pl.BlockSpec(memory_space=pltpu.SMEM)],
        out_specs=[blk(512), whole, whole, pl.BlockSpec((8, 128), lambda n: (0, 0))],
        out_shape=[jax.ShapeDtypeStruct((rows, 512), F32), jax.ShapeDtypeStruct((rows, 128), F32),
                   jax.ShapeDtypeStruct((rows, 128), F32), jax.ShapeDtypeStruct((8, 128), F32)],
        compiler_params=_cp(("arbitrary",), 48),
    )(qr, kr, kr, kr, vr, vr, vr, osw, dos, sinks)


def _gla_bwd(proj, oraw, states, dog, wg_p, bg, gnw):
    rows = proj.shape[0]
    nc = rows // GLA_CHUNK

    def body(q_ref, k_ref, v_ref, r_ref, lr_ref, oraw_ref, st_ref, dog_ref, wg_ref, bg_ref, gnw_ref,
             dq_ref, dk_ref, dv_ref, dr_ref, dlr_ref, dwg_ref, dbg_ref, dgnw_ref, dstate):
        t = pl.program_id(0)
        c = nc - 1 - t

        @pl.when(t == 0)
        def _():
            dstate[...] = jnp.zeros_like(dstate)
            dwg_ref[...] = jnp.zeros_like(dwg_ref)
            dbg_ref[...] = jnp.zeros_like(dbg_ref)
            dgnw_ref[...] = jnp.zeros_like(dgnw_ref)

        lr, wg = lr_ref[...], wg_ref[...]
        zg, live, tril, b = _gla_gates(lr, wg, bg_ref[...], c)
        eb, enb = jnp.exp(b), jnp.exp(-b)
        scale = GLA_DK ** -0.5
        gq = q_ref[...] * scale * eb
        gk = k_ref[...] * enb
        ebl = eb[GLA_CHUNK - 1:GLA_CHUNK, :]
        v, r, oraw_v, dog_v = v_ref[...], r_ref[...], oraw_ref[...], dog_ref[...]
        gnw_v = gnw_ref[...]
        is_last = lax.broadcasted_iota(jnp.int32, (GLA_CHUNK, 1), 0) == GLA_CHUNK - 1
        dq, dk, dv, dr, db = [], [], [], [], []
        dgnw = jnp.zeros((1, GLA_DV), F32)
        for h in range(GLA_HEADS):
            s64 = slice(h * GLA_DK, (h + 1) * GLA_DK)
            s128 = slice(h * GLA_DV, (h + 1) * GLA_DV)
            qh, kh, vh, eblh = gq[:, s64], gk[:, s64], v[:, s128], ebl[:, s64]
            klh = kh * eblh
            st = st_ref[0, h]
            o, rh, dout = oraw_v[:, s128], r[:, s128], dog_v[:, s128]
            rstd = lax.rsqrt(jnp.mean(o * o, axis=-1, keepdims=True) + EPS)
            on = o * rstd
            sg = _sigmoid(rh)
            dr.append(dout * (on * gnw_v) * (sg * (1.0 + rh * (1.0 - sg))))
            dy = dout * (rh * sg)
            dgnw = dgnw + jnp.sum(dy * on, axis=0, keepdims=True)
            don = dy * gnw_v
            do = rstd * (don - on * jnp.mean(don * on, axis=-1, keepdims=True))
            a = jnp.where(tril, _mm_nt(qh, kh), 0.0)
            da = jnp.where(tril, _mm_nt(do, vh), 0.0)
            dsp = dstate[h]
            dkl = _mm(vh, dsp)
            dv.append(_mm_tn(a, do) + _mm_nt(klh, dsp))
            debl = jnp.sum(dsp * st, axis=0, keepdims=True)
            dgq = _mm(da, kh) + _mm(do, st)
            dgk = _mm_tn(da, qh)
            dstate[h] = dsp * eblh + _mm_tn(do, qh)
            dq.append(dgq * (scale * eb[:, s64]))
            dk.append((dgk + dkl * eblh) * enb[:, s64])
            last = debl * eblh + jnp.sum(dkl * klh, axis=0, keepdims=True)
            db.append(dgq * qh - dgk * kh - dkl * klh + jnp.where(is_last, last, 0.0))
        dq_ref[...] = jnp.concatenate(dq, axis=1).astype(ACT_DTYPE)
        dk_ref[...] = jnp.concatenate(dk, axis=1).astype(ACT_DTYPE)
        dv_ref[...] = jnp.concatenate(dv, axis=1).astype(ACT_DTYPE)
        dr_ref[...] = jnp.concatenate(dr, axis=1).astype(ACT_DTYPE)
        triu = jnp.logical_not(tril) | (lax.broadcasted_iota(jnp.int32, (GLA_CHUNK, GLA_CHUNK), 0)
                                        == lax.broadcasted_iota(jnp.int32, (GLA_CHUNK, GLA_CHUNK), 1))
        dg = jnp.dot(triu.astype(F32), jnp.concatenate(db, axis=1), precision=HIGHEST, preferred_element_type=F32)
        dzg = jnp.where(live, dg * _sigmoid(-zg) * (1.0 / GLA_TAU), 0.0)
        dlr_ref[...] = _mm_nt(dzg, wg).astype(ACT_DTYPE)
        dwg_ref[...] += _mm_tn(lr, dzg)
        dbg_ref[...] += jnp.broadcast_to(jnp.sum(dzg, axis=0, keepdims=True), dbg_ref.shape)
        dgnw_ref[...] += jnp.broadcast_to(dgnw, dgnw_ref.shape)

    nb = lambda w, col: pl.BlockSpec((GLA_CHUNK, w), lambda t: (nc - 1 - t, col // w))
    const = lambda shape: pl.BlockSpec(shape, lambda t: (0,) * len(shape))
    return pl.pallas_call(
        body, name="gla_bwd", grid=(nc,),
        in_specs=[nb(256, C_GQ), nb(256, C_GK), nb(512, C_GV), nb(512, C_GR), nb(128, C_LR), nb(512, 0),
                  pl.BlockSpec((1, GLA_HEADS, GLA_DV, GLA_DK), lambda t: (nc - 1 - t, 0, 0, 0)), nb(512, 0),
                  const((128, 256)), const((1, 256)), const((1, 128))],
        out_specs=[nb(256, 0), nb(256, 0), nb(512, 0), nb(512, 0), nb(128, 0),
                   const((128, 256)), const((8, 256)), const((8, 128))],
        out_shape=[jax.ShapeDtypeStruct((rows, 256), ACT_DTYPE), jax.ShapeDtypeStruct((rows, 256), ACT_DTYPE),
                   jax.ShapeDtypeStruct((rows, 512), ACT_DTYPE), jax.ShapeDtypeStruct((rows, 512), ACT_DTYPE),
                   jax.ShapeDtypeStruct((rows, 128), ACT_DTYPE), jax.ShapeDtypeStruct((128, 256), F32),
                   jax.ShapeDtypeStruct((8, 256), F32), jax.ShapeDtypeStruct((8, 128), F32)],
        scratch_shapes=[pltpu.VMEM((GLA_HEADS, GLA_DV, GLA_DK), F32)],
        compiler_params=_cp(("arbitrary",)),
    )(proj, proj, proj, proj, proj, oraw, states, dog, wg_p, bg, gnw)


def _in_proj_bwd(h0, dh1, nw, win_p, dgv, dgr, dsq, dgq, dgk, dsk, dsv, dlr, tabs, tm):
    rows = h0.shape[0]

    def body(h_ref, dh1_ref, nw_ref, w_ref, dgv_ref, dgr_ref, dsq_ref, dgq_ref, dgk_ref, dsk_ref, dsv_ref, dlr_ref,
             c_ref, sa_ref, sb_ref, dh0_ref, dw_ref, gnm_ref):
        i = pl.program_id(0)

        @pl.when(i == 0)
        def _():
            dw_ref[...] = jnp.zeros_like(dw_ref)
            gnm_ref[...] = jnp.zeros_like(gnm_ref)

        cos, sa, sb = c_ref[...], sa_ref[...], sb_ref[...]
        dsq_v = (_unrope(dsq_ref[...], cos, sa, sb) * (SWA_HD ** -0.5)).astype(MXU_DTYPE)
        dsk_v = _unrope(dsk_ref[...], cos, sa, sb).astype(MXU_DTYPE)
        dproj = jnp.concatenate(
            [dgv_ref[...].astype(MXU_DTYPE), dgr_ref[...].astype(MXU_DTYPE), dsq_v, dgq_ref[...].astype(MXU_DTYPE),
             dgk_ref[...].astype(MXU_DTYPE), dsk_v, dsv_ref[...].astype(MXU_DTYPE), dlr_ref[...].astype(MXU_DTYPE)],
            axis=1)
        h = h_ref[...]
        rstd = lax.rsqrt(jnp.mean(h * h, axis=-1, keepdims=True) + EPS)
        hn = h * rstd
        nw_v = nw_ref[...]
        u = (hn * nw_v).astype(MXU_DTYPE)
        du = _mm_nt(dproj, w_ref[...])
        dw_ref[...] += _mm_tn(u, dproj)
        gnm_ref[...] += jnp.broadcast_to(jnp.sum(du * hn, axis=0, keepdims=True), gnm_ref.shape)
        dun = du * nw_v
        dh0_ref[...] = dh1_ref[...] + rstd * (dun - hn * jnp.mean(dun * hn, axis=-1, keepdims=True))

    row = lambda w: pl.BlockSpec((tm, w), lambda i: (i, 0))
    return pl.pallas_call(
        body, name="in_proj_bwd", grid=(rows // tm,),
        in_specs=[row(D), row(D), pl.BlockSpec((1, D), lambda i: (0, 0)), pl.BlockSpec((D, DINP), lambda i: (0, 0)),
                  row(512), row(512), row(512), row(256), row(256), row(128), row(128), row(128),
                  row(128), row(128), row(128)],
        out_specs=[row(D), pl.BlockSpec((D, DINP), lambda i: (0, 0)), pl.BlockSpec((8, D), lambda i: (0, 0))],
        out_shape=[jax.ShapeDtypeStruct((rows, D), F32), jax.ShapeDtypeStruct((D, DINP), F32),
                   jax.ShapeDtypeStruct((8, D), F32)],
        compiler_params=_cp(("arbitrary",), 56),
    )(h0, dh1, nw, win_p, dgv, dgr, dsq, dgq, dgk, dsk, dsv, dlr, *tabs)


def _adamw(w, g, m, v):
    m = ADAM_B1 * m + (1.0 - ADAM_B1) * g
    v = ADAM_B2 * v + (1.0 - ADAM_B2) * jnp.square(g)
    m_hat = m / (1.0 - ADAM_B1 ** ADAM_STEP)
    v_hat = v / (1.0 - ADAM_B2 ** ADAM_STEP)
    delta = -ADAM_LR * (m_hat / (jnp.sqrt(v_hat) + ADAM_EPS) + ADAM_WD * w)
    return delta, m, v


def _adamw_shard(where, parts, own, w, m, v, name):
    r, cdim = w.shape
    tr = 128 if r % 128 == 0 else r

    def body(where_ref, p_ref, own_ref, w_ref, m_ref, v_ref, g_ref, d_ref, nm_ref, nv_ref):
        g = ((p_ref[0] + p_ref[1]) + p_ref[2]) + own_ref[...]
        g_ref[...] = g
        d_ref[...], nm_ref[...], nv_ref[...] = _adamw(w_ref[...], g, m_ref[...], v_ref[...])

    spec = pl.BlockSpec((tr, cdim), lambda i, s: (i, 0))
    shape = jax.ShapeDtypeStruct((r, cdim), F32)
    return pl.pallas_call(
        body, name=name,
        grid_spec=pltpu.PrefetchScalarGridSpec(
            num_scalar_prefetch=1, grid=(r // tr,),
            in_specs=[pl.BlockSpec((3, tr, cdim), lambda i, s: (0, i, 0)),
                      pl.BlockSpec((None, tr, cdim), lambda i, s: (s[1], i, 0)), spec, spec, spec],
            out_specs=[spec] * 4),
        out_shape=[shape] * 4,
        compiler_params=_cp(("arbitrary",)),
    )(where, parts, own, w, m, v)


def _adamw_small(w, g, m, v):
    def body(w_ref, g_ref, m_ref, v_ref, d_ref, nm_ref, nv_ref):
        d_ref[...], nm_ref[...], nv_ref[...] = _adamw(w_ref[...], g_ref[...], m_ref[...], v_ref[...])

    vm = pl.BlockSpec(memory_space=pltpu.VMEM)
    shape = jax.ShapeDtypeStruct(w.shape, F32)
    return pl.pallas_call(body, name="adamw_small", in_specs=[vm] * 4, out_specs=[vm] * 3,
                          out_shape=[shape] * 3)(w, g, m, v)


def _add_own_half(where, full, theirs, name):
    _, _, r, cdim = full.shape
    tr = 128 if r % 128 == 0 else r

    def body(where_ref, a_ref, b_ref, o_ref):
        o_ref[...] = a_ref[...] + b_ref[...]

    spec = pl.BlockSpec((4, tr, cdim), lambda i, s: (0, i, 0))
    return pl.pallas_call(
        body, name=name,
        grid_spec=pltpu.PrefetchScalarGridSpec(
            num_scalar_prefetch=1, grid=(r // tr,),
            in_specs=[pl.BlockSpec((None, 4, tr, cdim), lambda i, s: (s[0], 0, i, 0)), spec], out_specs=spec),
        out_shape=jax.ShapeDtypeStruct(theirs.shape, F32), compiler_params=_cp(("arbitrary",)))(where, full, theirs)


def _to_rows128(a):
    return a.reshape(-1, 128)


def _pad_rows128(vec):
    flat = vec.reshape(-1)
    n = -(-flat.shape[0] // 128)
    return jnp.pad(flat, (0, n * 128 - flat.shape[0])).reshape(n, 128)


def kernel(x, meta_tokens, norm_mix_w, w_in, w_gate_up, b_gate, gla_norm_w, sinks, w_out, norm_ff_w, w_ff1, w_ff2, final_norm_w, loss_target, m_meta_tokens, m_norm_mix_w, m_w_in, m_w_gate_up, m_b_gate, m_gla_norm_w, m_sinks, m_w_out, m_norm_ff_w, m_w_ff1, m_w_ff2, m_final_norm_w, v_meta_tokens, v_norm_mix_w, v_w_in, v_w_gate_up, v_b_gate, v_gla_norm_w, v_sinks, v_w_out, v_norm_ff_w, v_w_ff1, v_w_ff2, v_final_norm_w):
    seq = x.shape[1]
    rows = LEAD + seq
    tm = _row_tile(rows)
    tm_small = tm // 2 if tm == 640 else tm
    dev = 4 * lax.axis_index("x") + 2 * lax.axis_index("y") + lax.axis_index("c")

    small_shard = jnp.concatenate([meta_tokens, w_gate_up[0], jnp.zeros((N_META, 96), F32)], axis=1)
    g_in, g_out, g_w1, g_w2, g_small = _all_gather(
        [w_in[0].astype(WIRE_DTYPE), w_out[0].astype(WIRE_DTYPE), w_ff1[0].astype(WIRE_DTYPE),
         w_ff2[0].astype(WIRE_DTYPE), small_shard])
    win_full = jnp.transpose(g_in, (1, 0, 2)).reshape(D, DIN)
    cols = lambda r: win_full[:, r[0]:r[1]]
    win_p = jnp.concatenate([cols(O_GV), cols(O_GR), cols(O_SQ), cols(O_GQ), cols(O_GK), cols(O_SK), cols(O_SV),
                             cols(O_LR), jnp.zeros((D, 128 - GLA_RANK), WIRE_DTYPE)], axis=1)
    wout_full = g_out.reshape(D, D)
    w2_full = g_w2.reshape(D_FF, D)
    meta_full = jnp.transpose(g_small[:, :, 0:128], (1, 0, 2)).reshape(N_META, D)
    wg_full = jnp.transpose(g_small[:, :, 128:160], (1, 0, 2)).reshape(GLA_RANK, GLA_HEADS * GLA_DK)
    wg_p = jnp.concatenate([wg_full, jnp.zeros((128 - GLA_RANK, 256), F32)], axis=0)

    h0 = jnp.concatenate([jnp.zeros((META0, D), F32), meta_full, x[0]], axis=0)
    tgt = jnp.concatenate([jnp.zeros((LEAD, D), F32), loss_target[0]], axis=0)
    tabs = _rope_tables(rows)
    proj = _in_proj(h0, norm_mix_w, win_p, tm)
    oraw, og, states = _gla_fwd(proj, wg_p, b_gate, gla_norm_w)
    qr, kr, vr = _swa_prep(proj, tabs, tm)
    osw = _swa_fwd(qr, kr, vr, sinks)
    h1, f = _out_proj(h0, og, osw, wout_full, norm_ff_w, tm)
    a, dh2, loss_p, gfn_p = _ffn_fwd(f, h1, g_w1, w2_full, tgt, final_norm_w.reshape(1, D), tm)

    da, dh1, gnf_p = _ffn_bwd_act(dh2, a, g_w1, w2_full, h1, norm_ff_w, tm)
    dw1, dw2 = _ffn_bwd_weights(f, a, da, dh2, tm)
    dog, dos, dwout = _out_proj_bwd(dh1, og, osw, wout_full, tm)
    dsq, dsk, dsv, dsink_p = _swa_bwd(qr, kr, vr, osw, dos, sinks)
    dgq, dgk, dgv, dgr, dlr, dwg_p, dbg_p, dgnw_p = _gla_bwd(proj, oraw, states, dog, wg_p, b_gate, gla_norm_w)
    dh0, dwin_p, gnm_p = _in_proj_bwd(h0, dh1, norm_mix_w, win_p, dgv, dgr, dsq, dgq, dgk, dsk, dsv, dlr, tabs, tm_small)
    grad_x = dh0[LEAD:][None]

    pcols = lambda c0, r: dwin_p[:, c0:c0 + (r[1] - r[0])]
    dwin = jnp.concatenate([pcols(C_GQ, O_GQ), pcols(C_GK, O_GK), pcols(C_GV, O_GV), pcols(C_GR, O_GR),
                            pcols(C_LR, O_LR), pcols(C_SQ, O_SQ), pcols(C_SK, O_SK), pcols(C_SV, O_SV)], axis=1)
    dwin = jnp.transpose(dwin.reshape(D, 4, 2, DIN // N_DEV), (2, 1, 0, 3))
    where = jnp.stack([lax.axis_index("c"), 2 * lax.axis_index("x") + lax.axis_index("y")]).astype(jnp.int32)
    partials = [dwin, dwout, dw1, dw2]
    theirs = _rs_sibling(partials)
    chip_sums = [_add_own_half(where, p, q, "reduce_pair_%d" % k) for k, (p, q) in enumerate(zip(partials, theirs))]
    parts = _rs_chips(chip_sums)

    small = [dh0[META0:LEAD], dwg_p[0:GLA_RANK], gnm_p[0:1], dbg_p[0:1], dgnw_p[0:1], dsink_p[:, 0], gnf_p[0:1],
             gfn_p[0:1], loss_p[0:1, 0:1]]
    sizes = [-(-s.size // 128) for s in small]
    pack = jnp.concatenate([_pad_rows128(s) for s in small], axis=0)
    pad_rows = -pack.shape[0] % 8
    pack = jnp.pad(pack, ((0, pad_rows), (0, 0)))
    total = _all_reduce_small(pack)
    offs = [sum(sizes[:k]) for k in range(len(sizes))]
    take = lambda k, shape: total[offs[k]:offs[k] + sizes[k]].reshape(-1)[:small[k].size].reshape(shape)
    g_meta_full = take(0, (N_META, D))
    g_wg_full = take(1, (GLA_RANK, 256))
    g_meta = lax.dynamic_slice_in_dim(g_meta_full, dev * 128, 128, axis=1)
    g_wg = lax.dynamic_slice_in_dim(g_wg_full, dev * 32, 32, axis=1)[None]
    g_norm_mix, g_b_gate, g_gla_norm = take(2, (1, D)), take(3, (1, 256)), take(4, (1, 128))
    g_sinks, g_norm_ff, g_final_norm = take(5, (1, 8)), take(6, (1, D)), take(7, (D,))
    loss = take(8, ())

    g_win, d_win, nm_win, nv_win = _adamw_shard(where, parts[0], chip_sums[0], w_in[0], m_w_in[0], v_w_in[0], "adamw_w_in")
    g_wout, d_wout, nm_wout, nv_wout = _adamw_shard(where, parts[1], chip_sums[1], w_out[0], m_w_out[0], v_w_out[0], "adamw_w_out")
    g_w1s, d_w1, nm_w1, nv_w1 = _adamw_shard(where, parts[2], chip_sums[2], w_ff1[0], m_w_ff1[0], v_w_ff1[0], "adamw_w_ff1")
    g_w2s, d_w2, nm_w2, nv_w2 = _adamw_shard(where, parts[3], chip_sums[3], w_ff2[0], m_w_ff2[0], v_w_ff2[0], "adamw_w_ff2")

    names = ["meta", "wg", "norm_mix", "b_gate", "gla_norm", "sinks", "norm_ff", "final_norm"]
    ws = [meta_tokens, w_gate_up, norm_mix_w, b_gate, gla_norm_w, sinks, norm_ff_w, final_norm_w]
    gs = [g_meta, g_wg, g_norm_mix, g_b_gate, g_gla_norm, g_sinks, g_norm_ff, g_final_norm]
    ms = [m_meta_tokens, m_w_gate_up, m_norm_mix_w, m_b_gate, m_gla_norm_w, m_sinks, m_norm_ff_w, m_final_norm_w]
    vs = [v_meta_tokens, v_w_gate_up, v_norm_mix_w, v_b_gate, v_gla_norm_w, v_sinks, v_norm_ff_w, v_final_norm_w]
    ssz = [-(-w.size // 128) for w in ws]
    packed = []
    for group in (ws, gs, ms, vs):
        p = jnp.concatenate([_pad_rows128(t) for t in group], axis=0)
        packed.append(jnp.pad(p, ((0, -p.shape[0] % 8), (0, 0))))
    d_s, nm_s, nv_s = _adamw_small(*packed)
    soffs = [sum(ssz[:k]) for k in range(len(ssz))]
    unpack = lambda t, k: t[soffs[k]:soffs[k] + ssz[k]].reshape(-1)[:ws[k].size].reshape(ws[k].shape)
    d_small = {n: unpack(d_s, k) for k, n in enumerate(names)}
    nm_small = {n: unpack(nm_s, k) for k, n in enumerate(names)}
    nv_small = {n: unpack(nv_s, k) for k, n in enumerate(names)}
    g_small_d = dict(zip(names, gs))

    def ordered(big, small_d):
        win_v, wout_v, w1_v, w2_v = big
        return (small_d["meta"], small_d["norm_mix"], win_v[None], small_d["wg"], small_d["b_gate"],
                small_d["gla_norm"], small_d["sinks"], wout_v[None], small_d["norm_ff"], w1_v[None], w2_v[None],
                small_d["final_norm"])

    return (loss, grad_x,
            *ordered((g_win, g_wout, g_w1s, g_w2s), g_small_d),
            *ordered((d_win, d_wout, d_w1, d_w2), d_small),
            *ordered((nm_win, nm_wout, nm_w1, nm_w2), nm_small),
            *ordered((nv_win, nv_wout, nv_w1, nv_w2), nv_small))
```

```python
import functools

import jax
import jax.numpy as jnp
from jax import lax
from jax.experimental import pallas as pl
from jax.experimental.pallas import tpu as pltpu

F32 = jnp.float32
MXU_DTYPE = jnp.bfloat16
ACT_DTYPE = jnp.bfloat16
WIRE_DTYPE = jnp.bfloat16

D = 1024
N_META = 16
LEAD = 128
META0 = LEAD - N_META
EPS = 1e-5
GLA_HEADS, GLA_DK, GLA_DV, GLA_RANK, GLA_CHUNK = 4, 64, 128, 16, 64
GLA_TAU = 16.0
SWA_HEADS, SWA_KV, SWA_GROUP, SWA_HD, SWA_BLOCK = 8, 2, 4, 64, 128
ROPE_DIM, ROPE_THETA = 16, 500000.0
D_FF = 4096
N_DEV = 8
FF_TILE = D_FF // N_DEV
NEG = -1e30

C_GV, C_GR, C_SQ, C_GQ, C_GK, C_SK, C_SV, C_LR = 0, 512, 1024, 1536, 1792, 2048, 2176, 2304
DINP = 2432
DIN = 2320
O_GQ, O_GK, O_GV, O_GR, O_LR, O_SQ, O_SK, O_SV = (0, 256), (256, 512), (512, 1024), (1024, 1536), (1536, 1552), (1552, 2064), (2064, 2192), (2192, 2320)

ADAM_LR, ADAM_B1, ADAM_B2, ADAM_EPS, ADAM_WD, ADAM_STEP = 0.001, 0.9, 0.999, 1e-08, 0.01, 10

MESH = pl.DeviceIdType.MESH
ANY = pl.BlockSpec(memory_space=pl.ANY)
HIGHEST = lax.Precision.HIGHEST


def _cp(sem=None, vmem_mb=None):
    kw = {}
    if sem is not None:
        kw["dimension_semantics"] = sem
    if vmem_mb is not None:
        kw["vmem_limit_bytes"] = vmem_mb << 20
    return pltpu.CompilerParams(**kw)


def _mm(a, b):
    return jnp.dot(a.astype(MXU_DTYPE), b.astype(MXU_DTYPE), preferred_element_type=F32)


def _mm_nt(a, b):
    return lax.dot_general(a.astype(MXU_DTYPE), b.astype(MXU_DTYPE), (((1,), (1,)), ((), ())),
                           preferred_element_type=F32)


def _mm_tn(a, b):
    return lax.dot_general(a.astype(MXU_DTYPE), b.astype(MXU_DTYPE), (((0,), (0,)), ((), ())),
                           preferred_element_type=F32)


def _logsigmoid(z):
    return jnp.minimum(z, 0.0) - jnp.log(1.0 + jnp.exp(-jnp.abs(z)))


def _sigmoid(z):
    return 1.0 / (1.0 + jnp.exp(-z))


def _row_tile(rows):
    return 640 if rows % 640 == 0 else 128


def _mesh_pos():
    return lax.axis_index("x"), lax.axis_index("y"), lax.axis_index("c")


def _all_gather(shards):
    n = len(shards)

    def body(*refs):
        start, forward, finish = _gather_schedule(refs[:n], refs[n:2 * n], *refs[2 * n:])
        start()
        for j in range(3):
            forward(j)
        finish()

    gathered = pl.pallas_call(
        body, name="all_gather_weights",
        out_shape=_gathered_shapes(shards), in_specs=[ANY] * n, out_specs=[ANY] * n,
        scratch_shapes=_gather_sems(n),
    )(*shards)
    return _with_own_block(gathered, shards)


def _gathered_shapes(shards):
    return [jax.ShapeDtypeStruct((N_DEV,) + s.shape, s.dtype) for s in shards]


def _gather_sems(n):
    return [pltpu.SemaphoreType.DMA((7 * n,)), pltpu.SemaphoreType.DMA((7 * n,))]


def _with_own_block(gathered, shards):
    dev = 4 * lax.axis_index("x") + 2 * lax.axis_index("y") + lax.axis_index("c")
    return [lax.dynamic_update_index_in_dim(g, s, dev, 0) for g, s in zip(gathered, shards)]


def _gather_schedule(ins, outs, send_sems, recv_sems):
    n = len(ins)
    x, y, c = _mesh_pos()
    me, sibling = (x, y, c), (x, y, 1 - c)
    chips = [(1 - x, y), (x, 1 - y), (1 - x, 1 - y)]

    def copy(a, k, block, to, src=None):
        dst = outs[a].at[4 * block[0] + 2 * block[1] + block[2]]
        return pltpu.make_async_remote_copy(
            src_ref=dst if src is None else src, dst_ref=dst,
            send_sem=send_sems.at[a * 7 + k], recv_sem=recv_sems.at[a * 7 + k],
            device_id=to, device_id_type=MESH)

    def first(a):
        return [copy(a, 0, me, sibling, src=ins[a])] + [copy(a, 1 + j, me, (*chip, c), src=ins[a])
                                                        for j, chip in enumerate(chips)]

    def start():
        for a in range(n):
            for cp in first(a):
                cp.start()

    def forward(j):
        for a in range(n):
            copy(a, 1 + j, (*chips[j], c), me).wait_recv()
            copy(a, 4 + j, (*chips[j], c), sibling).start()

    def finish():
        for a in range(n):
            copy(a, 0, sibling, me).wait_recv()
            for j, chip in enumerate(chips):
                copy(a, 4 + j, (*chip, 1 - c), me).wait_recv()
        for a in range(n):
            for cp in first(a) + [copy(a, 4 + j, (*chip, c), sibling) for j, chip in enumerate(chips)]:
                cp.wait_send()

    return start, forward, finish


def _rs_sibling(gs):
    n = len(gs)

    def body(*refs):
        start, finish = _sibling_schedule(refs[:n], refs[n:2 * n], *refs[2 * n:])
        start()
        finish()

    return pl.pallas_call(
        body, name="reduce_scatter_sibling",
        out_shape=_sibling_shapes(gs), in_specs=[ANY] * n, out_specs=[ANY] * n,
        scratch_shapes=_sibling_sems(n),
    )(*gs)


def _sibling_shapes(gs):
    return [jax.ShapeDtypeStruct(g.shape[1:], g.dtype) for g in gs]


def _sibling_sems(n):
    return [pltpu.SemaphoreType.DMA((n,)), pltpu.SemaphoreType.DMA((n,))]


def _sibling_schedule(ins, land, send_sems, recv_sems):
    x, y, c = _mesh_pos()

    def copies():
        return [pltpu.make_async_remote_copy(
            src_ref=ins[a].at[1 - c], dst_ref=land[a], send_sem=send_sems.at[a], recv_sem=recv_sems.at[a],
            device_id=(x, y, 1 - c), device_id_type=MESH) for a in range(len(ins))]

    def start():
        for cp in copies():
            cp.start()

    def finish():
        for cp in copies():
            cp.wait_recv()
        for cp in copies():
            cp.wait_send()

    return start, finish


def _rs_chips(ps):
    n = len(ps)

    def body(*refs):
        start, finish = _chips_schedule(refs[:n], refs[n:2 * n], *refs[2 * n:])
        start()
        finish()

    return pl.pallas_call(
        body, name="reduce_scatter_chips",
        out_shape=_chips_shapes(ps), in_specs=[ANY] * n, out_specs=[ANY] * n,
        scratch_shapes=_chips_sems(n),
    )(*ps)


def _chips_shapes(ps):
    return [jax.ShapeDtypeStruct((3,) + p.shape[1:], p.dtype) for p in ps]


def _chips_sems(n):
    return [pltpu.SemaphoreType.DMA((3 * n,)), pltpu.SemaphoreType.DMA((3 * n,))]


def _chips_schedule(ins, land, send_sems, recv_sems):
    x, y, c = _mesh_pos()
    chips = [(1 - x, y), (x, 1 - y), (1 - x, 1 - y)]

    def copies():
        return [pltpu.make_async_remote_copy(
            src_ref=ins[a].at[2 * chip[0] + chip[1]], dst_ref=land[a].at[j],
            send_sem=send_sems.at[3 * a + j], recv_sem=recv_sems.at[3 * a + j],
            device_id=(*chip, c), device_id_type=MESH) for a in range(len(ins)) for j, chip in enumerate(chips)]

    def start():
        for cp in copies():
            cp.start()

    def finish():
        for cp in copies():
            cp.wait_recv()
        for cp in copies():
            cp.wait_send()

    return start, finish


def _all_reduce_small(pack):
    rows = pack.shape[0]

    def body(p_ref, out_ref, land, send_sems, recv_sems):
        x, y, c = _mesh_pos()
        me = 4 * x + 2 * y + c
        land[me] = p_ref[...]
        copies = []
        for k in range(1, N_DEV):
            bx, by, bc = (k >> 2) & 1, (k >> 1) & 1, k & 1
            peer = (1 - x if bx else x, 1 - y if by else y, 1 - c if bc else c)
            copies.append(pltpu.make_async_remote_copy(
                src_ref=p_ref, dst_ref=land.at[me], send_sem=send_sems.at[k - 1], recv_sem=recv_sems.at[k - 1],
                device_id=peer, device_id_type=MESH))
        for cp in copies:
            cp.start()
        for cp in copies:
            cp.wait_recv()
        for cp in copies:
            cp.wait_send()
        acc = land[0]
        for d in range(1, N_DEV):
            acc = acc + land[d]
        out_ref[...] = acc

    return pl.pallas_call(
        body, name="all_reduce_small",
        out_shape=jax.ShapeDtypeStruct(pack.shape, F32),
        in_specs=[pl.BlockSpec(memory_space=pltpu.VMEM)], out_specs=pl.BlockSpec(memory_space=pltpu.VMEM),
        scratch_shapes=[pltpu.VMEM((N_DEV, rows, 128), F32), pltpu.SemaphoreType.DMA((7,)),
                        pltpu.SemaphoreType.DMA((7,))],
    )(pack)


def _in_proj(h0, nw, win_p, tm):
    rows = h0.shape[0]

    def body(h_ref, nw_ref, w_ref, o_ref):
        h = h_ref[...]
        rstd = lax.rsqrt(jnp.mean(h * h, axis=-1, keepdims=True) + EPS)
        u = (h * rstd * nw_ref[...]).astype(MXU_DTYPE)
        o_ref[...] = jnp.dot(u, w_ref[...].astype(MXU_DTYPE), preferred_element_type=F32)

    return pl.pallas_call(
        body, name="in_proj", grid=(rows // tm,),
        in_specs=[pl.BlockSpec((tm, D), lambda i: (i, 0)), pl.BlockSpec((1, D), lambda i: (0, 0)),
                  pl.BlockSpec((D, DINP), lambda i: (0, 0))],
        out_specs=pl.BlockSpec((tm, DINP), lambda i: (i, 0)),
        out_shape=jax.ShapeDtypeStruct((rows, DINP), F32),
        compiler_params=_cp(("arbitrary",), 56),
    )(h0, nw, win_p)


def _rope_tables(rows):
    pos = (jnp.arange(rows, dtype=jnp.int32) - META0).astype(F32)
    inv_freq = 1.0 / (ROPE_THETA ** (jnp.arange(0, ROPE_DIM, 2, dtype=F32) / ROPE_DIM))
    ang = pos[:, None] * jnp.tile(inv_freq, 128 // (ROPE_DIM // 2))[None, :]
    in_head = jnp.arange(128, dtype=jnp.int32)[None, :] % SWA_HD
    cos, sin = jnp.cos(ang), jnp.sin(ang)
    c_tab = jnp.where(in_head < ROPE_DIM, cos, 1.0)
    sa_tab = jnp.where(in_head < ROPE_DIM // 2, -sin, 0.0)
    sb_tab = jnp.where((in_head >= ROPE_DIM // 2) & (in_head < ROPE_DIM), sin, 0.0)
    return c_tab, sa_tab, sb_tab


def _rope(xv, cos, sa, sb):
    width = xv.shape[1]
    reps = width // 128
    if reps > 1:
        cos, sa, sb = (jnp.tile(t, (1, reps)) for t in (cos, sa, sb))
    return xv * cos + pltpu.roll(xv, width - 8, 1) * sa + pltpu.roll(xv, 8, 1) * sb


def _unrope(dy, cos, sa, sb):
    width = dy.shape[1]
    reps = width // 128
    if reps > 1:
        cos, sa, sb = (jnp.tile(t, (1, reps)) for t in (cos, sa, sb))
    return dy * cos + pltpu.roll(dy * sa, 8, 1) + pltpu.roll(dy * sb, width - 8, 1)


def _swa_prep(proj, tabs, tm):
    rows = proj.shape[0]

    def body(q_ref, k_ref, v_ref, c_ref, sa_ref, sb_ref, qo_ref, ko_ref, vo_ref):
        cos, sa, sb = c_ref[...], sa_ref[...], sb_ref[...]
        qo_ref[...] = (_rope(q_ref[...], cos, sa, sb) * (SWA_HD ** -0.5)).astype(ACT_DTYPE)
        ko_ref[...] = _rope(k_ref[...], cos, sa, sb).astype(ACT_DTYPE)
        vo_ref[...] = v_ref[...].astype(ACT_DTYPE)

    tab_spec = pl.BlockSpec((tm, 128), lambda i: (i, 0))
    return pl.pallas_call(
        body, name="swa_prep", grid=(rows // tm,),
        in_specs=[pl.BlockSpec((tm, 512), lambda i: (i, C_SQ // 512)),
                  pl.BlockSpec((tm, 128), lambda i: (i, C_SK // 128)),
                  pl.BlockSpec((tm, 128), lambda i: (i, C_SV // 128)), tab_spec, tab_spec, tab_spec],
        out_specs=[pl.BlockSpec((tm, 512), lambda i: (i, 0)), tab_spec, tab_spec],
        out_shape=[jax.ShapeDtypeStruct((rows, 512), ACT_DTYPE), jax.ShapeDtypeStruct((rows, 128), ACT_DTYPE),
                   jax.ShapeDtypeStruct((rows, 128), ACT_DTYPE)],
        compiler_params=_cp(("arbitrary",)),
    )(proj, proj, proj, *tabs)


def _gla_gates(lr, wg, bg, chunk):
    zg = _mm(lr, wg) + bg
    row = chunk * GLA_CHUNK + lax.broadcasted_iota(jnp.int32, (GLA_CHUNK, 1), 0)
    live = row >= META0
    g = jnp.where(live, _logsigmoid(zg) * (1.0 / GLA_TAU), 0.0)
    ii = lax.broadcasted_iota(jnp.int32, (GLA_CHUNK, GLA_CHUNK), 0)
    jj = lax.broadcasted_iota(jnp.int32, (GLA_CHUNK, GLA_CHUNK), 1)
    tril = jj <= ii
    b = jnp.dot(tril.astype(F32), g, precision=HIGHEST, preferred_element_type=F32)
    return zg, live, tril, b


def _gla_fwd(proj, wg_p, bg, gnw, shards):
    rows = proj.shape[0]
    nc = rows // GLA_CHUNK
    ns = len(shards)
    forward_steps = [nc * 7 // 10, nc * 8 // 10, nc * 9 // 10]

    def body(q_ref, k_ref, v_ref, r_ref, lr_ref, wg_ref, bg_ref, gnw_ref, *rest):
        shard_refs, rest = rest[:ns], rest[ns:]
        oraw_ref, og_ref, st_ref = rest[:3]
        gathered_refs, rest = rest[3:3 + ns], rest[3 + ns:]
        state, send_sems, recv_sems = rest
        c = pl.program_id(0)
        start, forward, finish = _gather_schedule(shard_refs, gathered_refs, send_sems, recv_sems)

        @pl.when(c == 0)
        def _():
            state[...] = jnp.zeros_like(state)
            start()

        for j, step in enumerate(forward_steps):
            pl.when(c == step)(functools.partial(forward, j))
        pl.when(c == nc - 1)(finish)

        zg, live, tril, b = _gla_gates(lr_ref[...], wg_ref[...], bg_ref[...], c)
        eb = jnp.exp(b)
        gq = q_ref[...] * (GLA_DK ** -0.5) * eb
        gk = k_ref[...] * jnp.exp(-b)
        ebl = eb[GLA_CHUNK - 1:GLA_CHUNK, :]
        v = v_ref[...]
        r = r_ref[...]
        gnw_v = gnw_ref[...]
        oraw, og = [], []
        for h in range(GLA_HEADS):
            s64 = slice(h * GLA_DK, (h + 1) * GLA_DK)
            s128 = slice(h * GLA_DV, (h + 1) * GLA_DV)
            qh, kh, vh, eblh = gq[:, s64], gk[:, s64], v[:, s128], ebl[:, s64]
            st = state[h]
            st_ref[0, h] = st
            a = jnp.where(tril, _mm_nt(qh, kh), 0.0)
            o = _mm(a, vh) + _mm_nt(qh, st)
            state[h] = st * eblh + _mm_tn(vh, kh * eblh)
            oraw.append(o)
            rstd = lax.rsqrt(jnp.mean(o * o, axis=-1, keepdims=True) + EPS)
            rh = r[:, s128]
            og.append(o * rstd * gnw_v * (rh * _sigmoid(rh)))
        oraw_ref[...] = jnp.concatenate(oraw, axis=1)
        og_ref[...] = jnp.concatenate(og, axis=1).astype(ACT_DTYPE)

    nb = lambda w, col: pl.BlockSpec((GLA_CHUNK, w), lambda c: (c, col // w))
    const = lambda shape: pl.BlockSpec(shape, lambda c: (0,) * len(shape))
    outs = pl.pallas_call(
        body, name="gla_fwd", grid=(nc,),
        in_specs=[nb(256, C_GQ), nb(256, C_GK), nb(512, C_GV), nb(512, C_GR), nb(128, C_LR),
                  const((128, 256)), const((1, 256)), const((1, 128))] + [ANY] * ns,
        out_specs=[pl.BlockSpec((GLA_CHUNK, 512), lambda c: (c, 0)), pl.BlockSpec((GLA_CHUNK, 512), lambda c: (c, 0)),
                   pl.BlockSpec((1, GLA_HEADS, GLA_DV, GLA_DK), lambda c: (c, 0, 0, 0))] + [ANY] * ns,
        out_shape=[jax.ShapeDtypeStruct((rows, 512), F32), jax.ShapeDtypeStruct((rows, 512), ACT_DTYPE),
                   jax.ShapeDtypeStruct((nc, GLA_HEADS, GLA_DV, GLA_DK), F32)] + _gathered_shapes(shards),
        scratch_shapes=[pltpu.VMEM((GLA_HEADS, GLA_DV, GLA_DK), F32)] + _gather_sems(ns),
        compiler_params=_cp(("arbitrary",)),
    )(proj, proj, proj, proj, proj, wg_p, bg, gnw, *shards)
    return outs[0], outs[1], outs[2], _with_own_block(outs[3:], shards)


def _swa_mask(n):
    shape = (SWA_GROUP * SWA_BLOCK, 3 * SWA_BLOCK)
    qi = lax.broadcasted_iota(jnp.int32, shape, 0) & (SWA_BLOCK - 1)
    jj = lax.broadcasted_iota(jnp.int32, shape, 1)
    meta = (jj < SWA_BLOCK) & (jj >= META0) & ((n > 0) | (jj <= qi))
    prev = (jj >= SWA_BLOCK) & (jj < 2 * SWA_BLOCK) & (n >= 2) & (jj - SWA_BLOCK > qi)
    cur = (jj >= 2 * SWA_BLOCK) & (n >= 1) & (jj - 2 * SWA_BLOCK <= qi)
    return meta | prev | cur


def _stack_heads(t, kvh):
    return jnp.concatenate([t[:, (kvh * SWA_GROUP + g) * SWA_HD:(kvh * SWA_GROUP + g + 1) * SWA_HD]
                            for g in range(SWA_GROUP)], axis=0)


def _stack_sinks(sink_ref, kvh):
    return jnp.concatenate([jnp.full((SWA_BLOCK, 1), sink_ref[0, kvh * SWA_GROUP + g], F32)
                            for g in range(SWA_GROUP)], axis=0)


def _swa_specs():
    blk = lambda w: pl.BlockSpec((SWA_BLOCK, w), lambda n: (n, 0))
    first = pl.BlockSpec((SWA_BLOCK, 128), lambda n: (0, 0))
    prev = pl.BlockSpec((SWA_BLOCK, 128), lambda n: (jnp.maximum(n - 1, 0), 0))
    return blk, first, prev


def _swa_fwd(qr, kr, vr, sinks):
    rows = qr.shape[0]
    nblk = rows // SWA_BLOCK

    def body(q_ref, k0, kp, kc, v0, vp, vc, sink_ref, o_ref):
        n = pl.program_id(0)
        q = q_ref[...]
        kall = jnp.concatenate([k0[...], kp[...], kc[...]], axis=0)
        vall = jnp.concatenate([v0[...], vp[...], vc[...]], axis=0)
        mask = _swa_mask(n)[0:SWA_BLOCK]
        outs = []
        for head in range(SWA_HEADS):
            kv = slice((head // SWA_GROUP) * SWA_HD, (head // SWA_GROUP + 1) * SWA_HD)
            s = jnp.where(mask, _mm_nt(q[:, head * SWA_HD:(head + 1) * SWA_HD], kall[:, kv]), NEG)
            sink = sink_ref[0, head]
            m = jnp.maximum(jnp.max(s, axis=-1, keepdims=True), sink)
            p = jnp.exp(s - m)
            den = jnp.sum(p, axis=-1, keepdims=True) + jnp.exp(sink - m)
            outs.append(_mm(p, vall[:, kv]) / den)
        o_ref[...] = jnp.concatenate(outs, axis=1).astype(ACT_DTYPE)

    blk, first, prev = _swa_specs()
    return pl.pallas_call(
        body, name="swa_fwd", grid=(nblk,),
        in_specs=[blk(512), first, prev, blk(128), first, prev, blk(128),
                  pl.BlockSpec(memory_space=pltpu.SMEM)],
        out_specs=blk(512),
        out_shape=jax.ShapeDtypeStruct((rows, 512), ACT_DTYPE),
        compiler_params=_cp(("arbitrary",)),
    )(qr, kr, kr, kr, vr, vr, vr, sinks)


def _out_proj(h0, og, osw, wout, nfw, tm):
    rows = h0.shape[0]

    def body(h_ref, og_ref, os_ref, w_ref, nw_ref, h1_ref, f_ref):
        h1 = h_ref[...] + _mm(og_ref[...], w_ref[0:512, :]) + _mm(os_ref[...], w_ref[512:1024, :])
        h1_ref[...] = h1
        rstd = lax.rsqrt(jnp.mean(h1 * h1, axis=-1, keepdims=True) + EPS)
        f_ref[...] = (h1 * rstd * nw_ref[...]).astype(ACT_DTYPE)

    row = lambda w: pl.BlockSpec((tm, w), lambda i: (i, 0))
    return pl.pallas_call(
        body, name="out_proj", grid=(rows // tm,),
        in_specs=[row(D), row(512), row(512), pl.BlockSpec((D, D), lambda i: (0, 0)), pl.BlockSpec((1, D), lambda i: (0, 0))],
        out_specs=[row(D), row(D)],
        out_shape=[jax.ShapeDtypeStruct((rows, D), F32), jax.ShapeDtypeStruct((rows, D), ACT_DTYPE)],
        compiler_params=_cp(("arbitrary",), 48),
    )(h0, og, osw, wout, nfw)


def _ffn_fwd(f, h1, w1g, w2, tgt, fnw, tm):
    rows = f.shape[0]
    nj = N_DEV

    def body(f_ref, h1_ref, w1_ref, w2_ref, t_ref, nw_ref, a_ref, dh2_ref, loss_ref, gfn_ref, acc):
        i, j = pl.program_id(0), pl.program_id(1)

        @pl.when((i == 0) & (j == 0))
        def _():
            loss_ref[...] = jnp.zeros_like(loss_ref)
            gfn_ref[...] = jnp.zeros_like(gfn_ref)

        @pl.when(j == 0)
        def _():
            acc[...] = jnp.zeros_like(acc)

        a = _mm(f_ref[...], w1_ref[...])
        a_ref[...] = a.astype(ACT_DTYPE)
        z = jnp.square(jnp.maximum(a, 0.0))
        acc[...] += _mm(z, w2_ref[...])

        @pl.when(j == nj - 1)
        def _():
            h2 = h1_ref[...] + acc[...]
            rstd = lax.rsqrt(jnp.mean(h2 * h2, axis=-1, keepdims=True) + EPS)
            hn = h2 * rstd
            nw = nw_ref[...]
            row = i * tm + lax.broadcasted_iota(jnp.int32, (tm, 1), 0)
            err = jnp.where(row >= LEAD, hn * nw - t_ref[...], 0.0)
            row_loss = jnp.sum(err * err, axis=-1, keepdims=True) * (1.0 / D)
            loss_ref[...] += jnp.broadcast_to(0.5 * jnp.sum(row_loss, axis=0, keepdims=True), loss_ref.shape)
            dy = err * (1.0 / D)
            gfn_ref[...] += jnp.broadcast_to(jnp.sum(dy * hn, axis=0, keepdims=True), gfn_ref.shape)
            dhn = dy * nw
            dh2_ref[...] = rstd * (dhn - hn * jnp.mean(dhn * hn, axis=-1, keepdims=True))

    return pl.pallas_call(
        body, name="ffn_fwd", grid=(rows // tm, nj),
        in_specs=[pl.BlockSpec((tm, D), lambda i, j: (i, 0)), pl.BlockSpec((tm, D), lambda i, j: (i, 0)),
                  pl.BlockSpec((None, D, FF_TILE), lambda i, j: (j, 0, 0)),
                  pl.BlockSpec((FF_TILE, D), lambda i, j: (j, 0)),
                  pl.BlockSpec((tm, D), lambda i, j: (i, 0)), pl.BlockSpec((1, D), lambda i, j: (0, 0))],
        out_specs=[pl.BlockSpec((tm, FF_TILE), lambda i, j: (i, j)), pl.BlockSpec((tm, D), lambda i, j: (i, 0)),
                   pl.BlockSpec((8, 128), lambda i, j: (0, 0)), pl.BlockSpec((8, D), lambda i, j: (0, 0))],
        out_shape=[jax.ShapeDtypeStruct((rows, D_FF), ACT_DTYPE), jax.ShapeDtypeStruct((rows, D), F32),
                   jax.ShapeDtypeStruct((8, 128), F32), jax.ShapeDtypeStruct((8, D), F32)],
        scratch_shapes=[pltpu.VMEM((tm, D), F32)],
        compiler_params=_cp(("arbitrary", "arbitrary"), 48),
    )(f, h1, w1g, w2, tgt, fnw)


def _ffn_bwd_act(dh2, a, w1g, w2, h1, nfw, tm):
    rows = dh2.shape[0]
    nj = N_DEV

    def body(dh2_ref, a_ref, w1_ref, w2_ref, h1_ref, nw_ref, da_ref, dh1_ref, gnf_ref, acc):
        i, j = pl.program_id(0), pl.program_id(1)

        @pl.when((i == 0) & (j == 0))
        def _():
            gnf_ref[...] = jnp.zeros_like(gnf_ref)

        @pl.when(j == 0)
        def _():
            acc[...] = jnp.zeros_like(acc)

        dz = _mm_nt(dh2_ref[...], w2_ref[...])
        da = dz * (2.0 * jnp.maximum(a_ref[...].astype(F32), 0.0))
        da_ref[...] = da.astype(ACT_DTYPE)
        acc[...] += _mm_nt(da, w1_ref[...])

        @pl.when(j == nj - 1)
        def _():
            h1 = h1_ref[...]
            rstd = lax.rsqrt(jnp.mean(h1 * h1, axis=-1, keepdims=True) + EPS)
            hn = h1 * rstd
            df = acc[...]
            gnf_ref[...] += jnp.broadcast_to(jnp.sum(df * hn, axis=0, keepdims=True), gnf_ref.shape)
            dfn = df * nw_ref[...]
            dh1_ref[...] = dh2_ref[...] + rstd * (dfn - hn * jnp.mean(dfn * hn, axis=-1, keepdims=True))

    return pl.pallas_call(
        body, name="ffn_bwd_act", grid=(rows // tm, nj),
        in_specs=[pl.BlockSpec((tm, D), lambda i, j: (i, 0)), pl.BlockSpec((tm, FF_TILE), lambda i, j: (i, j)),
                  pl.BlockSpec((None, D, FF_TILE), lambda i, j: (j, 0, 0)),
                  pl.BlockSpec((FF_TILE, D), lambda i, j: (j, 0)),
                  pl.BlockSpec((tm, D), lambda i, j: (i, 0)), pl.BlockSpec((1, D), lambda i, j: (0, 0))],
        out_specs=[pl.BlockSpec((tm, FF_TILE), lambda i, j: (i, j)), pl.BlockSpec((tm, D), lambda i, j: (i, 0)),
                   pl.BlockSpec((8, D), lambda i, j: (0, 0))],
        out_shape=[jax.ShapeDtypeStruct((rows, D_FF), ACT_DTYPE), jax.ShapeDtypeStruct((rows, D), F32),
                   jax.ShapeDtypeStruct((8, D), F32)],
        scratch_shapes=[pltpu.VMEM((tm, D), F32)],
        compiler_params=_cp(("arbitrary", "arbitrary"), 48),
    )(dh2, a, w1g, w2, h1, nfw)


def _ffn_bwd_weights(f, a, da, dh2, tm):
    rows = f.shape[0]

    def body(f_ref, a_ref, da_ref, dh2_ref, dw1_ref, dw2_ref):
        i = pl.program_id(1)

        @pl.when(i == 0)
        def _():
            dw1_ref[...] = jnp.zeros_like(dw1_ref)
            dw2_ref[...] = jnp.zeros_like(dw2_ref)

        z = jnp.square(jnp.maximum(a_ref[...].astype(F32), 0.0))
        dw1_ref[...] += _mm_tn(f_ref[...], da_ref[...])
        dw2_ref[...] += _mm_tn(z, dh2_ref[...])

    return pl.pallas_call(
        body, name="ffn_bwd_weights", grid=(N_DEV, rows // tm),
        in_specs=[pl.BlockSpec((tm, D), lambda j, i: (i, 0)), pl.BlockSpec((tm, FF_TILE), lambda j, i: (i, j)),
                  pl.BlockSpec((tm, FF_TILE), lambda j, i: (i, j)), pl.BlockSpec((tm, D), lambda j, i: (i, 0))],
        out_specs=[pl.BlockSpec((None, None, D, FF_TILE), lambda j, i: (j % 2, j // 2, 0, 0)),
                   pl.BlockSpec((None, None, FF_TILE, D), lambda j, i: (j % 2, j // 2, 0, 0))],
        out_shape=[jax.ShapeDtypeStruct((2, 4, D, FF_TILE), F32), jax.ShapeDtypeStruct((2, 4, FF_TILE, D), F32)],
        compiler_params=_cp(("arbitrary", "arbitrary"), 48),
    )(f, a, da, dh2)


def _out_proj_bwd(dh1, og, osw, wout, tm, partials):
    rows = dh1.shape[0]
    steps = rows // tm
    ns = len(partials)

    def body(dh1_ref, og_ref, os_ref, w_ref, *rest):
        part_refs, rest = rest[:ns], rest[ns:]
        dog_ref, dos_ref, dw_ref = rest[:3]
        land_refs, (send_sems, recv_sems) = rest[3:3 + ns], rest[3 + ns:]
        i = pl.program_id(0)
        start, finish = _sibling_schedule(part_refs, land_refs, send_sems, recv_sems)

        @pl.when(i == 0)
        def _():
            dw_ref[...] = jnp.zeros_like(dw_ref)
            start()

        pl.when(i == steps - 1)(finish)

        dh1 = dh1_ref[...].astype(MXU_DTYPE)
        dog_ref[...] = _mm_nt(dh1, w_ref[0:512, :])
        dos_ref[...] = _mm_nt(dh1, w_ref[512:1024, :])
        for half, ref in enumerate((og_ref, os_ref)):
            dw = _mm_tn(ref[...], dh1)
            for blk in range(4):
                shard = half * 4 + blk
                dw_ref[shard % 2, shard // 2] += dw[blk * 128:(blk + 1) * 128, :]

    row = lambda w: pl.BlockSpec((tm, w), lambda i: (i, 0))
    outs = pl.pallas_call(
        body, name="out_proj_bwd", grid=(steps,),
        in_specs=[row(D), row(512), row(512), pl.BlockSpec((D, D), lambda i: (0, 0))] + [ANY] * ns,
        out_specs=[row(512), row(512), pl.BlockSpec((2, 4, 128, D), lambda i: (0, 0, 0, 0))] + [ANY] * ns,
        out_shape=[jax.ShapeDtypeStruct((rows, 512), F32), jax.ShapeDtypeStruct((rows, 512), F32),
                   jax.ShapeDtypeStruct((2, 4, 128, D), F32)] + _sibling_shapes(partials),
        scratch_shapes=_sibling_sems(ns),
        compiler_params=_cp(("arbitrary",), 48),
    )(dh1, og, osw, wout, *partials)
    return outs[0], outs[1], outs[2], outs[3:]


def _swa_bwd(qr, kr, vr, osw, dos, sinks, chip_sums):
    rows = qr.shape[0]
    nblk = rows // SWA_BLOCK
    ns = len(chip_sums)

    def body(q_ref, k0, kp, kc, v0, vp, vc, o_ref, do_ref, sink_ref, *rest):
        sum_refs, rest = rest[:ns], rest[ns:]
        dq_ref, dk_ref, dv_ref, dsink_ref = rest[:4]
        land_refs, (send_sems, recv_sems) = rest[4:4 + ns], rest[4 + ns:]
        n = pl.program_id(0)
        start, finish = _chips_schedule(sum_refs, land_refs, send_sems, recv_sems)

        @pl.when(n == 0)
        def _():
            dk_ref[...] = jnp.zeros_like(dk_ref)
            dv_ref[...] = jnp.zeros_like(dv_ref)
            dsink_ref[...] = jnp.zeros_like(dsink_ref)
            start()

        pl.when(n == nblk - 1)(finish)

        q = q_ref[...]
        kall = jnp.concatenate([k0[...], kp[...], kc[...]], axis=0)
        vall = jnp.concatenate([v0[...], vp[...], vc[...]], axis=0)
        mask = _swa_mask(n)
        do_all = do_ref[...]
        o_all = o_ref[...].astype(F32)
        dq, dk, dv = [], [], []
        for kvh in range(SWA_KV):
            kv = slice(kvh * SWA_HD, (kvh + 1) * SWA_HD)
            q4, do4, o4 = _stack_heads(q, kvh), _stack_heads(do_all, kvh), _stack_heads(o_all, kvh)
            sink4 = _stack_sinks(sink_ref, kvh)
            s = jnp.where(mask, _mm_nt(q4, kall[:, kv]), NEG)
            m = jnp.maximum(jnp.max(s, axis=-1, keepdims=True), sink4)
            e = jnp.exp(s - m)
            inv = 1.0 / (jnp.sum(e, axis=-1, keepdims=True) + jnp.exp(sink4 - m))
            p = e * inv
            delta = jnp.sum(do4 * o4, axis=-1, keepdims=True)
            ds = p * (_mm_nt(do4, vall[:, kv]) - delta)
            dq4 = _mm(ds, kall[:, kv])
            dq += [dq4[g * SWA_BLOCK:(g + 1) * SWA_BLOCK] for g in range(SWA_GROUP)]
            dk.append(_mm_tn(ds, q4))
            dv.append(_mm_tn(p, do4))
            sink_term = jnp.exp(sink4 - m) * inv * delta
            for g in range(SWA_GROUP):
                head = kvh * SWA_GROUP + g
                dsink = -jnp.sum(sink_term[g * SWA_BLOCK:(g + 1) * SWA_BLOCK], axis=0, keepdims=True)
                dsink_ref[head:head + 1, :] += jnp.broadcast_to(dsink, (1, 128))
        dq_ref[...] = jnp.concatenate(dq, axis=1)
        dk_all = jnp.concatenate(dk, axis=1)
        dv_all = jnp.concatenate(dv, axis=1)
        prev0 = pl.multiple_of(jnp.maximum(n - 1, 0) * SWA_BLOCK, SWA_BLOCK)
        cur0 = pl.multiple_of(n * SWA_BLOCK, SWA_BLOCK)
        for ref, val in ((dk_ref, dk_all), (dv_ref, dv_all)):
            ref[0:SWA_BLOCK, :] += val[0:SWA_BLOCK]
            ref[pl.ds(prev0, SWA_BLOCK), :] += val[SWA_BLOCK:2 * SWA_BLOCK]
            ref[pl.ds(cur0, SWA_BLOCK), :] += val[2 * SWA_BLOCK:]

    blk, first, prev = _swa_specs()
    whole = pl.BlockSpec((rows, 128), lambda n: (0, 0))
    outs = pl.pallas_call(
        body, name="swa_bwd", grid=(nblk,),
        in_specs=[blk(512), first, prev, blk(128), first, prev, blk(128), blk(512), blk(512),
                  pl.BlockSpec(memory_space=pltpu.SMEM)] + [ANY] * ns,
        out_specs=[blk(512), whole, whole, pl.BlockSpec((8, 128), lambda n: (0, 0))] + [ANY] * ns,
        out_shape=[jax.ShapeDtypeStruct((rows, 512), F32), jax.ShapeDtypeStruct((rows, 128), F32),
                   jax.ShapeDtypeStruct((rows, 128), F32), jax.ShapeDtypeStruct((8, 128), F32)] + _chips_shapes(chip_sums),
        scratch_shapes=_chips_sems(ns),
        compiler_params=_cp(("arbitrary",), 48),
    )(qr, kr, kr, kr, vr, vr, vr, osw, dos, sinks, *chip_sums)
    return outs[0], outs[1], outs[2], outs[3], outs[4:]


def _gla_bwd(proj, oraw, states, dog, wg_p, bg, gnw):
    rows = proj.shape[0]
    nc = rows // GLA_CHUNK

    def body(q_ref, k_ref, v_ref, r_ref, lr_ref, oraw_ref, st_ref, dog_ref, wg_ref, bg_ref, gnw_ref,
             dq_ref, dk_ref, dv_ref, dr_ref, dlr_ref, dwg_ref, dbg_ref, dgnw_ref, dstate):
        t = pl.program_id(0)
        c = nc - 1 - t

        @pl.when(t == 0)
        def _():
            dstate[...] = jnp.zeros_like(dstate)
            dwg_ref[...] = jnp.zeros_like(dwg_ref)
            dbg_ref[...] = jnp.zeros_like(dbg_ref)
            dgnw_ref[...] = jnp.zeros_like(dgnw_ref)

        lr, wg = lr_ref[...], wg_ref[...]
        zg, live, tril, b = _gla_gates(lr, wg, bg_ref[...], c)
        eb, enb = jnp.exp(b), jnp.exp(-b)
        scale = GLA_DK ** -0.5
        gq = q_ref[...] * scale * eb
        gk = k_ref[...] * enb
        ebl = eb[GLA_CHUNK - 1:GLA_CHUNK, :]
        v, r, oraw_v, dog_v = v_ref[...], r_ref[...], oraw_ref[...], dog_ref[...]
        gnw_v = gnw_ref[...]
        is_last = lax.broadcasted_iota(jnp.int32, (GLA_CHUNK, 1), 0) == GLA_CHUNK - 1
        dq, dk, dv, dr, db = [], [], [], [], []
        dgnw = jnp.zeros((1, GLA_DV), F32)
        for h in range(GLA_HEADS):
            s64 = slice(h * GLA_DK, (h + 1) * GLA_DK)
            s128 = slice(h * GLA_DV, (h + 1) * GLA_DV)
            qh, kh, vh, eblh = gq[:, s64], gk[:, s64], v[:, s128], ebl[:, s64]
            klh = kh * eblh
            st = st_ref[0, h]
            o, rh, dout = oraw_v[:, s128], r[:, s128], dog_v[:, s128]
            rstd = lax.rsqrt(jnp.mean(o * o, axis=-1, keepdims=True) + EPS)
            on = o * rstd
            sg = _sigmoid(rh)
            dr.append(dout * (on * gnw_v) * (sg * (1.0 + rh * (1.0 - sg))))
            dy = dout * (rh * sg)
            dgnw = dgnw + jnp.sum(dy * on, axis=0, keepdims=True)
            don = dy * gnw_v
            do = rstd * (don - on * jnp.mean(don * on, axis=-1, keepdims=True))
            a = jnp.where(tril, _mm_nt(qh, kh), 0.0)
            da = jnp.where(tril, _mm_nt(do, vh), 0.0)
            dsp = dstate[h]
            dkl = _mm(vh, dsp)
            dv.append(_mm_tn(a, do) + _mm_nt(klh, dsp))
            debl = jnp.sum(dsp * st, axis=0, keepdims=True)
            dgq = _mm(da, kh) + _mm(do, st)
            dgk = _mm_tn(da, qh)
            dstate[h] = dsp * eblh + _mm_tn(do, qh)
            dq.append(dgq * (scale * eb[:, s64]))
            dk.append((dgk + dkl * eblh) * enb[:, s64])
            last = debl * eblh + jnp.sum(dkl * klh, axis=0, keepdims=True)
            db.append(dgq * qh - dgk * kh - dkl * klh + jnp.where(is_last, last, 0.0))
        dq_ref[...] = jnp.concatenate(dq, axis=1).astype(ACT_DTYPE)
        dk_ref[...] = jnp.concatenate(dk, axis=1).astype(ACT_DTYPE)
        dv_ref[...] = jnp.concatenate(dv, axis=1).astype(ACT_DTYPE)
        dr_ref[...] = jnp.concatenate(dr, axis=1).astype(ACT_DTYPE)
        triu = jnp.logical_not(tril) | (lax.broadcasted_iota(jnp.int32, (GLA_CHUNK, GLA_CHUNK), 0)
                                        == lax.broadcasted_iota(jnp.int32, (GLA_CHUNK, GLA_CHUNK), 1))
        dg = jnp.dot(triu.astype(F32), jnp.concatenate(db, axis=1), precision=HIGHEST, preferred_element_type=F32)
        dzg = jnp.where(live, dg * _sigmoid(-zg) * (1.0 / GLA_TAU), 0.0)
        dlr_ref[...] = _mm_nt(dzg, wg).astype(ACT_DTYPE)
        dwg_ref[...] += _mm_tn(lr, dzg)
        dbg_ref[...] += jnp.broadcast_to(jnp.sum(dzg, axis=0, keepdims=True), dbg_ref.shape)
        dgnw_ref[...] += jnp.broadcast_to(dgnw, dgnw_ref.shape)

    nb = lambda w, col: pl.BlockSpec((GLA_CHUNK, w), lambda t: (nc - 1 - t, col // w))
    const = lambda shape: pl.BlockSpec(shape, lambda t: (0,) * len(shape))
    return pl.pallas_call(
        body, name="gla_bwd", grid=(nc,),
        in_specs=[nb(256, C_GQ), nb(256, C_GK), nb(512, C_GV), nb(512, C_GR), nb(128, C_LR), nb(512, 0),
                  pl.BlockSpec((1, GLA_HEADS, GLA_DV, GLA_DK), lambda t: (nc - 1 - t, 0, 0, 0)), nb(512, 0),
                  const((128, 256)), const((1, 256)), const((1, 128))],
        out_specs=[nb(256, 0), nb(256, 0), nb(512, 0), nb(512, 0), nb(128, 0),
                   const((128, 256)), const((8, 256)), const((8, 128))],
        out_shape=[jax.ShapeDtypeStruct((rows, 256), ACT_DTYPE), jax.ShapeDtypeStruct((rows, 256), ACT_DTYPE),
                   jax.ShapeDtypeStruct((rows, 512), ACT_DTYPE), jax.ShapeDtypeStruct((rows, 512), ACT_DTYPE),
                   jax.ShapeDtypeStruct((rows, 128), ACT_DTYPE), jax.ShapeDtypeStruct((128, 256), F32),
                   jax.ShapeDtypeStruct((8, 256), F32), jax.ShapeDtypeStruct((8, 128), F32)],
        scratch_shapes=[pltpu.VMEM((GLA_HEADS, GLA_DV, GLA_DK), F32)],
        compiler_params=_cp(("arbitrary",)),
    )(proj, proj, proj, proj, proj, oraw, states, dog, wg_p, bg, gnw)


def _in_proj_bwd(h0, dh1, nw, win_p, dgv, dgr, dsq, dgq, dgk, dsk, dsv, dlr, tabs, tm):
    rows = h0.shape[0]

    def body(h_ref, dh1_ref, nw_ref, w_ref, dgv_ref, dgr_ref, dsq_ref, dgq_ref, dgk_ref, dsk_ref, dsv_ref, dlr_ref,
             c_ref, sa_ref, sb_ref, dh0_ref, dw_ref, gnm_ref):
        i = pl.program_id(0)

        @pl.when(i == 0)
        def _():
            dw_ref[...] = jnp.zeros_like(dw_ref)
            gnm_ref[...] = jnp.zeros_like(gnm_ref)

        cos, sa, sb = c_ref[...], sa_ref[...], sb_ref[...]
        dsq_v = (_unrope(dsq_ref[...], cos, sa, sb) * (SWA_HD ** -0.5)).astype(MXU_DTYPE)
        dsk_v = _unrope(dsk_ref[...], cos, sa, sb).astype(MXU_DTYPE)
        dproj = jnp.concatenate(
            [dgv_ref[...].astype(MXU_DTYPE), dgr_ref[...].astype(MXU_DTYPE), dsq_v, dgq_ref[...].astype(MXU_DTYPE),
             dgk_ref[...].astype(MXU_DTYPE), dsk_v, dsv_ref[...].astype(MXU_DTYPE), dlr_ref[...].astype(MXU_DTYPE)],
            axis=1)
        h = h_ref[...]
        rstd = lax.rsqrt(jnp.mean(h * h, axis=-1, keepdims=True) + EPS)
        hn = h * rstd
        nw_v = nw_ref[...]
        u = (hn * nw_v).astype(MXU_DTYPE)
        du = _mm_nt(dproj, w_ref[...])
        dw_ref[...] += _mm_tn(u, dproj)
        gnm_ref[...] += jnp.broadcast_to(jnp.sum(du * hn, axis=0, keepdims=True), gnm_ref.shape)
        dun = du * nw_v
        dh0_ref[...] = dh1_ref[...] + rstd * (dun - hn * jnp.mean(dun * hn, axis=-1, keepdims=True))

    row = lambda w: pl.BlockSpec((tm, w), lambda i: (i, 0))
    return pl.pallas_call(
        body, name="in_proj_bwd", grid=(rows // tm,),
        in_specs=[row(D), row(D), pl.BlockSpec((1, D), lambda i: (0, 0)), pl.BlockSpec((D, DINP), lambda i: (0, 0)),
                  row(512), row(512), row(512), row(256), row(256), row(128), row(128), row(128),
                  row(128), row(128), row(128)],
        out_specs=[row(D), pl.BlockSpec((D, DINP), lambda i: (0, 0)), pl.BlockSpec((8, D), lambda i: (0, 0))],
        out_shape=[jax.ShapeDtypeStruct((rows, D), F32), jax.ShapeDtypeStruct((D, DINP), F32),
                   jax.ShapeDtypeStruct((8, D), F32)],
        compiler_params=_cp(("arbitrary",), 56),
    )(h0, dh1, nw, win_p, dgv, dgr, dsq, dgq, dgk, dsk, dsv, dlr, *tabs)


def _adamw(w, g, m, v):
    m = ADAM_B1 * m + (1.0 - ADAM_B1) * g
    v = ADAM_B2 * v + (1.0 - ADAM_B2) * jnp.square(g)
    m_hat = m / (1.0 - ADAM_B1 ** ADAM_STEP)
    v_hat = v / (1.0 - ADAM_B2 ** ADAM_STEP)
    delta = -ADAM_LR * (m_hat / (jnp.sqrt(v_hat) + ADAM_EPS) + ADAM_WD * w)
    return delta, m, v


def _adamw_shard(where, parts, own, w, m, v, name):
    r, cdim = w.shape
    tr = 128 if r % 128 == 0 else r

    def body(where_ref, p_ref, own_ref, w_ref, m_ref, v_ref, g_ref, d_ref, nm_ref, nv_ref):
        g = ((p_ref[0] + p_ref[1]) + p_ref[2]) + own_ref[...]
        g_ref[...] = g
        d_ref[...], nm_ref[...], nv_ref[...] = _adamw(w_ref[...], g, m_ref[...], v_ref[...])

    spec = pl.BlockSpec((tr, cdim), lambda i, s: (i, 0))
    shape = jax.ShapeDtypeStruct((r, cdim), F32)
    return pl.pallas_call(
        body, name=name,
        grid_spec=pltpu.PrefetchScalarGridSpec(
            num_scalar_prefetch=1, grid=(r // tr,),
            in_specs=[pl.BlockSpec((3, tr, cdim), lambda i, s: (0, i, 0)),
                      pl.BlockSpec((None, tr, cdim), lambda i, s: (s[1], i, 0)), spec, spec, spec],
            out_specs=[spec] * 4),
        out_shape=[shape] * 4,
        compiler_params=_cp(("arbitrary",)),
    )(where, parts, own, w, m, v)


def _adamw_small(w, g, m, v):
    def body(w_ref, g_ref, m_ref, v_ref, d_ref, nm_ref, nv_ref):
        d_ref[...], nm_ref[...], nv_ref[...] = _adamw(w_ref[...], g_ref[...], m_ref[...], v_ref[...])

    vm = pl.BlockSpec(memory_space=pltpu.VMEM)
    shape = jax.ShapeDtypeStruct(w.shape, F32)
    return pl.pallas_call(body, name="adamw_small", in_specs=[vm] * 4, out_specs=[vm] * 3,
                          out_shape=[shape] * 3)(w, g, m, v)


def _add_own_half(where, full, theirs, name):
    _, _, r, cdim = full.shape
    tr = 128 if r % 128 == 0 else r

    def body(where_ref, a_ref, b_ref, o_ref):
        o_ref[...] = a_ref[...] + b_ref[...]

    spec = pl.BlockSpec((4, tr, cdim), lambda i, s: (0, i, 0))
    return pl.pallas_call(
        body, name=name,
        grid_spec=pltpu.PrefetchScalarGridSpec(
            num_scalar_prefetch=1, grid=(r // tr,),
            in_specs=[pl.BlockSpec((None, 4, tr, cdim), lambda i, s: (s[0], 0, i, 0)), spec], out_specs=spec),
        out_shape=jax.ShapeDtypeStruct(theirs.shape, F32), compiler_params=_cp(("arbitrary",)))(where, full, theirs)


def _to_rows128(a):
    return a.reshape(-1, 128)


def _pad_rows128(vec):
    flat = vec.reshape(-1)
    n = -(-flat.shape[0] // 128)
    return jnp.pad(flat, (0, n * 128 - flat.shape[0])).reshape(n, 128)


def kernel(x, meta_tokens, norm_mix_w, w_in, w_gate_up, b_gate, gla_norm_w, sinks, w_out, norm_ff_w, w_ff1, w_ff2, final_norm_w, loss_target, m_meta_tokens, m_norm_mix_w, m_w_in, m_w_gate_up, m_b_gate, m_gla_norm_w, m_sinks, m_w_out, m_norm_ff_w, m_w_ff1, m_w_ff2, m_final_norm_w, v_meta_tokens, v_norm_mix_w, v_w_in, v_w_gate_up, v_b_gate, v_gla_norm_w, v_sinks, v_w_out, v_norm_ff_w, v_w_ff1, v_w_ff2, v_final_norm_w):
    seq = x.shape[1]
    rows = LEAD + seq
    tm = _row_tile(rows)
    tm_small = tm // 2 if tm == 640 else tm
    dev = 4 * lax.axis_index("x") + 2 * lax.axis_index("y") + lax.axis_index("c")

    small_shard = jnp.concatenate([meta_tokens, w_gate_up[0], jnp.zeros((N_META, 96), F32)], axis=1)
    g_in, g_small = _all_gather([w_in[0].astype(WIRE_DTYPE), small_shard])
    later_shards = [w_out[0].astype(WIRE_DTYPE), w_ff1[0].astype(WIRE_DTYPE), w_ff2[0].astype(WIRE_DTYPE)]
    win_full = jnp.transpose(g_in, (1, 0, 2)).reshape(D, DIN)
    cols = lambda r: win_full[:, r[0]:r[1]]
    win_p = jnp.concatenate([cols(O_GV), cols(O_GR), cols(O_SQ), cols(O_GQ), cols(O_GK), cols(O_SK), cols(O_SV),
                             cols(O_LR), jnp.zeros((D, 128 - GLA_RANK), WIRE_DTYPE)], axis=1)
    meta_full = jnp.transpose(g_small[:, :, 0:128], (1, 0, 2)).reshape(N_META, D)
    wg_full = jnp.transpose(g_small[:, :, 128:160], (1, 0, 2)).reshape(GLA_RANK, GLA_HEADS * GLA_DK)
    wg_p = jnp.concatenate([wg_full, jnp.zeros((128 - GLA_RANK, 256), F32)], axis=0)

    h0 = jnp.concatenate([jnp.zeros((META0, D), F32), meta_full, x[0]], axis=0)
    tgt = jnp.concatenate([jnp.zeros((LEAD, D), F32), loss_target[0]], axis=0)
    tabs = _rope_tables(rows)
    proj = _in_proj(h0, norm_mix_w, win_p, tm)
    oraw, og, states, (g_out, g_w1, g_w2) = _gla_fwd(proj, wg_p, b_gate, gla_norm_w, later_shards)
    wout_full = g_out.reshape(D, D)
    w2_full = g_w2.reshape(D_FF, D)
    qr, kr, vr = _swa_prep(proj, tabs, tm)
    osw = _swa_fwd(qr, kr, vr, sinks)
    h1, f = _out_proj(h0, og, osw, wout_full, norm_ff_w, tm)
    a, dh2, loss_p, gfn_p = _ffn_fwd(f, h1, g_w1, w2_full, tgt, final_norm_w.reshape(1, D), tm)

    da, dh1, gnf_p = _ffn_bwd_act(dh2, a, g_w1, w2_full, h1, norm_ff_w, tm)
    dw1, dw2 = _ffn_bwd_weights(f, a, da, dh2, tm)
    where = jnp.stack([lax.axis_index("c"), 2 * lax.axis_index("x") + lax.axis_index("y")]).astype(jnp.int32)
    dog, dos, dwout, theirs_ffn = _out_proj_bwd(dh1, og, osw, wout_full, tm, [dw1, dw2])
    sums_ffn = [_add_own_half(where, p, q, "reduce_pair_%d" % (2 + k))
                for k, (p, q) in enumerate(zip([dw1, dw2], theirs_ffn))]
    dsq, dsk, dsv, dsink_p, parts_ffn = _swa_bwd(qr, kr, vr, osw, dos, sinks, sums_ffn)
    dgq, dgk, dgv, dgr, dlr, dwg_p, dbg_p, dgnw_p = _gla_bwd(proj, oraw, states, dog, wg_p, b_gate, gla_norm_w)
    dh0, dwin_p, gnm_p = _in_proj_bwd(h0, dh1, norm_mix_w, win_p, dgv, dgr, dsq, dgq, dgk, dsk, dsv, dlr, tabs, tm_small)
    grad_x = dh0[LEAD:][None]

    pcols = lambda c0, r: dwin_p[:, c0:c0 + (r[1] - r[0])]
    dwin = jnp.concatenate([pcols(C_GQ, O_GQ), pcols(C_GK, O_GK), pcols(C_GV, O_GV), pcols(C_GR, O_GR),
                            pcols(C_LR, O_LR), pcols(C_SQ, O_SQ), pcols(C_SK, O_SK), pcols(C_SV, O_SV)], axis=1)
    dwin = jnp.transpose(dwin.reshape(D, 4, 2, DIN // N_DEV), (2, 1, 0, 3))
    theirs_tail = _rs_sibling([dwin, dwout])
    sums_tail = [_add_own_half(where, p, q, "reduce_pair_%d" % k)
                 for k, (p, q) in enumerate(zip([dwin, dwout], theirs_tail))]
    chip_sums = sums_tail + sums_ffn
    parts = list(_rs_chips(sums_tail)) + list(parts_ffn)

    small = [dh0[META0:LEAD], dwg_p[0:GLA_RANK], gnm_p[0:1], dbg_p[0:1], dgnw_p[0:1], dsink_p[:, 0], gnf_p[0:1],
             gfn_p[0:1], loss_p[0:1, 0:1]]
    sizes = [-(-s.size // 128) for s in small]
    pack = jnp.concatenate([_pad_rows128(s) for s in small], axis=0)
    pad_rows = -pack.shape[0] % 8
    pack = jnp.pad(pack, ((0, pad_rows), (0, 0)))
    total = _all_reduce_small(pack)
    offs = [sum(sizes[:k]) for k in range(len(sizes))]
    take = lambda k, shape: total[offs[k]:offs[k] + sizes[k]].reshape(-1)[:small[k].size].reshape(shape)
    g_meta_full = take(0, (N_META, D))
    g_wg_full = take(1, (GLA_RANK, 256))
    g_meta = lax.dynamic_slice_in_dim(g_meta_full, dev * 128, 128, axis=1)
    g_wg = lax.dynamic_slice_in_dim(g_wg_full, dev * 32, 32, axis=1)[None]
    g_norm_mix, g_b_gate, g_gla_norm = take(2, (1, D)), take(3, (1, 256)), take(4, (1, 128))
    g_sinks, g_norm_ff, g_final_norm = take(5, (1, 8)), take(6, (1, D)), take(7, (D,))
    loss = take(8, ())

    g_win, d_win, nm_win, nv_win = _adamw_shard(where, parts[0], chip_sums[0], w_in[0], m_w_in[0], v_w_in[0], "adamw_w_in")
    g_wout, d_wout, nm_wout, nv_wout = _adamw_shard(where, parts[1], chip_sums[1], w_out[0], m_w_out[0], v_w_out[0], "adamw_w_out")
    g_w1s, d_w1, nm_w1, nv_w1 = _adamw_shard(where, parts[2], chip_sums[2], w_ff1[0], m_w_ff1[0], v_w_ff1[0], "adamw_w_ff1")
    g_w2s, d_w2, nm_w2, nv_w2 = _adamw_shard(where, parts[3], chip_sums[3], w_ff2[0], m_w_ff2[0], v_w_ff2[0], "adamw_w_ff2")

    names = ["meta", "wg", "norm_mix", "b_gate", "gla_norm", "sinks", "norm_ff", "final_norm"]
    ws = [meta_tokens, w_gate_up, norm_mix_w, b_gate, gla_norm_w, sinks, norm_ff_w, final_norm_w]
    gs = [g_meta, g_wg, g_norm_mix, g_b_gate, g_gla_norm, g_sinks, g_norm_ff, g_final_norm]
    ms = [m_meta_tokens, m_w_gate_up, m_norm_mix_w, m_b_gate, m_gla_norm_w, m_sinks, m_norm_ff_w, m_final_norm_w]
    vs = [v_meta_tokens, v_w_gate_up, v_norm_mix_w, v_b_gate, v_gla_norm_w, v_sinks, v_norm_ff_w, v_final_norm_w]
    ssz = [-(-w.size // 128) for w in ws]
    packed = []
    for group in (ws, gs, ms, vs):
        p = jnp.concatenate([_pad_rows128(t) for t in group], axis=0)
        packed.append(jnp.pad(p, ((0, -p.shape[0] % 8), (0, 0))))
    d_s, nm_s, nv_s = _adamw_small(*packed)
    soffs = [sum(ssz[:k]) for k in range(len(ssz))]
    unpack = lambda t, k: t[soffs[k]:soffs[k] + ssz[k]].reshape(-1)[:ws[k].size].reshape(ws[k].shape)
    d_small = {n: unpack(d_s, k) for k, n in enumerate(names)}
    nm_small = {n: unpack(nm_s, k) for k, n in enumerate(names)}
    nv_small = {n: unpack(nv_s, k) for k, n in enumerate(names)}
    g_small_d = dict(zip(names, gs))

    def ordered(big, small_d):
        win_v, wout_v, w1_v, w2_v = big
        return (small_d["meta"], small_d["norm_mix"], win_v[None], small_d["wg"], small_d["b_gate"],
                small_d["gla_norm"], small_d["sinks"], wout_v[None], small_d["norm_ff"], w1_v[None], w2_v[None],
                small_d["final_norm"])

    return (loss, grad_x,
            *ordered((g_win, g_wout, g_w1s, g_w2s), g_small_d),
            *ordered((d_win, d_wout, d_w1, d_w2), d_small),
            *ordered((nm_win, nm_wout, nm_w1, nm_w2), nm_small),
            *ordered((nv_win, nv_wout, nv_w1, nv_w2), nv_small))
```

```python
import functools

import jax
import jax.numpy as jnp
from jax import lax
from jax.experimental import pallas as pl
from jax.experimental.pallas import tpu as pltpu

F32 = jnp.float32
MXU_DTYPE = jnp.bfloat16
ACT_DTYPE = jnp.bfloat16
WIRE_DTYPE = jnp.bfloat16

D = 1024
N_META = 16
LEAD = 128
META0 = LEAD - N_META
EPS = 1e-5
GLA_HEADS, GLA_DK, GLA_DV, GLA_RANK, GLA_CHUNK = 4, 64, 128, 16, 64
GLA_TAU = 16.0
SWA_HEADS, SWA_KV, SWA_GROUP, SWA_HD, SWA_BLOCK = 8, 2, 4, 64, 128
ROPE_DIM, ROPE_THETA = 16, 500000.0
D_FF = 4096
N_DEV = 8
FF_TILE = D_FF // N_DEV
FF_WIDE = 1024
NEG = -1e30

C_GV, C_GR, C_SQ, C_GQ, C_GK, C_SK, C_SV, C_LR = 0, 512, 1024, 1536, 1792, 2048, 2176, 2304
DINP = 2432
DIN = 2320
O_GQ, O_GK, O_GV, O_GR, O_LR, O_SQ, O_SK, O_SV = (0, 256), (256, 512), (512, 1024), (1024, 1536), (1536, 1552), (1552, 2064), (2064, 2192), (2192, 2320)

ADAM_LR, ADAM_B1, ADAM_B2, ADAM_EPS, ADAM_WD, ADAM_STEP = 0.001, 0.9, 0.999, 1e-08, 0.01, 10

MESH = pl.DeviceIdType.MESH
ANY = pl.BlockSpec(memory_space=pl.ANY)
HIGHEST = lax.Precision.HIGHEST


def _cp(sem=None, vmem_mb=None):
    kw = {}
    if sem is not None:
        kw["dimension_semantics"] = sem
    if vmem_mb is not None:
        kw["vmem_limit_bytes"] = vmem_mb << 20
    return pltpu.CompilerParams(**kw)


def _mm(a, b):
    return jnp.dot(a.astype(MXU_DTYPE), b.astype(MXU_DTYPE), preferred_element_type=F32)


def _mm_nt(a, b):
    return lax.dot_general(a.astype(MXU_DTYPE), b.astype(MXU_DTYPE), (((1,), (1,)), ((), ())),
                           preferred_element_type=F32)


def _mm_tn(a, b):
    return lax.dot_general(a.astype(MXU_DTYPE), b.astype(MXU_DTYPE), (((0,), (0,)), ((), ())),
                           preferred_element_type=F32)


def _logsigmoid(z):
    return jnp.minimum(z, 0.0) - jnp.log(1.0 + jnp.exp(-jnp.abs(z)))


def _sigmoid(z):
    return 1.0 / (1.0 + jnp.exp(-z))


def _row_tile(rows):
    return 640 if rows % 640 == 0 else 128


def _mesh_pos():
    return lax.axis_index("x"), lax.axis_index("y"), lax.axis_index("c")


def _all_gather(shards):
    n = len(shards)

    def body(*refs):
        start, forward, finish = _gather_schedule(refs[:n], refs[n:2 * n], *refs[2 * n:])
        start()
        for j in range(3):
            forward(j)
        finish()

    gathered = pl.pallas_call(
        body, name="all_gather_weights",
        out_shape=_gathered_shapes(shards), in_specs=[ANY] * n, out_specs=[ANY] * n,
        scratch_shapes=_gather_sems(n),
    )(*shards)
    return _with_own_block(gathered, shards)


def _gathered_shapes(shards):
    return [jax.ShapeDtypeStruct((N_DEV,) + s.shape, s.dtype) for s in shards]


def _gather_sems(n):
    return [pltpu.SemaphoreType.DMA((7 * n,)), pltpu.SemaphoreType.DMA((7 * n,))]


def _with_own_block(gathered, shards):
    dev = 4 * lax.axis_index("x") + 2 * lax.axis_index("y") + lax.axis_index("c")
    return [lax.dynamic_update_index_in_dim(g, s, dev, 0) for g, s in zip(gathered, shards)]


def _gather_schedule(ins, outs, send_sems, recv_sems):
    n = len(ins)
    x, y, c = _mesh_pos()
    me, sibling = (x, y, c), (x, y, 1 - c)
    chips = [(1 - x, y), (x, 1 - y), (1 - x, 1 - y)]

    def copy(a, k, block, to, src=None):
        dst = outs[a].at[4 * block[0] + 2 * block[1] + block[2]]
        return pltpu.make_async_remote_copy(
            src_ref=dst if src is None else src, dst_ref=dst,
            send_sem=send_sems.at[a * 7 + k], recv_sem=recv_sems.at[a * 7 + k],
            device_id=to, device_id_type=MESH)

    def first(a):
        return [copy(a, 0, me, sibling, src=ins[a])] + [copy(a, 1 + j, me, (*chip, c), src=ins[a])
                                                        for j, chip in enumerate(chips)]

    def start():
        for a in range(n):
            for cp in first(a):
                cp.start()

    def forward(j):
        for a in range(n):
            copy(a, 1 + j, (*chips[j], c), me).wait_recv()
            copy(a, 4 + j, (*chips[j], c), sibling).start()

    def finish():
        for a in range(n):
            copy(a, 0, sibling, me).wait_recv()
            for j, chip in enumerate(chips):
                copy(a, 4 + j, (*chip, 1 - c), me).wait_recv()
        for a in range(n):
            for cp in first(a) + [copy(a, 4 + j, (*chip, c), sibling) for j, chip in enumerate(chips)]:
                cp.wait_send()

    return start, forward, finish


def _rs_sibling(gs):
    n = len(gs)

    def body(*refs):
        start, finish = _sibling_schedule(refs[:n], refs[n:2 * n], *refs[2 * n:])
        start()
        finish()

    return pl.pallas_call(
        body, name="reduce_scatter_sibling",
        out_shape=_sibling_shapes(gs), in_specs=[ANY] * n, out_specs=[ANY] * n,
        scratch_shapes=_sibling_sems(n),
    )(*gs)


def _sibling_shapes(gs):
    return [jax.ShapeDtypeStruct(g.shape[1:], g.dtype) for g in gs]


def _sibling_sems(n):
    return [pltpu.SemaphoreType.DMA((n,)), pltpu.SemaphoreType.DMA((n,))]


def _sibling_schedule(ins, land, send_sems, recv_sems):
    x, y, c = _mesh_pos()

    def copies():
        return [pltpu.make_async_remote_copy(
            src_ref=ins[a].at[1 - c], dst_ref=land[a], send_sem=send_sems.at[a], recv_sem=recv_sems.at[a],
            device_id=(x, y, 1 - c), device_id_type=MESH) for a in range(len(ins))]

    def start():
        for cp in copies():
            cp.start()

    def finish():
        for cp in copies():
            cp.wait_recv()
        for cp in copies():
            cp.wait_send()

    return start, finish


def _rs_chips(ps):
    n = len(ps)

    def body(*refs):
        start, finish = _chips_schedule(refs[:n], refs[n:2 * n], *refs[2 * n:])
        start()
        finish()

    return pl.pallas_call(
        body, name="reduce_scatter_chips",
        out_shape=_chips_shapes(ps), in_specs=[ANY] * n, out_specs=[ANY] * n,
        scratch_shapes=_chips_sems(n),
    )(*ps)


def _chips_shapes(ps):
    return [jax.ShapeDtypeStruct((3,) + p.shape[1:], p.dtype) for p in ps]


def _chips_sems(n):
    return [pltpu.SemaphoreType.DMA((3 * n,)), pltpu.SemaphoreType.DMA((3 * n,))]


def _chips_schedule(ins, land, send_sems, recv_sems):
    x, y, c = _mesh_pos()
    chips = [(1 - x, y), (x, 1 - y), (1 - x, 1 - y)]

    def copies():
        return [pltpu.make_async_remote_copy(
            src_ref=ins[a].at[2 * chip[0] + chip[1]], dst_ref=land[a].at[j],
            send_sem=send_sems.at[3 * a + j], recv_sem=recv_sems.at[3 * a + j],
            device_id=(*chip, c), device_id_type=MESH) for a in range(len(ins)) for j, chip in enumerate(chips)]

    def start():
        for cp in copies():
            cp.start()

    def finish():
        for cp in copies():
            cp.wait_recv()
        for cp in copies():
            cp.wait_send()

    return start, finish


def _all_reduce_small(pack):
    rows = pack.shape[0]

    def body(p_ref, out_ref, land, send_sems, recv_sems):
        x, y, c = _mesh_pos()
        me = 4 * x + 2 * y + c
        land[me] = p_ref[...]
        copies = []
        for k in range(1, N_DEV):
            bx, by, bc = (k >> 2) & 1, (k >> 1) & 1, k & 1
            peer = (1 - x if bx else x, 1 - y if by else y, 1 - c if bc else c)
            copies.append(pltpu.make_async_remote_copy(
                src_ref=p_ref, dst_ref=land.at[me], send_sem=send_sems.at[k - 1], recv_sem=recv_sems.at[k - 1],
                device_id=peer, device_id_type=MESH))
        for cp in copies:
            cp.start()
        for cp in copies:
            cp.wait_recv()
        for cp in copies:
            cp.wait_send()
        acc = land[0]
        for d in range(1, N_DEV):
            acc = acc + land[d]
        out_ref[...] = acc

    return pl.pallas_call(
        body, name="all_reduce_small",
        out_shape=jax.ShapeDtypeStruct(pack.shape, F32),
        in_specs=[pl.BlockSpec(memory_space=pltpu.VMEM)], out_specs=pl.BlockSpec(memory_space=pltpu.VMEM),
        scratch_shapes=[pltpu.VMEM((N_DEV, rows, 128), F32), pltpu.SemaphoreType.DMA((7,)),
                        pltpu.SemaphoreType.DMA((7,))],
    )(pack)


def _in_proj(h0, nw, win_p, tm):
    rows = h0.shape[0]

    def body(h_ref, nw_ref, w_ref, o_ref):
        h = h_ref[...]
        rstd = lax.rsqrt(jnp.mean(h * h, axis=-1, keepdims=True) + EPS)
        u = (h * rstd * nw_ref[...]).astype(MXU_DTYPE)
        o_ref[...] = jnp.dot(u, w_ref[...].astype(MXU_DTYPE), preferred_element_type=F32)

    return pl.pallas_call(
        body, name="in_proj", grid=(rows // tm,),
        in_specs=[pl.BlockSpec((tm, D), lambda i: (i, 0)), pl.BlockSpec((1, D), lambda i: (0, 0)),
                  pl.BlockSpec((D, DINP), lambda i: (0, 0))],
        out_specs=pl.BlockSpec((tm, DINP), lambda i: (i, 0)),
        out_shape=jax.ShapeDtypeStruct((rows, DINP), F32),
        compiler_params=_cp(("arbitrary",), 56),
    )(h0, nw, win_p)


def _rope_tables(rows):
    pos = (jnp.arange(rows, dtype=jnp.int32) - META0).astype(F32)
    inv_freq = 1.0 / (ROPE_THETA ** (jnp.arange(0, ROPE_DIM, 2, dtype=F32) / ROPE_DIM))
    ang = pos[:, None] * jnp.tile(inv_freq, 128 // (ROPE_DIM // 2))[None, :]
    in_head = jnp.arange(128, dtype=jnp.int32)[None, :] % SWA_HD
    cos, sin = jnp.cos(ang), jnp.sin(ang)
    c_tab = jnp.where(in_head < ROPE_DIM, cos, 1.0)
    sa_tab = jnp.where(in_head < ROPE_DIM // 2, -sin, 0.0)
    sb_tab = jnp.where((in_head >= ROPE_DIM // 2) & (in_head < ROPE_DIM), sin, 0.0)
    return c_tab, sa_tab, sb_tab


def _rope(xv, cos, sa, sb):
    width = xv.shape[1]
    reps = width // 128
    if reps > 1:
        cos, sa, sb = (jnp.tile(t, (1, reps)) for t in (cos, sa, sb))
    return xv * cos + pltpu.roll(xv, width - 8, 1) * sa + pltpu.roll(xv, 8, 1) * sb


def _unrope(dy, cos, sa, sb):
    width = dy.shape[1]
    reps = width // 128
    if reps > 1:
        cos, sa, sb = (jnp.tile(t, (1, reps)) for t in (cos, sa, sb))
    return dy * cos + pltpu.roll(dy * sa, 8, 1) + pltpu.roll(dy * sb, width - 8, 1)


def _swa_prep(proj, tabs, tm):
    rows = proj.shape[0]

    def body(q_ref, k_ref, v_ref, c_ref, sa_ref, sb_ref, qo_ref, ko_ref, vo_ref):
        cos, sa, sb = c_ref[...], sa_ref[...], sb_ref[...]
        qo_ref[...] = (_rope(q_ref[...], cos, sa, sb) * (SWA_HD ** -0.5)).astype(ACT_DTYPE)
        ko_ref[...] = _rope(k_ref[...], cos, sa, sb).astype(ACT_DTYPE)
        vo_ref[...] = v_ref[...].astype(ACT_DTYPE)

    tab_spec = pl.BlockSpec((tm, 128), lambda i: (i, 0))
    return pl.pallas_call(
        body, name="swa_prep", grid=(rows // tm,),
        in_specs=[pl.BlockSpec((tm, 512), lambda i: (i, C_SQ // 512)),
                  pl.BlockSpec((tm, 128), lambda i: (i, C_SK // 128)),
                  pl.BlockSpec((tm, 128), lambda i: (i, C_SV // 128)), tab_spec, tab_spec, tab_spec],
        out_specs=[pl.BlockSpec((tm, 512), lambda i: (i, 0)), tab_spec, tab_spec],
        out_shape=[jax.ShapeDtypeStruct((rows, 512), ACT_DTYPE), jax.ShapeDtypeStruct((rows, 128), ACT_DTYPE),
                   jax.ShapeDtypeStruct((rows, 128), ACT_DTYPE)],
        compiler_params=_cp(("arbitrary",)),
    )(proj, proj, proj, *tabs)


def _gla_gates(lr, wg, bg, chunk):
    zg = _mm(lr, wg) + bg
    row = chunk * GLA_CHUNK + lax.broadcasted_iota(jnp.int32, (GLA_CHUNK, 1), 0)
    live = row >= META0
    g = jnp.where(live, _logsigmoid(zg) * (1.0 / GLA_TAU), 0.0)
    ii = lax.broadcasted_iota(jnp.int32, (GLA_CHUNK, GLA_CHUNK), 0)
    jj = lax.broadcasted_iota(jnp.int32, (GLA_CHUNK, GLA_CHUNK), 1)
    tril = jj <= ii
    b = jnp.dot(tril.astype(F32), g, precision=HIGHEST, preferred_element_type=F32)
    return zg, live, tril, b


def _gla_fwd(proj, wg_p, bg, gnw, shards):
    rows = proj.shape[0]
    nc = rows // GLA_CHUNK
    ns = len(shards)
    forward_steps = [nc * 7 // 10, nc * 8 // 10, nc * 9 // 10]

    def body(q_ref, k_ref, v_ref, r_ref, lr_ref, wg_ref, bg_ref, gnw_ref, *rest):
        shard_refs, rest = rest[:ns], rest[ns:]
        oraw_ref, og_ref, st_ref = rest[:3]
        gathered_refs, rest = rest[3:3 + ns], rest[3 + ns:]
        state, send_sems, recv_sems = rest
        c = pl.program_id(0)
        start, forward, finish = _gather_schedule(shard_refs, gathered_refs, send_sems, recv_sems)

        @pl.when(c == 0)
        def _():
            state[...] = jnp.zeros_like(state)
            start()

        for j, step in enumerate(forward_steps):
            pl.when(c == step)(functools.partial(forward, j))
        pl.when(c == nc - 1)(finish)

        zg, live, tril, b = _gla_gates(lr_ref[...], wg_ref[...], bg_ref[...], c)
        eb = jnp.exp(b)
        gq = q_ref[...] * (GLA_DK ** -0.5) * eb
        gk = k_ref[...] * jnp.exp(-b)
        ebl = eb[GLA_CHUNK - 1:GLA_CHUNK, :]
        v = v_ref[...]
        r = r_ref[...]
        gnw_v = gnw_ref[...]
        oraw, og = [], []
        for h in range(GLA_HEADS):
            s64 = slice(h * GLA_DK, (h + 1) * GLA_DK)
            s128 = slice(h * GLA_DV, (h + 1) * GLA_DV)
            qh, kh, vh, eblh = gq[:, s64], gk[:, s64], v[:, s128], ebl[:, s64]
            st = state[h]
            st_ref[0, h] = st
            a = jnp.where(tril, _mm_nt(qh, kh), 0.0)
            o = _mm(a, vh) + _mm_nt(qh, st)
            state[h] = st * eblh + _mm_tn(vh, kh * eblh)
            oraw.append(o)
            rstd = lax.rsqrt(jnp.mean(o * o, axis=-1, keepdims=True) + EPS)
            rh = r[:, s128]
            og.append(o * rstd * gnw_v * (rh * _sigmoid(rh)))
        oraw_ref[...] = jnp.concatenate(oraw, axis=1)
        og_ref[...] = jnp.concatenate(og, axis=1).astype(ACT_DTYPE)

    nb = lambda w, col: pl.BlockSpec((GLA_CHUNK, w), lambda c: (c, col // w))
    const = lambda shape: pl.BlockSpec(shape, lambda c: (0,) * len(shape))
    outs = pl.pallas_call(
        body, name="gla_fwd", grid=(nc,),
        in_specs=[nb(256, C_GQ), nb(256, C_GK), nb(512, C_GV), nb(512, C_GR), nb(128, C_LR),
                  const((128, 256)), const((1, 256)), const((1, 128))] + [ANY] * ns,
        out_specs=[pl.BlockSpec((GLA_CHUNK, 512), lambda c: (c, 0)), pl.BlockSpec((GLA_CHUNK, 512), lambda c: (c, 0)),
                   pl.BlockSpec((1, GLA_HEADS, GLA_DV, GLA_DK), lambda c: (c, 0, 0, 0))] + [ANY] * ns,
        out_shape=[jax.ShapeDtypeStruct((rows, 512), F32), jax.ShapeDtypeStruct((rows, 512), ACT_DTYPE),
                   jax.ShapeDtypeStruct((nc, GLA_HEADS, GLA_DV, GLA_DK), F32)] + _gathered_shapes(shards),
        scratch_shapes=[pltpu.VMEM((GLA_HEADS, GLA_DV, GLA_DK), F32)] + _gather_sems(ns),
        compiler_params=_cp(("arbitrary",)),
    )(proj, proj, proj, proj, proj, wg_p, bg, gnw, *shards)
    return outs[0], outs[1], outs[2], _with_own_block(outs[3:], shards)


def _swa_mask(n):
    shape = (SWA_GROUP * SWA_BLOCK, 3 * SWA_BLOCK)
    qi = lax.broadcasted_iota(jnp.int32, shape, 0) & (SWA_BLOCK - 1)
    jj = lax.broadcasted_iota(jnp.int32, shape, 1)
    meta = (jj < SWA_BLOCK) & (jj >= META0) & ((n > 0) | (jj <= qi))
    prev = (jj >= SWA_BLOCK) & (jj < 2 * SWA_BLOCK) & (n >= 2) & (jj - SWA_BLOCK > qi)
    cur = (jj >= 2 * SWA_BLOCK) & (n >= 1) & (jj - 2 * SWA_BLOCK <= qi)
    return meta | prev | cur


def _stack_heads(t, kvh):
    return jnp.concatenate([t[:, (kvh * SWA_GROUP + g) * SWA_HD:(kvh * SWA_GROUP + g + 1) * SWA_HD]
                            for g in range(SWA_GROUP)], axis=0)


def _stack_sinks(sink_ref, kvh):
    return jnp.concatenate([jnp.full((SWA_BLOCK, 1), sink_ref[0, kvh * SWA_GROUP + g], F32)
                            for g in range(SWA_GROUP)], axis=0)


def _swa_specs():
    blk = lambda w: pl.BlockSpec((SWA_BLOCK, w), lambda n: (n, 0))
    first = pl.BlockSpec((SWA_BLOCK, 128), lambda n: (0, 0))
    prev = pl.BlockSpec((SWA_BLOCK, 128), lambda n: (jnp.maximum(n - 1, 0), 0))
    return blk, first, prev


def _swa_fwd(qr, kr, vr, sinks):
    rows = qr.shape[0]
    nblk = rows // SWA_BLOCK

    def body(q_ref, k0, kp, kc, v0, vp, vc, sink_ref, o_ref):
        n = pl.program_id(0)
        q = q_ref[...]
        kall = jnp.concatenate([k0[...], kp[...], kc[...]], axis=0)
        vall = jnp.concatenate([v0[...], vp[...], vc[...]], axis=0)
        mask = _swa_mask(n)[0:SWA_BLOCK]
        outs = []
        for head in range(SWA_HEADS):
            kv = slice((head // SWA_GROUP) * SWA_HD, (head // SWA_GROUP + 1) * SWA_HD)
            s = jnp.where(mask, _mm_nt(q[:, head * SWA_HD:(head + 1) * SWA_HD], kall[:, kv]), NEG)
            sink = sink_ref[0, head]
            m = jnp.maximum(jnp.max(s, axis=-1, keepdims=True), sink)
            p = jnp.exp(s - m)
            den = jnp.sum(p, axis=-1, keepdims=True) + jnp.exp(sink - m)
            outs.append(_mm(p, vall[:, kv]) / den)
        o_ref[...] = jnp.concatenate(outs, axis=1).astype(ACT_DTYPE)

    blk, first, prev = _swa_specs()
    return pl.pallas_call(
        body, name="swa_fwd", grid=(nblk,),
        in_specs=[blk(512), first, prev, blk(128), first, prev, blk(128),
                  pl.BlockSpec(memory_space=pltpu.SMEM)],
        out_specs=blk(512),
        out_shape=jax.ShapeDtypeStruct((rows, 512), ACT_DTYPE),
        compiler_params=_cp(("arbitrary",)),
    )(qr, kr, kr, kr, vr, vr, vr, sinks)


def _out_proj(h0, og, osw, wout, nfw, tm):
    rows = h0.shape[0]

    def body(h_ref, og_ref, os_ref, w_ref, nw_ref, h1_ref, f_ref, ft_ref):
        h1 = h_ref[...] + _mm(og_ref[...], w_ref[0:512, :]) + _mm(os_ref[...], w_ref[512:1024, :])
        h1_ref[...] = h1
        rstd = lax.rsqrt(jnp.mean(h1 * h1, axis=-1, keepdims=True) + EPS)
        f = h1 * rstd * nw_ref[...]
        f_ref[...] = f.astype(ACT_DTYPE)
        ft_ref[...] = f.T.astype(ACT_DTYPE)

    row = lambda w: pl.BlockSpec((tm, w), lambda i: (i, 0))
    return pl.pallas_call(
        body, name="out_proj", grid=(rows // tm,),
        in_specs=[row(D), row(512), row(512), pl.BlockSpec((D, D), lambda i: (0, 0)), pl.BlockSpec((1, D), lambda i: (0, 0))],
        out_specs=[row(D), row(D), pl.BlockSpec((D, tm), lambda i: (0, i))],
        out_shape=[jax.ShapeDtypeStruct((rows, D), F32), jax.ShapeDtypeStruct((rows, D), ACT_DTYPE),
                   jax.ShapeDtypeStruct((D, rows), ACT_DTYPE)],
        compiler_params=_cp(("arbitrary",), 48),
    )(h0, og, osw, wout, nfw)


def _ffn_fwd(f, h1, w1, w2, tgt, fnw, tm):
    rows = f.shape[0]
    nj = D_FF // FF_WIDE

    def body(f_ref, h1_ref, w1_ref, w2_ref, t_ref, nw_ref, a_ref, dh2_ref, dh2t_ref, loss_ref, gfn_ref, acc):
        i, j = pl.program_id(0), pl.program_id(1)

        @pl.when((i == 0) & (j == 0))
        def _():
            loss_ref[...] = jnp.zeros_like(loss_ref)
            gfn_ref[...] = jnp.zeros_like(gfn_ref)

        @pl.when(j == 0)
        def _():
            acc[...] = jnp.zeros_like(acc)

        a = _mm(f_ref[...], w1_ref[...])
        a_ref[...] = a.astype(ACT_DTYPE)
        z = jnp.square(jnp.maximum(a, 0.0))
        acc[...] += _mm(z, w2_ref[...])

        @pl.when(j == nj - 1)
        def _():
            h2 = h1_ref[...] + acc[...]
            rstd = lax.rsqrt(jnp.mean(h2 * h2, axis=-1, keepdims=True) + EPS)
            hn = h2 * rstd
            nw = nw_ref[...]
            row = i * tm + lax.broadcasted_iota(jnp.int32, (tm, 1), 0)
            err = jnp.where(row >= LEAD, hn * nw - t_ref[...], 0.0)
            row_loss = jnp.sum(err * err, axis=-1, keepdims=True) * (1.0 / D)
            loss_ref[...] += jnp.broadcast_to(0.5 * jnp.sum(row_loss, axis=0, keepdims=True), loss_ref.shape)
            dy = err * (1.0 / D)
            gfn_ref[...] += jnp.broadcast_to(jnp.sum(dy * hn, axis=0, keepdims=True), gfn_ref.shape)
            dhn = dy * nw
            dh2 = rstd * (dhn - hn * jnp.mean(dhn * hn, axis=-1, keepdims=True))
            dh2_ref[...] = dh2
            dh2t_ref[...] = dh2.T.astype(ACT_DTYPE)

    return pl.pallas_call(
        body, name="ffn_fwd", grid=(rows // tm, nj),
        in_specs=[pl.BlockSpec((tm, D), lambda i, j: (i, 0)), pl.BlockSpec((tm, D), lambda i, j: (i, 0)),
                  pl.BlockSpec((D, FF_WIDE), lambda i, j: (0, j)),
                  pl.BlockSpec((FF_WIDE, D), lambda i, j: (j, 0)),
                  pl.BlockSpec((tm, D), lambda i, j: (i, 0)), pl.BlockSpec((1, D), lambda i, j: (0, 0))],
        out_specs=[pl.BlockSpec((tm, FF_WIDE), lambda i, j: (i, j)), pl.BlockSpec((tm, D), lambda i, j: (i, 0)),
                   pl.BlockSpec((D, tm), lambda i, j: (0, i)),
                   pl.BlockSpec((8, 128), lambda i, j: (0, 0)), pl.BlockSpec((8, D), lambda i, j: (0, 0))],
        out_shape=[jax.ShapeDtypeStruct((rows, D_FF), ACT_DTYPE), jax.ShapeDtypeStruct((rows, D), F32),
                   jax.ShapeDtypeStruct((D, rows), ACT_DTYPE),
                   jax.ShapeDtypeStruct((8, 128), F32), jax.ShapeDtypeStruct((8, D), F32)],
        scratch_shapes=[pltpu.VMEM((tm, D), F32)],
        compiler_params=_cp(("arbitrary", "arbitrary"), 56),
    )(f, h1, w1, w2, tgt, fnw)


def _ffn_bwd_act(dh2, a, w1, w2, h1, nfw, tm):
    rows = dh2.shape[0]
    nj = D_FF // FF_WIDE

    def body(dh2_ref, a_ref, w1_ref, w2_ref, h1_ref, nw_ref, da_ref, dh1_ref, gnf_ref, acc):
        i, j = pl.program_id(0), pl.program_id(1)

        @pl.when((i == 0) & (j == 0))
        def _():
            gnf_ref[...] = jnp.zeros_like(gnf_ref)

        @pl.when(j == 0)
        def _():
            acc[...] = jnp.zeros_like(acc)

        dz = _mm_nt(dh2_ref[...], w2_ref[...])
        da = dz * (2.0 * jnp.maximum(a_ref[...].astype(F32), 0.0))
        da_ref[...] = da.astype(ACT_DTYPE)
        acc[...] += _mm_nt(da, w1_ref[...])

        @pl.when(j == nj - 1)
        def _():
            h1 = h1_ref[...]
            rstd = lax.rsqrt(jnp.mean(h1 * h1, axis=-1, keepdims=True) + EPS)
            hn = h1 * rstd
            df = acc[...]
            gnf_ref[...] += jnp.broadcast_to(jnp.sum(df * hn, axis=0, keepdims=True), gnf_ref.shape)
            dfn = df * nw_ref[...]
            dh1_ref[...] = dh2_ref[...] + rstd * (dfn - hn * jnp.mean(dfn * hn, axis=-1, keepdims=True))

    return pl.pallas_call(
        body, name="ffn_bwd_act", grid=(rows // tm, nj),
        in_specs=[pl.BlockSpec((tm, D), lambda i, j: (i, 0)), pl.BlockSpec((tm, FF_WIDE), lambda i, j: (i, j)),
                  pl.BlockSpec((D, FF_WIDE), lambda i, j: (0, j)),
                  pl.BlockSpec((FF_WIDE, D), lambda i, j: (j, 0)),
                  pl.BlockSpec((tm, D), lambda i, j: (i, 0)), pl.BlockSpec((1, D), lambda i, j: (0, 0))],
        out_specs=[pl.BlockSpec((tm, FF_WIDE), lambda i, j: (i, j)), pl.BlockSpec((tm, D), lambda i, j: (i, 0)),
                   pl.BlockSpec((8, D), lambda i, j: (0, 0))],
        out_shape=[jax.ShapeDtypeStruct((rows, D_FF), ACT_DTYPE), jax.ShapeDtypeStruct((rows, D), F32),
                   jax.ShapeDtypeStruct((8, D), F32)],
        scratch_shapes=[pltpu.VMEM((tm, D), F32)],
        compiler_params=_cp(("arbitrary", "arbitrary"), 56),
    )(dh2, a, w1, w2, h1, nfw)


def _ffn_bwd_weights(ft, a, da, dh2t, tm):
    rows = a.shape[0]
    steps = rows // tm

    def body(ft_ref, a_ref, da_ref, dh2t_ref, dw1_ref, dw2_ref, dw2t):
        i = pl.program_id(1)

        @pl.when(i == 0)
        def _():
            dw1_ref[...] = jnp.zeros_like(dw1_ref)
            dw2t[...] = jnp.zeros_like(dw2t)

        z = jnp.square(jnp.maximum(a_ref[...].astype(F32), 0.0))
        dw1_ref[...] += _mm(ft_ref[...], da_ref[...])
        dw2t[...] += _mm(dh2t_ref[...], z)

        @pl.when(i == steps - 1)
        def _():
            dw2_ref[...] = dw2t[...].T

    return pl.pallas_call(
        body, name="ffn_bwd_weights", grid=(N_DEV, steps),
        in_specs=[pl.BlockSpec((D, tm), lambda j, i: (0, i)), pl.BlockSpec((tm, FF_TILE), lambda j, i: (i, j)),
                  pl.BlockSpec((tm, FF_TILE), lambda j, i: (i, j)), pl.BlockSpec((D, tm), lambda j, i: (0, i))],
        out_specs=[pl.BlockSpec((None, None, D, FF_TILE), lambda j, i: (j % 2, j // 2, 0, 0)),
                   pl.BlockSpec((None, None, FF_TILE, D), lambda j, i: (j % 2, j // 2, 0, 0))],
        out_shape=[jax.ShapeDtypeStruct((2, 4, D, FF_TILE), F32), jax.ShapeDtypeStruct((2, 4, FF_TILE, D), F32)],
        scratch_shapes=[pltpu.VMEM((D, FF_TILE), F32)],
        compiler_params=_cp(("arbitrary", "arbitrary"), 48),
    )(ft, a, da, dh2t)


def _out_proj_bwd(dh1, og, osw, wout, tm, partials):
    rows = dh1.shape[0]
    steps = rows // tm
    ns = len(partials)

    def body(dh1_ref, og_ref, os_ref, w_ref, *rest):
        part_refs, rest = rest[:ns], rest[ns:]
        dog_ref, dos_ref, dw_ref = rest[:3]
        land_refs, (send_sems, recv_sems) = rest[3:3 + ns], rest[3 + ns:]
        i = pl.program_id(0)
        start, finish = _sibling_schedule(part_refs, land_refs, send_sems, recv_sems)

        @pl.when(i == 0)
        def _():
            dw_ref[...] = jnp.zeros_like(dw_ref)
            start()

        pl.when(i == steps - 1)(finish)

        dh1 = dh1_ref[...].astype(MXU_DTYPE)
        dog_ref[...] = _mm_nt(dh1, w_ref[0:512, :])
        dos_ref[...] = _mm_nt(dh1, w_ref[512:1024, :])
        for half, ref in enumerate((og_ref, os_ref)):
            dw = _mm_tn(ref[...], dh1)
            for blk in range(4):
                shard = half * 4 + blk
                dw_ref[shard % 2, shard // 2] += dw[blk * 128:(blk + 1) * 128, :]

    row = lambda w: pl.BlockSpec((tm, w), lambda i: (i, 0))
    outs = pl.pallas_call(
        body, name="out_proj_bwd", grid=(steps,),
        in_specs=[row(D), row(512), row(512), pl.BlockSpec((D, D), lambda i: (0, 0))] + [ANY] * ns,
        out_specs=[row(512), row(512), pl.BlockSpec((2, 4, 128, D), lambda i: (0, 0, 0, 0))] + [ANY] * ns,
        out_shape=[jax.ShapeDtypeStruct((rows, 512), F32), jax.ShapeDtypeStruct((rows, 512), F32),
                   jax.ShapeDtypeStruct((2, 4, 128, D), F32)] + _sibling_shapes(partials),
        scratch_shapes=_sibling_sems(ns),
        compiler_params=_cp(("arbitrary",), 48),
    )(dh1, og, osw, wout, *partials)
    return outs[0], outs[1], outs[2], outs[3:]


def _swa_bwd(qr, kr, vr, osw, dos, sinks, chip_sums):
    rows = qr.shape[0]
    nblk = rows // SWA_BLOCK
    ns = len(chip_sums)

    def body(q_ref, k0, kp, kc, v0, vp, vc, o_ref, do_ref, sink_ref, *rest):
        sum_refs, rest = rest[:ns], rest[ns:]
        dq_ref, dk_ref, dv_ref, dsink_ref = rest[:4]
        land_refs, (send_sems, recv_sems) = rest[4:4 + ns], rest[4 + ns:]
        n = pl.program_id(0)
        start, finish = _chips_schedule(sum_refs, land_refs, send_sems, recv_sems)

        @pl.when(n == 0)
        def _():
            dk_ref[...] = jnp.zeros_like(dk_ref)
            dv_ref[...] = jnp.zeros_like(dv_ref)
            dsink_ref[...] = jnp.zeros_like(dsink_ref)
            start()

        pl.when(n == nblk - 1)(finish)

        q = q_ref[...]
        kall = jnp.concatenate([k0[...], kp[...], kc[...]], axis=0)
        vall = jnp.concatenate([v0[...], vp[...], vc[...]], axis=0)
        mask = _swa_mask(n)
        do_all = do_ref[...]
        o_all = o_ref[...].astype(F32)
        dq, dk, dv = [], [], []
        for kvh in range(SWA_KV):
            kv = slice(kvh * SWA_HD, (kvh + 1) * SWA_HD)
            q4, do4, o4 = _stack_heads(q, kvh), _stack_heads(do_all, kvh), _stack_heads(o_all, kvh)
            sink4 = _stack_sinks(sink_ref, kvh)
            s = jnp.where(mask, _mm_nt(q4, kall[:, kv]), NEG)
            m = jnp.maximum(jnp.max(s, axis=-1, keepdims=True), sink4)
            e = jnp.exp(s - m)
            inv = 1.0 / (jnp.sum(e, axis=-1, keepdims=True) + jnp.exp(sink4 - m))
            p = e * inv
            delta = jnp.sum(do4 * o4, axis=-1, keepdims=True)
            ds = p * (_mm_nt(do4, vall[:, kv]) - delta)
            dq4 = _mm(ds, kall[:, kv])
            dq += [dq4[g * SWA_BLOCK:(g + 1) * SWA_BLOCK] for g in range(SWA_GROUP)]
            dk.append(_mm_tn(ds, q4))
            dv.append(_mm_tn(p, do4))
            sink_term = jnp.exp(sink4 - m) * inv * delta
            for g in range(SWA_GROUP):
                head = kvh * SWA_GROUP + g
                dsink = -jnp.sum(sink_term[g * SWA_BLOCK:(g + 1) * SWA_BLOCK], axis=0, keepdims=True)
                dsink_ref[head:head + 1, :] += jnp.broadcast_to(dsink, (1, 128))
        dq_ref[...] = jnp.concatenate(dq, axis=1)
        dk_all = jnp.concatenate(dk, axis=1)
        dv_all = jnp.concatenate(dv, axis=1)
        prev0 = pl.multiple_of(jnp.maximum(n - 1, 0) * SWA_BLOCK, SWA_BLOCK)
        cur0 = pl.multiple_of(n * SWA_BLOCK, SWA_BLOCK)
        for ref, val in ((dk_ref, dk_all), (dv_ref, dv_all)):
            ref[0:SWA_BLOCK, :] += val[0:SWA_BLOCK]
            ref[pl.ds(prev0, SWA_BLOCK), :] += val[SWA_BLOCK:2 * SWA_BLOCK]
            ref[pl.ds(cur0, SWA_BLOCK), :] += val[2 * SWA_BLOCK:]

    blk, first, prev = _swa_specs()
    whole = pl.BlockSpec((rows, 128), lambda n: (0, 0))
    outs = pl.pallas_call(
        body, name="swa_bwd", grid=(nblk,),
        in_specs=[blk(512), first, prev, blk(128), first, prev, blk(128), blk(512), blk(512),
                  pl.BlockSpec(memory_space=pltpu.SMEM)] + [ANY] * ns,
        out_specs=[blk(512), whole, whole, pl.BlockSpec((8, 128), lambda n: (0, 0))] + [ANY] * ns,
        out_shape=[jax.ShapeDtypeStruct((rows, 512), F32), jax.ShapeDtypeStruct((rows, 128), F32),
                   jax.ShapeDtypeStruct((rows, 128), F32), jax.ShapeDtypeStruct((8, 128), F32)] + _chips_shapes(chip_sums),
        scratch_shapes=_chips_sems(ns),
        compiler_params=_cp(("arbitrary",), 48),
    )(qr, kr, kr, kr, vr, vr, vr, osw, dos, sinks, *chip_sums)
    return outs[0], outs[1], outs[2], outs[3], outs[4:]


def _gla_bwd(proj, oraw, states, dog, wg_p, bg, gnw):
    rows = proj.shape[0]
    nc = rows // GLA_CHUNK

    def body(q_ref, k_ref, v_ref, r_ref, lr_ref, oraw_ref, st_ref, dog_ref, wg_ref, bg_ref, gnw_ref,
             dq_ref, dk_ref, dv_ref, dr_ref, dlr_ref, dwg_ref, dbg_ref, dgnw_ref, dstate):
        t = pl.program_id(0)
        c = nc - 1 - t

        @pl.when(t == 0)
        def _():
            dstate[...] = jnp.zeros_like(dstate)
            dwg_ref[...] = jnp.zeros_like(dwg_ref)
            dbg_ref[...] = jnp.zeros_like(dbg_ref)
            dgnw_ref[...] = jnp.zeros_like(dgnw_ref)

        lr, wg = lr_ref[...], wg_ref[...]
        zg, live, tril, b = _gla_gates(lr, wg, bg_ref[...], c)
        eb, enb = jnp.exp(b), jnp.exp(-b)
        scale = GLA_DK ** -0.5
        gq = q_ref[...] * scale * eb
        gk = k_ref[...] * enb
        ebl = eb[GLA_CHUNK - 1:GLA_CHUNK, :]
        v, r, oraw_v, dog_v = v_ref[...], r_ref[...], oraw_ref[...], dog_ref[...]
        gnw_v = gnw_ref[...]
        is_last = lax.broadcasted_iota(jnp.int32, (GLA_CHUNK, 1), 0) == GLA_CHUNK - 1
        dq, dk, dv, dr, db = [], [], [], [], []
        dgnw = jnp.zeros((1, GLA_DV), F32)
        for h in range(GLA_HEADS):
            s64 = slice(h * GLA_DK, (h + 1) * GLA_DK)
            s128 = slice(h * GLA_DV, (h + 1) * GLA_DV)
            qh, kh, vh, eblh = gq[:, s64], gk[:, s64], v[:, s128], ebl[:, s64]
            klh = kh * eblh
            st = st_ref[0, h]
            o, rh, dout = oraw_v[:, s128], r[:, s128], dog_v[:, s128]
            rstd = lax.rsqrt(jnp.mean(o * o, axis=-1, keepdims=True) + EPS)
            on = o * rstd
            sg = _sigmoid(rh)
            dr.append(dout * (on * gnw_v) * (sg * (1.0 + rh * (1.0 - sg))))
            dy = dout * (rh * sg)
            dgnw = dgnw + jnp.sum(dy * on, axis=0, keepdims=True)
            don = dy * gnw_v
            do = rstd * (don - on * jnp.mean(don * on, axis=-1, keepdims=True))
            a = jnp.where(tril, _mm_nt(qh, kh), 0.0)
            da = jnp.where(tril, _mm_nt(do, vh), 0.0)
            dsp = dstate[h]
            dkl = _mm(vh, dsp)
            dv.append(_mm_tn(a, do) + _mm_nt(klh, dsp))
            debl = jnp.sum(dsp * st, axis=0, keepdims=True)
            dgq = _mm(da, kh) + _mm(do, st)
            dgk = _mm_tn(da, qh)
            dstate[h] = dsp * eblh + _mm_tn(do, qh)
            dq.append(dgq * (scale * eb[:, s64]))
            dk.append((dgk + dkl * eblh) * enb[:, s64])
            last = debl * eblh + jnp.sum(dkl * klh, axis=0, keepdims=True)
            db.append(dgq * qh - dgk * kh - dkl * klh + jnp.where(is_last, last, 0.0))
        dq_ref[...] = jnp.concatenate(dq, axis=1).astype(ACT_DTYPE)
        dk_ref[...] = jnp.concatenate(dk, axis=1).astype(ACT_DTYPE)
        dv_ref[...] = jnp.concatenate(dv, axis=1).astype(ACT_DTYPE)
        dr_ref[...] = jnp.concatenate(dr, axis=1).astype(ACT_DTYPE)
        triu = jnp.logical_not(tril) | (lax.broadcasted_iota(jnp.int32, (GLA_CHUNK, GLA_CHUNK), 0)
                                        == lax.broadcasted_iota(jnp.int32, (GLA_CHUNK, GLA_CHUNK), 1))
        dg = jnp.dot(triu.astype(F32), jnp.concatenate(db, axis=1), precision=HIGHEST, preferred_element_type=F32)
        dzg = jnp.where(live, dg * _sigmoid(-zg) * (1.0 / GLA_TAU), 0.0)
        dlr_ref[...] = _mm_nt(dzg, wg).astype(ACT_DTYPE)
        dwg_ref[...] += _mm_tn(lr, dzg)
        dbg_ref[...] += jnp.broadcast_to(jnp.sum(dzg, axis=0, keepdims=True), dbg_ref.shape)
        dgnw_ref[...] += jnp.broadcast_to(dgnw, dgnw_ref.shape)

    nb = lambda w, col: pl.BlockSpec((GLA_CHUNK, w), lambda t: (nc - 1 - t, col // w))
    const = lambda shape: pl.BlockSpec(shape, lambda t: (0,) * len(shape))
    return pl.pallas_call(
        body, name="gla_bwd", grid=(nc,),
        in_specs=[nb(256, C_GQ), nb(256, C_GK), nb(512, C_GV), nb(512, C_GR), nb(128, C_LR), nb(512, 0),
                  pl.BlockSpec((1, GLA_HEADS, GLA_DV, GLA_DK), lambda t: (nc - 1 - t, 0, 0, 0)), nb(512, 0),
                  const((128, 256)), const((1, 256)), const((1, 128))],
        out_specs=[nb(256, 0), nb(256, 0), nb(512, 0), nb(512, 0), nb(128, 0),
                   const((128, 256)), const((8, 256)), const((8, 128))],
        out_shape=[jax.ShapeDtypeStruct((rows, 256), ACT_DTYPE), jax.ShapeDtypeStruct((rows, 256), ACT_DTYPE),
                   jax.ShapeDtypeStruct((rows, 512), ACT_DTYPE), jax.ShapeDtypeStruct((rows, 512), ACT_DTYPE),
                   jax.ShapeDtypeStruct((rows, 128), ACT_DTYPE), jax.ShapeDtypeStruct((128, 256), F32),
                   jax.ShapeDtypeStruct((8, 256), F32), jax.ShapeDtypeStruct((8, 128), F32)],
        scratch_shapes=[pltpu.VMEM((GLA_HEADS, GLA_DV, GLA_DK), F32)],
        compiler_params=_cp(("arbitrary",)),
    )(proj, proj, proj, proj, proj, oraw, states, dog, wg_p, bg, gnw)


def _in_proj_bwd(h0, dh1, nw, win_p, dgv, dgr, dsq, dgq, dgk, dsk, dsv, dlr, tabs, tm):
    rows = h0.shape[0]

    def body(h_ref, dh1_ref, nw_ref, w_ref, dgv_ref, dgr_ref, dsq_ref, dgq_ref, dgk_ref, dsk_ref, dsv_ref, dlr_ref,
             c_ref, sa_ref, sb_ref, dh0_ref, dw_ref, gnm_ref):
        i = pl.program_id(0)

        @pl.when(i == 0)
        def _():
            dw_ref[...] = jnp.zeros_like(dw_ref)
            gnm_ref[...] = jnp.zeros_like(gnm_ref)

        cos, sa, sb = c_ref[...], sa_ref[...], sb_ref[...]
        dsq_v = (_unrope(dsq_ref[...], cos, sa, sb) * (SWA_HD ** -0.5)).astype(MXU_DTYPE)
        dsk_v = _unrope(dsk_ref[...], cos, sa, sb).astype(MXU_DTYPE)
        dproj = jnp.concatenate(
            [dgv_ref[...].astype(MXU_DTYPE), dgr_ref[...].astype(MXU_DTYPE), dsq_v, dgq_ref[...].astype(MXU_DTYPE),
             dgk_ref[...].astype(MXU_DTYPE), dsk_v, dsv_ref[...].astype(MXU_DTYPE), dlr_ref[...].astype(MXU_DTYPE)],
            axis=1)
        h = h_ref[...]
        rstd = lax.rsqrt(jnp.mean(h * h, axis=-1, keepdims=True) + EPS)
        hn = h * rstd
        nw_v = nw_ref[...]
        u = (hn * nw_v).astype(MXU_DTYPE)
        du = _mm_nt(dproj, w_ref[...])
        dw_ref[...] += _mm_tn(u, dproj)
        gnm_ref[...] += jnp.broadcast_to(jnp.sum(du * hn, axis=0, keepdims=True), gnm_ref.shape)
        dun = du * nw_v
        dh0_ref[...] = dh1_ref[...] + rstd * (dun - hn * jnp.mean(dun * hn, axis=-1, keepdims=True))

    row = lambda w: pl.BlockSpec((tm, w), lambda i: (i, 0))
    return pl.pallas_call(
        body, name="in_proj_bwd", grid=(rows // tm,),
        in_specs=[row(D), row(D), pl.BlockSpec((1, D), lambda i: (0, 0)), pl.BlockSpec((D, DINP), lambda i: (0, 0)),
                  row(512), row(512), row(512), row(256), row(256), row(128), row(128), row(128),
                  row(128), row(128), row(128)],
        out_specs=[row(D), pl.BlockSpec((D, DINP), lambda i: (0, 0)), pl.BlockSpec((8, D), lambda i: (0, 0))],
        out_shape=[jax.ShapeDtypeStruct((rows, D), F32), jax.ShapeDtypeStruct((D, DINP), F32),
                   jax.ShapeDtypeStruct((8, D), F32)],
        compiler_params=_cp(("arbitrary",), 56),
    )(h0, dh1, nw, win_p, dgv, dgr, dsq, dgq, dgk, dsk, dsv, dlr, *tabs)


def _adamw(w, g, m, v):
    m = ADAM_B1 * m + (1.0 - ADAM_B1) * g
    v = ADAM_B2 * v + (1.0 - ADAM_B2) * jnp.square(g)
    m_hat = m / (1.0 - ADAM_B1 ** ADAM_STEP)
    v_hat = v / (1.0 - ADAM_B2 ** ADAM_STEP)
    delta = -ADAM_LR * (m_hat / (jnp.sqrt(v_hat) + ADAM_EPS) + ADAM_WD * w)
    return delta, m, v


def _adamw_shard(where, parts, own, w, m, v, name):
    r, cdim = w.shape
    tr = 128 if r % 128 == 0 else r

    def body(where_ref, p_ref, own_ref, w_ref, m_ref, v_ref, g_ref, d_ref, nm_ref, nv_ref):
        g = ((p_ref[0] + p_ref[1]) + p_ref[2]) + own_ref[...]
        g_ref[...] = g
        d_ref[...], nm_ref[...], nv_ref[...] = _adamw(w_ref[...], g, m_ref[...], v_ref[...])

    spec = pl.BlockSpec((tr, cdim), lambda i, s: (i, 0))
    shape = jax.ShapeDtypeStruct((r, cdim), F32)
    return pl.pallas_call(
        body, name=name,
        grid_spec=pltpu.PrefetchScalarGridSpec(
            num_scalar_prefetch=1, grid=(r // tr,),
            in_specs=[pl.BlockSpec((3, tr, cdim), lambda i, s: (0, i, 0)),
                      pl.BlockSpec((None, tr, cdim), lambda i, s: (s[1], i, 0)), spec, spec, spec],
            out_specs=[spec] * 4),
        out_shape=[shape] * 4,
        compiler_params=_cp(("arbitrary",)),
    )(where, parts, own, w, m, v)


def _adamw_small(w, g, m, v):
    def body(w_ref, g_ref, m_ref, v_ref, d_ref, nm_ref, nv_ref):
        d_ref[...], nm_ref[...], nv_ref[...] = _adamw(w_ref[...], g_ref[...], m_ref[...], v_ref[...])

    vm = pl.BlockSpec(memory_space=pltpu.VMEM)
    shape = jax.ShapeDtypeStruct(w.shape, F32)
    return pl.pallas_call(body, name="adamw_small", in_specs=[vm] * 4, out_specs=[vm] * 3,
                          out_shape=[shape] * 3)(w, g, m, v)


def _add_own_half(where, full, theirs, name):
    _, _, r, cdim = full.shape
    tr = 128 if r % 128 == 0 else r

    def body(where_ref, a_ref, b_ref, o_ref):
        o_ref[...] = a_ref[...] + b_ref[...]

    spec = pl.BlockSpec((4, tr, cdim), lambda i, s: (0, i, 0))
    return pl.pallas_call(
        body, name=name,
        grid_spec=pltpu.PrefetchScalarGridSpec(
            num_scalar_prefetch=1, grid=(r // tr,),
            in_specs=[pl.BlockSpec((None, 4, tr, cdim), lambda i, s: (s[0], 0, i, 0)), spec], out_specs=spec),
        out_shape=jax.ShapeDtypeStruct(theirs.shape, F32), compiler_params=_cp(("arbitrary",)))(where, full, theirs)


def _to_rows128(a):
    return a.reshape(-1, 128)


def _pad_rows128(vec):
    flat = vec.reshape(-1)
    n = -(-flat.shape[0] // 128)
    return jnp.pad(flat, (0, n * 128 - flat.shape[0])).reshape(n, 128)


def kernel(x, meta_tokens, norm_mix_w, w_in, w_gate_up, b_gate, gla_norm_w, sinks, w_out, norm_ff_w, w_ff1, w_ff2, final_norm_w, loss_target, m_meta_tokens, m_norm_mix_w, m_w_in, m_w_gate_up, m_b_gate, m_gla_norm_w, m_sinks, m_w_out, m_norm_ff_w, m_w_ff1, m_w_ff2, m_final_norm_w, v_meta_tokens, v_norm_mix_w, v_w_in, v_w_gate_up, v_b_gate, v_gla_norm_w, v_sinks, v_w_out, v_norm_ff_w, v_w_ff1, v_w_ff2, v_final_norm_w):
    seq = x.shape[1]
    rows = LEAD + seq
    tm = _row_tile(rows)
    tm_small = tm // 2 if tm == 640 else tm
    dev = 4 * lax.axis_index("x") + 2 * lax.axis_index("y") + lax.axis_index("c")

    small_shard = jnp.concatenate([meta_tokens, w_gate_up[0], jnp.zeros((N_META, 96), F32)], axis=1)
    g_in, g_small = _all_gather([w_in[0].astype(WIRE_DTYPE), small_shard])
    later_shards = [w_out[0].astype(WIRE_DTYPE), w_ff1[0].astype(WIRE_DTYPE), w_ff2[0].astype(WIRE_DTYPE)]
    win_full = jnp.transpose(g_in, (1, 0, 2)).reshape(D, DIN)
    cols = lambda r: win_full[:, r[0]:r[1]]
    win_p = jnp.concatenate([cols(O_GV), cols(O_GR), cols(O_SQ), cols(O_GQ), cols(O_GK), cols(O_SK), cols(O_SV),
                             cols(O_LR), jnp.zeros((D, 128 - GLA_RANK), WIRE_DTYPE)], axis=1)
    meta_full = jnp.transpose(g_small[:, :, 0:128], (1, 0, 2)).reshape(N_META, D)
    wg_full = jnp.transpose(g_small[:, :, 128:160], (1, 0, 2)).reshape(GLA_RANK, GLA_HEADS * GLA_DK)
    wg_p = jnp.concatenate([wg_full, jnp.zeros((128 - GLA_RANK, 256), F32)], axis=0)

    h0 = jnp.concatenate([jnp.zeros((META0, D), F32), meta_full, x[0]], axis=0)
    tgt = jnp.concatenate([jnp.zeros((LEAD, D), F32), loss_target[0]], axis=0)
    tabs = _rope_tables(rows)
    proj = _in_proj(h0, norm_mix_w, win_p, tm)
    oraw, og, states, (g_out, g_w1, g_w2) = _gla_fwd(proj, wg_p, b_gate, gla_norm_w, later_shards)
    wout_full = g_out.reshape(D, D)
    w2_full = g_w2.reshape(D_FF, D)
    w1_full = jnp.transpose(g_w1, (1, 0, 2)).reshape(D, D_FF)
    qr, kr, vr = _swa_prep(proj, tabs, tm)
    osw = _swa_fwd(qr, kr, vr, sinks)
    h1, f, ft = _out_proj(h0, og, osw, wout_full, norm_ff_w, tm)
    a, dh2, dh2t, loss_p, gfn_p = _ffn_fwd(f, h1, w1_full, w2_full, tgt, final_norm_w.reshape(1, D), tm)

    da, dh1, gnf_p = _ffn_bwd_act(dh2, a, w1_full, w2_full, h1, norm_ff_w, tm)
    dw1, dw2 = _ffn_bwd_weights(ft, a, da, dh2t, tm)
    where = jnp.stack([lax.axis_index("c"), 2 * lax.axis_index("x") + lax.axis_index("y")]).astype(jnp.int32)
    dog, dos, dwout, theirs_ffn = _out_proj_bwd(dh1, og, osw, wout_full, tm, [dw1, dw2])
    sums_ffn = [_add_own_half(where, p, q, "reduce_pair_%d" % (2 + k))
                for k, (p, q) in enumerate(zip([dw1, dw2], theirs_ffn))]
    dsq, dsk, dsv, dsink_p, parts_ffn = _swa_bwd(qr, kr, vr, osw, dos, sinks, sums_ffn)
    dgq, dgk, dgv, dgr, dlr, dwg_p, dbg_p, dgnw_p = _gla_bwd(proj, oraw, states, dog, wg_p, b_gate, gla_norm_w)
    dh0, dwin_p, gnm_p = _in_proj_bwd(h0, dh1, norm_mix_w, win_p, dgv, dgr, dsq, dgq, dgk, dsk, dsv, dlr, tabs, tm_small)
    grad_x = dh0[LEAD:][None]

    pcols = lambda c0, r: dwin_p[:, c0:c0 + (r[1] - r[0])]
    dwin = jnp.concatenate([pcols(C_GQ, O_GQ), pcols(C_GK, O_GK), pcols(C_GV, O_GV), pcols(C_GR, O_GR),
                            pcols(C_LR, O_LR), pcols(C_SQ, O_SQ), pcols(C_SK, O_SK), pcols(C_SV, O_SV)], axis=1)
    dwin = jnp.transpose(dwin.reshape(D, 4, 2, DIN // N_DEV), (2, 1, 0, 3))
    theirs_tail = _rs_sibling([dwin, dwout])
    sums_tail = [_add_own_half(where, p, q, "reduce_pair_%d" % k)
                 for k, (p, q) in enumerate(zip([dwin, dwout], theirs_tail))]
    chip_sums = sums_tail + sums_ffn
    parts = list(_rs_chips(sums_tail)) + list(parts_ffn)

    small = [dh0[META0:LEAD], dwg_p[0:GLA_RANK], gnm_p[0:1], dbg_p[0:1], dgnw_p[0:1], dsink_p[:, 0], gnf_p[0:1],
             gfn_p[0:1], loss_p[0:1, 0:1]]
    sizes = [-(-s.size // 128) for s in small]
    pack = jnp.concatenate([_pad_rows128(s) for s in small], axis=0)
    pad_rows = -pack.shape[0] % 8
    pack = jnp.pad(pack, ((0, pad_rows), (0, 0)))
    total = _all_reduce_small(pack)
    offs = [sum(sizes[:k]) for k in range(len(sizes))]
    take = lambda k, shape: total[offs[k]:offs[k] + sizes[k]].reshape(-1)[:small[k].size].reshape(shape)
    g_meta_full = take(0, (N_META, D))
    g_wg_full = take(1, (GLA_RANK, 256))
    g_meta = lax.dynamic_slice_in_dim(g_meta_full, dev * 128, 128, axis=1)
    g_wg = lax.dynamic_slice_in_dim(g_wg_full, dev * 32, 32, axis=1)[None]
    g_norm_mix, g_b_gate, g_gla_norm = take(2, (1, D)), take(3, (1, 256)), take(4, (1, 128))
    g_sinks, g_norm_ff, g_final_norm = take(5, (1, 8)), take(6, (1, D)), take(7, (D,))
    loss = take(8, ())

    g_win, d_win, nm_win, nv_win = _adamw_shard(where, parts[0], chip_sums[0], w_in[0], m_w_in[0], v_w_in[0], "adamw_w_in")
    g_wout, d_wout, nm_wout, nv_wout = _adamw_shard(where, parts[1], chip_sums[1], w_out[0], m_w_out[0], v_w_out[0], "adamw_w_out")
    g_w1s, d_w1, nm_w1, nv_w1 = _adamw_shard(where, parts[2], chip_sums[2], w_ff1[0], m_w_ff1[0], v_w_ff1[0], "adamw_w_ff1")
    g_w2s, d_w2, nm_w2, nv_w2 = _adamw_shard(where, parts[3], chip_sums[3], w_ff2[0], m_w_ff2[0], v_w_ff2[0], "adamw_w_ff2")

    names = ["meta", "wg", "norm_mix", "b_gate", "gla_norm", "sinks", "norm_ff", "final_norm"]
    ws = [meta_tokens, w_gate_up, norm_mix_w, b_gate, gla_norm_w, sinks, norm_ff_w, final_norm_w]
    gs = [g_meta, g_wg, g_norm_mix, g_b_gate, g_gla_norm, g_sinks, g_norm_ff, g_final_norm]
    ms = [m_meta_tokens, m_w_gate_up, m_norm_mix_w, m_b_gate, m_gla_norm_w, m_sinks, m_norm_ff_w, m_final_norm_w]
    vs = [v_meta_tokens, v_w_gate_up, v_norm_mix_w, v_b_gate, v_gla_norm_w, v_sinks, v_norm_ff_w, v_final_norm_w]
    ssz = [-(-w.size // 128) for w in ws]
    packed = []
    for group in (ws, gs, ms, vs):
        p = jnp.concatenate([_pad_rows128(t) for t in group], axis=0)
        packed.append(jnp.pad(p, ((0, -p.shape[0] % 8), (0, 0))))
    d_s, nm_s, nv_s = _adamw_small(*packed)
    soffs = [sum(ssz[:k]) for k in range(len(ssz))]
    unpack = lambda t, k: t[soffs[k]:soffs[k] + ssz[k]].reshape(-1)[:ws[k].size].reshape(ws[k].shape)
    d_small = {n: unpack(d_s, k) for k, n in enumerate(names)}
    nm_small = {n: unpack(nm_s, k) for k, n in enumerate(names)}
    nv_small = {n: unpack(nv_s, k) for k, n in enumerate(names)}
    g_small_d = dict(zip(names, gs))

    def ordered(big, small_d):
        win_v, wout_v, w1_v, w2_v = big
        return (small_d["meta"], small_d["norm_mix"], win_v[None], small_d["wg"], small_d["b_gate"],
                small_d["gla_norm"], small_d["sinks"], wout_v[None], small_d["norm_ff"], w1_v[None], w2_v[None],
                small_d["final_norm"])

    return (loss, grad_x,
            *ordered((g_win, g_wout, g_w1s, g_w2s), g_small_d),
            *ordered((d_win, d_wout, d_w1, d_w2), d_small),
            *ordered((nm_win, nm_wout, nm_w1, nm_w2), nm_small),
            *ordered((nv_win, nv_wout, nv_w1, nv_w2), nv_small))
```

```python
import functools

import jax
import jax.numpy as jnp
from jax import lax
from jax.experimental import pallas as pl
from jax.experimental.pallas import tpu as pltpu

F32 = jnp.float32
MXU_DTYPE = jnp.bfloat16
ACT_DTYPE = jnp.bfloat16
WIRE_DTYPE = jnp.bfloat16

D = 1024
N_META = 16
LEAD = 128
META0 = LEAD - N_META
EPS = 1e-5
GLA_HEADS, GLA_DK, GLA_DV, GLA_RANK, GLA_CHUNK = 4, 64, 128, 16, 64
GLA_TAU = 16.0
SWA_HEADS, SWA_KV, SWA_GROUP, SWA_HD, SWA_BLOCK = 8, 2, 4, 64, 128
ROPE_DIM, ROPE_THETA = 16, 500000.0
D_FF = 4096
N_DEV = 8
FF_TILE = D_FF // N_DEV
FF_WIDE = 1024
NEG = -1e30

C_GV, C_GR, C_SQ, C_GQ, C_GK, C_SK, C_SV, C_LR = 0, 512, 1024, 1536, 1792, 2048, 2176, 2304
DINP = 2432
DIN = 2320
O_GQ, O_GK, O_GV, O_GR, O_LR, O_SQ, O_SK, O_SV = (0, 256), (256, 512), (512, 1024), (1024, 1536), (1536, 1552), (1552, 2064), (2064, 2192), (2192, 2320)

ADAM_LR, ADAM_B1, ADAM_B2, ADAM_EPS, ADAM_WD, ADAM_STEP = 0.001, 0.9, 0.999, 1e-08, 0.01, 10

MESH = pl.DeviceIdType.MESH
ANY = pl.BlockSpec(memory_space=pl.ANY)
HIGHEST = lax.Precision.HIGHEST


def _cp(sem=None, vmem_mb=None):
    kw = {}
    if sem is not None:
        kw["dimension_semantics"] = sem
    if vmem_mb is not None:
        kw["vmem_limit_bytes"] = vmem_mb << 20
    return pltpu.CompilerParams(**kw)


def _mm(a, b):
    return jnp.dot(a.astype(MXU_DTYPE), b.astype(MXU_DTYPE), preferred_element_type=F32)


def _mm_nt(a, b):
    return lax.dot_general(a.astype(MXU_DTYPE), b.astype(MXU_DTYPE), (((1,), (1,)), ((), ())),
                           preferred_element_type=F32)


def _mm_tn(a, b):
    return lax.dot_general(a.astype(MXU_DTYPE), b.astype(MXU_DTYPE), (((0,), (0,)), ((), ())),
                           preferred_element_type=F32)


def _logsigmoid(z):
    return jnp.minimum(z, 0.0) - jnp.log(1.0 + jnp.exp(-jnp.abs(z)))


def _sigmoid(z):
    return 1.0 / (1.0 + jnp.exp(-z))


def _row_tile(rows):
    return 640 if rows % 640 == 0 else 128


def _mesh_pos():
    return lax.axis_index("x"), lax.axis_index("y"), lax.axis_index("c")


def _all_gather(shards):
    n = len(shards)

    def body(*refs):
        start, forward, finish = _gather_schedule(refs[:n], refs[n:2 * n], *refs[2 * n:])
        start()
        for j in range(3):
            forward(j)
        finish()

    gathered = pl.pallas_call(
        body, name="all_gather_weights",
        out_shape=_gathered_shapes(shards), in_specs=[ANY] * n, out_specs=[ANY] * n,
        scratch_shapes=_gather_sems(n),
    )(*shards)
    return _with_own_block(gathered, shards)


def _gathered_shapes(shards):
    return [jax.ShapeDtypeStruct((N_DEV,) + s.shape, s.dtype) for s in shards]


def _gather_sems(n):
    return [pltpu.SemaphoreType.DMA((7 * n,)), pltpu.SemaphoreType.DMA((7 * n,))]


def _with_own_block(gathered, shards):
    dev = 4 * lax.axis_index("x") + 2 * lax.axis_index("y") + lax.axis_index("c")
    return [lax.dynamic_update_index_in_dim(g, s, dev, 0) for g, s in zip(gathered, shards)]


def _gather_schedule(ins, outs, send_sems, recv_sems):
    n = len(ins)
    x, y, c = _mesh_pos()
    me, sibling = (x, y, c), (x, y, 1 - c)
    chips = [(1 - x, y), (x, 1 - y), (1 - x, 1 - y)]

    def copy(a, k, block, to, src=None):
        dst = outs[a].at[4 * block[0] + 2 * block[1] + block[2]]
        return pltpu.make_async_remote_copy(
            src_ref=dst if src is None else src, dst_ref=dst,
            send_sem=send_sems.at[a * 7 + k], recv_sem=recv_sems.at[a * 7 + k],
            device_id=to, device_id_type=MESH)

    def first(a):
        return [copy(a, 0, me, sibling, src=ins[a])] + [copy(a, 1 + j, me, (*chip, c), src=ins[a])
                                                        for j, chip in enumerate(chips)]

    def start():
        for a in range(n):
            for cp in first(a):
                cp.start()

    def forward(j):
        for a in range(n):
            copy(a, 1 + j, (*chips[j], c), me).wait_recv()
            copy(a, 4 + j, (*chips[j], c), sibling).start()

    def finish():
        for a in range(n):
            copy(a, 0, sibling, me).wait_recv()
            for j, chip in enumerate(chips):
                copy(a, 4 + j, (*chip, 1 - c), me).wait_recv()
        for a in range(n):
            for cp in first(a) + [copy(a, 4 + j, (*chip, c), sibling) for j, chip in enumerate(chips)]:
                cp.wait_send()

    return start, forward, finish


def _rs_sibling(gs):
    n = len(gs)

    def body(*refs):
        start, finish = _sibling_schedule(refs[:n], refs[n:2 * n], *refs[2 * n:])
        start()
        finish()

    return pl.pallas_call(
        body, name="reduce_scatter_sibling",
        out_shape=_sibling_shapes(gs), in_specs=[ANY] * n, out_specs=[ANY] * n,
        scratch_shapes=_sibling_sems(n),
    )(*gs)


def _sibling_shapes(gs):
    return [jax.ShapeDtypeStruct(g.shape[1:], g.dtype) for g in gs]


def _sibling_sems(n):
    return [pltpu.SemaphoreType.DMA((n,)), pltpu.SemaphoreType.DMA((n,))]


def _sibling_schedule(ins, land, send_sems, recv_sems):
    x, y, c = _mesh_pos()

    def copies():
        return [pltpu.make_async_remote_copy(
            src_ref=ins[a].at[1 - c], dst_ref=land[a], send_sem=send_sems.at[a], recv_sem=recv_sems.at[a],
            device_id=(x, y, 1 - c), device_id_type=MESH) for a in range(len(ins))]

    def start():
        for cp in copies():
            cp.start()

    def finish():
        for cp in copies():
            cp.wait_recv()
        for cp in copies():
            cp.wait_send()

    return start, finish


def _rs_chips(ps):
    n = len(ps)

    def body(*refs):
        start, finish = _chips_schedule(refs[:n], refs[n:2 * n], *refs[2 * n:])
        start()
        finish()

    return pl.pallas_call(
        body, name="reduce_scatter_chips",
        out_shape=_chips_shapes(ps), in_specs=[ANY] * n, out_specs=[ANY] * n,
        scratch_shapes=_chips_sems(n),
    )(*ps)


def _chips_shapes(ps):
    return [jax.ShapeDtypeStruct((3,) + p.shape[1:], p.dtype) for p in ps]


def _chips_sems(n):
    return [pltpu.SemaphoreType.DMA((3 * n,)), pltpu.SemaphoreType.DMA((3 * n,))]


def _chips_schedule(ins, land, send_sems, recv_sems):
    x, y, c = _mesh_pos()
    chips = [(1 - x, y), (x, 1 - y), (1 - x, 1 - y)]

    def copies():
        return [pltpu.make_async_remote_copy(
            src_ref=ins[a].at[2 * chip[0] + chip[1]], dst_ref=land[a].at[j],
            send_sem=send_sems.at[3 * a + j], recv_sem=recv_sems.at[3 * a + j],
            device_id=(*chip, c), device_id_type=MESH) for a in range(len(ins)) for j, chip in enumerate(chips)]

    def start():
        for cp in copies():
            cp.start()

    def finish():
        for cp in copies():
            cp.wait_recv()
        for cp in copies():
            cp.wait_send()

    return start, finish


def _all_reduce_small(pack):
    rows = pack.shape[0]

    def body(p_ref, out_ref, land, send_sems, recv_sems):
        x, y, c = _mesh_pos()
        me = 4 * x + 2 * y + c
        land[me] = p_ref[...]
        copies = []
        for k in range(1, N_DEV):
            bx, by, bc = (k >> 2) & 1, (k >> 1) & 1, k & 1
            peer = (1 - x if bx else x, 1 - y if by else y, 1 - c if bc else c)
            copies.append(pltpu.make_async_remote_copy(
                src_ref=p_ref, dst_ref=land.at[me], send_sem=send_sems.at[k - 1], recv_sem=recv_sems.at[k - 1],
                device_id=peer, device_id_type=MESH))
        for cp in copies:
            cp.start()
        for cp in copies:
            cp.wait_recv()
        for cp in copies:
            cp.wait_send()
        acc = land[0]
        for d in range(1, N_DEV):
            acc = acc + land[d]
        out_ref[...] = acc

    return pl.pallas_call(
        body, name="all_reduce_small",
        out_shape=jax.ShapeDtypeStruct(pack.shape, F32),
        in_specs=[pl.BlockSpec(memory_space=pltpu.VMEM)], out_specs=pl.BlockSpec(memory_space=pltpu.VMEM),
        scratch_shapes=[pltpu.VMEM((N_DEV, rows, 128), F32), pltpu.SemaphoreType.DMA((7,)),
                        pltpu.SemaphoreType.DMA((7,))],
    )(pack)


def _in_proj(h0, nw, win_p, tm):
    rows = h0.shape[0]

    def body(h_ref, nw_ref, w_ref, o_ref):
        h = h_ref[...]
        rstd = lax.rsqrt(jnp.mean(h * h, axis=-1, keepdims=True) + EPS)
        u = (h * rstd * nw_ref[...]).astype(MXU_DTYPE)
        o_ref[...] = jnp.dot(u, w_ref[...].astype(MXU_DTYPE), preferred_element_type=F32)

    return pl.pallas_call(
        body, name="in_proj", grid=(rows // tm,),
        in_specs=[pl.BlockSpec((tm, D), lambda i: (i, 0)), pl.BlockSpec((1, D), lambda i: (0, 0)),
                  pl.BlockSpec((D, DINP), lambda i: (0, 0))],
        out_specs=pl.BlockSpec((tm, DINP), lambda i: (i, 0)),
        out_shape=jax.ShapeDtypeStruct((rows, DINP), F32),
        compiler_params=_cp(("arbitrary",), 56),
    )(h0, nw, win_p)


def _rope_tables(rows):
    pos = (jnp.arange(rows, dtype=jnp.int32) - META0).astype(F32)
    inv_freq = 1.0 / (ROPE_THETA ** (jnp.arange(0, ROPE_DIM, 2, dtype=F32) / ROPE_DIM))
    ang = pos[:, None] * jnp.tile(inv_freq, 128 // (ROPE_DIM // 2))[None, :]
    in_head = jnp.arange(128, dtype=jnp.int32)[None, :] % SWA_HD
    cos, sin = jnp.cos(ang), jnp.sin(ang)
    c_tab = jnp.where(in_head < ROPE_DIM, cos, 1.0)
    sa_tab = jnp.where(in_head < ROPE_DIM // 2, -sin, 0.0)
    sb_tab = jnp.where((in_head >= ROPE_DIM // 2) & (in_head < ROPE_DIM), sin, 0.0)
    return c_tab, sa_tab, sb_tab


def _rope(xv, cos, sa, sb):
    width = xv.shape[1]
    reps = width // 128
    if reps > 1:
        cos, sa, sb = (jnp.tile(t, (1, reps)) for t in (cos, sa, sb))
    return xv * cos + pltpu.roll(xv, width - 8, 1) * sa + pltpu.roll(xv, 8, 1) * sb


def _unrope(dy, cos, sa, sb):
    width = dy.shape[1]
    reps = width // 128
    if reps > 1:
        cos, sa, sb = (jnp.tile(t, (1, reps)) for t in (cos, sa, sb))
    return dy * cos + pltpu.roll(dy * sa, 8, 1) + pltpu.roll(dy * sb, width - 8, 1)


def _swa_prep(proj, tabs, tm):
    rows = proj.shape[0]

    def body(q_ref, k_ref, v_ref, c_ref, sa_ref, sb_ref, qo_ref, ko_ref, vo_ref):
        cos, sa, sb = c_ref[...], sa_ref[...], sb_ref[...]
        qo_ref[...] = (_rope(q_ref[...], cos, sa, sb) * (SWA_HD ** -0.5)).astype(ACT_DTYPE)
        ko_ref[...] = _rope(k_ref[...], cos, sa, sb).astype(ACT_DTYPE)
        vo_ref[...] = v_ref[...].astype(ACT_DTYPE)

    tab_spec = pl.BlockSpec((tm, 128), lambda i: (i, 0))
    return pl.pallas_call(
        body, name="swa_prep", grid=(rows // tm,),
        in_specs=[pl.BlockSpec((tm, 512), lambda i: (i, C_SQ // 512)),
                  pl.BlockSpec((tm, 128), lambda i: (i, C_SK // 128)),
                  pl.BlockSpec((tm, 128), lambda i: (i, C_SV // 128)), tab_spec, tab_spec, tab_spec],
        out_specs=[pl.BlockSpec((tm, 512), lambda i: (i, 0)), tab_spec, tab_spec],
        out_shape=[jax.ShapeDtypeStruct((rows, 512), ACT_DTYPE), jax.ShapeDtypeStruct((rows, 128), ACT_DTYPE),
                   jax.ShapeDtypeStruct((rows, 128), ACT_DTYPE)],
        compiler_params=_cp(("arbitrary",)),
    )(proj, proj, proj, *tabs)


def _gla_group(nc):
    for g in (5, 2):
        if nc % g == 0:
            return g
    return 1


def _gla_gates(lr, wg, bg, first_row, nrows):
    zg = _mm(lr, wg) + bg
    row = first_row + lax.broadcasted_iota(jnp.int32, (nrows, 1), 0)
    live = row >= META0
    g = jnp.where(live, _logsigmoid(zg) * (1.0 / GLA_TAU), 0.0)
    ii = lax.broadcasted_iota(jnp.int32, (nrows, nrows), 0)
    jj = lax.broadcasted_iota(jnp.int32, (nrows, nrows), 1)
    same = (ii // GLA_CHUNK) == (jj // GLA_CHUNK)
    lower, upper = same & (jj <= ii), same & (jj >= ii)
    b = jnp.dot(lower.astype(F32), g, precision=HIGHEST, preferred_element_type=F32)
    return zg, live, lower, upper, b


def _tril64():
    ii = lax.broadcasted_iota(jnp.int32, (GLA_CHUNK, GLA_CHUNK), 0)
    jj = lax.broadcasted_iota(jnp.int32, (GLA_CHUNK, GLA_CHUNK), 1)
    return jj <= ii


def _gla_fwd(proj, wg_p, bg, gnw, shards):
    rows = proj.shape[0]
    nc = rows // GLA_CHUNK
    group = _gla_group(nc)
    steps, nrows = nc // group, group * GLA_CHUNK
    ns = len(shards)
    forward_steps = [steps * 7 // 10, steps * 8 // 10, steps * 9 // 10]

    def body(q_ref, k_ref, v_ref, r_ref, lr_ref, wg_ref, bg_ref, gnw_ref, *rest):
        shard_refs, rest = rest[:ns], rest[ns:]
        oraw_ref, og_ref, st_ref = rest[:3]
        gathered_refs, rest = rest[3:3 + ns], rest[3 + ns:]
        state, send_sems, recv_sems = rest
        c = pl.program_id(0)
        start, forward, finish = _gather_schedule(shard_refs, gathered_refs, send_sems, recv_sems)

        @pl.when(c == 0)
        def _():
            state[...] = jnp.zeros_like(state)
            start()

        for j, step in enumerate(forward_steps):
            pl.when(c == step)(functools.partial(forward, j))
        pl.when(c == steps - 1)(finish)

        _, _, _, _, b = _gla_gates(lr_ref[...], wg_ref[...], bg_ref[...], c * nrows, nrows)
        eb = jnp.exp(b)
        gq = q_ref[...] * (GLA_DK ** -0.5) * eb
        gk = k_ref[...] * jnp.exp(-b)
        v = v_ref[...]
        gnw_v = gnw_ref[...]
        tril = _tril64()
        for h in range(GLA_HEADS):
            s64 = slice(h * GLA_DK, (h + 1) * GLA_DK)
            s128 = slice(h * GLA_DV, (h + 1) * GLA_DV)
            st = state[h]
            for gi in range(group):
                rs = slice(gi * GLA_CHUNK, (gi + 1) * GLA_CHUNK)
                qh, kh, vh = gq[rs, s64], gk[rs, s64], v[rs, s128]
                eblh = eb[(gi + 1) * GLA_CHUNK - 1:(gi + 1) * GLA_CHUNK, s64]
                st_ref[gi, h] = st
                a = jnp.where(tril, _mm_nt(qh, kh), 0.0)
                o = _mm(a, vh) + _mm_nt(qh, st)
                st = st * eblh + _mm_tn(vh, kh * eblh)
                oraw_ref[rs, s128] = o
                rstd = lax.rsqrt(jnp.mean(o * o, axis=-1, keepdims=True) + EPS)
                rh = r_ref[rs, s128]
                og_ref[rs, s128] = (o * rstd * gnw_v * (rh * _sigmoid(rh))).astype(ACT_DTYPE)
            state[h] = st

    nb = lambda w, col: pl.BlockSpec((nrows, w), lambda c: (c, col // w))
    const = lambda shape: pl.BlockSpec(shape, lambda c: (0,) * len(shape))
    outs = pl.pallas_call(
        body, name="gla_fwd", grid=(steps,),
        in_specs=[nb(256, C_GQ), nb(256, C_GK), nb(512, C_GV), nb(512, C_GR), nb(128, C_LR),
                  const((128, 256)), const((1, 256)), const((1, 128))] + [ANY] * ns,
        out_specs=[pl.BlockSpec((nrows, 512), lambda c: (c, 0)), pl.BlockSpec((nrows, 512), lambda c: (c, 0)),
                   pl.BlockSpec((group, GLA_HEADS, GLA_DV, GLA_DK), lambda c: (c, 0, 0, 0))] + [ANY] * ns,
        out_shape=[jax.ShapeDtypeStruct((rows, 512), F32), jax.ShapeDtypeStruct((rows, 512), ACT_DTYPE),
                   jax.ShapeDtypeStruct((nc, GLA_HEADS, GLA_DV, GLA_DK), F32)] + _gathered_shapes(shards),
        scratch_shapes=[pltpu.VMEM((GLA_HEADS, GLA_DV, GLA_DK), F32)] + _gather_sems(ns),
        compiler_params=_cp(("arbitrary",)),
    )(proj, proj, proj, proj, proj, wg_p, bg, gnw, *shards)
    return outs[0], outs[1], outs[2], _with_own_block(outs[3:], shards)


def _swa_mask(n):
    shape = (SWA_GROUP * SWA_BLOCK, 3 * SWA_BLOCK)
    qi = lax.broadcasted_iota(jnp.int32, shape, 0) & (SWA_BLOCK - 1)
    jj = lax.broadcasted_iota(jnp.int32, shape, 1)
    meta = (jj < SWA_BLOCK) & (jj >= META0) & ((n > 0) | (jj <= qi))
    prev = (jj >= SWA_BLOCK) & (jj < 2 * SWA_BLOCK) & (n >= 2) & (jj - SWA_BLOCK > qi)
    cur = (jj >= 2 * SWA_BLOCK) & (n >= 1) & (jj - 2 * SWA_BLOCK <= qi)
    return meta | prev | cur


def _stack_heads(t, kvh):
    return jnp.concatenate([t[:, (kvh * SWA_GROUP + g) * SWA_HD:(kvh * SWA_GROUP + g + 1) * SWA_HD]
                            for g in range(SWA_GROUP)], axis=0)


def _stack_sinks(sink_ref, kvh):
    return jnp.concatenate([jnp.full((SWA_BLOCK, 1), sink_ref[0, kvh * SWA_GROUP + g], F32)
                            for g in range(SWA_GROUP)], axis=0)


def _swa_specs():
    blk = lambda w: pl.BlockSpec((SWA_BLOCK, w), lambda n: (n, 0))
    first = pl.BlockSpec((SWA_BLOCK, 128), lambda n: (0, 0))
    prev = pl.BlockSpec((SWA_BLOCK, 128), lambda n: (jnp.maximum(n - 1, 0), 0))
    return blk, first, prev


def _swa_fwd(qr, kr, vr, sinks):
    rows = qr.shape[0]
    nblk = rows // SWA_BLOCK

    def body(q_ref, k0, kp, kc, v0, vp, vc, sink_ref, o_ref):
        n = pl.program_id(0)
        q = q_ref[...]
        kall = jnp.concatenate([k0[...], kp[...], kc[...]], axis=0)
        vall = jnp.concatenate([v0[...], vp[...], vc[...]], axis=0)
        mask = _swa_mask(n)[0:SWA_BLOCK]
        outs = []
        for head in range(SWA_HEADS):
            kv = slice((head // SWA_GROUP) * SWA_HD, (head // SWA_GROUP + 1) * SWA_HD)
            s = jnp.where(mask, _mm_nt(q[:, head * SWA_HD:(head + 1) * SWA_HD], kall[:, kv]), NEG)
            sink = sink_ref[0, head]
            m = jnp.maximum(jnp.max(s, axis=-1, keepdims=True), sink)
            p = jnp.exp(s - m)
            den = jnp.sum(p, axis=-1, keepdims=True) + jnp.exp(sink - m)
            outs.append(_mm(p, vall[:, kv]) / den)
        o_ref[...] = jnp.concatenate(outs, axis=1).astype(ACT_DTYPE)

    blk, first, prev = _swa_specs()
    return pl.pallas_call(
        body, name="swa_fwd", grid=(nblk,),
        in_specs=[blk(512), first, prev, blk(128), first, prev, blk(128),
                  pl.BlockSpec(memory_space=pltpu.SMEM)],
        out_specs=blk(512),
        out_shape=jax.ShapeDtypeStruct((rows, 512), ACT_DTYPE),
        compiler_params=_cp(("arbitrary",)),
    )(qr, kr, kr, kr, vr, vr, vr, sinks)


def _out_proj(h0, og, osw, wout, nfw, tm):
    rows = h0.shape[0]

    def body(h_ref, og_ref, os_ref, w_ref, nw_ref, h1_ref, f_ref, ft_ref):
        h1 = h_ref[...] + _mm(og_ref[...], w_ref[0:512, :]) + _mm(os_ref[...], w_ref[512:1024, :])
        h1_ref[...] = h1
        rstd = lax.rsqrt(jnp.mean(h1 * h1, axis=-1, keepdims=True) + EPS)
        f = h1 * rstd * nw_ref[...]
        f_ref[...] = f.astype(ACT_DTYPE)
        ft_ref[...] = f.T.astype(ACT_DTYPE)

    row = lambda w: pl.BlockSpec((tm, w), lambda i: (i, 0))
    return pl.pallas_call(
        body, name="out_proj", grid=(rows // tm,),
        in_specs=[row(D), row(512), row(512), pl.BlockSpec((D, D), lambda i: (0, 0)), pl.BlockSpec((1, D), lambda i: (0, 0))],
        out_specs=[row(D), row(D), pl.BlockSpec((D, tm), lambda i: (0, i))],
        out_shape=[jax.ShapeDtypeStruct((rows, D), F32), jax.ShapeDtypeStruct((rows, D), ACT_DTYPE),
                   jax.ShapeDtypeStruct((D, rows), ACT_DTYPE)],
        compiler_params=_cp(("arbitrary",), 48),
    )(h0, og, osw, wout, nfw)


def _ffn_fwd(f, h1, w1, w2, tgt, fnw, tm):
    rows = f.shape[0]
    nj = D_FF // FF_WIDE

    def body(f_ref, h1_ref, w1_ref, w2_ref, t_ref, nw_ref, a_ref, dh2_ref, dh2t_ref, loss_ref, gfn_ref, acc):
        i, j = pl.program_id(0), pl.program_id(1)

        @pl.when((i == 0) & (j == 0))
        def _():
            loss_ref[...] = jnp.zeros_like(loss_ref)
            gfn_ref[...] = jnp.zeros_like(gfn_ref)

        @pl.when(j == 0)
        def _():
            acc[...] = jnp.zeros_like(acc)

        a = _mm(f_ref[...], w1_ref[...])
        a_ref[...] = a.astype(ACT_DTYPE)
        z = jnp.square(jnp.maximum(a, 0.0))
        acc[...] += _mm(z, w2_ref[...])

        @pl.when(j == nj - 1)
        def _():
            h2 = h1_ref[...] + acc[...]
            rstd = lax.rsqrt(jnp.mean(h2 * h2, axis=-1, keepdims=True) + EPS)
            hn = h2 * rstd
            nw = nw_ref[...]
            row = i * tm + lax.broadcasted_iota(jnp.int32, (tm, 1), 0)
            err = jnp.where(row >= LEAD, hn * nw - t_ref[...], 0.0)
            row_loss = jnp.sum(err * err, axis=-1, keepdims=True) * (1.0 / D)
            loss_ref[...] += jnp.broadcast_to(0.5 * jnp.sum(row_loss, axis=0, keepdims=True), loss_ref.shape)
            dy = err * (1.0 / D)
            gfn_ref[...] += jnp.broadcast_to(jnp.sum(dy * hn, axis=0, keepdims=True), gfn_ref.shape)
            dhn = dy * nw
            dh2 = rstd * (dhn - hn * jnp.mean(dhn * hn, axis=-1, keepdims=True))
            dh2_ref[...] = dh2
            dh2t_ref[...] = dh2.T.astype(ACT_DTYPE)

    return pl.pallas_call(
        body, name="ffn_fwd", grid=(rows // tm, nj),
        in_specs=[pl.BlockSpec((tm, D), lambda i, j: (i, 0)), pl.BlockSpec((tm, D), lambda i, j: (i, 0)),
                  pl.BlockSpec((D, FF_WIDE), lambda i, j: (0, j)),
                  pl.BlockSpec((FF_WIDE, D), lambda i, j: (j, 0)),
                  pl.BlockSpec((tm, D), lambda i, j: (i, 0)), pl.BlockSpec((1, D), lambda i, j: (0, 0))],
        out_specs=[pl.BlockSpec((tm, FF_WIDE), lambda i, j: (i, j)), pl.BlockSpec((tm, D), lambda i, j: (i, 0)),
                   pl.BlockSpec((D, tm), lambda i, j: (0, i)),
                   pl.BlockSpec((8, 128), lambda i, j: (0, 0)), pl.BlockSpec((8, D), lambda i, j: (0, 0))],
        out_shape=[jax.ShapeDtypeStruct((rows, D_FF), ACT_DTYPE), jax.ShapeDtypeStruct((rows, D), F32),
                   jax.ShapeDtypeStruct((D, rows), ACT_DTYPE),
                   jax.ShapeDtypeStruct((8, 128), F32), jax.ShapeDtypeStruct((8, D), F32)],
        scratch_shapes=[pltpu.VMEM((tm, D), F32)],
        compiler_params=_cp(("arbitrary", "arbitrary"), 56),
    )(f, h1, w1, w2, tgt, fnw)


def _ffn_bwd_act(dh2, a, w1, w2, h1, nfw, tm):
    rows = dh2.shape[0]
    nj = D_FF // FF_WIDE

    def body(dh2_ref, a_ref, w1_ref, w2_ref, h1_ref, nw_ref, da_ref, dh1_ref, gnf_ref, acc):
        i, j = pl.program_id(0), pl.program_id(1)

        @pl.when((i == 0) & (j == 0))
        def _():
            gnf_ref[...] = jnp.zeros_like(gnf_ref)

        @pl.when(j == 0)
        def _():
            acc[...] = jnp.zeros_like(acc)

        dz = _mm_nt(dh2_ref[...], w2_ref[...])
        da = dz * (2.0 * jnp.maximum(a_ref[...].astype(F32), 0.0))
        da_ref[...] = da.astype(ACT_DTYPE)
        acc[...] += _mm_nt(da, w1_ref[...])

        @pl.when(j == nj - 1)
        def _():
            h1 = h1_ref[...]
            rstd = lax.rsqrt(jnp.mean(h1 * h1, axis=-1, keepdims=True) + EPS)
            hn = h1 * rstd
            df = acc[...]
            gnf_ref[...] += jnp.broadcast_to(jnp.sum(df * hn, axis=0, keepdims=True), gnf_ref.shape)
            dfn = df * nw_ref[...]
            dh1_ref[...] = dh2_ref[...] + rstd * (dfn - hn * jnp.mean(dfn * hn, axis=-1, keepdims=True))

    return pl.pallas_call(
        body, name="ffn_bwd_act", grid=(rows // tm, nj),
        in_specs=[pl.BlockSpec((tm, D), lambda i, j: (i, 0)), pl.BlockSpec((tm, FF_WIDE), lambda i, j: (i, j)),
                  pl.BlockSpec((D, FF_WIDE), lambda i, j: (0, j)),
                  pl.BlockSpec((FF_WIDE, D), lambda i, j: (j, 0)),
                  pl.BlockSpec((tm, D), lambda i, j: (i, 0)), pl.BlockSpec((1, D), lambda i, j: (0, 0))],
        out_specs=[pl.BlockSpec((tm, FF_WIDE), lambda i, j: (i, j)), pl.BlockSpec((tm, D), lambda i, j: (i, 0)),
                   pl.BlockSpec((8, D), lambda i, j: (0, 0))],
        out_shape=[jax.ShapeDtypeStruct((rows, D_FF), ACT_DTYPE), jax.ShapeDtypeStruct((rows, D), F32),
                   jax.ShapeDtypeStruct((8, D), F32)],
        scratch_shapes=[pltpu.VMEM((tm, D), F32)],
        compiler_params=_cp(("arbitrary", "arbitrary"), 56),
    )(dh2, a, w1, w2, h1, nfw)


def _ffn_bwd_weights(ft, a, da, dh2t, tm):
    rows = a.shape[0]
    steps = rows // tm

    def body(ft_ref, a_ref, da_ref, dh2t_ref, dw1_ref, dw2_ref, dw2t):
        i = pl.program_id(1)

        @pl.when(i == 0)
        def _():
            dw1_ref[...] = jnp.zeros_like(dw1_ref)
            dw2t[...] = jnp.zeros_like(dw2t)

        z = jnp.square(jnp.maximum(a_ref[...].astype(F32), 0.0))
        dw1_ref[...] += _mm(ft_ref[...], da_ref[...])
        dw2t[...] += _mm(dh2t_ref[...], z)

        @pl.when(i == steps - 1)
        def _():
            dw2_ref[...] = dw2t[...].T

    return pl.pallas_call(
        body, name="ffn_bwd_weights", grid=(N_DEV, steps),
        in_specs=[pl.BlockSpec((D, tm), lambda j, i: (0, i)), pl.BlockSpec((tm, FF_TILE), lambda j, i: (i, j)),
                  pl.BlockSpec((tm, FF_TILE), lambda j, i: (i, j)), pl.BlockSpec((D, tm), lambda j, i: (0, i))],
        out_specs=[pl.BlockSpec((None, None, D, FF_TILE), lambda j, i: (j % 2, j // 2, 0, 0)),
                   pl.BlockSpec((None, None, FF_TILE, D), lambda j, i: (j % 2, j // 2, 0, 0))],
        out_shape=[jax.ShapeDtypeStruct((2, 4, D, FF_TILE), F32), jax.ShapeDtypeStruct((2, 4, FF_TILE, D), F32)],
        scratch_shapes=[pltpu.VMEM((D, FF_TILE), F32)],
        compiler_params=_cp(("arbitrary", "arbitrary"), 48),
    )(ft, a, da, dh2t)


def _out_proj_bwd(dh1, og, osw, wout, tm, partials):
    rows = dh1.shape[0]
    steps = rows // tm
    ns = len(partials)

    def body(dh1_ref, og_ref, os_ref, w_ref, *rest):
        part_refs, rest = rest[:ns], rest[ns:]
        dog_ref, dos_ref, dw_ref = rest[:3]
        land_refs, (send_sems, recv_sems) = rest[3:3 + ns], rest[3 + ns:]
        i = pl.program_id(0)
        start, finish = _sibling_schedule(part_refs, land_refs, send_sems, recv_sems)

        @pl.when(i == 0)
        def _():
            dw_ref[...] = jnp.zeros_like(dw_ref)
            start()

        pl.when(i == steps - 1)(finish)

        dh1 = dh1_ref[...].astype(MXU_DTYPE)
        dog_ref[...] = _mm_nt(dh1, w_ref[0:512, :])
        dos_ref[...] = _mm_nt(dh1, w_ref[512:1024, :])
        for half, ref in enumerate((og_ref, os_ref)):
            dw = _mm_tn(ref[...], dh1)
            for blk in range(4):
                shard = half * 4 + blk
                dw_ref[shard % 2, shard // 2] += dw[blk * 128:(blk + 1) * 128, :]

    row = lambda w: pl.BlockSpec((tm, w), lambda i: (i, 0))
    outs = pl.pallas_call(
        body, name="out_proj_bwd", grid=(steps,),
        in_specs=[row(D), row(512), row(512), pl.BlockSpec((D, D), lambda i: (0, 0))] + [ANY] * ns,
        out_specs=[row(512), row(512), pl.BlockSpec((2, 4, 128, D), lambda i: (0, 0, 0, 0))] + [ANY] * ns,
        out_shape=[jax.ShapeDtypeStruct((rows, 512), F32), jax.ShapeDtypeStruct((rows, 512), F32),
                   jax.ShapeDtypeStruct((2, 4, 128, D), F32)] + _sibling_shapes(partials),
        scratch_shapes=_sibling_sems(ns),
        compiler_params=_cp(("arbitrary",), 48),
    )(dh1, og, osw, wout, *partials)
    return outs[0], outs[1], outs[2], outs[3:]


def _swa_bwd(qr, kr, vr, osw, dos, sinks, chip_sums):
    rows = qr.shape[0]
    nblk = rows // SWA_BLOCK
    ns = len(chip_sums)

    def body(q_ref, k0, kp, kc, v0, vp, vc, o_ref, do_ref, sink_ref, *rest):
        sum_refs, rest = rest[:ns], rest[ns:]
        dq_ref, dk_ref, dv_ref, dsink_ref = rest[:4]
        land_refs, (send_sems, recv_sems) = rest[4:4 + ns], rest[4 + ns:]
        n = pl.program_id(0)
        start, finish = _chips_schedule(sum_refs, land_refs, send_sems, recv_sems)

        @pl.when(n == 0)
        def _():
            dk_ref[...] = jnp.zeros_like(dk_ref)
            dv_ref[...] = jnp.zeros_like(dv_ref)
            dsink_ref[...] = jnp.zeros_like(dsink_ref)
            start()

        pl.when(n == nblk - 1)(finish)

        q = q_ref[...]
        kall = jnp.concatenate([k0[...], kp[...], kc[...]], axis=0)
        vall = jnp.concatenate([v0[...], vp[...], vc[...]], axis=0)
        mask = _swa_mask(n)
        do_all = do_ref[...]
        o_all = o_ref[...].astype(F32)
        dq, dk, dv = [], [], []
        for kvh in range(SWA_KV):
            kv = slice(kvh * SWA_HD, (kvh + 1) * SWA_HD)
            q4, do4, o4 = _stack_heads(q, kvh), _stack_heads(do_all, kvh), _stack_heads(o_all, kvh)
            sink4 = _stack_sinks(sink_ref, kvh)
            s = jnp.where(mask, _mm_nt(q4, kall[:, kv]), NEG)
            m = jnp.maximum(jnp.max(s, axis=-1, keepdims=True), sink4)
            e = jnp.exp(s - m)
            inv = 1.0 / (jnp.sum(e, axis=-1, keepdims=True) + jnp.exp(sink4 - m))
            p = e * inv
            delta = jnp.sum(do4 * o4, axis=-1, keepdims=True)
            ds = p * (_mm_nt(do4, vall[:, kv]) - delta)
            dq4 = _mm(ds, kall[:, kv])
            dq += [dq4[g * SWA_BLOCK:(g + 1) * SWA_BLOCK] for g in range(SWA_GROUP)]
            dk.append(_mm_tn(ds, q4))
            dv.append(_mm_tn(p, do4))
            sink_term = jnp.exp(sink4 - m) * inv * delta
            for g in range(SWA_GROUP):
                head = kvh * SWA_GROUP + g
                dsink = -jnp.sum(sink_term[g * SWA_BLOCK:(g + 1) * SWA_BLOCK], axis=0, keepdims=True)
                dsink_ref[head:head + 1, :] += jnp.broadcast_to(dsink, (1, 128))
        dq_ref[...] = jnp.concatenate(dq, axis=1)
        dk_all = jnp.concatenate(dk, axis=1)
        dv_all = jnp.concatenate(dv, axis=1)
        prev0 = pl.multiple_of(jnp.maximum(n - 1, 0) * SWA_BLOCK, SWA_BLOCK)
        cur0 = pl.multiple_of(n * SWA_BLOCK, SWA_BLOCK)
        for ref, val in ((dk_ref, dk_all), (dv_ref, dv_all)):
            ref[0:SWA_BLOCK, :] += val[0:SWA_BLOCK]
            ref[pl.ds(prev0, SWA_BLOCK), :] += val[SWA_BLOCK:2 * SWA_BLOCK]
            ref[pl.ds(cur0, SWA_BLOCK), :] += val[2 * SWA_BLOCK:]

    blk, first, prev = _swa_specs()
    whole = pl.BlockSpec((rows, 128), lambda n: (0, 0))
    outs = pl.pallas_call(
        body, name="swa_bwd", grid=(nblk,),
        in_specs=[blk(512), first, prev, blk(128), first, prev, blk(128), blk(512), blk(512),
                  pl.BlockSpec(memory_space=pltpu.SMEM)] + [ANY] * ns,
        out_specs=[blk(512), whole, whole, pl.BlockSpec((8, 128), lambda n: (0, 0))] + [ANY] * ns,
        out_shape=[jax.ShapeDtypeStruct((rows, 512), F32), jax.ShapeDtypeStruct((rows, 128), F32),
                   jax.ShapeDtypeStruct((rows, 128), F32), jax.ShapeDtypeStruct((8, 128), F32)] + _chips_shapes(chip_sums),
        scratch_shapes=_chips_sems(ns),
        compiler_params=_cp(("arbitrary",), 48),
    )(qr, kr, kr, kr, vr, vr, vr, osw, dos, sinks, *chip_sums)
    return outs[0], outs[1], outs[2], outs[3], outs[4:]


def _gla_bwd(proj, oraw, states, dog, wg_p, bg, gnw):
    rows = proj.shape[0]
    nc = rows // GLA_CHUNK
    group = _gla_group(nc)
    steps, nrows = nc // group, group * GLA_CHUNK

    def body(q_ref, k_ref, v_ref, r_ref, lr_ref, oraw_ref, st_ref, dog_ref, wg_ref, bg_ref, gnw_ref,
             dq_ref, dk_ref, dv_ref, dr_ref, dlr_ref, dwg_ref, dbg_ref, dgnw_ref, dstate, db_scr):
        t = pl.program_id(0)
        c = steps - 1 - t

        @pl.when(t == 0)
        def _():
            dstate[...] = jnp.zeros_like(dstate)
            dwg_ref[...] = jnp.zeros_like(dwg_ref)
            dbg_ref[...] = jnp.zeros_like(dbg_ref)
            dgnw_ref[...] = jnp.zeros_like(dgnw_ref)

        lr, wg = lr_ref[...], wg_ref[...]
        zg, live, _, upper, b = _gla_gates(lr, wg, bg_ref[...], c * nrows, nrows)
        eb, enb = jnp.exp(b), jnp.exp(-b)
        scale = GLA_DK ** -0.5
        gq = q_ref[...] * scale * eb
        gk = k_ref[...] * enb
        v = v_ref[...]
        gnw_v = gnw_ref[...]
        tril = _tril64()
        is_last = lax.broadcasted_iota(jnp.int32, (GLA_CHUNK, 1), 0) == GLA_CHUNK - 1
        dgnw = jnp.zeros((1, GLA_DV), F32)
        for h in range(GLA_HEADS):
            s64 = slice(h * GLA_DK, (h + 1) * GLA_DK)
            s128 = slice(h * GLA_DV, (h + 1) * GLA_DV)
            dsp = dstate[h]
            for gi in reversed(range(group)):
                rs = slice(gi * GLA_CHUNK, (gi + 1) * GLA_CHUNK)
                qh, kh, vh = gq[rs, s64], gk[rs, s64], v[rs, s128]
                ebh, enbh = eb[rs, s64], enb[rs, s64]
                eblh = eb[(gi + 1) * GLA_CHUNK - 1:(gi + 1) * GLA_CHUNK, s64]
                klh = kh * eblh
                st = st_ref[gi, h]
                o, rh, dout = oraw_ref[rs, s128], r_ref[rs, s128], dog_ref[rs, s128]
                rstd = lax.rsqrt(jnp.mean(o * o, axis=-1, keepdims=True) + EPS)
                on = o * rstd
                sg = _sigmoid(rh)
                dr_ref[rs, s128] = (dout * (on * gnw_v) * (sg * (1.0 + rh * (1.0 - sg)))).astype(ACT_DTYPE)
                dy = dout * (rh * sg)
                dgnw = dgnw + jnp.sum(dy * on, axis=0, keepdims=True)
                don = dy * gnw_v
                do = rstd * (don - on * jnp.mean(don * on, axis=-1, keepdims=True))
                a = jnp.where(tril, _mm_nt(qh, kh), 0.0)
                da = jnp.where(tril, _mm_nt(do, vh), 0.0)
                dkl = _mm(vh, dsp)
                dv_ref[rs, s128] = (_mm_tn(a, do) + _mm_nt(klh, dsp)).astype(ACT_DTYPE)
                debl = jnp.sum(dsp * st, axis=0, keepdims=True)
                dgq = _mm(da, kh) + _mm(do, st)
                dgk = _mm_tn(da, qh)
                dsp = dsp * eblh + _mm_tn(do, qh)
                dq_ref[rs, s64] = (dgq * (scale * ebh)).astype(ACT_DTYPE)
                dk_ref[rs, s64] = ((dgk + dkl * eblh) * enbh).astype(ACT_DTYPE)
                last = debl * eblh + jnp.sum(dkl * klh, axis=0, keepdims=True)
                db_scr[rs, s64] = dgq * qh - dgk * kh - dkl * klh + jnp.where(is_last, last, 0.0)
            dstate[h] = dsp
        dg = jnp.dot(upper.astype(F32), db_scr[...], precision=HIGHEST, preferred_element_type=F32)
        dzg = jnp.where(live, dg * _sigmoid(-zg) * (1.0 / GLA_TAU), 0.0)
        dlr_ref[...] = _mm_nt(dzg, wg).astype(ACT_DTYPE)
        dwg_ref[...] += _mm_tn(lr, dzg)
        dbg_ref[...] += jnp.broadcast_to(jnp.sum(dzg, axis=0, keepdims=True), dbg_ref.shape)
        dgnw_ref[...] += jnp.broadcast_to(dgnw, dgnw_ref.shape)

    nb = lambda w, col: pl.BlockSpec((nrows, w), lambda t: (steps - 1 - t, col // w))
    const = lambda shape: pl.BlockSpec(shape, lambda t: (0,) * len(shape))
    return pl.pallas_call(
        body, name="gla_bwd", grid=(steps,),
        in_specs=[nb(256, C_GQ), nb(256, C_GK), nb(512, C_GV), nb(512, C_GR), nb(128, C_LR), nb(512, 0),
                  pl.BlockSpec((group, GLA_HEADS, GLA_DV, GLA_DK), lambda t: (steps - 1 - t, 0, 0, 0)), nb(512, 0),
                  const((128, 256)), const((1, 256)), const((1, 128))],
        out_specs=[nb(256, 0), nb(256, 0), nb(512, 0), nb(512, 0), nb(128, 0),
                   const((128, 256)), const((8, 256)), const((8, 128))],
        out_shape=[jax.ShapeDtypeStruct((rows, 256), ACT_DTYPE), jax.ShapeDtypeStruct((rows, 256), ACT_DTYPE),
                   jax.ShapeDtypeStruct((rows, 512), ACT_DTYPE), jax.ShapeDtypeStruct((rows, 512), ACT_DTYPE),
                   jax.ShapeDtypeStruct((rows, 128), ACT_DTYPE), jax.ShapeDtypeStruct((128, 256), F32),
                   jax.ShapeDtypeStruct((8, 256), F32), jax.ShapeDtypeStruct((8, 128), F32)],
        scratch_shapes=[pltpu.VMEM((GLA_HEADS, GLA_DV, GLA_DK), F32), pltpu.VMEM((nrows, 256), F32)],
        compiler_params=_cp(("arbitrary",)),
    )(proj, proj, proj, proj, proj, oraw, states, dog, wg_p, bg, gnw)


def _in_proj_bwd(h0, dh1, nw, win_p, dgv, dgr, dsq, dgq, dgk, dsk, dsv, dlr, tabs, tm):
    rows = h0.shape[0]

    def body(h_ref, dh1_ref, nw_ref, w_ref, dgv_ref, dgr_ref, dsq_ref, dgq_ref, dgk_ref, dsk_ref, dsv_ref, dlr_ref,
             c_ref, sa_ref, sb_ref, dh0_ref, dw_ref, gnm_ref):
        i = pl.program_id(0)

        @pl.when(i == 0)
        def _():
            dw_ref[...] = jnp.zeros_like(dw_ref)
            gnm_ref[...] = jnp.zeros_like(gnm_ref)

        cos, sa, sb = c_ref[...], sa_ref[...], sb_ref[...]
        dsq_v = (_unrope(dsq_ref[...], cos, sa, sb) * (SWA_HD ** -0.5)).astype(MXU_DTYPE)
        dsk_v = _unrope(dsk_ref[...], cos, sa, sb).astype(MXU_DTYPE)
        dproj = jnp.concatenate(
            [dgv_ref[...].astype(MXU_DTYPE), dgr_ref[...].astype(MXU_DTYPE), dsq_v, dgq_ref[...].astype(MXU_DTYPE),
             dgk_ref[...].astype(MXU_DTYPE), dsk_v, dsv_ref[...].astype(MXU_DTYPE), dlr_ref[...].astype(MXU_DTYPE)],
            axis=1)
        h = h_ref[...]
        rstd = lax.rsqrt(jnp.mean(h * h, axis=-1, keepdims=True) + EPS)
        hn = h * rstd
        nw_v = nw_ref[...]
        u = (hn * nw_v).astype(MXU_DTYPE)
        du = _mm_nt(dproj, w_ref[...])
        dw_ref[...] += _mm_tn(u, dproj)
        gnm_ref[...] += jnp.broadcast_to(jnp.sum(du * hn, axis=0, keepdims=True), gnm_ref.shape)
        dun = du * nw_v
        dh0_ref[...] = dh1_ref[...] + rstd * (dun - hn * jnp.mean(dun * hn, axis=-1, keepdims=True))

    row = lambda w: pl.BlockSpec((tm, w), lambda i: (i, 0))
    return pl.pallas_call(
        body, name="in_proj_bwd", grid=(rows // tm,),
        in_specs=[row(D), row(D), pl.BlockSpec((1, D), lambda i: (0, 0)), pl.BlockSpec((D, DINP), lambda i: (0, 0)),
                  row(512), row(512), row(512), row(256), row(256), row(128), row(128), row(128),
                  row(128), row(128), row(128)],
        out_specs=[row(D), pl.BlockSpec((D, DINP), lambda i: (0, 0)), pl.BlockSpec((8, D), lambda i: (0, 0))],
        out_shape=[jax.ShapeDtypeStruct((rows, D), F32), jax.ShapeDtypeStruct((D, DINP), F32),
                   jax.ShapeDtypeStruct((8, D), F32)],
        compiler_params=_cp(("arbitrary",), 56),
    )(h0, dh1, nw, win_p, dgv, dgr, dsq, dgq, dgk, dsk, dsv, dlr, *tabs)


def _adamw(w, g, m, v):
    m = ADAM_B1 * m + (1.0 - ADAM_B1) * g
    v = ADAM_B2 * v + (1.0 - ADAM_B2) * jnp.square(g)
    m_hat = m / (1.0 - ADAM_B1 ** ADAM_STEP)
    v_hat = v / (1.0 - ADAM_B2 ** ADAM_STEP)
    delta = -ADAM_LR * (m_hat / (jnp.sqrt(v_hat) + ADAM_EPS) + ADAM_WD * w)
    return delta, m, v


def _adamw_shard(where, parts, own, w, m, v, name):
    r, cdim = w.shape
    tr = 128 if r % 128 == 0 else r

    def body(where_ref, p_ref, own_ref, w_ref, m_ref, v_ref, g_ref, d_ref, nm_ref, nv_ref):
        g = ((p_ref[0] + p_ref[1]) + p_ref[2]) + own_ref[...]
        g_ref[...] = g
        d_ref[...], nm_ref[...], nv_ref[...] = _adamw(w_ref[...], g, m_ref[...], v_ref[...])

    spec = pl.BlockSpec((tr, cdim), lambda i, s: (i, 0))
    shape = jax.ShapeDtypeStruct((r, cdim), F32)
    return pl.pallas_call(
        body, name=name,
        grid_spec=pltpu.PrefetchScalarGridSpec(
            num_scalar_prefetch=1, grid=(r // tr,),
            in_specs=[pl.BlockSpec((3, tr, cdim), lambda i, s: (0, i, 0)),
                      pl.BlockSpec((None, tr, cdim), lambda i, s: (s[1], i, 0)), spec, spec, spec],
            out_specs=[spec] * 4),
        out_shape=[shape] * 4,
        compiler_params=_cp(("arbitrary",)),
    )(where, parts, own, w, m, v)


def _adamw_small(w, g, m, v):
    def body(w_ref, g_ref, m_ref, v_ref, d_ref, nm_ref, nv_ref):
        d_ref[...], nm_ref[...], nv_ref[...] = _adamw(w_ref[...], g_ref[...], m_ref[...], v_ref[...])

    vm = pl.BlockSpec(memory_space=pltpu.VMEM)
    shape = jax.ShapeDtypeStruct(w.shape, F32)
    return pl.pallas_call(body, name="adamw_small", in_specs=[vm] * 4, out_specs=[vm] * 3,
                          out_shape=[shape] * 3)(w, g, m, v)


def _add_own_half(where, full, theirs, name):
    _, _, r, cdim = full.shape
    tr = 128 if r % 128 == 0 else r

    def body(where_ref, a_ref, b_ref, o_ref):
        o_ref[...] = a_ref[...] + b_ref[...]

    spec = pl.BlockSpec((4, tr, cdim), lambda i, s: (0, i, 0))
    return pl.pallas_call(
        body, name=name,
        grid_spec=pltpu.PrefetchScalarGridSpec(
            num_scalar_prefetch=1, grid=(r // tr,),
            in_specs=[pl.BlockSpec((None, 4, tr, cdim), lambda i, s: (s[0], 0, i, 0)), spec], out_specs=spec),
        out_shape=jax.ShapeDtypeStruct(theirs.shape, F32), compiler_params=_cp(("arbitrary",)))(where, full, theirs)


def _to_rows128(a):
    return a.reshape(-1, 128)


def _pad_rows128(vec):
    flat = vec.reshape(-1)
    n = -(-flat.shape[0] // 128)
    return jnp.pad(flat, (0, n * 128 - flat.shape[0])).reshape(n, 128)


def kernel(x, meta_tokens, norm_mix_w, w_in, w_gate_up, b_gate, gla_norm_w, sinks, w_out, norm_ff_w, w_ff1, w_ff2, final_norm_w, loss_target, m_meta_tokens, m_norm_mix_w, m_w_in, m_w_gate_up, m_b_gate, m_gla_norm_w, m_sinks, m_w_out, m_norm_ff_w, m_w_ff1, m_w_ff2, m_final_norm_w, v_meta_tokens, v_norm_mix_w, v_w_in, v_w_gate_up, v_b_gate, v_gla_norm_w, v_sinks, v_w_out, v_norm_ff_w, v_w_ff1, v_w_ff2, v_final_norm_w):
    seq = x.shape[1]
    rows = LEAD + seq
    tm = _row_tile(rows)
    tm_small = tm // 2 if tm == 640 else tm
    dev = 4 * lax.axis_index("x") + 2 * lax.axis_index("y") + lax.axis_index("c")

    small_shard = jnp.concatenate([meta_tokens, w_gate_up[0], jnp.zeros((N_META, 96), F32)], axis=1)
    g_in, g_small = _all_gather([w_in[0].astype(WIRE_DTYPE), small_shard])
    later_shards = [w_out[0].astype(WIRE_DTYPE), w_ff1[0].astype(WIRE_DTYPE), w_ff2[0].astype(WIRE_DTYPE)]
    win_full = jnp.transpose(g_in, (1, 0, 2)).reshape(D, DIN)
    cols = lambda r: win_full[:, r[0]:r[1]]
    win_p = jnp.concatenate([cols(O_GV), cols(O_GR), cols(O_SQ), cols(O_GQ), cols(O_GK), cols(O_SK), cols(O_SV),
                             cols(O_LR), jnp.zeros((D, 128 - GLA_RANK), WIRE_DTYPE)], axis=1)
    meta_full = jnp.transpose(g_small[:, :, 0:128], (1, 0, 2)).reshape(N_META, D)
    wg_full = jnp.transpose(g_small[:, :, 128:160], (1, 0, 2)).reshape(GLA_RANK, GLA_HEADS * GLA_DK)
    wg_p = jnp.concatenate([wg_full, jnp.zeros((128 - GLA_RANK, 256), F32)], axis=0)

    h0 = jnp.concatenate([jnp.zeros((META0, D), F32), meta_full, x[0]], axis=0)
    tgt = jnp.concatenate([jnp.zeros((LEAD, D), F32), loss_target[0]], axis=0)
    tabs = _rope_tables(rows)
    proj = _in_proj(h0, norm_mix_w, win_p, tm)
    oraw, og, states, (g_out, g_w1, g_w2) = _gla_fwd(proj, wg_p, b_gate, gla_norm_w, later_shards)
    wout_full = g_out.reshape(D, D)
    w2_full = g_w2.reshape(D_FF, D)
    w1_full = jnp.transpose(g_w1, (1, 0, 2)).reshape(D, D_FF)
    qr, kr, vr = _swa_prep(proj, tabs, tm)
    osw = _swa_fwd(qr, kr, vr, sinks)
    h1, f, ft = _out_proj(h0, og, osw, wout_full, norm_ff_w, tm)
    a, dh2, dh2t, loss_p, gfn_p = _ffn_fwd(f, h1, w1_full, w2_full, tgt, final_norm_w.reshape(1, D), tm)

    da, dh1, gnf_p = _ffn_bwd_act(dh2, a, w1_full, w2_full, h1, norm_ff_w, tm)
    dw1, dw2 = _ffn_bwd_weights(ft, a, da, dh2t, 1664 if rows % 1664 == 0 else tm)
    where = jnp.stack([lax.axis_index("c"), 2 * lax.axis_index("x") + lax.axis_index("y")]).astype(jnp.int32)
    dog, dos, dwout, theirs_ffn = _out_proj_bwd(dh1, og, osw, wout_full, tm, [dw1, dw2])
    sums_ffn = [_add_own_half(where, p, q, "reduce_pair_%d" % (2 + k))
                for k, (p, q) in enumerate(zip([dw1, dw2], theirs_ffn))]
    dsq, dsk, dsv, dsink_p, parts_ffn = _swa_bwd(qr, kr, vr, osw, dos, sinks, sums_ffn)
    dgq, dgk, dgv, dgr, dlr, dwg_p, dbg_p, dgnw_p = _gla_bwd(proj, oraw, states, dog, wg_p, b_gate, gla_norm_w)
    dh0, dwin_p, gnm_p = _in_proj_bwd(h0, dh1, norm_mix_w, win_p, dgv, dgr, dsq, dgq, dgk, dsk, dsv, dlr, tabs, tm_small)
    grad_x = dh0[LEAD:][None]

    pcols = lambda c0, r: dwin_p[:, c0:c0 + (r[1] - r[0])]
    dwin = jnp.concatenate([pcols(C_GQ, O_GQ), pcols(C_GK, O_GK), pcols(C_GV, O_GV), pcols(C_GR, O_GR),
                            pcols(C_LR, O_LR), pcols(C_SQ, O_SQ), pcols(C_SK, O_SK), pcols(C_SV, O_SV)], axis=1)
    dwin = jnp.transpose(dwin.reshape(D, 4, 2, DIN // N_DEV), (2, 1, 0, 3))
    theirs_tail = _rs_sibling([dwin, dwout])
    sums_tail = [_add_own_half(where, p, q, "reduce_pair_%d" % k)
                 for k, (p, q) in enumerate(zip([dwin, dwout], theirs_tail))]
    chip_sums = sums_tail + sums_ffn
    parts = list(_rs_chips(sums_tail)) + list(parts_ffn)

    small = [dh0[META0:LEAD], dwg_p[0:GLA_RANK], gnm_p[0:1], dbg_p[0:1], dgnw_p[0:1], dsink_p[:, 0], gnf_p[0:1],
             gfn_p[0:1], loss_p[0:1, 0:1]]
    sizes = [-(-s.size // 128) for s in small]
    pack = jnp.concatenate([_pad_rows128(s) for s in small], axis=0)
    pad_rows = -pack.shape[0] % 8
    pack = jnp.pad(pack, ((0, pad_rows), (0, 0)))
    total = _all_reduce_small(pack)
    offs = [sum(sizes[:k]) for k in range(len(sizes))]
    take = lambda k, shape: total[offs[k]:offs[k] + sizes[k]].reshape(-1)[:small[k].size].reshape(shape)
    g_meta_full = take(0, (N_META, D))
    g_wg_full = take(1, (GLA_RANK, 256))
    g_meta = lax.dynamic_slice_in_dim(g_meta_full, dev * 128, 128, axis=1)
    g_wg = lax.dynamic_slice_in_dim(g_wg_full, dev * 32, 32, axis=1)[None]
    g_norm_mix, g_b_gate, g_gla_norm = take(2, (1, D)), take(3, (1, 256)), take(4, (1, 128))
    g_sinks, g_norm_ff, g_final_norm = take(5, (1, 8)), take(6, (1, D)), take(7, (D,))
    loss = take(8, ())

    g_win, d_win, nm_win, nv_win = _adamw_shard(where, parts[0], chip_sums[0], w_in[0], m_w_in[0], v_w_in[0], "adamw_w_in")
    g_wout, d_wout, nm_wout, nv_wout = _adamw_shard(where, parts[1], chip_sums[1], w_out[0], m_w_out[0], v_w_out[0], "adamw_w_out")
    g_w1s, d_w1, nm_w1, nv_w1 = _adamw_shard(where, parts[2], chip_sums[2], w_ff1[0], m_w_ff1[0], v_w_ff1[0], "adamw_w_ff1")
    g_w2s, d_w2, nm_w2, nv_w2 = _adamw_shard(where, parts[3], chip_sums[3], w_ff2[0], m_w_ff2[0], v_w_ff2[0], "adamw_w_ff2")

    names = ["meta", "wg", "norm_mix", "b_gate", "gla_norm", "sinks", "norm_ff", "final_norm"]
    ws = [meta_tokens, w_gate_up, norm_mix_w, b_gate, gla_norm_w, sinks, norm_ff_w, final_norm_w]
    gs = [g_meta, g_wg, g_norm_mix, g_b_gate, g_gla_norm, g_sinks, g_norm_ff, g_final_norm]
    ms = [m_meta_tokens, m_w_gate_up, m_norm_mix_w, m_b_gate, m_gla_norm_w, m_sinks, m_norm_ff_w, m_final_norm_w]
    vs = [v_meta_tokens, v_w_gate_up, v_norm_mix_w, v_b_gate, v_gla_norm_w, v_sinks, v_norm_ff_w, v_final_norm_w]
    ssz = [-(-w.size // 128) for w in ws]
    packed = []
    for group in (ws, gs, ms, vs):
        p = jnp.concatenate([_pad_rows128(t) for t in group], axis=0)
        packed.append(jnp.pad(p, ((0, -p.shape[0] % 8), (0, 0))))
    d_s, nm_s, nv_s = _adamw_small(*packed)
    soffs = [sum(ssz[:k]) for k in range(len(ssz))]
    unpack = lambda t, k: t[soffs[k]:soffs[k] + ssz[k]].reshape(-1)[:ws[k].size].reshape(ws[k].shape)
    d_small = {n: unpack(d_s, k) for k, n in enumerate(names)}
    nm_small = {n: unpack(nm_s, k) for k, n in enumerate(names)}
    nv_small = {n: unpack(nv_s, k) for k, n in enumerate(names)}
    g_small_d = dict(zip(names, gs))

    def ordered(big, small_d):
        win_v, wout_v, w1_v, w2_v = big
        return (small_d["meta"], small_d["norm_mix"], win_v[None], small_d["wg"], small_d["b_gate"],
                small_d["gla_norm"], small_d["sinks"], wout_v[None], small_d["norm_ff"], w1_v[None], w2_v[None],
                small_d["final_norm"])

    return (loss, grad_x,
            *ordered((g_win, g_wout, g_w1s, g_w2s), g_small_d),
            *ordered((d_win, d_wout, d_w1, d_w2), d_small),
            *ordered((nm_win, nm_wout, nm_w1, nm_w2), nm_small),
            *ordered((nv_win, nv_wout, nv_w1, nv_w2), nv_small))
```

```python
import functools

import jax
import jax.numpy as jnp
from jax import lax
from jax.experimental import pallas as pl
from jax.experimental.pallas import tpu as pltpu

F32 = jnp.float32
MXU_DTYPE = jnp.bfloat16
ACT_DTYPE = jnp.bfloat16
WIRE_DTYPE = jnp.bfloat16

D = 1024
N_META = 16
LEAD = 128
META0 = LEAD - N_META
EPS = 1e-5
GLA_HEADS, GLA_DK, GLA_DV, GLA_RANK, GLA_CHUNK = 4, 64, 128, 16, 64
GLA_TAU = 16.0
SWA_HEADS, SWA_KV, SWA_GROUP, SWA_HD, SWA_BLOCK = 8, 2, 4, 64, 128
ROPE_DIM, ROPE_THETA = 16, 500000.0
D_FF = 4096
N_DEV = 8
FF_TILE = D_FF // N_DEV
FF_WIDE = 1024
NEG = -1e30

C_GV, C_GR, C_SQ, C_GQ, C_GK, C_SK, C_SV, C_LR = 0, 512, 1024, 1536, 1792, 2048, 2176, 2304
DINP = 2432
DIN = 2320
O_GQ, O_GK, O_GV, O_GR, O_LR, O_SQ, O_SK, O_SV = (0, 256), (256, 512), (512, 1024), (1024, 1536), (1536, 1552), (1552, 2064), (2064, 2192), (2192, 2320)

ADAM_LR, ADAM_B1, ADAM_B2, ADAM_EPS, ADAM_WD, ADAM_STEP = 0.001, 0.9, 0.999, 1e-08, 0.01, 10

MESH = pl.DeviceIdType.MESH
ANY = pl.BlockSpec(memory_space=pl.ANY)
HIGHEST = lax.Precision.HIGHEST


def _cp(sem=None, vmem_mb=None):
    kw = {}
    if sem is not None:
        kw["dimension_semantics"] = sem
    if vmem_mb is not None:
        kw["vmem_limit_bytes"] = vmem_mb << 20
    return pltpu.CompilerParams(**kw)


def _mm(a, b):
    return jnp.dot(a.astype(MXU_DTYPE), b.astype(MXU_DTYPE), preferred_element_type=F32)


def _mm_nt(a, b):
    return lax.dot_general(a.astype(MXU_DTYPE), b.astype(MXU_DTYPE), (((1,), (1,)), ((), ())),
                           preferred_element_type=F32)


def _mm_tn(a, b):
    return lax.dot_general(a.astype(MXU_DTYPE), b.astype(MXU_DTYPE), (((0,), (0,)), ((), ())),
                           preferred_element_type=F32)


def _logsigmoid(z):
    return jnp.minimum(z, 0.0) - jnp.log(1.0 + jnp.exp(-jnp.abs(z)))


def _sigmoid(z):
    return 1.0 / (1.0 + jnp.exp(-z))


def _row_tile(rows):
    return 640 if rows % 640 == 0 else 128


def _mesh_pos():
    return lax.axis_index("x"), lax.axis_index("y"), lax.axis_index("c")


def _all_gather(shards):
    n = len(shards)

    def body(*refs):
        start, forward, finish = _gather_schedule(refs[:n], refs[n:2 * n], *refs[2 * n:])
        start()
        for j in range(3):
            forward(j)
        finish()

    gathered = pl.pallas_call(
        body, name="all_gather_weights",
        out_shape=_gathered_shapes(shards), in_specs=[ANY] * n, out_specs=[ANY] * n,
        scratch_shapes=_gather_sems(n),
    )(*shards)
    return _with_own_block(gathered, shards)


def _gathered_shapes(shards):
    return [jax.ShapeDtypeStruct((N_DEV,) + s.shape, s.dtype) for s in shards]


def _gather_sems(n):
    return [pltpu.SemaphoreType.DMA((7 * n,)), pltpu.SemaphoreType.DMA((7 * n,))]


def _with_own_block(gathered, shards):
    dev = 4 * lax.axis_index("x") + 2 * lax.axis_index("y") + lax.axis_index("c")
    return [lax.dynamic_update_index_in_dim(g, s, dev, 0) for g, s in zip(gathered, shards)]


def _gather_schedule(ins, outs, send_sems, recv_sems):
    n = len(ins)
    x, y, c = _mesh_pos()
    me, sibling = (x, y, c), (x, y, 1 - c)
    chips = [(1 - x, y), (x, 1 - y), (1 - x, 1 - y)]

    def copy(a, k, block, to, src=None):
        dst = outs[a].at[4 * block[0] + 2 * block[1] + block[2]]
        return pltpu.make_async_remote_copy(
            src_ref=dst if src is None else src, dst_ref=dst,
            send_sem=send_sems.at[a * 7 + k], recv_sem=recv_sems.at[a * 7 + k],
            device_id=to, device_id_type=MESH)

    def first(a):
        return [copy(a, 0, me, sibling, src=ins[a])] + [copy(a, 1 + j, me, (*chip, c), src=ins[a])
                                                        for j, chip in enumerate(chips)]

    def start():
        for a in range(n):
            for cp in first(a):
                cp.start()

    def forward(j):
        for a in range(n):
            copy(a, 1 + j, (*chips[j], c), me).wait_recv()
            copy(a, 4 + j, (*chips[j], c), sibling).start()

    def finish():
        for a in range(n):
            copy(a, 0, sibling, me).wait_recv()
            for j, chip in enumerate(chips):
                copy(a, 4 + j, (*chip, 1 - c), me).wait_recv()
        for a in range(n):
            for cp in first(a) + [copy(a, 4 + j, (*chip, c), sibling) for j, chip in enumerate(chips)]:
                cp.wait_send()

    return start, forward, finish


def _rs_sibling(gs):
    n = len(gs)

    def body(*refs):
        start, finish = _sibling_schedule(refs[:n], refs[n:2 * n], *refs[2 * n:])
        start()
        finish()

    return pl.pallas_call(
        body, name="reduce_scatter_sibling",
        out_shape=_sibling_shapes(gs), in_specs=[ANY] * n, out_specs=[ANY] * n,
        scratch_shapes=_sibling_sems(n),
    )(*gs)


def _sibling_shapes(gs):
    return [jax.ShapeDtypeStruct(g.shape[1:], g.dtype) for g in gs]


def _sibling_sems(n):
    return [pltpu.SemaphoreType.DMA((n,)), pltpu.SemaphoreType.DMA((n,))]


def _sibling_schedule(ins, land, send_sems, recv_sems):
    x, y, c = _mesh_pos()

    def copies():
        return [pltpu.make_async_remote_copy(
            src_ref=ins[a].at[1 - c], dst_ref=land[a], send_sem=send_sems.at[a], recv_sem=recv_sems.at[a],
            device_id=(x, y, 1 - c), device_id_type=MESH) for a in range(len(ins))]

    def start():
        for cp in copies():
            cp.start()

    def finish():
        for cp in copies():
            cp.wait_recv()
        for cp in copies():
            cp.wait_send()

    return start, finish


def _rs_chips(ps):
    n = len(ps)

    def body(*refs):
        start, finish = _chips_schedule(refs[:n], refs[n:2 * n], *refs[2 * n:])
        start()
        finish()

    return pl.pallas_call(
        body, name="reduce_scatter_chips",
        out_shape=_chips_shapes(ps), in_specs=[ANY] * n, out_specs=[ANY] * n,
        scratch_shapes=_chips_sems(n),
    )(*ps)


def _chips_shapes(ps):
    return [jax.ShapeDtypeStruct((3,) + p.shape[1:], p.dtype) for p in ps]


def _chips_sems(n):
    return [pltpu.SemaphoreType.DMA((3 * n,)), pltpu.SemaphoreType.DMA((3 * n,))]


def _chips_schedule(ins, land, send_sems, recv_sems):
    x, y, c = _mesh_pos()
    chips = [(1 - x, y), (x, 1 - y), (1 - x, 1 - y)]

    def copies():
        return [pltpu.make_async_remote_copy(
            src_ref=ins[a].at[2 * chip[0] + chip[1]], dst_ref=land[a].at[j],
            send_sem=send_sems.at[3 * a + j], recv_sem=recv_sems.at[3 * a + j],
            device_id=(*chip, c), device_id_type=MESH) for a in range(len(ins)) for j, chip in enumerate(chips)]

    def start():
        for cp in copies():
            cp.start()

    def finish():
        for cp in copies():
            cp.wait_recv()
        for cp in copies():
            cp.wait_send()

    return start, finish


class _Jobs:
    def __init__(self, jobs):
        self.jobs = jobs
        self.inputs = [a for _, arrs in jobs for a in arrs]
        self.out_shapes = [s for kind, arrs in jobs
                           for s in (_sibling_shapes(arrs) if kind == "sibling" else _chips_shapes(arrs))]
        self.sems = [s for kind, arrs in jobs
                     for s in (_sibling_sems(len(arrs)) if kind == "sibling" else _chips_sems(len(arrs)))]
        self.n = len(self.inputs)

    def bind(self, in_refs, out_refs, sem_refs):
        starts, finishes, at = [], [], 0
        for k, (kind, arrs) in enumerate(self.jobs):
            schedule = _sibling_schedule if kind == "sibling" else _chips_schedule
            start, finish = schedule(in_refs[at:at + len(arrs)], out_refs[at:at + len(arrs)],
                                     sem_refs[2 * k], sem_refs[2 * k + 1])
            starts.append(start)
            finishes.append(finish)
            at += len(arrs)

        def start_all():
            for f in starts:
                f()

        def finish_all():
            for f in finishes:
                f()

        return start_all, finish_all

    def split(self, outs):
        res, at = [], 0
        for _, arrs in self.jobs:
            res.append(list(outs[at:at + len(arrs)]))
            at += len(arrs)
        return res


def _all_reduce_small(pack):
    rows = pack.shape[0]

    def body(p_ref, out_ref, land, send_sems, recv_sems):
        x, y, c = _mesh_pos()
        me = 4 * x + 2 * y + c
        land[me] = p_ref[...]
        copies = []
        for k in range(1, N_DEV):
            bx, by, bc = (k >> 2) & 1, (k >> 1) & 1, k & 1
            peer = (1 - x if bx else x, 1 - y if by else y, 1 - c if bc else c)
            copies.append(pltpu.make_async_remote_copy(
                src_ref=p_ref, dst_ref=land.at[me], send_sem=send_sems.at[k - 1], recv_sem=recv_sems.at[k - 1],
                device_id=peer, device_id_type=MESH))
        for cp in copies:
            cp.start()
        for cp in copies:
            cp.wait_recv()
        for cp in copies:
            cp.wait_send()
        acc = land[0]
        for d in range(1, N_DEV):
            acc = acc + land[d]
        out_ref[...] = acc

    return pl.pallas_call(
        body, name="all_reduce_small",
        out_shape=jax.ShapeDtypeStruct(pack.shape, F32),
        in_specs=[pl.BlockSpec(memory_space=pltpu.VMEM)], out_specs=pl.BlockSpec(memory_space=pltpu.VMEM),
        scratch_shapes=[pltpu.VMEM((N_DEV, rows, 128), F32), pltpu.SemaphoreType.DMA((7,)),
                        pltpu.SemaphoreType.DMA((7,))],
    )(pack)


def _in_proj(h0, nw, win_p, tm):
    rows = h0.shape[0]

    def body(h_ref, nw_ref, w_ref, o_ref):
        h = h_ref[...]
        rstd = lax.rsqrt(jnp.mean(h * h, axis=-1, keepdims=True) + EPS)
        u = (h * rstd * nw_ref[...]).astype(MXU_DTYPE)
        o_ref[...] = jnp.dot(u, w_ref[...].astype(MXU_DTYPE), preferred_element_type=F32)

    return pl.pallas_call(
        body, name="in_proj", grid=(rows // tm,),
        in_specs=[pl.BlockSpec((tm, D), lambda i: (i, 0)), pl.BlockSpec((1, D), lambda i: (0, 0)),
                  pl.BlockSpec((D, DINP), lambda i: (0, 0))],
        out_specs=pl.BlockSpec((tm, DINP), lambda i: (i, 0)),
        out_shape=jax.ShapeDtypeStruct((rows, DINP), F32),
        compiler_params=_cp(("arbitrary",), 56),
    )(h0, nw, win_p)


def _rope_tables(rows):
    pos = (jnp.arange(rows, dtype=jnp.int32) - META0).astype(F32)
    inv_freq = 1.0 / (ROPE_THETA ** (jnp.arange(0, ROPE_DIM, 2, dtype=F32) / ROPE_DIM))
    ang = pos[:, None] * jnp.tile(inv_freq, 128 // (ROPE_DIM // 2))[None, :]
    in_head = jnp.arange(128, dtype=jnp.int32)[None, :] % SWA_HD
    cos, sin = jnp.cos(ang), jnp.sin(ang)
    c_tab = jnp.where(in_head < ROPE_DIM, cos, 1.0)
    sa_tab = jnp.where(in_head < ROPE_DIM // 2, -sin, 0.0)
    sb_tab = jnp.where((in_head >= ROPE_DIM // 2) & (in_head < ROPE_DIM), sin, 0.0)
    return c_tab, sa_tab, sb_tab


def _rope(xv, cos, sa, sb):
    width = xv.shape[1]
    reps = width // 128
    if reps > 1:
        cos, sa, sb = (jnp.tile(t, (1, reps)) for t in (cos, sa, sb))
    return xv * cos + pltpu.roll(xv, width - 8, 1) * sa + pltpu.roll(xv, 8, 1) * sb


def _unrope(dy, cos, sa, sb):
    width = dy.shape[1]
    reps = width // 128
    if reps > 1:
        cos, sa, sb = (jnp.tile(t, (1, reps)) for t in (cos, sa, sb))
    return dy * cos + pltpu.roll(dy * sa, 8, 1) + pltpu.roll(dy * sb, width - 8, 1)


def _swa_prep(proj, tabs, tm):
    rows = proj.shape[0]

    def body(q_ref, k_ref, v_ref, c_ref, sa_ref, sb_ref, qo_ref, ko_ref, vo_ref):
        cos, sa, sb = c_ref[...], sa_ref[...], sb_ref[...]
        qo_ref[...] = (_rope(q_ref[...], cos, sa, sb) * (SWA_HD ** -0.5)).astype(ACT_DTYPE)
        ko_ref[...] = _rope(k_ref[...], cos, sa, sb).astype(ACT_DTYPE)
        vo_ref[...] = v_ref[...].astype(ACT_DTYPE)

    tab_spec = pl.BlockSpec((tm, 128), lambda i: (i, 0))
    return pl.pallas_call(
        body, name="swa_prep", grid=(rows // tm,),
        in_specs=[pl.BlockSpec((tm, 512), lambda i: (i, C_SQ // 512)),
                  pl.BlockSpec((tm, 128), lambda i: (i, C_SK // 128)),
                  pl.BlockSpec((tm, 128), lambda i: (i, C_SV // 128)), tab_spec, tab_spec, tab_spec],
        out_specs=[pl.BlockSpec((tm, 512), lambda i: (i, 0)), tab_spec, tab_spec],
        out_shape=[jax.ShapeDtypeStruct((rows, 512), ACT_DTYPE), jax.ShapeDtypeStruct((rows, 128), ACT_DTYPE),
                   jax.ShapeDtypeStruct((rows, 128), ACT_DTYPE)],
        compiler_params=_cp(("arbitrary",)),
    )(proj, proj, proj, *tabs)


def _gla_group(nc):
    for g in (5, 2):
        if nc % g == 0:
            return g
    return 1


def _gla_gates(lr, wg, bg, first_row, nrows):
    zg = _mm(lr, wg) + bg
    row = first_row + lax.broadcasted_iota(jnp.int32, (nrows, 1), 0)
    live = row >= META0
    g = jnp.where(live, _logsigmoid(zg) * (1.0 / GLA_TAU), 0.0)
    ii = lax.broadcasted_iota(jnp.int32, (nrows, nrows), 0)
    jj = lax.broadcasted_iota(jnp.int32, (nrows, nrows), 1)
    same = (ii // GLA_CHUNK) == (jj // GLA_CHUNK)
    lower, upper = same & (jj <= ii), same & (jj >= ii)
    b = jnp.dot(lower.astype(F32), g, precision=HIGHEST, preferred_element_type=F32)
    return zg, live, lower, upper, b


def _tril64():
    ii = lax.broadcasted_iota(jnp.int32, (GLA_CHUNK, GLA_CHUNK), 0)
    jj = lax.broadcasted_iota(jnp.int32, (GLA_CHUNK, GLA_CHUNK), 1)
    return jj <= ii


def _gla_fwd(proj, wg_p, bg, gnw, shards):
    rows = proj.shape[0]
    nc = rows // GLA_CHUNK
    group = _gla_group(nc)
    steps, nrows = nc // group, group * GLA_CHUNK
    ns = len(shards)
    forward_steps = [steps * 7 // 10, steps * 8 // 10, steps * 9 // 10]

    def body(q_ref, k_ref, v_ref, r_ref, lr_ref, wg_ref, bg_ref, gnw_ref, *rest):
        shard_refs, rest = rest[:ns], rest[ns:]
        oraw_ref, og_ref, st_ref = rest[:3]
        gathered_refs, rest = rest[3:3 + ns], rest[3 + ns:]
        state, send_sems, recv_sems = rest
        c = pl.program_id(0)
        start, forward, finish = _gather_schedule(shard_refs, gathered_refs, send_sems, recv_sems)

        @pl.when(c == 0)
        def _():
            state[...] = jnp.zeros_like(state)
            start()

        for j, step in enumerate(forward_steps):
            pl.when(c == step)(functools.partial(forward, j))
        pl.when(c == steps - 1)(finish)

        _, _, _, _, b = _gla_gates(lr_ref[...], wg_ref[...], bg_ref[...], c * nrows, nrows)
        eb = jnp.exp(b)
        gq = q_ref[...] * (GLA_DK ** -0.5) * eb
        gk = k_ref[...] * jnp.exp(-b)
        v = v_ref[...]
        gnw_v = gnw_ref[...]
        tril = _tril64()
        for h in range(GLA_HEADS):
            s64 = slice(h * GLA_DK, (h + 1) * GLA_DK)
            s128 = slice(h * GLA_DV, (h + 1) * GLA_DV)
            st = state[h]
            for gi in range(group):
                rs = slice(gi * GLA_CHUNK, (gi + 1) * GLA_CHUNK)
                qh, kh, vh = gq[rs, s64], gk[rs, s64], v[rs, s128]
                eblh = eb[(gi + 1) * GLA_CHUNK - 1:(gi + 1) * GLA_CHUNK, s64]
                st_ref[gi, h] = st
                a = jnp.where(tril, _mm_nt(qh, kh), 0.0)
                o = _mm(a, vh) + _mm_nt(qh, st)
                st = st * eblh + _mm_tn(vh, kh * eblh)
                oraw_ref[rs, s128] = o
                rstd = lax.rsqrt(jnp.mean(o * o, axis=-1, keepdims=True) + EPS)
                rh = r_ref[rs, s128]
                og_ref[rs, s128] = (o * rstd * gnw_v * (rh * _sigmoid(rh))).astype(ACT_DTYPE)
            state[h] = st

    nb = lambda w, col: pl.BlockSpec((nrows, w), lambda c: (c, col // w))
    const = lambda shape: pl.BlockSpec(shape, lambda c: (0,) * len(shape))
    outs = pl.pallas_call(
        body, name="gla_fwd", grid=(steps,),
        in_specs=[nb(256, C_GQ), nb(256, C_GK), nb(512, C_GV), nb(512, C_GR), nb(128, C_LR),
                  const((128, 256)), const((1, 256)), const((1, 128))] + [ANY] * ns,
        out_specs=[pl.BlockSpec((nrows, 512), lambda c: (c, 0)), pl.BlockSpec((nrows, 512), lambda c: (c, 0)),
                   pl.BlockSpec((group, GLA_HEADS, GLA_DV, GLA_DK), lambda c: (c, 0, 0, 0))] + [ANY] * ns,
        out_shape=[jax.ShapeDtypeStruct((rows, 512), F32), jax.ShapeDtypeStruct((rows, 512), ACT_DTYPE),
                   jax.ShapeDtypeStruct((nc, GLA_HEADS, GLA_DV, GLA_DK), F32)] + _gathered_shapes(shards),
        scratch_shapes=[pltpu.VMEM((GLA_HEADS, GLA_DV, GLA_DK), F32)] + _gather_sems(ns),
        compiler_params=_cp(("arbitrary",)),
    )(proj, proj, proj, proj, proj, wg_p, bg, gnw, *shards)
    return outs[0], outs[1], outs[2], _with_own_block(outs[3:], shards)


def _swa_mask(n):
    shape = (SWA_GROUP * SWA_BLOCK, 3 * SWA_BLOCK)
    qi = lax.broadcasted_iota(jnp.int32, shape, 0) & (SWA_BLOCK - 1)
    jj = lax.broadcasted_iota(jnp.int32, shape, 1)
    meta = (jj < SWA_BLOCK) & (jj >= META0) & ((n > 0) | (jj <= qi))
    prev = (jj >= SWA_BLOCK) & (jj < 2 * SWA_BLOCK) & (n >= 2) & (jj - SWA_BLOCK > qi)
    cur = (jj >= 2 * SWA_BLOCK) & (n >= 1) & (jj - 2 * SWA_BLOCK <= qi)
    return meta | prev | cur


def _stack_heads(t, kvh):
    return jnp.concatenate([t[:, (kvh * SWA_GROUP + g) * SWA_HD:(kvh * SWA_GROUP + g + 1) * SWA_HD]
                            for g in range(SWA_GROUP)], axis=0)


def _stack_sinks(sink_ref, kvh):
    return jnp.concatenate([jnp.full((SWA_BLOCK, 1), sink_ref[0, kvh * SWA_GROUP + g], F32)
                            for g in range(SWA_GROUP)], axis=0)


def _swa_group(nblk):
    return 5 if nblk % 5 == 0 else 1


def _swa_specs(group):
    blk = lambda w: pl.BlockSpec((group * SWA_BLOCK, w), lambda n: (n, 0))
    first = pl.BlockSpec((SWA_BLOCK, 128), lambda n: (0, 0))
    prev = pl.BlockSpec((SWA_BLOCK, 128), lambda n: (jnp.maximum(n * group - 1, 0), 0))
    return blk, first, prev


def _swa_keys(first_ref, prev_ref, cur_ref, g):
    own = cur_ref[g * SWA_BLOCK:(g + 1) * SWA_BLOCK, :]
    before = prev_ref[...] if g == 0 else cur_ref[(g - 1) * SWA_BLOCK:g * SWA_BLOCK, :]
    return jnp.concatenate([first_ref[...], before, own], axis=0)


def _swa_fwd(qr, kr, vr, sinks):
    rows = qr.shape[0]
    nblk = rows // SWA_BLOCK
    group = _swa_group(nblk)

    def body(q_ref, k0, kp, kc, v0, vp, vc, sink_ref, o_ref):
        for g in range(group):
            n = pl.program_id(0) * group + g
            rs = slice(g * SWA_BLOCK, (g + 1) * SWA_BLOCK)
            kall, vall = _swa_keys(k0, kp, kc, g), _swa_keys(v0, vp, vc, g)
            mask = _swa_mask(n)[0:SWA_BLOCK]
            for head in range(SWA_HEADS):
                hs = slice(head * SWA_HD, (head + 1) * SWA_HD)
                kv = slice((head // SWA_GROUP) * SWA_HD, (head // SWA_GROUP + 1) * SWA_HD)
                s = jnp.where(mask, _mm_nt(q_ref[rs, hs], kall[:, kv]), NEG)
                sink = sink_ref[0, head]
                m = jnp.maximum(jnp.max(s, axis=-1, keepdims=True), sink)
                p = jnp.exp(s - m)
                den = jnp.sum(p, axis=-1, keepdims=True) + jnp.exp(sink - m)
                o_ref[rs, hs] = (_mm(p, vall[:, kv]) / den).astype(ACT_DTYPE)

    blk, first, prev = _swa_specs(group)
    return pl.pallas_call(
        body, name="swa_fwd", grid=(nblk // group,),
        in_specs=[blk(512), first, prev, blk(128), first, prev, blk(128),
                  pl.BlockSpec(memory_space=pltpu.SMEM)],
        out_specs=blk(512),
        out_shape=jax.ShapeDtypeStruct((rows, 512), ACT_DTYPE),
        compiler_params=_cp(("arbitrary",)),
    )(qr, kr, kr, kr, vr, vr, vr, sinks)


def _out_proj(h0, og, osw, wout, nfw, tm):
    rows = h0.shape[0]

    def body(h_ref, og_ref, os_ref, w_ref, nw_ref, h1_ref, f_ref, ft_ref):
        h1 = h_ref[...] + _mm(og_ref[...], w_ref[0:512, :]) + _mm(os_ref[...], w_ref[512:1024, :])
        h1_ref[...] = h1
        rstd = lax.rsqrt(jnp.mean(h1 * h1, axis=-1, keepdims=True) + EPS)
        f = h1 * rstd * nw_ref[...]
        f_ref[...] = f.astype(ACT_DTYPE)
        ft_ref[...] = f.T.astype(ACT_DTYPE)

    row = lambda w: pl.BlockSpec((tm, w), lambda i: (i, 0))
    return pl.pallas_call(
        body, name="out_proj", grid=(rows // tm,),
        in_specs=[row(D), row(512), row(512), pl.BlockSpec((D, D), lambda i: (0, 0)), pl.BlockSpec((1, D), lambda i: (0, 0))],
        out_specs=[row(D), row(D), pl.BlockSpec((D, tm), lambda i: (0, i))],
        out_shape=[jax.ShapeDtypeStruct((rows, D), F32), jax.ShapeDtypeStruct((rows, D), ACT_DTYPE),
                   jax.ShapeDtypeStruct((D, rows), ACT_DTYPE)],
        compiler_params=_cp(("arbitrary",), 48),
    )(h0, og, osw, wout, nfw)


def _ffn_fwd(f, h1, w1, w2, tgt, fnw, tm):
    rows = f.shape[0]
    nj = D_FF // FF_WIDE

    def body(f_ref, h1_ref, w1_ref, w2_ref, t_ref, nw_ref, a_ref, dh2_ref, dh2t_ref, loss_ref, gfn_ref, acc):
        i, j = pl.program_id(0), pl.program_id(1)

        @pl.when((i == 0) & (j == 0))
        def _():
            loss_ref[...] = jnp.zeros_like(loss_ref)
            gfn_ref[...] = jnp.zeros_like(gfn_ref)

        @pl.when(j == 0)
        def _():
            acc[...] = jnp.zeros_like(acc)

        a = _mm(f_ref[...], w1_ref[...])
        a_ref[...] = a.astype(ACT_DTYPE)
        z = jnp.square(jnp.maximum(a, 0.0))
        acc[...] += _mm(z, w2_ref[...])

        @pl.when(j == nj - 1)
        def _():
            h2 = h1_ref[...] + acc[...]
            rstd = lax.rsqrt(jnp.mean(h2 * h2, axis=-1, keepdims=True) + EPS)
            hn = h2 * rstd
            nw = nw_ref[...]
            row = i * tm + lax.broadcasted_iota(jnp.int32, (tm, 1), 0)
            err = jnp.where(row >= LEAD, hn * nw - t_ref[...], 0.0)
            row_loss = jnp.sum(err * err, axis=-1, keepdims=True) * (1.0 / D)
            loss_ref[...] += jnp.broadcast_to(0.5 * jnp.sum(row_loss, axis=0, keepdims=True), loss_ref.shape)
            dy = err * (1.0 / D)
            gfn_ref[...] += jnp.broadcast_to(jnp.sum(dy * hn, axis=0, keepdims=True), gfn_ref.shape)
            dhn = dy * nw
            dh2 = rstd * (dhn - hn * jnp.mean(dhn * hn, axis=-1, keepdims=True))
            dh2_ref[...] = dh2
            dh2t_ref[...] = dh2.T.astype(ACT_DTYPE)

    return pl.pallas_call(
        body, name="ffn_fwd", grid=(rows // tm, nj),
        in_specs=[pl.BlockSpec((tm, D), lambda i, j: (i, 0)), pl.BlockSpec((tm, D), lambda i, j: (i, 0)),
                  pl.BlockSpec((D, FF_WIDE), lambda i, j: (0, j)),
                  pl.BlockSpec((FF_WIDE, D), lambda i, j: (j, 0)),
                  pl.BlockSpec((tm, D), lambda i, j: (i, 0)), pl.BlockSpec((1, D), lambda i, j: (0, 0))],
        out_specs=[pl.BlockSpec((tm, FF_WIDE), lambda i, j: (i, j)), pl.BlockSpec((tm, D), lambda i, j: (i, 0)),
                   pl.BlockSpec((D, tm), lambda i, j: (0, i)),
                   pl.BlockSpec((8, 128), lambda i, j: (0, 0)), pl.BlockSpec((8, D), lambda i, j: (0, 0))],
        out_shape=[jax.ShapeDtypeStruct((rows, D_FF), ACT_DTYPE), jax.ShapeDtypeStruct((rows, D), F32),
                   jax.ShapeDtypeStruct((D, rows), ACT_DTYPE),
                   jax.ShapeDtypeStruct((8, 128), F32), jax.ShapeDtypeStruct((8, D), F32)],
        scratch_shapes=[pltpu.VMEM((tm, D), F32)],
        compiler_params=_cp(("arbitrary", "arbitrary"), 56),
    )(f, h1, w1, w2, tgt, fnw)


def _ffn_bwd_act(dh2, a, w1, w2, h1, nfw, tm):
    rows = dh2.shape[0]
    nj = D_FF // FF_WIDE

    def body(dh2_ref, a_ref, w1_ref, w2_ref, h1_ref, nw_ref, da_ref, dh1_ref, gnf_ref, acc):
        i, j = pl.program_id(0), pl.program_id(1)

        @pl.when((i == 0) & (j == 0))
        def _():
            gnf_ref[...] = jnp.zeros_like(gnf_ref)

        @pl.when(j == 0)
        def _():
            acc[...] = jnp.zeros_like(acc)

        dz = _mm_nt(dh2_ref[...], w2_ref[...])
        da = dz * (2.0 * jnp.maximum(a_ref[...].astype(F32), 0.0))
        da_ref[...] = da.astype(ACT_DTYPE)
        acc[...] += _mm_nt(da, w1_ref[...])

        @pl.when(j == nj - 1)
        def _():
            h1 = h1_ref[...]
            rstd = lax.rsqrt(jnp.mean(h1 * h1, axis=-1, keepdims=True) + EPS)
            hn = h1 * rstd
            df = acc[...]
            gnf_ref[...] += jnp.broadcast_to(jnp.sum(df * hn, axis=0, keepdims=True), gnf_ref.shape)
            dfn = df * nw_ref[...]
            dh1_ref[...] = dh2_ref[...] + rstd * (dfn - hn * jnp.mean(dfn * hn, axis=-1, keepdims=True))

    return pl.pallas_call(
        body, name="ffn_bwd_act", grid=(rows // tm, nj),
        in_specs=[pl.BlockSpec((tm, D), lambda i, j: (i, 0)), pl.BlockSpec((tm, FF_WIDE), lambda i, j: (i, j)),
                  pl.BlockSpec((D, FF_WIDE), lambda i, j: (0, j)),
                  pl.BlockSpec((FF_WIDE, D), lambda i, j: (j, 0)),
                  pl.BlockSpec((tm, D), lambda i, j: (i, 0)), pl.BlockSpec((1, D), lambda i, j: (0, 0))],
        out_specs=[pl.BlockSpec((tm, FF_WIDE), lambda i, j: (i, j)), pl.BlockSpec((tm, D), lambda i, j: (i, 0)),
                   pl.BlockSpec((8, D), lambda i, j: (0, 0))],
        out_shape=[jax.ShapeDtypeStruct((rows, D_FF), ACT_DTYPE), jax.ShapeDtypeStruct((rows, D), F32),
                   jax.ShapeDtypeStruct((8, D), F32)],
        scratch_shapes=[pltpu.VMEM((tm, D), F32)],
        compiler_params=_cp(("arbitrary", "arbitrary"), 56),
    )(dh2, a, w1, w2, h1, nfw)


def _ffn_bwd_weights(ft, a, da, dh2t, tm):
    rows = a.shape[0]
    steps = rows // tm

    def body(ft_ref, a_ref, da_ref, dh2t_ref, dw1_ref, dw2_ref, dw2t):
        i = pl.program_id(1)

        @pl.when(i == 0)
        def _():
            dw1_ref[...] = jnp.zeros_like(dw1_ref)
            dw2t[...] = jnp.zeros_like(dw2t)

        z = jnp.square(jnp.maximum(a_ref[...].astype(F32), 0.0))
        dw1_ref[...] += _mm(ft_ref[...], da_ref[...])
        dw2t[...] += _mm(dh2t_ref[...], z)

        @pl.when(i == steps - 1)
        def _():
            dw2_ref[...] = dw2t[...].T

    return pl.pallas_call(
        body, name="ffn_bwd_weights", grid=(N_DEV, steps),
        in_specs=[pl.BlockSpec((D, tm), lambda j, i: (0, i)), pl.BlockSpec((tm, FF_TILE), lambda j, i: (i, j)),
                  pl.BlockSpec((tm, FF_TILE), lambda j, i: (i, j)), pl.BlockSpec((D, tm), lambda j, i: (0, i))],
        out_specs=[pl.BlockSpec((None, None, D, FF_TILE), lambda j, i: (j % 2, j // 2, 0, 0)),
                   pl.BlockSpec((None, None, FF_TILE, D), lambda j, i: (j % 2, j // 2, 0, 0))],
        out_shape=[jax.ShapeDtypeStruct((2, 4, D, FF_TILE), F32), jax.ShapeDtypeStruct((2, 4, FF_TILE, D), F32)],
        scratch_shapes=[pltpu.VMEM((D, FF_TILE), F32)],
        compiler_params=_cp(("arbitrary", "arbitrary"), 48),
    )(ft, a, da, dh2t)


def _out_proj_bwd(dh1, og, osw, wout, tm, partials):
    rows = dh1.shape[0]
    steps = rows // tm
    ns = len(partials)

    def body(dh1_ref, og_ref, os_ref, w_ref, *rest):
        part_refs, rest = rest[:ns], rest[ns:]
        dog_ref, dos_ref, dw_ref = rest[:3]
        land_refs, (send_sems, recv_sems) = rest[3:3 + ns], rest[3 + ns:]
        i = pl.program_id(0)
        start, finish = _sibling_schedule(part_refs, land_refs, send_sems, recv_sems)

        @pl.when(i == 0)
        def _():
            dw_ref[...] = jnp.zeros_like(dw_ref)
            start()

        pl.when(i == steps - 1)(finish)

        dh1 = dh1_ref[...].astype(MXU_DTYPE)
        dog_ref[...] = _mm_nt(dh1, w_ref[0:512, :])
        dos_ref[...] = _mm_nt(dh1, w_ref[512:1024, :])
        for half, ref in enumerate((og_ref, os_ref)):
            dw = _mm_tn(ref[...], dh1)
            for blk in range(4):
                shard = half * 4 + blk
                dw_ref[shard % 2, shard // 2] += dw[blk * 128:(blk + 1) * 128, :]

    row = lambda w: pl.BlockSpec((tm, w), lambda i: (i, 0))
    outs = pl.pallas_call(
        body, name="out_proj_bwd", grid=(steps,),
        in_specs=[row(D), row(512), row(512), pl.BlockSpec((D, D), lambda i: (0, 0))] + [ANY] * ns,
        out_specs=[row(512), row(512), pl.BlockSpec((2, 4, 128, D), lambda i: (0, 0, 0, 0))] + [ANY] * ns,
        out_shape=[jax.ShapeDtypeStruct((rows, 512), F32), jax.ShapeDtypeStruct((rows, 512), F32),
                   jax.ShapeDtypeStruct((2, 4, 128, D), F32)] + _sibling_shapes(partials),
        scratch_shapes=_sibling_sems(ns),
        compiler_params=_cp(("arbitrary",), 48),
    )(dh1, og, osw, wout, *partials)
    return outs[0], outs[1], outs[2], outs[3:]


def _swa_bwd(qr, kr, vr, osw, dos, sinks, jobs):
    rows = qr.shape[0]
    nblk = rows // SWA_BLOCK
    group = _swa_group(nblk)
    steps = nblk // group
    ns = jobs.n

    def body(q_ref, k0, kp, kc, v0, vp, vc, o_ref, do_ref, sink_ref, *rest):
        dq_ref, dk_ref, dv_ref, dsink_ref = rest[ns:ns + 4]
        start, finish = jobs.bind(rest[:ns], rest[ns + 4:2 * ns + 4], rest[2 * ns + 4:])
        step = pl.program_id(0)

        @pl.when(step == 0)
        def _():
            dk_ref[...] = jnp.zeros_like(dk_ref)
            dv_ref[...] = jnp.zeros_like(dv_ref)
            dsink_ref[...] = jnp.zeros_like(dsink_ref)
            start()

        pl.when(step == steps - 1)(finish)
        for g in range(group):
            block(step * group + g, g, q_ref, k0, kp, kc, v0, vp, vc, o_ref, do_ref, sink_ref,
                  dq_ref, dk_ref, dv_ref, dsink_ref)

    def block(n, g, q_ref, k0, kp, kc, v0, vp, vc, o_ref, do_ref, sink_ref, dq_ref, dk_ref, dv_ref, dsink_ref):
        rs = slice(g * SWA_BLOCK, (g + 1) * SWA_BLOCK)
        q = q_ref[rs, :]
        kall, vall = _swa_keys(k0, kp, kc, g), _swa_keys(v0, vp, vc, g)
        mask = _swa_mask(n)
        do_all = do_ref[rs, :]
        o_all = o_ref[rs, :].astype(F32)
        dq, dk, dv = [], [], []
        for kvh in range(SWA_KV):
            kv = slice(kvh * SWA_HD, (kvh + 1) * SWA_HD)
            q4, do4, o4 = _stack_heads(q, kvh), _stack_heads(do_all, kvh), _stack_heads(o_all, kvh)
            sink4 = _stack_sinks(sink_ref, kvh)
            s = jnp.where(mask, _mm_nt(q4, kall[:, kv]), NEG)
            m = jnp.maximum(jnp.max(s, axis=-1, keepdims=True), sink4)
            e = jnp.exp(s - m)
            inv = 1.0 / (jnp.sum(e, axis=-1, keepdims=True) + jnp.exp(sink4 - m))
            p = e * inv
            delta = jnp.sum(do4 * o4, axis=-1, keepdims=True)
            ds = p * (_mm_nt(do4, vall[:, kv]) - delta)
            dq4 = _mm(ds, kall[:, kv])
            dq += [dq4[g * SWA_BLOCK:(g + 1) * SWA_BLOCK] for g in range(SWA_GROUP)]
            dk.append(_mm_tn(ds, q4))
            dv.append(_mm_tn(p, do4))
            sink_term = jnp.exp(sink4 - m) * inv * delta
            for g in range(SWA_GROUP):
                head = kvh * SWA_GROUP + g
                dsink = -jnp.sum(sink_term[g * SWA_BLOCK:(g + 1) * SWA_BLOCK], axis=0, keepdims=True)
                dsink_ref[head:head + 1, :] += jnp.broadcast_to(dsink, (1, 128))
        dq_ref[rs, :] = jnp.concatenate(dq, axis=1)
        dk_all = jnp.concatenate(dk, axis=1)
        dv_all = jnp.concatenate(dv, axis=1)
        prev0 = pl.multiple_of(jnp.maximum(n - 1, 0) * SWA_BLOCK, SWA_BLOCK)
        cur0 = pl.multiple_of(n * SWA_BLOCK, SWA_BLOCK)
        for ref, val in ((dk_ref, dk_all), (dv_ref, dv_all)):
            ref[0:SWA_BLOCK, :] += val[0:SWA_BLOCK]
            ref[pl.ds(prev0, SWA_BLOCK), :] += val[SWA_BLOCK:2 * SWA_BLOCK]
            ref[pl.ds(cur0, SWA_BLOCK), :] += val[2 * SWA_BLOCK:]

    blk, first, prev = _swa_specs(group)
    whole = pl.BlockSpec((rows, 128), lambda n: (0, 0))
    outs = pl.pallas_call(
        body, name="swa_bwd", grid=(steps,),
        in_specs=[blk(512), first, prev, blk(128), first, prev, blk(128), blk(512), blk(512),
                  pl.BlockSpec(memory_space=pltpu.SMEM)] + [ANY] * ns,
        out_specs=[blk(512), whole, whole, pl.BlockSpec((8, 128), lambda n: (0, 0))] + [ANY] * ns,
        out_shape=[jax.ShapeDtypeStruct((rows, 512), F32), jax.ShapeDtypeStruct((rows, 128), F32),
                   jax.ShapeDtypeStruct((rows, 128), F32), jax.ShapeDtypeStruct((8, 128), F32)] + jobs.out_shapes,
        scratch_shapes=jobs.sems,
        compiler_params=_cp(("arbitrary",), 48),
    )(qr, kr, kr, kr, vr, vr, vr, osw, dos, sinks, *jobs.inputs)
    return outs[0], outs[1], outs[2], outs[3], jobs.split(outs[4:])


def _gla_bwd(proj, oraw, states, dog, wg_p, bg, gnw, jobs):
    rows = proj.shape[0]
    nc = rows // GLA_CHUNK
    group = _gla_group(nc)
    steps, nrows = nc // group, group * GLA_CHUNK
    ns = jobs.n

    def body(q_ref, k_ref, v_ref, r_ref, lr_ref, oraw_ref, st_ref, dog_ref, wg_ref, bg_ref, gnw_ref, *rest):
        dq_ref, dk_ref, dv_ref, dr_ref, dlr_ref, dwg_ref, dbg_ref, dgnw_ref = rest[ns:ns + 8]
        dstate, db_scr = rest[2 * ns + 8:2 * ns + 10]
        start, finish = jobs.bind(rest[:ns], rest[ns + 8:2 * ns + 8], rest[2 * ns + 10:])
        t = pl.program_id(0)
        c = steps - 1 - t

        @pl.when(t == 0)
        def _():
            dstate[...] = jnp.zeros_like(dstate)
            dwg_ref[...] = jnp.zeros_like(dwg_ref)
            dbg_ref[...] = jnp.zeros_like(dbg_ref)
            dgnw_ref[...] = jnp.zeros_like(dgnw_ref)
            start()

        pl.when(t == steps - 1)(finish)

        lr, wg = lr_ref[...], wg_ref[...]
        zg, live, _, upper, b = _gla_gates(lr, wg, bg_ref[...], c * nrows, nrows)
        eb, enb = jnp.exp(b), jnp.exp(-b)
        scale = GLA_DK ** -0.5
        gq = q_ref[...] * scale * eb
        gk = k_ref[...] * enb
        v = v_ref[...]
        gnw_v = gnw_ref[...]
        tril = _tril64()
        is_last = lax.broadcasted_iota(jnp.int32, (GLA_CHUNK, 1), 0) == GLA_CHUNK - 1
        dgnw = jnp.zeros((1, GLA_DV), F32)
        for h in range(GLA_HEADS):
            s64 = slice(h * GLA_DK, (h + 1) * GLA_DK)
            s128 = slice(h * GLA_DV, (h + 1) * GLA_DV)
            dsp = dstate[h]
            for gi in reversed(range(group)):
                rs = slice(gi * GLA_CHUNK, (gi + 1) * GLA_CHUNK)
                qh, kh, vh = gq[rs, s64], gk[rs, s64], v[rs, s128]
                ebh, enbh = eb[rs, s64], enb[rs, s64]
                eblh = eb[(gi + 1) * GLA_CHUNK - 1:(gi + 1) * GLA_CHUNK, s64]
                klh = kh * eblh
                st = st_ref[gi, h]
                o, rh, dout = oraw_ref[rs, s128], r_ref[rs, s128], dog_ref[rs, s128]
                rstd = lax.rsqrt(jnp.mean(o * o, axis=-1, keepdims=True) + EPS)
                on = o * rstd
                sg = _sigmoid(rh)
                dr_ref[rs, s128] = (dout * (on * gnw_v) * (sg * (1.0 + rh * (1.0 - sg)))).astype(ACT_DTYPE)
                dy = dout * (rh * sg)
                dgnw = dgnw + jnp.sum(dy * on, axis=0, keepdims=True)
                don = dy * gnw_v
                do = rstd * (don - on * jnp.mean(don * on, axis=-1, keepdims=True))
                a = jnp.where(tril, _mm_nt(qh, kh), 0.0)
                da = jnp.where(tril, _mm_nt(do, vh), 0.0)
                dkl = _mm(vh, dsp)
                dv_ref[rs, s128] = (_mm_tn(a, do) + _mm_nt(klh, dsp)).astype(ACT_DTYPE)
                debl = jnp.sum(dsp * st, axis=0, keepdims=True)
                dgq = _mm(da, kh) + _mm(do, st)
                dgk = _mm_tn(da, qh)
                dsp = dsp * eblh + _mm_tn(do, qh)
                dq_ref[rs, s64] = (dgq * (scale * ebh)).astype(ACT_DTYPE)
                dk_ref[rs, s64] = ((dgk + dkl * eblh) * enbh).astype(ACT_DTYPE)
                last = debl * eblh + jnp.sum(dkl * klh, axis=0, keepdims=True)
                db_scr[rs, s64] = dgq * qh - dgk * kh - dkl * klh + jnp.where(is_last, last, 0.0)
            dstate[h] = dsp
        dg = jnp.dot(upper.astype(F32), db_scr[...], precision=HIGHEST, preferred_element_type=F32)
        dzg = jnp.where(live, dg * _sigmoid(-zg) * (1.0 / GLA_TAU), 0.0)
        dlr_ref[...] = _mm_nt(dzg, wg).astype(ACT_DTYPE)
        dwg_ref[...] += _mm_tn(lr, dzg)
        dbg_ref[...] += jnp.broadcast_to(jnp.sum(dzg, axis=0, keepdims=True), dbg_ref.shape)
        dgnw_ref[...] += jnp.broadcast_to(dgnw, dgnw_ref.shape)

    nb = lambda w, col: pl.BlockSpec((nrows, w), lambda t: (steps - 1 - t, col // w))
    const = lambda shape: pl.BlockSpec(shape, lambda t: (0,) * len(shape))
    outs = pl.pallas_call(
        body, name="gla_bwd", grid=(steps,),
        in_specs=[nb(256, C_GQ), nb(256, C_GK), nb(512, C_GV), nb(512, C_GR), nb(128, C_LR), nb(512, 0),
                  pl.BlockSpec((group, GLA_HEADS, GLA_DV, GLA_DK), lambda t: (steps - 1 - t, 0, 0, 0)), nb(512, 0),
                  const((128, 256)), const((1, 256)), const((1, 128))] + [ANY] * ns,
        out_specs=[nb(256, 0), nb(256, 0), nb(512, 0), nb(512, 0), nb(128, 0),
                   const((128, 256)), const((8, 256)), const((8, 128))] + [ANY] * ns,
        out_shape=[jax.ShapeDtypeStruct((rows, 256), ACT_DTYPE), jax.ShapeDtypeStruct((rows, 256), ACT_DTYPE),
                   jax.ShapeDtypeStruct((rows, 512), ACT_DTYPE), jax.ShapeDtypeStruct((rows, 512), ACT_DTYPE),
                   jax.ShapeDtypeStruct((rows, 128), ACT_DTYPE), jax.ShapeDtypeStruct((128, 256), F32),
                   jax.ShapeDtypeStruct((8, 256), F32), jax.ShapeDtypeStruct((8, 128), F32)] + jobs.out_shapes,
        scratch_shapes=[pltpu.VMEM((GLA_HEADS, GLA_DV, GLA_DK), F32), pltpu.VMEM((nrows, 256), F32)] + jobs.sems,
        compiler_params=_cp(("arbitrary",)),
    )(proj, proj, proj, proj, proj, oraw, states, dog, wg_p, bg, gnw, *jobs.inputs)
    return outs[:8], jobs.split(outs[8:])


def _in_proj_bwd(h0, dh1, nw, win_p, dgv, dgr, dsq, dgq, dgk, dsk, dsv, dlr, tabs, tm):
    rows = h0.shape[0]

    def body(h_ref, dh1_ref, nw_ref, w_ref, dgv_ref, dgr_ref, dsq_ref, dgq_ref, dgk_ref, dsk_ref, dsv_ref, dlr_ref,
             c_ref, sa_ref, sb_ref, dh0_ref, dw_ref, gnm_ref):
        i = pl.program_id(0)

        @pl.when(i == 0)
        def _():
            dw_ref[...] = jnp.zeros_like(dw_ref)
            gnm_ref[...] = jnp.zeros_like(gnm_ref)

        cos, sa, sb = c_ref[...], sa_ref[...], sb_ref[...]
        dsq_v = (_unrope(dsq_ref[...], cos, sa, sb) * (SWA_HD ** -0.5)).astype(MXU_DTYPE)
        dsk_v = _unrope(dsk_ref[...], cos, sa, sb).astype(MXU_DTYPE)
        dproj = jnp.concatenate(
            [dgv_ref[...].astype(MXU_DTYPE), dgr_ref[...].astype(MXU_DTYPE), dsq_v, dgq_ref[...].astype(MXU_DTYPE),
             dgk_ref[...].astype(MXU_DTYPE), dsk_v, dsv_ref[...].astype(MXU_DTYPE), dlr_ref[...].astype(MXU_DTYPE)],
            axis=1)
        h = h_ref[...]
        rstd = lax.rsqrt(jnp.mean(h * h, axis=-1, keepdims=True) + EPS)
        hn = h * rstd
        nw_v = nw_ref[...]
        u = (hn * nw_v).astype(MXU_DTYPE)
        du = _mm_nt(dproj, w_ref[...])
        dw_ref[...] += _mm_tn(u, dproj)
        gnm_ref[...] += jnp.broadcast_to(jnp.sum(du * hn, axis=0, keepdims=True), gnm_ref.shape)
        dun = du * nw_v
        dh0_ref[...] = dh1_ref[...] + rstd * (dun - hn * jnp.mean(dun * hn, axis=-1, keepdims=True))

    row = lambda w: pl.BlockSpec((tm, w), lambda i: (i, 0))
    return pl.pallas_call(
        body, name="in_proj_bwd", grid=(rows // tm,),
        in_specs=[row(D), row(D), pl.BlockSpec((1, D), lambda i: (0, 0)), pl.BlockSpec((D, DINP), lambda i: (0, 0)),
                  row(512), row(512), row(512), row(256), row(256), row(128), row(128), row(128),
                  row(128), row(128), row(128)],
        out_specs=[row(D), pl.BlockSpec((D, DINP), lambda i: (0, 0)), pl.BlockSpec((8, D), lambda i: (0, 0))],
        out_shape=[jax.ShapeDtypeStruct((rows, D), F32), jax.ShapeDtypeStruct((D, DINP), F32),
                   jax.ShapeDtypeStruct((8, D), F32)],
        compiler_params=_cp(("arbitrary",), 56),
    )(h0, dh1, nw, win_p, dgv, dgr, dsq, dgq, dgk, dsk, dsv, dlr, *tabs)


def _adamw(w, g, m, v):
    m = ADAM_B1 * m + (1.0 - ADAM_B1) * g
    v = ADAM_B2 * v + (1.0 - ADAM_B2) * jnp.square(g)
    m_hat = m / (1.0 - ADAM_B1 ** ADAM_STEP)
    v_hat = v / (1.0 - ADAM_B2 ** ADAM_STEP)
    delta = -ADAM_LR * (m_hat / (jnp.sqrt(v_hat) + ADAM_EPS) + ADAM_WD * w)
    return delta, m, v


def _adamw_shard(where, parts, own, w, m, v, name):
    r, cdim = w.shape
    tr = 128 if r % 128 == 0 else r

    def body(where_ref, p_ref, own_ref, w_ref, m_ref, v_ref, g_ref, d_ref, nm_ref, nv_ref):
        g = ((p_ref[0] + p_ref[1]) + p_ref[2]) + own_ref[...]
        g_ref[...] = g
        d_ref[...], nm_ref[...], nv_ref[...] = _adamw(w_ref[...], g, m_ref[...], v_ref[...])

    spec = pl.BlockSpec((tr, cdim), lambda i, s: (i, 0))
    shape = jax.ShapeDtypeStruct((r, cdim), F32)
    return pl.pallas_call(
        body, name=name,
        grid_spec=pltpu.PrefetchScalarGridSpec(
            num_scalar_prefetch=1, grid=(r // tr,),
            in_specs=[pl.BlockSpec((3, tr, cdim), lambda i, s: (0, i, 0)),
                      pl.BlockSpec((None, tr, cdim), lambda i, s: (s[1], i, 0)), spec, spec, spec],
            out_specs=[spec] * 4),
        out_shape=[shape] * 4,
        compiler_params=_cp(("arbitrary",)),
    )(where, parts, own, w, m, v)


def _adamw_small(w, g, m, v):
    def body(w_ref, g_ref, m_ref, v_ref, d_ref, nm_ref, nv_ref):
        d_ref[...], nm_ref[...], nv_ref[...] = _adamw(w_ref[...], g_ref[...], m_ref[...], v_ref[...])

    vm = pl.BlockSpec(memory_space=pltpu.VMEM)
    shape = jax.ShapeDtypeStruct(w.shape, F32)
    return pl.pallas_call(body, name="adamw_small", in_specs=[vm] * 4, out_specs=[vm] * 3,
                          out_shape=[shape] * 3)(w, g, m, v)


def _add_own_half(where, full, theirs, name):
    _, _, r, cdim = full.shape
    tr = 128 if r % 128 == 0 else r

    def body(where_ref, a_ref, b_ref, o_ref):
        o_ref[...] = a_ref[...] + b_ref[...]

    spec = pl.BlockSpec((4, tr, cdim), lambda i, s: (0, i, 0))
    return pl.pallas_call(
        body, name=name,
        grid_spec=pltpu.PrefetchScalarGridSpec(
            num_scalar_prefetch=1, grid=(r // tr,),
            in_specs=[pl.BlockSpec((None, 4, tr, cdim), lambda i, s: (s[0], 0, i, 0)), spec], out_specs=spec),
        out_shape=jax.ShapeDtypeStruct(theirs.shape, F32), compiler_params=_cp(("arbitrary",)))(where, full, theirs)


def _to_rows128(a):
    return a.reshape(-1, 128)


def _pad_rows128(vec):
    flat = vec.reshape(-1)
    n = -(-flat.shape[0] // 128)
    return jnp.pad(flat, (0, n * 128 - flat.shape[0])).reshape(n, 128)


def kernel(x, meta_tokens, norm_mix_w, w_in, w_gate_up, b_gate, gla_norm_w, sinks, w_out, norm_ff_w, w_ff1, w_ff2, final_norm_w, loss_target, m_meta_tokens, m_norm_mix_w, m_w_in, m_w_gate_up, m_b_gate, m_gla_norm_w, m_sinks, m_w_out, m_norm_ff_w, m_w_ff1, m_w_ff2, m_final_norm_w, v_meta_tokens, v_norm_mix_w, v_w_in, v_w_gate_up, v_b_gate, v_gla_norm_w, v_sinks, v_w_out, v_norm_ff_w, v_w_ff1, v_w_ff2, v_final_norm_w):
    seq = x.shape[1]
    rows = LEAD + seq
    tm = _row_tile(rows)
    tm_small = tm // 2 if tm == 640 else tm
    dev = 4 * lax.axis_index("x") + 2 * lax.axis_index("y") + lax.axis_index("c")

    small_shard = jnp.concatenate([meta_tokens, w_gate_up[0], jnp.zeros((N_META, 96), F32)], axis=1)
    g_in, g_small = _all_gather([w_in[0].astype(WIRE_DTYPE), small_shard])
    later_shards = [w_out[0].astype(WIRE_DTYPE), w_ff1[0].astype(WIRE_DTYPE), w_ff2[0].astype(WIRE_DTYPE)]
    win_full = jnp.transpose(g_in, (1, 0, 2)).reshape(D, DIN)
    cols = lambda r: win_full[:, r[0]:r[1]]
    win_p = jnp.concatenate([cols(O_GV), cols(O_GR), cols(O_SQ), cols(O_GQ), cols(O_GK), cols(O_SK), cols(O_SV),
                             cols(O_LR), jnp.zeros((D, 128 - GLA_RANK), WIRE_DTYPE)], axis=1)
    meta_full = jnp.transpose(g_small[:, :, 0:128], (1, 0, 2)).reshape(N_META, D)
    wg_full = jnp.transpose(g_small[:, :, 128:160], (1, 0, 2)).reshape(GLA_RANK, GLA_HEADS * GLA_DK)
    wg_p = jnp.concatenate([wg_full, jnp.zeros((128 - GLA_RANK, 256), F32)], axis=0)

    h0 = jnp.concatenate([jnp.zeros((META0, D), F32), meta_full, x[0]], axis=0)
    tgt = jnp.concatenate([jnp.zeros((LEAD, D), F32), loss_target[0]], axis=0)
    tabs = _rope_tables(rows)
    proj = _in_proj(h0, norm_mix_w, win_p, tm)
    oraw, og, states, (g_out, g_w1, g_w2) = _gla_fwd(proj, wg_p, b_gate, gla_norm_w, later_shards)
    wout_full = g_out.reshape(D, D)
    w2_full = g_w2.reshape(D_FF, D)
    w1_full = jnp.transpose(g_w1, (1, 0, 2)).reshape(D, D_FF)
    qr, kr, vr = _swa_prep(proj, tabs, tm)
    osw = _swa_fwd(qr, kr, vr, sinks)
    h1, f, ft = _out_proj(h0, og, osw, wout_full, norm_ff_w, tm)
    a, dh2, dh2t, loss_p, gfn_p = _ffn_fwd(f, h1, w1_full, w2_full, tgt, final_norm_w.reshape(1, D), tm)

    da, dh1, gnf_p = _ffn_bwd_act(dh2, a, w1_full, w2_full, h1, norm_ff_w, tm)
    dw1, dw2 = _ffn_bwd_weights(ft, a, da, dh2t, 1664 if rows % 1664 == 0 else tm)
    where = jnp.stack([lax.axis_index("c"), 2 * lax.axis_index("x") + lax.axis_index("y")]).astype(jnp.int32)
    dog, dos, dwout, theirs_ffn = _out_proj_bwd(dh1, og, osw, wout_full, tm, [dw1, dw2])
    sums_ffn = [_add_own_half(where, p, q, "reduce_pair_%d" % (2 + k))
                for k, (p, q) in enumerate(zip([dw1, dw2], theirs_ffn))]
    dsq, dsk, dsv, dsink_p, (parts_ffn, (theirs_wout,)) = _swa_bwd(
        qr, kr, vr, osw, dos, sinks, _Jobs([("chips", sums_ffn), ("sibling", [dwout])]))
    sum_wout = _add_own_half(where, dwout, theirs_wout, "reduce_pair_1")
    (dgq, dgk, dgv, dgr, dlr, dwg_p, dbg_p, dgnw_p), ((parts_wout,),) = _gla_bwd(
        proj, oraw, states, dog, wg_p, b_gate, gla_norm_w, _Jobs([("chips", [sum_wout])]))
    dh0, dwin_p, gnm_p = _in_proj_bwd(h0, dh1, norm_mix_w, win_p, dgv, dgr, dsq, dgq, dgk, dsk, dsv, dlr, tabs, tm_small)
    grad_x = dh0[LEAD:][None]

    pcols = lambda c0, r: dwin_p[:, c0:c0 + (r[1] - r[0])]
    dwin = jnp.concatenate([pcols(C_GQ, O_GQ), pcols(C_GK, O_GK), pcols(C_GV, O_GV), pcols(C_GR, O_GR),
                            pcols(C_LR, O_LR), pcols(C_SQ, O_SQ), pcols(C_SK, O_SK), pcols(C_SV, O_SV)], axis=1)
    dwin = jnp.transpose(dwin.reshape(D, 4, 2, DIN // N_DEV), (2, 1, 0, 3))
    (theirs_win,) = _rs_sibling([dwin])
    sum_win = _add_own_half(where, dwin, theirs_win, "reduce_pair_0")
    (parts_win,) = _rs_chips([sum_win])
    chip_sums = [sum_win, sum_wout] + sums_ffn
    parts = [parts_win, parts_wout] + list(parts_ffn)

    small = [dh0[META0:LEAD], dwg_p[0:GLA_RANK], gnm_p[0:1], dbg_p[0:1], dgnw_p[0:1], dsink_p[:, 0], gnf_p[0:1],
             gfn_p[0:1], loss_p[0:1, 0:1]]
    sizes = [-(-s.size // 128) for s in small]
    pack = jnp.concatenate([_pad_rows128(s) for s in small], axis=0)
    pad_rows = -pack.shape[0] % 8
    pack = jnp.pad(pack, ((0, pad_rows), (0, 0)))
    total = _all_reduce_small(pack)
    offs = [sum(sizes[:k]) for k in range(len(sizes))]
    take = lambda k, shape: total[offs[k]:offs[k] + sizes[k]].reshape(-1)[:small[k].size].reshape(shape)
    g_meta_full = take(0, (N_META, D))
    g_wg_full = take(1, (GLA_RANK, 256))
    g_meta = lax.dynamic_slice_in_dim(g_meta_full, dev * 128, 128, axis=1)
    g_wg = lax.dynamic_slice_in_dim(g_wg_full, dev * 32, 32, axis=1)[None]
    g_norm_mix, g_b_gate, g_gla_norm = take(2, (1, D)), take(3, (1, 256)), take(4, (1, 128))
    g_sinks, g_norm_ff, g_final_norm = take(5, (1, 8)), take(6, (1, D)), take(7, (D,))
    loss = take(8, ())

    g_win, d_win, nm_win, nv_win = _adamw_shard(where, parts[0], chip_sums[0], w_in[0], m_w_in[0], v_w_in[0], "adamw_w_in")
    g_wout, d_wout, nm_wout, nv_wout = _adamw_shard(where, parts[1], chip_sums[1], w_out[0], m_w_out[0], v_w_out[0], "adamw_w_out")
    g_w1s, d_w1, nm_w1, nv_w1 = _adamw_shard(where, parts[2], chip_sums[2], w_ff1[0], m_w_ff1[0], v_w_ff1[0], "adamw_w_ff1")
    g_w2s, d_w2, nm_w2, nv_w2 = _adamw_shard(where, parts[3], chip_sums[3], w_ff2[0], m_w_ff2[0], v_w_ff2[0], "adamw_w_ff2")

    names = ["meta", "wg", "norm_mix", "b_gate", "gla_norm", "sinks", "norm_ff", "final_norm"]
    ws = [meta_tokens, w_gate_up, norm_mix_w, b_gate, gla_norm_w, sinks, norm_ff_w, final_norm_w]
    gs = [g_meta, g_wg, g_norm_mix, g_b_gate, g_gla_norm, g_sinks, g_norm_ff, g_final_norm]
    ms = [m_meta_tokens, m_w_gate_up, m_norm_mix_w, m_b_gate, m_gla_norm_w, m_sinks, m_norm_ff_w, m_final_norm_w]
    vs = [v_meta_tokens, v_w_gate_up, v_norm_mix_w, v_b_gate, v_gla_norm_w, v_sinks, v_norm_ff_w, v_final_norm_w]
    ssz = [-(-w.size // 128) for w in ws]
    packed = []
    for group in (ws, gs, ms, vs):
        p = jnp.concatenate([_pad_rows128(t) for t in group], axis=0)
        packed.append(jnp.pad(p, ((0, -p.shape[0] % 8), (0, 0))))
    d_s, nm_s, nv_s = _adamw_small(*packed)
    soffs = [sum(ssz[:k]) for k in range(len(ssz))]
    unpack = lambda t, k: t[soffs[k]:soffs[k] + ssz[k]].reshape(-1)[:ws[k].size].reshape(ws[k].shape)
    d_small = {n: unpack(d_s, k) for k, n in enumerate(names)}
    nm_small = {n: unpack(nm_s, k) for k, n in enumerate(names)}
    nv_small = {n: unpack(nv_s, k) for k, n in enumerate(names)}
    g_small_d = dict(zip(names, gs))

    def ordered(big, small_d):
        win_v, wout_v, w1_v, w2_v = big
        return (small_d["meta"], small_d["norm_mix"], win_v[None], small_d["wg"], small_d["b_gate"],
                small_d["gla_norm"], small_d["sinks"], wout_v[None], small_d["norm_ff"], w1_v[None], w2_v[None],
                small_d["final_norm"])

    return (loss, grad_x,
            *ordered((g_win, g_wout, g_w1s, g_w2s), g_small_d),
            *ordered((d_win, d_wout, d_w1, d_w2), d_small),
            *ordered((nm_win, nm_wout, nm_w1, nm_w2), nm_small),
            *ordered((nv_win, nv_wout, nv_w1, nv_w2), nv_small))
```

```python
import functools

import jax
import jax.numpy as jnp
from jax import lax
from jax.experimental import pallas as pl
from jax.experimental.pallas import tpu as pltpu

F32 = jnp.float32
MXU_DTYPE = jnp.bfloat16
ACT_DTYPE = jnp.bfloat16
WIRE_DTYPE = jnp.bfloat16

D = 1024
N_META = 16
LEAD = 128
META0 = LEAD - N_META
EPS = 1e-5
GLA_HEADS, GLA_DK, GLA_DV, GLA_RANK, GLA_CHUNK = 4, 64, 128, 16, 64
GLA_TAU = 16.0
SWA_HEADS, SWA_KV, SWA_GROUP, SWA_HD, SWA_BLOCK = 8, 2, 4, 64, 128
ROPE_DIM, ROPE_THETA = 16, 500000.0
D_FF = 4096
N_DEV = 8
FF_TILE = D_FF // N_DEV
FF_WIDE = 1024
NEG = -1e30

C_GV, C_GR, C_SQ, C_GQ, C_GK, C_SK, C_SV, C_LR = 0, 512, 1024, 1536, 1792, 2048, 2176, 2304
DINP = 2432
DIN = 2320
O_GQ, O_GK, O_GV, O_GR, O_LR, O_SQ, O_SK, O_SV = (0, 256), (256, 512), (512, 1024), (1024, 1536), (1536, 1552), (1552, 2064), (2064, 2192), (2192, 2320)

ADAM_LR, ADAM_B1, ADAM_B2, ADAM_EPS, ADAM_WD, ADAM_STEP = 0.001, 0.9, 0.999, 1e-08, 0.01, 10

MESH = pl.DeviceIdType.MESH
ANY = pl.BlockSpec(memory_space=pl.ANY)
HIGHEST = lax.Precision.HIGHEST


def _cp(sem=None, vmem_mb=None):
    kw = {}
    if sem is not None:
        kw["dimension_semantics"] = sem
    if vmem_mb is not None:
        kw["vmem_limit_bytes"] = vmem_mb << 20
    return pltpu.CompilerParams(**kw)


def _mm(a, b):
    return jnp.dot(a.astype(MXU_DTYPE), b.astype(MXU_DTYPE), preferred_element_type=F32)


def _mm_nt(a, b):
    return lax.dot_general(a.astype(MXU_DTYPE), b.astype(MXU_DTYPE), (((1,), (1,)), ((), ())),
                           preferred_element_type=F32)


def _mm_tn(a, b):
    return lax.dot_general(a.astype(MXU_DTYPE), b.astype(MXU_DTYPE), (((0,), (0,)), ((), ())),
                           preferred_element_type=F32)


def _logsigmoid(z):
    return jnp.minimum(z, 0.0) - jnp.log(1.0 + jnp.exp(-jnp.abs(z)))


def _sigmoid(z):
    return 1.0 / (1.0 + jnp.exp(-z))


def _row_tile(rows):
    return 640 if rows % 640 == 0 else 128


def _mesh_pos():
    return lax.axis_index("x"), lax.axis_index("y"), lax.axis_index("c")


def _all_gather(shards):
    n = len(shards)

    def body(*refs):
        start, forward, finish = _gather_schedule(refs[:n], refs[n:2 * n], *refs[2 * n:])
        start()
        for j in range(3):
            forward(j)
        finish()

    gathered = pl.pallas_call(
        body, name="all_gather_weights",
        out_shape=_gathered_shapes(shards), in_specs=[ANY] * n, out_specs=[ANY] * n,
        scratch_shapes=_gather_sems(n),
    )(*shards)
    return _with_own_block(gathered, shards)


def _gathered_shapes(shards):
    return [jax.ShapeDtypeStruct((N_DEV,) + s.shape, s.dtype) for s in shards]


def _gather_sems(n):
    return [pltpu.SemaphoreType.DMA((7 * n,)), pltpu.SemaphoreType.DMA((7 * n,))]


def _with_own_block(gathered, shards):
    dev = 4 * lax.axis_index("x") + 2 * lax.axis_index("y") + lax.axis_index("c")
    return [lax.dynamic_update_index_in_dim(g, s, dev, 0) for g, s in zip(gathered, shards)]


def _gather_schedule(ins, outs, send_sems, recv_sems):
    n = len(ins)
    x, y, c = _mesh_pos()
    me, sibling = (x, y, c), (x, y, 1 - c)
    chips = [(1 - x, y), (x, 1 - y), (1 - x, 1 - y)]

    def copy(a, k, block, to, src=None):
        dst = outs[a].at[4 * block[0] + 2 * block[1] + block[2]]
        return pltpu.make_async_remote_copy(
            src_ref=dst if src is None else src, dst_ref=dst,
            send_sem=send_sems.at[a * 7 + k], recv_sem=recv_sems.at[a * 7 + k],
            device_id=to, device_id_type=MESH)

    def first(a):
        return [copy(a, 0, me, sibling, src=ins[a])] + [copy(a, 1 + j, me, (*chip, c), src=ins[a])
                                                        for j, chip in enumerate(chips)]

    def start():
        for a in range(n):
            for cp in first(a):
                cp.start()

    def forward(j):
        for a in range(n):
            copy(a, 1 + j, (*chips[j], c), me).wait_recv()
            copy(a, 4 + j, (*chips[j], c), sibling).start()

    def finish():
        for a in range(n):
            copy(a, 0, sibling, me).wait_recv()
            for j, chip in enumerate(chips):
                copy(a, 4 + j, (*chip, 1 - c), me).wait_recv()
        for a in range(n):
            for cp in first(a) + [copy(a, 4 + j, (*chip, c), sibling) for j, chip in enumerate(chips)]:
                cp.wait_send()

    return start, forward, finish


def _rs_sibling(gs):
    n = len(gs)

    def body(*refs):
        start, finish = _sibling_schedule(refs[:n], refs[n:2 * n], *refs[2 * n:])
        start()
        finish()

    return pl.pallas_call(
        body, name="reduce_scatter_sibling",
        out_shape=_sibling_shapes(gs), in_specs=[ANY] * n, out_specs=[ANY] * n,
        scratch_shapes=_sibling_sems(n),
    )(*gs)


def _sibling_shapes(gs):
    return [jax.ShapeDtypeStruct(g.shape[1:], g.dtype) for g in gs]


def _sibling_sems(n):
    return [pltpu.SemaphoreType.DMA((n,)), pltpu.SemaphoreType.DMA((n,))]


def _sibling_schedule(ins, land, send_sems, recv_sems):
    x, y, c = _mesh_pos()

    def copies():
        return [pltpu.make_async_remote_copy(
            src_ref=ins[a].at[1 - c], dst_ref=land[a], send_sem=send_sems.at[a], recv_sem=recv_sems.at[a],
            device_id=(x, y, 1 - c), device_id_type=MESH) for a in range(len(ins))]

    def start():
        for cp in copies():
            cp.start()

    def finish():
        for cp in copies():
            cp.wait_recv()
        for cp in copies():
            cp.wait_send()

    return start, finish


def _rs_chips(ps):
    n = len(ps)

    def body(*refs):
        start, finish = _chips_schedule(refs[:n], refs[n:2 * n], *refs[2 * n:])
        start()
        finish()

    return pl.pallas_call(
        body, name="reduce_scatter_chips",
        out_shape=_chips_shapes(ps), in_specs=[ANY] * n, out_specs=[ANY] * n,
        scratch_shapes=_chips_sems(n),
    )(*ps)


def _chips_shapes(ps):
    return [jax.ShapeDtypeStruct((3,) + p.shape[1:], p.dtype) for p in ps]


def _chips_sems(n):
    return [pltpu.SemaphoreType.DMA((3 * n,)), pltpu.SemaphoreType.DMA((3 * n,))]


def _chips_schedule(ins, land, send_sems, recv_sems):
    x, y, c = _mesh_pos()
    chips = [(1 - x, y), (x, 1 - y), (1 - x, 1 - y)]

    def copies():
        return [pltpu.make_async_remote_copy(
            src_ref=ins[a].at[2 * chip[0] + chip[1]], dst_ref=land[a].at[j],
            send_sem=send_sems.at[3 * a + j], recv_sem=recv_sems.at[3 * a + j],
            device_id=(*chip, c), device_id_type=MESH) for a in range(len(ins)) for j, chip in enumerate(chips)]

    def start():
        for cp in copies():
            cp.start()

    def finish():
        for cp in copies():
            cp.wait_recv()
        for cp in copies():
            cp.wait_send()

    return start, finish


class _Jobs:
    def __init__(self, jobs):
        self.jobs = jobs
        self.inputs = [a for _, arrs in jobs for a in arrs]
        self.out_shapes = [s for kind, arrs in jobs
                           for s in (_sibling_shapes(arrs) if kind == "sibling" else _chips_shapes(arrs))]
        self.sems = [s for kind, arrs in jobs
                     for s in (_sibling_sems(len(arrs)) if kind == "sibling" else _chips_sems(len(arrs)))]
        self.n = len(self.inputs)

    def bind(self, in_refs, out_refs, sem_refs):
        starts, finishes, at = [], [], 0
        for k, (kind, arrs) in enumerate(self.jobs):
            schedule = _sibling_schedule if kind == "sibling" else _chips_schedule
            start, finish = schedule(in_refs[at:at + len(arrs)], out_refs[at:at + len(arrs)],
                                     sem_refs[2 * k], sem_refs[2 * k + 1])
            starts.append(start)
            finishes.append(finish)
            at += len(arrs)

        def start_all():
            for f in starts:
                f()

        def finish_all():
            for f in finishes:
                f()

        return start_all, finish_all

    def split(self, outs):
        res, at = [], 0
        for _, arrs in self.jobs:
            res.append(list(outs[at:at + len(arrs)]))
            at += len(arrs)
        return res


def _all_reduce_small(pack):
    rows = pack.shape[0]

    def body(p_ref, out_ref, land, send_sems, recv_sems):
        x, y, c = _mesh_pos()
        me = 4 * x + 2 * y + c
        land[me] = p_ref[...]
        copies = []
        for k in range(1, N_DEV):
            bx, by, bc = (k >> 2) & 1, (k >> 1) & 1, k & 1
            peer = (1 - x if bx else x, 1 - y if by else y, 1 - c if bc else c)
            copies.append(pltpu.make_async_remote_copy(
                src_ref=p_ref, dst_ref=land.at[me], send_sem=send_sems.at[k - 1], recv_sem=recv_sems.at[k - 1],
                device_id=peer, device_id_type=MESH))
        for cp in copies:
            cp.start()
        for cp in copies:
            cp.wait_recv()
        for cp in copies:
            cp.wait_send()
        acc = land[0]
        for d in range(1, N_DEV):
            acc = acc + land[d]
        out_ref[...] = acc

    return pl.pallas_call(
        body, name="all_reduce_small",
        out_shape=jax.ShapeDtypeStruct(pack.shape, F32),
        in_specs=[pl.BlockSpec(memory_space=pltpu.VMEM)], out_specs=pl.BlockSpec(memory_space=pltpu.VMEM),
        scratch_shapes=[pltpu.VMEM((N_DEV, rows, 128), F32), pltpu.SemaphoreType.DMA((7,)),
                        pltpu.SemaphoreType.DMA((7,))],
    )(pack)


def _token_specs(tm, grid_rank=1):
    nb = tm // LEAD

    def spec(k):
        if grid_rank == 1:
            return pl.BlockSpec((LEAD, D), lambda i: (jnp.maximum(i * nb + k - 1, 0), 0))
        return pl.BlockSpec((LEAD, D), lambda i, j: (jnp.maximum(i * nb + k - 1, 0), 0))

    return [spec(k) for k in range(nb)]


def _h_tile(i, lead_ref, x_refs):
    first = jnp.where(i == 0, lead_ref[...], x_refs[0][...])
    return jnp.concatenate([first] + [r[...] for r in x_refs[1:]], axis=0)


def _in_proj(x, lead, nw, win_p, tm):
    rows = LEAD + x.shape[0]
    nb = tm // LEAD

    def body(*refs):
        x_refs, (lead_ref, nw_ref, w_ref, o_ref) = refs[:nb], refs[nb:]
        h = _h_tile(pl.program_id(0), lead_ref, x_refs)
        rstd = lax.rsqrt(jnp.mean(h * h, axis=-1, keepdims=True) + EPS)
        u = (h * rstd * nw_ref[...]).astype(MXU_DTYPE)
        o_ref[...] = jnp.dot(u, w_ref[...].astype(MXU_DTYPE), preferred_element_type=F32)

    return pl.pallas_call(
        body, name="in_proj", grid=(rows // tm,),
        in_specs=_token_specs(tm) + [pl.BlockSpec((LEAD, D), lambda i: (0, 0)), pl.BlockSpec((1, D), lambda i: (0, 0)),
                                     pl.BlockSpec((D, DINP), lambda i: (0, 0))],
        out_specs=pl.BlockSpec((tm, DINP), lambda i: (i, 0)),
        out_shape=jax.ShapeDtypeStruct((rows, DINP), F32),
        compiler_params=_cp(("arbitrary",), 56),
    )(*([x] * nb), lead, nw, win_p)


def _rope_tables(rows):
    pos = (jnp.arange(rows, dtype=jnp.int32) - META0).astype(F32)
    inv_freq = 1.0 / (ROPE_THETA ** (jnp.arange(0, ROPE_DIM, 2, dtype=F32) / ROPE_DIM))
    ang = pos[:, None] * jnp.tile(inv_freq, 128 // (ROPE_DIM // 2))[None, :]
    in_head = jnp.arange(128, dtype=jnp.int32)[None, :] % SWA_HD
    cos, sin = jnp.cos(ang), jnp.sin(ang)
    c_tab = jnp.where(in_head < ROPE_DIM, cos, 1.0)
    sa_tab = jnp.where(in_head < ROPE_DIM // 2, -sin, 0.0)
    sb_tab = jnp.where((in_head >= ROPE_DIM // 2) & (in_head < ROPE_DIM), sin, 0.0)
    return c_tab, sa_tab, sb_tab


def _rope(xv, cos, sa, sb):
    width = xv.shape[1]
    reps = width // 128
    if reps > 1:
        cos, sa, sb = (jnp.tile(t, (1, reps)) for t in (cos, sa, sb))
    return xv * cos + pltpu.roll(xv, width - 8, 1) * sa + pltpu.roll(xv, 8, 1) * sb


def _unrope(dy, cos, sa, sb):
    width = dy.shape[1]
    reps = width // 128
    if reps > 1:
        cos, sa, sb = (jnp.tile(t, (1, reps)) for t in (cos, sa, sb))
    return dy * cos + pltpu.roll(dy * sa, 8, 1) + pltpu.roll(dy * sb, width - 8, 1)


def _swa_prep(proj, tabs, tm):
    rows = proj.shape[0]

    def body(q_ref, k_ref, v_ref, c_ref, sa_ref, sb_ref, qo_ref, ko_ref, vo_ref):
        cos, sa, sb = c_ref[...], sa_ref[...], sb_ref[...]
        qo_ref[...] = (_rope(q_ref[...], cos, sa, sb) * (SWA_HD ** -0.5)).astype(ACT_DTYPE)
        ko_ref[...] = _rope(k_ref[...], cos, sa, sb).astype(ACT_DTYPE)
        vo_ref[...] = v_ref[...].astype(ACT_DTYPE)

    tab_spec = pl.BlockSpec((tm, 128), lambda i: (i, 0))
    return pl.pallas_call(
        body, name="swa_prep", grid=(rows // tm,),
        in_specs=[pl.BlockSpec((tm, 512), lambda i: (i, C_SQ // 512)),
                  pl.BlockSpec((tm, 128), lambda i: (i, C_SK // 128)),
                  pl.BlockSpec((tm, 128), lambda i: (i, C_SV // 128)), tab_spec, tab_spec, tab_spec],
        out_specs=[pl.BlockSpec((tm, 512), lambda i: (i, 0)), tab_spec, tab_spec],
        out_shape=[jax.ShapeDtypeStruct((rows, 512), ACT_DTYPE), jax.ShapeDtypeStruct((rows, 128), ACT_DTYPE),
                   jax.ShapeDtypeStruct((rows, 128), ACT_DTYPE)],
        compiler_params=_cp(("arbitrary",)),
    )(proj, proj, proj, *tabs)


def _gla_group(nc):
    for g in (5, 2):
        if nc % g == 0:
            return g
    return 1


def _gla_gates(lr, wg, bg, first_row, nrows):
    zg = _mm(lr, wg) + bg
    row = first_row + lax.broadcasted_iota(jnp.int32, (nrows, 1), 0)
    live = row >= META0
    g = jnp.where(live, _logsigmoid(zg) * (1.0 / GLA_TAU), 0.0)
    ii = lax.broadcasted_iota(jnp.int32, (nrows, nrows), 0)
    jj = lax.broadcasted_iota(jnp.int32, (nrows, nrows), 1)
    same = (ii // GLA_CHUNK) == (jj // GLA_CHUNK)
    lower, upper = same & (jj <= ii), same & (jj >= ii)
    b = jnp.dot(lower.astype(F32), g, precision=HIGHEST, preferred_element_type=F32)
    return zg, live, lower, upper, b


def _tril64():
    ii = lax.broadcasted_iota(jnp.int32, (GLA_CHUNK, GLA_CHUNK), 0)
    jj = lax.broadcasted_iota(jnp.int32, (GLA_CHUNK, GLA_CHUNK), 1)
    return jj <= ii


def _gla_fwd(proj, wg_p, bg, gnw, shards):
    rows = proj.shape[0]
    nc = rows // GLA_CHUNK
    group = _gla_group(nc)
    steps, nrows = nc // group, group * GLA_CHUNK
    ns = len(shards)
    forward_steps = [steps * 7 // 10, steps * 8 // 10, steps * 9 // 10]

    def body(q_ref, k_ref, v_ref, r_ref, lr_ref, wg_ref, bg_ref, gnw_ref, *rest):
        shard_refs, rest = rest[:ns], rest[ns:]
        oraw_ref, og_ref, st_ref = rest[:3]
        gathered_refs, rest = rest[3:3 + ns], rest[3 + ns:]
        state, send_sems, recv_sems = rest
        c = pl.program_id(0)
        start, forward, finish = _gather_schedule(shard_refs, gathered_refs, send_sems, recv_sems)

        @pl.when(c == 0)
        def _():
            state[...] = jnp.zeros_like(state)
            start()

        for j, step in enumerate(forward_steps):
            pl.when(c == step)(functools.partial(forward, j))
        pl.when(c == steps - 1)(finish)

        _, _, _, _, b = _gla_gates(lr_ref[...], wg_ref[...], bg_ref[...], c * nrows, nrows)
        eb = jnp.exp(b)
        gq = q_ref[...] * (GLA_DK ** -0.5) * eb
        gk = k_ref[...] * jnp.exp(-b)
        v = v_ref[...]
        gnw_v = gnw_ref[...]
        tril = _tril64()
        for h in range(GLA_HEADS):
            s64 = slice(h * GLA_DK, (h + 1) * GLA_DK)
            s128 = slice(h * GLA_DV, (h + 1) * GLA_DV)
            st = state[h]
            for gi in range(group):
                rs = slice(gi * GLA_CHUNK, (gi + 1) * GLA_CHUNK)
                qh, kh, vh = gq[rs, s64], gk[rs, s64], v[rs, s128]
                eblh = eb[(gi + 1) * GLA_CHUNK - 1:(gi + 1) * GLA_CHUNK, s64]
                st_ref[gi, h] = st
                a = jnp.where(tril, _mm_nt(qh, kh), 0.0)
                o = _mm(a, vh) + _mm_nt(qh, st)
                st = st * eblh + _mm_tn(vh, kh * eblh)
                oraw_ref[rs, s128] = o
                rstd = lax.rsqrt(jnp.mean(o * o, axis=-1, keepdims=True) + EPS)
                rh = r_ref[rs, s128]
                og_ref[rs, s128] = (o * rstd * gnw_v * (rh * _sigmoid(rh))).astype(ACT_DTYPE)
            state[h] = st

    nb = lambda w, col: pl.BlockSpec((nrows, w), lambda c: (c, col // w))
    const = lambda shape: pl.BlockSpec(shape, lambda c: (0,) * len(shape))
    outs = pl.pallas_call(
        body, name="gla_fwd", grid=(steps,),
        in_specs=[nb(256, C_GQ), nb(256, C_GK), nb(512, C_GV), nb(512, C_GR), nb(128, C_LR),
                  const((128, 256)), const((1, 256)), const((1, 128))] + [ANY] * ns,
        out_specs=[pl.BlockSpec((nrows, 512), lambda c: (c, 0)), pl.BlockSpec((nrows, 512), lambda c: (c, 0)),
                   pl.BlockSpec((group, GLA_HEADS, GLA_DV, GLA_DK), lambda c: (c, 0, 0, 0))] + [ANY] * ns,
        out_shape=[jax.ShapeDtypeStruct((rows, 512), F32), jax.ShapeDtypeStruct((rows, 512), ACT_DTYPE),
                   jax.ShapeDtypeStruct((nc, GLA_HEADS, GLA_DV, GLA_DK), F32)] + _gathered_shapes(shards),
        scratch_shapes=[pltpu.VMEM((GLA_HEADS, GLA_DV, GLA_DK), F32)] + _gather_sems(ns),
        compiler_params=_cp(("arbitrary",)),
    )(proj, proj, proj, proj, proj, wg_p, bg, gnw, *shards)
    return outs[0], outs[1], outs[2], _with_own_block(outs[3:], shards)


def _swa_mask(n):
    shape = (SWA_GROUP * SWA_BLOCK, 3 * SWA_BLOCK)
    qi = lax.broadcasted_iota(jnp.int32, shape, 0) & (SWA_BLOCK - 1)
    jj = lax.broadcasted_iota(jnp.int32, shape, 1)
    meta = (jj < SWA_BLOCK) & (jj >= META0) & ((n > 0) | (jj <= qi))
    prev = (jj >= SWA_BLOCK) & (jj < 2 * SWA_BLOCK) & (n >= 2) & (jj - SWA_BLOCK > qi)
    cur = (jj >= 2 * SWA_BLOCK) & (n >= 1) & (jj - 2 * SWA_BLOCK <= qi)
    return meta | prev | cur


def _stack_heads(t, kvh):
    return jnp.concatenate([t[:, (kvh * SWA_GROUP + g) * SWA_HD:(kvh * SWA_GROUP + g + 1) * SWA_HD]
                            for g in range(SWA_GROUP)], axis=0)


def _stack_sinks(sink_ref, kvh):
    return jnp.concatenate([jnp.full((SWA_BLOCK, 1), sink_ref[0, kvh * SWA_GROUP + g], F32)
                            for g in range(SWA_GROUP)], axis=0)


def _swa_group(nblk):
    return 5 if nblk % 5 == 0 else 1


def _swa_specs(group):
    blk = lambda w: pl.BlockSpec((group * SWA_BLOCK, w), lambda n: (n, 0))
    first = pl.BlockSpec((SWA_BLOCK, 128), lambda n: (0, 0))
    prev = pl.BlockSpec((SWA_BLOCK, 128), lambda n: (jnp.maximum(n * group - 1, 0), 0))
    return blk, first, prev


def _swa_keys(first_ref, prev_ref, cur_ref, g):
    own = cur_ref[g * SWA_BLOCK:(g + 1) * SWA_BLOCK, :]
    before = prev_ref[...] if g == 0 else cur_ref[(g - 1) * SWA_BLOCK:g * SWA_BLOCK, :]
    return jnp.concatenate([first_ref[...], before, own], axis=0)


def _swa_fwd(qr, kr, vr, sinks):
    rows = qr.shape[0]
    nblk = rows // SWA_BLOCK
    group = _swa_group(nblk)

    def body(q_ref, k0, kp, kc, v0, vp, vc, sink_ref, o_ref):
        for g in range(group):
            n = pl.program_id(0) * group + g
            rs = slice(g * SWA_BLOCK, (g + 1) * SWA_BLOCK)
            kall, vall = _swa_keys(k0, kp, kc, g), _swa_keys(v0, vp, vc, g)
            mask = _swa_mask(n)[0:SWA_BLOCK]
            for head in range(SWA_HEADS):
                hs = slice(head * SWA_HD, (head + 1) * SWA_HD)
                kv = slice((head // SWA_GROUP) * SWA_HD, (head // SWA_GROUP + 1) * SWA_HD)
                s = jnp.where(mask, _mm_nt(q_ref[rs, hs], kall[:, kv]), NEG)
                sink = sink_ref[0, head]
                m = jnp.maximum(jnp.max(s, axis=-1, keepdims=True), sink)
                p = jnp.exp(s - m)
                den = jnp.sum(p, axis=-1, keepdims=True) + jnp.exp(sink - m)
                o_ref[rs, hs] = (_mm(p, vall[:, kv]) / den).astype(ACT_DTYPE)

    blk, first, prev = _swa_specs(group)
    return pl.pallas_call(
        body, name="swa_fwd", grid=(nblk // group,),
        in_specs=[blk(512), first, prev, blk(128), first, prev, blk(128),
                  pl.BlockSpec(memory_space=pltpu.SMEM)],
        out_specs=blk(512),
        out_shape=jax.ShapeDtypeStruct((rows, 512), ACT_DTYPE),
        compiler_params=_cp(("arbitrary",)),
    )(qr, kr, kr, kr, vr, vr, vr, sinks)


def _out_proj(x, lead, og, osw, wout, nfw, tm):
    rows = LEAD + x.shape[0]
    nb = tm // LEAD

    def body(*refs):
        x_refs, (lead_ref, og_ref, os_ref, w_ref, nw_ref, h1_ref, f_ref, ft_ref) = refs[:nb], refs[nb:]
        h0 = _h_tile(pl.program_id(0), lead_ref, x_refs)
        h1 = h0 + _mm(og_ref[...], w_ref[0:512, :]) + _mm(os_ref[...], w_ref[512:1024, :])
        h1_ref[...] = h1
        rstd = lax.rsqrt(jnp.mean(h1 * h1, axis=-1, keepdims=True) + EPS)
        f = h1 * rstd * nw_ref[...]
        f_ref[...] = f.astype(ACT_DTYPE)
        ft_ref[...] = f.T.astype(ACT_DTYPE)

    row = lambda w: pl.BlockSpec((tm, w), lambda i: (i, 0))
    return pl.pallas_call(
        body, name="out_proj", grid=(rows // tm,),
        in_specs=_token_specs(tm) + [pl.BlockSpec((LEAD, D), lambda i: (0, 0)), row(512), row(512),
                                     pl.BlockSpec((D, D), lambda i: (0, 0)), pl.BlockSpec((1, D), lambda i: (0, 0))],
        out_specs=[row(D), row(D), pl.BlockSpec((D, tm), lambda i: (0, i))],
        out_shape=[jax.ShapeDtypeStruct((rows, D), F32), jax.ShapeDtypeStruct((rows, D), ACT_DTYPE),
                   jax.ShapeDtypeStruct((D, rows), ACT_DTYPE)],
        compiler_params=_cp(("arbitrary",), 48),
    )(*([x] * nb), lead, og, osw, wout, nfw)


def _ffn_fwd(f, h1, w1, w2, tgt, fnw, tm):
    rows = f.shape[0]
    nj = D_FF // FF_WIDE
    nb = tm // LEAD

    def body(f_ref, h1_ref, w1_ref, w2_ref, nw_ref, *rest):
        t_refs, (a_ref, dh2_ref, dh2t_ref, loss_ref, gfn_ref, acc) = rest[:nb], rest[nb:]
        i, j = pl.program_id(0), pl.program_id(1)

        @pl.when((i == 0) & (j == 0))
        def _():
            loss_ref[...] = jnp.zeros_like(loss_ref)
            gfn_ref[...] = jnp.zeros_like(gfn_ref)

        @pl.when(j == 0)
        def _():
            acc[...] = jnp.zeros_like(acc)

        a = _mm(f_ref[...], w1_ref[...])
        a_ref[...] = a.astype(ACT_DTYPE)
        z = jnp.square(jnp.maximum(a, 0.0))
        acc[...] += _mm(z, w2_ref[...])

        @pl.when(j == nj - 1)
        def _():
            h2 = h1_ref[...] + acc[...]
            rstd = lax.rsqrt(jnp.mean(h2 * h2, axis=-1, keepdims=True) + EPS)
            hn = h2 * rstd
            nw = nw_ref[...]
            row = i * tm + lax.broadcasted_iota(jnp.int32, (tm, 1), 0)
            target = jnp.concatenate([t[...] for t in t_refs], axis=0)
            err = jnp.where(row >= LEAD, hn * nw - target, 0.0)
            row_loss = jnp.sum(err * err, axis=-1, keepdims=True) * (1.0 / D)
            loss_ref[...] += jnp.broadcast_to(0.5 * jnp.sum(row_loss, axis=0, keepdims=True), loss_ref.shape)
            dy = err * (1.0 / D)
            gfn_ref[...] += jnp.broadcast_to(jnp.sum(dy * hn, axis=0, keepdims=True), gfn_ref.shape)
            dhn = dy * nw
            dh2 = rstd * (dhn - hn * jnp.mean(dhn * hn, axis=-1, keepdims=True))
            dh2_ref[...] = dh2
            dh2t_ref[...] = dh2.T.astype(ACT_DTYPE)

    return pl.pallas_call(
        body, name="ffn_fwd", grid=(rows // tm, nj),
        in_specs=[pl.BlockSpec((tm, D), lambda i, j: (i, 0)), pl.BlockSpec((tm, D), lambda i, j: (i, 0)),
                  pl.BlockSpec((D, FF_WIDE), lambda i, j: (0, j)),
                  pl.BlockSpec((FF_WIDE, D), lambda i, j: (j, 0)),
                  pl.BlockSpec((1, D), lambda i, j: (0, 0))] + _token_specs(tm, grid_rank=2),
        out_specs=[pl.BlockSpec((tm, FF_WIDE), lambda i, j: (i, j)), pl.BlockSpec((tm, D), lambda i, j: (i, 0)),
                   pl.BlockSpec((D, tm), lambda i, j: (0, i)),
                   pl.BlockSpec((8, 128), lambda i, j: (0, 0)), pl.BlockSpec((8, D), lambda i, j: (0, 0))],
        out_shape=[jax.ShapeDtypeStruct((rows, D_FF), ACT_DTYPE), jax.ShapeDtypeStruct((rows, D), F32),
                   jax.ShapeDtypeStruct((D, rows), ACT_DTYPE),
                   jax.ShapeDtypeStruct((8, 128), F32), jax.ShapeDtypeStruct((8, D), F32)],
        scratch_shapes=[pltpu.VMEM((tm, D), F32)],
        compiler_params=_cp(("arbitrary", "arbitrary"), 56),
    )(f, h1, w1, w2, fnw, *([tgt] * nb))


def _ffn_bwd_act(dh2, a, w1, w2, h1, nfw, tm):
    rows = dh2.shape[0]
    nj = D_FF // FF_WIDE

    def body(dh2_ref, a_ref, w1_ref, w2_ref, h1_ref, nw_ref, da_ref, dh1_ref, gnf_ref, acc):
        i, j = pl.program_id(0), pl.program_id(1)

        @pl.when((i == 0) & (j == 0))
        def _():
            gnf_ref[...] = jnp.zeros_like(gnf_ref)

        @pl.when(j == 0)
        def _():
            acc[...] = jnp.zeros_like(acc)

        dz = _mm_nt(dh2_ref[...], w2_ref[...])
        da = dz * (2.0 * jnp.maximum(a_ref[...].astype(F32), 0.0))
        da_ref[...] = da.astype(ACT_DTYPE)
        acc[...] += _mm_nt(da, w1_ref[...])

        @pl.when(j == nj - 1)
        def _():
            h1 = h1_ref[...]
            rstd = lax.rsqrt(jnp.mean(h1 * h1, axis=-1, keepdims=True) + EPS)
            hn = h1 * rstd
            df = acc[...]
            gnf_ref[...] += jnp.broadcast_to(jnp.sum(df * hn, axis=0, keepdims=True), gnf_ref.shape)
            dfn = df * nw_ref[...]
            dh1_ref[...] = dh2_ref[...] + rstd * (dfn - hn * jnp.mean(dfn * hn, axis=-1, keepdims=True))

    return pl.pallas_call(
        body, name="ffn_bwd_act", grid=(rows // tm, nj),
        in_specs=[pl.BlockSpec((tm, D), lambda i, j: (i, 0)), pl.BlockSpec((tm, FF_WIDE), lambda i, j: (i, j)),
                  pl.BlockSpec((D, FF_WIDE), lambda i, j: (0, j)),
                  pl.BlockSpec((FF_WIDE, D), lambda i, j: (j, 0)),
                  pl.BlockSpec((tm, D), lambda i, j: (i, 0)), pl.BlockSpec((1, D), lambda i, j: (0, 0))],
        out_specs=[pl.BlockSpec((tm, FF_WIDE), lambda i, j: (i, j)), pl.BlockSpec((tm, D), lambda i, j: (i, 0)),
                   pl.BlockSpec((8, D), lambda i, j: (0, 0))],
        out_shape=[jax.ShapeDtypeStruct((rows, D_FF), ACT_DTYPE), jax.ShapeDtypeStruct((rows, D), F32),
                   jax.ShapeDtypeStruct((8, D), F32)],
        scratch_shapes=[pltpu.VMEM((tm, D), F32)],
        compiler_params=_cp(("arbitrary", "arbitrary"), 56),
    )(dh2, a, w1, w2, h1, nfw)


def _ffn_bwd_weights(ft, a, da, dh2t, tm):
    rows = a.shape[0]
    steps = rows // tm

    def body(ft_ref, a_ref, da_ref, dh2t_ref, dw1_ref, dw2_ref, dw2t):
        i = pl.program_id(1)

        @pl.when(i == 0)
        def _():
            dw1_ref[...] = jnp.zeros_like(dw1_ref)
            dw2t[...] = jnp.zeros_like(dw2t)

        z = jnp.square(jnp.maximum(a_ref[...].astype(F32), 0.0))
        dw1_ref[...] += _mm(ft_ref[...], da_ref[...])
        dw2t[...] += _mm(dh2t_ref[...], z)

        @pl.when(i == steps - 1)
        def _():
            dw2_ref[...] = dw2t[...].T

    return pl.pallas_call(
        body, name="ffn_bwd_weights", grid=(N_DEV, steps),
        in_specs=[pl.BlockSpec((D, tm), lambda j, i: (0, i)), pl.BlockSpec((tm, FF_TILE), lambda j, i: (i, j)),
                  pl.BlockSpec((tm, FF_TILE), lambda j, i: (i, j)), pl.BlockSpec((D, tm), lambda j, i: (0, i))],
        out_specs=[pl.BlockSpec((None, None, D, FF_TILE), lambda j, i: (j % 2, j // 2, 0, 0)),
                   pl.BlockSpec((None, None, FF_TILE, D), lambda j, i: (j % 2, j // 2, 0, 0))],
        out_shape=[jax.ShapeDtypeStruct((2, 4, D, FF_TILE), F32), jax.ShapeDtypeStruct((2, 4, FF_TILE, D), F32)],
        scratch_shapes=[pltpu.VMEM((D, FF_TILE), F32)],
        compiler_params=_cp(("arbitrary", "arbitrary"), 48),
    )(ft, a, da, dh2t)


def _out_proj_bwd(dh1, og, osw, wout, tm, partials):
    rows = dh1.shape[0]
    steps = rows // tm
    ns = len(partials)

    def body(dh1_ref, og_ref, os_ref, w_ref, *rest):
        part_refs, rest = rest[:ns], rest[ns:]
        dog_ref, dos_ref, dw_ref = rest[:3]
        land_refs, (send_sems, recv_sems) = rest[3:3 + ns], rest[3 + ns:]
        i = pl.program_id(0)
        start, finish = _sibling_schedule(part_refs, land_refs, send_sems, recv_sems)

        @pl.when(i == 0)
        def _():
            dw_ref[...] = jnp.zeros_like(dw_ref)
            start()

        pl.when(i == steps - 1)(finish)

        dh1 = dh1_ref[...].astype(MXU_DTYPE)
        dog_ref[...] = _mm_nt(dh1, w_ref[0:512, :])
        dos_ref[...] = _mm_nt(dh1, w_ref[512:1024, :])
        for half, ref in enumerate((og_ref, os_ref)):
            dw = _mm_tn(ref[...], dh1)
            for blk in range(4):
                shard = half * 4 + blk
                dw_ref[shard % 2, shard // 2] += dw[blk * 128:(blk + 1) * 128, :]

    row = lambda w: pl.BlockSpec((tm, w), lambda i: (i, 0))
    outs = pl.pallas_call(
        body, name="out_proj_bwd", grid=(steps,),
        in_specs=[row(D), row(512), row(512), pl.BlockSpec((D, D), lambda i: (0, 0))] + [ANY] * ns,
        out_specs=[row(512), row(512), pl.BlockSpec((2, 4, 128, D), lambda i: (0, 0, 0, 0))] + [ANY] * ns,
        out_shape=[jax.ShapeDtypeStruct((rows, 512), F32), jax.ShapeDtypeStruct((rows, 512), F32),
                   jax.ShapeDtypeStruct((2, 4, 128, D), F32)] + _sibling_shapes(partials),
        scratch_shapes=_sibling_sems(ns),
        compiler_params=_cp(("arbitrary",), 48),
    )(dh1, og, osw, wout, *partials)
    return outs[0], outs[1], outs[2], outs[3:]


def _swa_bwd(qr, kr, vr, osw, dos, sinks, jobs):
    rows = qr.shape[0]
    nblk = rows // SWA_BLOCK
    group = _swa_group(nblk)
    steps = nblk // group
    ns = jobs.n

    def body(q_ref, k0, kp, kc, v0, vp, vc, o_ref, do_ref, sink_ref, *rest):
        dq_ref, dk_ref, dv_ref, dsink_ref = rest[ns:ns + 4]
        start, finish = jobs.bind(rest[:ns], rest[ns + 4:2 * ns + 4], rest[2 * ns + 4:])
        step = pl.program_id(0)

        @pl.when(step == 0)
        def _():
            dk_ref[...] = jnp.zeros_like(dk_ref)
            dv_ref[...] = jnp.zeros_like(dv_ref)
            dsink_ref[...] = jnp.zeros_like(dsink_ref)
            start()

        pl.when(step == steps - 1)(finish)
        for g in range(group):
            block(step * group + g, g, q_ref, k0, kp, kc, v0, vp, vc, o_ref, do_ref, sink_ref,
                  dq_ref, dk_ref, dv_ref, dsink_ref)

    def block(n, g, q_ref, k0, kp, kc, v0, vp, vc, o_ref, do_ref, sink_ref, dq_ref, dk_ref, dv_ref, dsink_ref):
        rs = slice(g * SWA_BLOCK, (g + 1) * SWA_BLOCK)
        q = q_ref[rs, :]
        kall, vall = _swa_keys(k0, kp, kc, g), _swa_keys(v0, vp, vc, g)
        mask = _swa_mask(n)
        do_all = do_ref[rs, :]
        o_all = o_ref[rs, :].astype(F32)
        dq, dk, dv = [], [], []
        for kvh in range(SWA_KV):
            kv = slice(kvh * SWA_HD, (kvh + 1) * SWA_HD)
            q4, do4, o4 = _stack_heads(q, kvh), _stack_heads(do_all, kvh), _stack_heads(o_all, kvh)
            sink4 = _stack_sinks(sink_ref, kvh)
            s = jnp.where(mask, _mm_nt(q4, kall[:, kv]), NEG)
            m = jnp.maximum(jnp.max(s, axis=-1, keepdims=True), sink4)
            e = jnp.exp(s - m)
            inv = 1.0 / (jnp.sum(e, axis=-1, keepdims=True) + jnp.exp(sink4 - m))
            p = e * inv
            delta = jnp.sum(do4 * o4, axis=-1, keepdims=True)
            ds = p * (_mm_nt(do4, vall[:, kv]) - delta)
            dq4 = _mm(ds, kall[:, kv])
            dq += [dq4[g * SWA_BLOCK:(g + 1) * SWA_BLOCK] for g in range(SWA_GROUP)]
            dk.append(_mm_tn(ds, q4))
            dv.append(_mm_tn(p, do4))
            sink_term = jnp.exp(sink4 - m) * inv * delta
            for g in range(SWA_GROUP):
                head = kvh * SWA_GROUP + g
                dsink = -jnp.sum(sink_term[g * SWA_BLOCK:(g + 1) * SWA_BLOCK], axis=0, keepdims=True)
                dsink_ref[head:head + 1, :] += jnp.broadcast_to(dsink, (1, 128))
        dq_ref[rs, :] = jnp.concatenate(dq, axis=1)
        dk_all = jnp.concatenate(dk, axis=1)
        dv_all = jnp.concatenate(dv, axis=1)
        prev0 = pl.multiple_of(jnp.maximum(n - 1, 0) * SWA_BLOCK, SWA_BLOCK)
        cur0 = pl.multiple_of(n * SWA_BLOCK, SWA_BLOCK)
        for ref, val in ((dk_ref, dk_all), (dv_ref, dv_all)):
            ref[0:SWA_BLOCK, :] += val[0:SWA_BLOCK]
            ref[pl.ds(prev0, SWA_BLOCK), :] += val[SWA_BLOCK:2 * SWA_BLOCK]
            ref[pl.ds(cur0, SWA_BLOCK), :] += val[2 * SWA_BLOCK:]

    blk, first, prev = _swa_specs(group)
    whole = pl.BlockSpec((rows, 128), lambda n: (0, 0))
    outs = pl.pallas_call(
        body, name="swa_bwd", grid=(steps,),
        in_specs=[blk(512), first, prev, blk(128), first, prev, blk(128), blk(512), blk(512),
                  pl.BlockSpec(memory_space=pltpu.SMEM)] + [ANY] * ns,
        out_specs=[blk(512), whole, whole, pl.BlockSpec((8, 128), lambda n: (0, 0))] + [ANY] * ns,
        out_shape=[jax.ShapeDtypeStruct((rows, 512), F32), jax.ShapeDtypeStruct((rows, 128), F32),
                   jax.ShapeDtypeStruct((rows, 128), F32), jax.ShapeDtypeStruct((8, 128), F32)] + jobs.out_shapes,
        scratch_shapes=jobs.sems,
        compiler_params=_cp(("arbitrary",), 48),
    )(qr, kr, kr, kr, vr, vr, vr, osw, dos, sinks, *jobs.inputs)
    return outs[0], outs[1], outs[2], outs[3], jobs.split(outs[4:])


def _gla_bwd(proj, oraw, states, dog, wg_p, bg, gnw, jobs):
    rows = proj.shape[0]
    nc = rows // GLA_CHUNK
    group = _gla_group(nc)
    steps, nrows = nc // group, group * GLA_CHUNK
    ns = jobs.n

    def body(q_ref, k_ref, v_ref, r_ref, lr_ref, oraw_ref, st_ref, dog_ref, wg_ref, bg_ref, gnw_ref, *rest):
        dq_ref, dk_ref, dv_ref, dr_ref, dlr_ref, dwg_ref, dbg_ref, dgnw_ref = rest[ns:ns + 8]
        dstate, db_scr = rest[2 * ns + 8:2 * ns + 10]
        start, finish = jobs.bind(rest[:ns], rest[ns + 8:2 * ns + 8], rest[2 * ns + 10:])
        t = pl.program_id(0)
        c = steps - 1 - t

        @pl.when(t == 0)
        def _():
            dstate[...] = jnp.zeros_like(dstate)
            dwg_ref[...] = jnp.zeros_like(dwg_ref)
            dbg_ref[...] = jnp.zeros_like(dbg_ref)
            dgnw_ref[...] = jnp.zeros_like(dgnw_ref)
            start()

        pl.when(t == steps - 1)(finish)

        lr, wg = lr_ref[...], wg_ref[...]
        zg, live, _, upper, b = _gla_gates(lr, wg, bg_ref[...], c * nrows, nrows)
        eb, enb = jnp.exp(b), jnp.exp(-b)
        scale = GLA_DK ** -0.5
        gq = q_ref[...] * scale * eb
        gk = k_ref[...] * enb
        v = v_ref[...]
        gnw_v = gnw_ref[...]
        tril = _tril64()
        is_last = lax.broadcasted_iota(jnp.int32, (GLA_CHUNK, 1), 0) == GLA_CHUNK - 1
        dgnw = jnp.zeros((1, GLA_DV), F32)
        for h in range(GLA_HEADS):
            s64 = slice(h * GLA_DK, (h + 1) * GLA_DK)
            s128 = slice(h * GLA_DV, (h + 1) * GLA_DV)
            dsp = dstate[h]
            for gi in reversed(range(group)):
                rs = slice(gi * GLA_CHUNK, (gi + 1) * GLA_CHUNK)
                qh, kh, vh = gq[rs, s64], gk[rs, s64], v[rs, s128]
                ebh, enbh = eb[rs, s64], enb[rs, s64]
                eblh = eb[(gi + 1) * GLA_CHUNK - 1:(gi + 1) * GLA_CHUNK, s64]
                klh = kh * eblh
                st = st_ref[gi, h]
                o, rh, dout = oraw_ref[rs, s128], r_ref[rs, s128], dog_ref[rs, s128]
                rstd = lax.rsqrt(jnp.mean(o * o, axis=-1, keepdims=True) + EPS)
                on = o * rstd
                sg = _sigmoid(rh)
                dr_ref[rs, s128] = (dout * (on * gnw_v) * (sg * (1.0 + rh * (1.0 - sg)))).astype(ACT_DTYPE)
                dy = dout * (rh * sg)
                dgnw = dgnw + jnp.sum(dy * on, axis=0, keepdims=True)
                don = dy * gnw_v
                do = rstd * (don - on * jnp.mean(don * on, axis=-1, keepdims=True))
                a = jnp.where(tril, _mm_nt(qh, kh), 0.0)
                da = jnp.where(tril, _mm_nt(do, vh), 0.0)
                dkl = _mm(vh, dsp)
                dv_ref[rs, s128] = (_mm_tn(a, do) + _mm_nt(klh, dsp)).astype(ACT_DTYPE)
                debl = jnp.sum(dsp * st, axis=0, keepdims=True)
                dgq = _mm(da, kh) + _mm(do, st)
                dgk = _mm_tn(da, qh)
                dsp = dsp * eblh + _mm_tn(do, qh)
                dq_ref[rs, s64] = (dgq * (scale * ebh)).astype(ACT_DTYPE)
                dk_ref[rs, s64] = ((dgk + dkl * eblh) * enbh).astype(ACT_DTYPE)
                last = debl * eblh + jnp.sum(dkl * klh, axis=0, keepdims=True)
                db_scr[rs, s64] = dgq * qh - dgk * kh - dkl * klh + jnp.where(is_last, last, 0.0)
            dstate[h] = dsp
        dg = jnp.dot(upper.astype(F32), db_scr[...], precision=HIGHEST, preferred_element_type=F32)
        dzg = jnp.where(live, dg * _sigmoid(-zg) * (1.0 / GLA_TAU), 0.0)
        dlr_ref[...] = _mm_nt(dzg, wg).astype(ACT_DTYPE)
        dwg_ref[...] += _mm_tn(lr, dzg)
        dbg_ref[...] += jnp.broadcast_to(jnp.sum(dzg, axis=0, keepdims=True), dbg_ref.shape)
        dgnw_ref[...] += jnp.broadcast_to(dgnw, dgnw_ref.shape)

    nb = lambda w, col: pl.BlockSpec((nrows, w), lambda t: (steps - 1 - t, col // w))
    const = lambda shape: pl.BlockSpec(shape, lambda t: (0,) * len(shape))
    outs = pl.pallas_call(
        body, name="gla_bwd", grid=(steps,),
        in_specs=[nb(256, C_GQ), nb(256, C_GK), nb(512, C_GV), nb(512, C_GR), nb(128, C_LR), nb(512, 0),
                  pl.BlockSpec((group, GLA_HEADS, GLA_DV, GLA_DK), lambda t: (steps - 1 - t, 0, 0, 0)), nb(512, 0),
                  const((128, 256)), const((1, 256)), const((1, 128))] + [ANY] * ns,
        out_specs=[nb(256, 0), nb(256, 0), nb(512, 0), nb(512, 0), nb(128, 0),
                   const((128, 256)), const((8, 256)), const((8, 128))] + [ANY] * ns,
        out_shape=[jax.ShapeDtypeStruct((rows, 256), ACT_DTYPE), jax.ShapeDtypeStruct((rows, 256), ACT_DTYPE),
                   jax.ShapeDtypeStruct((rows, 512), ACT_DTYPE), jax.ShapeDtypeStruct((rows, 512), ACT_DTYPE),
                   jax.ShapeDtypeStruct((rows, 128), ACT_DTYPE), jax.ShapeDtypeStruct((128, 256), F32),
                   jax.ShapeDtypeStruct((8, 256), F32), jax.ShapeDtypeStruct((8, 128), F32)] + jobs.out_shapes,
        scratch_shapes=[pltpu.VMEM((GLA_HEADS, GLA_DV, GLA_DK), F32), pltpu.VMEM((nrows, 256), F32)] + jobs.sems,
        compiler_params=_cp(("arbitrary",)),
    )(proj, proj, proj, proj, proj, oraw, states, dog, wg_p, bg, gnw, *jobs.inputs)
    return outs[:8], jobs.split(outs[8:])


def _in_proj_bwd(x, lead, dh1, nw, win_p, dgv, dgr, dsq, dgq, dgk, dsk, dsv, dlr, tabs, tm):
    seq = x.shape[0]
    rows = LEAD + seq
    nb = tm // LEAD
    steps = rows // tm

    def first_copy(scr, gx_ref, sem):
        return pltpu.make_async_copy(scr.at[pl.ds(LEAD, tm - LEAD)], gx_ref.at[pl.ds(0, tm - LEAD)], sem)

    def tile_copy(scr, gx_ref, sem, step):
        start = pl.multiple_of(jnp.maximum(step * tm - LEAD, 0), LEAD)
        return pltpu.make_async_copy(scr, gx_ref.at[pl.ds(start, tm)], sem)

    def body(*refs):
        x_refs, refs = refs[:nb], refs[nb:]
        (lead_ref, dh1_ref, nw_ref, w_ref, dgv_ref, dgr_ref, dsq_ref, dgq_ref, dgk_ref, dsk_ref, dsv_ref, dlr_ref,
         c_ref, sa_ref, sb_ref, gx_ref, dlead_ref, dproj_ref, ut_ref, gnm_ref, scr, sem) = refs
        i = pl.program_id(0)

        @pl.when(i == 0)
        def _():
            gnm_ref[...] = jnp.zeros_like(gnm_ref)

        cos, sa, sb = c_ref[...], sa_ref[...], sb_ref[...]
        dsq_v = (_unrope(dsq_ref[...], cos, sa, sb) * (SWA_HD ** -0.5)).astype(MXU_DTYPE)
        dsk_v = _unrope(dsk_ref[...], cos, sa, sb).astype(MXU_DTYPE)
        dproj = jnp.concatenate(
            [dgv_ref[...].astype(MXU_DTYPE), dgr_ref[...].astype(MXU_DTYPE), dsq_v, dgq_ref[...].astype(MXU_DTYPE),
             dgk_ref[...].astype(MXU_DTYPE), dsk_v, dsv_ref[...].astype(MXU_DTYPE), dlr_ref[...].astype(MXU_DTYPE)],
            axis=1)
        dproj_ref[...] = dproj
        h = _h_tile(i, lead_ref, x_refs)
        rstd = lax.rsqrt(jnp.mean(h * h, axis=-1, keepdims=True) + EPS)
        hn = h * rstd
        nw_v = nw_ref[...]
        ut_ref[...] = (hn * nw_v).T.astype(ACT_DTYPE)
        du = _mm_nt(dproj, w_ref[...])
        gnm_ref[...] += jnp.broadcast_to(jnp.sum(du * hn, axis=0, keepdims=True), gnm_ref.shape)
        dun = du * nw_v
        dh0 = dh1_ref[...] + rstd * (dun - hn * jnp.mean(dun * hn, axis=-1, keepdims=True))

        if tm > LEAD:
            pl.when(i == 1)(lambda: first_copy(scr, gx_ref, sem).wait())
        pl.when(i > 1)(lambda: tile_copy(scr, gx_ref, sem, i).wait())
        scr[...] = dh0

        @pl.when(i == 0)
        def _():
            dlead_ref[...] = dh0[0:LEAD]
            if tm > LEAD:
                first_copy(scr, gx_ref, sem).start()
                if steps == 1:
                    first_copy(scr, gx_ref, sem).wait()

        @pl.when(i > 0)
        def _():
            tile_copy(scr, gx_ref, sem, i).start()

        if steps > 1:
            pl.when(i == steps - 1)(lambda: tile_copy(scr, gx_ref, sem, i).wait())

    row = lambda w: pl.BlockSpec((tm, w), lambda i: (i, 0))
    const = lambda shape: pl.BlockSpec(shape, lambda i: (0,) * len(shape))
    return pl.pallas_call(
        body, name="in_proj_bwd", grid=(steps,),
        in_specs=_token_specs(tm) + [const((LEAD, D)), row(D), const((1, D)), const((D, DINP)),
                                     row(512), row(512), row(512), row(256), row(256), row(128), row(128), row(128),
                                     row(128), row(128), row(128)],
        out_specs=[ANY, const((LEAD, D)), row(DINP), pl.BlockSpec((D, tm), lambda i: (0, i)), const((8, D))],
        out_shape=[jax.ShapeDtypeStruct((seq, D), F32), jax.ShapeDtypeStruct((LEAD, D), F32),
                   jax.ShapeDtypeStruct((rows, DINP), ACT_DTYPE), jax.ShapeDtypeStruct((D, rows), ACT_DTYPE),
                   jax.ShapeDtypeStruct((8, D), F32)],
        scratch_shapes=[pltpu.VMEM((tm, D), F32), pltpu.SemaphoreType.DMA],
        compiler_params=_cp(("arbitrary",), 56),
    )(*([x] * nb), lead, dh1, nw, win_p, dgv, dgr, dsq, dgq, dgk, dsk, dsv, dlr, *tabs)


def _in_proj_bwd_weights(ut, dproj, tm):
    rows = dproj.shape[0]

    def body(ut_ref, dp_ref, dw_ref):
        @pl.when(pl.program_id(0) == 0)
        def _():
            dw_ref[...] = jnp.zeros_like(dw_ref)

        dw_ref[...] += _mm(ut_ref[...], dp_ref[...])

    return pl.pallas_call(
        body, name="in_proj_bwd_weights", grid=(rows // tm,),
        in_specs=[pl.BlockSpec((D, tm), lambda i: (0, i)), pl.BlockSpec((tm, DINP), lambda i: (i, 0))],
        out_specs=pl.BlockSpec((D, DINP), lambda i: (0, 0)),
        out_shape=jax.ShapeDtypeStruct((D, DINP), F32),
        compiler_params=_cp(("arbitrary",), 56),
    )(ut, dproj)


def _adamw(w, g, m, v):
    m = ADAM_B1 * m + (1.0 - ADAM_B1) * g
    v = ADAM_B2 * v + (1.0 - ADAM_B2) * jnp.square(g)
    m_hat = m / (1.0 - ADAM_B1 ** ADAM_STEP)
    v_hat = v / (1.0 - ADAM_B2 ** ADAM_STEP)
    delta = -ADAM_LR * (m_hat / (jnp.sqrt(v_hat) + ADAM_EPS) + ADAM_WD * w)
    return delta, m, v


ADAM_STEPS = 8


def _adamw_shards(where, items, name, jobs=None):
    jobs = jobs or _Jobs([])
    ns, nw = jobs.n, len(items)

    def body(where_ref, *rest):
        ins, rest = rest[:5 * nw], rest[5 * nw:]
        job_ins, rest = rest[:ns], rest[ns:]
        outs, rest = rest[:4 * nw], rest[4 * nw:]
        start, finish = jobs.bind(job_ins, rest[:ns], rest[ns:])
        i = pl.program_id(0)
        pl.when(i == 0)(start)
        pl.when(i == ADAM_STEPS - 1)(finish)
        for k in range(nw):
            p_ref, own_ref, w_ref, m_ref, v_ref = ins[5 * k:5 * k + 5]
            g_ref, d_ref, nm_ref, nv_ref = outs[4 * k:4 * k + 4]
            g = ((p_ref[0].astype(F32) + p_ref[1].astype(F32)) + p_ref[2].astype(F32)) + own_ref[...]
            g_ref[...] = g
            d_ref[...], nm_ref[...], nv_ref[...] = _adamw(w_ref[...], g, m_ref[...], v_ref[...])

    in_specs, out_specs, out_shape, operands = [], [], [], []
    for parts, own, w, m, v in items:
        r, cdim = w.shape
        tr = r // ADAM_STEPS
        spec = pl.BlockSpec((tr, cdim), lambda i, s: (i, 0))
        in_specs += [pl.BlockSpec((3, tr, cdim), lambda i, s: (0, i, 0)),
                     pl.BlockSpec((None, tr, cdim), lambda i, s: (s[1], i, 0)), spec, spec, spec]
        out_specs += [spec] * 4
        out_shape += [jax.ShapeDtypeStruct((r, cdim), F32)] * 4
        operands += [parts, own, w, m, v]
    outs = pl.pallas_call(
        body, name=name,
        grid_spec=pltpu.PrefetchScalarGridSpec(
            num_scalar_prefetch=1, grid=(ADAM_STEPS,),
            in_specs=in_specs + [ANY] * ns, out_specs=out_specs + [ANY] * ns, scratch_shapes=jobs.sems),
        out_shape=out_shape + jobs.out_shapes,
        compiler_params=_cp(("arbitrary",)),
    )(where, *operands, *jobs.inputs)
    return [outs[4 * k:4 * k + 4] for k in range(nw)], jobs.split(outs[4 * nw:])


def _adamw_small(w, g, m, v):
    def body(w_ref, g_ref, m_ref, v_ref, d_ref, nm_ref, nv_ref):
        d_ref[...], nm_ref[...], nv_ref[...] = _adamw(w_ref[...], g_ref[...], m_ref[...], v_ref[...])

    vm = pl.BlockSpec(memory_space=pltpu.VMEM)
    shape = jax.ShapeDtypeStruct(w.shape, F32)
    return pl.pallas_call(body, name="adamw_small", in_specs=[vm] * 4, out_specs=[vm] * 3,
                          out_shape=[shape] * 3)(w, g, m, v)


def _add_own_half(where, full, theirs, name, wire_copy=False):
    _, _, r, cdim = full.shape
    tr = 128 if r % 128 == 0 else r

    def body(where_ref, a_ref, b_ref, *o_refs):
        total = a_ref[...] + b_ref[...]
        o_refs[0][...] = total
        if wire_copy:
            o_refs[1][...] = total.astype(WIRE_DTYPE)

    spec = pl.BlockSpec((4, tr, cdim), lambda i, s: (0, i, 0))
    shapes = [jax.ShapeDtypeStruct(theirs.shape, F32)] + ([jax.ShapeDtypeStruct(theirs.shape, WIRE_DTYPE)] if wire_copy else [])
    outs = pl.pallas_call(
        body, name=name,
        grid_spec=pltpu.PrefetchScalarGridSpec(
            num_scalar_prefetch=1, grid=(r // tr,),
            in_specs=[pl.BlockSpec((None, 4, tr, cdim), lambda i, s: (s[0], 0, i, 0)), spec],
            out_specs=[spec] * len(shapes)),
        out_shape=shapes, compiler_params=_cp(("arbitrary",)))(where, full, theirs)
    return outs if wire_copy else outs[0]


def _to_rows128(a):
    return a.reshape(-1, 128)


def _pad_rows128(vec):
    flat = vec.reshape(-1)
    n = -(-flat.shape[0] // 128)
    return jnp.pad(flat, (0, n * 128 - flat.shape[0])).reshape(n, 128)


def kernel(x, meta_tokens, norm_mix_w, w_in, w_gate_up, b_gate, gla_norm_w, sinks, w_out, norm_ff_w, w_ff1, w_ff2, final_norm_w, loss_target, m_meta_tokens, m_norm_mix_w, m_w_in, m_w_gate_up, m_b_gate, m_gla_norm_w, m_sinks, m_w_out, m_norm_ff_w, m_w_ff1, m_w_ff2, m_final_norm_w, v_meta_tokens, v_norm_mix_w, v_w_in, v_w_gate_up, v_b_gate, v_gla_norm_w, v_sinks, v_w_out, v_norm_ff_w, v_w_ff1, v_w_ff2, v_final_norm_w):
    seq = x.shape[1]
    rows = LEAD + seq
    tm = _row_tile(rows)
    tm_wide = 1664 if rows % 1664 == 0 else tm
    dev =4 * lax.axis_index("x") + 2 * lax.axis_index("y") + lax.axis_index("c")

    small_shard = jnp.concatenate([meta_tokens, w_gate_up[0], jnp.zeros((N_META, 96), F32)], axis=1)
    g_in, g_small = _all_gather([w_in[0].astype(WIRE_DTYPE), small_shard])
    later_shards = [w_out[0].astype(WIRE_DTYPE), w_ff1[0].astype(WIRE_DTYPE), w_ff2[0].astype(WIRE_DTYPE)]
    win_full = jnp.transpose(g_in, (1, 0, 2)).reshape(D, DIN)
    cols = lambda r: win_full[:, r[0]:r[1]]
    win_p = jnp.concatenate([cols(O_GV), cols(O_GR), cols(O_SQ), cols(O_GQ), cols(O_GK), cols(O_SK), cols(O_SV),
                             cols(O_LR), jnp.zeros((D, 128 - GLA_RANK), WIRE_DTYPE)], axis=1)
    meta_full = jnp.transpose(g_small[:, :, 0:128], (1, 0, 2)).reshape(N_META, D)
    wg_full = jnp.transpose(g_small[:, :, 128:160], (1, 0, 2)).reshape(GLA_RANK, GLA_HEADS * GLA_DK)
    wg_p = jnp.concatenate([wg_full, jnp.zeros((128 - GLA_RANK, 256), F32)], axis=0)

    lead = jnp.concatenate([jnp.zeros((META0, D), F32), meta_full], axis=0)
    tabs = _rope_tables(rows)
    proj = _in_proj(x[0], lead, norm_mix_w, win_p, tm)
    oraw, og, states, (g_out, g_w1, g_w2) = _gla_fwd(proj, wg_p, b_gate, gla_norm_w, later_shards)
    wout_full = g_out.reshape(D, D)
    w2_full = g_w2.reshape(D_FF, D)
    w1_full = jnp.transpose(g_w1, (1, 0, 2)).reshape(D, D_FF)
    qr, kr, vr = _swa_prep(proj, tabs, tm)
    osw = _swa_fwd(qr, kr, vr, sinks)
    h1, f, ft = _out_proj(x[0], lead, og, osw, wout_full, norm_ff_w, tm)
    a, dh2, dh2t, loss_p, gfn_p = _ffn_fwd(f, h1, w1_full, w2_full, loss_target[0], final_norm_w.reshape(1, D), tm)

    da, dh1, gnf_p = _ffn_bwd_act(dh2, a, w1_full, w2_full, h1, norm_ff_w, tm)
    dw1, dw2 = _ffn_bwd_weights(ft, a, da, dh2t, tm_wide)
    where = jnp.stack([lax.axis_index("c"), 2 * lax.axis_index("x") + lax.axis_index("y")]).astype(jnp.int32)
    dog, dos, dwout, theirs_ffn = _out_proj_bwd(dh1, og, osw, wout_full, tm, [dw1, dw2])
    sums_ffn = [_add_own_half(where, p, q, "reduce_pair_%d" % (2 + k))
                for k, (p, q) in enumerate(zip([dw1, dw2], theirs_ffn))]
    dsq, dsk, dsv, dsink_p, (parts_ffn, (theirs_wout,)) = _swa_bwd(
        qr, kr, vr, osw, dos, sinks, _Jobs([("chips", sums_ffn), ("sibling", [dwout])]))
    sum_wout = _add_own_half(where, dwout, theirs_wout, "reduce_pair_1")
    (dgq, dgk, dgv, dgr, dlr, dwg_p, dbg_p, dgnw_p), ((parts_wout,),) = _gla_bwd(
        proj, oraw, states, dog, wg_p, b_gate, gla_norm_w, _Jobs([("chips", [sum_wout])]))
    grad_x, dlead, dproj, ut, gnm_p = _in_proj_bwd(x[0], lead, dh1, norm_mix_w, win_p, dgv, dgr, dsq, dgq, dgk, dsk,
                                                   dsv, dlr, tabs, tm)
    grad_x = grad_x[None]
    dwin_p = _in_proj_bwd_weights(ut, dproj, tm_wide)

    pcols = lambda c0, r: dwin_p[:, c0:c0 + (r[1] - r[0])]
    dwin = jnp.concatenate([pcols(C_GQ, O_GQ), pcols(C_GK, O_GK), pcols(C_GV, O_GV), pcols(C_GR, O_GR),
                            pcols(C_LR, O_LR), pcols(C_SQ, O_SQ), pcols(C_SK, O_SK), pcols(C_SV, O_SV)], axis=1)
    dwin = jnp.transpose(dwin.reshape(D, 4, 2, DIN // N_DEV), (2, 1, 0, 3))
    (theirs_win,) = _rs_sibling([dwin])
    sum_win, sum_win_wire = _add_own_half(where, dwin, theirs_win, "reduce_pair_0", wire_copy=True)

    small = [dlead[META0:LEAD],dwg_p[0:GLA_RANK], gnm_p[0:1], dbg_p[0:1], dgnw_p[0:1], dsink_p[:, 0], gnf_p[0:1],
             gfn_p[0:1], loss_p[0:1, 0:1]]
    sizes = [-(-s.size // 128) for s in small]
    pack = jnp.concatenate([_pad_rows128(s) for s in small], axis=0)
    pad_rows = -pack.shape[0] % 8
    pack = jnp.pad(pack, ((0, pad_rows), (0, 0)))
    total = _all_reduce_small(pack)
    offs = [sum(sizes[:k]) for k in range(len(sizes))]
    take = lambda k, shape: total[offs[k]:offs[k] + sizes[k]].reshape(-1)[:small[k].size].reshape(shape)
    g_meta_full = take(0, (N_META, D))
    g_wg_full = take(1, (GLA_RANK, 256))
    g_meta = lax.dynamic_slice_in_dim(g_meta_full, dev * 128, 128, axis=1)
    g_wg = lax.dynamic_slice_in_dim(g_wg_full, dev * 32, 32, axis=1)[None]
    g_norm_mix, g_b_gate, g_gla_norm = take(2, (1, D)), take(3, (1, 256)), take(4, (1, 128))
    g_sinks, g_norm_ff, g_final_norm = take(5, (1, 8)), take(6, (1, D)), take(7, (D,))
    loss = take(8, ())

    ((g_wout, d_wout, nm_wout, nv_wout), (g_w1s, d_w1, nm_w1, nv_w1), (g_w2s, d_w2, nm_w2, nv_w2)), ((parts_win,),) = \
        _adamw_shards(where, [(parts_wout, sum_wout, w_out[0], m_w_out[0], v_w_out[0]),
                              (parts_ffn[0], sums_ffn[0], w_ff1[0], m_w_ff1[0], v_w_ff1[0]),
                              (parts_ffn[1], sums_ffn[1], w_ff2[0], m_w_ff2[0], v_w_ff2[0])],
                      "adamw_w_out_ff", _Jobs([("chips", [sum_win_wire])]))
    ((g_win, d_win, nm_win, nv_win),), _ = _adamw_shards(
        where, [(parts_win, sum_win, w_in[0], m_w_in[0], v_w_in[0])], "adamw_w_in")

    names = ["meta", "wg", "norm_mix", "b_gate", "gla_norm", "sinks", "norm_ff", "final_norm"]
    ws = [meta_tokens, w_gate_up, norm_mix_w, b_gate, gla_norm_w, sinks, norm_ff_w, final_norm_w]
    gs = [g_meta, g_wg, g_norm_mix, g_b_gate, g_gla_norm, g_sinks, g_norm_ff, g_final_norm]
    ms = [m_meta_tokens, m_w_gate_up, m_norm_mix_w, m_b_gate, m_gla_norm_w, m_sinks, m_norm_ff_w, m_final_norm_w]
    vs = [v_meta_tokens, v_w_gate_up, v_norm_mix_w, v_b_gate, v_gla_norm_w, v_sinks, v_norm_ff_w, v_final_norm_w]
    ssz = [-(-w.size // 128) for w in ws]
    packed = []
    for group in (ws, gs, ms, vs):
        p = jnp.concatenate([_pad_rows128(t) for t in group], axis=0)
        packed.append(jnp.pad(p, ((0, -p.shape[0] % 8), (0, 0))))
    d_s, nm_s, nv_s = _adamw_small(*packed)
    soffs = [sum(ssz[:k]) for k in range(len(ssz))]
    unpack = lambda t, k: t[soffs[k]:soffs[k] + ssz[k]].reshape(-1)[:ws[k].size].reshape(ws[k].shape)
    d_small = {n: unpack(d_s, k) for k, n in enumerate(names)}
    nm_small = {n: unpack(nm_s, k) for k, n in enumerate(names)}
    nv_small = {n: unpack(nv_s, k) for k, n in enumerate(names)}
    g_small_d = dict(zip(names, gs))

    def ordered(big, small_d):
        win_v, wout_v, w1_v, w2_v = big
        return (small_d["meta"], small_d["norm_mix"], win_v[None], small_d["wg"], small_d["b_gate"],
                small_d["gla_norm"], small_d["sinks"], wout_v[None], small_d["norm_ff"], w1_v[None], w2_v[None],
                small_d["final_norm"])

    return (loss, grad_x,
            *ordered((g_win, g_wout, g_w1s, g_w2s), g_small_d),
            *ordered((d_win, d_wout, d_w1, d_w2), d_small),
            *ordered((nm_win, nm_wout, nm_w1, nm_w2), nm_small),
            *ordered((nv_win, nv_wout, nv_w1, nv_w2), nv_small))
```

```python
import functools

import jax
import jax.numpy as jnp
from jax import lax
from jax.experimental import pallas as pl
from jax.experimental.pallas import tpu as pltpu

F32 = jnp.float32
MXU_DTYPE = jnp.bfloat16
ACT_DTYPE = jnp.bfloat16
WIRE_DTYPE = jnp.bfloat16

D = 1024
N_META = 16
LEAD = 128
META0 = LEAD - N_META
EPS = 1e-5
GLA_HEADS, GLA_DK, GLA_DV, GLA_RANK, GLA_CHUNK = 4, 64, 128, 16, 64
GLA_TAU = 16.0
SWA_HEADS, SWA_KV, SWA_GROUP, SWA_HD, SWA_BLOCK = 8, 2, 4, 64, 128
ROPE_DIM, ROPE_THETA = 16, 500000.0
D_FF = 4096
N_DEV = 8
FF_TILE = D_FF // N_DEV
FF_WIDE = 1024
NEG = -1e30

C_GV, C_GR, C_SQ, C_GQ, C_GK, C_SK, C_SV, C_LR = 0, 512, 1024, 1536, 1792, 2048, 2176, 2304
DINP = 2432
DIN = 2320
O_GQ, O_GK, O_GV, O_GR, O_LR, O_SQ, O_SK, O_SV = (0, 256), (256, 512), (512, 1024), (1024, 1536), (1536, 1552), (1552, 2064), (2064, 2192), (2192, 2320)

ADAM_LR, ADAM_B1, ADAM_B2, ADAM_EPS, ADAM_WD, ADAM_STEP = 0.001, 0.9, 0.999, 1e-08, 0.01, 10

MESH = pl.DeviceIdType.MESH
ANY = pl.BlockSpec(memory_space=pl.ANY)
HIGHEST = lax.Precision.HIGHEST


def _cp(sem=None, vmem_mb=None):
    kw = {}
    if sem is not None:
        kw["dimension_semantics"] = sem
    if vmem_mb is not None:
        kw["vmem_limit_bytes"] = vmem_mb << 20
    return pltpu.CompilerParams(**kw)


def _mm(a, b):
    return jnp.dot(a.astype(MXU_DTYPE), b.astype(MXU_DTYPE), preferred_element_type=F32)


def _mm_nt(a, b):
    return lax.dot_general(a.astype(MXU_DTYPE), b.astype(MXU_DTYPE), (((1,), (1,)), ((), ())),
                           preferred_element_type=F32)


def _mm_tn(a, b):
    return lax.dot_general(a.astype(MXU_DTYPE), b.astype(MXU_DTYPE), (((0,), (0,)), ((), ())),
                           preferred_element_type=F32)


def _logsigmoid(z):
    return jnp.minimum(z, 0.0) - jnp.log(1.0 + jnp.exp(-jnp.abs(z)))


def _sigmoid(z):
    return 1.0 / (1.0 + jnp.exp(-z))


def _row_tile(rows):
    return 640 if rows % 640 == 0 else 128


def _mesh_pos():
    return lax.axis_index("x"), lax.axis_index("y"), lax.axis_index("c")


def _all_gather(shards):
    n = len(shards)

    def body(*refs):
        start, forward, finish = _gather_schedule(refs[:n], refs[n:2 * n], *refs[2 * n:])
        start()
        for j in range(3):
            forward(j)
        finish()

    gathered = pl.pallas_call(
        body, name="all_gather_weights",
        out_shape=_gathered_shapes(shards), in_specs=[ANY] * n, out_specs=[ANY] * n,
        scratch_shapes=_gather_sems(n),
    )(*shards)
    return _with_own_block(gathered, shards)


def _gathered_shapes(shards):
    return [jax.ShapeDtypeStruct((N_DEV,) + s.shape, s.dtype) for s in shards]


def _gather_sems(n):
    return [pltpu.SemaphoreType.DMA((7 * n,)), pltpu.SemaphoreType.DMA((7 * n,))] if n else []


def _place_gather(step, steps, shard_refs, gathered_refs, sems):
    if not shard_refs:
        return
    start, forward, finish = _gather_schedule(shard_refs, gathered_refs, *sems)
    pl.when(step == 0)(start)
    for j, at in enumerate((steps * 7 // 10, steps * 8 // 10, steps * 9 // 10)):
        pl.when(step == at)(functools.partial(forward, j))
    pl.when(step == steps - 1)(finish)


def _with_own_block(gathered, shards):
    dev = 4 * lax.axis_index("x") + 2 * lax.axis_index("y") + lax.axis_index("c")
    return [lax.dynamic_update_index_in_dim(g, s, dev, 0) for g, s in zip(gathered, shards)]


def _gather_schedule(ins, outs, send_sems, recv_sems):
    n = len(ins)
    x, y, c = _mesh_pos()
    me, sibling = (x, y, c), (x, y, 1 - c)
    chips = [(1 - x, y), (x, 1 - y), (1 - x, 1 - y)]

    def copy(a, k, block, to, src=None):
        dst = outs[a].at[4 * block[0] + 2 * block[1] + block[2]]
        return pltpu.make_async_remote_copy(
            src_ref=dst if src is None else src, dst_ref=dst,
            send_sem=send_sems.at[a * 7 + k], recv_sem=recv_sems.at[a * 7 + k],
            device_id=to, device_id_type=MESH)

    def first(a):
        return [copy(a, 0, me, sibling, src=ins[a])] + [copy(a, 1 + j, me, (*chip, c), src=ins[a])
                                                        for j, chip in enumerate(chips)]

    def start():
        for a in range(n):
            for cp in first(a):
                cp.start()

    def forward(j):
        for a in range(n):
            copy(a, 1 + j, (*chips[j], c), me).wait_recv()
            copy(a, 4 + j, (*chips[j], c), sibling).start()

    def finish():
        for a in range(n):
            copy(a, 0, sibling, me).wait_recv()
            for j, chip in enumerate(chips):
                copy(a, 4 + j, (*chip, 1 - c), me).wait_recv()
        for a in range(n):
            for cp in first(a) + [copy(a, 4 + j, (*chip, c), sibling) for j, chip in enumerate(chips)]:
                cp.wait_send()

    return start, forward, finish


def _rs_sibling(gs):
    n = len(gs)

    def body(*refs):
        start, finish = _sibling_schedule(refs[:n], refs[n:2 * n], *refs[2 * n:])
        start()
        finish()

    return pl.pallas_call(
        body, name="reduce_scatter_sibling",
        out_shape=_sibling_shapes(gs), in_specs=[ANY] * n, out_specs=[ANY] * n,
        scratch_shapes=_sibling_sems(n),
    )(*gs)


def _sibling_shapes(gs):
    return [jax.ShapeDtypeStruct(g.shape[1:], g.dtype) for g in gs]


def _sibling_sems(n):
    return [pltpu.SemaphoreType.DMA((n,)), pltpu.SemaphoreType.DMA((n,))]


def _sibling_schedule(ins, land, send_sems, recv_sems):
    x, y, c = _mesh_pos()

    def copies():
        return [pltpu.make_async_remote_copy(
            src_ref=ins[a].at[1 - c], dst_ref=land[a], send_sem=send_sems.at[a], recv_sem=recv_sems.at[a],
            device_id=(x, y, 1 - c), device_id_type=MESH) for a in range(len(ins))]

    def start():
        for cp in copies():
            cp.start()

    def finish():
        for cp in copies():
            cp.wait_recv()
        for cp in copies():
            cp.wait_send()

    return start, finish


def _rs_chips(ps):
    n = len(ps)

    def body(*refs):
        start, finish = _chips_schedule(refs[:n], refs[n:2 * n], *refs[2 * n:])
        start()
        finish()

    return pl.pallas_call(
        body, name="reduce_scatter_chips",
        out_shape=_chips_shapes(ps), in_specs=[ANY] * n, out_specs=[ANY] * n,
        scratch_shapes=_chips_sems(n),
    )(*ps)


def _chips_shapes(ps):
    return [jax.ShapeDtypeStruct((3,) + p.shape[1:], p.dtype) for p in ps]


def _chips_sems(n):
    return [pltpu.SemaphoreType.DMA((3 * n,)), pltpu.SemaphoreType.DMA((3 * n,))]


def _chips_schedule(ins, land, send_sems, recv_sems):
    x, y, c = _mesh_pos()
    chips = [(1 - x, y), (x, 1 - y), (1 - x, 1 - y)]

    def copies():
        return [pltpu.make_async_remote_copy(
            src_ref=ins[a].at[2 * chip[0] + chip[1]], dst_ref=land[a].at[j],
            send_sem=send_sems.at[3 * a + j], recv_sem=recv_sems.at[3 * a + j],
            device_id=(*chip, c), device_id_type=MESH) for a in range(len(ins)) for j, chip in enumerate(chips)]

    def start():
        for cp in copies():
            cp.start()

    def finish():
        for cp in copies():
            cp.wait_recv()
        for cp in copies():
            cp.wait_send()

    return start, finish


class _Jobs:
    def __init__(self, jobs):
        self.jobs = jobs
        self.inputs = [a for _, arrs in jobs for a in arrs]
        self.out_shapes = [s for kind, arrs in jobs
                           for s in (_sibling_shapes(arrs) if kind == "sibling" else _chips_shapes(arrs))]
        self.sems = [s for kind, arrs in jobs
                     for s in (_sibling_sems(len(arrs)) if kind == "sibling" else _chips_sems(len(arrs)))]
        self.n = len(self.inputs)

    def bind(self, in_refs, out_refs, sem_refs):
        starts, finishes, at = [], [], 0
        for k, (kind, arrs) in enumerate(self.jobs):
            schedule = _sibling_schedule if kind == "sibling" else _chips_schedule
            start, finish = schedule(in_refs[at:at + len(arrs)], out_refs[at:at + len(arrs)],
                                     sem_refs[2 * k], sem_refs[2 * k + 1])
            starts.append(start)
            finishes.append(finish)
            at += len(arrs)

        def start_all():
            for f in starts:
                f()

        def finish_all():
            for f in finishes:
                f()

        return start_all, finish_all

    def split(self, outs):
        res, at = [], 0
        for _, arrs in self.jobs:
            res.append(list(outs[at:at + len(arrs)]))
            at += len(arrs)
        return res


R_META, R_WG, R_NORM_MIX, R_NORM_FF, R_FINAL, R_B_GATE, R_GLA_NORM, R_LOSS, R_SINKS, SMALL_ROWS = 0, 16, 32, 33, 34, 35, 36, 37, 40, 48


def _all_reduce_small(dlead, dwg, gnm, gnf, gfn, dbg, dgnw, loss, dsink):
    def body(dlead_ref, dwg_ref, gnm_ref, gnf_ref, gfn_ref, dbg_ref, dgnw_ref, loss_ref, dsink_ref,
             out_ref, p_ref, land, send_sems, recv_sems):
        x, y, c = _mesh_pos()
        me = 4 * x + 2 * y + c
        p_ref[...] = jnp.zeros_like(p_ref)
        p_ref[R_META:R_META + N_META, :] = dlead_ref[META0:LEAD, :]
        p_ref[R_WG:R_WG + GLA_RANK, 0:256] = dwg_ref[0:GLA_RANK, :]
        p_ref[R_NORM_MIX:R_NORM_MIX + 1, :] = gnm_ref[0:1, :]
        p_ref[R_NORM_FF:R_NORM_FF + 1, :] = gnf_ref[0:1, :]
        p_ref[R_FINAL:R_FINAL + 1, :] = gfn_ref[0:1, :]
        p_ref[R_B_GATE:R_B_GATE + 1, 0:256] = dbg_ref[0:1, :]
        p_ref[R_GLA_NORM:R_GLA_NORM + 1, 0:128] = dgnw_ref[0:1, :]
        p_ref[R_LOSS:R_LOSS + 1, 0:128] = loss_ref[0:1, :]
        p_ref[R_SINKS:R_SINKS + SWA_HEADS, 0:128] = dsink_ref[...]
        land[me] = p_ref[...]
        copies = []
        for k in range(1, N_DEV):
            bx, by, bc = (k >> 2) & 1, (k >> 1) & 1, k & 1
            peer = (1 - x if bx else x, 1 - y if by else y, 1 - c if bc else c)
            copies.append(pltpu.make_async_remote_copy(
                src_ref=p_ref, dst_ref=land.at[me], send_sem=send_sems.at[k - 1], recv_sem=recv_sems.at[k - 1],
                device_id=peer, device_id_type=MESH))
        for cp in copies:
            cp.start()
        for cp in copies:
            cp.wait_recv()
        for cp in copies:
            cp.wait_send()
        acc = land[0]
        for d in range(1, N_DEV):
            acc = acc + land[d]
        out_ref[...] = acc

    return pl.pallas_call(
        body, name="all_reduce_small",
        out_shape=jax.ShapeDtypeStruct((SMALL_ROWS, D), F32),
        in_specs=[pl.BlockSpec(memory_space=pltpu.VMEM)] * 9, out_specs=pl.BlockSpec(memory_space=pltpu.VMEM),
        scratch_shapes=[pltpu.VMEM((SMALL_ROWS, D), F32), pltpu.VMEM((N_DEV, SMALL_ROWS, D), F32),
                        pltpu.SemaphoreType.DMA((7,)), pltpu.SemaphoreType.DMA((7,))],
    )(dlead, dwg, gnm, gnf, gfn, dbg, dgnw, loss, dsink)


def _token_specs(tm, grid_rank=1):
    nb = tm // LEAD

    def spec(k):
        if grid_rank == 1:
            return pl.BlockSpec((LEAD, D), lambda i: (jnp.maximum(i * nb + k - 1, 0), 0))
        return pl.BlockSpec((LEAD, D), lambda i, j: (jnp.maximum(i * nb + k - 1, 0), 0))

    return [spec(k) for k in range(nb)]


def _h_tile(i, lead_ref, x_refs):
    first = jnp.where(i == 0, lead_ref[...], x_refs[0][...])
    return jnp.concatenate([first] + [r[...] for r in x_refs[1:]], axis=0)


def _in_proj(x, lead, nw, win_p, tm):
    rows = LEAD + x.shape[0]
    nb = tm // LEAD

    def body(*refs):
        x_refs, (lead_ref, nw_ref, w_ref, o_ref) = refs[:nb], refs[nb:]
        h = _h_tile(pl.program_id(0), lead_ref, x_refs)
        rstd = lax.rsqrt(jnp.mean(h * h, axis=-1, keepdims=True) + EPS)
        u = (h * rstd * nw_ref[...]).astype(MXU_DTYPE)
        o_ref[...] = jnp.dot(u, w_ref[...].astype(MXU_DTYPE), preferred_element_type=F32)

    return pl.pallas_call(
        body, name="in_proj", grid=(rows // tm,),
        in_specs=_token_specs(tm) + [pl.BlockSpec((LEAD, D), lambda i: (0, 0)), pl.BlockSpec((1, D), lambda i: (0, 0)),
                                     pl.BlockSpec((D, DINP), lambda i: (0, 0))],
        out_specs=pl.BlockSpec((tm, DINP), lambda i: (i, 0)),
        out_shape=jax.ShapeDtypeStruct((rows, DINP), F32),
        compiler_params=_cp(("arbitrary",), 56),
    )(*([x] * nb), lead, nw, win_p)


def _rope_tables(rows):
    pos = (jnp.arange(rows, dtype=jnp.int32) - META0).astype(F32)
    inv_freq = 1.0 / (ROPE_THETA ** (jnp.arange(0, ROPE_DIM, 2, dtype=F32) / ROPE_DIM))
    ang = pos[:, None] * jnp.tile(inv_freq, 128 // (ROPE_DIM // 2))[None, :]
    in_head = jnp.arange(128, dtype=jnp.int32)[None, :] % SWA_HD
    cos, sin = jnp.cos(ang), jnp.sin(ang)
    c_tab = jnp.where(in_head < ROPE_DIM, cos, 1.0)
    sa_tab = jnp.where(in_head < ROPE_DIM // 2, -sin, 0.0)
    sb_tab = jnp.where((in_head >= ROPE_DIM // 2) & (in_head < ROPE_DIM), sin, 0.0)
    return c_tab, sa_tab, sb_tab


def _rope(xv, cos, sa, sb):
    width = xv.shape[1]
    reps = width // 128
    if reps > 1:
        cos, sa, sb = (jnp.tile(t, (1, reps)) for t in (cos, sa, sb))
    return xv * cos + pltpu.roll(xv, width - 8, 1) * sa + pltpu.roll(xv, 8, 1) * sb


def _unrope(dy, cos, sa, sb):
    width = dy.shape[1]
    reps = width // 128
    if reps > 1:
        cos, sa, sb = (jnp.tile(t, (1, reps)) for t in (cos, sa, sb))
    return dy * cos + pltpu.roll(dy * sa, 8, 1) + pltpu.roll(dy * sb, width - 8, 1)


def _swa_prep(proj, tabs, tm):
    rows = proj.shape[0]

    def body(q_ref, k_ref, v_ref, c_ref, sa_ref, sb_ref, qo_ref, ko_ref, vo_ref):
        cos, sa, sb = c_ref[...], sa_ref[...], sb_ref[...]
        qo_ref[...] = (_rope(q_ref[...], cos, sa, sb) * (SWA_HD ** -0.5)).astype(ACT_DTYPE)
        ko_ref[...] = _rope(k_ref[...], cos, sa, sb).astype(ACT_DTYPE)
        vo_ref[...] = v_ref[...].astype(ACT_DTYPE)

    tab_spec = pl.BlockSpec((tm, 128), lambda i: (i, 0))
    return pl.pallas_call(
        body, name="swa_prep", grid=(rows // tm,),
        in_specs=[pl.BlockSpec((tm, 512), lambda i: (i, C_SQ // 512)),
                  pl.BlockSpec((tm, 128), lambda i: (i, C_SK // 128)),
                  pl.BlockSpec((tm, 128), lambda i: (i, C_SV // 128)), tab_spec, tab_spec, tab_spec],
        out_specs=[pl.BlockSpec((tm, 512), lambda i: (i, 0)), tab_spec, tab_spec],
        out_shape=[jax.ShapeDtypeStruct((rows, 512), ACT_DTYPE), jax.ShapeDtypeStruct((rows, 128), ACT_DTYPE),
                   jax.ShapeDtypeStruct((rows, 128), ACT_DTYPE)],
        compiler_params=_cp(("arbitrary",)),
    )(proj, proj, proj, *tabs)


def _gla_group(nc):
    for g in (5, 2):
        if nc % g == 0:
            return g
    return 1


def _gla_gates(lr, wg, bg, first_row, nrows):
    zg = _mm(lr, wg) + bg
    row = first_row + lax.broadcasted_iota(jnp.int32, (nrows, 1), 0)
    live = row >= META0
    g = jnp.where(live, _logsigmoid(zg) * (1.0 / GLA_TAU), 0.0)
    ii = lax.broadcasted_iota(jnp.int32, (nrows, nrows), 0)
    jj = lax.broadcasted_iota(jnp.int32, (nrows, nrows), 1)
    same = (ii // GLA_CHUNK) == (jj // GLA_CHUNK)
    lower, upper = same & (jj <= ii), same & (jj >= ii)
    b = jnp.dot(lower.astype(F32), g, precision=HIGHEST, preferred_element_type=F32)
    return zg, live, lower, upper, b


def _tril64():
    ii = lax.broadcasted_iota(jnp.int32, (GLA_CHUNK, GLA_CHUNK), 0)
    jj = lax.broadcasted_iota(jnp.int32, (GLA_CHUNK, GLA_CHUNK), 1)
    return jj <= ii


def _gla_fwd(proj, wg_p, bg, gnw, shards):
    rows = proj.shape[0]
    nc = rows // GLA_CHUNK
    group = _gla_group(nc)
    steps, nrows = nc // group, group * GLA_CHUNK
    ns = len(shards)

    def body(q_ref, k_ref, v_ref, r_ref, lr_ref, wg_ref, bg_ref, gnw_ref, *rest):
        shard_refs, rest = rest[:ns], rest[ns:]
        oraw_ref, og_ref, st_ref = rest[:3]
        gathered_refs, rest = rest[3:3 + ns], rest[3 + ns:]
        state = rest[0]
        c = pl.program_id(0)

        @pl.when(c == 0)
        def _():
            state[...] = jnp.zeros_like(state)

        _place_gather(c, steps, shard_refs, gathered_refs, rest[1:])
        _, _, _, _, b = _gla_gates(lr_ref[...], wg_ref[...], bg_ref[...], c * nrows, nrows)
        eb = jnp.exp(b)
        gq = q_ref[...] * (GLA_DK ** -0.5) * eb
        gk = k_ref[...] * jnp.exp(-b)
        v = v_ref[...]
        gnw_v = gnw_ref[...]
        tril = _tril64()
        for h in range(GLA_HEADS):
            s64 = slice(h * GLA_DK, (h + 1) * GLA_DK)
            s128 = slice(h * GLA_DV, (h + 1) * GLA_DV)
            st = state[h]
            for gi in range(group):
                rs = slice(gi * GLA_CHUNK, (gi + 1) * GLA_CHUNK)
                qh, kh, vh = gq[rs, s64], gk[rs, s64], v[rs, s128]
                eblh = eb[(gi + 1) * GLA_CHUNK - 1:(gi + 1) * GLA_CHUNK, s64]
                st_ref[gi, h] = st
                a = jnp.where(tril, _mm_nt(qh, kh), 0.0)
                o = _mm(a, vh) + _mm_nt(qh, st)
                st = st * eblh + _mm_tn(vh, kh * eblh)
                oraw_ref[rs, s128] = o
                rstd = lax.rsqrt(jnp.mean(o * o, axis=-1, keepdims=True) + EPS)
                rh = r_ref[rs, s128]
                og_ref[rs, s128] = (o * rstd * gnw_v * (rh * _sigmoid(rh))).astype(ACT_DTYPE)
            state[h] = st

    nb = lambda w, col: pl.BlockSpec((nrows, w), lambda c: (c, col // w))
    const = lambda shape: pl.BlockSpec(shape, lambda c: (0,) * len(shape))
    outs = pl.pallas_call(
        body, name="gla_fwd", grid=(steps,),
        in_specs=[nb(256, C_GQ), nb(256, C_GK), nb(512, C_GV), nb(512, C_GR), nb(128, C_LR),
                  const((128, 256)), const((1, 256)), const((1, 128))] + [ANY] * ns,
        out_specs=[pl.BlockSpec((nrows, 512), lambda c: (c, 0)), pl.BlockSpec((nrows, 512), lambda c: (c, 0)),
                   pl.BlockSpec((group, GLA_HEADS, GLA_DV, GLA_DK), lambda c: (c, 0, 0, 0))] + [ANY] * ns,
        out_shape=[jax.ShapeDtypeStruct((rows, 512), F32), jax.ShapeDtypeStruct((rows, 512), ACT_DTYPE),
                   jax.ShapeDtypeStruct((nc, GLA_HEADS, GLA_DV, GLA_DK), F32)] + _gathered_shapes(shards),
        scratch_shapes=[pltpu.VMEM((GLA_HEADS, GLA_DV, GLA_DK), F32)] + _gather_sems(ns),
        compiler_params=_cp(("arbitrary",)),
    )(proj, proj, proj, proj, proj, wg_p, bg, gnw, *shards)
    return outs[0], outs[1], outs[2], _with_own_block(outs[3:], shards)


def _swa_mask(n):
    shape = (SWA_GROUP * SWA_BLOCK, 3 * SWA_BLOCK)
    qi = lax.broadcasted_iota(jnp.int32, shape, 0) & (SWA_BLOCK - 1)
    jj = lax.broadcasted_iota(jnp.int32, shape, 1)
    meta = (jj < SWA_BLOCK) & (jj >= META0) & ((n > 0) | (jj <= qi))
    prev = (jj >= SWA_BLOCK) & (jj < 2 * SWA_BLOCK) & (n >= 2) & (jj - SWA_BLOCK > qi)
    cur = (jj >= 2 * SWA_BLOCK) & (n >= 1) & (jj - 2 * SWA_BLOCK <= qi)
    return meta | prev | cur


def _stack_heads(t, kvh):
    return jnp.concatenate([t[:, (kvh * SWA_GROUP + g) * SWA_HD:(kvh * SWA_GROUP + g + 1) * SWA_HD]
                            for g in range(SWA_GROUP)], axis=0)


def _stack_sinks(sink_ref, kvh):
    return jnp.concatenate([jnp.full((SWA_BLOCK, 1), sink_ref[0, kvh * SWA_GROUP + g], F32)
                            for g in range(SWA_GROUP)], axis=0)


def _swa_group(nblk):
    return 5 if nblk % 5 == 0 else 1


def _swa_specs(group):
    blk = lambda w: pl.BlockSpec((group * SWA_BLOCK, w), lambda n: (n, 0))
    first = pl.BlockSpec((SWA_BLOCK, 128), lambda n: (0, 0))
    prev = pl.BlockSpec((SWA_BLOCK, 128), lambda n: (jnp.maximum(n * group - 1, 0), 0))
    return blk, first, prev


def _swa_keys(first_ref, prev_ref, cur_ref, g):
    own = cur_ref[g * SWA_BLOCK:(g + 1) * SWA_BLOCK, :]
    before = prev_ref[...] if g == 0 else cur_ref[(g - 1) * SWA_BLOCK:g * SWA_BLOCK, :]
    return jnp.concatenate([first_ref[...], before, own], axis=0)


def _swa_fwd(qr, kr, vr, sinks, shards):
    rows = qr.shape[0]
    nblk = rows // SWA_BLOCK
    group = _swa_group(nblk)
    steps = nblk // group
    ns = len(shards)

    def body(q_ref, k0, kp, kc, v0, vp, vc, sink_ref, *rest):
        o_ref = rest[ns]
        _place_gather(pl.program_id(0), steps, rest[:ns], rest[ns + 1:2 * ns + 1], rest[2 * ns + 1:])
        for g in range(group):
            n = pl.program_id(0) * group + g
            rs = slice(g * SWA_BLOCK, (g + 1) * SWA_BLOCK)
            kall, vall = _swa_keys(k0, kp, kc, g), _swa_keys(v0, vp, vc, g)
            mask = _swa_mask(n)[0:SWA_BLOCK]
            for head in range(SWA_HEADS):
                hs = slice(head * SWA_HD, (head + 1) * SWA_HD)
                kv = slice((head // SWA_GROUP) * SWA_HD, (head // SWA_GROUP + 1) * SWA_HD)
                s = jnp.where(mask, _mm_nt(q_ref[rs, hs], kall[:, kv]), NEG)
                sink = sink_ref[0, head]
                m = jnp.maximum(jnp.max(s, axis=-1, keepdims=True), sink)
                p = jnp.exp(s - m)
                den = jnp.sum(p, axis=-1, keepdims=True) + jnp.exp(sink - m)
                o_ref[rs, hs] = (_mm(p, vall[:, kv]) / den).astype(ACT_DTYPE)

    blk, first, prev = _swa_specs(group)
    outs = pl.pallas_call(
        body, name="swa_fwd", grid=(steps,),
        in_specs=[blk(512), first, prev, blk(128), first, prev, blk(128),
                  pl.BlockSpec(memory_space=pltpu.SMEM)] + [ANY] * ns,
        out_specs=[blk(512)] + [ANY] * ns,
        out_shape=[jax.ShapeDtypeStruct((rows, 512), ACT_DTYPE)] + _gathered_shapes(shards),
        scratch_shapes=_gather_sems(ns),
        compiler_params=_cp(("arbitrary",)),
    )(qr, kr, kr, kr, vr, vr, vr, sinks, *shards)
    return outs[0], _with_own_block(outs[1:], shards)


def _out_proj(x, lead, og, osw, wout, nfw, tm):
    rows = LEAD + x.shape[0]
    nb = tm // LEAD

    def body(*refs):
        x_refs, (lead_ref, og_ref, os_ref, w_ref, nw_ref, h1_ref, f_ref, ft_ref) = refs[:nb], refs[nb:]
        h0 = _h_tile(pl.program_id(0), lead_ref, x_refs)
        h1 = h0 + _mm(og_ref[...], w_ref[0:512, :]) + _mm(os_ref[...], w_ref[512:1024, :])
        h1_ref[...] = h1
        rstd = lax.rsqrt(jnp.mean(h1 * h1, axis=-1, keepdims=True) + EPS)
        f = h1 * rstd * nw_ref[...]
        f_ref[...] = f.astype(ACT_DTYPE)
        ft_ref[...] = f.T.astype(ACT_DTYPE)

    row = lambda w: pl.BlockSpec((tm, w), lambda i: (i, 0))
    return pl.pallas_call(
        body, name="out_proj", grid=(rows // tm,),
        in_specs=_token_specs(tm) + [pl.BlockSpec((LEAD, D), lambda i: (0, 0)), row(512), row(512),
                                     pl.BlockSpec((D, D), lambda i: (0, 0)), pl.BlockSpec((1, D), lambda i: (0, 0))],
        out_specs=[row(D), row(D), pl.BlockSpec((D, tm), lambda i: (0, i))],
        out_shape=[jax.ShapeDtypeStruct((rows, D), F32), jax.ShapeDtypeStruct((rows, D), ACT_DTYPE),
                   jax.ShapeDtypeStruct((D, rows), ACT_DTYPE)],
        compiler_params=_cp(("arbitrary",), 48),
    )(*([x] * nb), lead, og, osw, wout, nfw)


def _ffn_fwd(f, h1, w1, w2, tgt, fnw, tm):
    rows = f.shape[0]
    nj = D_FF // FF_WIDE
    nb = tm // LEAD

    def body(f_ref, h1_ref, w1_ref, w2_ref, nw_ref, *rest):
        t_refs, (a_ref, dh2_ref, dh2t_ref, loss_ref, gfn_ref, acc) = rest[:nb], rest[nb:]
        i, j = pl.program_id(0), pl.program_id(1)

        @pl.when((i == 0) & (j == 0))
        def _():
            loss_ref[...] = jnp.zeros_like(loss_ref)
            gfn_ref[...] = jnp.zeros_like(gfn_ref)

        @pl.when(j == 0)
        def _():
            acc[...] = jnp.zeros_like(acc)

        a = _mm(f_ref[...], w1_ref[...])
        a_ref[...] = a.astype(ACT_DTYPE)
        z = jnp.square(jnp.maximum(a, 0.0))
        acc[...] += _mm(z, w2_ref[...])

        @pl.when(j == nj - 1)
        def _():
            h2 = h1_ref[...] + acc[...]
            rstd = lax.rsqrt(jnp.mean(h2 * h2, axis=-1, keepdims=True) + EPS)
            hn = h2 * rstd
            nw = nw_ref[...]
            row = i * tm + lax.broadcasted_iota(jnp.int32, (tm, 1), 0)
            target = jnp.concatenate([t[...] for t in t_refs], axis=0)
            err = jnp.where(row >= LEAD, hn * nw - target, 0.0)
            row_loss = jnp.sum(err * err, axis=-1, keepdims=True) * (1.0 / D)
            loss_ref[...] += jnp.broadcast_to(0.5 * jnp.sum(row_loss, axis=0, keepdims=True), loss_ref.shape)
            dy = err * (1.0 / D)
            gfn_ref[...] += jnp.broadcast_to(jnp.sum(dy * hn, axis=0, keepdims=True), gfn_ref.shape)
            dhn = dy * nw
            dh2 = rstd * (dhn - hn * jnp.mean(dhn * hn, axis=-1, keepdims=True))
            dh2_ref[...] = dh2
            dh2t_ref[...] = dh2.T.astype(ACT_DTYPE)

    return pl.pallas_call(
        body, name="ffn_fwd", grid=(rows // tm, nj),
        in_specs=[pl.BlockSpec((tm, D), lambda i, j: (i, 0)), pl.BlockSpec((tm, D), lambda i, j: (i, 0)),
                  pl.BlockSpec((D, FF_WIDE), lambda i, j: (0, j)),
                  pl.BlockSpec((FF_WIDE, D), lambda i, j: (j, 0)),
                  pl.BlockSpec((1, D), lambda i, j: (0, 0))] + _token_specs(tm, grid_rank=2),
        out_specs=[pl.BlockSpec((tm, FF_WIDE), lambda i, j: (i, j)), pl.BlockSpec((tm, D), lambda i, j: (i, 0)),
                   pl.BlockSpec((D, tm), lambda i, j: (0, i)),
                   pl.BlockSpec((8, 128), lambda i, j: (0, 0)), pl.BlockSpec((8, D), lambda i, j: (0, 0))],
        out_shape=[jax.ShapeDtypeStruct((rows, D_FF), ACT_DTYPE), jax.ShapeDtypeStruct((rows, D), F32),
                   jax.ShapeDtypeStruct((D, rows), ACT_DTYPE),
                   jax.ShapeDtypeStruct((8, 128), F32), jax.ShapeDtypeStruct((8, D), F32)],
        scratch_shapes=[pltpu.VMEM((tm, D), F32)],
        compiler_params=_cp(("arbitrary", "arbitrary"), 56),
    )(f, h1, w1, w2, fnw, *([tgt] * nb))


def _ffn_bwd_act(dh2, a, w1, w2, h1, nfw, tm):
    rows = dh2.shape[0]
    nj = D_FF // FF_WIDE

    def body(dh2_ref, a_ref, w1_ref, w2_ref, h1_ref, nw_ref, da_ref, dh1_ref, gnf_ref, acc):
        i, j = pl.program_id(0), pl.program_id(1)

        @pl.when((i == 0) & (j == 0))
        def _():
            gnf_ref[...] = jnp.zeros_like(gnf_ref)

        @pl.when(j == 0)
        def _():
            acc[...] = jnp.zeros_like(acc)

        dz = _mm_nt(dh2_ref[...], w2_ref[...])
        da = dz * (2.0 * jnp.maximum(a_ref[...].astype(F32), 0.0))
        da_ref[...] = da.astype(ACT_DTYPE)
        acc[...] += _mm_nt(da, w1_ref[...])

        @pl.when(j == nj - 1)
        def _():
            h1 = h1_ref[...]
            rstd = lax.rsqrt(jnp.mean(h1 * h1, axis=-1, keepdims=True) + EPS)
            hn = h1 * rstd
            df = acc[...]
            gnf_ref[...] += jnp.broadcast_to(jnp.sum(df * hn, axis=0, keepdims=True), gnf_ref.shape)
            dfn = df * nw_ref[...]
            dh1_ref[...] = dh2_ref[...] + rstd * (dfn - hn * jnp.mean(dfn * hn, axis=-1, keepdims=True))

    return pl.pallas_call(
        body, name="ffn_bwd_act", grid=(rows // tm, nj),
        in_specs=[pl.BlockSpec((tm, D), lambda i, j: (i, 0)), pl.BlockSpec((tm, FF_WIDE), lambda i, j: (i, j)),
                  pl.BlockSpec((D, FF_WIDE), lambda i, j: (0, j)),
                  pl.BlockSpec((FF_WIDE, D), lambda i, j: (j, 0)),
                  pl.BlockSpec((tm, D), lambda i, j: (i, 0)), pl.BlockSpec((1, D), lambda i, j: (0, 0))],
        out_specs=[pl.BlockSpec((tm, FF_WIDE), lambda i, j: (i, j)), pl.BlockSpec((tm, D), lambda i, j: (i, 0)),
                   pl.BlockSpec((8, D), lambda i, j: (0, 0))],
        out_shape=[jax.ShapeDtypeStruct((rows, D_FF), ACT_DTYPE), jax.ShapeDtypeStruct((rows, D), F32),
                   jax.ShapeDtypeStruct((8, D), F32)],
        scratch_shapes=[pltpu.VMEM((tm, D), F32)],
        compiler_params=_cp(("arbitrary", "arbitrary"), 56),
    )(dh2, a, w1, w2, h1, nfw)


def _ffn_bwd_weights(ft, a, da, dh2t, tm):
    rows = a.shape[0]
    steps = rows // tm

    def body(ft_ref, a_ref, da_ref, dh2t_ref, dw1_ref, dw2_ref, dw2t):
        i = pl.program_id(1)

        @pl.when(i == 0)
        def _():
            dw1_ref[...] = jnp.zeros_like(dw1_ref)
            dw2t[...] = jnp.zeros_like(dw2t)

        z = jnp.square(jnp.maximum(a_ref[...].astype(F32), 0.0))
        dw1_ref[...] += _mm(ft_ref[...], da_ref[...])
        dw2t[...] += _mm(dh2t_ref[...], z)

        @pl.when(i == steps - 1)
        def _():
            dw2_ref[...] = dw2t[...].T

    return pl.pallas_call(
        body, name="ffn_bwd_weights", grid=(N_DEV, steps),
        in_specs=[pl.BlockSpec((D, tm), lambda j, i: (0, i)), pl.BlockSpec((tm, FF_TILE), lambda j, i: (i, j)),
                  pl.BlockSpec((tm, FF_TILE), lambda j, i: (i, j)), pl.BlockSpec((D, tm), lambda j, i: (0, i))],
        out_specs=[pl.BlockSpec((None, None, D, FF_TILE), lambda j, i: (j % 2, j // 2, 0, 0)),
                   pl.BlockSpec((None, None, FF_TILE, D), lambda j, i: (j % 2, j // 2, 0, 0))],
        out_shape=[jax.ShapeDtypeStruct((2, 4, D, FF_TILE), F32), jax.ShapeDtypeStruct((2, 4, FF_TILE, D), F32)],
        scratch_shapes=[pltpu.VMEM((D, FF_TILE), F32)],
        compiler_params=_cp(("arbitrary", "arbitrary"), 48),
    )(ft, a, da, dh2t)


def _out_proj_bwd(dh1, og, osw, wout, tm, partials):
    rows = dh1.shape[0]
    steps = rows // tm
    ns = len(partials)

    def body(dh1_ref, og_ref, os_ref, w_ref, *rest):
        part_refs, rest = rest[:ns], rest[ns:]
        dog_ref, dos_ref, dw_ref = rest[:3]
        land_refs, (send_sems, recv_sems) = rest[3:3 + ns], rest[3 + ns:]
        i = pl.program_id(0)
        start, finish = _sibling_schedule(part_refs, land_refs, send_sems, recv_sems)

        @pl.when(i == 0)
        def _():
            dw_ref[...] = jnp.zeros_like(dw_ref)
            start()

        pl.when(i == steps - 1)(finish)

        dh1 = dh1_ref[...].astype(MXU_DTYPE)
        dog_ref[...] = _mm_nt(dh1, w_ref[0:512, :])
        dos_ref[...] = _mm_nt(dh1, w_ref[512:1024, :])
        for half, ref in enumerate((og_ref, os_ref)):
            dw = _mm_tn(ref[...], dh1)
            for blk in range(4):
                shard = half * 4 + blk
                dw_ref[shard % 2, shard // 2] += dw[blk * 128:(blk + 1) * 128, :]

    row = lambda w: pl.BlockSpec((tm, w), lambda i: (i, 0))
    outs = pl.pallas_call(
        body, name="out_proj_bwd", grid=(steps,),
        in_specs=[row(D), row(512), row(512), pl.BlockSpec((D, D), lambda i: (0, 0))] + [ANY] * ns,
        out_specs=[row(512), row(512), pl.BlockSpec((2, 4, 128, D), lambda i: (0, 0, 0, 0))] + [ANY] * ns,
        out_shape=[jax.ShapeDtypeStruct((rows, 512), F32), jax.ShapeDtypeStruct((rows, 512), F32),
                   jax.ShapeDtypeStruct((2, 4, 128, D), F32)] + _sibling_shapes(partials),
        scratch_shapes=_sibling_sems(ns),
        compiler_params=_cp(("arbitrary",), 48),
    )(dh1, og, osw, wout, *partials)
    return outs[0], outs[1], outs[2], outs[3:]


def _swa_bwd(qr, kr, vr, osw, dos, sinks, jobs):
    rows = qr.shape[0]
    nblk = rows // SWA_BLOCK
    group = _swa_group(nblk)
    steps = nblk // group
    ns = jobs.n

    def body(q_ref, k0, kp, kc, v0, vp, vc, o_ref, do_ref, sink_ref, *rest):
        dq_ref, dk_ref, dv_ref, dsink_ref = rest[ns:ns + 4]
        start, finish = jobs.bind(rest[:ns], rest[ns + 4:2 * ns + 4], rest[2 * ns + 4:])
        step = pl.program_id(0)

        @pl.when(step == 0)
        def _():
            dk_ref[...] = jnp.zeros_like(dk_ref)
            dv_ref[...] = jnp.zeros_like(dv_ref)
            dsink_ref[...] = jnp.zeros_like(dsink_ref)
            start()

        pl.when(step == steps - 1)(finish)
        for g in range(group):
            block(step * group + g, g, q_ref, k0, kp, kc, v0, vp, vc, o_ref, do_ref, sink_ref,
                  dq_ref, dk_ref, dv_ref, dsink_ref)

    def block(n, g, q_ref, k0, kp, kc, v0, vp, vc, o_ref, do_ref, sink_ref, dq_ref, dk_ref, dv_ref, dsink_ref):
        rs = slice(g * SWA_BLOCK, (g + 1) * SWA_BLOCK)
        q = q_ref[rs, :]
        kall, vall = _swa_keys(k0, kp, kc, g), _swa_keys(v0, vp, vc, g)
        mask = _swa_mask(n)
        do_all = do_ref[rs, :]
        o_all = o_ref[rs, :].astype(F32)
        dq, dk, dv = [], [], []
        for kvh in range(SWA_KV):
            kv = slice(kvh * SWA_HD, (kvh + 1) * SWA_HD)
            q4, do4, o4 = _stack_heads(q, kvh), _stack_heads(do_all, kvh), _stack_heads(o_all, kvh)
            sink4 = _stack_sinks(sink_ref, kvh)
            s = jnp.where(mask, _mm_nt(q4, kall[:, kv]), NEG)
            m = jnp.maximum(jnp.max(s, axis=-1, keepdims=True), sink4)
            e = jnp.exp(s - m)
            inv = 1.0 / (jnp.sum(e, axis=-1, keepdims=True) + jnp.exp(sink4 - m))
            p = e * inv
            delta = jnp.sum(do4 * o4, axis=-1, keepdims=True)
            ds = p * (_mm_nt(do4, vall[:, kv]) - delta)
            dq4 = _mm(ds, kall[:, kv])
            dq += [dq4[g * SWA_BLOCK:(g + 1) * SWA_BLOCK] for g in range(SWA_GROUP)]
            dk.append(_mm_tn(ds, q4))
            dv.append(_mm_tn(p, do4))
            sink_term = jnp.exp(sink4 - m) * inv * delta
            for g in range(SWA_GROUP):
                head = kvh * SWA_GROUP + g
                dsink = -jnp.sum(sink_term[g * SWA_BLOCK:(g + 1) * SWA_BLOCK], axis=0, keepdims=True)
                dsink_ref[head:head + 1, :] += jnp.broadcast_to(dsink, (1, 128))
        dq_ref[rs, :] = jnp.concatenate(dq, axis=1)
        dk_all = jnp.concatenate(dk, axis=1)
        dv_all = jnp.concatenate(dv, axis=1)
        prev0 = pl.multiple_of(jnp.maximum(n - 1, 0) * SWA_BLOCK, SWA_BLOCK)
        cur0 = pl.multiple_of(n * SWA_BLOCK, SWA_BLOCK)
        for ref, val in ((dk_ref, dk_all), (dv_ref, dv_all)):
            ref[0:SWA_BLOCK, :] += val[0:SWA_BLOCK]
            ref[pl.ds(prev0, SWA_BLOCK), :] += val[SWA_BLOCK:2 * SWA_BLOCK]
            ref[pl.ds(cur0, SWA_BLOCK), :] += val[2 * SWA_BLOCK:]

    blk, first, prev = _swa_specs(group)
    whole = pl.BlockSpec((rows, 128), lambda n: (0, 0))
    outs = pl.pallas_call(
        body, name="swa_bwd", grid=(steps,),
        in_specs=[blk(512), first, prev, blk(128), first, prev, blk(128), blk(512), blk(512),
                  pl.BlockSpec(memory_space=pltpu.SMEM)] + [ANY] * ns,
        out_specs=[blk(512), whole, whole, pl.BlockSpec((8, 128), lambda n: (0, 0))] + [ANY] * ns,
        out_shape=[jax.ShapeDtypeStruct((rows, 512), F32), jax.ShapeDtypeStruct((rows, 128), F32),
                   jax.ShapeDtypeStruct((rows, 128), F32), jax.ShapeDtypeStruct((8, 128), F32)] + jobs.out_shapes,
        scratch_shapes=jobs.sems,
        compiler_params=_cp(("arbitrary",), 48),
    )(qr, kr, kr, kr, vr, vr, vr, osw, dos, sinks, *jobs.inputs)
    return outs[0], outs[1], outs[2], outs[3], jobs.split(outs[4:])


def _gla_bwd(proj, oraw, states, dog, wg_p, bg, gnw, jobs):
    rows = proj.shape[0]
    nc = rows // GLA_CHUNK
    group = _gla_group(nc)
    steps, nrows = nc // group, group * GLA_CHUNK
    ns = jobs.n

    def body(q_ref, k_ref, v_ref, r_ref, lr_ref, oraw_ref, st_ref, dog_ref, wg_ref, bg_ref, gnw_ref, *rest):
        dq_ref, dk_ref, dv_ref, dr_ref, dlr_ref, dwg_ref, dbg_ref, dgnw_ref = rest[ns:ns + 8]
        dstate, db_scr = rest[2 * ns + 8:2 * ns + 10]
        start, finish = jobs.bind(rest[:ns], rest[ns + 8:2 * ns + 8], rest[2 * ns + 10:])
        t = pl.program_id(0)
        c = steps - 1 - t

        @pl.when(t == 0)
        def _():
            dstate[...] = jnp.zeros_like(dstate)
            dwg_ref[...] = jnp.zeros_like(dwg_ref)
            dbg_ref[...] = jnp.zeros_like(dbg_ref)
            dgnw_ref[...] = jnp.zeros_like(dgnw_ref)
            start()

        pl.when(t == steps - 1)(finish)

        lr, wg = lr_ref[...], wg_ref[...]
        zg, live, _, upper, b = _gla_gates(lr, wg, bg_ref[...], c * nrows, nrows)
        eb, enb = jnp.exp(b), jnp.exp(-b)
        scale = GLA_DK ** -0.5
        gq = q_ref[...] * scale * eb
        gk = k_ref[...] * enb
        v = v_ref[...]
        gnw_v = gnw_ref[...]
        tril = _tril64()
        is_last = lax.broadcasted_iota(jnp.int32, (GLA_CHUNK, 1), 0) == GLA_CHUNK - 1
        dgnw = jnp.zeros((1, GLA_DV), F32)
        for h in range(GLA_HEADS):
            s64 = slice(h * GLA_DK, (h + 1) * GLA_DK)
            s128 = slice(h * GLA_DV, (h + 1) * GLA_DV)
            dsp = dstate[h]
            for gi in reversed(range(group)):
                rs = slice(gi * GLA_CHUNK, (gi + 1) * GLA_CHUNK)
                qh, kh, vh = gq[rs, s64], gk[rs, s64], v[rs, s128]
                ebh, enbh = eb[rs, s64], enb[rs, s64]
                eblh = eb[(gi + 1) * GLA_CHUNK - 1:(gi + 1) * GLA_CHUNK, s64]
                klh = kh * eblh
                st = st_ref[gi, h]
                o, rh, dout = oraw_ref[rs, s128], r_ref[rs, s128], dog_ref[rs, s128]
                rstd = lax.rsqrt(jnp.mean(o * o, axis=-1, keepdims=True) + EPS)
                on = o * rstd
                sg = _sigmoid(rh)
                dr_ref[rs, s128] = (dout * (on * gnw_v) * (sg * (1.0 + rh * (1.0 - sg)))).astype(ACT_DTYPE)
                dy = dout * (rh * sg)
                dgnw = dgnw + jnp.sum(dy * on, axis=0, keepdims=True)
                don = dy * gnw_v
                do = rstd * (don - on * jnp.mean(don * on, axis=-1, keepdims=True))
                a = jnp.where(tril, _mm_nt(qh, kh), 0.0)
                da = jnp.where(tril, _mm_nt(do, vh), 0.0)
                dkl = _mm(vh, dsp)
                dv_ref[rs, s128] = (_mm_tn(a, do) + _mm_nt(klh, dsp)).astype(ACT_DTYPE)
                debl = jnp.sum(dsp * st, axis=0, keepdims=True)
                dgq = _mm(da, kh) + _mm(do, st)
                dgk = _mm_tn(da, qh)
                dsp = dsp * eblh + _mm_tn(do, qh)
                dq_ref[rs, s64] = (dgq * (scale * ebh)).astype(ACT_DTYPE)
                dk_ref[rs, s64] = ((dgk + dkl * eblh) * enbh).astype(ACT_DTYPE)
                last = debl * eblh + jnp.sum(dkl * klh, axis=0, keepdims=True)
                db_scr[rs, s64] = dgq * qh - dgk * kh - dkl * klh + jnp.where(is_last, last, 0.0)
            dstate[h] = dsp
        dg = jnp.dot(upper.astype(F32), db_scr[...], precision=HIGHEST, preferred_element_type=F32)
        dzg = jnp.where(live, dg * _sigmoid(-zg) * (1.0 / GLA_TAU), 0.0)
        dlr_ref[...] = _mm_nt(dzg, wg).astype(ACT_DTYPE)
        dwg_ref[...] += _mm_tn(lr, dzg)
        dbg_ref[...] += jnp.broadcast_to(jnp.sum(dzg, axis=0, keepdims=True), dbg_ref.shape)
        dgnw_ref[...] += jnp.broadcast_to(dgnw, dgnw_ref.shape)

    nb = lambda w, col: pl.BlockSpec((nrows, w), lambda t: (steps - 1 - t, col // w))
    const = lambda shape: pl.BlockSpec(shape, lambda t: (0,) * len(shape))
    outs = pl.pallas_call(
        body, name="gla_bwd", grid=(steps,),
        in_specs=[nb(256, C_GQ), nb(256, C_GK), nb(512, C_GV), nb(512, C_GR), nb(128, C_LR), nb(512, 0),
                  pl.BlockSpec((group, GLA_HEADS, GLA_DV, GLA_DK), lambda t: (steps - 1 - t, 0, 0, 0)), nb(512, 0),
                  const((128, 256)), const((1, 256)), const((1, 128))] + [ANY] * ns,
        out_specs=[nb(256, 0), nb(256, 0), nb(512, 0), nb(512, 0), nb(128, 0),
                   const((128, 256)), const((8, 256)), const((8, 128))] + [ANY] * ns,
        out_shape=[jax.ShapeDtypeStruct((rows, 256), ACT_DTYPE), jax.ShapeDtypeStruct((rows, 256), ACT_DTYPE),
                   jax.ShapeDtypeStruct((rows, 512), ACT_DTYPE), jax.ShapeDtypeStruct((rows, 512), ACT_DTYPE),
                   jax.ShapeDtypeStruct((rows, 128), ACT_DTYPE), jax.ShapeDtypeStruct((128, 256), F32),
                   jax.ShapeDtypeStruct((8, 256), F32), jax.ShapeDtypeStruct((8, 128), F32)] + jobs.out_shapes,
        scratch_shapes=[pltpu.VMEM((GLA_HEADS, GLA_DV, GLA_DK), F32), pltpu.VMEM((nrows, 256), F32)] + jobs.sems,
        compiler_params=_cp(("arbitrary",)),
    )(proj, proj, proj, proj, proj, oraw, states, dog, wg_p, bg, gnw, *jobs.inputs)
    return outs[:8], jobs.split(outs[8:])


def _in_proj_bwd(x, lead, dh1, nw, win_p, dgv, dgr, dsq, dgq, dgk, dsk, dsv, dlr, tabs, tm):
    seq = x.shape[0]
    rows = LEAD + seq
    nb = tm // LEAD
    steps = rows // tm

    def first_copy(scr, gx_ref, sem):
        return pltpu.make_async_copy(scr.at[pl.ds(LEAD, tm - LEAD)], gx_ref.at[pl.ds(0, tm - LEAD)], sem)

    def tile_copy(scr, gx_ref, sem, step):
        start = pl.multiple_of(jnp.maximum(step * tm - LEAD, 0), LEAD)
        return pltpu.make_async_copy(scr, gx_ref.at[pl.ds(start, tm)], sem)

    def body(*refs):
        x_refs, refs = refs[:nb], refs[nb:]
        (lead_ref, dh1_ref, nw_ref, w_ref, dgv_ref, dgr_ref, dsq_ref, dgq_ref, dgk_ref, dsk_ref, dsv_ref, dlr_ref,
         c_ref, sa_ref, sb_ref, gx_ref, dlead_ref, dproj_ref, ut_ref, gnm_ref, scr, sem) = refs
        i = pl.program_id(0)

        @pl.when(i == 0)
        def _():
            gnm_ref[...] = jnp.zeros_like(gnm_ref)

        cos, sa, sb = c_ref[...], sa_ref[...], sb_ref[...]
        dsq_v = (_unrope(dsq_ref[...], cos, sa, sb) * (SWA_HD ** -0.5)).astype(MXU_DTYPE)
        dsk_v = _unrope(dsk_ref[...], cos, sa, sb).astype(MXU_DTYPE)
        dproj = jnp.concatenate(
            [dgv_ref[...].astype(MXU_DTYPE), dgr_ref[...].astype(MXU_DTYPE), dsq_v, dgq_ref[...].astype(MXU_DTYPE),
             dgk_ref[...].astype(MXU_DTYPE), dsk_v, dsv_ref[...].astype(MXU_DTYPE), dlr_ref[...].astype(MXU_DTYPE)],
            axis=1)
        dproj_ref[...] = dproj
        h = _h_tile(i, lead_ref, x_refs)
        rstd = lax.rsqrt(jnp.mean(h * h, axis=-1, keepdims=True) + EPS)
        hn = h * rstd
        nw_v = nw_ref[...]
        ut_ref[...] = (hn * nw_v).T.astype(ACT_DTYPE)
        du = _mm_nt(dproj, w_ref[...])
        gnm_ref[...] += jnp.broadcast_to(jnp.sum(du * hn, axis=0, keepdims=True), gnm_ref.shape)
        dun = du * nw_v
        dh0 = dh1_ref[...] + rstd * (dun - hn * jnp.mean(dun * hn, axis=-1, keepdims=True))

        if tm > LEAD:
            pl.when(i == 1)(lambda: first_copy(scr, gx_ref, sem).wait())
        pl.when(i > 1)(lambda: tile_copy(scr, gx_ref, sem, i).wait())
        scr[...] = dh0

        @pl.when(i == 0)
        def _():
            dlead_ref[...] = dh0[0:LEAD]
            if tm > LEAD:
                first_copy(scr, gx_ref, sem).start()
                if steps == 1:
                    first_copy(scr, gx_ref, sem).wait()

        @pl.when(i > 0)
        def _():
            tile_copy(scr, gx_ref, sem, i).start()

        if steps > 1:
            pl.when(i == steps - 1)(lambda: tile_copy(scr, gx_ref, sem, i).wait())

    row = lambda w: pl.BlockSpec((tm, w), lambda i: (i, 0))
    const = lambda shape: pl.BlockSpec(shape, lambda i: (0,) * len(shape))
    return pl.pallas_call(
        body, name="in_proj_bwd", grid=(steps,),
        in_specs=_token_specs(tm) + [const((LEAD, D)), row(D), const((1, D)), const((D, DINP)),
                                     row(512), row(512), row(512), row(256), row(256), row(128), row(128), row(128),
                                     row(128), row(128), row(128)],
        out_specs=[ANY, const((LEAD, D)), row(DINP), pl.BlockSpec((D, tm), lambda i: (0, i)), const((8, D))],
        out_shape=[jax.ShapeDtypeStruct((seq, D), F32), jax.ShapeDtypeStruct((LEAD, D), F32),
                   jax.ShapeDtypeStruct((rows, DINP), ACT_DTYPE), jax.ShapeDtypeStruct((D, rows), ACT_DTYPE),
                   jax.ShapeDtypeStruct((8, D), F32)],
        scratch_shapes=[pltpu.VMEM((tm, D), F32), pltpu.SemaphoreType.DMA],
        compiler_params=_cp(("arbitrary",), 56),
    )(*([x] * nb), lead, dh1, nw, win_p, dgv, dgr, dsq, dgq, dgk, dsk, dsv, dlr, *tabs)


def _in_proj_bwd_weights(ut, dproj, tm):
    rows = dproj.shape[0]

    def body(ut_ref, dp_ref, dw_ref):
        @pl.when(pl.program_id(0) == 0)
        def _():
            dw_ref[...] = jnp.zeros_like(dw_ref)

        dw_ref[...] += _mm(ut_ref[...], dp_ref[...])

    return pl.pallas_call(
        body, name="in_proj_bwd_weights", grid=(rows // tm,),
        in_specs=[pl.BlockSpec((D, tm), lambda i: (0, i)), pl.BlockSpec((tm, DINP), lambda i: (i, 0))],
        out_specs=pl.BlockSpec((D, DINP), lambda i: (0, 0)),
        out_shape=jax.ShapeDtypeStruct((D, DINP), F32),
        compiler_params=_cp(("arbitrary",), 56),
    )(ut, dproj)


def _adamw(w, g, m, v):
    m = ADAM_B1 * m + (1.0 - ADAM_B1) * g
    v = ADAM_B2 * v + (1.0 - ADAM_B2) * jnp.square(g)
    m_hat = m / (1.0 - ADAM_B1 ** ADAM_STEP)
    v_hat = v / (1.0 - ADAM_B2 ** ADAM_STEP)
    delta = -ADAM_LR * (m_hat / (jnp.sqrt(v_hat) + ADAM_EPS) + ADAM_WD * w)
    return delta, m, v


ADAM_STEPS = 8


def _adamw_shards(where, items, name, jobs=None):
    jobs = jobs or _Jobs([])
    ns, nw = jobs.n, len(items)

    def body(where_ref, *rest):
        ins, rest = rest[:5 * nw], rest[5 * nw:]
        job_ins, rest = rest[:ns], rest[ns:]
        outs, rest = rest[:4 * nw], rest[4 * nw:]
        start, finish = jobs.bind(job_ins, rest[:ns], rest[ns:])
        i = pl.program_id(0)
        pl.when(i == 0)(start)
        pl.when(i == ADAM_STEPS - 1)(finish)
        for k in range(nw):
            p_ref, own_ref, w_ref, m_ref, v_ref = ins[5 * k:5 * k + 5]
            g_ref, d_ref, nm_ref, nv_ref = outs[4 * k:4 * k + 4]
            g = ((p_ref[0].astype(F32) + p_ref[1].astype(F32)) + p_ref[2].astype(F32)) + own_ref[...]
            g_ref[...] = g
            d_ref[...], nm_ref[...], nv_ref[...] = _adamw(w_ref[...], g, m_ref[...], v_ref[...])

    in_specs, out_specs, out_shape, operands = [], [], [], []
    for parts, own, w, m, v in items:
        r, cdim = w.shape
        tr = r // ADAM_STEPS
        spec = pl.BlockSpec((tr, cdim), lambda i, s: (i, 0))
        in_specs += [pl.BlockSpec((3, tr, cdim), lambda i, s: (0, i, 0)),
                     pl.BlockSpec((None, tr, cdim), lambda i, s: (s[1], i, 0)), spec, spec, spec]
        out_specs += [spec] * 4
        out_shape += [jax.ShapeDtypeStruct((r, cdim), F32)] * 4
        operands += [parts, own, w, m, v]
    outs = pl.pallas_call(
        body, name=name,
        grid_spec=pltpu.PrefetchScalarGridSpec(
            num_scalar_prefetch=1, grid=(ADAM_STEPS,),
            in_specs=in_specs + [ANY] * ns, out_specs=out_specs + [ANY] * ns, scratch_shapes=jobs.sems),
        out_shape=out_shape + jobs.out_shapes,
        compiler_params=_cp(("arbitrary",)),
    )(where, *operands, *jobs.inputs)
    return [outs[4 * k:4 * k + 4] for k in range(nw)], jobs.split(outs[4 * nw:])


def _adamw_small(items):
    n = len(items)

    def body(*refs):
        ins, outs = refs[:4 * n], refs[4 * n:]
        for k in range(n):
            w_ref, g_ref, m_ref, v_ref = ins[4 * k:4 * k + 4]
            d_ref, nm_ref, nv_ref = outs[3 * k:3 * k + 3]
            d_ref[...], nm_ref[...], nv_ref[...] = _adamw(w_ref[...], g_ref[...], m_ref[...], v_ref[...])

    vm = pl.BlockSpec(memory_space=pltpu.VMEM)
    shapes = [jax.ShapeDtypeStruct(w.shape, F32) for w, _, _, _ in items for _ in range(3)]
    outs = pl.pallas_call(body, name="adamw_small", in_specs=[vm] * (4 * n), out_specs=[vm] * (3 * n),
                          out_shape=shapes)(*[t for item in items for t in item])
    return [outs[3 * k:3 * k + 3] for k in range(n)]


def _add_own_half(where, full, theirs, name, wire_copy=False):
    _, _, r, cdim = full.shape
    tr = 128 if r % 128 == 0 else r

    def body(where_ref, a_ref, b_ref, *o_refs):
        total = a_ref[...] + b_ref[...]
        o_refs[0][...] = total
        if wire_copy:
            o_refs[1][...] = total.astype(WIRE_DTYPE)

    spec = pl.BlockSpec((4, tr, cdim), lambda i, s: (0, i, 0))
    shapes = [jax.ShapeDtypeStruct(theirs.shape, F32)] + ([jax.ShapeDtypeStruct(theirs.shape, WIRE_DTYPE)] if wire_copy else [])
    outs = pl.pallas_call(
        body, name=name,
        grid_spec=pltpu.PrefetchScalarGridSpec(
            num_scalar_prefetch=1, grid=(r // tr,),
            in_specs=[pl.BlockSpec((None, 4, tr, cdim), lambda i, s: (s[0], 0, i, 0)), spec],
            out_specs=[spec] * len(shapes)),
        out_shape=shapes, compiler_params=_cp(("arbitrary",)))(where, full, theirs)
    return outs if wire_copy else outs[0]


def kernel(x, meta_tokens, norm_mix_w, w_in, w_gate_up, b_gate, gla_norm_w, sinks, w_out, norm_ff_w, w_ff1, w_ff2, final_norm_w, loss_target, m_meta_tokens, m_norm_mix_w, m_w_in, m_w_gate_up, m_b_gate, m_gla_norm_w, m_sinks, m_w_out, m_norm_ff_w, m_w_ff1, m_w_ff2, m_final_norm_w, v_meta_tokens, v_norm_mix_w, v_w_in, v_w_gate_up, v_b_gate, v_gla_norm_w, v_sinks, v_w_out, v_norm_ff_w, v_w_ff1, v_w_ff2, v_final_norm_w):
    seq = x.shape[1]
    rows = LEAD + seq
    tm = _row_tile(rows)
    tm_wide = 1664 if rows % 1664 == 0 else tm
    dev =4 * lax.axis_index("x") + 2 * lax.axis_index("y") + lax.axis_index("c")

    small_shard = jnp.concatenate([meta_tokens, w_gate_up[0], jnp.zeros((N_META, 96), F32)], axis=1)
    g_in, g_small = _all_gather([w_in[0].astype(WIRE_DTYPE), small_shard])
    later_shards = [w_out[0].astype(WIRE_DTYPE), w_ff1[0].astype(WIRE_DTYPE), w_ff2[0].astype(WIRE_DTYPE)]
    win_full = jnp.transpose(g_in, (1, 0, 2)).reshape(D, DIN)
    cols = lambda r: win_full[:, r[0]:r[1]]
    win_p = jnp.concatenate([cols(O_GV), cols(O_GR), cols(O_SQ), cols(O_GQ), cols(O_GK), cols(O_SK), cols(O_SV),
                             cols(O_LR), jnp.zeros((D, 128 - GLA_RANK), WIRE_DTYPE)], axis=1)
    meta_full = jnp.transpose(g_small[:, :, 0:128], (1, 0, 2)).reshape(N_META, D)
    wg_full = jnp.transpose(g_small[:, :, 128:160], (1, 0, 2)).reshape(GLA_RANK, GLA_HEADS * GLA_DK)
    wg_p = jnp.concatenate([wg_full, jnp.zeros((128 - GLA_RANK, 256), F32)], axis=0)

    lead = jnp.concatenate([jnp.zeros((META0, D), F32), meta_full], axis=0)
    tabs = _rope_tables(rows)
    proj = _in_proj(x[0], lead, norm_mix_w, win_p, tm)
    oraw, og, states, _ = _gla_fwd(proj, wg_p, b_gate, gla_norm_w, [])
    qr, kr, vr = _swa_prep(proj, tabs, tm)
    osw, (g_out, g_w1, g_w2) = _swa_fwd(qr, kr, vr, sinks, later_shards)
    wout_full = g_out.reshape(D, D)
    w2_full = g_w2.reshape(D_FF, D)
    w1_full = jnp.transpose(g_w1, (1, 0, 2)).reshape(D, D_FF)
    h1, f, ft = _out_proj(x[0], lead, og, osw, wout_full, norm_ff_w, tm)
    a, dh2, dh2t, loss_p, gfn_p = _ffn_fwd(f, h1, w1_full, w2_full, loss_target[0], final_norm_w.reshape(1, D), tm)

    da, dh1, gnf_p = _ffn_bwd_act(dh2, a, w1_full, w2_full, h1, norm_ff_w, tm)
    dw1, dw2 = _ffn_bwd_weights(ft, a, da, dh2t, tm_wide)
    where = jnp.stack([lax.axis_index("c"), 2 * lax.axis_index("x") + lax.axis_index("y")]).astype(jnp.int32)
    dog, dos, dwout, theirs_ffn = _out_proj_bwd(dh1, og, osw, wout_full, tm, [dw1, dw2])
    sums_ffn = [_add_own_half(where, p, q, "reduce_pair_%d" % (2 + k))
                for k, (p, q) in enumerate(zip([dw1, dw2], theirs_ffn))]
    dsq, dsk, dsv, dsink_p, (parts_ffn, (theirs_wout,)) = _swa_bwd(
        qr, kr, vr, osw, dos, sinks, _Jobs([("chips", sums_ffn), ("sibling", [dwout])]))
    sum_wout = _add_own_half(where, dwout, theirs_wout, "reduce_pair_1")
    (dgq, dgk, dgv, dgr, dlr, dwg_p, dbg_p, dgnw_p), ((parts_wout,),) = _gla_bwd(
        proj, oraw, states, dog, wg_p, b_gate, gla_norm_w, _Jobs([("chips", [sum_wout])]))
    grad_x, dlead, dproj, ut, gnm_p = _in_proj_bwd(x[0], lead, dh1, norm_mix_w, win_p, dgv, dgr, dsq, dgq, dgk, dsk,
                                                   dsv, dlr, tabs, tm)
    grad_x = grad_x[None]
    dwin_p = _in_proj_bwd_weights(ut, dproj, tm_wide)

    pcols = lambda c0, r: dwin_p[:, c0:c0 + (r[1] - r[0])]
    dwin = jnp.concatenate([pcols(C_GQ, O_GQ), pcols(C_GK, O_GK), pcols(C_GV, O_GV), pcols(C_GR, O_GR),
                            pcols(C_LR, O_LR), pcols(C_SQ, O_SQ), pcols(C_SK, O_SK), pcols(C_SV, O_SV)], axis=1)
    dwin = jnp.transpose(dwin.reshape(D, 4, 2, DIN // N_DEV), (2, 1, 0, 3))
    (theirs_win,) = _rs_sibling([dwin])
    sum_win, sum_win_wire = _add_own_half(where, dwin, theirs_win, "reduce_pair_0", wire_copy=True)

    total = _all_reduce_small(dlead, dwg_p, gnm_p, gnf_p, gfn_p, dbg_p, dgnw_p, loss_p, dsink_p)
    g_meta = lax.dynamic_slice(total, (R_META, dev * 128), (N_META, 128))
    g_wg = lax.dynamic_slice(total, (R_WG, dev * 32), (GLA_RANK, 32))
    g_norm_mix, g_norm_ff = total[R_NORM_MIX:R_NORM_MIX + 1], total[R_NORM_FF:R_NORM_FF + 1]
    g_final_norm = total[R_FINAL:R_FINAL + 1]
    g_b_gate, g_gla_norm = total[R_B_GATE:R_B_GATE + 1, 0:256], total[R_GLA_NORM:R_GLA_NORM + 1, 0:128]
    g_sinks = total[R_SINKS:R_SINKS + SWA_HEADS, 0].reshape(1, SWA_HEADS)
    loss = total[R_LOSS, 0]

    ((g_wout, d_wout, nm_wout, nv_wout), (g_w1s, d_w1, nm_w1, nv_w1), (g_w2s, d_w2, nm_w2, nv_w2)), ((parts_win,),) = \
        _adamw_shards(where, [(parts_wout, sum_wout, w_out[0], m_w_out[0], v_w_out[0]),
                              (parts_ffn[0], sums_ffn[0], w_ff1[0], m_w_ff1[0], v_w_ff1[0]),
                              (parts_ffn[1], sums_ffn[1], w_ff2[0], m_w_ff2[0], v_w_ff2[0])],
                      "adamw_w_out_ff", _Jobs([("chips", [sum_win_wire])]))
    ((g_win, d_win, nm_win, nv_win),), _ = _adamw_shards(
        where, [(parts_win, sum_win, w_in[0], m_w_in[0], v_w_in[0])], "adamw_w_in")

    names = ["meta", "wg", "norm_mix", "b_gate", "gla_norm", "sinks", "norm_ff", "final_norm"]
    ws = [meta_tokens, w_gate_up, norm_mix_w, b_gate, gla_norm_w, sinks, norm_ff_w, final_norm_w]
    gs = [g_meta, g_wg, g_norm_mix, g_b_gate, g_gla_norm, g_sinks, g_norm_ff, g_final_norm]
    ms = [m_meta_tokens, m_w_gate_up, m_norm_mix_w, m_b_gate, m_gla_norm_w, m_sinks, m_norm_ff_w, m_final_norm_w]
    vs = [v_meta_tokens, v_w_gate_up, v_norm_mix_w, v_b_gate, v_gla_norm_w, v_sinks, v_norm_ff_w, v_final_norm_w]
    flat = lambda t: t.reshape(-1, t.shape[-1])
    small_out = _adamw_small([(flat(w), flat(g), flat(m), flat(v)) for w, g, m, v in zip(ws, gs, ms, vs)])
    d_small = {n: small_out[k][0].reshape(ws[k].shape) for k, n in enumerate(names)}
    nm_small = {n: small_out[k][1].reshape(ws[k].shape) for k, n in enumerate(names)}
    nv_small = {n: small_out[k][2].reshape(ws[k].shape) for k, n in enumerate(names)}
    g_small_d = {n: g.reshape(ws[k].shape) for k, (n, g) in enumerate(zip(names, gs))}

    def ordered(big, small_d):
        win_v, wout_v, w1_v, w2_v = big
        return (small_d["meta"], small_d["norm_mix"], win_v[None], small_d["wg"], small_d["b_gate"],
                small_d["gla_norm"], small_d["sinks"], wout_v[None], small_d["norm_ff"], w1_v[None], w2_v[None],
                small_d["final_norm"])

    return (loss, grad_x,
            *ordered((g_win, g_wout, g_w1s, g_w2s), g_small_d),
            *ordered((d_win, d_wout, d_w1, d_w2), d_small),
            *ordered((nm_win, nm_wout, nm_w1, nm_w2), nm_small),
            *ordered((nv_win, nv_wout, nv_w1, nv_w2), nv_small))
```

```python
import functools

import jax
import jax.numpy as jnp
from jax import lax
from jax.experimental import pallas as pl
from jax.experimental.pallas import tpu as pltpu

F32 = jnp.float32
MXU_DTYPE = jnp.bfloat16
ACT_DTYPE = jnp.bfloat16
WIRE_DTYPE = jnp.bfloat16

D = 1024
N_META = 16
LEAD = 128
META0 = LEAD - N_META
EPS = 1e-5
GLA_HEADS, GLA_DK, GLA_DV, GLA_RANK, GLA_CHUNK = 4, 64, 128, 16, 64
GLA_TAU = 16.0
SWA_HEADS, SWA_KV, SWA_GROUP, SWA_HD, SWA_BLOCK = 8, 2, 4, 64, 128
ROPE_DIM, ROPE_THETA = 16, 500000.0
D_FF = 4096
N_DEV = 8
FF_TILE = D_FF // N_DEV
FF_WIDE = 1024
NEG = -1e30

C_GV, C_GR, C_SQ, C_GQ, C_GK, C_SK, C_SV, C_LR = 0, 512, 1024, 1536, 1792, 2048, 2176, 2304
DINP = 2432
DIN = 2320
O_GQ, O_GK, O_GV, O_GR, O_LR, O_SQ, O_SK, O_SV = (0, 256), (256, 512), (512, 1024), (1024, 1536), (1536, 1552), (1552, 2064), (2064, 2192), (2192, 2320)

ADAM_LR, ADAM_B1, ADAM_B2, ADAM_EPS, ADAM_WD, ADAM_STEP = 0.001, 0.9, 0.999, 1e-08, 0.01, 10

MESH = pl.DeviceIdType.MESH
ANY = pl.BlockSpec(memory_space=pl.ANY)
HIGHEST = lax.Precision.HIGHEST


def _cp(sem=None, vmem_mb=None):
    kw = {}
    if sem is not None:
        kw["dimension_semantics"] = sem
    if vmem_mb is not None:
        kw["vmem_limit_bytes"] = vmem_mb << 20
    return pltpu.CompilerParams(**kw)


def _mm(a, b):
    return jnp.dot(a.astype(MXU_DTYPE), b.astype(MXU_DTYPE), preferred_element_type=F32)


def _mm_nt(a, b):
    return lax.dot_general(a.astype(MXU_DTYPE), b.astype(MXU_DTYPE), (((1,), (1,)), ((), ())),
                           preferred_element_type=F32)


def _mm_tn(a, b):
    return lax.dot_general(a.astype(MXU_DTYPE), b.astype(MXU_DTYPE), (((0,), (0,)), ((), ())),
                           preferred_element_type=F32)


def _logsigmoid(z):
    return jnp.minimum(z, 0.0) - jnp.log(1.0 + jnp.exp(-jnp.abs(z)))


def _sigmoid(z):
    return 1.0 / (1.0 + jnp.exp(-z))


def _row_tile(rows):
    return 640 if rows % 640 == 0 else 128


def _mesh_pos():
    return lax.axis_index("x"), lax.axis_index("y"), lax.axis_index("c")


def _all_gather(shards):
    n = len(shards)

    def body(*refs):
        start, forward, finish = _gather_schedule(refs[:n], refs[n:2 * n], *refs[2 * n:])
        start()
        for j in range(3):
            forward(j)
        finish()

    gathered = pl.pallas_call(
        body, name="all_gather_weights",
        out_shape=_gathered_shapes(shards), in_specs=[ANY] * n, out_specs=[ANY] * n,
        scratch_shapes=_gather_sems(n),
    )(*shards)
    return _with_own_block(gathered, shards)


def _gathered_shapes(shards):
    return [jax.ShapeDtypeStruct((N_DEV,) + s.shape, s.dtype) for s in shards]


def _gather_sems(n):
    return [pltpu.SemaphoreType.DMA((7 * n,)), pltpu.SemaphoreType.DMA((7 * n,))] if n else []


def _place_gather(step, steps, shard_refs, gathered_refs, sems):
    if not shard_refs:
        return
    start, forward, finish = _gather_schedule(shard_refs, gathered_refs, *sems)
    pl.when(step == 0)(start)
    for j, at in enumerate((steps * 7 // 10, steps * 8 // 10, steps * 9 // 10)):
        pl.when(step == at)(functools.partial(forward, j))
    pl.when(step == steps - 1)(finish)


def _with_own_block(gathered, shards):
    dev = 4 * lax.axis_index("x") + 2 * lax.axis_index("y") + lax.axis_index("c")
    return [lax.dynamic_update_index_in_dim(g, s, dev, 0) for g, s in zip(gathered, shards)]


def _gather_schedule(ins, outs, send_sems, recv_sems):
    n = len(ins)
    x, y, c = _mesh_pos()
    me, sibling = (x, y, c), (x, y, 1 - c)
    chips = [(1 - x, y), (x, 1 - y), (1 - x, 1 - y)]

    def copy(a, k, block, to, src=None):
        dst = outs[a].at[4 * block[0] + 2 * block[1] + block[2]]
        return pltpu.make_async_remote_copy(
            src_ref=dst if src is None else src, dst_ref=dst,
            send_sem=send_sems.at[a * 7 + k], recv_sem=recv_sems.at[a * 7 + k],
            device_id=to, device_id_type=MESH)

    def first(a):
        return [copy(a, 0, me, sibling, src=ins[a])] + [copy(a, 1 + j, me, (*chip, c), src=ins[a])
                                                        for j, chip in enumerate(chips)]

    def start():
        for a in range(n):
            for cp in first(a):
                cp.start()

    def forward(j):
        for a in range(n):
            copy(a, 1 + j, (*chips[j], c), me).wait_recv()
            copy(a, 4 + j, (*chips[j], c), sibling).start()

    def finish():
        for a in range(n):
            copy(a, 0, sibling, me).wait_recv()
            for j, chip in enumerate(chips):
                copy(a, 4 + j, (*chip, 1 - c), me).wait_recv()
        for a in range(n):
            for cp in first(a) + [copy(a, 4 + j, (*chip, c), sibling) for j, chip in enumerate(chips)]:
                cp.wait_send()

    return start, forward, finish


def _rs_sibling(gs):
    n = len(gs)

    def body(*refs):
        start, finish = _sibling_schedule(refs[:n], refs[n:2 * n], *refs[2 * n:])
        start()
        finish()

    return pl.pallas_call(
        body, name="reduce_scatter_sibling",
        out_shape=_sibling_shapes(gs), in_specs=[ANY] * n, out_specs=[ANY] * n,
        scratch_shapes=_sibling_sems(n),
    )(*gs)


def _sibling_shapes(gs):
    return [jax.ShapeDtypeStruct(g.shape[1:], g.dtype) for g in gs]


def _sibling_sems(n):
    return [pltpu.SemaphoreType.DMA((n,)), pltpu.SemaphoreType.DMA((n,))]


def _sibling_schedule(ins, land, send_sems, recv_sems):
    x, y, c = _mesh_pos()

    def copies():
        return [pltpu.make_async_remote_copy(
            src_ref=ins[a].at[1 - c], dst_ref=land[a], send_sem=send_sems.at[a], recv_sem=recv_sems.at[a],
            device_id=(x, y, 1 - c), device_id_type=MESH) for a in range(len(ins))]

    def start():
        for cp in copies():
            cp.start()

    def finish():
        for cp in copies():
            cp.wait_recv()
        for cp in copies():
            cp.wait_send()

    return start, finish


def _rs_chips(ps):
    n = len(ps)

    def body(*refs):
        start, finish = _chips_schedule(refs[:n], refs[n:2 * n], *refs[2 * n:])
        start()
        finish()

    return pl.pallas_call(
        body, name="reduce_scatter_chips",
        out_shape=_chips_shapes(ps), in_specs=[ANY] * n, out_specs=[ANY] * n,
        scratch_shapes=_chips_sems(n),
    )(*ps)


def _chips_shapes(ps):
    return [jax.ShapeDtypeStruct((3,) + p.shape[1:], p.dtype) for p in ps]


def _chips_sems(n):
    return [pltpu.SemaphoreType.DMA((3 * n,)), pltpu.SemaphoreType.DMA((3 * n,))]


def _chips_schedule(ins, land, send_sems, recv_sems):
    x, y, c = _mesh_pos()
    chips = [(1 - x, y), (x, 1 - y), (1 - x, 1 - y)]

    def copies():
        return [pltpu.make_async_remote_copy(
            src_ref=ins[a].at[2 * chip[0] + chip[1]], dst_ref=land[a].at[j],
            send_sem=send_sems.at[3 * a + j], recv_sem=recv_sems.at[3 * a + j],
            device_id=(*chip, c), device_id_type=MESH) for a in range(len(ins)) for j, chip in enumerate(chips)]

    def start():
        for cp in copies():
            cp.start()

    def finish():
        for cp in copies():
            cp.wait_recv()
        for cp in copies():
            cp.wait_send()

    return start, finish


class _Jobs:
    def __init__(self, jobs):
        self.jobs = jobs
        self.inputs = [a for _, arrs in jobs for a in arrs]
        self.out_shapes = [s for kind, arrs in jobs
                           for s in (_sibling_shapes(arrs) if kind == "sibling" else _chips_shapes(arrs))]
        self.sems = [s for kind, arrs in jobs
                     for s in (_sibling_sems(len(arrs)) if kind == "sibling" else _chips_sems(len(arrs)))]
        self.n = len(self.inputs)

    def bind(self, in_refs, out_refs, sem_refs):
        starts, finishes, at = [], [], 0
        for k, (kind, arrs) in enumerate(self.jobs):
            schedule = _sibling_schedule if kind == "sibling" else _chips_schedule
            start, finish = schedule(in_refs[at:at + len(arrs)], out_refs[at:at + len(arrs)],
                                     sem_refs[2 * k], sem_refs[2 * k + 1])
            starts.append(start)
            finishes.append(finish)
            at += len(arrs)

        def start_all():
            for f in starts:
                f()

        def finish_all():
            for f in finishes:
                f()

        return start_all, finish_all

    def split(self, outs):
        res, at = [], 0
        for _, arrs in self.jobs:
            res.append(list(outs[at:at + len(arrs)]))
            at += len(arrs)
        return res


R_META, R_WG, R_NORM_MIX, R_NORM_FF, R_FINAL, R_B_GATE, R_GLA_NORM, R_LOSS, R_SINKS, SMALL_ROWS = 0, 16, 32, 33, 34, 35, 36, 37, 40, 48


def _all_reduce_small(dlead, dwg, gnm, gnf, gfn, dbg, dgnw, loss, dsink):
    def body(dlead_ref, dwg_ref, gnm_ref, gnf_ref, gfn_ref, dbg_ref, dgnw_ref, loss_ref, dsink_ref,
             out_ref, p_ref, land, send_sems, recv_sems):
        x, y, c = _mesh_pos()
        me = 4 * x + 2 * y + c
        p_ref[...] = jnp.zeros_like(p_ref)
        p_ref[R_META:R_META + N_META, :] = dlead_ref[META0:LEAD, :]
        p_ref[R_WG:R_WG + GLA_RANK, 0:256] = dwg_ref[0:GLA_RANK, :]
        p_ref[R_NORM_MIX:R_NORM_MIX + 1, :] = gnm_ref[0:1, :]
        p_ref[R_NORM_FF:R_NORM_FF + 1, :] = gnf_ref[0:1, :]
        p_ref[R_FINAL:R_FINAL + 1, :] = gfn_ref[0:1, :]
        p_ref[R_B_GATE:R_B_GATE + 1, 0:256] = dbg_ref[0:1, :]
        p_ref[R_GLA_NORM:R_GLA_NORM + 1, 0:128] = dgnw_ref[0:1, :]
        p_ref[R_LOSS:R_LOSS + 1, 0:128] = loss_ref[0:1, :]
        p_ref[R_SINKS:R_SINKS + SWA_HEADS, 0:128] = dsink_ref[...]
        land[me] = p_ref[...]
        copies = []
        for k in range(1, N_DEV):
            bx, by, bc = (k >> 2) & 1, (k >> 1) & 1, k & 1
            peer = (1 - x if bx else x, 1 - y if by else y, 1 - c if bc else c)
            copies.append(pltpu.make_async_remote_copy(
                src_ref=p_ref, dst_ref=land.at[me], send_sem=send_sems.at[k - 1], recv_sem=recv_sems.at[k - 1],
                device_id=peer, device_id_type=MESH))
        for cp in copies:
            cp.start()
        for cp in copies:
            cp.wait_recv()
        for cp in copies:
            cp.wait_send()
        acc = land[0]
        for d in range(1, N_DEV):
            acc = acc + land[d]
        out_ref[...] = acc

    return pl.pallas_call(
        body, name="all_reduce_small",
        out_shape=jax.ShapeDtypeStruct((SMALL_ROWS, D), F32),
        in_specs=[pl.BlockSpec(memory_space=pltpu.VMEM)] * 9, out_specs=pl.BlockSpec(memory_space=pltpu.VMEM),
        scratch_shapes=[pltpu.VMEM((SMALL_ROWS, D), F32), pltpu.VMEM((N_DEV, SMALL_ROWS, D), F32),
                        pltpu.SemaphoreType.DMA((7,)), pltpu.SemaphoreType.DMA((7,))],
    )(dlead, dwg, gnm, gnf, gfn, dbg, dgnw, loss, dsink)


def _token_specs(tm, grid_rank=1):
    nb = tm // LEAD

    def spec(k):
        if grid_rank == 1:
            return pl.BlockSpec((LEAD, D), lambda i: (jnp.maximum(i * nb + k - 1, 0), 0))
        return pl.BlockSpec((LEAD, D), lambda i, j: (jnp.maximum(i * nb + k - 1, 0), 0))

    return [spec(k) for k in range(nb)]


def _h_tile(i, lead_ref, x_refs):
    first = jnp.where(i == 0, lead_ref[...], x_refs[0][...])
    return jnp.concatenate([first] + [r[...] for r in x_refs[1:]], axis=0)


def _in_proj(x, lead, nw, win_p, tm):
    rows = LEAD + x.shape[0]
    nb = tm // LEAD

    def body(*refs):
        x_refs, (lead_ref, nw_ref, w_ref, o_ref) = refs[:nb], refs[nb:]
        h = _h_tile(pl.program_id(0), lead_ref, x_refs)
        rstd = lax.rsqrt(jnp.mean(h * h, axis=-1, keepdims=True) + EPS)
        u = (h * rstd * nw_ref[...]).astype(MXU_DTYPE)
        o_ref[...] = jnp.dot(u, w_ref[...].astype(MXU_DTYPE), preferred_element_type=F32)

    return pl.pallas_call(
        body, name="in_proj", grid=(rows // tm,),
        in_specs=_token_specs(tm) + [pl.BlockSpec((LEAD, D), lambda i: (0, 0)), pl.BlockSpec((1, D), lambda i: (0, 0)),
                                     pl.BlockSpec((D, DINP), lambda i: (0, 0))],
        out_specs=pl.BlockSpec((tm, DINP), lambda i: (i, 0)),
        out_shape=jax.ShapeDtypeStruct((rows, DINP), F32),
        compiler_params=_cp(("arbitrary",), 56),
    )(*([x] * nb), lead, nw, win_p)


def _rope_tables(rows):
    pos = (jnp.arange(rows, dtype=jnp.int32) - META0).astype(F32)
    inv_freq = 1.0 / (ROPE_THETA ** (jnp.arange(0, ROPE_DIM, 2, dtype=F32) / ROPE_DIM))
    ang = pos[:, None] * jnp.tile(inv_freq, 128 // (ROPE_DIM // 2))[None, :]
    in_head = jnp.arange(128, dtype=jnp.int32)[None, :] % SWA_HD
    cos, sin = jnp.cos(ang), jnp.sin(ang)
    c_tab = jnp.where(in_head < ROPE_DIM, cos, 1.0)
    sa_tab = jnp.where(in_head < ROPE_DIM // 2, -sin, 0.0)
    sb_tab = jnp.where((in_head >= ROPE_DIM // 2) & (in_head < ROPE_DIM), sin, 0.0)
    return c_tab, sa_tab, sb_tab


def _rope(xv, cos, sa, sb):
    width = xv.shape[1]
    reps = width // 128
    if reps > 1:
        cos, sa, sb = (jnp.tile(t, (1, reps)) for t in (cos, sa, sb))
    return xv * cos + pltpu.roll(xv, width - 8, 1) * sa + pltpu.roll(xv, 8, 1) * sb


def _unrope(dy, cos, sa, sb):
    width = dy.shape[1]
    reps = width // 128
    if reps > 1:
        cos, sa, sb = (jnp.tile(t, (1, reps)) for t in (cos, sa, sb))
    return dy * cos + pltpu.roll(dy * sa, 8, 1) + pltpu.roll(dy * sb, width - 8, 1)


def _swa_prep(proj, tabs, tm):
    rows = proj.shape[0]

    def body(q_ref, k_ref, v_ref, c_ref, sa_ref, sb_ref, qo_ref, ko_ref, vo_ref):
        cos, sa, sb = c_ref[...], sa_ref[...], sb_ref[...]
        qo_ref[...] = (_rope(q_ref[...], cos, sa, sb) * (SWA_HD ** -0.5)).astype(ACT_DTYPE)
        ko_ref[...] = _rope(k_ref[...], cos, sa, sb).astype(ACT_DTYPE)
        vo_ref[...] = v_ref[...].astype(ACT_DTYPE)

    tab_spec = pl.BlockSpec((tm, 128), lambda i: (i, 0))
    return pl.pallas_call(
        body, name="swa_prep", grid=(rows // tm,),
        in_specs=[pl.BlockSpec((tm, 512), lambda i: (i, C_SQ // 512)),
                  pl.BlockSpec((tm, 128), lambda i: (i, C_SK // 128)),
                  pl.BlockSpec((tm, 128), lambda i: (i, C_SV // 128)), tab_spec, tab_spec, tab_spec],
        out_specs=[pl.BlockSpec((tm, 512), lambda i: (i, 0)), tab_spec, tab_spec],
        out_shape=[jax.ShapeDtypeStruct((rows, 512), ACT_DTYPE), jax.ShapeDtypeStruct((rows, 128), ACT_DTYPE),
                   jax.ShapeDtypeStruct((rows, 128), ACT_DTYPE)],
        compiler_params=_cp(("arbitrary",)),
    )(proj, proj, proj, *tabs)


def _gla_group(nc):
    for g in (5, 2):
        if nc % g == 0:
            return g
    return 1


def _gla_gates(lr, wg, bg, first_row, nrows):
    zg = _mm(lr, wg) + bg
    row = first_row + lax.broadcasted_iota(jnp.int32, (nrows, 1), 0)
    live = row >= META0
    g = jnp.where(live, _logsigmoid(zg) * (1.0 / GLA_TAU), 0.0)
    ii = lax.broadcasted_iota(jnp.int32, (nrows, nrows), 0)
    jj = lax.broadcasted_iota(jnp.int32, (nrows, nrows), 1)
    same = (ii // GLA_CHUNK) == (jj // GLA_CHUNK)
    lower, upper = same & (jj <= ii), same & (jj >= ii)
    b = jnp.dot(lower.astype(F32), g, precision=HIGHEST, preferred_element_type=F32)
    return zg, live, lower, upper, b


def _tril64():
    ii = lax.broadcasted_iota(jnp.int32, (GLA_CHUNK, GLA_CHUNK), 0)
    jj = lax.broadcasted_iota(jnp.int32, (GLA_CHUNK, GLA_CHUNK), 1)
    return jj <= ii


def _gla_fwd(proj, wg_p, bg, gnw, shards):
    rows = proj.shape[0]
    nc = rows // GLA_CHUNK
    group = _gla_group(nc)
    steps, nrows = nc // group, group * GLA_CHUNK
    ns = len(shards)

    def body(q_ref, k_ref, v_ref, r_ref, lr_ref, wg_ref, bg_ref, gnw_ref, *rest):
        shard_refs, rest = rest[:ns], rest[ns:]
        oraw_ref, og_ref, st_ref = rest[:3]
        gathered_refs, rest = rest[3:3 + ns], rest[3 + ns:]
        state = rest[0]
        c = pl.program_id(0)

        @pl.when(c == 0)
        def _():
            state[...] = jnp.zeros_like(state)

        _place_gather(c, steps, shard_refs, gathered_refs, rest[1:])
        _, _, _, _, b = _gla_gates(lr_ref[...], wg_ref[...], bg_ref[...], c * nrows, nrows)
        eb = jnp.exp(b)
        gq = q_ref[...] * (GLA_DK ** -0.5) * eb
        gk = k_ref[...] * jnp.exp(-b)
        v = v_ref[...]
        gnw_v = gnw_ref[...]
        tril = _tril64()
        for h in range(GLA_HEADS):
            s64 = slice(h * GLA_DK, (h + 1) * GLA_DK)
            s128 = slice(h * GLA_DV, (h + 1) * GLA_DV)
            st = state[h]
            for gi in range(group):
                rs = slice(gi * GLA_CHUNK, (gi + 1) * GLA_CHUNK)
                qh, kh, vh = gq[rs, s64], gk[rs, s64], v[rs, s128]
                eblh = eb[(gi + 1) * GLA_CHUNK - 1:(gi + 1) * GLA_CHUNK, s64]
                st_ref[gi, h] = st
                a = jnp.where(tril, _mm_nt(qh, kh), 0.0)
                o = _mm(a, vh) + _mm_nt(qh, st)
                st = st * eblh + _mm_tn(vh, kh * eblh)
                oraw_ref[rs, s128] = o
                rstd = lax.rsqrt(jnp.mean(o * o, axis=-1, keepdims=True) + EPS)
                rh = r_ref[rs, s128]
                og_ref[rs, s128] = (o * rstd * gnw_v * (rh * _sigmoid(rh))).astype(ACT_DTYPE)
            state[h] = st

    nb = lambda w, col: pl.BlockSpec((nrows, w), lambda c: (c, col // w))
    const = lambda shape: pl.BlockSpec(shape, lambda c: (0,) * len(shape))
    outs = pl.pallas_call(
        body, name="gla_fwd", grid=(steps,),
        in_specs=[nb(256, C_GQ), nb(256, C_GK), nb(512, C_GV), nb(512, C_GR), nb(128, C_LR),
                  const((128, 256)), const((1, 256)), const((1, 128))] + [ANY] * ns,
        out_specs=[pl.BlockSpec((nrows, 512), lambda c: (c, 0)), pl.BlockSpec((nrows, 512), lambda c: (c, 0)),
                   pl.BlockSpec((group, GLA_HEADS, GLA_DV, GLA_DK), lambda c: (c, 0, 0, 0))] + [ANY] * ns,
        out_shape=[jax.ShapeDtypeStruct((rows, 512), F32), jax.ShapeDtypeStruct((rows, 512), ACT_DTYPE),
                   jax.ShapeDtypeStruct((nc, GLA_HEADS, GLA_DV, GLA_DK), F32)] + _gathered_shapes(shards),
        scratch_shapes=[pltpu.VMEM((GLA_HEADS, GLA_DV, GLA_DK), F32)] + _gather_sems(ns),
        compiler_params=_cp(("arbitrary",)),
    )(proj, proj, proj, proj, proj, wg_p, bg, gnw, *shards)
    return outs[0], outs[1], outs[2], _with_own_block(outs[3:], shards)


def _swa_mask(n):
    shape = (SWA_GROUP * SWA_BLOCK, 3 * SWA_BLOCK)
    qi = lax.broadcasted_iota(jnp.int32, shape, 0) & (SWA_BLOCK - 1)
    jj = lax.broadcasted_iota(jnp.int32, shape, 1)
    meta = (jj < SWA_BLOCK) & (jj >= META0) & ((n > 0) | (jj <= qi))
    prev = (jj >= SWA_BLOCK) & (jj < 2 * SWA_BLOCK) & (n >= 2) & (jj - SWA_BLOCK > qi)
    cur = (jj >= 2 * SWA_BLOCK) & (n >= 1) & (jj - 2 * SWA_BLOCK <= qi)
    return meta | prev | cur


def _stack_heads(t, kvh):
    return jnp.concatenate([t[:, (kvh * SWA_GROUP + g) * SWA_HD:(kvh * SWA_GROUP + g + 1) * SWA_HD]
                            for g in range(SWA_GROUP)], axis=0)


def _stack_sinks(sink_ref, kvh):
    return jnp.concatenate([jnp.full((SWA_BLOCK, 1), sink_ref[0, kvh * SWA_GROUP + g], F32)
                            for g in range(SWA_GROUP)], axis=0)


def _swa_group(nblk):
    return 5 if nblk % 5 == 0 else 1


def _swa_specs(group):
    blk = lambda w: pl.BlockSpec((group * SWA_BLOCK, w), lambda n: (n, 0))
    first = pl.BlockSpec((SWA_BLOCK, 128), lambda n: (0, 0))
    prev = pl.BlockSpec((SWA_BLOCK, 128), lambda n: (jnp.maximum(n * group - 1, 0), 0))
    return blk, first, prev


def _swa_keys(first_ref, prev_ref, cur_ref, g):
    own = cur_ref[g * SWA_BLOCK:(g + 1) * SWA_BLOCK, :]
    before = prev_ref[...] if g == 0 else cur_ref[(g - 1) * SWA_BLOCK:g * SWA_BLOCK, :]
    return jnp.concatenate([first_ref[...], before, own], axis=0)


def _swa_fwd(qr, kr, vr, sinks, shards):
    rows = qr.shape[0]
    nblk = rows // SWA_BLOCK
    group = _swa_group(nblk)
    steps = nblk // group
    ns = len(shards)

    def body(q_ref, k0, kp, kc, v0, vp, vc, sink_ref, *rest):
        o_ref = rest[ns]
        _place_gather(pl.program_id(0), steps, rest[:ns], rest[ns + 1:2 * ns + 1], rest[2 * ns + 1:])
        for g in range(group):
            n = pl.program_id(0) * group + g
            rs = slice(g * SWA_BLOCK, (g + 1) * SWA_BLOCK)
            kall, vall = _swa_keys(k0, kp, kc, g), _swa_keys(v0, vp, vc, g)
            mask = _swa_mask(n)[0:SWA_BLOCK]
            for head in range(SWA_HEADS):
                hs = slice(head * SWA_HD, (head + 1) * SWA_HD)
                kv = slice((head // SWA_GROUP) * SWA_HD, (head // SWA_GROUP + 1) * SWA_HD)
                s = jnp.where(mask, _mm_nt(q_ref[rs, hs], kall[:, kv]), NEG)
                sink = sink_ref[0, head]
                m = jnp.maximum(jnp.max(s, axis=-1, keepdims=True), sink)
                p = jnp.exp(s - m)
                den = jnp.sum(p, axis=-1, keepdims=True) + jnp.exp(sink - m)
                o_ref[rs, hs] = (_mm(p, vall[:, kv]) / den).astype(ACT_DTYPE)

    blk, first, prev = _swa_specs(group)
    outs = pl.pallas_call(
        body, name="swa_fwd", grid=(steps,),
        in_specs=[blk(512), first, prev, blk(128), first, prev, blk(128),
                  pl.BlockSpec(memory_space=pltpu.SMEM)] + [ANY] * ns,
        out_specs=[blk(512)] + [ANY] * ns,
        out_shape=[jax.ShapeDtypeStruct((rows, 512), ACT_DTYPE)] + _gathered_shapes(shards),
        scratch_shapes=_gather_sems(ns),
        compiler_params=_cp(("arbitrary",)),
    )(qr, kr, kr, kr, vr, vr, vr, sinks, *shards)
    return outs[0], _with_own_block(outs[1:], shards)


def _out_proj(x, lead, og, osw, wout, nfw, tm):
    rows = LEAD + x.shape[0]
    nb = tm // LEAD

    def body(*refs):
        x_refs, (lead_ref, og_ref, os_ref, w_ref, nw_ref, h1_ref, f_ref, ft_ref) = refs[:nb], refs[nb:]
        h0 = _h_tile(pl.program_id(0), lead_ref, x_refs)
        h1 = h0 + _mm(og_ref[...], w_ref[0:512, :]) + _mm(os_ref[...], w_ref[512:1024, :])
        h1_ref[...] = h1
        rstd = lax.rsqrt(jnp.mean(h1 * h1, axis=-1, keepdims=True) + EPS)
        f = h1 * rstd * nw_ref[...]
        f_ref[...] = f.astype(ACT_DTYPE)
        ft_ref[...] = f.T.astype(ACT_DTYPE)

    row = lambda w: pl.BlockSpec((tm, w), lambda i: (i, 0))
    return pl.pallas_call(
        body, name="out_proj", grid=(rows // tm,),
        in_specs=_token_specs(tm) + [pl.BlockSpec((LEAD, D), lambda i: (0, 0)), row(512), row(512),
                                     pl.BlockSpec((D, D), lambda i: (0, 0)), pl.BlockSpec((1, D), lambda i: (0, 0))],
        out_specs=[row(D), row(D), pl.BlockSpec((D, tm), lambda i: (0, i))],
        out_shape=[jax.ShapeDtypeStruct((rows, D), F32), jax.ShapeDtypeStruct((rows, D), ACT_DTYPE),
                   jax.ShapeDtypeStruct((D, rows), ACT_DTYPE)],
        compiler_params=_cp(("arbitrary",), 48),
    )(*([x] * nb), lead, og, osw, wout, nfw)


def _ffn_fwd(f, h1, w1, w2, tgt, fnw, tm):
    rows = f.shape[0]
    nj = D_FF // FF_WIDE
    nb = tm // LEAD

    def body(f_ref, h1_ref, w1_ref, w2_ref, nw_ref, *rest):
        t_refs, (a_ref, dh2_ref, dh2t_ref, loss_ref, gfn_ref, acc) = rest[:nb], rest[nb:]
        i, j = pl.program_id(0), pl.program_id(1)

        @pl.when((i == 0) & (j == 0))
        def _():
            loss_ref[...] = jnp.zeros_like(loss_ref)
            gfn_ref[...] = jnp.zeros_like(gfn_ref)

        @pl.when(j == 0)
        def _():
            acc[...] = jnp.zeros_like(acc)

        a = _mm(f_ref[...], w1_ref[...])
        a_ref[...] = a.astype(ACT_DTYPE)
        z = jnp.square(jnp.maximum(a, 0.0))
        acc[...] += _mm(z, w2_ref[...])

        @pl.when(j == nj - 1)
        def _():
            h2 = h1_ref[...] + acc[...]
            rstd = lax.rsqrt(jnp.mean(h2 * h2, axis=-1, keepdims=True) + EPS)
            hn = h2 * rstd
            nw = nw_ref[...]
            row = i * tm + lax.broadcasted_iota(jnp.int32, (tm, 1), 0)
            target = jnp.concatenate([t[...] for t in t_refs], axis=0)
            err = jnp.where(row >= LEAD, hn * nw - target, 0.0)
            row_loss = jnp.sum(err * err, axis=-1, keepdims=True) * (1.0 / D)
            loss_ref[...] += jnp.broadcast_to(0.5 * jnp.sum(row_loss, axis=0, keepdims=True), loss_ref.shape)
            dy = err * (1.0 / D)
            gfn_ref[...] += jnp.broadcast_to(jnp.sum(dy * hn, axis=0, keepdims=True), gfn_ref.shape)
            dhn = dy * nw
            dh2 = rstd * (dhn - hn * jnp.mean(dhn * hn, axis=-1, keepdims=True))
            dh2_ref[...] = dh2
            dh2t_ref[...] = dh2.T.astype(ACT_DTYPE)

    return pl.pallas_call(
        body, name="ffn_fwd", grid=(rows // tm, nj),
        in_specs=[pl.BlockSpec((tm, D), lambda i, j: (i, 0)), pl.BlockSpec((tm, D), lambda i, j: (i, 0)),
                  pl.BlockSpec((D, FF_WIDE), lambda i, j: (0, j)),
                  pl.BlockSpec((FF_WIDE, D), lambda i, j: (j, 0)),
                  pl.BlockSpec((1, D), lambda i, j: (0, 0))] + _token_specs(tm, grid_rank=2),
        out_specs=[pl.BlockSpec((tm, FF_WIDE), lambda i, j: (i, j)), pl.BlockSpec((tm, D), lambda i, j: (i, 0)),
                   pl.BlockSpec((D, tm), lambda i, j: (0, i)),
                   pl.BlockSpec((8, 128), lambda i, j: (0, 0)), pl.BlockSpec((8, D), lambda i, j: (0, 0))],
        out_shape=[jax.ShapeDtypeStruct((rows, D_FF), ACT_DTYPE), jax.ShapeDtypeStruct((rows, D), F32),
                   jax.ShapeDtypeStruct((D, rows), ACT_DTYPE),
                   jax.ShapeDtypeStruct((8, 128), F32), jax.ShapeDtypeStruct((8, D), F32)],
        scratch_shapes=[pltpu.VMEM((tm, D), F32)],
        compiler_params=_cp(("arbitrary", "arbitrary"), 56),
    )(f, h1, w1, w2, fnw, *([tgt] * nb))


def _ffn_bwd_act(dh2, a, w1, w2, h1, nfw, tm):
    rows = dh2.shape[0]
    nj = D_FF // FF_WIDE

    def body(dh2_ref, a_ref, w1_ref, w2_ref, h1_ref, nw_ref, da_ref, dh1_ref, gnf_ref, acc):
        i, j = pl.program_id(0), pl.program_id(1)

        @pl.when((i == 0) & (j == 0))
        def _():
            gnf_ref[...] = jnp.zeros_like(gnf_ref)

        @pl.when(j == 0)
        def _():
            acc[...] = jnp.zeros_like(acc)

        dz = _mm_nt(dh2_ref[...], w2_ref[...])
        da = dz * (2.0 * jnp.maximum(a_ref[...].astype(F32), 0.0))
        da_ref[...] = da.astype(ACT_DTYPE)
        acc[...] += _mm_nt(da, w1_ref[...])

        @pl.when(j == nj - 1)
        def _():
            h1 = h1_ref[...]
            rstd = lax.rsqrt(jnp.mean(h1 * h1, axis=-1, keepdims=True) + EPS)
            hn = h1 * rstd
            df = acc[...]
            gnf_ref[...] += jnp.broadcast_to(jnp.sum(df * hn, axis=0, keepdims=True), gnf_ref.shape)
            dfn = df * nw_ref[...]
            dh1_ref[...] = dh2_ref[...] + rstd * (dfn - hn * jnp.mean(dfn * hn, axis=-1, keepdims=True))

    return pl.pallas_call(
        body, name="ffn_bwd_act", grid=(rows // tm, nj),
        in_specs=[pl.BlockSpec((tm, D), lambda i, j: (i, 0)), pl.BlockSpec((tm, FF_WIDE), lambda i, j: (i, j)),
                  pl.BlockSpec((D, FF_WIDE), lambda i, j: (0, j)),
                  pl.BlockSpec((FF_WIDE, D), lambda i, j: (j, 0)),
                  pl.BlockSpec((tm, D), lambda i, j: (i, 0)), pl.BlockSpec((1, D), lambda i, j: (0, 0))],
        out_specs=[pl.BlockSpec((tm, FF_WIDE), lambda i, j: (i, j)), pl.BlockSpec((tm, D), lambda i, j: (i, 0)),
                   pl.BlockSpec((8, D), lambda i, j: (0, 0))],
        out_shape=[jax.ShapeDtypeStruct((rows, D_FF), ACT_DTYPE), jax.ShapeDtypeStruct((rows, D), F32),
                   jax.ShapeDtypeStruct((8, D), F32)],
        scratch_shapes=[pltpu.VMEM((tm, D), F32)],
        compiler_params=_cp(("arbitrary", "arbitrary"), 56),
    )(dh2, a, w1, w2, h1, nfw)


def _ffn_bwd_weights(ft, a, da, dh2t, tm):
    rows = a.shape[0]
    steps = rows // tm

    def body(ft_ref, a_ref, da_ref, dh2t_ref, dw1_ref, dw2_ref, dw2t):
        i = pl.program_id(1)

        @pl.when(i == 0)
        def _():
            dw1_ref[...] = jnp.zeros_like(dw1_ref)
            dw2t[...] = jnp.zeros_like(dw2t)

        z = jnp.square(jnp.maximum(a_ref[...].astype(F32), 0.0))
        dw1_ref[...] += _mm(ft_ref[...], da_ref[...])
        dw2t[...] += _mm(dh2t_ref[...], z)

        @pl.when(i == steps - 1)
        def _():
            dw2_ref[...] = dw2t[...].T

    return pl.pallas_call(
        body, name="ffn_bwd_weights", grid=(N_DEV, steps),
        in_specs=[pl.BlockSpec((D, tm), lambda j, i: (0, i)), pl.BlockSpec((tm, FF_TILE), lambda j, i: (i, j)),
                  pl.BlockSpec((tm, FF_TILE), lambda j, i: (i, j)), pl.BlockSpec((D, tm), lambda j, i: (0, i))],
        out_specs=[pl.BlockSpec((None, None, D, FF_TILE), lambda j, i: (j % 2, j // 2, 0, 0)),
                   pl.BlockSpec((None, None, FF_TILE, D), lambda j, i: (j % 2, j // 2, 0, 0))],
        out_shape=[jax.ShapeDtypeStruct((2, 4, D, FF_TILE), F32), jax.ShapeDtypeStruct((2, 4, FF_TILE, D), F32)],
        scratch_shapes=[pltpu.VMEM((D, FF_TILE), F32)],
        compiler_params=_cp(("arbitrary", "arbitrary"), 48),
    )(ft, a, da, dh2t)


def _out_proj_bwd(dh1, og, osw, wout, tm, partials):
    rows = dh1.shape[0]
    steps = rows // tm
    ns = len(partials)

    def body(dh1_ref, og_ref, os_ref, w_ref, *rest):
        part_refs, rest = rest[:ns], rest[ns:]
        dog_ref, dos_ref, dw_ref = rest[:3]
        land_refs, (send_sems, recv_sems) = rest[3:3 + ns], rest[3 + ns:]
        i = pl.program_id(0)
        start, finish = _sibling_schedule(part_refs, land_refs, send_sems, recv_sems)

        @pl.when(i == 0)
        def _():
            dw_ref[...] = jnp.zeros_like(dw_ref)
            start()

        pl.when(i == steps - 1)(finish)

        dh1 = dh1_ref[...].astype(MXU_DTYPE)
        dog_ref[...] = _mm_nt(dh1, w_ref[0:512, :])
        dos_ref[...] = _mm_nt(dh1, w_ref[512:1024, :])
        for half, ref in enumerate((og_ref, os_ref)):
            dw = _mm_tn(ref[...], dh1)
            for blk in range(4):
                shard = half * 4 + blk
                dw_ref[shard % 2, shard // 2] += dw[blk * 128:(blk + 1) * 128, :]

    row = lambda w: pl.BlockSpec((tm, w), lambda i: (i, 0))
    outs = pl.pallas_call(
        body, name="out_proj_bwd", grid=(steps,),
        in_specs=[row(D), row(512), row(512), pl.BlockSpec((D, D), lambda i: (0, 0))] + [ANY] * ns,
        out_specs=[row(512), row(512), pl.BlockSpec((2, 4, 128, D), lambda i: (0, 0, 0, 0))] + [ANY] * ns,
        out_shape=[jax.ShapeDtypeStruct((rows, 512), F32), jax.ShapeDtypeStruct((rows, 512), F32),
                   jax.ShapeDtypeStruct((2, 4, 128, D), F32)] + _sibling_shapes(partials),
        scratch_shapes=_sibling_sems(ns),
        compiler_params=_cp(("arbitrary",), 48),
    )(dh1, og, osw, wout, *partials)
    return outs[0], outs[1], outs[2], outs[3:]


def _swa_bwd(qr, kr, vr, osw, dos, sinks, jobs):
    rows = qr.shape[0]
    nblk = rows // SWA_BLOCK
    group = _swa_group(nblk)
    steps = nblk // group
    ns = jobs.n

    def body(q_ref, k0, kp, kc, v0, vp, vc, o_ref, do_ref, sink_ref, *rest):
        dq_ref, dk_ref, dv_ref, dsink_ref = rest[ns:ns + 4]
        start, finish = jobs.bind(rest[:ns], rest[ns + 4:2 * ns + 4], rest[2 * ns + 4:])
        step = pl.program_id(0)

        @pl.when(step == 0)
        def _():
            dk_ref[...] = jnp.zeros_like(dk_ref)
            dv_ref[...] = jnp.zeros_like(dv_ref)
            dsink_ref[...] = jnp.zeros_like(dsink_ref)
            start()

        pl.when(step == steps - 1)(finish)
        for g in range(group):
            block(step * group + g, g, q_ref, k0, kp, kc, v0, vp, vc, o_ref, do_ref, sink_ref,
                  dq_ref, dk_ref, dv_ref, dsink_ref)

    def block(n, g, q_ref, k0, kp, kc, v0, vp, vc, o_ref, do_ref, sink_ref, dq_ref, dk_ref, dv_ref, dsink_ref):
        rs = slice(g * SWA_BLOCK, (g + 1) * SWA_BLOCK)
        q = q_ref[rs, :]
        kall, vall = _swa_keys(k0, kp, kc, g), _swa_keys(v0, vp, vc, g)
        mask = _swa_mask(n)
        do_all = do_ref[rs, :]
        o_all = o_ref[rs, :].astype(F32)
        dq, dk, dv = [], [], []
        for kvh in range(SWA_KV):
            kv = slice(kvh * SWA_HD, (kvh + 1) * SWA_HD)
            q4, do4, o4 = _stack_heads(q, kvh), _stack_heads(do_all, kvh), _stack_heads(o_all, kvh)
            sink4 = _stack_sinks(sink_ref, kvh)
            s = jnp.where(mask, _mm_nt(q4, kall[:, kv]), NEG)
            m = jnp.maximum(jnp.max(s, axis=-1, keepdims=True), sink4)
            e = jnp.exp(s - m)
            inv = 1.0 / (jnp.sum(e, axis=-1, keepdims=True) + jnp.exp(sink4 - m))
            p = e * inv
            delta = jnp.sum(do4 * o4, axis=-1, keepdims=True)
            ds = p * (_mm_nt(do4, vall[:, kv]) - delta)
            dq4 = _mm(ds, kall[:, kv])
            dq += [dq4[g * SWA_BLOCK:(g + 1) * SWA_BLOCK] for g in range(SWA_GROUP)]
            dk.append(_mm_tn(ds, q4))
            dv.append(_mm_tn(p, do4))
            sink_term = jnp.exp(sink4 - m) * inv * delta
            for g in range(SWA_GROUP):
                head = kvh * SWA_GROUP + g
                dsink = -jnp.sum(sink_term[g * SWA_BLOCK:(g + 1) * SWA_BLOCK], axis=0, keepdims=True)
                dsink_ref[head:head + 1, :] += jnp.broadcast_to(dsink, (1, 128))
        dq_ref[rs, :] = jnp.concatenate(dq, axis=1)
        dk_all = jnp.concatenate(dk, axis=1)
        dv_all = jnp.concatenate(dv, axis=1)
        prev0 = pl.multiple_of(jnp.maximum(n - 1, 0) * SWA_BLOCK, SWA_BLOCK)
        cur0 = pl.multiple_of(n * SWA_BLOCK, SWA_BLOCK)
        for ref, val in ((dk_ref, dk_all), (dv_ref, dv_all)):
            ref[0:SWA_BLOCK, :] += val[0:SWA_BLOCK]
            ref[pl.ds(prev0, SWA_BLOCK), :] += val[SWA_BLOCK:2 * SWA_BLOCK]
            ref[pl.ds(cur0, SWA_BLOCK), :] += val[2 * SWA_BLOCK:]

    blk, first, prev = _swa_specs(group)
    whole = pl.BlockSpec((rows, 128), lambda n: (0, 0))
    outs = pl.pallas_call(
        body, name="swa_bwd", grid=(steps,),
        in_specs=[blk(512), first, prev, blk(128), first, prev, blk(128), blk(512), blk(512),
                  pl.BlockSpec(memory_space=pltpu.SMEM)] + [ANY] * ns,
        out_specs=[blk(512), whole, whole, pl.BlockSpec((8, 128), lambda n: (0, 0))] + [ANY] * ns,
        out_shape=[jax.ShapeDtypeStruct((rows, 512), F32), jax.ShapeDtypeStruct((rows, 128), F32),
                   jax.ShapeDtypeStruct((rows, 128), F32), jax.ShapeDtypeStruct((8, 128), F32)] + jobs.out_shapes,
        scratch_shapes=jobs.sems,
        compiler_params=_cp(("arbitrary",), 48),
    )(qr, kr, kr, kr, vr, vr, vr, osw, dos, sinks, *jobs.inputs)
    return outs[0], outs[1], outs[2], outs[3], jobs.split(outs[4:])


def _gla_bwd(proj, oraw, states, dog, wg_p, bg, gnw, jobs):
    rows = proj.shape[0]
    nc = rows // GLA_CHUNK
    group = _gla_group(nc)
    steps, nrows = nc // group, group * GLA_CHUNK
    ns = jobs.n

    def body(q_ref, k_ref, v_ref, r_ref, lr_ref, oraw_ref, st_ref, dog_ref, wg_ref, bg_ref, gnw_ref, *rest):
        dq_ref, dk_ref, dv_ref, dr_ref, dlr_ref, dwg_ref, dbg_ref, dgnw_ref = rest[ns:ns + 8]
        dstate, db_scr = rest[2 * ns + 8:2 * ns + 10]
        start, finish = jobs.bind(rest[:ns], rest[ns + 8:2 * ns + 8], rest[2 * ns + 10:])
        t = pl.program_id(0)
        c = steps - 1 - t

        @pl.when(t == 0)
        def _():
            dstate[...] = jnp.zeros_like(dstate)
            dwg_ref[...] = jnp.zeros_like(dwg_ref)
            dbg_ref[...] = jnp.zeros_like(dbg_ref)
            dgnw_ref[...] = jnp.zeros_like(dgnw_ref)
            start()

        pl.when(t == steps - 1)(finish)

        lr, wg = lr_ref[...], wg_ref[...]
        zg, live, _, upper, b = _gla_gates(lr, wg, bg_ref[...], c * nrows, nrows)
        eb, enb = jnp.exp(b), jnp.exp(-b)
        scale = GLA_DK ** -0.5
        gq = q_ref[...] * scale * eb
        gk = k_ref[...] * enb
        v = v_ref[...]
        gnw_v = gnw_ref[...]
        tril = _tril64()
        is_last = lax.broadcasted_iota(jnp.int32, (GLA_CHUNK, 1), 0) == GLA_CHUNK - 1
        dgnw = jnp.zeros((1, GLA_DV), F32)
        for h in range(GLA_HEADS):
            s64 = slice(h * GLA_DK, (h + 1) * GLA_DK)
            s128 = slice(h * GLA_DV, (h + 1) * GLA_DV)
            dsp = dstate[h]
            for gi in reversed(range(group)):
                rs = slice(gi * GLA_CHUNK, (gi + 1) * GLA_CHUNK)
                qh, kh, vh = gq[rs, s64], gk[rs, s64], v[rs, s128]
                ebh, enbh = eb[rs, s64], enb[rs, s64]
                eblh = eb[(gi + 1) * GLA_CHUNK - 1:(gi + 1) * GLA_CHUNK, s64]
                klh = kh * eblh
                st = st_ref[gi, h]
                o, rh, dout = oraw_ref[rs, s128], r_ref[rs, s128], dog_ref[rs, s128]
                rstd = lax.rsqrt(jnp.mean(o * o, axis=-1, keepdims=True) + EPS)
                on = o * rstd
                sg = _sigmoid(rh)
                dr_ref[rs, s128] = (dout * (on * gnw_v) * (sg * (1.0 + rh * (1.0 - sg)))).astype(ACT_DTYPE)
                dy = dout * (rh * sg)
                dgnw = dgnw + jnp.sum(dy * on, axis=0, keepdims=True)
                don = dy * gnw_v
                do = rstd * (don - on * jnp.mean(don * on, axis=-1, keepdims=True))
                a = jnp.where(tril, _mm_nt(qh, kh), 0.0)
                da = jnp.where(tril, _mm_nt(do, vh), 0.0)
                dkl = _mm(vh, dsp)
                dv_ref[rs, s128] = (_mm_tn(a, do) + _mm_nt(klh, dsp)).astype(ACT_DTYPE)
                debl = jnp.sum(dsp * st, axis=0, keepdims=True)
                dgq = _mm(da, kh) + _mm(do, st)
                dgk = _mm_tn(da, qh)
                dsp = dsp * eblh + _mm_tn(do, qh)
                dq_ref[rs, s64] = (dgq * (scale * ebh)).astype(ACT_DTYPE)
                dk_ref[rs, s64] = ((dgk + dkl * eblh) * enbh).astype(ACT_DTYPE)
                last = debl * eblh + jnp.sum(dkl * klh, axis=0, keepdims=True)
                db_scr[rs, s64] = dgq * qh - dgk * kh - dkl * klh + jnp.where(is_last, last, 0.0)
            dstate[h] = dsp
        dg = jnp.dot(upper.astype(F32), db_scr[...], precision=HIGHEST, preferred_element_type=F32)
        dzg = jnp.where(live, dg * _sigmoid(-zg) * (1.0 / GLA_TAU), 0.0)
        dlr_ref[...] = _mm_nt(dzg, wg).astype(ACT_DTYPE)
        dwg_ref[...] += _mm_tn(lr, dzg)
        dbg_ref[...] += jnp.broadcast_to(jnp.sum(dzg, axis=0, keepdims=True), dbg_ref.shape)
        dgnw_ref[...] += jnp.broadcast_to(dgnw, dgnw_ref.shape)

    nb = lambda w, col: pl.BlockSpec((nrows, w), lambda t: (steps - 1 - t, col // w))
    const = lambda shape: pl.BlockSpec(shape, lambda t: (0,) * len(shape))
    outs = pl.pallas_call(
        body, name="gla_bwd", grid=(steps,),
        in_specs=[nb(256, C_GQ), nb(256, C_GK), nb(512, C_GV), nb(512, C_GR), nb(128, C_LR), nb(512, 0),
                  pl.BlockSpec((group, GLA_HEADS, GLA_DV, GLA_DK), lambda t: (steps - 1 - t, 0, 0, 0)), nb(512, 0),
                  const((128, 256)), const((1, 256)), const((1, 128))] + [ANY] * ns,
        out_specs=[nb(256, 0), nb(256, 0), nb(512, 0), nb(512, 0), nb(128, 0),
                   const((128, 256)), const((8, 256)), const((8, 128))] + [ANY] * ns,
        out_shape=[jax.ShapeDtypeStruct((rows, 256), ACT_DTYPE), jax.ShapeDtypeStruct((rows, 256), ACT_DTYPE),
                   jax.ShapeDtypeStruct((rows, 512), ACT_DTYPE), jax.ShapeDtypeStruct((rows, 512), ACT_DTYPE),
                   jax.ShapeDtypeStruct((rows, 128), ACT_DTYPE), jax.ShapeDtypeStruct((128, 256), F32),
                   jax.ShapeDtypeStruct((8, 256), F32), jax.ShapeDtypeStruct((8, 128), F32)] + jobs.out_shapes,
        scratch_shapes=[pltpu.VMEM((GLA_HEADS, GLA_DV, GLA_DK), F32), pltpu.VMEM((nrows, 256), F32)] + jobs.sems,
        compiler_params=_cp(("arbitrary",)),
    )(proj, proj, proj, proj, proj, oraw, states, dog, wg_p, bg, gnw, *jobs.inputs)
    return outs[:8], jobs.split(outs[8:])


def _in_proj_bwd(x, lead, dh1, nw, win_p, dgv, dgr, dsq, dgq, dgk, dsk, dsv, dlr, tabs, tm):
    seq = x.shape[0]
    rows = LEAD + seq
    nb = tm // LEAD
    steps = rows // tm

    def first_copy(scr, gx_ref, sem):
        return pltpu.make_async_copy(scr.at[pl.ds(LEAD, tm - LEAD)], gx_ref.at[pl.ds(0, tm - LEAD)], sem)

    def tile_copy(scr, gx_ref, sem, step):
        start = pl.multiple_of(jnp.maximum(step * tm - LEAD, 0), LEAD)
        return pltpu.make_async_copy(scr, gx_ref.at[pl.ds(start, tm)], sem)

    def body(*refs):
        x_refs, refs = refs[:nb], refs[nb:]
        (lead_ref, dh1_ref, nw_ref, w_ref, dgv_ref, dgr_ref, dsq_ref, dgq_ref, dgk_ref, dsk_ref, dsv_ref, dlr_ref,
         c_ref, sa_ref, sb_ref, gx_ref, dlead_ref, dproj_ref, ut_ref, gnm_ref, scr, sem) = refs
        i = pl.program_id(0)

        @pl.when(i == 0)
        def _():
            gnm_ref[...] = jnp.zeros_like(gnm_ref)

        cos, sa, sb = c_ref[...], sa_ref[...], sb_ref[...]
        dsq_v = (_unrope(dsq_ref[...], cos, sa, sb) * (SWA_HD ** -0.5)).astype(MXU_DTYPE)
        dsk_v = _unrope(dsk_ref[...], cos, sa, sb).astype(MXU_DTYPE)
        dproj = jnp.concatenate(
            [dgv_ref[...].astype(MXU_DTYPE), dgr_ref[...].astype(MXU_DTYPE), dsq_v, dgq_ref[...].astype(MXU_DTYPE),
             dgk_ref[...].astype(MXU_DTYPE), dsk_v, dsv_ref[...].astype(MXU_DTYPE), dlr_ref[...].astype(MXU_DTYPE)],
            axis=1)
        dproj_ref[...] = dproj
        h = _h_tile(i, lead_ref, x_refs)
        rstd = lax.rsqrt(jnp.mean(h * h, axis=-1, keepdims=True) + EPS)
        hn = h * rstd
        nw_v = nw_ref[...]
        ut_ref[...] = (hn * nw_v).T.astype(ACT_DTYPE)
        du = _mm_nt(dproj, w_ref[...])
        gnm_ref[...] += jnp.broadcast_to(jnp.sum(du * hn, axis=0, keepdims=True), gnm_ref.shape)
        dun = du * nw_v
        dh0 = dh1_ref[...] + rstd * (dun - hn * jnp.mean(dun * hn, axis=-1, keepdims=True))

        if tm > LEAD:
            pl.when(i == 1)(lambda: first_copy(scr, gx_ref, sem).wait())
        pl.when(i > 1)(lambda: tile_copy(scr, gx_ref, sem, i).wait())
        scr[...] = dh0

        @pl.when(i == 0)
        def _():
            dlead_ref[...] = dh0[0:LEAD]
            if tm > LEAD:
                first_copy(scr, gx_ref, sem).start()
                if steps == 1:
                    first_copy(scr, gx_ref, sem).wait()

        @pl.when(i > 0)
        def _():
            tile_copy(scr, gx_ref, sem, i).start()

        if steps > 1:
            pl.when(i == steps - 1)(lambda: tile_copy(scr, gx_ref, sem, i).wait())

    row = lambda w: pl.BlockSpec((tm, w), lambda i: (i, 0))
    const = lambda shape: pl.BlockSpec(shape, lambda i: (0,) * len(shape))
    return pl.pallas_call(
        body, name="in_proj_bwd", grid=(steps,),
        in_specs=_token_specs(tm) + [const((LEAD, D)), row(D), const((1, D)), const((D, DINP)),
                                     row(512), row(512), row(512), row(256), row(256), row(128), row(128), row(128),
                                     row(128), row(128), row(128)],
        out_specs=[ANY, const((LEAD, D)), row(DINP), pl.BlockSpec((D, tm), lambda i: (0, i)), const((8, D))],
        out_shape=[jax.ShapeDtypeStruct((seq, D), F32), jax.ShapeDtypeStruct((LEAD, D), F32),
                   jax.ShapeDtypeStruct((rows, DINP), ACT_DTYPE), jax.ShapeDtypeStruct((D, rows), ACT_DTYPE),
                   jax.ShapeDtypeStruct((8, D), F32)],
        scratch_shapes=[pltpu.VMEM((tm, D), F32), pltpu.SemaphoreType.DMA],
        compiler_params=_cp(("arbitrary",), 56),
    )(*([x] * nb), lead, dh1, nw, win_p, dgv, dgr, dsq, dgq, dgk, dsk, dsv, dlr, *tabs)


def _win_runs():
    groups = [(O_GQ, C_GQ), (O_GK, C_GK), (O_GV, C_GV), (O_GR, C_GR), (O_LR, C_LR), (O_SQ, C_SQ), (O_SK, C_SK),
              (O_SV, C_SV)]
    per = DIN // N_DEV
    runs = []
    for (o0, o1), c0 in groups:
        o = o0
        while o < o1:
            d = o // per
            end = min(o1, (d + 1) * per)
            runs.append((d, o - d * per, c0 + o - o0, end - o))
            o = end
    return runs


def _win_padded(g_in):
    tr = 128

    def body(g_ref, o_ref):
        o_ref[...] = jnp.zeros_like(o_ref)
        for d, s, c, w in _win_runs():
            o_ref[:, c:c + w] = g_ref[d, :, s:s + w]

    return pl.pallas_call(
        body, name="w_in_layout", grid=(D // tr,),
        in_specs=[pl.BlockSpec((N_DEV, tr, DIN // N_DEV), lambda i: (0, i, 0))],
        out_specs=pl.BlockSpec((tr, DINP), lambda i: (i, 0)),
        out_shape=jax.ShapeDtypeStruct((D, DINP), g_in.dtype),
        compiler_params=_cp(("arbitrary",)),
    )(g_in)


def _in_proj_bwd_weights(ut, dproj, tm):
    rows = dproj.shape[0]
    steps = rows // tm
    per = DIN // N_DEV

    def body(ut_ref, dp_ref, out_ref, acc, stage, sems):
        i = pl.program_id(0)

        @pl.when(i == 0)
        def _():
            acc[...] = jnp.zeros_like(acc)

        acc[...] += _mm(ut_ref[...], dp_ref[...])

        @pl.when(i == steps - 1)
        def _():
            copies = []
            for d in range(N_DEV):
                slot = d % 2
                if d >= 2:
                    copies[d - 2].wait()
                for owner, s, c, w in _win_runs():
                    if owner == d:
                        stage[slot, :, s:s + w] = acc[:, c:c + w]
                cp = pltpu.make_async_copy(stage.at[slot], out_ref.at[d % 2, d // 2], sems.at[slot])
                cp.start()
                copies.append(cp)
            copies[N_DEV - 2].wait()
            copies[N_DEV - 1].wait()

    return pl.pallas_call(
        body, name="in_proj_bwd_weights", grid=(steps,),
        in_specs=[pl.BlockSpec((D, tm), lambda i: (0, i)), pl.BlockSpec((tm, DINP), lambda i: (i, 0))],
        out_specs=ANY,
        out_shape=jax.ShapeDtypeStruct((2, 4, D, per), F32),
        scratch_shapes=[pltpu.VMEM((D, DINP), F32), pltpu.VMEM((2, D, per), F32), pltpu.SemaphoreType.DMA((2,))],
        compiler_params=_cp(("arbitrary",), 56),
    )(ut, dproj)


def _adamw(w, g, m, v):
    m = ADAM_B1 * m + (1.0 - ADAM_B1) * g
    v = ADAM_B2 * v + (1.0 - ADAM_B2) * jnp.square(g)
    m_hat = m / (1.0 - ADAM_B1 ** ADAM_STEP)
    v_hat = v / (1.0 - ADAM_B2 ** ADAM_STEP)
    delta = -ADAM_LR * (m_hat / (jnp.sqrt(v_hat) + ADAM_EPS) + ADAM_WD * w)
    return delta, m, v


ADAM_STEPS = 8


def _adamw_shards(where, items, name, jobs=None):
    jobs = jobs or _Jobs([])
    ns, nw = jobs.n, len(items)

    def body(where_ref, *rest):
        ins, rest = rest[:5 * nw], rest[5 * nw:]
        job_ins, rest = rest[:ns], rest[ns:]
        outs, rest = rest[:4 * nw], rest[4 * nw:]
        start, finish = jobs.bind(job_ins, rest[:ns], rest[ns:])
        i = pl.program_id(0)
        pl.when(i == 0)(start)
        pl.when(i == ADAM_STEPS - 1)(finish)
        for k in range(nw):
            p_ref, own_ref, w_ref, m_ref, v_ref = ins[5 * k:5 * k + 5]
            g_ref, d_ref, nm_ref, nv_ref = outs[4 * k:4 * k + 4]
            g = ((p_ref[0].astype(F32) + p_ref[1].astype(F32)) + p_ref[2].astype(F32)) + own_ref[...]
            g_ref[...] = g
            d_ref[...], nm_ref[...], nv_ref[...] = _adamw(w_ref[...], g, m_ref[...], v_ref[...])

    in_specs, out_specs, out_shape, operands = [], [], [], []
    for parts, own, w, m, v in items:
        r, cdim = w.shape
        tr = r // ADAM_STEPS
        spec = pl.BlockSpec((tr, cdim), lambda i, s: (i, 0))
        in_specs += [pl.BlockSpec((3, tr, cdim), lambda i, s: (0, i, 0)),
                     pl.BlockSpec((None, tr, cdim), lambda i, s: (s[1], i, 0)), spec, spec, spec]
        out_specs += [spec] * 4
        out_shape += [jax.ShapeDtypeStruct((r, cdim), F32)] * 4
        operands += [parts, own, w, m, v]
    outs = pl.pallas_call(
        body, name=name,
        grid_spec=pltpu.PrefetchScalarGridSpec(
            num_scalar_prefetch=1, grid=(ADAM_STEPS,),
            in_specs=in_specs + [ANY] * ns, out_specs=out_specs + [ANY] * ns, scratch_shapes=jobs.sems),
        out_shape=out_shape + jobs.out_shapes,
        compiler_params=_cp(("arbitrary",)),
    )(where, *operands, *jobs.inputs)
    return [outs[4 * k:4 * k + 4] for k in range(nw)], jobs.split(outs[4 * nw:])


def _adamw_small(items):
    n = len(items)

    def body(*refs):
        ins, outs = refs[:4 * n], refs[4 * n:]
        for k in range(n):
            w_ref, g_ref, m_ref, v_ref = ins[4 * k:4 * k + 4]
            d_ref, nm_ref, nv_ref = outs[3 * k:3 * k + 3]
            d_ref[...], nm_ref[...], nv_ref[...] = _adamw(w_ref[...], g_ref[...], m_ref[...], v_ref[...])

    vm = pl.BlockSpec(memory_space=pltpu.VMEM)
    shapes = [jax.ShapeDtypeStruct(w.shape, F32) for w, _, _, _ in items for _ in range(3)]
    outs = pl.pallas_call(body, name="adamw_small", in_specs=[vm] * (4 * n), out_specs=[vm] * (3 * n),
                          out_shape=shapes)(*[t for item in items for t in item])
    return [outs[3 * k:3 * k + 3] for k in range(n)]


def _add_own_half(where, full, theirs, name, wire_copy=False):
    _, _, r, cdim = full.shape
    tr = 128 if r % 128 == 0 else r

    def body(where_ref, a_ref, b_ref, *o_refs):
        total = a_ref[...] + b_ref[...]
        o_refs[0][...] = total
        if wire_copy:
            o_refs[1][...] = total.astype(WIRE_DTYPE)

    spec = pl.BlockSpec((4, tr, cdim), lambda i, s: (0, i, 0))
    shapes = [jax.ShapeDtypeStruct(theirs.shape, F32)] + ([jax.ShapeDtypeStruct(theirs.shape, WIRE_DTYPE)] if wire_copy else [])
    outs = pl.pallas_call(
        body, name=name,
        grid_spec=pltpu.PrefetchScalarGridSpec(
            num_scalar_prefetch=1, grid=(r // tr,),
            in_specs=[pl.BlockSpec((None, 4, tr, cdim), lambda i, s: (s[0], 0, i, 0)), spec],
            out_specs=[spec] * len(shapes)),
        out_shape=shapes, compiler_params=_cp(("arbitrary",)))(where, full, theirs)
    return outs if wire_copy else outs[0]


def kernel(x, meta_tokens, norm_mix_w, w_in, w_gate_up, b_gate, gla_norm_w, sinks, w_out, norm_ff_w, w_ff1, w_ff2, final_norm_w, loss_target, m_meta_tokens, m_norm_mix_w, m_w_in, m_w_gate_up, m_b_gate, m_gla_norm_w, m_sinks, m_w_out, m_norm_ff_w, m_w_ff1, m_w_ff2, m_final_norm_w, v_meta_tokens, v_norm_mix_w, v_w_in, v_w_gate_up, v_b_gate, v_gla_norm_w, v_sinks, v_w_out, v_norm_ff_w, v_w_ff1, v_w_ff2, v_final_norm_w):
    seq = x.shape[1]
    rows = LEAD + seq
    tm = _row_tile(rows)
    tm_wide = 1664 if rows % 1664 == 0 else tm
    dev =4 * lax.axis_index("x") + 2 * lax.axis_index("y") + lax.axis_index("c")

    small_shard = jnp.concatenate([meta_tokens, w_gate_up[0], jnp.zeros((N_META, 96), F32)], axis=1)
    g_in, g_small = _all_gather([w_in[0].astype(WIRE_DTYPE), small_shard])
    later_shards = [w_out[0].astype(WIRE_DTYPE), w_ff1[0].astype(WIRE_DTYPE), w_ff2[0].astype(WIRE_DTYPE)]
    win_p = _win_padded(g_in)
    meta_full = jnp.transpose(g_small[:, :, 0:128], (1, 0, 2)).reshape(N_META, D)
    wg_full = jnp.transpose(g_small[:, :, 128:160], (1, 0, 2)).reshape(GLA_RANK, GLA_HEADS * GLA_DK)
    wg_p = jnp.concatenate([wg_full, jnp.zeros((128 - GLA_RANK, 256), F32)], axis=0)

    lead = jnp.concatenate([jnp.zeros((META0, D), F32), meta_full], axis=0)
    tabs = _rope_tables(rows)
    proj = _in_proj(x[0], lead, norm_mix_w, win_p, tm)
    oraw, og, states, _ = _gla_fwd(proj, wg_p, b_gate, gla_norm_w, [])
    qr, kr, vr = _swa_prep(proj, tabs, tm)
    osw, (g_out, g_w1, g_w2) = _swa_fwd(qr, kr, vr, sinks, later_shards)
    wout_full = g_out.reshape(D, D)
    w2_full = g_w2.reshape(D_FF, D)
    w1_full = jnp.transpose(g_w1, (1, 0, 2)).reshape(D, D_FF)
    h1, f, ft = _out_proj(x[0], lead, og, osw, wout_full, norm_ff_w, tm)
    a, dh2, dh2t, loss_p, gfn_p = _ffn_fwd(f, h1, w1_full, w2_full, loss_target[0], final_norm_w.reshape(1, D), tm)

    da, dh1, gnf_p = _ffn_bwd_act(dh2, a, w1_full, w2_full, h1, norm_ff_w, tm)
    dw1, dw2 = _ffn_bwd_weights(ft, a, da, dh2t, tm_wide)
    where = jnp.stack([lax.axis_index("c"), 2 * lax.axis_index("x") + lax.axis_index("y")]).astype(jnp.int32)
    dog, dos, dwout, theirs_ffn = _out_proj_bwd(dh1, og, osw, wout_full, tm, [dw1, dw2])
    sums_ffn = [_add_own_half(where, p, q, "reduce_pair_%d" % (2 + k))
                for k, (p, q) in enumerate(zip([dw1, dw2], theirs_ffn))]
    dsq, dsk, dsv, dsink_p, (parts_ffn, (theirs_wout,)) = _swa_bwd(
        qr, kr, vr, osw, dos, sinks, _Jobs([("chips", sums_ffn), ("sibling", [dwout])]))
    sum_wout = _add_own_half(where, dwout, theirs_wout, "reduce_pair_1")
    (dgq, dgk, dgv, dgr, dlr, dwg_p, dbg_p, dgnw_p), ((parts_wout,),) = _gla_bwd(
        proj, oraw, states, dog, wg_p, b_gate, gla_norm_w, _Jobs([("chips", [sum_wout])]))
    grad_x, dlead, dproj, ut, gnm_p = _in_proj_bwd(x[0], lead, dh1, norm_mix_w, win_p, dgv, dgr, dsq, dgq, dgk, dsk,
                                                   dsv, dlr, tabs, tm)
    grad_x = grad_x[None]
    dwin = _in_proj_bwd_weights(ut, dproj, tm_wide)

    (theirs_win,) = _rs_sibling([dwin])
    sum_win, sum_win_wire = _add_own_half(where, dwin, theirs_win, "reduce_pair_0", wire_copy=True)

    total = _all_reduce_small(dlead, dwg_p, gnm_p, gnf_p, gfn_p, dbg_p, dgnw_p, loss_p, dsink_p)
    g_meta = lax.dynamic_slice(total, (R_META, dev * 128), (N_META, 128))
    g_wg = lax.dynamic_slice(total, (R_WG, dev * 32), (GLA_RANK, 32))
    g_norm_mix, g_norm_ff = total[R_NORM_MIX:R_NORM_MIX + 1], total[R_NORM_FF:R_NORM_FF + 1]
    g_final_norm = total[R_FINAL:R_FINAL + 1]
    g_b_gate, g_gla_norm = total[R_B_GATE:R_B_GATE + 1, 0:256], total[R_GLA_NORM:R_GLA_NORM + 1, 0:128]
    g_sinks = total[R_SINKS:R_SINKS + SWA_HEADS, 0].reshape(1, SWA_HEADS)
    loss = total[R_LOSS, 0]

    ((g_wout, d_wout, nm_wout, nv_wout), (g_w1s, d_w1, nm_w1, nv_w1), (g_w2s, d_w2, nm_w2, nv_w2)), ((parts_win,),) = \
        _adamw_shards(where, [(parts_wout, sum_wout, w_out[0], m_w_out[0], v_w_out[0]),
                              (parts_ffn[0], sums_ffn[0], w_ff1[0], m_w_ff1[0], v_w_ff1[0]),
                              (parts_ffn[1], sums_ffn[1], w_ff2[0], m_w_ff2[0], v_w_ff2[0])],
                      "adamw_w_out_ff", _Jobs([("chips", [sum_win_wire])]))
    ((g_win, d_win, nm_win, nv_win),), _ = _adamw_shards(
        where, [(parts_win, sum_win, w_in[0], m_w_in[0], v_w_in[0])], "adamw_w_in")

    names = ["meta", "wg", "norm_mix", "b_gate", "gla_norm", "sinks", "norm_ff", "final_norm"]
    ws = [meta_tokens, w_gate_up, norm_mix_w, b_gate, gla_norm_w, sinks, norm_ff_w, final_norm_w]
    gs = [g_meta, g_wg, g_norm_mix, g_b_gate, g_gla_norm, g_sinks, g_norm_ff, g_final_norm]
    ms = [m_meta_tokens, m_w_gate_up, m_norm_mix_w, m_b_gate, m_gla_norm_w, m_sinks, m_norm_ff_w, m_final_norm_w]
    vs = [v_meta_tokens, v_w_gate_up, v_norm_mix_w, v_b_gate, v_gla_norm_w, v_sinks, v_norm_ff_w, v_final_norm_w]
    flat = lambda t: t.reshape(-1, t.shape[-1])
    small_out = _adamw_small([(flat(w), flat(g), flat(m), flat(v)) for w, g, m, v in zip(ws, gs, ms, vs)])
    d_small = {n: small_out[k][0].reshape(ws[k].shape) for k, n in enumerate(names)}
    nm_small = {n: small_out[k][1].reshape(ws[k].shape) for k, n in enumerate(names)}
    nv_small = {n: small_out[k][2].reshape(ws[k].shape) for k, n in enumerate(names)}
    g_small_d = {n: g.reshape(ws[k].shape) for k, (n, g) in enumerate(zip(names, gs))}

    def ordered(big, small_d):
        win_v, wout_v, w1_v, w2_v = big
        return (small_d["meta"], small_d["norm_mix"], win_v[None], small_d["wg"], small_d["b_gate"],
                small_d["gla_norm"], small_d["sinks"], wout_v[None], small_d["norm_ff"], w1_v[None], w2_v[None],
                small_d["final_norm"])

    return (loss, grad_x,
            *ordered((g_win, g_wout, g_w1s, g_w2s), g_small_d),
            *ordered((d_win, d_wout, d_w1, d_w2), d_small),
            *ordered((nm_win, nm_wout, nm_w1, nm_w2), nm_small),
            *ordered((nv_win, nv_wout, nv_w1, nv_w2), nv_small))
```

```python
import functools

import jax
import jax.numpy as jnp
from jax import lax
from jax.experimental import pallas as pl
from jax.experimental.pallas import tpu as pltpu

F32 = jnp.float32
MXU_DTYPE = jnp.bfloat16
ACT_DTYPE = jnp.bfloat16
WIRE_DTYPE = jnp.bfloat16

D = 1024
N_META = 16
LEAD = 128
META0 = LEAD - N_META
EPS = 1e-5
GLA_HEADS, GLA_DK, GLA_DV, GLA_RANK, GLA_CHUNK = 4, 64, 128, 16, 64
GLA_TAU = 16.0
SWA_HEADS, SWA_KV, SWA_GROUP, SWA_HD, SWA_BLOCK = 8, 2, 4, 64, 128
ROPE_DIM, ROPE_THETA = 16, 500000.0
D_FF = 4096
N_DEV = 8
FF_TILE = D_FF // N_DEV
FF_WIDE = 1024
NEG = -1e30

C_GV, C_GR, C_SQ, C_GQ, C_GK, C_SK, C_SV, C_LR = 0, 512, 1024, 1536, 1792, 2048, 2176, 2304
DINP = 2432
DIN = 2320
O_GQ, O_GK, O_GV, O_GR, O_LR, O_SQ, O_SK, O_SV = (0, 256), (256, 512), (512, 1024), (1024, 1536), (1536, 1552), (1552, 2064), (2064, 2192), (2192, 2320)

ADAM_LR, ADAM_B1, ADAM_B2, ADAM_EPS, ADAM_WD, ADAM_STEP = 0.001, 0.9, 0.999, 1e-08, 0.01, 10

MESH = pl.DeviceIdType.MESH
ANY = pl.BlockSpec(memory_space=pl.ANY)
HIGHEST = lax.Precision.HIGHEST


def _cp(sem=None, vmem_mb=None):
    kw = {}
    if sem is not None:
        kw["dimension_semantics"] = sem
    if vmem_mb is not None:
        kw["vmem_limit_bytes"] = vmem_mb << 20
    return pltpu.CompilerParams(**kw)


def _mm(a, b):
    return jnp.dot(a.astype(MXU_DTYPE), b.astype(MXU_DTYPE), preferred_element_type=F32)


def _mm_nt(a, b):
    return lax.dot_general(a.astype(MXU_DTYPE), b.astype(MXU_DTYPE), (((1,), (1,)), ((), ())),
                           preferred_element_type=F32)


def _mm_tn(a, b):
    return lax.dot_general(a.astype(MXU_DTYPE), b.astype(MXU_DTYPE), (((0,), (0,)), ((), ())),
                           preferred_element_type=F32)


def _logsigmoid(z):
    return jnp.minimum(z, 0.0) - jnp.log(1.0 + jnp.exp(-jnp.abs(z)))


def _sigmoid(z):
    return 1.0 / (1.0 + jnp.exp(-z))


def _row_tile(rows):
    return 640 if rows % 640 == 0 else 128


def _mesh_pos():
    return lax.axis_index("x"), lax.axis_index("y"), lax.axis_index("c")


def _all_gather(shards):
    n = len(shards)

    def body(*refs):
        start, forward, finish = _gather_schedule(refs[:n], refs[n:2 * n], *refs[2 * n:])
        start()
        for j in range(3):
            forward(j)
        finish()

    gathered = pl.pallas_call(
        body, name="all_gather_weights",
        out_shape=_gathered_shapes(shards), in_specs=[ANY] * n, out_specs=[ANY] * n,
        scratch_shapes=_gather_sems(n),
    )(*shards)
    return _with_own_block(gathered, shards)


def _gathered_shapes(shards):
    return [jax.ShapeDtypeStruct((N_DEV,) + s.shape, s.dtype) for s in shards]


def _gather_sems(n):
    return [pltpu.SemaphoreType.DMA((7 * n,)), pltpu.SemaphoreType.DMA((7 * n,))] if n else []


def _place_gather(step, steps, shard_refs, gathered_refs, sems):
    if not shard_refs:
        return
    start, forward, finish = _gather_schedule(shard_refs, gathered_refs, *sems)
    pl.when(step == 0)(start)
    for j, at in enumerate((steps * 7 // 10, steps * 8 // 10, steps * 9 // 10)):
        pl.when(step == at)(functools.partial(forward, j))
    pl.when(step == steps - 1)(finish)


def _with_own_block(gathered, shards):
    dev = 4 * lax.axis_index("x") + 2 * lax.axis_index("y") + lax.axis_index("c")
    return [lax.dynamic_update_index_in_dim(g, s, dev, 0) for g, s in zip(gathered, shards)]


def _gather_schedule(ins, outs, send_sems, recv_sems):
    n = len(ins)
    x, y, c = _mesh_pos()
    me, sibling = (x, y, c), (x, y, 1 - c)
    chips = [(1 - x, y), (x, 1 - y), (1 - x, 1 - y)]

    def copy(a, k, block, to, src=None):
        dst = outs[a].at[4 * block[0] + 2 * block[1] + block[2]]
        return pltpu.make_async_remote_copy(
            src_ref=dst if src is None else src, dst_ref=dst,
            send_sem=send_sems.at[a * 7 + k], recv_sem=recv_sems.at[a * 7 + k],
            device_id=to, device_id_type=MESH)

    def first(a):
        return [copy(a, 0, me, sibling, src=ins[a])] + [copy(a, 1 + j, me, (*chip, c), src=ins[a])
                                                        for j, chip in enumerate(chips)]

    def start():
        for a in range(n):
            for cp in first(a):
                cp.start()

    def forward(j):
        for a in range(n):
            copy(a, 1 + j, (*chips[j], c), me).wait_recv()
            copy(a, 4 + j, (*chips[j], c), sibling).start()

    def finish():
        for a in range(n):
            copy(a, 0, sibling, me).wait_recv()
            for j, chip in enumerate(chips):
                copy(a, 4 + j, (*chip, 1 - c), me).wait_recv()
        for a in range(n):
            for cp in first(a) + [copy(a, 4 + j, (*chip, c), sibling) for j, chip in enumerate(chips)]:
                cp.wait_send()

    return start, forward, finish


def _rs_sibling(gs):
    n = len(gs)

    def body(*refs):
        start, finish = _sibling_schedule(refs[:n], refs[n:2 * n], *refs[2 * n:])
        start()
        finish()

    return pl.pallas_call(
        body, name="reduce_scatter_sibling",
        out_shape=_sibling_shapes(gs), in_specs=[ANY] * n, out_specs=[ANY] * n,
        scratch_shapes=_sibling_sems(n),
    )(*gs)


def _sibling_shapes(gs):
    return [jax.ShapeDtypeStruct(g.shape[1:], g.dtype) for g in gs]


def _sibling_sems(n):
    return [pltpu.SemaphoreType.DMA((n,)), pltpu.SemaphoreType.DMA((n,))]


def _sibling_schedule(ins, land, send_sems, recv_sems):
    x, y, c = _mesh_pos()

    def copies():
        return [pltpu.make_async_remote_copy(
            src_ref=ins[a].at[1 - c], dst_ref=land[a], send_sem=send_sems.at[a], recv_sem=recv_sems.at[a],
            device_id=(x, y, 1 - c), device_id_type=MESH) for a in range(len(ins))]

    def start():
        for cp in copies():
            cp.start()

    def finish():
        for cp in copies():
            cp.wait_recv()
        for cp in copies():
            cp.wait_send()

    return start, finish


def _rs_chips(ps):
    n = len(ps)

    def body(*refs):
        start, finish = _chips_schedule(refs[:n], refs[n:2 * n], *refs[2 * n:])
        start()
        finish()

    return pl.pallas_call(
        body, name="reduce_scatter_chips",
        out_shape=_chips_shapes(ps), in_specs=[ANY] * n, out_specs=[ANY] * n,
        scratch_shapes=_chips_sems(n),
    )(*ps)


def _chips_shapes(ps):
    return [jax.ShapeDtypeStruct((3,) + p.shape[1:], p.dtype) for p in ps]


def _chips_sems(n):
    return [pltpu.SemaphoreType.DMA((3 * n,)), pltpu.SemaphoreType.DMA((3 * n,))]


def _chips_schedule(ins, land, send_sems, recv_sems):
    x, y, c = _mesh_pos()
    chips = [(1 - x, y), (x, 1 - y), (1 - x, 1 - y)]

    def copies():
        return [pltpu.make_async_remote_copy(
            src_ref=ins[a].at[2 * chip[0] + chip[1]], dst_ref=land[a].at[j],
            send_sem=send_sems.at[3 * a + j], recv_sem=recv_sems.at[3 * a + j],
            device_id=(*chip, c), device_id_type=MESH) for a in range(len(ins)) for j, chip in enumerate(chips)]

    def start():
        for cp in copies():
            cp.start()

    def finish():
        for cp in copies():
            cp.wait_recv()
        for cp in copies():
            cp.wait_send()

    return start, finish


class _Jobs:
    def __init__(self, jobs):
        self.jobs = jobs
        self.inputs = [a for _, arrs in jobs for a in arrs]
        self.out_shapes = [s for kind, arrs in jobs
                           for s in (_sibling_shapes(arrs) if kind == "sibling" else _chips_shapes(arrs))]
        self.sems = [s for kind, arrs in jobs
                     for s in (_sibling_sems(len(arrs)) if kind == "sibling" else _chips_sems(len(arrs)))]
        self.n = len(self.inputs)

    def bind(self, in_refs, out_refs, sem_refs):
        starts, finishes, at = [], [], 0
        for k, (kind, arrs) in enumerate(self.jobs):
            schedule = _sibling_schedule if kind == "sibling" else _chips_schedule
            start, finish = schedule(in_refs[at:at + len(arrs)], out_refs[at:at + len(arrs)],
                                     sem_refs[2 * k], sem_refs[2 * k + 1])
            starts.append(start)
            finishes.append(finish)
            at += len(arrs)

        def start_all():
            for f in starts:
                f()

        def finish_all():
            for f in finishes:
                f()

        return start_all, finish_all

    def split(self, outs):
        res, at = [], 0
        for _, arrs in self.jobs:
            res.append(list(outs[at:at + len(arrs)]))
            at += len(arrs)
        return res


R_META, R_WG, R_NORM_MIX, R_NORM_FF, R_FINAL, R_B_GATE, R_GLA_NORM, R_LOSS, R_SINKS, SMALL_ROWS = 0, 16, 32, 33, 34, 35, 36, 37, 40, 48


def _all_reduce_small(dlead, dwg, gnm, gnf, gfn, dbg, dgnw, loss, dsink):
    def body(dlead_ref, dwg_ref, gnm_ref, gnf_ref, gfn_ref, dbg_ref, dgnw_ref, loss_ref, dsink_ref,
             out_ref, p_ref, land, send_sems, recv_sems):
        x, y, c = _mesh_pos()
        me = 4 * x + 2 * y + c
        p_ref[...] = jnp.zeros_like(p_ref)
        p_ref[R_META:R_META + N_META, :] = dlead_ref[META0:LEAD, :]
        p_ref[R_WG:R_WG + GLA_RANK, 0:256] = dwg_ref[0:GLA_RANK, :]
        p_ref[R_NORM_MIX:R_NORM_MIX + 1, :] = gnm_ref[0:1, :]
        p_ref[R_NORM_FF:R_NORM_FF + 1, :] = gnf_ref[0:1, :]
        p_ref[R_FINAL:R_FINAL + 1, :] = gfn_ref[0:1, :]
        p_ref[R_B_GATE:R_B_GATE + 1, 0:256] = dbg_ref[0:1, :]
        p_ref[R_GLA_NORM:R_GLA_NORM + 1, 0:128] = dgnw_ref[0:1, :]
        p_ref[R_LOSS:R_LOSS + 1, 0:128] = loss_ref[0:1, :]
        p_ref[R_SINKS:R_SINKS + SWA_HEADS, 0:128] = dsink_ref[...]
        land[me] = p_ref[...]
        copies = []
        for k in range(1, N_DEV):
            bx, by, bc = (k >> 2) & 1, (k >> 1) & 1, k & 1
            peer = (1 - x if bx else x, 1 - y if by else y, 1 - c if bc else c)
            copies.append(pltpu.make_async_remote_copy(
                src_ref=p_ref, dst_ref=land.at[me], send_sem=send_sems.at[k - 1], recv_sem=recv_sems.at[k - 1],
                device_id=peer, device_id_type=MESH))
        for cp in copies:
            cp.start()
        for cp in copies:
            cp.wait_recv()
        for cp in copies:
            cp.wait_send()
        acc = land[0]
        for d in range(1, N_DEV):
            acc = acc + land[d]
        out_ref[...] = acc

    return pl.pallas_call(
        body, name="all_reduce_small",
        out_shape=jax.ShapeDtypeStruct((SMALL_ROWS, D), F32),
        in_specs=[pl.BlockSpec(memory_space=pltpu.VMEM)] * 9, out_specs=pl.BlockSpec(memory_space=pltpu.VMEM),
        scratch_shapes=[pltpu.VMEM((SMALL_ROWS, D), F32), pltpu.VMEM((N_DEV, SMALL_ROWS, D), F32),
                        pltpu.SemaphoreType.DMA((7,)), pltpu.SemaphoreType.DMA((7,))],
    )(dlead, dwg, gnm, gnf, gfn, dbg, dgnw, loss, dsink)


def _token_specs(tm, grid_rank=1):
    nb = tm // LEAD

    def spec(k):
        if grid_rank == 1:
            return pl.BlockSpec((LEAD, D), lambda i: (jnp.maximum(i * nb + k - 1, 0), 0))
        return pl.BlockSpec((LEAD, D), lambda i, j: (jnp.maximum(i * nb + k - 1, 0), 0))

    return [spec(k) for k in range(nb)]


def _h_tile(i, lead_ref, x_refs):
    first = jnp.where(i == 0, lead_ref[...], x_refs[0][...])
    return jnp.concatenate([first] + [r[...] for r in x_refs[1:]], axis=0)


def _in_proj(x, lead, nw, win_p, tm):
    rows = LEAD + x.shape[0]
    nb = tm // LEAD

    def body(*refs):
        x_refs, (lead_ref, nw_ref, w_ref, o_ref) = refs[:nb], refs[nb:]
        h = _h_tile(pl.program_id(0), lead_ref, x_refs)
        rstd = lax.rsqrt(jnp.mean(h * h, axis=-1, keepdims=True) + EPS)
        u = (h * rstd * nw_ref[...]).astype(MXU_DTYPE)
        o_ref[...] = jnp.dot(u, w_ref[...].astype(MXU_DTYPE), preferred_element_type=F32)

    return pl.pallas_call(
        body, name="in_proj", grid=(rows // tm,),
        in_specs=_token_specs(tm) + [pl.BlockSpec((LEAD, D), lambda i: (0, 0)), pl.BlockSpec((1, D), lambda i: (0, 0)),
                                     pl.BlockSpec((D, DINP), lambda i: (0, 0))],
        out_specs=pl.BlockSpec((tm, DINP), lambda i: (i, 0)),
        out_shape=jax.ShapeDtypeStruct((rows, DINP), F32),
        compiler_params=_cp(("arbitrary",), 56),
    )(*([x] * nb), lead, nw, win_p)


def _rope_tables(rows):
    pos = (jnp.arange(rows, dtype=jnp.int32) - META0).astype(F32)
    inv_freq = 1.0 / (ROPE_THETA ** (jnp.arange(0, ROPE_DIM, 2, dtype=F32) / ROPE_DIM))
    ang = pos[:, None] * jnp.tile(inv_freq, 128 // (ROPE_DIM // 2))[None, :]
    in_head = jnp.arange(128, dtype=jnp.int32)[None, :] % SWA_HD
    cos, sin = jnp.cos(ang), jnp.sin(ang)
    c_tab = jnp.where(in_head < ROPE_DIM, cos, 1.0)
    sa_tab = jnp.where(in_head < ROPE_DIM // 2, -sin, 0.0)
    sb_tab = jnp.where((in_head >= ROPE_DIM // 2) & (in_head < ROPE_DIM), sin, 0.0)
    return c_tab, sa_tab, sb_tab


def _rope(xv, cos, sa, sb):
    width = xv.shape[1]
    reps = width // 128
    if reps > 1:
        cos, sa, sb = (jnp.tile(t, (1, reps)) for t in (cos, sa, sb))
    return xv * cos + pltpu.roll(xv, width - 8, 1) * sa + pltpu.roll(xv, 8, 1) * sb


def _unrope(dy, cos, sa, sb):
    width = dy.shape[1]
    reps = width // 128
    if reps > 1:
        cos, sa, sb = (jnp.tile(t, (1, reps)) for t in (cos, sa, sb))
    return dy * cos + pltpu.roll(dy * sa, 8, 1) + pltpu.roll(dy * sb, width - 8, 1)


def _swa_prep(proj, tabs, tm):
    rows = proj.shape[0]

    def body(q_ref, k_ref, v_ref, c_ref, sa_ref, sb_ref, qo_ref, ko_ref, vo_ref):
        cos, sa, sb = c_ref[...], sa_ref[...], sb_ref[...]
        qo_ref[...] = (_rope(q_ref[...], cos, sa, sb) * (SWA_HD ** -0.5)).astype(ACT_DTYPE)
        ko_ref[...] = _rope(k_ref[...], cos, sa, sb).astype(ACT_DTYPE)
        vo_ref[...] = v_ref[...].astype(ACT_DTYPE)

    tab_spec = pl.BlockSpec((tm, 128), lambda i: (i, 0))
    return pl.pallas_call(
        body, name="swa_prep", grid=(rows // tm,),
        in_specs=[pl.BlockSpec((tm, 512), lambda i: (i, C_SQ // 512)),
                  pl.BlockSpec((tm, 128), lambda i: (i, C_SK // 128)),
                  pl.BlockSpec((tm, 128), lambda i: (i, C_SV // 128)), tab_spec, tab_spec, tab_spec],
        out_specs=[pl.BlockSpec((tm, 512), lambda i: (i, 0)), tab_spec, tab_spec],
        out_shape=[jax.ShapeDtypeStruct((rows, 512), ACT_DTYPE), jax.ShapeDtypeStruct((rows, 128), ACT_DTYPE),
                   jax.ShapeDtypeStruct((rows, 128), ACT_DTYPE)],
        compiler_params=_cp(("arbitrary",)),
    )(proj, proj, proj, *tabs)


def _gla_group(nc):
    for g in (5, 2):
        if nc % g == 0:
            return g
    return 1


def _gla_gates(lr, wg, bg, first_row, nrows):
    zg = _mm(lr, wg) + bg
    row = first_row + lax.broadcasted_iota(jnp.int32, (nrows, 1), 0)
    live = row >= META0
    g = jnp.where(live, _logsigmoid(zg) * (1.0 / GLA_TAU), 0.0)
    ii = lax.broadcasted_iota(jnp.int32, (nrows, nrows), 0)
    jj = lax.broadcasted_iota(jnp.int32, (nrows, nrows), 1)
    same = (ii // GLA_CHUNK) == (jj // GLA_CHUNK)
    lower, upper = same & (jj <= ii), same & (jj >= ii)
    b = jnp.dot(lower.astype(F32), g, precision=HIGHEST, preferred_element_type=F32)
    return zg, live, lower, upper, b


def _tril64():
    ii = lax.broadcasted_iota(jnp.int32, (GLA_CHUNK, GLA_CHUNK), 0)
    jj = lax.broadcasted_iota(jnp.int32, (GLA_CHUNK, GLA_CHUNK), 1)
    return jj <= ii


def _gla_fwd(proj, wg_p, bg, gnw, shards):
    rows = proj.shape[0]
    nc = rows // GLA_CHUNK
    group = _gla_group(nc)
    steps, nrows = nc // group, group * GLA_CHUNK
    ns = len(shards)

    def body(q_ref, k_ref, v_ref, r_ref, lr_ref, wg_ref, bg_ref, gnw_ref, *rest):
        shard_refs, rest = rest[:ns], rest[ns:]
        oraw_ref, og_ref, st_ref = rest[:3]
        gathered_refs, rest = rest[3:3 + ns], rest[3 + ns:]
        state = rest[0]
        c = pl.program_id(0)

        @pl.when(c == 0)
        def _():
            state[...] = jnp.zeros_like(state)

        _place_gather(c, steps, shard_refs, gathered_refs, rest[1:])
        _, _, _, _, b = _gla_gates(lr_ref[...], wg_ref[...], bg_ref[...], c * nrows, nrows)
        eb = jnp.exp(b)
        gq = q_ref[...] * (GLA_DK ** -0.5) * eb
        gk = k_ref[...] * jnp.exp(-b)
        v = v_ref[...]
        gnw_v = gnw_ref[...]
        tril = _tril64()
        pairs = [(h, gi) for h in range(GLA_HEADS) for gi in range(group)]
        rs = {gi: slice(gi * GLA_CHUNK, (gi + 1) * GLA_CHUNK) for gi in range(group)}
        s64 = {h: slice(h * GLA_DK, (h + 1) * GLA_DK) for h in range(GLA_HEADS)}
        s128 = {h: slice(h * GLA_DV, (h + 1) * GLA_DV) for h in range(GLA_HEADS)}
        qh = {(h, gi): gq[rs[gi], s64[h]] for h, gi in pairs}
        kh = {(h, gi): gk[rs[gi], s64[h]] for h, gi in pairs}
        vh = {(h, gi): v[rs[gi], s128[h]] for h, gi in pairs}
        ebl = {(h, gi): eb[(gi + 1) * GLA_CHUNK - 1:(gi + 1) * GLA_CHUNK, s64[h]] for h, gi in pairs}
        av = {pr: _mm(jnp.where(tril, _mm_nt(qh[pr], kh[pr]), 0.0), vh[pr]) for pr in pairs}
        inc = {pr: _mm_tn(vh[pr], kh[pr] * ebl[pr]) for pr in pairs}
        st = {}
        for h in range(GLA_HEADS):
            cur = state[h]
            for gi in range(group):
                st[h, gi] = cur
                st_ref[gi, h] = cur
                cur = cur * ebl[h, gi] + inc[h, gi]
            state[h] = cur
        for h, gi in pairs:
            o = av[h, gi] + _mm_nt(qh[h, gi], st[h, gi])
            oraw_ref[rs[gi], s128[h]] = o
            rstd = lax.rsqrt(jnp.mean(o * o, axis=-1, keepdims=True) + EPS)
            rh = r_ref[rs[gi], s128[h]]
            og_ref[rs[gi], s128[h]] = (o * rstd * gnw_v * (rh * _sigmoid(rh))).astype(ACT_DTYPE)

    nb = lambda w, col: pl.BlockSpec((nrows, w), lambda c: (c, col // w))
    const = lambda shape: pl.BlockSpec(shape, lambda c: (0,) * len(shape))
    outs = pl.pallas_call(
        body, name="gla_fwd", grid=(steps,),
        in_specs=[nb(256, C_GQ), nb(256, C_GK), nb(512, C_GV), nb(512, C_GR), nb(128, C_LR),
                  const((128, 256)), const((1, 256)), const((1, 128))] + [ANY] * ns,
        out_specs=[pl.BlockSpec((nrows, 512), lambda c: (c, 0)), pl.BlockSpec((nrows, 512), lambda c: (c, 0)),
                   pl.BlockSpec((group, GLA_HEADS, GLA_DV, GLA_DK), lambda c: (c, 0, 0, 0))] + [ANY] * ns,
        out_shape=[jax.ShapeDtypeStruct((rows, 512), F32), jax.ShapeDtypeStruct((rows, 512), ACT_DTYPE),
                   jax.ShapeDtypeStruct((nc, GLA_HEADS, GLA_DV, GLA_DK), F32)] + _gathered_shapes(shards),
        scratch_shapes=[pltpu.VMEM((GLA_HEADS, GLA_DV, GLA_DK), F32)] + _gather_sems(ns),
        compiler_params=_cp(("arbitrary",)),
    )(proj, proj, proj, proj, proj, wg_p, bg, gnw, *shards)
    return outs[0], outs[1], outs[2], _with_own_block(outs[3:], shards)


def _swa_mask(n):
    shape = (SWA_GROUP * SWA_BLOCK, 3 * SWA_BLOCK)
    qi = lax.broadcasted_iota(jnp.int32, shape, 0) & (SWA_BLOCK - 1)
    jj = lax.broadcasted_iota(jnp.int32, shape, 1)
    meta = (jj < SWA_BLOCK) & (jj >= META0) & ((n > 0) | (jj <= qi))
    prev = (jj >= SWA_BLOCK) & (jj < 2 * SWA_BLOCK) & (n >= 2) & (jj - SWA_BLOCK > qi)
    cur = (jj >= 2 * SWA_BLOCK) & (n >= 1) & (jj - 2 * SWA_BLOCK <= qi)
    return meta | prev | cur


def _stack_heads(t, kvh):
    return jnp.concatenate([t[:, (kvh * SWA_GROUP + g) * SWA_HD:(kvh * SWA_GROUP + g + 1) * SWA_HD]
                            for g in range(SWA_GROUP)], axis=0)


def _stack_sinks(sink_ref, kvh):
    return jnp.concatenate([jnp.full((SWA_BLOCK, 1), sink_ref[0, kvh * SWA_GROUP + g], F32)
                            for g in range(SWA_GROUP)], axis=0)


def _swa_group(nblk):
    return 5 if nblk % 5 == 0 else 1


def _swa_specs(group):
    blk = lambda w: pl.BlockSpec((group * SWA_BLOCK, w), lambda n: (n, 0))
    first = pl.BlockSpec((SWA_BLOCK, 128), lambda n: (0, 0))
    prev = pl.BlockSpec((SWA_BLOCK, 128), lambda n: (jnp.maximum(n * group - 1, 0), 0))
    return blk, first, prev


def _swa_keys(first_ref, prev_ref, cur_ref, g):
    own = cur_ref[g * SWA_BLOCK:(g + 1) * SWA_BLOCK, :]
    before = prev_ref[...] if g == 0 else cur_ref[(g - 1) * SWA_BLOCK:g * SWA_BLOCK, :]
    return jnp.concatenate([first_ref[...], before, own], axis=0)


def _swa_fwd(qr, kr, vr, sinks, shards):
    rows = qr.shape[0]
    nblk = rows // SWA_BLOCK
    group = _swa_group(nblk)
    steps = nblk // group
    ns = len(shards)

    def body(q_ref, k0, kp, kc, v0, vp, vc, sink_ref, *rest):
        o_ref = rest[ns]
        _place_gather(pl.program_id(0), steps, rest[:ns], rest[ns + 1:2 * ns + 1], rest[2 * ns + 1:])
        for g in range(group):
            n = pl.program_id(0) * group + g
            rs = slice(g * SWA_BLOCK, (g + 1) * SWA_BLOCK)
            kall, vall = _swa_keys(k0, kp, kc, g), _swa_keys(v0, vp, vc, g)
            mask = _swa_mask(n)[0:SWA_BLOCK]
            heads = range(SWA_HEADS)
            hs = [slice(h * SWA_HD, (h + 1) * SWA_HD) for h in heads]
            kv = [slice((h // SWA_GROUP) * SWA_HD, (h // SWA_GROUP + 1) * SWA_HD) for h in heads]
            s = [jnp.where(mask, _mm_nt(q_ref[rs, hs[h]], kall[:, kv[h]]), NEG) for h in heads]
            m = [jnp.maximum(jnp.max(s[h], axis=-1, keepdims=True), sink_ref[0, h]) for h in heads]
            p = [jnp.exp(s[h] - m[h]) for h in heads]
            den = [jnp.sum(p[h], axis=-1, keepdims=True) + jnp.exp(sink_ref[0, h] - m[h]) for h in heads]
            o = [_mm(p[h], vall[:, kv[h]]) for h in heads]
            for h in heads:
                o_ref[rs, hs[h]] = (o[h] / den[h]).astype(ACT_DTYPE)

    blk, first, prev = _swa_specs(group)
    outs = pl.pallas_call(
        body, name="swa_fwd", grid=(steps,),
        in_specs=[blk(512), first, prev, blk(128), first, prev, blk(128),
                  pl.BlockSpec(memory_space=pltpu.SMEM)] + [ANY] * ns,
        out_specs=[blk(512)] + [ANY] * ns,
        out_shape=[jax.ShapeDtypeStruct((rows, 512), ACT_DTYPE)] + _gathered_shapes(shards),
        scratch_shapes=_gather_sems(ns),
        compiler_params=_cp(("arbitrary",)),
    )(qr, kr, kr, kr, vr, vr, vr, sinks, *shards)
    return outs[0], _with_own_block(outs[1:], shards)


def _out_proj(x, lead, og, osw, wout, nfw, tm):
    rows = LEAD + x.shape[0]
    nb = tm // LEAD

    def body(*refs):
        x_refs, (lead_ref, og_ref, os_ref, w_ref, nw_ref, h1_ref, f_ref, ft_ref) = refs[:nb], refs[nb:]
        h0 = _h_tile(pl.program_id(0), lead_ref, x_refs)
        h1 = h0 + _mm(og_ref[...], w_ref[0:512, :]) + _mm(os_ref[...], w_ref[512:1024, :])
        h1_ref[...] = h1
        rstd = lax.rsqrt(jnp.mean(h1 * h1, axis=-1, keepdims=True) + EPS)
        f = h1 * rstd * nw_ref[...]
        f_ref[...] = f.astype(ACT_DTYPE)
        ft_ref[...] = f.T.astype(ACT_DTYPE)

    row = lambda w: pl.BlockSpec((tm, w), lambda i: (i, 0))
    return pl.pallas_call(
        body, name="out_proj", grid=(rows // tm,),
        in_specs=_token_specs(tm) + [pl.BlockSpec((LEAD, D), lambda i: (0, 0)), row(512), row(512),
                                     pl.BlockSpec((D, D), lambda i: (0, 0)), pl.BlockSpec((1, D), lambda i: (0, 0))],
        out_specs=[row(D), row(D), pl.BlockSpec((D, tm), lambda i: (0, i))],
        out_shape=[jax.ShapeDtypeStruct((rows, D), F32), jax.ShapeDtypeStruct((rows, D), ACT_DTYPE),
                   jax.ShapeDtypeStruct((D, rows), ACT_DTYPE)],
        compiler_params=_cp(("arbitrary",), 48),
    )(*([x] * nb), lead, og, osw, wout, nfw)


def _ffn_fwd(f, h1, w1, w2, tgt, fnw, tm):
    rows = f.shape[0]
    nj = D_FF // FF_WIDE
    nb = tm // LEAD

    def body(f_ref, h1_ref, w1_ref, w2_ref, nw_ref, *rest):
        t_refs, (a_ref, dh2_ref, dh2t_ref, loss_ref, gfn_ref, acc) = rest[:nb], rest[nb:]
        i, j = pl.program_id(0), pl.program_id(1)

        @pl.when((i == 0) & (j == 0))
        def _():
            loss_ref[...] = jnp.zeros_like(loss_ref)
            gfn_ref[...] = jnp.zeros_like(gfn_ref)

        @pl.when(j == 0)
        def _():
            acc[...] = jnp.zeros_like(acc)

        a = _mm(f_ref[...], w1_ref[...])
        a_ref[...] = a.astype(ACT_DTYPE)
        z = jnp.square(jnp.maximum(a, 0.0))
        acc[...] += _mm(z, w2_ref[...])

        @pl.when(j == nj - 1)
        def _():
            h2 = h1_ref[...] + acc[...]
            rstd = lax.rsqrt(jnp.mean(h2 * h2, axis=-1, keepdims=True) + EPS)
            hn = h2 * rstd
            nw = nw_ref[...]
            row = i * tm + lax.broadcasted_iota(jnp.int32, (tm, 1), 0)
            target = jnp.concatenate([t[...] for t in t_refs], axis=0)
            err = jnp.where(row >= LEAD, hn * nw - target, 0.0)
            row_loss = jnp.sum(err * err, axis=-1, keepdims=True) * (1.0 / D)
            loss_ref[...] += jnp.broadcast_to(0.5 * jnp.sum(row_loss, axis=0, keepdims=True), loss_ref.shape)
            dy = err * (1.0 / D)
            gfn_ref[...] += jnp.broadcast_to(jnp.sum(dy * hn, axis=0, keepdims=True), gfn_ref.shape)
            dhn = dy * nw
            dh2 = rstd * (dhn - hn * jnp.mean(dhn * hn, axis=-1, keepdims=True))
            dh2_ref[...] = dh2
            dh2t_ref[...] = dh2.T.astype(ACT_DTYPE)

    return pl.pallas_call(
        body, name="ffn_fwd", grid=(rows // tm, nj),
        in_specs=[pl.BlockSpec((tm, D), lambda i, j: (i, 0)), pl.BlockSpec((tm, D), lambda i, j: (i, 0)),
                  pl.BlockSpec((D, FF_WIDE), lambda i, j: (0, j)),
                  pl.BlockSpec((FF_WIDE, D), lambda i, j: (j, 0)),
                  pl.BlockSpec((1, D), lambda i, j: (0, 0))] + _token_specs(tm, grid_rank=2),
        out_specs=[pl.BlockSpec((tm, FF_WIDE), lambda i, j: (i, j)), pl.BlockSpec((tm, D), lambda i, j: (i, 0)),
                   pl.BlockSpec((D, tm), lambda i, j: (0, i)),
                   pl.BlockSpec((8, 128), lambda i, j: (0, 0)), pl.BlockSpec((8, D), lambda i, j: (0, 0))],
        out_shape=[jax.ShapeDtypeStruct((rows, D_FF), ACT_DTYPE), jax.ShapeDtypeStruct((rows, D), F32),
                   jax.ShapeDtypeStruct((D, rows), ACT_DTYPE),
                   jax.ShapeDtypeStruct((8, 128), F32), jax.ShapeDtypeStruct((8, D), F32)],
        scratch_shapes=[pltpu.VMEM((tm, D), F32)],
        compiler_params=_cp(("arbitrary", "arbitrary"), 56),
    )(f, h1, w1, w2, fnw, *([tgt] * nb))


def _ffn_bwd_act(dh2, a, w1, w2, h1, nfw, tm):
    rows = dh2.shape[0]
    nj = D_FF // FF_WIDE

    def body(dh2_ref, a_ref, w1_ref, w2_ref, h1_ref, nw_ref, da_ref, dh1_ref, gnf_ref, acc):
        i, j = pl.program_id(0), pl.program_id(1)

        @pl.when((i == 0) & (j == 0))
        def _():
            gnf_ref[...] = jnp.zeros_like(gnf_ref)

        @pl.when(j == 0)
        def _():
            acc[...] = jnp.zeros_like(acc)

        dz = _mm_nt(dh2_ref[...], w2_ref[...])
        da = dz * (2.0 * jnp.maximum(a_ref[...].astype(F32), 0.0))
        da_ref[...] = da.astype(ACT_DTYPE)
        acc[...] += _mm_nt(da, w1_ref[...])

        @pl.when(j == nj - 1)
        def _():
            h1 = h1_ref[...]
            rstd = lax.rsqrt(jnp.mean(h1 * h1, axis=-1, keepdims=True) + EPS)
            hn = h1 * rstd
            df = acc[...]
            gnf_ref[...] += jnp.broadcast_to(jnp.sum(df * hn, axis=0, keepdims=True), gnf_ref.shape)
            dfn = df * nw_ref[...]
            dh1_ref[...] = dh2_ref[...] + rstd * (dfn - hn * jnp.mean(dfn * hn, axis=-1, keepdims=True))

    return pl.pallas_call(
        body, name="ffn_bwd_act", grid=(rows // tm, nj),
        in_specs=[pl.BlockSpec((tm, D), lambda i, j: (i, 0)), pl.BlockSpec((tm, FF_WIDE), lambda i, j: (i, j)),
                  pl.BlockSpec((D, FF_WIDE), lambda i, j: (0, j)),
                  pl.BlockSpec((FF_WIDE, D), lambda i, j: (j, 0)),
                  pl.BlockSpec((tm, D), lambda i, j: (i, 0)), pl.BlockSpec((1, D), lambda i, j: (0, 0))],
        out_specs=[pl.BlockSpec((tm, FF_WIDE), lambda i, j: (i, j)), pl.BlockSpec((tm, D), lambda i, j: (i, 0)),
                   pl.BlockSpec((8, D), lambda i, j: (0, 0))],
        out_shape=[jax.ShapeDtypeStruct((rows, D_FF), ACT_DTYPE), jax.ShapeDtypeStruct((rows, D), F32),
                   jax.ShapeDtypeStruct((8, D), F32)],
        scratch_shapes=[pltpu.VMEM((tm, D), F32)],
        compiler_params=_cp(("arbitrary", "arbitrary"), 56),
    )(dh2, a, w1, w2, h1, nfw)


def _ffn_bwd_weights(ft, a, da, dh2t, tm):
    rows = a.shape[0]
    steps = rows // tm

    def body(ft_ref, a_ref, da_ref, dh2t_ref, dw1_ref, dw2_ref, dw2t):
        i = pl.program_id(1)

        @pl.when(i == 0)
        def _():
            dw1_ref[...] = jnp.zeros_like(dw1_ref)
            dw2t[...] = jnp.zeros_like(dw2t)

        z = jnp.square(jnp.maximum(a_ref[...].astype(F32), 0.0))
        dw1_ref[...] += _mm(ft_ref[...], da_ref[...])
        dw2t[...] += _mm(dh2t_ref[...], z)

        @pl.when(i == steps - 1)
        def _():
            dw2_ref[...] = dw2t[...].T

    return pl.pallas_call(
        body, name="ffn_bwd_weights", grid=(N_DEV, steps),
        in_specs=[pl.BlockSpec((D, tm), lambda j, i: (0, i)), pl.BlockSpec((tm, FF_TILE), lambda j, i: (i, j)),
                  pl.BlockSpec((tm, FF_TILE), lambda j, i: (i, j)), pl.BlockSpec((D, tm), lambda j, i: (0, i))],
        out_specs=[pl.BlockSpec((None, None, D, FF_TILE), lambda j, i: (j % 2, j // 2, 0, 0)),
                   pl.BlockSpec((None, None, FF_TILE, D), lambda j, i: (j % 2, j // 2, 0, 0))],
        out_shape=[jax.ShapeDtypeStruct((2, 4, D, FF_TILE), F32), jax.ShapeDtypeStruct((2, 4, FF_TILE, D), F32)],
        scratch_shapes=[pltpu.VMEM((D, FF_TILE), F32)],
        compiler_params=_cp(("arbitrary", "arbitrary"), 48),
    )(ft, a, da, dh2t)


def _out_proj_bwd(dh1, og, osw, wout, tm, partials):
    rows = dh1.shape[0]
    steps = rows // tm
    ns = len(partials)

    def body(dh1_ref, og_ref, os_ref, w_ref, *rest):
        part_refs, rest = rest[:ns], rest[ns:]
        dog_ref, dos_ref, dw_ref = rest[:3]
        land_refs, (send_sems, recv_sems) = rest[3:3 + ns], rest[3 + ns:]
        i = pl.program_id(0)
        start, finish = _sibling_schedule(part_refs, land_refs, send_sems, recv_sems)

        @pl.when(i == 0)
        def _():
            dw_ref[...] = jnp.zeros_like(dw_ref)
            start()

        pl.when(i == steps - 1)(finish)

        dh1 = dh1_ref[...].astype(MXU_DTYPE)
        dog_ref[...] = _mm_nt(dh1, w_ref[0:512, :])
        dos_ref[...] = _mm_nt(dh1, w_ref[512:1024, :])
        for half, ref in enumerate((og_ref, os_ref)):
            dw = _mm_tn(ref[...], dh1)
            for blk in range(4):
                shard = half * 4 + blk
                dw_ref[shard % 2, shard // 2] += dw[blk * 128:(blk + 1) * 128, :]

    row = lambda w: pl.BlockSpec((tm, w), lambda i: (i, 0))
    outs = pl.pallas_call(
        body, name="out_proj_bwd", grid=(steps,),
        in_specs=[row(D), row(512), row(512), pl.BlockSpec((D, D), lambda i: (0, 0))] + [ANY] * ns,
        out_specs=[row(512), row(512), pl.BlockSpec((2, 4, 128, D), lambda i: (0, 0, 0, 0))] + [ANY] * ns,
        out_shape=[jax.ShapeDtypeStruct((rows, 512), F32), jax.ShapeDtypeStruct((rows, 512), F32),
                   jax.ShapeDtypeStruct((2, 4, 128, D), F32)] + _sibling_shapes(partials),
        scratch_shapes=_sibling_sems(ns),
        compiler_params=_cp(("arbitrary",), 48),
    )(dh1, og, osw, wout, *partials)
    return outs[0], outs[1], outs[2], outs[3:]


def _swa_bwd(qr, kr, vr, osw, dos, sinks, jobs):
    rows = qr.shape[0]
    nblk = rows // SWA_BLOCK
    group = _swa_group(nblk)
    steps = nblk // group
    ns = jobs.n

    def body(q_ref, k0, kp, kc, v0, vp, vc, o_ref, do_ref, sink_ref, *rest):
        dq_ref, dk_ref, dv_ref, dsink_ref = rest[ns:ns + 4]
        start, finish = jobs.bind(rest[:ns], rest[ns + 4:2 * ns + 4], rest[2 * ns + 4:])
        step = pl.program_id(0)

        @pl.when(step == 0)
        def _():
            dk_ref[...] = jnp.zeros_like(dk_ref)
            dv_ref[...] = jnp.zeros_like(dv_ref)
            dsink_ref[...] = jnp.zeros_like(dsink_ref)
            start()

        pl.when(step == steps - 1)(finish)
        for g in range(group):
            block(step * group + g, g, q_ref, k0, kp, kc, v0, vp, vc, o_ref, do_ref, sink_ref,
                  dq_ref, dk_ref, dv_ref, dsink_ref)

    def block(n, g, q_ref, k0, kp, kc, v0, vp, vc, o_ref, do_ref, sink_ref, dq_ref, dk_ref, dv_ref, dsink_ref):
        rs = slice(g * SWA_BLOCK, (g + 1) * SWA_BLOCK)
        kall, vall = _swa_keys(k0, kp, kc, g), _swa_keys(v0, vp, vc, g)
        mask = _swa_mask(n)[0:SWA_BLOCK]
        heads = range(SWA_HEADS)
        hs = [slice(h * SWA_HD, (h + 1) * SWA_HD) for h in heads]
        kv = [slice((h // SWA_GROUP) * SWA_HD, (h // SWA_GROUP + 1) * SWA_HD) for h in heads]
        sink = [sink_ref[0, h] for h in heads]
        qh = [q_ref[rs, hs[h]] for h in heads]
        doh = [do_ref[rs, hs[h]] for h in heads]
        s = [jnp.where(mask, _mm_nt(qh[h], kall[:, kv[h]]), NEG) for h in heads]
        dp = [_mm_nt(doh[h], vall[:, kv[h]]) for h in heads]
        delta = [jnp.sum(doh[h] * o_ref[rs, hs[h]].astype(F32), axis=-1, keepdims=True) for h in heads]
        m = [jnp.maximum(jnp.max(s[h], axis=-1, keepdims=True), sink[h]) for h in heads]
        e = [jnp.exp(s[h] - m[h]) for h in heads]
        inv = [1.0 / (jnp.sum(e[h], axis=-1, keepdims=True) + jnp.exp(sink[h] - m[h])) for h in heads]
        p = [e[h] * inv[h] for h in heads]
        ds = [p[h] * (dp[h] - delta[h]) for h in heads]
        dq = [_mm(ds[h], kall[:, kv[h]]) for h in heads]
        dkh = [_mm_tn(ds[h], qh[h]) for h in heads]
        dvh = [_mm_tn(p[h], doh[h]) for h in heads]
        for h in heads:
            dsink = -jnp.sum(jnp.exp(sink[h] - m[h]) * inv[h] * delta[h], axis=0, keepdims=True)
            dsink_ref[h:h + 1, :] += jnp.broadcast_to(dsink, (1, 128))
        dq_ref[rs, :] = jnp.concatenate(dq, axis=1)
        group_sum = lambda parts, kvh: sum(parts[kvh * SWA_GROUP + 1:(kvh + 1) * SWA_GROUP], parts[kvh * SWA_GROUP])
        dk_all = jnp.concatenate([group_sum(dkh, kvh) for kvh in range(SWA_KV)], axis=1)
        dv_all = jnp.concatenate([group_sum(dvh, kvh) for kvh in range(SWA_KV)], axis=1)
        prev0 = pl.multiple_of(jnp.maximum(n - 1, 0) * SWA_BLOCK, SWA_BLOCK)
        cur0 = pl.multiple_of(n * SWA_BLOCK, SWA_BLOCK)
        for ref, val in ((dk_ref, dk_all), (dv_ref, dv_all)):
            ref[0:SWA_BLOCK, :] += val[0:SWA_BLOCK]
            ref[pl.ds(prev0, SWA_BLOCK), :] += val[SWA_BLOCK:2 * SWA_BLOCK]
            ref[pl.ds(cur0, SWA_BLOCK), :] += val[2 * SWA_BLOCK:]

    blk, first, prev = _swa_specs(group)
    whole = pl.BlockSpec((rows, 128), lambda n: (0, 0))
    outs = pl.pallas_call(
        body, name="swa_bwd", grid=(steps,),
        in_specs=[blk(512), first, prev, blk(128), first, prev, blk(128), blk(512), blk(512),
                  pl.BlockSpec(memory_space=pltpu.SMEM)] + [ANY] * ns,
        out_specs=[blk(512), whole, whole, pl.BlockSpec((8, 128), lambda n: (0, 0))] + [ANY] * ns,
        out_shape=[jax.ShapeDtypeStruct((rows, 512), F32), jax.ShapeDtypeStruct((rows, 128), F32),
                   jax.ShapeDtypeStruct((rows, 128), F32), jax.ShapeDtypeStruct((8, 128), F32)] + jobs.out_shapes,
        scratch_shapes=jobs.sems,
        compiler_params=_cp(("arbitrary",), 48),
    )(qr, kr, kr, kr, vr, vr, vr, osw, dos, sinks, *jobs.inputs)
    return outs[0], outs[1], outs[2], outs[3], jobs.split(outs[4:])


def _gla_bwd(proj, oraw, states, dog, wg_p, bg, gnw, jobs):
    rows = proj.shape[0]
    nc = rows // GLA_CHUNK
    group = _gla_group(nc)
    steps, nrows = nc // group, group * GLA_CHUNK
    ns = jobs.n

    def body(q_ref, k_ref, v_ref, r_ref, lr_ref, oraw_ref, st_ref, dog_ref, wg_ref, bg_ref, gnw_ref, *rest):
        dq_ref, dk_ref, dv_ref, dr_ref, dlr_ref, dwg_ref, dbg_ref, dgnw_ref = rest[ns:ns + 8]
        dstate, db_scr = rest[2 * ns + 8:2 * ns + 10]
        start, finish = jobs.bind(rest[:ns], rest[ns + 8:2 * ns + 8], rest[2 * ns + 10:])
        t = pl.program_id(0)
        c = steps - 1 - t

        @pl.when(t == 0)
        def _():
            dstate[...] = jnp.zeros_like(dstate)
            dwg_ref[...] = jnp.zeros_like(dwg_ref)
            dbg_ref[...] = jnp.zeros_like(dbg_ref)
            dgnw_ref[...] = jnp.zeros_like(dgnw_ref)
            start()

        pl.when(t == steps - 1)(finish)

        lr, wg = lr_ref[...], wg_ref[...]
        zg, live, _, upper, b = _gla_gates(lr, wg, bg_ref[...], c * nrows, nrows)
        eb, enb = jnp.exp(b), jnp.exp(-b)
        scale = GLA_DK ** -0.5
        gq = q_ref[...] * scale * eb
        gk = k_ref[...] * enb
        v = v_ref[...]
        gnw_v = gnw_ref[...]
        tril = _tril64()
        is_last = lax.broadcasted_iota(jnp.int32, (GLA_CHUNK, 1), 0) == GLA_CHUNK - 1
        dgnw = jnp.zeros((1, GLA_DV), F32)
        pairs = [(h, gi) for h in range(GLA_HEADS) for gi in range(group)]
        rs = {gi: slice(gi * GLA_CHUNK, (gi + 1) * GLA_CHUNK) for gi in range(group)}
        s64 = {h: slice(h * GLA_DK, (h + 1) * GLA_DK) for h in range(GLA_HEADS)}
        s128 = {h: slice(h * GLA_DV, (h + 1) * GLA_DV) for h in range(GLA_HEADS)}
        qh = {(h, gi): gq[rs[gi], s64[h]] for h, gi in pairs}
        kh = {(h, gi): gk[rs[gi], s64[h]] for h, gi in pairs}
        vh = {(h, gi): v[rs[gi], s128[h]] for h, gi in pairs}
        ebl = {(h, gi): eb[(gi + 1) * GLA_CHUNK - 1:(gi + 1) * GLA_CHUNK, s64[h]] for h, gi in pairs}
        kl = {pr: kh[pr] * ebl[pr] for pr in pairs}
        st = {(h, gi): st_ref[gi, h] for h, gi in pairs}
        do = {}
        for h, gi in pairs:
            o, rh, dout = oraw_ref[rs[gi], s128[h]], r_ref[rs[gi], s128[h]], dog_ref[rs[gi], s128[h]]
            rstd = lax.rsqrt(jnp.mean(o * o, axis=-1, keepdims=True) + EPS)
            on = o * rstd
            sg = _sigmoid(rh)
            dr_ref[rs[gi], s128[h]] = (dout * (on * gnw_v) * (sg * (1.0 + rh * (1.0 - sg)))).astype(ACT_DTYPE)
            dy = dout * (rh * sg)
            dgnw = dgnw + jnp.sum(dy * on, axis=0, keepdims=True)
            don = dy * gnw_v
            do[h, gi] = rstd * (don - on * jnp.mean(don * on, axis=-1, keepdims=True))
        a = {pr: jnp.where(tril, _mm_nt(qh[pr], kh[pr]), 0.0) for pr in pairs}
        da = {pr: jnp.where(tril, _mm_nt(do[pr], vh[pr]), 0.0) for pr in pairs}
        dinc = {pr: _mm_tn(do[pr], qh[pr]) for pr in pairs}
        dgq = {pr: _mm(da[pr], kh[pr]) + _mm(do[pr], st[pr]) for pr in pairs}
        dgk = {pr: _mm_tn(da[pr], qh[pr]) for pr in pairs}
        dv_a = {pr: _mm_tn(a[pr], do[pr]) for pr in pairs}
        dsp = {}
        for h in range(GLA_HEADS):
            cur = dstate[h]
            for gi in reversed(range(group)):
                dsp[h, gi] = cur
                cur = cur * ebl[h, gi] + dinc[h, gi]
            dstate[h] = cur
        for h, gi in pairs:
            pr = (h, gi)
            dkl = _mm(vh[pr], dsp[pr])
            dv_ref[rs[gi], s128[h]] = (dv_a[pr] + _mm_nt(kl[pr], dsp[pr])).astype(ACT_DTYPE)
            debl = jnp.sum(dsp[pr] * st[pr], axis=0, keepdims=True)
            dq_ref[rs[gi], s64[h]] = (dgq[pr] * (scale * eb[rs[gi], s64[h]])).astype(ACT_DTYPE)
            dk_ref[rs[gi], s64[h]] = ((dgk[pr] + dkl * ebl[pr]) * enb[rs[gi], s64[h]]).astype(ACT_DTYPE)
            last = debl * ebl[pr] + jnp.sum(dkl * kl[pr], axis=0, keepdims=True)
            db_scr[rs[gi], s64[h]] = (dgq[pr] * qh[pr] - dgk[pr] * kh[pr] - dkl * kl[pr]
                                      + jnp.where(is_last, last, 0.0))
        dg = jnp.dot(upper.astype(F32), db_scr[...], precision=HIGHEST, preferred_element_type=F32)
        dzg = jnp.where(live, dg * _sigmoid(-zg) * (1.0 / GLA_TAU), 0.0)
        dlr_ref[...] = _mm_nt(dzg, wg).astype(ACT_DTYPE)
        dwg_ref[...] += _mm_tn(lr, dzg)
        dbg_ref[...] += jnp.broadcast_to(jnp.sum(dzg, axis=0, keepdims=True), dbg_ref.shape)
        dgnw_ref[...] += jnp.broadcast_to(dgnw, dgnw_ref.shape)

    nb = lambda w, col: pl.BlockSpec((nrows, w), lambda t: (steps - 1 - t, col // w))
    const = lambda shape: pl.BlockSpec(shape, lambda t: (0,) * len(shape))
    outs = pl.pallas_call(
        body, name="gla_bwd", grid=(steps,),
        in_specs=[nb(256, C_GQ), nb(256, C_GK), nb(512, C_GV), nb(512, C_GR), nb(128, C_LR), nb(512, 0),
                  pl.BlockSpec((group, GLA_HEADS, GLA_DV, GLA_DK), lambda t: (steps - 1 - t, 0, 0, 0)), nb(512, 0),
                  const((128, 256)), const((1, 256)), const((1, 128))] + [ANY] * ns,
        out_specs=[nb(256, 0), nb(256, 0), nb(512, 0), nb(512, 0), nb(128, 0),
                   const((128, 256)), const((8, 256)), const((8, 128))] + [ANY] * ns,
        out_shape=[jax.ShapeDtypeStruct((rows, 256), ACT_DTYPE), jax.ShapeDtypeStruct((rows, 256), ACT_DTYPE),
                   jax.ShapeDtypeStruct((rows, 512), ACT_DTYPE), jax.ShapeDtypeStruct((rows, 512), ACT_DTYPE),
                   jax.ShapeDtypeStruct((rows, 128), ACT_DTYPE), jax.ShapeDtypeStruct((128, 256), F32),
                   jax.ShapeDtypeStruct((8, 256), F32), jax.ShapeDtypeStruct((8, 128), F32)] + jobs.out_shapes,
        scratch_shapes=[pltpu.VMEM((GLA_HEADS, GLA_DV, GLA_DK), F32), pltpu.VMEM((nrows, 256), F32)] + jobs.sems,
        compiler_params=_cp(("arbitrary",)),
    )(proj, proj, proj, proj, proj, oraw, states, dog, wg_p, bg, gnw, *jobs.inputs)
    return outs[:8], jobs.split(outs[8:])


def _in_proj_bwd(x, lead, dh1, nw, win_p, dgv, dgr, dsq, dgq, dgk, dsk, dsv, dlr, tabs, tm):
    seq = x.shape[0]
    rows = LEAD + seq
    nb = tm // LEAD
    steps = rows // tm

    def first_copy(scr, gx_ref, sem):
        return pltpu.make_async_copy(scr.at[pl.ds(LEAD, tm - LEAD)], gx_ref.at[pl.ds(0, tm - LEAD)], sem)

    def tile_copy(scr, gx_ref, sem, step):
        start = pl.multiple_of(jnp.maximum(step * tm - LEAD, 0), LEAD)
        return pltpu.make_async_copy(scr, gx_ref.at[pl.ds(start, tm)], sem)

    def body(*refs):
        x_refs, refs = refs[:nb], refs[nb:]
        (lead_ref, dh1_ref, nw_ref, w_ref, dgv_ref, dgr_ref, dsq_ref, dgq_ref, dgk_ref, dsk_ref, dsv_ref, dlr_ref,
         c_ref, sa_ref, sb_ref, gx_ref, dlead_ref, dproj_ref, ut_ref, gnm_ref, scr, sem) = refs
        i = pl.program_id(0)

        @pl.when(i == 0)
        def _():
            gnm_ref[...] = jnp.zeros_like(gnm_ref)

        cos, sa, sb = c_ref[...], sa_ref[...], sb_ref[...]
        dsq_v = (_unrope(dsq_ref[...], cos, sa, sb) * (SWA_HD ** -0.5)).astype(MXU_DTYPE)
        dsk_v = _unrope(dsk_ref[...], cos, sa, sb).astype(MXU_DTYPE)
        dproj = jnp.concatenate(
            [dgv_ref[...].astype(MXU_DTYPE), dgr_ref[...].astype(MXU_DTYPE), dsq_v, dgq_ref[...].astype(MXU_DTYPE),
             dgk_ref[...].astype(MXU_DTYPE), dsk_v, dsv_ref[...].astype(MXU_DTYPE), dlr_ref[...].astype(MXU_DTYPE)],
            axis=1)
        dproj_ref[...] = dproj
        h = _h_tile(i, lead_ref, x_refs)
        rstd = lax.rsqrt(jnp.mean(h * h, axis=-1, keepdims=True) + EPS)
        hn = h * rstd
        nw_v = nw_ref[...]
        ut_ref[...] = (hn * nw_v).T.astype(ACT_DTYPE)
        du = _mm_nt(dproj, w_ref[...])
        gnm_ref[...] += jnp.broadcast_to(jnp.sum(du * hn, axis=0, keepdims=True), gnm_ref.shape)
        dun = du * nw_v
        dh0 = dh1_ref[...] + rstd * (dun - hn * jnp.mean(dun * hn, axis=-1, keepdims=True))

        if tm > LEAD:
            pl.when(i == 1)(lambda: first_copy(scr, gx_ref, sem).wait())
        pl.when(i > 1)(lambda: tile_copy(scr, gx_ref, sem, i).wait())
        scr[...] = dh0

        @pl.when(i == 0)
        def _():
            dlead_ref[...] = dh0[0:LEAD]
            if tm > LEAD:
                first_copy(scr, gx_ref, sem).start()
                if steps == 1:
                    first_copy(scr, gx_ref, sem).wait()

        @pl.when(i > 0)
        def _():
            tile_copy(scr, gx_ref, sem, i).start()

        if steps > 1:
            pl.when(i == steps - 1)(lambda: tile_copy(scr, gx_ref, sem, i).wait())

    row = lambda w: pl.BlockSpec((tm, w), lambda i: (i, 0))
    const = lambda shape: pl.BlockSpec(shape, lambda i: (0,) * len(shape))
    return pl.pallas_call(
        body, name="in_proj_bwd", grid=(steps,),
        in_specs=_token_specs(tm) + [const((LEAD, D)), row(D), const((1, D)), const((D, DINP)),
                                     row(512), row(512), row(512), row(256), row(256), row(128), row(128), row(128),
                                     row(128), row(128), row(128)],
        out_specs=[ANY, const((LEAD, D)), row(DINP), pl.BlockSpec((D, tm), lambda i: (0, i)), const((8, D))],
        out_shape=[jax.ShapeDtypeStruct((seq, D), F32), jax.ShapeDtypeStruct((LEAD, D), F32),
                   jax.ShapeDtypeStruct((rows, DINP), ACT_DTYPE), jax.ShapeDtypeStruct((D, rows), ACT_DTYPE),
                   jax.ShapeDtypeStruct((8, D), F32)],
        scratch_shapes=[pltpu.VMEM((tm, D), F32), pltpu.SemaphoreType.DMA],
        compiler_params=_cp(("arbitrary",), 56),
    )(*([x] * nb), lead, dh1, nw, win_p, dgv, dgr, dsq, dgq, dgk, dsk, dsv, dlr, *tabs)


def _win_runs():
    groups = [(O_GQ, C_GQ), (O_GK, C_GK), (O_GV, C_GV), (O_GR, C_GR), (O_LR, C_LR), (O_SQ, C_SQ), (O_SK, C_SK),
              (O_SV, C_SV)]
    per = DIN // N_DEV
    runs = []
    for (o0, o1), c0 in groups:
        o = o0
        while o < o1:
            d = o // per
            end = min(o1, (d + 1) * per)
            runs.append((d, o - d * per, c0 + o - o0, end - o))
            o = end
    return runs


def _win_padded(g_in):
    tr = 128

    def body(g_ref, o_ref):
        o_ref[...] = jnp.zeros_like(o_ref)
        for d, s, c, w in _win_runs():
            o_ref[:, c:c + w] = g_ref[d, :, s:s + w]

    return pl.pallas_call(
        body, name="w_in_layout", grid=(D // tr,),
        in_specs=[pl.BlockSpec((N_DEV, tr, DIN // N_DEV), lambda i: (0, i, 0))],
        out_specs=pl.BlockSpec((tr, DINP), lambda i: (i, 0)),
        out_shape=jax.ShapeDtypeStruct((D, DINP), g_in.dtype),
        compiler_params=_cp(("arbitrary",)),
    )(g_in)


def _in_proj_bwd_weights(ut, dproj, tm):
    rows = dproj.shape[0]
    steps = rows // tm
    per = DIN // N_DEV

    def body(ut_ref, dp_ref, out_ref, acc, stage, sems):
        i = pl.program_id(0)

        @pl.when(i == 0)
        def _():
            acc[...] = jnp.zeros_like(acc)

        acc[...] += _mm(ut_ref[...], dp_ref[...])

        @pl.when(i == steps - 1)
        def _():
            copies = []
            for d in range(N_DEV):
                slot = d % 2
                if d >= 2:
                    copies[d - 2].wait()
                for owner, s, c, w in _win_runs():
                    if owner == d:
                        stage[slot, :, s:s + w] = acc[:, c:c + w]
                cp = pltpu.make_async_copy(stage.at[slot], out_ref.at[d % 2, d // 2], sems.at[slot])
                cp.start()
                copies.append(cp)
            copies[N_DEV - 2].wait()
            copies[N_DEV - 1].wait()

    return pl.pallas_call(
        body, name="in_proj_bwd_weights", grid=(steps,),
        in_specs=[pl.BlockSpec((D, tm), lambda i: (0, i)), pl.BlockSpec((tm, DINP), lambda i: (i, 0))],
        out_specs=ANY,
        out_shape=jax.ShapeDtypeStruct((2, 4, D, per), F32),
        scratch_shapes=[pltpu.VMEM((D, DINP), F32), pltpu.VMEM((2, D, per), F32), pltpu.SemaphoreType.DMA((2,))],
        compiler_params=_cp(("arbitrary",), 56),
    )(ut, dproj)


def _adamw(w, g, m, v):
    m = ADAM_B1 * m + (1.0 - ADAM_B1) * g
    v = ADAM_B2 * v + (1.0 - ADAM_B2) * jnp.square(g)
    m_hat = m / (1.0 - ADAM_B1 ** ADAM_STEP)
    v_hat = v / (1.0 - ADAM_B2 ** ADAM_STEP)
    delta = -ADAM_LR * (m_hat / (jnp.sqrt(v_hat) + ADAM_EPS) + ADAM_WD * w)
    return delta, m, v


ADAM_STEPS = 8


def _adamw_shards(where, items, name, jobs=None):
    jobs = jobs or _Jobs([])
    ns, nw = jobs.n, len(items)

    def body(where_ref, *rest):
        ins, rest = rest[:5 * nw], rest[5 * nw:]
        job_ins, rest = rest[:ns], rest[ns:]
        outs, rest = rest[:4 * nw], rest[4 * nw:]
        start, finish = jobs.bind(job_ins, rest[:ns], rest[ns:])
        i = pl.program_id(0)
        pl.when(i == 0)(start)
        pl.when(i == ADAM_STEPS - 1)(finish)
        for k in range(nw):
            p_ref, own_ref, w_ref, m_ref, v_ref = ins[5 * k:5 * k + 5]
            g_ref, d_ref, nm_ref, nv_ref = outs[4 * k:4 * k + 4]
            g = ((p_ref[0].astype(F32) + p_ref[1].astype(F32)) + p_ref[2].astype(F32)) + own_ref[...]
            g_ref[...] = g
            d_ref[...], nm_ref[...], nv_ref[...] = _adamw(w_ref[...], g, m_ref[...], v_ref[...])

    in_specs, out_specs, out_shape, operands = [], [], [], []
    for parts, own, w, m, v in items:
        r, cdim = w.shape
        tr = r // ADAM_STEPS
        spec = pl.BlockSpec((tr, cdim), lambda i, s: (i, 0))
        in_specs += [pl.BlockSpec((3, tr, cdim), lambda i, s: (0, i, 0)),
                     pl.BlockSpec((None, tr, cdim), lambda i, s: (s[1], i, 0)), spec, spec, spec]
        out_specs += [spec] * 4
        out_shape += [jax.ShapeDtypeStruct((r, cdim), F32)] * 4
        operands += [parts, own, w, m, v]
    outs = pl.pallas_call(
        body, name=name,
        grid_spec=pltpu.PrefetchScalarGridSpec(
            num_scalar_prefetch=1, grid=(ADAM_STEPS,),
            in_specs=in_specs + [ANY] * ns, out_specs=out_specs + [ANY] * ns, scratch_shapes=jobs.sems),
        out_shape=out_shape + jobs.out_shapes,
        compiler_params=_cp(("arbitrary",)),
    )(where, *operands, *jobs.inputs)
    return [outs[4 * k:4 * k + 4] for k in range(nw)], jobs.split(outs[4 * nw:])


def _adamw_small(items):
    n = len(items)

    def body(*refs):
        ins, outs = refs[:4 * n], refs[4 * n:]
        for k in range(n):
            w_ref, g_ref, m_ref, v_ref = ins[4 * k:4 * k + 4]
            d_ref, nm_ref, nv_ref = outs[3 * k:3 * k + 3]
            d_ref[...], nm_ref[...], nv_ref[...] = _adamw(w_ref[...], g_ref[...], m_ref[...], v_ref[...])

    vm = pl.BlockSpec(memory_space=pltpu.VMEM)
    shapes = [jax.ShapeDtypeStruct(w.shape, F32) for w, _, _, _ in items for _ in range(3)]
    outs = pl.pallas_call(body, name="adamw_small", in_specs=[vm] * (4 * n), out_specs=[vm] * (3 * n),
                          out_shape=shapes)(*[t for item in items for t in item])
    return [outs[3 * k:3 * k + 3] for k in range(n)]


def _add_own_half(where, full, theirs, name, wire_copy=False):
    _, _, r, cdim = full.shape
    tr = 128 if r % 128 == 0 else r

    def body(where_ref, a_ref, b_ref, *o_refs):
        total = a_ref[...] + b_ref[...]
        o_refs[0][...] = total
        if wire_copy:
            o_refs[1][...] = total.astype(WIRE_DTYPE)

    spec = pl.BlockSpec((4, tr, cdim), lambda i, s: (0, i, 0))
    shapes = [jax.ShapeDtypeStruct(theirs.shape, F32)] + ([jax.ShapeDtypeStruct(theirs.shape, WIRE_DTYPE)] if wire_copy else [])
    outs = pl.pallas_call(
        body, name=name,
        grid_spec=pltpu.PrefetchScalarGridSpec(
            num_scalar_prefetch=1, grid=(r // tr,),
            in_specs=[pl.BlockSpec((None, 4, tr, cdim), lambda i, s: (s[0], 0, i, 0)), spec],
            out_specs=[spec] * len(shapes)),
        out_shape=shapes, compiler_params=_cp(("arbitrary",)))(where, full, theirs)
    return outs if wire_copy else outs[0]


def kernel(x, meta_tokens, norm_mix_w, w_in, w_gate_up, b_gate, gla_norm_w, sinks, w_out, norm_ff_w, w_ff1, w_ff2, final_norm_w, loss_target, m_meta_tokens, m_norm_mix_w, m_w_in, m_w_gate_up, m_b_gate, m_gla_norm_w, m_sinks, m_w_out, m_norm_ff_w, m_w_ff1, m_w_ff2, m_final_norm_w, v_meta_tokens, v_norm_mix_w, v_w_in, v_w_gate_up, v_b_gate, v_gla_norm_w, v_sinks, v_w_out, v_norm_ff_w, v_w_ff1, v_w_ff2, v_final_norm_w):
    seq = x.shape[1]
    rows = LEAD + seq
    tm = _row_tile(rows)
    tm_wide = 1664 if rows % 1664 == 0 else tm
    dev =4 * lax.axis_index("x") + 2 * lax.axis_index("y") + lax.axis_index("c")

    small_shard = jnp.concatenate([meta_tokens, w_gate_up[0], jnp.zeros((N_META, 96), F32)], axis=1)
    g_in, g_small = _all_gather([w_in[0].astype(WIRE_DTYPE), small_shard])
    later_shards = [w_out[0].astype(WIRE_DTYPE), w_ff1[0].astype(WIRE_DTYPE), w_ff2[0].astype(WIRE_DTYPE)]
    win_p = _win_padded(g_in)
    meta_full = jnp.transpose(g_small[:, :, 0:128], (1, 0, 2)).reshape(N_META, D)
    wg_full = jnp.transpose(g_small[:, :, 128:160], (1, 0, 2)).reshape(GLA_RANK, GLA_HEADS * GLA_DK)
    wg_p = jnp.concatenate([wg_full, jnp.zeros((128 - GLA_RANK, 256), F32)], axis=0)

    lead = jnp.concatenate([jnp.zeros((META0, D), F32), meta_full], axis=0)
    tabs = _rope_tables(rows)
    proj = _in_proj(x[0], lead, norm_mix_w, win_p, tm)
    oraw, og, states, (g_out, g_w1) = _gla_fwd(proj, wg_p, b_gate, gla_norm_w, later_shards[0:2])
    qr, kr, vr = _swa_prep(proj, tabs, tm)
    osw, (g_w2,) = _swa_fwd(qr, kr, vr, sinks, later_shards[2:3])
    wout_full = g_out.reshape(D, D)
    w2_full = g_w2.reshape(D_FF, D)
    w1_full = jnp.transpose(g_w1, (1, 0, 2)).reshape(D, D_FF)
    h1, f, ft = _out_proj(x[0], lead, og, osw, wout_full, norm_ff_w, tm)
    a, dh2, dh2t, loss_p, gfn_p = _ffn_fwd(f, h1, w1_full, w2_full, loss_target[0], final_norm_w.reshape(1, D), tm)

    da, dh1, gnf_p = _ffn_bwd_act(dh2, a, w1_full, w2_full, h1, norm_ff_w, tm)
    dw1, dw2 = _ffn_bwd_weights(ft, a, da, dh2t, tm_wide)
    where = jnp.stack([lax.axis_index("c"), 2 * lax.axis_index("x") + lax.axis_index("y")]).astype(jnp.int32)
    dog, dos, dwout, theirs_ffn = _out_proj_bwd(dh1, og, osw, wout_full, tm, [dw1, dw2])
    sums_ffn = [_add_own_half(where, p, q, "reduce_pair_%d" % (2 + k))
                for k, (p, q) in enumerate(zip([dw1, dw2], theirs_ffn))]
    dsq, dsk, dsv, dsink_p, (parts_ffn, (theirs_wout,)) = _swa_bwd(
        qr, kr, vr, osw, dos, sinks, _Jobs([("chips", sums_ffn), ("sibling", [dwout])]))
    sum_wout = _add_own_half(where, dwout, theirs_wout, "reduce_pair_1")
    (dgq, dgk, dgv, dgr, dlr, dwg_p, dbg_p, dgnw_p), ((parts_wout,),) = _gla_bwd(
        proj, oraw, states, dog, wg_p, b_gate, gla_norm_w, _Jobs([("chips", [sum_wout])]))
    grad_x, dlead, dproj, ut, gnm_p = _in_proj_bwd(x[0], lead, dh1, norm_mix_w, win_p, dgv, dgr, dsq, dgq, dgk, dsk,
                                                   dsv, dlr, tabs, tm)
    grad_x = grad_x[None]
    dwin = _in_proj_bwd_weights(ut, dproj, tm_wide)

    (theirs_win,) = _rs_sibling([dwin])
    sum_win, sum_win_wire = _add_own_half(where, dwin, theirs_win, "reduce_pair_0", wire_copy=True)

    total = _all_reduce_small(dlead, dwg_p, gnm_p, gnf_p, gfn_p, dbg_p, dgnw_p, loss_p, dsink_p)
    g_meta = lax.dynamic_slice(total, (R_META, dev * 128), (N_META, 128))
    g_wg = lax.dynamic_slice(total, (R_WG, dev * 32), (GLA_RANK, 32))
    g_norm_mix, g_norm_ff = total[R_NORM_MIX:R_NORM_MIX + 1], total[R_NORM_FF:R_NORM_FF + 1]
    g_final_norm = total[R_FINAL:R_FINAL + 1]
    g_b_gate, g_gla_norm = total[R_B_GATE:R_B_GATE + 1, 0:256], total[R_GLA_NORM:R_GLA_NORM + 1, 0:128]
    g_sinks = total[R_SINKS:R_SINKS + SWA_HEADS, 0].reshape(1, SWA_HEADS)
    loss = total[R_LOSS, 0]

    ((g_wout, d_wout, nm_wout, nv_wout), (g_w1s, d_w1, nm_w1, nv_w1), (g_w2s, d_w2, nm_w2, nv_w2)), ((parts_win,),) = \
        _adamw_shards(where, [(parts_wout, sum_wout, w_out[0], m_w_out[0], v_w_out[0]),
                              (parts_ffn[0], sums_ffn[0], w_ff1[0], m_w_ff1[0], v_w_ff1[0]),
                              (parts_ffn[1], sums_ffn[1], w_ff2[0], m_w_ff2[0], v_w_ff2[0])],
                      "adamw_w_out_ff", _Jobs([("chips", [sum_win_wire])]))
    ((g_win, d_win, nm_win, nv_win),), _ = _adamw_shards(
        where, [(parts_win, sum_win, w_in[0], m_w_in[0], v_w_in[0])], "adamw_w_in")

    names = ["meta", "wg", "norm_mix", "b_gate", "gla_norm", "sinks", "norm_ff", "final_norm"]
    ws = [meta_tokens, w_gate_up, norm_mix_w, b_gate, gla_norm_w, sinks, norm_ff_w, final_norm_w]
    gs = [g_meta, g_wg, g_norm_mix, g_b_gate, g_gla_norm, g_sinks, g_norm_ff, g_final_norm]
    ms = [m_meta_tokens, m_w_gate_up, m_norm_mix_w, m_b_gate, m_gla_norm_w, m_sinks, m_norm_ff_w, m_final_norm_w]
    vs = [v_meta_tokens, v_w_gate_up, v_norm_mix_w, v_b_gate, v_gla_norm_w, v_sinks, v_norm_ff_w, v_final_norm_w]
    flat = lambda t: t.reshape(-1, t.shape[-1])
    small_out = _adamw_small([(flat(w), flat(g), flat(m), flat(v)) for w, g, m, v in zip(ws, gs, ms, vs)])
    d_small = {n: small_out[k][0].reshape(ws[k].shape) for k, n in enumerate(names)}
    nm_small = {n: small_out[k][1].reshape(ws[k].shape) for k, n in enumerate(names)}
    nv_small = {n: small_out[k][2].reshape(ws[k].shape) for k, n in enumerate(names)}
    g_small_d = {n: g.reshape(ws[k].shape) for k, (n, g) in enumerate(zip(names, gs))}

    def ordered(big, small_d):
        win_v, wout_v, w1_v, w2_v = big
        return (small_d["meta"], small_d["norm_mix"], win_v[None], small_d["wg"], small_d["b_gate"],
                small_d["gla_norm"], small_d["sinks"], wout_v[None], small_d["norm_ff"], w1_v[None], w2_v[None],
                small_d["final_norm"])

    return (loss, grad_x,
            *ordered((g_win, g_wout, g_w1s, g_w2s), g_small_d),
            *ordered((d_win, d_wout, d_w1, d_w2), d_small),
            *ordered((nm_win, nm_wout, nm_w1, nm_w2), nm_small),
            *ordered((nv_win, nv_wout, nv_w1, nv_w2), nv_small))
```

```python
import functools

import jax
import jax.numpy as jnp
from jax import lax
from jax.experimental import pallas as pl
from jax.experimental.pallas import tpu as pltpu

F32 = jnp.float32
MXU_DTYPE = jnp.bfloat16
ACT_DTYPE = jnp.bfloat16
WIRE_DTYPE = jnp.bfloat16

D = 1024
N_META = 16
LEAD = 128
META0 = LEAD - N_META
EPS = 1e-5
GLA_HEADS, GLA_DK, GLA_DV, GLA_RANK, GLA_CHUNK = 4, 64, 128, 16, 64
GLA_TAU = 16.0
SWA_HEADS, SWA_KV, SWA_GROUP, SWA_HD, SWA_BLOCK = 8, 2, 4, 64, 128
ROPE_DIM, ROPE_THETA = 16, 500000.0
D_FF = 4096
N_DEV = 8
FF_TILE = D_FF // N_DEV
FF_WIDE = 2048
NEG = -1e30

C_GV, C_GR, C_SQ, C_GQ, C_GK, C_SK, C_SV, C_LR = 0, 512, 1024, 1536, 1792, 2048, 2176, 2304
DINP = 2432
DIN = 2320
O_GQ, O_GK, O_GV, O_GR, O_LR, O_SQ, O_SK, O_SV = (0, 256), (256, 512), (512, 1024), (1024, 1536), (1536, 1552), (1552, 2064), (2064, 2192), (2192, 2320)

ADAM_LR, ADAM_B1, ADAM_B2, ADAM_EPS, ADAM_WD, ADAM_STEP = 0.001, 0.9, 0.999, 1e-08, 0.01, 10

MESH = pl.DeviceIdType.MESH
ANY = pl.BlockSpec(memory_space=pl.ANY)
HIGHEST = lax.Precision.HIGHEST


def _cp(sem=None, vmem_mb=None):
    kw = {}
    if sem is not None:
        kw["dimension_semantics"] = sem
    if vmem_mb is not None:
        kw["vmem_limit_bytes"] = vmem_mb << 20
    return pltpu.CompilerParams(**kw)


def _mm(a, b):
    return jnp.dot(a.astype(MXU_DTYPE), b.astype(MXU_DTYPE), preferred_element_type=F32)


def _mm_nt(a, b):
    return lax.dot_general(a.astype(MXU_DTYPE), b.astype(MXU_DTYPE), (((1,), (1,)), ((), ())),
                           preferred_element_type=F32)


def _mm_tn(a, b):
    return lax.dot_general(a.astype(MXU_DTYPE), b.astype(MXU_DTYPE), (((0,), (0,)), ((), ())),
                           preferred_element_type=F32)


def _logsigmoid(z):
    return jnp.minimum(z, 0.0) - jnp.log(1.0 + jnp.exp(-jnp.abs(z)))


def _sigmoid(z):
    return 1.0 / (1.0 + jnp.exp(-z))


def _row_tile(rows):
    return 640 if rows % 640 == 0 else 128


def _mesh_pos():
    return lax.axis_index("x"), lax.axis_index("y"), lax.axis_index("c")


def _all_gather(shards):
    n = len(shards)

    def body(*refs):
        start, forward, finish = _gather_schedule(refs[:n], refs[n:2 * n], *refs[2 * n:])
        start()
        for j in range(3):
            forward(j)
        finish()

    gathered = pl.pallas_call(
        body, name="all_gather_weights",
        out_shape=_gathered_shapes(shards), in_specs=[ANY] * n, out_specs=[ANY] * n,
        scratch_shapes=_gather_sems(n),
    )(*shards)
    return _with_own_block(gathered, shards)


def _gathered_shapes(shards):
    return [jax.ShapeDtypeStruct((N_DEV,) + s.shape, s.dtype) for s in shards]


def _gather_sems(n):
    return [pltpu.SemaphoreType.DMA((7 * n,)), pltpu.SemaphoreType.DMA((7 * n,))] if n else []


def _place_gather(step, steps, shard_refs, gathered_refs, sems):
    if not shard_refs:
        return
    start, forward, finish = _gather_schedule(shard_refs, gathered_refs, *sems)
    pl.when(step == 0)(start)
    for j, at in enumerate((steps * 7 // 10, steps * 8 // 10, steps * 9 // 10)):
        pl.when(step == at)(functools.partial(forward, j))
    pl.when(step == steps - 1)(finish)


def _with_own_block(gathered, shards):
    dev = 4 * lax.axis_index("x") + 2 * lax.axis_index("y") + lax.axis_index("c")
    return [lax.dynamic_update_index_in_dim(g, s, dev, 0) for g, s in zip(gathered, shards)]


def _gather_schedule(ins, outs, send_sems, recv_sems):
    n = len(ins)
    x, y, c = _mesh_pos()
    me, sibling = (x, y, c), (x, y, 1 - c)
    chips = [(1 - x, y), (x, 1 - y), (1 - x, 1 - y)]

    def copy(a, k, block, to, src=None):
        dst = outs[a].at[4 * block[0] + 2 * block[1] + block[2]]
        return pltpu.make_async_remote_copy(
            src_ref=dst if src is None else src, dst_ref=dst,
            send_sem=send_sems.at[a * 7 + k], recv_sem=recv_sems.at[a * 7 + k],
            device_id=to, device_id_type=MESH)

    def first(a):
        return [copy(a, 0, me, sibling, src=ins[a])] + [copy(a, 1 + j, me, (*chip, c), src=ins[a])
                                                        for j, chip in enumerate(chips)]

    def start():
        for a in range(n):
            for cp in first(a):
                cp.start()

    def forward(j):
        for a in range(n):
            copy(a, 1 + j, (*chips[j], c), me).wait_recv()
            copy(a, 4 + j, (*chips[j], c), sibling).start()

    def finish():
        for a in range(n):
            copy(a, 0, sibling, me).wait_recv()
            for j, chip in enumerate(chips):
                copy(a, 4 + j, (*chip, 1 - c), me).wait_recv()
        for a in range(n):
            for cp in first(a) + [copy(a, 4 + j, (*chip, c), sibling) for j, chip in enumerate(chips)]:
                cp.wait_send()

    return start, forward, finish


def _rs_sibling(gs):
    n = len(gs)

    def body(*refs):
        start, finish = _sibling_schedule(refs[:n], refs[n:2 * n], *refs[2 * n:])
        start()
        finish()

    return pl.pallas_call(
        body, name="reduce_scatter_sibling",
        out_shape=_sibling_shapes(gs), in_specs=[ANY] * n, out_specs=[ANY] * n,
        scratch_shapes=_sibling_sems(n),
    )(*gs)


def _sibling_shapes(gs):
    return [jax.ShapeDtypeStruct(g.shape[1:], g.dtype) for g in gs]


def _sibling_sems(n):
    return [pltpu.SemaphoreType.DMA((n,)), pltpu.SemaphoreType.DMA((n,))]


def _sibling_schedule(ins, land, send_sems, recv_sems):
    x, y, c = _mesh_pos()

    def copies():
        return [pltpu.make_async_remote_copy(
            src_ref=ins[a].at[1 - c], dst_ref=land[a], send_sem=send_sems.at[a], recv_sem=recv_sems.at[a],
            device_id=(x, y, 1 - c), device_id_type=MESH) for a in range(len(ins))]

    def start():
        for cp in copies():
            cp.start()

    def finish():
        for cp in copies():
            cp.wait_recv()
        for cp in copies():
            cp.wait_send()

    return start, finish


def _rs_chips(ps):
    n = len(ps)

    def body(*refs):
        start, finish = _chips_schedule(refs[:n], refs[n:2 * n], *refs[2 * n:])
        start()
        finish()

    return pl.pallas_call(
        body, name="reduce_scatter_chips",
        out_shape=_chips_shapes(ps), in_specs=[ANY] * n, out_specs=[ANY] * n,
        scratch_shapes=_chips_sems(n),
    )(*ps)


def _chips_shapes(ps):
    return [jax.ShapeDtypeStruct((3,) + p.shape[1:], p.dtype) for p in ps]


def _chips_sems(n):
    return [pltpu.SemaphoreType.DMA((3 * n,)), pltpu.SemaphoreType.DMA((3 * n,))]


def _chips_schedule(ins, land, send_sems, recv_sems):
    x, y, c = _mesh_pos()
    chips = [(1 - x, y), (x, 1 - y), (1 - x, 1 - y)]

    def copies():
        return [pltpu.make_async_remote_copy(
            src_ref=ins[a].at[2 * chip[0] + chip[1]], dst_ref=land[a].at[j],
            send_sem=send_sems.at[3 * a + j], recv_sem=recv_sems.at[3 * a + j],
            device_id=(*chip, c), device_id_type=MESH) for a in range(len(ins)) for j, chip in enumerate(chips)]

    def start():
        for cp in copies():
            cp.start()

    def finish():
        for cp in copies():
            cp.wait_recv()
        for cp in copies():
            cp.wait_send()

    return start, finish


class _Jobs:
    def __init__(self, jobs):
        self.jobs = jobs
        self.inputs = [a for _, arrs in jobs for a in arrs]
        self.out_shapes = [s for kind, arrs in jobs
                           for s in (_sibling_shapes(arrs) if kind == "sibling" else _chips_shapes(arrs))]
        self.sems = [s for kind, arrs in jobs
                     for s in (_sibling_sems(len(arrs)) if kind == "sibling" else _chips_sems(len(arrs)))]
        self.n = len(self.inputs)

    def bind(self, in_refs, out_refs, sem_refs):
        starts, finishes, at = [], [], 0
        for k, (kind, arrs) in enumerate(self.jobs):
            schedule = _sibling_schedule if kind == "sibling" else _chips_schedule
            start, finish = schedule(in_refs[at:at + len(arrs)], out_refs[at:at + len(arrs)],
                                     sem_refs[2 * k], sem_refs[2 * k + 1])
            starts.append(start)
            finishes.append(finish)
            at += len(arrs)

        def start_all():
            for f in starts:
                f()

        def finish_all():
            for f in finishes:
                f()

        return start_all, finish_all

    def split(self, outs):
        res, at = [], 0
        for _, arrs in self.jobs:
            res.append(list(outs[at:at + len(arrs)]))
            at += len(arrs)
        return res


R_META, R_WG, R_NORM_MIX, R_NORM_FF, R_FINAL, R_B_GATE, R_GLA_NORM, R_LOSS, R_SINKS, SMALL_ROWS = 0, 16, 32, 33, 34, 35, 36, 37, 40, 48


def _all_reduce_small(dlead, dwg, gnm, gnf, gfn, dbg, dgnw, loss, dsink):
    def body(dlead_ref, dwg_ref, gnm_ref, gnf_ref, gfn_ref, dbg_ref, dgnw_ref, loss_ref, dsink_ref,
             out_ref, p_ref, land, send_sems, recv_sems):
        x, y, c = _mesh_pos()
        me = 4 * x + 2 * y + c
        p_ref[...] = jnp.zeros_like(p_ref)
        p_ref[R_META:R_META + N_META, :] = dlead_ref[META0:LEAD, :]
        p_ref[R_WG:R_WG + GLA_RANK, 0:256] = dwg_ref[0:GLA_RANK, :]
        p_ref[R_NORM_MIX:R_NORM_MIX + 1, :] = gnm_ref[0:1, :]
        p_ref[R_NORM_FF:R_NORM_FF + 1, :] = gnf_ref[0:1, :]
        p_ref[R_FINAL:R_FINAL + 1, :] = gfn_ref[0:1, :]
        p_ref[R_B_GATE:R_B_GATE + 1, 0:256] = dbg_ref[0:1, :]
        p_ref[R_GLA_NORM:R_GLA_NORM + 1, 0:128] = dgnw_ref[0:1, :]
        p_ref[R_LOSS:R_LOSS + 1, 0:128] = loss_ref[0:1, :]
        p_ref[R_SINKS:R_SINKS + SWA_HEADS, 0:128] = dsink_ref[...]
        land[me] = p_ref[...]
        copies = []
        for k in range(1, N_DEV):
            bx, by, bc = (k >> 2) & 1, (k >> 1) & 1, k & 1
            peer = (1 - x if bx else x, 1 - y if by else y, 1 - c if bc else c)
            copies.append(pltpu.make_async_remote_copy(
                src_ref=p_ref, dst_ref=land.at[me], send_sem=send_sems.at[k - 1], recv_sem=recv_sems.at[k - 1],
                device_id=peer, device_id_type=MESH))
        for cp in copies:
            cp.start()
        for cp in copies:
            cp.wait_recv()
        for cp in copies:
            cp.wait_send()
        acc = land[0]
        for d in range(1, N_DEV):
            acc = acc + land[d]
        out_ref[...] = acc

    return pl.pallas_call(
        body, name="all_reduce_small",
        out_shape=jax.ShapeDtypeStruct((SMALL_ROWS, D), F32),
        in_specs=[pl.BlockSpec(memory_space=pltpu.VMEM)] * 9, out_specs=pl.BlockSpec(memory_space=pltpu.VMEM),
        scratch_shapes=[pltpu.VMEM((SMALL_ROWS, D), F32), pltpu.VMEM((N_DEV, SMALL_ROWS, D), F32),
                        pltpu.SemaphoreType.DMA((7,)), pltpu.SemaphoreType.DMA((7,))],
    )(dlead, dwg, gnm, gnf, gfn, dbg, dgnw, loss, dsink)


def _token_specs(tm, grid_rank=1):
    nb = tm // LEAD

    def spec(k):
        if grid_rank == 1:
            return pl.BlockSpec((LEAD, D), lambda i: (jnp.maximum(i * nb + k - 1, 0), 0))
        return pl.BlockSpec((LEAD, D), lambda i, j: (jnp.maximum(i * nb + k - 1, 0), 0))

    return [spec(k) for k in range(nb)]


def _h_tile(i, lead_ref, x_refs):
    first = jnp.where(i == 0, lead_ref[...], x_refs[0][...])
    return jnp.concatenate([first] + [r[...] for r in x_refs[1:]], axis=0)


def _in_proj(x, lead, nw, win_p, tm):
    rows = LEAD + x.shape[0]
    nb = tm // LEAD

    def body(*refs):
        x_refs, (lead_ref, nw_ref, w_ref, o_ref) = refs[:nb], refs[nb:]
        h = _h_tile(pl.program_id(0), lead_ref, x_refs)
        rstd = lax.rsqrt(jnp.mean(h * h, axis=-1, keepdims=True) + EPS)
        u = (h * rstd * nw_ref[...]).astype(MXU_DTYPE)
        o_ref[...] = jnp.dot(u, w_ref[...].astype(MXU_DTYPE), preferred_element_type=F32)

    return pl.pallas_call(
        body, name="in_proj", grid=(rows // tm,),
        in_specs=_token_specs(tm) + [pl.BlockSpec((LEAD, D), lambda i: (0, 0)), pl.BlockSpec((1, D), lambda i: (0, 0)),
                                     pl.BlockSpec((D, DINP), lambda i: (0, 0))],
        out_specs=pl.BlockSpec((tm, DINP), lambda i: (i, 0)),
        out_shape=jax.ShapeDtypeStruct((rows, DINP), F32),
        compiler_params=_cp(("arbitrary",), 56),
    )(*([x] * nb), lead, nw, win_p)


def _rope_tables(rows):
    pos = (jnp.arange(rows, dtype=jnp.int32) - META0).astype(F32)
    inv_freq = 1.0 / (ROPE_THETA ** (jnp.arange(0, ROPE_DIM, 2, dtype=F32) / ROPE_DIM))
    ang = pos[:, None] * jnp.tile(inv_freq, 128 // (ROPE_DIM // 2))[None, :]
    in_head = jnp.arange(128, dtype=jnp.int32)[None, :] % SWA_HD
    cos, sin = jnp.cos(ang), jnp.sin(ang)
    c_tab = jnp.where(in_head < ROPE_DIM, cos, 1.0)
    sa_tab = jnp.where(in_head < ROPE_DIM // 2, -sin, 0.0)
    sb_tab = jnp.where((in_head >= ROPE_DIM // 2) & (in_head < ROPE_DIM), sin, 0.0)
    return c_tab, sa_tab, sb_tab


def _rope(xv, cos, sa, sb):
    width = xv.shape[1]
    reps = width // 128
    if reps > 1:
        cos, sa, sb = (jnp.tile(t, (1, reps)) for t in (cos, sa, sb))
    return xv * cos + pltpu.roll(xv, width - 8, 1) * sa + pltpu.roll(xv, 8, 1) * sb


def _unrope(dy, cos, sa, sb):
    width = dy.shape[1]
    reps = width // 128
    if reps > 1:
        cos, sa, sb = (jnp.tile(t, (1, reps)) for t in (cos, sa, sb))
    return dy * cos + pltpu.roll(dy * sa, 8, 1) + pltpu.roll(dy * sb, width - 8, 1)


def _swa_prep(proj, tabs, tm):
    rows = proj.shape[0]

    def body(q_ref, k_ref, v_ref, c_ref, sa_ref, sb_ref, qo_ref, ko_ref, vo_ref):
        cos, sa, sb = c_ref[...], sa_ref[...], sb_ref[...]
        qo_ref[...] = (_rope(q_ref[...], cos, sa, sb) * (SWA_HD ** -0.5)).astype(ACT_DTYPE)
        ko_ref[...] = _rope(k_ref[...], cos, sa, sb).astype(ACT_DTYPE)
        vo_ref[...] = v_ref[...].astype(ACT_DTYPE)

    tab_spec = pl.BlockSpec((tm, 128), lambda i: (i, 0))
    return pl.pallas_call(
        body, name="swa_prep", grid=(rows // tm,),
        in_specs=[pl.BlockSpec((tm, 512), lambda i: (i, C_SQ // 512)),
                  pl.BlockSpec((tm, 128), lambda i: (i, C_SK // 128)),
                  pl.BlockSpec((tm, 128), lambda i: (i, C_SV // 128)), tab_spec, tab_spec, tab_spec],
        out_specs=[pl.BlockSpec((tm, 512), lambda i: (i, 0)), tab_spec, tab_spec],
        out_shape=[jax.ShapeDtypeStruct((rows, 512), ACT_DTYPE), jax.ShapeDtypeStruct((rows, 128), ACT_DTYPE),
                   jax.ShapeDtypeStruct((rows, 128), ACT_DTYPE)],
        compiler_params=_cp(("arbitrary",)),
    )(proj, proj, proj, *tabs)


def _gla_group(nc):
    for g in (5, 2):
        if nc % g == 0:
            return g
    return 1


def _gla_gates(lr, wg, bg, first_row, nrows):
    zg = _mm(lr, wg) + bg
    row = first_row + lax.broadcasted_iota(jnp.int32, (nrows, 1), 0)
    live = row >= META0
    g = jnp.where(live, _logsigmoid(zg) * (1.0 / GLA_TAU), 0.0)
    ii = lax.broadcasted_iota(jnp.int32, (nrows, nrows), 0)
    jj = lax.broadcasted_iota(jnp.int32, (nrows, nrows), 1)
    same = (ii // GLA_CHUNK) == (jj // GLA_CHUNK)
    lower, upper = same & (jj <= ii), same & (jj >= ii)
    b = jnp.dot(lower.astype(F32), g, precision=HIGHEST, preferred_element_type=F32)
    return zg, live, lower, upper, b


def _tril64():
    ii = lax.broadcasted_iota(jnp.int32, (GLA_CHUNK, GLA_CHUNK), 0)
    jj = lax.broadcasted_iota(jnp.int32, (GLA_CHUNK, GLA_CHUNK), 1)
    return jj <= ii


def _gla_fwd(proj, wg_p, bg, gnw, shards):
    rows = proj.shape[0]
    nc = rows // GLA_CHUNK
    group = _gla_group(nc)
    steps, nrows = nc // group, group * GLA_CHUNK
    ns = len(shards)

    def body(q_ref, k_ref, v_ref, r_ref, lr_ref, wg_ref, bg_ref, gnw_ref, *rest):
        shard_refs, rest = rest[:ns], rest[ns:]
        oraw_ref, og_ref, st_ref = rest[:3]
        gathered_refs, rest = rest[3:3 + ns], rest[3 + ns:]
        state = rest[0]
        c = pl.program_id(0)

        @pl.when(c == 0)
        def _():
            state[...] = jnp.zeros_like(state)

        _place_gather(c, steps, shard_refs, gathered_refs, rest[1:])
        _, _, _, _, b = _gla_gates(lr_ref[...], wg_ref[...], bg_ref[...], c * nrows, nrows)
        eb = jnp.exp(b)
        gq = q_ref[...] * (GLA_DK ** -0.5) * eb
        gk = k_ref[...] * jnp.exp(-b)
        v = v_ref[...]
        gnw_v = gnw_ref[...]
        tril = _tril64()
        pairs = [(h, gi) for h in range(GLA_HEADS) for gi in range(group)]
        rs = {gi: slice(gi * GLA_CHUNK, (gi + 1) * GLA_CHUNK) for gi in range(group)}
        s64 = {h: slice(h * GLA_DK, (h + 1) * GLA_DK) for h in range(GLA_HEADS)}
        s128 = {h: slice(h * GLA_DV, (h + 1) * GLA_DV) for h in range(GLA_HEADS)}
        qh = {(h, gi): gq[rs[gi], s64[h]] for h, gi in pairs}
        kh = {(h, gi): gk[rs[gi], s64[h]] for h, gi in pairs}
        vh = {(h, gi): v[rs[gi], s128[h]] for h, gi in pairs}
        ebl = {(h, gi): eb[(gi + 1) * GLA_CHUNK - 1:(gi + 1) * GLA_CHUNK, s64[h]] for h, gi in pairs}
        av = {pr: _mm(jnp.where(tril, _mm_nt(qh[pr], kh[pr]), 0.0), vh[pr]) for pr in pairs}
        inc = {pr: _mm_tn(vh[pr], kh[pr] * ebl[pr]) for pr in pairs}
        st = {}
        for h in range(GLA_HEADS):
            cur = state[h]
            for gi in range(group):
                st[h, gi] = cur
                st_ref[gi, h] = cur
                cur = cur * ebl[h, gi] + inc[h, gi]
            state[h] = cur
        for h, gi in pairs:
            o = av[h, gi] + _mm_nt(qh[h, gi], st[h, gi])
            oraw_ref[rs[gi], s128[h]] = o
            rstd = lax.rsqrt(jnp.mean(o * o, axis=-1, keepdims=True) + EPS)
            rh = r_ref[rs[gi], s128[h]]
            og_ref[rs[gi], s128[h]] = (o * rstd * gnw_v * (rh * _sigmoid(rh))).astype(ACT_DTYPE)

    nb = lambda w, col: pl.BlockSpec((nrows, w), lambda c: (c, col // w))
    const = lambda shape: pl.BlockSpec(shape, lambda c: (0,) * len(shape))
    outs = pl.pallas_call(
        body, name="gla_fwd", grid=(steps,),
        in_specs=[nb(256, C_GQ), nb(256, C_GK), nb(512, C_GV), nb(512, C_GR), nb(128, C_LR),
                  const((128, 256)), const((1, 256)), const((1, 128))] + [ANY] * ns,
        out_specs=[pl.BlockSpec((nrows, 512), lambda c: (c, 0)), pl.BlockSpec((nrows, 512), lambda c: (c, 0)),
                   pl.BlockSpec((group, GLA_HEADS, GLA_DV, GLA_DK), lambda c: (c, 0, 0, 0))] + [ANY] * ns,
        out_shape=[jax.ShapeDtypeStruct((rows, 512), F32), jax.ShapeDtypeStruct((rows, 512), ACT_DTYPE),
                   jax.ShapeDtypeStruct((nc, GLA_HEADS, GLA_DV, GLA_DK), F32)] + _gathered_shapes(shards),
        scratch_shapes=[pltpu.VMEM((GLA_HEADS, GLA_DV, GLA_DK), F32)] + _gather_sems(ns),
        compiler_params=_cp(("arbitrary",)),
    )(proj, proj, proj, proj, proj, wg_p, bg, gnw, *shards)
    return outs[0], outs[1], outs[2], _with_own_block(outs[3:], shards)


def _swa_mask(n):
    shape = (SWA_GROUP * SWA_BLOCK, 3 * SWA_BLOCK)
    qi = lax.broadcasted_iota(jnp.int32, shape, 0) & (SWA_BLOCK - 1)
    jj = lax.broadcasted_iota(jnp.int32, shape, 1)
    meta = (jj < SWA_BLOCK) & (jj >= META0) & ((n > 0) | (jj <= qi))
    prev = (jj >= SWA_BLOCK) & (jj < 2 * SWA_BLOCK) & (n >= 2) & (jj - SWA_BLOCK > qi)
    cur = (jj >= 2 * SWA_BLOCK) & (n >= 1) & (jj - 2 * SWA_BLOCK <= qi)
    return meta | prev | cur


def _stack_heads(t, kvh):
    return jnp.concatenate([t[:, (kvh * SWA_GROUP + g) * SWA_HD:(kvh * SWA_GROUP + g + 1) * SWA_HD]
                            for g in range(SWA_GROUP)], axis=0)


def _stack_sinks(sink_ref, kvh):
    return jnp.concatenate([jnp.full((SWA_BLOCK, 1), sink_ref[0, kvh * SWA_GROUP + g], F32)
                            for g in range(SWA_GROUP)], axis=0)


def _swa_group(nblk):
    return 5 if nblk % 5 == 0 else 1


def _swa_specs(group):
    blk = lambda w: pl.BlockSpec((group * SWA_BLOCK, w), lambda n: (n, 0))
    first = pl.BlockSpec((SWA_BLOCK, 128), lambda n: (0, 0))
    prev = pl.BlockSpec((SWA_BLOCK, 128), lambda n: (jnp.maximum(n * group - 1, 0), 0))
    return blk, first, prev


def _swa_keys(first_ref, prev_ref, cur_ref, g):
    own = cur_ref[g * SWA_BLOCK:(g + 1) * SWA_BLOCK, :]
    before = prev_ref[...] if g == 0 else cur_ref[(g - 1) * SWA_BLOCK:g * SWA_BLOCK, :]
    return jnp.concatenate([first_ref[...], before, own], axis=0)


def _swa_fwd(qr, kr, vr, sinks, shards):
    rows = qr.shape[0]
    nblk = rows // SWA_BLOCK
    group = _swa_group(nblk)
    steps = nblk // group
    ns = len(shards)

    def body(q_ref, k0, kp, kc, v0, vp, vc, sink_ref, *rest):
        o_ref = rest[ns]
        _place_gather(pl.program_id(0), steps, rest[:ns], rest[ns + 1:2 * ns + 1], rest[2 * ns + 1:])
        for g in range(group):
            n = pl.program_id(0) * group + g
            rs = slice(g * SWA_BLOCK, (g + 1) * SWA_BLOCK)
            kall, vall = _swa_keys(k0, kp, kc, g), _swa_keys(v0, vp, vc, g)
            mask = _swa_mask(n)[0:SWA_BLOCK]
            heads = range(SWA_HEADS)
            hs = [slice(h * SWA_HD, (h + 1) * SWA_HD) for h in heads]
            kv = [slice((h // SWA_GROUP) * SWA_HD, (h // SWA_GROUP + 1) * SWA_HD) for h in heads]
            s = [jnp.where(mask, _mm_nt(q_ref[rs, hs[h]], kall[:, kv[h]]), NEG) for h in heads]
            m = [jnp.maximum(jnp.max(s[h], axis=-1, keepdims=True), sink_ref[0, h]) for h in heads]
            p = [jnp.exp(s[h] - m[h]) for h in heads]
            den = [jnp.sum(p[h], axis=-1, keepdims=True) + jnp.exp(sink_ref[0, h] - m[h]) for h in heads]
            o = [_mm(p[h], vall[:, kv[h]]) for h in heads]
            for h in heads:
                o_ref[rs, hs[h]] = (o[h] / den[h]).astype(ACT_DTYPE)

    blk, first, prev = _swa_specs(group)
    outs = pl.pallas_call(
        body, name="swa_fwd", grid=(steps,),
        in_specs=[blk(512), first, prev, blk(128), first, prev, blk(128),
                  pl.BlockSpec(memory_space=pltpu.SMEM)] + [ANY] * ns,
        out_specs=[blk(512)] + [ANY] * ns,
        out_shape=[jax.ShapeDtypeStruct((rows, 512), ACT_DTYPE)] + _gathered_shapes(shards),
        scratch_shapes=_gather_sems(ns),
        compiler_params=_cp(("arbitrary",)),
    )(qr, kr, kr, kr, vr, vr, vr, sinks, *shards)
    return outs[0], _with_own_block(outs[1:], shards)


def _out_proj(x, lead, og, osw, wout, nfw, tm):
    rows = LEAD + x.shape[0]
    nb = tm // LEAD

    def body(*refs):
        x_refs, (lead_ref, og_ref, os_ref, w_ref, nw_ref, h1_ref, f_ref, ft_ref) = refs[:nb], refs[nb:]
        h0 = _h_tile(pl.program_id(0), lead_ref, x_refs)
        h1 = h0 + _mm(og_ref[...], w_ref[0:512, :]) + _mm(os_ref[...], w_ref[512:1024, :])
        h1_ref[...] = h1
        rstd = lax.rsqrt(jnp.mean(h1 * h1, axis=-1, keepdims=True) + EPS)
        f = h1 * rstd * nw_ref[...]
        f_ref[...] = f.astype(ACT_DTYPE)
        ft_ref[...] = f.T.astype(ACT_DTYPE)

    row = lambda w: pl.BlockSpec((tm, w), lambda i: (i, 0))
    return pl.pallas_call(
        body, name="out_proj", grid=(rows // tm,),
        in_specs=_token_specs(tm) + [pl.BlockSpec((LEAD, D), lambda i: (0, 0)), row(512), row(512),
                                     pl.BlockSpec((D, D), lambda i: (0, 0)), pl.BlockSpec((1, D), lambda i: (0, 0))],
        out_specs=[row(D), row(D), pl.BlockSpec((D, tm), lambda i: (0, i))],
        out_shape=[jax.ShapeDtypeStruct((rows, D), F32), jax.ShapeDtypeStruct((rows, D), ACT_DTYPE),
                   jax.ShapeDtypeStruct((D, rows), ACT_DTYPE)],
        compiler_params=_cp(("arbitrary",), 48),
    )(*([x] * nb), lead, og, osw, wout, nfw)


def _ffn_fwd(f, h1, w1, w2, tgt, fnw, tm):
    rows = f.shape[0]
    nj = D_FF // FF_WIDE
    nb = tm // LEAD

    def body(f_ref, h1_ref, w1_ref, w2_ref, nw_ref, *rest):
        t_refs, (a_ref, dh2_ref, dh2t_ref, loss_ref, gfn_ref, acc) = rest[:nb], rest[nb:]
        i, j = pl.program_id(0), pl.program_id(1)

        @pl.when((i == 0) & (j == 0))
        def _():
            loss_ref[...] = jnp.zeros_like(loss_ref)
            gfn_ref[...] = jnp.zeros_like(gfn_ref)

        @pl.when(j == 0)
        def _():
            acc[...] = jnp.zeros_like(acc)

        a = _mm(f_ref[...], w1_ref[...])
        a_ref[...] = a.astype(ACT_DTYPE)
        z = jnp.square(jnp.maximum(a, 0.0))
        acc[...] += _mm(z, w2_ref[...])

        @pl.when(j == nj - 1)
        def _():
            h2 = h1_ref[...] + acc[...]
            rstd = lax.rsqrt(jnp.mean(h2 * h2, axis=-1, keepdims=True) + EPS)
            hn = h2 * rstd
            nw = nw_ref[...]
            row = i * tm + lax.broadcasted_iota(jnp.int32, (tm, 1), 0)
            target = jnp.concatenate([t[...] for t in t_refs], axis=0)
            err = jnp.where(row >= LEAD, hn * nw - target, 0.0)
            row_loss = jnp.sum(err * err, axis=-1, keepdims=True) * (1.0 / D)
            loss_ref[...] += jnp.broadcast_to(0.5 * jnp.sum(row_loss, axis=0, keepdims=True), loss_ref.shape)
            dy = err * (1.0 / D)
            gfn_ref[...] += jnp.broadcast_to(jnp.sum(dy * hn, axis=0, keepdims=True), gfn_ref.shape)
            dhn = dy * nw
            dh2 = rstd * (dhn - hn * jnp.mean(dhn * hn, axis=-1, keepdims=True))
            dh2_ref[...] = dh2
            dh2t_ref[...] = dh2.T.astype(ACT_DTYPE)

    return pl.pallas_call(
        body, name="ffn_fwd", grid=(rows // tm, nj),
        in_specs=[pl.BlockSpec((tm, D), lambda i, j: (i, 0)), pl.BlockSpec((tm, D), lambda i, j: (i, 0)),
                  pl.BlockSpec((D, FF_WIDE), lambda i, j: (0, j)),
                  pl.BlockSpec((FF_WIDE, D), lambda i, j: (j, 0)),
                  pl.BlockSpec((1, D), lambda i, j: (0, 0))] + _token_specs(tm, grid_rank=2),
        out_specs=[pl.BlockSpec((tm, FF_WIDE), lambda i, j: (i, j)), pl.BlockSpec((tm, D), lambda i, j: (i, 0)),
                   pl.BlockSpec((D, tm), lambda i, j: (0, i)),
                   pl.BlockSpec((8, 128), lambda i, j: (0, 0)), pl.BlockSpec((8, D), lambda i, j: (0, 0))],
        out_shape=[jax.ShapeDtypeStruct((rows, D_FF), ACT_DTYPE), jax.ShapeDtypeStruct((rows, D), F32),
                   jax.ShapeDtypeStruct((D, rows), ACT_DTYPE),
                   jax.ShapeDtypeStruct((8, 128), F32), jax.ShapeDtypeStruct((8, D), F32)],
        scratch_shapes=[pltpu.VMEM((tm, D), F32)],
        compiler_params=_cp(("arbitrary", "arbitrary"), 56),
    )(f, h1, w1, w2, fnw, *([tgt] * nb))


def _ffn_bwd_act(dh2, a, w1, w2, h1, nfw, tm):
    rows = dh2.shape[0]
    nj = D_FF // FF_WIDE

    def body(dh2_ref, a_ref, w1_ref, w2_ref, h1_ref, nw_ref, da_ref, dh1_ref, gnf_ref, acc):
        i, j = pl.program_id(0), pl.program_id(1)

        @pl.when((i == 0) & (j == 0))
        def _():
            gnf_ref[...] = jnp.zeros_like(gnf_ref)

        @pl.when(j == 0)
        def _():
            acc[...] = jnp.zeros_like(acc)

        dz = _mm_nt(dh2_ref[...], w2_ref[...])
        da = dz * (2.0 * jnp.maximum(a_ref[...].astype(F32), 0.0))
        da_ref[...] = da.astype(ACT_DTYPE)
        acc[...] += _mm_nt(da, w1_ref[...])

        @pl.when(j == nj - 1)
        def _():
            h1 = h1_ref[...]
            rstd = lax.rsqrt(jnp.mean(h1 * h1, axis=-1, keepdims=True) + EPS)
            hn = h1 * rstd
            df = acc[...]
            gnf_ref[...] += jnp.broadcast_to(jnp.sum(df * hn, axis=0, keepdims=True), gnf_ref.shape)
            dfn = df * nw_ref[...]
            dh1_ref[...] = dh2_ref[...] + rstd * (dfn - hn * jnp.mean(dfn * hn, axis=-1, keepdims=True))

    return pl.pallas_call(
        body, name="ffn_bwd_act", grid=(rows // tm, nj),
        in_specs=[pl.BlockSpec((tm, D), lambda i, j: (i, 0)), pl.BlockSpec((tm, FF_WIDE), lambda i, j: (i, j)),
                  pl.BlockSpec((D, FF_WIDE), lambda i, j: (0, j)),
                  pl.BlockSpec((FF_WIDE, D), lambda i, j: (j, 0)),
                  pl.BlockSpec((tm, D), lambda i, j: (i, 0)), pl.BlockSpec((1, D), lambda i, j: (0, 0))],
        out_specs=[pl.BlockSpec((tm, FF_WIDE), lambda i, j: (i, j)), pl.BlockSpec((tm, D), lambda i, j: (i, 0)),
                   pl.BlockSpec((8, D), lambda i, j: (0, 0))],
        out_shape=[jax.ShapeDtypeStruct((rows, D_FF), ACT_DTYPE), jax.ShapeDtypeStruct((rows, D), F32),
                   jax.ShapeDtypeStruct((8, D), F32)],
        scratch_shapes=[pltpu.VMEM((tm, D), F32)],
        compiler_params=_cp(("arbitrary", "arbitrary"), 56),
    )(dh2, a, w1, w2, h1, nfw)


def _ffn_bwd_weights(ft, a, da, dh2t, tm):
    rows = a.shape[0]
    steps = rows // tm

    def body(ft_ref, a_ref, da_ref, dh2t_ref, dw1_ref, dw2_ref, dw2t):
        i = pl.program_id(1)

        @pl.when(i == 0)
        def _():
            dw1_ref[...] = jnp.zeros_like(dw1_ref)
            dw2t[...] = jnp.zeros_like(dw2t)

        z = jnp.square(jnp.maximum(a_ref[...].astype(F32), 0.0))
        dw1_ref[...] += _mm(ft_ref[...], da_ref[...])
        dw2t[...] += _mm(dh2t_ref[...], z)

        @pl.when(i == steps - 1)
        def _():
            dw2_ref[...] = dw2t[...].T

    return pl.pallas_call(
        body, name="ffn_bwd_weights", grid=(N_DEV, steps),
        in_specs=[pl.BlockSpec((D, tm), lambda j, i: (0, i)), pl.BlockSpec((tm, FF_TILE), lambda j, i: (i, j)),
                  pl.BlockSpec((tm, FF_TILE), lambda j, i: (i, j)), pl.BlockSpec((D, tm), lambda j, i: (0, i))],
        out_specs=[pl.BlockSpec((None, None, D, FF_TILE), lambda j, i: (j % 2, j // 2, 0, 0)),
                   pl.BlockSpec((None, None, FF_TILE, D), lambda j, i: (j % 2, j // 2, 0, 0))],
        out_shape=[jax.ShapeDtypeStruct((2, 4, D, FF_TILE), F32), jax.ShapeDtypeStruct((2, 4, FF_TILE, D), F32)],
        scratch_shapes=[pltpu.VMEM((D, FF_TILE), F32)],
        compiler_params=_cp(("arbitrary", "arbitrary"), 48),
    )(ft, a, da, dh2t)


def _out_proj_bwd(dh1, og, osw, wout, tm, partials):
    rows = dh1.shape[0]
    steps = rows // tm
    ns = len(partials)

    def body(dh1_ref, og_ref, os_ref, w_ref, *rest):
        part_refs, rest = rest[:ns], rest[ns:]
        dog_ref, dos_ref, dw_ref = rest[:3]
        land_refs, (send_sems, recv_sems) = rest[3:3 + ns], rest[3 + ns:]
        i = pl.program_id(0)
        start, finish = _sibling_schedule(part_refs, land_refs, send_sems, recv_sems)

        @pl.when(i == 0)
        def _():
            dw_ref[...] = jnp.zeros_like(dw_ref)
            start()

        pl.when(i == steps - 1)(finish)

        dh1 = dh1_ref[...].astype(MXU_DTYPE)
        dog_ref[...] = _mm_nt(dh1, w_ref[0:512, :])
        dos_ref[...] = _mm_nt(dh1, w_ref[512:1024, :])
        for half, ref in enumerate((og_ref, os_ref)):
            dw = _mm_tn(ref[...], dh1)
            for blk in range(4):
                shard = half * 4 + blk
                dw_ref[shard % 2, shard // 2] += dw[blk * 128:(blk + 1) * 128, :]

    row = lambda w: pl.BlockSpec((tm, w), lambda i: (i, 0))
    outs = pl.pallas_call(
        body, name="out_proj_bwd", grid=(steps,),
        in_specs=[row(D), row(512), row(512), pl.BlockSpec((D, D), lambda i: (0, 0))] + [ANY] * ns,
        out_specs=[row(512), row(512), pl.BlockSpec((2, 4, 128, D), lambda i: (0, 0, 0, 0))] + [ANY] * ns,
        out_shape=[jax.ShapeDtypeStruct((rows, 512), F32), jax.ShapeDtypeStruct((rows, 512), F32),
                   jax.ShapeDtypeStruct((2, 4, 128, D), F32)] + _sibling_shapes(partials),
        scratch_shapes=_sibling_sems(ns),
        compiler_params=_cp(("arbitrary",), 48),
    )(dh1, og, osw, wout, *partials)
    return outs[0], outs[1], outs[2], outs[3:]


def _swa_bwd(qr, kr, vr, osw, dos, sinks, jobs):
    rows = qr.shape[0]
    nblk = rows // SWA_BLOCK
    group = _swa_group(nblk)
    steps = nblk // group
    ns = jobs.n

    def body(q_ref, k0, kp, kc, v0, vp, vc, o_ref, do_ref, sink_ref, *rest):
        dq_ref, dk_ref, dv_ref, dsink_ref = rest[ns:ns + 4]
        start, finish = jobs.bind(rest[:ns], rest[ns + 4:2 * ns + 4], rest[2 * ns + 4:])
        step = pl.program_id(0)

        @pl.when(step == 0)
        def _():
            dk_ref[...] = jnp.zeros_like(dk_ref)
            dv_ref[...] = jnp.zeros_like(dv_ref)
            dsink_ref[...] = jnp.zeros_like(dsink_ref)
            start()

        pl.when(step == steps - 1)(finish)
        for g in range(group):
            block(step * group + g, g, q_ref, k0, kp, kc, v0, vp, vc, o_ref, do_ref, sink_ref,
                  dq_ref, dk_ref, dv_ref, dsink_ref)

    def block(n, g, q_ref, k0, kp, kc, v0, vp, vc, o_ref, do_ref, sink_ref, dq_ref, dk_ref, dv_ref, dsink_ref):
        rs = slice(g * SWA_BLOCK, (g + 1) * SWA_BLOCK)
        kall, vall = _swa_keys(k0, kp, kc, g), _swa_keys(v0, vp, vc, g)
        mask = _swa_mask(n)[0:SWA_BLOCK]
        heads = range(SWA_HEADS)
        hs = [slice(h * SWA_HD, (h + 1) * SWA_HD) for h in heads]
        kv = [slice((h // SWA_GROUP) * SWA_HD, (h // SWA_GROUP + 1) * SWA_HD) for h in heads]
        sink = [sink_ref[0, h] for h in heads]
        qh = [q_ref[rs, hs[h]] for h in heads]
        doh = [do_ref[rs, hs[h]] for h in heads]
        s = [jnp.where(mask, _mm_nt(qh[h], kall[:, kv[h]]), NEG) for h in heads]
        dp = [_mm_nt(doh[h], vall[:, kv[h]]) for h in heads]
        delta = [jnp.sum(doh[h] * o_ref[rs, hs[h]].astype(F32), axis=-1, keepdims=True) for h in heads]
        m = [jnp.maximum(jnp.max(s[h], axis=-1, keepdims=True), sink[h]) for h in heads]
        e = [jnp.exp(s[h] - m[h]) for h in heads]
        inv = [1.0 / (jnp.sum(e[h], axis=-1, keepdims=True) + jnp.exp(sink[h] - m[h])) for h in heads]
        p = [e[h] * inv[h] for h in heads]
        ds = [p[h] * (dp[h] - delta[h]) for h in heads]
        dq = [_mm(ds[h], kall[:, kv[h]]) for h in heads]
        dkh = [_mm_tn(ds[h], qh[h]) for h in heads]
        dvh = [_mm_tn(p[h], doh[h]) for h in heads]
        for h in heads:
            dsink = -jnp.sum(jnp.exp(sink[h] - m[h]) * inv[h] * delta[h], axis=0, keepdims=True)
            dsink_ref[h:h + 1, :] += jnp.broadcast_to(dsink, (1, 128))
        dq_ref[rs, :] = jnp.concatenate(dq, axis=1)
        group_sum = lambda parts, kvh: sum(parts[kvh * SWA_GROUP + 1:(kvh + 1) * SWA_GROUP], parts[kvh * SWA_GROUP])
        dk_all = jnp.concatenate([group_sum(dkh, kvh) for kvh in range(SWA_KV)], axis=1)
        dv_all = jnp.concatenate([group_sum(dvh, kvh) for kvh in range(SWA_KV)], axis=1)
        prev0 = pl.multiple_of(jnp.maximum(n - 1, 0) * SWA_BLOCK, SWA_BLOCK)
        cur0 = pl.multiple_of(n * SWA_BLOCK, SWA_BLOCK)
        for ref, val in ((dk_ref, dk_all), (dv_ref, dv_all)):
            ref[0:SWA_BLOCK, :] += val[0:SWA_BLOCK]
            ref[pl.ds(prev0, SWA_BLOCK), :] += val[SWA_BLOCK:2 * SWA_BLOCK]
            ref[pl.ds(cur0, SWA_BLOCK), :] += val[2 * SWA_BLOCK:]

    blk, first, prev = _swa_specs(group)
    whole = pl.BlockSpec((rows, 128), lambda n: (0, 0))
    outs = pl.pallas_call(
        body, name="swa_bwd", grid=(steps,),
        in_specs=[blk(512), first, prev, blk(128), first, prev, blk(128), blk(512), blk(512),
                  pl.BlockSpec(memory_space=pltpu.SMEM)] + [ANY] * ns,
        out_specs=[blk(512), whole, whole, pl.BlockSpec((8, 128), lambda n: (0, 0))] + [ANY] * ns,
        out_shape=[jax.ShapeDtypeStruct((rows, 512), F32), jax.ShapeDtypeStruct((rows, 128), F32),
                   jax.ShapeDtypeStruct((rows, 128), F32), jax.ShapeDtypeStruct((8, 128), F32)] + jobs.out_shapes,
        scratch_shapes=jobs.sems,
        compiler_params=_cp(("arbitrary",), 48),
    )(qr, kr, kr, kr, vr, vr, vr, osw, dos, sinks, *jobs.inputs)
    return outs[0], outs[1], outs[2], outs[3], jobs.split(outs[4:])


def _gla_bwd(proj, oraw, states, dog, wg_p, bg, gnw, jobs):
    rows = proj.shape[0]
    nc = rows // GLA_CHUNK
    group = _gla_group(nc)
    steps, nrows = nc // group, group * GLA_CHUNK
    ns = jobs.n

    def body(q_ref, k_ref, v_ref, r_ref, lr_ref, oraw_ref, st_ref, dog_ref, wg_ref, bg_ref, gnw_ref, *rest):
        dq_ref, dk_ref, dv_ref, dr_ref, dlr_ref, dwg_ref, dbg_ref, dgnw_ref = rest[ns:ns + 8]
        dstate, db_scr = rest[2 * ns + 8:2 * ns + 10]
        start, finish = jobs.bind(rest[:ns], rest[ns + 8:2 * ns + 8], rest[2 * ns + 10:])
        t = pl.program_id(0)
        c = steps - 1 - t

        @pl.when(t == 0)
        def _():
            dstate[...] = jnp.zeros_like(dstate)
            dwg_ref[...] = jnp.zeros_like(dwg_ref)
            dbg_ref[...] = jnp.zeros_like(dbg_ref)
            dgnw_ref[...] = jnp.zeros_like(dgnw_ref)
            start()

        pl.when(t == steps - 1)(finish)

        lr, wg = lr_ref[...], wg_ref[...]
        zg, live, _, upper, b = _gla_gates(lr, wg, bg_ref[...], c * nrows, nrows)
        eb, enb = jnp.exp(b), jnp.exp(-b)
        scale = GLA_DK ** -0.5
        gq = q_ref[...] * scale * eb
        gk = k_ref[...] * enb
        v = v_ref[...]
        gnw_v = gnw_ref[...]
        tril = _tril64()
        is_last = lax.broadcasted_iota(jnp.int32, (GLA_CHUNK, 1), 0) == GLA_CHUNK - 1
        dgnw = jnp.zeros((1, GLA_DV), F32)
        pairs = [(h, gi) for h in range(GLA_HEADS) for gi in range(group)]
        rs = {gi: slice(gi * GLA_CHUNK, (gi + 1) * GLA_CHUNK) for gi in range(group)}
        s64 = {h: slice(h * GLA_DK, (h + 1) * GLA_DK) for h in range(GLA_HEADS)}
        s128 = {h: slice(h * GLA_DV, (h + 1) * GLA_DV) for h in range(GLA_HEADS)}
        qh = {(h, gi): gq[rs[gi], s64[h]] for h, gi in pairs}
        kh = {(h, gi): gk[rs[gi], s64[h]] for h, gi in pairs}
        vh = {(h, gi): v[rs[gi], s128[h]] for h, gi in pairs}
        ebl = {(h, gi): eb[(gi + 1) * GLA_CHUNK - 1:(gi + 1) * GLA_CHUNK, s64[h]] for h, gi in pairs}
        kl = {pr: kh[pr] * ebl[pr] for pr in pairs}
        st = {(h, gi): st_ref[gi, h] for h, gi in pairs}
        do = {}
        for h, gi in pairs:
            o, rh, dout = oraw_ref[rs[gi], s128[h]], r_ref[rs[gi], s128[h]], dog_ref[rs[gi], s128[h]]
            rstd = lax.rsqrt(jnp.mean(o * o, axis=-1, keepdims=True) + EPS)
            on = o * rstd
            sg = _sigmoid(rh)
            dr_ref[rs[gi], s128[h]] = (dout * (on * gnw_v) * (sg * (1.0 + rh * (1.0 - sg)))).astype(ACT_DTYPE)
            dy = dout * (rh * sg)
            dgnw = dgnw + jnp.sum(dy * on, axis=0, keepdims=True)
            don = dy * gnw_v
            do[h, gi] = rstd * (don - on * jnp.mean(don * on, axis=-1, keepdims=True))
        a = {pr: jnp.where(tril, _mm_nt(qh[pr], kh[pr]), 0.0) for pr in pairs}
        da = {pr: jnp.where(tril, _mm_nt(do[pr], vh[pr]), 0.0) for pr in pairs}
        dinc = {pr: _mm_tn(do[pr], qh[pr]) for pr in pairs}
        dgq = {pr: _mm(da[pr], kh[pr]) + _mm(do[pr], st[pr]) for pr in pairs}
        dgk = {pr: _mm_tn(da[pr], qh[pr]) for pr in pairs}
        dv_a = {pr: _mm_tn(a[pr], do[pr]) for pr in pairs}
        dsp = {}
        for h in range(GLA_HEADS):
            cur = dstate[h]
            for gi in reversed(range(group)):
                dsp[h, gi] = cur
                cur = cur * ebl[h, gi] + dinc[h, gi]
            dstate[h] = cur
        for h, gi in pairs:
            pr = (h, gi)
            dkl = _mm(vh[pr], dsp[pr])
            dv_ref[rs[gi], s128[h]] = (dv_a[pr] + _mm_nt(kl[pr], dsp[pr])).astype(ACT_DTYPE)
            debl = jnp.sum(dsp[pr] * st[pr], axis=0, keepdims=True)
            dq_ref[rs[gi], s64[h]] = (dgq[pr] * (scale * eb[rs[gi], s64[h]])).astype(ACT_DTYPE)
            dk_ref[rs[gi], s64[h]] = ((dgk[pr] + dkl * ebl[pr]) * enb[rs[gi], s64[h]]).astype(ACT_DTYPE)
            last = debl * ebl[pr] + jnp.sum(dkl * kl[pr], axis=0, keepdims=True)
            db_scr[rs[gi], s64[h]] = (dgq[pr] * qh[pr] - dgk[pr] * kh[pr] - dkl * kl[pr]
                                      + jnp.where(is_last, last, 0.0))
        dg = jnp.dot(upper.astype(F32), db_scr[...], precision=HIGHEST, preferred_element_type=F32)
        dzg = jnp.where(live, dg * _sigmoid(-zg) * (1.0 / GLA_TAU), 0.0)
        dlr_ref[...] = _mm_nt(dzg, wg).astype(ACT_DTYPE)
        dwg_ref[...] += _mm_tn(lr, dzg)
        dbg_ref[...] += jnp.broadcast_to(jnp.sum(dzg, axis=0, keepdims=True), dbg_ref.shape)
        dgnw_ref[...] += jnp.broadcast_to(dgnw, dgnw_ref.shape)

    nb = lambda w, col: pl.BlockSpec((nrows, w), lambda t: (steps - 1 - t, col // w))
    const = lambda shape: pl.BlockSpec(shape, lambda t: (0,) * len(shape))
    outs = pl.pallas_call(
        body, name="gla_bwd", grid=(steps,),
        in_specs=[nb(256, C_GQ), nb(256, C_GK), nb(512, C_GV), nb(512, C_GR), nb(128, C_LR), nb(512, 0),
                  pl.BlockSpec((group, GLA_HEADS, GLA_DV, GLA_DK), lambda t: (steps - 1 - t, 0, 0, 0)), nb(512, 0),
                  const((128, 256)), const((1, 256)), const((1, 128))] + [ANY] * ns,
        out_specs=[nb(256, 0), nb(256, 0), nb(512, 0), nb(512, 0), nb(128, 0),
                   const((128, 256)), const((8, 256)), const((8, 128))] + [ANY] * ns,
        out_shape=[jax.ShapeDtypeStruct((rows, 256), ACT_DTYPE), jax.ShapeDtypeStruct((rows, 256), ACT_DTYPE),
                   jax.ShapeDtypeStruct((rows, 512), ACT_DTYPE), jax.ShapeDtypeStruct((rows, 512), ACT_DTYPE),
                   jax.ShapeDtypeStruct((rows, 128), ACT_DTYPE), jax.ShapeDtypeStruct((128, 256), F32),
                   jax.ShapeDtypeStruct((8, 256), F32), jax.ShapeDtypeStruct((8, 128), F32)] + jobs.out_shapes,
        scratch_shapes=[pltpu.VMEM((GLA_HEADS, GLA_DV, GLA_DK), F32), pltpu.VMEM((nrows, 256), F32)] + jobs.sems,
        compiler_params=_cp(("arbitrary",)),
    )(proj, proj, proj, proj, proj, oraw, states, dog, wg_p, bg, gnw, *jobs.inputs)
    return outs[:8], jobs.split(outs[8:])


def _in_proj_bwd(x, lead, dh1, nw, win_p, dgv, dgr, dsq, dgq, dgk, dsk, dsv, dlr, tabs, tm):
    seq = x.shape[0]
    rows = LEAD + seq
    nb = tm // LEAD
    steps = rows // tm

    def first_copy(scr, gx_ref, sem):
        return pltpu.make_async_copy(scr.at[pl.ds(LEAD, tm - LEAD)], gx_ref.at[pl.ds(0, tm - LEAD)], sem)

    def tile_copy(scr, gx_ref, sem, step):
        start = pl.multiple_of(jnp.maximum(step * tm - LEAD, 0), LEAD)
        return pltpu.make_async_copy(scr, gx_ref.at[pl.ds(start, tm)], sem)

    def body(*refs):
        x_refs, refs = refs[:nb], refs[nb:]
        (lead_ref, dh1_ref, nw_ref, w_ref, dgv_ref, dgr_ref, dsq_ref, dgq_ref, dgk_ref, dsk_ref, dsv_ref, dlr_ref,
         c_ref, sa_ref, sb_ref, gx_ref, dlead_ref, dproj_ref, ut_ref, gnm_ref, scr, sem) = refs
        i = pl.program_id(0)

        @pl.when(i == 0)
        def _():
            gnm_ref[...] = jnp.zeros_like(gnm_ref)

        cos, sa, sb = c_ref[...], sa_ref[...], sb_ref[...]
        dsq_v = (_unrope(dsq_ref[...], cos, sa, sb) * (SWA_HD ** -0.5)).astype(MXU_DTYPE)
        dsk_v = _unrope(dsk_ref[...], cos, sa, sb).astype(MXU_DTYPE)
        dproj = jnp.concatenate(
            [dgv_ref[...].astype(MXU_DTYPE), dgr_ref[...].astype(MXU_DTYPE), dsq_v, dgq_ref[...].astype(MXU_DTYPE),
             dgk_ref[...].astype(MXU_DTYPE), dsk_v, dsv_ref[...].astype(MXU_DTYPE), dlr_ref[...].astype(MXU_DTYPE)],
            axis=1)
        dproj_ref[...] = dproj
        h = _h_tile(i, lead_ref, x_refs)
        rstd = lax.rsqrt(jnp.mean(h * h, axis=-1, keepdims=True) + EPS)
        hn = h * rstd
        nw_v = nw_ref[...]
        ut_ref[...] = (hn * nw_v).T.astype(ACT_DTYPE)
        du = _mm_nt(dproj, w_ref[...])
        gnm_ref[...] += jnp.broadcast_to(jnp.sum(du * hn, axis=0, keepdims=True), gnm_ref.shape)
        dun = du * nw_v
        dh0 = dh1_ref[...] + rstd * (dun - hn * jnp.mean(dun * hn, axis=-1, keepdims=True))

        if tm > LEAD:
            pl.when(i == 1)(lambda: first_copy(scr, gx_ref, sem).wait())
        pl.when(i > 1)(lambda: tile_copy(scr, gx_ref, sem, i).wait())
        scr[...] = dh0

        @pl.when(i == 0)
        def _():
            dlead_ref[...] = dh0[0:LEAD]
            if tm > LEAD:
                first_copy(scr, gx_ref, sem).start()
                if steps == 1:
                    first_copy(scr, gx_ref, sem).wait()

        @pl.when(i > 0)
        def _():
            tile_copy(scr, gx_ref, sem, i).start()

        if steps > 1:
            pl.when(i == steps - 1)(lambda: tile_copy(scr, gx_ref, sem, i).wait())

    row = lambda w: pl.BlockSpec((tm, w), lambda i: (i, 0))
    const = lambda shape: pl.BlockSpec(shape, lambda i: (0,) * len(shape))
    return pl.pallas_call(
        body, name="in_proj_bwd", grid=(steps,),
        in_specs=_token_specs(tm) + [const((LEAD, D)), row(D), const((1, D)), const((D, DINP)),
                                     row(512), row(512), row(512), row(256), row(256), row(128), row(128), row(128),
                                     row(128), row(128), row(128)],
        out_specs=[ANY, const((LEAD, D)), row(DINP), pl.BlockSpec((D, tm), lambda i: (0, i)), const((8, D))],
        out_shape=[jax.ShapeDtypeStruct((seq, D), F32), jax.ShapeDtypeStruct((LEAD, D), F32),
                   jax.ShapeDtypeStruct((rows, DINP), ACT_DTYPE), jax.ShapeDtypeStruct((D, rows), ACT_DTYPE),
                   jax.ShapeDtypeStruct((8, D), F32)],
        scratch_shapes=[pltpu.VMEM((tm, D), F32), pltpu.SemaphoreType.DMA],
        compiler_params=_cp(("arbitrary",), 56),
    )(*([x] * nb), lead, dh1, nw, win_p, dgv, dgr, dsq, dgq, dgk, dsk, dsv, dlr, *tabs)


def _win_runs():
    groups = [(O_GQ, C_GQ), (O_GK, C_GK), (O_GV, C_GV), (O_GR, C_GR), (O_LR, C_LR), (O_SQ, C_SQ), (O_SK, C_SK),
              (O_SV, C_SV)]
    per = DIN // N_DEV
    runs = []
    for (o0, o1), c0 in groups:
        o = o0
        while o < o1:
            d = o // per
            end = min(o1, (d + 1) * per)
            runs.append((d, o - d * per, c0 + o - o0, end - o))
            o = end
    return runs


def _win_padded(g_in):
    tr = 128

    def body(g_ref, o_ref):
        o_ref[...] = jnp.zeros_like(o_ref)
        for d, s, c, w in _win_runs():
            o_ref[:, c:c + w] = g_ref[d, :, s:s + w]

    return pl.pallas_call(
        body, name="w_in_layout", grid=(D // tr,),
        in_specs=[pl.BlockSpec((N_DEV, tr, DIN // N_DEV), lambda i: (0, i, 0))],
        out_specs=pl.BlockSpec((tr, DINP), lambda i: (i, 0)),
        out_shape=jax.ShapeDtypeStruct((D, DINP), g_in.dtype),
        compiler_params=_cp(("arbitrary",)),
    )(g_in)


def _in_proj_bwd_weights(ut, dproj, tm):
    rows = dproj.shape[0]
    steps = rows // tm
    per = DIN // N_DEV

    def body(ut_ref, dp_ref, out_ref, acc, stage, sems):
        i = pl.program_id(0)

        @pl.when(i == 0)
        def _():
            acc[...] = jnp.zeros_like(acc)

        acc[...] += _mm(ut_ref[...], dp_ref[...])

        @pl.when(i == steps - 1)
        def _():
            copies = []
            for d in range(N_DEV):
                slot = d % 2
                if d >= 2:
                    copies[d - 2].wait()
                for owner, s, c, w in _win_runs():
                    if owner == d:
                        stage[slot, :, s:s + w] = acc[:, c:c + w]
                cp = pltpu.make_async_copy(stage.at[slot], out_ref.at[d % 2, d // 2], sems.at[slot])
                cp.start()
                copies.append(cp)
            copies[N_DEV - 2].wait()
            copies[N_DEV - 1].wait()

    return pl.pallas_call(
        body, name="in_proj_bwd_weights", grid=(steps,),
        in_specs=[pl.BlockSpec((D, tm), lambda i: (0, i)), pl.BlockSpec((tm, DINP), lambda i: (i, 0))],
        out_specs=ANY,
        out_shape=jax.ShapeDtypeStruct((2, 4, D, per), F32),
        scratch_shapes=[pltpu.VMEM((D, DINP), F32), pltpu.VMEM((2, D, per), F32), pltpu.SemaphoreType.DMA((2,))],
        compiler_params=_cp(("arbitrary",), 56),
    )(ut, dproj)


def _adamw(w, g, m, v):
    m = ADAM_B1 * m + (1.0 - ADAM_B1) * g
    v = ADAM_B2 * v + (1.0 - ADAM_B2) * jnp.square(g)
    m_hat = m / (1.0 - ADAM_B1 ** ADAM_STEP)
    v_hat = v / (1.0 - ADAM_B2 ** ADAM_STEP)
    delta = -ADAM_LR * (m_hat / (jnp.sqrt(v_hat) + ADAM_EPS) + ADAM_WD * w)
    return delta, m, v


ADAM_STEPS = 8


def _adamw_shards(where, items, name, jobs=None):
    jobs = jobs or _Jobs([])
    ns, nw = jobs.n, len(items)

    def body(where_ref, *rest):
        ins, rest = rest[:5 * nw], rest[5 * nw:]
        job_ins, rest = rest[:ns], rest[ns:]
        outs, rest = rest[:4 * nw], rest[4 * nw:]
        start, finish = jobs.bind(job_ins, rest[:ns], rest[ns:])
        i = pl.program_id(0)
        pl.when(i == 0)(start)
        pl.when(i == ADAM_STEPS - 1)(finish)
        for k in range(nw):
            p_ref, own_ref, w_ref, m_ref, v_ref = ins[5 * k:5 * k + 5]
            g_ref, d_ref, nm_ref, nv_ref = outs[4 * k:4 * k + 4]
            g = ((p_ref[0].astype(F32) + p_ref[1].astype(F32)) + p_ref[2].astype(F32)) + own_ref[...]
            g_ref[...] = g
            d_ref[...], nm_ref[...], nv_ref[...] = _adamw(w_ref[...], g, m_ref[...], v_ref[...])

    in_specs, out_specs, out_shape, operands = [], [], [], []
    for parts, own, w, m, v in items:
        r, cdim = w.shape
        tr = r // ADAM_STEPS
        spec = pl.BlockSpec((tr, cdim), lambda i, s: (i, 0))
        in_specs += [pl.BlockSpec((3, tr, cdim), lambda i, s: (0, i, 0)),
                     pl.BlockSpec((None, tr, cdim), lambda i, s: (s[1], i, 0)), spec, spec, spec]
        out_specs += [spec] * 4
        out_shape += [jax.ShapeDtypeStruct((r, cdim), F32)] * 4
        operands += [parts, own, w, m, v]
    outs = pl.pallas_call(
        body, name=name,
        grid_spec=pltpu.PrefetchScalarGridSpec(
            num_scalar_prefetch=1, grid=(ADAM_STEPS,),
            in_specs=in_specs + [ANY] * ns, out_specs=out_specs + [ANY] * ns, scratch_shapes=jobs.sems),
        out_shape=out_shape + jobs.out_shapes,
        compiler_params=_cp(("arbitrary",)),
    )(where, *operands, *jobs.inputs)
    return [outs[4 * k:4 * k + 4] for k in range(nw)], jobs.split(outs[4 * nw:])


def _adamw_small(items):
    n = len(items)

    def body(*refs):
        ins, outs = refs[:4 * n], refs[4 * n:]
        for k in range(n):
            w_ref, g_ref, m_ref, v_ref = ins[4 * k:4 * k + 4]
            d_ref, nm_ref, nv_ref = outs[3 * k:3 * k + 3]
            d_ref[...], nm_ref[...], nv_ref[...] = _adamw(w_ref[...], g_ref[...], m_ref[...], v_ref[...])

    vm = pl.BlockSpec(memory_space=pltpu.VMEM)
    shapes = [jax.ShapeDtypeStruct(w.shape, F32) for w, _, _, _ in items for _ in range(3)]
    outs = pl.pallas_call(body, name="adamw_small", in_specs=[vm] * (4 * n), out_specs=[vm] * (3 * n),
                          out_shape=shapes)(*[t for item in items for t in item])
    return [outs[3 * k:3 * k + 3] for k in range(n)]


def _add_own_half(where, full, theirs, name, wire_copy=False):
    _, _, r, cdim = full.shape
    tr = 128 if r % 128 == 0 else r

    def body(where_ref, a_ref, b_ref, *o_refs):
        total = a_ref[...] + b_ref[...]
        o_refs[0][...] = total
        if wire_copy:
            o_refs[1][...] = total.astype(WIRE_DTYPE)

    spec = pl.BlockSpec((4, tr, cdim), lambda i, s: (0, i, 0))
    shapes = [jax.ShapeDtypeStruct(theirs.shape, F32)] + ([jax.ShapeDtypeStruct(theirs.shape, WIRE_DTYPE)] if wire_copy else [])
    outs = pl.pallas_call(
        body, name=name,
        grid_spec=pltpu.PrefetchScalarGridSpec(
            num_scalar_prefetch=1, grid=(r // tr,),
            in_specs=[pl.BlockSpec((None, 4, tr, cdim), lambda i, s: (s[0], 0, i, 0)), spec],
            out_specs=[spec] * len(shapes)),
        out_shape=shapes, compiler_params=_cp(("arbitrary",)))(where, full, theirs)
    return outs if wire_copy else outs[0]


def kernel(x, meta_tokens, norm_mix_w, w_in, w_gate_up, b_gate, gla_norm_w, sinks, w_out, norm_ff_w, w_ff1, w_ff2, final_norm_w, loss_target, m_meta_tokens, m_norm_mix_w, m_w_in, m_w_gate_up, m_b_gate, m_gla_norm_w, m_sinks, m_w_out, m_norm_ff_w, m_w_ff1, m_w_ff2, m_final_norm_w, v_meta_tokens, v_norm_mix_w, v_w_in, v_w_gate_up, v_b_gate, v_gla_norm_w, v_sinks, v_w_out, v_norm_ff_w, v_w_ff1, v_w_ff2, v_final_norm_w):
    seq = x.shape[1]
    rows = LEAD + seq
    tm = _row_tile(rows)
    tm_wide = 1664 if rows % 1664 == 0 else tm
    dev =4 * lax.axis_index("x") + 2 * lax.axis_index("y") + lax.axis_index("c")

    small_shard = jnp.concatenate([meta_tokens, w_gate_up[0], jnp.zeros((N_META, 96), F32)], axis=1)
    g_in, g_small = _all_gather([w_in[0].astype(WIRE_DTYPE), small_shard])
    later_shards = [w_out[0].astype(WIRE_DTYPE), w_ff1[0].astype(WIRE_DTYPE), w_ff2[0].astype(WIRE_DTYPE)]
    win_p = _win_padded(g_in)
    meta_full = jnp.transpose(g_small[:, :, 0:128], (1, 0, 2)).reshape(N_META, D)
    wg_full = jnp.transpose(g_small[:, :, 128:160], (1, 0, 2)).reshape(GLA_RANK, GLA_HEADS * GLA_DK)
    wg_p = jnp.concatenate([wg_full, jnp.zeros((128 - GLA_RANK, 256), F32)], axis=0)

    lead = jnp.concatenate([jnp.zeros((META0, D), F32), meta_full], axis=0)
    tabs = _rope_tables(rows)
    proj = _in_proj(x[0], lead, norm_mix_w, win_p, tm)
    oraw, og, states, (g_out, g_w1) = _gla_fwd(proj, wg_p, b_gate, gla_norm_w, later_shards[0:2])
    qr, kr, vr = _swa_prep(proj, tabs, tm)
    osw, (g_w2,) = _swa_fwd(qr, kr, vr, sinks, later_shards[2:3])
    wout_full = g_out.reshape(D, D)
    w2_full = g_w2.reshape(D_FF, D)
    w1_full = jnp.transpose(g_w1, (1, 0, 2)).reshape(D, D_FF)
    h1, f, ft = _out_proj(x[0], lead, og, osw, wout_full, norm_ff_w, tm)
    a, dh2, dh2t, loss_p, gfn_p = _ffn_fwd(f, h1, w1_full, w2_full, loss_target[0], final_norm_w.reshape(1, D), tm)

    da, dh1, gnf_p = _ffn_bwd_act(dh2, a, w1_full, w2_full, h1, norm_ff_w, tm)
    dw1, dw2 = _ffn_bwd_weights(ft, a, da, dh2t, tm_wide)
    where = jnp.stack([lax.axis_index("c"), 2 * lax.axis_index("x") + lax.axis_index("y")]).astype(jnp.int32)
    dog, dos, dwout, theirs_ffn = _out_proj_bwd(dh1, og, osw, wout_full, tm, [dw1, dw2])
    pairs_ffn = [_add_own_half(where, p, q, "reduce_pair_%d" % (2 + k), wire_copy=True)
                 for k, (p, q) in enumerate(zip([dw1, dw2], theirs_ffn))]
    sums_ffn, wires_ffn = [p[0] for p in pairs_ffn], [p[1] for p in pairs_ffn]
    dsq, dsk, dsv, dsink_p, (parts_ffn, (theirs_wout,)) = _swa_bwd(
        qr, kr, vr, osw, dos, sinks, _Jobs([("chips", wires_ffn), ("sibling", [dwout])]))
    sum_wout, wire_wout = _add_own_half(where, dwout, theirs_wout, "reduce_pair_1", wire_copy=True)
    (dgq, dgk, dgv, dgr, dlr, dwg_p, dbg_p, dgnw_p), ((parts_wout,),) = _gla_bwd(
        proj, oraw, states, dog, wg_p, b_gate, gla_norm_w, _Jobs([("chips", [wire_wout])]))
    grad_x, dlead, dproj, ut, gnm_p = _in_proj_bwd(x[0], lead, dh1, norm_mix_w, win_p, dgv, dgr, dsq, dgq, dgk, dsk,
                                                   dsv, dlr, tabs, tm)
    grad_x = grad_x[None]
    dwin = _in_proj_bwd_weights(ut, dproj, tm_wide)

    (theirs_win,) = _rs_sibling([dwin])
    sum_win, sum_win_wire = _add_own_half(where, dwin, theirs_win, "reduce_pair_0", wire_copy=True)

    total = _all_reduce_small(dlead, dwg_p, gnm_p, gnf_p, gfn_p, dbg_p, dgnw_p, loss_p, dsink_p)
    g_meta = lax.dynamic_slice(total, (R_META, dev * 128), (N_META, 128))
    g_wg = lax.dynamic_slice(total, (R_WG, dev * 32), (GLA_RANK, 32))
    g_norm_mix, g_norm_ff = total[R_NORM_MIX:R_NORM_MIX + 1], total[R_NORM_FF:R_NORM_FF + 1]
    g_final_norm = total[R_FINAL:R_FINAL + 1]
    g_b_gate, g_gla_norm = total[R_B_GATE:R_B_GATE + 1, 0:256], total[R_GLA_NORM:R_GLA_NORM + 1, 0:128]
    g_sinks = total[R_SINKS:R_SINKS + SWA_HEADS, 0].reshape(1, SWA_HEADS)
    loss = total[R_LOSS, 0]

    ((g_wout, d_wout, nm_wout, nv_wout), (g_w1s, d_w1, nm_w1, nv_w1), (g_w2s, d_w2, nm_w2, nv_w2)), ((parts_win,),) = \
        _adamw_shards(where, [(parts_wout, sum_wout, w_out[0], m_w_out[0], v_w_out[0]),
                              (parts_ffn[0], sums_ffn[0], w_ff1[0], m_w_ff1[0], v_w_ff1[0]),
                              (parts_ffn[1], sums_ffn[1], w_ff2[0], m_w_ff2[0], v_w_ff2[0])],
                      "adamw_w_out_ff", _Jobs([("chips", [sum_win_wire])]))
    ((g_win, d_win, nm_win, nv_win),), _ = _adamw_shards(
        where, [(parts_win, sum_win, w_in[0], m_w_in[0], v_w_in[0])], "adamw_w_in")

    names = ["meta", "wg", "norm_mix", "b_gate", "gla_norm", "sinks", "norm_ff", "final_norm"]
    ws = [meta_tokens, w_gate_up, norm_mix_w, b_gate, gla_norm_w, sinks, norm_ff_w, final_norm_w]
    gs = [g_meta, g_wg, g_norm_mix, g_b_gate, g_gla_norm, g_sinks, g_norm_ff, g_final_norm]
    ms = [m_meta_tokens, m_w_gate_up, m_norm_mix_w, m_b_gate, m_gla_norm_w, m_sinks, m_norm_ff_w, m_final_norm_w]
    vs = [v_meta_tokens, v_w_gate_up, v_norm_mix_w, v_b_gate, v_gla_norm_w, v_sinks, v_norm_ff_w, v_final_norm_w]
    flat = lambda t: t.reshape(-1, t.shape[-1])
    small_out = _adamw_small([(flat(w), flat(g), flat(m), flat(v)) for w, g, m, v in zip(ws, gs, ms, vs)])
    d_small = {n: small_out[k][0].reshape(ws[k].shape) for k, n in enumerate(names)}
    nm_small = {n: small_out[k][1].reshape(ws[k].shape) for k, n in enumerate(names)}
    nv_small = {n: small_out[k][2].reshape(ws[k].shape) for k, n in enumerate(names)}
    g_small_d = {n: g.reshape(ws[k].shape) for k, (n, g) in enumerate(zip(names, gs))}

    def ordered(big, small_d):
        win_v, wout_v, w1_v, w2_v = big
        return (small_d["meta"], small_d["norm_mix"], win_v[None], small_d["wg"], small_d["b_gate"],
                small_d["gla_norm"], small_d["sinks"], wout_v[None], small_d["norm_ff"], w1_v[None], w2_v[None],
                small_d["final_norm"])

    return (loss, grad_x,
            *ordered((g_win, g_wout, g_w1s, g_w2s), g_small_d),
            *ordered((d_win, d_wout, d_w1, d_w2), d_small),
            *ordered((nm_win, nm_wout, nm_w1, nm_w2), nm_small),
            *ordered((nv_win, nv_wout, nv_w1, nv_w2), nv_small))
```

```python
import functools

import jax
import jax.numpy as jnp
from jax import lax
from jax.experimental import pallas as pl
from jax.experimental.pallas import tpu as pltpu

F32 = jnp.float32
MXU_DTYPE = jnp.bfloat16
ACT_DTYPE = jnp.bfloat16
WIRE_DTYPE = jnp.bfloat16

D = 1024
N_META = 16
LEAD = 128
META0 = LEAD - N_META
EPS = 1e-5
GLA_HEADS, GLA_DK, GLA_DV, GLA_RANK, GLA_CHUNK = 4, 64, 128, 16, 64
GLA_TAU = 16.0
SWA_HEADS, SWA_KV, SWA_GROUP, SWA_HD, SWA_BLOCK = 8, 2, 4, 64, 128
ROPE_DIM, ROPE_THETA = 16, 500000.0
D_FF = 4096
N_DEV = 8
FF_TILE = D_FF // N_DEV
FF_WIDE = 2048
NEG = -1e30

C_GV, C_GR, C_GQ, C_GK, C_LR, C_SQ, C_SK, C_SV = 0, 512, 1024, 1280, 1536, 1664, 2176, 2304
DGLA = 1664
DINP = 2432
DIN = 2320
O_GQ, O_GK, O_GV, O_GR, O_LR, O_SQ, O_SK, O_SV = (0, 256), (256, 512), (512, 1024), (1024, 1536), (1536, 1552), (1552, 2064), (2064, 2192), (2192, 2320)

ADAM_LR, ADAM_B1, ADAM_B2, ADAM_EPS, ADAM_WD, ADAM_STEP = 0.001, 0.9, 0.999, 1e-08, 0.01, 10

MESH = pl.DeviceIdType.MESH
ANY = pl.BlockSpec(memory_space=pl.ANY)
HIGHEST = lax.Precision.HIGHEST


def _cp(sem=None, vmem_mb=None):
    kw = {}
    if sem is not None:
        kw["dimension_semantics"] = sem
    if vmem_mb is not None:
        kw["vmem_limit_bytes"] = vmem_mb << 20
    return pltpu.CompilerParams(**kw)


def _mm(a, b):
    return jnp.dot(a.astype(MXU_DTYPE), b.astype(MXU_DTYPE), preferred_element_type=F32)


def _mm_nt(a, b):
    return lax.dot_general(a.astype(MXU_DTYPE), b.astype(MXU_DTYPE), (((1,), (1,)), ((), ())),
                           preferred_element_type=F32)


def _mm_tn(a, b):
    return lax.dot_general(a.astype(MXU_DTYPE), b.astype(MXU_DTYPE), (((0,), (0,)), ((), ())),
                           preferred_element_type=F32)


def _logsigmoid(z):
    return jnp.minimum(z, 0.0) - jnp.log(1.0 + jnp.exp(-jnp.abs(z)))


def _sigmoid(z):
    return 1.0 / (1.0 + jnp.exp(-z))


def _row_tile(rows):
    return 640 if rows % 640 == 0 else 128


def _mesh_pos():
    return lax.axis_index("x"), lax.axis_index("y"), lax.axis_index("c")


def _all_gather(shards):
    n = len(shards)

    def body(*refs):
        start, forward, finish = _gather_schedule(refs[:n], refs[n:2 * n], *refs[2 * n:])
        start()
        for j in range(3):
            forward(j)
        finish()

    gathered = pl.pallas_call(
        body, name="all_gather_weights",
        out_shape=_gathered_shapes(shards), in_specs=[ANY] * n, out_specs=[ANY] * n,
        scratch_shapes=_gather_sems(n),
    )(*shards)
    return _with_own_block(gathered, shards)


def _gathered_shapes(shards):
    return [jax.ShapeDtypeStruct((N_DEV,) + s.shape, s.dtype) for s in shards]


def _gather_sems(n):
    return [pltpu.SemaphoreType.DMA((7 * n,)), pltpu.SemaphoreType.DMA((7 * n,))] if n else []


def _place_gather(step, steps, shard_refs, gathered_refs, sems):
    if not shard_refs:
        return
    start, forward, finish = _gather_schedule(shard_refs, gathered_refs, *sems)
    pl.when(step == 0)(start)
    for j, at in enumerate((steps * 7 // 10, steps * 8 // 10, steps * 9 // 10)):
        pl.when(step == at)(functools.partial(forward, j))
    pl.when(step == steps - 1)(finish)


def _with_own_block(gathered, shards):
    dev = 4 * lax.axis_index("x") + 2 * lax.axis_index("y") + lax.axis_index("c")
    return [lax.dynamic_update_index_in_dim(g, s, dev, 0) for g, s in zip(gathered, shards)]


def _gather_schedule(ins, outs, send_sems, recv_sems):
    n = len(ins)
    x, y, c = _mesh_pos()
    me, sibling = (x, y, c), (x, y, 1 - c)
    chips = [(1 - x, y), (x, 1 - y), (1 - x, 1 - y)]

    def copy(a, k, block, to, src=None):
        dst = outs[a].at[4 * block[0] + 2 * block[1] + block[2]]
        return pltpu.make_async_remote_copy(
            src_ref=dst if src is None else src, dst_ref=dst,
            send_sem=send_sems.at[a * 7 + k], recv_sem=recv_sems.at[a * 7 + k],
            device_id=to, device_id_type=MESH)

    def first(a):
        return [copy(a, 0, me, sibling, src=ins[a])] + [copy(a, 1 + j, me, (*chip, c), src=ins[a])
                                                        for j, chip in enumerate(chips)]

    def start():
        for a in range(n):
            for cp in first(a):
                cp.start()

    def forward(j):
        for a in range(n):
            copy(a, 1 + j, (*chips[j], c), me).wait_recv()
            copy(a, 4 + j, (*chips[j], c), sibling).start()

    def finish():
        for a in range(n):
            copy(a, 0, sibling, me).wait_recv()
            for j, chip in enumerate(chips):
                copy(a, 4 + j, (*chip, 1 - c), me).wait_recv()
        for a in range(n):
            for cp in first(a) + [copy(a, 4 + j, (*chip, c), sibling) for j, chip in enumerate(chips)]:
                cp.wait_send()

    return start, forward, finish


def _rs_sibling(gs):
    n = len(gs)

    def body(*refs):
        start, finish = _sibling_schedule(refs[:n], refs[n:2 * n], *refs[2 * n:])
        start()
        finish()

    return pl.pallas_call(
        body, name="reduce_scatter_sibling",
        out_shape=_sibling_shapes(gs), in_specs=[ANY] * n, out_specs=[ANY] * n,
        scratch_shapes=_sibling_sems(n),
    )(*gs)


def _sibling_shapes(gs):
    return [jax.ShapeDtypeStruct(g.shape[1:], g.dtype) for g in gs]


def _sibling_sems(n):
    return [pltpu.SemaphoreType.DMA((n,)), pltpu.SemaphoreType.DMA((n,))]


def _sibling_schedule(ins, land, send_sems, recv_sems):
    x, y, c = _mesh_pos()

    def copies():
        return [pltpu.make_async_remote_copy(
            src_ref=ins[a].at[1 - c], dst_ref=land[a], send_sem=send_sems.at[a], recv_sem=recv_sems.at[a],
            device_id=(x, y, 1 - c), device_id_type=MESH) for a in range(len(ins))]

    def start():
        for cp in copies():
            cp.start()

    def finish():
        for cp in copies():
            cp.wait_recv()
        for cp in copies():
            cp.wait_send()

    return start, finish


def _rs_chips(ps):
    n = len(ps)

    def body(*refs):
        start, finish = _chips_schedule(refs[:n], refs[n:2 * n], *refs[2 * n:])
        start()
        finish()

    return pl.pallas_call(
        body, name="reduce_scatter_chips",
        out_shape=_chips_shapes(ps), in_specs=[ANY] * n, out_specs=[ANY] * n,
        scratch_shapes=_chips_sems(n),
    )(*ps)


def _chips_shapes(ps):
    return [jax.ShapeDtypeStruct((3,) + p.shape[1:], p.dtype) for p in ps]


def _chips_sems(n):
    return [pltpu.SemaphoreType.DMA((3 * n,)), pltpu.SemaphoreType.DMA((3 * n,))]


def _chips_schedule(ins, land, send_sems, recv_sems):
    x, y, c = _mesh_pos()
    chips = [(1 - x, y), (x, 1 - y), (1 - x, 1 - y)]

    def copies():
        return [pltpu.make_async_remote_copy(
            src_ref=ins[a].at[2 * chip[0] + chip[1]], dst_ref=land[a].at[j],
            send_sem=send_sems.at[3 * a + j], recv_sem=recv_sems.at[3 * a + j],
            device_id=(*chip, c), device_id_type=MESH) for a in range(len(ins)) for j, chip in enumerate(chips)]

    def start():
        for cp in copies():
            cp.start()

    def finish():
        for cp in copies():
            cp.wait_recv()
        for cp in copies():
            cp.wait_send()

    return start, finish


class _Jobs:
    def __init__(self, jobs):
        self.jobs = jobs
        self.inputs = [a for _, arrs in jobs for a in arrs]
        self.out_shapes = [s for kind, arrs in jobs
                           for s in (_sibling_shapes(arrs) if kind == "sibling" else _chips_shapes(arrs))]
        self.sems = [s for kind, arrs in jobs
                     for s in (_sibling_sems(len(arrs)) if kind == "sibling" else _chips_sems(len(arrs)))]
        self.n = len(self.inputs)

    def bind(self, in_refs, out_refs, sem_refs):
        starts, finishes, at = [], [], 0
        for k, (kind, arrs) in enumerate(self.jobs):
            schedule = _sibling_schedule if kind == "sibling" else _chips_schedule
            start, finish = schedule(in_refs[at:at + len(arrs)], out_refs[at:at + len(arrs)],
                                     sem_refs[2 * k], sem_refs[2 * k + 1])
            starts.append(start)
            finishes.append(finish)
            at += len(arrs)

        def start_all():
            for f in starts:
                f()

        def finish_all():
            for f in finishes:
                f()

        return start_all, finish_all

    def split(self, outs):
        res, at = [], 0
        for _, arrs in self.jobs:
            res.append(list(outs[at:at + len(arrs)]))
            at += len(arrs)
        return res


R_META, R_WG, R_NORM_MIX, R_NORM_FF, R_FINAL, R_B_GATE, R_GLA_NORM, R_LOSS, R_SINKS, SMALL_ROWS = 0, 16, 32, 33, 34, 35, 36, 37, 40, 48


SMALL_SPECS = [pl.BlockSpec((LEAD, D), lambda i: (0, 0)), pl.BlockSpec((128, 256), lambda i: (0, 0)),
               pl.BlockSpec((8, D), lambda i: (0, 0)), pl.BlockSpec((8, D), lambda i: (0, 0)),
               pl.BlockSpec((8, D), lambda i: (0, 0)), pl.BlockSpec((8, 256), lambda i: (0, 0)),
               pl.BlockSpec((8, 128), lambda i: (0, 0)), pl.BlockSpec((8, 128), lambda i: (0, 0)),
               pl.BlockSpec((8, 128), lambda i: (0, 0))]


def _small_sum_scratch():
    return [pltpu.VMEM((SMALL_ROWS, D), F32), pltpu.VMEM((N_DEV, SMALL_ROWS, D), F32),
            pltpu.SemaphoreType.DMA((7,)), pltpu.SemaphoreType.DMA((7,))]


def _small_sum_schedule(small_refs, out_ref, p_ref, land, send_sems, recv_sems):
    dlead_ref, dwg_ref, gnm_ref, gnf_ref, gfn_ref, dbg_ref, dgnw_ref, loss_ref, dsink_ref = small_refs
    x, y, c = _mesh_pos()
    me = 4 * x + 2 * y + c

    def copies():
        res = []
        for k in range(1, N_DEV):
            bx, by, bc = (k >> 2) & 1, (k >> 1) & 1, k & 1
            peer = (1 - x if bx else x, 1 - y if by else y, 1 - c if bc else c)
            res.append(pltpu.make_async_remote_copy(
                src_ref=p_ref, dst_ref=land.at[me], send_sem=send_sems.at[k - 1], recv_sem=recv_sems.at[k - 1],
                device_id=peer, device_id_type=MESH))
        return res

    def start():
        p_ref[...] = jnp.zeros_like(p_ref)
        p_ref[R_META:R_META + N_META, :] = dlead_ref[META0:LEAD, :]
        p_ref[R_WG:R_WG + GLA_RANK, 0:256] = dwg_ref[0:GLA_RANK, :]
        p_ref[R_NORM_MIX:R_NORM_MIX + 1, :] = gnm_ref[0:1, :]
        p_ref[R_NORM_FF:R_NORM_FF + 1, :] = gnf_ref[0:1, :]
        p_ref[R_FINAL:R_FINAL + 1, :] = gfn_ref[0:1, :]
        p_ref[R_B_GATE:R_B_GATE + 1, 0:256] = dbg_ref[0:1, :]
        p_ref[R_GLA_NORM:R_GLA_NORM + 1, 0:128] = dgnw_ref[0:1, :]
        p_ref[R_LOSS:R_LOSS + 1, 0:128] = loss_ref[0:1, :]
        p_ref[R_SINKS:R_SINKS + SWA_HEADS, 0:128] = dsink_ref[...]
        land[me] = p_ref[...]
        for cp in copies():
            cp.start()

    def finish():
        for cp in copies():
            cp.wait_recv()
        for cp in copies():
            cp.wait_send()
        acc = land[0]
        for d in range(1, N_DEV):
            acc = acc + land[d]
        out_ref[...] = acc

    return start, finish


def _token_specs(tm, grid_rank=1):
    nb = tm // LEAD

    def spec(k):
        if grid_rank == 1:
            return pl.BlockSpec((LEAD, D), lambda i: (jnp.maximum(i * nb + k - 1, 0), 0))
        return pl.BlockSpec((LEAD, D), lambda i, j: (jnp.maximum(i * nb + k - 1, 0), 0))

    return [spec(k) for k in range(nb)]


def _h_tile(i, lead_ref, x_refs):
    first = jnp.where(i == 0, lead_ref[...], x_refs[0][...])
    return jnp.concatenate([first] + [r[...] for r in x_refs[1:]], axis=0)


def _in_proj(x, lead, nw, win_p, tabs, tm):
    rows = LEAD + x.shape[0]
    nb = tm // LEAD

    def body(*refs):
        x_refs, (lead_ref, nw_ref, w_ref, c_ref, sa_ref, sb_ref, o_ref, q_ref, k_ref, v_ref) = refs[:nb], refs[nb:]
        h = _h_tile(pl.program_id(0), lead_ref, x_refs)
        rstd = lax.rsqrt(jnp.mean(h * h, axis=-1, keepdims=True) + EPS)
        u = (h * rstd * nw_ref[...]).astype(MXU_DTYPE)
        proj = jnp.dot(u, w_ref[...].astype(MXU_DTYPE), preferred_element_type=F32)
        o_ref[...] = proj[:, 0:DGLA]
        cos, sa, sb = c_ref[...], sa_ref[...], sb_ref[...]
        q_ref[...] = (_rope(proj[:, C_SQ:C_SK], cos, sa, sb) * (SWA_HD ** -0.5)).astype(ACT_DTYPE)
        k_ref[...] = _rope(proj[:, C_SK:C_SV], cos, sa, sb).astype(ACT_DTYPE)
        v_ref[...] = proj[:, C_SV:DINP].astype(ACT_DTYPE)

    row = lambda w: pl.BlockSpec((tm, w), lambda i: (i, 0))
    return pl.pallas_call(
        body, name="in_proj", grid=(rows // tm,),
        in_specs=_token_specs(tm) + [pl.BlockSpec((LEAD, D), lambda i: (0, 0)), pl.BlockSpec((1, D), lambda i: (0, 0)),
                                     pl.BlockSpec((D, DINP), lambda i: (0, 0)), row(128), row(128), row(128)],
        out_specs=[row(DGLA), row(512), row(128), row(128)],
        out_shape=[jax.ShapeDtypeStruct((rows, DGLA), F32), jax.ShapeDtypeStruct((rows, 512), ACT_DTYPE),
                   jax.ShapeDtypeStruct((rows, 128), ACT_DTYPE), jax.ShapeDtypeStruct((rows, 128), ACT_DTYPE)],
        compiler_params=_cp(("arbitrary",), 56),
    )(*([x] * nb), lead, nw, win_p, *tabs)


def _rope_tables(rows):
    pos = (jnp.arange(rows, dtype=jnp.int32) - META0).astype(F32)
    inv_freq = 1.0 / (ROPE_THETA ** (jnp.arange(0, ROPE_DIM, 2, dtype=F32) / ROPE_DIM))
    ang = pos[:, None] * jnp.tile(inv_freq, 128 // (ROPE_DIM // 2))[None, :]
    in_head = jnp.arange(128, dtype=jnp.int32)[None, :] % SWA_HD
    cos, sin = jnp.cos(ang), jnp.sin(ang)
    c_tab = jnp.where(in_head < ROPE_DIM, cos, 1.0)
    sa_tab = jnp.where(in_head < ROPE_DIM // 2, -sin, 0.0)
    sb_tab = jnp.where((in_head >= ROPE_DIM // 2) & (in_head < ROPE_DIM), sin, 0.0)
    return c_tab, sa_tab, sb_tab


def _rope(xv, cos, sa, sb):
    width = xv.shape[1]
    reps = width // 128
    if reps > 1:
        cos, sa, sb = (jnp.tile(t, (1, reps)) for t in (cos, sa, sb))
    return xv * cos + pltpu.roll(xv, width - 8, 1) * sa + pltpu.roll(xv, 8, 1) * sb


def _unrope(dy, cos, sa, sb):
    width = dy.shape[1]
    reps = width // 128
    if reps > 1:
        cos, sa, sb = (jnp.tile(t, (1, reps)) for t in (cos, sa, sb))
    return dy * cos + pltpu.roll(dy * sa, 8, 1) + pltpu.roll(dy * sb, width - 8, 1)


def _gla_group(nc):
    for g in (5, 2):
        if nc % g == 0:
            return g
    return 1


def _gla_gates(lr, wg, bg, first_row, nrows):
    zg = _mm(lr, wg) + bg
    row = first_row + lax.broadcasted_iota(jnp.int32, (nrows, 1), 0)
    live = row >= META0
    g = jnp.where(live, _logsigmoid(zg) * (1.0 / GLA_TAU), 0.0)
    ii = lax.broadcasted_iota(jnp.int32, (nrows, nrows), 0)
    jj = lax.broadcasted_iota(jnp.int32, (nrows, nrows), 1)
    same = (ii // GLA_CHUNK) == (jj // GLA_CHUNK)
    lower, upper = same & (jj <= ii), same & (jj >= ii)
    b = jnp.dot(lower.astype(F32), g, precision=HIGHEST, preferred_element_type=F32)
    return zg, live, lower, upper, b


def _tril64():
    ii = lax.broadcasted_iota(jnp.int32, (GLA_CHUNK, GLA_CHUNK), 0)
    jj = lax.broadcasted_iota(jnp.int32, (GLA_CHUNK, GLA_CHUNK), 1)
    return jj <= ii


def _gla_fwd(proj, wg_p, bg, gnw, shards):
    rows = proj.shape[0]
    nc = rows // GLA_CHUNK
    group = _gla_group(nc)
    steps, nrows = nc // group, group * GLA_CHUNK
    ns = len(shards)

    def body(q_ref, k_ref, v_ref, r_ref, lr_ref, wg_ref, bg_ref, gnw_ref, *rest):
        shard_refs, rest = rest[:ns], rest[ns:]
        oraw_ref, og_ref, st_ref = rest[:3]
        gathered_refs, rest = rest[3:3 + ns], rest[3 + ns:]
        state = rest[0]
        c = pl.program_id(0)

        @pl.when(c == 0)
        def _():
            state[...] = jnp.zeros_like(state)

        _place_gather(c, steps, shard_refs, gathered_refs, rest[1:])
        _, _, _, _, b = _gla_gates(lr_ref[...], wg_ref[...], bg_ref[...], c * nrows, nrows)
        eb = jnp.exp(b)
        gq = q_ref[...] * (GLA_DK ** -0.5) * eb
        gk = k_ref[...] * jnp.exp(-b)
        v = v_ref[...]
        gnw_v = gnw_ref[...]
        tril = _tril64()
        pairs = [(h, gi) for h in range(GLA_HEADS) for gi in range(group)]
        rs = {gi: slice(gi * GLA_CHUNK, (gi + 1) * GLA_CHUNK) for gi in range(group)}
        s64 = {h: slice(h * GLA_DK, (h + 1) * GLA_DK) for h in range(GLA_HEADS)}
        s128 = {h: slice(h * GLA_DV, (h + 1) * GLA_DV) for h in range(GLA_HEADS)}
        qh = {(h, gi): gq[rs[gi], s64[h]] for h, gi in pairs}
        kh = {(h, gi): gk[rs[gi], s64[h]] for h, gi in pairs}
        vh = {(h, gi): v[rs[gi], s128[h]] for h, gi in pairs}
        ebl = {(h, gi): eb[(gi + 1) * GLA_CHUNK - 1:(gi + 1) * GLA_CHUNK, s64[h]] for h, gi in pairs}
        av = {pr: _mm(jnp.where(tril, _mm_nt(qh[pr], kh[pr]), 0.0), vh[pr]) for pr in pairs}
        inc = {pr: _mm_tn(vh[pr], kh[pr] * ebl[pr]) for pr in pairs}
        st = {}
        for h in range(GLA_HEADS):
            cur = state[h]
            for gi in range(group):
                st[h, gi] = cur
                st_ref[gi, h] = cur
                cur = cur * ebl[h, gi] + inc[h, gi]
            state[h] = cur
        for h, gi in pairs:
            o = av[h, gi] + _mm_nt(qh[h, gi], st[h, gi])
            oraw_ref[rs[gi], s128[h]] = o
            rstd = lax.rsqrt(jnp.mean(o * o, axis=-1, keepdims=True) + EPS)
            rh = r_ref[rs[gi], s128[h]]
            og_ref[rs[gi], s128[h]] = (o * rstd * gnw_v * (rh * _sigmoid(rh))).astype(ACT_DTYPE)

    nb = lambda w, col: pl.BlockSpec((nrows, w), lambda c: (c, col // w))
    const = lambda shape: pl.BlockSpec(shape, lambda c: (0,) * len(shape))
    outs = pl.pallas_call(
        body, name="gla_fwd", grid=(steps,),
        in_specs=[nb(256, C_GQ), nb(256, C_GK), nb(512, C_GV), nb(512, C_GR), nb(128, C_LR),
                  const((128, 256)), const((1, 256)), const((1, 128))] + [ANY] * ns,
        out_specs=[pl.BlockSpec((nrows, 512), lambda c: (c, 0)), pl.BlockSpec((nrows, 512), lambda c: (c, 0)),
                   pl.BlockSpec((group, GLA_HEADS, GLA_DV, GLA_DK), lambda c: (c, 0, 0, 0))] + [ANY] * ns,
        out_shape=[jax.ShapeDtypeStruct((rows, 512), F32), jax.ShapeDtypeStruct((rows, 512), ACT_DTYPE),
                   jax.ShapeDtypeStruct((nc, GLA_HEADS, GLA_DV, GLA_DK), F32)] + _gathered_shapes(shards),
        scratch_shapes=[pltpu.VMEM((GLA_HEADS, GLA_DV, GLA_DK), F32)] + _gather_sems(ns),
        compiler_params=_cp(("arbitrary",)),
    )(proj, proj, proj, proj, proj, wg_p, bg, gnw, *shards)
    return outs[0], outs[1], outs[2], _with_own_block(outs[3:], shards)


def _swa_mask(n):
    shape = (SWA_GROUP * SWA_BLOCK, 3 * SWA_BLOCK)
    qi = lax.broadcasted_iota(jnp.int32, shape, 0) & (SWA_BLOCK - 1)
    jj = lax.broadcasted_iota(jnp.int32, shape, 1)
    meta = (jj < SWA_BLOCK) & (jj >= META0) & ((n > 0) | (jj <= qi))
    prev = (jj >= SWA_BLOCK) & (jj < 2 * SWA_BLOCK) & (n >= 2) & (jj - SWA_BLOCK > qi)
    cur = (jj >= 2 * SWA_BLOCK) & (n >= 1) & (jj - 2 * SWA_BLOCK <= qi)
    return meta | prev | cur


def _stack_heads(t, kvh):
    return jnp.concatenate([t[:, (kvh * SWA_GROUP + g) * SWA_HD:(kvh * SWA_GROUP + g + 1) * SWA_HD]
                            for g in range(SWA_GROUP)], axis=0)


def _stack_sinks(sink_ref, kvh):
    return jnp.concatenate([jnp.full((SWA_BLOCK, 1), sink_ref[0, kvh * SWA_GROUP + g], F32)
                            for g in range(SWA_GROUP)], axis=0)


def _swa_group(nblk):
    return 5 if nblk % 5 == 0 else 1


def _swa_specs(group):
    blk = lambda w: pl.BlockSpec((group * SWA_BLOCK, w), lambda n: (n, 0))
    first = pl.BlockSpec((SWA_BLOCK, 128), lambda n: (0, 0))
    prev = pl.BlockSpec((SWA_BLOCK, 128), lambda n: (jnp.maximum(n * group - 1, 0), 0))
    return blk, first, prev


def _swa_keys(first_ref, prev_ref, cur_ref, g):
    own = cur_ref[g * SWA_BLOCK:(g + 1) * SWA_BLOCK, :]
    before = prev_ref[...] if g == 0 else cur_ref[(g - 1) * SWA_BLOCK:g * SWA_BLOCK, :]
    return jnp.concatenate([first_ref[...], before, own], axis=0)


def _swa_fwd(qr, kr, vr, sinks, shards):
    rows = qr.shape[0]
    nblk = rows // SWA_BLOCK
    group = _swa_group(nblk)
    steps = nblk // group
    ns = len(shards)

    def body(q_ref, k0, kp, kc, v0, vp, vc, sink_ref, *rest):
        o_ref = rest[ns]
        _place_gather(pl.program_id(0), steps, rest[:ns], rest[ns + 1:2 * ns + 1], rest[2 * ns + 1:])
        for g in range(group):
            n = pl.program_id(0) * group + g
            rs = slice(g * SWA_BLOCK, (g + 1) * SWA_BLOCK)
            kall, vall = _swa_keys(k0, kp, kc, g), _swa_keys(v0, vp, vc, g)
            mask = _swa_mask(n)[0:SWA_BLOCK]
            heads = range(SWA_HEADS)
            hs = [slice(h * SWA_HD, (h + 1) * SWA_HD) for h in heads]
            kv = [slice((h // SWA_GROUP) * SWA_HD, (h // SWA_GROUP + 1) * SWA_HD) for h in heads]
            s = [jnp.where(mask, _mm_nt(q_ref[rs, hs[h]], kall[:, kv[h]]), NEG) for h in heads]
            m = [jnp.maximum(jnp.max(s[h], axis=-1, keepdims=True), sink_ref[0, h]) for h in heads]
            p = [jnp.exp(s[h] - m[h]) for h in heads]
            den = [jnp.sum(p[h], axis=-1, keepdims=True) + jnp.exp(sink_ref[0, h] - m[h]) for h in heads]
            o = [_mm(p[h], vall[:, kv[h]]) for h in heads]
            for h in heads:
                o_ref[rs, hs[h]] = (o[h] / den[h]).astype(ACT_DTYPE)

    blk, first, prev = _swa_specs(group)
    outs = pl.pallas_call(
        body, name="swa_fwd", grid=(steps,),
        in_specs=[blk(512), first, prev, blk(128), first, prev, blk(128),
                  pl.BlockSpec(memory_space=pltpu.SMEM)] + [ANY] * ns,
        out_specs=[blk(512)] + [ANY] * ns,
        out_shape=[jax.ShapeDtypeStruct((rows, 512), ACT_DTYPE)] + _gathered_shapes(shards),
        scratch_shapes=_gather_sems(ns),
        compiler_params=_cp(("arbitrary",)),
    )(qr, kr, kr, kr, vr, vr, vr, sinks, *shards)
    return outs[0], _with_own_block(outs[1:], shards)


def _out_proj(x, lead, og, osw, wout, nfw, tm):
    rows = LEAD + x.shape[0]
    nb = tm // LEAD

    def body(*refs):
        x_refs, (lead_ref, og_ref, os_ref, w_ref, nw_ref, h1_ref, f_ref, ft_ref) = refs[:nb], refs[nb:]
        h0 = _h_tile(pl.program_id(0), lead_ref, x_refs)
        h1 = h0 + _mm(og_ref[...], w_ref[0:512, :]) + _mm(os_ref[...], w_ref[512:1024, :])
        h1_ref[...] = h1
        rstd = lax.rsqrt(jnp.mean(h1 * h1, axis=-1, keepdims=True) + EPS)
        f = h1 * rstd * nw_ref[...]
        f_ref[...] = f.astype(ACT_DTYPE)
        ft_ref[...] = f.T.astype(ACT_DTYPE)

    row = lambda w: pl.BlockSpec((tm, w), lambda i: (i, 0))
    return pl.pallas_call(
        body, name="out_proj", grid=(rows // tm,),
        in_specs=_token_specs(tm) + [pl.BlockSpec((LEAD, D), lambda i: (0, 0)), row(512), row(512),
                                     pl.BlockSpec((D, D), lambda i: (0, 0)), pl.BlockSpec((1, D), lambda i: (0, 0))],
        out_specs=[row(D), row(D), pl.BlockSpec((D, tm), lambda i: (0, i))],
        out_shape=[jax.ShapeDtypeStruct((rows, D), F32), jax.ShapeDtypeStruct((rows, D), ACT_DTYPE),
                   jax.ShapeDtypeStruct((D, rows), ACT_DTYPE)],
        compiler_params=_cp(("arbitrary",), 48),
    )(*([x] * nb), lead, og, osw, wout, nfw)


def _ffn_fwd(f, h1, w1, w2, tgt, fnw, tm):
    rows = f.shape[0]
    nj = D_FF // FF_WIDE
    nb = tm // LEAD

    def body(f_ref, h1_ref, w1_ref, w2_ref, nw_ref, *rest):
        t_refs, (a_ref, dh2_ref, dh2t_ref, loss_ref, gfn_ref, acc) = rest[:nb], rest[nb:]
        i, j = pl.program_id(0), pl.program_id(1)

        @pl.when((i == 0) & (j == 0))
        def _():
            loss_ref[...] = jnp.zeros_like(loss_ref)
            gfn_ref[...] = jnp.zeros_like(gfn_ref)

        @pl.when(j == 0)
        def _():
            acc[...] = jnp.zeros_like(acc)

        a = _mm(f_ref[...], w1_ref[...])
        a_ref[...] = a.astype(ACT_DTYPE)
        z = jnp.square(jnp.maximum(a, 0.0))
        acc[...] += _mm(z, w2_ref[...])

        @pl.when(j == nj - 1)
        def _():
            h2 = h1_ref[...] + acc[...]
            rstd = lax.rsqrt(jnp.mean(h2 * h2, axis=-1, keepdims=True) + EPS)
            hn = h2 * rstd
            nw = nw_ref[...]
            row = i * tm + lax.broadcasted_iota(jnp.int32, (tm, 1), 0)
            target = jnp.concatenate([t[...] for t in t_refs], axis=0)
            err = jnp.where(row >= LEAD, hn * nw - target, 0.0)
            row_loss = jnp.sum(err * err, axis=-1, keepdims=True) * (1.0 / D)
            loss_ref[...] += jnp.broadcast_to(0.5 * jnp.sum(row_loss, axis=0, keepdims=True), loss_ref.shape)
            dy = err * (1.0 / D)
            gfn_ref[...] += jnp.broadcast_to(jnp.sum(dy * hn, axis=0, keepdims=True), gfn_ref.shape)
            dhn = dy * nw
            dh2 = rstd * (dhn - hn * jnp.mean(dhn * hn, axis=-1, keepdims=True))
            dh2_ref[...] = dh2
            dh2t_ref[...] = dh2.T.astype(ACT_DTYPE)

    return pl.pallas_call(
        body, name="ffn_fwd", grid=(rows // tm, nj),
        in_specs=[pl.BlockSpec((tm, D), lambda i, j: (i, 0)), pl.BlockSpec((tm, D), lambda i, j: (i, 0)),
                  pl.BlockSpec((D, FF_WIDE), lambda i, j: (0, j)),
                  pl.BlockSpec((FF_WIDE, D), lambda i, j: (j, 0)),
                  pl.BlockSpec((1, D), lambda i, j: (0, 0))] + _token_specs(tm, grid_rank=2),
        out_specs=[pl.BlockSpec((tm, FF_WIDE), lambda i, j: (i, j)), pl.BlockSpec((tm, D), lambda i, j: (i, 0)),
                   pl.BlockSpec((D, tm), lambda i, j: (0, i)),
                   pl.BlockSpec((8, 128), lambda i, j: (0, 0)), pl.BlockSpec((8, D), lambda i, j: (0, 0))],
        out_shape=[jax.ShapeDtypeStruct((rows, D_FF), ACT_DTYPE), jax.ShapeDtypeStruct((rows, D), F32),
                   jax.ShapeDtypeStruct((D, rows), ACT_DTYPE),
                   jax.ShapeDtypeStruct((8, 128), F32), jax.ShapeDtypeStruct((8, D), F32)],
        scratch_shapes=[pltpu.VMEM((tm, D), F32)],
        compiler_params=_cp(("arbitrary", "arbitrary"), 56),
    )(f, h1, w1, w2, fnw, *([tgt] * nb))


def _ffn_bwd_act(dh2, a, w1, w2, h1, nfw, tm):
    rows = dh2.shape[0]
    nj = D_FF // FF_WIDE

    def body(dh2_ref, a_ref, w1_ref, w2_ref, h1_ref, nw_ref, da_ref, dh1_ref, gnf_ref, acc):
        i, j = pl.program_id(0), pl.program_id(1)

        @pl.when((i == 0) & (j == 0))
        def _():
            gnf_ref[...] = jnp.zeros_like(gnf_ref)

        @pl.when(j == 0)
        def _():
            acc[...] = jnp.zeros_like(acc)

        dz = _mm_nt(dh2_ref[...], w2_ref[...])
        da = dz * (2.0 * jnp.maximum(a_ref[...].astype(F32), 0.0))
        da_ref[...] = da.astype(ACT_DTYPE)
        acc[...] += _mm_nt(da, w1_ref[...])

        @pl.when(j == nj - 1)
        def _():
            h1 = h1_ref[...]
            rstd = lax.rsqrt(jnp.mean(h1 * h1, axis=-1, keepdims=True) + EPS)
            hn = h1 * rstd
            df = acc[...]
            gnf_ref[...] += jnp.broadcast_to(jnp.sum(df * hn, axis=0, keepdims=True), gnf_ref.shape)
            dfn = df * nw_ref[...]
            dh1_ref[...] = dh2_ref[...] + rstd * (dfn - hn * jnp.mean(dfn * hn, axis=-1, keepdims=True))

    return pl.pallas_call(
        body, name="ffn_bwd_act", grid=(rows // tm, nj),
        in_specs=[pl.BlockSpec((tm, D), lambda i, j: (i, 0)), pl.BlockSpec((tm, FF_WIDE), lambda i, j: (i, j)),
                  pl.BlockSpec((D, FF_WIDE), lambda i, j: (0, j)),
                  pl.BlockSpec((FF_WIDE, D), lambda i, j: (j, 0)),
                  pl.BlockSpec((tm, D), lambda i, j: (i, 0)), pl.BlockSpec((1, D), lambda i, j: (0, 0))],
        out_specs=[pl.BlockSpec((tm, FF_WIDE), lambda i, j: (i, j)), pl.BlockSpec((tm, D), lambda i, j: (i, 0)),
                   pl.BlockSpec((8, D), lambda i, j: (0, 0))],
        out_shape=[jax.ShapeDtypeStruct((rows, D_FF), ACT_DTYPE), jax.ShapeDtypeStruct((rows, D), F32),
                   jax.ShapeDtypeStruct((8, D), F32)],
        scratch_shapes=[pltpu.VMEM((tm, D), F32)],
        compiler_params=_cp(("arbitrary", "arbitrary"), 56),
    )(dh2, a, w1, w2, h1, nfw)


def _ffn_bwd_weights(ft, a, da, dh2t, tm):
    rows = a.shape[0]
    steps = rows // tm

    def body(ft_ref, a_ref, da_ref, dh2t_ref, dw1_ref, dw2_ref, dw2t):
        i = pl.program_id(1)

        @pl.when(i == 0)
        def _():
            dw1_ref[...] = jnp.zeros_like(dw1_ref)
            dw2t[...] = jnp.zeros_like(dw2t)

        z = jnp.square(jnp.maximum(a_ref[...].astype(F32), 0.0))
        dw1_ref[...] += _mm(ft_ref[...], da_ref[...])
        dw2t[...] += _mm(dh2t_ref[...], z)

        @pl.when(i == steps - 1)
        def _():
            dw2_ref[...] = dw2t[...].T

    return pl.pallas_call(
        body, name="ffn_bwd_weights", grid=(N_DEV, steps),
        in_specs=[pl.BlockSpec((D, tm), lambda j, i: (0, i)), pl.BlockSpec((tm, FF_TILE), lambda j, i: (i, j)),
                  pl.BlockSpec((tm, FF_TILE), lambda j, i: (i, j)), pl.BlockSpec((D, tm), lambda j, i: (0, i))],
        out_specs=[pl.BlockSpec((None, None, D, FF_TILE), lambda j, i: (j % 2, j // 2, 0, 0)),
                   pl.BlockSpec((None, None, FF_TILE, D), lambda j, i: (j % 2, j // 2, 0, 0))],
        out_shape=[jax.ShapeDtypeStruct((2, 4, D, FF_TILE), F32), jax.ShapeDtypeStruct((2, 4, FF_TILE, D), F32)],
        scratch_shapes=[pltpu.VMEM((D, FF_TILE), F32)],
        compiler_params=_cp(("arbitrary", "arbitrary"), 48),
    )(ft, a, da, dh2t)


def _out_proj_bwd(dh1, og, osw, wout, tm, partials):
    rows = dh1.shape[0]
    steps = rows // tm
    ns = len(partials)

    def body(dh1_ref, og_ref, os_ref, w_ref, *rest):
        part_refs, rest = rest[:ns], rest[ns:]
        dog_ref, dos_ref, dw_ref = rest[:3]
        land_refs, (send_sems, recv_sems) = rest[3:3 + ns], rest[3 + ns:]
        i = pl.program_id(0)
        start, finish = _sibling_schedule(part_refs, land_refs, send_sems, recv_sems)

        @pl.when(i == 0)
        def _():
            dw_ref[...] = jnp.zeros_like(dw_ref)
            start()

        pl.when(i == steps - 1)(finish)

        dh1 = dh1_ref[...].astype(MXU_DTYPE)
        dog_ref[...] = _mm_nt(dh1, w_ref[0:512, :])
        dos_ref[...] = _mm_nt(dh1, w_ref[512:1024, :])
        for half, ref in enumerate((og_ref, os_ref)):
            dw = _mm_tn(ref[...], dh1)
            for blk in range(4):
                shard = half * 4 + blk
                dw_ref[shard % 2, shard // 2] += dw[blk * 128:(blk + 1) * 128, :]

    row = lambda w: pl.BlockSpec((tm, w), lambda i: (i, 0))
    outs = pl.pallas_call(
        body, name="out_proj_bwd", grid=(steps,),
        in_specs=[row(D), row(512), row(512), pl.BlockSpec((D, D), lambda i: (0, 0))] + [ANY] * ns,
        out_specs=[row(512), row(512), pl.BlockSpec((2, 4, 128, D), lambda i: (0, 0, 0, 0))] + [ANY] * ns,
        out_shape=[jax.ShapeDtypeStruct((rows, 512), F32), jax.ShapeDtypeStruct((rows, 512), F32),
                   jax.ShapeDtypeStruct((2, 4, 128, D), F32)] + _sibling_shapes(partials),
        scratch_shapes=_sibling_sems(ns),
        compiler_params=_cp(("arbitrary",), 48),
    )(dh1, og, osw, wout, *partials)
    return outs[0], outs[1], outs[2], outs[3:]


def _swa_bwd(qr, kr, vr, osw, dos, sinks, jobs):
    rows = qr.shape[0]
    nblk = rows // SWA_BLOCK
    group = _swa_group(nblk)
    steps = nblk // group
    ns = jobs.n

    def body(q_ref, k0, kp, kc, v0, vp, vc, o_ref, do_ref, sink_ref, *rest):
        dq_ref, dk_ref, dv_ref, dsink_ref = rest[ns:ns + 4]
        start, finish = jobs.bind(rest[:ns], rest[ns + 4:2 * ns + 4], rest[2 * ns + 4:])
        step = pl.program_id(0)

        @pl.when(step == 0)
        def _():
            dk_ref[...] = jnp.zeros_like(dk_ref)
            dv_ref[...] = jnp.zeros_like(dv_ref)
            dsink_ref[...] = jnp.zeros_like(dsink_ref)
            start()

        pl.when(step == steps - 1)(finish)
        for g in range(group):
            block(step * group + g, g, q_ref, k0, kp, kc, v0, vp, vc, o_ref, do_ref, sink_ref,
                  dq_ref, dk_ref, dv_ref, dsink_ref)

    def block(n, g, q_ref, k0, kp, kc, v0, vp, vc, o_ref, do_ref, sink_ref, dq_ref, dk_ref, dv_ref, dsink_ref):
        rs = slice(g * SWA_BLOCK, (g + 1) * SWA_BLOCK)
        kall, vall = _swa_keys(k0, kp, kc, g), _swa_keys(v0, vp, vc, g)
        mask = _swa_mask(n)[0:SWA_BLOCK]
        heads = range(SWA_HEADS)
        hs = [slice(h * SWA_HD, (h + 1) * SWA_HD) for h in heads]
        kv = [slice((h // SWA_GROUP) * SWA_HD, (h // SWA_GROUP + 1) * SWA_HD) for h in heads]
        sink = [sink_ref[0, h] for h in heads]
        qh = [q_ref[rs, hs[h]] for h in heads]
        doh = [do_ref[rs, hs[h]] for h in heads]
        s = [jnp.where(mask, _mm_nt(qh[h], kall[:, kv[h]]), NEG) for h in heads]
        dp = [_mm_nt(doh[h], vall[:, kv[h]]) for h in heads]
        delta = [jnp.sum(doh[h] * o_ref[rs, hs[h]].astype(F32), axis=-1, keepdims=True) for h in heads]
        m = [jnp.maximum(jnp.max(s[h], axis=-1, keepdims=True), sink[h]) for h in heads]
        e = [jnp.exp(s[h] - m[h]) for h in heads]
        inv = [1.0 / (jnp.sum(e[h], axis=-1, keepdims=True) + jnp.exp(sink[h] - m[h])) for h in heads]
        p = [e[h] * inv[h] for h in heads]
        ds = [p[h] * (dp[h] - delta[h]) for h in heads]
        dq = [_mm(ds[h], kall[:, kv[h]]) for h in heads]
        dkh = [_mm_tn(ds[h], qh[h]) for h in heads]
        dvh = [_mm_tn(p[h], doh[h]) for h in heads]
        for h in heads:
            dsink = -jnp.sum(jnp.exp(sink[h] - m[h]) * inv[h] * delta[h], axis=0, keepdims=True)
            dsink_ref[h:h + 1, :] += jnp.broadcast_to(dsink, (1, 128))
        dq_ref[rs, :] = jnp.concatenate(dq, axis=1)
        group_sum = lambda parts, kvh: sum(parts[kvh * SWA_GROUP + 1:(kvh + 1) * SWA_GROUP], parts[kvh * SWA_GROUP])
        dk_all = jnp.concatenate([group_sum(dkh, kvh) for kvh in range(SWA_KV)], axis=1)
        dv_all = jnp.concatenate([group_sum(dvh, kvh) for kvh in range(SWA_KV)], axis=1)
        prev0 = pl.multiple_of(jnp.maximum(n - 1, 0) * SWA_BLOCK, SWA_BLOCK)
        cur0 = pl.multiple_of(n * SWA_BLOCK, SWA_BLOCK)
        for ref, val in ((dk_ref, dk_all), (dv_ref, dv_all)):
            ref[0:SWA_BLOCK, :] += val[0:SWA_BLOCK]
            ref[pl.ds(prev0, SWA_BLOCK), :] += val[SWA_BLOCK:2 * SWA_BLOCK]
            ref[pl.ds(cur0, SWA_BLOCK), :] += val[2 * SWA_BLOCK:]

    blk, first, prev = _swa_specs(group)
    whole = pl.BlockSpec((rows, 128), lambda n: (0, 0))
    outs = pl.pallas_call(
        body, name="swa_bwd", grid=(steps,),
        in_specs=[blk(512), first, prev, blk(128), first, prev, blk(128), blk(512), blk(512),
                  pl.BlockSpec(memory_space=pltpu.SMEM)] + [ANY] * ns,
        out_specs=[blk(512), whole, whole, pl.BlockSpec((8, 128), lambda n: (0, 0))] + [ANY] * ns,
        out_shape=[jax.ShapeDtypeStruct((rows, 512), F32), jax.ShapeDtypeStruct((rows, 128), F32),
                   jax.ShapeDtypeStruct((rows, 128), F32), jax.ShapeDtypeStruct((8, 128), F32)] + jobs.out_shapes,
        scratch_shapes=jobs.sems,
        compiler_params=_cp(("arbitrary",), 48),
    )(qr, kr, kr, kr, vr, vr, vr, osw, dos, sinks, *jobs.inputs)
    return outs[0], outs[1], outs[2], outs[3], jobs.split(outs[4:])


def _gla_bwd(proj, oraw, states, dog, wg_p, bg, gnw, jobs):
    rows = proj.shape[0]
    nc = rows // GLA_CHUNK
    group = _gla_group(nc)
    steps, nrows = nc // group, group * GLA_CHUNK
    ns = jobs.n

    def body(q_ref, k_ref, v_ref, r_ref, lr_ref, oraw_ref, st_ref, dog_ref, wg_ref, bg_ref, gnw_ref, *rest):
        dq_ref, dk_ref, dv_ref, dr_ref, dlr_ref, dwg_ref, dbg_ref, dgnw_ref = rest[ns:ns + 8]
        dstate, db_scr = rest[2 * ns + 8:2 * ns + 10]
        start, finish = jobs.bind(rest[:ns], rest[ns + 8:2 * ns + 8], rest[2 * ns + 10:])
        t = pl.program_id(0)
        c = steps - 1 - t

        @pl.when(t == 0)
        def _():
            dstate[...] = jnp.zeros_like(dstate)
            dwg_ref[...] = jnp.zeros_like(dwg_ref)
            dbg_ref[...] = jnp.zeros_like(dbg_ref)
            dgnw_ref[...] = jnp.zeros_like(dgnw_ref)
            start()

        pl.when(t == steps - 1)(finish)

        lr, wg = lr_ref[...], wg_ref[...]
        zg, live, _, upper, b = _gla_gates(lr, wg, bg_ref[...], c * nrows, nrows)
        eb, enb = jnp.exp(b), jnp.exp(-b)
        scale = GLA_DK ** -0.5
        gq = q_ref[...] * scale * eb
        gk = k_ref[...] * enb
        v = v_ref[...]
        gnw_v = gnw_ref[...]
        tril = _tril64()
        is_last = lax.broadcasted_iota(jnp.int32, (GLA_CHUNK, 1), 0) == GLA_CHUNK - 1
        dgnw = jnp.zeros((1, GLA_DV), F32)
        pairs = [(h, gi) for h in range(GLA_HEADS) for gi in range(group)]
        rs = {gi: slice(gi * GLA_CHUNK, (gi + 1) * GLA_CHUNK) for gi in range(group)}
        s64 = {h: slice(h * GLA_DK, (h + 1) * GLA_DK) for h in range(GLA_HEADS)}
        s128 = {h: slice(h * GLA_DV, (h + 1) * GLA_DV) for h in range(GLA_HEADS)}
        qh = {(h, gi): gq[rs[gi], s64[h]] for h, gi in pairs}
        kh = {(h, gi): gk[rs[gi], s64[h]] for h, gi in pairs}
        vh = {(h, gi): v[rs[gi], s128[h]] for h, gi in pairs}
        ebl = {(h, gi): eb[(gi + 1) * GLA_CHUNK - 1:(gi + 1) * GLA_CHUNK, s64[h]] for h, gi in pairs}
        kl = {pr: kh[pr] * ebl[pr] for pr in pairs}
        st = {(h, gi): st_ref[gi, h] for h, gi in pairs}
        do = {}
        for h, gi in pairs:
            o, rh, dout = oraw_ref[rs[gi], s128[h]], r_ref[rs[gi], s128[h]], dog_ref[rs[gi], s128[h]]
            rstd = lax.rsqrt(jnp.mean(o * o, axis=-1, keepdims=True) + EPS)
            on = o * rstd
            sg = _sigmoid(rh)
            dr_ref[rs[gi], s128[h]] = (dout * (on * gnw_v) * (sg * (1.0 + rh * (1.0 - sg)))).astype(ACT_DTYPE)
            dy = dout * (rh * sg)
            dgnw = dgnw + jnp.sum(dy * on, axis=0, keepdims=True)
            don = dy * gnw_v
            do[h, gi] = rstd * (don - on * jnp.mean(don * on, axis=-1, keepdims=True))
        a = {pr: jnp.where(tril, _mm_nt(qh[pr], kh[pr]), 0.0) for pr in pairs}
        da = {pr: jnp.where(tril, _mm_nt(do[pr], vh[pr]), 0.0) for pr in pairs}
        dinc = {pr: _mm_tn(do[pr], qh[pr]) for pr in pairs}
        dgq = {pr: _mm(da[pr], kh[pr]) + _mm(do[pr], st[pr]) for pr in pairs}
        dgk = {pr: _mm_tn(da[pr], qh[pr]) for pr in pairs}
        dv_a = {pr: _mm_tn(a[pr], do[pr]) for pr in pairs}
        dsp = {}
        for h in range(GLA_HEADS):
            cur = dstate[h]
            for gi in reversed(range(group)):
                dsp[h, gi] = cur
                cur = cur * ebl[h, gi] + dinc[h, gi]
            dstate[h] = cur
        for h, gi in pairs:
            pr = (h, gi)
            dkl = _mm(vh[pr], dsp[pr])
            dv_ref[rs[gi], s128[h]] = (dv_a[pr] + _mm_nt(kl[pr], dsp[pr])).astype(ACT_DTYPE)
            debl = jnp.sum(dsp[pr] * st[pr], axis=0, keepdims=True)
            dq_ref[rs[gi], s64[h]] = (dgq[pr] * (scale * eb[rs[gi], s64[h]])).astype(ACT_DTYPE)
            dk_ref[rs[gi], s64[h]] = ((dgk[pr] + dkl * ebl[pr]) * enb[rs[gi], s64[h]]).astype(ACT_DTYPE)
            last = debl * ebl[pr] + jnp.sum(dkl * kl[pr], axis=0, keepdims=True)
            db_scr[rs[gi], s64[h]] = (dgq[pr] * qh[pr] - dgk[pr] * kh[pr] - dkl * kl[pr]
                                      + jnp.where(is_last, last, 0.0))
        dg = jnp.dot(upper.astype(F32), db_scr[...], precision=HIGHEST, preferred_element_type=F32)
        dzg = jnp.where(live, dg * _sigmoid(-zg) * (1.0 / GLA_TAU), 0.0)
        dlr_ref[...] = _mm_nt(dzg, wg).astype(ACT_DTYPE)
        dwg_ref[...] += _mm_tn(lr, dzg)
        dbg_ref[...] += jnp.broadcast_to(jnp.sum(dzg, axis=0, keepdims=True), dbg_ref.shape)
        dgnw_ref[...] += jnp.broadcast_to(dgnw, dgnw_ref.shape)

    nb = lambda w, col: pl.BlockSpec((nrows, w), lambda t: (steps - 1 - t, col // w))
    const = lambda shape: pl.BlockSpec(shape, lambda t: (0,) * len(shape))
    outs = pl.pallas_call(
        body, name="gla_bwd", grid=(steps,),
        in_specs=[nb(256, C_GQ), nb(256, C_GK), nb(512, C_GV), nb(512, C_GR), nb(128, C_LR), nb(512, 0),
                  pl.BlockSpec((group, GLA_HEADS, GLA_DV, GLA_DK), lambda t: (steps - 1 - t, 0, 0, 0)), nb(512, 0),
                  const((128, 256)), const((1, 256)), const((1, 128))] + [ANY] * ns,
        out_specs=[nb(256, 0), nb(256, 0), nb(512, 0), nb(512, 0), nb(128, 0),
                   const((128, 256)), const((8, 256)), const((8, 128))] + [ANY] * ns,
        out_shape=[jax.ShapeDtypeStruct((rows, 256), ACT_DTYPE), jax.ShapeDtypeStruct((rows, 256), ACT_DTYPE),
                   jax.ShapeDtypeStruct((rows, 512), ACT_DTYPE), jax.ShapeDtypeStruct((rows, 512), ACT_DTYPE),
                   jax.ShapeDtypeStruct((rows, 128), ACT_DTYPE), jax.ShapeDtypeStruct((128, 256), F32),
                   jax.ShapeDtypeStruct((8, 256), F32), jax.ShapeDtypeStruct((8, 128), F32)] + jobs.out_shapes,
        scratch_shapes=[pltpu.VMEM((GLA_HEADS, GLA_DV, GLA_DK), F32), pltpu.VMEM((nrows, 256), F32)] + jobs.sems,
        compiler_params=_cp(("arbitrary",)),
    )(proj, proj, proj, proj, proj, oraw, states, dog, wg_p, bg, gnw, *jobs.inputs)
    return outs[:8], jobs.split(outs[8:])


def _in_proj_bwd(x, lead, dh1, nw, win_p, dgv, dgr, dsq, dgq, dgk, dsk, dsv, dlr, tabs, tm):
    seq = x.shape[0]
    rows = LEAD + seq
    nb = tm // LEAD
    steps = rows // tm

    def first_copy(scr, gx_ref, sem):
        return pltpu.make_async_copy(scr.at[pl.ds(LEAD, tm - LEAD)], gx_ref.at[pl.ds(0, tm - LEAD)], sem)

    def tile_copy(scr, gx_ref, sem, step):
        start = pl.multiple_of(jnp.maximum(step * tm - LEAD, 0), LEAD)
        return pltpu.make_async_copy(scr, gx_ref.at[pl.ds(start, tm)], sem)

    def body(*refs):
        x_refs, refs = refs[:nb], refs[nb:]
        (lead_ref, dh1_ref, nw_ref, w_ref, dgv_ref, dgr_ref, dsq_ref, dgq_ref, dgk_ref, dsk_ref, dsv_ref, dlr_ref,
         c_ref, sa_ref, sb_ref, gx_ref, dlead_ref, dproj_ref, ut_ref, gnm_ref, scr, sem) = refs
        i = pl.program_id(0)

        @pl.when(i == 0)
        def _():
            gnm_ref[...] = jnp.zeros_like(gnm_ref)

        cos, sa, sb = c_ref[...], sa_ref[...], sb_ref[...]
        dsq_v = (_unrope(dsq_ref[...], cos, sa, sb) * (SWA_HD ** -0.5)).astype(MXU_DTYPE)
        dsk_v = _unrope(dsk_ref[...], cos, sa, sb).astype(MXU_DTYPE)
        dproj = jnp.concatenate(
            [dgv_ref[...].astype(MXU_DTYPE), dgr_ref[...].astype(MXU_DTYPE), dgq_ref[...].astype(MXU_DTYPE),
             dgk_ref[...].astype(MXU_DTYPE), dlr_ref[...].astype(MXU_DTYPE), dsq_v, dsk_v,
             dsv_ref[...].astype(MXU_DTYPE)],
            axis=1)
        dproj_ref[...] = dproj
        h = _h_tile(i, lead_ref, x_refs)
        rstd = lax.rsqrt(jnp.mean(h * h, axis=-1, keepdims=True) + EPS)
        hn = h * rstd
        nw_v = nw_ref[...]
        ut_ref[...] = (hn * nw_v).T.astype(ACT_DTYPE)
        du = _mm_nt(dproj, w_ref[...])
        gnm_ref[...] += jnp.broadcast_to(jnp.sum(du * hn, axis=0, keepdims=True), gnm_ref.shape)
        dun = du * nw_v
        dh0 = dh1_ref[...] + rstd * (dun - hn * jnp.mean(dun * hn, axis=-1, keepdims=True))

        if tm > LEAD:
            pl.when(i == 1)(lambda: first_copy(scr, gx_ref, sem).wait())
        pl.when(i > 1)(lambda: tile_copy(scr, gx_ref, sem, i).wait())
        scr[...] = dh0

        @pl.when(i == 0)
        def _():
            dlead_ref[...] = dh0[0:LEAD]
            if tm > LEAD:
                first_copy(scr, gx_ref, sem).start()
                if steps == 1:
                    first_copy(scr, gx_ref, sem).wait()

        @pl.when(i > 0)
        def _():
            tile_copy(scr, gx_ref, sem, i).start()

        if steps > 1:
            pl.when(i == steps - 1)(lambda: tile_copy(scr, gx_ref, sem, i).wait())

    row = lambda w: pl.BlockSpec((tm, w), lambda i: (i, 0))
    const = lambda shape: pl.BlockSpec(shape, lambda i: (0,) * len(shape))
    return pl.pallas_call(
        body, name="in_proj_bwd", grid=(steps,),
        in_specs=_token_specs(tm) + [const((LEAD, D)), row(D), const((1, D)), const((D, DINP)),
                                     row(512), row(512), row(512), row(256), row(256), row(128), row(128), row(128),
                                     row(128), row(128), row(128)],
        out_specs=[ANY, const((LEAD, D)), row(DINP), pl.BlockSpec((D, tm), lambda i: (0, i)), const((8, D))],
        out_shape=[jax.ShapeDtypeStruct((seq, D), F32), jax.ShapeDtypeStruct((LEAD, D), F32),
                   jax.ShapeDtypeStruct((rows, DINP), ACT_DTYPE), jax.ShapeDtypeStruct((D, rows), ACT_DTYPE),
                   jax.ShapeDtypeStruct((8, D), F32)],
        scratch_shapes=[pltpu.VMEM((tm, D), F32), pltpu.SemaphoreType.DMA],
        compiler_params=_cp(("arbitrary",), 56),
    )(*([x] * nb), lead, dh1, nw, win_p, dgv, dgr, dsq, dgq, dgk, dsk, dsv, dlr, *tabs)


def _win_runs():
    groups = [(O_GQ, C_GQ), (O_GK, C_GK), (O_GV, C_GV), (O_GR, C_GR), (O_LR, C_LR), (O_SQ, C_SQ), (O_SK, C_SK),
              (O_SV, C_SV)]
    per = DIN // N_DEV
    runs = []
    for (o0, o1), c0 in groups:
        o = o0
        while o < o1:
            d = o // per
            end = min(o1, (d + 1) * per)
            runs.append((d, o - d * per, c0 + o - o0, end - o))
            o = end
    return runs


def _win_padded(g_in):
    tr = 128

    def body(g_ref, o_ref):
        o_ref[...] = jnp.zeros_like(o_ref)
        for d, s, c, w in _win_runs():
            o_ref[:, c:c + w] = g_ref[d, :, s:s + w]

    return pl.pallas_call(
        body, name="w_in_layout", grid=(D // tr,),
        in_specs=[pl.BlockSpec((N_DEV, tr, DIN // N_DEV), lambda i: (0, i, 0))],
        out_specs=pl.BlockSpec((tr, DINP), lambda i: (i, 0)),
        out_shape=jax.ShapeDtypeStruct((D, DINP), g_in.dtype),
        compiler_params=_cp(("arbitrary",)),
    )(g_in)


def _in_proj_bwd_weights(ut, dproj, tm, small):
    rows = dproj.shape[0]
    steps = rows // tm
    per = DIN // N_DEV

    def body(ut_ref, dp_ref, *rest):
        small_refs, (out_ref, total_ref, acc, stage, sems), sum_scratch = rest[:9], rest[9:14], rest[14:]
        i = pl.program_id(0)
        start, finish = _small_sum_schedule(small_refs, total_ref, *sum_scratch)

        @pl.when(i == 0)
        def _():
            acc[...] = jnp.zeros_like(acc)
            start()

        acc[...] += _mm(ut_ref[...], dp_ref[...])
        pl.when(i == steps - 1)(finish)

        @pl.when(i == steps - 1)
        def _():
            copies = []
            for d in range(N_DEV):
                slot = d % 2
                if d >= 2:
                    copies[d - 2].wait()
                for owner, s, c, w in _win_runs():
                    if owner == d:
                        stage[slot, :, s:s + w] = acc[:, c:c + w]
                cp = pltpu.make_async_copy(stage.at[slot], out_ref.at[d % 2, d // 2], sems.at[slot])
                cp.start()
                copies.append(cp)
            copies[N_DEV - 2].wait()
            copies[N_DEV - 1].wait()

    return pl.pallas_call(
        body, name="in_proj_bwd_weights", grid=(steps,),
        in_specs=[pl.BlockSpec((D, tm), lambda i: (0, i)), pl.BlockSpec((tm, DINP), lambda i: (i, 0))] + SMALL_SPECS,
        out_specs=[ANY, pl.BlockSpec((SMALL_ROWS, D), lambda i: (0, 0))],
        out_shape=[jax.ShapeDtypeStruct((2, 4, D, per), F32), jax.ShapeDtypeStruct((SMALL_ROWS, D), F32)],
        scratch_shapes=[pltpu.VMEM((D, DINP), F32), pltpu.VMEM((2, D, per), F32), pltpu.SemaphoreType.DMA((2,))]
        + _small_sum_scratch(),
        compiler_params=_cp(("arbitrary",), 56),
    )(ut, dproj, *small)


def _adamw(w, g, m, v):
    m = ADAM_B1 * m + (1.0 - ADAM_B1) * g
    v = ADAM_B2 * v + (1.0 - ADAM_B2) * jnp.square(g)
    m_hat = m / (1.0 - ADAM_B1 ** ADAM_STEP)
    v_hat = v / (1.0 - ADAM_B2 ** ADAM_STEP)
    delta = -ADAM_LR * (m_hat / (jnp.sqrt(v_hat) + ADAM_EPS) + ADAM_WD * w)
    return delta, m, v


ADAM_STEPS = 8


def _adamw_shards(where, items, name, jobs=None):
    jobs = jobs or _Jobs([])
    ns, nw = jobs.n, len(items)

    def body(where_ref, *rest):
        ins, rest = rest[:5 * nw], rest[5 * nw:]
        job_ins, rest = rest[:ns], rest[ns:]
        outs, rest = rest[:4 * nw], rest[4 * nw:]
        start, finish = jobs.bind(job_ins, rest[:ns], rest[ns:])
        i = pl.program_id(0)
        pl.when(i == 0)(start)
        pl.when(i == ADAM_STEPS - 1)(finish)
        for k in range(nw):
            p_ref, own_ref, w_ref, m_ref, v_ref = ins[5 * k:5 * k + 5]
            g_ref, d_ref, nm_ref, nv_ref = outs[4 * k:4 * k + 4]
            g = ((p_ref[0].astype(F32) + p_ref[1].astype(F32)) + p_ref[2].astype(F32)) + own_ref[...]
            g_ref[...] = g
            d_ref[...], nm_ref[...], nv_ref[...] = _adamw(w_ref[...], g, m_ref[...], v_ref[...])

    in_specs, out_specs, out_shape, operands = [], [], [], []
    for parts, own, w, m, v in items:
        r, cdim = w.shape
        tr = r // ADAM_STEPS
        spec = pl.BlockSpec((tr, cdim), lambda i, s: (i, 0))
        in_specs += [pl.BlockSpec((3, tr, cdim), lambda i, s: (0, i, 0)),
                     pl.BlockSpec((None, tr, cdim), lambda i, s: (s[1], i, 0)), spec, spec, spec]
        out_specs += [spec] * 4
        out_shape += [jax.ShapeDtypeStruct((r, cdim), F32)] * 4
        operands += [parts, own, w, m, v]
    outs = pl.pallas_call(
        body, name=name,
        grid_spec=pltpu.PrefetchScalarGridSpec(
            num_scalar_prefetch=1, grid=(ADAM_STEPS,),
            in_specs=in_specs + [ANY] * ns, out_specs=out_specs + [ANY] * ns, scratch_shapes=jobs.sems),
        out_shape=out_shape + jobs.out_shapes,
        compiler_params=_cp(("arbitrary",)),
    )(where, *operands, *jobs.inputs)
    return [outs[4 * k:4 * k + 4] for k in range(nw)], jobs.split(outs[4 * nw:])


def _adamw_small(items):
    n = len(items)

    def body(*refs):
        ins, outs = refs[:4 * n], refs[4 * n:]
        for k in range(n):
            w_ref, g_ref, m_ref, v_ref = ins[4 * k:4 * k + 4]
            d_ref, nm_ref, nv_ref = outs[3 * k:3 * k + 3]
            d_ref[...], nm_ref[...], nv_ref[...] = _adamw(w_ref[...], g_ref[...], m_ref[...], v_ref[...])

    vm = pl.BlockSpec(memory_space=pltpu.VMEM)
    shapes = [jax.ShapeDtypeStruct(w.shape, F32) for w, _, _, _ in items for _ in range(3)]
    outs = pl.pallas_call(body, name="adamw_small", in_specs=[vm] * (4 * n), out_specs=[vm] * (3 * n),
                          out_shape=shapes)(*[t for item in items for t in item])
    return [outs[3 * k:3 * k + 3] for k in range(n)]


def _add_own_half(where, full, theirs, name, wire_copy=False):
    _, _, r, cdim = full.shape
    tr = 128 if r % 128 == 0 else r

    def body(where_ref, a_ref, b_ref, *o_refs):
        total = a_ref[...] + b_ref[...]
        o_refs[0][...] = total
        if wire_copy:
            o_refs[1][...] = total.astype(WIRE_DTYPE)

    spec = pl.BlockSpec((4, tr, cdim), lambda i, s: (0, i, 0))
    shapes = [jax.ShapeDtypeStruct(theirs.shape, F32)] + ([jax.ShapeDtypeStruct(theirs.shape, WIRE_DTYPE)] if wire_copy else [])
    outs = pl.pallas_call(
        body, name=name,
        grid_spec=pltpu.PrefetchScalarGridSpec(
            num_scalar_prefetch=1, grid=(r // tr,),
            in_specs=[pl.BlockSpec((None, 4, tr, cdim), lambda i, s: (s[0], 0, i, 0)), spec],
            out_specs=[spec] * len(shapes)),
        out_shape=shapes, compiler_params=_cp(("arbitrary",)))(where, full, theirs)
    return outs if wire_copy else outs[0]


def kernel(x, meta_tokens, norm_mix_w, w_in, w_gate_up, b_gate, gla_norm_w, sinks, w_out, norm_ff_w, w_ff1, w_ff2, final_norm_w, loss_target, m_meta_tokens, m_norm_mix_w, m_w_in, m_w_gate_up, m_b_gate, m_gla_norm_w, m_sinks, m_w_out, m_norm_ff_w, m_w_ff1, m_w_ff2, m_final_norm_w, v_meta_tokens, v_norm_mix_w, v_w_in, v_w_gate_up, v_b_gate, v_gla_norm_w, v_sinks, v_w_out, v_norm_ff_w, v_w_ff1, v_w_ff2, v_final_norm_w):
    seq = x.shape[1]
    rows = LEAD + seq
    tm = _row_tile(rows)
    tm_wide = 1664 if rows % 1664 == 0 else tm
    dev =4 * lax.axis_index("x") + 2 * lax.axis_index("y") + lax.axis_index("c")

    small_shard = jnp.concatenate([meta_tokens, w_gate_up[0], jnp.zeros((N_META, 96), F32)], axis=1)
    g_in, g_small = _all_gather([w_in[0].astype(WIRE_DTYPE), small_shard])
    later_shards = [w_out[0].astype(WIRE_DTYPE), w_ff1[0].astype(WIRE_DTYPE), w_ff2[0].astype(WIRE_DTYPE)]
    win_p = _win_padded(g_in)
    meta_full = jnp.transpose(g_small[:, :, 0:128], (1, 0, 2)).reshape(N_META, D)
    wg_full = jnp.transpose(g_small[:, :, 128:160], (1, 0, 2)).reshape(GLA_RANK, GLA_HEADS * GLA_DK)
    wg_p = jnp.concatenate([wg_full, jnp.zeros((128 - GLA_RANK, 256), F32)], axis=0)

    lead = jnp.concatenate([jnp.zeros((META0, D), F32), meta_full], axis=0)
    tabs = _rope_tables(rows)
    proj, qr, kr, vr = _in_proj(x[0], lead, norm_mix_w, win_p, tabs, tm)
    oraw, og, states, (g_out, g_w1) = _gla_fwd(proj, wg_p, b_gate, gla_norm_w, later_shards[0:2])
    osw, (g_w2,) = _swa_fwd(qr, kr, vr, sinks, later_shards[2:3])
    wout_full = g_out.reshape(D, D)
    w2_full = g_w2.reshape(D_FF, D)
    w1_full = jnp.transpose(g_w1, (1, 0, 2)).reshape(D, D_FF)
    h1, f, ft = _out_proj(x[0], lead, og, osw, wout_full, norm_ff_w, tm)
    a, dh2, dh2t, loss_p, gfn_p = _ffn_fwd(f, h1, w1_full, w2_full, loss_target[0], final_norm_w.reshape(1, D), tm)

    da, dh1, gnf_p = _ffn_bwd_act(dh2, a, w1_full, w2_full, h1, norm_ff_w, tm)
    dw1, dw2 = _ffn_bwd_weights(ft, a, da, dh2t, tm_wide)
    where = jnp.stack([lax.axis_index("c"), 2 * lax.axis_index("x") + lax.axis_index("y")]).astype(jnp.int32)
    dog, dos, dwout, theirs_ffn = _out_proj_bwd(dh1, og, osw, wout_full, tm, [dw1, dw2])
    pairs_ffn = [_add_own_half(where, p, q, "reduce_pair_%d" % (2 + k), wire_copy=True)
                 for k, (p, q) in enumerate(zip([dw1, dw2], theirs_ffn))]
    sums_ffn, wires_ffn = [p[0] for p in pairs_ffn], [p[1] for p in pairs_ffn]
    dsq, dsk, dsv, dsink_p, (parts_ffn, (theirs_wout,)) = _swa_bwd(
        qr, kr, vr, osw, dos, sinks, _Jobs([("chips", wires_ffn), ("sibling", [dwout])]))
    sum_wout, wire_wout = _add_own_half(where, dwout, theirs_wout, "reduce_pair_1", wire_copy=True)
    (dgq, dgk, dgv, dgr, dlr, dwg_p, dbg_p, dgnw_p), ((parts_wout,),) = _gla_bwd(
        proj, oraw, states, dog, wg_p, b_gate, gla_norm_w, _Jobs([("chips", [wire_wout])]))
    grad_x, dlead, dproj, ut, gnm_p = _in_proj_bwd(x[0], lead, dh1, norm_mix_w, win_p, dgv, dgr, dsq, dgq, dgk, dsk,
                                                   dsv, dlr, tabs, tm)
    grad_x = grad_x[None]
    dwin, total = _in_proj_bwd_weights(ut, dproj, tm_wide,
                                       [dlead, dwg_p, gnm_p, gnf_p, gfn_p, dbg_p, dgnw_p, loss_p, dsink_p])

    (theirs_win,) = _rs_sibling([dwin])
    sum_win, sum_win_wire = _add_own_half(where, dwin, theirs_win, "reduce_pair_0", wire_copy=True)

    g_meta = lax.dynamic_slice(total, (R_META, dev * 128), (N_META, 128))
    g_wg = lax.dynamic_slice(total, (R_WG, dev * 32), (GLA_RANK, 32))
    g_norm_mix, g_norm_ff = total[R_NORM_MIX:R_NORM_MIX + 1], total[R_NORM_FF:R_NORM_FF + 1]
    g_final_norm = total[R_FINAL:R_FINAL + 1]
    g_b_gate, g_gla_norm = total[R_B_GATE:R_B_GATE + 1, 0:256], total[R_GLA_NORM:R_GLA_NORM + 1, 0:128]
    g_sinks = total[R_SINKS:R_SINKS + SWA_HEADS, 0].reshape(1, SWA_HEADS)
    loss = total[R_LOSS, 0]

    ((g_wout, d_wout, nm_wout, nv_wout), (g_w1s, d_w1, nm_w1, nv_w1), (g_w2s, d_w2, nm_w2, nv_w2)), ((parts_win,),) = \
        _adamw_shards(where, [(parts_wout, sum_wout, w_out[0], m_w_out[0], v_w_out[0]),
                              (parts_ffn[0], sums_ffn[0], w_ff1[0], m_w_ff1[0], v_w_ff1[0]),
                              (parts_ffn[1], sums_ffn[1], w_ff2[0], m_w_ff2[0], v_w_ff2[0])],
                      "adamw_w_out_ff", _Jobs([("chips", [sum_win_wire])]))
    ((g_win, d_win, nm_win, nv_win),), _ = _adamw_shards(
        where, [(parts_win, sum_win, w_in[0], m_w_in[0], v_w_in[0])], "adamw_w_in")

    names = ["meta", "wg", "norm_mix", "b_gate", "gla_norm", "sinks", "norm_ff", "final_norm"]
    ws = [meta_tokens, w_gate_up, norm_mix_w, b_gate, gla_norm_w, sinks, norm_ff_w, final_norm_w]
    gs = [g_meta, g_wg, g_norm_mix, g_b_gate, g_gla_norm, g_sinks, g_norm_ff, g_final_norm]
    ms = [m_meta_tokens, m_w_gate_up, m_norm_mix_w, m_b_gate, m_gla_norm_w, m_sinks, m_norm_ff_w, m_final_norm_w]
    vs = [v_meta_tokens, v_w_gate_up, v_norm_mix_w, v_b_gate, v_gla_norm_w, v_sinks, v_norm_ff_w, v_final_norm_w]
    flat = lambda t: t.reshape(-1, t.shape[-1])
    small_out = _adamw_small([(flat(w), flat(g), flat(m), flat(v)) for w, g, m, v in zip(ws, gs, ms, vs)])
    d_small = {n: small_out[k][0].reshape(ws[k].shape) for k, n in enumerate(names)}
    nm_small = {n: small_out[k][1].reshape(ws[k].shape) for k, n in enumerate(names)}
    nv_small = {n: small_out[k][2].reshape(ws[k].shape) for k, n in enumerate(names)}
    g_small_d = {n: g.reshape(ws[k].shape) for k, (n, g) in enumerate(zip(names, gs))}

    def ordered(big, small_d):
        win_v, wout_v, w1_v, w2_v = big
        return (small_d["meta"], small_d["norm_mix"], win_v[None], small_d["wg"], small_d["b_gate"],
                small_d["gla_norm"], small_d["sinks"], wout_v[None], small_d["norm_ff"], w1_v[None], w2_v[None],
                small_d["final_norm"])

    return (loss, grad_x,
            *ordered((g_win, g_wout, g_w1s, g_w2s), g_small_d),
            *ordered((d_win, d_wout, d_w1, d_w2), d_small),
            *ordered((nm_win, nm_wout, nm_w1, nm_w2), nm_small),
            *ordered((nv_win, nv_wout, nv_w1, nv_w2), nv_small))
```

```python
import functools

import jax
import jax.numpy as jnp
from jax import lax
from jax.experimental import pallas as pl
from jax.experimental.pallas import tpu as pltpu

F32 = jnp.float32
MXU_DTYPE = jnp.bfloat16
ACT_DTYPE = jnp.bfloat16
WIRE_DTYPE = jnp.bfloat16

D = 1024
N_META = 16
LEAD = 128
META0 = LEAD - N_META
EPS = 1e-5
GLA_HEADS, GLA_DK, GLA_DV, GLA_RANK, GLA_CHUNK = 4, 64, 128, 16, 64
GLA_TAU = 16.0
SWA_HEADS, SWA_KV, SWA_GROUP, SWA_HD, SWA_BLOCK = 8, 2, 4, 64, 128
ROPE_DIM, ROPE_THETA = 16, 500000.0
D_FF = 4096
N_DEV = 8
FF_TILE = D_FF // N_DEV
FF_WIDE = 2048
NEG = -1e30

C_GV, C_GR, C_GQ, C_GK, C_LR, C_SQ, C_SK, C_SV = 0, 512, 1024, 1280, 1536, 1664, 2176, 2304
DGLA = 1664
DINP = 2432
DIN = 2320
O_GQ, O_GK, O_GV, O_GR, O_LR, O_SQ, O_SK, O_SV = (0, 256), (256, 512), (512, 1024), (1024, 1536), (1536, 1552), (1552, 2064), (2064, 2192), (2192, 2320)

ADAM_LR, ADAM_B1, ADAM_B2, ADAM_EPS, ADAM_WD, ADAM_STEP = 0.001, 0.9, 0.999, 1e-08, 0.01, 10

MESH = pl.DeviceIdType.MESH
ANY = pl.BlockSpec(memory_space=pl.ANY)
HIGHEST = lax.Precision.HIGHEST


def _cp(sem=None, vmem_mb=None):
    kw = {}
    if sem is not None:
        kw["dimension_semantics"] = sem
    if vmem_mb is not None:
        kw["vmem_limit_bytes"] = vmem_mb << 20
    return pltpu.CompilerParams(**kw)


def _mm(a, b):
    return jnp.dot(a.astype(MXU_DTYPE), b.astype(MXU_DTYPE), preferred_element_type=F32)


def _mm_nt(a, b):
    return lax.dot_general(a.astype(MXU_DTYPE), b.astype(MXU_DTYPE), (((1,), (1,)), ((), ())),
                           preferred_element_type=F32)


def _mm_tn(a, b):
    return lax.dot_general(a.astype(MXU_DTYPE), b.astype(MXU_DTYPE), (((0,), (0,)), ((), ())),
                           preferred_element_type=F32)


def _logsigmoid(z):
    return jnp.minimum(z, 0.0) - jnp.log(1.0 + jnp.exp(-jnp.abs(z)))


def _sigmoid(z):
    return 1.0 / (1.0 + jnp.exp(-z))


def _row_tile(rows):
    return 640 if rows % 640 == 0 else 128


def _mesh_pos():
    return lax.axis_index("x"), lax.axis_index("y"), lax.axis_index("c")


def _all_gather(shards):
    n = len(shards)

    def body(*refs):
        start, forward, finish = _gather_schedule(refs[:n], refs[n:2 * n], *refs[2 * n:])
        start()
        for j in range(3):
            forward(j)
        finish()

    gathered = pl.pallas_call(
        body, name="all_gather_weights",
        out_shape=_gathered_shapes(shards), in_specs=[ANY] * n, out_specs=[ANY] * n,
        scratch_shapes=_gather_sems(n),
    )(*shards)
    return _with_own_block(gathered, shards)


def _gathered_shapes(shards):
    return [jax.ShapeDtypeStruct((N_DEV,) + s.shape, s.dtype) for s in shards]


def _gather_sems(n):
    return [pltpu.SemaphoreType.DMA((7 * n,)), pltpu.SemaphoreType.DMA((7 * n,))] if n else []


def _place_gather(step, steps, shard_refs, gathered_refs, sems):
    if not shard_refs:
        return
    start, forward, finish = _gather_schedule(shard_refs, gathered_refs, *sems)
    pl.when(step == 0)(start)
    for j, at in enumerate((steps * 7 // 10, steps * 8 // 10, steps * 9 // 10)):
        pl.when(step == at)(functools.partial(forward, j))
    pl.when(step == steps - 1)(finish)


def _with_own_block(gathered, shards):
    dev = 4 * lax.axis_index("x") + 2 * lax.axis_index("y") + lax.axis_index("c")
    return [lax.dynamic_update_index_in_dim(g, s, dev, 0) for g, s in zip(gathered, shards)]


def _gather_schedule(ins, outs, send_sems, recv_sems):
    n = len(ins)
    x, y, c = _mesh_pos()
    me, sibling = (x, y, c), (x, y, 1 - c)
    chips = [(1 - x, y), (x, 1 - y), (1 - x, 1 - y)]

    def copy(a, k, block, to, src=None):
        dst = outs[a].at[4 * block[0] + 2 * block[1] + block[2]]
        return pltpu.make_async_remote_copy(
            src_ref=dst if src is None else src, dst_ref=dst,
            send_sem=send_sems.at[a * 7 + k], recv_sem=recv_sems.at[a * 7 + k],
            device_id=to, device_id_type=MESH)

    def first(a):
        return [copy(a, 0, me, sibling, src=ins[a])] + [copy(a, 1 + j, me, (*chip, c), src=ins[a])
                                                        for j, chip in enumerate(chips)]

    def start():
        for a in range(n):
            for cp in first(a):
                cp.start()

    def forward(j):
        for a in range(n):
            copy(a, 1 + j, (*chips[j], c), me).wait_recv()
            copy(a, 4 + j, (*chips[j], c), sibling).start()

    def finish():
        for a in range(n):
            copy(a, 0, sibling, me).wait_recv()
            for j, chip in enumerate(chips):
                copy(a, 4 + j, (*chip, 1 - c), me).wait_recv()
        for a in range(n):
            for cp in first(a) + [copy(a, 4 + j, (*chip, c), sibling) for j, chip in enumerate(chips)]:
                cp.wait_send()

    return start, forward, finish


def _rs_sibling(gs):
    n = len(gs)

    def body(*refs):
        start, finish = _sibling_schedule(refs[:n], refs[n:2 * n], *refs[2 * n:])
        start()
        finish()

    return pl.pallas_call(
        body, name="reduce_scatter_sibling",
        out_shape=_sibling_shapes(gs), in_specs=[ANY] * n, out_specs=[ANY] * n,
        scratch_shapes=_sibling_sems(n),
    )(*gs)


def _sibling_shapes(gs):
    return [jax.ShapeDtypeStruct(g.shape[1:], g.dtype) for g in gs]


def _sibling_sems(n):
    return [pltpu.SemaphoreType.DMA((n,)), pltpu.SemaphoreType.DMA((n,))]


def _sibling_schedule(ins, land, send_sems, recv_sems):
    x, y, c = _mesh_pos()

    def copies():
        return [pltpu.make_async_remote_copy(
            src_ref=ins[a].at[1 - c], dst_ref=land[a], send_sem=send_sems.at[a], recv_sem=recv_sems.at[a],
            device_id=(x, y, 1 - c), device_id_type=MESH) for a in range(len(ins))]

    def start():
        for cp in copies():
            cp.start()

    def finish():
        for cp in copies():
            cp.wait_recv()
        for cp in copies():
            cp.wait_send()

    return start, finish


def _rs_chips(ps):
    n = len(ps)

    def body(*refs):
        start, finish = _chips_schedule(refs[:n], refs[n:2 * n], *refs[2 * n:])
        start()
        finish()

    return pl.pallas_call(
        body, name="reduce_scatter_chips",
        out_shape=_chips_shapes(ps), in_specs=[ANY] * n, out_specs=[ANY] * n,
        scratch_shapes=_chips_sems(n),
    )(*ps)


def _chips_shapes(ps):
    return [jax.ShapeDtypeStruct((3,) + p.shape[1:], p.dtype) for p in ps]


def _chips_sems(n):
    return [pltpu.SemaphoreType.DMA((3 * n,)), pltpu.SemaphoreType.DMA((3 * n,))]


def _chips_schedule(ins, land, send_sems, recv_sems):
    x, y, c = _mesh_pos()
    chips = [(1 - x, y), (x, 1 - y), (1 - x, 1 - y)]

    def copies():
        return [pltpu.make_async_remote_copy(
            src_ref=ins[a].at[2 * chip[0] + chip[1]], dst_ref=land[a].at[j],
            send_sem=send_sems.at[3 * a + j], recv_sem=recv_sems.at[3 * a + j],
            device_id=(*chip, c), device_id_type=MESH) for a in range(len(ins)) for j, chip in enumerate(chips)]

    def start():
        for cp in copies():
            cp.start()

    def finish():
        for cp in copies():
            cp.wait_recv()
        for cp in copies():
            cp.wait_send()

    return start, finish


class _Jobs:
    def __init__(self, jobs):
        self.jobs = jobs
        self.inputs = [a for _, arrs in jobs for a in arrs]
        self.out_shapes = [s for kind, arrs in jobs
                           for s in (_sibling_shapes(arrs) if kind == "sibling" else _chips_shapes(arrs))]
        self.sems = [s for kind, arrs in jobs
                     for s in (_sibling_sems(len(arrs)) if kind == "sibling" else _chips_sems(len(arrs)))]
        self.n = len(self.inputs)

    def bind(self, in_refs, out_refs, sem_refs):
        starts, finishes, at = [], [], 0
        for k, (kind, arrs) in enumerate(self.jobs):
            schedule = _sibling_schedule if kind == "sibling" else _chips_schedule
            start, finish = schedule(in_refs[at:at + len(arrs)], out_refs[at:at + len(arrs)],
                                     sem_refs[2 * k], sem_refs[2 * k + 1])
            starts.append(start)
            finishes.append(finish)
            at += len(arrs)

        def start_all():
            for f in starts:
                f()

        def finish_all():
            for f in finishes:
                f()

        return start_all, finish_all

    def split(self, outs):
        res, at = [], 0
        for _, arrs in self.jobs:
            res.append(list(outs[at:at + len(arrs)]))
            at += len(arrs)
        return res


R_META, R_WG, R_NORM_MIX, R_NORM_FF, R_FINAL, R_B_GATE, R_GLA_NORM, R_LOSS, R_SINKS, SMALL_ROWS = 0, 16, 32, 33, 34, 35, 36, 37, 40, 48


SMALL_SPECS = [pl.BlockSpec((LEAD, D), lambda i: (0, 0)), pl.BlockSpec((128, 256), lambda i: (0, 0)),
               pl.BlockSpec((8, D), lambda i: (0, 0)), pl.BlockSpec((8, D), lambda i: (0, 0)),
               pl.BlockSpec((8, D), lambda i: (0, 0)), pl.BlockSpec((8, 256), lambda i: (0, 0)),
               pl.BlockSpec((8, 128), lambda i: (0, 0)), pl.BlockSpec((8, 128), lambda i: (0, 0)),
               pl.BlockSpec((8, 128), lambda i: (0, 0))]


def _small_sum_scratch():
    return [pltpu.VMEM((SMALL_ROWS, D), F32), pltpu.VMEM((N_DEV, SMALL_ROWS, D), F32),
            pltpu.SemaphoreType.DMA((7,)), pltpu.SemaphoreType.DMA((7,))]


def _small_sum_schedule(small_refs, out_ref, p_ref, land, send_sems, recv_sems):
    dlead_ref, dwg_ref, gnm_ref, gnf_ref, gfn_ref, dbg_ref, dgnw_ref, loss_ref, dsink_ref = small_refs
    x, y, c = _mesh_pos()
    me = 4 * x + 2 * y + c

    def copies():
        res = []
        for k in range(1, N_DEV):
            bx, by, bc = (k >> 2) & 1, (k >> 1) & 1, k & 1
            peer = (1 - x if bx else x, 1 - y if by else y, 1 - c if bc else c)
            res.append(pltpu.make_async_remote_copy(
                src_ref=p_ref, dst_ref=land.at[me], send_sem=send_sems.at[k - 1], recv_sem=recv_sems.at[k - 1],
                device_id=peer, device_id_type=MESH))
        return res

    def start():
        p_ref[...] = jnp.zeros_like(p_ref)
        p_ref[R_META:R_META + N_META, :] = dlead_ref[META0:LEAD, :]
        p_ref[R_WG:R_WG + GLA_RANK, 0:256] = dwg_ref[0:GLA_RANK, :]
        p_ref[R_NORM_MIX:R_NORM_MIX + 1, :] = gnm_ref[0:1, :]
        p_ref[R_NORM_FF:R_NORM_FF + 1, :] = gnf_ref[0:1, :]
        p_ref[R_FINAL:R_FINAL + 1, :] = gfn_ref[0:1, :]
        p_ref[R_B_GATE:R_B_GATE + 1, 0:256] = dbg_ref[0:1, :]
        p_ref[R_GLA_NORM:R_GLA_NORM + 1, 0:128] = dgnw_ref[0:1, :]
        p_ref[R_LOSS:R_LOSS + 1, 0:128] = loss_ref[0:1, :]
        p_ref[R_SINKS:R_SINKS + SWA_HEADS, 0:128] = dsink_ref[...]
        land[me] = p_ref[...]
        for cp in copies():
            cp.start()

    def finish():
        for cp in copies():
            cp.wait_recv()
        for cp in copies():
            cp.wait_send()
        acc = land[0]
        for d in range(1, N_DEV):
            acc = acc + land[d]
        out_ref[...] = acc

    return start, finish


def _token_specs(tm, grid_rank=1):
    nb = tm // LEAD

    def spec(k):
        if grid_rank == 1:
            return pl.BlockSpec((LEAD, D), lambda i: (jnp.maximum(i * nb + k - 1, 0), 0))
        return pl.BlockSpec((LEAD, D), lambda i, j: (jnp.maximum(i * nb + k - 1, 0), 0))

    return [spec(k) for k in range(nb)]


def _h_tile(i, lead_ref, x_refs):
    first = jnp.where(i == 0, lead_ref[...], x_refs[0][...])
    return jnp.concatenate([first] + [r[...] for r in x_refs[1:]], axis=0)


def _in_proj(x, lead, nw, win_p, tabs, tm):
    rows = LEAD + x.shape[0]
    nb = tm // LEAD

    def body(*refs):
        x_refs, (lead_ref, nw_ref, w_ref, c_ref, sa_ref, sb_ref, o_ref, q_ref, k_ref, v_ref) = refs[:nb], refs[nb:]
        h = _h_tile(pl.program_id(0), lead_ref, x_refs)
        rstd = lax.rsqrt(jnp.mean(h * h, axis=-1, keepdims=True) + EPS)
        u = (h * rstd * nw_ref[...]).astype(MXU_DTYPE)
        proj = jnp.dot(u, w_ref[...].astype(MXU_DTYPE), preferred_element_type=F32)
        o_ref[...] = proj[:, 0:DGLA]
        cos, sa, sb = c_ref[...], sa_ref[...], sb_ref[...]
        q_ref[...] = (_rope(proj[:, C_SQ:C_SK], cos, sa, sb) * (SWA_HD ** -0.5)).astype(ACT_DTYPE)
        k_ref[...] = _rope(proj[:, C_SK:C_SV], cos, sa, sb).astype(ACT_DTYPE)
        v_ref[...] = proj[:, C_SV:DINP].astype(ACT_DTYPE)

    row = lambda w: pl.BlockSpec((tm, w), lambda i: (i, 0))
    return pl.pallas_call(
        body, name="in_proj", grid=(rows // tm,),
        in_specs=_token_specs(tm) + [pl.BlockSpec((LEAD, D), lambda i: (0, 0)), pl.BlockSpec((1, D), lambda i: (0, 0)),
                                     pl.BlockSpec((D, DINP), lambda i: (0, 0)), row(128), row(128), row(128)],
        out_specs=[row(DGLA), row(512), row(128), row(128)],
        out_shape=[jax.ShapeDtypeStruct((rows, DGLA), F32), jax.ShapeDtypeStruct((rows, 512), ACT_DTYPE),
                   jax.ShapeDtypeStruct((rows, 128), ACT_DTYPE), jax.ShapeDtypeStruct((rows, 128), ACT_DTYPE)],
        compiler_params=_cp(("arbitrary",), 56),
    )(*([x] * nb), lead, nw, win_p, *tabs)


def _rope_tables(rows):
    pos = (jnp.arange(rows, dtype=jnp.int32) - META0).astype(F32)
    inv_freq = 1.0 / (ROPE_THETA ** (jnp.arange(0, ROPE_DIM, 2, dtype=F32) / ROPE_DIM))
    ang = pos[:, None] * jnp.tile(inv_freq, 128 // (ROPE_DIM // 2))[None, :]
    in_head = jnp.arange(128, dtype=jnp.int32)[None, :] % SWA_HD
    cos, sin = jnp.cos(ang), jnp.sin(ang)
    c_tab = jnp.where(in_head < ROPE_DIM, cos, 1.0)
    sa_tab = jnp.where(in_head < ROPE_DIM // 2, -sin, 0.0)
    sb_tab = jnp.where((in_head >= ROPE_DIM // 2) & (in_head < ROPE_DIM), sin, 0.0)
    return c_tab, sa_tab, sb_tab


def _rope(xv, cos, sa, sb):
    width = xv.shape[1]
    reps = width // 128
    if reps > 1:
        cos, sa, sb = (jnp.tile(t, (1, reps)) for t in (cos, sa, sb))
    return xv * cos + pltpu.roll(xv, width - 8, 1) * sa + pltpu.roll(xv, 8, 1) * sb


def _unrope(dy, cos, sa, sb):
    width = dy.shape[1]
    reps = width // 128
    if reps > 1:
        cos, sa, sb = (jnp.tile(t, (1, reps)) for t in (cos, sa, sb))
    return dy * cos + pltpu.roll(dy * sa, 8, 1) + pltpu.roll(dy * sb, width - 8, 1)


def _gla_group(nc):
    for g in (5, 2):
        if nc % g == 0:
            return g
    return 1


def _gla_gates(lr, wg, bg, first_row, nrows):
    zg = _mm(lr, wg) + bg
    row = first_row + lax.broadcasted_iota(jnp.int32, (nrows, 1), 0)
    live = row >= META0
    g = jnp.where(live, _logsigmoid(zg) * (1.0 / GLA_TAU), 0.0)
    ii = lax.broadcasted_iota(jnp.int32, (nrows, nrows), 0)
    jj = lax.broadcasted_iota(jnp.int32, (nrows, nrows), 1)
    same = (ii // GLA_CHUNK) == (jj // GLA_CHUNK)
    lower, upper = same & (jj <= ii), same & (jj >= ii)
    b = jnp.dot(lower.astype(F32), g, precision=HIGHEST, preferred_element_type=F32)
    return zg, live, lower, upper, b


def _tril64():
    ii = lax.broadcasted_iota(jnp.int32, (GLA_CHUNK, GLA_CHUNK), 0)
    jj = lax.broadcasted_iota(jnp.int32, (GLA_CHUNK, GLA_CHUNK), 1)
    return jj <= ii


def _gla_fwd(proj, wg_p, bg, gnw, shards):
    rows = proj.shape[0]
    nc = rows // GLA_CHUNK
    group = _gla_group(nc)
    steps, nrows = nc // group, group * GLA_CHUNK
    ns = len(shards)

    def body(q_ref, k_ref, v_ref, r_ref, lr_ref, wg_ref, bg_ref, gnw_ref, *rest):
        shard_refs, rest = rest[:ns], rest[ns:]
        oraw_ref, og_ref, st_ref = rest[:3]
        gathered_refs, rest = rest[3:3 + ns], rest[3 + ns:]
        state = rest[0]
        c = pl.program_id(0)

        @pl.when(c == 0)
        def _():
            state[...] = jnp.zeros_like(state)

        _place_gather(c, steps, shard_refs, gathered_refs, rest[1:])
        _, _, _, _, b = _gla_gates(lr_ref[...], wg_ref[...], bg_ref[...], c * nrows, nrows)
        eb = jnp.exp(b)
        gq = q_ref[...] * (GLA_DK ** -0.5) * eb
        gk = k_ref[...] * jnp.exp(-b)
        v = v_ref[...]
        gnw_v = gnw_ref[...]
        tril = _tril64()
        pairs = [(h, gi) for h in range(GLA_HEADS) for gi in range(group)]
        rs = {gi: slice(gi * GLA_CHUNK, (gi + 1) * GLA_CHUNK) for gi in range(group)}
        s64 = {h: slice(h * GLA_DK, (h + 1) * GLA_DK) for h in range(GLA_HEADS)}
        s128 = {h: slice(h * GLA_DV, (h + 1) * GLA_DV) for h in range(GLA_HEADS)}
        qh = {(h, gi): gq[rs[gi], s64[h]] for h, gi in pairs}
        kh = {(h, gi): gk[rs[gi], s64[h]] for h, gi in pairs}
        vh = {(h, gi): v[rs[gi], s128[h]] for h, gi in pairs}
        ebl = {(h, gi): eb[(gi + 1) * GLA_CHUNK - 1:(gi + 1) * GLA_CHUNK, s64[h]] for h, gi in pairs}
        av = {pr: _mm(jnp.where(tril, _mm_nt(qh[pr], kh[pr]), 0.0), vh[pr]) for pr in pairs}
        inc = {pr: _mm_tn(vh[pr], kh[pr] * ebl[pr]) for pr in pairs}
        st = {}
        for h in range(GLA_HEADS):
            cur = state[h]
            for gi in range(group):
                st[h, gi] = cur
                st_ref[gi, h] = cur
                cur = cur * ebl[h, gi] + inc[h, gi]
            state[h] = cur
        for h, gi in pairs:
            o = av[h, gi] + _mm_nt(qh[h, gi], st[h, gi])
            oraw_ref[rs[gi], s128[h]] = o
            rstd = lax.rsqrt(jnp.mean(o * o, axis=-1, keepdims=True) + EPS)
            rh = r_ref[rs[gi], s128[h]]
            og_ref[rs[gi], s128[h]] = (o * rstd * gnw_v * (rh * _sigmoid(rh))).astype(ACT_DTYPE)

    nb = lambda w, col: pl.BlockSpec((nrows, w), lambda c: (c, col // w))
    const = lambda shape: pl.BlockSpec(shape, lambda c: (0,) * len(shape))
    outs = pl.pallas_call(
        body, name="gla_fwd", grid=(steps,),
        in_specs=[nb(256, C_GQ), nb(256, C_GK), nb(512, C_GV), nb(512, C_GR), nb(128, C_LR),
                  const((128, 256)), const((1, 256)), const((1, 128))] + [ANY] * ns,
        out_specs=[pl.BlockSpec((nrows, 512), lambda c: (c, 0)), pl.BlockSpec((nrows, 512), lambda c: (c, 0)),
                   pl.BlockSpec((group, GLA_HEADS, GLA_DV, GLA_DK), lambda c: (c, 0, 0, 0))] + [ANY] * ns,
        out_shape=[jax.ShapeDtypeStruct((rows, 512), F32), jax.ShapeDtypeStruct((rows, 512), ACT_DTYPE),
                   jax.ShapeDtypeStruct((nc, GLA_HEADS, GLA_DV, GLA_DK), F32)] + _gathered_shapes(shards),
        scratch_shapes=[pltpu.VMEM((GLA_HEADS, GLA_DV, GLA_DK), F32)] + _gather_sems(ns),
        compiler_params=_cp(("arbitrary",)),
    )(proj, proj, proj, proj, proj, wg_p, bg, gnw, *shards)
    return outs[0], outs[1], outs[2], _with_own_block(outs[3:], shards)


def _swa_mask(n):
    shape = (SWA_GROUP * SWA_BLOCK, 3 * SWA_BLOCK)
    qi = lax.broadcasted_iota(jnp.int32, shape, 0) & (SWA_BLOCK - 1)
    jj = lax.broadcasted_iota(jnp.int32, shape, 1)
    meta = (jj < SWA_BLOCK) & (jj >= META0) & ((n > 0) | (jj <= qi))
    prev = (jj >= SWA_BLOCK) & (jj < 2 * SWA_BLOCK) & (n >= 2) & (jj - SWA_BLOCK > qi)
    cur = (jj >= 2 * SWA_BLOCK) & (n >= 1) & (jj - 2 * SWA_BLOCK <= qi)
    return meta | prev | cur


def _stack_heads(t, kvh):
    return jnp.concatenate([t[:, (kvh * SWA_GROUP + g) * SWA_HD:(kvh * SWA_GROUP + g + 1) * SWA_HD]
                            for g in range(SWA_GROUP)], axis=0)


def _stack_sinks(sink_ref, kvh):
    return jnp.concatenate([jnp.full((SWA_BLOCK, 1), sink_ref[0, kvh * SWA_GROUP + g], F32)
                            for g in range(SWA_GROUP)], axis=0)


def _swa_group(nblk):
    return 5 if nblk % 5 == 0 else 1


def _swa_specs(group):
    blk = lambda w: pl.BlockSpec((group * SWA_BLOCK, w), lambda n: (n, 0))
    first = pl.BlockSpec((SWA_BLOCK, 128), lambda n: (0, 0))
    prev = pl.BlockSpec((SWA_BLOCK, 128), lambda n: (jnp.maximum(n * group - 1, 0), 0))
    return blk, first, prev


def _swa_keys(first_ref, prev_ref, cur_ref, g):
    own = cur_ref[g * SWA_BLOCK:(g + 1) * SWA_BLOCK, :]
    before = prev_ref[...] if g == 0 else cur_ref[(g - 1) * SWA_BLOCK:g * SWA_BLOCK, :]
    return jnp.concatenate([first_ref[...], before, own], axis=0)


def _swa_fwd(qr, kr, vr, sinks, shards):
    rows = qr.shape[0]
    nblk = rows // SWA_BLOCK
    group = _swa_group(nblk)
    steps = nblk // group
    ns = len(shards)

    def body(q_ref, k0, kp, kc, v0, vp, vc, sink_ref, *rest):
        o_ref = rest[ns]
        _place_gather(pl.program_id(0), steps, rest[:ns], rest[ns + 1:2 * ns + 1], rest[2 * ns + 1:])
        for g in range(group):
            n = pl.program_id(0) * group + g
            rs = slice(g * SWA_BLOCK, (g + 1) * SWA_BLOCK)
            kall, vall = _swa_keys(k0, kp, kc, g), _swa_keys(v0, vp, vc, g)
            mask = _swa_mask(n)[0:SWA_BLOCK]
            heads = range(SWA_HEADS)
            hs = [slice(h * SWA_HD, (h + 1) * SWA_HD) for h in heads]
            kv = [slice((h // SWA_GROUP) * SWA_HD, (h // SWA_GROUP + 1) * SWA_HD) for h in heads]
            s = [jnp.where(mask, _mm_nt(q_ref[rs, hs[h]], kall[:, kv[h]]), NEG) for h in heads]
            m = [jnp.maximum(jnp.max(s[h], axis=-1, keepdims=True), sink_ref[0, h]) for h in heads]
            p = [jnp.exp(s[h] - m[h]) for h in heads]
            den = [jnp.sum(p[h], axis=-1, keepdims=True) + jnp.exp(sink_ref[0, h] - m[h]) for h in heads]
            o = [_mm(p[h], vall[:, kv[h]]) for h in heads]
            for h in heads:
                o_ref[rs, hs[h]] = (o[h] / den[h]).astype(ACT_DTYPE)

    blk, first, prev = _swa_specs(group)
    outs = pl.pallas_call(
        body, name="swa_fwd", grid=(steps,),
        in_specs=[blk(512), first, prev, blk(128), first, prev, blk(128),
                  pl.BlockSpec(memory_space=pltpu.SMEM)] + [ANY] * ns,
        out_specs=[blk(512)] + [ANY] * ns,
        out_shape=[jax.ShapeDtypeStruct((rows, 512), ACT_DTYPE)] + _gathered_shapes(shards),
        scratch_shapes=_gather_sems(ns),
        compiler_params=_cp(("arbitrary",)),
    )(qr, kr, kr, kr, vr, vr, vr, sinks, *shards)
    return outs[0], _with_own_block(outs[1:], shards)


def _out_proj(x, lead, og, osw, wout, nfw, tm):
    rows = LEAD + x.shape[0]
    nb = tm // LEAD

    def body(*refs):
        x_refs, (lead_ref, og_ref, os_ref, w_ref, nw_ref, h1_ref, f_ref, ft_ref) = refs[:nb], refs[nb:]
        h0 = _h_tile(pl.program_id(0), lead_ref, x_refs)
        h1 = h0 + _mm(og_ref[...], w_ref[0:512, :]) + _mm(os_ref[...], w_ref[512:1024, :])
        h1_ref[...] = h1
        rstd = lax.rsqrt(jnp.mean(h1 * h1, axis=-1, keepdims=True) + EPS)
        f = h1 * rstd * nw_ref[...]
        f_ref[...] = f.astype(ACT_DTYPE)
        ft_ref[...] = f.T.astype(ACT_DTYPE)

    row = lambda w: pl.BlockSpec((tm, w), lambda i: (i, 0))
    return pl.pallas_call(
        body, name="out_proj", grid=(rows // tm,),
        in_specs=_token_specs(tm) + [pl.BlockSpec((LEAD, D), lambda i: (0, 0)), row(512), row(512),
                                     pl.BlockSpec((D, D), lambda i: (0, 0)), pl.BlockSpec((1, D), lambda i: (0, 0))],
        out_specs=[row(D), row(D), pl.BlockSpec((D, tm), lambda i: (0, i))],
        out_shape=[jax.ShapeDtypeStruct((rows, D), F32), jax.ShapeDtypeStruct((rows, D), ACT_DTYPE),
                   jax.ShapeDtypeStruct((D, rows), ACT_DTYPE)],
        compiler_params=_cp(("arbitrary",), 48),
    )(*([x] * nb), lead, og, osw, wout, nfw)


def _ffn_fwd(f, h1, w1, w2, tgt, fnw, tm):
    rows = f.shape[0]
    nj = D_FF // FF_WIDE
    nb = tm // LEAD

    def body(f_ref, h1_ref, w1_ref, w2_ref, nw_ref, *rest):
        t_refs, (a_ref, dh2_ref, dh2t_ref, loss_ref, gfn_ref, acc) = rest[:nb], rest[nb:]
        i, j = pl.program_id(0), pl.program_id(1)

        @pl.when((i == 0) & (j == 0))
        def _():
            loss_ref[...] = jnp.zeros_like(loss_ref)
            gfn_ref[...] = jnp.zeros_like(gfn_ref)

        @pl.when(j == 0)
        def _():
            acc[...] = jnp.zeros_like(acc)

        a = _mm(f_ref[...], w1_ref[...])
        a_ref[...] = a.astype(ACT_DTYPE)
        z = jnp.square(jnp.maximum(a, 0.0))
        acc[...] += _mm(z, w2_ref[...])

        @pl.when(j == nj - 1)
        def _():
            h2 = h1_ref[...] + acc[...]
            rstd = lax.rsqrt(jnp.mean(h2 * h2, axis=-1, keepdims=True) + EPS)
            hn = h2 * rstd
            nw = nw_ref[...]
            row = i * tm + lax.broadcasted_iota(jnp.int32, (tm, 1), 0)
            target = jnp.concatenate([t[...] for t in t_refs], axis=0)
            err = jnp.where(row >= LEAD, hn * nw - target, 0.0)
            row_loss = jnp.sum(err * err, axis=-1, keepdims=True) * (1.0 / D)
            loss_ref[...] += jnp.broadcast_to(0.5 * jnp.sum(row_loss, axis=0, keepdims=True), loss_ref.shape)
            dy = err * (1.0 / D)
            gfn_ref[...] += jnp.broadcast_to(jnp.sum(dy * hn, axis=0, keepdims=True), gfn_ref.shape)
            dhn = dy * nw
            dh2 = rstd * (dhn - hn * jnp.mean(dhn * hn, axis=-1, keepdims=True))
            dh2_ref[...] = dh2
            dh2t_ref[...] = dh2.T.astype(ACT_DTYPE)

    return pl.pallas_call(
        body, name="ffn_fwd", grid=(rows // tm, nj),
        in_specs=[pl.BlockSpec((tm, D), lambda i, j: (i, 0)), pl.BlockSpec((tm, D), lambda i, j: (i, 0)),
                  pl.BlockSpec((D, FF_WIDE), lambda i, j: (0, j)),
                  pl.BlockSpec((FF_WIDE, D), lambda i, j: (j, 0)),
                  pl.BlockSpec((1, D), lambda i, j: (0, 0))] + _token_specs(tm, grid_rank=2),
        out_specs=[pl.BlockSpec((tm, FF_WIDE), lambda i, j: (i, j)), pl.BlockSpec((tm, D), lambda i, j: (i, 0)),
                   pl.BlockSpec((D, tm), lambda i, j: (0, i)),
                   pl.BlockSpec((8, 128), lambda i, j: (0, 0)), pl.BlockSpec((8, D), lambda i, j: (0, 0))],
        out_shape=[jax.ShapeDtypeStruct((rows, D_FF), ACT_DTYPE), jax.ShapeDtypeStruct((rows, D), F32),
                   jax.ShapeDtypeStruct((D, rows), ACT_DTYPE),
                   jax.ShapeDtypeStruct((8, 128), F32), jax.ShapeDtypeStruct((8, D), F32)],
        scratch_shapes=[pltpu.VMEM((tm, D), F32)],
        compiler_params=_cp(("arbitrary", "arbitrary"), 56),
    )(f, h1, w1, w2, fnw, *([tgt] * nb))


def _ffn_bwd_act(dh2, a, w1, w2, h1, nfw, tm):
    rows = dh2.shape[0]
    nj = D_FF // FF_WIDE

    def body(dh2_ref, a_ref, w1_ref, w2_ref, h1_ref, nw_ref, da_ref, dh1_ref, gnf_ref, acc):
        i, j = pl.program_id(0), pl.program_id(1)

        @pl.when((i == 0) & (j == 0))
        def _():
            gnf_ref[...] = jnp.zeros_like(gnf_ref)

        @pl.when(j == 0)
        def _():
            acc[...] = jnp.zeros_like(acc)

        dz = _mm_nt(dh2_ref[...], w2_ref[...])
        da = dz * (2.0 * jnp.maximum(a_ref[...].astype(F32), 0.0))
        da_ref[...] = da.astype(ACT_DTYPE)
        acc[...] += _mm_nt(da, w1_ref[...])

        @pl.when(j == nj - 1)
        def _():
            h1 = h1_ref[...]
            rstd = lax.rsqrt(jnp.mean(h1 * h1, axis=-1, keepdims=True) + EPS)
            hn = h1 * rstd
            df = acc[...]
            gnf_ref[...] += jnp.broadcast_to(jnp.sum(df * hn, axis=0, keepdims=True), gnf_ref.shape)
            dfn = df * nw_ref[...]
            dh1_ref[...] = dh2_ref[...] + rstd * (dfn - hn * jnp.mean(dfn * hn, axis=-1, keepdims=True))

    return pl.pallas_call(
        body, name="ffn_bwd_act", grid=(rows // tm, nj),
        in_specs=[pl.BlockSpec((tm, D), lambda i, j: (i, 0)), pl.BlockSpec((tm, FF_WIDE), lambda i, j: (i, j)),
                  pl.BlockSpec((D, FF_WIDE), lambda i, j: (0, j)),
                  pl.BlockSpec((FF_WIDE, D), lambda i, j: (j, 0)),
                  pl.BlockSpec((tm, D), lambda i, j: (i, 0)), pl.BlockSpec((1, D), lambda i, j: (0, 0))],
        out_specs=[pl.BlockSpec((tm, FF_WIDE), lambda i, j: (i, j)), pl.BlockSpec((tm, D), lambda i, j: (i, 0)),
                   pl.BlockSpec((8, D), lambda i, j: (0, 0))],
        out_shape=[jax.ShapeDtypeStruct((rows, D_FF), ACT_DTYPE), jax.ShapeDtypeStruct((rows, D), F32),
                   jax.ShapeDtypeStruct((8, D), F32)],
        scratch_shapes=[pltpu.VMEM((tm, D), F32)],
        compiler_params=_cp(("arbitrary", "arbitrary"), 56),
    )(dh2, a, w1, w2, h1, nfw)


def _ffn_bwd_weights(ft, a, da, dh2t, tm):
    rows = a.shape[0]
    steps = rows // tm
    pair = 2 * FF_TILE

    def body(ft_ref, a_ref, da_ref, dh2t_ref, dw1_ref, dw2_ref, dw2t):
        i = pl.program_id(1)

        @pl.when(i == 0)
        def _():
            dw1_ref[...] = jnp.zeros_like(dw1_ref)
            dw2t[...] = jnp.zeros_like(dw2t)

        z = jnp.square(jnp.maximum(a_ref[...].astype(F32), 0.0))
        dw1 = _mm(ft_ref[...], da_ref[...])
        for core in range(2):
            dw1_ref[core] += dw1[:, core * FF_TILE:(core + 1) * FF_TILE]
        dw2t[...] += _mm(dh2t_ref[...], z)

        @pl.when(i == steps - 1)
        def _():
            for core in range(2):
                dw2_ref[core] = dw2t[:, core * FF_TILE:(core + 1) * FF_TILE].T

    return pl.pallas_call(
        body, name="ffn_bwd_weights", grid=(N_DEV // 2, steps),
        in_specs=[pl.BlockSpec((D, tm), lambda j, i: (0, i)), pl.BlockSpec((tm, pair), lambda j, i: (i, j)),
                  pl.BlockSpec((tm, pair), lambda j, i: (i, j)), pl.BlockSpec((D, tm), lambda j, i: (0, i))],
        out_specs=[pl.BlockSpec((2, None, D, FF_TILE), lambda j, i: (0, j, 0, 0)),
                   pl.BlockSpec((2, None, FF_TILE, D), lambda j, i: (0, j, 0, 0))],
        out_shape=[jax.ShapeDtypeStruct((2, 4, D, FF_TILE), F32), jax.ShapeDtypeStruct((2, 4, FF_TILE, D), F32)],
        scratch_shapes=[pltpu.VMEM((D, pair), F32)],
        compiler_params=_cp(("arbitrary", "arbitrary"), 56),
    )(ft, a, da, dh2t)


def _out_proj_bwd(dh1, og, osw, wout, tm, partials):
    rows = dh1.shape[0]
    steps = rows // tm
    ns = len(partials)

    def body(dh1_ref, og_ref, os_ref, w_ref, *rest):
        part_refs, rest = rest[:ns], rest[ns:]
        dog_ref, dos_ref, dw_ref = rest[:3]
        land_refs, (send_sems, recv_sems) = rest[3:3 + ns], rest[3 + ns:]
        i = pl.program_id(0)
        start, finish = _sibling_schedule(part_refs, land_refs, send_sems, recv_sems)

        @pl.when(i == 0)
        def _():
            dw_ref[...] = jnp.zeros_like(dw_ref)
            start()

        pl.when(i == steps - 1)(finish)

        dh1 = dh1_ref[...].astype(MXU_DTYPE)
        dog_ref[...] = _mm_nt(dh1, w_ref[0:512, :])
        dos_ref[...] = _mm_nt(dh1, w_ref[512:1024, :])
        for half, ref in enumerate((og_ref, os_ref)):
            dw = _mm_tn(ref[...], dh1)
            for blk in range(4):
                shard = half * 4 + blk
                dw_ref[shard % 2, shard // 2] += dw[blk * 128:(blk + 1) * 128, :]

    row = lambda w: pl.BlockSpec((tm, w), lambda i: (i, 0))
    outs = pl.pallas_call(
        body, name="out_proj_bwd", grid=(steps,),
        in_specs=[row(D), row(512), row(512), pl.BlockSpec((D, D), lambda i: (0, 0))] + [ANY] * ns,
        out_specs=[row(512), row(512), pl.BlockSpec((2, 4, 128, D), lambda i: (0, 0, 0, 0))] + [ANY] * ns,
        out_shape=[jax.ShapeDtypeStruct((rows, 512), F32), jax.ShapeDtypeStruct((rows, 512), F32),
                   jax.ShapeDtypeStruct((2, 4, 128, D), F32)] + _sibling_shapes(partials),
        scratch_shapes=_sibling_sems(ns),
        compiler_params=_cp(("arbitrary",), 48),
    )(dh1, og, osw, wout, *partials)
    return outs[0], outs[1], outs[2], outs[3:]


def _swa_bwd(qr, kr, vr, osw, dos, sinks, jobs):
    rows = qr.shape[0]
    nblk = rows // SWA_BLOCK
    group = _swa_group(nblk)
    steps = nblk // group
    ns = jobs.n

    def body(q_ref, k0, kp, kc, v0, vp, vc, o_ref, do_ref, sink_ref, *rest):
        dq_ref, dk_ref, dv_ref, dsink_ref = rest[ns:ns + 4]
        start, finish = jobs.bind(rest[:ns], rest[ns + 4:2 * ns + 4], rest[2 * ns + 4:])
        step = pl.program_id(0)

        @pl.when(step == 0)
        def _():
            dk_ref[...] = jnp.zeros_like(dk_ref)
            dv_ref[...] = jnp.zeros_like(dv_ref)
            dsink_ref[...] = jnp.zeros_like(dsink_ref)
            start()

        pl.when(step == steps - 1)(finish)
        for g in range(group):
            block(step * group + g, g, q_ref, k0, kp, kc, v0, vp, vc, o_ref, do_ref, sink_ref,
                  dq_ref, dk_ref, dv_ref, dsink_ref)

    def block(n, g, q_ref, k0, kp, kc, v0, vp, vc, o_ref, do_ref, sink_ref, dq_ref, dk_ref, dv_ref, dsink_ref):
        rs = slice(g * SWA_BLOCK, (g + 1) * SWA_BLOCK)
        kall, vall = _swa_keys(k0, kp, kc, g), _swa_keys(v0, vp, vc, g)
        mask = _swa_mask(n)[0:SWA_BLOCK]
        heads = range(SWA_HEADS)
        hs = [slice(h * SWA_HD, (h + 1) * SWA_HD) for h in heads]
        kv = [slice((h // SWA_GROUP) * SWA_HD, (h // SWA_GROUP + 1) * SWA_HD) for h in heads]
        sink = [sink_ref[0, h] for h in heads]
        qh = [q_ref[rs, hs[h]] for h in heads]
        doh = [do_ref[rs, hs[h]] for h in heads]
        s = [jnp.where(mask, _mm_nt(qh[h], kall[:, kv[h]]), NEG) for h in heads]
        dp = [_mm_nt(doh[h], vall[:, kv[h]]) for h in heads]
        delta = [jnp.sum(doh[h] * o_ref[rs, hs[h]].astype(F32), axis=-1, keepdims=True) for h in heads]
        m = [jnp.maximum(jnp.max(s[h], axis=-1, keepdims=True), sink[h]) for h in heads]
        e = [jnp.exp(s[h] - m[h]) for h in heads]
        inv = [1.0 / (jnp.sum(e[h], axis=-1, keepdims=True) + jnp.exp(sink[h] - m[h])) for h in heads]
        p = [e[h] * inv[h] for h in heads]
        ds = [p[h] * (dp[h] - delta[h]) for h in heads]
        dq = [_mm(ds[h], kall[:, kv[h]]) for h in heads]
        dkh = [_mm_tn(ds[h], qh[h]) for h in heads]
        dvh = [_mm_tn(p[h], doh[h]) for h in heads]
        for h in heads:
            dsink = -jnp.sum(jnp.exp(sink[h] - m[h]) * inv[h] * delta[h], axis=0, keepdims=True)
            dsink_ref[h:h + 1, :] += jnp.broadcast_to(dsink, (1, 128))
        dq_ref[rs, :] = jnp.concatenate(dq, axis=1)
        group_sum = lambda parts, kvh: sum(parts[kvh * SWA_GROUP + 1:(kvh + 1) * SWA_GROUP], parts[kvh * SWA_GROUP])
        dk_all = jnp.concatenate([group_sum(dkh, kvh) for kvh in range(SWA_KV)], axis=1)
        dv_all = jnp.concatenate([group_sum(dvh, kvh) for kvh in range(SWA_KV)], axis=1)
        prev0 = pl.multiple_of(jnp.maximum(n - 1, 0) * SWA_BLOCK, SWA_BLOCK)
        cur0 = pl.multiple_of(n * SWA_BLOCK, SWA_BLOCK)
        for ref, val in ((dk_ref, dk_all), (dv_ref, dv_all)):
            ref[0:SWA_BLOCK, :] += val[0:SWA_BLOCK]
            ref[pl.ds(prev0, SWA_BLOCK), :] += val[SWA_BLOCK:2 * SWA_BLOCK]
            ref[pl.ds(cur0, SWA_BLOCK), :] += val[2 * SWA_BLOCK:]

    blk, first, prev = _swa_specs(group)
    whole = pl.BlockSpec((rows, 128), lambda n: (0, 0))
    outs = pl.pallas_call(
        body, name="swa_bwd", grid=(steps,),
        in_specs=[blk(512), first, prev, blk(128), first, prev, blk(128), blk(512), blk(512),
                  pl.BlockSpec(memory_space=pltpu.SMEM)] + [ANY] * ns,
        out_specs=[blk(512), whole, whole, pl.BlockSpec((8, 128), lambda n: (0, 0))] + [ANY] * ns,
        out_shape=[jax.ShapeDtypeStruct((rows, 512), F32), jax.ShapeDtypeStruct((rows, 128), F32),
                   jax.ShapeDtypeStruct((rows, 128), F32), jax.ShapeDtypeStruct((8, 128), F32)] + jobs.out_shapes,
        scratch_shapes=jobs.sems,
        compiler_params=_cp(("arbitrary",), 48),
    )(qr, kr, kr, kr, vr, vr, vr, osw, dos, sinks, *jobs.inputs)
    return outs[0], outs[1], outs[2], outs[3], jobs.split(outs[4:])


def _gla_bwd(proj, oraw, states, dog, wg_p, bg, gnw, jobs):
    rows = proj.shape[0]
    nc = rows // GLA_CHUNK
    group = _gla_group(nc)
    steps, nrows = nc // group, group * GLA_CHUNK
    ns = jobs.n

    def body(q_ref, k_ref, v_ref, r_ref, lr_ref, oraw_ref, st_ref, dog_ref, wg_ref, bg_ref, gnw_ref, *rest):
        dq_ref, dk_ref, dv_ref, dr_ref, dlr_ref, dwg_ref, dbg_ref, dgnw_ref = rest[ns:ns + 8]
        dstate, db_scr = rest[2 * ns + 8:2 * ns + 10]
        start, finish = jobs.bind(rest[:ns], rest[ns + 8:2 * ns + 8], rest[2 * ns + 10:])
        t = pl.program_id(0)
        c = steps - 1 - t

        @pl.when(t == 0)
        def _():
            dstate[...] = jnp.zeros_like(dstate)
            dwg_ref[...] = jnp.zeros_like(dwg_ref)
            dbg_ref[...] = jnp.zeros_like(dbg_ref)
            dgnw_ref[...] = jnp.zeros_like(dgnw_ref)
            start()

        pl.when(t == steps - 1)(finish)

        lr, wg = lr_ref[...], wg_ref[...]
        zg, live, _, upper, b = _gla_gates(lr, wg, bg_ref[...], c * nrows, nrows)
        eb, enb = jnp.exp(b), jnp.exp(-b)
        scale = GLA_DK ** -0.5
        gq = q_ref[...] * scale * eb
        gk = k_ref[...] * enb
        v = v_ref[...]
        gnw_v = gnw_ref[...]
        tril = _tril64()
        is_last = lax.broadcasted_iota(jnp.int32, (GLA_CHUNK, 1), 0) == GLA_CHUNK - 1
        dgnw = jnp.zeros((1, GLA_DV), F32)
        pairs = [(h, gi) for h in range(GLA_HEADS) for gi in range(group)]
        rs = {gi: slice(gi * GLA_CHUNK, (gi + 1) * GLA_CHUNK) for gi in range(group)}
        s64 = {h: slice(h * GLA_DK, (h + 1) * GLA_DK) for h in range(GLA_HEADS)}
        s128 = {h: slice(h * GLA_DV, (h + 1) * GLA_DV) for h in range(GLA_HEADS)}
        qh = {(h, gi): gq[rs[gi], s64[h]] for h, gi in pairs}
        kh = {(h, gi): gk[rs[gi], s64[h]] for h, gi in pairs}
        vh = {(h, gi): v[rs[gi], s128[h]] for h, gi in pairs}
        ebl = {(h, gi): eb[(gi + 1) * GLA_CHUNK - 1:(gi + 1) * GLA_CHUNK, s64[h]] for h, gi in pairs}
        kl = {pr: kh[pr] * ebl[pr] for pr in pairs}
        st = {(h, gi): st_ref[gi, h] for h, gi in pairs}
        do = {}
        for h, gi in pairs:
            o, rh, dout = oraw_ref[rs[gi], s128[h]], r_ref[rs[gi], s128[h]], dog_ref[rs[gi], s128[h]]
            rstd = lax.rsqrt(jnp.mean(o * o, axis=-1, keepdims=True) + EPS)
            on = o * rstd
            sg = _sigmoid(rh)
            dr_ref[rs[gi], s128[h]] = (dout * (on * gnw_v) * (sg * (1.0 + rh * (1.0 - sg)))).astype(ACT_DTYPE)
            dy = dout * (rh * sg)
            dgnw = dgnw + jnp.sum(dy * on, axis=0, keepdims=True)
            don = dy * gnw_v
            do[h, gi] = rstd * (don - on * jnp.mean(don * on, axis=-1, keepdims=True))
        a = {pr: jnp.where(tril, _mm_nt(qh[pr], kh[pr]), 0.0) for pr in pairs}
        da = {pr: jnp.where(tril, _mm_nt(do[pr], vh[pr]), 0.0) for pr in pairs}
        dinc = {pr: _mm_tn(do[pr], qh[pr]) for pr in pairs}
        dgq = {pr: _mm(da[pr], kh[pr]) + _mm(do[pr], st[pr]) for pr in pairs}
        dgk = {pr: _mm_tn(da[pr], qh[pr]) for pr in pairs}
        dv_a = {pr: _mm_tn(a[pr], do[pr]) for pr in pairs}
        dsp = {}
        for h in range(GLA_HEADS):
            cur = dstate[h]
            for gi in reversed(range(group)):
                dsp[h, gi] = cur
                cur = cur * ebl[h, gi] + dinc[h, gi]
            dstate[h] = cur
        for h, gi in pairs:
            pr = (h, gi)
            dkl = _mm(vh[pr], dsp[pr])
            dv_ref[rs[gi], s128[h]] = (dv_a[pr] + _mm_nt(kl[pr], dsp[pr])).astype(ACT_DTYPE)
            debl = jnp.sum(dsp[pr] * st[pr], axis=0, keepdims=True)
            dq_ref[rs[gi], s64[h]] = (dgq[pr] * (scale * eb[rs[gi], s64[h]])).astype(ACT_DTYPE)
            dk_ref[rs[gi], s64[h]] = ((dgk[pr] + dkl * ebl[pr]) * enb[rs[gi], s64[h]]).astype(ACT_DTYPE)
            last = debl * ebl[pr] + jnp.sum(dkl * kl[pr], axis=0, keepdims=True)
            db_scr[rs[gi], s64[h]] = (dgq[pr] * qh[pr] - dgk[pr] * kh[pr] - dkl * kl[pr]
                                      + jnp.where(is_last, last, 0.0))
        dg = jnp.dot(upper.astype(F32), db_scr[...], precision=HIGHEST, preferred_element_type=F32)
        dzg = jnp.where(live, dg * _sigmoid(-zg) * (1.0 / GLA_TAU), 0.0)
        dlr_ref[...] = _mm_nt(dzg, wg).astype(ACT_DTYPE)
        dwg_ref[...] += _mm_tn(lr, dzg)
        dbg_ref[...] += jnp.broadcast_to(jnp.sum(dzg, axis=0, keepdims=True), dbg_ref.shape)
        dgnw_ref[...] += jnp.broadcast_to(dgnw, dgnw_ref.shape)

    nb = lambda w, col: pl.BlockSpec((nrows, w), lambda t: (steps - 1 - t, col // w))
    const = lambda shape: pl.BlockSpec(shape, lambda t: (0,) * len(shape))
    outs = pl.pallas_call(
        body, name="gla_bwd", grid=(steps,),
        in_specs=[nb(256, C_GQ), nb(256, C_GK), nb(512, C_GV), nb(512, C_GR), nb(128, C_LR), nb(512, 0),
                  pl.BlockSpec((group, GLA_HEADS, GLA_DV, GLA_DK), lambda t: (steps - 1 - t, 0, 0, 0)), nb(512, 0),
                  const((128, 256)), const((1, 256)), const((1, 128))] + [ANY] * ns,
        out_specs=[nb(256, 0), nb(256, 0), nb(512, 0), nb(512, 0), nb(128, 0),
                   const((128, 256)), const((8, 256)), const((8, 128))] + [ANY] * ns,
        out_shape=[jax.ShapeDtypeStruct((rows, 256), ACT_DTYPE), jax.ShapeDtypeStruct((rows, 256), ACT_DTYPE),
                   jax.ShapeDtypeStruct((rows, 512), ACT_DTYPE), jax.ShapeDtypeStruct((rows, 512), ACT_DTYPE),
                   jax.ShapeDtypeStruct((rows, 128), ACT_DTYPE), jax.ShapeDtypeStruct((128, 256), F32),
                   jax.ShapeDtypeStruct((8, 256), F32), jax.ShapeDtypeStruct((8, 128), F32)] + jobs.out_shapes,
        scratch_shapes=[pltpu.VMEM((GLA_HEADS, GLA_DV, GLA_DK), F32), pltpu.VMEM((nrows, 256), F32)] + jobs.sems,
        compiler_params=_cp(("arbitrary",)),
    )(proj, proj, proj, proj, proj, oraw, states, dog, wg_p, bg, gnw, *jobs.inputs)
    return outs[:8], jobs.split(outs[8:])


def _in_proj_bwd(x, lead, dh1, nw, win_p, dgv, dgr, dsq, dgq, dgk, dsk, dsv, dlr, tabs, tm):
    seq = x.shape[0]
    rows = LEAD + seq
    nb = tm // LEAD
    steps = rows // tm

    def first_copy(scr, gx_ref, sem):
        return pltpu.make_async_copy(scr.at[pl.ds(LEAD, tm - LEAD)], gx_ref.at[pl.ds(0, tm - LEAD)], sem)

    def tile_copy(scr, gx_ref, sem, step):
        start = pl.multiple_of(jnp.maximum(step * tm - LEAD, 0), LEAD)
        return pltpu.make_async_copy(scr, gx_ref.at[pl.ds(start, tm)], sem)

    def body(*refs):
        x_refs, refs = refs[:nb], refs[nb:]
        (lead_ref, dh1_ref, nw_ref, w_ref, dgv_ref, dgr_ref, dsq_ref, dgq_ref, dgk_ref, dsk_ref, dsv_ref, dlr_ref,
         c_ref, sa_ref, sb_ref, gx_ref, dlead_ref, dproj_ref, ut_ref, gnm_ref, scr, sem) = refs
        i = pl.program_id(0)

        @pl.when(i == 0)
        def _():
            gnm_ref[...] = jnp.zeros_like(gnm_ref)

        cos, sa, sb = c_ref[...], sa_ref[...], sb_ref[...]
        dsq_v = (_unrope(dsq_ref[...], cos, sa, sb) * (SWA_HD ** -0.5)).astype(MXU_DTYPE)
        dsk_v = _unrope(dsk_ref[...], cos, sa, sb).astype(MXU_DTYPE)
        dproj = jnp.concatenate(
            [dgv_ref[...].astype(MXU_DTYPE), dgr_ref[...].astype(MXU_DTYPE), dgq_ref[...].astype(MXU_DTYPE),
             dgk_ref[...].astype(MXU_DTYPE), dlr_ref[...].astype(MXU_DTYPE), dsq_v, dsk_v,
             dsv_ref[...].astype(MXU_DTYPE)],
            axis=1)
        dproj_ref[...] = dproj
        h = _h_tile(i, lead_ref, x_refs)
        rstd = lax.rsqrt(jnp.mean(h * h, axis=-1, keepdims=True) + EPS)
        hn = h * rstd
        nw_v = nw_ref[...]
        ut_ref[...] = (hn * nw_v).T.astype(ACT_DTYPE)
        du = _mm_nt(dproj, w_ref[...])
        gnm_ref[...] += jnp.broadcast_to(jnp.sum(du * hn, axis=0, keepdims=True), gnm_ref.shape)
        dun = du * nw_v
        dh0 = dh1_ref[...] + rstd * (dun - hn * jnp.mean(dun * hn, axis=-1, keepdims=True))

        if tm > LEAD:
            pl.when(i == 1)(lambda: first_copy(scr, gx_ref, sem).wait())
        pl.when(i > 1)(lambda: tile_copy(scr, gx_ref, sem, i).wait())
        scr[...] = dh0

        @pl.when(i == 0)
        def _():
            dlead_ref[...] = dh0[0:LEAD]
            if tm > LEAD:
                first_copy(scr, gx_ref, sem).start()
                if steps == 1:
                    first_copy(scr, gx_ref, sem).wait()

        @pl.when(i > 0)
        def _():
            tile_copy(scr, gx_ref, sem, i).start()

        if steps > 1:
            pl.when(i == steps - 1)(lambda: tile_copy(scr, gx_ref, sem, i).wait())

    row = lambda w: pl.BlockSpec((tm, w), lambda i: (i, 0))
    const = lambda shape: pl.BlockSpec(shape, lambda i: (0,) * len(shape))
    return pl.pallas_call(
        body, name="in_proj_bwd", grid=(steps,),
        in_specs=_token_specs(tm) + [const((LEAD, D)), row(D), const((1, D)), const((D, DINP)),
                                     row(512), row(512), row(512), row(256), row(256), row(128), row(128), row(128),
                                     row(128), row(128), row(128)],
        out_specs=[ANY, const((LEAD, D)), row(DINP), pl.BlockSpec((D, tm), lambda i: (0, i)), const((8, D))],
        out_shape=[jax.ShapeDtypeStruct((seq, D), F32), jax.ShapeDtypeStruct((LEAD, D), F32),
                   jax.ShapeDtypeStruct((rows, DINP), ACT_DTYPE), jax.ShapeDtypeStruct((D, rows), ACT_DTYPE),
                   jax.ShapeDtypeStruct((8, D), F32)],
        scratch_shapes=[pltpu.VMEM((tm, D), F32), pltpu.SemaphoreType.DMA],
        compiler_params=_cp(("arbitrary",), 56),
    )(*([x] * nb), lead, dh1, nw, win_p, dgv, dgr, dsq, dgq, dgk, dsk, dsv, dlr, *tabs)


def _win_runs():
    groups = [(O_GQ, C_GQ), (O_GK, C_GK), (O_GV, C_GV), (O_GR, C_GR), (O_LR, C_LR), (O_SQ, C_SQ), (O_SK, C_SK),
              (O_SV, C_SV)]
    per = DIN // N_DEV
    runs = []
    for (o0, o1), c0 in groups:
        o = o0
        while o < o1:
            d = o // per
            end = min(o1, (d + 1) * per)
            runs.append((d, o - d * per, c0 + o - o0, end - o))
            o = end
    return runs


def _win_padded(g_in):
    tr = 128

    def body(g_ref, o_ref):
        o_ref[...] = jnp.zeros_like(o_ref)
        for d, s, c, w in _win_runs():
            o_ref[:, c:c + w] = g_ref[d, :, s:s + w]

    return pl.pallas_call(
        body, name="w_in_layout", grid=(D // tr,),
        in_specs=[pl.BlockSpec((N_DEV, tr, DIN // N_DEV), lambda i: (0, i, 0))],
        out_specs=pl.BlockSpec((tr, DINP), lambda i: (i, 0)),
        out_shape=jax.ShapeDtypeStruct((D, DINP), g_in.dtype),
        compiler_params=_cp(("arbitrary",)),
    )(g_in)


def _in_proj_bwd_weights(ut, dproj, tm, small):
    rows = dproj.shape[0]
    steps = rows // tm
    per = DIN // N_DEV

    def body(ut_ref, dp_ref, *rest):
        small_refs, (out_ref, total_ref, acc, stage, sems), sum_scratch = rest[:9], rest[9:14], rest[14:]
        i = pl.program_id(0)
        start, finish = _small_sum_schedule(small_refs, total_ref, *sum_scratch)

        @pl.when(i == 0)
        def _():
            acc[...] = jnp.zeros_like(acc)
            start()

        acc[...] += _mm(ut_ref[...], dp_ref[...])
        pl.when(i == steps - 1)(finish)

        @pl.when(i == steps - 1)
        def _():
            copies = []
            for d in range(N_DEV):
                slot = d % 2
                if d >= 2:
                    copies[d - 2].wait()
                for owner, s, c, w in _win_runs():
                    if owner == d:
                        stage[slot, :, s:s + w] = acc[:, c:c + w]
                cp = pltpu.make_async_copy(stage.at[slot], out_ref.at[d % 2, d // 2], sems.at[slot])
                cp.start()
                copies.append(cp)
            copies[N_DEV - 2].wait()
            copies[N_DEV - 1].wait()

    return pl.pallas_call(
        body, name="in_proj_bwd_weights", grid=(steps,),
        in_specs=[pl.BlockSpec((D, tm), lambda i: (0, i)), pl.BlockSpec((tm, DINP), lambda i: (i, 0))] + SMALL_SPECS,
        out_specs=[ANY, pl.BlockSpec((SMALL_ROWS, D), lambda i: (0, 0))],
        out_shape=[jax.ShapeDtypeStruct((2, 4, D, per), F32), jax.ShapeDtypeStruct((SMALL_ROWS, D), F32)],
        scratch_shapes=[pltpu.VMEM((D, DINP), F32), pltpu.VMEM((2, D, per), F32), pltpu.SemaphoreType.DMA((2,))]
        + _small_sum_scratch(),
        compiler_params=_cp(("arbitrary",), 56),
    )(ut, dproj, *small)


def _adamw(w, g, m, v):
    m = ADAM_B1 * m + (1.0 - ADAM_B1) * g
    v = ADAM_B2 * v + (1.0 - ADAM_B2) * jnp.square(g)
    m_hat = m / (1.0 - ADAM_B1 ** ADAM_STEP)
    v_hat = v / (1.0 - ADAM_B2 ** ADAM_STEP)
    delta = -ADAM_LR * (m_hat / (jnp.sqrt(v_hat) + ADAM_EPS) + ADAM_WD * w)
    return delta, m, v


ADAM_STEPS = 8


def _adamw_shards(where, items, name, jobs=None):
    jobs = jobs or _Jobs([])
    ns, nw = jobs.n, len(items)

    def body(where_ref, *rest):
        ins, rest = rest[:5 * nw], rest[5 * nw:]
        job_ins, rest = rest[:ns], rest[ns:]
        outs, rest = rest[:4 * nw], rest[4 * nw:]
        start, finish = jobs.bind(job_ins, rest[:ns], rest[ns:])
        i = pl.program_id(0)
        pl.when(i == 0)(start)
        pl.when(i == ADAM_STEPS - 1)(finish)
        for k in range(nw):
            p_ref, own_ref, w_ref, m_ref, v_ref = ins[5 * k:5 * k + 5]
            g_ref, d_ref, nm_ref, nv_ref = outs[4 * k:4 * k + 4]
            g = ((p_ref[0].astype(F32) + p_ref[1].astype(F32)) + p_ref[2].astype(F32)) + own_ref[...]
            g_ref[...] = g
            d_ref[...], nm_ref[...], nv_ref[...] = _adamw(w_ref[...], g, m_ref[...], v_ref[...])

    in_specs, out_specs, out_shape, operands = [], [], [], []
    for parts, own, w, m, v in items:
        r, cdim = w.shape
        tr = r // ADAM_STEPS
        spec = pl.BlockSpec((tr, cdim), lambda i, s: (i, 0))
        in_specs += [pl.BlockSpec((3, tr, cdim), lambda i, s: (0, i, 0)),
                     pl.BlockSpec((None, tr, cdim), lambda i, s: (s[1], i, 0)), spec, spec, spec]
        out_specs += [spec] * 4
        out_shape += [jax.ShapeDtypeStruct((r, cdim), F32)] * 4
        operands += [parts, own, w, m, v]
    outs = pl.pallas_call(
        body, name=name,
        grid_spec=pltpu.PrefetchScalarGridSpec(
            num_scalar_prefetch=1, grid=(ADAM_STEPS,),
            in_specs=in_specs + [ANY] * ns, out_specs=out_specs + [ANY] * ns, scratch_shapes=jobs.sems),
        out_shape=out_shape + jobs.out_shapes,
        compiler_params=_cp(("arbitrary",)),
    )(where, *operands, *jobs.inputs)
    return [outs[4 * k:4 * k + 4] for k in range(nw)], jobs.split(outs[4 * nw:])


def _adamw_small(items):
    n = len(items)

    def body(*refs):
        ins, outs = refs[:4 * n], refs[4 * n:]
        for k in range(n):
            w_ref, g_ref, m_ref, v_ref = ins[4 * k:4 * k + 4]
            d_ref, nm_ref, nv_ref = outs[3 * k:3 * k + 3]
            d_ref[...], nm_ref[...], nv_ref[...] = _adamw(w_ref[...], g_ref[...], m_ref[...], v_ref[...])

    vm = pl.BlockSpec(memory_space=pltpu.VMEM)
    shapes = [jax.ShapeDtypeStruct(w.shape, F32) for w, _, _, _ in items for _ in range(3)]
    outs = pl.pallas_call(body, name="adamw_small", in_specs=[vm] * (4 * n), out_specs=[vm] * (3 * n),
                          out_shape=shapes)(*[t for item in items for t in item])
    return [outs[3 * k:3 * k + 3] for k in range(n)]


def _add_own_half(where, full, theirs, name, wire_copy=False):
    _, _, r, cdim = full.shape
    tr = 128 if r % 128 == 0 else r

    def body(where_ref, a_ref, b_ref, *o_refs):
        total = a_ref[...] + b_ref[...]
        o_refs[0][...] = total
        if wire_copy:
            o_refs[1][...] = total.astype(WIRE_DTYPE)

    spec = pl.BlockSpec((4, tr, cdim), lambda i, s: (0, i, 0))
    shapes = [jax.ShapeDtypeStruct(theirs.shape, F32)] + ([jax.ShapeDtypeStruct(theirs.shape, WIRE_DTYPE)] if wire_copy else [])
    outs = pl.pallas_call(
        body, name=name,
        grid_spec=pltpu.PrefetchScalarGridSpec(
            num_scalar_prefetch=1, grid=(r // tr,),
            in_specs=[pl.BlockSpec((None, 4, tr, cdim), lambda i, s: (s[0], 0, i, 0)), spec],
            out_specs=[spec] * len(shapes)),
        out_shape=shapes, compiler_params=_cp(("arbitrary",)))(where, full, theirs)
    return outs if wire_copy else outs[0]


def kernel(x, meta_tokens, norm_mix_w, w_in, w_gate_up, b_gate, gla_norm_w, sinks, w_out, norm_ff_w, w_ff1, w_ff2, final_norm_w, loss_target, m_meta_tokens, m_norm_mix_w, m_w_in, m_w_gate_up, m_b_gate, m_gla_norm_w, m_sinks, m_w_out, m_norm_ff_w, m_w_ff1, m_w_ff2, m_final_norm_w, v_meta_tokens, v_norm_mix_w, v_w_in, v_w_gate_up, v_b_gate, v_gla_norm_w, v_sinks, v_w_out, v_norm_ff_w, v_w_ff1, v_w_ff2, v_final_norm_w):
    seq = x.shape[1]
    rows = LEAD + seq
    tm = _row_tile(rows)
    tm_wide = 1664 if rows % 1664 == 0 else tm
    dev =4 * lax.axis_index("x") + 2 * lax.axis_index("y") + lax.axis_index("c")

    small_shard = jnp.concatenate([meta_tokens, w_gate_up[0], jnp.zeros((N_META, 96), F32)], axis=1)
    g_in, g_small = _all_gather([w_in[0].astype(WIRE_DTYPE), small_shard])
    later_shards = [w_out[0].astype(WIRE_DTYPE), w_ff1[0].astype(WIRE_DTYPE), w_ff2[0].astype(WIRE_DTYPE)]
    win_p = _win_padded(g_in)
    meta_full = jnp.transpose(g_small[:, :, 0:128], (1, 0, 2)).reshape(N_META, D)
    wg_full = jnp.transpose(g_small[:, :, 128:160], (1, 0, 2)).reshape(GLA_RANK, GLA_HEADS * GLA_DK)
    wg_p = jnp.concatenate([wg_full, jnp.zeros((128 - GLA_RANK, 256), F32)], axis=0)

    lead = jnp.concatenate([jnp.zeros((META0, D), F32), meta_full], axis=0)
    tabs = _rope_tables(rows)
    proj, qr, kr, vr = _in_proj(x[0], lead, norm_mix_w, win_p, tabs, tm)
    oraw, og, states, (g_out, g_w1) = _gla_fwd(proj, wg_p, b_gate, gla_norm_w, later_shards[0:2])
    osw, (g_w2,) = _swa_fwd(qr, kr, vr, sinks, later_shards[2:3])
    wout_full = g_out.reshape(D, D)
    w2_full = g_w2.reshape(D_FF, D)
    w1_full = jnp.transpose(g_w1, (1, 0, 2)).reshape(D, D_FF)
    h1, f, ft = _out_proj(x[0], lead, og, osw, wout_full, norm_ff_w, tm)
    a, dh2, dh2t, loss_p, gfn_p = _ffn_fwd(f, h1, w1_full, w2_full, loss_target[0], final_norm_w.reshape(1, D), tm)

    da, dh1, gnf_p = _ffn_bwd_act(dh2, a, w1_full, w2_full, h1, norm_ff_w, tm)
    dw1, dw2 = _ffn_bwd_weights(ft, a, da, dh2t, tm_wide)
    where = jnp.stack([lax.axis_index("c"), 2 * lax.axis_index("x") + lax.axis_index("y")]).astype(jnp.int32)
    dog, dos, dwout, theirs_ffn = _out_proj_bwd(dh1, og, osw, wout_full, tm, [dw1, dw2])
    pairs_ffn = [_add_own_half(where, p, q, "reduce_pair_%d" % (2 + k), wire_copy=True)
                 for k, (p, q) in enumerate(zip([dw1, dw2], theirs_ffn))]
    sums_ffn, wires_ffn = [p[0] for p in pairs_ffn], [p[1] for p in pairs_ffn]
    dsq, dsk, dsv, dsink_p, (parts_ffn, (theirs_wout,)) = _swa_bwd(
        qr, kr, vr, osw, dos, sinks, _Jobs([("chips", wires_ffn), ("sibling", [dwout])]))
    sum_wout, wire_wout = _add_own_half(where, dwout, theirs_wout, "reduce_pair_1", wire_copy=True)
    (dgq, dgk, dgv, dgr, dlr, dwg_p, dbg_p, dgnw_p), ((parts_wout,),) = _gla_bwd(
        proj, oraw, states, dog, wg_p, b_gate, gla_norm_w, _Jobs([("chips", [wire_wout])]))
    grad_x, dlead, dproj, ut, gnm_p = _in_proj_bwd(x[0], lead, dh1, norm_mix_w, win_p, dgv, dgr, dsq, dgq, dgk, dsk,
                                                   dsv, dlr, tabs, tm)
    grad_x = grad_x[None]
    dwin, total = _in_proj_bwd_weights(ut, dproj, tm_wide,
                                       [dlead, dwg_p, gnm_p, gnf_p, gfn_p, dbg_p, dgnw_p, loss_p, dsink_p])

    (theirs_win,) = _rs_sibling([dwin])
    sum_win, sum_win_wire = _add_own_half(where, dwin, theirs_win, "reduce_pair_0", wire_copy=True)

    g_meta = lax.dynamic_slice(total, (R_META, dev * 128), (N_META, 128))
    g_wg = lax.dynamic_slice(total, (R_WG, dev * 32), (GLA_RANK, 32))
    g_norm_mix, g_norm_ff = total[R_NORM_MIX:R_NORM_MIX + 1], total[R_NORM_FF:R_NORM_FF + 1]
    g_final_norm = total[R_FINAL:R_FINAL + 1]
    g_b_gate, g_gla_norm = total[R_B_GATE:R_B_GATE + 1, 0:256], total[R_GLA_NORM:R_GLA_NORM + 1, 0:128]
    g_sinks = total[R_SINKS:R_SINKS + SWA_HEADS, 0].reshape(1, SWA_HEADS)
    loss = total[R_LOSS, 0]

    ((g_wout, d_wout, nm_wout, nv_wout), (g_w1s, d_w1, nm_w1, nv_w1), (g_w2s, d_w2, nm_w2, nv_w2)), ((parts_win,),) = \
        _adamw_shards(where, [(parts_wout, sum_wout, w_out[0], m_w_out[0], v_w_out[0]),
                              (parts_ffn[0], sums_ffn[0], w_ff1[0], m_w_ff1[0], v_w_ff1[0]),
                              (parts_ffn[1], sums_ffn[1], w_ff2[0], m_w_ff2[0], v_w_ff2[0])],
                      "adamw_w_out_ff", _Jobs([("chips", [sum_win_wire])]))
    ((g_win, d_win, nm_win, nv_win),), _ = _adamw_shards(
        where, [(parts_win, sum_win, w_in[0], m_w_in[0], v_w_in[0])], "adamw_w_in")

    names = ["meta", "wg", "norm_mix", "b_gate", "gla_norm", "sinks", "norm_ff", "final_norm"]
    ws = [meta_tokens, w_gate_up, norm_mix_w, b_gate, gla_norm_w, sinks, norm_ff_w, final_norm_w]
    gs = [g_meta, g_wg, g_norm_mix, g_b_gate, g_gla_norm, g_sinks, g_norm_ff, g_final_norm]
    ms = [m_meta_tokens, m_w_gate_up, m_norm_mix_w, m_b_gate, m_gla_norm_w, m_sinks, m_norm_ff_w, m_final_norm_w]
    vs = [v_meta_tokens, v_w_gate_up, v_norm_mix_w, v_b_gate, v_gla_norm_w, v_sinks, v_norm_ff_w, v_final_norm_w]
    flat = lambda t: t.reshape(-1, t.shape[-1])
    small_out = _adamw_small([(flat(w), flat(g), flat(m), flat(v)) for w, g, m, v in zip(ws, gs, ms, vs)])
    d_small = {n: small_out[k][0].reshape(ws[k].shape) for k, n in enumerate(names)}
    nm_small = {n: small_out[k][1].reshape(ws[k].shape) for k, n in enumerate(names)}
    nv_small = {n: small_out[k][2].reshape(ws[k].shape) for k, n in enumerate(names)}
    g_small_d = {n: g.reshape(ws[k].shape) for k, (n, g) in enumerate(zip(names, gs))}

    def ordered(big, small_d):
        win_v, wout_v, w1_v, w2_v = big
        return (small_d["meta"], small_d["norm_mix"], win_v[None], small_d["wg"], small_d["b_gate"],
                small_d["gla_norm"], small_d["sinks"], wout_v[None], small_d["norm_ff"], w1_v[None], w2_v[None],
                small_d["final_norm"])

    return (loss, grad_x,
            *ordered((g_win, g_wout, g_w1s, g_w2s), g_small_d),
            *ordered((d_win, d_wout, d_w1, d_w2), d_small),
            *ordered((nm_win, nm_wout, nm_w1, nm_w2), nm_small),
            *ordered((nv_win, nv_wout, nv_w1, nv_w2), nv_small))
```

```python
import functools

import jax
import jax.numpy as jnp
from jax import lax
from jax.experimental import pallas as pl
from jax.experimental.pallas import tpu as pltpu

F32 = jnp.float32
MXU_DTYPE = jnp.bfloat16
ACT_DTYPE = jnp.bfloat16
WIRE_DTYPE = jnp.bfloat16

D = 1024
N_META = 16
LEAD = 128
META0 = LEAD - N_META
EPS = 1e-5
GLA_HEADS, GLA_DK, GLA_DV, GLA_RANK, GLA_CHUNK = 4, 64, 128, 16, 64
GLA_TAU = 16.0
SWA_HEADS, SWA_KV, SWA_GROUP, SWA_HD, SWA_BLOCK = 8, 2, 4, 64, 128
ROPE_DIM, ROPE_THETA = 16, 500000.0
D_FF = 4096
N_DEV = 8
FF_TILE = D_FF // N_DEV
FF_WIDE = 2048
NEG = -1e30

C_GV, C_GR, C_GQ, C_GK, C_LR, C_SQ, C_SK, C_SV = 0, 512, 1024, 1280, 1536, 1664, 2176, 2304
DGLA = 1664
DINP = 2432
DIN = 2320
O_GQ, O_GK, O_GV, O_GR, O_LR, O_SQ, O_SK, O_SV = (0, 256), (256, 512), (512, 1024), (1024, 1536), (1536, 1552), (1552, 2064), (2064, 2192), (2192, 2320)

ADAM_LR, ADAM_B1, ADAM_B2, ADAM_EPS, ADAM_WD, ADAM_STEP = 0.001, 0.9, 0.999, 1e-08, 0.01, 10

MESH = pl.DeviceIdType.MESH
ANY = pl.BlockSpec(memory_space=pl.ANY)
HIGHEST = lax.Precision.HIGHEST


def _cp(sem=None, vmem_mb=None):
    kw = {}
    if sem is not None:
        kw["dimension_semantics"] = sem
    if vmem_mb is not None:
        kw["vmem_limit_bytes"] = vmem_mb << 20
    return pltpu.CompilerParams(**kw)


def _mm(a, b):
    return jnp.dot(a.astype(MXU_DTYPE), b.astype(MXU_DTYPE), preferred_element_type=F32)


def _mm_nt(a, b):
    return lax.dot_general(a.astype(MXU_DTYPE), b.astype(MXU_DTYPE), (((1,), (1,)), ((), ())),
                           preferred_element_type=F32)


def _mm_tn(a, b):
    return lax.dot_general(a.astype(MXU_DTYPE), b.astype(MXU_DTYPE), (((0,), (0,)), ((), ())),
                           preferred_element_type=F32)


def _logsigmoid(z):
    return jnp.minimum(z, 0.0) - jnp.log(1.0 + jnp.exp(-jnp.abs(z)))


def _sigmoid(z):
    return 1.0 / (1.0 + jnp.exp(-z))


def _row_tile(rows):
    return 640 if rows % 640 == 0 else 128


def _mesh_pos():
    return lax.axis_index("x"), lax.axis_index("y"), lax.axis_index("c")


def _gathered_shapes(shards):
    return [jax.ShapeDtypeStruct((N_DEV,) + s.shape, s.dtype) for s in shards]


def _gather_sems(n):
    return [pltpu.SemaphoreType.DMA((7 * n,)), pltpu.SemaphoreType.DMA((7 * n,))] if n else []


def _place_gather(step, steps, shard_refs, gathered_refs, sems):
    if not shard_refs:
        return
    start, forward, finish = _gather_schedule(shard_refs, gathered_refs, *sems)
    pl.when(step == 0)(start)
    for j, at in enumerate((steps * 7 // 10, steps * 8 // 10, steps * 9 // 10)):
        pl.when(step == at)(functools.partial(forward, j))
    pl.when(step == steps - 1)(finish)


def _with_own_block(gathered, shards):
    dev = 4 * lax.axis_index("x") + 2 * lax.axis_index("y") + lax.axis_index("c")
    return [lax.dynamic_update_index_in_dim(g, s, dev, 0) for g, s in zip(gathered, shards)]


def _gather_schedule(ins, outs, send_sems, recv_sems):
    n = len(ins)
    x, y, c = _mesh_pos()
    me, sibling = (x, y, c), (x, y, 1 - c)
    chips = [(1 - x, y), (x, 1 - y), (1 - x, 1 - y)]

    def copy(a, k, block, to, src=None):
        dst = outs[a].at[4 * block[0] + 2 * block[1] + block[2]]
        return pltpu.make_async_remote_copy(
            src_ref=dst if src is None else src, dst_ref=dst,
            send_sem=send_sems.at[a * 7 + k], recv_sem=recv_sems.at[a * 7 + k],
            device_id=to, device_id_type=MESH)

    def first(a):
        return [copy(a, 0, me, sibling, src=ins[a])] + [copy(a, 1 + j, me, (*chip, c), src=ins[a])
                                                        for j, chip in enumerate(chips)]

    def start():
        for a in range(n):
            for cp in first(a):
                cp.start()

    def forward(j):
        for a in range(n):
            copy(a, 1 + j, (*chips[j], c), me).wait_recv()
            copy(a, 4 + j, (*chips[j], c), sibling).start()

    def finish():
        for a in range(n):
            copy(a, 0, sibling, me).wait_recv()
            for j, chip in enumerate(chips):
                copy(a, 4 + j, (*chip, 1 - c), me).wait_recv()
        for a in range(n):
            for cp in first(a) + [copy(a, 4 + j, (*chip, c), sibling) for j, chip in enumerate(chips)]:
                cp.wait_send()

    return start, forward, finish


def _rs_sibling(gs):
    n = len(gs)

    def body(*refs):
        start, finish = _sibling_schedule(refs[:n], refs[n:2 * n], *refs[2 * n:])
        start()
        finish()

    return pl.pallas_call(
        body, name="reduce_scatter_sibling",
        out_shape=_sibling_shapes(gs), in_specs=[ANY] * n, out_specs=[ANY] * n,
        scratch_shapes=_sibling_sems(n),
    )(*gs)


def _sibling_shapes(gs):
    return [jax.ShapeDtypeStruct(g.shape[1:], g.dtype) for g in gs]


def _sibling_sems(n):
    return [pltpu.SemaphoreType.DMA((n,)), pltpu.SemaphoreType.DMA((n,))]


def _sibling_schedule(ins, land, send_sems, recv_sems):
    x, y, c = _mesh_pos()

    def copies():
        return [pltpu.make_async_remote_copy(
            src_ref=ins[a].at[1 - c], dst_ref=land[a], send_sem=send_sems.at[a], recv_sem=recv_sems.at[a],
            device_id=(x, y, 1 - c), device_id_type=MESH) for a in range(len(ins))]

    def start():
        for cp in copies():
            cp.start()

    def finish():
        for cp in copies():
            cp.wait_recv()
        for cp in copies():
            cp.wait_send()

    return start, finish


def _rs_chips(ps):
    n = len(ps)

    def body(*refs):
        start, finish = _chips_schedule(refs[:n], refs[n:2 * n], *refs[2 * n:])
        start()
        finish()

    return pl.pallas_call(
        body, name="reduce_scatter_chips",
        out_shape=_chips_shapes(ps), in_specs=[ANY] * n, out_specs=[ANY] * n,
        scratch_shapes=_chips_sems(n),
    )(*ps)


def _chips_shapes(ps):
    return [jax.ShapeDtypeStruct((3,) + p.shape[1:], p.dtype) for p in ps]


def _chips_sems(n):
    return [pltpu.SemaphoreType.DMA((3 * n,)), pltpu.SemaphoreType.DMA((3 * n,))]


def _chips_schedule(ins, land, send_sems, recv_sems):
    x, y, c = _mesh_pos()
    chips = [(1 - x, y), (x, 1 - y), (1 - x, 1 - y)]

    def copies():
        return [pltpu.make_async_remote_copy(
            src_ref=ins[a].at[2 * chip[0] + chip[1]], dst_ref=land[a].at[j],
            send_sem=send_sems.at[3 * a + j], recv_sem=recv_sems.at[3 * a + j],
            device_id=(*chip, c), device_id_type=MESH) for a in range(len(ins)) for j, chip in enumerate(chips)]

    def start():
        for cp in copies():
            cp.start()

    def finish():
        for cp in copies():
            cp.wait_recv()
        for cp in copies():
            cp.wait_send()

    return start, finish


class _Jobs:
    def __init__(self, jobs):
        self.jobs = jobs
        self.inputs = [a for _, arrs in jobs for a in arrs]
        self.out_shapes = [s for kind, arrs in jobs
                           for s in (_sibling_shapes(arrs) if kind == "sibling" else _chips_shapes(arrs))]
        self.sems = [s for kind, arrs in jobs
                     for s in (_sibling_sems(len(arrs)) if kind == "sibling" else _chips_sems(len(arrs)))]
        self.n = len(self.inputs)

    def bind(self, in_refs, out_refs, sem_refs):
        starts, finishes, at = [], [], 0
        for k, (kind, arrs) in enumerate(self.jobs):
            schedule = _sibling_schedule if kind == "sibling" else _chips_schedule
            start, finish = schedule(in_refs[at:at + len(arrs)], out_refs[at:at + len(arrs)],
                                     sem_refs[2 * k], sem_refs[2 * k + 1])
            starts.append(start)
            finishes.append(finish)
            at += len(arrs)

        def start_all():
            for f in starts:
                f()

        def finish_all():
            for f in finishes:
                f()

        return start_all, finish_all

    def split(self, outs):
        res, at = [], 0
        for _, arrs in self.jobs:
            res.append(list(outs[at:at + len(arrs)]))
            at += len(arrs)
        return res


R_META, R_WG, R_NORM_MIX, R_NORM_FF, R_FINAL, R_B_GATE, R_GLA_NORM, R_LOSS, R_SINKS, SMALL_ROWS = 0, 16, 32, 33, 34, 35, 36, 37, 40, 48


SMALL_SPECS = [pl.BlockSpec((LEAD, D), lambda i: (0, 0)), pl.BlockSpec((128, 256), lambda i: (0, 0)),
               pl.BlockSpec((8, D), lambda i: (0, 0)), pl.BlockSpec((8, D), lambda i: (0, 0)),
               pl.BlockSpec((8, D), lambda i: (0, 0)), pl.BlockSpec((8, 256), lambda i: (0, 0)),
               pl.BlockSpec((8, 128), lambda i: (0, 0)), pl.BlockSpec((8, 128), lambda i: (0, 0)),
               pl.BlockSpec((8, 128), lambda i: (0, 0))]


def _small_sum_scratch():
    return [pltpu.VMEM((SMALL_ROWS, D), F32), pltpu.VMEM((N_DEV, SMALL_ROWS, D), F32),
            pltpu.SemaphoreType.DMA((7,)), pltpu.SemaphoreType.DMA((7,))]


def _small_sum_schedule(small_refs, out_ref, p_ref, land, send_sems, recv_sems):
    dlead_ref, dwg_ref, gnm_ref, gnf_ref, gfn_ref, dbg_ref, dgnw_ref, loss_ref, dsink_ref = small_refs
    x, y, c = _mesh_pos()
    me = 4 * x + 2 * y + c

    def copies():
        res = []
        for k in range(1, N_DEV):
            bx, by, bc = (k >> 2) & 1, (k >> 1) & 1, k & 1
            peer = (1 - x if bx else x, 1 - y if by else y, 1 - c if bc else c)
            res.append(pltpu.make_async_remote_copy(
                src_ref=p_ref, dst_ref=land.at[me], send_sem=send_sems.at[k - 1], recv_sem=recv_sems.at[k - 1],
                device_id=peer, device_id_type=MESH))
        return res

    def start():
        p_ref[...] = jnp.zeros_like(p_ref)
        p_ref[R_META:R_META + N_META, :] = dlead_ref[META0:LEAD, :]
        p_ref[R_WG:R_WG + GLA_RANK, 0:256] = dwg_ref[0:GLA_RANK, :]
        p_ref[R_NORM_MIX:R_NORM_MIX + 1, :] = gnm_ref[0:1, :]
        p_ref[R_NORM_FF:R_NORM_FF + 1, :] = gnf_ref[0:1, :]
        p_ref[R_FINAL:R_FINAL + 1, :] = gfn_ref[0:1, :]
        p_ref[R_B_GATE:R_B_GATE + 1, 0:256] = dbg_ref[0:1, :]
        p_ref[R_GLA_NORM:R_GLA_NORM + 1, 0:128] = dgnw_ref[0:1, :]
        p_ref[R_LOSS:R_LOSS + 1, 0:128] = loss_ref[0:1, :]
        p_ref[R_SINKS:R_SINKS + SWA_HEADS, 0:128] = dsink_ref[...]
        land[me] = p_ref[...]
        for cp in copies():
            cp.start()

    def finish():
        for cp in copies():
            cp.wait_recv()
        for cp in copies():
            cp.wait_send()
        acc = land[0]
        for d in range(1, N_DEV):
            acc = acc + land[d]
        out_ref[...] = acc

    return start, finish


def _token_specs(tm, grid_rank=1):
    nb = tm // LEAD

    def spec(k):
        if grid_rank == 1:
            return pl.BlockSpec((LEAD, D), lambda i: (jnp.maximum(i * nb + k - 1, 0), 0))
        return pl.BlockSpec((LEAD, D), lambda i, j: (jnp.maximum(i * nb + k - 1, 0), 0))

    return [spec(k) for k in range(nb)]


def _h_tile(i, lead_ref, x_refs):
    first = jnp.where(i == 0, lead_ref[...], x_refs[0][...])
    return jnp.concatenate([first] + [r[...] for r in x_refs[1:]], axis=0)


def _in_norm(x, x_pad, nw, tm, shards):
    rows = LEAD + x.shape[0]
    nb = tm // LEAD
    steps = rows // tm
    ns = len(shards)

    def body(*refs):
        x_refs, refs = refs[:nb], refs[nb:]
        pad_ref, nw_ref = refs[:2]
        shard_refs, (u_ref, ut_ref), gathered, sems = refs[2:2 + ns], refs[2 + ns:4 + ns], refs[4 + ns:4 + 2 * ns], refs[4 + 2 * ns:]
        i = pl.program_id(0)
        _place_gather(i, steps, shard_refs, gathered, sems)
        h = _h_tile(i, pad_ref, x_refs)
        rstd = lax.rsqrt(jnp.mean(h * h, axis=-1, keepdims=True) + EPS)
        u = h * rstd * nw_ref[...]
        u_ref[...] = u.astype(ACT_DTYPE)
        ut_ref[...] = u.T.astype(ACT_DTYPE)

    outs = pl.pallas_call(
        body, name="in_norm", grid=(steps,),
        in_specs=_token_specs(tm) + [pl.BlockSpec((LEAD, D), lambda i: (0, 0)), pl.BlockSpec((1, D), lambda i: (0, 0))]
        + [ANY] * ns,
        out_specs=[pl.BlockSpec((tm, D), lambda i: (i, 0)), pl.BlockSpec((D, tm), lambda i: (0, i))] + [ANY] * ns,
        out_shape=[jax.ShapeDtypeStruct((rows, D), ACT_DTYPE), jax.ShapeDtypeStruct((D, rows), ACT_DTYPE)]
        + _gathered_shapes(shards),
        scratch_shapes=_gather_sems(ns),
        compiler_params=_cp(("arbitrary",)),
    )(*([x] * nb), x_pad, nw, *shards)
    return outs[0], outs[1], _with_own_block(outs[2:], shards)


def _in_proj(u, win_p, tabs, tm):
    rows = u.shape[0]

    def body(u_ref, w_ref, c_ref, sa_ref, sb_ref, o_ref, q_ref, k_ref, v_ref):
        proj = jnp.dot(u_ref[...].astype(MXU_DTYPE), w_ref[...].astype(MXU_DTYPE), preferred_element_type=F32)
        o_ref[...] = proj[:, 0:DGLA]
        cos, sa, sb = c_ref[...], sa_ref[...], sb_ref[...]
        q_ref[...] = (_rope(proj[:, C_SQ:C_SK], cos, sa, sb) * (SWA_HD ** -0.5)).astype(ACT_DTYPE)
        k_ref[...] = _rope(proj[:, C_SK:C_SV], cos, sa, sb).astype(ACT_DTYPE)
        v_ref[...] = proj[:, C_SV:DINP].astype(ACT_DTYPE)

    row = lambda w: pl.BlockSpec((tm, w), lambda i: (i, 0))
    return pl.pallas_call(
        body, name="in_proj", grid=(rows // tm,),
        in_specs=[row(D), pl.BlockSpec((D, DINP), lambda i: (0, 0)), row(128), row(128), row(128)],
        out_specs=[row(DGLA), row(512), row(128), row(128)],
        out_shape=[jax.ShapeDtypeStruct((rows, DGLA), F32), jax.ShapeDtypeStruct((rows, 512), ACT_DTYPE),
                   jax.ShapeDtypeStruct((rows, 128), ACT_DTYPE), jax.ShapeDtypeStruct((rows, 128), ACT_DTYPE)],
        compiler_params=_cp(("arbitrary",), 56),
    )(u, win_p, *tabs)


def _lead_norm(lead, nw):
    def body(l_ref, nw_ref, u_ref, ut_ref):
        h = l_ref[...]
        u = h * lax.rsqrt(jnp.mean(h * h, axis=-1, keepdims=True) + EPS) * nw_ref[...]
        u_ref[...] = u.astype(ACT_DTYPE)
        ut_ref[...] = u.T.astype(ACT_DTYPE)

    vm = pl.BlockSpec(memory_space=pltpu.VMEM)
    return pl.pallas_call(
        body, name="lead_norm", in_specs=[vm, vm], out_specs=[vm, vm],
        out_shape=[jax.ShapeDtypeStruct((LEAD, D), ACT_DTYPE), jax.ShapeDtypeStruct((D, LEAD), ACT_DTYPE)],
    )(lead, nw)


def _rope_tables(rows):
    pos = (jnp.arange(rows, dtype=jnp.int32) - META0).astype(F32)
    inv_freq = 1.0 / (ROPE_THETA ** (jnp.arange(0, ROPE_DIM, 2, dtype=F32) / ROPE_DIM))
    ang = pos[:, None] * jnp.tile(inv_freq, 128 // (ROPE_DIM // 2))[None, :]
    in_head = jnp.arange(128, dtype=jnp.int32)[None, :] % SWA_HD
    cos, sin = jnp.cos(ang), jnp.sin(ang)
    c_tab = jnp.where(in_head < ROPE_DIM, cos, 1.0)
    sa_tab = jnp.where(in_head < ROPE_DIM // 2, -sin, 0.0)
    sb_tab = jnp.where((in_head >= ROPE_DIM // 2) & (in_head < ROPE_DIM), sin, 0.0)
    return c_tab, sa_tab, sb_tab


def _rope(xv, cos, sa, sb):
    width = xv.shape[1]
    reps = width // 128
    if reps > 1:
        cos, sa, sb = (jnp.tile(t, (1, reps)) for t in (cos, sa, sb))
    return xv * cos + pltpu.roll(xv, width - 8, 1) * sa + pltpu.roll(xv, 8, 1) * sb


def _unrope(dy, cos, sa, sb):
    width = dy.shape[1]
    reps = width // 128
    if reps > 1:
        cos, sa, sb = (jnp.tile(t, (1, reps)) for t in (cos, sa, sb))
    return dy * cos + pltpu.roll(dy * sa, 8, 1) + pltpu.roll(dy * sb, width - 8, 1)


def _gla_group(nc):
    for g in (5, 2):
        if nc % g == 0:
            return g
    return 1


def _gla_gates(lr, wg, bg, first_row, nrows):
    zg = _mm(lr, wg) + bg
    row = first_row + lax.broadcasted_iota(jnp.int32, (nrows, 1), 0)
    live = row >= META0
    g = jnp.where(live, _logsigmoid(zg) * (1.0 / GLA_TAU), 0.0)
    ii = lax.broadcasted_iota(jnp.int32, (nrows, nrows), 0)
    jj = lax.broadcasted_iota(jnp.int32, (nrows, nrows), 1)
    same = (ii // GLA_CHUNK) == (jj // GLA_CHUNK)
    lower, upper = same & (jj <= ii), same & (jj >= ii)
    b = jnp.dot(lower.astype(F32), g, precision=HIGHEST, preferred_element_type=F32)
    return zg, live, lower, upper, b


def _tril64():
    ii = lax.broadcasted_iota(jnp.int32, (GLA_CHUNK, GLA_CHUNK), 0)
    jj = lax.broadcasted_iota(jnp.int32, (GLA_CHUNK, GLA_CHUNK), 1)
    return jj <= ii


def _gla_fwd(proj, wg_p, bg, gnw, shards):
    rows = proj.shape[0]
    nc = rows // GLA_CHUNK
    group = _gla_group(nc)
    steps, nrows = nc // group, group * GLA_CHUNK
    ns = len(shards)

    def body(q_ref, k_ref, v_ref, r_ref, lr_ref, wg_ref, bg_ref, gnw_ref, *rest):
        shard_refs, rest = rest[:ns], rest[ns:]
        oraw_ref, og_ref, st_ref = rest[:3]
        gathered_refs, rest = rest[3:3 + ns], rest[3 + ns:]
        state = rest[0]
        c = pl.program_id(0)

        @pl.when(c == 0)
        def _():
            state[...] = jnp.zeros_like(state)

        _place_gather(c, steps, shard_refs, gathered_refs, rest[1:])
        _, _, _, _, b = _gla_gates(lr_ref[...], wg_ref[...], bg_ref[...], c * nrows, nrows)
        eb = jnp.exp(b)
        gq = q_ref[...] * (GLA_DK ** -0.5) * eb
        gk = k_ref[...] * jnp.exp(-b)
        v = v_ref[...]
        gnw_v = gnw_ref[...]
        tril = _tril64()
        pairs = [(h, gi) for h in range(GLA_HEADS) for gi in range(group)]
        rs = {gi: slice(gi * GLA_CHUNK, (gi + 1) * GLA_CHUNK) for gi in range(group)}
        s64 = {h: slice(h * GLA_DK, (h + 1) * GLA_DK) for h in range(GLA_HEADS)}
        s128 = {h: slice(h * GLA_DV, (h + 1) * GLA_DV) for h in range(GLA_HEADS)}
        qh = {(h, gi): gq[rs[gi], s64[h]] for h, gi in pairs}
        kh = {(h, gi): gk[rs[gi], s64[h]] for h, gi in pairs}
        vh = {(h, gi): v[rs[gi], s128[h]] for h, gi in pairs}
        ebl = {(h, gi): eb[(gi + 1) * GLA_CHUNK - 1:(gi + 1) * GLA_CHUNK, s64[h]] for h, gi in pairs}
        av = {pr: _mm(jnp.where(tril, _mm_nt(qh[pr], kh[pr]), 0.0), vh[pr]) for pr in pairs}
        inc = {pr: _mm_tn(vh[pr], kh[pr] * ebl[pr]) for pr in pairs}
        st = {}
        for h in range(GLA_HEADS):
            cur = state[h]
            for gi in range(group):
                st[h, gi] = cur
                st_ref[gi, h] = cur
                cur = cur * ebl[h, gi] + inc[h, gi]
            state[h] = cur
        for h, gi in pairs:
            o = av[h, gi] + _mm_nt(qh[h, gi], st[h, gi])
            oraw_ref[rs[gi], s128[h]] = o
            rstd = lax.rsqrt(jnp.mean(o * o, axis=-1, keepdims=True) + EPS)
            rh = r_ref[rs[gi], s128[h]]
            og_ref[rs[gi], s128[h]] = (o * rstd * gnw_v * (rh * _sigmoid(rh))).astype(ACT_DTYPE)

    nb = lambda w, col: pl.BlockSpec((nrows, w), lambda c: (c, col // w))
    const = lambda shape: pl.BlockSpec(shape, lambda c: (0,) * len(shape))
    outs = pl.pallas_call(
        body, name="gla_fwd", grid=(steps,),
        in_specs=[nb(256, C_GQ), nb(256, C_GK), nb(512, C_GV), nb(512, C_GR), nb(128, C_LR),
                  const((128, 256)), const((1, 256)), const((1, 128))] + [ANY] * ns,
        out_specs=[pl.BlockSpec((nrows, 512), lambda c: (c, 0)), pl.BlockSpec((nrows, 512), lambda c: (c, 0)),
                   pl.BlockSpec((group, GLA_HEADS, GLA_DV, GLA_DK), lambda c: (c, 0, 0, 0))] + [ANY] * ns,
        out_shape=[jax.ShapeDtypeStruct((rows, 512), F32), jax.ShapeDtypeStruct((rows, 512), ACT_DTYPE),
                   jax.ShapeDtypeStruct((nc, GLA_HEADS, GLA_DV, GLA_DK), F32)] + _gathered_shapes(shards),
        scratch_shapes=[pltpu.VMEM((GLA_HEADS, GLA_DV, GLA_DK), F32)] + _gather_sems(ns),
        compiler_params=_cp(("arbitrary",)),
    )(proj, proj, proj, proj, proj, wg_p, bg, gnw, *shards)
    return outs[0], outs[1], outs[2], _with_own_block(outs[3:], shards)


def _swa_mask(n):
    shape = (SWA_GROUP * SWA_BLOCK, 3 * SWA_BLOCK)
    qi = lax.broadcasted_iota(jnp.int32, shape, 0) & (SWA_BLOCK - 1)
    jj = lax.broadcasted_iota(jnp.int32, shape, 1)
    meta = (jj < SWA_BLOCK) & (jj >= META0) & ((n > 0) | (jj <= qi))
    prev = (jj >= SWA_BLOCK) & (jj < 2 * SWA_BLOCK) & (n >= 2) & (jj - SWA_BLOCK > qi)
    cur = (jj >= 2 * SWA_BLOCK) & (n >= 1) & (jj - 2 * SWA_BLOCK <= qi)
    return meta | prev | cur


def _stack_heads(t, kvh):
    return jnp.concatenate([t[:, (kvh * SWA_GROUP + g) * SWA_HD:(kvh * SWA_GROUP + g + 1) * SWA_HD]
                            for g in range(SWA_GROUP)], axis=0)


def _stack_sinks(sink_ref, kvh):
    return jnp.concatenate([jnp.full((SWA_BLOCK, 1), sink_ref[0, kvh * SWA_GROUP + g], F32)
                            for g in range(SWA_GROUP)], axis=0)


def _swa_group(nblk):
    return 5 if nblk % 5 == 0 else 1


def _swa_specs(group):
    blk = lambda w: pl.BlockSpec((group * SWA_BLOCK, w), lambda n: (n, 0))
    first = pl.BlockSpec((SWA_BLOCK, 128), lambda n: (0, 0))
    prev = pl.BlockSpec((SWA_BLOCK, 128), lambda n: (jnp.maximum(n * group - 1, 0), 0))
    return blk, first, prev


def _swa_keys(first_ref, prev_ref, cur_ref, g):
    own = cur_ref[g * SWA_BLOCK:(g + 1) * SWA_BLOCK, :]
    before = prev_ref[...] if g == 0 else cur_ref[(g - 1) * SWA_BLOCK:g * SWA_BLOCK, :]
    return jnp.concatenate([first_ref[...], before, own], axis=0)


def _swa_fwd(qr, kr, vr, sinks, shards):
    rows = qr.shape[0]
    nblk = rows // SWA_BLOCK
    group = _swa_group(nblk)
    steps = nblk // group
    ns = len(shards)

    def body(q_ref, k0, kp, kc, v0, vp, vc, sink_ref, *rest):
        o_ref = rest[ns]
        _place_gather(pl.program_id(0), steps, rest[:ns], rest[ns + 1:2 * ns + 1], rest[2 * ns + 1:])
        for g in range(group):
            n = pl.program_id(0) * group + g
            rs = slice(g * SWA_BLOCK, (g + 1) * SWA_BLOCK)
            kall, vall = _swa_keys(k0, kp, kc, g), _swa_keys(v0, vp, vc, g)
            mask = _swa_mask(n)[0:SWA_BLOCK]
            heads = range(SWA_HEADS)
            hs = [slice(h * SWA_HD, (h + 1) * SWA_HD) for h in heads]
            kv = [slice((h // SWA_GROUP) * SWA_HD, (h // SWA_GROUP + 1) * SWA_HD) for h in heads]
            s = [jnp.where(mask, _mm_nt(q_ref[rs, hs[h]], kall[:, kv[h]]), NEG) for h in heads]
            m = [jnp.maximum(jnp.max(s[h], axis=-1, keepdims=True), sink_ref[0, h]) for h in heads]
            p = [jnp.exp(s[h] - m[h]) for h in heads]
            den = [jnp.sum(p[h], axis=-1, keepdims=True) + jnp.exp(sink_ref[0, h] - m[h]) for h in heads]
            o = [_mm(p[h], vall[:, kv[h]]) for h in heads]
            for h in heads:
                o_ref[rs, hs[h]] = (o[h] / den[h]).astype(ACT_DTYPE)

    blk, first, prev = _swa_specs(group)
    outs = pl.pallas_call(
        body, name="swa_fwd", grid=(steps,),
        in_specs=[blk(512), first, prev, blk(128), first, prev, blk(128),
                  pl.BlockSpec(memory_space=pltpu.SMEM)] + [ANY] * ns,
        out_specs=[blk(512)] + [ANY] * ns,
        out_shape=[jax.ShapeDtypeStruct((rows, 512), ACT_DTYPE)] + _gathered_shapes(shards),
        scratch_shapes=_gather_sems(ns),
        compiler_params=_cp(("arbitrary",)),
    )(qr, kr, kr, kr, vr, vr, vr, sinks, *shards)
    return outs[0], _with_own_block(outs[1:], shards)


def _out_proj(x, lead, og, osw, wout, nfw, tm):
    rows = LEAD + x.shape[0]
    nb = tm // LEAD

    def body(*refs):
        x_refs, (lead_ref, og_ref, os_ref, w_ref, nw_ref, h1_ref, f_ref, ft_ref) = refs[:nb], refs[nb:]
        h0 = _h_tile(pl.program_id(0), lead_ref, x_refs)
        h1 = h0 + _mm(og_ref[...], w_ref[0:512, :]) + _mm(os_ref[...], w_ref[512:1024, :])
        h1_ref[...] = h1
        rstd = lax.rsqrt(jnp.mean(h1 * h1, axis=-1, keepdims=True) + EPS)
        f = h1 * rstd * nw_ref[...]
        f_ref[...] = f.astype(ACT_DTYPE)
        ft_ref[...] = f.T.astype(ACT_DTYPE)

    row = lambda w: pl.BlockSpec((tm, w), lambda i: (i, 0))
    return pl.pallas_call(
        body, name="out_proj", grid=(rows // tm,),
        in_specs=_token_specs(tm) + [pl.BlockSpec((LEAD, D), lambda i: (0, 0)), row(512), row(512),
                                     pl.BlockSpec((D, D), lambda i: (0, 0)), pl.BlockSpec((1, D), lambda i: (0, 0))],
        out_specs=[row(D), row(D), pl.BlockSpec((D, tm), lambda i: (0, i))],
        out_shape=[jax.ShapeDtypeStruct((rows, D), F32), jax.ShapeDtypeStruct((rows, D), ACT_DTYPE),
                   jax.ShapeDtypeStruct((D, rows), ACT_DTYPE)],
        compiler_params=_cp(("arbitrary",), 48),
    )(*([x] * nb), lead, og, osw, wout, nfw)


def _ffn_fwd(f, h1, w1, w2, tgt, fnw, tm):
    rows = f.shape[0]
    nj = D_FF // FF_WIDE
    nb = tm // LEAD

    def body(f_ref, h1_ref, w1_ref, w2_ref, nw_ref, *rest):
        t_refs, (a_ref, dh2_ref, dh2t_ref, loss_ref, gfn_ref, acc) = rest[:nb], rest[nb:]
        i, j = pl.program_id(0), pl.program_id(1)

        @pl.when((i == 0) & (j == 0))
        def _():
            loss_ref[...] = jnp.zeros_like(loss_ref)
            gfn_ref[...] = jnp.zeros_like(gfn_ref)

        @pl.when(j == 0)
        def _():
            acc[...] = jnp.zeros_like(acc)

        a = _mm(f_ref[...], w1_ref[...])
        a_ref[...] = a.astype(ACT_DTYPE)
        z = jnp.square(jnp.maximum(a, 0.0))
        acc[...] += _mm(z, w2_ref[...])

        @pl.when(j == nj - 1)
        def _():
            h2 = h1_ref[...] + acc[...]
            rstd = lax.rsqrt(jnp.mean(h2 * h2, axis=-1, keepdims=True) + EPS)
            hn = h2 * rstd
            nw = nw_ref[...]
            row = i * tm + lax.broadcasted_iota(jnp.int32, (tm, 1), 0)
            target = jnp.concatenate([t[...] for t in t_refs], axis=0)
            err = jnp.where(row >= LEAD, hn * nw - target, 0.0)
            row_loss = jnp.sum(err * err, axis=-1, keepdims=True) * (1.0 / D)
            loss_ref[...] += jnp.broadcast_to(0.5 * jnp.sum(row_loss, axis=0, keepdims=True), loss_ref.shape)
            dy = err * (1.0 / D)
            gfn_ref[...] += jnp.broadcast_to(jnp.sum(dy * hn, axis=0, keepdims=True), gfn_ref.shape)
            dhn = dy * nw
            dh2 = rstd * (dhn - hn * jnp.mean(dhn * hn, axis=-1, keepdims=True))
            dh2_ref[...] = dh2
            dh2t_ref[...] = dh2.T.astype(ACT_DTYPE)

    return pl.pallas_call(
        body, name="ffn_fwd", grid=(rows // tm, nj),
        in_specs=[pl.BlockSpec((tm, D), lambda i, j: (i, 0)), pl.BlockSpec((tm, D), lambda i, j: (i, 0)),
                  pl.BlockSpec((D, FF_WIDE), lambda i, j: (0, j)),
                  pl.BlockSpec((FF_WIDE, D), lambda i, j: (j, 0)),
                  pl.BlockSpec((1, D), lambda i, j: (0, 0))] + _token_specs(tm, grid_rank=2),
        out_specs=[pl.BlockSpec((tm, FF_WIDE), lambda i, j: (i, j)), pl.BlockSpec((tm, D), lambda i, j: (i, 0)),
                   pl.BlockSpec((D, tm), lambda i, j: (0, i)),
                   pl.BlockSpec((8, 128), lambda i, j: (0, 0)), pl.BlockSpec((8, D), lambda i, j: (0, 0))],
        out_shape=[jax.ShapeDtypeStruct((rows, D_FF), ACT_DTYPE), jax.ShapeDtypeStruct((rows, D), F32),
                   jax.ShapeDtypeStruct((D, rows), ACT_DTYPE),
                   jax.ShapeDtypeStruct((8, 128), F32), jax.ShapeDtypeStruct((8, D), F32)],
        scratch_shapes=[pltpu.VMEM((tm, D), F32)],
        compiler_params=_cp(("arbitrary", "arbitrary"), 56),
    )(f, h1, w1, w2, fnw, *([tgt] * nb))


def _ffn_bwd_act(dh2, a, w1, w2, h1, nfw, tm):
    rows = dh2.shape[0]
    nj = D_FF // FF_WIDE

    def body(dh2_ref, a_ref, w1_ref, w2_ref, h1_ref, nw_ref, da_ref, dh1_ref, gnf_ref, acc):
        i, j = pl.program_id(0), pl.program_id(1)

        @pl.when((i == 0) & (j == 0))
        def _():
            gnf_ref[...] = jnp.zeros_like(gnf_ref)

        @pl.when(j == 0)
        def _():
            acc[...] = jnp.zeros_like(acc)

        dz = _mm_nt(dh2_ref[...], w2_ref[...])
        da = dz * (2.0 * jnp.maximum(a_ref[...].astype(F32), 0.0))
        da_ref[...] = da.astype(ACT_DTYPE)
        acc[...] += _mm_nt(da, w1_ref[...])

        @pl.when(j == nj - 1)
        def _():
            h1 = h1_ref[...]
            rstd = lax.rsqrt(jnp.mean(h1 * h1, axis=-1, keepdims=True) + EPS)
            hn = h1 * rstd
            df = acc[...]
            gnf_ref[...] += jnp.broadcast_to(jnp.sum(df * hn, axis=0, keepdims=True), gnf_ref.shape)
            dfn = df * nw_ref[...]
            dh1_ref[...] = dh2_ref[...] + rstd * (dfn - hn * jnp.mean(dfn * hn, axis=-1, keepdims=True))

    return pl.pallas_call(
        body, name="ffn_bwd_act", grid=(rows // tm, nj),
        in_specs=[pl.BlockSpec((tm, D), lambda i, j: (i, 0)), pl.BlockSpec((tm, FF_WIDE), lambda i, j: (i, j)),
                  pl.BlockSpec((D, FF_WIDE), lambda i, j: (0, j)),
                  pl.BlockSpec((FF_WIDE, D), lambda i, j: (j, 0)),
                  pl.BlockSpec((tm, D), lambda i, j: (i, 0)), pl.BlockSpec((1, D), lambda i, j: (0, 0))],
        out_specs=[pl.BlockSpec((tm, FF_WIDE), lambda i, j: (i, j)), pl.BlockSpec((tm, D), lambda i, j: (i, 0)),
                   pl.BlockSpec((8, D), lambda i, j: (0, 0))],
        out_shape=[jax.ShapeDtypeStruct((rows, D_FF), ACT_DTYPE), jax.ShapeDtypeStruct((rows, D), F32),
                   jax.ShapeDtypeStruct((8, D), F32)],
        scratch_shapes=[pltpu.VMEM((tm, D), F32)],
        compiler_params=_cp(("arbitrary", "arbitrary"), 56),
    )(dh2, a, w1, w2, h1, nfw)


def _ffn_bwd_weights(ft, a, da, dh2t, tm):
    rows = a.shape[0]
    steps = rows // tm
    pair = 2 * FF_TILE

    def body(ft_ref, a_ref, da_ref, dh2t_ref, dw1_ref, dw2_ref, dw2t):
        i = pl.program_id(1)

        @pl.when(i == 0)
        def _():
            dw1_ref[...] = jnp.zeros_like(dw1_ref)
            dw2t[...] = jnp.zeros_like(dw2t)

        z = jnp.square(jnp.maximum(a_ref[...].astype(F32), 0.0))
        dw1 = _mm(ft_ref[...], da_ref[...])
        for core in range(2):
            dw1_ref[core] += dw1[:, core * FF_TILE:(core + 1) * FF_TILE]
        dw2t[...] += _mm(dh2t_ref[...], z)

        @pl.when(i == steps - 1)
        def _():
            for core in range(2):
                dw2_ref[core] = dw2t[:, core * FF_TILE:(core + 1) * FF_TILE].T

    return pl.pallas_call(
        body, name="ffn_bwd_weights", grid=(N_DEV // 2, steps),
        in_specs=[pl.BlockSpec((D, tm), lambda j, i: (0, i)), pl.BlockSpec((tm, pair), lambda j, i: (i, j)),
                  pl.BlockSpec((tm, pair), lambda j, i: (i, j)), pl.BlockSpec((D, tm), lambda j, i: (0, i))],
        out_specs=[pl.BlockSpec((2, None, D, FF_TILE), lambda j, i: (0, j, 0, 0)),
                   pl.BlockSpec((2, None, FF_TILE, D), lambda j, i: (0, j, 0, 0))],
        out_shape=[jax.ShapeDtypeStruct((2, 4, D, FF_TILE), F32), jax.ShapeDtypeStruct((2, 4, FF_TILE, D), F32)],
        scratch_shapes=[pltpu.VMEM((D, pair), F32)],
        compiler_params=_cp(("arbitrary", "arbitrary"), 56),
    )(ft, a, da, dh2t)


def _out_proj_bwd(dh1, og, osw, wout, tm, partials):
    rows = dh1.shape[0]
    steps = rows // tm
    ns = len(partials)

    def body(dh1_ref, og_ref, os_ref, w_ref, *rest):
        part_refs, rest = rest[:ns], rest[ns:]
        dog_ref, dos_ref, dw_ref = rest[:3]
        land_refs, (send_sems, recv_sems) = rest[3:3 + ns], rest[3 + ns:]
        i = pl.program_id(0)
        start, finish = _sibling_schedule(part_refs, land_refs, send_sems, recv_sems)

        @pl.when(i == 0)
        def _():
            dw_ref[...] = jnp.zeros_like(dw_ref)
            start()

        pl.when(i == steps - 1)(finish)

        dh1 = dh1_ref[...].astype(MXU_DTYPE)
        dog_ref[...] = _mm_nt(dh1, w_ref[0:512, :])
        dos_ref[...] = _mm_nt(dh1, w_ref[512:1024, :])
        for half, ref in enumerate((og_ref, os_ref)):
            dw = _mm_tn(ref[...], dh1)
            for blk in range(4):
                shard = half * 4 + blk
                dw_ref[shard % 2, shard // 2] += dw[blk * 128:(blk + 1) * 128, :]

    row = lambda w: pl.BlockSpec((tm, w), lambda i: (i, 0))
    outs = pl.pallas_call(
        body, name="out_proj_bwd", grid=(steps,),
        in_specs=[row(D), row(512), row(512), pl.BlockSpec((D, D), lambda i: (0, 0))] + [ANY] * ns,
        out_specs=[row(512), row(512), pl.BlockSpec((2, 4, 128, D), lambda i: (0, 0, 0, 0))] + [ANY] * ns,
        out_shape=[jax.ShapeDtypeStruct((rows, 512), F32), jax.ShapeDtypeStruct((rows, 512), F32),
                   jax.ShapeDtypeStruct((2, 4, 128, D), F32)] + _sibling_shapes(partials),
        scratch_shapes=_sibling_sems(ns),
        compiler_params=_cp(("arbitrary",), 48),
    )(dh1, og, osw, wout, *partials)
    return outs[0], outs[1], outs[2], outs[3:]


def _swa_bwd(qr, kr, vr, osw, dos, sinks, jobs):
    rows = qr.shape[0]
    nblk = rows // SWA_BLOCK
    group = _swa_group(nblk)
    steps = nblk // group
    ns = jobs.n

    def body(q_ref, k0, kp, kc, v0, vp, vc, o_ref, do_ref, sink_ref, *rest):
        dq_ref, dk_ref, dv_ref, dsink_ref = rest[ns:ns + 4]
        start, finish = jobs.bind(rest[:ns], rest[ns + 4:2 * ns + 4], rest[2 * ns + 4:])
        step = pl.program_id(0)

        @pl.when(step == 0)
        def _():
            dk_ref[...] = jnp.zeros_like(dk_ref)
            dv_ref[...] = jnp.zeros_like(dv_ref)
            dsink_ref[...] = jnp.zeros_like(dsink_ref)
            start()

        pl.when(step == steps - 1)(finish)
        for g in range(group):
            block(step * group + g, g, q_ref, k0, kp, kc, v0, vp, vc, o_ref, do_ref, sink_ref,
                  dq_ref, dk_ref, dv_ref, dsink_ref)

    def block(n, g, q_ref, k0, kp, kc, v0, vp, vc, o_ref, do_ref, sink_ref, dq_ref, dk_ref, dv_ref, dsink_ref):
        rs = slice(g * SWA_BLOCK, (g + 1) * SWA_BLOCK)
        kall, vall = _swa_keys(k0, kp, kc, g), _swa_keys(v0, vp, vc, g)
        mask = _swa_mask(n)[0:SWA_BLOCK]
        heads = range(SWA_HEADS)
        hs = [slice(h * SWA_HD, (h + 1) * SWA_HD) for h in heads]
        kv = [slice((h // SWA_GROUP) * SWA_HD, (h // SWA_GROUP + 1) * SWA_HD) for h in heads]
        sink = [sink_ref[0, h] for h in heads]
        qh = [q_ref[rs, hs[h]] for h in heads]
        doh = [do_ref[rs, hs[h]] for h in heads]
        s = [jnp.where(mask, _mm_nt(qh[h], kall[:, kv[h]]), NEG) for h in heads]
        dp = [_mm_nt(doh[h], vall[:, kv[h]]) for h in heads]
        delta = [jnp.sum(doh[h] * o_ref[rs, hs[h]].astype(F32), axis=-1, keepdims=True) for h in heads]
        m = [jnp.maximum(jnp.max(s[h], axis=-1, keepdims=True), sink[h]) for h in heads]
        e = [jnp.exp(s[h] - m[h]) for h in heads]
        inv = [1.0 / (jnp.sum(e[h], axis=-1, keepdims=True) + jnp.exp(sink[h] - m[h])) for h in heads]
        p = [e[h] * inv[h] for h in heads]
        ds = [p[h] * (dp[h] - delta[h]) for h in heads]
        dq = [_mm(ds[h], kall[:, kv[h]]) for h in heads]
        dkh = [_mm_tn(ds[h], qh[h]) for h in heads]
        dvh = [_mm_tn(p[h], doh[h]) for h in heads]
        for h in heads:
            dsink = -jnp.sum(jnp.exp(sink[h] - m[h]) * inv[h] * delta[h], axis=0, keepdims=True)
            dsink_ref[h:h + 1, :] += jnp.broadcast_to(dsink, (1, 128))
        dq_ref[rs, :] = jnp.concatenate(dq, axis=1)
        group_sum = lambda parts, kvh: sum(parts[kvh * SWA_GROUP + 1:(kvh + 1) * SWA_GROUP], parts[kvh * SWA_GROUP])
        dk_all = jnp.concatenate([group_sum(dkh, kvh) for kvh in range(SWA_KV)], axis=1)
        dv_all = jnp.concatenate([group_sum(dvh, kvh) for kvh in range(SWA_KV)], axis=1)
        prev0 = pl.multiple_of(jnp.maximum(n - 1, 0) * SWA_BLOCK, SWA_BLOCK)
        cur0 = pl.multiple_of(n * SWA_BLOCK, SWA_BLOCK)
        for ref, val in ((dk_ref, dk_all), (dv_ref, dv_all)):
            ref[0:SWA_BLOCK, :] += val[0:SWA_BLOCK]
            ref[pl.ds(prev0, SWA_BLOCK), :] += val[SWA_BLOCK:2 * SWA_BLOCK]
            ref[pl.ds(cur0, SWA_BLOCK), :] += val[2 * SWA_BLOCK:]

    blk, first, prev = _swa_specs(group)
    whole = pl.BlockSpec((rows, 128), lambda n: (0, 0))
    outs = pl.pallas_call(
        body, name="swa_bwd", grid=(steps,),
        in_specs=[blk(512), first, prev, blk(128), first, prev, blk(128), blk(512), blk(512),
                  pl.BlockSpec(memory_space=pltpu.SMEM)] + [ANY] * ns,
        out_specs=[blk(512), whole, whole, pl.BlockSpec((8, 128), lambda n: (0, 0))] + [ANY] * ns,
        out_shape=[jax.ShapeDtypeStruct((rows, 512), F32), jax.ShapeDtypeStruct((rows, 128), F32),
                   jax.ShapeDtypeStruct((rows, 128), F32), jax.ShapeDtypeStruct((8, 128), F32)] + jobs.out_shapes,
        scratch_shapes=jobs.sems,
        compiler_params=_cp(("arbitrary",), 48),
    )(qr, kr, kr, kr, vr, vr, vr, osw, dos, sinks, *jobs.inputs)
    return outs[0], outs[1], outs[2], outs[3], jobs.split(outs[4:])


def _gla_bwd(proj, oraw, states, dog, wg_p, bg, gnw, jobs):
    rows = proj.shape[0]
    nc = rows // GLA_CHUNK
    group = _gla_group(nc)
    steps, nrows = nc // group, group * GLA_CHUNK
    ns = jobs.n

    def body(q_ref, k_ref, v_ref, r_ref, lr_ref, oraw_ref, st_ref, dog_ref, wg_ref, bg_ref, gnw_ref, *rest):
        dq_ref, dk_ref, dv_ref, dr_ref, dlr_ref, dwg_ref, dbg_ref, dgnw_ref = rest[ns:ns + 8]
        dstate, db_scr = rest[2 * ns + 8:2 * ns + 10]
        start, finish = jobs.bind(rest[:ns], rest[ns + 8:2 * ns + 8], rest[2 * ns + 10:])
        t = pl.program_id(0)
        c = steps - 1 - t

        @pl.when(t == 0)
        def _():
            dstate[...] = jnp.zeros_like(dstate)
            dwg_ref[...] = jnp.zeros_like(dwg_ref)
            dbg_ref[...] = jnp.zeros_like(dbg_ref)
            dgnw_ref[...] = jnp.zeros_like(dgnw_ref)
            start()

        pl.when(t == steps - 1)(finish)

        lr, wg = lr_ref[...], wg_ref[...]
        zg, live, _, upper, b = _gla_gates(lr, wg, bg_ref[...], c * nrows, nrows)
        eb, enb = jnp.exp(b), jnp.exp(-b)
        scale = GLA_DK ** -0.5
        gq = q_ref[...] * scale * eb
        gk = k_ref[...] * enb
        v = v_ref[...]
        gnw_v = gnw_ref[...]
        tril = _tril64()
        is_last = lax.broadcasted_iota(jnp.int32, (GLA_CHUNK, 1), 0) == GLA_CHUNK - 1
        dgnw = jnp.zeros((1, GLA_DV), F32)
        pairs = [(h, gi) for h in range(GLA_HEADS) for gi in range(group)]
        rs = {gi: slice(gi * GLA_CHUNK, (gi + 1) * GLA_CHUNK) for gi in range(group)}
        s64 = {h: slice(h * GLA_DK, (h + 1) * GLA_DK) for h in range(GLA_HEADS)}
        s128 = {h: slice(h * GLA_DV, (h + 1) * GLA_DV) for h in range(GLA_HEADS)}
        qh = {(h, gi): gq[rs[gi], s64[h]] for h, gi in pairs}
        kh = {(h, gi): gk[rs[gi], s64[h]] for h, gi in pairs}
        vh = {(h, gi): v[rs[gi], s128[h]] for h, gi in pairs}
        ebl = {(h, gi): eb[(gi + 1) * GLA_CHUNK - 1:(gi + 1) * GLA_CHUNK, s64[h]] for h, gi in pairs}
        kl = {pr: kh[pr] * ebl[pr] for pr in pairs}
        st = {(h, gi): st_ref[gi, h] for h, gi in pairs}
        do = {}
        for h, gi in pairs:
            o, rh, dout = oraw_ref[rs[gi], s128[h]], r_ref[rs[gi], s128[h]], dog_ref[rs[gi], s128[h]]
            rstd = lax.rsqrt(jnp.mean(o * o, axis=-1, keepdims=True) + EPS)
            on = o * rstd
            sg = _sigmoid(rh)
            dr_ref[rs[gi], s128[h]] = (dout * (on * gnw_v) * (sg * (1.0 + rh * (1.0 - sg)))).astype(ACT_DTYPE)
            dy = dout * (rh * sg)
            dgnw = dgnw + jnp.sum(dy * on, axis=0, keepdims=True)
            don = dy * gnw_v
            do[h, gi] = rstd * (don - on * jnp.mean(don * on, axis=-1, keepdims=True))
        a = {pr: jnp.where(tril, _mm_nt(qh[pr], kh[pr]), 0.0) for pr in pairs}
        da = {pr: jnp.where(tril, _mm_nt(do[pr], vh[pr]), 0.0) for pr in pairs}
        dinc = {pr: _mm_tn(do[pr], qh[pr]) for pr in pairs}
        dgq = {pr: _mm(da[pr], kh[pr]) + _mm(do[pr], st[pr]) for pr in pairs}
        dgk = {pr: _mm_tn(da[pr], qh[pr]) for pr in pairs}
        dv_a = {pr: _mm_tn(a[pr], do[pr]) for pr in pairs}
        dsp = {}
        for h in range(GLA_HEADS):
            cur = dstate[h]
            for gi in reversed(range(group)):
                dsp[h, gi] = cur
                cur = cur * ebl[h, gi] + dinc[h, gi]
            dstate[h] = cur
        for h, gi in pairs:
            pr = (h, gi)
            dkl = _mm(vh[pr], dsp[pr])
            dv_ref[rs[gi], s128[h]] = (dv_a[pr] + _mm_nt(kl[pr], dsp[pr])).astype(ACT_DTYPE)
            debl = jnp.sum(dsp[pr] * st[pr], axis=0, keepdims=True)
            dq_ref[rs[gi], s64[h]] = (dgq[pr] * (scale * eb[rs[gi], s64[h]])).astype(ACT_DTYPE)
            dk_ref[rs[gi], s64[h]] = ((dgk[pr] + dkl * ebl[pr]) * enb[rs[gi], s64[h]]).astype(ACT_DTYPE)
            last = debl * ebl[pr] + jnp.sum(dkl * kl[pr], axis=0, keepdims=True)
            db_scr[rs[gi], s64[h]] = (dgq[pr] * qh[pr] - dgk[pr] * kh[pr] - dkl * kl[pr]
                                      + jnp.where(is_last, last, 0.0))
        dg = jnp.dot(upper.astype(F32), db_scr[...], precision=HIGHEST, preferred_element_type=F32)
        dzg = jnp.where(live, dg * _sigmoid(-zg) * (1.0 / GLA_TAU), 0.0)
        dlr_ref[...] = _mm_nt(dzg, wg).astype(ACT_DTYPE)
        dwg_ref[...] += _mm_tn(lr, dzg)
        dbg_ref[...] += jnp.broadcast_to(jnp.sum(dzg, axis=0, keepdims=True), dbg_ref.shape)
        dgnw_ref[...] += jnp.broadcast_to(dgnw, dgnw_ref.shape)

    nb = lambda w, col: pl.BlockSpec((nrows, w), lambda t: (steps - 1 - t, col // w))
    const = lambda shape: pl.BlockSpec(shape, lambda t: (0,) * len(shape))
    outs = pl.pallas_call(
        body, name="gla_bwd", grid=(steps,),
        in_specs=[nb(256, C_GQ), nb(256, C_GK), nb(512, C_GV), nb(512, C_GR), nb(128, C_LR), nb(512, 0),
                  pl.BlockSpec((group, GLA_HEADS, GLA_DV, GLA_DK), lambda t: (steps - 1 - t, 0, 0, 0)), nb(512, 0),
                  const((128, 256)), const((1, 256)), const((1, 128))] + [ANY] * ns,
        out_specs=[nb(256, 0), nb(256, 0), nb(512, 0), nb(512, 0), nb(128, 0),
                   const((128, 256)), const((8, 256)), const((8, 128))] + [ANY] * ns,
        out_shape=[jax.ShapeDtypeStruct((rows, 256), ACT_DTYPE), jax.ShapeDtypeStruct((rows, 256), ACT_DTYPE),
                   jax.ShapeDtypeStruct((rows, 512), ACT_DTYPE), jax.ShapeDtypeStruct((rows, 512), ACT_DTYPE),
                   jax.ShapeDtypeStruct((rows, 128), ACT_DTYPE), jax.ShapeDtypeStruct((128, 256), F32),
                   jax.ShapeDtypeStruct((8, 256), F32), jax.ShapeDtypeStruct((8, 128), F32)] + jobs.out_shapes,
        scratch_shapes=[pltpu.VMEM((GLA_HEADS, GLA_DV, GLA_DK), F32), pltpu.VMEM((nrows, 256), F32)] + jobs.sems,
        compiler_params=_cp(("arbitrary",)),
    )(proj, proj, proj, proj, proj, oraw, states, dog, wg_p, bg, gnw, *jobs.inputs)
    return outs[:8], jobs.split(outs[8:])


def _in_proj_bwd(x, lead, dh1, nw, win_p, dgv, dgr, dsq, dgq, dgk, dsk, dsv, dlr, tabs, tm):
    seq = x.shape[0]
    rows = LEAD + seq
    nb = tm // LEAD
    steps = rows // tm

    def first_copy(scr, gx_ref, sem):
        return pltpu.make_async_copy(scr.at[pl.ds(LEAD, tm - LEAD)], gx_ref.at[pl.ds(0, tm - LEAD)], sem)

    def tile_copy(scr, gx_ref, sem, step):
        start = pl.multiple_of(jnp.maximum(step * tm - LEAD, 0), LEAD)
        return pltpu.make_async_copy(scr, gx_ref.at[pl.ds(start, tm)], sem)

    def body(*refs):
        x_refs, refs = refs[:nb], refs[nb:]
        (lead_ref, dh1_ref, nw_ref, w_ref, dgv_ref, dgr_ref, dsq_ref, dgq_ref, dgk_ref, dsk_ref, dsv_ref, dlr_ref,
         c_ref, sa_ref, sb_ref, gx_ref, dlead_ref, dproj_ref, gnm_ref, scr, sem) = refs
        i = pl.program_id(0)

        @pl.when(i == 0)
        def _():
            gnm_ref[...] = jnp.zeros_like(gnm_ref)

        cos, sa, sb = c_ref[...], sa_ref[...], sb_ref[...]
        dsq_v = (_unrope(dsq_ref[...], cos, sa, sb) * (SWA_HD ** -0.5)).astype(MXU_DTYPE)
        dsk_v = _unrope(dsk_ref[...], cos, sa, sb).astype(MXU_DTYPE)
        dproj = jnp.concatenate(
            [dgv_ref[...].astype(MXU_DTYPE), dgr_ref[...].astype(MXU_DTYPE), dgq_ref[...].astype(MXU_DTYPE),
             dgk_ref[...].astype(MXU_DTYPE), dlr_ref[...].astype(MXU_DTYPE), dsq_v, dsk_v,
             dsv_ref[...].astype(MXU_DTYPE)],
            axis=1)
        dproj_ref[...] = dproj
        h = _h_tile(i, lead_ref, x_refs)
        rstd = lax.rsqrt(jnp.mean(h * h, axis=-1, keepdims=True) + EPS)
        hn = h * rstd
        nw_v = nw_ref[...]
        du = _mm_nt(dproj, w_ref[...])
        gnm_ref[...] += jnp.broadcast_to(jnp.sum(du * hn, axis=0, keepdims=True), gnm_ref.shape)
        dun = du * nw_v
        dh0 = dh1_ref[...] + rstd * (dun - hn * jnp.mean(dun * hn, axis=-1, keepdims=True))

        if tm > LEAD:
            pl.when(i == 1)(lambda: first_copy(scr, gx_ref, sem).wait())
        pl.when(i > 1)(lambda: tile_copy(scr, gx_ref, sem, i).wait())
        scr[...] = dh0

        @pl.when(i == 0)
        def _():
            dlead_ref[...] = dh0[0:LEAD]
            if tm > LEAD:
                first_copy(scr, gx_ref, sem).start()
                if steps == 1:
                    first_copy(scr, gx_ref, sem).wait()

        @pl.when(i > 0)
        def _():
            tile_copy(scr, gx_ref, sem, i).start()

        if steps > 1:
            pl.when(i == steps - 1)(lambda: tile_copy(scr, gx_ref, sem, i).wait())

    row = lambda w: pl.BlockSpec((tm, w), lambda i: (i, 0))
    const = lambda shape: pl.BlockSpec(shape, lambda i: (0,) * len(shape))
    return pl.pallas_call(
        body, name="in_proj_bwd", grid=(steps,),
        in_specs=_token_specs(tm) + [const((LEAD, D)), row(D), const((1, D)), const((D, DINP)),
                                     row(512), row(512), row(512), row(256), row(256), row(128), row(128), row(128),
                                     row(128), row(128), row(128)],
        out_specs=[ANY, const((LEAD, D)), row(DINP), const((8, D))],
        out_shape=[jax.ShapeDtypeStruct((seq, D), F32), jax.ShapeDtypeStruct((LEAD, D), F32),
                   jax.ShapeDtypeStruct((rows, DINP), ACT_DTYPE), jax.ShapeDtypeStruct((8, D), F32)],
        scratch_shapes=[pltpu.VMEM((tm, D), F32), pltpu.SemaphoreType.DMA],
        compiler_params=_cp(("arbitrary",), 56),
    )(*([x] * nb), lead, dh1, nw, win_p, dgv, dgr, dsq, dgq, dgk, dsk, dsv, dlr, *tabs)


def _win_runs():
    groups = [(O_GQ, C_GQ), (O_GK, C_GK), (O_GV, C_GV), (O_GR, C_GR), (O_LR, C_LR), (O_SQ, C_SQ), (O_SK, C_SK),
              (O_SV, C_SV)]
    per = DIN // N_DEV
    runs = []
    for (o0, o1), c0 in groups:
        o = o0
        while o < o1:
            d = o // per
            end = min(o1, (d + 1) * per)
            runs.append((d, o - d * per, c0 + o - o0, end - o))
            o = end
    return runs


def _win_padded(g_in):
    tr = 128

    def body(g_ref, o_ref):
        o_ref[...] = jnp.zeros_like(o_ref)
        for d, s, c, w in _win_runs():
            o_ref[:, c:c + w] = g_ref[d, :, s:s + w]

    return pl.pallas_call(
        body, name="w_in_layout", grid=(D // tr,),
        in_specs=[pl.BlockSpec((N_DEV, tr, DIN // N_DEV), lambda i: (0, i, 0))],
        out_specs=pl.BlockSpec((tr, DINP), lambda i: (i, 0)),
        out_shape=jax.ShapeDtypeStruct((D, DINP), g_in.dtype),
        compiler_params=_cp(("arbitrary",)),
    )(g_in)


def _in_proj_bwd_weights(ut, dproj, tm, small):
    rows = dproj.shape[0]
    steps = rows // tm
    per = DIN // N_DEV

    def body(ut_ref, dp_ref, *rest):
        small_refs, (out_ref, total_ref, acc, stage, sems), sum_scratch = rest[:9], rest[9:14], rest[14:]
        i = pl.program_id(0)
        start, finish = _small_sum_schedule(small_refs, total_ref, *sum_scratch)

        @pl.when(i == 0)
        def _():
            acc[...] = jnp.zeros_like(acc)
            start()

        acc[...] += _mm(ut_ref[...], dp_ref[...])
        pl.when(i == steps - 1)(finish)

        @pl.when(i == steps - 1)
        def _():
            copies = []
            for d in range(N_DEV):
                slot = d % 2
                if d >= 2:
                    copies[d - 2].wait()
                for owner, s, c, w in _win_runs():
                    if owner == d:
                        stage[slot, :, s:s + w] = acc[:, c:c + w]
                cp = pltpu.make_async_copy(stage.at[slot], out_ref.at[d % 2, d // 2], sems.at[slot])
                cp.start()
                copies.append(cp)
            copies[N_DEV - 2].wait()
            copies[N_DEV - 1].wait()

    return pl.pallas_call(
        body, name="in_proj_bwd_weights", grid=(steps,),
        in_specs=[pl.BlockSpec((D, tm), lambda i: (0, i)), pl.BlockSpec((tm, DINP), lambda i: (i, 0))] + SMALL_SPECS,
        out_specs=[ANY, pl.BlockSpec((SMALL_ROWS, D), lambda i: (0, 0))],
        out_shape=[jax.ShapeDtypeStruct((2, 4, D, per), F32), jax.ShapeDtypeStruct((SMALL_ROWS, D), F32)],
        scratch_shapes=[pltpu.VMEM((D, DINP), F32), pltpu.VMEM((2, D, per), F32), pltpu.SemaphoreType.DMA((2,))]
        + _small_sum_scratch(),
        compiler_params=_cp(("arbitrary",), 56),
    )(ut, dproj, *small)


def _adamw(w, g, m, v):
    m = ADAM_B1 * m + (1.0 - ADAM_B1) * g
    v = ADAM_B2 * v + (1.0 - ADAM_B2) * jnp.square(g)
    m_hat = m / (1.0 - ADAM_B1 ** ADAM_STEP)
    v_hat = v / (1.0 - ADAM_B2 ** ADAM_STEP)
    delta = -ADAM_LR * (m_hat / (jnp.sqrt(v_hat) + ADAM_EPS) + ADAM_WD * w)
    return delta, m, v


ADAM_STEPS = 8


def _adamw_shards(where, items, name, jobs=None):
    jobs = jobs or _Jobs([])
    ns, nw = jobs.n, len(items)

    def body(where_ref, *rest):
        ins, rest = rest[:5 * nw], rest[5 * nw:]
        job_ins, rest = rest[:ns], rest[ns:]
        outs, rest = rest[:4 * nw], rest[4 * nw:]
        start, finish = jobs.bind(job_ins, rest[:ns], rest[ns:])
        i = pl.program_id(0)
        pl.when(i == 0)(start)
        pl.when(i == ADAM_STEPS - 1)(finish)
        for k in range(nw):
            p_ref, own_ref, w_ref, m_ref, v_ref = ins[5 * k:5 * k + 5]
            g_ref, d_ref, nm_ref, nv_ref = outs[4 * k:4 * k + 4]
            g = ((p_ref[0].astype(F32) + p_ref[1].astype(F32)) + p_ref[2].astype(F32)) + own_ref[...]
            g_ref[...] = g
            d_ref[...], nm_ref[...], nv_ref[...] = _adamw(w_ref[...], g, m_ref[...], v_ref[...])

    in_specs, out_specs, out_shape, operands = [], [], [], []
    for parts, own, w, m, v in items:
        r, cdim = w.shape
        tr = r // ADAM_STEPS
        spec = pl.BlockSpec((tr, cdim), lambda i, s: (i, 0))
        in_specs += [pl.BlockSpec((3, tr, cdim), lambda i, s: (0, i, 0)),
                     pl.BlockSpec((None, tr, cdim), lambda i, s: (s[1], i, 0)), spec, spec, spec]
        out_specs += [spec] * 4
        out_shape += [jax.ShapeDtypeStruct((r, cdim), F32)] * 4
        operands += [parts, own, w, m, v]
    outs = pl.pallas_call(
        body, name=name,
        grid_spec=pltpu.PrefetchScalarGridSpec(
            num_scalar_prefetch=1, grid=(ADAM_STEPS,),
            in_specs=in_specs + [ANY] * ns, out_specs=out_specs + [ANY] * ns, scratch_shapes=jobs.sems),
        out_shape=out_shape + jobs.out_shapes,
        compiler_params=_cp(("arbitrary",)),
    )(where, *operands, *jobs.inputs)
    return [outs[4 * k:4 * k + 4] for k in range(nw)], jobs.split(outs[4 * nw:])


def _adamw_small(items):
    n = len(items)

    def body(*refs):
        ins, outs = refs[:4 * n], refs[4 * n:]
        for k in range(n):
            w_ref, g_ref, m_ref, v_ref = ins[4 * k:4 * k + 4]
            d_ref, nm_ref, nv_ref = outs[3 * k:3 * k + 3]
            d_ref[...], nm_ref[...], nv_ref[...] = _adamw(w_ref[...], g_ref[...], m_ref[...], v_ref[...])

    vm = pl.BlockSpec(memory_space=pltpu.VMEM)
    shapes = [jax.ShapeDtypeStruct(w.shape, F32) for w, _, _, _ in items for _ in range(3)]
    outs = pl.pallas_call(body, name="adamw_small", in_specs=[vm] * (4 * n), out_specs=[vm] * (3 * n),
                          out_shape=shapes)(*[t for item in items for t in item])
    return [outs[3 * k:3 * k + 3] for k in range(n)]


def _add_own_half(where, full, theirs, name, wire_copy=False):
    _, _, r, cdim = full.shape
    tr = 128 if r % 128 == 0 else r

    def body(where_ref, a_ref, b_ref, *o_refs):
        total = a_ref[...] + b_ref[...]
        o_refs[0][...] = total
        if wire_copy:
            o_refs[1][...] = total.astype(WIRE_DTYPE)

    spec = pl.BlockSpec((4, tr, cdim), lambda i, s: (0, i, 0))
    shapes = [jax.ShapeDtypeStruct(theirs.shape, F32)] + ([jax.ShapeDtypeStruct(theirs.shape, WIRE_DTYPE)] if wire_copy else [])
    outs = pl.pallas_call(
        body, name=name,
        grid_spec=pltpu.PrefetchScalarGridSpec(
            num_scalar_prefetch=1, grid=(r // tr,),
            in_specs=[pl.BlockSpec((None, 4, tr, cdim), lambda i, s: (s[0], 0, i, 0)), spec],
            out_specs=[spec] * len(shapes)),
        out_shape=shapes, compiler_params=_cp(("arbitrary",)))(where, full, theirs)
    return outs if wire_copy else outs[0]


def kernel(x, meta_tokens, norm_mix_w, w_in, w_gate_up, b_gate, gla_norm_w, sinks, w_out, norm_ff_w, w_ff1, w_ff2, final_norm_w, loss_target, m_meta_tokens, m_norm_mix_w, m_w_in, m_w_gate_up, m_b_gate, m_gla_norm_w, m_sinks, m_w_out, m_norm_ff_w, m_w_ff1, m_w_ff2, m_final_norm_w, v_meta_tokens, v_norm_mix_w, v_w_in, v_w_gate_up, v_b_gate, v_gla_norm_w, v_sinks, v_w_out, v_norm_ff_w, v_w_ff1, v_w_ff2, v_final_norm_w):
    seq = x.shape[1]
    rows = LEAD + seq
    tm = _row_tile(rows)
    tm_wide = 1664 if rows % 1664 == 0 else tm
    dev =4 * lax.axis_index("x") + 2 * lax.axis_index("y") + lax.axis_index("c")

    small_shard = jnp.concatenate([meta_tokens, w_gate_up[0], jnp.zeros((N_META, 96), F32)], axis=1)
    u, ut, (g_in, g_small) = _in_norm(x[0], jnp.zeros((LEAD, D), F32), norm_mix_w, tm,
                                      [w_in[0].astype(WIRE_DTYPE), small_shard])
    later_shards = [w_out[0].astype(WIRE_DTYPE), w_ff1[0].astype(WIRE_DTYPE), w_ff2[0].astype(WIRE_DTYPE)]
    win_p = _win_padded(g_in)
    meta_full = jnp.transpose(g_small[:, :, 0:128], (1, 0, 2)).reshape(N_META, D)
    wg_full = jnp.transpose(g_small[:, :, 128:160], (1, 0, 2)).reshape(GLA_RANK, GLA_HEADS * GLA_DK)
    wg_p = jnp.concatenate([wg_full, jnp.zeros((128 - GLA_RANK, 256), F32)], axis=0)

    lead = jnp.concatenate([jnp.zeros((META0, D), F32), meta_full], axis=0)
    tabs = _rope_tables(rows)
    u_lead, ut_lead = _lead_norm(lead, norm_mix_w)
    u = lax.dynamic_update_slice(u, u_lead, (0, 0))
    ut = lax.dynamic_update_slice(ut, ut_lead, (0, 0))
    proj, qr, kr, vr = _in_proj(u, win_p, tabs, tm)
    oraw, og, states, (g_out, g_w1) = _gla_fwd(proj, wg_p, b_gate, gla_norm_w, later_shards[0:2])
    osw, (g_w2,) = _swa_fwd(qr, kr, vr, sinks, later_shards[2:3])
    wout_full = g_out.reshape(D, D)
    w2_full = g_w2.reshape(D_FF, D)
    w1_full = jnp.transpose(g_w1, (1, 0, 2)).reshape(D, D_FF)
    h1, f, ft = _out_proj(x[0], lead, og, osw, wout_full, norm_ff_w, tm)
    a, dh2, dh2t, loss_p, gfn_p = _ffn_fwd(f, h1, w1_full, w2_full, loss_target[0], final_norm_w.reshape(1, D), tm)

    da, dh1, gnf_p = _ffn_bwd_act(dh2, a, w1_full, w2_full, h1, norm_ff_w, tm)
    dw1, dw2 = _ffn_bwd_weights(ft, a, da, dh2t, tm_wide)
    where = jnp.stack([lax.axis_index("c"), 2 * lax.axis_index("x") + lax.axis_index("y")]).astype(jnp.int32)
    dog, dos, dwout, theirs_ffn = _out_proj_bwd(dh1, og, osw, wout_full, tm, [dw1, dw2])
    pairs_ffn = [_add_own_half(where, p, q, "reduce_pair_%d" % (2 + k), wire_copy=True)
                 for k, (p, q) in enumerate(zip([dw1, dw2], theirs_ffn))]
    sums_ffn, wires_ffn = [p[0] for p in pairs_ffn], [p[1] for p in pairs_ffn]
    dsq, dsk, dsv, dsink_p, (parts_ffn, (theirs_wout,)) = _swa_bwd(
        qr, kr, vr, osw, dos, sinks, _Jobs([("chips", wires_ffn), ("sibling", [dwout])]))
    sum_wout, wire_wout = _add_own_half(where, dwout, theirs_wout, "reduce_pair_1", wire_copy=True)
    (dgq, dgk, dgv, dgr, dlr, dwg_p, dbg_p, dgnw_p), ((parts_wout,),) = _gla_bwd(
        proj, oraw, states, dog, wg_p, b_gate, gla_norm_w, _Jobs([("chips", [wire_wout])]))
    grad_x, dlead, dproj, gnm_p = _in_proj_bwd(x[0], lead, dh1, norm_mix_w, win_p, dgv, dgr, dsq, dgq, dgk, dsk,
                                                   dsv, dlr, tabs, tm)
    grad_x = grad_x[None]
    dwin, total = _in_proj_bwd_weights(ut, dproj, tm_wide,
                                       [dlead, dwg_p, gnm_p, gnf_p, gfn_p, dbg_p, dgnw_p, loss_p, dsink_p])

    (theirs_win,) = _rs_sibling([dwin])
    sum_win, sum_win_wire = _add_own_half(where, dwin, theirs_win, "reduce_pair_0", wire_copy=True)

    g_meta = lax.dynamic_slice(total, (R_META, dev * 128), (N_META, 128))
    g_wg = lax.dynamic_slice(total, (R_WG, dev * 32), (GLA_RANK, 32))
    g_norm_mix, g_norm_ff = total[R_NORM_MIX:R_NORM_MIX + 1], total[R_NORM_FF:R_NORM_FF + 1]
    g_final_norm = total[R_FINAL:R_FINAL + 1]
    g_b_gate, g_gla_norm = total[R_B_GATE:R_B_GATE + 1, 0:256], total[R_GLA_NORM:R_GLA_NORM + 1, 0:128]
    g_sinks = total[R_SINKS:R_SINKS + SWA_HEADS, 0].reshape(1, SWA_HEADS)
    loss = total[R_LOSS, 0]

    ((g_wout, d_wout, nm_wout, nv_wout), (g_w1s, d_w1, nm_w1, nv_w1), (g_w2s, d_w2, nm_w2, nv_w2)), ((parts_win,),) = \
        _adamw_shards(where, [(parts_wout, sum_wout, w_out[0], m_w_out[0], v_w_out[0]),
                              (parts_ffn[0], sums_ffn[0], w_ff1[0], m_w_ff1[0], v_w_ff1[0]),
                              (parts_ffn[1], sums_ffn[1], w_ff2[0], m_w_ff2[0], v_w_ff2[0])],
                      "adamw_w_out_ff", _Jobs([("chips", [sum_win_wire])]))
    ((g_win, d_win, nm_win, nv_win),), _ = _adamw_shards(
        where, [(parts_win, sum_win, w_in[0], m_w_in[0], v_w_in[0])], "adamw_w_in")

    names = ["meta", "wg", "norm_mix", "b_gate", "gla_norm", "sinks", "norm_ff", "final_norm"]
    ws = [meta_tokens, w_gate_up, norm_mix_w, b_gate, gla_norm_w, sinks, norm_ff_w, final_norm_w]
    gs = [g_meta, g_wg, g_norm_mix, g_b_gate, g_gla_norm, g_sinks, g_norm_ff, g_final_norm]
    ms = [m_meta_tokens, m_w_gate_up, m_norm_mix_w, m_b_gate, m_gla_norm_w, m_sinks, m_norm_ff_w, m_final_norm_w]
    vs = [v_meta_tokens, v_w_gate_up, v_norm_mix_w, v_b_gate, v_gla_norm_w, v_sinks, v_norm_ff_w, v_final_norm_w]
    flat = lambda t: t.reshape(-1, t.shape[-1])
    small_out = _adamw_small([(flat(w), flat(g), flat(m), flat(v)) for w, g, m, v in zip(ws, gs, ms, vs)])
    d_small = {n: small_out[k][0].reshape(ws[k].shape) for k, n in enumerate(names)}
    nm_small = {n: small_out[k][1].reshape(ws[k].shape) for k, n in enumerate(names)}
    nv_small = {n: small_out[k][2].reshape(ws[k].shape) for k, n in enumerate(names)}
    g_small_d = {n: g.reshape(ws[k].shape) for k, (n, g) in enumerate(zip(names, gs))}

    def ordered(big, small_d):
        win_v, wout_v, w1_v, w2_v = big
        return (small_d["meta"], small_d["norm_mix"], win_v[None], small_d["wg"], small_d["b_gate"],
                small_d["gla_norm"], small_d["sinks"], wout_v[None], small_d["norm_ff"], w1_v[None], w2_v[None],
                small_d["final_norm"])

    return (loss, grad_x,
            *ordered((g_win, g_wout, g_w1s, g_w2s), g_small_d),
            *ordered((d_win, d_wout, d_w1, d_w2), d_small),
            *ordered((nm_win, nm_wout, nm_w1, nm_w2), nm_small),
            *ordered((nv_win, nv_wout, nv_w1, nv_w2), nv_small))
```

```python
import functools

import jax
import jax.numpy as jnp
from jax import lax
from jax.experimental import pallas as pl
from jax.experimental.pallas import tpu as pltpu

F32 = jnp.float32
MXU_DTYPE = jnp.bfloat16
ACT_DTYPE = jnp.bfloat16
WIRE_DTYPE = jnp.bfloat16

D = 1024
N_META = 16
LEAD = 128
META0 = LEAD - N_META
EPS = 1e-5
GLA_HEADS, GLA_DK, GLA_DV, GLA_RANK, GLA_CHUNK = 4, 64, 128, 16, 64
GLA_TAU = 16.0
SWA_HEADS, SWA_KV, SWA_GROUP, SWA_HD, SWA_BLOCK = 8, 2, 4, 64, 128
ROPE_DIM, ROPE_THETA = 16, 500000.0
D_FF = 4096
N_DEV = 8
FF_TILE = D_FF // N_DEV
FF_WIDE = 2048
NEG = -1e30

C_GV, C_GR, C_GQ, C_GK, C_LR, C_SQ, C_SK, C_SV = 0, 512, 1024, 1280, 1536, 1664, 2176, 2304
DGLA = 1664
DINP = 2432
DIN = 2320
O_GQ, O_GK, O_GV, O_GR, O_LR, O_SQ, O_SK, O_SV = (0, 256), (256, 512), (512, 1024), (1024, 1536), (1536, 1552), (1552, 2064), (2064, 2192), (2192, 2320)

ADAM_LR, ADAM_B1, ADAM_B2, ADAM_EPS, ADAM_WD, ADAM_STEP = 0.001, 0.9, 0.999, 1e-08, 0.01, 10

MESH = pl.DeviceIdType.MESH
ANY = pl.BlockSpec(memory_space=pl.ANY)
HIGHEST = lax.Precision.HIGHEST


def _cp(sem=None, vmem_mb=None):
    kw = {}
    if sem is not None:
        kw["dimension_semantics"] = sem
    if vmem_mb is not None:
        kw["vmem_limit_bytes"] = vmem_mb << 20
    return pltpu.CompilerParams(**kw)


def _mm(a, b):
    return jnp.dot(a.astype(MXU_DTYPE), b.astype(MXU_DTYPE), preferred_element_type=F32)


def _mm_nt(a, b):
    return lax.dot_general(a.astype(MXU_DTYPE), b.astype(MXU_DTYPE), (((1,), (1,)), ((), ())),
                           preferred_element_type=F32)


def _mm_tn(a, b):
    return lax.dot_general(a.astype(MXU_DTYPE), b.astype(MXU_DTYPE), (((0,), (0,)), ((), ())),
                           preferred_element_type=F32)


def _logsigmoid(z):
    return jnp.minimum(z, 0.0) - jnp.log(1.0 + jnp.exp(-jnp.abs(z)))


def _sigmoid(z):
    return 1.0 / (1.0 + jnp.exp(-z))


def _resident(shape):
    return pl.BlockSpec(shape, lambda i, j: (0,) * len(shape), pipeline_mode=pl.Buffered(1))


def _row_tile(rows):
    return 640 if rows % 640 == 0 else 128


def _mesh_pos():
    return lax.axis_index("x"), lax.axis_index("y"), lax.axis_index("c")


def _all_gather(shards):
    n = len(shards)

    def body(*refs):
        start, forward, finish = _gather_schedule(refs[:n], refs[n:2 * n], *refs[2 * n:])
        start()
        for j in range(3):
            forward(j)
        finish()

    gathered = pl.pallas_call(
        body, name="all_gather_weights",
        out_shape=_gathered_shapes(shards), in_specs=[ANY] * n, out_specs=[ANY] * n,
        scratch_shapes=_gather_sems(n),
    )(*shards)
    return _with_own_block(gathered, shards)


def _gathered_shapes(shards):
    return [jax.ShapeDtypeStruct((N_DEV,) + s.shape, s.dtype) for s in shards]


def _gather_sems(n):
    return [pltpu.SemaphoreType.DMA((7 * n,)), pltpu.SemaphoreType.DMA((7 * n,))] if n else []


def _place_gather(step, steps, shard_refs, gathered_refs, sems):
    if not shard_refs:
        return
    start, forward, finish = _gather_schedule(shard_refs, gathered_refs, *sems)
    pl.when(step == 0)(start)
    for j, at in enumerate((steps * 7 // 10, steps * 8 // 10, steps * 9 // 10)):
        pl.when(step == at)(functools.partial(forward, j))
    pl.when(step == steps - 1)(finish)


def _with_own_block(gathered, shards):
    dev = 4 * lax.axis_index("x") + 2 * lax.axis_index("y") + lax.axis_index("c")
    return [lax.dynamic_update_index_in_dim(g, s, dev, 0) for g, s in zip(gathered, shards)]


def _gather_schedule(ins, outs, send_sems, recv_sems):
    n = len(ins)
    x, y, c = _mesh_pos()
    me, sibling = (x, y, c), (x, y, 1 - c)
    chips = [(1 - x, y), (x, 1 - y), (1 - x, 1 - y)]

    def copy(a, k, block, to, src=None):
        dst = outs[a].at[4 * block[0] + 2 * block[1] + block[2]]
        return pltpu.make_async_remote_copy(
            src_ref=dst if src is None else src, dst_ref=dst,
            send_sem=send_sems.at[a * 7 + k], recv_sem=recv_sems.at[a * 7 + k],
            device_id=to, device_id_type=MESH)

    def first(a):
        return [copy(a, 0, me, sibling, src=ins[a])] + [copy(a, 1 + j, me, (*chip, c), src=ins[a])
                                                        for j, chip in enumerate(chips)]

    def start():
        for a in range(n):
            for cp in first(a):
                cp.start()

    def forward(j):
        for a in range(n):
            copy(a, 1 + j, (*chips[j], c), me).wait_recv()
            copy(a, 4 + j, (*chips[j], c), sibling).start()

    def finish():
        for a in range(n):
            copy(a, 0, sibling, me).wait_recv()
            for j, chip in enumerate(chips):
                copy(a, 4 + j, (*chip, 1 - c), me).wait_recv()
        for a in range(n):
            for cp in first(a) + [copy(a, 4 + j, (*chip, c), sibling) for j, chip in enumerate(chips)]:
                cp.wait_send()

    return start, forward, finish


def _rs_sibling(gs):
    n = len(gs)

    def body(*refs):
        start, finish = _sibling_schedule(refs[:n], refs[n:2 * n], *refs[2 * n:])
        start()
        finish()

    return pl.pallas_call(
        body, name="reduce_scatter_sibling",
        out_shape=_sibling_shapes(gs), in_specs=[ANY] * n, out_specs=[ANY] * n,
        scratch_shapes=_sibling_sems(n),
    )(*gs)


def _sibling_shapes(gs):
    return [jax.ShapeDtypeStruct(g.shape[1:], g.dtype) for g in gs]


def _sibling_sems(n):
    return [pltpu.SemaphoreType.DMA((n,)), pltpu.SemaphoreType.DMA((n,))]


def _sibling_schedule(ins, land, send_sems, recv_sems):
    x, y, c = _mesh_pos()

    def copies():
        return [pltpu.make_async_remote_copy(
            src_ref=ins[a].at[1 - c], dst_ref=land[a], send_sem=send_sems.at[a], recv_sem=recv_sems.at[a],
            device_id=(x, y, 1 - c), device_id_type=MESH) for a in range(len(ins))]

    def start():
        for cp in copies():
            cp.start()

    def finish():
        for cp in copies():
            cp.wait_recv()
        for cp in copies():
            cp.wait_send()

    return start, finish


def _rs_chips(ps):
    n = len(ps)

    def body(*refs):
        start, finish = _chips_schedule(refs[:n], refs[n:2 * n], *refs[2 * n:])
        start()
        finish()

    return pl.pallas_call(
        body, name="reduce_scatter_chips",
        out_shape=_chips_shapes(ps), in_specs=[ANY] * n, out_specs=[ANY] * n,
        scratch_shapes=_chips_sems(n),
    )(*ps)


def _chips_shapes(ps):
    return [jax.ShapeDtypeStruct((3,) + p.shape[1:], p.dtype) for p in ps]


def _chips_sems(n):
    return [pltpu.SemaphoreType.DMA((3 * n,)), pltpu.SemaphoreType.DMA((3 * n,))]


def _chips_schedule(ins, land, send_sems, recv_sems):
    x, y, c = _mesh_pos()
    chips = [(1 - x, y), (x, 1 - y), (1 - x, 1 - y)]

    def copies():
        return [pltpu.make_async_remote_copy(
            src_ref=ins[a].at[2 * chip[0] + chip[1]], dst_ref=land[a].at[j],
            send_sem=send_sems.at[3 * a + j], recv_sem=recv_sems.at[3 * a + j],
            device_id=(*chip, c), device_id_type=MESH) for a in range(len(ins)) for j, chip in enumerate(chips)]

    def start():
        for cp in copies():
            cp.start()

    def finish():
        for cp in copies():
            cp.wait_recv()
        for cp in copies():
            cp.wait_send()

    return start, finish


class _Jobs:
    def __init__(self, jobs):
        self.jobs = jobs
        self.inputs = [a for _, arrs in jobs for a in arrs]
        self.out_shapes = [s for kind, arrs in jobs
                           for s in (_sibling_shapes(arrs) if kind == "sibling" else _chips_shapes(arrs))]
        self.sems = [s for kind, arrs in jobs
                     for s in (_sibling_sems(len(arrs)) if kind == "sibling" else _chips_sems(len(arrs)))]
        self.n = len(self.inputs)

    def bind(self, in_refs, out_refs, sem_refs):
        starts, finishes, at = [], [], 0
        for k, (kind, arrs) in enumerate(self.jobs):
            schedule = _sibling_schedule if kind == "sibling" else _chips_schedule
            start, finish = schedule(in_refs[at:at + len(arrs)], out_refs[at:at + len(arrs)],
                                     sem_refs[2 * k], sem_refs[2 * k + 1])
            starts.append(start)
            finishes.append(finish)
            at += len(arrs)

        def start_all():
            for f in starts:
                f()

        def finish_all():
            for f in finishes:
                f()

        return start_all, finish_all

    def split(self, outs):
        res, at = [], 0
        for _, arrs in self.jobs:
            res.append(list(outs[at:at + len(arrs)]))
            at += len(arrs)
        return res


R_META, R_WG, R_NORM_MIX, R_NORM_FF, R_FINAL, R_B_GATE, R_GLA_NORM, R_LOSS, R_SINKS, SMALL_ROWS = 0, 16, 32, 33, 34, 35, 36, 37, 40, 48


SMALL_SPECS = [pl.BlockSpec((LEAD, D), lambda i: (0, 0)), pl.BlockSpec((128, 256), lambda i: (0, 0)),
               pl.BlockSpec((8, D), lambda i: (0, 0)), pl.BlockSpec((8, D), lambda i: (0, 0)),
               pl.BlockSpec((8, D), lambda i: (0, 0)), pl.BlockSpec((8, 256), lambda i: (0, 0)),
               pl.BlockSpec((8, 128), lambda i: (0, 0)), pl.BlockSpec((8, 128), lambda i: (0, 0)),
               pl.BlockSpec((8, 128), lambda i: (0, 0))]


def _small_sum_scratch():
    return [pltpu.VMEM((SMALL_ROWS, D), F32), pltpu.VMEM((N_DEV, SMALL_ROWS, D), F32),
            pltpu.SemaphoreType.DMA((7,)), pltpu.SemaphoreType.DMA((7,))]


def _small_sum_schedule(small_refs, out_ref, p_ref, land, send_sems, recv_sems):
    dlead_ref, dwg_ref, gnm_ref, gnf_ref, gfn_ref, dbg_ref, dgnw_ref, loss_ref, dsink_ref = small_refs
    x, y, c = _mesh_pos()
    me = 4 * x + 2 * y + c

    def copies():
        res = []
        for k in range(1, N_DEV):
            bx, by, bc = (k >> 2) & 1, (k >> 1) & 1, k & 1
            peer = (1 - x if bx else x, 1 - y if by else y, 1 - c if bc else c)
            res.append(pltpu.make_async_remote_copy(
                src_ref=p_ref, dst_ref=land.at[me], send_sem=send_sems.at[k - 1], recv_sem=recv_sems.at[k - 1],
                device_id=peer, device_id_type=MESH))
        return res

    def start():
        p_ref[...] = jnp.zeros_like(p_ref)
        p_ref[R_META:R_META + N_META, :] = dlead_ref[META0:LEAD, :]
        p_ref[R_WG:R_WG + GLA_RANK, 0:256] = dwg_ref[0:GLA_RANK, :]
        p_ref[R_NORM_MIX:R_NORM_MIX + 1, :] = gnm_ref[0:1, :]
        p_ref[R_NORM_FF:R_NORM_FF + 1, :] = gnf_ref[0:1, :]
        p_ref[R_FINAL:R_FINAL + 1, :] = gfn_ref[0:1, :]
        p_ref[R_B_GATE:R_B_GATE + 1, 0:256] = dbg_ref[0:1, :]
        p_ref[R_GLA_NORM:R_GLA_NORM + 1, 0:128] = dgnw_ref[0:1, :]
        p_ref[R_LOSS:R_LOSS + 1, 0:128] = loss_ref[0:1, :]
        p_ref[R_SINKS:R_SINKS + SWA_HEADS, 0:128] = dsink_ref[...]
        land[me] = p_ref[...]
        for cp in copies():
            cp.start()

    def finish():
        for cp in copies():
            cp.wait_recv()
        for cp in copies():
            cp.wait_send()
        acc = land[0]
        for d in range(1, N_DEV):
            acc = acc + land[d]
        out_ref[...] = acc

    return start, finish


def _token_specs(tm, grid_rank=1):
    nb = tm // LEAD

    def spec(k):
        if grid_rank == 1:
            return pl.BlockSpec((LEAD, D), lambda i: (jnp.maximum(i * nb + k - 1, 0), 0))
        return pl.BlockSpec((LEAD, D), lambda i, j: (jnp.maximum(i * nb + k - 1, 0), 0))

    return [spec(k) for k in range(nb)]


def _h_tile(i, lead_ref, x_refs):
    first = jnp.where(i == 0, lead_ref[...], x_refs[0][...])
    return jnp.concatenate([first] + [r[...] for r in x_refs[1:]], axis=0)


def _in_proj(x, lead, nw, win_p, tabs, tm):
    rows = LEAD + x.shape[0]
    nb = tm // LEAD

    def body(*refs):
        x_refs, (lead_ref, nw_ref, w_ref, c_ref, sa_ref, sb_ref, o_ref, q_ref, k_ref, v_ref) = refs[:nb], refs[nb:]
        h = _h_tile(pl.program_id(0), lead_ref, x_refs)
        rstd = lax.rsqrt(jnp.mean(h * h, axis=-1, keepdims=True) + EPS)
        u = (h * rstd * nw_ref[...]).astype(MXU_DTYPE)
        proj = jnp.dot(u, w_ref[...].astype(MXU_DTYPE), preferred_element_type=F32)
        o_ref[...] = proj[:, 0:DGLA]
        cos, sa, sb = c_ref[...], sa_ref[...], sb_ref[...]
        q_ref[...] = (_rope(proj[:, C_SQ:C_SK], cos, sa, sb) * (SWA_HD ** -0.5)).astype(ACT_DTYPE)
        k_ref[...] = _rope(proj[:, C_SK:C_SV], cos, sa, sb).astype(ACT_DTYPE)
        v_ref[...] = proj[:, C_SV:DINP].astype(ACT_DTYPE)

    row = lambda w: pl.BlockSpec((tm, w), lambda i: (i, 0))
    return pl.pallas_call(
        body, name="in_proj", grid=(rows // tm,),
        in_specs=_token_specs(tm) + [pl.BlockSpec((LEAD, D), lambda i: (0, 0)), pl.BlockSpec((1, D), lambda i: (0, 0)),
                                     pl.BlockSpec((D, DINP), lambda i: (0, 0)), row(128), row(128), row(128)],
        out_specs=[row(DGLA), row(512), row(128), row(128)],
        out_shape=[jax.ShapeDtypeStruct((rows, DGLA), F32), jax.ShapeDtypeStruct((rows, 512), ACT_DTYPE),
                   jax.ShapeDtypeStruct((rows, 128), ACT_DTYPE), jax.ShapeDtypeStruct((rows, 128), ACT_DTYPE)],
        compiler_params=_cp(("arbitrary",), 56),
    )(*([x] * nb), lead, nw, win_p, *tabs)


def _rope_tables(rows):
    pos = (jnp.arange(rows, dtype=jnp.int32) - META0).astype(F32)
    inv_freq = 1.0 / (ROPE_THETA ** (jnp.arange(0, ROPE_DIM, 2, dtype=F32) / ROPE_DIM))
    ang = pos[:, None] * jnp.tile(inv_freq, 128 // (ROPE_DIM // 2))[None, :]
    in_head = jnp.arange(128, dtype=jnp.int32)[None, :] % SWA_HD
    cos, sin = jnp.cos(ang), jnp.sin(ang)
    c_tab = jnp.where(in_head < ROPE_DIM, cos, 1.0)
    sa_tab = jnp.where(in_head < ROPE_DIM // 2, -sin, 0.0)
    sb_tab = jnp.where((in_head >= ROPE_DIM // 2) & (in_head < ROPE_DIM), sin, 0.0)
    return c_tab, sa_tab, sb_tab


def _rope(xv, cos, sa, sb):
    width = xv.shape[1]
    reps = width // 128
    if reps > 1:
        cos, sa, sb = (jnp.tile(t, (1, reps)) for t in (cos, sa, sb))
    return xv * cos + pltpu.roll(xv, width - 8, 1) * sa + pltpu.roll(xv, 8, 1) * sb


def _unrope(dy, cos, sa, sb):
    width = dy.shape[1]
    reps = width // 128
    if reps > 1:
        cos, sa, sb = (jnp.tile(t, (1, reps)) for t in (cos, sa, sb))
    return dy * cos + pltpu.roll(dy * sa, 8, 1) + pltpu.roll(dy * sb, width - 8, 1)


def _gla_group(nc):
    for g in (5, 2):
        if nc % g == 0:
            return g
    return 1


def _gla_gates(lr, wg, bg, first_row, nrows):
    zg = _mm(lr, wg) + bg
    row = first_row + lax.broadcasted_iota(jnp.int32, (nrows, 1), 0)
    live = row >= META0
    g = jnp.where(live, _logsigmoid(zg) * (1.0 / GLA_TAU), 0.0)
    ii = lax.broadcasted_iota(jnp.int32, (nrows, nrows), 0)
    jj = lax.broadcasted_iota(jnp.int32, (nrows, nrows), 1)
    same = (ii // GLA_CHUNK) == (jj // GLA_CHUNK)
    lower, upper = same & (jj <= ii), same & (jj >= ii)
    b = jnp.dot(lower.astype(F32), g, precision=HIGHEST, preferred_element_type=F32)
    return zg, live, lower, upper, b


def _tril64():
    ii = lax.broadcasted_iota(jnp.int32, (GLA_CHUNK, GLA_CHUNK), 0)
    jj = lax.broadcasted_iota(jnp.int32, (GLA_CHUNK, GLA_CHUNK), 1)
    return jj <= ii


def _gla_fwd(proj, wg_p, bg, gnw, shards):
    rows = proj.shape[0]
    nc = rows // GLA_CHUNK
    group = _gla_group(nc)
    steps, nrows = nc // group, group * GLA_CHUNK
    ns = len(shards)

    def body(q_ref, k_ref, v_ref, r_ref, lr_ref, wg_ref, bg_ref, gnw_ref, *rest):
        shard_refs, rest = rest[:ns], rest[ns:]
        oraw_ref, og_ref, st_ref = rest[:3]
        gathered_refs, rest = rest[3:3 + ns], rest[3 + ns:]
        state = rest[0]
        c = pl.program_id(0)

        @pl.when(c == 0)
        def _():
            state[...] = jnp.zeros_like(state)

        _place_gather(c, steps, shard_refs, gathered_refs, rest[1:])
        _, _, _, _, b = _gla_gates(lr_ref[...], wg_ref[...], bg_ref[...], c * nrows, nrows)
        eb = jnp.exp(b)
        gq = q_ref[...] * (GLA_DK ** -0.5) * eb
        gk = k_ref[...] * jnp.exp(-b)
        v = v_ref[...]
        gnw_v = gnw_ref[...]
        tril = _tril64()
        pairs = [(h, gi) for h in range(GLA_HEADS) for gi in range(group)]
        rs = {gi: slice(gi * GLA_CHUNK, (gi + 1) * GLA_CHUNK) for gi in range(group)}
        s64 = {h: slice(h * GLA_DK, (h + 1) * GLA_DK) for h in range(GLA_HEADS)}
        s128 = {h: slice(h * GLA_DV, (h + 1) * GLA_DV) for h in range(GLA_HEADS)}
        qh = {(h, gi): gq[rs[gi], s64[h]] for h, gi in pairs}
        kh = {(h, gi): gk[rs[gi], s64[h]] for h, gi in pairs}
        vh = {(h, gi): v[rs[gi], s128[h]] for h, gi in pairs}
        ebl = {(h, gi): eb[(gi + 1) * GLA_CHUNK - 1:(gi + 1) * GLA_CHUNK, s64[h]] for h, gi in pairs}
        av = {pr: _mm(jnp.where(tril, _mm_nt(qh[pr], kh[pr]), 0.0), vh[pr]) for pr in pairs}
        inc = {pr: _mm_tn(vh[pr], kh[pr] * ebl[pr]) for pr in pairs}
        st = {}
        for h in range(GLA_HEADS):
            cur = state[h]
            for gi in range(group):
                st[h, gi] = cur
                st_ref[gi, h] = cur
                cur = cur * ebl[h, gi] + inc[h, gi]
            state[h] = cur
        for h, gi in pairs:
            o = av[h, gi] + _mm_nt(qh[h, gi], st[h, gi])
            oraw_ref[rs[gi], s128[h]] = o
            rstd = lax.rsqrt(jnp.mean(o * o, axis=-1, keepdims=True) + EPS)
            rh = r_ref[rs[gi], s128[h]]
            og_ref[rs[gi], s128[h]] = (o * rstd * gnw_v * (rh * _sigmoid(rh))).astype(ACT_DTYPE)

    nb = lambda w, col: pl.BlockSpec((nrows, w), lambda c: (c, col // w))
    const = lambda shape: pl.BlockSpec(shape, lambda c: (0,) * len(shape))
    outs = pl.pallas_call(
        body, name="gla_fwd", grid=(steps,),
        in_specs=[nb(256, C_GQ), nb(256, C_GK), nb(512, C_GV), nb(512, C_GR), nb(128, C_LR),
                  const((128, 256)), const((1, 256)), const((1, 128))] + [ANY] * ns,
        out_specs=[pl.BlockSpec((nrows, 512), lambda c: (c, 0)), pl.BlockSpec((nrows, 512), lambda c: (c, 0)),
                   pl.BlockSpec((group, GLA_HEADS, GLA_DV, GLA_DK), lambda c: (c, 0, 0, 0))] + [ANY] * ns,
        out_shape=[jax.ShapeDtypeStruct((rows, 512), F32), jax.ShapeDtypeStruct((rows, 512), ACT_DTYPE),
                   jax.ShapeDtypeStruct((nc, GLA_HEADS, GLA_DV, GLA_DK), F32)] + _gathered_shapes(shards),
        scratch_shapes=[pltpu.VMEM((GLA_HEADS, GLA_DV, GLA_DK), F32)] + _gather_sems(ns),
        compiler_params=_cp(("arbitrary",)),
    )(proj, proj, proj, proj, proj, wg_p, bg, gnw, *shards)
    return outs[0], outs[1], outs[2], _with_own_block(outs[3:], shards)


def _swa_mask(n):
    shape = (SWA_GROUP * SWA_BLOCK, 3 * SWA_BLOCK)
    qi = lax.broadcasted_iota(jnp.int32, shape, 0) & (SWA_BLOCK - 1)
    jj = lax.broadcasted_iota(jnp.int32, shape, 1)
    meta = (jj < SWA_BLOCK) & (jj >= META0) & ((n > 0) | (jj <= qi))
    prev = (jj >= SWA_BLOCK) & (jj < 2 * SWA_BLOCK) & (n >= 2) & (jj - SWA_BLOCK > qi)
    cur = (jj >= 2 * SWA_BLOCK) & (n >= 1) & (jj - 2 * SWA_BLOCK <= qi)
    return meta | prev | cur


def _stack_heads(t, kvh):
    return jnp.concatenate([t[:, (kvh * SWA_GROUP + g) * SWA_HD:(kvh * SWA_GROUP + g + 1) * SWA_HD]
                            for g in range(SWA_GROUP)], axis=0)


def _stack_sinks(sink_ref, kvh):
    return jnp.concatenate([jnp.full((SWA_BLOCK, 1), sink_ref[0, kvh * SWA_GROUP + g], F32)
                            for g in range(SWA_GROUP)], axis=0)


def _swa_group(nblk):
    return 5 if nblk % 5 == 0 else 1


def _swa_specs(group):
    blk = lambda w: pl.BlockSpec((group * SWA_BLOCK, w), lambda n: (n, 0))
    first = pl.BlockSpec((SWA_BLOCK, 128), lambda n: (0, 0))
    prev = pl.BlockSpec((SWA_BLOCK, 128), lambda n: (jnp.maximum(n * group - 1, 0), 0))
    return blk, first, prev


def _swa_keys(first_ref, prev_ref, cur_ref, g):
    own = cur_ref[g * SWA_BLOCK:(g + 1) * SWA_BLOCK, :]
    before = prev_ref[...] if g == 0 else cur_ref[(g - 1) * SWA_BLOCK:g * SWA_BLOCK, :]
    return jnp.concatenate([first_ref[...], before, own], axis=0)


def _swa_fwd(qr, kr, vr, sinks, shards):
    rows = qr.shape[0]
    nblk = rows // SWA_BLOCK
    group = _swa_group(nblk)
    steps = nblk // group
    ns = len(shards)

    def body(q_ref, k0, kp, kc, v0, vp, vc, sink_ref, *rest):
        o_ref = rest[ns]
        _place_gather(pl.program_id(0), steps, rest[:ns], rest[ns + 1:2 * ns + 1], rest[2 * ns + 1:])
        for g in range(group):
            n = pl.program_id(0) * group + g
            rs = slice(g * SWA_BLOCK, (g + 1) * SWA_BLOCK)
            kall, vall = _swa_keys(k0, kp, kc, g), _swa_keys(v0, vp, vc, g)
            mask = _swa_mask(n)[0:SWA_BLOCK]
            heads = range(SWA_HEADS)
            hs = [slice(h * SWA_HD, (h + 1) * SWA_HD) for h in heads]
            kv = [slice((h // SWA_GROUP) * SWA_HD, (h // SWA_GROUP + 1) * SWA_HD) for h in heads]
            s = [jnp.where(mask, _mm_nt(q_ref[rs, hs[h]], kall[:, kv[h]]), NEG) for h in heads]
            m = [jnp.maximum(jnp.max(s[h], axis=-1, keepdims=True), sink_ref[0, h]) for h in heads]
            p = [jnp.exp(s[h] - m[h]) for h in heads]
            den = [jnp.sum(p[h], axis=-1, keepdims=True) + jnp.exp(sink_ref[0, h] - m[h]) for h in heads]
            o = [_mm(p[h], vall[:, kv[h]]) for h in heads]
            for h in heads:
                o_ref[rs, hs[h]] = (o[h] / den[h]).astype(ACT_DTYPE)

    blk, first, prev = _swa_specs(group)
    outs = pl.pallas_call(
        body, name="swa_fwd", grid=(steps,),
        in_specs=[blk(512), first, prev, blk(128), first, prev, blk(128),
                  pl.BlockSpec(memory_space=pltpu.SMEM)] + [ANY] * ns,
        out_specs=[blk(512)] + [ANY] * ns,
        out_shape=[jax.ShapeDtypeStruct((rows, 512), ACT_DTYPE)] + _gathered_shapes(shards),
        scratch_shapes=_gather_sems(ns),
        compiler_params=_cp(("arbitrary",)),
    )(qr, kr, kr, kr, vr, vr, vr, sinks, *shards)
    return outs[0], _with_own_block(outs[1:], shards)


def _out_proj(x, lead, og, osw, wout, nfw, tm):
    rows = LEAD + x.shape[0]
    nb = tm // LEAD

    def body(*refs):
        x_refs, (lead_ref, og_ref, os_ref, w_ref, nw_ref, h1_ref, f_ref, ft_ref) = refs[:nb], refs[nb:]
        h0 = _h_tile(pl.program_id(0), lead_ref, x_refs)
        h1 = h0 + _mm(og_ref[...], w_ref[0:512, :]) + _mm(os_ref[...], w_ref[512:1024, :])
        h1_ref[...] = h1
        rstd = lax.rsqrt(jnp.mean(h1 * h1, axis=-1, keepdims=True) + EPS)
        f = h1 * rstd * nw_ref[...]
        f_ref[...] = f.astype(ACT_DTYPE)
        ft_ref[...] = f.T.astype(ACT_DTYPE)

    row = lambda w: pl.BlockSpec((tm, w), lambda i: (i, 0))
    return pl.pallas_call(
        body, name="out_proj", grid=(rows // tm,),
        in_specs=_token_specs(tm) + [pl.BlockSpec((LEAD, D), lambda i: (0, 0)), row(512), row(512),
                                     pl.BlockSpec((D, D), lambda i: (0, 0)), pl.BlockSpec((1, D), lambda i: (0, 0))],
        out_specs=[row(D), row(D), pl.BlockSpec((D, tm), lambda i: (0, i))],
        out_shape=[jax.ShapeDtypeStruct((rows, D), F32), jax.ShapeDtypeStruct((rows, D), ACT_DTYPE),
                   jax.ShapeDtypeStruct((D, rows), ACT_DTYPE)],
        compiler_params=_cp(("arbitrary",), 48),
    )(*([x] * nb), lead, og, osw, wout, nfw)


def _ffn_fwd(f, h1, w1, w2, tgt, fnw, tm):
    rows = f.shape[0]
    nj = D_FF // FF_WIDE
    nb = tm // LEAD

    def body(f_ref, h1_ref, w1_ref, w2_ref, nw_ref, *rest):
        t_refs, (a_ref, dh2_ref, dh2t_ref, loss_ref, gfn_ref, acc) = rest[:nb], rest[nb:]
        i, j = pl.program_id(0), pl.program_id(1)

        @pl.when((i == 0) & (j == 0))
        def _():
            loss_ref[...] = jnp.zeros_like(loss_ref)
            gfn_ref[...] = jnp.zeros_like(gfn_ref)

        @pl.when(j == 0)
        def _():
            acc[...] = jnp.zeros_like(acc)

        a = _mm(f_ref[...], w1_ref[j])
        a_ref[...] = a.astype(ACT_DTYPE)
        z = jnp.square(jnp.maximum(a, 0.0))
        acc[...] += _mm(z, w2_ref[j])

        @pl.when(j == nj - 1)
        def _():
            h2 = h1_ref[...] + acc[...]
            rstd = lax.rsqrt(jnp.mean(h2 * h2, axis=-1, keepdims=True) + EPS)
            hn = h2 * rstd
            nw = nw_ref[...]
            row = i * tm + lax.broadcasted_iota(jnp.int32, (tm, 1), 0)
            target = jnp.concatenate([t[...] for t in t_refs], axis=0)
            err = jnp.where(row >= LEAD, hn * nw - target, 0.0)
            row_loss = jnp.sum(err * err, axis=-1, keepdims=True) * (1.0 / D)
            loss_ref[...] += jnp.broadcast_to(0.5 * jnp.sum(row_loss, axis=0, keepdims=True), loss_ref.shape)
            dy = err * (1.0 / D)
            gfn_ref[...] += jnp.broadcast_to(jnp.sum(dy * hn, axis=0, keepdims=True), gfn_ref.shape)
            dhn = dy * nw
            dh2 = rstd * (dhn - hn * jnp.mean(dhn * hn, axis=-1, keepdims=True))
            dh2_ref[...] = dh2
            dh2t_ref[...] = dh2.T.astype(ACT_DTYPE)

    return pl.pallas_call(
        body, name="ffn_fwd", grid=(rows // tm, nj),
        in_specs=[pl.BlockSpec((tm, D), lambda i, j: (i, 0)), pl.BlockSpec((tm, D), lambda i, j: (i, 0)),
                  _resident((D_FF // FF_WIDE, D, FF_WIDE)), _resident((D_FF // FF_WIDE, FF_WIDE, D)),
                  pl.BlockSpec((1, D), lambda i, j: (0, 0))] + _token_specs(tm, grid_rank=2),
        out_specs=[pl.BlockSpec((tm, FF_WIDE), lambda i, j: (i, j)), pl.BlockSpec((tm, D), lambda i, j: (i, 0)),
                   pl.BlockSpec((D, tm), lambda i, j: (0, i)),
                   pl.BlockSpec((8, 128), lambda i, j: (0, 0)), pl.BlockSpec((8, D), lambda i, j: (0, 0))],
        out_shape=[jax.ShapeDtypeStruct((rows, D_FF), ACT_DTYPE), jax.ShapeDtypeStruct((rows, D), F32),
                   jax.ShapeDtypeStruct((D, rows), ACT_DTYPE),
                   jax.ShapeDtypeStruct((8, 128), F32), jax.ShapeDtypeStruct((8, D), F32)],
        scratch_shapes=[pltpu.VMEM((tm, D), F32)],
        compiler_params=_cp(("arbitrary", "arbitrary"), 56),
    )(f, h1, w1, w2, fnw, *([tgt] * nb))


def _ffn_bwd_act(dh2, a, w1, w2, h1, nfw, tm):
    rows = dh2.shape[0]
    nj = D_FF // FF_WIDE

    def body(dh2_ref, a_ref, w1_ref, w2_ref, h1_ref, nw_ref, da_ref, dh1_ref, gnf_ref, acc):
        i, j = pl.program_id(0), pl.program_id(1)

        @pl.when((i == 0) & (j == 0))
        def _():
            gnf_ref[...] = jnp.zeros_like(gnf_ref)

        @pl.when(j == 0)
        def _():
            acc[...] = jnp.zeros_like(acc)

        dz = _mm_nt(dh2_ref[...], w2_ref[j])
        da = dz * (2.0 * jnp.maximum(a_ref[...].astype(F32), 0.0))
        da_ref[...] = da.astype(ACT_DTYPE)
        acc[...] += _mm_nt(da, w1_ref[j])

        @pl.when(j == nj - 1)
        def _():
            h1 = h1_ref[...]
            rstd = lax.rsqrt(jnp.mean(h1 * h1, axis=-1, keepdims=True) + EPS)
            hn = h1 * rstd
            df = acc[...]
            gnf_ref[...] += jnp.broadcast_to(jnp.sum(df * hn, axis=0, keepdims=True), gnf_ref.shape)
            dfn = df * nw_ref[...]
            dh1_ref[...] = dh2_ref[...] + rstd * (dfn - hn * jnp.mean(dfn * hn, axis=-1, keepdims=True))

    return pl.pallas_call(
        body, name="ffn_bwd_act", grid=(rows // tm, nj),
        in_specs=[pl.BlockSpec((tm, D), lambda i, j: (i, 0)), pl.BlockSpec((tm, FF_WIDE), lambda i, j: (i, j)),
                  _resident((D_FF // FF_WIDE, D, FF_WIDE)), _resident((D_FF // FF_WIDE, FF_WIDE, D)),
                  pl.BlockSpec((tm, D), lambda i, j: (i, 0)), pl.BlockSpec((1, D), lambda i, j: (0, 0))],
        out_specs=[pl.BlockSpec((tm, FF_WIDE), lambda i, j: (i, j)), pl.BlockSpec((tm, D), lambda i, j: (i, 0)),
                   pl.BlockSpec((8, D), lambda i, j: (0, 0))],
        out_shape=[jax.ShapeDtypeStruct((rows, D_FF), ACT_DTYPE), jax.ShapeDtypeStruct((rows, D), F32),
                   jax.ShapeDtypeStruct((8, D), F32)],
        scratch_shapes=[pltpu.VMEM((tm, D), F32)],
        compiler_params=_cp(("arbitrary", "arbitrary"), 56),
    )(dh2, a, w1, w2, h1, nfw)


def _ffn_bwd_weights(ft, a, da, dh2t, tm):
    rows = a.shape[0]
    steps = rows // tm
    pair = 2 * FF_TILE

    def body(ft_ref, a_ref, da_ref, dh2t_ref, dw1_ref, dw2_ref, dw2t):
        i = pl.program_id(1)

        @pl.when(i == 0)
        def _():
            dw1_ref[...] = jnp.zeros_like(dw1_ref)
            dw2t[...] = jnp.zeros_like(dw2t)

        z = jnp.square(jnp.maximum(a_ref[...].astype(F32), 0.0))
        dw1 = _mm(ft_ref[...], da_ref[...])
        for core in range(2):
            dw1_ref[core] += dw1[:, core * FF_TILE:(core + 1) * FF_TILE]
        dw2t[...] += _mm(dh2t_ref[...], z)

        @pl.when(i == steps - 1)
        def _():
            for core in range(2):
                dw2_ref[core] = dw2t[:, core * FF_TILE:(core + 1) * FF_TILE].T

    return pl.pallas_call(
        body, name="ffn_bwd_weights", grid=(N_DEV // 2, steps),
        in_specs=[pl.BlockSpec((D, tm), lambda j, i: (0, i)), pl.BlockSpec((tm, pair), lambda j, i: (i, j)),
                  pl.BlockSpec((tm, pair), lambda j, i: (i, j)), pl.BlockSpec((D, tm), lambda j, i: (0, i))],
        out_specs=[pl.BlockSpec((2, None, D, FF_TILE), lambda j, i: (0, j, 0, 0)),
                   pl.BlockSpec((2, None, FF_TILE, D), lambda j, i: (0, j, 0, 0))],
        out_shape=[jax.ShapeDtypeStruct((2, 4, D, FF_TILE), F32), jax.ShapeDtypeStruct((2, 4, FF_TILE, D), F32)],
        scratch_shapes=[pltpu.VMEM((D, pair), F32)],
        compiler_params=_cp(("arbitrary", "arbitrary"), 56),
    )(ft, a, da, dh2t)


def _out_proj_bwd(dh1, og, osw, wout, tm, partials):
    rows = dh1.shape[0]
    steps = rows // tm
    ns = len(partials)

    def body(dh1_ref, og_ref, os_ref, w_ref, *rest):
        part_refs, rest = rest[:ns], rest[ns:]
        dog_ref, dos_ref, dw_ref = rest[:3]
        land_refs, (send_sems, recv_sems) = rest[3:3 + ns], rest[3 + ns:]
        i = pl.program_id(0)
        start, finish = _sibling_schedule(part_refs, land_refs, send_sems, recv_sems)

        @pl.when(i == 0)
        def _():
            dw_ref[...] = jnp.zeros_like(dw_ref)
            start()

        pl.when(i == steps - 1)(finish)

        dh1 = dh1_ref[...].astype(MXU_DTYPE)
        dog_ref[...] = _mm_nt(dh1, w_ref[0:512, :])
        dos_ref[...] = _mm_nt(dh1, w_ref[512:1024, :])
        for half, ref in enumerate((og_ref, os_ref)):
            dw = _mm_tn(ref[...], dh1)
            for blk in range(4):
                shard = half * 4 + blk
                dw_ref[shard % 2, shard // 2] += dw[blk * 128:(blk + 1) * 128, :]

    row = lambda w: pl.BlockSpec((tm, w), lambda i: (i, 0))
    outs = pl.pallas_call(
        body, name="out_proj_bwd", grid=(steps,),
        in_specs=[row(D), row(512), row(512), pl.BlockSpec((D, D), lambda i: (0, 0))] + [ANY] * ns,
        out_specs=[row(512), row(512), pl.BlockSpec((2, 4, 128, D), lambda i: (0, 0, 0, 0))] + [ANY] * ns,
        out_shape=[jax.ShapeDtypeStruct((rows, 512), F32), jax.ShapeDtypeStruct((rows, 512), F32),
                   jax.ShapeDtypeStruct((2, 4, 128, D), F32)] + _sibling_shapes(partials),
        scratch_shapes=_sibling_sems(ns),
        compiler_params=_cp(("arbitrary",), 48),
    )(dh1, og, osw, wout, *partials)
    return outs[0], outs[1], outs[2], outs[3:]


def _swa_bwd(qr, kr, vr, osw, dos, sinks, jobs):
    rows = qr.shape[0]
    nblk = rows // SWA_BLOCK
    group = _swa_group(nblk)
    steps = nblk // group
    ns = jobs.n

    def body(q_ref, k0, kp, kc, v0, vp, vc, o_ref, do_ref, sink_ref, *rest):
        dq_ref, dk_ref, dv_ref, dsink_ref = rest[ns:ns + 4]
        start, finish = jobs.bind(rest[:ns], rest[ns + 4:2 * ns + 4], rest[2 * ns + 4:])
        step = pl.program_id(0)

        @pl.when(step == 0)
        def _():
            dk_ref[...] = jnp.zeros_like(dk_ref)
            dv_ref[...] = jnp.zeros_like(dv_ref)
            dsink_ref[...] = jnp.zeros_like(dsink_ref)
            start()

        pl.when(step == steps - 1)(finish)
        for g in range(group):
            block(step * group + g, g, q_ref, k0, kp, kc, v0, vp, vc, o_ref, do_ref, sink_ref,
                  dq_ref, dk_ref, dv_ref, dsink_ref)

    def block(n, g, q_ref, k0, kp, kc, v0, vp, vc, o_ref, do_ref, sink_ref, dq_ref, dk_ref, dv_ref, dsink_ref):
        rs = slice(g * SWA_BLOCK, (g + 1) * SWA_BLOCK)
        kall, vall = _swa_keys(k0, kp, kc, g), _swa_keys(v0, vp, vc, g)
        mask = _swa_mask(n)[0:SWA_BLOCK]
        heads = range(SWA_HEADS)
        hs = [slice(h * SWA_HD, (h + 1) * SWA_HD) for h in heads]
        kv = [slice((h // SWA_GROUP) * SWA_HD, (h // SWA_GROUP + 1) * SWA_HD) for h in heads]
        sink = [sink_ref[0, h] for h in heads]
        qh = [q_ref[rs, hs[h]] for h in heads]
        doh = [do_ref[rs, hs[h]] for h in heads]
        s = [jnp.where(mask, _mm_nt(qh[h], kall[:, kv[h]]), NEG) for h in heads]
        dp = [_mm_nt(doh[h], vall[:, kv[h]]) for h in heads]
        delta = [jnp.sum(doh[h] * o_ref[rs, hs[h]].astype(F32), axis=-1, keepdims=True) for h in heads]
        m = [jnp.maximum(jnp.max(s[h], axis=-1, keepdims=True), sink[h]) for h in heads]
        e = [jnp.exp(s[h] - m[h]) for h in heads]
        inv = [1.0 / (jnp.sum(e[h], axis=-1, keepdims=True) + jnp.exp(sink[h] - m[h])) for h in heads]
        p = [e[h] * inv[h] for h in heads]
        ds = [p[h] * (dp[h] - delta[h]) for h in heads]
        dq = [_mm(ds[h], kall[:, kv[h]]) for h in heads]
        dkh = [_mm_tn(ds[h], qh[h]) for h in heads]
        dvh = [_mm_tn(p[h], doh[h]) for h in heads]
        for h in heads:
            dsink = -jnp.sum(jnp.exp(sink[h] - m[h]) * inv[h] * delta[h], axis=0, keepdims=True)
            dsink_ref[h:h + 1, :] += jnp.broadcast_to(dsink, (1, 128))
        dq_ref[rs, :] = jnp.concatenate(dq, axis=1)
        group_sum = lambda parts, kvh: sum(parts[kvh * SWA_GROUP + 1:(kvh + 1) * SWA_GROUP], parts[kvh * SWA_GROUP])
        dk_all = jnp.concatenate([group_sum(dkh, kvh) for kvh in range(SWA_KV)], axis=1)
        dv_all = jnp.concatenate([group_sum(dvh, kvh) for kvh in range(SWA_KV)], axis=1)
        prev0 = pl.multiple_of(jnp.maximum(n - 1, 0) * SWA_BLOCK, SWA_BLOCK)
        cur0 = pl.multiple_of(n * SWA_BLOCK, SWA_BLOCK)
        for ref, val in ((dk_ref, dk_all), (dv_ref, dv_all)):
            ref[0:SWA_BLOCK, :] += val[0:SWA_BLOCK]
            ref[pl.ds(prev0, SWA_BLOCK), :] += val[SWA_BLOCK:2 * SWA_BLOCK]
            ref[pl.ds(cur0, SWA_BLOCK), :] += val[2 * SWA_BLOCK:]

    blk, first, prev = _swa_specs(group)
    whole = pl.BlockSpec((rows, 128), lambda n: (0, 0))
    outs = pl.pallas_call(
        body, name="swa_bwd", grid=(steps,),
        in_specs=[blk(512), first, prev, blk(128), first, prev, blk(128), blk(512), blk(512),
                  pl.BlockSpec(memory_space=pltpu.SMEM)] + [ANY] * ns,
        out_specs=[blk(512), whole, whole, pl.BlockSpec((8, 128), lambda n: (0, 0))] + [ANY] * ns,
        out_shape=[jax.ShapeDtypeStruct((rows, 512), F32), jax.ShapeDtypeStruct((rows, 128), F32),
                   jax.ShapeDtypeStruct((rows, 128), F32), jax.ShapeDtypeStruct((8, 128), F32)] + jobs.out_shapes,
        scratch_shapes=jobs.sems,
        compiler_params=_cp(("arbitrary",), 48),
    )(qr, kr, kr, kr, vr, vr, vr, osw, dos, sinks, *jobs.inputs)
    return outs[0], outs[1], outs[2], outs[3], jobs.split(outs[4:])


def _gla_bwd(proj, oraw, states, dog, wg_p, bg, gnw, jobs):
    rows = proj.shape[0]
    nc = rows // GLA_CHUNK
    group = _gla_group(nc)
    steps, nrows = nc // group, group * GLA_CHUNK
    ns = jobs.n

    def body(q_ref, k_ref, v_ref, r_ref, lr_ref, oraw_ref, st_ref, dog_ref, wg_ref, bg_ref, gnw_ref, *rest):
        dq_ref, dk_ref, dv_ref, dr_ref, dlr_ref, dwg_ref, dbg_ref, dgnw_ref = rest[ns:ns + 8]
        dstate, db_scr = rest[2 * ns + 8:2 * ns + 10]
        start, finish = jobs.bind(rest[:ns], rest[ns + 8:2 * ns + 8], rest[2 * ns + 10:])
        t = pl.program_id(0)
        c = steps - 1 - t

        @pl.when(t == 0)
        def _():
            dstate[...] = jnp.zeros_like(dstate)
            dwg_ref[...] = jnp.zeros_like(dwg_ref)
            dbg_ref[...] = jnp.zeros_like(dbg_ref)
            dgnw_ref[...] = jnp.zeros_like(dgnw_ref)
            start()

        pl.when(t == steps - 1)(finish)

        lr, wg = lr_ref[...], wg_ref[...]
        zg, live, _, upper, b = _gla_gates(lr, wg, bg_ref[...], c * nrows, nrows)
        eb, enb = jnp.exp(b), jnp.exp(-b)
        scale = GLA_DK ** -0.5
        gq = q_ref[...] * scale * eb
        gk = k_ref[...] * enb
        v = v_ref[...]
        gnw_v = gnw_ref[...]
        tril = _tril64()
        is_last = lax.broadcasted_iota(jnp.int32, (GLA_CHUNK, 1), 0) == GLA_CHUNK - 1
        dgnw = jnp.zeros((1, GLA_DV), F32)
        pairs = [(h, gi) for h in range(GLA_HEADS) for gi in range(group)]
        rs = {gi: slice(gi * GLA_CHUNK, (gi + 1) * GLA_CHUNK) for gi in range(group)}
        s64 = {h: slice(h * GLA_DK, (h + 1) * GLA_DK) for h in range(GLA_HEADS)}
        s128 = {h: slice(h * GLA_DV, (h + 1) * GLA_DV) for h in range(GLA_HEADS)}
        qh = {(h, gi): gq[rs[gi], s64[h]] for h, gi in pairs}
        kh = {(h, gi): gk[rs[gi], s64[h]] for h, gi in pairs}
        vh = {(h, gi): v[rs[gi], s128[h]] for h, gi in pairs}
        ebl = {(h, gi): eb[(gi + 1) * GLA_CHUNK - 1:(gi + 1) * GLA_CHUNK, s64[h]] for h, gi in pairs}
        kl = {pr: kh[pr] * ebl[pr] for pr in pairs}
        st = {(h, gi): st_ref[gi, h] for h, gi in pairs}
        do = {}
        for h, gi in pairs:
            o, rh, dout = oraw_ref[rs[gi], s128[h]], r_ref[rs[gi], s128[h]], dog_ref[rs[gi], s128[h]]
            rstd = lax.rsqrt(jnp.mean(o * o, axis=-1, keepdims=True) + EPS)
            on = o * rstd
            sg = _sigmoid(rh)
            dr_ref[rs[gi], s128[h]] = (dout * (on * gnw_v) * (sg * (1.0 + rh * (1.0 - sg)))).astype(ACT_DTYPE)
            dy = dout * (rh * sg)
            dgnw = dgnw + jnp.sum(dy * on, axis=0, keepdims=True)
            don = dy * gnw_v
            do[h, gi] = rstd * (don - on * jnp.mean(don * on, axis=-1, keepdims=True))
        a = {pr: jnp.where(tril, _mm_nt(qh[pr], kh[pr]), 0.0) for pr in pairs}
        da = {pr: jnp.where(tril, _mm_nt(do[pr], vh[pr]), 0.0) for pr in pairs}
        dinc = {pr: _mm_tn(do[pr], qh[pr]) for pr in pairs}
        dgq = {pr: _mm(da[pr], kh[pr]) + _mm(do[pr], st[pr]) for pr in pairs}
        dgk = {pr: _mm_tn(da[pr], qh[pr]) for pr in pairs}
        dv_a = {pr: _mm_tn(a[pr], do[pr]) for pr in pairs}
        dsp = {}
        for h in range(GLA_HEADS):
            cur = dstate[h]
            for gi in reversed(range(group)):
                dsp[h, gi] = cur
                cur = cur * ebl[h, gi] + dinc[h, gi]
            dstate[h] = cur
        for h, gi in pairs:
            pr = (h, gi)
            dkl = _mm(vh[pr], dsp[pr])
            dv_ref[rs[gi], s128[h]] = (dv_a[pr] + _mm_nt(kl[pr], dsp[pr])).astype(ACT_DTYPE)
            debl = jnp.sum(dsp[pr] * st[pr], axis=0, keepdims=True)
            dq_ref[rs[gi], s64[h]] = (dgq[pr] * (scale * eb[rs[gi], s64[h]])).astype(ACT_DTYPE)
            dk_ref[rs[gi], s64[h]] = ((dgk[pr] + dkl * ebl[pr]) * enb[rs[gi], s64[h]]).astype(ACT_DTYPE)
            last = debl * ebl[pr] + jnp.sum(dkl * kl[pr], axis=0, keepdims=True)
            db_scr[rs[gi], s64[h]] = (dgq[pr] * qh[pr] - dgk[pr] * kh[pr] - dkl * kl[pr]
                                      + jnp.where(is_last, last, 0.0))
        dg = jnp.dot(upper.astype(F32), db_scr[...], precision=HIGHEST, preferred_element_type=F32)
        dzg = jnp.where(live, dg * _sigmoid(-zg) * (1.0 / GLA_TAU), 0.0)
        dlr_ref[...] = _mm_nt(dzg, wg).astype(ACT_DTYPE)
        dwg_ref[...] += _mm_tn(lr, dzg)
        dbg_ref[...] += jnp.broadcast_to(jnp.sum(dzg, axis=0, keepdims=True), dbg_ref.shape)
        dgnw_ref[...] += jnp.broadcast_to(dgnw, dgnw_ref.shape)

    nb = lambda w, col: pl.BlockSpec((nrows, w), lambda t: (steps - 1 - t, col // w))
    const = lambda shape: pl.BlockSpec(shape, lambda t: (0,) * len(shape))
    outs = pl.pallas_call(
        body, name="gla_bwd", grid=(steps,),
        in_specs=[nb(256, C_GQ), nb(256, C_GK), nb(512, C_GV), nb(512, C_GR), nb(128, C_LR), nb(512, 0),
                  pl.BlockSpec((group, GLA_HEADS, GLA_DV, GLA_DK), lambda t: (steps - 1 - t, 0, 0, 0)), nb(512, 0),
                  const((128, 256)), const((1, 256)), const((1, 128))] + [ANY] * ns,
        out_specs=[nb(256, 0), nb(256, 0), nb(512, 0), nb(512, 0), nb(128, 0),
                   const((128, 256)), const((8, 256)), const((8, 128))] + [ANY] * ns,
        out_shape=[jax.ShapeDtypeStruct((rows, 256), ACT_DTYPE), jax.ShapeDtypeStruct((rows, 256), ACT_DTYPE),
                   jax.ShapeDtypeStruct((rows, 512), ACT_DTYPE), jax.ShapeDtypeStruct((rows, 512), ACT_DTYPE),
                   jax.ShapeDtypeStruct((rows, 128), ACT_DTYPE), jax.ShapeDtypeStruct((128, 256), F32),
                   jax.ShapeDtypeStruct((8, 256), F32), jax.ShapeDtypeStruct((8, 128), F32)] + jobs.out_shapes,
        scratch_shapes=[pltpu.VMEM((GLA_HEADS, GLA_DV, GLA_DK), F32), pltpu.VMEM((nrows, 256), F32)] + jobs.sems,
        compiler_params=_cp(("arbitrary",)),
    )(proj, proj, proj, proj, proj, oraw, states, dog, wg_p, bg, gnw, *jobs.inputs)
    return outs[:8], jobs.split(outs[8:])


def _in_proj_bwd(x, lead, dh1, nw, win_p, dgv, dgr, dsq, dgq, dgk, dsk, dsv, dlr, tabs, tm):
    seq = x.shape[0]
    rows = LEAD + seq
    nb = tm // LEAD
    steps = rows // tm

    def first_copy(scr, gx_ref, sem):
        return pltpu.make_async_copy(scr.at[pl.ds(LEAD, tm - LEAD)], gx_ref.at[pl.ds(0, tm - LEAD)], sem)

    def tile_copy(scr, gx_ref, sem, step):
        start = pl.multiple_of(jnp.maximum(step * tm - LEAD, 0), LEAD)
        return pltpu.make_async_copy(scr, gx_ref.at[pl.ds(start, tm)], sem)

    def body(*refs):
        x_refs, refs = refs[:nb], refs[nb:]
        (lead_ref, dh1_ref, nw_ref, w_ref, dgv_ref, dgr_ref, dsq_ref, dgq_ref, dgk_ref, dsk_ref, dsv_ref, dlr_ref,
         c_ref, sa_ref, sb_ref, gx_ref, dlead_ref, dproj_ref, ut_ref, gnm_ref, scr, sem) = refs
        i = pl.program_id(0)

        @pl.when(i == 0)
        def _():
            gnm_ref[...] = jnp.zeros_like(gnm_ref)

        cos, sa, sb = c_ref[...], sa_ref[...], sb_ref[...]
        dsq_v = (_unrope(dsq_ref[...], cos, sa, sb) * (SWA_HD ** -0.5)).astype(MXU_DTYPE)
        dsk_v = _unrope(dsk_ref[...], cos, sa, sb).astype(MXU_DTYPE)
        dproj = jnp.concatenate(
            [dgv_ref[...].astype(MXU_DTYPE), dgr_ref[...].astype(MXU_DTYPE), dgq_ref[...].astype(MXU_DTYPE),
             dgk_ref[...].astype(MXU_DTYPE), dlr_ref[...].astype(MXU_DTYPE), dsq_v, dsk_v,
             dsv_ref[...].astype(MXU_DTYPE)],
            axis=1)
        dproj_ref[...] = dproj
        h = _h_tile(i, lead_ref, x_refs)
        rstd = lax.rsqrt(jnp.mean(h * h, axis=-1, keepdims=True) + EPS)
        hn = h * rstd
        nw_v = nw_ref[...]
        ut_ref[...] = (hn * nw_v).T.astype(ACT_DTYPE)
        du = _mm_nt(dproj, w_ref[...])
        gnm_ref[...] += jnp.broadcast_to(jnp.sum(du * hn, axis=0, keepdims=True), gnm_ref.shape)
        dun = du * nw_v
        dh0 = dh1_ref[...] + rstd * (dun - hn * jnp.mean(dun * hn, axis=-1, keepdims=True))

        if tm > LEAD:
            pl.when(i == 1)(lambda: first_copy(scr, gx_ref, sem).wait())
        pl.when(i > 1)(lambda: tile_copy(scr, gx_ref, sem, i).wait())
        scr[...] = dh0

        @pl.when(i == 0)
        def _():
            dlead_ref[...] = dh0[0:LEAD]
            if tm > LEAD:
                first_copy(scr, gx_ref, sem).start()
                if steps == 1:
                    first_copy(scr, gx_ref, sem).wait()

        @pl.when(i > 0)
        def _():
            tile_copy(scr, gx_ref, sem, i).start()

        if steps > 1:
            pl.when(i == steps - 1)(lambda: tile_copy(scr, gx_ref, sem, i).wait())

    row = lambda w: pl.BlockSpec((tm, w), lambda i: (i, 0))
    const = lambda shape: pl.BlockSpec(shape, lambda i: (0,) * len(shape))
    return pl.pallas_call(
        body, name="in_proj_bwd", grid=(steps,),
        in_specs=_token_specs(tm) + [const((LEAD, D)), row(D), const((1, D)), const((D, DINP)),
                                     row(512), row(512), row(512), row(256), row(256), row(128), row(128), row(128),
                                     row(128), row(128), row(128)],
        out_specs=[ANY, const((LEAD, D)), row(DINP), pl.BlockSpec((D, tm), lambda i: (0, i)), const((8, D))],
        out_shape=[jax.ShapeDtypeStruct((seq, D), F32), jax.ShapeDtypeStruct((LEAD, D), F32),
                   jax.ShapeDtypeStruct((rows, DINP), ACT_DTYPE), jax.ShapeDtypeStruct((D, rows), ACT_DTYPE),
                   jax.ShapeDtypeStruct((8, D), F32)],
        scratch_shapes=[pltpu.VMEM((tm, D), F32), pltpu.SemaphoreType.DMA],
        compiler_params=_cp(("arbitrary",), 56),
    )(*([x] * nb), lead, dh1, nw, win_p, dgv, dgr, dsq, dgq, dgk, dsk, dsv, dlr, *tabs)


def _win_runs():
    groups = [(O_GQ, C_GQ), (O_GK, C_GK), (O_GV, C_GV), (O_GR, C_GR), (O_LR, C_LR), (O_SQ, C_SQ), (O_SK, C_SK),
              (O_SV, C_SV)]
    per = DIN // N_DEV
    runs = []
    for (o0, o1), c0 in groups:
        o = o0
        while o < o1:
            d = o // per
            end = min(o1, (d + 1) * per)
            runs.append((d, o - d * per, c0 + o - o0, end - o))
            o = end
    return runs


def _win_padded(g_in):
    tr = 128

    def body(g_ref, o_ref):
        o_ref[...] = jnp.zeros_like(o_ref)
        for d, s, c, w in _win_runs():
            o_ref[:, c:c + w] = g_ref[d, :, s:s + w]

    return pl.pallas_call(
        body, name="w_in_layout", grid=(D // tr,),
        in_specs=[pl.BlockSpec((N_DEV, tr, DIN // N_DEV), lambda i: (0, i, 0))],
        out_specs=pl.BlockSpec((tr, DINP), lambda i: (i, 0)),
        out_shape=jax.ShapeDtypeStruct((D, DINP), g_in.dtype),
        compiler_params=_cp(("arbitrary",)),
    )(g_in)


def _in_proj_bwd_weights(ut, dproj, tm, small):
    rows = dproj.shape[0]
    steps = rows // tm
    per = DIN // N_DEV

    def body(ut_ref, dp_ref, *rest):
        small_refs, (out_ref, total_ref, acc, stage, sems), sum_scratch = rest[:9], rest[9:14], rest[14:]
        i = pl.program_id(0)
        start, finish = _small_sum_schedule(small_refs, total_ref, *sum_scratch)

        @pl.when(i == 0)
        def _():
            acc[...] = jnp.zeros_like(acc)
            start()

        acc[...] += _mm(ut_ref[...], dp_ref[...])
        pl.when(i == steps - 1)(finish)

        @pl.when(i == steps - 1)
        def _():
            copies = []
            for d in range(N_DEV):
                slot = d % 2
                if d >= 2:
                    copies[d - 2].wait()
                for owner, s, c, w in _win_runs():
                    if owner == d:
                        stage[slot, :, s:s + w] = acc[:, c:c + w]
                cp = pltpu.make_async_copy(stage.at[slot], out_ref.at[d % 2, d // 2], sems.at[slot])
                cp.start()
                copies.append(cp)
            copies[N_DEV - 2].wait()
            copies[N_DEV - 1].wait()

    return pl.pallas_call(
        body, name="in_proj_bwd_weights", grid=(steps,),
        in_specs=[pl.BlockSpec((D, tm), lambda i: (0, i)), pl.BlockSpec((tm, DINP), lambda i: (i, 0))] + SMALL_SPECS,
        out_specs=[ANY, pl.BlockSpec((SMALL_ROWS, D), lambda i: (0, 0))],
        out_shape=[jax.ShapeDtypeStruct((2, 4, D, per), F32), jax.ShapeDtypeStruct((SMALL_ROWS, D), F32)],
        scratch_shapes=[pltpu.VMEM((D, DINP), F32), pltpu.VMEM((2, D, per), F32), pltpu.SemaphoreType.DMA((2,))]
        + _small_sum_scratch(),
        compiler_params=_cp(("arbitrary",), 56),
    )(ut, dproj, *small)


def _adamw(w, g, m, v):
    m = ADAM_B1 * m + (1.0 - ADAM_B1) * g
    v = ADAM_B2 * v + (1.0 - ADAM_B2) * jnp.square(g)
    m_hat = m / (1.0 - ADAM_B1 ** ADAM_STEP)
    v_hat = v / (1.0 - ADAM_B2 ** ADAM_STEP)
    delta = -ADAM_LR * (m_hat / (jnp.sqrt(v_hat) + ADAM_EPS) + ADAM_WD * w)
    return delta, m, v


ADAM_STEPS = 8


def _adamw_shards(where, items, name, jobs=None):
    jobs = jobs or _Jobs([])
    ns, nw = jobs.n, len(items)

    def body(where_ref, *rest):
        ins, rest = rest[:5 * nw], rest[5 * nw:]
        job_ins, rest = rest[:ns], rest[ns:]
        outs, rest = rest[:4 * nw], rest[4 * nw:]
        start, finish = jobs.bind(job_ins, rest[:ns], rest[ns:])
        i = pl.program_id(0)
        pl.when(i == 0)(start)
        pl.when(i == ADAM_STEPS - 1)(finish)
        for k in range(nw):
            p_ref, own_ref, w_ref, m_ref, v_ref = ins[5 * k:5 * k + 5]
            g_ref, d_ref, nm_ref, nv_ref = outs[4 * k:4 * k + 4]
            g = ((p_ref[0].astype(F32) + p_ref[1].astype(F32)) + p_ref[2].astype(F32)) + own_ref[...]
            g_ref[...] = g
            d_ref[...], nm_ref[...], nv_ref[...] = _adamw(w_ref[...], g, m_ref[...], v_ref[...])

    in_specs, out_specs, out_shape, operands = [], [], [], []
    for parts, own, w, m, v in items:
        r, cdim = w.shape
        tr = r // ADAM_STEPS
        spec = pl.BlockSpec((tr, cdim), lambda i, s: (i, 0))
        in_specs += [pl.BlockSpec((3, tr, cdim), lambda i, s: (0, i, 0)),
                     pl.BlockSpec((None, tr, cdim), lambda i, s: (s[1], i, 0)), spec, spec, spec]
        out_specs += [spec] * 4
        out_shape += [jax.ShapeDtypeStruct((r, cdim), F32)] * 4
        operands += [parts, own, w, m, v]
    outs = pl.pallas_call(
        body, name=name,
        grid_spec=pltpu.PrefetchScalarGridSpec(
            num_scalar_prefetch=1, grid=(ADAM_STEPS,),
            in_specs=in_specs + [ANY] * ns, out_specs=out_specs + [ANY] * ns, scratch_shapes=jobs.sems),
        out_shape=out_shape + jobs.out_shapes,
        compiler_params=_cp(("arbitrary",)),
    )(where, *operands, *jobs.inputs)
    return [outs[4 * k:4 * k + 4] for k in range(nw)], jobs.split(outs[4 * nw:])


def _adamw_small(items):
    n = len(items)

    def body(*refs):
        ins, outs = refs[:4 * n], refs[4 * n:]
        for k in range(n):
            w_ref, g_ref, m_ref, v_ref = ins[4 * k:4 * k + 4]
            d_ref, nm_ref, nv_ref = outs[3 * k:3 * k + 3]
            d_ref[...], nm_ref[...], nv_ref[...] = _adamw(w_ref[...], g_ref[...], m_ref[...], v_ref[...])

    vm = pl.BlockSpec(memory_space=pltpu.VMEM)
    shapes = [jax.ShapeDtypeStruct(w.shape, F32) for w, _, _, _ in items for _ in range(3)]
    outs = pl.pallas_call(body, name="adamw_small", in_specs=[vm] * (4 * n), out_specs=[vm] * (3 * n),
                          out_shape=shapes)(*[t for item in items for t in item])
    return [outs[3 * k:3 * k + 3] for k in range(n)]


def _add_own_half(where, full, theirs, name, wire_copy=False):
    _, _, r, cdim = full.shape
    tr = 128 if r % 128 == 0 else r

    def body(where_ref, a_ref, b_ref, *o_refs):
        total = a_ref[...] + b_ref[...]
        o_refs[0][...] = total
        if wire_copy:
            o_refs[1][...] = total.astype(WIRE_DTYPE)

    spec = pl.BlockSpec((4, tr, cdim), lambda i, s: (0, i, 0))
    shapes = [jax.ShapeDtypeStruct(theirs.shape, F32)] + ([jax.ShapeDtypeStruct(theirs.shape, WIRE_DTYPE)] if wire_copy else [])
    outs = pl.pallas_call(
        body, name=name,
        grid_spec=pltpu.PrefetchScalarGridSpec(
            num_scalar_prefetch=1, grid=(r // tr,),
            in_specs=[pl.BlockSpec((None, 4, tr, cdim), lambda i, s: (s[0], 0, i, 0)), spec],
            out_specs=[spec] * len(shapes)),
        out_shape=shapes, compiler_params=_cp(("arbitrary",)))(where, full, theirs)
    return outs if wire_copy else outs[0]


def kernel(x, meta_tokens, norm_mix_w, w_in, w_gate_up, b_gate, gla_norm_w, sinks, w_out, norm_ff_w, w_ff1, w_ff2, final_norm_w, loss_target, m_meta_tokens, m_norm_mix_w, m_w_in, m_w_gate_up, m_b_gate, m_gla_norm_w, m_sinks, m_w_out, m_norm_ff_w, m_w_ff1, m_w_ff2, m_final_norm_w, v_meta_tokens, v_norm_mix_w, v_w_in, v_w_gate_up, v_b_gate, v_gla_norm_w, v_sinks, v_w_out, v_norm_ff_w, v_w_ff1, v_w_ff2, v_final_norm_w):
    seq = x.shape[1]
    rows = LEAD + seq
    tm = _row_tile(rows)
    tm_wide = 1664 if rows % 1664 == 0 else tm
    dev =4 * lax.axis_index("x") + 2 * lax.axis_index("y") + lax.axis_index("c")

    small_shard = jnp.concatenate([meta_tokens, w_gate_up[0], jnp.zeros((N_META, 96), F32)], axis=1)
    g_in, g_small = _all_gather([w_in[0].astype(WIRE_DTYPE), small_shard])
    later_shards = [w_out[0].astype(WIRE_DTYPE), w_ff1[0].astype(WIRE_DTYPE), w_ff2[0].astype(WIRE_DTYPE)]
    win_p = _win_padded(g_in)
    meta_full = jnp.transpose(g_small[:, :, 0:128], (1, 0, 2)).reshape(N_META, D)
    wg_full = jnp.transpose(g_small[:, :, 128:160], (1, 0, 2)).reshape(GLA_RANK, GLA_HEADS * GLA_DK)
    wg_p = jnp.concatenate([wg_full, jnp.zeros((128 - GLA_RANK, 256), F32)], axis=0)

    lead = jnp.concatenate([jnp.zeros((META0, D), F32), meta_full], axis=0)
    tabs = _rope_tables(rows)
    proj, qr, kr, vr = _in_proj(x[0], lead, norm_mix_w, win_p, tabs, tm)
    oraw, og, states, (g_out, g_w1) = _gla_fwd(proj, wg_p, b_gate, gla_norm_w, later_shards[0:2])
    osw, (g_w2,) = _swa_fwd(qr, kr, vr, sinks, later_shards[2:3])
    wout_full = g_out.reshape(D, D)
    w2_full = g_w2.reshape(D_FF, D)
    per = FF_WIDE // FF_TILE
    w1_full = jnp.transpose(g_w1.reshape(N_DEV // per, per, D, FF_TILE), (0, 2, 1, 3)).reshape(-1, D, FF_WIDE)
    w2_full = w2_full.reshape(-1, FF_WIDE, D)
    h1, f, ft = _out_proj(x[0], lead, og, osw, wout_full, norm_ff_w, tm)
    a, dh2, dh2t, loss_p, gfn_p = _ffn_fwd(f, h1, w1_full, w2_full, loss_target[0], final_norm_w.reshape(1, D), tm)

    da, dh1, gnf_p = _ffn_bwd_act(dh2, a, w1_full, w2_full, h1, norm_ff_w, tm)
    dw1, dw2 = _ffn_bwd_weights(ft, a, da, dh2t, tm_wide)
    where = jnp.stack([lax.axis_index("c"), 2 * lax.axis_index("x") + lax.axis_index("y")]).astype(jnp.int32)
    dog, dos, dwout, theirs_ffn = _out_proj_bwd(dh1, og, osw, wout_full, tm, [dw1, dw2])
    pairs_ffn = [_add_own_half(where, p, q, "reduce_pair_%d" % (2 + k), wire_copy=True)
                 for k, (p, q) in enumerate(zip([dw1, dw2], theirs_ffn))]
    sums_ffn, wires_ffn = [p[0] for p in pairs_ffn], [p[1] for p in pairs_ffn]
    dsq, dsk, dsv, dsink_p, (parts_ffn, (theirs_wout,)) = _swa_bwd(
        qr, kr, vr, osw, dos, sinks, _Jobs([("chips", wires_ffn), ("sibling", [dwout])]))
    sum_wout, wire_wout = _add_own_half(where, dwout, theirs_wout, "reduce_pair_1", wire_copy=True)
    (dgq, dgk, dgv, dgr, dlr, dwg_p, dbg_p, dgnw_p), ((parts_wout,),) = _gla_bwd(
        proj, oraw, states, dog, wg_p, b_gate, gla_norm_w, _Jobs([("chips", [wire_wout])]))
    grad_x, dlead, dproj, ut, gnm_p = _in_proj_bwd(x[0], lead, dh1, norm_mix_w, win_p, dgv, dgr, dsq, dgq, dgk, dsk,
                                                   dsv, dlr, tabs, tm)
    grad_x = grad_x[None]
    dwin, total = _in_proj_bwd_weights(ut, dproj, tm_wide,
                                       [dlead, dwg_p, gnm_p, gnf_p, gfn_p, dbg_p, dgnw_p, loss_p, dsink_p])

    (theirs_win,) = _rs_sibling([dwin])
    sum_win, sum_win_wire = _add_own_half(where, dwin, theirs_win, "reduce_pair_0", wire_copy=True)

    g_meta = lax.dynamic_slice(total, (R_META, dev * 128), (N_META, 128))
    g_wg = lax.dynamic_slice(total, (R_WG, dev * 32), (GLA_RANK, 32))
    g_norm_mix, g_norm_ff = total[R_NORM_MIX:R_NORM_MIX + 1], total[R_NORM_FF:R_NORM_FF + 1]
    g_final_norm = total[R_FINAL:R_FINAL + 1]
    g_b_gate, g_gla_norm = total[R_B_GATE:R_B_GATE + 1, 0:256], total[R_GLA_NORM:R_GLA_NORM + 1, 0:128]
    g_sinks = total[R_SINKS:R_SINKS + SWA_HEADS, 0].reshape(1, SWA_HEADS)
    loss = total[R_LOSS, 0]

    ((g_wout, d_wout, nm_wout, nv_wout), (g_w1s, d_w1, nm_w1, nv_w1), (g_w2s, d_w2, nm_w2, nv_w2)), ((parts_win,),) = \
        _adamw_shards(where, [(parts_wout, sum_wout, w_out[0], m_w_out[0], v_w_out[0]),
                              (parts_ffn[0], sums_ffn[0], w_ff1[0], m_w_ff1[0], v_w_ff1[0]),
                              (parts_ffn[1], sums_ffn[1], w_ff2[0], m_w_ff2[0], v_w_ff2[0])],
                      "adamw_w_out_ff", _Jobs([("chips", [sum_win_wire])]))
    ((g_win, d_win, nm_win, nv_win),), _ = _adamw_shards(
        where, [(parts_win, sum_win, w_in[0], m_w_in[0], v_w_in[0])], "adamw_w_in")

    names = ["meta", "wg", "norm_mix", "b_gate", "gla_norm", "sinks", "norm_ff", "final_norm"]
    ws = [meta_tokens, w_gate_up, norm_mix_w, b_gate, gla_norm_w, sinks, norm_ff_w, final_norm_w]
    gs = [g_meta, g_wg, g_norm_mix, g_b_gate, g_gla_norm, g_sinks, g_norm_ff, g_final_norm]
    ms = [m_meta_tokens, m_w_gate_up, m_norm_mix_w, m_b_gate, m_gla_norm_w, m_sinks, m_norm_ff_w, m_final_norm_w]
    vs = [v_meta_tokens, v_w_gate_up, v_norm_mix_w, v_b_gate, v_gla_norm_w, v_sinks, v_norm_ff_w, v_final_norm_w]
    flat = lambda t: t.reshape(-1, t.shape[-1])
    small_out = _adamw_small([(flat(w), flat(g), flat(m), flat(v)) for w, g, m, v in zip(ws, gs, ms, vs)])
    d_small = {n: small_out[k][0].reshape(ws[k].shape) for k, n in enumerate(names)}
    nm_small = {n: small_out[k][1].reshape(ws[k].shape) for k, n in enumerate(names)}
    nv_small = {n: small_out[k][2].reshape(ws[k].shape) for k, n in enumerate(names)}
    g_small_d = {n: g.reshape(ws[k].shape) for k, (n, g) in enumerate(zip(names, gs))}

    def ordered(big, small_d):
        win_v, wout_v, w1_v, w2_v = big
        return (small_d["meta"], small_d["norm_mix"], win_v[None], small_d["wg"], small_d["b_gate"],
                small_d["gla_norm"], small_d["sinks"], wout_v[None], small_d["norm_ff"], w1_v[None], w2_v[None],
                small_d["final_norm"])

    return (loss, grad_x,
            *ordered((g_win, g_wout, g_w1s, g_w2s), g_small_d),
            *ordered((d_win, d_wout, d_w1, d_w2), d_small),
            *ordered((nm_win, nm_wout, nm_w1, nm_w2), nm_small),
            *ordered((nv_win, nv_wout, nv_w1, nv_w2), nv_small))
```

```python
import functools

import jax
import jax.numpy as jnp
from jax import lax
from jax.experimental import pallas as pl
from jax.experimental.pallas import tpu as pltpu

F32 = jnp.float32
MXU_DTYPE = jnp.bfloat16
ACT_DTYPE = jnp.bfloat16
WIRE_DTYPE = jnp.bfloat16

D = 1024
N_META = 16
LEAD = 128
META0 = LEAD - N_META
EPS = 1e-5
GLA_HEADS, GLA_DK, GLA_DV, GLA_RANK, GLA_CHUNK = 4, 64, 128, 16, 64
GLA_TAU = 16.0
SWA_HEADS, SWA_KV, SWA_GROUP, SWA_HD, SWA_BLOCK = 8, 2, 4, 64, 128
ROPE_DIM, ROPE_THETA = 16, 500000.0
D_FF = 4096
N_DEV = 8
FF_TILE = D_FF // N_DEV
FF_WIDE = 2048
NEG = -1e30

C_GV, C_GR, C_GQ, C_GK, C_LR, C_SQ, C_SK, C_SV = 0, 512, 1024, 1280, 1536, 1664, 2176, 2304
DGLA = 1664
DINP = 2432
DIN = 2320
O_GQ, O_GK, O_GV, O_GR, O_LR, O_SQ, O_SK, O_SV = (0, 256), (256, 512), (512, 1024), (1024, 1536), (1536, 1552), (1552, 2064), (2064, 2192), (2192, 2320)

ADAM_LR, ADAM_B1, ADAM_B2, ADAM_EPS, ADAM_WD, ADAM_STEP = 0.001, 0.9, 0.999, 1e-08, 0.01, 10

MESH = pl.DeviceIdType.MESH
ANY = pl.BlockSpec(memory_space=pl.ANY)
HIGHEST = lax.Precision.HIGHEST
VMEM_TILE_MB, VMEM_WIDE_MB = 48, 56


def _cp(sem=None, vmem_mb=None):
    kw = {}
    if sem is not None:
        kw["dimension_semantics"] = sem
    if vmem_mb is not None:
        kw["vmem_limit_bytes"] = vmem_mb << 20
    return pltpu.CompilerParams(**kw)


def _mm(a, b):
    return jnp.dot(a.astype(MXU_DTYPE), b.astype(MXU_DTYPE), preferred_element_type=F32)


def _mm_nt(a, b):
    return lax.dot_general(a.astype(MXU_DTYPE), b.astype(MXU_DTYPE), (((1,), (1,)), ((), ())),
                           preferred_element_type=F32)


def _mm_tn(a, b):
    return lax.dot_general(a.astype(MXU_DTYPE), b.astype(MXU_DTYPE), (((0,), (0,)), ((), ())),
                           preferred_element_type=F32)


def _logsigmoid(z):
    return jnp.minimum(z, 0.0) - jnp.log(1.0 + jnp.exp(-jnp.abs(z)))


def _sigmoid(z):
    return 1.0 / (1.0 + jnp.exp(-z))


ROW_TILE, WIDE_ROW_TILE = 640, 1664


def _row_tile(rows, want=ROW_TILE):
    return want if rows % want == 0 else LEAD


def _mesh_pos():
    return lax.axis_index("x"), lax.axis_index("y"), lax.axis_index("c")


def _all_gather(shards):
    n = len(shards)

    def body(*refs):
        start, forward, finish = _gather_schedule(refs[:n], refs[n:2 * n], *refs[2 * n:])
        start()
        for j in range(3):
            forward(j)
        finish()

    gathered = pl.pallas_call(
        body, name="all_gather_weights",
        out_shape=_gathered_shapes(shards), in_specs=[ANY] * n, out_specs=[ANY] * n,
        scratch_shapes=_gather_sems(n),
    )(*shards)
    return _with_own_block(gathered, shards)


def _gathered_shapes(shards):
    return [jax.ShapeDtypeStruct((N_DEV,) + s.shape, s.dtype) for s in shards]


def _gather_sems(n):
    return [pltpu.SemaphoreType.DMA((7 * n,)), pltpu.SemaphoreType.DMA((7 * n,))] if n else []


def _place_gather(step, steps, shard_refs, gathered_refs, sems):
    if not shard_refs:
        return
    start, forward, finish = _gather_schedule(shard_refs, gathered_refs, *sems)
    pl.when(step == 0)(start)
    for j, at in enumerate((steps * 7 // 10, steps * 8 // 10, steps * 9 // 10)):
        pl.when(step == at)(functools.partial(forward, j))
    pl.when(step == steps - 1)(finish)


def _with_own_block(gathered, shards):
    dev = 4 * lax.axis_index("x") + 2 * lax.axis_index("y") + lax.axis_index("c")
    return [lax.dynamic_update_index_in_dim(g, s, dev, 0) for g, s in zip(gathered, shards)]


def _gather_schedule(ins, outs, send_sems, recv_sems):
    n = len(ins)
    x, y, c = _mesh_pos()
    me, sibling = (x, y, c), (x, y, 1 - c)
    chips = [(1 - x, y), (x, 1 - y), (1 - x, 1 - y)]

    def copy(a, k, block, to, src=None):
        dst = outs[a].at[4 * block[0] + 2 * block[1] + block[2]]
        return pltpu.make_async_remote_copy(
            src_ref=dst if src is None else src, dst_ref=dst,
            send_sem=send_sems.at[a * 7 + k], recv_sem=recv_sems.at[a * 7 + k],
            device_id=to, device_id_type=MESH)

    def first(a):
        return [copy(a, 0, me, sibling, src=ins[a])] + [copy(a, 1 + j, me, (*chip, c), src=ins[a])
                                                        for j, chip in enumerate(chips)]

    def start():
        for a in range(n):
            for cp in first(a):
                cp.start()

    def forward(j):
        for a in range(n):
            copy(a, 1 + j, (*chips[j], c), me).wait_recv()
            copy(a, 4 + j, (*chips[j], c), sibling).start()

    def finish():
        for a in range(n):
            copy(a, 0, sibling, me).wait_recv()
            for j, chip in enumerate(chips):
                copy(a, 4 + j, (*chip, 1 - c), me).wait_recv()
        for a in range(n):
            for cp in first(a) + [copy(a, 4 + j, (*chip, c), sibling) for j, chip in enumerate(chips)]:
                cp.wait_send()

    return start, forward, finish


def _sibling_shapes(gs):
    return [jax.ShapeDtypeStruct(g.shape[1:], g.dtype) for g in gs]


def _sibling_sems(n):
    return [pltpu.SemaphoreType.DMA((n,)), pltpu.SemaphoreType.DMA((n,))]


def _sibling_schedule(ins, land, send_sems, recv_sems):
    x, y, c = _mesh_pos()

    def copies():
        return [pltpu.make_async_remote_copy(
            src_ref=ins[a].at[1 - c], dst_ref=land[a], send_sem=send_sems.at[a], recv_sem=recv_sems.at[a],
            device_id=(x, y, 1 - c), device_id_type=MESH) for a in range(len(ins))]

    def start():
        for cp in copies():
            cp.start()

    def finish():
        for cp in copies():
            cp.wait_recv()
        for cp in copies():
            cp.wait_send()

    return start, finish


def _chips_shapes(ps):
    return [jax.ShapeDtypeStruct((3,) + p.shape[1:], p.dtype) for p in ps]


def _chips_sems(n):
    return [pltpu.SemaphoreType.DMA((3 * n,)), pltpu.SemaphoreType.DMA((3 * n,))]


def _chips_schedule(ins, land, send_sems, recv_sems):
    x, y, c = _mesh_pos()
    chips = [(1 - x, y), (x, 1 - y), (1 - x, 1 - y)]

    def copies():
        return [pltpu.make_async_remote_copy(
            src_ref=ins[a].at[2 * chip[0] + chip[1]], dst_ref=land[a].at[j],
            send_sem=send_sems.at[3 * a + j], recv_sem=recv_sems.at[3 * a + j],
            device_id=(*chip, c), device_id_type=MESH) for a in range(len(ins)) for j, chip in enumerate(chips)]

    def start():
        for cp in copies():
            cp.start()

    def finish():
        for cp in copies():
            cp.wait_recv()
        for cp in copies():
            cp.wait_send()

    return start, finish


class _Jobs:
    def __init__(self, jobs):
        self.jobs = jobs
        self.inputs = [a for _, arrs in jobs for a in arrs]
        self.out_shapes = [s for kind, arrs in jobs
                           for s in (_sibling_shapes(arrs) if kind == "sibling" else _chips_shapes(arrs))]
        self.sems = [s for kind, arrs in jobs
                     for s in (_sibling_sems(len(arrs)) if kind == "sibling" else _chips_sems(len(arrs)))]
        self.n = len(self.inputs)

    def bind(self, in_refs, out_refs, sem_refs):
        starts, finishes, at = [], [], 0
        for k, (kind, arrs) in enumerate(self.jobs):
            schedule = _sibling_schedule if kind == "sibling" else _chips_schedule
            start, finish = schedule(in_refs[at:at + len(arrs)], out_refs[at:at + len(arrs)],
                                     sem_refs[2 * k], sem_refs[2 * k + 1])
            starts.append(start)
            finishes.append(finish)
            at += len(arrs)

        def start_all():
            for f in starts:
                f()

        def finish_all():
            for f in finishes:
                f()

        return start_all, finish_all

    def split(self, outs):
        res, at = [], 0
        for _, arrs in self.jobs:
            res.append(list(outs[at:at + len(arrs)]))
            at += len(arrs)
        return res


R_META, R_WG, R_NORM_MIX, R_NORM_FF, R_FINAL, R_B_GATE, R_GLA_NORM, R_LOSS, R_SINKS, SMALL_ROWS = 0, 16, 32, 33, 34, 35, 36, 37, 40, 48


SMALL_SPECS = [pl.BlockSpec((LEAD, D), lambda i: (0, 0)), pl.BlockSpec((128, 256), lambda i: (0, 0)),
               pl.BlockSpec((8, D), lambda i: (0, 0)), pl.BlockSpec((8, D), lambda i: (0, 0)),
               pl.BlockSpec((8, D), lambda i: (0, 0)), pl.BlockSpec((8, 256), lambda i: (0, 0)),
               pl.BlockSpec((8, 128), lambda i: (0, 0)), pl.BlockSpec((8, 128), lambda i: (0, 0)),
               pl.BlockSpec((8, 128), lambda i: (0, 0))]


def _small_sum_scratch():
    return [pltpu.VMEM((SMALL_ROWS, D), F32), pltpu.VMEM((N_DEV, SMALL_ROWS, D), F32),
            pltpu.SemaphoreType.DMA((7,)), pltpu.SemaphoreType.DMA((7,))]


def _small_sum_schedule(small_refs, out_ref, p_ref, land, send_sems, recv_sems):
    dlead_ref, dwg_ref, gnm_ref, gnf_ref, gfn_ref, dbg_ref, dgnw_ref, loss_ref, dsink_ref = small_refs
    x, y, c = _mesh_pos()
    me = 4 * x + 2 * y + c

    def copies():
        res = []
        for k in range(1, N_DEV):
            bx, by, bc = (k >> 2) & 1, (k >> 1) & 1, k & 1
            peer = (1 - x if bx else x, 1 - y if by else y, 1 - c if bc else c)
            res.append(pltpu.make_async_remote_copy(
                src_ref=p_ref, dst_ref=land.at[me], send_sem=send_sems.at[k - 1], recv_sem=recv_sems.at[k - 1],
                device_id=peer, device_id_type=MESH))
        return res

    def start():
        p_ref[...] = jnp.zeros_like(p_ref)
        p_ref[R_META:R_META + N_META, :] = dlead_ref[META0:LEAD, :]
        p_ref[R_WG:R_WG + GLA_RANK, 0:256] = dwg_ref[0:GLA_RANK, :]
        p_ref[R_NORM_MIX:R_NORM_MIX + 1, :] = gnm_ref[0:1, :]
        p_ref[R_NORM_FF:R_NORM_FF + 1, :] = gnf_ref[0:1, :]
        p_ref[R_FINAL:R_FINAL + 1, :] = gfn_ref[0:1, :]
        p_ref[R_B_GATE:R_B_GATE + 1, 0:256] = dbg_ref[0:1, :]
        p_ref[R_GLA_NORM:R_GLA_NORM + 1, 0:128] = dgnw_ref[0:1, :]
        p_ref[R_LOSS:R_LOSS + 1, 0:128] = loss_ref[0:1, :]
        p_ref[R_SINKS:R_SINKS + SWA_HEADS, 0:128] = dsink_ref[...]
        land[me] = p_ref[...]
        for cp in copies():
            cp.start()

    def finish():
        for cp in copies():
            cp.wait_recv()
        for cp in copies():
            cp.wait_send()
        acc = land[0]
        for d in range(1, N_DEV):
            acc = acc + land[d]
        out_ref[...] = acc

    return start, finish


def _token_specs(tm, grid_rank=1):
    nb = tm // LEAD

    def spec(k):
        if grid_rank == 1:
            return pl.BlockSpec((LEAD, D), lambda i: (jnp.maximum(i * nb + k - 1, 0), 0))
        return pl.BlockSpec((LEAD, D), lambda i, j: (jnp.maximum(i * nb + k - 1, 0), 0))

    return [spec(k) for k in range(nb)]


def _h_tile(i, lead_ref, x_refs):
    first = jnp.where(i == 0, lead_ref[...], x_refs[0][...])
    return jnp.concatenate([first] + [r[...] for r in x_refs[1:]], axis=0)


def _in_proj(x, lead, nw, win_p, tabs, tm):
    rows = LEAD + x.shape[0]
    nb = tm // LEAD

    def body(*refs):
        x_refs, (lead_ref, nw_ref, w_ref, c_ref, sa_ref, sb_ref, o_ref, q_ref, k_ref, v_ref) = refs[:nb], refs[nb:]
        h = _h_tile(pl.program_id(0), lead_ref, x_refs)
        rstd = lax.rsqrt(jnp.mean(h * h, axis=-1, keepdims=True) + EPS)
        u = (h * rstd * nw_ref[...]).astype(MXU_DTYPE)
        proj = jnp.dot(u, w_ref[...].astype(MXU_DTYPE), preferred_element_type=F32)
        o_ref[...] = proj[:, 0:DGLA]
        cos, sa, sb = c_ref[...], sa_ref[...], sb_ref[...]
        q_ref[...] = (_rope(proj[:, C_SQ:C_SK], cos, sa, sb) * (SWA_HD ** -0.5)).astype(ACT_DTYPE)
        k_ref[...] = _rope(proj[:, C_SK:C_SV], cos, sa, sb).astype(ACT_DTYPE)
        v_ref[...] = proj[:, C_SV:DINP].astype(ACT_DTYPE)

    row = lambda w: pl.BlockSpec((tm, w), lambda i: (i, 0))
    return pl.pallas_call(
        body, name="in_proj", grid=(rows // tm,),
        in_specs=_token_specs(tm) + [pl.BlockSpec((LEAD, D), lambda i: (0, 0)), pl.BlockSpec((1, D), lambda i: (0, 0)),
                                     pl.BlockSpec((D, DINP), lambda i: (0, 0)), row(128), row(128), row(128)],
        out_specs=[row(DGLA), row(512), row(128), row(128)],
        out_shape=[jax.ShapeDtypeStruct((rows, DGLA), F32), jax.ShapeDtypeStruct((rows, 512), ACT_DTYPE),
                   jax.ShapeDtypeStruct((rows, 128), ACT_DTYPE), jax.ShapeDtypeStruct((rows, 128), ACT_DTYPE)],
        compiler_params=_cp(("arbitrary",), VMEM_WIDE_MB),
    )(*([x] * nb), lead, nw, win_p, *tabs)


def _rope_tables(rows):
    pos = (jnp.arange(rows, dtype=jnp.int32) - META0).astype(F32)
    inv_freq = 1.0 / (ROPE_THETA ** (jnp.arange(0, ROPE_DIM, 2, dtype=F32) / ROPE_DIM))
    ang = pos[:, None] * jnp.tile(inv_freq, 128 // (ROPE_DIM // 2))[None, :]
    in_head = jnp.arange(128, dtype=jnp.int32)[None, :] % SWA_HD
    cos, sin = jnp.cos(ang), jnp.sin(ang)
    c_tab = jnp.where(in_head < ROPE_DIM, cos, 1.0)
    sa_tab = jnp.where(in_head < ROPE_DIM // 2, -sin, 0.0)
    sb_tab = jnp.where((in_head >= ROPE_DIM // 2) & (in_head < ROPE_DIM), sin, 0.0)
    return c_tab, sa_tab, sb_tab


def _rope(xv, cos, sa, sb):
    width = xv.shape[1]
    reps = width // 128
    if reps > 1:
        cos, sa, sb = (jnp.tile(t, (1, reps)) for t in (cos, sa, sb))
    return xv * cos + pltpu.roll(xv, width - 8, 1) * sa + pltpu.roll(xv, 8, 1) * sb


def _unrope(dy, cos, sa, sb):
    width = dy.shape[1]
    reps = width // 128
    if reps > 1:
        cos, sa, sb = (jnp.tile(t, (1, reps)) for t in (cos, sa, sb))
    return dy * cos + pltpu.roll(dy * sa, 8, 1) + pltpu.roll(dy * sb, width - 8, 1)


def _gla_group(nc):
    for g in (5, 2):
        if nc % g == 0:
            return g
    return 1


def _gla_gates(lr, wg, bg, first_row, nrows):
    zg = _mm(lr, wg) + bg
    row = first_row + lax.broadcasted_iota(jnp.int32, (nrows, 1), 0)
    live = row >= META0
    g = jnp.where(live, _logsigmoid(zg) * (1.0 / GLA_TAU), 0.0)
    ii = lax.broadcasted_iota(jnp.int32, (nrows, nrows), 0)
    jj = lax.broadcasted_iota(jnp.int32, (nrows, nrows), 1)
    same = (ii // GLA_CHUNK) == (jj // GLA_CHUNK)
    lower, upper = same & (jj <= ii), same & (jj >= ii)
    b = jnp.dot(lower.astype(F32), g, precision=HIGHEST, preferred_element_type=F32)
    return zg, live, lower, upper, b


def _tril64():
    ii = lax.broadcasted_iota(jnp.int32, (GLA_CHUNK, GLA_CHUNK), 0)
    jj = lax.broadcasted_iota(jnp.int32, (GLA_CHUNK, GLA_CHUNK), 1)
    return jj <= ii


def _gla_fwd(proj, wg_p, bg, gnw, shards):
    rows = proj.shape[0]
    nc = rows // GLA_CHUNK
    group = _gla_group(nc)
    steps, nrows = nc // group, group * GLA_CHUNK
    ns = len(shards)

    def body(q_ref, k_ref, v_ref, r_ref, lr_ref, wg_ref, bg_ref, gnw_ref, *rest):
        shard_refs, rest = rest[:ns], rest[ns:]
        oraw_ref, og_ref, st_ref = rest[:3]
        gathered_refs, rest = rest[3:3 + ns], rest[3 + ns:]
        state = rest[0]
        c = pl.program_id(0)

        @pl.when(c == 0)
        def _():
            state[...] = jnp.zeros_like(state)

        _place_gather(c, steps, shard_refs, gathered_refs, rest[1:])
        _, _, _, _, b = _gla_gates(lr_ref[...], wg_ref[...], bg_ref[...], c * nrows, nrows)
        eb = jnp.exp(b)
        gq = q_ref[...] * (GLA_DK ** -0.5) * eb
        gk = k_ref[...] * jnp.exp(-b)
        v = v_ref[...]
        gnw_v = gnw_ref[...]
        tril = _tril64()
        pairs = [(h, gi) for h in range(GLA_HEADS) for gi in range(group)]
        rs = {gi: slice(gi * GLA_CHUNK, (gi + 1) * GLA_CHUNK) for gi in range(group)}
        s64 = {h: slice(h * GLA_DK, (h + 1) * GLA_DK) for h in range(GLA_HEADS)}
        s128 = {h: slice(h * GLA_DV, (h + 1) * GLA_DV) for h in range(GLA_HEADS)}
        qh = {(h, gi): gq[rs[gi], s64[h]] for h, gi in pairs}
        kh = {(h, gi): gk[rs[gi], s64[h]] for h, gi in pairs}
        vh = {(h, gi): v[rs[gi], s128[h]] for h, gi in pairs}
        ebl = {(h, gi): eb[(gi + 1) * GLA_CHUNK - 1:(gi + 1) * GLA_CHUNK, s64[h]] for h, gi in pairs}
        av = {pr: _mm(jnp.where(tril, _mm_nt(qh[pr], kh[pr]), 0.0), vh[pr]) for pr in pairs}
        inc = {pr: _mm_tn(vh[pr], kh[pr] * ebl[pr]) for pr in pairs}
        st = {}
        for h in range(GLA_HEADS):
            cur = state[h]
            for gi in range(group):
                st[h, gi] = cur
                st_ref[gi, h] = cur
                cur = cur * ebl[h, gi] + inc[h, gi]
            state[h] = cur
        for h, gi in pairs:
            o = av[h, gi] + _mm_nt(qh[h, gi], st[h, gi])
            oraw_ref[rs[gi], s128[h]] = o
            rstd = lax.rsqrt(jnp.mean(o * o, axis=-1, keepdims=True) + EPS)
            rh = r_ref[rs[gi], s128[h]]
            og_ref[rs[gi], s128[h]] = (o * rstd * gnw_v * (rh * _sigmoid(rh))).astype(ACT_DTYPE)

    nb = lambda w, col: pl.BlockSpec((nrows, w), lambda c: (c, col // w))
    const = lambda shape: pl.BlockSpec(shape, lambda c: (0,) * len(shape))
    outs = pl.pallas_call(
        body, name="gla_fwd", grid=(steps,),
        in_specs=[nb(256, C_GQ), nb(256, C_GK), nb(512, C_GV), nb(512, C_GR), nb(128, C_LR),
                  const((128, 256)), const((1, 256)), const((1, 128))] + [ANY] * ns,
        out_specs=[pl.BlockSpec((nrows, 512), lambda c: (c, 0)), pl.BlockSpec((nrows, 512), lambda c: (c, 0)),
                   pl.BlockSpec((group, GLA_HEADS, GLA_DV, GLA_DK), lambda c: (c, 0, 0, 0))] + [ANY] * ns,
        out_shape=[jax.ShapeDtypeStruct((rows, 512), F32), jax.ShapeDtypeStruct((rows, 512), ACT_DTYPE),
                   jax.ShapeDtypeStruct((nc, GLA_HEADS, GLA_DV, GLA_DK), F32)] + _gathered_shapes(shards),
        scratch_shapes=[pltpu.VMEM((GLA_HEADS, GLA_DV, GLA_DK), F32)] + _gather_sems(ns),
        compiler_params=_cp(("arbitrary",)),
    )(proj, proj, proj, proj, proj, wg_p, bg, gnw, *shards)
    return outs[0], outs[1], outs[2], _with_own_block(outs[3:], shards)


def _swa_mask(n):
    shape = (SWA_GROUP * SWA_BLOCK, 3 * SWA_BLOCK)
    qi = lax.broadcasted_iota(jnp.int32, shape, 0) & (SWA_BLOCK - 1)
    jj = lax.broadcasted_iota(jnp.int32, shape, 1)
    meta = (jj < SWA_BLOCK) & (jj >= META0) & ((n > 0) | (jj <= qi))
    prev = (jj >= SWA_BLOCK) & (jj < 2 * SWA_BLOCK) & (n >= 2) & (jj - SWA_BLOCK > qi)
    cur = (jj >= 2 * SWA_BLOCK) & (n >= 1) & (jj - 2 * SWA_BLOCK <= qi)
    return meta | prev | cur


def _stack_heads(t, kvh):
    return jnp.concatenate([t[:, (kvh * SWA_GROUP + g) * SWA_HD:(kvh * SWA_GROUP + g + 1) * SWA_HD]
                            for g in range(SWA_GROUP)], axis=0)


def _stack_sinks(sink_ref, kvh):
    return jnp.concatenate([jnp.full((SWA_BLOCK, 1), sink_ref[0, kvh * SWA_GROUP + g], F32)
                            for g in range(SWA_GROUP)], axis=0)


def _swa_group(nblk):
    return 5 if nblk % 5 == 0 else 1


def _swa_specs(group):
    blk = lambda w: pl.BlockSpec((group * SWA_BLOCK, w), lambda n: (n, 0))
    first = pl.BlockSpec((SWA_BLOCK, 128), lambda n: (0, 0))
    prev = pl.BlockSpec((SWA_BLOCK, 128), lambda n: (jnp.maximum(n * group - 1, 0), 0))
    return blk, first, prev


def _swa_keys(first_ref, prev_ref, cur_ref, g):
    own = cur_ref[g * SWA_BLOCK:(g + 1) * SWA_BLOCK, :]
    before = prev_ref[...] if g == 0 else cur_ref[(g - 1) * SWA_BLOCK:g * SWA_BLOCK, :]
    return jnp.concatenate([first_ref[...], before, own], axis=0)


def _swa_fwd(qr, kr, vr, sinks, shards):
    rows = qr.shape[0]
    nblk = rows // SWA_BLOCK
    group = _swa_group(nblk)
    steps = nblk // group
    ns = len(shards)

    def body(q_ref, k0, kp, kc, v0, vp, vc, sink_ref, *rest):
        o_ref = rest[ns]
        _place_gather(pl.program_id(0), steps, rest[:ns], rest[ns + 1:2 * ns + 1], rest[2 * ns + 1:])
        for g in range(group):
            n = pl.program_id(0) * group + g
            rs = slice(g * SWA_BLOCK, (g + 1) * SWA_BLOCK)
            kall, vall = _swa_keys(k0, kp, kc, g), _swa_keys(v0, vp, vc, g)
            mask = _swa_mask(n)[0:SWA_BLOCK]
            heads = range(SWA_HEADS)
            hs = [slice(h * SWA_HD, (h + 1) * SWA_HD) for h in heads]
            kv = [slice((h // SWA_GROUP) * SWA_HD, (h // SWA_GROUP + 1) * SWA_HD) for h in heads]
            s = [jnp.where(mask, _mm_nt(q_ref[rs, hs[h]], kall[:, kv[h]]), NEG) for h in heads]
            m = [jnp.maximum(jnp.max(s[h], axis=-1, keepdims=True), sink_ref[0, h]) for h in heads]
            p = [jnp.exp(s[h] - m[h]) for h in heads]
            den = [jnp.sum(p[h], axis=-1, keepdims=True) + jnp.exp(sink_ref[0, h] - m[h]) for h in heads]
            o = [_mm(p[h], vall[:, kv[h]]) for h in heads]
            for h in heads:
                o_ref[rs, hs[h]] = (o[h] / den[h]).astype(ACT_DTYPE)

    blk, first, prev = _swa_specs(group)
    outs = pl.pallas_call(
        body, name="swa_fwd", grid=(steps,),
        in_specs=[blk(512), first, prev, blk(128), first, prev, blk(128),
                  pl.BlockSpec(memory_space=pltpu.SMEM)] + [ANY] * ns,
        out_specs=[blk(512)] + [ANY] * ns,
        out_shape=[jax.ShapeDtypeStruct((rows, 512), ACT_DTYPE)] + _gathered_shapes(shards),
        scratch_shapes=_gather_sems(ns),
        compiler_params=_cp(("arbitrary",)),
    )(qr, kr, kr, kr, vr, vr, vr, sinks, *shards)
    return outs[0], _with_own_block(outs[1:], shards)


def _out_proj(x, lead, og, osw, wout, nfw, tm):
    rows = LEAD + x.shape[0]
    nb = tm // LEAD

    def body(*refs):
        x_refs, (lead_ref, og_ref, os_ref, w_ref, nw_ref, h1_ref, f_ref, ft_ref) = refs[:nb], refs[nb:]
        h0 = _h_tile(pl.program_id(0), lead_ref, x_refs)
        h1 = h0 + _mm(og_ref[...], w_ref[0:512, :]) + _mm(os_ref[...], w_ref[512:1024, :])
        h1_ref[...] = h1
        rstd = lax.rsqrt(jnp.mean(h1 * h1, axis=-1, keepdims=True) + EPS)
        f = h1 * rstd * nw_ref[...]
        f_ref[...] = f.astype(ACT_DTYPE)
        ft_ref[...] = f.T.astype(ACT_DTYPE)

    row = lambda w: pl.BlockSpec((tm, w), lambda i: (i, 0))
    return pl.pallas_call(
        body, name="out_proj", grid=(rows // tm,),
        in_specs=_token_specs(tm) + [pl.BlockSpec((LEAD, D), lambda i: (0, 0)), row(512), row(512),
                                     pl.BlockSpec((D, D), lambda i: (0, 0)), pl.BlockSpec((1, D), lambda i: (0, 0))],
        out_specs=[row(D), row(D), pl.BlockSpec((D, tm), lambda i: (0, i))],
        out_shape=[jax.ShapeDtypeStruct((rows, D), F32), jax.ShapeDtypeStruct((rows, D), ACT_DTYPE),
                   jax.ShapeDtypeStruct((D, rows), ACT_DTYPE)],
        compiler_params=_cp(("arbitrary",), VMEM_TILE_MB),
    )(*([x] * nb), lead, og, osw, wout, nfw)


def _ffn_fwd(f, h1, w1, w2, tgt, fnw, tm):
    rows = f.shape[0]
    nj = D_FF // FF_WIDE
    nb = tm // LEAD

    def body(f_ref, h1_ref, w1_ref, w2_ref, nw_ref, *rest):
        t_refs, (a_ref, dh2_ref, dh2t_ref, loss_ref, gfn_ref, acc) = rest[:nb], rest[nb:]
        i, j = pl.program_id(0), pl.program_id(1)

        @pl.when((i == 0) & (j == 0))
        def _():
            loss_ref[...] = jnp.zeros_like(loss_ref)
            gfn_ref[...] = jnp.zeros_like(gfn_ref)

        @pl.when(j == 0)
        def _():
            acc[...] = jnp.zeros_like(acc)

        a = _mm(f_ref[...], w1_ref[...])
        a_ref[...] = a.astype(ACT_DTYPE)
        z = jnp.square(jnp.maximum(a, 0.0))
        acc[...] += _mm(z, w2_ref[...])

        @pl.when(j == nj - 1)
        def _():
            h2 = h1_ref[...] + acc[...]
            rstd = lax.rsqrt(jnp.mean(h2 * h2, axis=-1, keepdims=True) + EPS)
            hn = h2 * rstd
            nw = nw_ref[...]
            row = i * tm + lax.broadcasted_iota(jnp.int32, (tm, 1), 0)
            target = jnp.concatenate([t[...] for t in t_refs], axis=0)
            err = jnp.where(row >= LEAD, hn * nw - target, 0.0)
            row_loss = jnp.sum(err * err, axis=-1, keepdims=True) * (1.0 / D)
            loss_ref[...] += jnp.broadcast_to(0.5 * jnp.sum(row_loss, axis=0, keepdims=True), loss_ref.shape)
            dy = err * (1.0 / D)
            gfn_ref[...] += jnp.broadcast_to(jnp.sum(dy * hn, axis=0, keepdims=True), gfn_ref.shape)
            dhn = dy * nw
            dh2 = rstd * (dhn - hn * jnp.mean(dhn * hn, axis=-1, keepdims=True))
            dh2_ref[...] = dh2
            dh2t_ref[...] = dh2.T.astype(ACT_DTYPE)

    return pl.pallas_call(
        body, name="ffn_fwd", grid=(rows // tm, nj),
        in_specs=[pl.BlockSpec((tm, D), lambda i, j: (i, 0)), pl.BlockSpec((tm, D), lambda i, j: (i, 0)),
                  pl.BlockSpec((D, FF_WIDE), lambda i, j: (0, j)),
                  pl.BlockSpec((FF_WIDE, D), lambda i, j: (j, 0)),
                  pl.BlockSpec((1, D), lambda i, j: (0, 0))] + _token_specs(tm, grid_rank=2),
        out_specs=[pl.BlockSpec((tm, FF_WIDE), lambda i, j: (i, j)), pl.BlockSpec((tm, D), lambda i, j: (i, 0)),
                   pl.BlockSpec((D, tm), lambda i, j: (0, i)),
                   pl.BlockSpec((8, 128), lambda i, j: (0, 0)), pl.BlockSpec((8, D), lambda i, j: (0, 0))],
        out_shape=[jax.ShapeDtypeStruct((rows, D_FF), ACT_DTYPE), jax.ShapeDtypeStruct((rows, D), F32),
                   jax.ShapeDtypeStruct((D, rows), ACT_DTYPE),
                   jax.ShapeDtypeStruct((8, 128), F32), jax.ShapeDtypeStruct((8, D), F32)],
        scratch_shapes=[pltpu.VMEM((tm, D), F32)],
        compiler_params=_cp(("arbitrary", "arbitrary"), VMEM_WIDE_MB),
    )(f, h1, w1, w2, fnw, *([tgt] * nb))


def _ffn_bwd_act(dh2, a, w1, w2, h1, nfw, tm):
    rows = dh2.shape[0]
    nj = D_FF // FF_WIDE

    def body(dh2_ref, a_ref, w1_ref, w2_ref, h1_ref, nw_ref, da_ref, dh1_ref, gnf_ref, acc):
        i, j = pl.program_id(0), pl.program_id(1)

        @pl.when((i == 0) & (j == 0))
        def _():
            gnf_ref[...] = jnp.zeros_like(gnf_ref)

        @pl.when(j == 0)
        def _():
            acc[...] = jnp.zeros_like(acc)

        dz = _mm_nt(dh2_ref[...], w2_ref[...])
        da = dz * (2.0 * jnp.maximum(a_ref[...].astype(F32), 0.0))
        da_ref[...] = da.astype(ACT_DTYPE)
        acc[...] += _mm_nt(da, w1_ref[...])

        @pl.when(j == nj - 1)
        def _():
            h1 = h1_ref[...]
            rstd = lax.rsqrt(jnp.mean(h1 * h1, axis=-1, keepdims=True) + EPS)
            hn = h1 * rstd
            df = acc[...]
            gnf_ref[...] += jnp.broadcast_to(jnp.sum(df * hn, axis=0, keepdims=True), gnf_ref.shape)
            dfn = df * nw_ref[...]
            dh1_ref[...] = dh2_ref[...] + rstd * (dfn - hn * jnp.mean(dfn * hn, axis=-1, keepdims=True))

    return pl.pallas_call(
        body, name="ffn_bwd_act", grid=(rows // tm, nj),
        in_specs=[pl.BlockSpec((tm, D), lambda i, j: (i, 0)), pl.BlockSpec((tm, FF_WIDE), lambda i, j: (i, j)),
                  pl.BlockSpec((D, FF_WIDE), lambda i, j: (0, j)),
                  pl.BlockSpec((FF_WIDE, D), lambda i, j: (j, 0)),
                  pl.BlockSpec((tm, D), lambda i, j: (i, 0)), pl.BlockSpec((1, D), lambda i, j: (0, 0))],
        out_specs=[pl.BlockSpec((tm, FF_WIDE), lambda i, j: (i, j)), pl.BlockSpec((tm, D), lambda i, j: (i, 0)),
                   pl.BlockSpec((8, D), lambda i, j: (0, 0))],
        out_shape=[jax.ShapeDtypeStruct((rows, D_FF), ACT_DTYPE), jax.ShapeDtypeStruct((rows, D), F32),
                   jax.ShapeDtypeStruct((8, D), F32)],
        scratch_shapes=[pltpu.VMEM((tm, D), F32)],
        compiler_params=_cp(("arbitrary", "arbitrary"), VMEM_WIDE_MB),
    )(dh2, a, w1, w2, h1, nfw)


def _ffn_bwd_weights(ft, a, da, dh2t, tm):
    rows = a.shape[0]
    steps = rows // tm
    pair = 2 * FF_TILE

    def body(ft_ref, a_ref, da_ref, dh2t_ref, dw1_ref, dw2_ref, dw2t):
        i = pl.program_id(1)

        @pl.when(i == 0)
        def _():
            dw1_ref[...] = jnp.zeros_like(dw1_ref)
            dw2t[...] = jnp.zeros_like(dw2t)

        z = jnp.square(jnp.maximum(a_ref[...].astype(F32), 0.0))
        dw1 = _mm(ft_ref[...], da_ref[...])
        for core in range(2):
            dw1_ref[core] += dw1[:, core * FF_TILE:(core + 1) * FF_TILE]
        dw2t[...] += _mm(dh2t_ref[...], z)

        @pl.when(i == steps - 1)
        def _():
            for core in range(2):
                dw2_ref[core] = dw2t[:, core * FF_TILE:(core + 1) * FF_TILE].T

    return pl.pallas_call(
        body, name="ffn_bwd_weights", grid=(N_DEV // 2, steps),
        in_specs=[pl.BlockSpec((D, tm), lambda j, i: (0, i)), pl.BlockSpec((tm, pair), lambda j, i: (i, j)),
                  pl.BlockSpec((tm, pair), lambda j, i: (i, j)), pl.BlockSpec((D, tm), lambda j, i: (0, i))],
        out_specs=[pl.BlockSpec((2, None, D, FF_TILE), lambda j, i: (0, j, 0, 0)),
                   pl.BlockSpec((2, None, FF_TILE, D), lambda j, i: (0, j, 0, 0))],
        out_shape=[jax.ShapeDtypeStruct((2, 4, D, FF_TILE), F32), jax.ShapeDtypeStruct((2, 4, FF_TILE, D), F32)],
        scratch_shapes=[pltpu.VMEM((D, pair), F32)],
        compiler_params=_cp(("arbitrary", "arbitrary"), VMEM_WIDE_MB),
    )(ft, a, da, dh2t)


def _out_proj_bwd(dh1, og, osw, wout, tm, partials):
    rows = dh1.shape[0]
    steps = rows // tm
    ns = len(partials)

    def body(dh1_ref, og_ref, os_ref, w_ref, *rest):
        part_refs, rest = rest[:ns], rest[ns:]
        dog_ref, dos_ref, dw_ref = rest[:3]
        land_refs, (send_sems, recv_sems) = rest[3:3 + ns], rest[3 + ns:]
        i = pl.program_id(0)
        start, finish = _sibling_schedule(part_refs, land_refs, send_sems, recv_sems)

        @pl.when(i == 0)
        def _():
            dw_ref[...] = jnp.zeros_like(dw_ref)
            start()

        pl.when(i == steps - 1)(finish)

        dh1 = dh1_ref[...].astype(MXU_DTYPE)
        dog_ref[...] = _mm_nt(dh1, w_ref[0:512, :])
        dos_ref[...] = _mm_nt(dh1, w_ref[512:1024, :])
        for half, ref in enumerate((og_ref, os_ref)):
            dw = _mm_tn(ref[...], dh1)
            for blk in range(4):
                shard = half * 4 + blk
                dw_ref[shard % 2, shard // 2] += dw[blk * 128:(blk + 1) * 128, :]

    row = lambda w: pl.BlockSpec((tm, w), lambda i: (i, 0))
    outs = pl.pallas_call(
        body, name="out_proj_bwd", grid=(steps,),
        in_specs=[row(D), row(512), row(512), pl.BlockSpec((D, D), lambda i: (0, 0))] + [ANY] * ns,
        out_specs=[row(512), row(512), pl.BlockSpec((2, 4, 128, D), lambda i: (0, 0, 0, 0))] + [ANY] * ns,
        out_shape=[jax.ShapeDtypeStruct((rows, 512), F32), jax.ShapeDtypeStruct((rows, 512), F32),
                   jax.ShapeDtypeStruct((2, 4, 128, D), F32)] + _sibling_shapes(partials),
        scratch_shapes=_sibling_sems(ns),
        compiler_params=_cp(("arbitrary",), VMEM_TILE_MB),
    )(dh1, og, osw, wout, *partials)
    return outs[0], outs[1], outs[2], outs[3:]


def _swa_bwd(qr, kr, vr, osw, dos, sinks, jobs):
    rows = qr.shape[0]
    nblk = rows // SWA_BLOCK
    group = _swa_group(nblk)
    steps = nblk // group
    ns = jobs.n

    def body(q_ref, k0, kp, kc, v0, vp, vc, o_ref, do_ref, sink_ref, *rest):
        dq_ref, dk_ref, dv_ref, dsink_ref = rest[ns:ns + 4]
        start, finish = jobs.bind(rest[:ns], rest[ns + 4:2 * ns + 4], rest[2 * ns + 4:])
        step = pl.program_id(0)

        @pl.when(step == 0)
        def _():
            dk_ref[...] = jnp.zeros_like(dk_ref)
            dv_ref[...] = jnp.zeros_like(dv_ref)
            dsink_ref[...] = jnp.zeros_like(dsink_ref)
            start()

        pl.when(step == steps - 1)(finish)
        for g in range(group):
            block(step * group + g, g, q_ref, k0, kp, kc, v0, vp, vc, o_ref, do_ref, sink_ref,
                  dq_ref, dk_ref, dv_ref, dsink_ref)

    def block(n, g, q_ref, k0, kp, kc, v0, vp, vc, o_ref, do_ref, sink_ref, dq_ref, dk_ref, dv_ref, dsink_ref):
        rs = slice(g * SWA_BLOCK, (g + 1) * SWA_BLOCK)
        kall, vall = _swa_keys(k0, kp, kc, g), _swa_keys(v0, vp, vc, g)
        mask = _swa_mask(n)[0:SWA_BLOCK]
        heads = range(SWA_HEADS)
        hs = [slice(h * SWA_HD, (h + 1) * SWA_HD) for h in heads]
        kv = [slice((h // SWA_GROUP) * SWA_HD, (h // SWA_GROUP + 1) * SWA_HD) for h in heads]
        sink = [sink_ref[0, h] for h in heads]
        qh = [q_ref[rs, hs[h]] for h in heads]
        doh = [do_ref[rs, hs[h]] for h in heads]
        s = [jnp.where(mask, _mm_nt(qh[h], kall[:, kv[h]]), NEG) for h in heads]
        dp = [_mm_nt(doh[h], vall[:, kv[h]]) for h in heads]
        delta = [jnp.sum(doh[h] * o_ref[rs, hs[h]].astype(F32), axis=-1, keepdims=True) for h in heads]
        m = [jnp.maximum(jnp.max(s[h], axis=-1, keepdims=True), sink[h]) for h in heads]
        e = [jnp.exp(s[h] - m[h]) for h in heads]
        inv = [1.0 / (jnp.sum(e[h], axis=-1, keepdims=True) + jnp.exp(sink[h] - m[h])) for h in heads]
        p = [e[h] * inv[h] for h in heads]
        ds = [p[h] * (dp[h] - delta[h]) for h in heads]
        dq = [_mm(ds[h], kall[:, kv[h]]) for h in heads]
        dkh = [_mm_tn(ds[h], qh[h]) for h in heads]
        dvh = [_mm_tn(p[h], doh[h]) for h in heads]
        for h in heads:
            dsink = -jnp.sum(jnp.exp(sink[h] - m[h]) * inv[h] * delta[h], axis=0, keepdims=True)
            dsink_ref[h:h + 1, :] += jnp.broadcast_to(dsink, (1, 128))
        dq_ref[rs, :] = jnp.concatenate(dq, axis=1)
        group_sum = lambda parts, kvh: sum(parts[kvh * SWA_GROUP + 1:(kvh + 1) * SWA_GROUP], parts[kvh * SWA_GROUP])
        dk_all = jnp.concatenate([group_sum(dkh, kvh) for kvh in range(SWA_KV)], axis=1)
        dv_all = jnp.concatenate([group_sum(dvh, kvh) for kvh in range(SWA_KV)], axis=1)
        prev0 = pl.multiple_of(jnp.maximum(n - 1, 0) * SWA_BLOCK, SWA_BLOCK)
        cur0 = pl.multiple_of(n * SWA_BLOCK, SWA_BLOCK)
        for ref, val in ((dk_ref, dk_all), (dv_ref, dv_all)):
            ref[0:SWA_BLOCK, :] += val[0:SWA_BLOCK]
            ref[pl.ds(prev0, SWA_BLOCK), :] += val[SWA_BLOCK:2 * SWA_BLOCK]
            ref[pl.ds(cur0, SWA_BLOCK), :] += val[2 * SWA_BLOCK:]

    blk, first, prev = _swa_specs(group)
    whole = pl.BlockSpec((rows, 128), lambda n: (0, 0))
    outs = pl.pallas_call(
        body, name="swa_bwd", grid=(steps,),
        in_specs=[blk(512), first, prev, blk(128), first, prev, blk(128), blk(512), blk(512),
                  pl.BlockSpec(memory_space=pltpu.SMEM)] + [ANY] * ns,
        out_specs=[blk(512), whole, whole, pl.BlockSpec((8, 128), lambda n: (0, 0))] + [ANY] * ns,
        out_shape=[jax.ShapeDtypeStruct((rows, 512), F32), jax.ShapeDtypeStruct((rows, 128), F32),
                   jax.ShapeDtypeStruct((rows, 128), F32), jax.ShapeDtypeStruct((8, 128), F32)] + jobs.out_shapes,
        scratch_shapes=jobs.sems,
        compiler_params=_cp(("arbitrary",), VMEM_TILE_MB),
    )(qr, kr, kr, kr, vr, vr, vr, osw, dos, sinks, *jobs.inputs)
    return outs[0], outs[1], outs[2], outs[3], jobs.split(outs[4:])


def _gla_bwd(proj, oraw, states, dog, wg_p, bg, gnw, jobs):
    rows = proj.shape[0]
    nc = rows // GLA_CHUNK
    group = _gla_group(nc)
    steps, nrows = nc // group, group * GLA_CHUNK
    ns = jobs.n

    def body(q_ref, k_ref, v_ref, r_ref, lr_ref, oraw_ref, st_ref, dog_ref, wg_ref, bg_ref, gnw_ref, *rest):
        dq_ref, dk_ref, dv_ref, dr_ref, dlr_ref, dwg_ref, dbg_ref, dgnw_ref = rest[ns:ns + 8]
        dstate, db_scr = rest[2 * ns + 8:2 * ns + 10]
        start, finish = jobs.bind(rest[:ns], rest[ns + 8:2 * ns + 8], rest[2 * ns + 10:])
        t = pl.program_id(0)
        c = steps - 1 - t

        @pl.when(t == 0)
        def _():
            dstate[...] = jnp.zeros_like(dstate)
            dwg_ref[...] = jnp.zeros_like(dwg_ref)
            dbg_ref[...] = jnp.zeros_like(dbg_ref)
            dgnw_ref[...] = jnp.zeros_like(dgnw_ref)
            start()

        pl.when(t == steps - 1)(finish)

        lr, wg = lr_ref[...], wg_ref[...]
        zg, live, _, upper, b = _gla_gates(lr, wg, bg_ref[...], c * nrows, nrows)
        eb, enb = jnp.exp(b), jnp.exp(-b)
        scale = GLA_DK ** -0.5
        gq = q_ref[...] * scale * eb
        gk = k_ref[...] * enb
        v = v_ref[...]
        gnw_v = gnw_ref[...]
        tril = _tril64()
        is_last = lax.broadcasted_iota(jnp.int32, (GLA_CHUNK, 1), 0) == GLA_CHUNK - 1
        dgnw = jnp.zeros((1, GLA_DV), F32)
        pairs = [(h, gi) for h in range(GLA_HEADS) for gi in range(group)]
        rs = {gi: slice(gi * GLA_CHUNK, (gi + 1) * GLA_CHUNK) for gi in range(group)}
        s64 = {h: slice(h * GLA_DK, (h + 1) * GLA_DK) for h in range(GLA_HEADS)}
        s128 = {h: slice(h * GLA_DV, (h + 1) * GLA_DV) for h in range(GLA_HEADS)}
        qh = {(h, gi): gq[rs[gi], s64[h]] for h, gi in pairs}
        kh = {(h, gi): gk[rs[gi], s64[h]] for h, gi in pairs}
        vh = {(h, gi): v[rs[gi], s128[h]] for h, gi in pairs}
        ebl = {(h, gi): eb[(gi + 1) * GLA_CHUNK - 1:(gi + 1) * GLA_CHUNK, s64[h]] for h, gi in pairs}
        kl = {pr: kh[pr] * ebl[pr] for pr in pairs}
        st = {(h, gi): st_ref[gi, h] for h, gi in pairs}
        do = {}
        for h, gi in pairs:
            o, rh, dout = oraw_ref[rs[gi], s128[h]], r_ref[rs[gi], s128[h]], dog_ref[rs[gi], s128[h]]
            rstd = lax.rsqrt(jnp.mean(o * o, axis=-1, keepdims=True) + EPS)
            on = o * rstd
            sg = _sigmoid(rh)
            dr_ref[rs[gi], s128[h]] = (dout * (on * gnw_v) * (sg * (1.0 + rh * (1.0 - sg)))).astype(ACT_DTYPE)
            dy = dout * (rh * sg)
            dgnw = dgnw + jnp.sum(dy * on, axis=0, keepdims=True)
            don = dy * gnw_v
            do[h, gi] = rstd * (don - on * jnp.mean(don * on, axis=-1, keepdims=True))
        a = {pr: jnp.where(tril, _mm_nt(qh[pr], kh[pr]), 0.0) for pr in pairs}
        da = {pr: jnp.where(tril, _mm_nt(do[pr], vh[pr]), 0.0) for pr in pairs}
        dinc = {pr: _mm_tn(do[pr], qh[pr]) for pr in pairs}
        dgq = {pr: _mm(da[pr], kh[pr]) + _mm(do[pr], st[pr]) for pr in pairs}
        dgk = {pr: _mm_tn(da[pr], qh[pr]) for pr in pairs}
        dv_a = {pr: _mm_tn(a[pr], do[pr]) for pr in pairs}
        dsp = {}
        for h in range(GLA_HEADS):
            cur = dstate[h]
            for gi in reversed(range(group)):
                dsp[h, gi] = cur
                cur = cur * ebl[h, gi] + dinc[h, gi]
            dstate[h] = cur
        for h, gi in pairs:
            pr = (h, gi)
            dkl = _mm(vh[pr], dsp[pr])
            dv_ref[rs[gi], s128[h]] = (dv_a[pr] + _mm_nt(kl[pr], dsp[pr])).astype(ACT_DTYPE)
            debl = jnp.sum(dsp[pr] * st[pr], axis=0, keepdims=True)
            dq_ref[rs[gi], s64[h]] = (dgq[pr] * (scale * eb[rs[gi], s64[h]])).astype(ACT_DTYPE)
            dk_ref[rs[gi], s64[h]] = ((dgk[pr] + dkl * ebl[pr]) * enb[rs[gi], s64[h]]).astype(ACT_DTYPE)
            last = debl * ebl[pr] + jnp.sum(dkl * kl[pr], axis=0, keepdims=True)
            db_scr[rs[gi], s64[h]] = (dgq[pr] * qh[pr] - dgk[pr] * kh[pr] - dkl * kl[pr]
                                      + jnp.where(is_last, last, 0.0))
        dg = jnp.dot(upper.astype(F32), db_scr[...], precision=HIGHEST, preferred_element_type=F32)
        dzg = jnp.where(live, dg * _sigmoid(-zg) * (1.0 / GLA_TAU), 0.0)
        dlr_ref[...] = _mm_nt(dzg, wg).astype(ACT_DTYPE)
        dwg_ref[...] += _mm_tn(lr, dzg)
        dbg_ref[...] += jnp.broadcast_to(jnp.sum(dzg, axis=0, keepdims=True), dbg_ref.shape)
        dgnw_ref[...] += jnp.broadcast_to(dgnw, dgnw_ref.shape)

    nb = lambda w, col: pl.BlockSpec((nrows, w), lambda t: (steps - 1 - t, col // w))
    const = lambda shape: pl.BlockSpec(shape, lambda t: (0,) * len(shape))
    outs = pl.pallas_call(
        body, name="gla_bwd", grid=(steps,),
        in_specs=[nb(256, C_GQ), nb(256, C_GK), nb(512, C_GV), nb(512, C_GR), nb(128, C_LR), nb(512, 0),
                  pl.BlockSpec((group, GLA_HEADS, GLA_DV, GLA_DK), lambda t: (steps - 1 - t, 0, 0, 0)), nb(512, 0),
                  const((128, 256)), const((1, 256)), const((1, 128))] + [ANY] * ns,
        out_specs=[nb(256, 0), nb(256, 0), nb(512, 0), nb(512, 0), nb(128, 0),
                   const((128, 256)), const((8, 256)), const((8, 128))] + [ANY] * ns,
        out_shape=[jax.ShapeDtypeStruct((rows, 256), ACT_DTYPE), jax.ShapeDtypeStruct((rows, 256), ACT_DTYPE),
                   jax.ShapeDtypeStruct((rows, 512), ACT_DTYPE), jax.ShapeDtypeStruct((rows, 512), ACT_DTYPE),
                   jax.ShapeDtypeStruct((rows, 128), ACT_DTYPE), jax.ShapeDtypeStruct((128, 256), F32),
                   jax.ShapeDtypeStruct((8, 256), F32), jax.ShapeDtypeStruct((8, 128), F32)] + jobs.out_shapes,
        scratch_shapes=[pltpu.VMEM((GLA_HEADS, GLA_DV, GLA_DK), F32), pltpu.VMEM((nrows, 256), F32)] + jobs.sems,
        compiler_params=_cp(("arbitrary",)),
    )(proj, proj, proj, proj, proj, oraw, states, dog, wg_p, bg, gnw, *jobs.inputs)
    return outs[:8], jobs.split(outs[8:])


def _in_proj_bwd(x, lead, dh1, nw, win_p, dgv, dgr, dsq, dgq, dgk, dsk, dsv, dlr, tabs, tm):
    seq = x.shape[0]
    rows = LEAD + seq
    nb = tm // LEAD
    steps = rows // tm

    def first_copy(scr, gx_ref, sem):
        return pltpu.make_async_copy(scr.at[pl.ds(LEAD, tm - LEAD)], gx_ref.at[pl.ds(0, tm - LEAD)], sem)

    def tile_copy(scr, gx_ref, sem, step):
        start = pl.multiple_of(jnp.maximum(step * tm - LEAD, 0), LEAD)
        return pltpu.make_async_copy(scr, gx_ref.at[pl.ds(start, tm)], sem)

    def body(*refs):
        x_refs, refs = refs[:nb], refs[nb:]
        (lead_ref, dh1_ref, nw_ref, w_ref, dgv_ref, dgr_ref, dsq_ref, dgq_ref, dgk_ref, dsk_ref, dsv_ref, dlr_ref,
         c_ref, sa_ref, sb_ref, gx_ref, dlead_ref, dproj_ref, ut_ref, gnm_ref, scr, sem) = refs
        i = pl.program_id(0)

        @pl.when(i == 0)
        def _():
            gnm_ref[...] = jnp.zeros_like(gnm_ref)

        cos, sa, sb = c_ref[...], sa_ref[...], sb_ref[...]
        dsq_v = (_unrope(dsq_ref[...], cos, sa, sb) * (SWA_HD ** -0.5)).astype(MXU_DTYPE)
        dsk_v = _unrope(dsk_ref[...], cos, sa, sb).astype(MXU_DTYPE)
        dproj = jnp.concatenate(
            [dgv_ref[...].astype(MXU_DTYPE), dgr_ref[...].astype(MXU_DTYPE), dgq_ref[...].astype(MXU_DTYPE),
             dgk_ref[...].astype(MXU_DTYPE), dlr_ref[...].astype(MXU_DTYPE), dsq_v, dsk_v,
             dsv_ref[...].astype(MXU_DTYPE)],
            axis=1)
        dproj_ref[...] = dproj
        h = _h_tile(i, lead_ref, x_refs)
        rstd = lax.rsqrt(jnp.mean(h * h, axis=-1, keepdims=True) + EPS)
        hn = h * rstd
        nw_v = nw_ref[...]
        ut_ref[...] = (hn * nw_v).T.astype(ACT_DTYPE)
        du = _mm_nt(dproj, w_ref[...])
        gnm_ref[...] += jnp.broadcast_to(jnp.sum(du * hn, axis=0, keepdims=True), gnm_ref.shape)
        dun = du * nw_v
        dh0 = dh1_ref[...] + rstd * (dun - hn * jnp.mean(dun * hn, axis=-1, keepdims=True))

        if tm > LEAD:
            pl.when(i == 1)(lambda: first_copy(scr, gx_ref, sem).wait())
        pl.when(i > 1)(lambda: tile_copy(scr, gx_ref, sem, i).wait())
        scr[...] = dh0

        @pl.when(i == 0)
        def _():
            dlead_ref[...] = dh0[0:LEAD]
            if tm > LEAD:
                first_copy(scr, gx_ref, sem).start()
                if steps == 1:
                    first_copy(scr, gx_ref, sem).wait()

        @pl.when(i > 0)
        def _():
            tile_copy(scr, gx_ref, sem, i).start()

        if steps > 1:
            pl.when(i == steps - 1)(lambda: tile_copy(scr, gx_ref, sem, i).wait())

    row = lambda w: pl.BlockSpec((tm, w), lambda i: (i, 0))
    const = lambda shape: pl.BlockSpec(shape, lambda i: (0,) * len(shape))
    return pl.pallas_call(
        body, name="in_proj_bwd", grid=(steps,),
        in_specs=_token_specs(tm) + [const((LEAD, D)), row(D), const((1, D)), const((D, DINP)),
                                     row(512), row(512), row(512), row(256), row(256), row(128), row(128), row(128),
                                     row(128), row(128), row(128)],
        out_specs=[ANY, const((LEAD, D)), row(DINP), pl.BlockSpec((D, tm), lambda i: (0, i)), const((8, D))],
        out_shape=[jax.ShapeDtypeStruct((seq, D), F32), jax.ShapeDtypeStruct((LEAD, D), F32),
                   jax.ShapeDtypeStruct((rows, DINP), ACT_DTYPE), jax.ShapeDtypeStruct((D, rows), ACT_DTYPE),
                   jax.ShapeDtypeStruct((8, D), F32)],
        scratch_shapes=[pltpu.VMEM((tm, D), F32), pltpu.SemaphoreType.DMA],
        compiler_params=_cp(("arbitrary",), VMEM_WIDE_MB),
    )(*([x] * nb), lead, dh1, nw, win_p, dgv, dgr, dsq, dgq, dgk, dsk, dsv, dlr, *tabs)


def _win_runs():
    groups = [(O_GQ, C_GQ), (O_GK, C_GK), (O_GV, C_GV), (O_GR, C_GR), (O_LR, C_LR), (O_SQ, C_SQ), (O_SK, C_SK),
              (O_SV, C_SV)]
    per = DIN // N_DEV
    runs = []
    for (o0, o1), c0 in groups:
        o = o0
        while o < o1:
            d = o // per
            end = min(o1, (d + 1) * per)
            runs.append((d, o - d * per, c0 + o - o0, end - o))
            o = end
    return runs


def _win_padded(g_in):
    tr = 128

    def body(g_ref, o_ref):
        o_ref[...] = jnp.zeros_like(o_ref)
        for d, s, c, w in _win_runs():
            o_ref[:, c:c + w] = g_ref[d, :, s:s + w]

    return pl.pallas_call(
        body, name="w_in_layout", grid=(D // tr,),
        in_specs=[pl.BlockSpec((N_DEV, tr, DIN // N_DEV), lambda i: (0, i, 0))],
        out_specs=pl.BlockSpec((tr, DINP), lambda i: (i, 0)),
        out_shape=jax.ShapeDtypeStruct((D, DINP), g_in.dtype),
        compiler_params=_cp(("arbitrary",)),
    )(g_in)


def _in_proj_bwd_weights(ut, dproj, tm, small):
    rows = dproj.shape[0]
    steps = rows // tm
    per = DIN // N_DEV

    def body(ut_ref, dp_ref, *rest):
        small_refs, (mine_ref, theirs_ref, total_ref, acc, stage, local_sems, send_sems, recv_sems) = rest[:9], rest[9:17]
        i = pl.program_id(0)
        start, finish = _small_sum_schedule(small_refs, total_ref, *rest[17:])
        x, y, c = _mesh_pos()

        @pl.when(i == 0)
        def _():
            acc[...] = jnp.zeros_like(acc)
            start()

        acc[...] += _mm(ut_ref[...], dp_ref[...])
        pl.when(i == steps - 1)(finish)

        def keep(slot, chip):
            return pltpu.make_async_copy(stage.at[slot], mine_ref.at[chip], local_sems.at[slot])

        def send(slot, chip):
            return pltpu.make_async_remote_copy(
                src_ref=stage.at[slot], dst_ref=theirs_ref.at[chip], send_sem=send_sems.at[slot],
                recv_sem=recv_sems.at[chip], device_id=(x, y, 1 - c), device_id_type=MESH)

        def drained(d):
            pl.when(c == d % 2)(keep(d % 2, d // 2).wait)
            pl.when(c != d % 2)(send(d % 2, d // 2).wait_send)

        @pl.when(i == steps - 1)
        def _():
            for d in range(N_DEV):
                slot, chip = d % 2, d // 2
                if d >= 2:
                    drained(d - 2)
                for owner, s, col, w in _win_runs():
                    if owner == d:
                        stage[slot, :, s:s + w] = acc[:, col:col + w]
                pl.when(c == slot)(keep(slot, chip).start)
                pl.when(c != slot)(send(slot, chip).start)
            drained(N_DEV - 2)
            drained(N_DEV - 1)
            for chip in range(4):
                send(0, chip).wait_recv()

    half = jax.ShapeDtypeStruct((4, D, per), F32)
    return pl.pallas_call(
        body, name="in_proj_bwd_weights", grid=(steps,),
        in_specs=[pl.BlockSpec((D, tm), lambda i: (0, i)), pl.BlockSpec((tm, DINP), lambda i: (i, 0))] + SMALL_SPECS,
        out_specs=[ANY, ANY, pl.BlockSpec((SMALL_ROWS, D), lambda i: (0, 0))],
        out_shape=[half, half, jax.ShapeDtypeStruct((SMALL_ROWS, D), F32)],
        scratch_shapes=[pltpu.VMEM((D, DINP), F32), pltpu.VMEM((2, D, per), F32), pltpu.SemaphoreType.DMA((2,)),
                        pltpu.SemaphoreType.DMA((2,)), pltpu.SemaphoreType.DMA((4,))] + _small_sum_scratch(),
        compiler_params=_cp(("arbitrary",), VMEM_WIDE_MB),
    )(ut, dproj, *small)


def _adamw(w, g, m, v):
    m = ADAM_B1 * m + (1.0 - ADAM_B1) * g
    v = ADAM_B2 * v + (1.0 - ADAM_B2) * jnp.square(g)
    m_hat = m / (1.0 - ADAM_B1 ** ADAM_STEP)
    v_hat = v / (1.0 - ADAM_B2 ** ADAM_STEP)
    delta = -ADAM_LR * (m_hat / (jnp.sqrt(v_hat) + ADAM_EPS) + ADAM_WD * w)
    return delta, m, v


ADAM_STEPS = 8


def _adamw_shards(where, items, name, jobs=None):
    jobs = jobs or _Jobs([])
    ns, nw = jobs.n, len(items)

    def body(where_ref, *rest):
        ins, rest = rest[:5 * nw], rest[5 * nw:]
        job_ins, rest = rest[:ns], rest[ns:]
        outs, rest = rest[:4 * nw], rest[4 * nw:]
        start, finish = jobs.bind(job_ins, rest[:ns], rest[ns:])
        i = pl.program_id(0)
        pl.when(i == 0)(start)
        pl.when(i == ADAM_STEPS - 1)(finish)
        for k in range(nw):
            p_ref, own_ref, w_ref, m_ref, v_ref = ins[5 * k:5 * k + 5]
            g_ref, d_ref, nm_ref, nv_ref = outs[4 * k:4 * k + 4]
            g = ((p_ref[0].astype(F32) + p_ref[1].astype(F32)) + p_ref[2].astype(F32)) + own_ref[...]
            g_ref[...] = g
            d_ref[...], nm_ref[...], nv_ref[...] = _adamw(w_ref[...], g, m_ref[...], v_ref[...])

    in_specs, out_specs, out_shape, operands = [], [], [], []
    for parts, own, w, m, v in items:
        r, cdim = w.shape
        tr = r // ADAM_STEPS
        spec = pl.BlockSpec((tr, cdim), lambda i, s: (i, 0))
        in_specs += [pl.BlockSpec((3, tr, cdim), lambda i, s: (0, i, 0)),
                     pl.BlockSpec((None, tr, cdim), lambda i, s: (s[1], i, 0)), spec, spec, spec]
        out_specs += [spec] * 4
        out_shape += [jax.ShapeDtypeStruct((r, cdim), F32)] * 4
        operands += [parts, own, w, m, v]
    outs = pl.pallas_call(
        body, name=name,
        grid_spec=pltpu.PrefetchScalarGridSpec(
            num_scalar_prefetch=1, grid=(ADAM_STEPS,),
            in_specs=in_specs + [ANY] * ns, out_specs=out_specs + [ANY] * ns, scratch_shapes=jobs.sems),
        out_shape=out_shape + jobs.out_shapes,
        compiler_params=_cp(("arbitrary",)),
    )(where, *operands, *jobs.inputs)
    return [outs[4 * k:4 * k + 4] for k in range(nw)], jobs.split(outs[4 * nw:])


def _adamw_small(items):
    n = len(items)

    def body(*refs):
        ins, outs = refs[:4 * n], refs[4 * n:]
        for k in range(n):
            w_ref, g_ref, m_ref, v_ref = ins[4 * k:4 * k + 4]
            d_ref, nm_ref, nv_ref = outs[3 * k:3 * k + 3]
            d_ref[...], nm_ref[...], nv_ref[...] = _adamw(w_ref[...], g_ref[...], m_ref[...], v_ref[...])

    vm = pl.BlockSpec(memory_space=pltpu.VMEM)
    shapes = [jax.ShapeDtypeStruct(w.shape, F32) for w, _, _, _ in items for _ in range(3)]
    outs = pl.pallas_call(body, name="adamw_small", in_specs=[vm] * (4 * n), out_specs=[vm] * (3 * n),
                          out_shape=shapes)(*[t for item in items for t in item])
    return [outs[3 * k:3 * k + 3] for k in range(n)]


def _add_halves(mine, theirs, name):
    _, r, cdim = mine.shape
    tr = 128 if r % 128 == 0 else r

    def body(a_ref, b_ref, o_ref, w_ref):
        total = a_ref[...] + b_ref[...]
        o_ref[...] = total
        w_ref[...] = total.astype(WIRE_DTYPE)

    spec = pl.BlockSpec((4, tr, cdim), lambda i: (0, i, 0))
    return pl.pallas_call(
        body, name=name, grid=(r // tr,), in_specs=[spec, spec], out_specs=[spec, spec],
        out_shape=[jax.ShapeDtypeStruct(mine.shape, F32), jax.ShapeDtypeStruct(mine.shape, WIRE_DTYPE)],
        compiler_params=_cp(("arbitrary",)))(mine, theirs)


def _add_own_half(where, full, theirs, name, wire_copy=False):
    _, _, r, cdim = full.shape
    tr = 128 if r % 128 == 0 else r

    def body(where_ref, a_ref, b_ref, *o_refs):
        total = a_ref[...] + b_ref[...]
        o_refs[0][...] = total
        if wire_copy:
            o_refs[1][...] = total.astype(WIRE_DTYPE)

    spec = pl.BlockSpec((4, tr, cdim), lambda i, s: (0, i, 0))
    shapes = [jax.ShapeDtypeStruct(theirs.shape, F32)] + ([jax.ShapeDtypeStruct(theirs.shape, WIRE_DTYPE)] if wire_copy else [])
    outs = pl.pallas_call(
        body, name=name,
        grid_spec=pltpu.PrefetchScalarGridSpec(
            num_scalar_prefetch=1, grid=(r // tr,),
            in_specs=[pl.BlockSpec((None, 4, tr, cdim), lambda i, s: (s[0], 0, i, 0)), spec],
            out_specs=[spec] * len(shapes)),
        out_shape=shapes, compiler_params=_cp(("arbitrary",)))(where, full, theirs)
    return outs if wire_copy else outs[0]


def kernel(x, meta_tokens, norm_mix_w, w_in, w_gate_up, b_gate, gla_norm_w, sinks, w_out, norm_ff_w, w_ff1, w_ff2, final_norm_w, loss_target, m_meta_tokens, m_norm_mix_w, m_w_in, m_w_gate_up, m_b_gate, m_gla_norm_w, m_sinks, m_w_out, m_norm_ff_w, m_w_ff1, m_w_ff2, m_final_norm_w, v_meta_tokens, v_norm_mix_w, v_w_in, v_w_gate_up, v_b_gate, v_gla_norm_w, v_sinks, v_w_out, v_norm_ff_w, v_w_ff1, v_w_ff2, v_final_norm_w):
    seq = x.shape[1]
    rows = LEAD + seq
    tm = _row_tile(rows)
    tm_wide = WIDE_ROW_TILE if rows % WIDE_ROW_TILE == 0 else tm
    dev =4 * lax.axis_index("x") + 2 * lax.axis_index("y") + lax.axis_index("c")

    small_shard = jnp.concatenate([meta_tokens, w_gate_up[0], jnp.zeros((N_META, 96), F32)], axis=1)
    g_in, g_small = _all_gather([w_in[0].astype(WIRE_DTYPE), small_shard])
    later_shards = [w_out[0].astype(WIRE_DTYPE), w_ff1[0].astype(WIRE_DTYPE), w_ff2[0].astype(WIRE_DTYPE)]
    win_p = _win_padded(g_in)
    meta_full = jnp.transpose(g_small[:, :, 0:128], (1, 0, 2)).reshape(N_META, D)
    wg_full = jnp.transpose(g_small[:, :, 128:160], (1, 0, 2)).reshape(GLA_RANK, GLA_HEADS * GLA_DK)
    wg_p = jnp.concatenate([wg_full, jnp.zeros((128 - GLA_RANK, 256), F32)], axis=0)

    lead = jnp.concatenate([jnp.zeros((META0, D), F32), meta_full], axis=0)
    tabs = _rope_tables(rows)
    proj, qr, kr, vr = _in_proj(x[0], lead, norm_mix_w, win_p, tabs, tm)
    oraw, og, states, (g_out, g_w1) = _gla_fwd(proj, wg_p, b_gate, gla_norm_w, later_shards[0:2])
    osw, (g_w2,) = _swa_fwd(qr, kr, vr, sinks, later_shards[2:3])
    wout_full = g_out.reshape(D, D)
    w2_full = g_w2.reshape(D_FF, D)
    w1_full = jnp.transpose(g_w1, (1, 0, 2)).reshape(D, D_FF)
    h1, f, ft = _out_proj(x[0], lead, og, osw, wout_full, norm_ff_w, tm)
    a, dh2, dh2t, loss_p, gfn_p = _ffn_fwd(f, h1, w1_full, w2_full, loss_target[0], final_norm_w.reshape(1, D), tm)

    da, dh1, gnf_p = _ffn_bwd_act(dh2, a, w1_full, w2_full, h1, norm_ff_w, tm)
    dw1, dw2 = _ffn_bwd_weights(ft, a, da, dh2t, tm_wide)
    where = jnp.stack([lax.axis_index("c"), 2 * lax.axis_index("x") + lax.axis_index("y")]).astype(jnp.int32)
    dog, dos, dwout, theirs_ffn = _out_proj_bwd(dh1, og, osw, wout_full, tm, [dw1, dw2])
    pairs_ffn = [_add_own_half(where, p, q, "reduce_pair_%d" % (2 + k), wire_copy=True)
                 for k, (p, q) in enumerate(zip([dw1, dw2], theirs_ffn))]
    sums_ffn, wires_ffn = [p[0] for p in pairs_ffn], [p[1] for p in pairs_ffn]
    dsq, dsk, dsv, dsink_p, (parts_ffn, (theirs_wout,)) = _swa_bwd(
        qr, kr, vr, osw, dos, sinks, _Jobs([("chips", wires_ffn), ("sibling", [dwout])]))
    sum_wout, wire_wout = _add_own_half(where, dwout, theirs_wout, "reduce_pair_1", wire_copy=True)
    (dgq, dgk, dgv, dgr, dlr, dwg_p, dbg_p, dgnw_p), ((parts_wout,),) = _gla_bwd(
        proj, oraw, states, dog, wg_p, b_gate, gla_norm_w, _Jobs([("chips", [wire_wout])]))
    grad_x, dlead, dproj, ut, gnm_p = _in_proj_bwd(x[0], lead, dh1, norm_mix_w, win_p, dgv, dgr, dsq, dgq, dgk, dsk,
                                                   dsv, dlr, tabs, tm)
    grad_x = grad_x[None]
    dwin_mine, dwin_theirs, total = _in_proj_bwd_weights(
        ut, dproj, tm_wide, [dlead, dwg_p, gnm_p, gnf_p, gfn_p, dbg_p, dgnw_p, loss_p, dsink_p])
    sum_win, sum_win_wire = _add_halves(dwin_mine, dwin_theirs, "reduce_pair_0")

    g_meta = lax.dynamic_slice(total, (R_META, dev * 128), (N_META, 128))
    g_wg = lax.dynamic_slice(total, (R_WG, dev * 32), (GLA_RANK, 32))
    g_norm_mix, g_norm_ff = total[R_NORM_MIX:R_NORM_MIX + 1], total[R_NORM_FF:R_NORM_FF + 1]
    g_final_norm = total[R_FINAL:R_FINAL + 1]
    g_b_gate, g_gla_norm = total[R_B_GATE:R_B_GATE + 1, 0:256], total[R_GLA_NORM:R_GLA_NORM + 1, 0:128]
    g_sinks = total[R_SINKS:R_SINKS + SWA_HEADS, 0].reshape(1, SWA_HEADS)
    loss = total[R_LOSS, 0]

    ((g_wout, d_wout, nm_wout, nv_wout), (g_w1s, d_w1, nm_w1, nv_w1), (g_w2s, d_w2, nm_w2, nv_w2)), ((parts_win,),) = \
        _adamw_shards(where, [(parts_wout, sum_wout, w_out[0], m_w_out[0], v_w_out[0]),
                              (parts_ffn[0], sums_ffn[0], w_ff1[0], m_w_ff1[0], v_w_ff1[0]),
                              (parts_ffn[1], sums_ffn[1], w_ff2[0], m_w_ff2[0], v_w_ff2[0])],
                      "adamw_w_out_ff", _Jobs([("chips", [sum_win_wire])]))
    ((g_win, d_win, nm_win, nv_win),), _ = _adamw_shards(
        where, [(parts_win, sum_win, w_in[0], m_w_in[0], v_w_in[0])], "adamw_w_in")

    names = ["meta", "wg", "norm_mix", "b_gate", "gla_norm", "sinks", "norm_ff", "final_norm"]
    ws = [meta_tokens, w_gate_up, norm_mix_w, b_gate, gla_norm_w, sinks, norm_ff_w, final_norm_w]
    gs = [g_meta, g_wg, g_norm_mix, g_b_gate, g_gla_norm, g_sinks, g_norm_ff, g_final_norm]
    ms = [m_meta_tokens, m_w_gate_up, m_norm_mix_w, m_b_gate, m_gla_norm_w, m_sinks, m_norm_ff_w, m_final_norm_w]
    vs = [v_meta_tokens, v_w_gate_up, v_norm_mix_w, v_b_gate, v_gla_norm_w, v_sinks, v_norm_ff_w, v_final_norm_w]
    flat = lambda t: t.reshape(-1, t.shape[-1])
    small_out = _adamw_small([(flat(w), flat(g), flat(m), flat(v)) for w, g, m, v in zip(ws, gs, ms, vs)])
    d_small = {n: small_out[k][0].reshape(ws[k].shape) for k, n in enumerate(names)}
    nm_small = {n: small_out[k][1].reshape(ws[k].shape) for k, n in enumerate(names)}
    nv_small = {n: small_out[k][2].reshape(ws[k].shape) for k, n in enumerate(names)}
    g_small_d = {n: g.reshape(ws[k].shape) for k, (n, g) in enumerate(zip(names, gs))}

    def ordered(big, small_d):
        win_v, wout_v, w1_v, w2_v = big
        return (small_d["meta"], small_d["norm_mix"], win_v[None], small_d["wg"], small_d["b_gate"],
                small_d["gla_norm"], small_d["sinks"], wout_v[None], small_d["norm_ff"], w1_v[None], w2_v[None],
                small_d["final_norm"])

    return (loss, grad_x,
            *ordered((g_win, g_wout, g_w1s, g_w2s), g_small_d),
            *ordered((d_win, d_wout, d_w1, d_w2), d_small),
            *ordered((nm_win, nm_wout, nm_w1, nm_w2), nm_small),
            *ordered((nv_win, nv_wout, nv_w1, nv_w2), nv_small))
```

```python
import functools

import jax
import jax.numpy as jnp
from jax import lax
from jax.experimental import pallas as pl
from jax.experimental.pallas import tpu as pltpu

F32 = jnp.float32
MXU_DTYPE = jnp.bfloat16
ACT_DTYPE = jnp.bfloat16
WIRE_DTYPE = jnp.bfloat16

D = 1024
N_META = 16
LEAD = 128
META0 = LEAD - N_META
EPS = 1e-5
GLA_HEADS, GLA_DK, GLA_DV, GLA_RANK, GLA_CHUNK = 4, 64, 128, 16, 64
GLA_TAU = 16.0
SWA_HEADS, SWA_KV, SWA_GROUP, SWA_HD, SWA_BLOCK = 8, 2, 4, 64, 128
ROPE_DIM, ROPE_THETA = 16, 500000.0
D_FF = 4096
N_DEV = 8
FF_TILE = D_FF // N_DEV
FF_WIDE = 2048
NEG = -1e30

C_GV, C_GR, C_GQ, C_GK, C_LR, C_SQ, C_SK, C_SV = 0, 512, 1024, 1280, 1536, 1664, 2176, 2304
DGLA = 1664
DINP = 2432
DIN = 2320
O_GQ, O_GK, O_GV, O_GR, O_LR, O_SQ, O_SK, O_SV = (0, 256), (256, 512), (512, 1024), (1024, 1536), (1536, 1552), (1552, 2064), (2064, 2192), (2192, 2320)

ADAM_LR, ADAM_B1, ADAM_B2, ADAM_EPS, ADAM_WD, ADAM_STEP = 0.001, 0.9, 0.999, 1e-08, 0.01, 10

MESH = pl.DeviceIdType.MESH
ANY = pl.BlockSpec(memory_space=pl.ANY)
VMEM_TILE_MB, VMEM_WIDE_MB = 48, 56


def _cp(sem=None, vmem_mb=None):
    kw = {}
    if sem is not None:
        kw["dimension_semantics"] = sem
    if vmem_mb is not None:
        kw["vmem_limit_bytes"] = vmem_mb << 20
    return pltpu.CompilerParams(**kw)


def _mm(a, b):
    return jnp.dot(a.astype(MXU_DTYPE), b.astype(MXU_DTYPE), preferred_element_type=F32)


def _mm_nt(a, b):
    return lax.dot_general(a.astype(MXU_DTYPE), b.astype(MXU_DTYPE), (((1,), (1,)), ((), ())),
                           preferred_element_type=F32)


def _mm_tn(a, b):
    return lax.dot_general(a.astype(MXU_DTYPE), b.astype(MXU_DTYPE), (((0,), (0,)), ((), ())),
                           preferred_element_type=F32)


def _masked_sums(mask, t):
    m = mask.astype(jnp.bfloat16)
    hi = t.astype(jnp.bfloat16)
    rest = t - hi.astype(F32)
    mid = rest.astype(jnp.bfloat16)
    low = (rest - mid.astype(F32)).astype(jnp.bfloat16)
    dot = lambda part: jnp.dot(m, part, preferred_element_type=F32)
    return dot(hi) + (dot(mid) + dot(low))


def _logsigmoid(z):
    return jnp.minimum(z, 0.0) - jnp.log(1.0 + jnp.exp(-jnp.abs(z)))


def _sigmoid(z):
    return 1.0 / (1.0 + jnp.exp(-z))


ROW_TILE, WIDE_ROW_TILE = 640, 1664


def _row_tile(rows, want=ROW_TILE):
    return want if rows % want == 0 else LEAD


def _mesh_pos():
    return lax.axis_index("x"), lax.axis_index("y"), lax.axis_index("c")


def _all_gather(shards):
    n = len(shards)

    def body(*refs):
        start, forward, finish = _gather_schedule(refs[:n], refs[n:2 * n], *refs[2 * n:])
        start()
        for j in range(3):
            forward(j)
        finish()

    gathered = pl.pallas_call(
        body, name="all_gather_weights",
        out_shape=_gathered_shapes(shards), in_specs=[ANY] * n, out_specs=[ANY] * n,
        scratch_shapes=_gather_sems(n),
    )(*shards)
    return _with_own_block(gathered, shards)


def _gathered_shapes(shards):
    return [jax.ShapeDtypeStruct((N_DEV,) + s.shape, s.dtype) for s in shards]


def _gather_sems(n):
    return [pltpu.SemaphoreType.DMA((7 * n,)), pltpu.SemaphoreType.DMA((7 * n,))] if n else []


def _place_gather(step, steps, shard_refs, gathered_refs, sems):
    if not shard_refs:
        return
    start, forward, finish = _gather_schedule(shard_refs, gathered_refs, *sems)
    pl.when(step == 0)(start)
    for j, at in enumerate((steps * 7 // 10, steps * 8 // 10, steps * 9 // 10)):
        pl.when(step == at)(functools.partial(forward, j))
    pl.when(step == steps - 1)(finish)


def _with_own_block(gathered, shards):
    dev = 4 * lax.axis_index("x") + 2 * lax.axis_index("y") + lax.axis_index("c")
    return [lax.dynamic_update_index_in_dim(g, s, dev, 0) for g, s in zip(gathered, shards)]


def _gather_schedule(ins, outs, send_sems, recv_sems):
    n = len(ins)
    x, y, c = _mesh_pos()
    me, sibling = (x, y, c), (x, y, 1 - c)
    chips = [(1 - x, y), (x, 1 - y), (1 - x, 1 - y)]

    def copy(a, k, block, to, src=None):
        dst = outs[a].at[4 * block[0] + 2 * block[1] + block[2]]
        return pltpu.make_async_remote_copy(
            src_ref=dst if src is None else src, dst_ref=dst,
            send_sem=send_sems.at[a * 7 + k], recv_sem=recv_sems.at[a * 7 + k],
            device_id=to, device_id_type=MESH)

    def first(a):
        return [copy(a, 0, me, sibling, src=ins[a])] + [copy(a, 1 + j, me, (*chip, c), src=ins[a])
                                                        for j, chip in enumerate(chips)]

    def start():
        for a in range(n):
            for cp in first(a):
                cp.start()

    def forward(j):
        for a in range(n):
            copy(a, 1 + j, (*chips[j], c), me).wait_recv()
            copy(a, 4 + j, (*chips[j], c), sibling).start()

    def finish():
        for a in range(n):
            copy(a, 0, sibling, me).wait_recv()
            for j, chip in enumerate(chips):
                copy(a, 4 + j, (*chip, 1 - c), me).wait_recv()
        for a in range(n):
            for cp in first(a) + [copy(a, 4 + j, (*chip, c), sibling) for j, chip in enumerate(chips)]:
                cp.wait_send()

    return start, forward, finish


def _sibling_shapes(gs):
    return [jax.ShapeDtypeStruct(g.shape[1:], g.dtype) for g in gs]


def _sibling_sems(n):
    return [pltpu.SemaphoreType.DMA((n,)), pltpu.SemaphoreType.DMA((n,))]


def _sibling_schedule(ins, land, send_sems, recv_sems):
    x, y, c = _mesh_pos()

    def copies():
        return [pltpu.make_async_remote_copy(
            src_ref=ins[a].at[1 - c], dst_ref=land[a], send_sem=send_sems.at[a], recv_sem=recv_sems.at[a],
            device_id=(x, y, 1 - c), device_id_type=MESH) for a in range(len(ins))]

    def start():
        for cp in copies():
            cp.start()

    def finish():
        for cp in copies():
            cp.wait_recv()
        for cp in copies():
            cp.wait_send()

    return start, finish


def _chips_shapes(ps):
    return [jax.ShapeDtypeStruct((3,) + p.shape[1:], p.dtype) for p in ps]


def _chips_sems(n):
    return [pltpu.SemaphoreType.DMA((3 * n,)), pltpu.SemaphoreType.DMA((3 * n,))]


def _chips_schedule(ins, land, send_sems, recv_sems):
    x, y, c = _mesh_pos()
    chips = [(1 - x, y), (x, 1 - y), (1 - x, 1 - y)]

    def copies():
        return [pltpu.make_async_remote_copy(
            src_ref=ins[a].at[2 * chip[0] + chip[1]], dst_ref=land[a].at[j],
            send_sem=send_sems.at[3 * a + j], recv_sem=recv_sems.at[3 * a + j],
            device_id=(*chip, c), device_id_type=MESH) for a in range(len(ins)) for j, chip in enumerate(chips)]

    def start():
        for cp in copies():
            cp.start()

    def finish():
        for cp in copies():
            cp.wait_recv()
        for cp in copies():
            cp.wait_send()

    return start, finish


class _Jobs:
    def __init__(self, jobs):
        self.jobs = jobs
        self.inputs = [a for _, arrs in jobs for a in arrs]
        self.out_shapes = [s for kind, arrs in jobs
                           for s in (_sibling_shapes(arrs) if kind == "sibling" else _chips_shapes(arrs))]
        self.sems = [s for kind, arrs in jobs
                     for s in (_sibling_sems(len(arrs)) if kind == "sibling" else _chips_sems(len(arrs)))]
        self.n = len(self.inputs)

    def bind(self, in_refs, out_refs, sem_refs):
        starts, finishes, at = [], [], 0
        for k, (kind, arrs) in enumerate(self.jobs):
            schedule = _sibling_schedule if kind == "sibling" else _chips_schedule
            start, finish = schedule(in_refs[at:at + len(arrs)], out_refs[at:at + len(arrs)],
                                     sem_refs[2 * k], sem_refs[2 * k + 1])
            starts.append(start)
            finishes.append(finish)
            at += len(arrs)

        def start_all():
            for f in starts:
                f()

        def finish_all():
            for f in finishes:
                f()

        return start_all, finish_all

    def split(self, outs):
        res, at = [], 0
        for _, arrs in self.jobs:
            res.append(list(outs[at:at + len(arrs)]))
            at += len(arrs)
        return res


R_META, R_WG, R_NORM_MIX, R_NORM_FF, R_FINAL, R_B_GATE, R_GLA_NORM, R_LOSS, R_SINKS, SMALL_ROWS = 0, 16, 32, 33, 34, 35, 36, 37, 40, 48


SMALL_SPECS = [pl.BlockSpec((LEAD, D), lambda i: (0, 0)), pl.BlockSpec((128, 256), lambda i: (0, 0)),
               pl.BlockSpec((8, D), lambda i: (0, 0)), pl.BlockSpec((8, D), lambda i: (0, 0)),
               pl.BlockSpec((8, D), lambda i: (0, 0)), pl.BlockSpec((8, 256), lambda i: (0, 0)),
               pl.BlockSpec((8, 128), lambda i: (0, 0)), pl.BlockSpec((8, 128), lambda i: (0, 0)),
               pl.BlockSpec((8, 128), lambda i: (0, 0))]


def _small_sum_scratch():
    return [pltpu.VMEM((SMALL_ROWS, D), F32), pltpu.VMEM((N_DEV, SMALL_ROWS, D), F32),
            pltpu.SemaphoreType.DMA((7,)), pltpu.SemaphoreType.DMA((7,))]


def _small_sum_schedule(small_refs, out_ref, p_ref, land, send_sems, recv_sems):
    dlead_ref, dwg_ref, gnm_ref, gnf_ref, gfn_ref, dbg_ref, dgnw_ref, loss_ref, dsink_ref = small_refs
    x, y, c = _mesh_pos()
    me = 4 * x + 2 * y + c

    def copies():
        res = []
        for k in range(1, N_DEV):
            bx, by, bc = (k >> 2) & 1, (k >> 1) & 1, k & 1
            peer = (1 - x if bx else x, 1 - y if by else y, 1 - c if bc else c)
            res.append(pltpu.make_async_remote_copy(
                src_ref=p_ref, dst_ref=land.at[me], send_sem=send_sems.at[k - 1], recv_sem=recv_sems.at[k - 1],
                device_id=peer, device_id_type=MESH))
        return res

    def start():
        p_ref[...] = jnp.zeros_like(p_ref)
        p_ref[R_META:R_META + N_META, :] = dlead_ref[META0:LEAD, :]
        p_ref[R_WG:R_WG + GLA_RANK, 0:256] = dwg_ref[0:GLA_RANK, :]
        p_ref[R_NORM_MIX:R_NORM_MIX + 1, :] = gnm_ref[0:1, :]
        p_ref[R_NORM_FF:R_NORM_FF + 1, :] = gnf_ref[0:1, :]
        p_ref[R_FINAL:R_FINAL + 1, :] = gfn_ref[0:1, :]
        p_ref[R_B_GATE:R_B_GATE + 1, 0:256] = dbg_ref[0:1, :]
        p_ref[R_GLA_NORM:R_GLA_NORM + 1, 0:128] = dgnw_ref[0:1, :]
        p_ref[R_LOSS:R_LOSS + 1, 0:128] = loss_ref[0:1, :]
        p_ref[R_SINKS:R_SINKS + SWA_HEADS, 0:128] = dsink_ref[...]
        land[me] = p_ref[...]
        for cp in copies():
            cp.start()

    def finish():
        for cp in copies():
            cp.wait_recv()
        for cp in copies():
            cp.wait_send()
        acc = land[0]
        for d in range(1, N_DEV):
            acc = acc + land[d]
        out_ref[...] = acc

    return start, finish


def _token_specs(tm, grid_rank=1):
    nb = tm // LEAD

    def spec(k):
        if grid_rank == 1:
            return pl.BlockSpec((LEAD, D), lambda i: (jnp.maximum(i * nb + k - 1, 0), 0))
        return pl.BlockSpec((LEAD, D), lambda i, j: (jnp.maximum(i * nb + k - 1, 0), 0))

    return [spec(k) for k in range(nb)]


def _h_tile(i, lead_ref, x_refs):
    first = jnp.where(i == 0, lead_ref[...], x_refs[0][...])
    return jnp.concatenate([first] + [r[...] for r in x_refs[1:]], axis=0)


def _in_proj(x, lead, nw, win_p, tabs, tm):
    rows = LEAD + x.shape[0]
    nb = tm // LEAD

    def body(*refs):
        x_refs, (lead_ref, nw_ref, w_ref, c_ref, sa_ref, sb_ref, o_ref, q_ref, k_ref, v_ref) = refs[:nb], refs[nb:]
        h = _h_tile(pl.program_id(0), lead_ref, x_refs)
        rstd = lax.rsqrt(jnp.mean(h * h, axis=-1, keepdims=True) + EPS)
        u = (h * rstd * nw_ref[...]).astype(MXU_DTYPE)
        proj = jnp.dot(u, w_ref[...].astype(MXU_DTYPE), preferred_element_type=F32)
        o_ref[...] = proj[:, 0:DGLA]
        cos, sa, sb = c_ref[...], sa_ref[...], sb_ref[...]
        q_ref[...] = (_rope(proj[:, C_SQ:C_SK], cos, sa, sb) * (SWA_HD ** -0.5)).astype(ACT_DTYPE)
        k_ref[...] = _rope(proj[:, C_SK:C_SV], cos, sa, sb).astype(ACT_DTYPE)
        v_ref[...] = proj[:, C_SV:DINP].astype(ACT_DTYPE)

    row = lambda w: pl.BlockSpec((tm, w), lambda i: (i, 0))
    return pl.pallas_call(
        body, name="in_proj", grid=(rows // tm,),
        in_specs=_token_specs(tm) + [pl.BlockSpec((LEAD, D), lambda i: (0, 0)), pl.BlockSpec((1, D), lambda i: (0, 0)),
                                     pl.BlockSpec((D, DINP), lambda i: (0, 0)), row(128), row(128), row(128)],
        out_specs=[row(DGLA), row(512), row(128), row(128)],
        out_shape=[jax.ShapeDtypeStruct((rows, DGLA), F32), jax.ShapeDtypeStruct((rows, 512), ACT_DTYPE),
                   jax.ShapeDtypeStruct((rows, 128), ACT_DTYPE), jax.ShapeDtypeStruct((rows, 128), ACT_DTYPE)],
        compiler_params=_cp(("arbitrary",), VMEM_WIDE_MB),
    )(*([x] * nb), lead, nw, win_p, *tabs)


def _rope_tables(rows):
    pos = (jnp.arange(rows, dtype=jnp.int32) - META0).astype(F32)
    inv_freq = 1.0 / (ROPE_THETA ** (jnp.arange(0, ROPE_DIM, 2, dtype=F32) / ROPE_DIM))
    ang = pos[:, None] * jnp.tile(inv_freq, 128 // (ROPE_DIM // 2))[None, :]
    in_head = jnp.arange(128, dtype=jnp.int32)[None, :] % SWA_HD
    cos, sin = jnp.cos(ang), jnp.sin(ang)
    c_tab = jnp.where(in_head < ROPE_DIM, cos, 1.0)
    sa_tab = jnp.where(in_head < ROPE_DIM // 2, -sin, 0.0)
    sb_tab = jnp.where((in_head >= ROPE_DIM // 2) & (in_head < ROPE_DIM), sin, 0.0)
    return c_tab, sa_tab, sb_tab


def _rope(xv, cos, sa, sb):
    width = xv.shape[1]
    reps = width // 128
    if reps > 1:
        cos, sa, sb = (jnp.tile(t, (1, reps)) for t in (cos, sa, sb))
    return xv * cos + pltpu.roll(xv, width - 8, 1) * sa + pltpu.roll(xv, 8, 1) * sb


def _unrope(dy, cos, sa, sb):
    width = dy.shape[1]
    reps = width // 128
    if reps > 1:
        cos, sa, sb = (jnp.tile(t, (1, reps)) for t in (cos, sa, sb))
    return dy * cos + pltpu.roll(dy * sa, 8, 1) + pltpu.roll(dy * sb, width - 8, 1)


def _gla_group(nc):
    for g in (5, 2):
        if nc % g == 0:
            return g
    return 1


def _gla_gates(lr, wg, bg, first_row, nrows):
    zg = _mm(lr, wg) + bg
    row = first_row + lax.broadcasted_iota(jnp.int32, (nrows, 1), 0)
    live = row >= META0
    g = jnp.where(live, _logsigmoid(zg) * (1.0 / GLA_TAU), 0.0)
    ii = lax.broadcasted_iota(jnp.int32, (nrows, nrows), 0)
    jj = lax.broadcasted_iota(jnp.int32, (nrows, nrows), 1)
    same = (ii // GLA_CHUNK) == (jj // GLA_CHUNK)
    lower, upper = same & (jj <= ii), same & (jj >= ii)
    b = _masked_sums(lower, g)
    return zg, live, lower, upper, b


def _tril64():
    ii = lax.broadcasted_iota(jnp.int32, (GLA_CHUNK, GLA_CHUNK), 0)
    jj = lax.broadcasted_iota(jnp.int32, (GLA_CHUNK, GLA_CHUNK), 1)
    return jj <= ii


def _gla_fwd(proj, wg_p, bg, gnw, shards):
    rows = proj.shape[0]
    nc = rows // GLA_CHUNK
    group = _gla_group(nc)
    steps, nrows = nc // group, group * GLA_CHUNK
    ns = len(shards)

    def body(q_ref, k_ref, v_ref, r_ref, lr_ref, wg_ref, bg_ref, gnw_ref, *rest):
        shard_refs, rest = rest[:ns], rest[ns:]
        oraw_ref, og_ref, st_ref = rest[:3]
        gathered_refs, rest = rest[3:3 + ns], rest[3 + ns:]
        state = rest[0]
        c = pl.program_id(0)

        @pl.when(c == 0)
        def _():
            state[...] = jnp.zeros_like(state)

        _place_gather(c, steps, shard_refs, gathered_refs, rest[1:])
        _, _, _, _, b = _gla_gates(lr_ref[...], wg_ref[...], bg_ref[...], c * nrows, nrows)
        eb = jnp.exp(b)
        gq = q_ref[...] * (GLA_DK ** -0.5) * eb
        gk = k_ref[...] * jnp.exp(-b)
        v = v_ref[...]
        gnw_v = gnw_ref[...]
        tril = _tril64()
        pairs = [(h, gi) for h in range(GLA_HEADS) for gi in range(group)]
        rs = {gi: slice(gi * GLA_CHUNK, (gi + 1) * GLA_CHUNK) for gi in range(group)}
        s64 = {h: slice(h * GLA_DK, (h + 1) * GLA_DK) for h in range(GLA_HEADS)}
        s128 = {h: slice(h * GLA_DV, (h + 1) * GLA_DV) for h in range(GLA_HEADS)}
        qh = {(h, gi): gq[rs[gi], s64[h]] for h, gi in pairs}
        kh = {(h, gi): gk[rs[gi], s64[h]] for h, gi in pairs}
        vh = {(h, gi): v[rs[gi], s128[h]] for h, gi in pairs}
        ebl = {(h, gi): eb[(gi + 1) * GLA_CHUNK - 1:(gi + 1) * GLA_CHUNK, s64[h]] for h, gi in pairs}
        av = {pr: _mm(jnp.where(tril, _mm_nt(qh[pr], kh[pr]), 0.0), vh[pr]) for pr in pairs}
        inc = {pr: _mm_tn(vh[pr], kh[pr] * ebl[pr]) for pr in pairs}
        st = {}
        for h in range(GLA_HEADS):
            cur = state[h]
            for gi in range(group):
                st[h, gi] = cur
                st_ref[gi, h] = cur
                cur = cur * ebl[h, gi] + inc[h, gi]
            state[h] = cur
        for h, gi in pairs:
            o = av[h, gi] + _mm_nt(qh[h, gi], st[h, gi])
            oraw_ref[rs[gi], s128[h]] = o
            rstd = lax.rsqrt(jnp.mean(o * o, axis=-1, keepdims=True) + EPS)
            rh = r_ref[rs[gi], s128[h]]
            og_ref[rs[gi], s128[h]] = (o * rstd * gnw_v * (rh * _sigmoid(rh))).astype(ACT_DTYPE)

    nb = lambda w, col: pl.BlockSpec((nrows, w), lambda c: (c, col // w))
    const = lambda shape: pl.BlockSpec(shape, lambda c: (0,) * len(shape))
    outs = pl.pallas_call(
        body, name="gla_fwd", grid=(steps,),
        in_specs=[nb(256, C_GQ), nb(256, C_GK), nb(512, C_GV), nb(512, C_GR), nb(128, C_LR),
                  const((128, 256)), const((1, 256)), const((1, 128))] + [ANY] * ns,
        out_specs=[pl.BlockSpec((nrows, 512), lambda c: (c, 0)), pl.BlockSpec((nrows, 512), lambda c: (c, 0)),
                   pl.BlockSpec((group, GLA_HEADS, GLA_DV, GLA_DK), lambda c: (c, 0, 0, 0))] + [ANY] * ns,
        out_shape=[jax.ShapeDtypeStruct((rows, 512), F32), jax.ShapeDtypeStruct((rows, 512), ACT_DTYPE),
                   jax.ShapeDtypeStruct((nc, GLA_HEADS, GLA_DV, GLA_DK), F32)] + _gathered_shapes(shards),
        scratch_shapes=[pltpu.VMEM((GLA_HEADS, GLA_DV, GLA_DK), F32)] + _gather_sems(ns),
        compiler_params=_cp(("arbitrary",)),
    )(proj, proj, proj, proj, proj, wg_p, bg, gnw, *shards)
    return outs[0], outs[1], outs[2], _with_own_block(outs[3:], shards)


def _swa_mask(n):
    shape = (SWA_GROUP * SWA_BLOCK, 3 * SWA_BLOCK)
    qi = lax.broadcasted_iota(jnp.int32, shape, 0) & (SWA_BLOCK - 1)
    jj = lax.broadcasted_iota(jnp.int32, shape, 1)
    meta = (jj < SWA_BLOCK) & (jj >= META0) & ((n > 0) | (jj <= qi))
    prev = (jj >= SWA_BLOCK) & (jj < 2 * SWA_BLOCK) & (n >= 2) & (jj - SWA_BLOCK > qi)
    cur = (jj >= 2 * SWA_BLOCK) & (n >= 1) & (jj - 2 * SWA_BLOCK <= qi)
    return meta | prev | cur


def _stack_heads(t, kvh):
    return jnp.concatenate([t[:, (kvh * SWA_GROUP + g) * SWA_HD:(kvh * SWA_GROUP + g + 1) * SWA_HD]
                            for g in range(SWA_GROUP)], axis=0)


def _stack_sinks(sink_ref, kvh):
    return jnp.concatenate([jnp.full((SWA_BLOCK, 1), sink_ref[0, kvh * SWA_GROUP + g], F32)
                            for g in range(SWA_GROUP)], axis=0)


def _swa_group(nblk):
    return 5 if nblk % 5 == 0 else 1


def _swa_specs(group):
    blk = lambda w: pl.BlockSpec((group * SWA_BLOCK, w), lambda n: (n, 0))
    first = pl.BlockSpec((SWA_BLOCK, 128), lambda n: (0, 0))
    prev = pl.BlockSpec((SWA_BLOCK, 128), lambda n: (jnp.maximum(n * group - 1, 0), 0))
    return blk, first, prev


def _swa_keys(first_ref, prev_ref, cur_ref, g):
    own = cur_ref[g * SWA_BLOCK:(g + 1) * SWA_BLOCK, :]
    before = prev_ref[...] if g == 0 else cur_ref[(g - 1) * SWA_BLOCK:g * SWA_BLOCK, :]
    return jnp.concatenate([first_ref[...], before, own], axis=0)


def _swa_fwd(qr, kr, vr, sinks, shards):
    rows = qr.shape[0]
    nblk = rows // SWA_BLOCK
    group = _swa_group(nblk)
    steps = nblk // group
    ns = len(shards)

    def body(q_ref, k0, kp, kc, v0, vp, vc, sink_ref, *rest):
        o_ref = rest[ns]
        _place_gather(pl.program_id(0), steps, rest[:ns], rest[ns + 1:2 * ns + 1], rest[2 * ns + 1:])
        for g in range(group):
            n = pl.program_id(0) * group + g
            rs = slice(g * SWA_BLOCK, (g + 1) * SWA_BLOCK)
            kall, vall = _swa_keys(k0, kp, kc, g), _swa_keys(v0, vp, vc, g)
            mask = _swa_mask(n)[0:SWA_BLOCK]
            heads = range(SWA_HEADS)
            hs = [slice(h * SWA_HD, (h + 1) * SWA_HD) for h in heads]
            kv = [slice((h // SWA_GROUP) * SWA_HD, (h // SWA_GROUP + 1) * SWA_HD) for h in heads]
            s = [jnp.where(mask, _mm_nt(q_ref[rs, hs[h]], kall[:, kv[h]]), NEG) for h in heads]
            m = [jnp.maximum(jnp.max(s[h], axis=-1, keepdims=True), sink_ref[0, h]) for h in heads]
            p = [jnp.exp(s[h] - m[h]) for h in heads]
            den = [jnp.sum(p[h], axis=-1, keepdims=True) + jnp.exp(sink_ref[0, h] - m[h]) for h in heads]
            o = [_mm(p[h], vall[:, kv[h]]) for h in heads]
            for h in heads:
                o_ref[rs, hs[h]] = (o[h] / den[h]).astype(ACT_DTYPE)

    blk, first, prev = _swa_specs(group)
    outs = pl.pallas_call(
        body, name="swa_fwd", grid=(steps,),
        in_specs=[blk(512), first, prev, blk(128), first, prev, blk(128),
                  pl.BlockSpec(memory_space=pltpu.SMEM)] + [ANY] * ns,
        out_specs=[blk(512)] + [ANY] * ns,
        out_shape=[jax.ShapeDtypeStruct((rows, 512), ACT_DTYPE)] + _gathered_shapes(shards),
        scratch_shapes=_gather_sems(ns),
        compiler_params=_cp(("arbitrary",)),
    )(qr, kr, kr, kr, vr, vr, vr, sinks, *shards)
    return outs[0], _with_own_block(outs[1:], shards)


def _out_proj(x, lead, og, osw, wout, nfw, tm):
    rows = LEAD + x.shape[0]
    nb = tm // LEAD

    def body(*refs):
        x_refs, (lead_ref, og_ref, os_ref, w_ref, nw_ref, h1_ref, f_ref, ft_ref) = refs[:nb], refs[nb:]
        h0 = _h_tile(pl.program_id(0), lead_ref, x_refs)
        h1 = h0 + _mm(og_ref[...], w_ref[0:512, :]) + _mm(os_ref[...], w_ref[512:1024, :])
        h1_ref[...] = h1
        rstd = lax.rsqrt(jnp.mean(h1 * h1, axis=-1, keepdims=True) + EPS)
        f = h1 * rstd * nw_ref[...]
        f_ref[...] = f.astype(ACT_DTYPE)
        ft_ref[...] = f.T.astype(ACT_DTYPE)

    row = lambda w: pl.BlockSpec((tm, w), lambda i: (i, 0))
    return pl.pallas_call(
        body, name="out_proj", grid=(rows // tm,),
        in_specs=_token_specs(tm) + [pl.BlockSpec((LEAD, D), lambda i: (0, 0)), row(512), row(512),
                                     pl.BlockSpec((D, D), lambda i: (0, 0)), pl.BlockSpec((1, D), lambda i: (0, 0))],
        out_specs=[row(D), row(D), pl.BlockSpec((D, tm), lambda i: (0, i))],
        out_shape=[jax.ShapeDtypeStruct((rows, D), F32), jax.ShapeDtypeStruct((rows, D), ACT_DTYPE),
                   jax.ShapeDtypeStruct((D, rows), ACT_DTYPE)],
        compiler_params=_cp(("arbitrary",), VMEM_TILE_MB),
    )(*([x] * nb), lead, og, osw, wout, nfw)


def _ffn_fwd(f, h1, w1, w2, tgt, fnw, tm):
    rows = f.shape[0]
    nj = D_FF // FF_WIDE
    nb = tm // LEAD

    def body(f_ref, h1_ref, w1_ref, w2_ref, nw_ref, *rest):
        t_refs, (a_ref, dh2_ref, dh2t_ref, loss_ref, gfn_ref, acc) = rest[:nb], rest[nb:]
        i, j = pl.program_id(0), pl.program_id(1)

        @pl.when((i == 0) & (j == 0))
        def _():
            loss_ref[...] = jnp.zeros_like(loss_ref)
            gfn_ref[...] = jnp.zeros_like(gfn_ref)

        @pl.when(j == 0)
        def _():
            acc[...] = jnp.zeros_like(acc)

        a = _mm(f_ref[...], w1_ref[...])
        a_ref[...] = a.astype(ACT_DTYPE)
        z = jnp.square(jnp.maximum(a, 0.0))
        acc[...] += _mm(z, w2_ref[...])

        @pl.when(j == nj - 1)
        def _():
            h2 = h1_ref[...] + acc[...]
            rstd = lax.rsqrt(jnp.mean(h2 * h2, axis=-1, keepdims=True) + EPS)
            hn = h2 * rstd
            nw = nw_ref[...]
            row = i * tm + lax.broadcasted_iota(jnp.int32, (tm, 1), 0)
            target = jnp.concatenate([t[...] for t in t_refs], axis=0)
            err = jnp.where(row >= LEAD, hn * nw - target, 0.0)
            row_loss = jnp.sum(err * err, axis=-1, keepdims=True) * (1.0 / D)
            loss_ref[...] += jnp.broadcast_to(0.5 * jnp.sum(row_loss, axis=0, keepdims=True), loss_ref.shape)
            dy = err * (1.0 / D)
            gfn_ref[...] += jnp.broadcast_to(jnp.sum(dy * hn, axis=0, keepdims=True), gfn_ref.shape)
            dhn = dy * nw
            dh2 = rstd * (dhn - hn * jnp.mean(dhn * hn, axis=-1, keepdims=True))
            dh2_ref[...] = dh2
            dh2t_ref[...] = dh2.T.astype(ACT_DTYPE)

    return pl.pallas_call(
        body, name="ffn_fwd", grid=(rows // tm, nj),
        in_specs=[pl.BlockSpec((tm, D), lambda i, j: (i, 0)), pl.BlockSpec((tm, D), lambda i, j: (i, 0)),
                  pl.BlockSpec((D, FF_WIDE), lambda i, j: (0, j)),
                  pl.BlockSpec((FF_WIDE, D), lambda i, j: (j, 0)),
                  pl.BlockSpec((1, D), lambda i, j: (0, 0))] + _token_specs(tm, grid_rank=2),
        out_specs=[pl.BlockSpec((tm, FF_WIDE), lambda i, j: (i, j)), pl.BlockSpec((tm, D), lambda i, j: (i, 0)),
                   pl.BlockSpec((D, tm), lambda i, j: (0, i)),
                   pl.BlockSpec((8, 128), lambda i, j: (0, 0)), pl.BlockSpec((8, D), lambda i, j: (0, 0))],
        out_shape=[jax.ShapeDtypeStruct((rows, D_FF), ACT_DTYPE), jax.ShapeDtypeStruct((rows, D), F32),
                   jax.ShapeDtypeStruct((D, rows), ACT_DTYPE),
                   jax.ShapeDtypeStruct((8, 128), F32), jax.ShapeDtypeStruct((8, D), F32)],
        scratch_shapes=[pltpu.VMEM((tm, D), F32)],
        compiler_params=_cp(("arbitrary", "arbitrary"), VMEM_WIDE_MB),
    )(f, h1, w1, w2, fnw, *([tgt] * nb))


def _ffn_bwd_act(dh2, a, w1, w2, h1, nfw, tm):
    rows = dh2.shape[0]
    nj = D_FF // FF_WIDE

    def body(dh2_ref, a_ref, w1_ref, w2_ref, h1_ref, nw_ref, da_ref, dh1_ref, gnf_ref, acc):
        i, j = pl.program_id(0), pl.program_id(1)

        @pl.when((i == 0) & (j == 0))
        def _():
            gnf_ref[...] = jnp.zeros_like(gnf_ref)

        @pl.when(j == 0)
        def _():
            acc[...] = jnp.zeros_like(acc)

        dz = _mm_nt(dh2_ref[...], w2_ref[...])
        da = dz * (2.0 * jnp.maximum(a_ref[...].astype(F32), 0.0))
        da_ref[...] = da.astype(ACT_DTYPE)
        acc[...] += _mm_nt(da, w1_ref[...])

        @pl.when(j == nj - 1)
        def _():
            h1 = h1_ref[...]
            rstd = lax.rsqrt(jnp.mean(h1 * h1, axis=-1, keepdims=True) + EPS)
            hn = h1 * rstd
            df = acc[...]
            gnf_ref[...] += jnp.broadcast_to(jnp.sum(df * hn, axis=0, keepdims=True), gnf_ref.shape)
            dfn = df * nw_ref[...]
            dh1_ref[...] = dh2_ref[...] + rstd * (dfn - hn * jnp.mean(dfn * hn, axis=-1, keepdims=True))

    return pl.pallas_call(
        body, name="ffn_bwd_act", grid=(rows // tm, nj),
        in_specs=[pl.BlockSpec((tm, D), lambda i, j: (i, 0)), pl.BlockSpec((tm, FF_WIDE), lambda i, j: (i, j)),
                  pl.BlockSpec((D, FF_WIDE), lambda i, j: (0, j)),
                  pl.BlockSpec((FF_WIDE, D), lambda i, j: (j, 0)),
                  pl.BlockSpec((tm, D), lambda i, j: (i, 0)), pl.BlockSpec((1, D), lambda i, j: (0, 0))],
        out_specs=[pl.BlockSpec((tm, FF_WIDE), lambda i, j: (i, j)), pl.BlockSpec((tm, D), lambda i, j: (i, 0)),
                   pl.BlockSpec((8, D), lambda i, j: (0, 0))],
        out_shape=[jax.ShapeDtypeStruct((rows, D_FF), ACT_DTYPE), jax.ShapeDtypeStruct((rows, D), F32),
                   jax.ShapeDtypeStruct((8, D), F32)],
        scratch_shapes=[pltpu.VMEM((tm, D), F32)],
        compiler_params=_cp(("arbitrary", "arbitrary"), VMEM_WIDE_MB),
    )(dh2, a, w1, w2, h1, nfw)


def _ffn_bwd_weights(ft, a, da, dh2t, tm):
    rows = a.shape[0]
    steps = rows // tm
    pair = 2 * FF_TILE

    def body(ft_ref, a_ref, da_ref, dh2t_ref, dw1_ref, dw2_ref, dw2t):
        i = pl.program_id(1)

        @pl.when(i == 0)
        def _():
            dw1_ref[...] = jnp.zeros_like(dw1_ref)
            dw2t[...] = jnp.zeros_like(dw2t)

        z = jnp.square(jnp.maximum(a_ref[...].astype(F32), 0.0))
        dw1 = _mm(ft_ref[...], da_ref[...])
        for core in range(2):
            dw1_ref[core] += dw1[:, core * FF_TILE:(core + 1) * FF_TILE]
        dw2t[...] += _mm(dh2t_ref[...], z)

        @pl.when(i == steps - 1)
        def _():
            for core in range(2):
                dw2_ref[core] = dw2t[:, core * FF_TILE:(core + 1) * FF_TILE].T

    return pl.pallas_call(
        body, name="ffn_bwd_weights", grid=(N_DEV // 2, steps),
        in_specs=[pl.BlockSpec((D, tm), lambda j, i: (0, i)), pl.BlockSpec((tm, pair), lambda j, i: (i, j)),
                  pl.BlockSpec((tm, pair), lambda j, i: (i, j)), pl.BlockSpec((D, tm), lambda j, i: (0, i))],
        out_specs=[pl.BlockSpec((2, None, D, FF_TILE), lambda j, i: (0, j, 0, 0)),
                   pl.BlockSpec((2, None, FF_TILE, D), lambda j, i: (0, j, 0, 0))],
        out_shape=[jax.ShapeDtypeStruct((2, 4, D, FF_TILE), F32), jax.ShapeDtypeStruct((2, 4, FF_TILE, D), F32)],
        scratch_shapes=[pltpu.VMEM((D, pair), F32)],
        compiler_params=_cp(("arbitrary", "arbitrary"), VMEM_WIDE_MB),
    )(ft, a, da, dh2t)


def _out_proj_bwd(dh1, og, osw, wout, tm, partials):
    rows = dh1.shape[0]
    steps = rows // tm
    ns = len(partials)

    def body(dh1_ref, og_ref, os_ref, w_ref, *rest):
        part_refs, rest = rest[:ns], rest[ns:]
        dog_ref, dos_ref, dw_ref = rest[:3]
        land_refs, (send_sems, recv_sems) = rest[3:3 + ns], rest[3 + ns:]
        i = pl.program_id(0)
        start, finish = _sibling_schedule(part_refs, land_refs, send_sems, recv_sems)

        @pl.when(i == 0)
        def _():
            dw_ref[...] = jnp.zeros_like(dw_ref)
            start()

        pl.when(i == steps - 1)(finish)

        dh1 = dh1_ref[...].astype(MXU_DTYPE)
        dog_ref[...] = _mm_nt(dh1, w_ref[0:512, :])
        dos_ref[...] = _mm_nt(dh1, w_ref[512:1024, :])
        for half, ref in enumerate((og_ref, os_ref)):
            dw = _mm_tn(ref[...], dh1)
            for blk in range(4):
                shard = half * 4 + blk
                dw_ref[shard % 2, shard // 2] += dw[blk * 128:(blk + 1) * 128, :]

    row = lambda w: pl.BlockSpec((tm, w), lambda i: (i, 0))
    outs = pl.pallas_call(
        body, name="out_proj_bwd", grid=(steps,),
        in_specs=[row(D), row(512), row(512), pl.BlockSpec((D, D), lambda i: (0, 0))] + [ANY] * ns,
        out_specs=[row(512), row(512), pl.BlockSpec((2, 4, 128, D), lambda i: (0, 0, 0, 0))] + [ANY] * ns,
        out_shape=[jax.ShapeDtypeStruct((rows, 512), F32), jax.ShapeDtypeStruct((rows, 512), F32),
                   jax.ShapeDtypeStruct((2, 4, 128, D), F32)] + _sibling_shapes(partials),
        scratch_shapes=_sibling_sems(ns),
        compiler_params=_cp(("arbitrary",), VMEM_TILE_MB),
    )(dh1, og, osw, wout, *partials)
    return outs[0], outs[1], outs[2], outs[3:]


def _swa_bwd(qr, kr, vr, osw, dos, sinks, jobs):
    rows = qr.shape[0]
    nblk = rows // SWA_BLOCK
    group = _swa_group(nblk)
    steps = nblk // group
    ns = jobs.n

    def body(q_ref, k0, kp, kc, v0, vp, vc, o_ref, do_ref, sink_ref, *rest):
        dq_ref, dk_ref, dv_ref, dsink_ref = rest[ns:ns + 4]
        start, finish = jobs.bind(rest[:ns], rest[ns + 4:2 * ns + 4], rest[2 * ns + 4:])
        step = pl.program_id(0)

        @pl.when(step == 0)
        def _():
            dk_ref[...] = jnp.zeros_like(dk_ref)
            dv_ref[...] = jnp.zeros_like(dv_ref)
            dsink_ref[...] = jnp.zeros_like(dsink_ref)
            start()

        pl.when(step == steps - 1)(finish)
        for g in range(group):
            block(step * group + g, g, q_ref, k0, kp, kc, v0, vp, vc, o_ref, do_ref, sink_ref,
                  dq_ref, dk_ref, dv_ref, dsink_ref)

    def block(n, g, q_ref, k0, kp, kc, v0, vp, vc, o_ref, do_ref, sink_ref, dq_ref, dk_ref, dv_ref, dsink_ref):
        rs = slice(g * SWA_BLOCK, (g + 1) * SWA_BLOCK)
        kall, vall = _swa_keys(k0, kp, kc, g), _swa_keys(v0, vp, vc, g)
        mask = _swa_mask(n)[0:SWA_BLOCK]
        heads = range(SWA_HEADS)
        hs = [slice(h * SWA_HD, (h + 1) * SWA_HD) for h in heads]
        kv = [slice((h // SWA_GROUP) * SWA_HD, (h // SWA_GROUP + 1) * SWA_HD) for h in heads]
        sink = [sink_ref[0, h] for h in heads]
        qh = [q_ref[rs, hs[h]] for h in heads]
        doh = [do_ref[rs, hs[h]] for h in heads]
        s = [jnp.where(mask, _mm_nt(qh[h], kall[:, kv[h]]), NEG) for h in heads]
        dp = [_mm_nt(doh[h], vall[:, kv[h]]) for h in heads]
        delta = [jnp.sum(doh[h] * o_ref[rs, hs[h]].astype(F32), axis=-1, keepdims=True) for h in heads]
        m = [jnp.maximum(jnp.max(s[h], axis=-1, keepdims=True), sink[h]) for h in heads]
        e = [jnp.exp(s[h] - m[h]) for h in heads]
        inv = [1.0 / (jnp.sum(e[h], axis=-1, keepdims=True) + jnp.exp(sink[h] - m[h])) for h in heads]
        p = [e[h] * inv[h] for h in heads]
        ds = [p[h] * (dp[h] - delta[h]) for h in heads]
        dq = [_mm(ds[h], kall[:, kv[h]]) for h in heads]
        dkh = [_mm_tn(ds[h], qh[h]) for h in heads]
        dvh = [_mm_tn(p[h], doh[h]) for h in heads]
        for h in heads:
            dsink = -jnp.sum(jnp.exp(sink[h] - m[h]) * inv[h] * delta[h], axis=0, keepdims=True)
            dsink_ref[h:h + 1, :] += jnp.broadcast_to(dsink, (1, 128))
        dq_ref[rs, :] = jnp.concatenate(dq, axis=1)
        group_sum = lambda parts, kvh: sum(parts[kvh * SWA_GROUP + 1:(kvh + 1) * SWA_GROUP], parts[kvh * SWA_GROUP])
        dk_all = jnp.concatenate([group_sum(dkh, kvh) for kvh in range(SWA_KV)], axis=1)
        dv_all = jnp.concatenate([group_sum(dvh, kvh) for kvh in range(SWA_KV)], axis=1)
        prev0 = pl.multiple_of(jnp.maximum(n - 1, 0) * SWA_BLOCK, SWA_BLOCK)
        cur0 = pl.multiple_of(n * SWA_BLOCK, SWA_BLOCK)
        for ref, val in ((dk_ref, dk_all), (dv_ref, dv_all)):
            ref[0:SWA_BLOCK, :] += val[0:SWA_BLOCK]
            ref[pl.ds(prev0, SWA_BLOCK), :] += val[SWA_BLOCK:2 * SWA_BLOCK]
            ref[pl.ds(cur0, SWA_BLOCK), :] += val[2 * SWA_BLOCK:]

    blk, first, prev = _swa_specs(group)
    whole = pl.BlockSpec((rows, 128), lambda n: (0, 0))
    outs = pl.pallas_call(
        body, name="swa_bwd", grid=(steps,),
        in_specs=[blk(512), first, prev, blk(128), first, prev, blk(128), blk(512), blk(512),
                  pl.BlockSpec(memory_space=pltpu.SMEM)] + [ANY] * ns,
        out_specs=[blk(512), whole, whole, pl.BlockSpec((8, 128), lambda n: (0, 0))] + [ANY] * ns,
        out_shape=[jax.ShapeDtypeStruct((rows, 512), F32), jax.ShapeDtypeStruct((rows, 128), F32),
                   jax.ShapeDtypeStruct((rows, 128), F32), jax.ShapeDtypeStruct((8, 128), F32)] + jobs.out_shapes,
        scratch_shapes=jobs.sems,
        compiler_params=_cp(("arbitrary",), VMEM_TILE_MB),
    )(qr, kr, kr, kr, vr, vr, vr, osw, dos, sinks, *jobs.inputs)
    return outs[0], outs[1], outs[2], outs[3], jobs.split(outs[4:])


def _gla_bwd(proj, oraw, states, dog, wg_p, bg, gnw, jobs):
    rows = proj.shape[0]
    nc = rows // GLA_CHUNK
    group = _gla_group(nc)
    steps, nrows = nc // group, group * GLA_CHUNK
    ns = jobs.n

    def body(q_ref, k_ref, v_ref, r_ref, lr_ref, oraw_ref, st_ref, dog_ref, wg_ref, bg_ref, gnw_ref, *rest):
        dq_ref, dk_ref, dv_ref, dr_ref, dlr_ref, dwg_ref, dbg_ref, dgnw_ref = rest[ns:ns + 8]
        dstate, db_scr = rest[2 * ns + 8:2 * ns + 10]
        start, finish = jobs.bind(rest[:ns], rest[ns + 8:2 * ns + 8], rest[2 * ns + 10:])
        t = pl.program_id(0)
        c = steps - 1 - t

        @pl.when(t == 0)
        def _():
            dstate[...] = jnp.zeros_like(dstate)
            dwg_ref[...] = jnp.zeros_like(dwg_ref)
            dbg_ref[...] = jnp.zeros_like(dbg_ref)
            dgnw_ref[...] = jnp.zeros_like(dgnw_ref)
            start()

        pl.when(t == steps - 1)(finish)

        lr, wg = lr_ref[...], wg_ref[...]
        zg, live, _, upper, b = _gla_gates(lr, wg, bg_ref[...], c * nrows, nrows)
        eb, enb = jnp.exp(b), jnp.exp(-b)
        scale = GLA_DK ** -0.5
        gq = q_ref[...] * scale * eb
        gk = k_ref[...] * enb
        v = v_ref[...]
        gnw_v = gnw_ref[...]
        tril = _tril64()
        is_last = lax.broadcasted_iota(jnp.int32, (GLA_CHUNK, 1), 0) == GLA_CHUNK - 1
        dgnw = jnp.zeros((1, GLA_DV), F32)
        pairs = [(h, gi) for h in range(GLA_HEADS) for gi in range(group)]
        rs = {gi: slice(gi * GLA_CHUNK, (gi + 1) * GLA_CHUNK) for gi in range(group)}
        s64 = {h: slice(h * GLA_DK, (h + 1) * GLA_DK) for h in range(GLA_HEADS)}
        s128 = {h: slice(h * GLA_DV, (h + 1) * GLA_DV) for h in range(GLA_HEADS)}
        qh = {(h, gi): gq[rs[gi], s64[h]] for h, gi in pairs}
        kh = {(h, gi): gk[rs[gi], s64[h]] for h, gi in pairs}
        vh = {(h, gi): v[rs[gi], s128[h]] for h, gi in pairs}
        ebl = {(h, gi): eb[(gi + 1) * GLA_CHUNK - 1:(gi + 1) * GLA_CHUNK, s64[h]] for h, gi in pairs}
        kl = {pr: kh[pr] * ebl[pr] for pr in pairs}
        st = {(h, gi): st_ref[gi, h] for h, gi in pairs}
        do = {}
        for h, gi in pairs:
            o, rh, dout = oraw_ref[rs[gi], s128[h]], r_ref[rs[gi], s128[h]], dog_ref[rs[gi], s128[h]]
            rstd = lax.rsqrt(jnp.mean(o * o, axis=-1, keepdims=True) + EPS)
            on = o * rstd
            sg = _sigmoid(rh)
            dr_ref[rs[gi], s128[h]] = (dout * (on * gnw_v) * (sg * (1.0 + rh * (1.0 - sg)))).astype(ACT_DTYPE)
            dy = dout * (rh * sg)
            dgnw = dgnw + jnp.sum(dy * on, axis=0, keepdims=True)
            don = dy * gnw_v
            do[h, gi] = rstd * (don - on * jnp.mean(don * on, axis=-1, keepdims=True))
        a = {pr: jnp.where(tril, _mm_nt(qh[pr], kh[pr]), 0.0) for pr in pairs}
        da = {pr: jnp.where(tril, _mm_nt(do[pr], vh[pr]), 0.0) for pr in pairs}
        dinc = {pr: _mm_tn(do[pr], qh[pr]) for pr in pairs}
        dgq = {pr: _mm(da[pr], kh[pr]) + _mm(do[pr], st[pr]) for pr in pairs}
        dgk = {pr: _mm_tn(da[pr], qh[pr]) for pr in pairs}
        dv_a = {pr: _mm_tn(a[pr], do[pr]) for pr in pairs}
        dsp = {}
        for h in range(GLA_HEADS):
            cur = dstate[h]
            for gi in reversed(range(group)):
                dsp[h, gi] = cur
                cur = cur * ebl[h, gi] + dinc[h, gi]
            dstate[h] = cur
        for h, gi in pairs:
            pr = (h, gi)
            dkl = _mm(vh[pr], dsp[pr])
            dv_ref[rs[gi], s128[h]] = (dv_a[pr] + _mm_nt(kl[pr], dsp[pr])).astype(ACT_DTYPE)
            debl = jnp.sum(dsp[pr] * st[pr], axis=0, keepdims=True)
            dq_ref[rs[gi], s64[h]] = (dgq[pr] * (scale * eb[rs[gi], s64[h]])).astype(ACT_DTYPE)
            dk_ref[rs[gi], s64[h]] = ((dgk[pr] + dkl * ebl[pr]) * enb[rs[gi], s64[h]]).astype(ACT_DTYPE)
            last = debl * ebl[pr] + jnp.sum(dkl * kl[pr], axis=0, keepdims=True)
            db_scr[rs[gi], s64[h]] = (dgq[pr] * qh[pr] - dgk[pr] * kh[pr] - dkl * kl[pr]
                                      + jnp.where(is_last, last, 0.0))
        dg = _masked_sums(upper, db_scr[...])
        dzg = jnp.where(live, dg * _sigmoid(-zg) * (1.0 / GLA_TAU), 0.0)
        dlr_ref[...] = _mm_nt(dzg, wg).astype(ACT_DTYPE)
        dwg_ref[...] += _mm_tn(lr, dzg)
        dbg_ref[...] += jnp.broadcast_to(jnp.sum(dzg, axis=0, keepdims=True), dbg_ref.shape)
        dgnw_ref[...] += jnp.broadcast_to(dgnw, dgnw_ref.shape)

    nb = lambda w, col: pl.BlockSpec((nrows, w), lambda t: (steps - 1 - t, col // w))
    const = lambda shape: pl.BlockSpec(shape, lambda t: (0,) * len(shape))
    outs = pl.pallas_call(
        body, name="gla_bwd", grid=(steps,),
        in_specs=[nb(256, C_GQ), nb(256, C_GK), nb(512, C_GV), nb(512, C_GR), nb(128, C_LR), nb(512, 0),
                  pl.BlockSpec((group, GLA_HEADS, GLA_DV, GLA_DK), lambda t: (steps - 1 - t, 0, 0, 0)), nb(512, 0),
                  const((128, 256)), const((1, 256)), const((1, 128))] + [ANY] * ns,
        out_specs=[nb(256, 0), nb(256, 0), nb(512, 0), nb(512, 0), nb(128, 0),
                   const((128, 256)), const((8, 256)), const((8, 128))] + [ANY] * ns,
        out_shape=[jax.ShapeDtypeStruct((rows, 256), ACT_DTYPE), jax.ShapeDtypeStruct((rows, 256), ACT_DTYPE),
                   jax.ShapeDtypeStruct((rows, 512), ACT_DTYPE), jax.ShapeDtypeStruct((rows, 512), ACT_DTYPE),
                   jax.ShapeDtypeStruct((rows, 128), ACT_DTYPE), jax.ShapeDtypeStruct((128, 256), F32),
                   jax.ShapeDtypeStruct((8, 256), F32), jax.ShapeDtypeStruct((8, 128), F32)] + jobs.out_shapes,
        scratch_shapes=[pltpu.VMEM((GLA_HEADS, GLA_DV, GLA_DK), F32), pltpu.VMEM((nrows, 256), F32)] + jobs.sems,
        compiler_params=_cp(("arbitrary",)),
    )(proj, proj, proj, proj, proj, oraw, states, dog, wg_p, bg, gnw, *jobs.inputs)
    return outs[:8], jobs.split(outs[8:])


def _in_proj_bwd(x, lead, dh1, nw, win_p, dgv, dgr, dsq, dgq, dgk, dsk, dsv, dlr, tabs, tm):
    seq = x.shape[0]
    rows = LEAD + seq
    nb = tm // LEAD
    steps = rows // tm

    def first_copy(scr, gx_ref, sem):
        return pltpu.make_async_copy(scr.at[pl.ds(LEAD, tm - LEAD)], gx_ref.at[pl.ds(0, tm - LEAD)], sem)

    def tile_copy(scr, gx_ref, sem, step):
        start = pl.multiple_of(jnp.maximum(step * tm - LEAD, 0), LEAD)
        return pltpu.make_async_copy(scr, gx_ref.at[pl.ds(start, tm)], sem)

    def body(*refs):
        x_refs, refs = refs[:nb], refs[nb:]
        (lead_ref, dh1_ref, nw_ref, w_ref, dgv_ref, dgr_ref, dsq_ref, dgq_ref, dgk_ref, dsk_ref, dsv_ref, dlr_ref,
         c_ref, sa_ref, sb_ref, gx_ref, dlead_ref, dproj_ref, ut_ref, gnm_ref, scr, sem) = refs
        i = pl.program_id(0)

        @pl.when(i == 0)
        def _():
            gnm_ref[...] = jnp.zeros_like(gnm_ref)

        cos, sa, sb = c_ref[...], sa_ref[...], sb_ref[...]
        dsq_v = (_unrope(dsq_ref[...], cos, sa, sb) * (SWA_HD ** -0.5)).astype(MXU_DTYPE)
        dsk_v = _unrope(dsk_ref[...], cos, sa, sb).astype(MXU_DTYPE)
        dproj = jnp.concatenate(
            [dgv_ref[...].astype(MXU_DTYPE), dgr_ref[...].astype(MXU_DTYPE), dgq_ref[...].astype(MXU_DTYPE),
             dgk_ref[...].astype(MXU_DTYPE), dlr_ref[...].astype(MXU_DTYPE), dsq_v, dsk_v,
             dsv_ref[...].astype(MXU_DTYPE)],
            axis=1)
        dproj_ref[...] = dproj
        h = _h_tile(i, lead_ref, x_refs)
        rstd = lax.rsqrt(jnp.mean(h * h, axis=-1, keepdims=True) + EPS)
        hn = h * rstd
        nw_v = nw_ref[...]
        ut_ref[...] = (hn * nw_v).T.astype(ACT_DTYPE)
        du = _mm_nt(dproj, w_ref[...])
        gnm_ref[...] += jnp.broadcast_to(jnp.sum(du * hn, axis=0, keepdims=True), gnm_ref.shape)
        dun = du * nw_v
        dh0 = dh1_ref[...] + rstd * (dun - hn * jnp.mean(dun * hn, axis=-1, keepdims=True))

        if tm > LEAD:
            pl.when(i == 1)(lambda: first_copy(scr, gx_ref, sem).wait())
        pl.when(i > 1)(lambda: tile_copy(scr, gx_ref, sem, i).wait())
        scr[...] = dh0

        @pl.when(i == 0)
        def _():
            dlead_ref[...] = dh0[0:LEAD]
            if tm > LEAD:
                first_copy(scr, gx_ref, sem).start()
                if steps == 1:
                    first_copy(scr, gx_ref, sem).wait()

        @pl.when(i > 0)
        def _():
            tile_copy(scr, gx_ref, sem, i).start()

        if steps > 1:
            pl.when(i == steps - 1)(lambda: tile_copy(scr, gx_ref, sem, i).wait())

    row = lambda w: pl.BlockSpec((tm, w), lambda i: (i, 0))
    const = lambda shape: pl.BlockSpec(shape, lambda i: (0,) * len(shape))
    return pl.pallas_call(
        body, name="in_proj_bwd", grid=(steps,),
        in_specs=_token_specs(tm) + [const((LEAD, D)), row(D), const((1, D)), const((D, DINP)),
                                     row(512), row(512), row(512), row(256), row(256), row(128), row(128), row(128),
                                     row(128), row(128), row(128)],
        out_specs=[ANY, const((LEAD, D)), row(DINP), pl.BlockSpec((D, tm), lambda i: (0, i)), const((8, D))],
        out_shape=[jax.ShapeDtypeStruct((seq, D), F32), jax.ShapeDtypeStruct((LEAD, D), F32),
                   jax.ShapeDtypeStruct((rows, DINP), ACT_DTYPE), jax.ShapeDtypeStruct((D, rows), ACT_DTYPE),
                   jax.ShapeDtypeStruct((8, D), F32)],
        scratch_shapes=[pltpu.VMEM((tm, D), F32), pltpu.SemaphoreType.DMA],
        compiler_params=_cp(("arbitrary",), VMEM_WIDE_MB),
    )(*([x] * nb), lead, dh1, nw, win_p, dgv, dgr, dsq, dgq, dgk, dsk, dsv, dlr, *tabs)


def _win_runs():
    groups = [(O_GQ, C_GQ), (O_GK, C_GK), (O_GV, C_GV), (O_GR, C_GR), (O_LR, C_LR), (O_SQ, C_SQ), (O_SK, C_SK),
              (O_SV, C_SV)]
    per = DIN // N_DEV
    runs = []
    for (o0, o1), c0 in groups:
        o = o0
        while o < o1:
            d = o // per
            end = min(o1, (d + 1) * per)
            runs.append((d, o - d * per, c0 + o - o0, end - o))
            o = end
    return runs


def _win_padded(g_in):
    tr = 128

    def body(g_ref, o_ref):
        o_ref[...] = jnp.zeros_like(o_ref)
        for d, s, c, w in _win_runs():
            o_ref[:, c:c + w] = g_ref[d, :, s:s + w]

    return pl.pallas_call(
        body, name="w_in_layout", grid=(D // tr,),
        in_specs=[pl.BlockSpec((N_DEV, tr, DIN // N_DEV), lambda i: (0, i, 0))],
        out_specs=pl.BlockSpec((tr, DINP), lambda i: (i, 0)),
        out_shape=jax.ShapeDtypeStruct((D, DINP), g_in.dtype),
        compiler_params=_cp(("arbitrary",)),
    )(g_in)


def _in_proj_bwd_weights(ut, dproj, tm, small):
    rows = dproj.shape[0]
    steps = rows // tm
    per = DIN // N_DEV

    def body(ut_ref, dp_ref, *rest):
        small_refs, (mine_ref, theirs_ref, total_ref, acc, stage, local_sems, send_sems, recv_sems) = rest[:9], rest[9:17]
        i = pl.program_id(0)
        start, finish = _small_sum_schedule(small_refs, total_ref, *rest[17:])
        x, y, c = _mesh_pos()

        @pl.when(i == 0)
        def _():
            acc[...] = jnp.zeros_like(acc)
            start()

        acc[...] += _mm(ut_ref[...], dp_ref[...])
        pl.when(i == steps - 1)(finish)

        def keep(slot, chip):
            return pltpu.make_async_copy(stage.at[slot], mine_ref.at[chip], local_sems.at[slot])

        def send(slot, chip):
            return pltpu.make_async_remote_copy(
                src_ref=stage.at[slot], dst_ref=theirs_ref.at[chip], send_sem=send_sems.at[slot],
                recv_sem=recv_sems.at[chip], device_id=(x, y, 1 - c), device_id_type=MESH)

        def drained(d):
            pl.when(c == d % 2)(keep(d % 2, d // 2).wait)
            pl.when(c != d % 2)(send(d % 2, d // 2).wait_send)

        @pl.when(i == steps - 1)
        def _():
            for d in range(N_DEV):
                slot, chip = d % 2, d // 2
                if d >= 2:
                    drained(d - 2)
                for owner, s, col, w in _win_runs():
                    if owner == d:
                        stage[slot, :, s:s + w] = acc[:, col:col + w]
                pl.when(c == slot)(keep(slot, chip).start)
                pl.when(c != slot)(send(slot, chip).start)
            drained(N_DEV - 2)
            drained(N_DEV - 1)
            for chip in range(4):
                send(0, chip).wait_recv()

    half = jax.ShapeDtypeStruct((4, D, per), F32)
    return pl.pallas_call(
        body, name="in_proj_bwd_weights", grid=(steps,),
        in_specs=[pl.BlockSpec((D, tm), lambda i: (0, i)), pl.BlockSpec((tm, DINP), lambda i: (i, 0))] + SMALL_SPECS,
        out_specs=[ANY, ANY, pl.BlockSpec((SMALL_ROWS, D), lambda i: (0, 0))],
        out_shape=[half, half, jax.ShapeDtypeStruct((SMALL_ROWS, D), F32)],
        scratch_shapes=[pltpu.VMEM((D, DINP), F32), pltpu.VMEM((2, D, per), F32), pltpu.SemaphoreType.DMA((2,)),
                        pltpu.SemaphoreType.DMA((2,)), pltpu.SemaphoreType.DMA((4,))] + _small_sum_scratch(),
        compiler_params=_cp(("arbitrary",), VMEM_WIDE_MB),
    )(ut, dproj, *small)


def _adamw(w, g, m, v):
    m = ADAM_B1 * m + (1.0 - ADAM_B1) * g
    v = ADAM_B2 * v + (1.0 - ADAM_B2) * jnp.square(g)
    m_hat = m / (1.0 - ADAM_B1 ** ADAM_STEP)
    v_hat = v / (1.0 - ADAM_B2 ** ADAM_STEP)
    delta = -ADAM_LR * (m_hat / (jnp.sqrt(v_hat) + ADAM_EPS) + ADAM_WD * w)
    return delta, m, v


ADAM_STEPS = 8


def _adamw_shards(where, items, name, jobs=None):
    jobs = jobs or _Jobs([])
    ns, nw = jobs.n, len(items)

    def body(where_ref, *rest):
        ins, rest = rest[:5 * nw], rest[5 * nw:]
        job_ins, rest = rest[:ns], rest[ns:]
        outs, rest = rest[:4 * nw], rest[4 * nw:]
        start, finish = jobs.bind(job_ins, rest[:ns], rest[ns:])
        i = pl.program_id(0)
        pl.when(i == 0)(start)
        pl.when(i == ADAM_STEPS - 1)(finish)
        for k in range(nw):
            p_ref, own_ref, w_ref, m_ref, v_ref = ins[5 * k:5 * k + 5]
            g_ref, d_ref, nm_ref, nv_ref = outs[4 * k:4 * k + 4]
            g = ((p_ref[0].astype(F32) + p_ref[1].astype(F32)) + p_ref[2].astype(F32)) + own_ref[...]
            g_ref[...] = g
            d_ref[...], nm_ref[...], nv_ref[...] = _adamw(w_ref[...], g, m_ref[...], v_ref[...])

    in_specs, out_specs, out_shape, operands = [], [], [], []
    for parts, own, w, m, v in items:
        r, cdim = w.shape
        tr = r // ADAM_STEPS
        spec = pl.BlockSpec((tr, cdim), lambda i, s: (i, 0))
        in_specs += [pl.BlockSpec((3, tr, cdim), lambda i, s: (0, i, 0)),
                     pl.BlockSpec((None, tr, cdim), lambda i, s: (s[1], i, 0)), spec, spec, spec]
        out_specs += [spec] * 4
        out_shape += [jax.ShapeDtypeStruct((r, cdim), F32)] * 4
        operands += [parts, own, w, m, v]
    outs = pl.pallas_call(
        body, name=name,
        grid_spec=pltpu.PrefetchScalarGridSpec(
            num_scalar_prefetch=1, grid=(ADAM_STEPS,),
            in_specs=in_specs + [ANY] * ns, out_specs=out_specs + [ANY] * ns, scratch_shapes=jobs.sems),
        out_shape=out_shape + jobs.out_shapes,
        compiler_params=_cp(("arbitrary",)),
    )(where, *operands, *jobs.inputs)
    return [outs[4 * k:4 * k + 4] for k in range(nw)], jobs.split(outs[4 * nw:])


def _adamw_small(items):
    n = len(items)

    def body(*refs):
        ins, outs = refs[:4 * n], refs[4 * n:]
        for k in range(n):
            w_ref, g_ref, m_ref, v_ref = ins[4 * k:4 * k + 4]
            d_ref, nm_ref, nv_ref = outs[3 * k:3 * k + 3]
            d_ref[...], nm_ref[...], nv_ref[...] = _adamw(w_ref[...], g_ref[...], m_ref[...], v_ref[...])

    vm = pl.BlockSpec(memory_space=pltpu.VMEM)
    shapes = [jax.ShapeDtypeStruct(w.shape, F32) for w, _, _, _ in items for _ in range(3)]
    outs = pl.pallas_call(body, name="adamw_small", in_specs=[vm] * (4 * n), out_specs=[vm] * (3 * n),
                          out_shape=shapes)(*[t for item in items for t in item])
    return [outs[3 * k:3 * k + 3] for k in range(n)]


def _add_halves(mine, theirs, name):
    _, r, cdim = mine.shape
    tr = 128 if r % 128 == 0 else r

    def body(a_ref, b_ref, o_ref, w_ref):
        total = a_ref[...] + b_ref[...]
        o_ref[...] = total
        w_ref[...] = total.astype(WIRE_DTYPE)

    spec = pl.BlockSpec((4, tr, cdim), lambda i: (0, i, 0))
    return pl.pallas_call(
        body, name=name, grid=(r // tr,), in_specs=[spec, spec], out_specs=[spec, spec],
        out_shape=[jax.ShapeDtypeStruct(mine.shape, F32), jax.ShapeDtypeStruct(mine.shape, WIRE_DTYPE)],
        compiler_params=_cp(("arbitrary",)))(mine, theirs)


def _add_own_half(where, full, theirs, name, wire_copy=False):
    _, _, r, cdim = full.shape
    tr = 128 if r % 128 == 0 else r

    def body(where_ref, a_ref, b_ref, *o_refs):
        total = a_ref[...] + b_ref[...]
        o_refs[0][...] = total
        if wire_copy:
            o_refs[1][...] = total.astype(WIRE_DTYPE)

    spec = pl.BlockSpec((4, tr, cdim), lambda i, s: (0, i, 0))
    shapes = [jax.ShapeDtypeStruct(theirs.shape, F32)] + ([jax.ShapeDtypeStruct(theirs.shape, WIRE_DTYPE)] if wire_copy else [])
    outs = pl.pallas_call(
        body, name=name,
        grid_spec=pltpu.PrefetchScalarGridSpec(
            num_scalar_prefetch=1, grid=(r // tr,),
            in_specs=[pl.BlockSpec((None, 4, tr, cdim), lambda i, s: (s[0], 0, i, 0)), spec],
            out_specs=[spec] * len(shapes)),
        out_shape=shapes, compiler_params=_cp(("arbitrary",)))(where, full, theirs)
    return outs if wire_copy else outs[0]


def kernel(x, meta_tokens, norm_mix_w, w_in, w_gate_up, b_gate, gla_norm_w, sinks, w_out, norm_ff_w, w_ff1, w_ff2, final_norm_w, loss_target, m_meta_tokens, m_norm_mix_w, m_w_in, m_w_gate_up, m_b_gate, m_gla_norm_w, m_sinks, m_w_out, m_norm_ff_w, m_w_ff1, m_w_ff2, m_final_norm_w, v_meta_tokens, v_norm_mix_w, v_w_in, v_w_gate_up, v_b_gate, v_gla_norm_w, v_sinks, v_w_out, v_norm_ff_w, v_w_ff1, v_w_ff2, v_final_norm_w):
    seq = x.shape[1]
    rows = LEAD + seq
    tm = _row_tile(rows)
    tm_wide = WIDE_ROW_TILE if rows % WIDE_ROW_TILE == 0 else tm
    dev =4 * lax.axis_index("x") + 2 * lax.axis_index("y") + lax.axis_index("c")

    small_shard = jnp.concatenate([meta_tokens, w_gate_up[0], jnp.zeros((N_META, 96), F32)], axis=1)
    g_in, g_small = _all_gather([w_in[0].astype(WIRE_DTYPE), small_shard])
    later_shards = [w_out[0].astype(WIRE_DTYPE), w_ff1[0].astype(WIRE_DTYPE), w_ff2[0].astype(WIRE_DTYPE)]
    win_p = _win_padded(g_in)
    meta_full = jnp.transpose(g_small[:, :, 0:128], (1, 0, 2)).reshape(N_META, D)
    wg_full = jnp.transpose(g_small[:, :, 128:160], (1, 0, 2)).reshape(GLA_RANK, GLA_HEADS * GLA_DK)
    wg_p = jnp.concatenate([wg_full, jnp.zeros((128 - GLA_RANK, 256), F32)], axis=0)

    lead = jnp.concatenate([jnp.zeros((META0, D), F32), meta_full], axis=0)
    tabs = _rope_tables(rows)
    proj, qr, kr, vr = _in_proj(x[0], lead, norm_mix_w, win_p, tabs, tm)
    oraw, og, states, (g_out, g_w1) = _gla_fwd(proj, wg_p, b_gate, gla_norm_w, later_shards[0:2])
    osw, (g_w2,) = _swa_fwd(qr, kr, vr, sinks, later_shards[2:3])
    wout_full = g_out.reshape(D, D)
    w2_full = g_w2.reshape(D_FF, D)
    w1_full = jnp.transpose(g_w1, (1, 0, 2)).reshape(D, D_FF)
    h1, f, ft = _out_proj(x[0], lead, og, osw, wout_full, norm_ff_w, tm)
    a, dh2, dh2t, loss_p, gfn_p = _ffn_fwd(f, h1, w1_full, w2_full, loss_target[0], final_norm_w.reshape(1, D), tm)

    da, dh1, gnf_p = _ffn_bwd_act(dh2, a, w1_full, w2_full, h1, norm_ff_w, tm)
    dw1, dw2 = _ffn_bwd_weights(ft, a, da, dh2t, tm_wide)
    where = jnp.stack([lax.axis_index("c"), 2 * lax.axis_index("x") + lax.axis_index("y")]).astype(jnp.int32)
    dog, dos, dwout, theirs_ffn = _out_proj_bwd(dh1, og, osw, wout_full, tm, [dw1, dw2])
    pairs_ffn = [_add_own_half(where, p, q, "reduce_pair_%d" % (2 + k), wire_copy=True)
                 for k, (p, q) in enumerate(zip([dw1, dw2], theirs_ffn))]
    sums_ffn, wires_ffn = [p[0] for p in pairs_ffn], [p[1] for p in pairs_ffn]
    dsq, dsk, dsv, dsink_p, (parts_ffn, (theirs_wout,)) = _swa_bwd(
        qr, kr, vr, osw, dos, sinks, _Jobs([("chips", wires_ffn), ("sibling", [dwout])]))
    sum_wout, wire_wout = _add_own_half(where, dwout, theirs_wout, "reduce_pair_1", wire_copy=True)
    (dgq, dgk, dgv, dgr, dlr, dwg_p, dbg_p, dgnw_p), ((parts_wout,),) = _gla_bwd(
        proj, oraw, states, dog, wg_p, b_gate, gla_norm_w, _Jobs([("chips", [wire_wout])]))
    grad_x, dlead, dproj, ut, gnm_p = _in_proj_bwd(x[0], lead, dh1, norm_mix_w, win_p, dgv, dgr, dsq, dgq, dgk, dsk,
                                                   dsv, dlr, tabs, tm)
    grad_x = grad_x[None]
    dwin_mine, dwin_theirs, total = _in_proj_bwd_weights(
        ut, dproj, tm_wide, [dlead, dwg_p, gnm_p, gnf_p, gfn_p, dbg_p, dgnw_p, loss_p, dsink_p])
    sum_win, sum_win_wire = _add_halves(dwin_mine, dwin_theirs, "reduce_pair_0")

    g_meta = lax.dynamic_slice(total, (R_META, dev * 128), (N_META, 128))
    g_wg = lax.dynamic_slice(total, (R_WG, dev * 32), (GLA_RANK, 32))
    g_norm_mix, g_norm_ff = total[R_NORM_MIX:R_NORM_MIX + 1], total[R_NORM_FF:R_NORM_FF + 1]
    g_final_norm = total[R_FINAL:R_FINAL + 1]
    g_b_gate, g_gla_norm = total[R_B_GATE:R_B_GATE + 1, 0:256], total[R_GLA_NORM:R_GLA_NORM + 1, 0:128]
    g_sinks = total[R_SINKS:R_SINKS + SWA_HEADS, 0].reshape(1, SWA_HEADS)
    loss = total[R_LOSS, 0]

    ((g_wout, d_wout, nm_wout, nv_wout), (g_w1s, d_w1, nm_w1, nv_w1), (g_w2s, d_w2, nm_w2, nv_w2)), ((parts_win,),) = \
        _adamw_shards(where, [(parts_wout, sum_wout, w_out[0], m_w_out[0], v_w_out[0]),
                              (parts_ffn[0], sums_ffn[0], w_ff1[0], m_w_ff1[0], v_w_ff1[0]),
                              (parts_ffn[1], sums_ffn[1], w_ff2[0], m_w_ff2[0], v_w_ff2[0])],
                      "adamw_w_out_ff", _Jobs([("chips", [sum_win_wire])]))
    ((g_win, d_win, nm_win, nv_win),), _ = _adamw_shards(
        where, [(parts_win, sum_win, w_in[0], m_w_in[0], v_w_in[0])], "adamw_w_in")

    names = ["meta", "wg", "norm_mix", "b_gate", "gla_norm", "sinks", "norm_ff", "final_norm"]
    ws = [meta_tokens, w_gate_up, norm_mix_w, b_gate, gla_norm_w, sinks, norm_ff_w, final_norm_w]
    gs = [g_meta, g_wg, g_norm_mix, g_b_gate, g_gla_norm, g_sinks, g_norm_ff, g_final_norm]
    ms = [m_meta_tokens, m_w_gate_up, m_norm_mix_w, m_b_gate, m_gla_norm_w, m_sinks, m_norm_ff_w, m_final_norm_w]
    vs = [v_meta_tokens, v_w_gate_up, v_norm_mix_w, v_b_gate, v_gla_norm_w, v_sinks, v_norm_ff_w, v_final_norm_w]
    flat = lambda t: t.reshape(-1, t.shape[-1])
    small_out = _adamw_small([(flat(w), flat(g), flat(m), flat(v)) for w, g, m, v in zip(ws, gs, ms, vs)])
    d_small = {n: small_out[k][0].reshape(ws[k].shape) for k, n in enumerate(names)}
    nm_small = {n: small_out[k][1].reshape(ws[k].shape) for k, n in enumerate(names)}
    nv_small = {n: small_out[k][2].reshape(ws[k].shape) for k, n in enumerate(names)}
    g_small_d = {n: g.reshape(ws[k].shape) for k, (n, g) in enumerate(zip(names, gs))}

    def ordered(big, small_d):
        win_v, wout_v, w1_v, w2_v = big
        return (small_d["meta"], small_d["norm_mix"], win_v[None], small_d["wg"], small_d["b_gate"],
                small_d["gla_norm"], small_d["sinks"], wout_v[None], small_d["norm_ff"], w1_v[None], w2_v[None],
                small_d["final_norm"])

    return (loss, grad_x,
            *ordered((g_win, g_wout, g_w1s, g_w2s), g_small_d),
            *ordered((d_win, d_wout, d_w1, d_w2), d_small),
            *ordered((nm_win, nm_wout, nm_w1, nm_w2), nm_small),
            *ordered((nv_win, nv_wout, nv_w1, nv_w2), nv_small))
```

```python
import functools

import jax
import jax.numpy as jnp
from jax import lax
from jax.experimental import pallas as pl
from jax.experimental.pallas import tpu as pltpu

F32 = jnp.float32
MXU_DTYPE = jnp.bfloat16
ACT_DTYPE = jnp.bfloat16
WIRE_DTYPE = jnp.bfloat16

D = 1024
N_META = 16
LEAD = 128
META0 = LEAD - N_META
EPS = 1e-5
GLA_HEADS, GLA_DK, GLA_DV, GLA_RANK, GLA_CHUNK = 4, 64, 128, 16, 64
GLA_TAU = 16.0
SWA_HEADS, SWA_KV, SWA_GROUP, SWA_HD, SWA_BLOCK = 8, 2, 4, 64, 128
ROPE_DIM, ROPE_THETA = 16, 500000.0
D_FF = 4096
N_DEV = 8
FF_TILE = D_FF // N_DEV
FF_WIDE = 2048
NEG = -1e30

C_GV, C_GR, C_GQ, C_GK, C_LR, C_SQ, C_SK, C_SV = 0, 512, 1024, 1280, 1536, 1664, 2176, 2304
DGLA = 1664
DINP = 2432
DIN = 2320
O_GQ, O_GK, O_GV, O_GR, O_LR, O_SQ, O_SK, O_SV = (0, 256), (256, 512), (512, 1024), (1024, 1536), (1536, 1552), (1552, 2064), (2064, 2192), (2192, 2320)

ADAM_LR, ADAM_B1, ADAM_B2, ADAM_EPS, ADAM_WD, ADAM_STEP = 0.001, 0.9, 0.999, 1e-08, 0.01, 10

MESH = pl.DeviceIdType.MESH
ANY = pl.BlockSpec(memory_space=pl.ANY)
VMEM_TILE_MB, VMEM_WIDE_MB = 48, 56


def _cp(sem=None, vmem_mb=None):
    kw = {}
    if sem is not None:
        kw["dimension_semantics"] = sem
    if vmem_mb is not None:
        kw["vmem_limit_bytes"] = vmem_mb << 20
    return pltpu.CompilerParams(**kw)


def _mm(a, b):
    return jnp.dot(a.astype(MXU_DTYPE), b.astype(MXU_DTYPE), preferred_element_type=F32)


def _mm_nt(a, b):
    return lax.dot_general(a.astype(MXU_DTYPE), b.astype(MXU_DTYPE), (((1,), (1,)), ((), ())),
                           preferred_element_type=F32)


def _mm_tn(a, b):
    return lax.dot_general(a.astype(MXU_DTYPE), b.astype(MXU_DTYPE), (((0,), (0,)), ((), ())),
                           preferred_element_type=F32)


def _masked_sums(mask, t):
    m = mask.astype(jnp.bfloat16)
    hi = t.astype(jnp.bfloat16)
    rest = t - hi.astype(F32)
    mid = rest.astype(jnp.bfloat16)
    low = (rest - mid.astype(F32)).astype(jnp.bfloat16)
    dot = lambda part: jnp.dot(m, part, preferred_element_type=F32)
    return dot(hi) + (dot(mid) + dot(low))


def _logsigmoid(z):
    return jnp.minimum(z, 0.0) - jnp.log(1.0 + jnp.exp(-jnp.abs(z)))


def _sigmoid(z):
    return 1.0 / (1.0 + jnp.exp(-z))


ROW_TILE, WIDE_ROW_TILE = 640, 1664


def _row_tile(rows, want=ROW_TILE):
    return want if rows % want == 0 else LEAD


def _mesh_pos():
    return lax.axis_index("x"), lax.axis_index("y"), lax.axis_index("c")


def _all_gather(shards):
    n = len(shards)

    def body(*refs):
        start, forward, finish = _gather_schedule(refs[:n], refs[n:2 * n], *refs[2 * n:])
        start()
        for j in range(3):
            forward(j)
        finish()

    gathered = pl.pallas_call(
        body, name="all_gather_weights",
        out_shape=_gathered_shapes(shards), in_specs=[ANY] * n, out_specs=[ANY] * n,
        scratch_shapes=_gather_sems(n),
    )(*shards)
    return _with_own_block(gathered, shards)


def _gathered_shapes(shards):
    return [jax.ShapeDtypeStruct((N_DEV,) + s.shape, s.dtype) for s in shards]


def _gather_sems(n):
    return [pltpu.SemaphoreType.DMA((7 * n,)), pltpu.SemaphoreType.DMA((7 * n,))] if n else []


def _place_gather(step, steps, shard_refs, gathered_refs, sems):
    if not shard_refs:
        return
    start, forward, finish = _gather_schedule(shard_refs, gathered_refs, *sems)
    pl.when(step == 0)(start)
    for j, at in enumerate((steps * 7 // 10, steps * 8 // 10, steps * 9 // 10)):
        pl.when(step == at)(functools.partial(forward, j))
    pl.when(step == steps - 1)(finish)


def _with_own_block(gathered, shards):
    dev = 4 * lax.axis_index("x") + 2 * lax.axis_index("y") + lax.axis_index("c")
    return [lax.dynamic_update_index_in_dim(g, s, dev, 0) for g, s in zip(gathered, shards)]


def _gather_schedule(ins, outs, send_sems, recv_sems):
    n = len(ins)
    x, y, c = _mesh_pos()
    me, sibling = (x, y, c), (x, y, 1 - c)
    chips = [(1 - x, y), (x, 1 - y), (1 - x, 1 - y)]

    def copy(a, k, block, to, src=None):
        dst = outs[a].at[4 * block[0] + 2 * block[1] + block[2]]
        return pltpu.make_async_remote_copy(
            src_ref=dst if src is None else src, dst_ref=dst,
            send_sem=send_sems.at[a * 7 + k], recv_sem=recv_sems.at[a * 7 + k],
            device_id=to, device_id_type=MESH)

    def first(a):
        return [copy(a, 0, me, sibling, src=ins[a])] + [copy(a, 1 + j, me, (*chip, c), src=ins[a])
                                                        for j, chip in enumerate(chips)]

    def start():
        for a in range(n):
            for cp in first(a):
                cp.start()

    def forward(j):
        for a in range(n):
            copy(a, 1 + j, (*chips[j], c), me).wait_recv()
            copy(a, 4 + j, (*chips[j], c), sibling).start()

    def finish():
        for a in range(n):
            copy(a, 0, sibling, me).wait_recv()
            for j, chip in enumerate(chips):
                copy(a, 4 + j, (*chip, 1 - c), me).wait_recv()
        for a in range(n):
            for cp in first(a) + [copy(a, 4 + j, (*chip, c), sibling) for j, chip in enumerate(chips)]:
                cp.wait_send()

    return start, forward, finish


def _sibling_shapes(gs):
    return [jax.ShapeDtypeStruct(g.shape[1:], g.dtype) for g in gs]


def _sibling_sems(n):
    return [pltpu.SemaphoreType.DMA((n,)), pltpu.SemaphoreType.DMA((n,))]


def _sibling_schedule(ins, land, send_sems, recv_sems):
    x, y, c = _mesh_pos()

    def copies():
        return [pltpu.make_async_remote_copy(
            src_ref=ins[a].at[1 - c], dst_ref=land[a], send_sem=send_sems.at[a], recv_sem=recv_sems.at[a],
            device_id=(x, y, 1 - c), device_id_type=MESH) for a in range(len(ins))]

    def start():
        for cp in copies():
            cp.start()

    def finish():
        for cp in copies():
            cp.wait_recv()
        for cp in copies():
            cp.wait_send()

    return start, finish


def _chips_shapes(ps):
    return [jax.ShapeDtypeStruct((3,) + p.shape[1:], p.dtype) for p in ps]


def _chips_sems(n):
    return [pltpu.SemaphoreType.DMA((3 * n,)), pltpu.SemaphoreType.DMA((3 * n,))]


def _chips_schedule(ins, land, send_sems, recv_sems):
    x, y, c = _mesh_pos()
    chips = [(1 - x, y), (x, 1 - y), (1 - x, 1 - y)]

    def copies():
        return [pltpu.make_async_remote_copy(
            src_ref=ins[a].at[2 * chip[0] + chip[1]], dst_ref=land[a].at[j],
            send_sem=send_sems.at[3 * a + j], recv_sem=recv_sems.at[3 * a + j],
            device_id=(*chip, c), device_id_type=MESH) for a in range(len(ins)) for j, chip in enumerate(chips)]

    def start():
        for cp in copies():
            cp.start()

    def finish():
        for cp in copies():
            cp.wait_recv()
        for cp in copies():
            cp.wait_send()

    return start, finish


class _Jobs:
    def __init__(self, jobs):
        self.jobs = jobs
        self.inputs = [a for _, arrs in jobs for a in arrs]
        self.out_shapes = [s for kind, arrs in jobs
                           for s in (_sibling_shapes(arrs) if kind == "sibling" else _chips_shapes(arrs))]
        self.sems = [s for kind, arrs in jobs
                     for s in (_sibling_sems(len(arrs)) if kind == "sibling" else _chips_sems(len(arrs)))]
        self.n = len(self.inputs)

    def bind(self, in_refs, out_refs, sem_refs):
        starts, finishes, at = [], [], 0
        for k, (kind, arrs) in enumerate(self.jobs):
            schedule = _sibling_schedule if kind == "sibling" else _chips_schedule
            start, finish = schedule(in_refs[at:at + len(arrs)], out_refs[at:at + len(arrs)],
                                     sem_refs[2 * k], sem_refs[2 * k + 1])
            starts.append(start)
            finishes.append(finish)
            at += len(arrs)

        def start_all():
            for f in starts:
                f()

        def finish_all():
            for f in finishes:
                f()

        return start_all, finish_all

    def split(self, outs):
        res, at = [], 0
        for _, arrs in self.jobs:
            res.append(list(outs[at:at + len(arrs)]))
            at += len(arrs)
        return res


R_META, R_WG, R_NORM_MIX, R_NORM_FF, R_FINAL, R_B_GATE, R_GLA_NORM, R_LOSS, R_SINKS, SMALL_ROWS = 0, 16, 32, 33, 34, 35, 36, 37, 40, 48


SMALL_SPECS = [pl.BlockSpec((LEAD, D), lambda i: (0, 0)), pl.BlockSpec((128, 256), lambda i: (0, 0)),
               pl.BlockSpec((8, D), lambda i: (0, 0)), pl.BlockSpec((8, D), lambda i: (0, 0)),
               pl.BlockSpec((8, D), lambda i: (0, 0)), pl.BlockSpec((8, 256), lambda i: (0, 0)),
               pl.BlockSpec((8, 128), lambda i: (0, 0)), pl.BlockSpec((8, 128), lambda i: (0, 0)),
               pl.BlockSpec((8, 128), lambda i: (0, 0))]


def _small_sum_scratch():
    return [pltpu.VMEM((SMALL_ROWS, D), F32), pltpu.VMEM((N_DEV, SMALL_ROWS, D), F32),
            pltpu.SemaphoreType.DMA((7,)), pltpu.SemaphoreType.DMA((7,))]


def _small_sum_schedule(small_refs, out_ref, p_ref, land, send_sems, recv_sems):
    dlead_ref, dwg_ref, gnm_ref, gnf_ref, gfn_ref, dbg_ref, dgnw_ref, loss_ref, dsink_ref = small_refs
    x, y, c = _mesh_pos()
    me = 4 * x + 2 * y + c

    def copies():
        res = []
        for k in range(1, N_DEV):
            bx, by, bc = (k >> 2) & 1, (k >> 1) & 1, k & 1
            peer = (1 - x if bx else x, 1 - y if by else y, 1 - c if bc else c)
            res.append(pltpu.make_async_remote_copy(
                src_ref=p_ref, dst_ref=land.at[me], send_sem=send_sems.at[k - 1], recv_sem=recv_sems.at[k - 1],
                device_id=peer, device_id_type=MESH))
        return res

    def start():
        p_ref[...] = jnp.zeros_like(p_ref)
        p_ref[R_META:R_META + N_META, :] = dlead_ref[META0:LEAD, :]
        p_ref[R_WG:R_WG + GLA_RANK, 0:256] = dwg_ref[0:GLA_RANK, :]
        p_ref[R_NORM_MIX:R_NORM_MIX + 1, :] = gnm_ref[0:1, :]
        p_ref[R_NORM_FF:R_NORM_FF + 1, :] = gnf_ref[0:1, :]
        p_ref[R_FINAL:R_FINAL + 1, :] = gfn_ref[0:1, :]
        p_ref[R_B_GATE:R_B_GATE + 1, 0:256] = dbg_ref[0:1, :]
        p_ref[R_GLA_NORM:R_GLA_NORM + 1, 0:128] = dgnw_ref[0:1, :]
        p_ref[R_LOSS:R_LOSS + 1, 0:128] = loss_ref[0:1, :]
        p_ref[R_SINKS:R_SINKS + SWA_HEADS, 0:128] = dsink_ref[...]
        land[me] = p_ref[...]
        for cp in copies():
            cp.start()

    def finish():
        for cp in copies():
            cp.wait_recv()
        for cp in copies():
            cp.wait_send()
        acc = land[0]
        for d in range(1, N_DEV):
            acc = acc + land[d]
        out_ref[...] = acc

    return start, finish


def _token_specs(tm, grid_rank=1):
    nb = tm // LEAD

    def spec(k):
        if grid_rank == 1:
            return pl.BlockSpec((LEAD, D), lambda i: (jnp.maximum(i * nb + k - 1, 0), 0))
        return pl.BlockSpec((LEAD, D), lambda i, j: (jnp.maximum(i * nb + k - 1, 0), 0))

    return [spec(k) for k in range(nb)]


def _h_tile(i, lead_ref, x_refs):
    first = jnp.where(i == 0, lead_ref[...], x_refs[0][...])
    return jnp.concatenate([first] + [r[...] for r in x_refs[1:]], axis=0)


def _in_proj(x, lead, nw, win_p, tabs, tm):
    rows = LEAD + x.shape[0]
    nb = tm // LEAD

    def body(*refs):
        x_refs, (lead_ref, nw_ref, w_ref, c_ref, sa_ref, sb_ref, o_ref, q_ref, k_ref, v_ref) = refs[:nb], refs[nb:]
        h = _h_tile(pl.program_id(0), lead_ref, x_refs)
        rstd = lax.rsqrt(jnp.mean(h * h, axis=-1, keepdims=True) + EPS)
        u = (h * rstd * nw_ref[...]).astype(MXU_DTYPE)
        proj = jnp.dot(u, w_ref[...].astype(MXU_DTYPE), preferred_element_type=F32)
        o_ref[...] = proj[:, 0:DGLA]
        cos, sa, sb = c_ref[...], sa_ref[...], sb_ref[...]
        q_ref[...] = (_rope(proj[:, C_SQ:C_SK], cos, sa, sb) * (SWA_HD ** -0.5)).astype(ACT_DTYPE)
        k_ref[...] = _rope(proj[:, C_SK:C_SV], cos, sa, sb).astype(ACT_DTYPE)
        v_ref[...] = proj[:, C_SV:DINP].astype(ACT_DTYPE)

    row = lambda w: pl.BlockSpec((tm, w), lambda i: (i, 0))
    return pl.pallas_call(
        body, name="in_proj", grid=(rows // tm,),
        in_specs=_token_specs(tm) + [pl.BlockSpec((LEAD, D), lambda i: (0, 0)), pl.BlockSpec((1, D), lambda i: (0, 0)),
                                     pl.BlockSpec((D, DINP), lambda i: (0, 0)), row(128), row(128), row(128)],
        out_specs=[row(DGLA), row(512), row(128), row(128)],
        out_shape=[jax.ShapeDtypeStruct((rows, DGLA), F32), jax.ShapeDtypeStruct((rows, 512), ACT_DTYPE),
                   jax.ShapeDtypeStruct((rows, 128), ACT_DTYPE), jax.ShapeDtypeStruct((rows, 128), ACT_DTYPE)],
        compiler_params=_cp(("arbitrary",), VMEM_WIDE_MB),
    )(*([x] * nb), lead, nw, win_p, *tabs)


def _rope_tables(rows):
    pos = (jnp.arange(rows, dtype=jnp.int32) - META0).astype(F32)
    inv_freq = 1.0 / (ROPE_THETA ** (jnp.arange(0, ROPE_DIM, 2, dtype=F32) / ROPE_DIM))
    ang = pos[:, None] * jnp.tile(inv_freq, 128 // (ROPE_DIM // 2))[None, :]
    in_head = jnp.arange(128, dtype=jnp.int32)[None, :] % SWA_HD
    cos, sin = jnp.cos(ang), jnp.sin(ang)
    c_tab = jnp.where(in_head < ROPE_DIM, cos, 1.0)
    sa_tab = jnp.where(in_head < ROPE_DIM // 2, -sin, 0.0)
    sb_tab = jnp.where((in_head >= ROPE_DIM // 2) & (in_head < ROPE_DIM), sin, 0.0)
    return c_tab, sa_tab, sb_tab


def _rope(xv, cos, sa, sb):
    width = xv.shape[1]
    reps = width // 128
    if reps > 1:
        cos, sa, sb = (jnp.tile(t, (1, reps)) for t in (cos, sa, sb))
    return xv * cos + pltpu.roll(xv, width - 8, 1) * sa + pltpu.roll(xv, 8, 1) * sb


def _unrope(dy, cos, sa, sb):
    width = dy.shape[1]
    reps = width // 128
    if reps > 1:
        cos, sa, sb = (jnp.tile(t, (1, reps)) for t in (cos, sa, sb))
    return dy * cos + pltpu.roll(dy * sa, 8, 1) + pltpu.roll(dy * sb, width - 8, 1)


def _gla_group(nc):
    for g in (5, 2):
        if nc % g == 0:
            return g
    return 1


def _chunk_masks(nrows):
    ii = lax.broadcasted_iota(jnp.int32, (nrows, nrows), 0)
    jj = lax.broadcasted_iota(jnp.int32, (nrows, nrows), 1)
    same = (ii // GLA_CHUNK) == (jj // GLA_CHUNK)
    return same & (jj <= ii), same & (jj >= ii)


def _gla_gates(lr, wg, bg, first_row):
    nrows = lr.shape[0]
    zg = _mm(lr, wg) + bg
    live = first_row + lax.broadcasted_iota(jnp.int32, (nrows, 1), 0) >= META0
    g = jnp.where(live, _logsigmoid(zg) * (1.0 / GLA_TAU), 0.0)
    return _masked_sums(_chunk_masks(nrows)[0], g), jnp.where(live, _sigmoid(-zg) * (1.0 / GLA_TAU), 0.0)


def _tril64():
    ii = lax.broadcasted_iota(jnp.int32, (GLA_CHUNK, GLA_CHUNK), 0)
    jj = lax.broadcasted_iota(jnp.int32, (GLA_CHUNK, GLA_CHUNK), 1)
    return jj <= ii


def _gla_fwd(proj, wg_p, bg, gnw, shards):
    rows = proj.shape[0]
    nc = rows // GLA_CHUNK
    group = _gla_group(nc)
    steps, nrows = nc // group, group * GLA_CHUNK
    ns = len(shards)

    def body(q_ref, k_ref, v_ref, r_ref, lr_ref, lr_next_ref, wg_ref, bg_ref, gnw_ref, *rest):
        shard_refs, rest = rest[:ns], rest[ns:]
        oraw_ref, og_ref, st_ref, decay_ref, dgate_ref = rest[:5]
        gathered_refs, rest = rest[5:5 + ns], rest[5 + ns:]
        state, gates = rest[:2]
        c = pl.program_id(0)

        @pl.when(c == 0)
        def _():
            state[...] = jnp.zeros_like(state)
            gates[0, 0], gates[0, 1] = _gla_gates(lr_ref[...], wg_ref[...], bg_ref[...], 0)

        _place_gather(c, steps, shard_refs, gathered_refs, rest[2:])
        slot = c % 2
        b = gates[slot, 0]
        decay_ref[...] = b
        dgate_ref[...] = gates[slot, 1]
        gates[1 - slot, 0], gates[1 - slot, 1] = _gla_gates(lr_next_ref[...], wg_ref[...], bg_ref[...], (c + 1) * nrows)
        eb = jnp.exp(b)
        gq = q_ref[...] * (GLA_DK ** -0.5) * eb
        gk = k_ref[...] * jnp.exp(-b)
        v = v_ref[...]
        gnw_v = gnw_ref[...]
        tril = _tril64()
        pairs = [(h, gi) for h in range(GLA_HEADS) for gi in range(group)]
        rs = {gi: slice(gi * GLA_CHUNK, (gi + 1) * GLA_CHUNK) for gi in range(group)}
        s64 = {h: slice(h * GLA_DK, (h + 1) * GLA_DK) for h in range(GLA_HEADS)}
        s128 = {h: slice(h * GLA_DV, (h + 1) * GLA_DV) for h in range(GLA_HEADS)}
        qh = {(h, gi): gq[rs[gi], s64[h]] for h, gi in pairs}
        kh = {(h, gi): gk[rs[gi], s64[h]] for h, gi in pairs}
        vh = {(h, gi): v[rs[gi], s128[h]] for h, gi in pairs}
        ebl = {(h, gi): eb[(gi + 1) * GLA_CHUNK - 1:(gi + 1) * GLA_CHUNK, s64[h]] for h, gi in pairs}
        av = {pr: _mm(jnp.where(tril, _mm_nt(qh[pr], kh[pr]), 0.0), vh[pr]) for pr in pairs}
        inc = {pr: _mm_tn(vh[pr], kh[pr] * ebl[pr]) for pr in pairs}
        st = {}
        for h in range(GLA_HEADS):
            cur = state[h]
            for gi in range(group):
                st[h, gi] = cur
                st_ref[gi, h] = cur
                cur = cur * ebl[h, gi] + inc[h, gi]
            state[h] = cur
        for h, gi in pairs:
            o = av[h, gi] + _mm_nt(qh[h, gi], st[h, gi])
            oraw_ref[rs[gi], s128[h]] = o
            rstd = lax.rsqrt(jnp.mean(o * o, axis=-1, keepdims=True) + EPS)
            rh = r_ref[rs[gi], s128[h]]
            og_ref[rs[gi], s128[h]] = (o * rstd * gnw_v * (rh * _sigmoid(rh))).astype(ACT_DTYPE)

    nb = lambda w, col: pl.BlockSpec((nrows, w), lambda c: (c, col // w))
    const = lambda shape: pl.BlockSpec(shape, lambda c: (0,) * len(shape))
    outs = pl.pallas_call(
        body, name="gla_fwd", grid=(steps,),
        in_specs=[nb(256, C_GQ), nb(256, C_GK), nb(512, C_GV), nb(512, C_GR), nb(128, C_LR),
                  pl.BlockSpec((nrows, 128), lambda c: (jnp.minimum(c + 1, steps - 1), C_LR // 128)),
                  const((128, 256)), const((1, 256)), const((1, 128))] + [ANY] * ns,
        out_specs=[nb(512, 0), nb(512, 0),
                   pl.BlockSpec((group, GLA_HEADS, GLA_DV, GLA_DK), lambda c: (c, 0, 0, 0)),
                   nb(256, 0), nb(256, 0)] + [ANY] * ns,
        out_shape=[jax.ShapeDtypeStruct((rows, 512), F32), jax.ShapeDtypeStruct((rows, 512), ACT_DTYPE),
                   jax.ShapeDtypeStruct((nc, GLA_HEADS, GLA_DV, GLA_DK), F32),
                   jax.ShapeDtypeStruct((rows, 256), F32), jax.ShapeDtypeStruct((rows, 256), F32)]
        + _gathered_shapes(shards),
        scratch_shapes=[pltpu.VMEM((GLA_HEADS, GLA_DV, GLA_DK), F32), pltpu.VMEM((2, 2, nrows, 256), F32)]
        + _gather_sems(ns),
        compiler_params=_cp(("arbitrary",)),
    )(proj, proj, proj, proj, proj, proj, wg_p, bg, gnw, *shards)
    return outs[0], outs[1], outs[2], outs[3], outs[4], _with_own_block(outs[5:], shards)


def _swa_mask(n):
    shape = (SWA_GROUP * SWA_BLOCK, 3 * SWA_BLOCK)
    qi = lax.broadcasted_iota(jnp.int32, shape, 0) & (SWA_BLOCK - 1)
    jj = lax.broadcasted_iota(jnp.int32, shape, 1)
    meta = (jj < SWA_BLOCK) & (jj >= META0) & ((n > 0) | (jj <= qi))
    prev = (jj >= SWA_BLOCK) & (jj < 2 * SWA_BLOCK) & (n >= 2) & (jj - SWA_BLOCK > qi)
    cur = (jj >= 2 * SWA_BLOCK) & (n >= 1) & (jj - 2 * SWA_BLOCK <= qi)
    return meta | prev | cur


def _stack_heads(t, kvh):
    return jnp.concatenate([t[:, (kvh * SWA_GROUP + g) * SWA_HD:(kvh * SWA_GROUP + g + 1) * SWA_HD]
                            for g in range(SWA_GROUP)], axis=0)


def _stack_sinks(sink_ref, kvh):
    return jnp.concatenate([jnp.full((SWA_BLOCK, 1), sink_ref[0, kvh * SWA_GROUP + g], F32)
                            for g in range(SWA_GROUP)], axis=0)


def _swa_group(nblk):
    return 5 if nblk % 5 == 0 else 1


def _swa_specs(group):
    blk = lambda w: pl.BlockSpec((group * SWA_BLOCK, w), lambda n: (n, 0))
    first = pl.BlockSpec((SWA_BLOCK, 128), lambda n: (0, 0))
    prev = pl.BlockSpec((SWA_BLOCK, 128), lambda n: (jnp.maximum(n * group - 1, 0), 0))
    return blk, first, prev


def _swa_keys(first_ref, prev_ref, cur_ref, g):
    own = cur_ref[g * SWA_BLOCK:(g + 1) * SWA_BLOCK, :]
    before = prev_ref[...] if g == 0 else cur_ref[(g - 1) * SWA_BLOCK:g * SWA_BLOCK, :]
    return jnp.concatenate([first_ref[...], before, own], axis=0)


def _swa_fwd(qr, kr, vr, sinks, shards):
    rows = qr.shape[0]
    nblk = rows // SWA_BLOCK
    group = _swa_group(nblk)
    steps = nblk // group
    ns = len(shards)

    def body(q_ref, k0, kp, kc, v0, vp, vc, sink_ref, *rest):
        o_ref = rest[ns]
        _place_gather(pl.program_id(0), steps, rest[:ns], rest[ns + 1:2 * ns + 1], rest[2 * ns + 1:])
        for g in range(group):
            n = pl.program_id(0) * group + g
            rs = slice(g * SWA_BLOCK, (g + 1) * SWA_BLOCK)
            kall, vall = _swa_keys(k0, kp, kc, g), _swa_keys(v0, vp, vc, g)
            mask = _swa_mask(n)[0:SWA_BLOCK]
            heads = range(SWA_HEADS)
            hs = [slice(h * SWA_HD, (h + 1) * SWA_HD) for h in heads]
            kv = [slice((h // SWA_GROUP) * SWA_HD, (h // SWA_GROUP + 1) * SWA_HD) for h in heads]
            s = [jnp.where(mask, _mm_nt(q_ref[rs, hs[h]], kall[:, kv[h]]), NEG) for h in heads]
            m = [jnp.maximum(jnp.max(s[h], axis=-1, keepdims=True), sink_ref[0, h]) for h in heads]
            p = [jnp.exp(s[h] - m[h]) for h in heads]
            den = [jnp.sum(p[h], axis=-1, keepdims=True) + jnp.exp(sink_ref[0, h] - m[h]) for h in heads]
            o = [_mm(p[h], vall[:, kv[h]]) for h in heads]
            for h in heads:
                o_ref[rs, hs[h]] = (o[h] / den[h]).astype(ACT_DTYPE)

    blk, first, prev = _swa_specs(group)
    outs = pl.pallas_call(
        body, name="swa_fwd", grid=(steps,),
        in_specs=[blk(512), first, prev, blk(128), first, prev, blk(128),
                  pl.BlockSpec(memory_space=pltpu.SMEM)] + [ANY] * ns,
        out_specs=[blk(512)] + [ANY] * ns,
        out_shape=[jax.ShapeDtypeStruct((rows, 512), ACT_DTYPE)] + _gathered_shapes(shards),
        scratch_shapes=_gather_sems(ns),
        compiler_params=_cp(("arbitrary",)),
    )(qr, kr, kr, kr, vr, vr, vr, sinks, *shards)
    return outs[0], _with_own_block(outs[1:], shards)


def _out_proj(x, lead, og, osw, wout, nfw, tm):
    rows = LEAD + x.shape[0]
    nb = tm // LEAD

    def body(*refs):
        x_refs, (lead_ref, og_ref, os_ref, w_ref, nw_ref, h1_ref, f_ref, ft_ref) = refs[:nb], refs[nb:]
        h0 = _h_tile(pl.program_id(0), lead_ref, x_refs)
        h1 = h0 + _mm(og_ref[...], w_ref[0:512, :]) + _mm(os_ref[...], w_ref[512:1024, :])
        h1_ref[...] = h1
        rstd = lax.rsqrt(jnp.mean(h1 * h1, axis=-1, keepdims=True) + EPS)
        f = h1 * rstd * nw_ref[...]
        f_ref[...] = f.astype(ACT_DTYPE)
        ft_ref[...] = f.T.astype(ACT_DTYPE)

    row = lambda w: pl.BlockSpec((tm, w), lambda i: (i, 0))
    return pl.pallas_call(
        body, name="out_proj", grid=(rows // tm,),
        in_specs=_token_specs(tm) + [pl.BlockSpec((LEAD, D), lambda i: (0, 0)), row(512), row(512),
                                     pl.BlockSpec((D, D), lambda i: (0, 0)), pl.BlockSpec((1, D), lambda i: (0, 0))],
        out_specs=[row(D), row(D), pl.BlockSpec((D, tm), lambda i: (0, i))],
        out_shape=[jax.ShapeDtypeStruct((rows, D), F32), jax.ShapeDtypeStruct((rows, D), ACT_DTYPE),
                   jax.ShapeDtypeStruct((D, rows), ACT_DTYPE)],
        compiler_params=_cp(("arbitrary",), VMEM_TILE_MB),
    )(*([x] * nb), lead, og, osw, wout, nfw)


def _ffn_fwd(f, h1, w1, w2, tgt, fnw, tm):
    rows = f.shape[0]
    nj = D_FF // FF_WIDE
    nb = tm // LEAD

    def body(f_ref, h1_ref, w1_ref, w2_ref, nw_ref, *rest):
        t_refs, (a_ref, dh2_ref, dh2t_ref, loss_ref, gfn_ref, acc) = rest[:nb], rest[nb:]
        i, j = pl.program_id(0), pl.program_id(1)

        @pl.when((i == 0) & (j == 0))
        def _():
            loss_ref[...] = jnp.zeros_like(loss_ref)
            gfn_ref[...] = jnp.zeros_like(gfn_ref)

        @pl.when(j == 0)
        def _():
            acc[...] = jnp.zeros_like(acc)

        a = _mm(f_ref[...], w1_ref[...])
        a_ref[...] = a.astype(ACT_DTYPE)
        z = jnp.square(jnp.maximum(a, 0.0))
        acc[...] += _mm(z, w2_ref[...])

        @pl.when(j == nj - 1)
        def _():
            h2 = h1_ref[...] + acc[...]
            rstd = lax.rsqrt(jnp.mean(h2 * h2, axis=-1, keepdims=True) + EPS)
            hn = h2 * rstd
            nw = nw_ref[...]
            row = i * tm + lax.broadcasted_iota(jnp.int32, (tm, 1), 0)
            target = jnp.concatenate([t[...] for t in t_refs], axis=0)
            err = jnp.where(row >= LEAD, hn * nw - target, 0.0)
            row_loss = jnp.sum(err * err, axis=-1, keepdims=True) * (1.0 / D)
            loss_ref[...] += jnp.broadcast_to(0.5 * jnp.sum(row_loss, axis=0, keepdims=True), loss_ref.shape)
            dy = err * (1.0 / D)
            gfn_ref[...] += jnp.broadcast_to(jnp.sum(dy * hn, axis=0, keepdims=True), gfn_ref.shape)
            dhn = dy * nw
            dh2 = rstd * (dhn - hn * jnp.mean(dhn * hn, axis=-1, keepdims=True))
            dh2_ref[...] = dh2
            dh2t_ref[...] = dh2.T.astype(ACT_DTYPE)

    return pl.pallas_call(
        body, name="ffn_fwd", grid=(rows // tm, nj),
        in_specs=[pl.BlockSpec((tm, D), lambda i, j: (i, 0)), pl.BlockSpec((tm, D), lambda i, j: (i, 0)),
                  pl.BlockSpec((D, FF_WIDE), lambda i, j: (0, j)),
                  pl.BlockSpec((FF_WIDE, D), lambda i, j: (j, 0)),
                  pl.BlockSpec((1, D), lambda i, j: (0, 0))] + _token_specs(tm, grid_rank=2),
        out_specs=[pl.BlockSpec((tm, FF_WIDE), lambda i, j: (i, j)), pl.BlockSpec((tm, D), lambda i, j: (i, 0)),
                   pl.BlockSpec((D, tm), lambda i, j: (0, i)),
                   pl.BlockSpec((8, 128), lambda i, j: (0, 0)), pl.BlockSpec((8, D), lambda i, j: (0, 0))],
        out_shape=[jax.ShapeDtypeStruct((rows, D_FF), ACT_DTYPE), jax.ShapeDtypeStruct((rows, D), F32),
                   jax.ShapeDtypeStruct((D, rows), ACT_DTYPE),
                   jax.ShapeDtypeStruct((8, 128), F32), jax.ShapeDtypeStruct((8, D), F32)],
        scratch_shapes=[pltpu.VMEM((tm, D), F32)],
        compiler_params=_cp(("arbitrary", "arbitrary"), VMEM_WIDE_MB),
    )(f, h1, w1, w2, fnw, *([tgt] * nb))


def _ffn_bwd_act(dh2, a, w1, w2, h1, nfw, tm):
    rows = dh2.shape[0]
    nj = D_FF // FF_WIDE

    def body(dh2_ref, a_ref, w1_ref, w2_ref, h1_ref, nw_ref, da_ref, dh1_ref, gnf_ref, acc):
        i, j = pl.program_id(0), pl.program_id(1)

        @pl.when((i == 0) & (j == 0))
        def _():
            gnf_ref[...] = jnp.zeros_like(gnf_ref)

        @pl.when(j == 0)
        def _():
            acc[...] = jnp.zeros_like(acc)

        dz = _mm_nt(dh2_ref[...], w2_ref[...])
        da = dz * (2.0 * jnp.maximum(a_ref[...].astype(F32), 0.0))
        da_ref[...] = da.astype(ACT_DTYPE)
        acc[...] += _mm_nt(da, w1_ref[...])

        @pl.when(j == nj - 1)
        def _():
            h1 = h1_ref[...]
            rstd = lax.rsqrt(jnp.mean(h1 * h1, axis=-1, keepdims=True) + EPS)
            hn = h1 * rstd
            df = acc[...]
            gnf_ref[...] += jnp.broadcast_to(jnp.sum(df * hn, axis=0, keepdims=True), gnf_ref.shape)
            dfn = df * nw_ref[...]
            dh1_ref[...] = dh2_ref[...] + rstd * (dfn - hn * jnp.mean(dfn * hn, axis=-1, keepdims=True))

    return pl.pallas_call(
        body, name="ffn_bwd_act", grid=(rows // tm, nj),
        in_specs=[pl.BlockSpec((tm, D), lambda i, j: (i, 0)), pl.BlockSpec((tm, FF_WIDE), lambda i, j: (i, j)),
                  pl.BlockSpec((D, FF_WIDE), lambda i, j: (0, j)),
                  pl.BlockSpec((FF_WIDE, D), lambda i, j: (j, 0)),
                  pl.BlockSpec((tm, D), lambda i, j: (i, 0)), pl.BlockSpec((1, D), lambda i, j: (0, 0))],
        out_specs=[pl.BlockSpec((tm, FF_WIDE), lambda i, j: (i, j)), pl.BlockSpec((tm, D), lambda i, j: (i, 0)),
                   pl.BlockSpec((8, D), lambda i, j: (0, 0))],
        out_shape=[jax.ShapeDtypeStruct((rows, D_FF), ACT_DTYPE), jax.ShapeDtypeStruct((rows, D), F32),
                   jax.ShapeDtypeStruct((8, D), F32)],
        scratch_shapes=[pltpu.VMEM((tm, D), F32)],
        compiler_params=_cp(("arbitrary", "arbitrary"), VMEM_WIDE_MB),
    )(dh2, a, w1, w2, h1, nfw)


def _ffn_bwd_weights(ft, a, da, dh2t, tm):
    rows = a.shape[0]
    steps = rows // tm
    pair = 2 * FF_TILE

    def body(ft_ref, a_ref, da_ref, dh2t_ref, dw1_ref, dw2_ref, dw2t):
        i = pl.program_id(1)

        @pl.when(i == 0)
        def _():
            dw1_ref[...] = jnp.zeros_like(dw1_ref)
            dw2t[...] = jnp.zeros_like(dw2t)

        z = jnp.square(jnp.maximum(a_ref[...].astype(F32), 0.0))
        dw1 = _mm(ft_ref[...], da_ref[...])
        for core in range(2):
            dw1_ref[core] += dw1[:, core * FF_TILE:(core + 1) * FF_TILE]
        dw2t[...] += _mm(dh2t_ref[...], z)

        @pl.when(i == steps - 1)
        def _():
            for core in range(2):
                dw2_ref[core] = dw2t[:, core * FF_TILE:(core + 1) * FF_TILE].T

    return pl.pallas_call(
        body, name="ffn_bwd_weights", grid=(N_DEV // 2, steps),
        in_specs=[pl.BlockSpec((D, tm), lambda j, i: (0, i)), pl.BlockSpec((tm, pair), lambda j, i: (i, j)),
                  pl.BlockSpec((tm, pair), lambda j, i: (i, j)), pl.BlockSpec((D, tm), lambda j, i: (0, i))],
        out_specs=[pl.BlockSpec((2, None, D, FF_TILE), lambda j, i: (0, j, 0, 0)),
                   pl.BlockSpec((2, None, FF_TILE, D), lambda j, i: (0, j, 0, 0))],
        out_shape=[jax.ShapeDtypeStruct((2, 4, D, FF_TILE), F32), jax.ShapeDtypeStruct((2, 4, FF_TILE, D), F32)],
        scratch_shapes=[pltpu.VMEM((D, pair), F32)],
        compiler_params=_cp(("arbitrary", "arbitrary"), VMEM_WIDE_MB),
    )(ft, a, da, dh2t)


def _out_proj_bwd(dh1, og, osw, wout, tm, partials):
    rows = dh1.shape[0]
    steps = rows // tm
    ns = len(partials)

    def body(dh1_ref, og_ref, os_ref, w_ref, *rest):
        part_refs, rest = rest[:ns], rest[ns:]
        dog_ref, dos_ref, dw_ref = rest[:3]
        land_refs, (send_sems, recv_sems) = rest[3:3 + ns], rest[3 + ns:]
        i = pl.program_id(0)
        start, finish = _sibling_schedule(part_refs, land_refs, send_sems, recv_sems)

        @pl.when(i == 0)
        def _():
            dw_ref[...] = jnp.zeros_like(dw_ref)
            start()

        pl.when(i == steps - 1)(finish)

        dh1 = dh1_ref[...].astype(MXU_DTYPE)
        dog_ref[...] = _mm_nt(dh1, w_ref[0:512, :])
        dos_ref[...] = _mm_nt(dh1, w_ref[512:1024, :])
        for half, ref in enumerate((og_ref, os_ref)):
            dw = _mm_tn(ref[...], dh1)
            for blk in range(4):
                shard = half * 4 + blk
                dw_ref[shard % 2, shard // 2] += dw[blk * 128:(blk + 1) * 128, :]

    row = lambda w: pl.BlockSpec((tm, w), lambda i: (i, 0))
    outs = pl.pallas_call(
        body, name="out_proj_bwd", grid=(steps,),
        in_specs=[row(D), row(512), row(512), pl.BlockSpec((D, D), lambda i: (0, 0))] + [ANY] * ns,
        out_specs=[row(512), row(512), pl.BlockSpec((2, 4, 128, D), lambda i: (0, 0, 0, 0))] + [ANY] * ns,
        out_shape=[jax.ShapeDtypeStruct((rows, 512), F32), jax.ShapeDtypeStruct((rows, 512), F32),
                   jax.ShapeDtypeStruct((2, 4, 128, D), F32)] + _sibling_shapes(partials),
        scratch_shapes=_sibling_sems(ns),
        compiler_params=_cp(("arbitrary",), VMEM_TILE_MB),
    )(dh1, og, osw, wout, *partials)
    return outs[0], outs[1], outs[2], outs[3:]


def _swa_bwd(qr, kr, vr, osw, dos, sinks, jobs):
    rows = qr.shape[0]
    nblk = rows // SWA_BLOCK
    group = _swa_group(nblk)
    steps = nblk // group
    ns = jobs.n

    def body(q_ref, k0, kp, kc, v0, vp, vc, o_ref, do_ref, sink_ref, *rest):
        dq_ref, dk_ref, dv_ref, dsink_ref = rest[ns:ns + 4]
        start, finish = jobs.bind(rest[:ns], rest[ns + 4:2 * ns + 4], rest[2 * ns + 4:])
        step = pl.program_id(0)

        @pl.when(step == 0)
        def _():
            dk_ref[...] = jnp.zeros_like(dk_ref)
            dv_ref[...] = jnp.zeros_like(dv_ref)
            dsink_ref[...] = jnp.zeros_like(dsink_ref)
            start()

        pl.when(step == steps - 1)(finish)
        for g in range(group):
            block(step * group + g, g, q_ref, k0, kp, kc, v0, vp, vc, o_ref, do_ref, sink_ref,
                  dq_ref, dk_ref, dv_ref, dsink_ref)

    def block(n, g, q_ref, k0, kp, kc, v0, vp, vc, o_ref, do_ref, sink_ref, dq_ref, dk_ref, dv_ref, dsink_ref):
        rs = slice(g * SWA_BLOCK, (g + 1) * SWA_BLOCK)
        kall, vall = _swa_keys(k0, kp, kc, g), _swa_keys(v0, vp, vc, g)
        mask = _swa_mask(n)[0:SWA_BLOCK]
        heads = range(SWA_HEADS)
        hs = [slice(h * SWA_HD, (h + 1) * SWA_HD) for h in heads]
        kv = [slice((h // SWA_GROUP) * SWA_HD, (h // SWA_GROUP + 1) * SWA_HD) for h in heads]
        sink = [sink_ref[0, h] for h in heads]
        qh = [q_ref[rs, hs[h]] for h in heads]
        doh = [do_ref[rs, hs[h]] for h in heads]
        s = [jnp.where(mask, _mm_nt(qh[h], kall[:, kv[h]]), NEG) for h in heads]
        dp = [_mm_nt(doh[h], vall[:, kv[h]]) for h in heads]
        delta = [jnp.sum(doh[h] * o_ref[rs, hs[h]].astype(F32), axis=-1, keepdims=True) for h in heads]
        m = [jnp.maximum(jnp.max(s[h], axis=-1, keepdims=True), sink[h]) for h in heads]
        e = [jnp.exp(s[h] - m[h]) for h in heads]
        inv = [1.0 / (jnp.sum(e[h], axis=-1, keepdims=True) + jnp.exp(sink[h] - m[h])) for h in heads]
        p = [e[h] * inv[h] for h in heads]
        ds = [p[h] * (dp[h] - delta[h]) for h in heads]
        dq = [_mm(ds[h], kall[:, kv[h]]) for h in heads]
        dkh = [_mm_tn(ds[h], qh[h]) for h in heads]
        dvh = [_mm_tn(p[h], doh[h]) for h in heads]
        for h in heads:
            dsink = -jnp.sum(jnp.exp(sink[h] - m[h]) * inv[h] * delta[h], axis=0, keepdims=True)
            dsink_ref[h:h + 1, :] += jnp.broadcast_to(dsink, (1, 128))
        dq_ref[rs, :] = jnp.concatenate(dq, axis=1)
        group_sum = lambda parts, kvh: sum(parts[kvh * SWA_GROUP + 1:(kvh + 1) * SWA_GROUP], parts[kvh * SWA_GROUP])
        dk_all = jnp.concatenate([group_sum(dkh, kvh) for kvh in range(SWA_KV)], axis=1)
        dv_all = jnp.concatenate([group_sum(dvh, kvh) for kvh in range(SWA_KV)], axis=1)
        prev0 = pl.multiple_of(jnp.maximum(n - 1, 0) * SWA_BLOCK, SWA_BLOCK)
        cur0 = pl.multiple_of(n * SWA_BLOCK, SWA_BLOCK)
        for ref, val in ((dk_ref, dk_all), (dv_ref, dv_all)):
            ref[0:SWA_BLOCK, :] += val[0:SWA_BLOCK]
            ref[pl.ds(prev0, SWA_BLOCK), :] += val[SWA_BLOCK:2 * SWA_BLOCK]
            ref[pl.ds(cur0, SWA_BLOCK), :] += val[2 * SWA_BLOCK:]

    blk, first, prev = _swa_specs(group)
    whole = pl.BlockSpec((rows, 128), lambda n: (0, 0))
    outs = pl.pallas_call(
        body, name="swa_bwd", grid=(steps,),
        in_specs=[blk(512), first, prev, blk(128), first, prev, blk(128), blk(512), blk(512),
                  pl.BlockSpec(memory_space=pltpu.SMEM)] + [ANY] * ns,
        out_specs=[blk(512), whole, whole, pl.BlockSpec((8, 128), lambda n: (0, 0))] + [ANY] * ns,
        out_shape=[jax.ShapeDtypeStruct((rows, 512), F32), jax.ShapeDtypeStruct((rows, 128), F32),
                   jax.ShapeDtypeStruct((rows, 128), F32), jax.ShapeDtypeStruct((8, 128), F32)] + jobs.out_shapes,
        scratch_shapes=jobs.sems,
        compiler_params=_cp(("arbitrary",), VMEM_TILE_MB),
    )(qr, kr, kr, kr, vr, vr, vr, osw, dos, sinks, *jobs.inputs)
    return outs[0], outs[1], outs[2], outs[3], jobs.split(outs[4:])


def _gla_bwd(proj, decay, dgate, oraw, states, dog, wg_p, gnw, jobs):
    rows = proj.shape[0]
    nc = rows // GLA_CHUNK
    group = _gla_group(nc)
    steps, nrows = nc // group, group * GLA_CHUNK
    ns = jobs.n

    def body(q_ref, k_ref, v_ref, r_ref, lr_ref, b_ref, dgate_ref, oraw_ref, st_ref, dog_ref, wg_ref, gnw_ref, *rest):
        dq_ref, dk_ref, dv_ref, dr_ref, dlr_ref, dwg_ref, dbg_ref, dgnw_ref = rest[ns:ns + 8]
        dstate, db_scr = rest[2 * ns + 8:2 * ns + 10]
        start, finish = jobs.bind(rest[:ns], rest[ns + 8:2 * ns + 8], rest[2 * ns + 10:])
        t = pl.program_id(0)

        @pl.when(t == 0)
        def _():
            dstate[...] = jnp.zeros_like(dstate)
            dwg_ref[...] = jnp.zeros_like(dwg_ref)
            dbg_ref[...] = jnp.zeros_like(dbg_ref)
            dgnw_ref[...] = jnp.zeros_like(dgnw_ref)
            start()

        pl.when(t == steps - 1)(finish)

        lr, wg = lr_ref[...], wg_ref[...]
        b = b_ref[...]
        eb, enb = jnp.exp(b), jnp.exp(-b)
        scale = GLA_DK ** -0.5
        gq = q_ref[...] * scale * eb
        gk = k_ref[...] * enb
        v = v_ref[...]
        gnw_v = gnw_ref[...]
        tril = _tril64()
        is_last = lax.broadcasted_iota(jnp.int32, (GLA_CHUNK, 1), 0) == GLA_CHUNK - 1
        dgnw = jnp.zeros((1, GLA_DV), F32)
        pairs = [(h, gi) for h in range(GLA_HEADS) for gi in range(group)]
        rs = {gi: slice(gi * GLA_CHUNK, (gi + 1) * GLA_CHUNK) for gi in range(group)}
        s64 = {h: slice(h * GLA_DK, (h + 1) * GLA_DK) for h in range(GLA_HEADS)}
        s128 = {h: slice(h * GLA_DV, (h + 1) * GLA_DV) for h in range(GLA_HEADS)}
        qh = {(h, gi): gq[rs[gi], s64[h]] for h, gi in pairs}
        kh = {(h, gi): gk[rs[gi], s64[h]] for h, gi in pairs}
        vh = {(h, gi): v[rs[gi], s128[h]] for h, gi in pairs}
        ebl = {(h, gi): eb[(gi + 1) * GLA_CHUNK - 1:(gi + 1) * GLA_CHUNK, s64[h]] for h, gi in pairs}
        kl = {pr: kh[pr] * ebl[pr] for pr in pairs}
        st = {(h, gi): st_ref[gi, h] for h, gi in pairs}
        do = {}
        for h, gi in pairs:
            o, rh, dout = oraw_ref[rs[gi], s128[h]], r_ref[rs[gi], s128[h]], dog_ref[rs[gi], s128[h]]
            rstd = lax.rsqrt(jnp.mean(o * o, axis=-1, keepdims=True) + EPS)
            on = o * rstd
            sg = _sigmoid(rh)
            dr_ref[rs[gi], s128[h]] = (dout * (on * gnw_v) * (sg * (1.0 + rh * (1.0 - sg)))).astype(ACT_DTYPE)
            dy = dout * (rh * sg)
            dgnw = dgnw + jnp.sum(dy * on, axis=0, keepdims=True)
            don = dy * gnw_v
            do[h, gi] = rstd * (don - on * jnp.mean(don * on, axis=-1, keepdims=True))
        a = {pr: jnp.where(tril, _mm_nt(qh[pr], kh[pr]), 0.0) for pr in pairs}
        da = {pr: jnp.where(tril, _mm_nt(do[pr], vh[pr]), 0.0) for pr in pairs}
        dinc = {pr: _mm_tn(do[pr], qh[pr]) for pr in pairs}
        dgq = {pr: _mm(da[pr], kh[pr]) + _mm(do[pr], st[pr]) for pr in pairs}
        dgk = {pr: _mm_tn(da[pr], qh[pr]) for pr in pairs}
        dv_a = {pr: _mm_tn(a[pr], do[pr]) for pr in pairs}
        dsp = {}
        for h in range(GLA_HEADS):
            cur = dstate[h]
            for gi in reversed(range(group)):
                dsp[h, gi] = cur
                cur = cur * ebl[h, gi] + dinc[h, gi]
            dstate[h] = cur
        for h, gi in pairs:
            pr = (h, gi)
            dkl = _mm(vh[pr], dsp[pr])
            dv_ref[rs[gi], s128[h]] = (dv_a[pr] + _mm_nt(kl[pr], dsp[pr])).astype(ACT_DTYPE)
            debl = jnp.sum(dsp[pr] * st[pr], axis=0, keepdims=True)
            dq_ref[rs[gi], s64[h]] = (dgq[pr] * (scale * eb[rs[gi], s64[h]])).astype(ACT_DTYPE)
            dk_ref[rs[gi], s64[h]] = ((dgk[pr] + dkl * ebl[pr]) * enb[rs[gi], s64[h]]).astype(ACT_DTYPE)
            last = debl * ebl[pr] + jnp.sum(dkl * kl[pr], axis=0, keepdims=True)
            db_scr[rs[gi], s64[h]] = (dgq[pr] * qh[pr] - dgk[pr] * kh[pr] - dkl * kl[pr]
                                      + jnp.where(is_last, last, 0.0))
        dzg = _masked_sums(_chunk_masks(nrows)[1], db_scr[...]) * dgate_ref[...]
        dlr_ref[...] = _mm_nt(dzg, wg).astype(ACT_DTYPE)
        dwg_ref[...] += _mm_tn(lr, dzg)
        dbg_ref[...] += jnp.broadcast_to(jnp.sum(dzg, axis=0, keepdims=True), dbg_ref.shape)
        dgnw_ref[...] += jnp.broadcast_to(dgnw, dgnw_ref.shape)

    nb = lambda w, col: pl.BlockSpec((nrows, w), lambda t: (steps - 1 - t, col // w))
    const = lambda shape: pl.BlockSpec(shape, lambda t: (0,) * len(shape))
    outs = pl.pallas_call(
        body, name="gla_bwd", grid=(steps,),
        in_specs=[nb(256, C_GQ), nb(256, C_GK), nb(512, C_GV), nb(512, C_GR), nb(128, C_LR), nb(256, 0), nb(256, 0),
                  nb(512, 0),
                  pl.BlockSpec((group, GLA_HEADS, GLA_DV, GLA_DK), lambda t: (steps - 1 - t, 0, 0, 0)), nb(512, 0),
                  const((128, 256)), const((1, 128))] + [ANY] * ns,
        out_specs=[nb(256, 0), nb(256, 0), nb(512, 0), nb(512, 0), nb(128, 0),
                   const((128, 256)), const((8, 256)), const((8, 128))] + [ANY] * ns,
        out_shape=[jax.ShapeDtypeStruct((rows, 256), ACT_DTYPE), jax.ShapeDtypeStruct((rows, 256), ACT_DTYPE),
                   jax.ShapeDtypeStruct((rows, 512), ACT_DTYPE), jax.ShapeDtypeStruct((rows, 512), ACT_DTYPE),
                   jax.ShapeDtypeStruct((rows, 128), ACT_DTYPE), jax.ShapeDtypeStruct((128, 256), F32),
                   jax.ShapeDtypeStruct((8, 256), F32), jax.ShapeDtypeStruct((8, 128), F32)] + jobs.out_shapes,
        scratch_shapes=[pltpu.VMEM((GLA_HEADS, GLA_DV, GLA_DK), F32), pltpu.VMEM((nrows, 256), F32)] + jobs.sems,
        compiler_params=_cp(("arbitrary",)),
    )(proj, proj, proj, proj, proj, decay, dgate, oraw, states, dog, wg_p, gnw, *jobs.inputs)
    return outs[:8], jobs.split(outs[8:])


def _in_proj_bwd(x, lead, dh1, nw, win_p, dgv, dgr, dsq, dgq, dgk, dsk, dsv, dlr, tabs, tm):
    seq = x.shape[0]
    rows = LEAD + seq
    nb = tm // LEAD
    steps = rows // tm

    def first_copy(scr, gx_ref, sem):
        return pltpu.make_async_copy(scr.at[pl.ds(LEAD, tm - LEAD)], gx_ref.at[pl.ds(0, tm - LEAD)], sem)

    def tile_copy(scr, gx_ref, sem, step):
        start = pl.multiple_of(jnp.maximum(step * tm - LEAD, 0), LEAD)
        return pltpu.make_async_copy(scr, gx_ref.at[pl.ds(start, tm)], sem)

    def body(*refs):
        x_refs, refs = refs[:nb], refs[nb:]
        (lead_ref, dh1_ref, nw_ref, w_ref, dgv_ref, dgr_ref, dsq_ref, dgq_ref, dgk_ref, dsk_ref, dsv_ref, dlr_ref,
         c_ref, sa_ref, sb_ref, gx_ref, dlead_ref, dproj_ref, ut_ref, gnm_ref, scr, sem) = refs
        i = pl.program_id(0)

        @pl.when(i == 0)
        def _():
            gnm_ref[...] = jnp.zeros_like(gnm_ref)

        cos, sa, sb = c_ref[...], sa_ref[...], sb_ref[...]
        dsq_v = (_unrope(dsq_ref[...], cos, sa, sb) * (SWA_HD ** -0.5)).astype(MXU_DTYPE)
        dsk_v = _unrope(dsk_ref[...], cos, sa, sb).astype(MXU_DTYPE)
        dproj = jnp.concatenate(
            [dgv_ref[...].astype(MXU_DTYPE), dgr_ref[...].astype(MXU_DTYPE), dgq_ref[...].astype(MXU_DTYPE),
             dgk_ref[...].astype(MXU_DTYPE), dlr_ref[...].astype(MXU_DTYPE), dsq_v, dsk_v,
             dsv_ref[...].astype(MXU_DTYPE)],
            axis=1)
        dproj_ref[...] = dproj
        h = _h_tile(i, lead_ref, x_refs)
        rstd = lax.rsqrt(jnp.mean(h * h, axis=-1, keepdims=True) + EPS)
        hn = h * rstd
        nw_v = nw_ref[...]
        ut_ref[...] = (hn * nw_v).T.astype(ACT_DTYPE)
        du = _mm_nt(dproj, w_ref[...])
        gnm_ref[...] += jnp.broadcast_to(jnp.sum(du * hn, axis=0, keepdims=True), gnm_ref.shape)
        dun = du * nw_v
        dh0 = dh1_ref[...] + rstd * (dun - hn * jnp.mean(dun * hn, axis=-1, keepdims=True))

        if tm > LEAD:
            pl.when(i == 1)(lambda: first_copy(scr, gx_ref, sem).wait())
        pl.when(i > 1)(lambda: tile_copy(scr, gx_ref, sem, i).wait())
        scr[...] = dh0

        @pl.when(i == 0)
        def _():
            dlead_ref[...] = dh0[0:LEAD]
            if tm > LEAD:
                first_copy(scr, gx_ref, sem).start()
                if steps == 1:
                    first_copy(scr, gx_ref, sem).wait()

        @pl.when(i > 0)
        def _():
            tile_copy(scr, gx_ref, sem, i).start()

        if steps > 1:
            pl.when(i == steps - 1)(lambda: tile_copy(scr, gx_ref, sem, i).wait())

    row = lambda w: pl.BlockSpec((tm, w), lambda i: (i, 0))
    const = lambda shape: pl.BlockSpec(shape, lambda i: (0,) * len(shape))
    return pl.pallas_call(
        body, name="in_proj_bwd", grid=(steps,),
        in_specs=_token_specs(tm) + [const((LEAD, D)), row(D), const((1, D)), const((D, DINP)),
                                     row(512), row(512), row(512), row(256), row(256), row(128), row(128), row(128),
                                     row(128), row(128), row(128)],
        out_specs=[ANY, const((LEAD, D)), row(DINP), pl.BlockSpec((D, tm), lambda i: (0, i)), const((8, D))],
        out_shape=[jax.ShapeDtypeStruct((seq, D), F32), jax.ShapeDtypeStruct((LEAD, D), F32),
                   jax.ShapeDtypeStruct((rows, DINP), ACT_DTYPE), jax.ShapeDtypeStruct((D, rows), ACT_DTYPE),
                   jax.ShapeDtypeStruct((8, D), F32)],
        scratch_shapes=[pltpu.VMEM((tm, D), F32), pltpu.SemaphoreType.DMA],
        compiler_params=_cp(("arbitrary",), VMEM_WIDE_MB),
    )(*([x] * nb), lead, dh1, nw, win_p, dgv, dgr, dsq, dgq, dgk, dsk, dsv, dlr, *tabs)


def _win_runs():
    groups = [(O_GQ, C_GQ), (O_GK, C_GK), (O_GV, C_GV), (O_GR, C_GR), (O_LR, C_LR), (O_SQ, C_SQ), (O_SK, C_SK),
              (O_SV, C_SV)]
    per = DIN // N_DEV
    runs = []
    for (o0, o1), c0 in groups:
        o = o0
        while o < o1:
            d = o // per
            end = min(o1, (d + 1) * per)
            runs.append((d, o - d * per, c0 + o - o0, end - o))
            o = end
    return runs


def _win_padded(g_in):
    tr = 128

    def body(g_ref, o_ref):
        o_ref[...] = jnp.zeros_like(o_ref)
        for d, s, c, w in _win_runs():
            o_ref[:, c:c + w] = g_ref[d, :, s:s + w]

    return pl.pallas_call(
        body, name="w_in_layout", grid=(D // tr,),
        in_specs=[pl.BlockSpec((N_DEV, tr, DIN // N_DEV), lambda i: (0, i, 0))],
        out_specs=pl.BlockSpec((tr, DINP), lambda i: (i, 0)),
        out_shape=jax.ShapeDtypeStruct((D, DINP), g_in.dtype),
        compiler_params=_cp(("arbitrary",)),
    )(g_in)


def _in_proj_bwd_weights(ut, dproj, tm, small):
    rows = dproj.shape[0]
    steps = rows // tm
    per = DIN // N_DEV

    def body(ut_ref, dp_ref, *rest):
        small_refs, (mine_ref, theirs_ref, total_ref, acc, stage, local_sems, send_sems, recv_sems) = rest[:9], rest[9:17]
        i = pl.program_id(0)
        start, finish = _small_sum_schedule(small_refs, total_ref, *rest[17:])
        x, y, c = _mesh_pos()

        @pl.when(i == 0)
        def _():
            acc[...] = jnp.zeros_like(acc)
            start()

        acc[...] += _mm(ut_ref[...], dp_ref[...])
        pl.when(i == steps - 1)(finish)

        def keep(slot, chip):
            return pltpu.make_async_copy(stage.at[slot], mine_ref.at[chip], local_sems.at[slot])

        def send(slot, chip):
            return pltpu.make_async_remote_copy(
                src_ref=stage.at[slot], dst_ref=theirs_ref.at[chip], send_sem=send_sems.at[slot],
                recv_sem=recv_sems.at[chip], device_id=(x, y, 1 - c), device_id_type=MESH)

        def drained(d):
            pl.when(c == d % 2)(keep(d % 2, d // 2).wait)
            pl.when(c != d % 2)(send(d % 2, d // 2).wait_send)

        @pl.when(i == steps - 1)
        def _():
            for d in range(N_DEV):
                slot, chip = d % 2, d // 2
                if d >= 2:
                    drained(d - 2)
                for owner, s, col, w in _win_runs():
                    if owner == d:
                        stage[slot, :, s:s + w] = acc[:, col:col + w]
                pl.when(c == slot)(keep(slot, chip).start)
                pl.when(c != slot)(send(slot, chip).start)
            drained(N_DEV - 2)
            drained(N_DEV - 1)
            for chip in range(4):
                send(0, chip).wait_recv()

    half = jax.ShapeDtypeStruct((4, D, per), F32)
    return pl.pallas_call(
        body, name="in_proj_bwd_weights", grid=(steps,),
        in_specs=[pl.BlockSpec((D, tm), lambda i: (0, i)), pl.BlockSpec((tm, DINP), lambda i: (i, 0))] + SMALL_SPECS,
        out_specs=[ANY, ANY, pl.BlockSpec((SMALL_ROWS, D), lambda i: (0, 0))],
        out_shape=[half, half, jax.ShapeDtypeStruct((SMALL_ROWS, D), F32)],
        scratch_shapes=[pltpu.VMEM((D, DINP), F32), pltpu.VMEM((2, D, per), F32), pltpu.SemaphoreType.DMA((2,)),
                        pltpu.SemaphoreType.DMA((2,)), pltpu.SemaphoreType.DMA((4,))] + _small_sum_scratch(),
        compiler_params=_cp(("arbitrary",), VMEM_WIDE_MB),
    )(ut, dproj, *small)


def _adamw(w, g, m, v):
    m = ADAM_B1 * m + (1.0 - ADAM_B1) * g
    v = ADAM_B2 * v + (1.0 - ADAM_B2) * jnp.square(g)
    m_hat = m / (1.0 - ADAM_B1 ** ADAM_STEP)
    v_hat = v / (1.0 - ADAM_B2 ** ADAM_STEP)
    delta = -ADAM_LR * (m_hat / (jnp.sqrt(v_hat) + ADAM_EPS) + ADAM_WD * w)
    return delta, m, v


ADAM_STEPS = 8


def _adamw_shards(where, items, name, jobs=None):
    jobs = jobs or _Jobs([])
    ns, nw = jobs.n, len(items)

    def body(where_ref, *rest):
        ins, rest = rest[:5 * nw], rest[5 * nw:]
        job_ins, rest = rest[:ns], rest[ns:]
        outs, rest = rest[:4 * nw], rest[4 * nw:]
        start, finish = jobs.bind(job_ins, rest[:ns], rest[ns:])
        i = pl.program_id(0)
        pl.when(i == 0)(start)
        pl.when(i == ADAM_STEPS - 1)(finish)
        for k in range(nw):
            p_ref, own_ref, w_ref, m_ref, v_ref = ins[5 * k:5 * k + 5]
            g_ref, d_ref, nm_ref, nv_ref = outs[4 * k:4 * k + 4]
            g = ((p_ref[0].astype(F32) + p_ref[1].astype(F32)) + p_ref[2].astype(F32)) + own_ref[...]
            g_ref[...] = g
            d_ref[...], nm_ref[...], nv_ref[...] = _adamw(w_ref[...], g, m_ref[...], v_ref[...])

    in_specs, out_specs, out_shape, operands = [], [], [], []
    for parts, own, w, m, v in items:
        r, cdim = w.shape
        tr = r // ADAM_STEPS
        spec = pl.BlockSpec((tr, cdim), lambda i, s: (i, 0))
        in_specs += [pl.BlockSpec((3, tr, cdim), lambda i, s: (0, i, 0)),
                     pl.BlockSpec((None, tr, cdim), lambda i, s: (s[1], i, 0)), spec, spec, spec]
        out_specs += [spec] * 4
        out_shape += [jax.ShapeDtypeStruct((r, cdim), F32)] * 4
        operands += [parts, own, w, m, v]
    outs = pl.pallas_call(
        body, name=name,
        grid_spec=pltpu.PrefetchScalarGridSpec(
            num_scalar_prefetch=1, grid=(ADAM_STEPS,),
            in_specs=in_specs + [ANY] * ns, out_specs=out_specs + [ANY] * ns, scratch_shapes=jobs.sems),
        out_shape=out_shape + jobs.out_shapes,
        compiler_params=_cp(("arbitrary",)),
    )(where, *operands, *jobs.inputs)
    return [outs[4 * k:4 * k + 4] for k in range(nw)], jobs.split(outs[4 * nw:])


def _adamw_small(items):
    n = len(items)

    def body(*refs):
        ins, outs = refs[:4 * n], refs[4 * n:]
        for k in range(n):
            w_ref, g_ref, m_ref, v_ref = ins[4 * k:4 * k + 4]
            d_ref, nm_ref, nv_ref = outs[3 * k:3 * k + 3]
            d_ref[...], nm_ref[...], nv_ref[...] = _adamw(w_ref[...], g_ref[...], m_ref[...], v_ref[...])

    vm = pl.BlockSpec(memory_space=pltpu.VMEM)
    shapes = [jax.ShapeDtypeStruct(w.shape, F32) for w, _, _, _ in items for _ in range(3)]
    outs = pl.pallas_call(body, name="adamw_small", in_specs=[vm] * (4 * n), out_specs=[vm] * (3 * n),
                          out_shape=shapes)(*[t for item in items for t in item])
    return [outs[3 * k:3 * k + 3] for k in range(n)]


def _add_halves(mine, theirs, name):
    _, r, cdim = mine.shape
    tr = 128 if r % 128 == 0 else r

    def body(a_ref, b_ref, o_ref, w_ref):
        total = a_ref[...] + b_ref[...]
        o_ref[...] = total
        w_ref[...] = total.astype(WIRE_DTYPE)

    spec = pl.BlockSpec((4, tr, cdim), lambda i: (0, i, 0))
    return pl.pallas_call(
        body, name=name, grid=(r // tr,), in_specs=[spec, spec], out_specs=[spec, spec],
        out_shape=[jax.ShapeDtypeStruct(mine.shape, F32), jax.ShapeDtypeStruct(mine.shape, WIRE_DTYPE)],
        compiler_params=_cp(("arbitrary",)))(mine, theirs)


def _add_own_half(where, full, theirs, name, wire_copy=False):
    _, _, r, cdim = full.shape
    tr = 128 if r % 128 == 0 else r

    def body(where_ref, a_ref, b_ref, *o_refs):
        total = a_ref[...] + b_ref[...]
        o_refs[0][...] = total
        if wire_copy:
            o_refs[1][...] = total.astype(WIRE_DTYPE)

    spec = pl.BlockSpec((4, tr, cdim), lambda i, s: (0, i, 0))
    shapes = [jax.ShapeDtypeStruct(theirs.shape, F32)] + ([jax.ShapeDtypeStruct(theirs.shape, WIRE_DTYPE)] if wire_copy else [])
    outs = pl.pallas_call(
        body, name=name,
        grid_spec=pltpu.PrefetchScalarGridSpec(
            num_scalar_prefetch=1, grid=(r // tr,),
            in_specs=[pl.BlockSpec((None, 4, tr, cdim), lambda i, s: (s[0], 0, i, 0)), spec],
            out_specs=[spec] * len(shapes)),
        out_shape=shapes, compiler_params=_cp(("arbitrary",)))(where, full, theirs)
    return outs if wire_copy else outs[0]


def kernel(x, meta_tokens, norm_mix_w, w_in, w_gate_up, b_gate, gla_norm_w, sinks, w_out, norm_ff_w, w_ff1, w_ff2, final_norm_w, loss_target, m_meta_tokens, m_norm_mix_w, m_w_in, m_w_gate_up, m_b_gate, m_gla_norm_w, m_sinks, m_w_out, m_norm_ff_w, m_w_ff1, m_w_ff2, m_final_norm_w, v_meta_tokens, v_norm_mix_w, v_w_in, v_w_gate_up, v_b_gate, v_gla_norm_w, v_sinks, v_w_out, v_norm_ff_w, v_w_ff1, v_w_ff2, v_final_norm_w):
    seq = x.shape[1]
    rows = LEAD + seq
    tm = _row_tile(rows)
    tm_wide = WIDE_ROW_TILE if rows % WIDE_ROW_TILE == 0 else tm
    dev =4 * lax.axis_index("x") + 2 * lax.axis_index("y") + lax.axis_index("c")

    small_shard = jnp.concatenate([meta_tokens, w_gate_up[0], jnp.zeros((N_META, 96), F32)], axis=1)
    g_in, g_small = _all_gather([w_in[0].astype(WIRE_DTYPE), small_shard])
    later_shards = [w_out[0].astype(WIRE_DTYPE), w_ff1[0].astype(WIRE_DTYPE), w_ff2[0].astype(WIRE_DTYPE)]
    win_p = _win_padded(g_in)
    meta_full = jnp.transpose(g_small[:, :, 0:128], (1, 0, 2)).reshape(N_META, D)
    wg_full = jnp.transpose(g_small[:, :, 128:160], (1, 0, 2)).reshape(GLA_RANK, GLA_HEADS * GLA_DK)
    wg_p = jnp.concatenate([wg_full, jnp.zeros((128 - GLA_RANK, 256), F32)], axis=0)

    lead = jnp.concatenate([jnp.zeros((META0, D), F32), meta_full], axis=0)
    tabs = _rope_tables(rows)
    proj, qr, kr, vr = _in_proj(x[0], lead, norm_mix_w, win_p, tabs, tm)
    oraw, og, states, decay, dgate, (g_out, g_w1) = _gla_fwd(proj, wg_p, b_gate, gla_norm_w, later_shards[0:2])
    osw, (g_w2,) = _swa_fwd(qr, kr, vr, sinks, later_shards[2:3])
    wout_full = g_out.reshape(D, D)
    w2_full = g_w2.reshape(D_FF, D)
    w1_full = jnp.transpose(g_w1, (1, 0, 2)).reshape(D, D_FF)
    h1, f, ft = _out_proj(x[0], lead, og, osw, wout_full, norm_ff_w, tm)
    a, dh2, dh2t, loss_p, gfn_p = _ffn_fwd(f, h1, w1_full, w2_full, loss_target[0], final_norm_w.reshape(1, D), tm)

    da, dh1, gnf_p = _ffn_bwd_act(dh2, a, w1_full, w2_full, h1, norm_ff_w, tm)
    dw1, dw2 = _ffn_bwd_weights(ft, a, da, dh2t, tm_wide)
    where = jnp.stack([lax.axis_index("c"), 2 * lax.axis_index("x") + lax.axis_index("y")]).astype(jnp.int32)
    dog, dos, dwout, theirs_ffn = _out_proj_bwd(dh1, og, osw, wout_full, tm, [dw1, dw2])
    pairs_ffn = [_add_own_half(where, p, q, "reduce_pair_%d" % (2 + k), wire_copy=True)
                 for k, (p, q) in enumerate(zip([dw1, dw2], theirs_ffn))]
    sums_ffn, wires_ffn = [p[0] for p in pairs_ffn], [p[1] for p in pairs_ffn]
    dsq, dsk, dsv, dsink_p, (parts_ffn, (theirs_wout,)) = _swa_bwd(
        qr, kr, vr, osw, dos, sinks, _Jobs([("chips", wires_ffn), ("sibling", [dwout])]))
    sum_wout, wire_wout = _add_own_half(where, dwout, theirs_wout, "reduce_pair_1", wire_copy=True)
    (dgq, dgk, dgv, dgr, dlr, dwg_p, dbg_p, dgnw_p), ((parts_wout,),) = _gla_bwd(
        proj, decay, dgate, oraw, states, dog, wg_p, gla_norm_w, _Jobs([("chips", [wire_wout])]))
    grad_x, dlead, dproj, ut, gnm_p = _in_proj_bwd(x[0], lead, dh1, norm_mix_w, win_p, dgv, dgr, dsq, dgq, dgk, dsk,
                                                   dsv, dlr, tabs, tm)
    grad_x = grad_x[None]
    dwin_mine, dwin_theirs, total = _in_proj_bwd_weights(
        ut, dproj, tm_wide, [dlead, dwg_p, gnm_p, gnf_p, gfn_p, dbg_p, dgnw_p, loss_p, dsink_p])
    sum_win, sum_win_wire = _add_halves(dwin_mine, dwin_theirs, "reduce_pair_0")

    g_meta = lax.dynamic_slice(total, (R_META, dev * 128), (N_META, 128))
    g_wg = lax.dynamic_slice(total, (R_WG, dev * 32), (GLA_RANK, 32))
    g_norm_mix, g_norm_ff = total[R_NORM_MIX:R_NORM_MIX + 1], total[R_NORM_FF:R_NORM_FF + 1]
    g_final_norm = total[R_FINAL:R_FINAL + 1]
    g_b_gate, g_gla_norm = total[R_B_GATE:R_B_GATE + 1, 0:256], total[R_GLA_NORM:R_GLA_NORM + 1, 0:128]
    g_sinks = total[R_SINKS:R_SINKS + SWA_HEADS, 0].reshape(1, SWA_HEADS)
    loss = total[R_LOSS, 0]

    ((g_wout, d_wout, nm_wout, nv_wout), (g_w1s, d_w1, nm_w1, nv_w1), (g_w2s, d_w2, nm_w2, nv_w2)), ((parts_win,),) = \
        _adamw_shards(where, [(parts_wout, sum_wout, w_out[0], m_w_out[0], v_w_out[0]),
                              (parts_ffn[0], sums_ffn[0], w_ff1[0], m_w_ff1[0], v_w_ff1[0]),
                              (parts_ffn[1], sums_ffn[1], w_ff2[0], m_w_ff2[0], v_w_ff2[0])],
                      "adamw_w_out_ff", _Jobs([("chips", [sum_win_wire])]))
    ((g_win, d_win, nm_win, nv_win),), _ = _adamw_shards(
        where, [(parts_win, sum_win, w_in[0], m_w_in[0], v_w_in[0])], "adamw_w_in")

    names = ["meta", "wg", "norm_mix", "b_gate", "gla_norm", "sinks", "norm_ff", "final_norm"]
    ws = [meta_tokens, w_gate_up, norm_mix_w, b_gate, gla_norm_w, sinks, norm_ff_w, final_norm_w]
    gs = [g_meta, g_wg, g_norm_mix, g_b_gate, g_gla_norm, g_sinks, g_norm_ff, g_final_norm]
    ms = [m_meta_tokens, m_w_gate_up, m_norm_mix_w, m_b_gate, m_gla_norm_w, m_sinks, m_norm_ff_w, m_final_norm_w]
    vs = [v_meta_tokens, v_w_gate_up, v_norm_mix_w, v_b_gate, v_gla_norm_w, v_sinks, v_norm_ff_w, v_final_norm_w]
    flat = lambda t: t.reshape(-1, t.shape[-1])
    small_out = _adamw_small([(flat(w), flat(g), flat(m), flat(v)) for w, g, m, v in zip(ws, gs, ms, vs)])
    d_small = {n: small_out[k][0].reshape(ws[k].shape) for k, n in enumerate(names)}
    nm_small = {n: small_out[k][1].reshape(ws[k].shape) for k, n in enumerate(names)}
    nv_small = {n: small_out[k][2].reshape(ws[k].shape) for k, n in enumerate(names)}
    g_small_d = {n: g.reshape(ws[k].shape) for k, (n, g) in enumerate(zip(names, gs))}

    def ordered(big, small_d):
        win_v, wout_v, w1_v, w2_v = big
        return (small_d["meta"], small_d["norm_mix"], win_v[None], small_d["wg"], small_d["b_gate"],
                small_d["gla_norm"], small_d["sinks"], wout_v[None], small_d["norm_ff"], w1_v[None], w2_v[None],
                small_d["final_norm"])

    return (loss, grad_x,
            *ordered((g_win, g_wout, g_w1s, g_w2s), g_small_d),
            *ordered((d_win, d_wout, d_w1, d_w2), d_small),
            *ordered((nm_win, nm_wout, nm_w1, nm_w2), nm_small),
            *ordered((nv_win, nv_wout, nv_w1, nv_w2), nv_small))
```

```python
import functools

import jax
import jax.numpy as jnp
from jax import lax
from jax.experimental import pallas as pl
from jax.experimental.pallas import tpu as pltpu

F32 = jnp.float32
MXU_DTYPE = jnp.bfloat16
ACT_DTYPE = jnp.bfloat16
WIRE_DTYPE = jnp.bfloat16

D = 1024
N_META = 16
LEAD = 128
META0 = LEAD - N_META
EPS = 1e-5
GLA_HEADS, GLA_DK, GLA_DV, GLA_RANK, GLA_CHUNK = 4, 64, 128, 16, 64
GLA_TAU = 16.0
SWA_HEADS, SWA_KV, SWA_GROUP, SWA_HD, SWA_BLOCK = 8, 2, 4, 64, 128
ROPE_DIM, ROPE_THETA = 16, 500000.0
D_FF = 4096
N_DEV = 8
FF_TILE = D_FF // N_DEV
FF_WIDE = 2048
NEG = -1e30

C_GV, C_GR, C_GQ, C_GK, C_LR, C_SQ, C_SK, C_SV = 0, 512, 1024, 1280, 1536, 1664, 2176, 2304
DGLA = 1664
DINP = 2432
DIN = 2320
O_GQ, O_GK, O_GV, O_GR, O_LR, O_SQ, O_SK, O_SV = (0, 256), (256, 512), (512, 1024), (1024, 1536), (1536, 1552), (1552, 2064), (2064, 2192), (2192, 2320)

ADAM_LR, ADAM_B1, ADAM_B2, ADAM_EPS, ADAM_WD, ADAM_STEP = 0.001, 0.9, 0.999, 1e-08, 0.01, 10

MESH = pl.DeviceIdType.MESH
ANY = pl.BlockSpec(memory_space=pl.ANY)
VMEM_TILE_MB, VMEM_WIDE_MB = 48, 56


def _cp(sem=None, vmem_mb=None):
    kw = {}
    if sem is not None:
        kw["dimension_semantics"] = sem
    if vmem_mb is not None:
        kw["vmem_limit_bytes"] = vmem_mb << 20
    return pltpu.CompilerParams(**kw)


def _mm(a, b):
    return jnp.dot(a.astype(MXU_DTYPE), b.astype(MXU_DTYPE), preferred_element_type=F32)


def _mm_nt(a, b):
    return lax.dot_general(a.astype(MXU_DTYPE), b.astype(MXU_DTYPE), (((1,), (1,)), ((), ())),
                           preferred_element_type=F32)


def _mm_tn(a, b):
    return lax.dot_general(a.astype(MXU_DTYPE), b.astype(MXU_DTYPE), (((0,), (0,)), ((), ())),
                           preferred_element_type=F32)


def _masked_sums(mask, t):
    m = mask.astype(jnp.bfloat16)
    hi = t.astype(jnp.bfloat16)
    rest = t - hi.astype(F32)
    mid = rest.astype(jnp.bfloat16)
    low = (rest - mid.astype(F32)).astype(jnp.bfloat16)
    dot = lambda part: jnp.dot(m, part, preferred_element_type=F32)
    return dot(hi) + (dot(mid) + dot(low))


def _logsigmoid(z):
    return jnp.minimum(z, 0.0) - jnp.log(1.0 + jnp.exp(-jnp.abs(z)))


def _sigmoid(z):
    return 1.0 / (1.0 + jnp.exp(-z))


ROW_TILE, WIDE_ROW_TILE = 640, 1664


def _row_tile(rows, want=ROW_TILE):
    return want if rows % want == 0 else LEAD


def _mesh_pos():
    return lax.axis_index("x"), lax.axis_index("y"), lax.axis_index("c")


def _all_gather(shards):
    n = len(shards)

    def body(*refs):
        start, forward, finish = _gather_schedule(refs[:n], refs[n:2 * n], *refs[2 * n:])
        start()
        for j in range(3):
            forward(j)
        finish()

    gathered = pl.pallas_call(
        body, name="all_gather_weights",
        out_shape=_gathered_shapes(shards), in_specs=[ANY] * n, out_specs=[ANY] * n,
        scratch_shapes=_gather_sems(n),
    )(*shards)
    return _with_own_block(gathered, shards)


def _gathered_shapes(shards):
    return [jax.ShapeDtypeStruct((N_DEV,) + s.shape, s.dtype) for s in shards]


def _gather_sems(n):
    return [pltpu.SemaphoreType.DMA((7 * n,)), pltpu.SemaphoreType.DMA((7 * n,))] if n else []


def _place_gather(step, steps, shard_refs, gathered_refs, sems):
    if not shard_refs:
        return
    start, forward, finish = _gather_schedule(shard_refs, gathered_refs, *sems)
    pl.when(step == 0)(start)
    for j, at in enumerate((steps * 7 // 10, steps * 8 // 10, steps * 9 // 10)):
        pl.when(step == at)(functools.partial(forward, j))
    pl.when(step == steps - 1)(finish)


def _with_own_block(gathered, shards):
    dev = 4 * lax.axis_index("x") + 2 * lax.axis_index("y") + lax.axis_index("c")
    return [lax.dynamic_update_index_in_dim(g, s, dev, 0) for g, s in zip(gathered, shards)]


def _gather_schedule(ins, outs, send_sems, recv_sems):
    n = len(ins)
    x, y, c = _mesh_pos()
    me, sibling = (x, y, c), (x, y, 1 - c)
    chips = [(1 - x, y), (x, 1 - y), (1 - x, 1 - y)]

    def copy(a, k, block, to, src=None):
        dst = outs[a].at[4 * block[0] + 2 * block[1] + block[2]]
        return pltpu.make_async_remote_copy(
            src_ref=dst if src is None else src, dst_ref=dst,
            send_sem=send_sems.at[a * 7 + k], recv_sem=recv_sems.at[a * 7 + k],
            device_id=to, device_id_type=MESH)

    def first(a):
        return [copy(a, 0, me, sibling, src=ins[a])] + [copy(a, 1 + j, me, (*chip, c), src=ins[a])
                                                        for j, chip in enumerate(chips)]

    def start():
        for a in range(n):
            for cp in first(a):
                cp.start()

    def forward(j):
        for a in range(n):
            copy(a, 1 + j, (*chips[j], c), me).wait_recv()
            copy(a, 4 + j, (*chips[j], c), sibling).start()

    def finish():
        for a in range(n):
            copy(a, 0, sibling, me).wait_recv()
            for j, chip in enumerate(chips):
                copy(a, 4 + j, (*chip, 1 - c), me).wait_recv()
        for a in range(n):
            for cp in first(a) + [copy(a, 4 + j, (*chip, c), sibling) for j, chip in enumerate(chips)]:
                cp.wait_send()

    return start, forward, finish


def _sibling_shapes(gs):
    return [jax.ShapeDtypeStruct(g.shape[1:], g.dtype) for g in gs]


def _sibling_sems(n):
    return [pltpu.SemaphoreType.DMA((n,)), pltpu.SemaphoreType.DMA((n,))]


def _sibling_schedule(ins, land, send_sems, recv_sems):
    x, y, c = _mesh_pos()

    def copies():
        return [pltpu.make_async_remote_copy(
            src_ref=ins[a].at[1 - c], dst_ref=land[a], send_sem=send_sems.at[a], recv_sem=recv_sems.at[a],
            device_id=(x, y, 1 - c), device_id_type=MESH) for a in range(len(ins))]

    def start():
        for cp in copies():
            cp.start()

    def finish():
        for cp in copies():
            cp.wait_recv()
        for cp in copies():
            cp.wait_send()

    return start, finish


def _chips_shapes(ps):
    return [jax.ShapeDtypeStruct((3,) + p.shape[1:], p.dtype) for p in ps]


def _chips_sems(n):
    return [pltpu.SemaphoreType.DMA((3 * n,)), pltpu.SemaphoreType.DMA((3 * n,))]


def _chips_schedule(ins, land, send_sems, recv_sems):
    x, y, c = _mesh_pos()
    chips = [(1 - x, y), (x, 1 - y), (1 - x, 1 - y)]

    def copies():
        return [pltpu.make_async_remote_copy(
            src_ref=ins[a].at[2 * chip[0] + chip[1]], dst_ref=land[a].at[j],
            send_sem=send_sems.at[3 * a + j], recv_sem=recv_sems.at[3 * a + j],
            device_id=(*chip, c), device_id_type=MESH) for a in range(len(ins)) for j, chip in enumerate(chips)]

    def start():
        for cp in copies():
            cp.start()

    def finish():
        for cp in copies():
            cp.wait_recv()
        for cp in copies():
            cp.wait_send()

    return start, finish


class _Jobs:
    def __init__(self, jobs):
        self.jobs = jobs
        self.inputs = [a for _, arrs in jobs for a in arrs]
        self.out_shapes = [s for kind, arrs in jobs
                           for s in (_sibling_shapes(arrs) if kind == "sibling" else _chips_shapes(arrs))]
        self.sems = [s for kind, arrs in jobs
                     for s in (_sibling_sems(len(arrs)) if kind == "sibling" else _chips_sems(len(arrs)))]
        self.n = len(self.inputs)

    def bind(self, in_refs, out_refs, sem_refs):
        starts, finishes, at = [], [], 0
        for k, (kind, arrs) in enumerate(self.jobs):
            schedule = _sibling_schedule if kind == "sibling" else _chips_schedule
            start, finish = schedule(in_refs[at:at + len(arrs)], out_refs[at:at + len(arrs)],
                                     sem_refs[2 * k], sem_refs[2 * k + 1])
            starts.append(start)
            finishes.append(finish)
            at += len(arrs)

        def start_all():
            for f in starts:
                f()

        def finish_all():
            for f in finishes:
                f()

        return start_all, finish_all

    def split(self, outs):
        res, at = [], 0
        for _, arrs in self.jobs:
            res.append(list(outs[at:at + len(arrs)]))
            at += len(arrs)
        return res


R_META, R_WG, R_NORM_MIX, R_NORM_FF, R_FINAL, R_B_GATE, R_GLA_NORM, R_LOSS, R_SINKS, SMALL_ROWS = 0, 16, 32, 33, 34, 35, 36, 37, 40, 48


SMALL_SPECS = [pl.BlockSpec((LEAD, D), lambda i: (0, 0)), pl.BlockSpec((128, 256), lambda i: (0, 0)),
               pl.BlockSpec((8, D), lambda i: (0, 0)), pl.BlockSpec((8, D), lambda i: (0, 0)),
               pl.BlockSpec((8, D), lambda i: (0, 0)), pl.BlockSpec((8, 256), lambda i: (0, 0)),
               pl.BlockSpec((8, 128), lambda i: (0, 0)), pl.BlockSpec((8, 128), lambda i: (0, 0)),
               pl.BlockSpec((8, 128), lambda i: (0, 0))]


def _small_sum_scratch():
    return [pltpu.VMEM((SMALL_ROWS, D), F32), pltpu.VMEM((N_DEV, SMALL_ROWS, D), F32),
            pltpu.SemaphoreType.DMA((7,)), pltpu.SemaphoreType.DMA((7,))]


def _small_sum_schedule(small_refs, out_ref, p_ref, land, send_sems, recv_sems):
    dlead_ref, dwg_ref, gnm_ref, gnf_ref, gfn_ref, dbg_ref, dgnw_ref, loss_ref, dsink_ref = small_refs
    x, y, c = _mesh_pos()
    me = 4 * x + 2 * y + c

    def copies():
        res = []
        for k in range(1, N_DEV):
            bx, by, bc = (k >> 2) & 1, (k >> 1) & 1, k & 1
            peer = (1 - x if bx else x, 1 - y if by else y, 1 - c if bc else c)
            res.append(pltpu.make_async_remote_copy(
                src_ref=p_ref, dst_ref=land.at[me], send_sem=send_sems.at[k - 1], recv_sem=recv_sems.at[k - 1],
                device_id=peer, device_id_type=MESH))
        return res

    def start():
        p_ref[...] = jnp.zeros_like(p_ref)
        p_ref[R_META:R_META + N_META, :] = dlead_ref[META0:LEAD, :]
        p_ref[R_WG:R_WG + GLA_RANK, 0:256] = dwg_ref[0:GLA_RANK, :]
        p_ref[R_NORM_MIX:R_NORM_MIX + 1, :] = gnm_ref[0:1, :]
        p_ref[R_NORM_FF:R_NORM_FF + 1, :] = gnf_ref[0:1, :]
        p_ref[R_FINAL:R_FINAL + 1, :] = gfn_ref[0:1, :]
        p_ref[R_B_GATE:R_B_GATE + 1, 0:256] = dbg_ref[0:1, :]
        p_ref[R_GLA_NORM:R_GLA_NORM + 1, 0:128] = dgnw_ref[0:1, :]
        p_ref[R_LOSS:R_LOSS + 1, 0:128] = loss_ref[0:1, :]
        p_ref[R_SINKS:R_SINKS + SWA_HEADS, 0:128] = dsink_ref[...]
        land[me] = p_ref[...]
        for cp in copies():
            cp.start()

    def finish():
        for cp in copies():
            cp.wait_recv()
        for cp in copies():
            cp.wait_send()
        acc = land[0]
        for d in range(1, N_DEV):
            acc = acc + land[d]
        out_ref[...] = acc

    return start, finish


def _token_specs(tm, grid_rank=1):
    nb = tm // LEAD

    def spec(k):
        if grid_rank == 1:
            return pl.BlockSpec((LEAD, D), lambda i: (jnp.maximum(i * nb + k - 1, 0), 0))
        return pl.BlockSpec((LEAD, D), lambda i, j: (jnp.maximum(i * nb + k - 1, 0), 0))

    return [spec(k) for k in range(nb)]


def _h_tile(i, lead_ref, x_refs):
    first = jnp.where(i == 0, lead_ref[...], x_refs[0][...])
    return jnp.concatenate([first] + [r[...] for r in x_refs[1:]], axis=0)


def _in_proj(x, lead, nw, win_p, tabs, tm, shards):
    rows = LEAD + x.shape[0]
    nb = tm // LEAD
    steps = rows // tm
    ns = len(shards)

    def body(*refs):
        x_refs, refs = refs[:nb], refs[nb:]
        lead_ref, nw_ref, w_ref, c_ref, sa_ref, sb_ref = refs[:6]
        shard_refs, (o_ref, q_ref, k_ref, v_ref) = refs[6:6 + ns], refs[6 + ns:10 + ns]
        _place_gather(pl.program_id(0), steps, shard_refs, refs[10 + ns:10 + 2 * ns], refs[10 + 2 * ns:])
        h = _h_tile(pl.program_id(0), lead_ref, x_refs)
        rstd = lax.rsqrt(jnp.mean(h * h, axis=-1, keepdims=True) + EPS)
        u = (h * rstd * nw_ref[...]).astype(MXU_DTYPE)
        proj = jnp.dot(u, w_ref[...].astype(MXU_DTYPE), preferred_element_type=F32)
        o_ref[...] = proj[:, 0:DGLA]
        cos, sa, sb = c_ref[...], sa_ref[...], sb_ref[...]
        q_ref[...] = (_rope(proj[:, C_SQ:C_SK], cos, sa, sb) * (SWA_HD ** -0.5)).astype(ACT_DTYPE)
        k_ref[...] = _rope(proj[:, C_SK:C_SV], cos, sa, sb).astype(ACT_DTYPE)
        v_ref[...] = proj[:, C_SV:DINP].astype(ACT_DTYPE)

    row = lambda w: pl.BlockSpec((tm, w), lambda i: (i, 0))
    outs = pl.pallas_call(
        body, name="in_proj", grid=(steps,),
        in_specs=_token_specs(tm) + [pl.BlockSpec((LEAD, D), lambda i: (0, 0)), pl.BlockSpec((1, D), lambda i: (0, 0)),
                                     pl.BlockSpec((D, DINP), lambda i: (0, 0)), row(128), row(128), row(128)]
        + [ANY] * ns,
        out_specs=[row(DGLA), row(512), row(128), row(128)] + [ANY] * ns,
        out_shape=[jax.ShapeDtypeStruct((rows, DGLA), F32), jax.ShapeDtypeStruct((rows, 512), ACT_DTYPE),
                   jax.ShapeDtypeStruct((rows, 128), ACT_DTYPE), jax.ShapeDtypeStruct((rows, 128), ACT_DTYPE)]
        + _gathered_shapes(shards),
        scratch_shapes=_gather_sems(ns),
        compiler_params=_cp(("arbitrary",), VMEM_WIDE_MB),
    )(*([x] * nb), lead, nw, win_p, *tabs, *shards)
    return outs[0], outs[1], outs[2], outs[3], _with_own_block(outs[4:], shards)


def _rope_tables(rows):
    pos = (jnp.arange(rows, dtype=jnp.int32) - META0).astype(F32)
    inv_freq = 1.0 / (ROPE_THETA ** (jnp.arange(0, ROPE_DIM, 2, dtype=F32) / ROPE_DIM))
    ang = pos[:, None] * jnp.tile(inv_freq, 128 // (ROPE_DIM // 2))[None, :]
    in_head = jnp.arange(128, dtype=jnp.int32)[None, :] % SWA_HD
    cos, sin = jnp.cos(ang), jnp.sin(ang)
    c_tab = jnp.where(in_head < ROPE_DIM, cos, 1.0)
    sa_tab = jnp.where(in_head < ROPE_DIM // 2, -sin, 0.0)
    sb_tab = jnp.where((in_head >= ROPE_DIM // 2) & (in_head < ROPE_DIM), sin, 0.0)
    return c_tab, sa_tab, sb_tab


def _rope(xv, cos, sa, sb):
    width = xv.shape[1]
    reps = width // 128
    if reps > 1:
        cos, sa, sb = (jnp.tile(t, (1, reps)) for t in (cos, sa, sb))
    return xv * cos + pltpu.roll(xv, width - 8, 1) * sa + pltpu.roll(xv, 8, 1) * sb


def _unrope(dy, cos, sa, sb):
    width = dy.shape[1]
    reps = width // 128
    if reps > 1:
        cos, sa, sb = (jnp.tile(t, (1, reps)) for t in (cos, sa, sb))
    return dy * cos + pltpu.roll(dy * sa, 8, 1) + pltpu.roll(dy * sb, width - 8, 1)


def _gla_group(nc):
    for g in (5, 2):
        if nc % g == 0:
            return g
    return 1


def _chunk_masks(nrows):
    ii = lax.broadcasted_iota(jnp.int32, (nrows, nrows), 0)
    jj = lax.broadcasted_iota(jnp.int32, (nrows, nrows), 1)
    same = (ii // GLA_CHUNK) == (jj // GLA_CHUNK)
    return same & (jj <= ii), same & (jj >= ii)


def _gla_gates(lr, wg, bg, first_row):
    nrows = lr.shape[0]
    zg = _mm(lr, wg) + bg
    live = first_row + lax.broadcasted_iota(jnp.int32, (nrows, 1), 0) >= META0
    g = jnp.where(live, _logsigmoid(zg) * (1.0 / GLA_TAU), 0.0)
    return _masked_sums(_chunk_masks(nrows)[0], g), jnp.where(live, _sigmoid(-zg) * (1.0 / GLA_TAU), 0.0)


def _tril64():
    ii = lax.broadcasted_iota(jnp.int32, (GLA_CHUNK, GLA_CHUNK), 0)
    jj = lax.broadcasted_iota(jnp.int32, (GLA_CHUNK, GLA_CHUNK), 1)
    return jj <= ii


def _gla_fwd(proj, wg_p, bg, gnw, shards):
    rows = proj.shape[0]
    nc = rows // GLA_CHUNK
    group = _gla_group(nc)
    steps, nrows = nc // group, group * GLA_CHUNK
    ns = len(shards)

    def body(q_ref, k_ref, v_ref, r_ref, lr_ref, lr_next_ref, wg_ref, bg_ref, gnw_ref, *rest):
        shard_refs, rest = rest[:ns], rest[ns:]
        oraw_ref, og_ref, st_ref, decay_ref, dgate_ref = rest[:5]
        gathered_refs, rest = rest[5:5 + ns], rest[5 + ns:]
        state, gates = rest[:2]
        c = pl.program_id(0)

        @pl.when(c == 0)
        def _():
            state[...] = jnp.zeros_like(state)
            gates[0, 0], gates[0, 1] = _gla_gates(lr_ref[...], wg_ref[...], bg_ref[...], 0)

        _place_gather(c, steps, shard_refs, gathered_refs, rest[2:])
        slot = c % 2
        b = gates[slot, 0]
        decay_ref[...] = b
        dgate_ref[...] = gates[slot, 1]
        gates[1 - slot, 0], gates[1 - slot, 1] = _gla_gates(lr_next_ref[...], wg_ref[...], bg_ref[...], (c + 1) * nrows)
        eb = jnp.exp(b)
        gq = q_ref[...] * (GLA_DK ** -0.5) * eb
        gk = k_ref[...] * jnp.exp(-b)
        v = v_ref[...]
        gnw_v = gnw_ref[...]
        tril = _tril64()
        pairs = [(h, gi) for h in range(GLA_HEADS) for gi in range(group)]
        rs = {gi: slice(gi * GLA_CHUNK, (gi + 1) * GLA_CHUNK) for gi in range(group)}
        s64 = {h: slice(h * GLA_DK, (h + 1) * GLA_DK) for h in range(GLA_HEADS)}
        s128 = {h: slice(h * GLA_DV, (h + 1) * GLA_DV) for h in range(GLA_HEADS)}
        qh = {(h, gi): gq[rs[gi], s64[h]] for h, gi in pairs}
        kh = {(h, gi): gk[rs[gi], s64[h]] for h, gi in pairs}
        vh = {(h, gi): v[rs[gi], s128[h]] for h, gi in pairs}
        ebl = {(h, gi): eb[(gi + 1) * GLA_CHUNK - 1:(gi + 1) * GLA_CHUNK, s64[h]] for h, gi in pairs}
        av = {pr: _mm(jnp.where(tril, _mm_nt(qh[pr], kh[pr]), 0.0), vh[pr]) for pr in pairs}
        inc = {pr: _mm_tn(vh[pr], kh[pr] * ebl[pr]) for pr in pairs}
        st = {}
        for h in range(GLA_HEADS):
            cur = state[h]
            for gi in range(group):
                st[h, gi] = cur
                st_ref[gi, h] = cur
                cur = cur * ebl[h, gi] + inc[h, gi]
            state[h] = cur
        for h, gi in pairs:
            o = av[h, gi] + _mm_nt(qh[h, gi], st[h, gi])
            oraw_ref[rs[gi], s128[h]] = o
            rstd = lax.rsqrt(jnp.mean(o * o, axis=-1, keepdims=True) + EPS)
            rh = r_ref[rs[gi], s128[h]]
            og_ref[rs[gi], s128[h]] = (o * rstd * gnw_v * (rh * _sigmoid(rh))).astype(ACT_DTYPE)

    nb = lambda w, col: pl.BlockSpec((nrows, w), lambda c: (c, col // w))
    const = lambda shape: pl.BlockSpec(shape, lambda c: (0,) * len(shape))
    outs = pl.pallas_call(
        body, name="gla_fwd", grid=(steps,),
        in_specs=[nb(256, C_GQ), nb(256, C_GK), nb(512, C_GV), nb(512, C_GR), nb(128, C_LR),
                  pl.BlockSpec((nrows, 128), lambda c: (jnp.minimum(c + 1, steps - 1), C_LR // 128)),
                  const((128, 256)), const((1, 256)), const((1, 128))] + [ANY] * ns,
        out_specs=[nb(512, 0), nb(512, 0),
                   pl.BlockSpec((group, GLA_HEADS, GLA_DV, GLA_DK), lambda c: (c, 0, 0, 0)),
                   nb(256, 0), nb(256, 0)] + [ANY] * ns,
        out_shape=[jax.ShapeDtypeStruct((rows, 512), F32), jax.ShapeDtypeStruct((rows, 512), ACT_DTYPE),
                   jax.ShapeDtypeStruct((nc, GLA_HEADS, GLA_DV, GLA_DK), F32),
                   jax.ShapeDtypeStruct((rows, 256), F32), jax.ShapeDtypeStruct((rows, 256), F32)]
        + _gathered_shapes(shards),
        scratch_shapes=[pltpu.VMEM((GLA_HEADS, GLA_DV, GLA_DK), F32), pltpu.VMEM((2, 2, nrows, 256), F32)]
        + _gather_sems(ns),
        compiler_params=_cp(("arbitrary",)),
    )(proj, proj, proj, proj, proj, proj, wg_p, bg, gnw, *shards)
    return outs[0], outs[1], outs[2], outs[3], outs[4], _with_own_block(outs[5:], shards)


def _swa_mask(n):
    shape = (SWA_GROUP * SWA_BLOCK, 3 * SWA_BLOCK)
    qi = lax.broadcasted_iota(jnp.int32, shape, 0) & (SWA_BLOCK - 1)
    jj = lax.broadcasted_iota(jnp.int32, shape, 1)
    meta = (jj < SWA_BLOCK) & (jj >= META0) & ((n > 0) | (jj <= qi))
    prev = (jj >= SWA_BLOCK) & (jj < 2 * SWA_BLOCK) & (n >= 2) & (jj - SWA_BLOCK > qi)
    cur = (jj >= 2 * SWA_BLOCK) & (n >= 1) & (jj - 2 * SWA_BLOCK <= qi)
    return meta | prev | cur


def _stack_heads(t, kvh):
    return jnp.concatenate([t[:, (kvh * SWA_GROUP + g) * SWA_HD:(kvh * SWA_GROUP + g + 1) * SWA_HD]
                            for g in range(SWA_GROUP)], axis=0)


def _stack_sinks(sink_ref, kvh):
    return jnp.concatenate([jnp.full((SWA_BLOCK, 1), sink_ref[0, kvh * SWA_GROUP + g], F32)
                            for g in range(SWA_GROUP)], axis=0)


def _swa_group(nblk):
    return 5 if nblk % 5 == 0 else 1


def _swa_specs(group):
    blk = lambda w: pl.BlockSpec((group * SWA_BLOCK, w), lambda n: (n, 0))
    first = pl.BlockSpec((SWA_BLOCK, 128), lambda n: (0, 0))
    prev = pl.BlockSpec((SWA_BLOCK, 128), lambda n: (jnp.maximum(n * group - 1, 0), 0))
    return blk, first, prev


def _swa_keys(first_ref, prev_ref, cur_ref, g):
    own = cur_ref[g * SWA_BLOCK:(g + 1) * SWA_BLOCK, :]
    before = prev_ref[...] if g == 0 else cur_ref[(g - 1) * SWA_BLOCK:g * SWA_BLOCK, :]
    return jnp.concatenate([first_ref[...], before, own], axis=0)


def _swa_fwd(qr, kr, vr, sinks, shards):
    rows = qr.shape[0]
    nblk = rows // SWA_BLOCK
    group = _swa_group(nblk)
    steps = nblk // group
    ns = len(shards)

    def body(q_ref, k0, kp, kc, v0, vp, vc, sink_ref, *rest):
        o_ref = rest[ns]
        _place_gather(pl.program_id(0), steps, rest[:ns], rest[ns + 1:2 * ns + 1], rest[2 * ns + 1:])
        for g in range(group):
            n = pl.program_id(0) * group + g
            rs = slice(g * SWA_BLOCK, (g + 1) * SWA_BLOCK)
            kall, vall = _swa_keys(k0, kp, kc, g), _swa_keys(v0, vp, vc, g)
            mask = _swa_mask(n)[0:SWA_BLOCK]
            heads = range(SWA_HEADS)
            hs = [slice(h * SWA_HD, (h + 1) * SWA_HD) for h in heads]
            kv = [slice((h // SWA_GROUP) * SWA_HD, (h // SWA_GROUP + 1) * SWA_HD) for h in heads]
            s = [jnp.where(mask, _mm_nt(q_ref[rs, hs[h]], kall[:, kv[h]]), NEG) for h in heads]
            m = [jnp.maximum(jnp.max(s[h], axis=-1, keepdims=True), sink_ref[0, h]) for h in heads]
            p = [jnp.exp(s[h] - m[h]) for h in heads]
            den = [jnp.sum(p[h], axis=-1, keepdims=True) + jnp.exp(sink_ref[0, h] - m[h]) for h in heads]
            o = [_mm(p[h], vall[:, kv[h]]) for h in heads]
            for h in heads:
                o_ref[rs, hs[h]] = (o[h] / den[h]).astype(ACT_DTYPE)

    blk, first, prev = _swa_specs(group)
    outs = pl.pallas_call(
        body, name="swa_fwd", grid=(steps,),
        in_specs=[blk(512), first, prev, blk(128), first, prev, blk(128),
                  pl.BlockSpec(memory_space=pltpu.SMEM)] + [ANY] * ns,
        out_specs=[blk(512)] + [ANY] * ns,
        out_shape=[jax.ShapeDtypeStruct((rows, 512), ACT_DTYPE)] + _gathered_shapes(shards),
        scratch_shapes=_gather_sems(ns),
        compiler_params=_cp(("arbitrary",)),
    )(qr, kr, kr, kr, vr, vr, vr, sinks, *shards)
    return outs[0], _with_own_block(outs[1:], shards)


def _out_proj(x, lead, og, osw, wout, nfw, tm):
    rows = LEAD + x.shape[0]
    nb = tm // LEAD

    def body(*refs):
        x_refs, (lead_ref, og_ref, os_ref, w_ref, nw_ref, h1_ref, f_ref, ft_ref) = refs[:nb], refs[nb:]
        h0 = _h_tile(pl.program_id(0), lead_ref, x_refs)
        h1 = h0 + _mm(og_ref[...], w_ref[0:512, :]) + _mm(os_ref[...], w_ref[512:1024, :])
        h1_ref[...] = h1
        rstd = lax.rsqrt(jnp.mean(h1 * h1, axis=-1, keepdims=True) + EPS)
        f = h1 * rstd * nw_ref[...]
        f_ref[...] = f.astype(ACT_DTYPE)
        ft_ref[...] = f.T.astype(ACT_DTYPE)

    row = lambda w: pl.BlockSpec((tm, w), lambda i: (i, 0))
    return pl.pallas_call(
        body, name="out_proj", grid=(rows // tm,),
        in_specs=_token_specs(tm) + [pl.BlockSpec((LEAD, D), lambda i: (0, 0)), row(512), row(512),
                                     pl.BlockSpec((D, D), lambda i: (0, 0)), pl.BlockSpec((1, D), lambda i: (0, 0))],
        out_specs=[row(D), row(D), pl.BlockSpec((D, tm), lambda i: (0, i))],
        out_shape=[jax.ShapeDtypeStruct((rows, D), F32), jax.ShapeDtypeStruct((rows, D), ACT_DTYPE),
                   jax.ShapeDtypeStruct((D, rows), ACT_DTYPE)],
        compiler_params=_cp(("arbitrary",), VMEM_TILE_MB),
    )(*([x] * nb), lead, og, osw, wout, nfw)


def _ffn_fwd(f, h1, w1, w2, tgt, fnw, tm):
    rows = f.shape[0]
    nj = D_FF // FF_WIDE
    nb = tm // LEAD

    def body(f_ref, h1_ref, w1_ref, w2_ref, nw_ref, *rest):
        t_refs, (a_ref, dh2_ref, dh2t_ref, loss_ref, gfn_ref, acc) = rest[:nb], rest[nb:]
        i, j = pl.program_id(0), pl.program_id(1)

        @pl.when((i == 0) & (j == 0))
        def _():
            loss_ref[...] = jnp.zeros_like(loss_ref)
            gfn_ref[...] = jnp.zeros_like(gfn_ref)

        @pl.when(j == 0)
        def _():
            acc[...] = jnp.zeros_like(acc)

        a = _mm(f_ref[...], w1_ref[...])
        a_ref[...] = a.astype(ACT_DTYPE)
        z = jnp.square(jnp.maximum(a, 0.0))
        acc[...] += _mm(z, w2_ref[...])

        @pl.when(j == nj - 1)
        def _():
            h2 = h1_ref[...] + acc[...]
            rstd = lax.rsqrt(jnp.mean(h2 * h2, axis=-1, keepdims=True) + EPS)
            hn = h2 * rstd
            nw = nw_ref[...]
            row = i * tm + lax.broadcasted_iota(jnp.int32, (tm, 1), 0)
            target = jnp.concatenate([t[...] for t in t_refs], axis=0)
            err = jnp.where(row >= LEAD, hn * nw - target, 0.0)
            row_loss = jnp.sum(err * err, axis=-1, keepdims=True) * (1.0 / D)
            loss_ref[...] += jnp.broadcast_to(0.5 * jnp.sum(row_loss, axis=0, keepdims=True), loss_ref.shape)
            dy = err * (1.0 / D)
            gfn_ref[...] += jnp.broadcast_to(jnp.sum(dy * hn, axis=0, keepdims=True), gfn_ref.shape)
            dhn = dy * nw
            dh2 = rstd * (dhn - hn * jnp.mean(dhn * hn, axis=-1, keepdims=True))
            dh2_ref[...] = dh2
            dh2t_ref[...] = dh2.T.astype(ACT_DTYPE)

    return pl.pallas_call(
        body, name="ffn_fwd", grid=(rows // tm, nj),
        in_specs=[pl.BlockSpec((tm, D), lambda i, j: (i, 0)), pl.BlockSpec((tm, D), lambda i, j: (i, 0)),
                  pl.BlockSpec((D, FF_WIDE), lambda i, j: (0, j)),
                  pl.BlockSpec((FF_WIDE, D), lambda i, j: (j, 0)),
                  pl.BlockSpec((1, D), lambda i, j: (0, 0))] + _token_specs(tm, grid_rank=2),
        out_specs=[pl.BlockSpec((tm, FF_WIDE), lambda i, j: (i, j)), pl.BlockSpec((tm, D), lambda i, j: (i, 0)),
                   pl.BlockSpec((D, tm), lambda i, j: (0, i)),
                   pl.BlockSpec((8, 128), lambda i, j: (0, 0)), pl.BlockSpec((8, D), lambda i, j: (0, 0))],
        out_shape=[jax.ShapeDtypeStruct((rows, D_FF), ACT_DTYPE), jax.ShapeDtypeStruct((rows, D), F32),
                   jax.ShapeDtypeStruct((D, rows), ACT_DTYPE),
                   jax.ShapeDtypeStruct((8, 128), F32), jax.ShapeDtypeStruct((8, D), F32)],
        scratch_shapes=[pltpu.VMEM((tm, D), F32)],
        compiler_params=_cp(("arbitrary", "arbitrary"), VMEM_WIDE_MB),
    )(f, h1, w1, w2, fnw, *([tgt] * nb))


def _ffn_bwd_act(dh2, a, w1, w2, h1, nfw, tm):
    rows = dh2.shape[0]
    nj = D_FF // FF_WIDE

    def body(dh2_ref, a_ref, w1_ref, w2_ref, h1_ref, nw_ref, da_ref, dh1_ref, gnf_ref, acc):
        i, j = pl.program_id(0), pl.program_id(1)

        @pl.when((i == 0) & (j == 0))
        def _():
            gnf_ref[...] = jnp.zeros_like(gnf_ref)

        @pl.when(j == 0)
        def _():
            acc[...] = jnp.zeros_like(acc)

        dz = _mm_nt(dh2_ref[...], w2_ref[...])
        da = dz * (2.0 * jnp.maximum(a_ref[...].astype(F32), 0.0))
        da_ref[...] = da.astype(ACT_DTYPE)
        acc[...] += _mm_nt(da, w1_ref[...])

        @pl.when(j == nj - 1)
        def _():
            h1 = h1_ref[...]
            rstd = lax.rsqrt(jnp.mean(h1 * h1, axis=-1, keepdims=True) + EPS)
            hn = h1 * rstd
            df = acc[...]
            gnf_ref[...] += jnp.broadcast_to(jnp.sum(df * hn, axis=0, keepdims=True), gnf_ref.shape)
            dfn = df * nw_ref[...]
            dh1_ref[...] = dh2_ref[...] + rstd * (dfn - hn * jnp.mean(dfn * hn, axis=-1, keepdims=True))

    return pl.pallas_call(
        body, name="ffn_bwd_act", grid=(rows // tm, nj),
        in_specs=[pl.BlockSpec((tm, D), lambda i, j: (i, 0)), pl.BlockSpec((tm, FF_WIDE), lambda i, j: (i, j)),
                  pl.BlockSpec((D, FF_WIDE), lambda i, j: (0, j)),
                  pl.BlockSpec((FF_WIDE, D), lambda i, j: (j, 0)),
                  pl.BlockSpec((tm, D), lambda i, j: (i, 0)), pl.BlockSpec((1, D), lambda i, j: (0, 0))],
        out_specs=[pl.BlockSpec((tm, FF_WIDE), lambda i, j: (i, j)), pl.BlockSpec((tm, D), lambda i, j: (i, 0)),
                   pl.BlockSpec((8, D), lambda i, j: (0, 0))],
        out_shape=[jax.ShapeDtypeStruct((rows, D_FF), ACT_DTYPE), jax.ShapeDtypeStruct((rows, D), F32),
                   jax.ShapeDtypeStruct((8, D), F32)],
        scratch_shapes=[pltpu.VMEM((tm, D), F32)],
        compiler_params=_cp(("arbitrary", "arbitrary"), VMEM_WIDE_MB),
    )(dh2, a, w1, w2, h1, nfw)


def _ffn_bwd_weights(ft, a, da, dh2t, tm):
    rows = a.shape[0]
    steps = rows // tm
    pair = 2 * FF_TILE

    def body(ft_ref, a_ref, da_ref, dh2t_ref, dw1_ref, dw2_ref, dw2t):
        i = pl.program_id(1)

        @pl.when(i == 0)
        def _():
            dw1_ref[...] = jnp.zeros_like(dw1_ref)
            dw2t[...] = jnp.zeros_like(dw2t)

        z = jnp.square(jnp.maximum(a_ref[...].astype(F32), 0.0))
        dw1 = _mm(ft_ref[...], da_ref[...])
        for core in range(2):
            dw1_ref[core] += dw1[:, core * FF_TILE:(core + 1) * FF_TILE]
        dw2t[...] += _mm(dh2t_ref[...], z)

        @pl.when(i == steps - 1)
        def _():
            for core in range(2):
                dw2_ref[core] = dw2t[:, core * FF_TILE:(core + 1) * FF_TILE].T

    return pl.pallas_call(
        body, name="ffn_bwd_weights", grid=(N_DEV // 2, steps),
        in_specs=[pl.BlockSpec((D, tm), lambda j, i: (0, i)), pl.BlockSpec((tm, pair), lambda j, i: (i, j)),
                  pl.BlockSpec((tm, pair), lambda j, i: (i, j)), pl.BlockSpec((D, tm), lambda j, i: (0, i))],
        out_specs=[pl.BlockSpec((2, None, D, FF_TILE), lambda j, i: (0, j, 0, 0)),
                   pl.BlockSpec((2, None, FF_TILE, D), lambda j, i: (0, j, 0, 0))],
        out_shape=[jax.ShapeDtypeStruct((2, 4, D, FF_TILE), F32), jax.ShapeDtypeStruct((2, 4, FF_TILE, D), F32)],
        scratch_shapes=[pltpu.VMEM((D, pair), F32)],
        compiler_params=_cp(("arbitrary", "arbitrary"), VMEM_WIDE_MB),
    )(ft, a, da, dh2t)


def _out_proj_bwd(dh1, og, osw, wout, tm, partials):
    rows = dh1.shape[0]
    steps = rows // tm
    ns = len(partials)

    def body(dh1_ref, og_ref, os_ref, w_ref, *rest):
        part_refs, rest = rest[:ns], rest[ns:]
        dog_ref, dos_ref, dw_ref = rest[:3]
        land_refs, (send_sems, recv_sems) = rest[3:3 + ns], rest[3 + ns:]
        i = pl.program_id(0)
        start, finish = _sibling_schedule(part_refs, land_refs, send_sems, recv_sems)

        @pl.when(i == 0)
        def _():
            dw_ref[...] = jnp.zeros_like(dw_ref)
            start()

        pl.when(i == steps - 1)(finish)

        dh1 = dh1_ref[...].astype(MXU_DTYPE)
        dog_ref[...] = _mm_nt(dh1, w_ref[0:512, :])
        dos_ref[...] = _mm_nt(dh1, w_ref[512:1024, :])
        for half, ref in enumerate((og_ref, os_ref)):
            dw = _mm_tn(ref[...], dh1)
            for blk in range(4):
                shard = half * 4 + blk
                dw_ref[shard % 2, shard // 2] += dw[blk * 128:(blk + 1) * 128, :]

    row = lambda w: pl.BlockSpec((tm, w), lambda i: (i, 0))
    outs = pl.pallas_call(
        body, name="out_proj_bwd", grid=(steps,),
        in_specs=[row(D), row(512), row(512), pl.BlockSpec((D, D), lambda i: (0, 0))] + [ANY] * ns,
        out_specs=[row(512), row(512), pl.BlockSpec((2, 4, 128, D), lambda i: (0, 0, 0, 0))] + [ANY] * ns,
        out_shape=[jax.ShapeDtypeStruct((rows, 512), F32), jax.ShapeDtypeStruct((rows, 512), F32),
                   jax.ShapeDtypeStruct((2, 4, 128, D), F32)] + _sibling_shapes(partials),
        scratch_shapes=_sibling_sems(ns),
        compiler_params=_cp(("arbitrary",), VMEM_TILE_MB),
    )(dh1, og, osw, wout, *partials)
    return outs[0], outs[1], outs[2], outs[3:]


def _swa_bwd(qr, kr, vr, osw, dos, sinks, jobs):
    rows = qr.shape[0]
    nblk = rows // SWA_BLOCK
    group = _swa_group(nblk)
    steps = nblk // group
    ns = jobs.n

    def body(q_ref, k0, kp, kc, v0, vp, vc, o_ref, do_ref, sink_ref, *rest):
        dq_ref, dk_ref, dv_ref, dsink_ref = rest[ns:ns + 4]
        start, finish = jobs.bind(rest[:ns], rest[ns + 4:2 * ns + 4], rest[2 * ns + 4:])
        step = pl.program_id(0)

        @pl.when(step == 0)
        def _():
            dk_ref[...] = jnp.zeros_like(dk_ref)
            dv_ref[...] = jnp.zeros_like(dv_ref)
            dsink_ref[...] = jnp.zeros_like(dsink_ref)
            start()

        pl.when(step == steps - 1)(finish)
        for g in range(group):
            block(step * group + g, g, q_ref, k0, kp, kc, v0, vp, vc, o_ref, do_ref, sink_ref,
                  dq_ref, dk_ref, dv_ref, dsink_ref)

    def block(n, g, q_ref, k0, kp, kc, v0, vp, vc, o_ref, do_ref, sink_ref, dq_ref, dk_ref, dv_ref, dsink_ref):
        rs = slice(g * SWA_BLOCK, (g + 1) * SWA_BLOCK)
        kall, vall = _swa_keys(k0, kp, kc, g), _swa_keys(v0, vp, vc, g)
        mask = _swa_mask(n)[0:SWA_BLOCK]
        heads = range(SWA_HEADS)
        hs = [slice(h * SWA_HD, (h + 1) * SWA_HD) for h in heads]
        kv = [slice((h // SWA_GROUP) * SWA_HD, (h // SWA_GROUP + 1) * SWA_HD) for h in heads]
        sink = [sink_ref[0, h] for h in heads]
        qh = [q_ref[rs, hs[h]] for h in heads]
        doh = [do_ref[rs, hs[h]] for h in heads]
        s = [jnp.where(mask, _mm_nt(qh[h], kall[:, kv[h]]), NEG) for h in heads]
        dp = [_mm_nt(doh[h], vall[:, kv[h]]) for h in heads]
        delta = [jnp.sum(doh[h] * o_ref[rs, hs[h]].astype(F32), axis=-1, keepdims=True) for h in heads]
        m = [jnp.maximum(jnp.max(s[h], axis=-1, keepdims=True), sink[h]) for h in heads]
        e = [jnp.exp(s[h] - m[h]) for h in heads]
        inv = [1.0 / (jnp.sum(e[h], axis=-1, keepdims=True) + jnp.exp(sink[h] - m[h])) for h in heads]
        p = [e[h] * inv[h] for h in heads]
        ds = [p[h] * (dp[h] - delta[h]) for h in heads]
        dq = [_mm(ds[h], kall[:, kv[h]]) for h in heads]
        dkh = [_mm_tn(ds[h], qh[h]) for h in heads]
        dvh = [_mm_tn(p[h], doh[h]) for h in heads]
        for h in heads:
            dsink = -jnp.sum(jnp.exp(sink[h] - m[h]) * inv[h] * delta[h], axis=0, keepdims=True)
            dsink_ref[h:h + 1, :] += jnp.broadcast_to(dsink, (1, 128))
        dq_ref[rs, :] = jnp.concatenate(dq, axis=1)
        group_sum = lambda parts, kvh: sum(parts[kvh * SWA_GROUP + 1:(kvh + 1) * SWA_GROUP], parts[kvh * SWA_GROUP])
        dk_all = jnp.concatenate([group_sum(dkh, kvh) for kvh in range(SWA_KV)], axis=1)
        dv_all = jnp.concatenate([group_sum(dvh, kvh) for kvh in range(SWA_KV)], axis=1)
        prev0 = pl.multiple_of(jnp.maximum(n - 1, 0) * SWA_BLOCK, SWA_BLOCK)
        cur0 = pl.multiple_of(n * SWA_BLOCK, SWA_BLOCK)
        for ref, val in ((dk_ref, dk_all), (dv_ref, dv_all)):
            ref[0:SWA_BLOCK, :] += val[0:SWA_BLOCK]
            ref[pl.ds(prev0, SWA_BLOCK), :] += val[SWA_BLOCK:2 * SWA_BLOCK]
            ref[pl.ds(cur0, SWA_BLOCK), :] += val[2 * SWA_BLOCK:]

    blk, first, prev = _swa_specs(group)
    whole = pl.BlockSpec((rows, 128), lambda n: (0, 0))
    outs = pl.pallas_call(
        body, name="swa_bwd", grid=(steps,),
        in_specs=[blk(512), first, prev, blk(128), first, prev, blk(128), blk(512), blk(512),
                  pl.BlockSpec(memory_space=pltpu.SMEM)] + [ANY] * ns,
        out_specs=[blk(512), whole, whole, pl.BlockSpec((8, 128), lambda n: (0, 0))] + [ANY] * ns,
        out_shape=[jax.ShapeDtypeStruct((rows, 512), F32), jax.ShapeDtypeStruct((rows, 128), F32),
                   jax.ShapeDtypeStruct((rows, 128), F32), jax.ShapeDtypeStruct((8, 128), F32)] + jobs.out_shapes,
        scratch_shapes=jobs.sems,
        compiler_params=_cp(("arbitrary",), VMEM_TILE_MB),
    )(qr, kr, kr, kr, vr, vr, vr, osw, dos, sinks, *jobs.inputs)
    return outs[0], outs[1], outs[2], outs[3], jobs.split(outs[4:])


def _gla_bwd(proj, decay, dgate, oraw, states, dog, wg_p, gnw, jobs):
    rows = proj.shape[0]
    nc = rows // GLA_CHUNK
    group = _gla_group(nc)
    steps, nrows = nc // group, group * GLA_CHUNK
    ns = jobs.n

    def body(q_ref, k_ref, v_ref, r_ref, lr_ref, b_ref, dgate_ref, oraw_ref, st_ref, dog_ref, wg_ref, gnw_ref, *rest):
        dq_ref, dk_ref, dv_ref, dr_ref, dlr_ref, dwg_ref, dbg_ref, dgnw_ref = rest[ns:ns + 8]
        dstate, db_scr = rest[2 * ns + 8:2 * ns + 10]
        start, finish = jobs.bind(rest[:ns], rest[ns + 8:2 * ns + 8], rest[2 * ns + 10:])
        t = pl.program_id(0)

        @pl.when(t == 0)
        def _():
            dstate[...] = jnp.zeros_like(dstate)
            dwg_ref[...] = jnp.zeros_like(dwg_ref)
            dbg_ref[...] = jnp.zeros_like(dbg_ref)
            dgnw_ref[...] = jnp.zeros_like(dgnw_ref)
            start()

        pl.when(t == steps - 1)(finish)

        lr, wg = lr_ref[...], wg_ref[...]
        b = b_ref[...]
        eb, enb = jnp.exp(b), jnp.exp(-b)
        scale = GLA_DK ** -0.5
        gq = q_ref[...] * scale * eb
        gk = k_ref[...] * enb
        v = v_ref[...]
        gnw_v = gnw_ref[...]
        tril = _tril64()
        is_last = lax.broadcasted_iota(jnp.int32, (GLA_CHUNK, 1), 0) == GLA_CHUNK - 1
        dgnw = jnp.zeros((1, GLA_DV), F32)
        pairs = [(h, gi) for h in range(GLA_HEADS) for gi in range(group)]
        rs = {gi: slice(gi * GLA_CHUNK, (gi + 1) * GLA_CHUNK) for gi in range(group)}
        s64 = {h: slice(h * GLA_DK, (h + 1) * GLA_DK) for h in range(GLA_HEADS)}
        s128 = {h: slice(h * GLA_DV, (h + 1) * GLA_DV) for h in range(GLA_HEADS)}
        qh = {(h, gi): gq[rs[gi], s64[h]] for h, gi in pairs}
        kh = {(h, gi): gk[rs[gi], s64[h]] for h, gi in pairs}
        vh = {(h, gi): v[rs[gi], s128[h]] for h, gi in pairs}
        ebl = {(h, gi): eb[(gi + 1) * GLA_CHUNK - 1:(gi + 1) * GLA_CHUNK, s64[h]] for h, gi in pairs}
        kl = {pr: kh[pr] * ebl[pr] for pr in pairs}
        st = {(h, gi): st_ref[gi, h] for h, gi in pairs}
        do = {}
        for h, gi in pairs:
            o, rh, dout = oraw_ref[rs[gi], s128[h]], r_ref[rs[gi], s128[h]], dog_ref[rs[gi], s128[h]]
            rstd = lax.rsqrt(jnp.mean(o * o, axis=-1, keepdims=True) + EPS)
            on = o * rstd
            sg = _sigmoid(rh)
            dr_ref[rs[gi], s128[h]] = (dout * (on * gnw_v) * (sg * (1.0 + rh * (1.0 - sg)))).astype(ACT_DTYPE)
            dy = dout * (rh * sg)
            dgnw = dgnw + jnp.sum(dy * on, axis=0, keepdims=True)
            don = dy * gnw_v
            do[h, gi] = rstd * (don - on * jnp.mean(don * on, axis=-1, keepdims=True))
        a = {pr: jnp.where(tril, _mm_nt(qh[pr], kh[pr]), 0.0) for pr in pairs}
        da = {pr: jnp.where(tril, _mm_nt(do[pr], vh[pr]), 0.0) for pr in pairs}
        dinc = {pr: _mm_tn(do[pr], qh[pr]) for pr in pairs}
        dgq = {pr: _mm(da[pr], kh[pr]) + _mm(do[pr], st[pr]) for pr in pairs}
        dgk = {pr: _mm_tn(da[pr], qh[pr]) for pr in pairs}
        dv_a = {pr: _mm_tn(a[pr], do[pr]) for pr in pairs}
        dsp = {}
        for h in range(GLA_HEADS):
            cur = dstate[h]
            for gi in reversed(range(group)):
                dsp[h, gi] = cur
                cur = cur * ebl[h, gi] + dinc[h, gi]
            dstate[h] = cur
        for h, gi in pairs:
            pr = (h, gi)
            dkl = _mm(vh[pr], dsp[pr])
            dv_ref[rs[gi], s128[h]] = (dv_a[pr] + _mm_nt(kl[pr], dsp[pr])).astype(ACT_DTYPE)
            debl = jnp.sum(dsp[pr] * st[pr], axis=0, keepdims=True)
            dq_ref[rs[gi], s64[h]] = (dgq[pr] * (scale * eb[rs[gi], s64[h]])).astype(ACT_DTYPE)
            dk_ref[rs[gi], s64[h]] = ((dgk[pr] + dkl * ebl[pr]) * enb[rs[gi], s64[h]]).astype(ACT_DTYPE)
            last = debl * ebl[pr] + jnp.sum(dkl * kl[pr], axis=0, keepdims=True)
            db_scr[rs[gi], s64[h]] = (dgq[pr] * qh[pr] - dgk[pr] * kh[pr] - dkl * kl[pr]
                                      + jnp.where(is_last, last, 0.0))
        dzg = _masked_sums(_chunk_masks(nrows)[1], db_scr[...]) * dgate_ref[...]
        dlr_ref[...] = _mm_nt(dzg, wg).astype(ACT_DTYPE)
        dwg_ref[...] += _mm_tn(lr, dzg)
        dbg_ref[...] += jnp.broadcast_to(jnp.sum(dzg, axis=0, keepdims=True), dbg_ref.shape)
        dgnw_ref[...] += jnp.broadcast_to(dgnw, dgnw_ref.shape)

    nb = lambda w, col: pl.BlockSpec((nrows, w), lambda t: (steps - 1 - t, col // w))
    const = lambda shape: pl.BlockSpec(shape, lambda t: (0,) * len(shape))
    outs = pl.pallas_call(
        body, name="gla_bwd", grid=(steps,),
        in_specs=[nb(256, C_GQ), nb(256, C_GK), nb(512, C_GV), nb(512, C_GR), nb(128, C_LR), nb(256, 0), nb(256, 0),
                  nb(512, 0),
                  pl.BlockSpec((group, GLA_HEADS, GLA_DV, GLA_DK), lambda t: (steps - 1 - t, 0, 0, 0)), nb(512, 0),
                  const((128, 256)), const((1, 128))] + [ANY] * ns,
        out_specs=[nb(256, 0), nb(256, 0), nb(512, 0), nb(512, 0), nb(128, 0),
                   const((128, 256)), const((8, 256)), const((8, 128))] + [ANY] * ns,
        out_shape=[jax.ShapeDtypeStruct((rows, 256), ACT_DTYPE), jax.ShapeDtypeStruct((rows, 256), ACT_DTYPE),
                   jax.ShapeDtypeStruct((rows, 512), ACT_DTYPE), jax.ShapeDtypeStruct((rows, 512), ACT_DTYPE),
                   jax.ShapeDtypeStruct((rows, 128), ACT_DTYPE), jax.ShapeDtypeStruct((128, 256), F32),
                   jax.ShapeDtypeStruct((8, 256), F32), jax.ShapeDtypeStruct((8, 128), F32)] + jobs.out_shapes,
        scratch_shapes=[pltpu.VMEM((GLA_HEADS, GLA_DV, GLA_DK), F32), pltpu.VMEM((nrows, 256), F32)] + jobs.sems,
        compiler_params=_cp(("arbitrary",)),
    )(proj, proj, proj, proj, proj, decay, dgate, oraw, states, dog, wg_p, gnw, *jobs.inputs)
    return outs[:8], jobs.split(outs[8:])


def _in_proj_bwd(x, lead, dh1, nw, win_p, dgv, dgr, dsq, dgq, dgk, dsk, dsv, dlr, tabs, tm):
    seq = x.shape[0]
    rows = LEAD + seq
    nb = tm // LEAD
    steps = rows // tm

    def first_copy(scr, gx_ref, sem):
        return pltpu.make_async_copy(scr.at[pl.ds(LEAD, tm - LEAD)], gx_ref.at[pl.ds(0, tm - LEAD)], sem)

    def tile_copy(scr, gx_ref, sem, step):
        start = pl.multiple_of(jnp.maximum(step * tm - LEAD, 0), LEAD)
        return pltpu.make_async_copy(scr, gx_ref.at[pl.ds(start, tm)], sem)

    def body(*refs):
        x_refs, refs = refs[:nb], refs[nb:]
        (lead_ref, dh1_ref, nw_ref, w_ref, dgv_ref, dgr_ref, dsq_ref, dgq_ref, dgk_ref, dsk_ref, dsv_ref, dlr_ref,
         c_ref, sa_ref, sb_ref, gx_ref, dlead_ref, dproj_ref, ut_ref, gnm_ref, scr, sem) = refs
        i = pl.program_id(0)

        @pl.when(i == 0)
        def _():
            gnm_ref[...] = jnp.zeros_like(gnm_ref)

        cos, sa, sb = c_ref[...], sa_ref[...], sb_ref[...]
        dsq_v = (_unrope(dsq_ref[...], cos, sa, sb) * (SWA_HD ** -0.5)).astype(MXU_DTYPE)
        dsk_v = _unrope(dsk_ref[...], cos, sa, sb).astype(MXU_DTYPE)
        dproj = jnp.concatenate(
            [dgv_ref[...].astype(MXU_DTYPE), dgr_ref[...].astype(MXU_DTYPE), dgq_ref[...].astype(MXU_DTYPE),
             dgk_ref[...].astype(MXU_DTYPE), dlr_ref[...].astype(MXU_DTYPE), dsq_v, dsk_v,
             dsv_ref[...].astype(MXU_DTYPE)],
            axis=1)
        dproj_ref[...] = dproj
        h = _h_tile(i, lead_ref, x_refs)
        rstd = lax.rsqrt(jnp.mean(h * h, axis=-1, keepdims=True) + EPS)
        hn = h * rstd
        nw_v = nw_ref[...]
        ut_ref[...] = (hn * nw_v).T.astype(ACT_DTYPE)
        du = _mm_nt(dproj, w_ref[...])
        gnm_ref[...] += jnp.broadcast_to(jnp.sum(du * hn, axis=0, keepdims=True), gnm_ref.shape)
        dun = du * nw_v
        dh0 = dh1_ref[...] + rstd * (dun - hn * jnp.mean(dun * hn, axis=-1, keepdims=True))

        if tm > LEAD:
            pl.when(i == 1)(lambda: first_copy(scr, gx_ref, sem).wait())
        pl.when(i > 1)(lambda: tile_copy(scr, gx_ref, sem, i).wait())
        scr[...] = dh0

        @pl.when(i == 0)
        def _():
            dlead_ref[...] = dh0[0:LEAD]
            if tm > LEAD:
                first_copy(scr, gx_ref, sem).start()
                if steps == 1:
                    first_copy(scr, gx_ref, sem).wait()

        @pl.when(i > 0)
        def _():
            tile_copy(scr, gx_ref, sem, i).start()

        if steps > 1:
            pl.when(i == steps - 1)(lambda: tile_copy(scr, gx_ref, sem, i).wait())

    row = lambda w: pl.BlockSpec((tm, w), lambda i: (i, 0))
    const = lambda shape: pl.BlockSpec(shape, lambda i: (0,) * len(shape))
    return pl.pallas_call(
        body, name="in_proj_bwd", grid=(steps,),
        in_specs=_token_specs(tm) + [const((LEAD, D)), row(D), const((1, D)), const((D, DINP)),
                                     row(512), row(512), row(512), row(256), row(256), row(128), row(128), row(128),
                                     row(128), row(128), row(128)],
        out_specs=[ANY, const((LEAD, D)), row(DINP), pl.BlockSpec((D, tm), lambda i: (0, i)), const((8, D))],
        out_shape=[jax.ShapeDtypeStruct((seq, D), F32), jax.ShapeDtypeStruct((LEAD, D), F32),
                   jax.ShapeDtypeStruct((rows, DINP), ACT_DTYPE), jax.ShapeDtypeStruct((D, rows), ACT_DTYPE),
                   jax.ShapeDtypeStruct((8, D), F32)],
        scratch_shapes=[pltpu.VMEM((tm, D), F32), pltpu.SemaphoreType.DMA],
        compiler_params=_cp(("arbitrary",), VMEM_WIDE_MB),
    )(*([x] * nb), lead, dh1, nw, win_p, dgv, dgr, dsq, dgq, dgk, dsk, dsv, dlr, *tabs)


def _win_runs():
    groups = [(O_GQ, C_GQ), (O_GK, C_GK), (O_GV, C_GV), (O_GR, C_GR), (O_LR, C_LR), (O_SQ, C_SQ), (O_SK, C_SK),
              (O_SV, C_SV)]
    per = DIN // N_DEV
    runs = []
    for (o0, o1), c0 in groups:
        o = o0
        while o < o1:
            d = o // per
            end = min(o1, (d + 1) * per)
            runs.append((d, o - d * per, c0 + o - o0, end - o))
            o = end
    return runs


def _win_padded(g_in):
    tr = 128

    def body(g_ref, o_ref):
        o_ref[...] = jnp.zeros_like(o_ref)
        for d, s, c, w in _win_runs():
            o_ref[:, c:c + w] = g_ref[d, :, s:s + w]

    return pl.pallas_call(
        body, name="w_in_layout", grid=(D // tr,),
        in_specs=[pl.BlockSpec((N_DEV, tr, DIN // N_DEV), lambda i: (0, i, 0))],
        out_specs=pl.BlockSpec((tr, DINP), lambda i: (i, 0)),
        out_shape=jax.ShapeDtypeStruct((D, DINP), g_in.dtype),
        compiler_params=_cp(("arbitrary",)),
    )(g_in)


def _in_proj_bwd_weights(ut, dproj, tm, small):
    rows = dproj.shape[0]
    steps = rows // tm
    per = DIN // N_DEV

    def body(ut_ref, dp_ref, *rest):
        small_refs, (mine_ref, theirs_ref, total_ref, acc, stage, local_sems, send_sems, recv_sems) = rest[:9], rest[9:17]
        i = pl.program_id(0)
        start, finish = _small_sum_schedule(small_refs, total_ref, *rest[17:])
        x, y, c = _mesh_pos()

        @pl.when(i == 0)
        def _():
            acc[...] = jnp.zeros_like(acc)
            start()

        acc[...] += _mm(ut_ref[...], dp_ref[...])
        pl.when(i == steps - 1)(finish)

        def keep(slot, chip):
            return pltpu.make_async_copy(stage.at[slot], mine_ref.at[chip], local_sems.at[slot])

        def send(slot, chip):
            return pltpu.make_async_remote_copy(
                src_ref=stage.at[slot], dst_ref=theirs_ref.at[chip], send_sem=send_sems.at[slot],
                recv_sem=recv_sems.at[chip], device_id=(x, y, 1 - c), device_id_type=MESH)

        def drained(d):
            pl.when(c == d % 2)(keep(d % 2, d // 2).wait)
            pl.when(c != d % 2)(send(d % 2, d // 2).wait_send)

        @pl.when(i == steps - 1)
        def _():
            for d in range(N_DEV):
                slot, chip = d % 2, d // 2
                if d >= 2:
                    drained(d - 2)
                for owner, s, col, w in _win_runs():
                    if owner == d:
                        stage[slot, :, s:s + w] = acc[:, col:col + w]
                pl.when(c == slot)(keep(slot, chip).start)
                pl.when(c != slot)(send(slot, chip).start)
            drained(N_DEV - 2)
            drained(N_DEV - 1)
            for chip in range(4):
                send(0, chip).wait_recv()

    half = jax.ShapeDtypeStruct((4, D, per), F32)
    return pl.pallas_call(
        body, name="in_proj_bwd_weights", grid=(steps,),
        in_specs=[pl.BlockSpec((D, tm), lambda i: (0, i)), pl.BlockSpec((tm, DINP), lambda i: (i, 0))] + SMALL_SPECS,
        out_specs=[ANY, ANY, pl.BlockSpec((SMALL_ROWS, D), lambda i: (0, 0))],
        out_shape=[half, half, jax.ShapeDtypeStruct((SMALL_ROWS, D), F32)],
        scratch_shapes=[pltpu.VMEM((D, DINP), F32), pltpu.VMEM((2, D, per), F32), pltpu.SemaphoreType.DMA((2,)),
                        pltpu.SemaphoreType.DMA((2,)), pltpu.SemaphoreType.DMA((4,))] + _small_sum_scratch(),
        compiler_params=_cp(("arbitrary",), VMEM_WIDE_MB),
    )(ut, dproj, *small)


def _adamw(w, g, m, v):
    m = ADAM_B1 * m + (1.0 - ADAM_B1) * g
    v = ADAM_B2 * v + (1.0 - ADAM_B2) * jnp.square(g)
    m_hat = m / (1.0 - ADAM_B1 ** ADAM_STEP)
    v_hat = v / (1.0 - ADAM_B2 ** ADAM_STEP)
    delta = -ADAM_LR * (m_hat / (jnp.sqrt(v_hat) + ADAM_EPS) + ADAM_WD * w)
    return delta, m, v


ADAM_STEPS = 8


def _adamw_shards(where, items, name, jobs=None):
    jobs = jobs or _Jobs([])
    ns, nw = jobs.n, len(items)

    def body(where_ref, *rest):
        ins, rest = rest[:5 * nw], rest[5 * nw:]
        job_ins, rest = rest[:ns], rest[ns:]
        outs, rest = rest[:4 * nw], rest[4 * nw:]
        start, finish = jobs.bind(job_ins, rest[:ns], rest[ns:])
        i = pl.program_id(0)
        pl.when(i == 0)(start)
        pl.when(i == ADAM_STEPS - 1)(finish)
        for k in range(nw):
            p_ref, own_ref, w_ref, m_ref, v_ref = ins[5 * k:5 * k + 5]
            g_ref, d_ref, nm_ref, nv_ref = outs[4 * k:4 * k + 4]
            g = ((p_ref[0].astype(F32) + p_ref[1].astype(F32)) + p_ref[2].astype(F32)) + own_ref[...]
            g_ref[...] = g
            d_ref[...], nm_ref[...], nv_ref[...] = _adamw(w_ref[...], g, m_ref[...], v_ref[...])

    in_specs, out_specs, out_shape, operands = [], [], [], []
    for parts, own, w, m, v in items:
        r, cdim = w.shape
        tr = r // ADAM_STEPS
        spec = pl.BlockSpec((tr, cdim), lambda i, s: (i, 0))
        in_specs += [pl.BlockSpec((3, tr, cdim), lambda i, s: (0, i, 0)),
                     pl.BlockSpec((None, tr, cdim), lambda i, s: (s[1], i, 0)), spec, spec, spec]
        out_specs += [spec] * 4
        out_shape += [jax.ShapeDtypeStruct((r, cdim), F32)] * 4
        operands += [parts, own, w, m, v]
    outs = pl.pallas_call(
        body, name=name,
        grid_spec=pltpu.PrefetchScalarGridSpec(
            num_scalar_prefetch=1, grid=(ADAM_STEPS,),
            in_specs=in_specs + [ANY] * ns, out_specs=out_specs + [ANY] * ns, scratch_shapes=jobs.sems),
        out_shape=out_shape + jobs.out_shapes,
        compiler_params=_cp(("arbitrary",)),
    )(where, *operands, *jobs.inputs)
    return [outs[4 * k:4 * k + 4] for k in range(nw)], jobs.split(outs[4 * nw:])


def _adamw_small(items):
    n = len(items)

    def body(*refs):
        ins, outs = refs[:4 * n], refs[4 * n:]
        for k in range(n):
            w_ref, g_ref, m_ref, v_ref = ins[4 * k:4 * k + 4]
            d_ref, nm_ref, nv_ref = outs[3 * k:3 * k + 3]
            d_ref[...], nm_ref[...], nv_ref[...] = _adamw(w_ref[...], g_ref[...], m_ref[...], v_ref[...])

    vm = pl.BlockSpec(memory_space=pltpu.VMEM)
    shapes = [jax.ShapeDtypeStruct(w.shape, F32) for w, _, _, _ in items for _ in range(3)]
    outs = pl.pallas_call(body, name="adamw_small", in_specs=[vm] * (4 * n), out_specs=[vm] * (3 * n),
                          out_shape=shapes)(*[t for item in items for t in item])
    return [outs[3 * k:3 * k + 3] for k in range(n)]


def _add_halves(mine, theirs, name):
    _, r, cdim = mine.shape
    tr = 128 if r % 128 == 0 else r

    def body(a_ref, b_ref, o_ref, w_ref):
        total = a_ref[...] + b_ref[...]
        o_ref[...] = total
        w_ref[...] = total.astype(WIRE_DTYPE)

    spec = pl.BlockSpec((4, tr, cdim), lambda i: (0, i, 0))
    return pl.pallas_call(
        body, name=name, grid=(r // tr,), in_specs=[spec, spec], out_specs=[spec, spec],
        out_shape=[jax.ShapeDtypeStruct(mine.shape, F32), jax.ShapeDtypeStruct(mine.shape, WIRE_DTYPE)],
        compiler_params=_cp(("arbitrary",)))(mine, theirs)


def _add_own_half(where, full, theirs, name, wire_copy=False):
    _, _, r, cdim = full.shape
    tr = 128 if r % 128 == 0 else r

    def body(where_ref, a_ref, b_ref, *o_refs):
        total = a_ref[...] + b_ref[...]
        o_refs[0][...] = total
        if wire_copy:
            o_refs[1][...] = total.astype(WIRE_DTYPE)

    spec = pl.BlockSpec((4, tr, cdim), lambda i, s: (0, i, 0))
    shapes = [jax.ShapeDtypeStruct(theirs.shape, F32)] + ([jax.ShapeDtypeStruct(theirs.shape, WIRE_DTYPE)] if wire_copy else [])
    outs = pl.pallas_call(
        body, name=name,
        grid_spec=pltpu.PrefetchScalarGridSpec(
            num_scalar_prefetch=1, grid=(r // tr,),
            in_specs=[pl.BlockSpec((None, 4, tr, cdim), lambda i, s: (s[0], 0, i, 0)), spec],
            out_specs=[spec] * len(shapes)),
        out_shape=shapes, compiler_params=_cp(("arbitrary",)))(where, full, theirs)
    return outs if wire_copy else outs[0]


def kernel(x, meta_tokens, norm_mix_w, w_in, w_gate_up, b_gate, gla_norm_w, sinks, w_out, norm_ff_w, w_ff1, w_ff2, final_norm_w, loss_target, m_meta_tokens, m_norm_mix_w, m_w_in, m_w_gate_up, m_b_gate, m_gla_norm_w, m_sinks, m_w_out, m_norm_ff_w, m_w_ff1, m_w_ff2, m_final_norm_w, v_meta_tokens, v_norm_mix_w, v_w_in, v_w_gate_up, v_b_gate, v_gla_norm_w, v_sinks, v_w_out, v_norm_ff_w, v_w_ff1, v_w_ff2, v_final_norm_w):
    seq = x.shape[1]
    rows = LEAD + seq
    tm = _row_tile(rows)
    tm_wide = WIDE_ROW_TILE if rows % WIDE_ROW_TILE == 0 else tm
    dev =4 * lax.axis_index("x") + 2 * lax.axis_index("y") + lax.axis_index("c")

    small_shard = jnp.concatenate([meta_tokens, w_gate_up[0], jnp.zeros((N_META, 96), F32)], axis=1)
    g_in, g_small = _all_gather([w_in[0].astype(WIRE_DTYPE), small_shard])
    later_shards = [w_out[0].astype(WIRE_DTYPE), w_ff1[0].astype(WIRE_DTYPE), w_ff2[0].astype(WIRE_DTYPE)]
    win_p = _win_padded(g_in)
    meta_full = jnp.transpose(g_small[:, :, 0:128], (1, 0, 2)).reshape(N_META, D)
    wg_full = jnp.transpose(g_small[:, :, 128:160], (1, 0, 2)).reshape(GLA_RANK, GLA_HEADS * GLA_DK)
    wg_p = jnp.concatenate([wg_full, jnp.zeros((128 - GLA_RANK, 256), F32)], axis=0)

    lead = jnp.concatenate([jnp.zeros((META0, D), F32), meta_full], axis=0)
    tabs = _rope_tables(rows)
    proj, qr, kr, vr, (g_w1,) = _in_proj(x[0], lead, norm_mix_w, win_p, tabs, tm, later_shards[1:2])
    oraw, og, states, decay, dgate, (g_out,) = _gla_fwd(proj, wg_p, b_gate, gla_norm_w, later_shards[0:1])
    osw, (g_w2,) = _swa_fwd(qr, kr, vr, sinks, later_shards[2:3])
    wout_full = g_out.reshape(D, D)
    w2_full = g_w2.reshape(D_FF, D)
    w1_full = jnp.transpose(g_w1, (1, 0, 2)).reshape(D, D_FF)
    h1, f, ft = _out_proj(x[0], lead, og, osw, wout_full, norm_ff_w, tm)
    a, dh2, dh2t, loss_p, gfn_p = _ffn_fwd(f, h1, w1_full, w2_full, loss_target[0], final_norm_w.reshape(1, D), tm)

    da, dh1, gnf_p = _ffn_bwd_act(dh2, a, w1_full, w2_full, h1, norm_ff_w, tm)
    dw1, dw2 = _ffn_bwd_weights(ft, a, da, dh2t, tm_wide)
    where = jnp.stack([lax.axis_index("c"), 2 * lax.axis_index("x") + lax.axis_index("y")]).astype(jnp.int32)
    dog, dos, dwout, theirs_ffn = _out_proj_bwd(dh1, og, osw, wout_full, tm, [dw1, dw2])
    pairs_ffn = [_add_own_half(where, p, q, "reduce_pair_%d" % (2 + k), wire_copy=True)
                 for k, (p, q) in enumerate(zip([dw1, dw2], theirs_ffn))]
    sums_ffn, wires_ffn = [p[0] for p in pairs_ffn], [p[1] for p in pairs_ffn]
    dsq, dsk, dsv, dsink_p, (parts_ffn, (theirs_wout,)) = _swa_bwd(
        qr, kr, vr, osw, dos, sinks, _Jobs([("chips", wires_ffn), ("sibling", [dwout])]))
    sum_wout, wire_wout = _add_own_half(where, dwout, theirs_wout, "reduce_pair_1", wire_copy=True)
    (dgq, dgk, dgv, dgr, dlr, dwg_p, dbg_p, dgnw_p), ((parts_wout,),) = _gla_bwd(
        proj, decay, dgate, oraw, states, dog, wg_p, gla_norm_w, _Jobs([("chips", [wire_wout])]))
    grad_x, dlead, dproj, ut, gnm_p = _in_proj_bwd(x[0], lead, dh1, norm_mix_w, win_p, dgv, dgr, dsq, dgq, dgk, dsk,
                                                   dsv, dlr, tabs, tm)
    grad_x = grad_x[None]
    dwin_mine, dwin_theirs, total = _in_proj_bwd_weights(
        ut, dproj, tm_wide, [dlead, dwg_p, gnm_p, gnf_p, gfn_p, dbg_p, dgnw_p, loss_p, dsink_p])
    sum_win, sum_win_wire = _add_halves(dwin_mine, dwin_theirs, "reduce_pair_0")

    g_meta = lax.dynamic_slice(total, (R_META, dev * 128), (N_META, 128))
    g_wg = lax.dynamic_slice(total, (R_WG, dev * 32), (GLA_RANK, 32))
    g_norm_mix, g_norm_ff = total[R_NORM_MIX:R_NORM_MIX + 1], total[R_NORM_FF:R_NORM_FF + 1]
    g_final_norm = total[R_FINAL:R_FINAL + 1]
    g_b_gate, g_gla_norm = total[R_B_GATE:R_B_GATE + 1, 0:256], total[R_GLA_NORM:R_GLA_NORM + 1, 0:128]
    g_sinks = total[R_SINKS:R_SINKS + SWA_HEADS, 0].reshape(1, SWA_HEADS)
    loss = total[R_LOSS, 0]

    ((g_wout, d_wout, nm_wout, nv_wout), (g_w1s, d_w1, nm_w1, nv_w1), (g_w2s, d_w2, nm_w2, nv_w2)), ((parts_win,),) = \
        _adamw_shards(where, [(parts_wout, sum_wout, w_out[0], m_w_out[0], v_w_out[0]),
                              (parts_ffn[0], sums_ffn[0], w_ff1[0], m_w_ff1[0], v_w_ff1[0]),
                              (parts_ffn[1], sums_ffn[1], w_ff2[0], m_w_ff2[0], v_w_ff2[0])],
                      "adamw_w_out_ff", _Jobs([("chips", [sum_win_wire])]))
    ((g_win, d_win, nm_win, nv_win),), _ = _adamw_shards(
        where, [(parts_win, sum_win, w_in[0], m_w_in[0], v_w_in[0])], "adamw_w_in")

    names = ["meta", "wg", "norm_mix", "b_gate", "gla_norm", "sinks", "norm_ff", "final_norm"]
    ws = [meta_tokens, w_gate_up, norm_mix_w, b_gate, gla_norm_w, sinks, norm_ff_w, final_norm_w]
    gs = [g_meta, g_wg, g_norm_mix, g_b_gate, g_gla_norm, g_sinks, g_norm_ff, g_final_norm]
    ms = [m_meta_tokens, m_w_gate_up, m_norm_mix_w, m_b_gate, m_gla_norm_w, m_sinks, m_norm_ff_w, m_final_norm_w]
    vs = [v_meta_tokens, v_w_gate_up, v_norm_mix_w, v_b_gate, v_gla_norm_w, v_sinks, v_norm_ff_w, v_final_norm_w]
    flat = lambda t: t.reshape(-1, t.shape[-1])
    small_out = _adamw_small([(flat(w), flat(g), flat(m), flat(v)) for w, g, m, v in zip(ws, gs, ms, vs)])
    d_small = {n: small_out[k][0].reshape(ws[k].shape) for k, n in enumerate(names)}
    nm_small = {n: small_out[k][1].reshape(ws[k].shape) for k, n in enumerate(names)}
    nv_small = {n: small_out[k][2].reshape(ws[k].shape) for k, n in enumerate(names)}
    g_small_d = {n: g.reshape(ws[k].shape) for k, (n, g) in enumerate(zip(names, gs))}

    def ordered(big, small_d):
        win_v, wout_v, w1_v, w2_v = big
        return (small_d["meta"], small_d["norm_mix"], win_v[None], small_d["wg"], small_d["b_gate"],
                small_d["gla_norm"], small_d["sinks"], wout_v[None], small_d["norm_ff"], w1_v[None], w2_v[None],
                small_d["final_norm"])

    return (loss, grad_x,
            *ordered((g_win, g_wout, g_w1s, g_w2s), g_small_d),
            *ordered((d_win, d_wout, d_w1, d_w2), d_small),
            *ordered((nm_win, nm_wout, nm_w1, nm_w2), nm_small),
            *ordered((nv_win, nv_wout, nv_w1, nv_w2), nv_small))
```

```python
import functools

import jax
import jax.numpy as jnp
from jax import lax
from jax.experimental import pallas as pl
from jax.experimental.pallas import tpu as pltpu

F32 = jnp.float32
MXU_DTYPE = jnp.bfloat16
ACT_DTYPE = jnp.bfloat16
WIRE_DTYPE = jnp.bfloat16

D = 1024
N_META = 16
LEAD = 128
META0 = LEAD - N_META
EPS = 1e-5
GLA_HEADS, GLA_DK, GLA_DV, GLA_RANK, GLA_CHUNK = 4, 64, 128, 16, 64
GLA_TAU = 16.0
SWA_HEADS, SWA_KV, SWA_GROUP, SWA_HD, SWA_BLOCK = 8, 2, 4, 64, 128
ROPE_DIM, ROPE_THETA = 16, 500000.0
D_FF = 4096
N_DEV = 8
FF_TILE = D_FF // N_DEV
FF_WIDE = 2048
NEG = -1e30

C_GV, C_GR, C_GQ, C_GK, C_LR, C_SQ, C_SK, C_SV = 0, 512, 1024, 1280, 1536, 1664, 2176, 2304
DGLA = 1664
DINP = 2432
DIN = 2320
O_GQ, O_GK, O_GV, O_GR, O_LR, O_SQ, O_SK, O_SV = (0, 256), (256, 512), (512, 1024), (1024, 1536), (1536, 1552), (1552, 2064), (2064, 2192), (2192, 2320)

ADAM_LR, ADAM_B1, ADAM_B2, ADAM_EPS, ADAM_WD, ADAM_STEP = 0.001, 0.9, 0.999, 1e-08, 0.01, 10

MESH = pl.DeviceIdType.MESH
ANY = pl.BlockSpec(memory_space=pl.ANY)
VMEM_TILE_MB, VMEM_WIDE_MB = 48, 56


def _cp(sem=None, vmem_mb=None):
    kw = {}
    if sem is not None:
        kw["dimension_semantics"] = sem
    if vmem_mb is not None:
        kw["vmem_limit_bytes"] = vmem_mb << 20
    return pltpu.CompilerParams(**kw)


def _mm(a, b):
    return jnp.dot(a.astype(MXU_DTYPE), b.astype(MXU_DTYPE), preferred_element_type=F32)


def _mm_nt(a, b):
    return lax.dot_general(a.astype(MXU_DTYPE), b.astype(MXU_DTYPE), (((1,), (1,)), ((), ())),
                           preferred_element_type=F32)


def _mm_tn(a, b):
    return lax.dot_general(a.astype(MXU_DTYPE), b.astype(MXU_DTYPE), (((0,), (0,)), ((), ())),
                           preferred_element_type=F32)


def _masked_sums(mask, t):
    m = mask.astype(jnp.bfloat16)
    hi = t.astype(jnp.bfloat16)
    rest = t - hi.astype(F32)
    mid = rest.astype(jnp.bfloat16)
    low = (rest - mid.astype(F32)).astype(jnp.bfloat16)
    dot = lambda part: jnp.dot(m, part, preferred_element_type=F32)
    return dot(hi) + (dot(mid) + dot(low))


def _logsigmoid(z):
    return jnp.minimum(z, 0.0) - jnp.log(1.0 + jnp.exp(-jnp.abs(z)))


def _sigmoid(z):
    return 1.0 / (1.0 + jnp.exp(-z))


ROW_TILE, WIDE_ROW_TILE = 640, 1664


def _row_tile(rows, want=ROW_TILE):
    return want if rows % want == 0 else LEAD


def _mesh_pos():
    return lax.axis_index("x"), lax.axis_index("y"), lax.axis_index("c")


def _all_gather(shards):
    n = len(shards)

    def body(*refs):
        start, forward, finish = _gather_schedule(refs[:n], refs[n:2 * n], *refs[2 * n:])
        start()
        for j in range(3):
            forward(j)
        finish()

    gathered = pl.pallas_call(
        body, name="all_gather_weights",
        out_shape=_gathered_shapes(shards), in_specs=[ANY] * n, out_specs=[ANY] * n,
        scratch_shapes=_gather_sems(n),
    )(*shards)
    return _with_own_block(gathered, shards)


def _gathered_shapes(shards):
    return [jax.ShapeDtypeStruct((N_DEV,) + s.shape, s.dtype) for s in shards]


def _gather_sems(n):
    return [pltpu.SemaphoreType.DMA((7 * n,)), pltpu.SemaphoreType.DMA((7 * n,))] if n else []


def _place_gather(step, steps, shard_refs, gathered_refs, sems):
    if not shard_refs:
        return
    start, forward, finish = _gather_schedule(shard_refs, gathered_refs, *sems)
    pl.when(step == 0)(start)
    for j, at in enumerate((steps * 7 // 10, steps * 8 // 10, steps * 9 // 10)):
        pl.when(step == at)(functools.partial(forward, j))
    pl.when(step == steps - 1)(finish)


def _with_own_block(gathered, shards):
    dev = 4 * lax.axis_index("x") + 2 * lax.axis_index("y") + lax.axis_index("c")
    return [lax.dynamic_update_index_in_dim(g, s, dev, 0) for g, s in zip(gathered, shards)]


def _gather_schedule(ins, outs, send_sems, recv_sems):
    n = len(ins)
    x, y, c = _mesh_pos()
    me, sibling = (x, y, c), (x, y, 1 - c)
    chips = [(1 - x, y), (x, 1 - y), (1 - x, 1 - y)]

    def copy(a, k, block, to, src=None):
        dst = outs[a].at[4 * block[0] + 2 * block[1] + block[2]]
        return pltpu.make_async_remote_copy(
            src_ref=dst if src is None else src, dst_ref=dst,
            send_sem=send_sems.at[a * 7 + k], recv_sem=recv_sems.at[a * 7 + k],
            device_id=to, device_id_type=MESH)

    def first(a):
        return [copy(a, 0, me, sibling, src=ins[a])] + [copy(a, 1 + j, me, (*chip, c), src=ins[a])
                                                        for j, chip in enumerate(chips)]

    def start():
        for a in range(n):
            for cp in first(a):
                cp.start()

    def forward(j):
        for a in range(n):
            copy(a, 1 + j, (*chips[j], c), me).wait_recv()
            copy(a, 4 + j, (*chips[j], c), sibling).start()

    def finish():
        for a in range(n):
            copy(a, 0, sibling, me).wait_recv()
            for j, chip in enumerate(chips):
                copy(a, 4 + j, (*chip, 1 - c), me).wait_recv()
        for a in range(n):
            for cp in first(a) + [copy(a, 4 + j, (*chip, c), sibling) for j, chip in enumerate(chips)]:
                cp.wait_send()

    return start, forward, finish


def _sibling_shapes(gs):
    return [jax.ShapeDtypeStruct(g.shape[1:], g.dtype) for g in gs]


def _sibling_sems(n):
    return [pltpu.SemaphoreType.DMA((n,)), pltpu.SemaphoreType.DMA((n,))]


def _sibling_schedule(ins, land, send_sems, recv_sems):
    x, y, c = _mesh_pos()

    def copies():
        return [pltpu.make_async_remote_copy(
            src_ref=ins[a].at[1 - c], dst_ref=land[a], send_sem=send_sems.at[a], recv_sem=recv_sems.at[a],
            device_id=(x, y, 1 - c), device_id_type=MESH) for a in range(len(ins))]

    def start():
        for cp in copies():
            cp.start()

    def finish():
        for cp in copies():
            cp.wait_recv()
        for cp in copies():
            cp.wait_send()

    return start, finish


def _chips_shapes(ps):
    return [jax.ShapeDtypeStruct((3,) + p.shape[1:], p.dtype) for p in ps]


def _chips_sems(n):
    return [pltpu.SemaphoreType.DMA((3 * n,)), pltpu.SemaphoreType.DMA((3 * n,))]


def _chips_schedule(ins, land, send_sems, recv_sems):
    x, y, c = _mesh_pos()
    chips = [(1 - x, y), (x, 1 - y), (1 - x, 1 - y)]

    def copies():
        return [pltpu.make_async_remote_copy(
            src_ref=ins[a].at[2 * chip[0] + chip[1]], dst_ref=land[a].at[j],
            send_sem=send_sems.at[3 * a + j], recv_sem=recv_sems.at[3 * a + j],
            device_id=(*chip, c), device_id_type=MESH) for a in range(len(ins)) for j, chip in enumerate(chips)]

    def start():
        for cp in copies():
            cp.start()

    def finish():
        for cp in copies():
            cp.wait_recv()
        for cp in copies():
            cp.wait_send()

    return start, finish


class _Jobs:
    def __init__(self, jobs):
        self.jobs = jobs
        self.inputs = [a for _, arrs in jobs for a in arrs]
        self.out_shapes = [s for kind, arrs in jobs
                           for s in (_sibling_shapes(arrs) if kind == "sibling" else _chips_shapes(arrs))]
        self.sems = [s for kind, arrs in jobs
                     for s in (_sibling_sems(len(arrs)) if kind == "sibling" else _chips_sems(len(arrs)))]
        self.n = len(self.inputs)

    def bind(self, in_refs, out_refs, sem_refs):
        starts, finishes, at = [], [], 0
        for k, (kind, arrs) in enumerate(self.jobs):
            schedule = _sibling_schedule if kind == "sibling" else _chips_schedule
            start, finish = schedule(in_refs[at:at + len(arrs)], out_refs[at:at + len(arrs)],
                                     sem_refs[2 * k], sem_refs[2 * k + 1])
            starts.append(start)
            finishes.append(finish)
            at += len(arrs)

        def start_all():
            for f in starts:
                f()

        def finish_all():
            for f in finishes:
                f()

        return start_all, finish_all

    def split(self, outs):
        res, at = [], 0
        for _, arrs in self.jobs:
            res.append(list(outs[at:at + len(arrs)]))
            at += len(arrs)
        return res


R_META, R_WG, R_NORM_MIX, R_NORM_FF, R_FINAL, R_B_GATE, R_GLA_NORM, R_LOSS, R_SINKS, SMALL_ROWS = 0, 16, 32, 33, 34, 35, 36, 37, 40, 48


SMALL_SPECS = [pl.BlockSpec((LEAD, D), lambda i: (0, 0)), pl.BlockSpec((128, 256), lambda i: (0, 0)),
               pl.BlockSpec((8, D), lambda i: (0, 0)), pl.BlockSpec((8, D), lambda i: (0, 0)),
               pl.BlockSpec((8, D), lambda i: (0, 0)), pl.BlockSpec((8, 256), lambda i: (0, 0)),
               pl.BlockSpec((8, 128), lambda i: (0, 0)), pl.BlockSpec((8, 128), lambda i: (0, 0)),
               pl.BlockSpec((8, 128), lambda i: (0, 0))]


def _small_sum_scratch():
    return [pltpu.VMEM((SMALL_ROWS, D), F32), pltpu.VMEM((N_DEV, SMALL_ROWS, D), F32),
            pltpu.SemaphoreType.DMA((7,)), pltpu.SemaphoreType.DMA((7,))]


def _small_sum_schedule(small_refs, out_ref, p_ref, land, send_sems, recv_sems):
    dlead_ref, dwg_ref, gnm_ref, gnf_ref, gfn_ref, dbg_ref, dgnw_ref, loss_ref, dsink_ref = small_refs
    x, y, c = _mesh_pos()
    me = 4 * x + 2 * y + c

    def copies():
        res = []
        for k in range(1, N_DEV):
            bx, by, bc = (k >> 2) & 1, (k >> 1) & 1, k & 1
            peer = (1 - x if bx else x, 1 - y if by else y, 1 - c if bc else c)
            res.append(pltpu.make_async_remote_copy(
                src_ref=p_ref, dst_ref=land.at[me], send_sem=send_sems.at[k - 1], recv_sem=recv_sems.at[k - 1],
                device_id=peer, device_id_type=MESH))
        return res

    def start():
        p_ref[...] = jnp.zeros_like(p_ref)
        p_ref[R_META:R_META + N_META, :] = dlead_ref[META0:LEAD, :]
        p_ref[R_WG:R_WG + GLA_RANK, 0:256] = dwg_ref[0:GLA_RANK, :]
        p_ref[R_NORM_MIX:R_NORM_MIX + 1, :] = gnm_ref[0:1, :]
        p_ref[R_NORM_FF:R_NORM_FF + 1, :] = gnf_ref[0:1, :]
        p_ref[R_FINAL:R_FINAL + 1, :] = gfn_ref[0:1, :]
        p_ref[R_B_GATE:R_B_GATE + 1, 0:256] = dbg_ref[0:1, :]
        p_ref[R_GLA_NORM:R_GLA_NORM + 1, 0:128] = dgnw_ref[0:1, :]
        p_ref[R_LOSS:R_LOSS + 1, 0:128] = loss_ref[0:1, :]
        p_ref[R_SINKS:R_SINKS + SWA_HEADS, 0:128] = dsink_ref[...]
        land[me] = p_ref[...]
        for cp in copies():
            cp.start()

    def finish():
        for cp in copies():
            cp.wait_recv()
        for cp in copies():
            cp.wait_send()
        acc = land[0]
        for d in range(1, N_DEV):
            acc = acc + land[d]
        out_ref[...] = acc

    return start, finish


def _token_specs(tm, grid_rank=1):
    nb = tm // LEAD

    def spec(k):
        if grid_rank == 1:
            return pl.BlockSpec((LEAD, D), lambda i: (jnp.maximum(i * nb + k - 1, 0), 0))
        return pl.BlockSpec((LEAD, D), lambda i, j: (jnp.maximum(i * nb + k - 1, 0), 0))

    return [spec(k) for k in range(nb)]


def _h_tile(i, lead_ref, x_refs):
    first = jnp.where(i == 0, lead_ref[...], x_refs[0][...])
    return jnp.concatenate([first] + [r[...] for r in x_refs[1:]], axis=0)


def _in_proj(x, lead, nw, win_p, tabs, tm, shards):
    rows = LEAD + x.shape[0]
    nb = tm // LEAD
    steps = rows // tm
    ns = len(shards)

    def body(*refs):
        x_refs, refs = refs[:nb], refs[nb:]
        lead_ref, nw_ref, w_ref, c_ref, sa_ref, sb_ref = refs[:6]
        shard_refs, (o_ref, q_ref, k_ref, v_ref) = refs[6:6 + ns], refs[6 + ns:10 + ns]
        _place_gather(pl.program_id(0), steps, shard_refs, refs[10 + ns:10 + 2 * ns], refs[10 + 2 * ns:])
        h = _h_tile(pl.program_id(0), lead_ref, x_refs)
        rstd = lax.rsqrt(jnp.mean(h * h, axis=-1, keepdims=True) + EPS)
        u = (h * rstd * nw_ref[...]).astype(MXU_DTYPE)
        proj = jnp.dot(u, w_ref[...].astype(MXU_DTYPE), preferred_element_type=F32)
        o_ref[...] = proj[:, 0:DGLA]
        cos, sa, sb = c_ref[...], sa_ref[...], sb_ref[...]
        q_ref[...] = (_rope(proj[:, C_SQ:C_SK], cos, sa, sb) * (SWA_HD ** -0.5)).astype(ACT_DTYPE)
        k_ref[...] = _rope(proj[:, C_SK:C_SV], cos, sa, sb).astype(ACT_DTYPE)
        v_ref[...] = proj[:, C_SV:DINP].astype(ACT_DTYPE)

    row = lambda w: pl.BlockSpec((tm, w), lambda i: (i, 0))
    outs = pl.pallas_call(
        body, name="in_proj", grid=(steps,),
        in_specs=_token_specs(tm) + [pl.BlockSpec((LEAD, D), lambda i: (0, 0)), pl.BlockSpec((1, D), lambda i: (0, 0)),
                                     pl.BlockSpec((D, DINP), lambda i: (0, 0)), row(128), row(128), row(128)]
        + [ANY] * ns,
        out_specs=[row(DGLA), row(512), row(128), row(128)] + [ANY] * ns,
        out_shape=[jax.ShapeDtypeStruct((rows, DGLA), F32), jax.ShapeDtypeStruct((rows, 512), ACT_DTYPE),
                   jax.ShapeDtypeStruct((rows, 128), ACT_DTYPE), jax.ShapeDtypeStruct((rows, 128), ACT_DTYPE)]
        + _gathered_shapes(shards),
        scratch_shapes=_gather_sems(ns),
        compiler_params=_cp(("arbitrary",), VMEM_WIDE_MB),
    )(*([x] * nb), lead, nw, win_p, *tabs, *shards)
    return outs[0], outs[1], outs[2], outs[3], _with_own_block(outs[4:], shards)


def _rope_tables(rows):
    pos = (jnp.arange(rows, dtype=jnp.int32) - META0).astype(F32)
    inv_freq = 1.0 / (ROPE_THETA ** (jnp.arange(0, ROPE_DIM, 2, dtype=F32) / ROPE_DIM))
    ang = pos[:, None] * jnp.tile(inv_freq, 128 // (ROPE_DIM // 2))[None, :]
    in_head = jnp.arange(128, dtype=jnp.int32)[None, :] % SWA_HD
    cos, sin = jnp.cos(ang), jnp.sin(ang)
    c_tab = jnp.where(in_head < ROPE_DIM, cos, 1.0)
    sa_tab = jnp.where(in_head < ROPE_DIM // 2, -sin, 0.0)
    sb_tab = jnp.where((in_head >= ROPE_DIM // 2) & (in_head < ROPE_DIM), sin, 0.0)
    return c_tab, sa_tab, sb_tab


def _rope(xv, cos, sa, sb):
    width = xv.shape[1]
    reps = width // 128
    if reps > 1:
        cos, sa, sb = (jnp.tile(t, (1, reps)) for t in (cos, sa, sb))
    return xv * cos + pltpu.roll(xv, width - 8, 1) * sa + pltpu.roll(xv, 8, 1) * sb


def _unrope(dy, cos, sa, sb):
    width = dy.shape[1]
    reps = width // 128
    if reps > 1:
        cos, sa, sb = (jnp.tile(t, (1, reps)) for t in (cos, sa, sb))
    return dy * cos + pltpu.roll(dy * sa, 8, 1) + pltpu.roll(dy * sb, width - 8, 1)


def _gla_group(nc, most):
    for g in (10, 5, 2):
        if g <= most and nc % g == 0:
            return g
    return 1


def _chunk_masks(nrows):
    ii = lax.broadcasted_iota(jnp.int32, (nrows, nrows), 0)
    jj = lax.broadcasted_iota(jnp.int32, (nrows, nrows), 1)
    same = (ii // GLA_CHUNK) == (jj // GLA_CHUNK)
    return same & (jj <= ii), same & (jj >= ii)


def _gla_gates(lr, wg, bg, first_row):
    nrows = lr.shape[0]
    zg = _mm(lr, wg) + bg
    live = first_row + lax.broadcasted_iota(jnp.int32, (nrows, 1), 0) >= META0
    g = jnp.where(live, _logsigmoid(zg) * (1.0 / GLA_TAU), 0.0)
    return _masked_sums(_chunk_masks(nrows)[0], g), jnp.where(live, _sigmoid(-zg) * (1.0 / GLA_TAU), 0.0)


def _tril64():
    ii = lax.broadcasted_iota(jnp.int32, (GLA_CHUNK, GLA_CHUNK), 0)
    jj = lax.broadcasted_iota(jnp.int32, (GLA_CHUNK, GLA_CHUNK), 1)
    return jj <= ii


def _gla_fwd(proj, wg_p, bg, gnw, shards):
    rows = proj.shape[0]
    nc = rows // GLA_CHUNK
    group = _gla_group(nc, 10)
    steps, nrows = nc // group, group * GLA_CHUNK
    ns = len(shards)

    def body(q_ref, k_ref, v_ref, r_ref, lr_ref, lr_next_ref, wg_ref, bg_ref, gnw_ref, *rest):
        shard_refs, rest = rest[:ns], rest[ns:]
        oraw_ref, og_ref, st_ref, decay_ref, dgate_ref = rest[:5]
        gathered_refs, rest = rest[5:5 + ns], rest[5 + ns:]
        state, gates = rest[:2]
        c = pl.program_id(0)

        @pl.when(c == 0)
        def _():
            state[...] = jnp.zeros_like(state)
            gates[0, 0], gates[0, 1] = _gla_gates(lr_ref[...], wg_ref[...], bg_ref[...], 0)

        _place_gather(c, steps, shard_refs, gathered_refs, rest[2:])
        slot = c % 2
        b = gates[slot, 0]
        decay_ref[...] = b
        dgate_ref[...] = gates[slot, 1]
        gates[1 - slot, 0], gates[1 - slot, 1] = _gla_gates(lr_next_ref[...], wg_ref[...], bg_ref[...], (c + 1) * nrows)
        eb = jnp.exp(b)
        gq = q_ref[...] * (GLA_DK ** -0.5) * eb
        gk = k_ref[...] * jnp.exp(-b)
        v = v_ref[...]
        gnw_v = gnw_ref[...]
        tril = _tril64()
        pairs = [(h, gi) for h in range(GLA_HEADS) for gi in range(group)]
        rs = {gi: slice(gi * GLA_CHUNK, (gi + 1) * GLA_CHUNK) for gi in range(group)}
        s64 = {h: slice(h * GLA_DK, (h + 1) * GLA_DK) for h in range(GLA_HEADS)}
        s128 = {h: slice(h * GLA_DV, (h + 1) * GLA_DV) for h in range(GLA_HEADS)}
        qh = {(h, gi): gq[rs[gi], s64[h]] for h, gi in pairs}
        kh = {(h, gi): gk[rs[gi], s64[h]] for h, gi in pairs}
        vh = {(h, gi): v[rs[gi], s128[h]] for h, gi in pairs}
        ebl = {(h, gi): eb[(gi + 1) * GLA_CHUNK - 1:(gi + 1) * GLA_CHUNK, s64[h]] for h, gi in pairs}
        av = {pr: _mm(jnp.where(tril, _mm_nt(qh[pr], kh[pr]), 0.0), vh[pr]) for pr in pairs}
        inc = {pr: _mm_tn(vh[pr], kh[pr] * ebl[pr]) for pr in pairs}
        st = {}
        for h in range(GLA_HEADS):
            cur = state[h]
            for gi in range(group):
                st[h, gi] = cur
                st_ref[gi, h] = cur
                cur = cur * ebl[h, gi] + inc[h, gi]
            state[h] = cur
        for h, gi in pairs:
            o = av[h, gi] + _mm_nt(qh[h, gi], st[h, gi])
            oraw_ref[rs[gi], s128[h]] = o
            rstd = lax.rsqrt(jnp.mean(o * o, axis=-1, keepdims=True) + EPS)
            rh = r_ref[rs[gi], s128[h]]
            og_ref[rs[gi], s128[h]] = (o * rstd * gnw_v * (rh * _sigmoid(rh))).astype(ACT_DTYPE)

    nb = lambda w, col: pl.BlockSpec((nrows, w), lambda c: (c, col // w))
    const = lambda shape: pl.BlockSpec(shape, lambda c: (0,) * len(shape))
    outs = pl.pallas_call(
        body, name="gla_fwd", grid=(steps,),
        in_specs=[nb(256, C_GQ), nb(256, C_GK), nb(512, C_GV), nb(512, C_GR), nb(128, C_LR),
                  pl.BlockSpec((nrows, 128), lambda c: (jnp.minimum(c + 1, steps - 1), C_LR // 128)),
                  const((128, 256)), const((1, 256)), const((1, 128))] + [ANY] * ns,
        out_specs=[nb(512, 0), nb(512, 0),
                   pl.BlockSpec((group, GLA_HEADS, GLA_DV, GLA_DK), lambda c: (c, 0, 0, 0)),
                   nb(256, 0), nb(256, 0)] + [ANY] * ns,
        out_shape=[jax.ShapeDtypeStruct((rows, 512), F32), jax.ShapeDtypeStruct((rows, 512), ACT_DTYPE),
                   jax.ShapeDtypeStruct((nc, GLA_HEADS, GLA_DV, GLA_DK), F32),
                   jax.ShapeDtypeStruct((rows, 256), F32), jax.ShapeDtypeStruct((rows, 256), F32)]
        + _gathered_shapes(shards),
        scratch_shapes=[pltpu.VMEM((GLA_HEADS, GLA_DV, GLA_DK), F32), pltpu.VMEM((2, 2, nrows, 256), F32)]
        + _gather_sems(ns),
        compiler_params=_cp(("arbitrary",)),
    )(proj, proj, proj, proj, proj, proj, wg_p, bg, gnw, *shards)
    return outs[0], outs[1], outs[2], outs[3], outs[4], _with_own_block(outs[5:], shards)


def _swa_mask(n):
    shape = (SWA_GROUP * SWA_BLOCK, 3 * SWA_BLOCK)
    qi = lax.broadcasted_iota(jnp.int32, shape, 0) & (SWA_BLOCK - 1)
    jj = lax.broadcasted_iota(jnp.int32, shape, 1)
    meta = (jj < SWA_BLOCK) & (jj >= META0) & ((n > 0) | (jj <= qi))
    prev = (jj >= SWA_BLOCK) & (jj < 2 * SWA_BLOCK) & (n >= 2) & (jj - SWA_BLOCK > qi)
    cur = (jj >= 2 * SWA_BLOCK) & (n >= 1) & (jj - 2 * SWA_BLOCK <= qi)
    return meta | prev | cur


def _stack_heads(t, kvh):
    return jnp.concatenate([t[:, (kvh * SWA_GROUP + g) * SWA_HD:(kvh * SWA_GROUP + g + 1) * SWA_HD]
                            for g in range(SWA_GROUP)], axis=0)


def _stack_sinks(sink_ref, kvh):
    return jnp.concatenate([jnp.full((SWA_BLOCK, 1), sink_ref[0, kvh * SWA_GROUP + g], F32)
                            for g in range(SWA_GROUP)], axis=0)


def _swa_group(nblk):
    return 5 if nblk % 5 == 0 else 1


def _swa_specs(group):
    blk = lambda w: pl.BlockSpec((group * SWA_BLOCK, w), lambda n: (n, 0))
    first = pl.BlockSpec((SWA_BLOCK, 128), lambda n: (0, 0))
    prev = pl.BlockSpec((SWA_BLOCK, 128), lambda n: (jnp.maximum(n * group - 1, 0), 0))
    return blk, first, prev


def _swa_keys(first_ref, prev_ref, cur_ref, g):
    own = cur_ref[g * SWA_BLOCK:(g + 1) * SWA_BLOCK, :]
    before = prev_ref[...] if g == 0 else cur_ref[(g - 1) * SWA_BLOCK:g * SWA_BLOCK, :]
    return jnp.concatenate([first_ref[...], before, own], axis=0)


def _swa_fwd(qr, kr, vr, sinks, shards):
    rows = qr.shape[0]
    nblk = rows // SWA_BLOCK
    group = _swa_group(nblk)
    steps = nblk // group
    ns = len(shards)

    def body(q_ref, k0, kp, kc, v0, vp, vc, sink_ref, *rest):
        o_ref = rest[ns]
        _place_gather(pl.program_id(0), steps, rest[:ns], rest[ns + 1:2 * ns + 1], rest[2 * ns + 1:])
        for g in range(group):
            n = pl.program_id(0) * group + g
            rs = slice(g * SWA_BLOCK, (g + 1) * SWA_BLOCK)
            kall, vall = _swa_keys(k0, kp, kc, g), _swa_keys(v0, vp, vc, g)
            mask = _swa_mask(n)[0:SWA_BLOCK]
            heads = range(SWA_HEADS)
            hs = [slice(h * SWA_HD, (h + 1) * SWA_HD) for h in heads]
            kv = [slice((h // SWA_GROUP) * SWA_HD, (h // SWA_GROUP + 1) * SWA_HD) for h in heads]
            s = [jnp.where(mask, _mm_nt(q_ref[rs, hs[h]], kall[:, kv[h]]), NEG) for h in heads]
            m = [jnp.maximum(jnp.max(s[h], axis=-1, keepdims=True), sink_ref[0, h]) for h in heads]
            p = [jnp.exp(s[h] - m[h]) for h in heads]
            den = [jnp.sum(p[h], axis=-1, keepdims=True) + jnp.exp(sink_ref[0, h] - m[h]) for h in heads]
            o = [_mm(p[h], vall[:, kv[h]]) for h in heads]
            for h in heads:
                o_ref[rs, hs[h]] = (o[h] / den[h]).astype(ACT_DTYPE)

    blk, first, prev = _swa_specs(group)
    outs = pl.pallas_call(
        body, name="swa_fwd", grid=(steps,),
        in_specs=[blk(512), first, prev, blk(128), first, prev, blk(128),
                  pl.BlockSpec(memory_space=pltpu.SMEM)] + [ANY] * ns,
        out_specs=[blk(512)] + [ANY] * ns,
        out_shape=[jax.ShapeDtypeStruct((rows, 512), ACT_DTYPE)] + _gathered_shapes(shards),
        scratch_shapes=_gather_sems(ns),
        compiler_params=_cp(("arbitrary",)),
    )(qr, kr, kr, kr, vr, vr, vr, sinks, *shards)
    return outs[0], _with_own_block(outs[1:], shards)


def _out_proj(x, lead, og, osw, wout, nfw, tm):
    rows = LEAD + x.shape[0]
    nb = tm // LEAD

    def body(*refs):
        x_refs, (lead_ref, og_ref, os_ref, w_ref, nw_ref, h1_ref, f_ref, ft_ref) = refs[:nb], refs[nb:]
        h0 = _h_tile(pl.program_id(0), lead_ref, x_refs)
        h1 = h0 + _mm(og_ref[...], w_ref[0:512, :]) + _mm(os_ref[...], w_ref[512:1024, :])
        h1_ref[...] = h1
        rstd = lax.rsqrt(jnp.mean(h1 * h1, axis=-1, keepdims=True) + EPS)
        f = h1 * rstd * nw_ref[...]
        f_ref[...] = f.astype(ACT_DTYPE)
        ft_ref[...] = f.T.astype(ACT_DTYPE)

    row = lambda w: pl.BlockSpec((tm, w), lambda i: (i, 0))
    return pl.pallas_call(
        body, name="out_proj", grid=(rows // tm,),
        in_specs=_token_specs(tm) + [pl.BlockSpec((LEAD, D), lambda i: (0, 0)), row(512), row(512),
                                     pl.BlockSpec((D, D), lambda i: (0, 0)), pl.BlockSpec((1, D), lambda i: (0, 0))],
        out_specs=[row(D), row(D), pl.BlockSpec((D, tm), lambda i: (0, i))],
        out_shape=[jax.ShapeDtypeStruct((rows, D), F32), jax.ShapeDtypeStruct((rows, D), ACT_DTYPE),
                   jax.ShapeDtypeStruct((D, rows), ACT_DTYPE)],
        compiler_params=_cp(("arbitrary",), VMEM_TILE_MB),
    )(*([x] * nb), lead, og, osw, wout, nfw)


def _ffn_fwd(f, h1, w1, w2, tgt, fnw, tm):
    rows = f.shape[0]
    nj = D_FF // FF_WIDE
    nb = tm // LEAD

    def body(f_ref, h1_ref, w1_ref, w2_ref, nw_ref, *rest):
        t_refs, (a_ref, dh2_ref, dh2t_ref, loss_ref, gfn_ref, acc) = rest[:nb], rest[nb:]
        i, j = pl.program_id(0), pl.program_id(1)

        @pl.when((i == 0) & (j == 0))
        def _():
            loss_ref[...] = jnp.zeros_like(loss_ref)
            gfn_ref[...] = jnp.zeros_like(gfn_ref)

        @pl.when(j == 0)
        def _():
            acc[...] = jnp.zeros_like(acc)

        a = _mm(f_ref[...], w1_ref[...])
        a_ref[...] = a.astype(ACT_DTYPE)
        z = jnp.square(jnp.maximum(a, 0.0))
        acc[...] += _mm(z, w2_ref[...])

        @pl.when(j == nj - 1)
        def _():
            h2 = h1_ref[...] + acc[...]
            rstd = lax.rsqrt(jnp.mean(h2 * h2, axis=-1, keepdims=True) + EPS)
            hn = h2 * rstd
            nw = nw_ref[...]
            row = i * tm + lax.broadcasted_iota(jnp.int32, (tm, 1), 0)
            target = jnp.concatenate([t[...] for t in t_refs], axis=0)
            err = jnp.where(row >= LEAD, hn * nw - target, 0.0)
            row_loss = jnp.sum(err * err, axis=-1, keepdims=True) * (1.0 / D)
            loss_ref[...] += jnp.broadcast_to(0.5 * jnp.sum(row_loss, axis=0, keepdims=True), loss_ref.shape)
            dy = err * (1.0 / D)
            gfn_ref[...] += jnp.broadcast_to(jnp.sum(dy * hn, axis=0, keepdims=True), gfn_ref.shape)
            dhn = dy * nw
            dh2 = rstd * (dhn - hn * jnp.mean(dhn * hn, axis=-1, keepdims=True))
            dh2_ref[...] = dh2
            dh2t_ref[...] = dh2.T.astype(ACT_DTYPE)

    return pl.pallas_call(
        body, name="ffn_fwd", grid=(rows // tm, nj),
        in_specs=[pl.BlockSpec((tm, D), lambda i, j: (i, 0)), pl.BlockSpec((tm, D), lambda i, j: (i, 0)),
                  pl.BlockSpec((D, FF_WIDE), lambda i, j: (0, j)),
                  pl.BlockSpec((FF_WIDE, D), lambda i, j: (j, 0)),
                  pl.BlockSpec((1, D), lambda i, j: (0, 0))] + _token_specs(tm, grid_rank=2),
        out_specs=[pl.BlockSpec((tm, FF_WIDE), lambda i, j: (i, j)), pl.BlockSpec((tm, D), lambda i, j: (i, 0)),
                   pl.BlockSpec((D, tm), lambda i, j: (0, i)),
                   pl.BlockSpec((8, 128), lambda i, j: (0, 0)), pl.BlockSpec((8, D), lambda i, j: (0, 0))],
        out_shape=[jax.ShapeDtypeStruct((rows, D_FF), ACT_DTYPE), jax.ShapeDtypeStruct((rows, D), F32),
                   jax.ShapeDtypeStruct((D, rows), ACT_DTYPE),
                   jax.ShapeDtypeStruct((8, 128), F32), jax.ShapeDtypeStruct((8, D), F32)],
        scratch_shapes=[pltpu.VMEM((tm, D), F32)],
        compiler_params=_cp(("arbitrary", "arbitrary"), VMEM_WIDE_MB),
    )(f, h1, w1, w2, fnw, *([tgt] * nb))


def _ffn_bwd_act(dh2, a, w1, w2, h1, nfw, tm):
    rows = dh2.shape[0]
    nj = D_FF // FF_WIDE

    def body(dh2_ref, a_ref, w1_ref, w2_ref, h1_ref, nw_ref, da_ref, dh1_ref, gnf_ref, acc):
        i, j = pl.program_id(0), pl.program_id(1)

        @pl.when((i == 0) & (j == 0))
        def _():
            gnf_ref[...] = jnp.zeros_like(gnf_ref)

        @pl.when(j == 0)
        def _():
            acc[...] = jnp.zeros_like(acc)

        dz = _mm_nt(dh2_ref[...], w2_ref[...])
        da = dz * (2.0 * jnp.maximum(a_ref[...].astype(F32), 0.0))
        da_ref[...] = da.astype(ACT_DTYPE)
        acc[...] += _mm_nt(da, w1_ref[...])

        @pl.when(j == nj - 1)
        def _():
            h1 = h1_ref[...]
            rstd = lax.rsqrt(jnp.mean(h1 * h1, axis=-1, keepdims=True) + EPS)
            hn = h1 * rstd
            df = acc[...]
            gnf_ref[...] += jnp.broadcast_to(jnp.sum(df * hn, axis=0, keepdims=True), gnf_ref.shape)
            dfn = df * nw_ref[...]
            dh1_ref[...] = dh2_ref[...] + rstd * (dfn - hn * jnp.mean(dfn * hn, axis=-1, keepdims=True))

    return pl.pallas_call(
        body, name="ffn_bwd_act", grid=(rows // tm, nj),
        in_specs=[pl.BlockSpec((tm, D), lambda i, j: (i, 0)), pl.BlockSpec((tm, FF_WIDE), lambda i, j: (i, j)),
                  pl.BlockSpec((D, FF_WIDE), lambda i, j: (0, j)),
                  pl.BlockSpec((FF_WIDE, D), lambda i, j: (j, 0)),
                  pl.BlockSpec((tm, D), lambda i, j: (i, 0)), pl.BlockSpec((1, D), lambda i, j: (0, 0))],
        out_specs=[pl.BlockSpec((tm, FF_WIDE), lambda i, j: (i, j)), pl.BlockSpec((tm, D), lambda i, j: (i, 0)),
                   pl.BlockSpec((8, D), lambda i, j: (0, 0))],
        out_shape=[jax.ShapeDtypeStruct((rows, D_FF), ACT_DTYPE), jax.ShapeDtypeStruct((rows, D), F32),
                   jax.ShapeDtypeStruct((8, D), F32)],
        scratch_shapes=[pltpu.VMEM((tm, D), F32)],
        compiler_params=_cp(("arbitrary", "arbitrary"), VMEM_WIDE_MB),
    )(dh2, a, w1, w2, h1, nfw)


def _ffn_bwd_weights(ft, a, da, dh2t, tm):
    rows = a.shape[0]
    steps = rows // tm
    pair = 2 * FF_TILE

    def body(ft_ref, a_ref, da_ref, dh2t_ref, dw1_ref, dw2_ref, dw2t):
        i = pl.program_id(1)

        @pl.when(i == 0)
        def _():
            dw1_ref[...] = jnp.zeros_like(dw1_ref)
            dw2t[...] = jnp.zeros_like(dw2t)

        z = jnp.square(jnp.maximum(a_ref[...].astype(F32), 0.0))
        dw1 = _mm(ft_ref[...], da_ref[...])
        for core in range(2):
            dw1_ref[core] += dw1[:, core * FF_TILE:(core + 1) * FF_TILE]
        dw2t[...] += _mm(dh2t_ref[...], z)

        @pl.when(i == steps - 1)
        def _():
            for core in range(2):
                dw2_ref[core] = dw2t[:, core * FF_TILE:(core + 1) * FF_TILE].T

    return pl.pallas_call(
        body, name="ffn_bwd_weights", grid=(N_DEV // 2, steps),
        in_specs=[pl.BlockSpec((D, tm), lambda j, i: (0, i)), pl.BlockSpec((tm, pair), lambda j, i: (i, j)),
                  pl.BlockSpec((tm, pair), lambda j, i: (i, j)), pl.BlockSpec((D, tm), lambda j, i: (0, i))],
        out_specs=[pl.BlockSpec((2, None, D, FF_TILE), lambda j, i: (0, j, 0, 0)),
                   pl.BlockSpec((2, None, FF_TILE, D), lambda j, i: (0, j, 0, 0))],
        out_shape=[jax.ShapeDtypeStruct((2, 4, D, FF_TILE), F32), jax.ShapeDtypeStruct((2, 4, FF_TILE, D), F32)],
        scratch_shapes=[pltpu.VMEM((D, pair), F32)],
        compiler_params=_cp(("arbitrary", "arbitrary"), VMEM_WIDE_MB),
    )(ft, a, da, dh2t)


def _out_proj_bwd(dh1, og, osw, wout, tm, partials):
    rows = dh1.shape[0]
    steps = rows // tm
    ns = len(partials)

    def body(dh1_ref, og_ref, os_ref, w_ref, *rest):
        part_refs, rest = rest[:ns], rest[ns:]
        dog_ref, dos_ref, dw_ref = rest[:3]
        land_refs, (send_sems, recv_sems) = rest[3:3 + ns], rest[3 + ns:]
        i = pl.program_id(0)
        start, finish = _sibling_schedule(part_refs, land_refs, send_sems, recv_sems)

        @pl.when(i == 0)
        def _():
            dw_ref[...] = jnp.zeros_like(dw_ref)
            start()

        pl.when(i == steps - 1)(finish)

        dh1 = dh1_ref[...].astype(MXU_DTYPE)
        dog_ref[...] = _mm_nt(dh1, w_ref[0:512, :])
        dos_ref[...] = _mm_nt(dh1, w_ref[512:1024, :])
        for half, ref in enumerate((og_ref, os_ref)):
            dw = _mm_tn(ref[...], dh1)
            for blk in range(4):
                shard = half * 4 + blk
                dw_ref[shard % 2, shard // 2] += dw[blk * 128:(blk + 1) * 128, :]

    row = lambda w: pl.BlockSpec((tm, w), lambda i: (i, 0))
    outs = pl.pallas_call(
        body, name="out_proj_bwd", grid=(steps,),
        in_specs=[row(D), row(512), row(512), pl.BlockSpec((D, D), lambda i: (0, 0))] + [ANY] * ns,
        out_specs=[row(512), row(512), pl.BlockSpec((2, 4, 128, D), lambda i: (0, 0, 0, 0))] + [ANY] * ns,
        out_shape=[jax.ShapeDtypeStruct((rows, 512), F32), jax.ShapeDtypeStruct((rows, 512), F32),
                   jax.ShapeDtypeStruct((2, 4, 128, D), F32)] + _sibling_shapes(partials),
        scratch_shapes=_sibling_sems(ns),
        compiler_params=_cp(("arbitrary",), VMEM_TILE_MB),
    )(dh1, og, osw, wout, *partials)
    return outs[0], outs[1], outs[2], outs[3:]


def _swa_bwd(qr, kr, vr, osw, dos, sinks, jobs):
    rows = qr.shape[0]
    nblk = rows // SWA_BLOCK
    group = _swa_group(nblk)
    steps = nblk // group
    ns = jobs.n

    def body(q_ref, k0, kp, kc, v0, vp, vc, o_ref, do_ref, sink_ref, *rest):
        dq_ref, dk_ref, dv_ref, dsink_ref = rest[ns:ns + 4]
        start, finish = jobs.bind(rest[:ns], rest[ns + 4:2 * ns + 4], rest[2 * ns + 4:])
        step = pl.program_id(0)

        @pl.when(step == 0)
        def _():
            dk_ref[...] = jnp.zeros_like(dk_ref)
            dv_ref[...] = jnp.zeros_like(dv_ref)
            dsink_ref[...] = jnp.zeros_like(dsink_ref)
            start()

        pl.when(step == steps - 1)(finish)
        for g in range(group):
            block(step * group + g, g, q_ref, k0, kp, kc, v0, vp, vc, o_ref, do_ref, sink_ref,
                  dq_ref, dk_ref, dv_ref, dsink_ref)

    def block(n, g, q_ref, k0, kp, kc, v0, vp, vc, o_ref, do_ref, sink_ref, dq_ref, dk_ref, dv_ref, dsink_ref):
        rs = slice(g * SWA_BLOCK, (g + 1) * SWA_BLOCK)
        kall, vall = _swa_keys(k0, kp, kc, g), _swa_keys(v0, vp, vc, g)
        mask = _swa_mask(n)[0:SWA_BLOCK]
        heads = range(SWA_HEADS)
        hs = [slice(h * SWA_HD, (h + 1) * SWA_HD) for h in heads]
        kv = [slice((h // SWA_GROUP) * SWA_HD, (h // SWA_GROUP + 1) * SWA_HD) for h in heads]
        sink = [sink_ref[0, h] for h in heads]
        qh = [q_ref[rs, hs[h]] for h in heads]
        doh = [do_ref[rs, hs[h]] for h in heads]
        s = [jnp.where(mask, _mm_nt(qh[h], kall[:, kv[h]]), NEG) for h in heads]
        dp = [_mm_nt(doh[h], vall[:, kv[h]]) for h in heads]
        delta = [jnp.sum(doh[h] * o_ref[rs, hs[h]].astype(F32), axis=-1, keepdims=True) for h in heads]
        m = [jnp.maximum(jnp.max(s[h], axis=-1, keepdims=True), sink[h]) for h in heads]
        e = [jnp.exp(s[h] - m[h]) for h in heads]
        inv = [1.0 / (jnp.sum(e[h], axis=-1, keepdims=True) + jnp.exp(sink[h] - m[h])) for h in heads]
        p = [e[h] * inv[h] for h in heads]
        ds = [p[h] * (dp[h] - delta[h]) for h in heads]
        dq = [_mm(ds[h], kall[:, kv[h]]) for h in heads]
        dkh = [_mm_tn(ds[h], qh[h]) for h in heads]
        dvh = [_mm_tn(p[h], doh[h]) for h in heads]
        for h in heads:
            dsink = -jnp.sum(jnp.exp(sink[h] - m[h]) * inv[h] * delta[h], axis=0, keepdims=True)
            dsink_ref[h:h + 1, :] += jnp.broadcast_to(dsink, (1, 128))
        dq_ref[rs, :] = jnp.concatenate(dq, axis=1)
        group_sum = lambda parts, kvh: sum(parts[kvh * SWA_GROUP + 1:(kvh + 1) * SWA_GROUP], parts[kvh * SWA_GROUP])
        dk_all = jnp.concatenate([group_sum(dkh, kvh) for kvh in range(SWA_KV)], axis=1)
        dv_all = jnp.concatenate([group_sum(dvh, kvh) for kvh in range(SWA_KV)], axis=1)
        prev0 = pl.multiple_of(jnp.maximum(n - 1, 0) * SWA_BLOCK, SWA_BLOCK)
        cur0 = pl.multiple_of(n * SWA_BLOCK, SWA_BLOCK)
        for ref, val in ((dk_ref, dk_all), (dv_ref, dv_all)):
            ref[0:SWA_BLOCK, :] += val[0:SWA_BLOCK]
            ref[pl.ds(prev0, SWA_BLOCK), :] += val[SWA_BLOCK:2 * SWA_BLOCK]
            ref[pl.ds(cur0, SWA_BLOCK), :] += val[2 * SWA_BLOCK:]

    blk, first, prev = _swa_specs(group)
    whole = pl.BlockSpec((rows, 128), lambda n: (0, 0))
    outs = pl.pallas_call(
        body, name="swa_bwd", grid=(steps,),
        in_specs=[blk(512), first, prev, blk(128), first, prev, blk(128), blk(512), blk(512),
                  pl.BlockSpec(memory_space=pltpu.SMEM)] + [ANY] * ns,
        out_specs=[blk(512), whole, whole, pl.BlockSpec((8, 128), lambda n: (0, 0))] + [ANY] * ns,
        out_shape=[jax.ShapeDtypeStruct((rows, 512), F32), jax.ShapeDtypeStruct((rows, 128), F32),
                   jax.ShapeDtypeStruct((rows, 128), F32), jax.ShapeDtypeStruct((8, 128), F32)] + jobs.out_shapes,
        scratch_shapes=jobs.sems,
        compiler_params=_cp(("arbitrary",), VMEM_TILE_MB),
    )(qr, kr, kr, kr, vr, vr, vr, osw, dos, sinks, *jobs.inputs)
    return outs[0], outs[1], outs[2], outs[3], jobs.split(outs[4:])


def _gla_bwd(proj, decay, dgate, oraw, states, dog, wg_p, gnw, jobs):
    rows = proj.shape[0]
    nc = rows // GLA_CHUNK
    group = _gla_group(nc, 5)
    steps, nrows = nc // group, group * GLA_CHUNK
    ns = jobs.n

    def body(q_ref, k_ref, v_ref, r_ref, lr_ref, b_ref, dgate_ref, oraw_ref, st_ref, dog_ref, wg_ref, gnw_ref, *rest):
        dq_ref, dk_ref, dv_ref, dr_ref, dlr_ref, dwg_ref, dbg_ref, dgnw_ref = rest[ns:ns + 8]
        dstate, db_scr = rest[2 * ns + 8:2 * ns + 10]
        start, finish = jobs.bind(rest[:ns], rest[ns + 8:2 * ns + 8], rest[2 * ns + 10:])
        t = pl.program_id(0)

        @pl.when(t == 0)
        def _():
            dstate[...] = jnp.zeros_like(dstate)
            dwg_ref[...] = jnp.zeros_like(dwg_ref)
            dbg_ref[...] = jnp.zeros_like(dbg_ref)
            dgnw_ref[...] = jnp.zeros_like(dgnw_ref)
            start()

        pl.when(t == steps - 1)(finish)

        lr, wg = lr_ref[...], wg_ref[...]
        b = b_ref[...]
        eb, enb = jnp.exp(b), jnp.exp(-b)
        scale = GLA_DK ** -0.5
        gq = q_ref[...] * scale * eb
        gk = k_ref[...] * enb
        v = v_ref[...]
        gnw_v = gnw_ref[...]
        tril = _tril64()
        is_last = lax.broadcasted_iota(jnp.int32, (GLA_CHUNK, 1), 0) == GLA_CHUNK - 1
        dgnw = jnp.zeros((1, GLA_DV), F32)
        pairs = [(h, gi) for h in range(GLA_HEADS) for gi in range(group)]
        rs = {gi: slice(gi * GLA_CHUNK, (gi + 1) * GLA_CHUNK) for gi in range(group)}
        s64 = {h: slice(h * GLA_DK, (h + 1) * GLA_DK) for h in range(GLA_HEADS)}
        s128 = {h: slice(h * GLA_DV, (h + 1) * GLA_DV) for h in range(GLA_HEADS)}
        qh = {(h, gi): gq[rs[gi], s64[h]] for h, gi in pairs}
        kh = {(h, gi): gk[rs[gi], s64[h]] for h, gi in pairs}
        vh = {(h, gi): v[rs[gi], s128[h]] for h, gi in pairs}
        ebl = {(h, gi): eb[(gi + 1) * GLA_CHUNK - 1:(gi + 1) * GLA_CHUNK, s64[h]] for h, gi in pairs}
        kl = {pr: kh[pr] * ebl[pr] for pr in pairs}
        st = {(h, gi): st_ref[gi, h] for h, gi in pairs}
        do = {}
        for h, gi in pairs:
            o, rh, dout = oraw_ref[rs[gi], s128[h]], r_ref[rs[gi], s128[h]], dog_ref[rs[gi], s128[h]]
            rstd = lax.rsqrt(jnp.mean(o * o, axis=-1, keepdims=True) + EPS)
            on = o * rstd
            sg = _sigmoid(rh)
            dr_ref[rs[gi], s128[h]] = (dout * (on * gnw_v) * (sg * (1.0 + rh * (1.0 - sg)))).astype(ACT_DTYPE)
            dy = dout * (rh * sg)
            dgnw = dgnw + jnp.sum(dy * on, axis=0, keepdims=True)
            don = dy * gnw_v
            do[h, gi] = rstd * (don - on * jnp.mean(don * on, axis=-1, keepdims=True))
        a = {pr: jnp.where(tril, _mm_nt(qh[pr], kh[pr]), 0.0) for pr in pairs}
        da = {pr: jnp.where(tril, _mm_nt(do[pr], vh[pr]), 0.0) for pr in pairs}
        dinc = {pr: _mm_tn(do[pr], qh[pr]) for pr in pairs}
        dgq = {pr: _mm(da[pr], kh[pr]) + _mm(do[pr], st[pr]) for pr in pairs}
        dgk = {pr: _mm_tn(da[pr], qh[pr]) for pr in pairs}
        dv_a = {pr: _mm_tn(a[pr], do[pr]) for pr in pairs}
        dsp = {}
        for h in range(GLA_HEADS):
            cur = dstate[h]
            for gi in reversed(range(group)):
                dsp[h, gi] = cur
                cur = cur * ebl[h, gi] + dinc[h, gi]
            dstate[h] = cur
        for h, gi in pairs:
            pr = (h, gi)
            dkl = _mm(vh[pr], dsp[pr])
            dv_ref[rs[gi], s128[h]] = (dv_a[pr] + _mm_nt(kl[pr], dsp[pr])).astype(ACT_DTYPE)
            debl = jnp.sum(dsp[pr] * st[pr], axis=0, keepdims=True)
            dq_ref[rs[gi], s64[h]] = (dgq[pr] * (scale * eb[rs[gi], s64[h]])).astype(ACT_DTYPE)
            dk_ref[rs[gi], s64[h]] = ((dgk[pr] + dkl * ebl[pr]) * enb[rs[gi], s64[h]]).astype(ACT_DTYPE)
            last = debl * ebl[pr] + jnp.sum(dkl * kl[pr], axis=0, keepdims=True)
            db_scr[rs[gi], s64[h]] = (dgq[pr] * qh[pr] - dgk[pr] * kh[pr] - dkl * kl[pr]
                                      + jnp.where(is_last, last, 0.0))
        dzg = _masked_sums(_chunk_masks(nrows)[1], db_scr[...]) * dgate_ref[...]
        dlr_ref[...] = _mm_nt(dzg, wg).astype(ACT_DTYPE)
        dwg_ref[...] += _mm_tn(lr, dzg)
        dbg_ref[...] += jnp.broadcast_to(jnp.sum(dzg, axis=0, keepdims=True), dbg_ref.shape)
        dgnw_ref[...] += jnp.broadcast_to(dgnw, dgnw_ref.shape)

    nb = lambda w, col: pl.BlockSpec((nrows, w), lambda t: (steps - 1 - t, col // w))
    const = lambda shape: pl.BlockSpec(shape, lambda t: (0,) * len(shape))
    outs = pl.pallas_call(
        body, name="gla_bwd", grid=(steps,),
        in_specs=[nb(256, C_GQ), nb(256, C_GK), nb(512, C_GV), nb(512, C_GR), nb(128, C_LR), nb(256, 0), nb(256, 0),
                  nb(512, 0),
                  pl.BlockSpec((group, GLA_HEADS, GLA_DV, GLA_DK), lambda t: (steps - 1 - t, 0, 0, 0)), nb(512, 0),
                  const((128, 256)), const((1, 128))] + [ANY] * ns,
        out_specs=[nb(256, 0), nb(256, 0), nb(512, 0), nb(512, 0), nb(128, 0),
                   const((128, 256)), const((8, 256)), const((8, 128))] + [ANY] * ns,
        out_shape=[jax.ShapeDtypeStruct((rows, 256), ACT_DTYPE), jax.ShapeDtypeStruct((rows, 256), ACT_DTYPE),
                   jax.ShapeDtypeStruct((rows, 512), ACT_DTYPE), jax.ShapeDtypeStruct((rows, 512), ACT_DTYPE),
                   jax.ShapeDtypeStruct((rows, 128), ACT_DTYPE), jax.ShapeDtypeStruct((128, 256), F32),
                   jax.ShapeDtypeStruct((8, 256), F32), jax.ShapeDtypeStruct((8, 128), F32)] + jobs.out_shapes,
        scratch_shapes=[pltpu.VMEM((GLA_HEADS, GLA_DV, GLA_DK), F32), pltpu.VMEM((nrows, 256), F32)] + jobs.sems,
        compiler_params=_cp(("arbitrary",)),
    )(proj, proj, proj, proj, proj, decay, dgate, oraw, states, dog, wg_p, gnw, *jobs.inputs)
    return outs[:8], jobs.split(outs[8:])


def _in_proj_bwd(x, lead, dh1, nw, win_p, dgv, dgr, dsq, dgq, dgk, dsk, dsv, dlr, tabs, tm):
    seq = x.shape[0]
    rows = LEAD + seq
    nb = tm // LEAD
    steps = rows // tm

    def first_copy(scr, gx_ref, sem):
        return pltpu.make_async_copy(scr.at[pl.ds(LEAD, tm - LEAD)], gx_ref.at[pl.ds(0, tm - LEAD)], sem)

    def tile_copy(scr, gx_ref, sem, step):
        start = pl.multiple_of(jnp.maximum(step * tm - LEAD, 0), LEAD)
        return pltpu.make_async_copy(scr, gx_ref.at[pl.ds(start, tm)], sem)

    def body(*refs):
        x_refs, refs = refs[:nb], refs[nb:]
        (lead_ref, dh1_ref, nw_ref, w_ref, dgv_ref, dgr_ref, dsq_ref, dgq_ref, dgk_ref, dsk_ref, dsv_ref, dlr_ref,
         c_ref, sa_ref, sb_ref, gx_ref, dlead_ref, dproj_ref, ut_ref, gnm_ref, scr, sem) = refs
        i = pl.program_id(0)

        @pl.when(i == 0)
        def _():
            gnm_ref[...] = jnp.zeros_like(gnm_ref)

        cos, sa, sb = c_ref[...], sa_ref[...], sb_ref[...]
        dsq_v = (_unrope(dsq_ref[...], cos, sa, sb) * (SWA_HD ** -0.5)).astype(MXU_DTYPE)
        dsk_v = _unrope(dsk_ref[...], cos, sa, sb).astype(MXU_DTYPE)
        dproj = jnp.concatenate(
            [dgv_ref[...].astype(MXU_DTYPE), dgr_ref[...].astype(MXU_DTYPE), dgq_ref[...].astype(MXU_DTYPE),
             dgk_ref[...].astype(MXU_DTYPE), dlr_ref[...].astype(MXU_DTYPE), dsq_v, dsk_v,
             dsv_ref[...].astype(MXU_DTYPE)],
            axis=1)
        dproj_ref[...] = dproj
        h = _h_tile(i, lead_ref, x_refs)
        rstd = lax.rsqrt(jnp.mean(h * h, axis=-1, keepdims=True) + EPS)
        hn = h * rstd
        nw_v = nw_ref[...]
        ut_ref[...] = (hn * nw_v).T.astype(ACT_DTYPE)
        du = _mm_nt(dproj, w_ref[...])
        gnm_ref[...] += jnp.broadcast_to(jnp.sum(du * hn, axis=0, keepdims=True), gnm_ref.shape)
        dun = du * nw_v
        dh0 = dh1_ref[...] + rstd * (dun - hn * jnp.mean(dun * hn, axis=-1, keepdims=True))

        if tm > LEAD:
            pl.when(i == 1)(lambda: first_copy(scr, gx_ref, sem).wait())
        pl.when(i > 1)(lambda: tile_copy(scr, gx_ref, sem, i).wait())
        scr[...] = dh0

        @pl.when(i == 0)
        def _():
            dlead_ref[...] = dh0[0:LEAD]
            if tm > LEAD:
                first_copy(scr, gx_ref, sem).start()
                if steps == 1:
                    first_copy(scr, gx_ref, sem).wait()

        @pl.when(i > 0)
        def _():
            tile_copy(scr, gx_ref, sem, i).start()

        if steps > 1:
            pl.when(i == steps - 1)(lambda: tile_copy(scr, gx_ref, sem, i).wait())

    row = lambda w: pl.BlockSpec((tm, w), lambda i: (i, 0))
    const = lambda shape: pl.BlockSpec(shape, lambda i: (0,) * len(shape))
    return pl.pallas_call(
        body, name="in_proj_bwd", grid=(steps,),
        in_specs=_token_specs(tm) + [const((LEAD, D)), row(D), const((1, D)), const((D, DINP)),
                                     row(512), row(512), row(512), row(256), row(256), row(128), row(128), row(128),
                                     row(128), row(128), row(128)],
        out_specs=[ANY, const((LEAD, D)), row(DINP), pl.BlockSpec((D, tm), lambda i: (0, i)), const((8, D))],
        out_shape=[jax.ShapeDtypeStruct((seq, D), F32), jax.ShapeDtypeStruct((LEAD, D), F32),
                   jax.ShapeDtypeStruct((rows, DINP), ACT_DTYPE), jax.ShapeDtypeStruct((D, rows), ACT_DTYPE),
                   jax.ShapeDtypeStruct((8, D), F32)],
        scratch_shapes=[pltpu.VMEM((tm, D), F32), pltpu.SemaphoreType.DMA],
        compiler_params=_cp(("arbitrary",), VMEM_WIDE_MB),
    )(*([x] * nb), lead, dh1, nw, win_p, dgv, dgr, dsq, dgq, dgk, dsk, dsv, dlr, *tabs)


def _win_runs():
    groups = [(O_GQ, C_GQ), (O_GK, C_GK), (O_GV, C_GV), (O_GR, C_GR), (O_LR, C_LR), (O_SQ, C_SQ), (O_SK, C_SK),
              (O_SV, C_SV)]
    per = DIN // N_DEV
    runs = []
    for (o0, o1), c0 in groups:
        o = o0
        while o < o1:
            d = o // per
            end = min(o1, (d + 1) * per)
            runs.append((d, o - d * per, c0 + o - o0, end - o))
            o = end
    return runs


def _win_padded(g_in):
    tr = 128

    def body(g_ref, o_ref):
        o_ref[...] = jnp.zeros_like(o_ref)
        for d, s, c, w in _win_runs():
            o_ref[:, c:c + w] = g_ref[d, :, s:s + w]

    return pl.pallas_call(
        body, name="w_in_layout", grid=(D // tr,),
        in_specs=[pl.BlockSpec((N_DEV, tr, DIN // N_DEV), lambda i: (0, i, 0))],
        out_specs=pl.BlockSpec((tr, DINP), lambda i: (i, 0)),
        out_shape=jax.ShapeDtypeStruct((D, DINP), g_in.dtype),
        compiler_params=_cp(("arbitrary",)),
    )(g_in)


def _in_proj_bwd_weights(ut, dproj, tm, small):
    rows = dproj.shape[0]
    steps = rows // tm
    per = DIN // N_DEV

    def body(ut_ref, dp_ref, *rest):
        small_refs, (mine_ref, theirs_ref, total_ref, acc, stage, local_sems, send_sems, recv_sems) = rest[:9], rest[9:17]
        i = pl.program_id(0)
        start, finish = _small_sum_schedule(small_refs, total_ref, *rest[17:])
        x, y, c = _mesh_pos()

        @pl.when(i == 0)
        def _():
            acc[...] = jnp.zeros_like(acc)
            start()

        acc[...] += _mm(ut_ref[...], dp_ref[...])
        pl.when(i == steps - 1)(finish)

        def keep(slot, chip):
            return pltpu.make_async_copy(stage.at[slot], mine_ref.at[chip], local_sems.at[slot])

        def send(slot, chip):
            return pltpu.make_async_remote_copy(
                src_ref=stage.at[slot], dst_ref=theirs_ref.at[chip], send_sem=send_sems.at[slot],
                recv_sem=recv_sems.at[chip], device_id=(x, y, 1 - c), device_id_type=MESH)

        def drained(d):
            pl.when(c == d % 2)(keep(d % 2, d // 2).wait)
            pl.when(c != d % 2)(send(d % 2, d // 2).wait_send)

        @pl.when(i == steps - 1)
        def _():
            for d in range(N_DEV):
                slot, chip = d % 2, d // 2
                if d >= 2:
                    drained(d - 2)
                for owner, s, col, w in _win_runs():
                    if owner == d:
                        stage[slot, :, s:s + w] = acc[:, col:col + w]
                pl.when(c == slot)(keep(slot, chip).start)
                pl.when(c != slot)(send(slot, chip).start)
            drained(N_DEV - 2)
            drained(N_DEV - 1)
            for chip in range(4):
                send(0, chip).wait_recv()

    half = jax.ShapeDtypeStruct((4, D, per), F32)
    return pl.pallas_call(
        body, name="in_proj_bwd_weights", grid=(steps,),
        in_specs=[pl.BlockSpec((D, tm), lambda i: (0, i)), pl.BlockSpec((tm, DINP), lambda i: (i, 0))] + SMALL_SPECS,
        out_specs=[ANY, ANY, pl.BlockSpec((SMALL_ROWS, D), lambda i: (0, 0))],
        out_shape=[half, half, jax.ShapeDtypeStruct((SMALL_ROWS, D), F32)],
        scratch_shapes=[pltpu.VMEM((D, DINP), F32), pltpu.VMEM((2, D, per), F32), pltpu.SemaphoreType.DMA((2,)),
                        pltpu.SemaphoreType.DMA((2,)), pltpu.SemaphoreType.DMA((4,))] + _small_sum_scratch(),
        compiler_params=_cp(("arbitrary",), VMEM_WIDE_MB),
    )(ut, dproj, *small)


def _adamw(w, g, m, v):
    m = ADAM_B1 * m + (1.0 - ADAM_B1) * g
    v = ADAM_B2 * v + (1.0 - ADAM_B2) * jnp.square(g)
    m_hat = m / (1.0 - ADAM_B1 ** ADAM_STEP)
    v_hat = v / (1.0 - ADAM_B2 ** ADAM_STEP)
    delta = -ADAM_LR * (m_hat / (jnp.sqrt(v_hat) + ADAM_EPS) + ADAM_WD * w)
    return delta, m, v


ADAM_STEPS = 8


def _adamw_shards(where, items, name, jobs=None):
    jobs = jobs or _Jobs([])
    ns, nw = jobs.n, len(items)

    def body(where_ref, *rest):
        ins, rest = rest[:5 * nw], rest[5 * nw:]
        job_ins, rest = rest[:ns], rest[ns:]
        outs, rest = rest[:4 * nw], rest[4 * nw:]
        start, finish = jobs.bind(job_ins, rest[:ns], rest[ns:])
        i = pl.program_id(0)
        pl.when(i == 0)(start)
        pl.when(i == ADAM_STEPS - 1)(finish)
        for k in range(nw):
            p_ref, own_ref, w_ref, m_ref, v_ref = ins[5 * k:5 * k + 5]
            g_ref, d_ref, nm_ref, nv_ref = outs[4 * k:4 * k + 4]
            g = ((p_ref[0].astype(F32) + p_ref[1].astype(F32)) + p_ref[2].astype(F32)) + own_ref[...]
            g_ref[...] = g
            d_ref[...], nm_ref[...], nv_ref[...] = _adamw(w_ref[...], g, m_ref[...], v_ref[...])

    in_specs, out_specs, out_shape, operands = [], [], [], []
    for parts, own, w, m, v in items:
        r, cdim = w.shape
        tr = r // ADAM_STEPS
        spec = pl.BlockSpec((tr, cdim), lambda i, s: (i, 0))
        in_specs += [pl.BlockSpec((3, tr, cdim), lambda i, s: (0, i, 0)),
                     pl.BlockSpec((None, tr, cdim), lambda i, s: (s[1], i, 0)), spec, spec, spec]
        out_specs += [spec] * 4
        out_shape += [jax.ShapeDtypeStruct((r, cdim), F32)] * 4
        operands += [parts, own, w, m, v]
    outs = pl.pallas_call(
        body, name=name,
        grid_spec=pltpu.PrefetchScalarGridSpec(
            num_scalar_prefetch=1, grid=(ADAM_STEPS,),
            in_specs=in_specs + [ANY] * ns, out_specs=out_specs + [ANY] * ns, scratch_shapes=jobs.sems),
        out_shape=out_shape + jobs.out_shapes,
        compiler_params=_cp(("arbitrary",)),
    )(where, *operands, *jobs.inputs)
    return [outs[4 * k:4 * k + 4] for k in range(nw)], jobs.split(outs[4 * nw:])


def _adamw_small(items):
    n = len(items)

    def body(*refs):
        ins, outs = refs[:4 * n], refs[4 * n:]
        for k in range(n):
            w_ref, g_ref, m_ref, v_ref = ins[4 * k:4 * k + 4]
            d_ref, nm_ref, nv_ref = outs[3 * k:3 * k + 3]
            d_ref[...], nm_ref[...], nv_ref[...] = _adamw(w_ref[...], g_ref[...], m_ref[...], v_ref[...])

    vm = pl.BlockSpec(memory_space=pltpu.VMEM)
    shapes = [jax.ShapeDtypeStruct(w.shape, F32) for w, _, _, _ in items for _ in range(3)]
    outs = pl.pallas_call(body, name="adamw_small", in_specs=[vm] * (4 * n), out_specs=[vm] * (3 * n),
                          out_shape=shapes)(*[t for item in items for t in item])
    return [outs[3 * k:3 * k + 3] for k in range(n)]


def _add_halves(mine, theirs, name):
    _, r, cdim = mine.shape
    tr = 128 if r % 128 == 0 else r

    def body(a_ref, b_ref, o_ref, w_ref):
        total = a_ref[...] + b_ref[...]
        o_ref[...] = total
        w_ref[...] = total.astype(WIRE_DTYPE)

    spec = pl.BlockSpec((4, tr, cdim), lambda i: (0, i, 0))
    return pl.pallas_call(
        body, name=name, grid=(r // tr,), in_specs=[spec, spec], out_specs=[spec, spec],
        out_shape=[jax.ShapeDtypeStruct(mine.shape, F32), jax.ShapeDtypeStruct(mine.shape, WIRE_DTYPE)],
        compiler_params=_cp(("arbitrary",)))(mine, theirs)


def _add_own_half(where, full, theirs, name, wire_copy=False):
    _, _, r, cdim = full.shape
    tr = 128 if r % 128 == 0 else r

    def body(where_ref, a_ref, b_ref, *o_refs):
        total = a_ref[...] + b_ref[...]
        o_refs[0][...] = total
        if wire_copy:
            o_refs[1][...] = total.astype(WIRE_DTYPE)

    spec = pl.BlockSpec((4, tr, cdim), lambda i, s: (0, i, 0))
    shapes = [jax.ShapeDtypeStruct(theirs.shape, F32)] + ([jax.ShapeDtypeStruct(theirs.shape, WIRE_DTYPE)] if wire_copy else [])
    outs = pl.pallas_call(
        body, name=name,
        grid_spec=pltpu.PrefetchScalarGridSpec(
            num_scalar_prefetch=1, grid=(r // tr,),
            in_specs=[pl.BlockSpec((None, 4, tr, cdim), lambda i, s: (s[0], 0, i, 0)), spec],
            out_specs=[spec] * len(shapes)),
        out_shape=shapes, compiler_params=_cp(("arbitrary",)))(where, full, theirs)
    return outs if wire_copy else outs[0]


def kernel(x, meta_tokens, norm_mix_w, w_in, w_gate_up, b_gate, gla_norm_w, sinks, w_out, norm_ff_w, w_ff1, w_ff2, final_norm_w, loss_target, m_meta_tokens, m_norm_mix_w, m_w_in, m_w_gate_up, m_b_gate, m_gla_norm_w, m_sinks, m_w_out, m_norm_ff_w, m_w_ff1, m_w_ff2, m_final_norm_w, v_meta_tokens, v_norm_mix_w, v_w_in, v_w_gate_up, v_b_gate, v_gla_norm_w, v_sinks, v_w_out, v_norm_ff_w, v_w_ff1, v_w_ff2, v_final_norm_w):
    seq = x.shape[1]
    rows = LEAD + seq
    tm = _row_tile(rows)
    tm_wide = WIDE_ROW_TILE if rows % WIDE_ROW_TILE == 0 else tm
    dev =4 * lax.axis_index("x") + 2 * lax.axis_index("y") + lax.axis_index("c")

    small_shard = jnp.concatenate([meta_tokens, w_gate_up[0], jnp.zeros((N_META, 96), F32)], axis=1)
    g_in, g_small = _all_gather([w_in[0].astype(WIRE_DTYPE), small_shard])
    later_shards = [w_out[0].astype(WIRE_DTYPE), w_ff1[0].astype(WIRE_DTYPE), w_ff2[0].astype(WIRE_DTYPE)]
    win_p = _win_padded(g_in)
    meta_full = jnp.transpose(g_small[:, :, 0:128], (1, 0, 2)).reshape(N_META, D)
    wg_full = jnp.transpose(g_small[:, :, 128:160], (1, 0, 2)).reshape(GLA_RANK, GLA_HEADS * GLA_DK)
    wg_p = jnp.concatenate([wg_full, jnp.zeros((128 - GLA_RANK, 256), F32)], axis=0)

    lead = jnp.concatenate([jnp.zeros((META0, D), F32), meta_full], axis=0)
    tabs = _rope_tables(rows)
    proj, qr, kr, vr, (g_w1,) = _in_proj(x[0], lead, norm_mix_w, win_p, tabs, tm, later_shards[1:2])
    oraw, og, states, decay, dgate, (g_out,) = _gla_fwd(proj, wg_p, b_gate, gla_norm_w, later_shards[0:1])
    osw, (g_w2,) = _swa_fwd(qr, kr, vr, sinks, later_shards[2:3])
    wout_full = g_out.reshape(D, D)
    w2_full = g_w2.reshape(D_FF, D)
    w1_full = jnp.transpose(g_w1, (1, 0, 2)).reshape(D, D_FF)
    h1, f, ft = _out_proj(x[0], lead, og, osw, wout_full, norm_ff_w, tm)
    a, dh2, dh2t, loss_p, gfn_p = _ffn_fwd(f, h1, w1_full, w2_full, loss_target[0], final_norm_w.reshape(1, D), tm)

    da, dh1, gnf_p = _ffn_bwd_act(dh2, a, w1_full, w2_full, h1, norm_ff_w, tm)
    dw1, dw2 = _ffn_bwd_weights(ft, a, da, dh2t, tm_wide)
    where = jnp.stack([lax.axis_index("c"), 2 * lax.axis_index("x") + lax.axis_index("y")]).astype(jnp.int32)
    dog, dos, dwout, theirs_ffn = _out_proj_bwd(dh1, og, osw, wout_full, tm, [dw1, dw2])
    pairs_ffn = [_add_own_half(where, p, q, "reduce_pair_%d" % (2 + k), wire_copy=True)
                 for k, (p, q) in enumerate(zip([dw1, dw2], theirs_ffn))]
    sums_ffn, wires_ffn = [p[0] for p in pairs_ffn], [p[1] for p in pairs_ffn]
    dsq, dsk, dsv, dsink_p, (parts_ffn, (theirs_wout,)) = _swa_bwd(
        qr, kr, vr, osw, dos, sinks, _Jobs([("chips", wires_ffn), ("sibling", [dwout])]))
    sum_wout, wire_wout = _add_own_half(where, dwout, theirs_wout, "reduce_pair_1", wire_copy=True)
    (dgq, dgk, dgv, dgr, dlr, dwg_p, dbg_p, dgnw_p), ((parts_wout,),) = _gla_bwd(
        proj, decay, dgate, oraw, states, dog, wg_p, gla_norm_w, _Jobs([("chips", [wire_wout])]))
    grad_x, dlead, dproj, ut, gnm_p = _in_proj_bwd(x[0], lead, dh1, norm_mix_w, win_p, dgv, dgr, dsq, dgq, dgk, dsk,
                                                   dsv, dlr, tabs, tm)
    grad_x = grad_x[None]
    dwin_mine, dwin_theirs, total = _in_proj_bwd_weights(
        ut, dproj, tm_wide, [dlead, dwg_p, gnm_p, gnf_p, gfn_p, dbg_p, dgnw_p, loss_p, dsink_p])
    sum_win, sum_win_wire = _add_halves(dwin_mine, dwin_theirs, "reduce_pair_0")

    g_meta = lax.dynamic_slice(total, (R_META, dev * 128), (N_META, 128))
    g_wg = lax.dynamic_slice(total, (R_WG, dev * 32), (GLA_RANK, 32))
    g_norm_mix, g_norm_ff = total[R_NORM_MIX:R_NORM_MIX + 1], total[R_NORM_FF:R_NORM_FF + 1]
    g_final_norm = total[R_FINAL:R_FINAL + 1]
    g_b_gate, g_gla_norm = total[R_B_GATE:R_B_GATE + 1, 0:256], total[R_GLA_NORM:R_GLA_NORM + 1, 0:128]
    g_sinks = total[R_SINKS:R_SINKS + SWA_HEADS, 0].reshape(1, SWA_HEADS)
    loss = total[R_LOSS, 0]

    ((g_wout, d_wout, nm_wout, nv_wout), (g_w1s, d_w1, nm_w1, nv_w1), (g_w2s, d_w2, nm_w2, nv_w2)), ((parts_win,),) = \
        _adamw_shards(where, [(parts_wout, sum_wout, w_out[0], m_w_out[0], v_w_out[0]),
                              (parts_ffn[0], sums_ffn[0], w_ff1[0], m_w_ff1[0], v_w_ff1[0]),
                              (parts_ffn[1], sums_ffn[1], w_ff2[0], m_w_ff2[0], v_w_ff2[0])],
                      "adamw_w_out_ff", _Jobs([("chips", [sum_win_wire])]))
    ((g_win, d_win, nm_win, nv_win),), _ = _adamw_shards(
        where, [(parts_win, sum_win, w_in[0], m_w_in[0], v_w_in[0])], "adamw_w_in")

    names = ["meta", "wg", "norm_mix", "b_gate", "gla_norm", "sinks", "norm_ff", "final_norm"]
    ws = [meta_tokens, w_gate_up, norm_mix_w, b_gate, gla_norm_w, sinks, norm_ff_w, final_norm_w]
    gs = [g_meta, g_wg, g_norm_mix, g_b_gate, g_gla_norm, g_sinks, g_norm_ff, g_final_norm]
    ms = [m_meta_tokens, m_w_gate_up, m_norm_mix_w, m_b_gate, m_gla_norm_w, m_sinks, m_norm_ff_w, m_final_norm_w]
    vs = [v_meta_tokens, v_w_gate_up, v_norm_mix_w, v_b_gate, v_gla_norm_w, v_sinks, v_norm_ff_w, v_final_norm_w]
    flat = lambda t: t.reshape(-1, t.shape[-1])
    small_out = _adamw_small([(flat(w), flat(g), flat(m), flat(v)) for w, g, m, v in zip(ws, gs, ms, vs)])
    d_small = {n: small_out[k][0].reshape(ws[k].shape) for k, n in enumerate(names)}
    nm_small = {n: small_out[k][1].reshape(ws[k].shape) for k, n in enumerate(names)}
    nv_small = {n: small_out[k][2].reshape(ws[k].shape) for k, n in enumerate(names)}
    g_small_d = {n: g.reshape(ws[k].shape) for k, (n, g) in enumerate(zip(names, gs))}

    def ordered(big, small_d):
        win_v, wout_v, w1_v, w2_v = big
        return (small_d["meta"], small_d["norm_mix"], win_v[None], small_d["wg"], small_d["b_gate"],
                small_d["gla_norm"], small_d["sinks"], wout_v[None], small_d["norm_ff"], w1_v[None], w2_v[None],
                small_d["final_norm"])

    return (loss, grad_x,
            *ordered((g_win, g_wout, g_w1s, g_w2s), g_small_d),
            *ordered((d_win, d_wout, d_w1, d_w2), d_small),
            *ordered((nm_win, nm_wout, nm_w1, nm_w2), nm_small),
            *ordered((nv_win, nv_wout, nv_w1, nv_w2), nv_small))
```

```python
import functools

import jax
import jax.numpy as jnp
from jax import lax
from jax.experimental import pallas as pl
from jax.experimental.pallas import tpu as pltpu

F32 = jnp.float32
MXU_DTYPE = jnp.bfloat16
ACT_DTYPE = jnp.bfloat16
WIRE_DTYPE = jnp.bfloat16

D = 1024
N_META = 16
LEAD = 128
META0 = LEAD - N_META
EPS = 1e-5
GLA_HEADS, GLA_DK, GLA_DV, GLA_RANK, GLA_CHUNK = 4, 64, 128, 16, 64
GLA_TAU = 16.0
SWA_HEADS, SWA_KV, SWA_GROUP, SWA_HD, SWA_BLOCK = 8, 2, 4, 64, 128
ROPE_DIM, ROPE_THETA = 16, 500000.0
D_FF = 4096
N_DEV = 8
FF_TILE = D_FF // N_DEV
FF_WIDE = 2048
NEG = -1e30

C_GV, C_GR, C_GQ, C_GK, C_LR, C_SQ, C_SK, C_SV = 0, 512, 1024, 1280, 1536, 1664, 2176, 2304
DGLA = 1664
DINP = 2432
DIN = 2320
O_GQ, O_GK, O_GV, O_GR, O_LR, O_SQ, O_SK, O_SV = (0, 256), (256, 512), (512, 1024), (1024, 1536), (1536, 1552), (1552, 2064), (2064, 2192), (2192, 2320)

ADAM_LR, ADAM_B1, ADAM_B2, ADAM_EPS, ADAM_WD, ADAM_STEP = 0.001, 0.9, 0.999, 1e-08, 0.01, 10

MESH = pl.DeviceIdType.MESH
ANY = pl.BlockSpec(memory_space=pl.ANY)
VMEM_TILE_MB, VMEM_WIDE_MB = 48, 56


def _cp(sem=None, vmem_mb=None):
    kw = {}
    if sem is not None:
        kw["dimension_semantics"] = sem
    if vmem_mb is not None:
        kw["vmem_limit_bytes"] = vmem_mb << 20
    return pltpu.CompilerParams(**kw)


def _mm(a, b):
    return jnp.dot(a.astype(MXU_DTYPE), b.astype(MXU_DTYPE), preferred_element_type=F32)


def _mm_nt(a, b):
    return lax.dot_general(a.astype(MXU_DTYPE), b.astype(MXU_DTYPE), (((1,), (1,)), ((), ())),
                           preferred_element_type=F32)


def _mm_tn(a, b):
    return lax.dot_general(a.astype(MXU_DTYPE), b.astype(MXU_DTYPE), (((0,), (0,)), ((), ())),
                           preferred_element_type=F32)


def _masked_sums(mask, t):
    m = mask.astype(jnp.bfloat16)
    hi = t.astype(jnp.bfloat16)
    rest = t - hi.astype(F32)
    mid = rest.astype(jnp.bfloat16)
    low = (rest - mid.astype(F32)).astype(jnp.bfloat16)
    dot = lambda part: jnp.dot(m, part, preferred_element_type=F32)
    return dot(hi) + (dot(mid) + dot(low))


def _logsigmoid(z):
    return jnp.minimum(z, 0.0) - jnp.log(1.0 + jnp.exp(-jnp.abs(z)))


def _sigmoid(z):
    return 1.0 / (1.0 + jnp.exp(-z))


ROW_TILE, WIDE_ROW_TILE = 640, 1664


def _row_tile(rows, want=ROW_TILE):
    return want if rows % want == 0 else LEAD


def _mesh_pos():
    return lax.axis_index("x"), lax.axis_index("y"), lax.axis_index("c")


def _all_gather(shards):
    n = len(shards)

    def body(*refs):
        start, forward, finish = _gather_schedule(refs[:n], refs[n:2 * n], *refs[2 * n:])
        start()
        for j in range(3):
            forward(j)
        finish()

    gathered = pl.pallas_call(
        body, name="all_gather_weights",
        out_shape=_gathered_shapes(shards), in_specs=[ANY] * n, out_specs=[ANY] * n,
        scratch_shapes=_gather_sems(n),
    )(*shards)
    return _with_own_block(gathered, shards)


def _gathered_shapes(shards):
    return [jax.ShapeDtypeStruct((N_DEV,) + s.shape, s.dtype) for s in shards]


def _gather_sems(n):
    return [pltpu.SemaphoreType.DMA((7 * n,)), pltpu.SemaphoreType.DMA((7 * n,))] if n else []


def _place_gather(step, steps, shard_refs, gathered_refs, sems):
    if not shard_refs:
        return
    start, forward, finish = _gather_schedule(shard_refs, gathered_refs, *sems)
    pl.when(step == 0)(start)
    for j, at in enumerate((steps * 7 // 10, steps * 8 // 10, steps * 9 // 10)):
        pl.when(step == at)(functools.partial(forward, j))
    pl.when(step == steps - 1)(finish)


def _with_own_block(gathered, shards):
    dev = 4 * lax.axis_index("x") + 2 * lax.axis_index("y") + lax.axis_index("c")
    return [lax.dynamic_update_index_in_dim(g, s, dev, 0) for g, s in zip(gathered, shards)]


def _gather_schedule(ins, outs, send_sems, recv_sems):
    n = len(ins)
    x, y, c = _mesh_pos()
    me, sibling = (x, y, c), (x, y, 1 - c)
    chips = [(1 - x, y), (x, 1 - y), (1 - x, 1 - y)]

    def copy(a, k, block, to, src=None):
        dst = outs[a].at[4 * block[0] + 2 * block[1] + block[2]]
        return pltpu.make_async_remote_copy(
            src_ref=dst if src is None else src, dst_ref=dst,
            send_sem=send_sems.at[a * 7 + k], recv_sem=recv_sems.at[a * 7 + k],
            device_id=to, device_id_type=MESH)

    def first(a):
        return [copy(a, 0, me, sibling, src=ins[a])] + [copy(a, 1 + j, me, (*chip, c), src=ins[a])
                                                        for j, chip in enumerate(chips)]

    def start():
        for a in range(n):
            for cp in first(a):
                cp.start()

    def forward(j):
        for a in range(n):
            copy(a, 1 + j, (*chips[j], c), me).wait_recv()
            copy(a, 4 + j, (*chips[j], c), sibling).start()

    def finish():
        for a in range(n):
            copy(a, 0, sibling, me).wait_recv()
            for j, chip in enumerate(chips):
                copy(a, 4 + j, (*chip, 1 - c), me).wait_recv()
        for a in range(n):
            for cp in first(a) + [copy(a, 4 + j, (*chip, c), sibling) for j, chip in enumerate(chips)]:
                cp.wait_send()

    return start, forward, finish


def _sibling_shapes(gs):
    return [jax.ShapeDtypeStruct(g.shape[1:], g.dtype) for g in gs]


def _sibling_sems(n):
    return [pltpu.SemaphoreType.DMA((n,)), pltpu.SemaphoreType.DMA((n,))]


def _sibling_schedule(ins, land, send_sems, recv_sems):
    x, y, c = _mesh_pos()

    def copies():
        return [pltpu.make_async_remote_copy(
            src_ref=ins[a].at[1 - c], dst_ref=land[a], send_sem=send_sems.at[a], recv_sem=recv_sems.at[a],
            device_id=(x, y, 1 - c), device_id_type=MESH) for a in range(len(ins))]

    def start():
        for cp in copies():
            cp.start()

    def finish():
        for cp in copies():
            cp.wait_recv()
        for cp in copies():
            cp.wait_send()

    return start, finish


def _chips_shapes(ps):
    return [jax.ShapeDtypeStruct((3,) + p.shape[1:], p.dtype) for p in ps]


def _chips_sems(n):
    return [pltpu.SemaphoreType.DMA((3 * n,)), pltpu.SemaphoreType.DMA((3 * n,))]


def _chips_schedule(ins, land, send_sems, recv_sems):
    x, y, c = _mesh_pos()
    chips = [(1 - x, y), (x, 1 - y), (1 - x, 1 - y)]

    def copies():
        return [pltpu.make_async_remote_copy(
            src_ref=ins[a].at[2 * chip[0] + chip[1]], dst_ref=land[a].at[j],
            send_sem=send_sems.at[3 * a + j], recv_sem=recv_sems.at[3 * a + j],
            device_id=(*chip, c), device_id_type=MESH) for a in range(len(ins)) for j, chip in enumerate(chips)]

    def start():
        for cp in copies():
            cp.start()

    def finish():
        for cp in copies():
            cp.wait_recv()
        for cp in copies():
            cp.wait_send()

    return start, finish


class _Jobs:
    def __init__(self, jobs):
        self.jobs = jobs
        self.inputs = [a for _, arrs in jobs for a in arrs]
        self.out_shapes = [s for kind, arrs in jobs
                           for s in (_sibling_shapes(arrs) if kind == "sibling" else _chips_shapes(arrs))]
        self.sems = [s for kind, arrs in jobs
                     for s in (_sibling_sems(len(arrs)) if kind == "sibling" else _chips_sems(len(arrs)))]
        self.n = len(self.inputs)

    def bind(self, in_refs, out_refs, sem_refs):
        starts, finishes, at = [], [], 0
        for k, (kind, arrs) in enumerate(self.jobs):
            schedule = _sibling_schedule if kind == "sibling" else _chips_schedule
            start, finish = schedule(in_refs[at:at + len(arrs)], out_refs[at:at + len(arrs)],
                                     sem_refs[2 * k], sem_refs[2 * k + 1])
            starts.append(start)
            finishes.append(finish)
            at += len(arrs)

        def start_all():
            for f in starts:
                f()

        def finish_all():
            for f in finishes:
                f()

        return start_all, finish_all

    def split(self, outs):
        res, at = [], 0
        for _, arrs in self.jobs:
            res.append(list(outs[at:at + len(arrs)]))
            at += len(arrs)
        return res


R_META, R_WG, R_NORM_MIX, R_NORM_FF, R_FINAL, R_B_GATE, R_GLA_NORM, R_LOSS, R_SINKS, SMALL_ROWS = 0, 16, 32, 33, 34, 35, 36, 37, 40, 48


SMALL_SPECS = [pl.BlockSpec((LEAD, D), lambda i: (0, 0)), pl.BlockSpec((128, 256), lambda i: (0, 0)),
               pl.BlockSpec((8, D), lambda i: (0, 0)), pl.BlockSpec((8, D), lambda i: (0, 0)),
               pl.BlockSpec((8, D), lambda i: (0, 0)), pl.BlockSpec((8, 256), lambda i: (0, 0)),
               pl.BlockSpec((8, 128), lambda i: (0, 0)), pl.BlockSpec((8, 128), lambda i: (0, 0)),
               pl.BlockSpec((8, 128), lambda i: (0, 0))]


def _small_sum_scratch():
    return [pltpu.VMEM((SMALL_ROWS, D), F32), pltpu.VMEM((N_DEV, SMALL_ROWS, D), F32),
            pltpu.SemaphoreType.DMA((7,)), pltpu.SemaphoreType.DMA((7,))]


def _small_sum_schedule(small_refs, out_ref, p_ref, land, send_sems, recv_sems):
    dlead_ref, dwg_ref, gnm_ref, gnf_ref, gfn_ref, dbg_ref, dgnw_ref, loss_ref, dsink_ref = small_refs
    x, y, c = _mesh_pos()
    me = 4 * x + 2 * y + c

    def copies():
        res = []
        for k in range(1, N_DEV):
            bx, by, bc = (k >> 2) & 1, (k >> 1) & 1, k & 1
            peer = (1 - x if bx else x, 1 - y if by else y, 1 - c if bc else c)
            res.append(pltpu.make_async_remote_copy(
                src_ref=p_ref, dst_ref=land.at[me], send_sem=send_sems.at[k - 1], recv_sem=recv_sems.at[k - 1],
                device_id=peer, device_id_type=MESH))
        return res

    def start():
        p_ref[...] = jnp.zeros_like(p_ref)
        p_ref[R_META:R_META + N_META, :] = dlead_ref[META0:LEAD, :]
        p_ref[R_WG:R_WG + GLA_RANK, 0:256] = dwg_ref[0:GLA_RANK, :]
        p_ref[R_NORM_MIX:R_NORM_MIX + 1, :] = gnm_ref[0:1, :]
        p_ref[R_NORM_FF:R_NORM_FF + 1, :] = gnf_ref[0:1, :]
        p_ref[R_FINAL:R_FINAL + 1, :] = gfn_ref[0:1, :]
        p_ref[R_B_GATE:R_B_GATE + 1, 0:256] = dbg_ref[0:1, :]
        p_ref[R_GLA_NORM:R_GLA_NORM + 1, 0:128] = dgnw_ref[0:1, :]
        p_ref[R_LOSS:R_LOSS + 1, 0:128] = loss_ref[0:1, :]
        p_ref[R_SINKS:R_SINKS + SWA_HEADS, 0:128] = dsink_ref[...]
        land[me] = p_ref[...]
        for cp in copies():
            cp.start()

    def finish():
        for cp in copies():
            cp.wait_recv()
        for cp in copies():
            cp.wait_send()
        acc = land[0]
        for d in range(1, N_DEV):
            acc = acc + land[d]
        out_ref[...] = acc

    return start, finish


def _token_specs(tm, grid_rank=1):
    nb = tm // LEAD

    def spec(k):
        if grid_rank == 1:
            return pl.BlockSpec((LEAD, D), lambda i: (jnp.maximum(i * nb + k - 1, 0), 0))
        return pl.BlockSpec((LEAD, D), lambda i, j: (jnp.maximum(i * nb + k - 1, 0), 0))

    return [spec(k) for k in range(nb)]


def _h_tile(i, lead_ref, x_refs):
    first = jnp.where(i == 0, lead_ref[...], x_refs[0][...])
    return jnp.concatenate([first] + [r[...] for r in x_refs[1:]], axis=0)


def _in_proj(x, lead, nw, win_p, tabs, tm, shards):
    rows = LEAD + x.shape[0]
    nb = tm // LEAD
    steps = rows // tm
    ns = len(shards)

    def body(*refs):
        x_refs, refs = refs[:nb], refs[nb:]
        lead_ref, nw_ref, w_ref, c_ref, sa_ref, sb_ref = refs[:6]
        shard_refs, (o_ref, q_ref, k_ref, v_ref) = refs[6:6 + ns], refs[6 + ns:10 + ns]
        _place_gather(pl.program_id(0), steps, shard_refs, refs[10 + ns:10 + 2 * ns], refs[10 + 2 * ns:])
        h = _h_tile(pl.program_id(0), lead_ref, x_refs)
        rstd = lax.rsqrt(jnp.mean(h * h, axis=-1, keepdims=True) + EPS)
        u = (h * rstd * nw_ref[...]).astype(MXU_DTYPE)
        proj = jnp.dot(u, w_ref[...].astype(MXU_DTYPE), preferred_element_type=F32)
        o_ref[...] = proj[:, 0:DGLA]
        cos, sa, sb = c_ref[...], sa_ref[...], sb_ref[...]
        q_ref[...] = (_rope(proj[:, C_SQ:C_SK], cos, sa, sb) * (SWA_HD ** -0.5)).astype(ACT_DTYPE)
        k_ref[...] = _rope(proj[:, C_SK:C_SV], cos, sa, sb).astype(ACT_DTYPE)
        v_ref[...] = proj[:, C_SV:DINP].astype(ACT_DTYPE)

    row = lambda w: pl.BlockSpec((tm, w), lambda i: (i, 0))
    outs = pl.pallas_call(
        body, name="in_proj", grid=(steps,),
        in_specs=_token_specs(tm) + [pl.BlockSpec((LEAD, D), lambda i: (0, 0)), pl.BlockSpec((1, D), lambda i: (0, 0)),
                                     pl.BlockSpec((D, DINP), lambda i: (0, 0)), row(128), row(128), row(128)]
        + [ANY] * ns,
        out_specs=[row(DGLA), row(512), row(128), row(128)] + [ANY] * ns,
        out_shape=[jax.ShapeDtypeStruct((rows, DGLA), F32), jax.ShapeDtypeStruct((rows, 512), ACT_DTYPE),
                   jax.ShapeDtypeStruct((rows, 128), ACT_DTYPE), jax.ShapeDtypeStruct((rows, 128), ACT_DTYPE)]
        + _gathered_shapes(shards),
        scratch_shapes=_gather_sems(ns),
        compiler_params=_cp(("arbitrary",), VMEM_WIDE_MB),
    )(*([x] * nb), lead, nw, win_p, *tabs, *shards)
    return outs[0], outs[1], outs[2], outs[3], _with_own_block(outs[4:], shards)


def _rope_tables(rows):
    pos = (jnp.arange(rows, dtype=jnp.int32) - META0).astype(F32)
    inv_freq = 1.0 / (ROPE_THETA ** (jnp.arange(0, ROPE_DIM, 2, dtype=F32) / ROPE_DIM))
    ang = pos[:, None] * jnp.tile(inv_freq, 128 // (ROPE_DIM // 2))[None, :]
    in_head = jnp.arange(128, dtype=jnp.int32)[None, :] % SWA_HD
    cos, sin = jnp.cos(ang), jnp.sin(ang)
    c_tab = jnp.where(in_head < ROPE_DIM, cos, 1.0)
    sa_tab = jnp.where(in_head < ROPE_DIM // 2, -sin, 0.0)
    sb_tab = jnp.where((in_head >= ROPE_DIM // 2) & (in_head < ROPE_DIM), sin, 0.0)
    return c_tab, sa_tab, sb_tab


def _rope(xv, cos, sa, sb):
    width = xv.shape[1]
    reps = width // 128
    if reps > 1:
        cos, sa, sb = (jnp.tile(t, (1, reps)) for t in (cos, sa, sb))
    return xv * cos + pltpu.roll(xv, width - 8, 1) * sa + pltpu.roll(xv, 8, 1) * sb


def _unrope(dy, cos, sa, sb):
    width = dy.shape[1]
    reps = width // 128
    if reps > 1:
        cos, sa, sb = (jnp.tile(t, (1, reps)) for t in (cos, sa, sb))
    return dy * cos + pltpu.roll(dy * sa, 8, 1) + pltpu.roll(dy * sb, width - 8, 1)


def _gla_group(nc, most):
    for g in (10, 5, 2):
        if g <= most and nc % g == 0:
            return g
    return 1


def _chunk_masks(nrows):
    ii = lax.broadcasted_iota(jnp.int32, (nrows, nrows), 0)
    jj = lax.broadcasted_iota(jnp.int32, (nrows, nrows), 1)
    same = (ii // GLA_CHUNK) == (jj // GLA_CHUNK)
    return same & (jj <= ii), same & (jj >= ii)


def _gla_gates(lr, wg, bg, first_row):
    nrows = lr.shape[0]
    zg = _mm(lr, wg) + bg
    live = first_row + lax.broadcasted_iota(jnp.int32, (nrows, 1), 0) >= META0
    g = jnp.where(live, _logsigmoid(zg) * (1.0 / GLA_TAU), 0.0)
    return _masked_sums(_chunk_masks(nrows)[0], g), jnp.where(live, _sigmoid(-zg) * (1.0 / GLA_TAU), 0.0)


def _tril64():
    ii = lax.broadcasted_iota(jnp.int32, (GLA_CHUNK, GLA_CHUNK), 0)
    jj = lax.broadcasted_iota(jnp.int32, (GLA_CHUNK, GLA_CHUNK), 1)
    return jj <= ii


def _gla_fwd(proj, wg_p, bg, gnw, shards):
    rows = proj.shape[0]
    nc = rows // GLA_CHUNK
    group = _gla_group(nc, 10)
    steps, nrows = nc // group, group * GLA_CHUNK
    ns = len(shards)

    def body(q_ref, k_ref, v_ref, r_ref, lr_ref, lr_next_ref, wg_ref, bg_ref, gnw_ref, *rest):
        shard_refs, rest = rest[:ns], rest[ns:]
        oraw_ref, og_ref, st_ref, decay_ref, dgate_ref = rest[:5]
        gathered_refs, rest = rest[5:5 + ns], rest[5 + ns:]
        state, gates = rest[:2]
        c = pl.program_id(0)

        @pl.when(c == 0)
        def _():
            state[...] = jnp.zeros_like(state)
            gates[0, 0], gates[0, 1] = _gla_gates(lr_ref[...], wg_ref[...], bg_ref[...], 0)

        _place_gather(c, steps, shard_refs, gathered_refs, rest[2:])
        slot = c % 2
        b = gates[slot, 0]
        decay_ref[...] = b
        dgate_ref[...] = gates[slot, 1]
        gates[1 - slot, 0], gates[1 - slot, 1] = _gla_gates(lr_next_ref[...], wg_ref[...], bg_ref[...], (c + 1) * nrows)
        eb = jnp.exp(b)
        gq = q_ref[...] * (GLA_DK ** -0.5) * eb
        gk = k_ref[...] * jnp.exp(-b)
        v = v_ref[...]
        gnw_v = gnw_ref[...]
        tril = _tril64()
        pairs = [(h, gi) for h in range(GLA_HEADS) for gi in range(group)]
        rs = {gi: slice(gi * GLA_CHUNK, (gi + 1) * GLA_CHUNK) for gi in range(group)}
        s64 = {h: slice(h * GLA_DK, (h + 1) * GLA_DK) for h in range(GLA_HEADS)}
        s128 = {h: slice(h * GLA_DV, (h + 1) * GLA_DV) for h in range(GLA_HEADS)}
        qh = {(h, gi): gq[rs[gi], s64[h]] for h, gi in pairs}
        kh = {(h, gi): gk[rs[gi], s64[h]] for h, gi in pairs}
        vh = {(h, gi): v[rs[gi], s128[h]] for h, gi in pairs}
        ebl = {(h, gi): eb[(gi + 1) * GLA_CHUNK - 1:(gi + 1) * GLA_CHUNK, s64[h]] for h, gi in pairs}
        av = {pr: _mm(jnp.where(tril, _mm_nt(qh[pr], kh[pr]), 0.0), vh[pr]) for pr in pairs}
        inc = {pr: _mm_tn(vh[pr], kh[pr] * ebl[pr]) for pr in pairs}
        st = {}
        for h in range(GLA_HEADS):
            cur = state[h]
            for gi in range(group):
                st[h, gi] = cur
                st_ref[gi, h] = cur
                cur = cur * ebl[h, gi] + inc[h, gi]
            state[h] = cur
        for h, gi in pairs:
            o = av[h, gi] + _mm_nt(qh[h, gi], st[h, gi])
            oraw_ref[rs[gi], s128[h]] = o
            rstd = lax.rsqrt(jnp.mean(o * o, axis=-1, keepdims=True) + EPS)
            rh = r_ref[rs[gi], s128[h]]
            og_ref[rs[gi], s128[h]] = (o * rstd * gnw_v * (rh * _sigmoid(rh))).astype(ACT_DTYPE)

    nb = lambda w, col: pl.BlockSpec((nrows, w), lambda c: (c, col // w))
    const = lambda shape: pl.BlockSpec(shape, lambda c: (0,) * len(shape))
    outs = pl.pallas_call(
        body, name="gla_fwd", grid=(steps,),
        in_specs=[nb(256, C_GQ), nb(256, C_GK), nb(512, C_GV), nb(512, C_GR), nb(128, C_LR),
                  pl.BlockSpec((nrows, 128), lambda c: (jnp.minimum(c + 1, steps - 1), C_LR // 128)),
                  const((128, 256)), const((1, 256)), const((1, 128))] + [ANY] * ns,
        out_specs=[nb(512, 0), nb(512, 0),
                   pl.BlockSpec((group, GLA_HEADS, GLA_DV, GLA_DK), lambda c: (c, 0, 0, 0)),
                   nb(256, 0), nb(256, 0)] + [ANY] * ns,
        out_shape=[jax.ShapeDtypeStruct((rows, 512), F32), jax.ShapeDtypeStruct((rows, 512), ACT_DTYPE),
                   jax.ShapeDtypeStruct((nc, GLA_HEADS, GLA_DV, GLA_DK), F32),
                   jax.ShapeDtypeStruct((rows, 256), F32), jax.ShapeDtypeStruct((rows, 256), F32)]
        + _gathered_shapes(shards),
        scratch_shapes=[pltpu.VMEM((GLA_HEADS, GLA_DV, GLA_DK), F32), pltpu.VMEM((2, 2, nrows, 256), F32)]
        + _gather_sems(ns),
        compiler_params=_cp(("arbitrary",)),
    )(proj, proj, proj, proj, proj, proj, wg_p, bg, gnw, *shards)
    return outs[0], outs[1], outs[2], outs[3], outs[4], _with_own_block(outs[5:], shards)


def _swa_mask(n):
    shape = (SWA_GROUP * SWA_BLOCK, 3 * SWA_BLOCK)
    qi = lax.broadcasted_iota(jnp.int32, shape, 0) & (SWA_BLOCK - 1)
    jj = lax.broadcasted_iota(jnp.int32, shape, 1)
    meta = (jj < SWA_BLOCK) & (jj >= META0) & ((n > 0) | (jj <= qi))
    prev = (jj >= SWA_BLOCK) & (jj < 2 * SWA_BLOCK) & (n >= 2) & (jj - SWA_BLOCK > qi)
    cur = (jj >= 2 * SWA_BLOCK) & (n >= 1) & (jj - 2 * SWA_BLOCK <= qi)
    return meta | prev | cur


def _stack_heads(t, kvh):
    return jnp.concatenate([t[:, (kvh * SWA_GROUP + g) * SWA_HD:(kvh * SWA_GROUP + g + 1) * SWA_HD]
                            for g in range(SWA_GROUP)], axis=0)


def _stack_sinks(sink_ref, kvh):
    return jnp.concatenate([jnp.full((SWA_BLOCK, 1), sink_ref[0, kvh * SWA_GROUP + g], F32)
                            for g in range(SWA_GROUP)], axis=0)


def _swa_group(nblk):
    return 5 if nblk % 5 == 0 else 1


def _swa_specs(group):
    blk = lambda w: pl.BlockSpec((group * SWA_BLOCK, w), lambda n: (n, 0))
    first = pl.BlockSpec((SWA_BLOCK, 128), lambda n: (0, 0))
    prev = pl.BlockSpec((SWA_BLOCK, 128), lambda n: (jnp.maximum(n * group - 1, 0), 0))
    return blk, first, prev


def _swa_keys(first_ref, prev_ref, cur_ref, g):
    own = cur_ref[g * SWA_BLOCK:(g + 1) * SWA_BLOCK, :]
    before = prev_ref[...] if g == 0 else cur_ref[(g - 1) * SWA_BLOCK:g * SWA_BLOCK, :]
    return jnp.concatenate([first_ref[...], before, own], axis=0)


def _swa_fwd(qr, kr, vr, sinks, shards):
    rows = qr.shape[0]
    nblk = rows // SWA_BLOCK
    group = _swa_group(nblk)
    steps = nblk // group
    ns = len(shards)

    def body(q_ref, k0, kp, kc, v0, vp, vc, sink_ref, *rest):
        o_ref = rest[ns]
        _place_gather(pl.program_id(0), steps, rest[:ns], rest[ns + 1:2 * ns + 1], rest[2 * ns + 1:])
        for g in range(group):
            n = pl.program_id(0) * group + g
            rs = slice(g * SWA_BLOCK, (g + 1) * SWA_BLOCK)
            kall, vall = _swa_keys(k0, kp, kc, g), _swa_keys(v0, vp, vc, g)
            mask = _swa_mask(n)[0:SWA_BLOCK]
            heads = range(SWA_HEADS)
            hs = [slice(h * SWA_HD, (h + 1) * SWA_HD) for h in heads]
            kv = [slice((h // SWA_GROUP) * SWA_HD, (h // SWA_GROUP + 1) * SWA_HD) for h in heads]
            s = [jnp.where(mask, _mm_nt(q_ref[rs, hs[h]], kall[:, kv[h]]), NEG) for h in heads]
            m = [jnp.maximum(jnp.max(s[h], axis=-1, keepdims=True), sink_ref[0, h]) for h in heads]
            p = [jnp.exp(s[h] - m[h]) for h in heads]
            den = [jnp.sum(p[h], axis=-1, keepdims=True) + jnp.exp(sink_ref[0, h] - m[h]) for h in heads]
            o = [_mm(p[h], vall[:, kv[h]]) for h in heads]
            for h in heads:
                o_ref[rs, hs[h]] = (o[h] / den[h]).astype(ACT_DTYPE)

    blk, first, prev = _swa_specs(group)
    outs = pl.pallas_call(
        body, name="swa_fwd", grid=(steps,),
        in_specs=[blk(512), first, prev, blk(128), first, prev, blk(128),
                  pl.BlockSpec(memory_space=pltpu.SMEM)] + [ANY] * ns,
        out_specs=[blk(512)] + [ANY] * ns,
        out_shape=[jax.ShapeDtypeStruct((rows, 512), ACT_DTYPE)] + _gathered_shapes(shards),
        scratch_shapes=_gather_sems(ns),
        compiler_params=_cp(("arbitrary",)),
    )(qr, kr, kr, kr, vr, vr, vr, sinks, *shards)
    return outs[0], _with_own_block(outs[1:], shards)


def _out_proj(x, lead, og, osw, wout, nfw, tm):
    rows = LEAD + x.shape[0]
    nb = tm // LEAD

    def body(*refs):
        x_refs, (lead_ref, og_ref, os_ref, w_ref, nw_ref, h1_ref, f_ref, ft_ref) = refs[:nb], refs[nb:]
        h0 = _h_tile(pl.program_id(0), lead_ref, x_refs)
        h1 = h0 + _mm(og_ref[...], w_ref[0:512, :]) + _mm(os_ref[...], w_ref[512:1024, :])
        h1_ref[...] = h1
        rstd = lax.rsqrt(jnp.mean(h1 * h1, axis=-1, keepdims=True) + EPS)
        f = h1 * rstd * nw_ref[...]
        f_ref[...] = f.astype(ACT_DTYPE)
        ft_ref[...] = f.T.astype(ACT_DTYPE)

    row = lambda w: pl.BlockSpec((tm, w), lambda i: (i, 0))
    return pl.pallas_call(
        body, name="out_proj", grid=(rows // tm,),
        in_specs=_token_specs(tm) + [pl.BlockSpec((LEAD, D), lambda i: (0, 0)), row(512), row(512),
                                     pl.BlockSpec((D, D), lambda i: (0, 0)), pl.BlockSpec((1, D), lambda i: (0, 0))],
        out_specs=[row(D), row(D), pl.BlockSpec((D, tm), lambda i: (0, i))],
        out_shape=[jax.ShapeDtypeStruct((rows, D), F32), jax.ShapeDtypeStruct((rows, D), ACT_DTYPE),
                   jax.ShapeDtypeStruct((D, rows), ACT_DTYPE)],
        compiler_params=_cp(("arbitrary",), VMEM_TILE_MB),
    )(*([x] * nb), lead, og, osw, wout, nfw)


def _ffn_fwd(f, h1, w1, w2, tgt, fnw, tm):
    rows = f.shape[0]
    nj = D_FF // FF_WIDE
    nb = tm // LEAD

    def body(f_ref, h1_ref, w1_ref, w2_ref, nw_ref, *rest):
        t_refs, (a_ref, dh2_ref, dh2t_ref, loss_ref, gfn_ref, acc) = rest[:nb], rest[nb:]
        i, j = pl.program_id(0), pl.program_id(1)

        @pl.when((i == 0) & (j == 0))
        def _():
            loss_ref[...] = jnp.zeros_like(loss_ref)
            gfn_ref[...] = jnp.zeros_like(gfn_ref)

        @pl.when(j == 0)
        def _():
            acc[...] = jnp.zeros_like(acc)

        a = _mm(f_ref[...], w1_ref[...])
        a_ref[...] = a.astype(ACT_DTYPE)
        z = jnp.square(jnp.maximum(a, 0.0))
        acc[...] += _mm(z, w2_ref[...])

        @pl.when(j == nj - 1)
        def _():
            h2 = h1_ref[...] + acc[...]
            rstd = lax.rsqrt(jnp.mean(h2 * h2, axis=-1, keepdims=True) + EPS)
            hn = h2 * rstd
            nw = nw_ref[...]
            row = i * tm + lax.broadcasted_iota(jnp.int32, (tm, 1), 0)
            target = jnp.concatenate([t[...] for t in t_refs], axis=0)
            err = jnp.where(row >= LEAD, hn * nw - target, 0.0)
            row_loss = jnp.sum(err * err, axis=-1, keepdims=True) * (1.0 / D)
            loss_ref[...] += jnp.broadcast_to(0.5 * jnp.sum(row_loss, axis=0, keepdims=True), loss_ref.shape)
            dy = err * (1.0 / D)
            gfn_ref[...] += jnp.broadcast_to(jnp.sum(dy * hn, axis=0, keepdims=True), gfn_ref.shape)
            dhn = dy * nw
            dh2 = rstd * (dhn - hn * jnp.mean(dhn * hn, axis=-1, keepdims=True))
            dh2_ref[...] = dh2
            dh2t_ref[...] = dh2.T.astype(ACT_DTYPE)

    return pl.pallas_call(
        body, name="ffn_fwd", grid=(rows // tm, nj),
        in_specs=[pl.BlockSpec((tm, D), lambda i, j: (i, 0)), pl.BlockSpec((tm, D), lambda i, j: (i, 0)),
                  pl.BlockSpec((D, FF_WIDE), lambda i, j: (0, j)),
                  pl.BlockSpec((FF_WIDE, D), lambda i, j: (j, 0)),
                  pl.BlockSpec((1, D), lambda i, j: (0, 0))] + _token_specs(tm, grid_rank=2),
        out_specs=[pl.BlockSpec((tm, FF_WIDE), lambda i, j: (i, j)), pl.BlockSpec((tm, D), lambda i, j: (i, 0)),
                   pl.BlockSpec((D, tm), lambda i, j: (0, i)),
                   pl.BlockSpec((8, 128), lambda i, j: (0, 0)), pl.BlockSpec((8, D), lambda i, j: (0, 0))],
        out_shape=[jax.ShapeDtypeStruct((rows, D_FF), ACT_DTYPE), jax.ShapeDtypeStruct((rows, D), F32),
                   jax.ShapeDtypeStruct((D, rows), ACT_DTYPE),
                   jax.ShapeDtypeStruct((8, 128), F32), jax.ShapeDtypeStruct((8, D), F32)],
        scratch_shapes=[pltpu.VMEM((tm, D), F32)],
        compiler_params=_cp(("arbitrary", "arbitrary"), VMEM_WIDE_MB),
    )(f, h1, w1, w2, fnw, *([tgt] * nb))


def _ffn_bwd_act(dh2, a, w1, w2, h1, nfw, tm):
    rows = dh2.shape[0]
    nj = D_FF // FF_WIDE

    def body(dh2_ref, a_ref, w1_ref, w2_ref, h1_ref, nw_ref, da_ref, dh1_ref, gnf_ref, acc):
        i, j = pl.program_id(0), pl.program_id(1)

        @pl.when((i == 0) & (j == 0))
        def _():
            gnf_ref[...] = jnp.zeros_like(gnf_ref)

        @pl.when(j == 0)
        def _():
            acc[...] = jnp.zeros_like(acc)

        dz = _mm_nt(dh2_ref[...], w2_ref[...])
        da = dz * (2.0 * jnp.maximum(a_ref[...].astype(F32), 0.0))
        da_ref[...] = da.astype(ACT_DTYPE)
        acc[...] += _mm_nt(da, w1_ref[...])

        @pl.when(j == nj - 1)
        def _():
            h1 = h1_ref[...]
            rstd = lax.rsqrt(jnp.mean(h1 * h1, axis=-1, keepdims=True) + EPS)
            hn = h1 * rstd
            df = acc[...]
            gnf_ref[...] += jnp.broadcast_to(jnp.sum(df * hn, axis=0, keepdims=True), gnf_ref.shape)
            dfn = df * nw_ref[...]
            dh1_ref[...] = dh2_ref[...] + rstd * (dfn - hn * jnp.mean(dfn * hn, axis=-1, keepdims=True))

    return pl.pallas_call(
        body, name="ffn_bwd_act", grid=(rows // tm, nj),
        in_specs=[pl.BlockSpec((tm, D), lambda i, j: (i, 0)), pl.BlockSpec((tm, FF_WIDE), lambda i, j: (i, j)),
                  pl.BlockSpec((D, FF_WIDE), lambda i, j: (0, j)),
                  pl.BlockSpec((FF_WIDE, D), lambda i, j: (j, 0)),
                  pl.BlockSpec((tm, D), lambda i, j: (i, 0)), pl.BlockSpec((1, D), lambda i, j: (0, 0))],
        out_specs=[pl.BlockSpec((tm, FF_WIDE), lambda i, j: (i, j)), pl.BlockSpec((tm, D), lambda i, j: (i, 0)),
                   pl.BlockSpec((8, D), lambda i, j: (0, 0))],
        out_shape=[jax.ShapeDtypeStruct((rows, D_FF), ACT_DTYPE), jax.ShapeDtypeStruct((rows, D), F32),
                   jax.ShapeDtypeStruct((8, D), F32)],
        scratch_shapes=[pltpu.VMEM((tm, D), F32)],
        compiler_params=_cp(("arbitrary", "arbitrary"), VMEM_WIDE_MB),
    )(dh2, a, w1, w2, h1, nfw)


def _ffn_bwd_weights(ft, a, da, dh2t, tm):
    rows = a.shape[0]
    steps = rows // tm
    pair = 2 * FF_TILE

    def body(ft_ref, a_ref, da_ref, dh2t_ref, dw1_ref, dw2_ref, dw2t):
        i = pl.program_id(1)

        @pl.when(i == 0)
        def _():
            dw1_ref[...] = jnp.zeros_like(dw1_ref)
            dw2t[...] = jnp.zeros_like(dw2t)

        z = jnp.square(jnp.maximum(a_ref[...].astype(F32), 0.0))
        dw1 = _mm(ft_ref[...], da_ref[...])
        for core in range(2):
            dw1_ref[core] += dw1[:, core * FF_TILE:(core + 1) * FF_TILE]
        dw2t[...] += _mm(dh2t_ref[...], z)

        @pl.when(i == steps - 1)
        def _():
            for core in range(2):
                dw2_ref[core] = dw2t[:, core * FF_TILE:(core + 1) * FF_TILE].T

    return pl.pallas_call(
        body, name="ffn_bwd_weights", grid=(N_DEV // 2, steps),
        in_specs=[pl.BlockSpec((D, tm), lambda j, i: (0, i)), pl.BlockSpec((tm, pair), lambda j, i: (i, j)),
                  pl.BlockSpec((tm, pair), lambda j, i: (i, j)), pl.BlockSpec((D, tm), lambda j, i: (0, i))],
        out_specs=[pl.BlockSpec((2, None, D, FF_TILE), lambda j, i: (0, j, 0, 0)),
                   pl.BlockSpec((2, None, FF_TILE, D), lambda j, i: (0, j, 0, 0))],
        out_shape=[jax.ShapeDtypeStruct((2, 4, D, FF_TILE), F32), jax.ShapeDtypeStruct((2, 4, FF_TILE, D), F32)],
        scratch_shapes=[pltpu.VMEM((D, pair), F32)],
        compiler_params=_cp(("arbitrary", "arbitrary"), VMEM_WIDE_MB),
    )(ft, a, da, dh2t)


def _out_proj_bwd(dh1, og, osw, wout, tm, partials):
    rows = dh1.shape[0]
    steps = rows // tm
    ns = len(partials)

    def body(dh1_ref, og_ref, os_ref, w_ref, *rest):
        part_refs, rest = rest[:ns], rest[ns:]
        dog_ref, dos_ref, dw_ref = rest[:3]
        land_refs, (send_sems, recv_sems) = rest[3:3 + ns], rest[3 + ns:]
        i = pl.program_id(0)
        start, finish = _sibling_schedule(part_refs, land_refs, send_sems, recv_sems)

        @pl.when(i == 0)
        def _():
            dw_ref[...] = jnp.zeros_like(dw_ref)
            start()

        pl.when(i == steps - 1)(finish)

        dh1 = dh1_ref[...].astype(MXU_DTYPE)
        dog_ref[...] = _mm_nt(dh1, w_ref[0:512, :])
        dos_ref[...] = _mm_nt(dh1, w_ref[512:1024, :])
        for half, ref in enumerate((og_ref, os_ref)):
            dw = _mm_tn(ref[...], dh1)
            for blk in range(4):
                shard = half * 4 + blk
                dw_ref[shard % 2, shard // 2] += dw[blk * 128:(blk + 1) * 128, :]

    row = lambda w: pl.BlockSpec((tm, w), lambda i: (i, 0))
    outs = pl.pallas_call(
        body, name="out_proj_bwd", grid=(steps,),
        in_specs=[row(D), row(512), row(512), pl.BlockSpec((D, D), lambda i: (0, 0))] + [ANY] * ns,
        out_specs=[row(512), row(512), pl.BlockSpec((2, 4, 128, D), lambda i: (0, 0, 0, 0))] + [ANY] * ns,
        out_shape=[jax.ShapeDtypeStruct((rows, 512), F32), jax.ShapeDtypeStruct((rows, 512), F32),
                   jax.ShapeDtypeStruct((2, 4, 128, D), F32)] + _sibling_shapes(partials),
        scratch_shapes=_sibling_sems(ns),
        compiler_params=_cp(("arbitrary",), VMEM_TILE_MB),
    )(dh1, og, osw, wout, *partials)
    return outs[0], outs[1], outs[2], outs[3:]


def _swa_bwd(qr, kr, vr, osw, dos, sinks, jobs):
    rows = qr.shape[0]
    nblk = rows // SWA_BLOCK
    group = _swa_group(nblk)
    steps = nblk // group
    ns = jobs.n

    def body(q_ref, k0, kp, kc, v0, vp, vc, o_ref, do_ref, sink_ref, *rest):
        dq_ref, dk_ref, dv_ref, dsink_ref = rest[ns:ns + 4]
        start, finish = jobs.bind(rest[:ns], rest[ns + 4:2 * ns + 4], rest[2 * ns + 4:])
        step = pl.program_id(0)

        @pl.when(step == 0)
        def _():
            dk_ref[...] = jnp.zeros_like(dk_ref)
            dv_ref[...] = jnp.zeros_like(dv_ref)
            dsink_ref[...] = jnp.zeros_like(dsink_ref)
            start()

        pl.when(step == steps - 1)(finish)
        for g in range(group):
            block(step * group + g, g, q_ref, k0, kp, kc, v0, vp, vc, o_ref, do_ref, sink_ref,
                  dq_ref, dk_ref, dv_ref, dsink_ref)

    def block(n, g, q_ref, k0, kp, kc, v0, vp, vc, o_ref, do_ref, sink_ref, dq_ref, dk_ref, dv_ref, dsink_ref):
        rs = slice(g * SWA_BLOCK, (g + 1) * SWA_BLOCK)
        kall, vall = _swa_keys(k0, kp, kc, g), _swa_keys(v0, vp, vc, g)
        mask = _swa_mask(n)[0:SWA_BLOCK]
        heads = range(SWA_HEADS)
        hs = [slice(h * SWA_HD, (h + 1) * SWA_HD) for h in heads]
        kv = [slice((h // SWA_GROUP) * SWA_HD, (h // SWA_GROUP + 1) * SWA_HD) for h in heads]
        sink = [sink_ref[0, h] for h in heads]
        qh = [q_ref[rs, hs[h]] for h in heads]
        doh = [do_ref[rs, hs[h]] for h in heads]
        s = [jnp.where(mask, _mm_nt(qh[h], kall[:, kv[h]]), NEG) for h in heads]
        dp = [_mm_nt(doh[h], vall[:, kv[h]]) for h in heads]
        delta = [jnp.sum(doh[h] * o_ref[rs, hs[h]].astype(F32), axis=-1, keepdims=True) for h in heads]
        m = [jnp.maximum(jnp.max(s[h], axis=-1, keepdims=True), sink[h]) for h in heads]
        e = [jnp.exp(s[h] - m[h]) for h in heads]
        inv = [1.0 / (jnp.sum(e[h], axis=-1, keepdims=True) + jnp.exp(sink[h] - m[h])) for h in heads]
        p = [e[h] * inv[h] for h in heads]
        ds = [p[h] * (dp[h] - delta[h]) for h in heads]
        dq = [_mm(ds[h], kall[:, kv[h]]) for h in heads]
        dkh = [_mm_tn(ds[h], qh[h]) for h in heads]
        dvh = [_mm_tn(p[h], doh[h]) for h in heads]
        for h in heads:
            dsink = -jnp.sum(jnp.exp(sink[h] - m[h]) * inv[h] * delta[h], axis=0, keepdims=True)
            dsink_ref[h:h + 1, :] += jnp.broadcast_to(dsink, (1, 128))
        dq_ref[rs, :] = jnp.concatenate(dq, axis=1).astype(ACT_DTYPE)
        group_sum = lambda parts, kvh: sum(parts[kvh * SWA_GROUP + 1:(kvh + 1) * SWA_GROUP], parts[kvh * SWA_GROUP])
        dk_all = jnp.concatenate([group_sum(dkh, kvh) for kvh in range(SWA_KV)], axis=1)
        dv_all = jnp.concatenate([group_sum(dvh, kvh) for kvh in range(SWA_KV)], axis=1)
        prev0 = pl.multiple_of(jnp.maximum(n - 1, 0) * SWA_BLOCK, SWA_BLOCK)
        cur0 = pl.multiple_of(n * SWA_BLOCK, SWA_BLOCK)
        for ref, val in ((dk_ref, dk_all), (dv_ref, dv_all)):
            ref[0:SWA_BLOCK, :] += val[0:SWA_BLOCK]
            ref[pl.ds(prev0, SWA_BLOCK), :] += val[SWA_BLOCK:2 * SWA_BLOCK]
            ref[pl.ds(cur0, SWA_BLOCK), :] += val[2 * SWA_BLOCK:]

    blk, first, prev = _swa_specs(group)
    whole = pl.BlockSpec((rows, 128), lambda n: (0, 0))
    outs = pl.pallas_call(
        body, name="swa_bwd", grid=(steps,),
        in_specs=[blk(512), first, prev, blk(128), first, prev, blk(128), blk(512), blk(512),
                  pl.BlockSpec(memory_space=pltpu.SMEM)] + [ANY] * ns,
        out_specs=[blk(512), whole, whole, pl.BlockSpec((8, 128), lambda n: (0, 0))] + [ANY] * ns,
        out_shape=[jax.ShapeDtypeStruct((rows, 512), ACT_DTYPE), jax.ShapeDtypeStruct((rows, 128), F32),
                   jax.ShapeDtypeStruct((rows, 128), F32), jax.ShapeDtypeStruct((8, 128), F32)] + jobs.out_shapes,
        scratch_shapes=jobs.sems,
        compiler_params=_cp(("arbitrary",), VMEM_TILE_MB),
    )(qr, kr, kr, kr, vr, vr, vr, osw, dos, sinks, *jobs.inputs)
    return outs[0], outs[1], outs[2], outs[3], jobs.split(outs[4:])


def _gla_bwd(proj, decay, dgate, oraw, states, dog, wg_p, gnw, jobs):
    rows = proj.shape[0]
    nc = rows // GLA_CHUNK
    group = _gla_group(nc, 5)
    steps, nrows = nc // group, group * GLA_CHUNK
    ns = jobs.n

    def body(q_ref, k_ref, v_ref, r_ref, lr_ref, b_ref, dgate_ref, oraw_ref, st_ref, dog_ref, wg_ref, gnw_ref, *rest):
        dq_ref, dk_ref, dv_ref, dr_ref, dlr_ref, dwg_ref, dbg_ref, dgnw_ref = rest[ns:ns + 8]
        dstate, db_scr = rest[2 * ns + 8:2 * ns + 10]
        start, finish = jobs.bind(rest[:ns], rest[ns + 8:2 * ns + 8], rest[2 * ns + 10:])
        t = pl.program_id(0)

        @pl.when(t == 0)
        def _():
            dstate[...] = jnp.zeros_like(dstate)
            dwg_ref[...] = jnp.zeros_like(dwg_ref)
            dbg_ref[...] = jnp.zeros_like(dbg_ref)
            dgnw_ref[...] = jnp.zeros_like(dgnw_ref)
            start()

        pl.when(t == steps - 1)(finish)

        lr, wg = lr_ref[...], wg_ref[...]
        b = b_ref[...]
        eb, enb = jnp.exp(b), jnp.exp(-b)
        scale = GLA_DK ** -0.5
        gq = q_ref[...] * scale * eb
        gk = k_ref[...] * enb
        v = v_ref[...]
        gnw_v = gnw_ref[...]
        tril = _tril64()
        is_last = lax.broadcasted_iota(jnp.int32, (GLA_CHUNK, 1), 0) == GLA_CHUNK - 1
        dgnw = jnp.zeros((1, GLA_DV), F32)
        pairs = [(h, gi) for h in range(GLA_HEADS) for gi in range(group)]
        rs = {gi: slice(gi * GLA_CHUNK, (gi + 1) * GLA_CHUNK) for gi in range(group)}
        s64 = {h: slice(h * GLA_DK, (h + 1) * GLA_DK) for h in range(GLA_HEADS)}
        s128 = {h: slice(h * GLA_DV, (h + 1) * GLA_DV) for h in range(GLA_HEADS)}
        qh = {(h, gi): gq[rs[gi], s64[h]] for h, gi in pairs}
        kh = {(h, gi): gk[rs[gi], s64[h]] for h, gi in pairs}
        vh = {(h, gi): v[rs[gi], s128[h]] for h, gi in pairs}
        ebl = {(h, gi): eb[(gi + 1) * GLA_CHUNK - 1:(gi + 1) * GLA_CHUNK, s64[h]] for h, gi in pairs}
        kl = {pr: kh[pr] * ebl[pr] for pr in pairs}
        st = {(h, gi): st_ref[gi, h] for h, gi in pairs}
        do = {}
        for h, gi in pairs:
            o, rh, dout = oraw_ref[rs[gi], s128[h]], r_ref[rs[gi], s128[h]], dog_ref[rs[gi], s128[h]]
            rstd = lax.rsqrt(jnp.mean(o * o, axis=-1, keepdims=True) + EPS)
            on = o * rstd
            sg = _sigmoid(rh)
            dr_ref[rs[gi], s128[h]] = (dout * (on * gnw_v) * (sg * (1.0 + rh * (1.0 - sg)))).astype(ACT_DTYPE)
            dy = dout * (rh * sg)
            dgnw = dgnw + jnp.sum(dy * on, axis=0, keepdims=True)
            don = dy * gnw_v
            do[h, gi] = rstd * (don - on * jnp.mean(don * on, axis=-1, keepdims=True))
        a = {pr: jnp.where(tril, _mm_nt(qh[pr], kh[pr]), 0.0) for pr in pairs}
        da = {pr: jnp.where(tril, _mm_nt(do[pr], vh[pr]), 0.0) for pr in pairs}
        dinc = {pr: _mm_tn(do[pr], qh[pr]) for pr in pairs}
        dgq = {pr: _mm(da[pr], kh[pr]) + _mm(do[pr], st[pr]) for pr in pairs}
        dgk = {pr: _mm_tn(da[pr], qh[pr]) for pr in pairs}
        dv_a = {pr: _mm_tn(a[pr], do[pr]) for pr in pairs}
        dsp = {}
        for h in range(GLA_HEADS):
            cur = dstate[h]
            for gi in reversed(range(group)):
                dsp[h, gi] = cur
                cur = cur * ebl[h, gi] + dinc[h, gi]
            dstate[h] = cur
        for h, gi in pairs:
            pr = (h, gi)
            dkl = _mm(vh[pr], dsp[pr])
            dv_ref[rs[gi], s128[h]] = (dv_a[pr] + _mm_nt(kl[pr], dsp[pr])).astype(ACT_DTYPE)
            debl = jnp.sum(dsp[pr] * st[pr], axis=0, keepdims=True)
            dq_ref[rs[gi], s64[h]] = (dgq[pr] * (scale * eb[rs[gi], s64[h]])).astype(ACT_DTYPE)
            dk_ref[rs[gi], s64[h]] = ((dgk[pr] + dkl * ebl[pr]) * enb[rs[gi], s64[h]]).astype(ACT_DTYPE)
            last = debl * ebl[pr] + jnp.sum(dkl * kl[pr], axis=0, keepdims=True)
            db_scr[rs[gi], s64[h]] = (dgq[pr] * qh[pr] - dgk[pr] * kh[pr] - dkl * kl[pr]
                                      + jnp.where(is_last, last, 0.0))
        dzg = _masked_sums(_chunk_masks(nrows)[1], db_scr[...]) * dgate_ref[...]
        dlr_ref[...] = _mm_nt(dzg, wg).astype(ACT_DTYPE)
        dwg_ref[...] += _mm_tn(lr, dzg)
        dbg_ref[...] += jnp.broadcast_to(jnp.sum(dzg, axis=0, keepdims=True), dbg_ref.shape)
        dgnw_ref[...] += jnp.broadcast_to(dgnw, dgnw_ref.shape)

    nb = lambda w, col: pl.BlockSpec((nrows, w), lambda t: (steps - 1 - t, col // w))
    const = lambda shape: pl.BlockSpec(shape, lambda t: (0,) * len(shape))
    outs = pl.pallas_call(
        body, name="gla_bwd", grid=(steps,),
        in_specs=[nb(256, C_GQ), nb(256, C_GK), nb(512, C_GV), nb(512, C_GR), nb(128, C_LR), nb(256, 0), nb(256, 0),
                  nb(512, 0),
                  pl.BlockSpec((group, GLA_HEADS, GLA_DV, GLA_DK), lambda t: (steps - 1 - t, 0, 0, 0)), nb(512, 0),
                  const((128, 256)), const((1, 128))] + [ANY] * ns,
        out_specs=[nb(256, 0), nb(256, 0), nb(512, 0), nb(512, 0), nb(128, 0),
                   const((128, 256)), const((8, 256)), const((8, 128))] + [ANY] * ns,
        out_shape=[jax.ShapeDtypeStruct((rows, 256), ACT_DTYPE), jax.ShapeDtypeStruct((rows, 256), ACT_DTYPE),
                   jax.ShapeDtypeStruct((rows, 512), ACT_DTYPE), jax.ShapeDtypeStruct((rows, 512), ACT_DTYPE),
                   jax.ShapeDtypeStruct((rows, 128), ACT_DTYPE), jax.ShapeDtypeStruct((128, 256), F32),
                   jax.ShapeDtypeStruct((8, 256), F32), jax.ShapeDtypeStruct((8, 128), F32)] + jobs.out_shapes,
        scratch_shapes=[pltpu.VMEM((GLA_HEADS, GLA_DV, GLA_DK), F32), pltpu.VMEM((nrows, 256), F32)] + jobs.sems,
        compiler_params=_cp(("arbitrary",)),
    )(proj, proj, proj, proj, proj, decay, dgate, oraw, states, dog, wg_p, gnw, *jobs.inputs)
    return outs[:8], jobs.split(outs[8:])


def _in_proj_bwd(x, lead, dh1, nw, win_p, dgv, dgr, dsq, dgq, dgk, dsk, dsv, dlr, tabs, tm):
    seq = x.shape[0]
    rows = LEAD + seq
    nb = tm // LEAD
    steps = rows // tm

    def first_copy(scr, gx_ref, sem):
        return pltpu.make_async_copy(scr.at[pl.ds(LEAD, tm - LEAD)], gx_ref.at[pl.ds(0, tm - LEAD)], sem)

    def tile_copy(scr, gx_ref, sem, step):
        start = pl.multiple_of(jnp.maximum(step * tm - LEAD, 0), LEAD)
        return pltpu.make_async_copy(scr, gx_ref.at[pl.ds(start, tm)], sem)

    def body(*refs):
        x_refs, refs = refs[:nb], refs[nb:]
        (lead_ref, dh1_ref, nw_ref, w_ref, dgv_ref, dgr_ref, dsq_ref, dgq_ref, dgk_ref, dsk_ref, dsv_ref, dlr_ref,
         c_ref, sa_ref, sb_ref, gx_ref, dlead_ref, dproj_ref, ut_ref, gnm_ref, scr, sem) = refs
        i = pl.program_id(0)

        @pl.when(i == 0)
        def _():
            gnm_ref[...] = jnp.zeros_like(gnm_ref)

        cos, sa, sb = c_ref[...], sa_ref[...], sb_ref[...]
        dsq_v = (_unrope(dsq_ref[...].astype(F32), cos, sa, sb) * (SWA_HD ** -0.5)).astype(MXU_DTYPE)
        dsk_v = _unrope(dsk_ref[...], cos, sa, sb).astype(MXU_DTYPE)
        dproj = jnp.concatenate(
            [dgv_ref[...].astype(MXU_DTYPE), dgr_ref[...].astype(MXU_DTYPE), dgq_ref[...].astype(MXU_DTYPE),
             dgk_ref[...].astype(MXU_DTYPE), dlr_ref[...].astype(MXU_DTYPE), dsq_v, dsk_v,
             dsv_ref[...].astype(MXU_DTYPE)],
            axis=1)
        dproj_ref[...] = dproj
        h = _h_tile(i, lead_ref, x_refs)
        rstd = lax.rsqrt(jnp.mean(h * h, axis=-1, keepdims=True) + EPS)
        hn = h * rstd
        nw_v = nw_ref[...]
        ut_ref[...] = (hn * nw_v).T.astype(ACT_DTYPE)
        du = _mm_nt(dproj, w_ref[...])
        gnm_ref[...] += jnp.broadcast_to(jnp.sum(du * hn, axis=0, keepdims=True), gnm_ref.shape)
        dun = du * nw_v
        dh0 = dh1_ref[...] + rstd * (dun - hn * jnp.mean(dun * hn, axis=-1, keepdims=True))

        if tm > LEAD:
            pl.when(i == 1)(lambda: first_copy(scr, gx_ref, sem).wait())
        pl.when(i > 1)(lambda: tile_copy(scr, gx_ref, sem, i).wait())
        scr[...] = dh0

        @pl.when(i == 0)
        def _():
            dlead_ref[...] = dh0[0:LEAD]
            if tm > LEAD:
                first_copy(scr, gx_ref, sem).start()
                if steps == 1:
                    first_copy(scr, gx_ref, sem).wait()

        @pl.when(i > 0)
        def _():
            tile_copy(scr, gx_ref, sem, i).start()

        if steps > 1:
            pl.when(i == steps - 1)(lambda: tile_copy(scr, gx_ref, sem, i).wait())

    row = lambda w: pl.BlockSpec((tm, w), lambda i: (i, 0))
    const = lambda shape: pl.BlockSpec(shape, lambda i: (0,) * len(shape))
    return pl.pallas_call(
        body, name="in_proj_bwd", grid=(steps,),
        in_specs=_token_specs(tm) + [const((LEAD, D)), row(D), const((1, D)), const((D, DINP)),
                                     row(512), row(512), row(512), row(256), row(256), row(128), row(128), row(128),
                                     row(128), row(128), row(128)],
        out_specs=[ANY, const((LEAD, D)), row(DINP), pl.BlockSpec((D, tm), lambda i: (0, i)), const((8, D))],
        out_shape=[jax.ShapeDtypeStruct((seq, D), F32), jax.ShapeDtypeStruct((LEAD, D), F32),
                   jax.ShapeDtypeStruct((rows, DINP), ACT_DTYPE), jax.ShapeDtypeStruct((D, rows), ACT_DTYPE),
                   jax.ShapeDtypeStruct((8, D), F32)],
        scratch_shapes=[pltpu.VMEM((tm, D), F32), pltpu.SemaphoreType.DMA],
        compiler_params=_cp(("arbitrary",), VMEM_WIDE_MB),
    )(*([x] * nb), lead, dh1, nw, win_p, dgv, dgr, dsq, dgq, dgk, dsk, dsv, dlr, *tabs)


def _win_runs():
    groups = [(O_GQ, C_GQ), (O_GK, C_GK), (O_GV, C_GV), (O_GR, C_GR), (O_LR, C_LR), (O_SQ, C_SQ), (O_SK, C_SK),
              (O_SV, C_SV)]
    per = DIN // N_DEV
    runs = []
    for (o0, o1), c0 in groups:
        o = o0
        while o < o1:
            d = o // per
            end = min(o1, (d + 1) * per)
            runs.append((d, o - d * per, c0 + o - o0, end - o))
            o = end
    return runs


def _win_padded(g_in):
    tr = 128

    def body(g_ref, o_ref):
        o_ref[...] = jnp.zeros_like(o_ref)
        for d, s, c, w in _win_runs():
            o_ref[:, c:c + w] = g_ref[d, :, s:s + w]

    return pl.pallas_call(
        body, name="w_in_layout", grid=(D // tr,),
        in_specs=[pl.BlockSpec((N_DEV, tr, DIN // N_DEV), lambda i: (0, i, 0))],
        out_specs=pl.BlockSpec((tr, DINP), lambda i: (i, 0)),
        out_shape=jax.ShapeDtypeStruct((D, DINP), g_in.dtype),
        compiler_params=_cp(("arbitrary",)),
    )(g_in)


def _in_proj_bwd_weights(ut, dproj, tm, small):
    rows = dproj.shape[0]
    steps = rows // tm
    per = DIN // N_DEV

    def body(ut_ref, dp_ref, *rest):
        small_refs, (mine_ref, theirs_ref, total_ref, acc, stage, local_sems, send_sems, recv_sems) = rest[:9], rest[9:17]
        i = pl.program_id(0)
        start, finish = _small_sum_schedule(small_refs, total_ref, *rest[17:])
        x, y, c = _mesh_pos()

        @pl.when(i == 0)
        def _():
            acc[...] = jnp.zeros_like(acc)
            start()

        acc[...] += _mm(ut_ref[...], dp_ref[...])
        pl.when(i == steps - 1)(finish)

        def keep(slot, chip):
            return pltpu.make_async_copy(stage.at[slot], mine_ref.at[chip], local_sems.at[slot])

        def send(slot, chip):
            return pltpu.make_async_remote_copy(
                src_ref=stage.at[slot], dst_ref=theirs_ref.at[chip], send_sem=send_sems.at[slot],
                recv_sem=recv_sems.at[chip], device_id=(x, y, 1 - c), device_id_type=MESH)

        def drained(d):
            pl.when(c == d % 2)(keep(d % 2, d // 2).wait)
            pl.when(c != d % 2)(send(d % 2, d // 2).wait_send)

        @pl.when(i == steps - 1)
        def _():
            for d in range(N_DEV):
                slot, chip = d % 2, d // 2
                if d >= 2:
                    drained(d - 2)
                for owner, s, col, w in _win_runs():
                    if owner == d:
                        stage[slot, :, s:s + w] = acc[:, col:col + w]
                pl.when(c == slot)(keep(slot, chip).start)
                pl.when(c != slot)(send(slot, chip).start)
            drained(N_DEV - 2)
            drained(N_DEV - 1)
            for chip in range(4):
                send(0, chip).wait_recv()

    half = jax.ShapeDtypeStruct((4, D, per), F32)
    return pl.pallas_call(
        body, name="in_proj_bwd_weights", grid=(steps,),
        in_specs=[pl.BlockSpec((D, tm), lambda i: (0, i)), pl.BlockSpec((tm, DINP), lambda i: (i, 0))] + SMALL_SPECS,
        out_specs=[ANY, ANY, pl.BlockSpec((SMALL_ROWS, D), lambda i: (0, 0))],
        out_shape=[half, half, jax.ShapeDtypeStruct((SMALL_ROWS, D), F32)],
        scratch_shapes=[pltpu.VMEM((D, DINP), F32), pltpu.VMEM((2, D, per), F32), pltpu.SemaphoreType.DMA((2,)),
                        pltpu.SemaphoreType.DMA((2,)), pltpu.SemaphoreType.DMA((4,))] + _small_sum_scratch(),
        compiler_params=_cp(("arbitrary",), VMEM_WIDE_MB),
    )(ut, dproj, *small)


def _adamw(w, g, m, v):
    m = ADAM_B1 * m + (1.0 - ADAM_B1) * g
    v = ADAM_B2 * v + (1.0 - ADAM_B2) * jnp.square(g)
    m_hat = m / (1.0 - ADAM_B1 ** ADAM_STEP)
    v_hat = v / (1.0 - ADAM_B2 ** ADAM_STEP)
    delta = -ADAM_LR * (m_hat / (jnp.sqrt(v_hat) + ADAM_EPS) + ADAM_WD * w)
    return delta, m, v


ADAM_STEPS = 8


def _adamw_shards(items, name, jobs=None):
    jobs = jobs or _Jobs([])
    ns, nw = jobs.n, len(items)

    def body(*rest):
        ins, rest = rest[:5 * nw], rest[5 * nw:]
        job_ins, rest = rest[:ns], rest[ns:]
        outs, rest = rest[:4 * nw], rest[4 * nw:]
        start, finish = jobs.bind(job_ins, rest[:ns], rest[ns:])
        i = pl.program_id(0)
        pl.when(i == 0)(start)
        pl.when(i == ADAM_STEPS - 1)(finish)
        for k in range(nw):
            p_ref, own_ref, w_ref, m_ref, v_ref = ins[5 * k:5 * k + 5]
            g_ref, d_ref, nm_ref, nv_ref = outs[4 * k:4 * k + 4]
            g = ((p_ref[0].astype(F32) + p_ref[1].astype(F32)) + p_ref[2].astype(F32)) + own_ref[...]
            g_ref[...] = g
            d_ref[...], nm_ref[...], nv_ref[...] = _adamw(w_ref[...], g, m_ref[...], v_ref[...])

    in_specs, out_specs, out_shape, operands = [], [], [], []
    for parts, own, w, m, v in items:
        r, cdim = w.shape
        tr = r // ADAM_STEPS
        spec = pl.BlockSpec((tr, cdim), lambda i: (i, 0))
        in_specs += [pl.BlockSpec((3, tr, cdim), lambda i: (0, i, 0)), spec, spec, spec, spec]
        out_specs += [spec] * 4
        out_shape += [jax.ShapeDtypeStruct((r, cdim), F32)] * 4
        operands += [parts, own, w, m, v]
    outs = pl.pallas_call(
        body, name=name, grid=(ADAM_STEPS,),
        in_specs=in_specs + [ANY] * ns, out_specs=out_specs + [ANY] * ns, scratch_shapes=jobs.sems,
        out_shape=out_shape + jobs.out_shapes,
        compiler_params=_cp(("arbitrary",)),
    )(*operands, *jobs.inputs)
    return [outs[4 * k:4 * k + 4] for k in range(nw)], jobs.split(outs[4 * nw:])


def _adamw_small(items):
    n = len(items)

    def body(*refs):
        ins, outs = refs[:4 * n], refs[4 * n:]
        for k in range(n):
            w_ref, g_ref, m_ref, v_ref = ins[4 * k:4 * k + 4]
            d_ref, nm_ref, nv_ref = outs[3 * k:3 * k + 3]
            d_ref[...], nm_ref[...], nv_ref[...] = _adamw(w_ref[...], g_ref[...], m_ref[...], v_ref[...])

    vm = pl.BlockSpec(memory_space=pltpu.VMEM)
    shapes = [jax.ShapeDtypeStruct(w.shape, F32) for w, _, _, _ in items for _ in range(3)]
    outs = pl.pallas_call(body, name="adamw_small", in_specs=[vm] * (4 * n), out_specs=[vm] * (3 * n),
                          out_shape=shapes)(*[t for item in items for t in item])
    return [outs[3 * k:3 * k + 3] for k in range(n)]


def _pair_sums(where, mine, theirs, name):
    _, r, cdim = theirs.shape
    tr = 128 if r % 128 == 0 else r

    def body(where_ref, a_ref, b_ref, own_ref, wire_ref):
        chip = where_ref[1]
        own_ref[...] = a_ref[chip] + b_ref[chip]
        wire_ref[...] = (a_ref[...] + b_ref[...]).astype(WIRE_DTYPE)

    spec = pl.BlockSpec((4, tr, cdim), lambda i, s: (0, i, 0))
    mine_spec = spec if mine.ndim == 3 else pl.BlockSpec((None, 4, tr, cdim), lambda i, s: (s[0], 0, i, 0))
    return pl.pallas_call(
        body, name=name,
        grid_spec=pltpu.PrefetchScalarGridSpec(
            num_scalar_prefetch=1, grid=(r // tr,), in_specs=[mine_spec, spec],
            out_specs=[pl.BlockSpec((tr, cdim), lambda i, s: (i, 0)), spec]),
        out_shape=[jax.ShapeDtypeStruct((r, cdim), F32), jax.ShapeDtypeStruct(theirs.shape, WIRE_DTYPE)],
        compiler_params=_cp(("arbitrary",)))(where, mine, theirs)


def kernel(x, meta_tokens, norm_mix_w, w_in, w_gate_up, b_gate, gla_norm_w, sinks, w_out, norm_ff_w, w_ff1, w_ff2, final_norm_w, loss_target, m_meta_tokens, m_norm_mix_w, m_w_in, m_w_gate_up, m_b_gate, m_gla_norm_w, m_sinks, m_w_out, m_norm_ff_w, m_w_ff1, m_w_ff2, m_final_norm_w, v_meta_tokens, v_norm_mix_w, v_w_in, v_w_gate_up, v_b_gate, v_gla_norm_w, v_sinks, v_w_out, v_norm_ff_w, v_w_ff1, v_w_ff2, v_final_norm_w):
    seq = x.shape[1]
    rows = LEAD + seq
    tm = _row_tile(rows)
    tm_wide = WIDE_ROW_TILE if rows % WIDE_ROW_TILE == 0 else tm
    dev =4 * lax.axis_index("x") + 2 * lax.axis_index("y") + lax.axis_index("c")

    small_shard = jnp.concatenate([meta_tokens, w_gate_up[0], jnp.zeros((N_META, 96), F32)], axis=1)
    g_in, g_small = _all_gather([w_in[0].astype(WIRE_DTYPE), small_shard])
    later_shards = [w_out[0].astype(WIRE_DTYPE), w_ff1[0].astype(WIRE_DTYPE), w_ff2[0].astype(WIRE_DTYPE)]
    win_p = _win_padded(g_in)
    meta_full = jnp.transpose(g_small[:, :, 0:128], (1, 0, 2)).reshape(N_META, D)
    wg_full = jnp.transpose(g_small[:, :, 128:160], (1, 0, 2)).reshape(GLA_RANK, GLA_HEADS * GLA_DK)
    wg_p = jnp.concatenate([wg_full, jnp.zeros((128 - GLA_RANK, 256), F32)], axis=0)

    lead = jnp.concatenate([jnp.zeros((META0, D), F32), meta_full], axis=0)
    tabs = _rope_tables(rows)
    proj, qr, kr, vr, (g_w1,) = _in_proj(x[0], lead, norm_mix_w, win_p, tabs, tm, later_shards[1:2])
    oraw, og, states, decay, dgate, (g_out,) = _gla_fwd(proj, wg_p, b_gate, gla_norm_w, later_shards[0:1])
    osw, (g_w2,) = _swa_fwd(qr, kr, vr, sinks, later_shards[2:3])
    wout_full = g_out.reshape(D, D)
    w2_full = g_w2.reshape(D_FF, D)
    w1_full = jnp.transpose(g_w1, (1, 0, 2)).reshape(D, D_FF)
    h1, f, ft = _out_proj(x[0], lead, og, osw, wout_full, norm_ff_w, tm)
    a, dh2, dh2t, loss_p, gfn_p = _ffn_fwd(f, h1, w1_full, w2_full, loss_target[0], final_norm_w.reshape(1, D), tm)

    da, dh1, gnf_p = _ffn_bwd_act(dh2, a, w1_full, w2_full, h1, norm_ff_w, tm)
    dw1, dw2 = _ffn_bwd_weights(ft, a, da, dh2t, tm_wide)
    where = jnp.stack([lax.axis_index("c"), 2 * lax.axis_index("x") + lax.axis_index("y")]).astype(jnp.int32)
    dog, dos, dwout, theirs_ffn = _out_proj_bwd(dh1, og, osw, wout_full, tm, [dw1, dw2])
    pairs_ffn = [_pair_sums(where, p, q, "reduce_pair_%d" % (2 + k))
                 for k, (p, q) in enumerate(zip([dw1, dw2], theirs_ffn))]
    sums_ffn, wires_ffn = [p[0] for p in pairs_ffn], [p[1] for p in pairs_ffn]
    dsq, dsk, dsv, dsink_p, (parts_ffn, (theirs_wout,)) = _swa_bwd(
        qr, kr, vr, osw, dos, sinks, _Jobs([("chips", wires_ffn), ("sibling", [dwout])]))
    sum_wout, wire_wout = _pair_sums(where, dwout, theirs_wout, "reduce_pair_1")
    (dgq, dgk, dgv, dgr, dlr, dwg_p, dbg_p, dgnw_p), ((parts_wout,),) = _gla_bwd(
        proj, decay, dgate, oraw, states, dog, wg_p, gla_norm_w, _Jobs([("chips", [wire_wout])]))
    grad_x, dlead, dproj, ut, gnm_p = _in_proj_bwd(x[0], lead, dh1, norm_mix_w, win_p, dgv, dgr, dsq, dgq, dgk, dsk,
                                                   dsv, dlr, tabs, tm)
    grad_x = grad_x[None]
    dwin_mine, dwin_theirs, total = _in_proj_bwd_weights(
        ut, dproj, tm_wide, [dlead, dwg_p, gnm_p, gnf_p, gfn_p, dbg_p, dgnw_p, loss_p, dsink_p])
    sum_win, sum_win_wire = _pair_sums(where, dwin_mine, dwin_theirs, "reduce_pair_0")

    g_meta = lax.dynamic_slice(total, (R_META, dev * 128), (N_META, 128))
    g_wg = lax.dynamic_slice(total, (R_WG, dev * 32), (GLA_RANK, 32))
    g_norm_mix, g_norm_ff = total[R_NORM_MIX:R_NORM_MIX + 1], total[R_NORM_FF:R_NORM_FF + 1]
    g_final_norm = total[R_FINAL:R_FINAL + 1]
    g_b_gate, g_gla_norm = total[R_B_GATE:R_B_GATE + 1, 0:256], total[R_GLA_NORM:R_GLA_NORM + 1, 0:128]
    g_sinks = total[R_SINKS:R_SINKS + SWA_HEADS, 0].reshape(1, SWA_HEADS)
    loss = total[R_LOSS, 0]

    ((g_wout, d_wout, nm_wout, nv_wout), (g_w1s, d_w1, nm_w1, nv_w1), (g_w2s, d_w2, nm_w2, nv_w2)), ((parts_win,),) = \
        _adamw_shards([(parts_wout, sum_wout, w_out[0], m_w_out[0], v_w_out[0]),
                       (parts_ffn[0], sums_ffn[0], w_ff1[0], m_w_ff1[0], v_w_ff1[0]),
                       (parts_ffn[1], sums_ffn[1], w_ff2[0], m_w_ff2[0], v_w_ff2[0])],
                      "adamw_w_out_ff", _Jobs([("chips", [sum_win_wire])]))
    ((g_win, d_win, nm_win, nv_win),), _ = _adamw_shards(
        [(parts_win, sum_win, w_in[0], m_w_in[0], v_w_in[0])], "adamw_w_in")

    names = ["meta", "wg", "norm_mix", "b_gate", "gla_norm", "sinks", "norm_ff", "final_norm"]
    ws = [meta_tokens, w_gate_up, norm_mix_w, b_gate, gla_norm_w, sinks, norm_ff_w, final_norm_w]
    gs = [g_meta, g_wg, g_norm_mix, g_b_gate, g_gla_norm, g_sinks, g_norm_ff, g_final_norm]
    ms = [m_meta_tokens, m_w_gate_up, m_norm_mix_w, m_b_gate, m_gla_norm_w, m_sinks, m_norm_ff_w, m_final_norm_w]
    vs = [v_meta_tokens, v_w_gate_up, v_norm_mix_w, v_b_gate, v_gla_norm_w, v_sinks, v_norm_ff_w, v_final_norm_w]
    flat = lambda t: t.reshape(-1, t.shape[-1])
    small_out = _adamw_small([(flat(w), flat(g), flat(m), flat(v)) for w, g, m, v in zip(ws, gs, ms, vs)])
    d_small = {n: small_out[k][0].reshape(ws[k].shape) for k, n in enumerate(names)}
    nm_small = {n: small_out[k][1].reshape(ws[k].shape) for k, n in enumerate(names)}
    nv_small = {n: small_out[k][2].reshape(ws[k].shape) for k, n in enumerate(names)}
    g_small_d = {n: g.reshape(ws[k].shape) for k, (n, g) in enumerate(zip(names, gs))}

    def ordered(big, small_d):
        win_v, wout_v, w1_v, w2_v = big
        return (small_d["meta"], small_d["norm_mix"], win_v[None], small_d["wg"], small_d["b_gate"],
                small_d["gla_norm"], small_d["sinks"], wout_v[None], small_d["norm_ff"], w1_v[None], w2_v[None],
                small_d["final_norm"])

    return (loss, grad_x,
            *ordered((g_win, g_wout, g_w1s, g_w2s), g_small_d),
            *ordered((d_win, d_wout, d_w1, d_w2), d_small),
            *ordered((nm_win, nm_wout, nm_w1, nm_w2), nm_small),
            *ordered((nv_win, nv_wout, nv_w1, nv_w2), nv_small))
```

```python
import functools

import jax
import jax.numpy as jnp
from jax import lax
from jax.experimental import pallas as pl
from jax.experimental.pallas import tpu as pltpu

F32 = jnp.float32
MXU_DTYPE = jnp.bfloat16
ACT_DTYPE = jnp.bfloat16
WIRE_DTYPE = jnp.bfloat16

D = 1024
N_META = 16
LEAD = 128
META0 = LEAD - N_META
EPS = 1e-5
GLA_HEADS, GLA_DK, GLA_DV, GLA_RANK, GLA_CHUNK = 4, 64, 128, 16, 64
GLA_TAU = 16.0
SWA_HEADS, SWA_KV, SWA_GROUP, SWA_HD, SWA_BLOCK = 8, 2, 4, 64, 128
ROPE_DIM, ROPE_THETA = 16, 500000.0
D_FF = 4096
N_DEV = 8
FF_TILE = D_FF // N_DEV
FF_WIDE = 2048
NEG = -1e30

C_GV, C_GR, C_GQ, C_GK, C_LR, C_SQ, C_SK, C_SV = 0, 512, 1024, 1280, 1536, 1664, 2176, 2304
DGLA = 1664
DINP = 2432
DIN = 2320
O_GQ, O_GK, O_GV, O_GR, O_LR, O_SQ, O_SK, O_SV = (0, 256), (256, 512), (512, 1024), (1024, 1536), (1536, 1552), (1552, 2064), (2064, 2192), (2192, 2320)

ADAM_LR, ADAM_B1, ADAM_B2, ADAM_EPS, ADAM_WD, ADAM_STEP = 0.001, 0.9, 0.999, 1e-08, 0.01, 10

MESH = pl.DeviceIdType.MESH
ANY = pl.BlockSpec(memory_space=pl.ANY)
VMEM_TILE_MB, VMEM_WIDE_MB = 48, 56


def _cp(sem=None, vmem_mb=None):
    kw = {}
    if sem is not None:
        kw["dimension_semantics"] = sem
    if vmem_mb is not None:
        kw["vmem_limit_bytes"] = vmem_mb << 20
    return pltpu.CompilerParams(**kw)


def _mm(a, b):
    return jnp.dot(a.astype(MXU_DTYPE), b.astype(MXU_DTYPE), preferred_element_type=F32)


def _mm_nt(a, b):
    return lax.dot_general(a.astype(MXU_DTYPE), b.astype(MXU_DTYPE), (((1,), (1,)), ((), ())),
                           preferred_element_type=F32)


def _mm_tn(a, b):
    return lax.dot_general(a.astype(MXU_DTYPE), b.astype(MXU_DTYPE), (((0,), (0,)), ((), ())),
                           preferred_element_type=F32)


def _masked_sums(mask, t):
    m = mask.astype(jnp.bfloat16)
    hi = t.astype(jnp.bfloat16)
    rest = t - hi.astype(F32)
    mid = rest.astype(jnp.bfloat16)
    low = (rest - mid.astype(F32)).astype(jnp.bfloat16)
    dot = lambda part: jnp.dot(m, part, preferred_element_type=F32)
    return dot(hi) + (dot(mid) + dot(low))


def _logsigmoid(z):
    return jnp.minimum(z, 0.0) - jnp.log(1.0 + jnp.exp(-jnp.abs(z)))


def _sigmoid(z):
    return 1.0 / (1.0 + jnp.exp(-z))


ROW_TILE, WIDE_ROW_TILE = 640, 1664


def _row_tile(rows, want=ROW_TILE):
    return want if rows % want == 0 else LEAD


def _mesh_pos():
    return lax.axis_index("x"), lax.axis_index("y"), lax.axis_index("c")


def _all_gather(shards):
    n = len(shards)

    def body(*refs):
        start, forward, finish = _gather_schedule(refs[:n], refs[n:2 * n], *refs[2 * n:])
        start()
        for j in range(3):
            forward(j)
        finish()

    gathered = pl.pallas_call(
        body, name="all_gather_weights",
        out_shape=_gathered_shapes(shards), in_specs=[ANY] * n, out_specs=[ANY] * n,
        scratch_shapes=_gather_sems(n),
    )(*shards)
    return _with_own_block(gathered, shards)


def _gathered_shapes(shards):
    return [jax.ShapeDtypeStruct((N_DEV,) + s.shape, s.dtype) for s in shards]


def _gather_sems(n):
    return [pltpu.SemaphoreType.DMA((7 * n,)), pltpu.SemaphoreType.DMA((7 * n,))] if n else []


def _place_gather(step, steps, shard_refs, gathered_refs, sems):
    if not shard_refs:
        return
    start, forward, finish = _gather_schedule(shard_refs, gathered_refs, *sems)
    pl.when(step == 0)(start)
    for j, at in enumerate((steps * 7 // 10, steps * 8 // 10, steps * 9 // 10)):
        pl.when(step == at)(functools.partial(forward, j))
    pl.when(step == steps - 1)(finish)


def _with_own_block(gathered, shards):
    dev = 4 * lax.axis_index("x") + 2 * lax.axis_index("y") + lax.axis_index("c")
    return [lax.dynamic_update_index_in_dim(g, s, dev, 0) for g, s in zip(gathered, shards)]


def _gather_schedule(ins, outs, send_sems, recv_sems):
    n = len(ins)
    x, y, c = _mesh_pos()
    me, sibling = (x, y, c), (x, y, 1 - c)
    chips = [(1 - x, y), (x, 1 - y), (1 - x, 1 - y)]

    def copy(a, k, block, to, src=None):
        dst = outs[a].at[4 * block[0] + 2 * block[1] + block[2]]
        return pltpu.make_async_remote_copy(
            src_ref=dst if src is None else src, dst_ref=dst,
            send_sem=send_sems.at[a * 7 + k], recv_sem=recv_sems.at[a * 7 + k],
            device_id=to, device_id_type=MESH)

    def first(a):
        return [copy(a, 0, me, sibling, src=ins[a])] + [copy(a, 1 + j, me, (*chip, c), src=ins[a])
                                                        for j, chip in enumerate(chips)]

    def start():
        for a in range(n):
            for cp in first(a):
                cp.start()

    def forward(j):
        for a in range(n):
            copy(a, 1 + j, (*chips[j], c), me).wait_recv()
            copy(a, 4 + j, (*chips[j], c), sibling).start()

    def finish():
        for a in range(n):
            copy(a, 0, sibling, me).wait_recv()
            for j, chip in enumerate(chips):
                copy(a, 4 + j, (*chip, 1 - c), me).wait_recv()
        for a in range(n):
            for cp in first(a) + [copy(a, 4 + j, (*chip, c), sibling) for j, chip in enumerate(chips)]:
                cp.wait_send()

    return start, forward, finish


def _sibling_shapes(gs):
    return [jax.ShapeDtypeStruct(g.shape[1:], g.dtype) for g in gs]


def _sibling_sems(n):
    return [pltpu.SemaphoreType.DMA((n,)), pltpu.SemaphoreType.DMA((n,))]


def _sibling_schedule(ins, land, send_sems, recv_sems):
    x, y, c = _mesh_pos()

    def copies():
        return [pltpu.make_async_remote_copy(
            src_ref=ins[a].at[1 - c], dst_ref=land[a], send_sem=send_sems.at[a], recv_sem=recv_sems.at[a],
            device_id=(x, y, 1 - c), device_id_type=MESH) for a in range(len(ins))]

    def start():
        for cp in copies():
            cp.start()

    def finish():
        for cp in copies():
            cp.wait_recv()
        for cp in copies():
            cp.wait_send()

    return start, finish


def _chips_shapes(ps):
    return [jax.ShapeDtypeStruct((3,) + p.shape[1:], p.dtype) for p in ps]


def _chips_sems(n):
    return [pltpu.SemaphoreType.DMA((3 * n,)), pltpu.SemaphoreType.DMA((3 * n,))]


def _chips_schedule(ins, land, send_sems, recv_sems):
    x, y, c = _mesh_pos()
    chips = [(1 - x, y), (x, 1 - y), (1 - x, 1 - y)]

    def copies():
        return [pltpu.make_async_remote_copy(
            src_ref=ins[a].at[2 * chip[0] + chip[1]], dst_ref=land[a].at[j],
            send_sem=send_sems.at[3 * a + j], recv_sem=recv_sems.at[3 * a + j],
            device_id=(*chip, c), device_id_type=MESH) for a in range(len(ins)) for j, chip in enumerate(chips)]

    def start():
        for cp in copies():
            cp.start()

    def finish():
        for cp in copies():
            cp.wait_recv()
        for cp in copies():
            cp.wait_send()

    return start, finish


class _Jobs:
    def __init__(self, jobs):
        self.jobs = jobs
        self.inputs = [a for _, arrs in jobs for a in arrs]
        self.out_shapes = [s for kind, arrs in jobs
                           for s in (_sibling_shapes(arrs) if kind == "sibling" else _chips_shapes(arrs))]
        self.sems = [s for kind, arrs in jobs
                     for s in (_sibling_sems(len(arrs)) if kind == "sibling" else _chips_sems(len(arrs)))]
        self.n = len(self.inputs)

    def bind(self, in_refs, out_refs, sem_refs):
        starts, finishes, at = [], [], 0
        for k, (kind, arrs) in enumerate(self.jobs):
            schedule = _sibling_schedule if kind == "sibling" else _chips_schedule
            start, finish = schedule(in_refs[at:at + len(arrs)], out_refs[at:at + len(arrs)],
                                     sem_refs[2 * k], sem_refs[2 * k + 1])
            starts.append(start)
            finishes.append(finish)
            at += len(arrs)

        def start_all():
            for f in starts:
                f()

        def finish_all():
            for f in finishes:
                f()

        return start_all, finish_all

    def split(self, outs):
        res, at = [], 0
        for _, arrs in self.jobs:
            res.append(list(outs[at:at + len(arrs)]))
            at += len(arrs)
        return res


R_META, R_WG, R_NORM_MIX, R_NORM_FF, R_FINAL, R_B_GATE, R_GLA_NORM, R_LOSS, R_SINKS, SMALL_ROWS = 0, 16, 32, 33, 34, 35, 36, 37, 40, 48


SMALL_SPECS = [pl.BlockSpec((LEAD, D), lambda i: (0, 0)), pl.BlockSpec((128, 256), lambda i: (0, 0)),
               pl.BlockSpec((8, D), lambda i: (0, 0)), pl.BlockSpec((8, D), lambda i: (0, 0)),
               pl.BlockSpec((8, D), lambda i: (0, 0)), pl.BlockSpec((8, 256), lambda i: (0, 0)),
               pl.BlockSpec((8, 128), lambda i: (0, 0)), pl.BlockSpec((8, 128), lambda i: (0, 0)),
               pl.BlockSpec((8, 128), lambda i: (0, 0))]


def _small_sum_scratch():
    return [pltpu.VMEM((SMALL_ROWS, D), F32), pltpu.VMEM((N_DEV, SMALL_ROWS, D), F32),
            pltpu.SemaphoreType.DMA((7,)), pltpu.SemaphoreType.DMA((7,))]


def _small_sum_schedule(small_refs, out_ref, p_ref, land, send_sems, recv_sems):
    dlead_ref, dwg_ref, gnm_ref, gnf_ref, gfn_ref, dbg_ref, dgnw_ref, loss_ref, dsink_ref = small_refs
    x, y, c = _mesh_pos()
    me = 4 * x + 2 * y + c

    def copies():
        res = []
        for k in range(1, N_DEV):
            bx, by, bc = (k >> 2) & 1, (k >> 1) & 1, k & 1
            peer = (1 - x if bx else x, 1 - y if by else y, 1 - c if bc else c)
            res.append(pltpu.make_async_remote_copy(
                src_ref=p_ref, dst_ref=land.at[me], send_sem=send_sems.at[k - 1], recv_sem=recv_sems.at[k - 1],
                device_id=peer, device_id_type=MESH))
        return res

    def start():
        p_ref[...] = jnp.zeros_like(p_ref)
        p_ref[R_META:R_META + N_META, :] = dlead_ref[META0:LEAD, :]
        p_ref[R_WG:R_WG + GLA_RANK, 0:256] = dwg_ref[0:GLA_RANK, :]
        p_ref[R_NORM_MIX:R_NORM_MIX + 1, :] = gnm_ref[0:1, :]
        p_ref[R_NORM_FF:R_NORM_FF + 1, :] = gnf_ref[0:1, :]
        p_ref[R_FINAL:R_FINAL + 1, :] = gfn_ref[0:1, :]
        p_ref[R_B_GATE:R_B_GATE + 1, 0:256] = dbg_ref[0:1, :]
        p_ref[R_GLA_NORM:R_GLA_NORM + 1, 0:128] = dgnw_ref[0:1, :]
        p_ref[R_LOSS:R_LOSS + 1, 0:128] = loss_ref[0:1, :]
        p_ref[R_SINKS:R_SINKS + SWA_HEADS, 0:128] = dsink_ref[...]
        land[me] = p_ref[...]
        for cp in copies():
            cp.start()

    def finish():
        for cp in copies():
            cp.wait_recv()
        for cp in copies():
            cp.wait_send()
        acc = land[0]
        for d in range(1, N_DEV):
            acc = acc + land[d]
        out_ref[...] = acc

    return start, finish


def _token_specs(tm, grid_rank=1):
    nb = tm // LEAD

    def spec(k):
        if grid_rank == 1:
            return pl.BlockSpec((LEAD, D), lambda i: (jnp.maximum(i * nb + k - 1, 0), 0))
        return pl.BlockSpec((LEAD, D), lambda i, j: (jnp.maximum(i * nb + k - 1, 0), 0))

    return [spec(k) for k in range(nb)]


def _h_tile(i, lead_ref, x_refs):
    first = jnp.where(i == 0, lead_ref[...], x_refs[0][...])
    return jnp.concatenate([first] + [r[...] for r in x_refs[1:]], axis=0)


def _in_proj(x, lead, nw, win_p, angles, tm, shards):
    rows = LEAD + x.shape[0]
    nb = tm // LEAD
    steps = rows // tm
    ns = len(shards)

    def body(*refs):
        x_refs, refs = refs[:nb], refs[nb:]
        lead_ref, nw_ref, w_ref, cs_ref = refs[:4]
        shard_refs, (o_ref, q_ref, k_ref, v_ref) = refs[4:4 + ns], refs[4 + ns:8 + ns]
        _place_gather(pl.program_id(0), steps, shard_refs, refs[8 + ns:8 + 2 * ns], refs[8 + 2 * ns:])
        h = _h_tile(pl.program_id(0), lead_ref, x_refs)
        rstd = lax.rsqrt(jnp.mean(h * h, axis=-1, keepdims=True) + EPS)
        u = (h * rstd * nw_ref[...]).astype(MXU_DTYPE)
        proj = jnp.dot(u, w_ref[...].astype(MXU_DTYPE), preferred_element_type=F32)
        o_ref[...] = proj[:, 0:DGLA]
        cos, sa, sb = _rope_tables(cs_ref[...])
        q_ref[...] = (_rope(proj[:, C_SQ:C_SK], cos, sa, sb) * (SWA_HD ** -0.5)).astype(ACT_DTYPE)
        k_ref[...] = _rope(proj[:, C_SK:C_SV], cos, sa, sb).astype(ACT_DTYPE)
        v_ref[...] = proj[:, C_SV:DINP].astype(ACT_DTYPE)

    row = lambda w: pl.BlockSpec((tm, w), lambda i: (i, 0))
    outs = pl.pallas_call(
        body, name="in_proj", grid=(steps,),
        in_specs=_token_specs(tm) + [pl.BlockSpec((LEAD, D), lambda i: (0, 0)), pl.BlockSpec((1, D), lambda i: (0, 0)),
                                     pl.BlockSpec((D, DINP), lambda i: (0, 0)), row(ROPE_DIM)]
        + [ANY] * ns,
        out_specs=[row(DGLA), row(512), row(128), row(128)] + [ANY] * ns,
        out_shape=[jax.ShapeDtypeStruct((rows, DGLA), F32), jax.ShapeDtypeStruct((rows, 512), ACT_DTYPE),
                   jax.ShapeDtypeStruct((rows, 128), ACT_DTYPE), jax.ShapeDtypeStruct((rows, 128), ACT_DTYPE)]
        + _gathered_shapes(shards),
        scratch_shapes=_gather_sems(ns),
        compiler_params=_cp(("arbitrary",), VMEM_WIDE_MB),
    )(*([x] * nb), lead, nw, win_p, angles, *shards)
    return outs[0], outs[1], outs[2], outs[3], _with_own_block(outs[4:], shards)


def _rope_angles(rows):
    pos = (jnp.arange(rows, dtype=jnp.int32) - META0).astype(F32)
    inv_freq = 1.0 / (ROPE_THETA ** (jnp.arange(0, ROPE_DIM, 2, dtype=F32) / ROPE_DIM))
    ang = pos[:, None] * inv_freq[None, :]
    return jnp.concatenate([jnp.cos(ang), jnp.sin(ang)], axis=1)


def _rope_tables(cs):
    shape = (2 * (ROPE_DIM // 2), 3 * 128)
    j = lax.broadcasted_iota(jnp.int32, shape, 0)
    col = lax.broadcasted_iota(jnp.int32, shape, 1)
    table, in_head = col // 128, col % SWA_HD
    match = (j % (ROPE_DIM // 2)) == (in_head % (ROPE_DIM // 2))
    is_cos = j < ROPE_DIM // 2
    first, second = in_head < ROPE_DIM // 2, (in_head >= ROPE_DIM // 2) & (in_head < ROPE_DIM)
    spread = (jnp.where(match & is_cos & (table == 0) & (first | second), 1.0, 0.0)
              + jnp.where(match & ~is_cos & (table == 1) & first, -1.0, 0.0)
              + jnp.where(match & ~is_cos & (table == 2) & second, 1.0, 0.0)).astype(jnp.bfloat16)
    hi = cs.astype(jnp.bfloat16)
    rest = cs - hi.astype(F32)
    mid = rest.astype(jnp.bfloat16)
    low = (rest - mid.astype(F32)).astype(jnp.bfloat16)
    dot = lambda part: jnp.dot(part, spread, preferred_element_type=F32)
    tabs = dot(hi) + (dot(mid) + dot(low))
    lane = lax.broadcasted_iota(jnp.int32, (1, 128), 1) % SWA_HD
    return tabs[:, 0:128] + jnp.where(lane >= ROPE_DIM, 1.0, 0.0), tabs[:, 128:256], tabs[:, 256:384]


def _rope(xv, cos, sa, sb):
    width = xv.shape[1]
    reps = width // 128
    if reps > 1:
        cos, sa, sb = (jnp.tile(t, (1, reps)) for t in (cos, sa, sb))
    return xv * cos + pltpu.roll(xv, width - 8, 1) * sa + pltpu.roll(xv, 8, 1) * sb


def _unrope(dy, cos, sa, sb):
    width = dy.shape[1]
    reps = width // 128
    if reps > 1:
        cos, sa, sb = (jnp.tile(t, (1, reps)) for t in (cos, sa, sb))
    return dy * cos + pltpu.roll(dy * sa, 8, 1) + pltpu.roll(dy * sb, width - 8, 1)


def _gla_group(nc, most):
    for g in (10, 5, 2):
        if g <= most and nc % g == 0:
            return g
    return 1


def _chunk_masks(nrows):
    ii = lax.broadcasted_iota(jnp.int32, (nrows, nrows), 0)
    jj = lax.broadcasted_iota(jnp.int32, (nrows, nrows), 1)
    same = (ii // GLA_CHUNK) == (jj // GLA_CHUNK)
    return same & (jj <= ii), same & (jj >= ii)


def _gla_gates(lr, wg, bg, first_row):
    nrows = lr.shape[0]
    zg = _mm(lr, wg) + bg
    live = first_row + lax.broadcasted_iota(jnp.int32, (nrows, 1), 0) >= META0
    g = jnp.where(live, _logsigmoid(zg) * (1.0 / GLA_TAU), 0.0)
    return _masked_sums(_chunk_masks(nrows)[0], g), jnp.where(live, _sigmoid(-zg) * (1.0 / GLA_TAU), 0.0)


def _tril64():
    ii = lax.broadcasted_iota(jnp.int32, (GLA_CHUNK, GLA_CHUNK), 0)
    jj = lax.broadcasted_iota(jnp.int32, (GLA_CHUNK, GLA_CHUNK), 1)
    return jj <= ii


def _gla_fwd(proj, wg_p, bg, gnw, shards):
    rows = proj.shape[0]
    nc = rows // GLA_CHUNK
    group = _gla_group(nc, 10)
    steps, nrows = nc // group, group * GLA_CHUNK
    ns = len(shards)

    def body(q_ref, k_ref, v_ref, r_ref, lr_ref, lr_next_ref, wg_ref, bg_ref, gnw_ref, *rest):
        shard_refs, rest = rest[:ns], rest[ns:]
        oraw_ref, og_ref, st_ref, decay_ref, dgate_ref = rest[:5]
        gathered_refs, rest = rest[5:5 + ns], rest[5 + ns:]
        state, gates = rest[:2]
        c = pl.program_id(0)

        @pl.when(c == 0)
        def _():
            state[...] = jnp.zeros_like(state)
            gates[0, 0], gates[0, 1] = _gla_gates(lr_ref[...], wg_ref[...], bg_ref[...], 0)

        _place_gather(c, steps, shard_refs, gathered_refs, rest[2:])
        slot = c % 2
        b = gates[slot, 0]
        decay_ref[...] = b
        dgate_ref[...] = gates[slot, 1]
        gates[1 - slot, 0], gates[1 - slot, 1] = _gla_gates(lr_next_ref[...], wg_ref[...], bg_ref[...], (c + 1) * nrows)
        eb = jnp.exp(b)
        gq = q_ref[...] * (GLA_DK ** -0.5) * eb
        gk = k_ref[...] * jnp.exp(-b)
        v = v_ref[...]
        gnw_v = gnw_ref[...]
        tril = _tril64()
        pairs = [(h, gi) for h in range(GLA_HEADS) for gi in range(group)]
        rs = {gi: slice(gi * GLA_CHUNK, (gi + 1) * GLA_CHUNK) for gi in range(group)}
        s64 = {h: slice(h * GLA_DK, (h + 1) * GLA_DK) for h in range(GLA_HEADS)}
        s128 = {h: slice(h * GLA_DV, (h + 1) * GLA_DV) for h in range(GLA_HEADS)}
        qh = {(h, gi): gq[rs[gi], s64[h]] for h, gi in pairs}
        kh = {(h, gi): gk[rs[gi], s64[h]] for h, gi in pairs}
        vh = {(h, gi): v[rs[gi], s128[h]] for h, gi in pairs}
        ebl = {(h, gi): eb[(gi + 1) * GLA_CHUNK - 1:(gi + 1) * GLA_CHUNK, s64[h]] for h, gi in pairs}
        av = {pr: _mm(jnp.where(tril, _mm_nt(qh[pr], kh[pr]), 0.0), vh[pr]) for pr in pairs}
        inc = {pr: _mm_tn(vh[pr], kh[pr] * ebl[pr]) for pr in pairs}
        st = {}
        for h in range(GLA_HEADS):
            cur = state[h]
            for gi in range(group):
                st[h, gi] = cur
                st_ref[gi, h] = cur
                cur = cur * ebl[h, gi] + inc[h, gi]
            state[h] = cur
        for h, gi in pairs:
            o = av[h, gi] + _mm_nt(qh[h, gi], st[h, gi])
            oraw_ref[rs[gi], s128[h]] = o
            rstd = lax.rsqrt(jnp.mean(o * o, axis=-1, keepdims=True) + EPS)
            rh = r_ref[rs[gi], s128[h]]
            og_ref[rs[gi], s128[h]] = (o * rstd * gnw_v * (rh * _sigmoid(rh))).astype(ACT_DTYPE)

    nb = lambda w, col: pl.BlockSpec((nrows, w), lambda c: (c, col // w))
    const = lambda shape: pl.BlockSpec(shape, lambda c: (0,) * len(shape))
    outs = pl.pallas_call(
        body, name="gla_fwd", grid=(steps,),
        in_specs=[nb(256, C_GQ), nb(256, C_GK), nb(512, C_GV), nb(512, C_GR), nb(128, C_LR),
                  pl.BlockSpec((nrows, 128), lambda c: (jnp.minimum(c + 1, steps - 1), C_LR // 128)),
                  const((128, 256)), const((1, 256)), const((1, 128))] + [ANY] * ns,
        out_specs=[nb(512, 0), nb(512, 0),
                   pl.BlockSpec((group, GLA_HEADS, GLA_DV, GLA_DK), lambda c: (c, 0, 0, 0)),
                   nb(256, 0), nb(256, 0)] + [ANY] * ns,
        out_shape=[jax.ShapeDtypeStruct((rows, 512), F32), jax.ShapeDtypeStruct((rows, 512), ACT_DTYPE),
                   jax.ShapeDtypeStruct((nc, GLA_HEADS, GLA_DV, GLA_DK), F32),
                   jax.ShapeDtypeStruct((rows, 256), F32), jax.ShapeDtypeStruct((rows, 256), F32)]
        + _gathered_shapes(shards),
        scratch_shapes=[pltpu.VMEM((GLA_HEADS, GLA_DV, GLA_DK), F32), pltpu.VMEM((2, 2, nrows, 256), F32)]
        + _gather_sems(ns),
        compiler_params=_cp(("arbitrary",)),
    )(proj, proj, proj, proj, proj, proj, wg_p, bg, gnw, *shards)
    return outs[0], outs[1], outs[2], outs[3], outs[4], _with_own_block(outs[5:], shards)


def _swa_mask(n):
    shape = (SWA_GROUP * SWA_BLOCK, 3 * SWA_BLOCK)
    qi = lax.broadcasted_iota(jnp.int32, shape, 0) & (SWA_BLOCK - 1)
    jj = lax.broadcasted_iota(jnp.int32, shape, 1)
    meta = (jj < SWA_BLOCK) & (jj >= META0) & ((n > 0) | (jj <= qi))
    prev = (jj >= SWA_BLOCK) & (jj < 2 * SWA_BLOCK) & (n >= 2) & (jj - SWA_BLOCK > qi)
    cur = (jj >= 2 * SWA_BLOCK) & (n >= 1) & (jj - 2 * SWA_BLOCK <= qi)
    return meta | prev | cur


def _stack_heads(t, kvh):
    return jnp.concatenate([t[:, (kvh * SWA_GROUP + g) * SWA_HD:(kvh * SWA_GROUP + g + 1) * SWA_HD]
                            for g in range(SWA_GROUP)], axis=0)


def _stack_sinks(sink_ref, kvh):
    return jnp.concatenate([jnp.full((SWA_BLOCK, 1), sink_ref[0, kvh * SWA_GROUP + g], F32)
                            for g in range(SWA_GROUP)], axis=0)


def _swa_group(nblk):
    return 5 if nblk % 5 == 0 else 1


def _swa_specs(group):
    blk = lambda w: pl.BlockSpec((group * SWA_BLOCK, w), lambda n: (n, 0))
    first = pl.BlockSpec((SWA_BLOCK, 128), lambda n: (0, 0))
    prev = pl.BlockSpec((SWA_BLOCK, 128), lambda n: (jnp.maximum(n * group - 1, 0), 0))
    return blk, first, prev


def _swa_keys(first_ref, prev_ref, cur_ref, g):
    own = cur_ref[g * SWA_BLOCK:(g + 1) * SWA_BLOCK, :]
    before = prev_ref[...] if g == 0 else cur_ref[(g - 1) * SWA_BLOCK:g * SWA_BLOCK, :]
    return jnp.concatenate([first_ref[...], before, own], axis=0)


def _swa_fwd(qr, kr, vr, sinks, shards):
    rows = qr.shape[0]
    nblk = rows // SWA_BLOCK
    group = _swa_group(nblk)
    steps = nblk // group
    ns = len(shards)

    def body(q_ref, k0, kp, kc, v0, vp, vc, sink_ref, *rest):
        o_ref = rest[ns]
        _place_gather(pl.program_id(0), steps, rest[:ns], rest[ns + 1:2 * ns + 1], rest[2 * ns + 1:])
        for g in range(group):
            n = pl.program_id(0) * group + g
            rs = slice(g * SWA_BLOCK, (g + 1) * SWA_BLOCK)
            kall, vall = _swa_keys(k0, kp, kc, g), _swa_keys(v0, vp, vc, g)
            mask = _swa_mask(n)[0:SWA_BLOCK]
            heads = range(SWA_HEADS)
            hs = [slice(h * SWA_HD, (h + 1) * SWA_HD) for h in heads]
            kv = [slice((h // SWA_GROUP) * SWA_HD, (h // SWA_GROUP + 1) * SWA_HD) for h in heads]
            s = [jnp.where(mask, _mm_nt(q_ref[rs, hs[h]], kall[:, kv[h]]), NEG) for h in heads]
            m = [jnp.maximum(jnp.max(s[h], axis=-1, keepdims=True), sink_ref[0, h]) for h in heads]
            p = [jnp.exp(s[h] - m[h]) for h in heads]
            den = [jnp.sum(p[h], axis=-1, keepdims=True) + jnp.exp(sink_ref[0, h] - m[h]) for h in heads]
            o = [_mm(p[h], vall[:, kv[h]]) for h in heads]
            for h in heads:
                o_ref[rs, hs[h]] = (o[h] / den[h]).astype(ACT_DTYPE)

    blk, first, prev = _swa_specs(group)
    outs = pl.pallas_call(
        body, name="swa_fwd", grid=(steps,),
        in_specs=[blk(512), first, prev, blk(128), first, prev, blk(128),
                  pl.BlockSpec(memory_space=pltpu.SMEM)] + [ANY] * ns,
        out_specs=[blk(512)] + [ANY] * ns,
        out_shape=[jax.ShapeDtypeStruct((rows, 512), ACT_DTYPE)] + _gathered_shapes(shards),
        scratch_shapes=_gather_sems(ns),
        compiler_params=_cp(("arbitrary",)),
    )(qr, kr, kr, kr, vr, vr, vr, sinks, *shards)
    return outs[0], _with_own_block(outs[1:], shards)


def _out_proj(x, lead, og, osw, wout, nfw, tm):
    rows = LEAD + x.shape[0]
    nb = tm // LEAD

    def body(*refs):
        x_refs, (lead_ref, og_ref, os_ref, w_ref, nw_ref, h1_ref, f_ref, ft_ref) = refs[:nb], refs[nb:]
        h0 = _h_tile(pl.program_id(0), lead_ref, x_refs)
        h1 = h0 + _mm(og_ref[...], w_ref[0:512, :]) + _mm(os_ref[...], w_ref[512:1024, :])
        h1_ref[...] = h1
        rstd = lax.rsqrt(jnp.mean(h1 * h1, axis=-1, keepdims=True) + EPS)
        f = h1 * rstd * nw_ref[...]
        f_ref[...] = f.astype(ACT_DTYPE)
        ft_ref[...] = f.T.astype(ACT_DTYPE)

    row = lambda w: pl.BlockSpec((tm, w), lambda i: (i, 0))
    return pl.pallas_call(
        body, name="out_proj", grid=(rows // tm,),
        in_specs=_token_specs(tm) + [pl.BlockSpec((LEAD, D), lambda i: (0, 0)), row(512), row(512),
                                     pl.BlockSpec((D, D), lambda i: (0, 0)), pl.BlockSpec((1, D), lambda i: (0, 0))],
        out_specs=[row(D), row(D), pl.BlockSpec((D, tm), lambda i: (0, i))],
        out_shape=[jax.ShapeDtypeStruct((rows, D), F32), jax.ShapeDtypeStruct((rows, D), ACT_DTYPE),
                   jax.ShapeDtypeStruct((D, rows), ACT_DTYPE)],
        compiler_params=_cp(("arbitrary",), VMEM_TILE_MB),
    )(*([x] * nb), lead, og, osw, wout, nfw)


def _ffn_fwd(f, h1, w1, w2, tgt, fnw, tm):
    rows = f.shape[0]
    nj = D_FF // FF_WIDE
    nb = tm // LEAD

    def body(f_ref, h1_ref, w1_ref, w2_ref, nw_ref, *rest):
        t_refs, (a_ref, dh2_ref, dh2t_ref, loss_ref, gfn_ref, acc) = rest[:nb], rest[nb:]
        i, j = pl.program_id(0), pl.program_id(1)

        @pl.when((i == 0) & (j == 0))
        def _():
            loss_ref[...] = jnp.zeros_like(loss_ref)
            gfn_ref[...] = jnp.zeros_like(gfn_ref)

        @pl.when(j == 0)
        def _():
            acc[...] = jnp.zeros_like(acc)

        a = _mm(f_ref[...], w1_ref[...])
        a_ref[...] = a.astype(ACT_DTYPE)
        z = jnp.square(jnp.maximum(a, 0.0))
        acc[...] += _mm(z, w2_ref[...])

        @pl.when(j == nj - 1)
        def _():
            h2 = h1_ref[...] + acc[...]
            rstd = lax.rsqrt(jnp.mean(h2 * h2, axis=-1, keepdims=True) + EPS)
            hn = h2 * rstd
            nw = nw_ref[...]
            row = i * tm + lax.broadcasted_iota(jnp.int32, (tm, 1), 0)
            target = jnp.concatenate([t[...] for t in t_refs], axis=0)
            err = jnp.where(row >= LEAD, hn * nw - target, 0.0)
            row_loss = jnp.sum(err * err, axis=-1, keepdims=True) * (1.0 / D)
            loss_ref[...] += jnp.broadcast_to(0.5 * jnp.sum(row_loss, axis=0, keepdims=True), loss_ref.shape)
            dy = err * (1.0 / D)
            gfn_ref[...] += jnp.broadcast_to(jnp.sum(dy * hn, axis=0, keepdims=True), gfn_ref.shape)
            dhn = dy * nw
            dh2 = rstd * (dhn - hn * jnp.mean(dhn * hn, axis=-1, keepdims=True))
            dh2_ref[...] = dh2
            dh2t_ref[...] = dh2.T.astype(ACT_DTYPE)

    return pl.pallas_call(
        body, name="ffn_fwd", grid=(rows // tm, nj),
        in_specs=[pl.BlockSpec((tm, D), lambda i, j: (i, 0)), pl.BlockSpec((tm, D), lambda i, j: (i, 0)),
                  pl.BlockSpec((D, FF_WIDE), lambda i, j: (0, j)),
                  pl.BlockSpec((FF_WIDE, D), lambda i, j: (j, 0)),
                  pl.BlockSpec((1, D), lambda i, j: (0, 0))] + _token_specs(tm, grid_rank=2),
        out_specs=[pl.BlockSpec((tm, FF_WIDE), lambda i, j: (i, j)), pl.BlockSpec((tm, D), lambda i, j: (i, 0)),
                   pl.BlockSpec((D, tm), lambda i, j: (0, i)),
                   pl.BlockSpec((8, 128), lambda i, j: (0, 0)), pl.BlockSpec((8, D), lambda i, j: (0, 0))],
        out_shape=[jax.ShapeDtypeStruct((rows, D_FF), ACT_DTYPE), jax.ShapeDtypeStruct((rows, D), F32),
                   jax.ShapeDtypeStruct((D, rows), ACT_DTYPE),
                   jax.ShapeDtypeStruct((8, 128), F32), jax.ShapeDtypeStruct((8, D), F32)],
        scratch_shapes=[pltpu.VMEM((tm, D), F32)],
        compiler_params=_cp(("arbitrary", "arbitrary"), VMEM_WIDE_MB),
    )(f, h1, w1, w2, fnw, *([tgt] * nb))


def _ffn_bwd_act(dh2, a, w1, w2, h1, nfw, tm):
    rows = dh2.shape[0]
    nj = D_FF // FF_WIDE

    def body(dh2_ref, a_ref, w1_ref, w2_ref, h1_ref, nw_ref, da_ref, dh1_ref, gnf_ref, acc):
        i, j = pl.program_id(0), pl.program_id(1)

        @pl.when((i == 0) & (j == 0))
        def _():
            gnf_ref[...] = jnp.zeros_like(gnf_ref)

        @pl.when(j == 0)
        def _():
            acc[...] = jnp.zeros_like(acc)

        dz = _mm_nt(dh2_ref[...], w2_ref[...])
        da = dz * (2.0 * jnp.maximum(a_ref[...].astype(F32), 0.0))
        da_ref[...] = da.astype(ACT_DTYPE)
        acc[...] += _mm_nt(da, w1_ref[...])

        @pl.when(j == nj - 1)
        def _():
            h1 = h1_ref[...]
            rstd = lax.rsqrt(jnp.mean(h1 * h1, axis=-1, keepdims=True) + EPS)
            hn = h1 * rstd
            df = acc[...]
            gnf_ref[...] += jnp.broadcast_to(jnp.sum(df * hn, axis=0, keepdims=True), gnf_ref.shape)
            dfn = df * nw_ref[...]
            dh1_ref[...] = dh2_ref[...] + rstd * (dfn - hn * jnp.mean(dfn * hn, axis=-1, keepdims=True))

    return pl.pallas_call(
        body, name="ffn_bwd_act", grid=(rows // tm, nj),
        in_specs=[pl.BlockSpec((tm, D), lambda i, j: (i, 0)), pl.BlockSpec((tm, FF_WIDE), lambda i, j: (i, j)),
                  pl.BlockSpec((D, FF_WIDE), lambda i, j: (0, j)),
                  pl.BlockSpec((FF_WIDE, D), lambda i, j: (j, 0)),
                  pl.BlockSpec((tm, D), lambda i, j: (i, 0)), pl.BlockSpec((1, D), lambda i, j: (0, 0))],
        out_specs=[pl.BlockSpec((tm, FF_WIDE), lambda i, j: (i, j)), pl.BlockSpec((tm, D), lambda i, j: (i, 0)),
                   pl.BlockSpec((8, D), lambda i, j: (0, 0))],
        out_shape=[jax.ShapeDtypeStruct((rows, D_FF), ACT_DTYPE), jax.ShapeDtypeStruct((rows, D), F32),
                   jax.ShapeDtypeStruct((8, D), F32)],
        scratch_shapes=[pltpu.VMEM((tm, D), F32)],
        compiler_params=_cp(("arbitrary", "arbitrary"), VMEM_WIDE_MB),
    )(dh2, a, w1, w2, h1, nfw)


def _ffn_bwd_weights(ft, a, da, dh2t, tm):
    rows = a.shape[0]
    steps = rows // tm
    pair = 2 * FF_TILE

    def body(ft_ref, a_ref, da_ref, dh2t_ref, dw1_ref, dw2_ref, dw2t):
        i = pl.program_id(1)

        @pl.when(i == 0)
        def _():
            dw1_ref[...] = jnp.zeros_like(dw1_ref)
            dw2t[...] = jnp.zeros_like(dw2t)

        z = jnp.square(jnp.maximum(a_ref[...].astype(F32), 0.0))
        dw1 = _mm(ft_ref[...], da_ref[...])
        for core in range(2):
            dw1_ref[core] += dw1[:, core * FF_TILE:(core + 1) * FF_TILE]
        dw2t[...] += _mm(dh2t_ref[...], z)

        @pl.when(i == steps - 1)
        def _():
            for core in range(2):
                dw2_ref[core] = dw2t[:, core * FF_TILE:(core + 1) * FF_TILE].T

    return pl.pallas_call(
        body, name="ffn_bwd_weights", grid=(N_DEV // 2, steps),
        in_specs=[pl.BlockSpec((D, tm), lambda j, i: (0, i)), pl.BlockSpec((tm, pair), lambda j, i: (i, j)),
                  pl.BlockSpec((tm, pair), lambda j, i: (i, j)), pl.BlockSpec((D, tm), lambda j, i: (0, i))],
        out_specs=[pl.BlockSpec((2, None, D, FF_TILE), lambda j, i: (0, j, 0, 0)),
                   pl.BlockSpec((2, None, FF_TILE, D), lambda j, i: (0, j, 0, 0))],
        out_shape=[jax.ShapeDtypeStruct((2, 4, D, FF_TILE), F32), jax.ShapeDtypeStruct((2, 4, FF_TILE, D), F32)],
        scratch_shapes=[pltpu.VMEM((D, pair), F32)],
        compiler_params=_cp(("arbitrary", "arbitrary"), VMEM_WIDE_MB),
    )(ft, a, da, dh2t)


def _out_proj_bwd(dh1, og, osw, wout, tm, partials):
    rows = dh1.shape[0]
    steps = rows // tm
    ns = len(partials)

    def body(dh1_ref, og_ref, os_ref, w_ref, *rest):
        part_refs, rest = rest[:ns], rest[ns:]
        dog_ref, dos_ref, dw_ref = rest[:3]
        land_refs, (send_sems, recv_sems) = rest[3:3 + ns], rest[3 + ns:]
        i = pl.program_id(0)
        start, finish = _sibling_schedule(part_refs, land_refs, send_sems, recv_sems)

        @pl.when(i == 0)
        def _():
            dw_ref[...] = jnp.zeros_like(dw_ref)
            start()

        pl.when(i == steps - 1)(finish)

        dh1 = dh1_ref[...].astype(MXU_DTYPE)
        dog_ref[...] = _mm_nt(dh1, w_ref[0:512, :])
        dos_ref[...] = _mm_nt(dh1, w_ref[512:1024, :])
        for half, ref in enumerate((og_ref, os_ref)):
            dw = _mm_tn(ref[...], dh1)
            for blk in range(4):
                shard = half * 4 + blk
                dw_ref[shard % 2, shard // 2] += dw[blk * 128:(blk + 1) * 128, :]

    row = lambda w: pl.BlockSpec((tm, w), lambda i: (i, 0))
    outs = pl.pallas_call(
        body, name="out_proj_bwd", grid=(steps,),
        in_specs=[row(D), row(512), row(512), pl.BlockSpec((D, D), lambda i: (0, 0))] + [ANY] * ns,
        out_specs=[row(512), row(512), pl.BlockSpec((2, 4, 128, D), lambda i: (0, 0, 0, 0))] + [ANY] * ns,
        out_shape=[jax.ShapeDtypeStruct((rows, 512), F32), jax.ShapeDtypeStruct((rows, 512), F32),
                   jax.ShapeDtypeStruct((2, 4, 128, D), F32)] + _sibling_shapes(partials),
        scratch_shapes=_sibling_sems(ns),
        compiler_params=_cp(("arbitrary",), VMEM_TILE_MB),
    )(dh1, og, osw, wout, *partials)
    return outs[0], outs[1], outs[2], outs[3:]


def _swa_bwd(qr, kr, vr, osw, dos, sinks, jobs):
    rows = qr.shape[0]
    nblk = rows // SWA_BLOCK
    group = _swa_group(nblk)
    steps = nblk // group
    ns = jobs.n

    def body(q_ref, k0, kp, kc, v0, vp, vc, o_ref, do_ref, sink_ref, *rest):
        dq_ref, dk_ref, dv_ref, dsink_ref = rest[ns:ns + 4]
        start, finish = jobs.bind(rest[:ns], rest[ns + 4:2 * ns + 4], rest[2 * ns + 4:])
        step = pl.program_id(0)

        @pl.when(step == 0)
        def _():
            dk_ref[...] = jnp.zeros_like(dk_ref)
            dv_ref[...] = jnp.zeros_like(dv_ref)
            dsink_ref[...] = jnp.zeros_like(dsink_ref)
            start()

        pl.when(step == steps - 1)(finish)
        for g in range(group):
            block(step * group + g, g, q_ref, k0, kp, kc, v0, vp, vc, o_ref, do_ref, sink_ref,
                  dq_ref, dk_ref, dv_ref, dsink_ref)

    def block(n, g, q_ref, k0, kp, kc, v0, vp, vc, o_ref, do_ref, sink_ref, dq_ref, dk_ref, dv_ref, dsink_ref):
        rs = slice(g * SWA_BLOCK, (g + 1) * SWA_BLOCK)
        kall, vall = _swa_keys(k0, kp, kc, g), _swa_keys(v0, vp, vc, g)
        mask = _swa_mask(n)[0:SWA_BLOCK]
        heads = range(SWA_HEADS)
        hs = [slice(h * SWA_HD, (h + 1) * SWA_HD) for h in heads]
        kv = [slice((h // SWA_GROUP) * SWA_HD, (h // SWA_GROUP + 1) * SWA_HD) for h in heads]
        sink = [sink_ref[0, h] for h in heads]
        qh = [q_ref[rs, hs[h]] for h in heads]
        doh = [do_ref[rs, hs[h]] for h in heads]
        s = [jnp.where(mask, _mm_nt(qh[h], kall[:, kv[h]]), NEG) for h in heads]
        dp = [_mm_nt(doh[h], vall[:, kv[h]]) for h in heads]
        delta = [jnp.sum(doh[h] * o_ref[rs, hs[h]].astype(F32), axis=-1, keepdims=True) for h in heads]
        m = [jnp.maximum(jnp.max(s[h], axis=-1, keepdims=True), sink[h]) for h in heads]
        e = [jnp.exp(s[h] - m[h]) for h in heads]
        inv = [1.0 / (jnp.sum(e[h], axis=-1, keepdims=True) + jnp.exp(sink[h] - m[h])) for h in heads]
        p = [e[h] * inv[h] for h in heads]
        ds = [p[h] * (dp[h] - delta[h]) for h in heads]
        dq = [_mm(ds[h], kall[:, kv[h]]) for h in heads]
        dkh = [_mm_tn(ds[h], qh[h]) for h in heads]
        dvh = [_mm_tn(p[h], doh[h]) for h in heads]
        for h in heads:
            dsink = -jnp.sum(jnp.exp(sink[h] - m[h]) * inv[h] * delta[h], axis=0, keepdims=True)
            dsink_ref[h:h + 1, :] += jnp.broadcast_to(dsink, (1, 128))
        dq_ref[rs, :] = jnp.concatenate(dq, axis=1).astype(ACT_DTYPE)
        group_sum = lambda parts, kvh: sum(parts[kvh * SWA_GROUP + 1:(kvh + 1) * SWA_GROUP], parts[kvh * SWA_GROUP])
        dk_all = jnp.concatenate([group_sum(dkh, kvh) for kvh in range(SWA_KV)], axis=1)
        dv_all = jnp.concatenate([group_sum(dvh, kvh) for kvh in range(SWA_KV)], axis=1)
        prev0 = pl.multiple_of(jnp.maximum(n - 1, 0) * SWA_BLOCK, SWA_BLOCK)
        cur0 = pl.multiple_of(n * SWA_BLOCK, SWA_BLOCK)
        for ref, val in ((dk_ref, dk_all), (dv_ref, dv_all)):
            ref[0:SWA_BLOCK, :] += val[0:SWA_BLOCK]
            ref[pl.ds(prev0, SWA_BLOCK), :] += val[SWA_BLOCK:2 * SWA_BLOCK]
            ref[pl.ds(cur0, SWA_BLOCK), :] += val[2 * SWA_BLOCK:]

    blk, first, prev = _swa_specs(group)
    whole = pl.BlockSpec((rows, 128), lambda n: (0, 0))
    outs = pl.pallas_call(
        body, name="swa_bwd", grid=(steps,),
        in_specs=[blk(512), first, prev, blk(128), first, prev, blk(128), blk(512), blk(512),
                  pl.BlockSpec(memory_space=pltpu.SMEM)] + [ANY] * ns,
        out_specs=[blk(512), whole, whole, pl.BlockSpec((8, 128), lambda n: (0, 0))] + [ANY] * ns,
        out_shape=[jax.ShapeDtypeStruct((rows, 512), ACT_DTYPE), jax.ShapeDtypeStruct((rows, 128), F32),
                   jax.ShapeDtypeStruct((rows, 128), F32), jax.ShapeDtypeStruct((8, 128), F32)] + jobs.out_shapes,
        scratch_shapes=jobs.sems,
        compiler_params=_cp(("arbitrary",), VMEM_TILE_MB),
    )(qr, kr, kr, kr, vr, vr, vr, osw, dos, sinks, *jobs.inputs)
    return outs[0], outs[1], outs[2], outs[3], jobs.split(outs[4:])


def _gla_bwd(proj, decay, dgate, oraw, states, dog, wg_p, gnw, jobs):
    rows = proj.shape[0]
    nc = rows // GLA_CHUNK
    group = _gla_group(nc, 5)
    steps, nrows = nc // group, group * GLA_CHUNK
    ns = jobs.n

    def body(q_ref, k_ref, v_ref, r_ref, lr_ref, b_ref, dgate_ref, oraw_ref, st_ref, dog_ref, wg_ref, gnw_ref, *rest):
        dq_ref, dk_ref, dv_ref, dr_ref, dlr_ref, dwg_ref, dbg_ref, dgnw_ref = rest[ns:ns + 8]
        dstate, db_scr = rest[2 * ns + 8:2 * ns + 10]
        start, finish = jobs.bind(rest[:ns], rest[ns + 8:2 * ns + 8], rest[2 * ns + 10:])
        t = pl.program_id(0)

        @pl.when(t == 0)
        def _():
            dstate[...] = jnp.zeros_like(dstate)
            dwg_ref[...] = jnp.zeros_like(dwg_ref)
            dbg_ref[...] = jnp.zeros_like(dbg_ref)
            dgnw_ref[...] = jnp.zeros_like(dgnw_ref)
            start()

        pl.when(t == steps - 1)(finish)

        lr, wg = lr_ref[...], wg_ref[...]
        b = b_ref[...]
        eb, enb = jnp.exp(b), jnp.exp(-b)
        scale = GLA_DK ** -0.5
        gq = q_ref[...] * scale * eb
        gk = k_ref[...] * enb
        v = v_ref[...]
        gnw_v = gnw_ref[...]
        tril = _tril64()
        is_last = lax.broadcasted_iota(jnp.int32, (GLA_CHUNK, 1), 0) == GLA_CHUNK - 1
        dgnw = jnp.zeros((1, GLA_DV), F32)
        pairs = [(h, gi) for h in range(GLA_HEADS) for gi in range(group)]
        rs = {gi: slice(gi * GLA_CHUNK, (gi + 1) * GLA_CHUNK) for gi in range(group)}
        s64 = {h: slice(h * GLA_DK, (h + 1) * GLA_DK) for h in range(GLA_HEADS)}
        s128 = {h: slice(h * GLA_DV, (h + 1) * GLA_DV) for h in range(GLA_HEADS)}
        qh = {(h, gi): gq[rs[gi], s64[h]] for h, gi in pairs}
        kh = {(h, gi): gk[rs[gi], s64[h]] for h, gi in pairs}
        vh = {(h, gi): v[rs[gi], s128[h]] for h, gi in pairs}
        ebl = {(h, gi): eb[(gi + 1) * GLA_CHUNK - 1:(gi + 1) * GLA_CHUNK, s64[h]] for h, gi in pairs}
        kl = {pr: kh[pr] * ebl[pr] for pr in pairs}
        st = {(h, gi): st_ref[gi, h] for h, gi in pairs}
        do = {}
        for h, gi in pairs:
            o, rh, dout = oraw_ref[rs[gi], s128[h]], r_ref[rs[gi], s128[h]], dog_ref[rs[gi], s128[h]]
            rstd = lax.rsqrt(jnp.mean(o * o, axis=-1, keepdims=True) + EPS)
            on = o * rstd
            sg = _sigmoid(rh)
            dr_ref[rs[gi], s128[h]] = (dout * (on * gnw_v) * (sg * (1.0 + rh * (1.0 - sg)))).astype(ACT_DTYPE)
            dy = dout * (rh * sg)
            dgnw = dgnw + jnp.sum(dy * on, axis=0, keepdims=True)
            don = dy * gnw_v
            do[h, gi] = rstd * (don - on * jnp.mean(don * on, axis=-1, keepdims=True))
        a = {pr: jnp.where(tril, _mm_nt(qh[pr], kh[pr]), 0.0) for pr in pairs}
        da = {pr: jnp.where(tril, _mm_nt(do[pr], vh[pr]), 0.0) for pr in pairs}
        dinc = {pr: _mm_tn(do[pr], qh[pr]) for pr in pairs}
        dgq = {pr: _mm(da[pr], kh[pr]) + _mm(do[pr], st[pr]) for pr in pairs}
        dgk = {pr: _mm_tn(da[pr], qh[pr]) for pr in pairs}
        dv_a = {pr: _mm_tn(a[pr], do[pr]) for pr in pairs}
        dsp = {}
        for h in range(GLA_HEADS):
            cur = dstate[h]
            for gi in reversed(range(group)):
                dsp[h, gi] = cur
                cur = cur * ebl[h, gi] + dinc[h, gi]
            dstate[h] = cur
        for h, gi in pairs:
            pr = (h, gi)
            dkl = _mm(vh[pr], dsp[pr])
            dv_ref[rs[gi], s128[h]] = (dv_a[pr] + _mm_nt(kl[pr], dsp[pr])).astype(ACT_DTYPE)
            debl = jnp.sum(dsp[pr] * st[pr], axis=0, keepdims=True)
            dq_ref[rs[gi], s64[h]] = (dgq[pr] * (scale * eb[rs[gi], s64[h]])).astype(ACT_DTYPE)
            dk_ref[rs[gi], s64[h]] = ((dgk[pr] + dkl * ebl[pr]) * enb[rs[gi], s64[h]]).astype(ACT_DTYPE)
            last = debl * ebl[pr] + jnp.sum(dkl * kl[pr], axis=0, keepdims=True)
            db_scr[rs[gi], s64[h]] = (dgq[pr] * qh[pr] - dgk[pr] * kh[pr] - dkl * kl[pr]
                                      + jnp.where(is_last, last, 0.0))
        dzg = _masked_sums(_chunk_masks(nrows)[1], db_scr[...]) * dgate_ref[...]
        dlr_ref[...] = _mm_nt(dzg, wg).astype(ACT_DTYPE)
        dwg_ref[...] += _mm_tn(lr, dzg)
        dbg_ref[...] += jnp.broadcast_to(jnp.sum(dzg, axis=0, keepdims=True), dbg_ref.shape)
        dgnw_ref[...] += jnp.broadcast_to(dgnw, dgnw_ref.shape)

    nb = lambda w, col: pl.BlockSpec((nrows, w), lambda t: (steps - 1 - t, col // w))
    const = lambda shape: pl.BlockSpec(shape, lambda t: (0,) * len(shape))
    outs = pl.pallas_call(
        body, name="gla_bwd", grid=(steps,),
        in_specs=[nb(256, C_GQ), nb(256, C_GK), nb(512, C_GV), nb(512, C_GR), nb(128, C_LR), nb(256, 0), nb(256, 0),
                  nb(512, 0),
                  pl.BlockSpec((group, GLA_HEADS, GLA_DV, GLA_DK), lambda t: (steps - 1 - t, 0, 0, 0)), nb(512, 0),
                  const((128, 256)), const((1, 128))] + [ANY] * ns,
        out_specs=[nb(256, 0), nb(256, 0), nb(512, 0), nb(512, 0), nb(128, 0),
                   const((128, 256)), const((8, 256)), const((8, 128))] + [ANY] * ns,
        out_shape=[jax.ShapeDtypeStruct((rows, 256), ACT_DTYPE), jax.ShapeDtypeStruct((rows, 256), ACT_DTYPE),
                   jax.ShapeDtypeStruct((rows, 512), ACT_DTYPE), jax.ShapeDtypeStruct((rows, 512), ACT_DTYPE),
                   jax.ShapeDtypeStruct((rows, 128), ACT_DTYPE), jax.ShapeDtypeStruct((128, 256), F32),
                   jax.ShapeDtypeStruct((8, 256), F32), jax.ShapeDtypeStruct((8, 128), F32)] + jobs.out_shapes,
        scratch_shapes=[pltpu.VMEM((GLA_HEADS, GLA_DV, GLA_DK), F32), pltpu.VMEM((nrows, 256), F32)] + jobs.sems,
        compiler_params=_cp(("arbitrary",)),
    )(proj, proj, proj, proj, proj, decay, dgate, oraw, states, dog, wg_p, gnw, *jobs.inputs)
    return outs[:8], jobs.split(outs[8:])


def _in_proj_bwd(x, lead, dh1, nw, win_p, dgv, dgr, dsq, dgq, dgk, dsk, dsv, dlr, angles, tm):
    seq = x.shape[0]
    rows = LEAD + seq
    nb = tm // LEAD
    steps = rows // tm

    def first_copy(scr, gx_ref, sem):
        return pltpu.make_async_copy(scr.at[pl.ds(LEAD, tm - LEAD)], gx_ref.at[pl.ds(0, tm - LEAD)], sem)

    def tile_copy(scr, gx_ref, sem, step):
        start = pl.multiple_of(jnp.maximum(step * tm - LEAD, 0), LEAD)
        return pltpu.make_async_copy(scr, gx_ref.at[pl.ds(start, tm)], sem)

    def body(*refs):
        x_refs, refs = refs[:nb], refs[nb:]
        (lead_ref, dh1_ref, nw_ref, w_ref, dgv_ref, dgr_ref, dsq_ref, dgq_ref, dgk_ref, dsk_ref, dsv_ref, dlr_ref,
         cs_ref, gx_ref, dlead_ref, dproj_ref, ut_ref, gnm_ref, scr, sem) = refs
        i = pl.program_id(0)

        @pl.when(i == 0)
        def _():
            gnm_ref[...] = jnp.zeros_like(gnm_ref)

        cos, sa, sb = _rope_tables(cs_ref[...])
        dsq_v = (_unrope(dsq_ref[...].astype(F32), cos, sa, sb) * (SWA_HD ** -0.5)).astype(MXU_DTYPE)
        dsk_v = _unrope(dsk_ref[...], cos, sa, sb).astype(MXU_DTYPE)
        dproj = jnp.concatenate(
            [dgv_ref[...].astype(MXU_DTYPE), dgr_ref[...].astype(MXU_DTYPE), dgq_ref[...].astype(MXU_DTYPE),
             dgk_ref[...].astype(MXU_DTYPE), dlr_ref[...].astype(MXU_DTYPE), dsq_v, dsk_v,
             dsv_ref[...].astype(MXU_DTYPE)],
            axis=1)
        dproj_ref[...] = dproj
        h = _h_tile(i, lead_ref, x_refs)
        rstd = lax.rsqrt(jnp.mean(h * h, axis=-1, keepdims=True) + EPS)
        hn = h * rstd
        nw_v = nw_ref[...]
        ut_ref[...] = (hn * nw_v).T.astype(ACT_DTYPE)
        du = _mm_nt(dproj, w_ref[...])
        gnm_ref[...] += jnp.broadcast_to(jnp.sum(du * hn, axis=0, keepdims=True), gnm_ref.shape)
        dun = du * nw_v
        dh0 = dh1_ref[...] + rstd * (dun - hn * jnp.mean(dun * hn, axis=-1, keepdims=True))

        if tm > LEAD:
            pl.when(i == 1)(lambda: first_copy(scr, gx_ref, sem).wait())
        pl.when(i > 1)(lambda: tile_copy(scr, gx_ref, sem, i).wait())
        scr[...] = dh0

        @pl.when(i == 0)
        def _():
            dlead_ref[...] = dh0[0:LEAD]
            if tm > LEAD:
                first_copy(scr, gx_ref, sem).start()
                if steps == 1:
                    first_copy(scr, gx_ref, sem).wait()

        @pl.when(i > 0)
        def _():
            tile_copy(scr, gx_ref, sem, i).start()

        if steps > 1:
            pl.when(i == steps - 1)(lambda: tile_copy(scr, gx_ref, sem, i).wait())

    row = lambda w: pl.BlockSpec((tm, w), lambda i: (i, 0))
    const = lambda shape: pl.BlockSpec(shape, lambda i: (0,) * len(shape))
    return pl.pallas_call(
        body, name="in_proj_bwd", grid=(steps,),
        in_specs=_token_specs(tm) + [const((LEAD, D)), row(D), const((1, D)), const((D, DINP)),
                                     row(512), row(512), row(512), row(256), row(256), row(128), row(128), row(128),
                                     row(ROPE_DIM)],
        out_specs=[ANY, const((LEAD, D)), row(DINP), pl.BlockSpec((D, tm), lambda i: (0, i)), const((8, D))],
        out_shape=[jax.ShapeDtypeStruct((seq, D), F32), jax.ShapeDtypeStruct((LEAD, D), F32),
                   jax.ShapeDtypeStruct((rows, DINP), ACT_DTYPE), jax.ShapeDtypeStruct((D, rows), ACT_DTYPE),
                   jax.ShapeDtypeStruct((8, D), F32)],
        scratch_shapes=[pltpu.VMEM((tm, D), F32), pltpu.SemaphoreType.DMA],
        compiler_params=_cp(("arbitrary",), VMEM_WIDE_MB),
    )(*([x] * nb), lead, dh1, nw, win_p, dgv, dgr, dsq, dgq, dgk, dsk, dsv, dlr, angles)


def _win_runs():
    groups = [(O_GQ, C_GQ), (O_GK, C_GK), (O_GV, C_GV), (O_GR, C_GR), (O_LR, C_LR), (O_SQ, C_SQ), (O_SK, C_SK),
              (O_SV, C_SV)]
    per = DIN // N_DEV
    runs = []
    for (o0, o1), c0 in groups:
        o = o0
        while o < o1:
            d = o // per
            end = min(o1, (d + 1) * per)
            runs.append((d, o - d * per, c0 + o - o0, end - o))
            o = end
    return runs


def _win_padded(g_in):
    tr = 128

    def body(g_ref, o_ref):
        o_ref[...] = jnp.zeros_like(o_ref)
        for d, s, c, w in _win_runs():
            o_ref[:, c:c + w] = g_ref[d, :, s:s + w]

    return pl.pallas_call(
        body, name="w_in_layout", grid=(D // tr,),
        in_specs=[pl.BlockSpec((N_DEV, tr, DIN // N_DEV), lambda i: (0, i, 0))],
        out_specs=pl.BlockSpec((tr, DINP), lambda i: (i, 0)),
        out_shape=jax.ShapeDtypeStruct((D, DINP), g_in.dtype),
        compiler_params=_cp(("arbitrary",)),
    )(g_in)


def _in_proj_bwd_weights(ut, dproj, tm, small):
    rows = dproj.shape[0]
    steps = rows // tm
    per = DIN // N_DEV

    def body(ut_ref, dp_ref, *rest):
        small_refs, (mine_ref, theirs_ref, total_ref, acc, stage, local_sems, send_sems, recv_sems) = rest[:9], rest[9:17]
        i = pl.program_id(0)
        start, finish = _small_sum_schedule(small_refs, total_ref, *rest[17:])
        x, y, c = _mesh_pos()

        @pl.when(i == 0)
        def _():
            acc[...] = jnp.zeros_like(acc)
            start()

        acc[...] += _mm(ut_ref[...], dp_ref[...])
        pl.when(i == steps - 1)(finish)

        def keep(slot, chip):
            return pltpu.make_async_copy(stage.at[slot], mine_ref.at[chip], local_sems.at[slot])

        def send(slot, chip):
            return pltpu.make_async_remote_copy(
                src_ref=stage.at[slot], dst_ref=theirs_ref.at[chip], send_sem=send_sems.at[slot],
                recv_sem=recv_sems.at[chip], device_id=(x, y, 1 - c), device_id_type=MESH)

        def drained(d):
            pl.when(c == d % 2)(keep(d % 2, d // 2).wait)
            pl.when(c != d % 2)(send(d % 2, d // 2).wait_send)

        @pl.when(i == steps - 1)
        def _():
            for d in range(N_DEV):
                slot, chip = d % 2, d // 2
                if d >= 2:
                    drained(d - 2)
                for owner, s, col, w in _win_runs():
                    if owner == d:
                        stage[slot, :, s:s + w] = acc[:, col:col + w]
                pl.when(c == slot)(keep(slot, chip).start)
                pl.when(c != slot)(send(slot, chip).start)
            drained(N_DEV - 2)
            drained(N_DEV - 1)
            for chip in range(4):
                send(0, chip).wait_recv()

    half = jax.ShapeDtypeStruct((4, D, per), F32)
    return pl.pallas_call(
        body, name="in_proj_bwd_weights", grid=(steps,),
        in_specs=[pl.BlockSpec((D, tm), lambda i: (0, i)), pl.BlockSpec((tm, DINP), lambda i: (i, 0))] + SMALL_SPECS,
        out_specs=[ANY, ANY, pl.BlockSpec((SMALL_ROWS, D), lambda i: (0, 0))],
        out_shape=[half, half, jax.ShapeDtypeStruct((SMALL_ROWS, D), F32)],
        scratch_shapes=[pltpu.VMEM((D, DINP), F32), pltpu.VMEM((2, D, per), F32), pltpu.SemaphoreType.DMA((2,)),
                        pltpu.SemaphoreType.DMA((2,)), pltpu.SemaphoreType.DMA((4,))] + _small_sum_scratch(),
        compiler_params=_cp(("arbitrary",), VMEM_WIDE_MB),
    )(ut, dproj, *small)


def _adamw(w, g, m, v):
    m = ADAM_B1 * m + (1.0 - ADAM_B1) * g
    v = ADAM_B2 * v + (1.0 - ADAM_B2) * jnp.square(g)
    m_hat = m / (1.0 - ADAM_B1 ** ADAM_STEP)
    v_hat = v / (1.0 - ADAM_B2 ** ADAM_STEP)
    delta = -ADAM_LR * (m_hat / (jnp.sqrt(v_hat) + ADAM_EPS) + ADAM_WD * w)
    return delta, m, v


ADAM_STEPS = 8


def _adamw_shards(items, name, jobs=None):
    jobs = jobs or _Jobs([])
    ns, nw = jobs.n, len(items)

    def body(*rest):
        ins, rest = rest[:5 * nw], rest[5 * nw:]
        job_ins, rest = rest[:ns], rest[ns:]
        outs, rest = rest[:4 * nw], rest[4 * nw:]
        start, finish = jobs.bind(job_ins, rest[:ns], rest[ns:])
        i = pl.program_id(0)
        pl.when(i == 0)(start)
        pl.when(i == ADAM_STEPS - 1)(finish)
        for k in range(nw):
            p_ref, own_ref, w_ref, m_ref, v_ref = ins[5 * k:5 * k + 5]
            g_ref, d_ref, nm_ref, nv_ref = outs[4 * k:4 * k + 4]
            g = ((p_ref[0].astype(F32) + p_ref[1].astype(F32)) + p_ref[2].astype(F32)) + own_ref[...]
            g_ref[...] = g
            d_ref[...], nm_ref[...], nv_ref[...] = _adamw(w_ref[...], g, m_ref[...], v_ref[...])

    in_specs, out_specs, out_shape, operands = [], [], [], []
    for parts, own, w, m, v in items:
        r, cdim = w.shape
        tr = r // ADAM_STEPS
        spec = pl.BlockSpec((tr, cdim), lambda i: (i, 0))
        in_specs += [pl.BlockSpec((3, tr, cdim), lambda i: (0, i, 0)), spec, spec, spec, spec]
        out_specs += [spec] * 4
        out_shape += [jax.ShapeDtypeStruct((r, cdim), F32)] * 4
        operands += [parts, own, w, m, v]
    outs = pl.pallas_call(
        body, name=name, grid=(ADAM_STEPS,),
        in_specs=in_specs + [ANY] * ns, out_specs=out_specs + [ANY] * ns, scratch_shapes=jobs.sems,
        out_shape=out_shape + jobs.out_shapes,
        compiler_params=_cp(("arbitrary",)),
    )(*operands, *jobs.inputs)
    return [outs[4 * k:4 * k + 4] for k in range(nw)], jobs.split(outs[4 * nw:])


def _adamw_small(items):
    n = len(items)

    def body(*refs):
        ins, outs = refs[:4 * n], refs[4 * n:]
        for k in range(n):
            w_ref, g_ref, m_ref, v_ref = ins[4 * k:4 * k + 4]
            d_ref, nm_ref, nv_ref = outs[3 * k:3 * k + 3]
            d_ref[...], nm_ref[...], nv_ref[...] = _adamw(w_ref[...], g_ref[...], m_ref[...], v_ref[...])

    vm = pl.BlockSpec(memory_space=pltpu.VMEM)
    shapes = [jax.ShapeDtypeStruct(w.shape, F32) for w, _, _, _ in items for _ in range(3)]
    outs = pl.pallas_call(body, name="adamw_small", in_specs=[vm] * (4 * n), out_specs=[vm] * (3 * n),
                          out_shape=shapes)(*[t for item in items for t in item])
    return [outs[3 * k:3 * k + 3] for k in range(n)]


def _pair_sums(where, mine, theirs, name):
    _, r, cdim = theirs.shape
    tr = 128 if r % 128 == 0 else r

    def body(where_ref, a_ref, b_ref, own_ref, wire_ref):
        chip = where_ref[1]
        own_ref[...] = a_ref[chip] + b_ref[chip]
        wire_ref[...] = (a_ref[...] + b_ref[...]).astype(WIRE_DTYPE)

    spec = pl.BlockSpec((4, tr, cdim), lambda i, s: (0, i, 0))
    mine_spec = spec if mine.ndim == 3 else pl.BlockSpec((None, 4, tr, cdim), lambda i, s: (s[0], 0, i, 0))
    return pl.pallas_call(
        body, name=name,
        grid_spec=pltpu.PrefetchScalarGridSpec(
            num_scalar_prefetch=1, grid=(r // tr,), in_specs=[mine_spec, spec],
            out_specs=[pl.BlockSpec((tr, cdim), lambda i, s: (i, 0)), spec]),
        out_shape=[jax.ShapeDtypeStruct((r, cdim), F32), jax.ShapeDtypeStruct(theirs.shape, WIRE_DTYPE)],
        compiler_params=_cp(("arbitrary",)))(where, mine, theirs)


def kernel(x, meta_tokens, norm_mix_w, w_in, w_gate_up, b_gate, gla_norm_w, sinks, w_out, norm_ff_w, w_ff1, w_ff2, final_norm_w, loss_target, m_meta_tokens, m_norm_mix_w, m_w_in, m_w_gate_up, m_b_gate, m_gla_norm_w, m_sinks, m_w_out, m_norm_ff_w, m_w_ff1, m_w_ff2, m_final_norm_w, v_meta_tokens, v_norm_mix_w, v_w_in, v_w_gate_up, v_b_gate, v_gla_norm_w, v_sinks, v_w_out, v_norm_ff_w, v_w_ff1, v_w_ff2, v_final_norm_w):
    seq = x.shape[1]
    rows = LEAD + seq
    tm = _row_tile(rows)
    tm_wide = WIDE_ROW_TILE if rows % WIDE_ROW_TILE == 0 else tm
    dev =4 * lax.axis_index("x") + 2 * lax.axis_index("y") + lax.axis_index("c")

    small_shard = jnp.concatenate([meta_tokens, w_gate_up[0], jnp.zeros((N_META, 96), F32)], axis=1)
    g_in, g_small = _all_gather([w_in[0].astype(WIRE_DTYPE), small_shard])
    later_shards = [w_out[0].astype(WIRE_DTYPE), w_ff1[0].astype(WIRE_DTYPE), w_ff2[0].astype(WIRE_DTYPE)]
    win_p = _win_padded(g_in)
    meta_full = jnp.transpose(g_small[:, :, 0:128], (1, 0, 2)).reshape(N_META, D)
    wg_full = jnp.transpose(g_small[:, :, 128:160], (1, 0, 2)).reshape(GLA_RANK, GLA_HEADS * GLA_DK)
    wg_p = jnp.concatenate([wg_full, jnp.zeros((128 - GLA_RANK, 256), F32)], axis=0)

    lead = jnp.concatenate([jnp.zeros((META0, D), F32), meta_full], axis=0)
    angles = _rope_angles(rows)
    proj, qr, kr, vr, (g_w1,) = _in_proj(x[0], lead, norm_mix_w, win_p, angles, tm, later_shards[1:2])
    oraw, og, states, decay, dgate, (g_out,) = _gla_fwd(proj, wg_p, b_gate, gla_norm_w, later_shards[0:1])
    osw, (g_w2,) = _swa_fwd(qr, kr, vr, sinks, later_shards[2:3])
    wout_full = g_out.reshape(D, D)
    w2_full = g_w2.reshape(D_FF, D)
    w1_full = jnp.transpose(g_w1, (1, 0, 2)).reshape(D, D_FF)
    h1, f, ft = _out_proj(x[0], lead, og, osw, wout_full, norm_ff_w, tm)
    a, dh2, dh2t, loss_p, gfn_p = _ffn_fwd(f, h1, w1_full, w2_full, loss_target[0], final_norm_w.reshape(1, D), tm)

    da, dh1, gnf_p = _ffn_bwd_act(dh2, a, w1_full, w2_full, h1, norm_ff_w, tm)
    dw1, dw2 = _ffn_bwd_weights(ft, a, da, dh2t, tm_wide)
    where = jnp.stack([lax.axis_index("c"), 2 * lax.axis_index("x") + lax.axis_index("y")]).astype(jnp.int32)
    dog, dos, dwout, theirs_ffn = _out_proj_bwd(dh1, og, osw, wout_full, tm, [dw1, dw2])
    pairs_ffn = [_pair_sums(where, p, q, "reduce_pair_%d" % (2 + k))
                 for k, (p, q) in enumerate(zip([dw1, dw2], theirs_ffn))]
    sums_ffn, wires_ffn = [p[0] for p in pairs_ffn], [p[1] for p in pairs_ffn]
    dsq, dsk, dsv, dsink_p, (parts_ffn, (theirs_wout,)) = _swa_bwd(
        qr, kr, vr, osw, dos, sinks, _Jobs([("chips", wires_ffn), ("sibling", [dwout])]))
    sum_wout, wire_wout = _pair_sums(where, dwout, theirs_wout, "reduce_pair_1")
    (dgq, dgk, dgv, dgr, dlr, dwg_p, dbg_p, dgnw_p), ((parts_wout,),) = _gla_bwd(
        proj, decay, dgate, oraw, states, dog, wg_p, gla_norm_w, _Jobs([("chips", [wire_wout])]))
    grad_x, dlead, dproj, ut, gnm_p = _in_proj_bwd(x[0], lead, dh1, norm_mix_w, win_p, dgv, dgr, dsq, dgq, dgk, dsk,
                                                   dsv, dlr, angles, tm)
    grad_x = grad_x[None]
    dwin_mine, dwin_theirs, total = _in_proj_bwd_weights(
        ut, dproj, tm_wide, [dlead, dwg_p, gnm_p, gnf_p, gfn_p, dbg_p, dgnw_p, loss_p, dsink_p])
    sum_win, sum_win_wire = _pair_sums(where, dwin_mine, dwin_theirs, "reduce_pair_0")

    g_meta = lax.dynamic_slice(total, (R_META, dev * 128), (N_META, 128))
    g_wg = lax.dynamic_slice(total, (R_WG, dev * 32), (GLA_RANK, 32))
    g_norm_mix, g_norm_ff = total[R_NORM_MIX:R_NORM_MIX + 1], total[R_NORM_FF:R_NORM_FF + 1]
    g_final_norm = total[R_FINAL:R_FINAL + 1]
    g_b_gate, g_gla_norm = total[R_B_GATE:R_B_GATE + 1, 0:256], total[R_GLA_NORM:R_GLA_NORM + 1, 0:128]
    g_sinks = total[R_SINKS:R_SINKS + SWA_HEADS, 0].reshape(1, SWA_HEADS)
    loss = total[R_LOSS, 0]

    ((g_wout, d_wout, nm_wout, nv_wout), (g_w1s, d_w1, nm_w1, nv_w1), (g_w2s, d_w2, nm_w2, nv_w2)), ((parts_win,),) = \
        _adamw_shards([(parts_wout, sum_wout, w_out[0], m_w_out[0], v_w_out[0]),
                       (parts_ffn[0], sums_ffn[0], w_ff1[0], m_w_ff1[0], v_w_ff1[0]),
                       (parts_ffn[1], sums_ffn[1], w_ff2[0], m_w_ff2[0], v_w_ff2[0])],
                      "adamw_w_out_ff", _Jobs([("chips", [sum_win_wire])]))
    ((g_win, d_win, nm_win, nv_win),), _ = _adamw_shards(
        [(parts_win, sum_win, w_in[0], m_w_in[0], v_w_in[0])], "adamw_w_in")

    names = ["meta", "wg", "norm_mix", "b_gate", "gla_norm", "sinks", "norm_ff", "final_norm"]
    ws = [meta_tokens, w_gate_up, norm_mix_w, b_gate, gla_norm_w, sinks, norm_ff_w, final_norm_w]
    gs = [g_meta, g_wg, g_norm_mix, g_b_gate, g_gla_norm, g_sinks, g_norm_ff, g_final_norm]
    ms = [m_meta_tokens, m_w_gate_up, m_norm_mix_w, m_b_gate, m_gla_norm_w, m_sinks, m_norm_ff_w, m_final_norm_w]
    vs = [v_meta_tokens, v_w_gate_up, v_norm_mix_w, v_b_gate, v_gla_norm_w, v_sinks, v_norm_ff_w, v_final_norm_w]
    flat = lambda t: t.reshape(-1, t.shape[-1])
    small_out = _adamw_small([(flat(w), flat(g), flat(m), flat(v)) for w, g, m, v in zip(ws, gs, ms, vs)])
    d_small = {n: small_out[k][0].reshape(ws[k].shape) for k, n in enumerate(names)}
    nm_small = {n: small_out[k][1].reshape(ws[k].shape) for k, n in enumerate(names)}
    nv_small = {n: small_out[k][2].reshape(ws[k].shape) for k, n in enumerate(names)}
    g_small_d = {n: g.reshape(ws[k].shape) for k, (n, g) in enumerate(zip(names, gs))}

    def ordered(big, small_d):
        win_v, wout_v, w1_v, w2_v = big
        return (small_d["meta"], small_d["norm_mix"], win_v[None], small_d["wg"], small_d["b_gate"],
                small_d["gla_norm"], small_d["sinks"], wout_v[None], small_d["norm_ff"], w1_v[None], w2_v[None],
                small_d["final_norm"])

    return (loss, grad_x,
            *ordered((g_win, g_wout, g_w1s, g_w2s), g_small_d),
            *ordered((d_win, d_wout, d_w1, d_w2), d_small),
            *ordered((nm_win, nm_wout, nm_w1, nm_w2), nm_small),
            *ordered((nv_win, nv_wout, nv_w1, nv_w2), nv_small))
```

```python
import functools

import jax
import jax.numpy as jnp
from jax import lax
from jax.experimental import pallas as pl
from jax.experimental.pallas import tpu as pltpu

F32 = jnp.float32
MXU_DTYPE = jnp.bfloat16
ACT_DTYPE = jnp.bfloat16
WIRE_DTYPE = jnp.bfloat16

D = 1024
N_META = 16
LEAD = 128
META0 = LEAD - N_META
EPS = 1e-5
GLA_HEADS, GLA_DK, GLA_DV, GLA_RANK, GLA_CHUNK = 4, 64, 128, 16, 64
GLA_TAU = 16.0
SWA_HEADS, SWA_KV, SWA_GROUP, SWA_HD, SWA_BLOCK = 8, 2, 4, 64, 128
ROPE_DIM, ROPE_THETA = 16, 500000.0
D_FF = 4096
N_DEV = 8
FF_TILE = D_FF // N_DEV
FF_WIDE = 2048
NEG = -1e30

C_GV, C_GR, C_GQ, C_GK, C_LR, C_SQ, C_SK, C_SV = 0, 512, 1024, 1280, 1536, 1664, 2176, 2304
DGLA = 1664
DINP = 2432
DIN = 2320
O_GQ, O_GK, O_GV, O_GR, O_LR, O_SQ, O_SK, O_SV = (0, 256), (256, 512), (512, 1024), (1024, 1536), (1536, 1552), (1552, 2064), (2064, 2192), (2192, 2320)

ADAM_LR, ADAM_B1, ADAM_B2, ADAM_EPS, ADAM_WD, ADAM_STEP = 0.001, 0.9, 0.999, 1e-08, 0.01, 10

MESH = pl.DeviceIdType.MESH
ANY = pl.BlockSpec(memory_space=pl.ANY)
VMEM_TILE_MB, VMEM_WIDE_MB = 48, 56


def _cp(sem=None, vmem_mb=None):
    kw = {}
    if sem is not None:
        kw["dimension_semantics"] = sem
    if vmem_mb is not None:
        kw["vmem_limit_bytes"] = vmem_mb << 20
    return pltpu.CompilerParams(**kw)


def _mm(a, b):
    return jnp.dot(a.astype(MXU_DTYPE), b.astype(MXU_DTYPE), preferred_element_type=F32)


def _mm_nt(a, b):
    return lax.dot_general(a.astype(MXU_DTYPE), b.astype(MXU_DTYPE), (((1,), (1,)), ((), ())),
                           preferred_element_type=F32)


def _mm_tn(a, b):
    return lax.dot_general(a.astype(MXU_DTYPE), b.astype(MXU_DTYPE), (((0,), (0,)), ((), ())),
                           preferred_element_type=F32)


def _masked_sums(mask, t):
    m = mask.astype(jnp.bfloat16)
    hi = t.astype(jnp.bfloat16)
    rest = t - hi.astype(F32)
    mid = rest.astype(jnp.bfloat16)
    low = (rest - mid.astype(F32)).astype(jnp.bfloat16)
    dot = lambda part: jnp.dot(m, part, preferred_element_type=F32)
    return dot(hi) + (dot(mid) + dot(low))


def _logsigmoid(z):
    return jnp.minimum(z, 0.0) - jnp.log(1.0 + jnp.exp(-jnp.abs(z)))


def _sigmoid(z):
    return 1.0 / (1.0 + jnp.exp(-z))


ROW_TILE, WIDE_ROW_TILE = 640, 1664


def _row_tile(rows, want=ROW_TILE):
    return want if rows % want == 0 else LEAD


def _mesh_pos():
    return lax.axis_index("x"), lax.axis_index("y"), lax.axis_index("c")


def _all_gather(shards):
    n = len(shards)

    def body(*refs):
        start, forward, finish = _gather_schedule(refs[:n], refs[n:2 * n], *refs[2 * n:])
        start()
        for j in range(3):
            forward(j)
        finish()

    gathered = pl.pallas_call(
        body, name="all_gather_weights",
        out_shape=_gathered_shapes(shards), in_specs=[ANY] * n, out_specs=[ANY] * n,
        scratch_shapes=_gather_sems(n),
    )(*shards)
    return _with_own_block(gathered, shards)


def _gathered_shapes(shards):
    return [jax.ShapeDtypeStruct((N_DEV,) + s.shape, s.dtype) for s in shards]


def _gather_sems(n):
    return [pltpu.SemaphoreType.DMA((7 * n,)), pltpu.SemaphoreType.DMA((7 * n,))] if n else []


def _place_gather(step, steps, shard_refs, gathered_refs, sems):
    if not shard_refs:
        return
    start, forward, finish = _gather_schedule(shard_refs, gathered_refs, *sems)
    pl.when(step == 0)(start)
    for j, at in enumerate((steps * 7 // 10, steps * 8 // 10, steps * 9 // 10)):
        pl.when(step == at)(functools.partial(forward, j))
    pl.when(step == steps - 1)(finish)


def _with_own_block(gathered, shards):
    dev = 4 * lax.axis_index("x") + 2 * lax.axis_index("y") + lax.axis_index("c")
    return [lax.dynamic_update_index_in_dim(g, s, dev, 0) for g, s in zip(gathered, shards)]


def _gather_schedule(ins, outs, send_sems, recv_sems):
    n = len(ins)
    x, y, c = _mesh_pos()
    me, sibling = (x, y, c), (x, y, 1 - c)
    chips = [(1 - x, y), (x, 1 - y), (1 - x, 1 - y)]

    def copy(a, k, block, to, src=None):
        dst = outs[a].at[4 * block[0] + 2 * block[1] + block[2]]
        return pltpu.make_async_remote_copy(
            src_ref=dst if src is None else src, dst_ref=dst,
            send_sem=send_sems.at[a * 7 + k], recv_sem=recv_sems.at[a * 7 + k],
            device_id=to, device_id_type=MESH)

    def first(a):
        return [copy(a, 0, me, sibling, src=ins[a])] + [copy(a, 1 + j, me, (*chip, c), src=ins[a])
                                                        for j, chip in enumerate(chips)]

    def start():
        for a in range(n):
            for cp in first(a):
                cp.start()

    def forward(j):
        for a in range(n):
            copy(a, 1 + j, (*chips[j], c), me).wait_recv()
            copy(a, 4 + j, (*chips[j], c), sibling).start()

    def finish():
        for a in range(n):
            copy(a, 0, sibling, me).wait_recv()
            for j, chip in enumerate(chips):
                copy(a, 4 + j, (*chip, 1 - c), me).wait_recv()
        for a in range(n):
            for cp in first(a) + [copy(a, 4 + j, (*chip, c), sibling) for j, chip in enumerate(chips)]:
                cp.wait_send()

    return start, forward, finish


def _sibling_shapes(gs):
    return [jax.ShapeDtypeStruct(g.shape[1:], g.dtype) for g in gs]


def _sibling_sems(n):
    return [pltpu.SemaphoreType.DMA((n,)), pltpu.SemaphoreType.DMA((n,))]


def _sibling_schedule(ins, land, send_sems, recv_sems):
    x, y, c = _mesh_pos()

    def copies():
        return [pltpu.make_async_remote_copy(
            src_ref=ins[a].at[1 - c], dst_ref=land[a], send_sem=send_sems.at[a], recv_sem=recv_sems.at[a],
            device_id=(x, y, 1 - c), device_id_type=MESH) for a in range(len(ins))]

    def start():
        for cp in copies():
            cp.start()

    def finish():
        for cp in copies():
            cp.wait_recv()
        for cp in copies():
            cp.wait_send()

    return start, finish


def _chips_shapes(ps):
    return [jax.ShapeDtypeStruct((3,) + p.shape[1:], p.dtype) for p in ps]


def _chips_sems(n):
    return [pltpu.SemaphoreType.DMA((3 * n,)), pltpu.SemaphoreType.DMA((3 * n,))]


def _chips_schedule(ins, land, send_sems, recv_sems):
    x, y, c = _mesh_pos()
    chips = [(1 - x, y), (x, 1 - y), (1 - x, 1 - y)]

    def copies():
        return [pltpu.make_async_remote_copy(
            src_ref=ins[a].at[2 * chip[0] + chip[1]], dst_ref=land[a].at[j],
            send_sem=send_sems.at[3 * a + j], recv_sem=recv_sems.at[3 * a + j],
            device_id=(*chip, c), device_id_type=MESH) for a in range(len(ins)) for j, chip in enumerate(chips)]

    def start():
        for cp in copies():
            cp.start()

    def finish():
        for cp in copies():
            cp.wait_recv()
        for cp in copies():
            cp.wait_send()

    return start, finish


class _Jobs:
    def __init__(self, jobs):
        self.jobs = jobs
        self.inputs = [a for _, arrs in jobs for a in arrs]
        self.out_shapes = [s for kind, arrs in jobs
                           for s in (_sibling_shapes(arrs) if kind == "sibling" else _chips_shapes(arrs))]
        self.sems = [s for kind, arrs in jobs
                     for s in (_sibling_sems(len(arrs)) if kind == "sibling" else _chips_sems(len(arrs)))]
        self.n = len(self.inputs)

    def bind(self, in_refs, out_refs, sem_refs):
        starts, finishes, at = [], [], 0
        for k, (kind, arrs) in enumerate(self.jobs):
            schedule = _sibling_schedule if kind == "sibling" else _chips_schedule
            start, finish = schedule(in_refs[at:at + len(arrs)], out_refs[at:at + len(arrs)],
                                     sem_refs[2 * k], sem_refs[2 * k + 1])
            starts.append(start)
            finishes.append(finish)
            at += len(arrs)

        def start_all():
            for f in starts:
                f()

        def finish_all():
            for f in finishes:
                f()

        return start_all, finish_all

    def split(self, outs):
        res, at = [], 0
        for _, arrs in self.jobs:
            res.append(list(outs[at:at + len(arrs)]))
            at += len(arrs)
        return res


R_META, R_WG, R_NORM_MIX, R_NORM_FF, R_FINAL, R_B_GATE, R_GLA_NORM, R_LOSS, R_SINKS, SMALL_ROWS = 0, 16, 32, 33, 34, 35, 36, 37, 40, 48


SMALL_SPECS = [pl.BlockSpec((LEAD, D), lambda i: (0, 0)), pl.BlockSpec((128, 256), lambda i: (0, 0)),
               pl.BlockSpec((8, D), lambda i: (0, 0)), pl.BlockSpec((8, D), lambda i: (0, 0)),
               pl.BlockSpec((8, D), lambda i: (0, 0)), pl.BlockSpec((8, 256), lambda i: (0, 0)),
               pl.BlockSpec((8, 128), lambda i: (0, 0)), pl.BlockSpec((8, 128), lambda i: (0, 0)),
               pl.BlockSpec((8, 128), lambda i: (0, 0))]


def _small_sum_scratch():
    return [pltpu.VMEM((SMALL_ROWS, D), F32), pltpu.VMEM((N_DEV, SMALL_ROWS, D), F32),
            pltpu.SemaphoreType.DMA((7,)), pltpu.SemaphoreType.DMA((7,))]


def _small_sum_schedule(small_refs, out_ref, p_ref, land, send_sems, recv_sems):
    dlead_ref, dwg_ref, gnm_ref, gnf_ref, gfn_ref, dbg_ref, dgnw_ref, loss_ref, dsink_ref = small_refs
    x, y, c = _mesh_pos()
    me = 4 * x + 2 * y + c

    def copies():
        res = []
        for k in range(1, N_DEV):
            bx, by, bc = (k >> 2) & 1, (k >> 1) & 1, k & 1
            peer = (1 - x if bx else x, 1 - y if by else y, 1 - c if bc else c)
            res.append(pltpu.make_async_remote_copy(
                src_ref=p_ref, dst_ref=land.at[me], send_sem=send_sems.at[k - 1], recv_sem=recv_sems.at[k - 1],
                device_id=peer, device_id_type=MESH))
        return res

    def start():
        p_ref[...] = jnp.zeros_like(p_ref)
        p_ref[R_META:R_META + N_META, :] = dlead_ref[META0:LEAD, :]
        p_ref[R_WG:R_WG + GLA_RANK, 0:256] = dwg_ref[0:GLA_RANK, :]
        p_ref[R_NORM_MIX:R_NORM_MIX + 1, :] = gnm_ref[0:1, :]
        p_ref[R_NORM_FF:R_NORM_FF + 1, :] = gnf_ref[0:1, :]
        p_ref[R_FINAL:R_FINAL + 1, :] = gfn_ref[0:1, :]
        p_ref[R_B_GATE:R_B_GATE + 1, 0:256] = dbg_ref[0:1, :]
        p_ref[R_GLA_NORM:R_GLA_NORM + 1, 0:128] = dgnw_ref[0:1, :]
        p_ref[R_LOSS:R_LOSS + 1, 0:128] = loss_ref[0:1, :]
        p_ref[R_SINKS:R_SINKS + SWA_HEADS, 0:128] = dsink_ref[...]
        land[me] = p_ref[...]
        for cp in copies():
            cp.start()

    def finish():
        for cp in copies():
            cp.wait_recv()
        for cp in copies():
            cp.wait_send()
        acc = land[0]
        for d in range(1, N_DEV):
            acc = acc + land[d]
        out_ref[...] = acc

    return start, finish


def _token_specs(tm, grid_rank=1):
    nb = tm // LEAD

    def spec(k):
        if grid_rank == 1:
            return pl.BlockSpec((LEAD, D), lambda i: (jnp.maximum(i * nb + k - 1, 0), 0))
        return pl.BlockSpec((LEAD, D), lambda i, j: (jnp.maximum(i * nb + k - 1, 0), 0))

    return [spec(k) for k in range(nb)]


def _h_tile(i, lead_ref, x_refs):
    first = jnp.where(i == 0, lead_ref[...], x_refs[0][...])
    return jnp.concatenate([first] + [r[...] for r in x_refs[1:]], axis=0)


def _in_proj(x, lead, nw, win_p, angles, tm, shards):
    rows = LEAD + x.shape[0]
    nb = tm // LEAD
    steps = rows // tm
    ns = len(shards)

    def body(*refs):
        x_refs, refs = refs[:nb], refs[nb:]
        lead_ref, nw_ref, w_ref, cs_ref = refs[:4]
        shard_refs, (o_ref, q_ref, k_ref, v_ref) = refs[4:4 + ns], refs[4 + ns:8 + ns]
        _place_gather(pl.program_id(0), steps, shard_refs, refs[8 + ns:8 + 2 * ns], refs[8 + 2 * ns:])
        h = _h_tile(pl.program_id(0), lead_ref, x_refs)
        rstd = lax.rsqrt(jnp.mean(h * h, axis=-1, keepdims=True) + EPS)
        u = (h * rstd * nw_ref[...]).astype(MXU_DTYPE)
        proj = jnp.dot(u, w_ref[...].astype(MXU_DTYPE), preferred_element_type=F32)
        o_ref[...] = proj[:, 0:DGLA]
        cos, sa, sb = _rope_tables(cs_ref[...])
        q_ref[...] = (_rope(proj[:, C_SQ:C_SK], cos, sa, sb) * (SWA_HD ** -0.5)).astype(ACT_DTYPE)
        k_ref[...] = _rope(proj[:, C_SK:C_SV], cos, sa, sb).astype(ACT_DTYPE)
        v_ref[...] = proj[:, C_SV:DINP].astype(ACT_DTYPE)

    row = lambda w: pl.BlockSpec((tm, w), lambda i: (i, 0))
    outs = pl.pallas_call(
        body, name="in_proj", grid=(steps,),
        in_specs=_token_specs(tm) + [pl.BlockSpec((LEAD, D), lambda i: (0, 0)), pl.BlockSpec((1, D), lambda i: (0, 0)),
                                     pl.BlockSpec((D, DINP), lambda i: (0, 0)), row(ROPE_DIM)]
        + [ANY] * ns,
        out_specs=[row(DGLA), row(512), row(128), row(128)] + [ANY] * ns,
        out_shape=[jax.ShapeDtypeStruct((rows, DGLA), F32), jax.ShapeDtypeStruct((rows, 512), ACT_DTYPE),
                   jax.ShapeDtypeStruct((rows, 128), ACT_DTYPE), jax.ShapeDtypeStruct((rows, 128), ACT_DTYPE)]
        + _gathered_shapes(shards),
        scratch_shapes=_gather_sems(ns),
        compiler_params=_cp(("arbitrary",), VMEM_WIDE_MB),
    )(*([x] * nb), lead, nw, win_p, angles, *shards)
    return outs[0], outs[1], outs[2], outs[3], _with_own_block(outs[4:], shards)


def _rope_angles(rows):
    pos = (jnp.arange(rows, dtype=jnp.int32) - META0).astype(F32)
    inv_freq = 1.0 / (ROPE_THETA ** (jnp.arange(0, ROPE_DIM, 2, dtype=F32) / ROPE_DIM))
    ang = pos[:, None] * inv_freq[None, :]
    return jnp.concatenate([jnp.cos(ang), jnp.sin(ang)], axis=1)


def _rope_tables(cs):
    shape = (2 * (ROPE_DIM // 2), 3 * 128)
    j = lax.broadcasted_iota(jnp.int32, shape, 0)
    col = lax.broadcasted_iota(jnp.int32, shape, 1)
    table, in_head = col // 128, col % SWA_HD
    match = (j % (ROPE_DIM // 2)) == (in_head % (ROPE_DIM // 2))
    is_cos = j < ROPE_DIM // 2
    first, second = in_head < ROPE_DIM // 2, (in_head >= ROPE_DIM // 2) & (in_head < ROPE_DIM)
    spread = (jnp.where(match & is_cos & (table == 0) & (first | second), 1.0, 0.0)
              + jnp.where(match & ~is_cos & (table == 1) & first, -1.0, 0.0)
              + jnp.where(match & ~is_cos & (table == 2) & second, 1.0, 0.0)).astype(jnp.bfloat16)
    hi = cs.astype(jnp.bfloat16)
    rest = cs - hi.astype(F32)
    mid = rest.astype(jnp.bfloat16)
    low = (rest - mid.astype(F32)).astype(jnp.bfloat16)
    dot = lambda part: jnp.dot(part, spread, preferred_element_type=F32)
    tabs = dot(hi) + (dot(mid) + dot(low))
    lane = lax.broadcasted_iota(jnp.int32, (1, 128), 1) % SWA_HD
    return tabs[:, 0:128] + jnp.where(lane >= ROPE_DIM, 1.0, 0.0), tabs[:, 128:256], tabs[:, 256:384]


def _rope(xv, cos, sa, sb):
    width = xv.shape[1]
    reps = width // 128
    if reps > 1:
        cos, sa, sb = (jnp.tile(t, (1, reps)) for t in (cos, sa, sb))
    return xv * cos + pltpu.roll(xv, width - 8, 1) * sa + pltpu.roll(xv, 8, 1) * sb


def _unrope(dy, cos, sa, sb):
    width = dy.shape[1]
    reps = width // 128
    if reps > 1:
        cos, sa, sb = (jnp.tile(t, (1, reps)) for t in (cos, sa, sb))
    return dy * cos + pltpu.roll(dy * sa, 8, 1) + pltpu.roll(dy * sb, width - 8, 1)


def _gla_group(nc, most):
    for g in (10, 5, 2):
        if g <= most and nc % g == 0:
            return g
    return 1


def _chunk_masks(nrows):
    ii = lax.broadcasted_iota(jnp.int32, (nrows, nrows), 0)
    jj = lax.broadcasted_iota(jnp.int32, (nrows, nrows), 1)
    same = (ii // GLA_CHUNK) == (jj // GLA_CHUNK)
    return same & (jj <= ii), same & (jj >= ii)


def _gla_gates(lr, wg, bg, first_row):
    nrows = lr.shape[0]
    zg = _mm(lr, wg) + bg
    live = first_row + lax.broadcasted_iota(jnp.int32, (nrows, 1), 0) >= META0
    g = jnp.where(live, _logsigmoid(zg) * (1.0 / GLA_TAU), 0.0)
    return _masked_sums(_chunk_masks(nrows)[0], g), jnp.where(live, _sigmoid(-zg) * (1.0 / GLA_TAU), 0.0)


def _tril64():
    ii = lax.broadcasted_iota(jnp.int32, (GLA_CHUNK, GLA_CHUNK), 0)
    jj = lax.broadcasted_iota(jnp.int32, (GLA_CHUNK, GLA_CHUNK), 1)
    return jj <= ii


def _gla_fwd(proj, wg_p, bg, gnw, shards):
    rows = proj.shape[0]
    nc = rows // GLA_CHUNK
    group = _gla_group(nc, 10)
    steps, nrows = nc // group, group * GLA_CHUNK
    ns = len(shards)

    def body(q_ref, k_ref, v_ref, r_ref, lr_ref, lr_next_ref, wg_ref, bg_ref, gnw_ref, *rest):
        shard_refs, rest = rest[:ns], rest[ns:]
        oraw_ref, og_ref, st_ref, decay_ref, dgate_ref = rest[:5]
        gathered_refs, rest = rest[5:5 + ns], rest[5 + ns:]
        state, gates = rest[:2]
        c = pl.program_id(0)

        @pl.when(c == 0)
        def _():
            state[...] = jnp.zeros_like(state)
            gates[0, 0], gates[0, 1] = _gla_gates(lr_ref[...], wg_ref[...], bg_ref[...], 0)

        _place_gather(c, steps, shard_refs, gathered_refs, rest[2:])
        slot = c % 2
        b = gates[slot, 0]
        decay_ref[...] = b
        dgate_ref[...] = gates[slot, 1]
        gates[1 - slot, 0], gates[1 - slot, 1] = _gla_gates(lr_next_ref[...], wg_ref[...], bg_ref[...], (c + 1) * nrows)
        eb = jnp.exp(b)
        gq = q_ref[...] * (GLA_DK ** -0.5) * eb
        gk = k_ref[...] * jnp.exp(-b)
        v = v_ref[...]
        gnw_v = gnw_ref[...]
        tril = _tril64()
        pairs = [(h, gi) for h in range(GLA_HEADS) for gi in range(group)]
        rs = {gi: slice(gi * GLA_CHUNK, (gi + 1) * GLA_CHUNK) for gi in range(group)}
        s64 = {h: slice(h * GLA_DK, (h + 1) * GLA_DK) for h in range(GLA_HEADS)}
        s128 = {h: slice(h * GLA_DV, (h + 1) * GLA_DV) for h in range(GLA_HEADS)}
        qh = {(h, gi): gq[rs[gi], s64[h]] for h, gi in pairs}
        kh = {(h, gi): gk[rs[gi], s64[h]] for h, gi in pairs}
        vh = {(h, gi): v[rs[gi], s128[h]] for h, gi in pairs}
        ebl = {(h, gi): eb[(gi + 1) * GLA_CHUNK - 1:(gi + 1) * GLA_CHUNK, s64[h]] for h, gi in pairs}
        av = {pr: _mm(jnp.where(tril, _mm_nt(qh[pr], kh[pr]), 0.0), vh[pr]) for pr in pairs}
        inc = {pr: _mm_tn(vh[pr], kh[pr] * ebl[pr]) for pr in pairs}
        st = {}
        for h in range(GLA_HEADS):
            cur = state[h]
            for gi in range(group):
                st[h, gi] = cur
                st_ref[gi, h] = cur
                cur = cur * ebl[h, gi] + inc[h, gi]
            state[h] = cur
        for h, gi in pairs:
            o = av[h, gi] + _mm_nt(qh[h, gi], st[h, gi])
            oraw_ref[rs[gi], s128[h]] = o
            rstd = lax.rsqrt(jnp.mean(o * o, axis=-1, keepdims=True) + EPS)
            rh = r_ref[rs[gi], s128[h]]
            og_ref[rs[gi], s128[h]] = (o * rstd * gnw_v * (rh * _sigmoid(rh))).astype(ACT_DTYPE)

    nb = lambda w, col: pl.BlockSpec((nrows, w), lambda c: (c, col // w))
    const = lambda shape: pl.BlockSpec(shape, lambda c: (0,) * len(shape))
    outs = pl.pallas_call(
        body, name="gla_fwd", grid=(steps,),
        in_specs=[nb(256, C_GQ), nb(256, C_GK), nb(512, C_GV), nb(512, C_GR), nb(128, C_LR),
                  pl.BlockSpec((nrows, 128), lambda c: (jnp.minimum(c + 1, steps - 1), C_LR // 128)),
                  const((128, 256)), const((1, 256)), const((1, 128))] + [ANY] * ns,
        out_specs=[nb(512, 0), nb(512, 0),
                   pl.BlockSpec((group, GLA_HEADS, GLA_DV, GLA_DK), lambda c: (c, 0, 0, 0)),
                   nb(256, 0), nb(256, 0)] + [ANY] * ns,
        out_shape=[jax.ShapeDtypeStruct((rows, 512), F32), jax.ShapeDtypeStruct((rows, 512), ACT_DTYPE),
                   jax.ShapeDtypeStruct((nc, GLA_HEADS, GLA_DV, GLA_DK), F32),
                   jax.ShapeDtypeStruct((rows, 256), F32), jax.ShapeDtypeStruct((rows, 256), F32)]
        + _gathered_shapes(shards),
        scratch_shapes=[pltpu.VMEM((GLA_HEADS, GLA_DV, GLA_DK), F32), pltpu.VMEM((2, 2, nrows, 256), F32)]
        + _gather_sems(ns),
        compiler_params=_cp(("arbitrary",)),
    )(proj, proj, proj, proj, proj, proj, wg_p, bg, gnw, *shards)
    return outs[0], outs[1], outs[2], outs[3], outs[4], _with_own_block(outs[5:], shards)


def _swa_mask(n):
    shape = (SWA_GROUP * SWA_BLOCK, 3 * SWA_BLOCK)
    qi = lax.broadcasted_iota(jnp.int32, shape, 0) & (SWA_BLOCK - 1)
    jj = lax.broadcasted_iota(jnp.int32, shape, 1)
    meta = (jj < SWA_BLOCK) & (jj >= META0) & ((n > 0) | (jj <= qi))
    prev = (jj >= SWA_BLOCK) & (jj < 2 * SWA_BLOCK) & (n >= 2) & (jj - SWA_BLOCK > qi)
    cur = (jj >= 2 * SWA_BLOCK) & (n >= 1) & (jj - 2 * SWA_BLOCK <= qi)
    return meta | prev | cur


def _stack_heads(t, kvh):
    return jnp.concatenate([t[:, (kvh * SWA_GROUP + g) * SWA_HD:(kvh * SWA_GROUP + g + 1) * SWA_HD]
                            for g in range(SWA_GROUP)], axis=0)


def _stack_sinks(sink_ref, kvh):
    return jnp.concatenate([jnp.full((SWA_BLOCK, 1), sink_ref[0, kvh * SWA_GROUP + g], F32)
                            for g in range(SWA_GROUP)], axis=0)


def _swa_group(nblk):
    return 13 if nblk % 13 == 0 else (5 if nblk % 5 == 0 else 1)


def _swa_specs(group):
    blk = lambda w: pl.BlockSpec((group * SWA_BLOCK, w), lambda n: (n, 0))
    first = pl.BlockSpec((SWA_BLOCK, 128), lambda n: (0, 0))
    prev = pl.BlockSpec((SWA_BLOCK, 128), lambda n: (jnp.maximum(n * group - 1, 0), 0))
    return blk, first, prev


def _swa_keys(first_ref, prev_ref, cur_ref, g):
    own = cur_ref[g * SWA_BLOCK:(g + 1) * SWA_BLOCK, :]
    before = prev_ref[...] if g == 0 else cur_ref[(g - 1) * SWA_BLOCK:g * SWA_BLOCK, :]
    return jnp.concatenate([first_ref[...], before, own], axis=0)


def _swa_fwd(qr, kr, vr, sinks, shards):
    rows = qr.shape[0]
    nblk = rows // SWA_BLOCK
    group = _swa_group(nblk)
    steps = nblk // group
    ns = len(shards)

    def body(q_ref, k0, kp, kc, v0, vp, vc, sink_ref, *rest):
        o_ref = rest[ns]
        _place_gather(pl.program_id(0), steps, rest[:ns], rest[ns + 1:2 * ns + 1], rest[2 * ns + 1:])
        for g in range(group):
            n = pl.program_id(0) * group + g
            rs = slice(g * SWA_BLOCK, (g + 1) * SWA_BLOCK)
            kall, vall = _swa_keys(k0, kp, kc, g), _swa_keys(v0, vp, vc, g)
            mask = _swa_mask(n)[0:SWA_BLOCK]
            heads = range(SWA_HEADS)
            hs = [slice(h * SWA_HD, (h + 1) * SWA_HD) for h in heads]
            kv = [slice((h // SWA_GROUP) * SWA_HD, (h // SWA_GROUP + 1) * SWA_HD) for h in heads]
            s = [jnp.where(mask, _mm_nt(q_ref[rs, hs[h]], kall[:, kv[h]]), NEG) for h in heads]
            m = [jnp.maximum(jnp.max(s[h], axis=-1, keepdims=True), sink_ref[0, h]) for h in heads]
            p = [jnp.exp(s[h] - m[h]) for h in heads]
            den = [jnp.sum(p[h], axis=-1, keepdims=True) + jnp.exp(sink_ref[0, h] - m[h]) for h in heads]
            o = [_mm(p[h], vall[:, kv[h]]) for h in heads]
            for h in heads:
                o_ref[rs, hs[h]] = (o[h] / den[h]).astype(ACT_DTYPE)

    blk, first, prev = _swa_specs(group)
    outs = pl.pallas_call(
        body, name="swa_fwd", grid=(steps,),
        in_specs=[blk(512), first, prev, blk(128), first, prev, blk(128),
                  pl.BlockSpec(memory_space=pltpu.SMEM)] + [ANY] * ns,
        out_specs=[blk(512)] + [ANY] * ns,
        out_shape=[jax.ShapeDtypeStruct((rows, 512), ACT_DTYPE)] + _gathered_shapes(shards),
        scratch_shapes=_gather_sems(ns),
        compiler_params=_cp(("arbitrary",)),
    )(qr, kr, kr, kr, vr, vr, vr, sinks, *shards)
    return outs[0], _with_own_block(outs[1:], shards)


def _out_proj(x, lead, og, osw, wout, nfw, tm):
    rows = LEAD + x.shape[0]
    nb = tm // LEAD

    def body(*refs):
        x_refs, (lead_ref, og_ref, os_ref, w_ref, nw_ref, h1_ref, f_ref, ft_ref) = refs[:nb], refs[nb:]
        h0 = _h_tile(pl.program_id(0), lead_ref, x_refs)
        h1 = h0 + _mm(og_ref[...], w_ref[0:512, :]) + _mm(os_ref[...], w_ref[512:1024, :])
        h1_ref[...] = h1
        rstd = lax.rsqrt(jnp.mean(h1 * h1, axis=-1, keepdims=True) + EPS)
        f = h1 * rstd * nw_ref[...]
        f_ref[...] = f.astype(ACT_DTYPE)
        ft_ref[...] = f.T.astype(ACT_DTYPE)

    row = lambda w: pl.BlockSpec((tm, w), lambda i: (i, 0))
    return pl.pallas_call(
        body, name="out_proj", grid=(rows // tm,),
        in_specs=_token_specs(tm) + [pl.BlockSpec((LEAD, D), lambda i: (0, 0)), row(512), row(512),
                                     pl.BlockSpec((D, D), lambda i: (0, 0)), pl.BlockSpec((1, D), lambda i: (0, 0))],
        out_specs=[row(D), row(D), pl.BlockSpec((D, tm), lambda i: (0, i))],
        out_shape=[jax.ShapeDtypeStruct((rows, D), F32), jax.ShapeDtypeStruct((rows, D), ACT_DTYPE),
                   jax.ShapeDtypeStruct((D, rows), ACT_DTYPE)],
        compiler_params=_cp(("arbitrary",), VMEM_TILE_MB),
    )(*([x] * nb), lead, og, osw, wout, nfw)


def _ffn_fwd(f, h1, w1, w2, tgt, fnw, tm):
    rows = f.shape[0]
    nj = D_FF // FF_WIDE
    nb = tm // LEAD

    def body(f_ref, h1_ref, w1_ref, w2_ref, nw_ref, *rest):
        t_refs, (a_ref, dh2_ref, dh2t_ref, loss_ref, gfn_ref, acc) = rest[:nb], rest[nb:]
        i, j = pl.program_id(0), pl.program_id(1)

        @pl.when((i == 0) & (j == 0))
        def _():
            loss_ref[...] = jnp.zeros_like(loss_ref)
            gfn_ref[...] = jnp.zeros_like(gfn_ref)

        @pl.when(j == 0)
        def _():
            acc[...] = jnp.zeros_like(acc)

        a = _mm(f_ref[...], w1_ref[...])
        a_ref[...] = a.astype(ACT_DTYPE)
        z = jnp.square(jnp.maximum(a, 0.0))
        acc[...] += _mm(z, w2_ref[...])

        @pl.when(j == nj - 1)
        def _():
            h2 = h1_ref[...] + acc[...]
            rstd = lax.rsqrt(jnp.mean(h2 * h2, axis=-1, keepdims=True) + EPS)
            hn = h2 * rstd
            nw = nw_ref[...]
            row = i * tm + lax.broadcasted_iota(jnp.int32, (tm, 1), 0)
            target = jnp.concatenate([t[...] for t in t_refs], axis=0)
            err = jnp.where(row >= LEAD, hn * nw - target, 0.0)
            row_loss = jnp.sum(err * err, axis=-1, keepdims=True) * (1.0 / D)
            loss_ref[...] += jnp.broadcast_to(0.5 * jnp.sum(row_loss, axis=0, keepdims=True), loss_ref.shape)
            dy = err * (1.0 / D)
            gfn_ref[...] += jnp.broadcast_to(jnp.sum(dy * hn, axis=0, keepdims=True), gfn_ref.shape)
            dhn = dy * nw
            dh2 = rstd * (dhn - hn * jnp.mean(dhn * hn, axis=-1, keepdims=True))
            dh2_ref[...] = dh2
            dh2t_ref[...] = dh2.T.astype(ACT_DTYPE)

    return pl.pallas_call(
        body, name="ffn_fwd", grid=(rows // tm, nj),
        in_specs=[pl.BlockSpec((tm, D), lambda i, j: (i, 0)), pl.BlockSpec((tm, D), lambda i, j: (i, 0)),
                  pl.BlockSpec((D, FF_WIDE), lambda i, j: (0, j)),
                  pl.BlockSpec((FF_WIDE, D), lambda i, j: (j, 0)),
                  pl.BlockSpec((1, D), lambda i, j: (0, 0))] + _token_specs(tm, grid_rank=2),
        out_specs=[pl.BlockSpec((tm, FF_WIDE), lambda i, j: (i, j)), pl.BlockSpec((tm, D), lambda i, j: (i, 0)),
                   pl.BlockSpec((D, tm), lambda i, j: (0, i)),
                   pl.BlockSpec((8, 128), lambda i, j: (0, 0)), pl.BlockSpec((8, D), lambda i, j: (0, 0))],
        out_shape=[jax.ShapeDtypeStruct((rows, D_FF), ACT_DTYPE), jax.ShapeDtypeStruct((rows, D), F32),
                   jax.ShapeDtypeStruct((D, rows), ACT_DTYPE),
                   jax.ShapeDtypeStruct((8, 128), F32), jax.ShapeDtypeStruct((8, D), F32)],
        scratch_shapes=[pltpu.VMEM((tm, D), F32)],
        compiler_params=_cp(("arbitrary", "arbitrary"), VMEM_WIDE_MB),
    )(f, h1, w1, w2, fnw, *([tgt] * nb))


def _ffn_bwd_act(dh2, a, w1, w2, h1, nfw, tm):
    rows = dh2.shape[0]
    nj = D_FF // FF_WIDE

    def body(dh2_ref, a_ref, w1_ref, w2_ref, h1_ref, nw_ref, da_ref, dh1_ref, gnf_ref, acc):
        i, j = pl.program_id(0), pl.program_id(1)

        @pl.when((i == 0) & (j == 0))
        def _():
            gnf_ref[...] = jnp.zeros_like(gnf_ref)

        @pl.when(j == 0)
        def _():
            acc[...] = jnp.zeros_like(acc)

        dz = _mm_nt(dh2_ref[...], w2_ref[...])
        da = dz * (2.0 * jnp.maximum(a_ref[...].astype(F32), 0.0))
        da_ref[...] = da.astype(ACT_DTYPE)
        acc[...] += _mm_nt(da, w1_ref[...])

        @pl.when(j == nj - 1)
        def _():
            h1 = h1_ref[...]
            rstd = lax.rsqrt(jnp.mean(h1 * h1, axis=-1, keepdims=True) + EPS)
            hn = h1 * rstd
            df = acc[...]
            gnf_ref[...] += jnp.broadcast_to(jnp.sum(df * hn, axis=0, keepdims=True), gnf_ref.shape)
            dfn = df * nw_ref[...]
            dh1_ref[...] = dh2_ref[...] + rstd * (dfn - hn * jnp.mean(dfn * hn, axis=-1, keepdims=True))

    return pl.pallas_call(
        body, name="ffn_bwd_act", grid=(rows // tm, nj),
        in_specs=[pl.BlockSpec((tm, D), lambda i, j: (i, 0)), pl.BlockSpec((tm, FF_WIDE), lambda i, j: (i, j)),
                  pl.BlockSpec((D, FF_WIDE), lambda i, j: (0, j)),
                  pl.BlockSpec((FF_WIDE, D), lambda i, j: (j, 0)),
                  pl.BlockSpec((tm, D), lambda i, j: (i, 0)), pl.BlockSpec((1, D), lambda i, j: (0, 0))],
        out_specs=[pl.BlockSpec((tm, FF_WIDE), lambda i, j: (i, j)), pl.BlockSpec((tm, D), lambda i, j: (i, 0)),
                   pl.BlockSpec((8, D), lambda i, j: (0, 0))],
        out_shape=[jax.ShapeDtypeStruct((rows, D_FF), ACT_DTYPE), jax.ShapeDtypeStruct((rows, D), F32),
                   jax.ShapeDtypeStruct((8, D), F32)],
        scratch_shapes=[pltpu.VMEM((tm, D), F32)],
        compiler_params=_cp(("arbitrary", "arbitrary"), VMEM_WIDE_MB),
    )(dh2, a, w1, w2, h1, nfw)


def _ffn_bwd_weights(ft, a, da, dh2t, tm):
    rows = a.shape[0]
    steps = rows // tm
    pair = 2 * FF_TILE

    def body(ft_ref, a_ref, da_ref, dh2t_ref, dw1_ref, dw2_ref, dw2t):
        i = pl.program_id(1)

        @pl.when(i == 0)
        def _():
            dw1_ref[...] = jnp.zeros_like(dw1_ref)
            dw2t[...] = jnp.zeros_like(dw2t)

        z = jnp.square(jnp.maximum(a_ref[...].astype(F32), 0.0))
        dw1 = _mm(ft_ref[...], da_ref[...])
        for core in range(2):
            dw1_ref[core] += dw1[:, core * FF_TILE:(core + 1) * FF_TILE]
        dw2t[...] += _mm(dh2t_ref[...], z)

        @pl.when(i == steps - 1)
        def _():
            for core in range(2):
                dw2_ref[core] = dw2t[:, core * FF_TILE:(core + 1) * FF_TILE].T

    return pl.pallas_call(
        body, name="ffn_bwd_weights", grid=(N_DEV // 2, steps),
        in_specs=[pl.BlockSpec((D, tm), lambda j, i: (0, i)), pl.BlockSpec((tm, pair), lambda j, i: (i, j)),
                  pl.BlockSpec((tm, pair), lambda j, i: (i, j)), pl.BlockSpec((D, tm), lambda j, i: (0, i))],
        out_specs=[pl.BlockSpec((2, None, D, FF_TILE), lambda j, i: (0, j, 0, 0)),
                   pl.BlockSpec((2, None, FF_TILE, D), lambda j, i: (0, j, 0, 0))],
        out_shape=[jax.ShapeDtypeStruct((2, 4, D, FF_TILE), F32), jax.ShapeDtypeStruct((2, 4, FF_TILE, D), F32)],
        scratch_shapes=[pltpu.VMEM((D, pair), F32)],
        compiler_params=_cp(("arbitrary", "arbitrary"), VMEM_WIDE_MB),
    )(ft, a, da, dh2t)


def _out_proj_bwd(dh1, og, osw, wout, tm, partials):
    rows = dh1.shape[0]
    steps = rows // tm
    ns = len(partials)

    def body(dh1_ref, og_ref, os_ref, w_ref, *rest):
        part_refs, rest = rest[:ns], rest[ns:]
        dog_ref, dos_ref, dw_ref = rest[:3]
        land_refs, (send_sems, recv_sems) = rest[3:3 + ns], rest[3 + ns:]
        i = pl.program_id(0)
        start, finish = _sibling_schedule(part_refs, land_refs, send_sems, recv_sems)

        @pl.when(i == 0)
        def _():
            dw_ref[...] = jnp.zeros_like(dw_ref)
            start()

        pl.when(i == steps - 1)(finish)

        dh1 = dh1_ref[...].astype(MXU_DTYPE)
        dog_ref[...] = _mm_nt(dh1, w_ref[0:512, :])
        dos_ref[...] = _mm_nt(dh1, w_ref[512:1024, :])
        for half, ref in enumerate((og_ref, os_ref)):
            dw = _mm_tn(ref[...], dh1)
            for blk in range(4):
                shard = half * 4 + blk
                dw_ref[shard % 2, shard // 2] += dw[blk * 128:(blk + 1) * 128, :]

    row = lambda w: pl.BlockSpec((tm, w), lambda i: (i, 0))
    outs = pl.pallas_call(
        body, name="out_proj_bwd", grid=(steps,),
        in_specs=[row(D), row(512), row(512), pl.BlockSpec((D, D), lambda i: (0, 0))] + [ANY] * ns,
        out_specs=[row(512), row(512), pl.BlockSpec((2, 4, 128, D), lambda i: (0, 0, 0, 0))] + [ANY] * ns,
        out_shape=[jax.ShapeDtypeStruct((rows, 512), F32), jax.ShapeDtypeStruct((rows, 512), F32),
                   jax.ShapeDtypeStruct((2, 4, 128, D), F32)] + _sibling_shapes(partials),
        scratch_shapes=_sibling_sems(ns),
        compiler_params=_cp(("arbitrary",), VMEM_TILE_MB),
    )(dh1, og, osw, wout, *partials)
    return outs[0], outs[1], outs[2], outs[3:]


def _swa_bwd(qr, kr, vr, osw, dos, sinks, jobs):
    rows = qr.shape[0]
    nblk = rows // SWA_BLOCK
    group = _swa_group(nblk)
    steps = nblk // group
    ns = jobs.n

    def body(q_ref, k0, kp, kc, v0, vp, vc, o_ref, do_ref, sink_ref, *rest):
        dq_ref, dk_ref, dv_ref, dsink_ref = rest[ns:ns + 4]
        start, finish = jobs.bind(rest[:ns], rest[ns + 4:2 * ns + 4], rest[2 * ns + 4:])
        step = pl.program_id(0)

        @pl.when(step == 0)
        def _():
            dk_ref[...] = jnp.zeros_like(dk_ref)
            dv_ref[...] = jnp.zeros_like(dv_ref)
            dsink_ref[...] = jnp.zeros_like(dsink_ref)
            start()

        pl.when(step == steps - 1)(finish)
        for g in range(group):
            block(step * group + g, g, q_ref, k0, kp, kc, v0, vp, vc, o_ref, do_ref, sink_ref,
                  dq_ref, dk_ref, dv_ref, dsink_ref)

    def block(n, g, q_ref, k0, kp, kc, v0, vp, vc, o_ref, do_ref, sink_ref, dq_ref, dk_ref, dv_ref, dsink_ref):
        rs = slice(g * SWA_BLOCK, (g + 1) * SWA_BLOCK)
        kall, vall = _swa_keys(k0, kp, kc, g), _swa_keys(v0, vp, vc, g)
        mask = _swa_mask(n)[0:SWA_BLOCK]
        heads = range(SWA_HEADS)
        hs = [slice(h * SWA_HD, (h + 1) * SWA_HD) for h in heads]
        kv = [slice((h // SWA_GROUP) * SWA_HD, (h // SWA_GROUP + 1) * SWA_HD) for h in heads]
        sink = [sink_ref[0, h] for h in heads]
        qh = [q_ref[rs, hs[h]] for h in heads]
        doh = [do_ref[rs, hs[h]] for h in heads]
        s = [jnp.where(mask, _mm_nt(qh[h], kall[:, kv[h]]), NEG) for h in heads]
        dp = [_mm_nt(doh[h], vall[:, kv[h]]) for h in heads]
        delta = [jnp.sum(doh[h] * o_ref[rs, hs[h]].astype(F32), axis=-1, keepdims=True) for h in heads]
        m = [jnp.maximum(jnp.max(s[h], axis=-1, keepdims=True), sink[h]) for h in heads]
        e = [jnp.exp(s[h] - m[h]) for h in heads]
        inv = [1.0 / (jnp.sum(e[h], axis=-1, keepdims=True) + jnp.exp(sink[h] - m[h])) for h in heads]
        p = [e[h] * inv[h] for h in heads]
        ds = [p[h] * (dp[h] - delta[h]) for h in heads]
        dq = [_mm(ds[h], kall[:, kv[h]]) for h in heads]
        dkh = [_mm_tn(ds[h], qh[h]) for h in heads]
        dvh = [_mm_tn(p[h], doh[h]) for h in heads]
        for h in heads:
            dsink = -jnp.sum(jnp.exp(sink[h] - m[h]) * inv[h] * delta[h], axis=0, keepdims=True)
            dsink_ref[h:h + 1, :] += jnp.broadcast_to(dsink, (1, 128))
        dq_ref[rs, :] = jnp.concatenate(dq, axis=1).astype(ACT_DTYPE)
        group_sum = lambda parts, kvh: sum(parts[kvh * SWA_GROUP + 1:(kvh + 1) * SWA_GROUP], parts[kvh * SWA_GROUP])
        dk_all = jnp.concatenate([group_sum(dkh, kvh) for kvh in range(SWA_KV)], axis=1)
        dv_all = jnp.concatenate([group_sum(dvh, kvh) for kvh in range(SWA_KV)], axis=1)
        prev0 = pl.multiple_of(jnp.maximum(n - 1, 0) * SWA_BLOCK, SWA_BLOCK)
        cur0 = pl.multiple_of(n * SWA_BLOCK, SWA_BLOCK)
        for ref, val in ((dk_ref, dk_all), (dv_ref, dv_all)):
            ref[0:SWA_BLOCK, :] += val[0:SWA_BLOCK]
            ref[pl.ds(prev0, SWA_BLOCK), :] += val[SWA_BLOCK:2 * SWA_BLOCK]
            ref[pl.ds(cur0, SWA_BLOCK), :] += val[2 * SWA_BLOCK:]

    blk, first, prev = _swa_specs(group)
    whole = pl.BlockSpec((rows, 128), lambda n: (0, 0))
    outs = pl.pallas_call(
        body, name="swa_bwd", grid=(steps,),
        in_specs=[blk(512), first, prev, blk(128), first, prev, blk(128), blk(512), blk(512),
                  pl.BlockSpec(memory_space=pltpu.SMEM)] + [ANY] * ns,
        out_specs=[blk(512), whole, whole, pl.BlockSpec((8, 128), lambda n: (0, 0))] + [ANY] * ns,
        out_shape=[jax.ShapeDtypeStruct((rows, 512), ACT_DTYPE), jax.ShapeDtypeStruct((rows, 128), F32),
                   jax.ShapeDtypeStruct((rows, 128), F32), jax.ShapeDtypeStruct((8, 128), F32)] + jobs.out_shapes,
        scratch_shapes=jobs.sems,
        compiler_params=_cp(("arbitrary",), VMEM_TILE_MB),
    )(qr, kr, kr, kr, vr, vr, vr, osw, dos, sinks, *jobs.inputs)
    return outs[0], outs[1], outs[2], outs[3], jobs.split(outs[4:])


def _gla_bwd(proj, decay, dgate, oraw, states, dog, wg_p, gnw, jobs):
    rows = proj.shape[0]
    nc = rows // GLA_CHUNK
    group = _gla_group(nc, 5)
    steps, nrows = nc // group, group * GLA_CHUNK
    ns = jobs.n

    def body(q_ref, k_ref, v_ref, r_ref, lr_ref, b_ref, dgate_ref, oraw_ref, st_ref, dog_ref, wg_ref, gnw_ref, *rest):
        dq_ref, dk_ref, dv_ref, dr_ref, dlr_ref, dwg_ref, dbg_ref, dgnw_ref = rest[ns:ns + 8]
        dstate, db_scr = rest[2 * ns + 8:2 * ns + 10]
        start, finish = jobs.bind(rest[:ns], rest[ns + 8:2 * ns + 8], rest[2 * ns + 10:])
        t = pl.program_id(0)

        @pl.when(t == 0)
        def _():
            dstate[...] = jnp.zeros_like(dstate)
            dwg_ref[...] = jnp.zeros_like(dwg_ref)
            dbg_ref[...] = jnp.zeros_like(dbg_ref)
            dgnw_ref[...] = jnp.zeros_like(dgnw_ref)
            start()

        pl.when(t == steps - 1)(finish)

        lr, wg = lr_ref[...], wg_ref[...]
        b = b_ref[...]
        eb, enb = jnp.exp(b), jnp.exp(-b)
        scale = GLA_DK ** -0.5
        gq = q_ref[...] * scale * eb
        gk = k_ref[...] * enb
        v = v_ref[...]
        gnw_v = gnw_ref[...]
        tril = _tril64()
        is_last = lax.broadcasted_iota(jnp.int32, (GLA_CHUNK, 1), 0) == GLA_CHUNK - 1
        dgnw = jnp.zeros((1, GLA_DV), F32)
        pairs = [(h, gi) for h in range(GLA_HEADS) for gi in range(group)]
        rs = {gi: slice(gi * GLA_CHUNK, (gi + 1) * GLA_CHUNK) for gi in range(group)}
        s64 = {h: slice(h * GLA_DK, (h + 1) * GLA_DK) for h in range(GLA_HEADS)}
        s128 = {h: slice(h * GLA_DV, (h + 1) * GLA_DV) for h in range(GLA_HEADS)}
        qh = {(h, gi): gq[rs[gi], s64[h]] for h, gi in pairs}
        kh = {(h, gi): gk[rs[gi], s64[h]] for h, gi in pairs}
        vh = {(h, gi): v[rs[gi], s128[h]] for h, gi in pairs}
        ebl = {(h, gi): eb[(gi + 1) * GLA_CHUNK - 1:(gi + 1) * GLA_CHUNK, s64[h]] for h, gi in pairs}
        kl = {pr: kh[pr] * ebl[pr] for pr in pairs}
        st = {(h, gi): st_ref[gi, h] for h, gi in pairs}
        do = {}
        for h, gi in pairs:
            o, rh, dout = oraw_ref[rs[gi], s128[h]], r_ref[rs[gi], s128[h]], dog_ref[rs[gi], s128[h]]
            rstd = lax.rsqrt(jnp.mean(o * o, axis=-1, keepdims=True) + EPS)
            on = o * rstd
            sg = _sigmoid(rh)
            dr_ref[rs[gi], s128[h]] = (dout * (on * gnw_v) * (sg * (1.0 + rh * (1.0 - sg)))).astype(ACT_DTYPE)
            dy = dout * (rh * sg)
            dgnw = dgnw + jnp.sum(dy * on, axis=0, keepdims=True)
            don = dy * gnw_v
            do[h, gi] = rstd * (don - on * jnp.mean(don * on, axis=-1, keepdims=True))
        a = {pr: jnp.where(tril, _mm_nt(qh[pr], kh[pr]), 0.0) for pr in pairs}
        da = {pr: jnp.where(tril, _mm_nt(do[pr], vh[pr]), 0.0) for pr in pairs}
        dinc = {pr: _mm_tn(do[pr], qh[pr]) for pr in pairs}
        dgq = {pr: _mm(da[pr], kh[pr]) + _mm(do[pr], st[pr]) for pr in pairs}
        dgk = {pr: _mm_tn(da[pr], qh[pr]) for pr in pairs}
        dv_a = {pr: _mm_tn(a[pr], do[pr]) for pr in pairs}
        dsp = {}
        for h in range(GLA_HEADS):
            cur = dstate[h]
            for gi in reversed(range(group)):
                dsp[h, gi] = cur
                cur = cur * ebl[h, gi] + dinc[h, gi]
            dstate[h] = cur
        for h, gi in pairs:
            pr = (h, gi)
            dkl = _mm(vh[pr], dsp[pr])
            dv_ref[rs[gi], s128[h]] = (dv_a[pr] + _mm_nt(kl[pr], dsp[pr])).astype(ACT_DTYPE)
            debl = jnp.sum(dsp[pr] * st[pr], axis=0, keepdims=True)
            dq_ref[rs[gi], s64[h]] = (dgq[pr] * (scale * eb[rs[gi], s64[h]])).astype(ACT_DTYPE)
            dk_ref[rs[gi], s64[h]] = ((dgk[pr] + dkl * ebl[pr]) * enb[rs[gi], s64[h]]).astype(ACT_DTYPE)
            last = debl * ebl[pr] + jnp.sum(dkl * kl[pr], axis=0, keepdims=True)
            db_scr[rs[gi], s64[h]] = (dgq[pr] * qh[pr] - dgk[pr] * kh[pr] - dkl * kl[pr]
                                      + jnp.where(is_last, last, 0.0))
        dzg = _masked_sums(_chunk_masks(nrows)[1], db_scr[...]) * dgate_ref[...]
        dlr_ref[...] = _mm_nt(dzg, wg).astype(ACT_DTYPE)
        dwg_ref[...] += _mm_tn(lr, dzg)
        dbg_ref[...] += jnp.broadcast_to(jnp.sum(dzg, axis=0, keepdims=True), dbg_ref.shape)
        dgnw_ref[...] += jnp.broadcast_to(dgnw, dgnw_ref.shape)

    nb = lambda w, col: pl.BlockSpec((nrows, w), lambda t: (steps - 1 - t, col // w))
    const = lambda shape: pl.BlockSpec(shape, lambda t: (0,) * len(shape))
    outs = pl.pallas_call(
        body, name="gla_bwd", grid=(steps,),
        in_specs=[nb(256, C_GQ), nb(256, C_GK), nb(512, C_GV), nb(512, C_GR), nb(128, C_LR), nb(256, 0), nb(256, 0),
                  nb(512, 0),
                  pl.BlockSpec((group, GLA_HEADS, GLA_DV, GLA_DK), lambda t: (steps - 1 - t, 0, 0, 0)), nb(512, 0),
                  const((128, 256)), const((1, 128))] + [ANY] * ns,
        out_specs=[nb(256, 0), nb(256, 0), nb(512, 0), nb(512, 0), nb(128, 0),
                   const((128, 256)), const((8, 256)), const((8, 128))] + [ANY] * ns,
        out_shape=[jax.ShapeDtypeStruct((rows, 256), ACT_DTYPE), jax.ShapeDtypeStruct((rows, 256), ACT_DTYPE),
                   jax.ShapeDtypeStruct((rows, 512), ACT_DTYPE), jax.ShapeDtypeStruct((rows, 512), ACT_DTYPE),
                   jax.ShapeDtypeStruct((rows, 128), ACT_DTYPE), jax.ShapeDtypeStruct((128, 256), F32),
                   jax.ShapeDtypeStruct((8, 256), F32), jax.ShapeDtypeStruct((8, 128), F32)] + jobs.out_shapes,
        scratch_shapes=[pltpu.VMEM((GLA_HEADS, GLA_DV, GLA_DK), F32), pltpu.VMEM((nrows, 256), F32)] + jobs.sems,
        compiler_params=_cp(("arbitrary",)),
    )(proj, proj, proj, proj, proj, decay, dgate, oraw, states, dog, wg_p, gnw, *jobs.inputs)
    return outs[:8], jobs.split(outs[8:])


def _in_proj_bwd(x, lead, dh1, nw, win_p, dgv, dgr, dsq, dgq, dgk, dsk, dsv, dlr, angles, tm):
    seq = x.shape[0]
    rows = LEAD + seq
    nb = tm // LEAD
    steps = rows // tm

    def first_copy(scr, gx_ref, sem):
        return pltpu.make_async_copy(scr.at[pl.ds(LEAD, tm - LEAD)], gx_ref.at[pl.ds(0, tm - LEAD)], sem)

    def tile_copy(scr, gx_ref, sem, step):
        start = pl.multiple_of(jnp.maximum(step * tm - LEAD, 0), LEAD)
        return pltpu.make_async_copy(scr, gx_ref.at[pl.ds(start, tm)], sem)

    def body(*refs):
        x_refs, refs = refs[:nb], refs[nb:]
        (lead_ref, dh1_ref, nw_ref, w_ref, dgv_ref, dgr_ref, dsq_ref, dgq_ref, dgk_ref, dsk_ref, dsv_ref, dlr_ref,
         cs_ref, gx_ref, dlead_ref, dproj_ref, ut_ref, gnm_ref, scr, sem) = refs
        i = pl.program_id(0)

        @pl.when(i == 0)
        def _():
            gnm_ref[...] = jnp.zeros_like(gnm_ref)

        cos, sa, sb = _rope_tables(cs_ref[...])
        dsq_v = (_unrope(dsq_ref[...].astype(F32), cos, sa, sb) * (SWA_HD ** -0.5)).astype(MXU_DTYPE)
        dsk_v = _unrope(dsk_ref[...], cos, sa, sb).astype(MXU_DTYPE)
        dproj = jnp.concatenate(
            [dgv_ref[...].astype(MXU_DTYPE), dgr_ref[...].astype(MXU_DTYPE), dgq_ref[...].astype(MXU_DTYPE),
             dgk_ref[...].astype(MXU_DTYPE), dlr_ref[...].astype(MXU_DTYPE), dsq_v, dsk_v,
             dsv_ref[...].astype(MXU_DTYPE)],
            axis=1)
        dproj_ref[...] = dproj
        h = _h_tile(i, lead_ref, x_refs)
        rstd = lax.rsqrt(jnp.mean(h * h, axis=-1, keepdims=True) + EPS)
        hn = h * rstd
        nw_v = nw_ref[...]
        ut_ref[...] = (hn * nw_v).T.astype(ACT_DTYPE)
        du = _mm_nt(dproj, w_ref[...])
        gnm_ref[...] += jnp.broadcast_to(jnp.sum(du * hn, axis=0, keepdims=True), gnm_ref.shape)
        dun = du * nw_v
        dh0 = dh1_ref[...] + rstd * (dun - hn * jnp.mean(dun * hn, axis=-1, keepdims=True))

        if tm > LEAD:
            pl.when(i == 1)(lambda: first_copy(scr, gx_ref, sem).wait())
        pl.when(i > 1)(lambda: tile_copy(scr, gx_ref, sem, i).wait())
        scr[...] = dh0

        @pl.when(i == 0)
        def _():
            dlead_ref[...] = dh0[0:LEAD]
            if tm > LEAD:
                first_copy(scr, gx_ref, sem).start()
                if steps == 1:
                    first_copy(scr, gx_ref, sem).wait()

        @pl.when(i > 0)
        def _():
            tile_copy(scr, gx_ref, sem, i).start()

        if steps > 1:
            pl.when(i == steps - 1)(lambda: tile_copy(scr, gx_ref, sem, i).wait())

    row = lambda w: pl.BlockSpec((tm, w), lambda i: (i, 0))
    const = lambda shape: pl.BlockSpec(shape, lambda i: (0,) * len(shape))
    return pl.pallas_call(
        body, name="in_proj_bwd", grid=(steps,),
        in_specs=_token_specs(tm) + [const((LEAD, D)), row(D), const((1, D)), const((D, DINP)),
                                     row(512), row(512), row(512), row(256), row(256), row(128), row(128), row(128),
                                     row(ROPE_DIM)],
        out_specs=[ANY, const((LEAD, D)), row(DINP), pl.BlockSpec((D, tm), lambda i: (0, i)), const((8, D))],
        out_shape=[jax.ShapeDtypeStruct((seq, D), F32), jax.ShapeDtypeStruct((LEAD, D), F32),
                   jax.ShapeDtypeStruct((rows, DINP), ACT_DTYPE), jax.ShapeDtypeStruct((D, rows), ACT_DTYPE),
                   jax.ShapeDtypeStruct((8, D), F32)],
        scratch_shapes=[pltpu.VMEM((tm, D), F32), pltpu.SemaphoreType.DMA],
        compiler_params=_cp(("arbitrary",), VMEM_WIDE_MB),
    )(*([x] * nb), lead, dh1, nw, win_p, dgv, dgr, dsq, dgq, dgk, dsk, dsv, dlr, angles)


def _win_runs():
    groups = [(O_GQ, C_GQ), (O_GK, C_GK), (O_GV, C_GV), (O_GR, C_GR), (O_LR, C_LR), (O_SQ, C_SQ), (O_SK, C_SK),
              (O_SV, C_SV)]
    per = DIN // N_DEV
    runs = []
    for (o0, o1), c0 in groups:
        o = o0
        while o < o1:
            d = o // per
            end = min(o1, (d + 1) * per)
            runs.append((d, o - d * per, c0 + o - o0, end - o))
            o = end
    return runs


def _win_padded(g_in):
    tr = 128

    def body(g_ref, o_ref):
        o_ref[...] = jnp.zeros_like(o_ref)
        for d, s, c, w in _win_runs():
            o_ref[:, c:c + w] = g_ref[d, :, s:s + w]

    return pl.pallas_call(
        body, name="w_in_layout", grid=(D // tr,),
        in_specs=[pl.BlockSpec((N_DEV, tr, DIN // N_DEV), lambda i: (0, i, 0))],
        out_specs=pl.BlockSpec((tr, DINP), lambda i: (i, 0)),
        out_shape=jax.ShapeDtypeStruct((D, DINP), g_in.dtype),
        compiler_params=_cp(("arbitrary",)),
    )(g_in)


def _in_proj_bwd_weights(ut, dproj, tm, small):
    rows = dproj.shape[0]
    steps = rows // tm
    per = DIN // N_DEV

    def body(ut_ref, dp_ref, *rest):
        small_refs, (mine_ref, theirs_ref, total_ref, acc, stage, local_sems, send_sems, recv_sems) = rest[:9], rest[9:17]
        i = pl.program_id(0)
        start, finish = _small_sum_schedule(small_refs, total_ref, *rest[17:])
        x, y, c = _mesh_pos()

        @pl.when(i == 0)
        def _():
            acc[...] = jnp.zeros_like(acc)
            start()

        acc[...] += _mm(ut_ref[...], dp_ref[...])
        pl.when(i == steps - 1)(finish)

        def keep(slot, chip):
            return pltpu.make_async_copy(stage.at[slot], mine_ref.at[chip], local_sems.at[slot])

        def send(slot, chip):
            return pltpu.make_async_remote_copy(
                src_ref=stage.at[slot], dst_ref=theirs_ref.at[chip], send_sem=send_sems.at[slot],
                recv_sem=recv_sems.at[chip], device_id=(x, y, 1 - c), device_id_type=MESH)

        def drained(d):
            pl.when(c == d % 2)(keep(d % 2, d // 2).wait)
            pl.when(c != d % 2)(send(d % 2, d // 2).wait_send)

        @pl.when(i == steps - 1)
        def _():
            for d in range(N_DEV):
                slot, chip = d % 2, d // 2
                if d >= 2:
                    drained(d - 2)
                for owner, s, col, w in _win_runs():
                    if owner == d:
                        stage[slot, :, s:s + w] = acc[:, col:col + w]
                pl.when(c == slot)(keep(slot, chip).start)
                pl.when(c != slot)(send(slot, chip).start)
            drained(N_DEV - 2)
            drained(N_DEV - 1)
            for chip in range(4):
                send(0, chip).wait_recv()

    half = jax.ShapeDtypeStruct((4, D, per), F32)
    return pl.pallas_call(
        body, name="in_proj_bwd_weights", grid=(steps,),
        in_specs=[pl.BlockSpec((D, tm), lambda i: (0, i)), pl.BlockSpec((tm, DINP), lambda i: (i, 0))] + SMALL_SPECS,
        out_specs=[ANY, ANY, pl.BlockSpec((SMALL_ROWS, D), lambda i: (0, 0))],
        out_shape=[half, half, jax.ShapeDtypeStruct((SMALL_ROWS, D), F32)],
        scratch_shapes=[pltpu.VMEM((D, DINP), F32), pltpu.VMEM((2, D, per), F32), pltpu.SemaphoreType.DMA((2,)),
                        pltpu.SemaphoreType.DMA((2,)), pltpu.SemaphoreType.DMA((4,))] + _small_sum_scratch(),
        compiler_params=_cp(("arbitrary",), VMEM_WIDE_MB),
    )(ut, dproj, *small)


def _adamw(w, g, m, v):
    m = ADAM_B1 * m + (1.0 - ADAM_B1) * g
    v = ADAM_B2 * v + (1.0 - ADAM_B2) * jnp.square(g)
    m_hat = m / (1.0 - ADAM_B1 ** ADAM_STEP)
    v_hat = v / (1.0 - ADAM_B2 ** ADAM_STEP)
    delta = -ADAM_LR * (m_hat / (jnp.sqrt(v_hat) + ADAM_EPS) + ADAM_WD * w)
    return delta, m, v


ADAM_STEPS = 8


def _adamw_shards(items, name, jobs=None):
    jobs = jobs or _Jobs([])
    ns, nw = jobs.n, len(items)

    def body(*rest):
        ins, rest = rest[:5 * nw], rest[5 * nw:]
        job_ins, rest = rest[:ns], rest[ns:]
        outs, rest = rest[:4 * nw], rest[4 * nw:]
        start, finish = jobs.bind(job_ins, rest[:ns], rest[ns:])
        i = pl.program_id(0)
        pl.when(i == 0)(start)
        pl.when(i == ADAM_STEPS - 1)(finish)
        for k in range(nw):
            p_ref, own_ref, w_ref, m_ref, v_ref = ins[5 * k:5 * k + 5]
            g_ref, d_ref, nm_ref, nv_ref = outs[4 * k:4 * k + 4]
            g = ((p_ref[0].astype(F32) + p_ref[1].astype(F32)) + p_ref[2].astype(F32)) + own_ref[...]
            g_ref[...] = g
            d_ref[...], nm_ref[...], nv_ref[...] = _adamw(w_ref[...], g, m_ref[...], v_ref[...])

    in_specs, out_specs, out_shape, operands = [], [], [], []
    for parts, own, w, m, v in items:
        r, cdim = w.shape
        tr = r // ADAM_STEPS
        spec = pl.BlockSpec((tr, cdim), lambda i: (i, 0))
        in_specs += [pl.BlockSpec((3, tr, cdim), lambda i: (0, i, 0)), spec, spec, spec, spec]
        out_specs += [spec] * 4
        out_shape += [jax.ShapeDtypeStruct((r, cdim), F32)] * 4
        operands += [parts, own, w, m, v]
    outs = pl.pallas_call(
        body, name=name, grid=(ADAM_STEPS,),
        in_specs=in_specs + [ANY] * ns, out_specs=out_specs + [ANY] * ns, scratch_shapes=jobs.sems,
        out_shape=out_shape + jobs.out_shapes,
        compiler_params=_cp(("arbitrary",)),
    )(*operands, *jobs.inputs)
    return [outs[4 * k:4 * k + 4] for k in range(nw)], jobs.split(outs[4 * nw:])


def _adamw_small(items):
    n = len(items)

    def body(*refs):
        ins, outs = refs[:4 * n], refs[4 * n:]
        for k in range(n):
            w_ref, g_ref, m_ref, v_ref = ins[4 * k:4 * k + 4]
            d_ref, nm_ref, nv_ref = outs[3 * k:3 * k + 3]
            d_ref[...], nm_ref[...], nv_ref[...] = _adamw(w_ref[...], g_ref[...], m_ref[...], v_ref[...])

    vm = pl.BlockSpec(memory_space=pltpu.VMEM)
    shapes = [jax.ShapeDtypeStruct(w.shape, F32) for w, _, _, _ in items for _ in range(3)]
    outs = pl.pallas_call(body, name="adamw_small", in_specs=[vm] * (4 * n), out_specs=[vm] * (3 * n),
                          out_shape=shapes)(*[t for item in items for t in item])
    return [outs[3 * k:3 * k + 3] for k in range(n)]


def _pair_sums(where, mine, theirs, name):
    _, r, cdim = theirs.shape
    tr = 128 if r % 128 == 0 else r

    def body(where_ref, a_ref, b_ref, own_ref, wire_ref):
        chip = where_ref[1]
        own_ref[...] = a_ref[chip] + b_ref[chip]
        wire_ref[...] = (a_ref[...] + b_ref[...]).astype(WIRE_DTYPE)

    spec = pl.BlockSpec((4, tr, cdim), lambda i, s: (0, i, 0))
    mine_spec = spec if mine.ndim == 3 else pl.BlockSpec((None, 4, tr, cdim), lambda i, s: (s[0], 0, i, 0))
    return pl.pallas_call(
        body, name=name,
        grid_spec=pltpu.PrefetchScalarGridSpec(
            num_scalar_prefetch=1, grid=(r // tr,), in_specs=[mine_spec, spec],
            out_specs=[pl.BlockSpec((tr, cdim), lambda i, s: (i, 0)), spec]),
        out_shape=[jax.ShapeDtypeStruct((r, cdim), F32), jax.ShapeDtypeStruct(theirs.shape, WIRE_DTYPE)],
        compiler_params=_cp(("arbitrary",)))(where, mine, theirs)


def kernel(x, meta_tokens, norm_mix_w, w_in, w_gate_up, b_gate, gla_norm_w, sinks, w_out, norm_ff_w, w_ff1, w_ff2, final_norm_w, loss_target, m_meta_tokens, m_norm_mix_w, m_w_in, m_w_gate_up, m_b_gate, m_gla_norm_w, m_sinks, m_w_out, m_norm_ff_w, m_w_ff1, m_w_ff2, m_final_norm_w, v_meta_tokens, v_norm_mix_w, v_w_in, v_w_gate_up, v_b_gate, v_gla_norm_w, v_sinks, v_w_out, v_norm_ff_w, v_w_ff1, v_w_ff2, v_final_norm_w):
    seq = x.shape[1]
    rows = LEAD + seq
    tm = _row_tile(rows)
    tm_wide = WIDE_ROW_TILE if rows % WIDE_ROW_TILE == 0 else tm
    dev =4 * lax.axis_index("x") + 2 * lax.axis_index("y") + lax.axis_index("c")

    small_shard = jnp.concatenate([meta_tokens, w_gate_up[0], jnp.zeros((N_META, 96), F32)], axis=1)
    g_in, g_small = _all_gather([w_in[0].astype(WIRE_DTYPE), small_shard])
    later_shards = [w_out[0].astype(WIRE_DTYPE), w_ff1[0].astype(WIRE_DTYPE), w_ff2[0].astype(WIRE_DTYPE)]
    win_p = _win_padded(g_in)
    meta_full = jnp.transpose(g_small[:, :, 0:128], (1, 0, 2)).reshape(N_META, D)
    wg_full = jnp.transpose(g_small[:, :, 128:160], (1, 0, 2)).reshape(GLA_RANK, GLA_HEADS * GLA_DK)
    wg_p = jnp.concatenate([wg_full, jnp.zeros((128 - GLA_RANK, 256), F32)], axis=0)

    lead = jnp.concatenate([jnp.zeros((META0, D), F32), meta_full], axis=0)
    angles = _rope_angles(rows)
    proj, qr, kr, vr, (g_w1,) = _in_proj(x[0], lead, norm_mix_w, win_p, angles, tm, later_shards[1:2])
    oraw, og, states, decay, dgate, (g_out,) = _gla_fwd(proj, wg_p, b_gate, gla_norm_w, later_shards[0:1])
    osw, (g_w2,) = _swa_fwd(qr, kr, vr, sinks, later_shards[2:3])
    wout_full = g_out.reshape(D, D)
    w2_full = g_w2.reshape(D_FF, D)
    w1_full = jnp.transpose(g_w1, (1, 0, 2)).reshape(D, D_FF)
    h1, f, ft = _out_proj(x[0], lead, og, osw, wout_full, norm_ff_w, tm)
    a, dh2, dh2t, loss_p, gfn_p = _ffn_fwd(f, h1, w1_full, w2_full, loss_target[0], final_norm_w.reshape(1, D), tm)

    da, dh1, gnf_p = _ffn_bwd_act(dh2, a, w1_full, w2_full, h1, norm_ff_w, tm)
    dw1, dw2 = _ffn_bwd_weights(ft, a, da, dh2t, tm_wide)
    where = jnp.stack([lax.axis_index("c"), 2 * lax.axis_index("x") + lax.axis_index("y")]).astype(jnp.int32)
    dog, dos, dwout, theirs_ffn = _out_proj_bwd(dh1, og, osw, wout_full, tm, [dw1, dw2])
    pairs_ffn = [_pair_sums(where, p, q, "reduce_pair_%d" % (2 + k))
                 for k, (p, q) in enumerate(zip([dw1, dw2], theirs_ffn))]
    sums_ffn, wires_ffn = [p[0] for p in pairs_ffn], [p[1] for p in pairs_ffn]
    dsq, dsk, dsv, dsink_p, (parts_ffn, (theirs_wout,)) = _swa_bwd(
        qr, kr, vr, osw, dos, sinks, _Jobs([("chips", wires_ffn), ("sibling", [dwout])]))
    sum_wout, wire_wout = _pair_sums(where, dwout, theirs_wout, "reduce_pair_1")
    (dgq, dgk, dgv, dgr, dlr, dwg_p, dbg_p, dgnw_p), ((parts_wout,),) = _gla_bwd(
        proj, decay, dgate, oraw, states, dog, wg_p, gla_norm_w, _Jobs([("chips", [wire_wout])]))
    grad_x, dlead, dproj, ut, gnm_p = _in_proj_bwd(x[0], lead, dh1, norm_mix_w, win_p, dgv, dgr, dsq, dgq, dgk, dsk,
                                                   dsv, dlr, angles, tm)
    grad_x = grad_x[None]
    dwin_mine, dwin_theirs, total = _in_proj_bwd_weights(
        ut, dproj, tm_wide, [dlead, dwg_p, gnm_p, gnf_p, gfn_p, dbg_p, dgnw_p, loss_p, dsink_p])
    sum_win, sum_win_wire = _pair_sums(where, dwin_mine, dwin_theirs, "reduce_pair_0")

    g_meta = lax.dynamic_slice(total, (R_META, dev * 128), (N_META, 128))
    g_wg = lax.dynamic_slice(total, (R_WG, dev * 32), (GLA_RANK, 32))
    g_norm_mix, g_norm_ff = total[R_NORM_MIX:R_NORM_MIX + 1], total[R_NORM_FF:R_NORM_FF + 1]
    g_final_norm = total[R_FINAL:R_FINAL + 1]
    g_b_gate, g_gla_norm = total[R_B_GATE:R_B_GATE + 1, 0:256], total[R_GLA_NORM:R_GLA_NORM + 1, 0:128]
    g_sinks = total[R_SINKS:R_SINKS + SWA_HEADS, 0].reshape(1, SWA_HEADS)
    loss = total[R_LOSS, 0]

    ((g_wout, d_wout, nm_wout, nv_wout), (g_w1s, d_w1, nm_w1, nv_w1), (g_w2s, d_w2, nm_w2, nv_w2)), ((parts_win,),) = \
        _adamw_shards([(parts_wout, sum_wout, w_out[0], m_w_out[0], v_w_out[0]),
                       (parts_ffn[0], sums_ffn[0], w_ff1[0], m_w_ff1[0], v_w_ff1[0]),
                       (parts_ffn[1], sums_ffn[1], w_ff2[0], m_w_ff2[0], v_w_ff2[0])],
                      "adamw_w_out_ff", _Jobs([("chips", [sum_win_wire])]))
    ((g_win, d_win, nm_win, nv_win),), _ = _adamw_shards(
        [(parts_win, sum_win, w_in[0], m_w_in[0], v_w_in[0])], "adamw_w_in")

    names = ["meta", "wg", "norm_mix", "b_gate", "gla_norm", "sinks", "norm_ff", "final_norm"]
    ws = [meta_tokens, w_gate_up, norm_mix_w, b_gate, gla_norm_w, sinks, norm_ff_w, final_norm_w]
    gs = [g_meta, g_wg, g_norm_mix, g_b_gate, g_gla_norm, g_sinks, g_norm_ff, g_final_norm]
    ms = [m_meta_tokens, m_w_gate_up, m_norm_mix_w, m_b_gate, m_gla_norm_w, m_sinks, m_norm_ff_w, m_final_norm_w]
    vs = [v_meta_tokens, v_w_gate_up, v_norm_mix_w, v_b_gate, v_gla_norm_w, v_sinks, v_norm_ff_w, v_final_norm_w]
    flat = lambda t: t.reshape(-1, t.shape[-1])
    small_out = _adamw_small([(flat(w), flat(g), flat(m), flat(v)) for w, g, m, v in zip(ws, gs, ms, vs)])
    d_small = {n: small_out[k][0].reshape(ws[k].shape) for k, n in enumerate(names)}
    nm_small = {n: small_out[k][1].reshape(ws[k].shape) for k, n in enumerate(names)}
    nv_small = {n: small_out[k][2].reshape(ws[k].shape) for k, n in enumerate(names)}
    g_small_d = {n: g.reshape(ws[k].shape) for k, (n, g) in enumerate(zip(names, gs))}

    def ordered(big, small_d):
        win_v, wout_v, w1_v, w2_v = big
        return (small_d["meta"], small_d["norm_mix"], win_v[None], small_d["wg"], small_d["b_gate"],
                small_d["gla_norm"], small_d["sinks"], wout_v[None], small_d["norm_ff"], w1_v[None], w2_v[None],
                small_d["final_norm"])

    return (loss, grad_x,
            *ordered((g_win, g_wout, g_w1s, g_w2s), g_small_d),
            *ordered((d_win, d_wout, d_w1, d_w2), d_small),
            *ordered((nm_win, nm_wout, nm_w1, nm_w2), nm_small),
            *ordered((nv_win, nv_wout, nv_w1, nv_w2), nv_small))
```

```python
import functools

import jax
import jax.numpy as jnp
from jax import lax
from jax.experimental import pallas as pl
from jax.experimental.pallas import tpu as pltpu

F32 = jnp.float32
MXU_DTYPE = jnp.bfloat16
ACT_DTYPE = jnp.bfloat16
WIRE_DTYPE = jnp.bfloat16

D = 1024
N_META = 16
LEAD = 128
META0 = LEAD - N_META
EPS = 1e-5
GLA_HEADS, GLA_DK, GLA_DV, GLA_RANK, GLA_CHUNK = 4, 64, 128, 16, 64
GLA_TAU = 16.0
SWA_HEADS, SWA_KV, SWA_GROUP, SWA_HD, SWA_BLOCK = 8, 2, 4, 64, 128
ROPE_DIM, ROPE_THETA = 16, 500000.0
D_FF = 4096
N_DEV = 8
FF_TILE = D_FF // N_DEV
FF_WIDE = 2048
NEG = -1e30

C_GV, C_GR, C_GQ, C_GK, C_LR, C_SQ, C_SK, C_SV = 0, 512, 1024, 1280, 1536, 1664, 2176, 2304
DGLA = 1664
DINP = 2432
DIN = 2320
O_GQ, O_GK, O_GV, O_GR, O_LR, O_SQ, O_SK, O_SV = (0, 256), (256, 512), (512, 1024), (1024, 1536), (1536, 1552), (1552, 2064), (2064, 2192), (2192, 2320)

ADAM_LR, ADAM_B1, ADAM_B2, ADAM_EPS, ADAM_WD, ADAM_STEP = 0.001, 0.9, 0.999, 1e-08, 0.01, 10

MESH = pl.DeviceIdType.MESH
ANY = pl.BlockSpec(memory_space=pl.ANY)
VMEM_TILE_MB, VMEM_WIDE_MB = 48, 56


def _cp(sem=None, vmem_mb=None):
    kw = {}
    if sem is not None:
        kw["dimension_semantics"] = sem
    if vmem_mb is not None:
        kw["vmem_limit_bytes"] = vmem_mb << 20
    return pltpu.CompilerParams(**kw)


def _mm(a, b):
    return jnp.dot(a.astype(MXU_DTYPE), b.astype(MXU_DTYPE), preferred_element_type=F32)


def _mm_nt(a, b):
    return lax.dot_general(a.astype(MXU_DTYPE), b.astype(MXU_DTYPE), (((1,), (1,)), ((), ())),
                           preferred_element_type=F32)


def _mm_tn(a, b):
    return lax.dot_general(a.astype(MXU_DTYPE), b.astype(MXU_DTYPE), (((0,), (0,)), ((), ())),
                           preferred_element_type=F32)


def _masked_sums(mask, t):
    m = mask.astype(jnp.bfloat16)
    hi = t.astype(jnp.bfloat16)
    rest = t - hi.astype(F32)
    mid = rest.astype(jnp.bfloat16)
    low = (rest - mid.astype(F32)).astype(jnp.bfloat16)
    dot = lambda part: jnp.dot(m, part, preferred_element_type=F32)
    return dot(hi) + (dot(mid) + dot(low))


def _logsigmoid(z):
    return jnp.minimum(z, 0.0) - jnp.log(1.0 + jnp.exp(-jnp.abs(z)))


def _sigmoid(z):
    return 1.0 / (1.0 + jnp.exp(-z))


ROW_TILE, WIDE_ROW_TILE = 640, 1664


def _row_tile(rows, want=ROW_TILE):
    return want if rows % want == 0 else LEAD


def _mesh_pos():
    return lax.axis_index("x"), lax.axis_index("y"), lax.axis_index("c")


def _all_gather(shards):
    n = len(shards)

    def body(*refs):
        start, forward, finish = _gather_schedule(refs[:n], refs[n:2 * n], *refs[2 * n:])
        start()
        for j in range(3):
            forward(j)
        finish()

    gathered = pl.pallas_call(
        body, name="all_gather_weights",
        out_shape=_gathered_shapes(shards), in_specs=[ANY] * n, out_specs=[ANY] * n,
        scratch_shapes=_gather_sems(n),
    )(*shards)
    return _with_own_block(gathered, shards)


def _gathered_shapes(shards):
    return [jax.ShapeDtypeStruct((N_DEV,) + s.shape, s.dtype) for s in shards]


def _gather_sems(n):
    return [pltpu.SemaphoreType.DMA((7 * n,)), pltpu.SemaphoreType.DMA((7 * n,))] if n else []


def _place_gather(step, steps, shard_refs, gathered_refs, sems):
    if not shard_refs:
        return
    start, forward, finish = _gather_schedule(shard_refs, gathered_refs, *sems)
    pl.when(step == 0)(start)
    for j, at in enumerate((steps * 7 // 10, steps * 8 // 10, steps * 9 // 10)):
        pl.when(step == at)(functools.partial(forward, j))
    pl.when(step == steps - 1)(finish)


def _with_own_block(gathered, shards):
    dev = 4 * lax.axis_index("x") + 2 * lax.axis_index("y") + lax.axis_index("c")
    return [lax.dynamic_update_index_in_dim(g, s, dev, 0) for g, s in zip(gathered, shards)]


def _gather_schedule(ins, outs, send_sems, recv_sems):
    n = len(ins)
    x, y, c = _mesh_pos()
    me, sibling = (x, y, c), (x, y, 1 - c)
    chips = [(1 - x, y), (x, 1 - y), (1 - x, 1 - y)]

    def copy(a, k, block, to, src=None):
        dst = outs[a].at[4 * block[0] + 2 * block[1] + block[2]]
        return pltpu.make_async_remote_copy(
            src_ref=dst if src is None else src, dst_ref=dst,
            send_sem=send_sems.at[a * 7 + k], recv_sem=recv_sems.at[a * 7 + k],
            device_id=to, device_id_type=MESH)

    def first(a):
        return [copy(a, 0, me, sibling, src=ins[a])] + [copy(a, 1 + j, me, (*chip, c), src=ins[a])
                                                        for j, chip in enumerate(chips)]

    def start():
        for a in range(n):
            for cp in first(a):
                cp.start()

    def forward(j):
        for a in range(n):
            copy(a, 1 + j, (*chips[j], c), me).wait_recv()
            copy(a, 4 + j, (*chips[j], c), sibling).start()

    def finish():
        for a in range(n):
            copy(a, 0, sibling, me).wait_recv()
            for j, chip in enumerate(chips):
                copy(a, 4 + j, (*chip, 1 - c), me).wait_recv()
        for a in range(n):
            for cp in first(a) + [copy(a, 4 + j, (*chip, c), sibling) for j, chip in enumerate(chips)]:
                cp.wait_send()

    return start, forward, finish


def _sibling_shapes(gs):
    return [jax.ShapeDtypeStruct(g.shape[1:], g.dtype) for g in gs]


def _sibling_sems(n):
    return [pltpu.SemaphoreType.DMA((n,)), pltpu.SemaphoreType.DMA((n,))]


def _sibling_schedule(ins, land, send_sems, recv_sems):
    x, y, c = _mesh_pos()

    def copies():
        return [pltpu.make_async_remote_copy(
            src_ref=ins[a].at[1 - c], dst_ref=land[a], send_sem=send_sems.at[a], recv_sem=recv_sems.at[a],
            device_id=(x, y, 1 - c), device_id_type=MESH) for a in range(len(ins))]

    def start():
        for cp in copies():
            cp.start()

    def finish():
        for cp in copies():
            cp.wait_recv()
        for cp in copies():
            cp.wait_send()

    return start, finish


def _chips_shapes(ps):
    return [jax.ShapeDtypeStruct((3,) + p.shape[1:], p.dtype) for p in ps]


def _chips_sems(n):
    return [pltpu.SemaphoreType.DMA((3 * n,)), pltpu.SemaphoreType.DMA((3 * n,))]


def _chips_schedule(ins, land, send_sems, recv_sems):
    x, y, c = _mesh_pos()
    chips = [(1 - x, y), (x, 1 - y), (1 - x, 1 - y)]

    def copies():
        return [pltpu.make_async_remote_copy(
            src_ref=ins[a].at[2 * chip[0] + chip[1]], dst_ref=land[a].at[j],
            send_sem=send_sems.at[3 * a + j], recv_sem=recv_sems.at[3 * a + j],
            device_id=(*chip, c), device_id_type=MESH) for a in range(len(ins)) for j, chip in enumerate(chips)]

    def start():
        for cp in copies():
            cp.start()

    def finish():
        for cp in copies():
            cp.wait_recv()
        for cp in copies():
            cp.wait_send()

    return start, finish


class _Jobs:
    def __init__(self, jobs):
        self.jobs = jobs
        self.inputs = [a for _, arrs in jobs for a in arrs]
        self.out_shapes = [s for kind, arrs in jobs
                           for s in (_sibling_shapes(arrs) if kind == "sibling" else _chips_shapes(arrs))]
        self.sems = [s for kind, arrs in jobs
                     for s in (_sibling_sems(len(arrs)) if kind == "sibling" else _chips_sems(len(arrs)))]
        self.n = len(self.inputs)

    def bind(self, in_refs, out_refs, sem_refs):
        starts, finishes, at = [], [], 0
        for k, (kind, arrs) in enumerate(self.jobs):
            schedule = _sibling_schedule if kind == "sibling" else _chips_schedule
            start, finish = schedule(in_refs[at:at + len(arrs)], out_refs[at:at + len(arrs)],
                                     sem_refs[2 * k], sem_refs[2 * k + 1])
            starts.append(start)
            finishes.append(finish)
            at += len(arrs)

        def start_all():
            for f in starts:
                f()

        def finish_all():
            for f in finishes:
                f()

        return start_all, finish_all

    def split(self, outs):
        res, at = [], 0
        for _, arrs in self.jobs:
            res.append(list(outs[at:at + len(arrs)]))
            at += len(arrs)
        return res


R_META, R_WG, R_NORM_MIX, R_NORM_FF, R_FINAL, R_B_GATE, R_GLA_NORM, R_LOSS, R_SINKS, SMALL_ROWS = 0, 16, 32, 33, 34, 35, 36, 37, 40, 48


SMALL_SPECS = [pl.BlockSpec((LEAD, D), lambda i: (0, 0)), pl.BlockSpec((128, 256), lambda i: (0, 0)),
               pl.BlockSpec((8, D), lambda i: (0, 0)), pl.BlockSpec((8, D), lambda i: (0, 0)),
               pl.BlockSpec((8, D), lambda i: (0, 0)), pl.BlockSpec((8, 256), lambda i: (0, 0)),
               pl.BlockSpec((8, 128), lambda i: (0, 0)), pl.BlockSpec((8, 128), lambda i: (0, 0)),
               pl.BlockSpec((8, 128), lambda i: (0, 0))]


def _small_sum_scratch():
    return [pltpu.VMEM((SMALL_ROWS, D), F32), pltpu.VMEM((N_DEV, SMALL_ROWS, D), F32),
            pltpu.SemaphoreType.DMA((7,)), pltpu.SemaphoreType.DMA((7,))]


def _small_sum_schedule(small_refs, out_ref, p_ref, land, send_sems, recv_sems):
    dlead_ref, dwg_ref, gnm_ref, gnf_ref, gfn_ref, dbg_ref, dgnw_ref, loss_ref, dsink_ref = small_refs
    x, y, c = _mesh_pos()
    me = 4 * x + 2 * y + c

    def copies():
        res = []
        for k in range(1, N_DEV):
            bx, by, bc = (k >> 2) & 1, (k >> 1) & 1, k & 1
            peer = (1 - x if bx else x, 1 - y if by else y, 1 - c if bc else c)
            res.append(pltpu.make_async_remote_copy(
                src_ref=p_ref, dst_ref=land.at[me], send_sem=send_sems.at[k - 1], recv_sem=recv_sems.at[k - 1],
                device_id=peer, device_id_type=MESH))
        return res

    def start():
        p_ref[...] = jnp.zeros_like(p_ref)
        p_ref[R_META:R_META + N_META, :] = dlead_ref[META0:LEAD, :]
        p_ref[R_WG:R_WG + GLA_RANK, 0:256] = dwg_ref[0:GLA_RANK, :]
        p_ref[R_NORM_MIX:R_NORM_MIX + 1, :] = gnm_ref[0:1, :]
        p_ref[R_NORM_FF:R_NORM_FF + 1, :] = gnf_ref[0:1, :]
        p_ref[R_FINAL:R_FINAL + 1, :] = gfn_ref[0:1, :]
        p_ref[R_B_GATE:R_B_GATE + 1, 0:256] = dbg_ref[0:1, :]
        p_ref[R_GLA_NORM:R_GLA_NORM + 1, 0:128] = dgnw_ref[0:1, :]
        p_ref[R_LOSS:R_LOSS + 1, 0:128] = loss_ref[0:1, :]
        p_ref[R_SINKS:R_SINKS + SWA_HEADS, 0:128] = dsink_ref[...]
        land[me] = p_ref[...]
        for cp in copies():
            cp.start()

    def finish():
        for cp in copies():
            cp.wait_recv()
        for cp in copies():
            cp.wait_send()
        acc = land[0]
        for d in range(1, N_DEV):
            acc = acc + land[d]
        out_ref[...] = acc

    return start, finish


def _token_specs(tm, grid_rank=1):
    nb = tm // LEAD

    def spec(k):
        if grid_rank == 1:
            return pl.BlockSpec((LEAD, D), lambda i: (jnp.maximum(i * nb + k - 1, 0), 0))
        return pl.BlockSpec((LEAD, D), lambda i, j: (jnp.maximum(i * nb + k - 1, 0), 0))

    return [spec(k) for k in range(nb)]


def _h_tile(i, lead_ref, x_refs):
    first = jnp.where(i == 0, lead_ref[...], x_refs[0][...])
    return jnp.concatenate([first] + [r[...] for r in x_refs[1:]], axis=0)


def _in_proj(x, lead, nw, win_p, angles, tm, shards):
    rows = LEAD + x.shape[0]
    nb = tm // LEAD
    steps = rows // tm
    ns = len(shards)

    def body(*refs):
        x_refs, refs = refs[:nb], refs[nb:]
        lead_ref, nw_ref, w_ref, cs_ref = refs[:4]
        shard_refs, (o_ref, q_ref, k_ref, v_ref) = refs[4:4 + ns], refs[4 + ns:8 + ns]
        _place_gather(pl.program_id(0), steps, shard_refs, refs[8 + ns:8 + 2 * ns], refs[8 + 2 * ns:])
        h = _h_tile(pl.program_id(0), lead_ref, x_refs)
        rstd = lax.rsqrt(jnp.mean(h * h, axis=-1, keepdims=True) + EPS)
        u = (h * rstd * nw_ref[...]).astype(MXU_DTYPE)
        proj = jnp.dot(u, w_ref[...].astype(MXU_DTYPE), preferred_element_type=F32)
        o_ref[...] = proj[:, 0:DGLA]
        cos, sa, sb = _rope_tables(cs_ref[...])
        q_ref[...] = (_rope(proj[:, C_SQ:C_SK], cos, sa, sb) * (SWA_HD ** -0.5)).astype(ACT_DTYPE)
        k_ref[...] = _rope(proj[:, C_SK:C_SV], cos, sa, sb).astype(ACT_DTYPE)
        v_ref[...] = proj[:, C_SV:DINP].astype(ACT_DTYPE)

    row = lambda w: pl.BlockSpec((tm, w), lambda i: (i, 0))
    outs = pl.pallas_call(
        body, name="in_proj", grid=(steps,),
        in_specs=_token_specs(tm) + [pl.BlockSpec((LEAD, D), lambda i: (0, 0)), pl.BlockSpec((1, D), lambda i: (0, 0)),
                                     pl.BlockSpec((D, DINP), lambda i: (0, 0)), row(ROPE_DIM)]
        + [ANY] * ns,
        out_specs=[row(DGLA), row(512), row(128), row(128)] + [ANY] * ns,
        out_shape=[jax.ShapeDtypeStruct((rows, DGLA), F32), jax.ShapeDtypeStruct((rows, 512), ACT_DTYPE),
                   jax.ShapeDtypeStruct((rows, 128), ACT_DTYPE), jax.ShapeDtypeStruct((rows, 128), ACT_DTYPE)]
        + _gathered_shapes(shards),
        scratch_shapes=_gather_sems(ns),
        compiler_params=_cp(("arbitrary",), VMEM_WIDE_MB),
    )(*([x] * nb), lead, nw, win_p, angles, *shards)
    return outs[0], outs[1], outs[2], outs[3], _with_own_block(outs[4:], shards)


def _rope_angles(rows):
    pos = (jnp.arange(rows, dtype=jnp.int32) - META0).astype(F32)
    inv_freq = 1.0 / (ROPE_THETA ** (jnp.arange(0, ROPE_DIM, 2, dtype=F32) / ROPE_DIM))
    ang = pos[:, None] * inv_freq[None, :]
    return jnp.concatenate([jnp.cos(ang), jnp.sin(ang)], axis=1)


def _rope_tables(cs):
    shape = (2 * (ROPE_DIM // 2), 3 * 128)
    j = lax.broadcasted_iota(jnp.int32, shape, 0)
    col = lax.broadcasted_iota(jnp.int32, shape, 1)
    table, in_head = col // 128, col % SWA_HD
    match = (j % (ROPE_DIM // 2)) == (in_head % (ROPE_DIM // 2))
    is_cos = j < ROPE_DIM // 2
    first, second = in_head < ROPE_DIM // 2, (in_head >= ROPE_DIM // 2) & (in_head < ROPE_DIM)
    spread = (jnp.where(match & is_cos & (table == 0) & (first | second), 1.0, 0.0)
              + jnp.where(match & ~is_cos & (table == 1) & first, -1.0, 0.0)
              + jnp.where(match & ~is_cos & (table == 2) & second, 1.0, 0.0)).astype(jnp.bfloat16)
    hi = cs.astype(jnp.bfloat16)
    rest = cs - hi.astype(F32)
    mid = rest.astype(jnp.bfloat16)
    low = (rest - mid.astype(F32)).astype(jnp.bfloat16)
    dot = lambda part: jnp.dot(part, spread, preferred_element_type=F32)
    tabs = dot(hi) + (dot(mid) + dot(low))
    lane = lax.broadcasted_iota(jnp.int32, (1, 128), 1) % SWA_HD
    return tabs[:, 0:128] + jnp.where(lane >= ROPE_DIM, 1.0, 0.0), tabs[:, 128:256], tabs[:, 256:384]


def _rope(xv, cos, sa, sb):
    width = xv.shape[1]
    reps = width // 128
    if reps > 1:
        cos, sa, sb = (jnp.tile(t, (1, reps)) for t in (cos, sa, sb))
    return xv * cos + pltpu.roll(xv, width - 8, 1) * sa + pltpu.roll(xv, 8, 1) * sb


def _unrope(dy, cos, sa, sb):
    width = dy.shape[1]
    reps = width // 128
    if reps > 1:
        cos, sa, sb = (jnp.tile(t, (1, reps)) for t in (cos, sa, sb))
    return dy * cos + pltpu.roll(dy * sa, 8, 1) + pltpu.roll(dy * sb, width - 8, 1)


def _gla_group(nc, most):
    for g in (10, 5, 2):
        if g <= most and nc % g == 0:
            return g
    return 1


def _chunk_masks(nrows):
    ii = lax.broadcasted_iota(jnp.int32, (nrows, nrows), 0)
    jj = lax.broadcasted_iota(jnp.int32, (nrows, nrows), 1)
    same = (ii // GLA_CHUNK) == (jj // GLA_CHUNK)
    return same & (jj <= ii), same & (jj >= ii)


def _gla_gates(lr, wg, bg, first_row):
    nrows = lr.shape[0]
    zg = _mm(lr, wg) + bg
    live = first_row + lax.broadcasted_iota(jnp.int32, (nrows, 1), 0) >= META0
    g = jnp.where(live, _logsigmoid(zg) * (1.0 / GLA_TAU), 0.0)
    return _masked_sums(_chunk_masks(nrows)[0], g), jnp.where(live, _sigmoid(-zg) * (1.0 / GLA_TAU), 0.0)


def _tril64():
    ii = lax.broadcasted_iota(jnp.int32, (GLA_CHUNK, GLA_CHUNK), 0)
    jj = lax.broadcasted_iota(jnp.int32, (GLA_CHUNK, GLA_CHUNK), 1)
    return jj <= ii


def _gla_fwd(proj, wg_p, bg, gnw, shards):
    rows = proj.shape[0]
    nc = rows // GLA_CHUNK
    group = _gla_group(nc, 10)
    steps, nrows = nc // group, group * GLA_CHUNK
    ns = len(shards)

    def body(q_ref, k_ref, v_ref, r_ref, lr_ref, lr_next_ref, wg_ref, bg_ref, gnw_ref, *rest):
        shard_refs, rest = rest[:ns], rest[ns:]
        oraw_ref, og_ref, st_ref, decay_ref, dgate_ref = rest[:5]
        gathered_refs, rest = rest[5:5 + ns], rest[5 + ns:]
        state, gates = rest[:2]
        c = pl.program_id(0)

        @pl.when(c == 0)
        def _():
            state[...] = jnp.zeros_like(state)
            gates[0, 0], gates[0, 1] = _gla_gates(lr_ref[...], wg_ref[...], bg_ref[...], 0)

        _place_gather(c, steps, shard_refs, gathered_refs, rest[2:])
        slot = c % 2
        b = gates[slot, 0]
        decay_ref[...] = b
        dgate_ref[...] = gates[slot, 1]
        gates[1 - slot, 0], gates[1 - slot, 1] = _gla_gates(lr_next_ref[...], wg_ref[...], bg_ref[...], (c + 1) * nrows)
        eb = jnp.exp(b)
        gq = q_ref[...] * (GLA_DK ** -0.5) * eb
        gk = k_ref[...] * jnp.exp(-b)
        v = v_ref[...]
        gnw_v = gnw_ref[...]
        tril = _tril64()
        pairs = [(h, gi) for h in range(GLA_HEADS) for gi in range(group)]
        rs = {gi: slice(gi * GLA_CHUNK, (gi + 1) * GLA_CHUNK) for gi in range(group)}
        s64 = {h: slice(h * GLA_DK, (h + 1) * GLA_DK) for h in range(GLA_HEADS)}
        s128 = {h: slice(h * GLA_DV, (h + 1) * GLA_DV) for h in range(GLA_HEADS)}
        qh = {(h, gi): gq[rs[gi], s64[h]] for h, gi in pairs}
        kh = {(h, gi): gk[rs[gi], s64[h]] for h, gi in pairs}
        vh = {(h, gi): v[rs[gi], s128[h]] for h, gi in pairs}
        ebl = {(h, gi): eb[(gi + 1) * GLA_CHUNK - 1:(gi + 1) * GLA_CHUNK, s64[h]] for h, gi in pairs}
        av = {pr: _mm(jnp.where(tril, _mm_nt(qh[pr], kh[pr]), 0.0), vh[pr]) for pr in pairs}
        inc = {pr: _mm_tn(vh[pr], kh[pr] * ebl[pr]) for pr in pairs}
        st = {}
        for h in range(GLA_HEADS):
            cur = state[h]
            for gi in range(group):
                st[h, gi] = cur
                st_ref[gi, h] = cur
                cur = cur * ebl[h, gi] + inc[h, gi]
            state[h] = cur
        for h, gi in pairs:
            o = av[h, gi] + _mm_nt(qh[h, gi], st[h, gi])
            oraw_ref[rs[gi], s128[h]] = o
            rstd = lax.rsqrt(jnp.mean(o * o, axis=-1, keepdims=True) + EPS)
            rh = r_ref[rs[gi], s128[h]]
            og_ref[rs[gi], s128[h]] = (o * rstd * gnw_v * (rh * _sigmoid(rh))).astype(ACT_DTYPE)

    nb = lambda w, col: pl.BlockSpec((nrows, w), lambda c: (c, col // w))
    const = lambda shape: pl.BlockSpec(shape, lambda c: (0,) * len(shape))
    outs = pl.pallas_call(
        body, name="gla_fwd", grid=(steps,),
        in_specs=[nb(256, C_GQ), nb(256, C_GK), nb(512, C_GV), nb(512, C_GR), nb(128, C_LR),
                  pl.BlockSpec((nrows, 128), lambda c: (jnp.minimum(c + 1, steps - 1), C_LR // 128)),
                  const((128, 256)), const((1, 256)), const((1, 128))] + [ANY] * ns,
        out_specs=[nb(512, 0), nb(512, 0),
                   pl.BlockSpec((group, GLA_HEADS, GLA_DV, GLA_DK), lambda c: (c, 0, 0, 0)),
                   nb(256, 0), nb(256, 0)] + [ANY] * ns,
        out_shape=[jax.ShapeDtypeStruct((rows, 512), F32), jax.ShapeDtypeStruct((rows, 512), ACT_DTYPE),
                   jax.ShapeDtypeStruct((nc, GLA_HEADS, GLA_DV, GLA_DK), F32),
                   jax.ShapeDtypeStruct((rows, 256), F32), jax.ShapeDtypeStruct((rows, 256), F32)]
        + _gathered_shapes(shards),
        scratch_shapes=[pltpu.VMEM((GLA_HEADS, GLA_DV, GLA_DK), F32), pltpu.VMEM((2, 2, nrows, 256), F32)]
        + _gather_sems(ns),
        compiler_params=_cp(("arbitrary",)),
    )(proj, proj, proj, proj, proj, proj, wg_p, bg, gnw, *shards)
    return outs[0], outs[1], outs[2], outs[3], outs[4], _with_own_block(outs[5:], shards)


def _swa_mask(n):
    shape = (SWA_BLOCK, 3 * SWA_BLOCK)
    qi = lax.broadcasted_iota(jnp.int32, shape, 0)
    jj = lax.broadcasted_iota(jnp.int32, shape, 1)
    meta = (jj < SWA_BLOCK) & (jj >= META0) & ((n > 0) | (jj <= qi))
    prev = (jj >= SWA_BLOCK) & (jj < 2 * SWA_BLOCK) & (n >= 2) & (jj - SWA_BLOCK > qi)
    cur = (jj >= 2 * SWA_BLOCK) & (n >= 1) & (jj - 2 * SWA_BLOCK <= qi)
    return meta | prev | cur


def _swa_group(nblk):
    return 5 if nblk % 5 == 0 else 1


def _swa_specs(group):
    blk = lambda w: pl.BlockSpec((group * SWA_BLOCK, w), lambda n: (n, 0))
    first = pl.BlockSpec((SWA_BLOCK, 128), lambda n: (0, 0))
    prev = pl.BlockSpec((SWA_BLOCK, 128), lambda n: (jnp.maximum(n * group - 1, 0), 0))
    return blk, first, prev


def _swa_keys(first_ref, prev_ref, cur_ref, g):
    own = cur_ref[g * SWA_BLOCK:(g + 1) * SWA_BLOCK, :]
    before = prev_ref[...] if g == 0 else cur_ref[(g - 1) * SWA_BLOCK:g * SWA_BLOCK, :]
    return jnp.concatenate([first_ref[...], before, own], axis=0)


def _swa_fwd(qr, kr, vr, sinks, shards):
    rows = qr.shape[0]
    nblk = rows // SWA_BLOCK
    group = _swa_group(nblk)
    steps = nblk // group
    ns = len(shards)

    def body(q_ref, k0, kp, kc, v0, vp, vc, sink_ref, *rest):
        o_ref = rest[ns]
        _place_gather(pl.program_id(0), steps, rest[:ns], rest[ns + 1:2 * ns + 1], rest[2 * ns + 1:])
        for g in range(group):
            n = pl.program_id(0) * group + g
            rs = slice(g * SWA_BLOCK, (g + 1) * SWA_BLOCK)
            kall, vall = _swa_keys(k0, kp, kc, g), _swa_keys(v0, vp, vc, g)
            mask = _swa_mask(n)
            heads = range(SWA_HEADS)
            hs = [slice(h * SWA_HD, (h + 1) * SWA_HD) for h in heads]
            kv = [slice((h // SWA_GROUP) * SWA_HD, (h // SWA_GROUP + 1) * SWA_HD) for h in heads]
            s = [jnp.where(mask, _mm_nt(q_ref[rs, hs[h]], kall[:, kv[h]]), NEG) for h in heads]
            m = [jnp.maximum(jnp.max(s[h], axis=-1, keepdims=True), sink_ref[0, h]) for h in heads]
            p = [jnp.exp(s[h] - m[h]) for h in heads]
            den = [jnp.sum(p[h], axis=-1, keepdims=True) + jnp.exp(sink_ref[0, h] - m[h]) for h in heads]
            o = [_mm(p[h], vall[:, kv[h]]) for h in heads]
            for h in heads:
                o_ref[rs, hs[h]] = (o[h] / den[h]).astype(ACT_DTYPE)

    blk, first, prev = _swa_specs(group)
    outs = pl.pallas_call(
        body, name="swa_fwd", grid=(steps,),
        in_specs=[blk(512), first, prev, blk(128), first, prev, blk(128),
                  pl.BlockSpec(memory_space=pltpu.SMEM)] + [ANY] * ns,
        out_specs=[blk(512)] + [ANY] * ns,
        out_shape=[jax.ShapeDtypeStruct((rows, 512), ACT_DTYPE)] + _gathered_shapes(shards),
        scratch_shapes=_gather_sems(ns),
        compiler_params=_cp(("arbitrary",)),
    )(qr, kr, kr, kr, vr, vr, vr, sinks, *shards)
    return outs[0], _with_own_block(outs[1:], shards)


def _out_proj(x, lead, og, osw, wout, nfw, tm):
    rows = LEAD + x.shape[0]
    nb = tm // LEAD

    def body(*refs):
        x_refs, (lead_ref, og_ref, os_ref, w_ref, nw_ref, h1_ref, f_ref, ft_ref) = refs[:nb], refs[nb:]
        h0 = _h_tile(pl.program_id(0), lead_ref, x_refs)
        h1 = h0 + _mm(og_ref[...], w_ref[0:512, :]) + _mm(os_ref[...], w_ref[512:1024, :])
        h1_ref[...] = h1
        rstd = lax.rsqrt(jnp.mean(h1 * h1, axis=-1, keepdims=True) + EPS)
        f = h1 * rstd * nw_ref[...]
        f_ref[...] = f.astype(ACT_DTYPE)
        ft_ref[...] = f.T.astype(ACT_DTYPE)

    row = lambda w: pl.BlockSpec((tm, w), lambda i: (i, 0))
    return pl.pallas_call(
        body, name="out_proj", grid=(rows // tm,),
        in_specs=_token_specs(tm) + [pl.BlockSpec((LEAD, D), lambda i: (0, 0)), row(512), row(512),
                                     pl.BlockSpec((D, D), lambda i: (0, 0)), pl.BlockSpec((1, D), lambda i: (0, 0))],
        out_specs=[row(D), row(D), pl.BlockSpec((D, tm), lambda i: (0, i))],
        out_shape=[jax.ShapeDtypeStruct((rows, D), F32), jax.ShapeDtypeStruct((rows, D), ACT_DTYPE),
                   jax.ShapeDtypeStruct((D, rows), ACT_DTYPE)],
        compiler_params=_cp(("arbitrary",), VMEM_TILE_MB),
    )(*([x] * nb), lead, og, osw, wout, nfw)


def _ffn_fwd(f, h1, w1, w2, tgt, fnw, tm):
    rows = f.shape[0]
    nj = D_FF // FF_WIDE
    nb = tm // LEAD

    def body(f_ref, h1_ref, w1_ref, w2_ref, nw_ref, *rest):
        t_refs, (a_ref, dh2_ref, dh2t_ref, loss_ref, gfn_ref, acc) = rest[:nb], rest[nb:]
        i, j = pl.program_id(0), pl.program_id(1)

        @pl.when((i == 0) & (j == 0))
        def _():
            loss_ref[...] = jnp.zeros_like(loss_ref)
            gfn_ref[...] = jnp.zeros_like(gfn_ref)

        @pl.when(j == 0)
        def _():
            acc[...] = jnp.zeros_like(acc)

        a = _mm(f_ref[...], w1_ref[...])
        a_ref[...] = a.astype(ACT_DTYPE)
        z = jnp.square(jnp.maximum(a, 0.0))
        acc[...] += _mm(z, w2_ref[...])

        @pl.when(j == nj - 1)
        def _():
            h2 = h1_ref[...] + acc[...]
            rstd = lax.rsqrt(jnp.mean(h2 * h2, axis=-1, keepdims=True) + EPS)
            hn = h2 * rstd
            nw = nw_ref[...]
            row = i * tm + lax.broadcasted_iota(jnp.int32, (tm, 1), 0)
            target = jnp.concatenate([t[...] for t in t_refs], axis=0)
            err = jnp.where(row >= LEAD, hn * nw - target, 0.0)
            row_loss = jnp.sum(err * err, axis=-1, keepdims=True) * (1.0 / D)
            loss_ref[...] += jnp.broadcast_to(0.5 * jnp.sum(row_loss, axis=0, keepdims=True), loss_ref.shape)
            dy = err * (1.0 / D)
            gfn_ref[...] += jnp.broadcast_to(jnp.sum(dy * hn, axis=0, keepdims=True), gfn_ref.shape)
            dhn = dy * nw
            dh2 = rstd * (dhn - hn * jnp.mean(dhn * hn, axis=-1, keepdims=True))
            dh2_ref[...] = dh2
            dh2t_ref[...] = dh2.T.astype(ACT_DTYPE)

    return pl.pallas_call(
        body, name="ffn_fwd", grid=(rows // tm, nj),
        in_specs=[pl.BlockSpec((tm, D), lambda i, j: (i, 0)), pl.BlockSpec((tm, D), lambda i, j: (i, 0)),
                  pl.BlockSpec((D, FF_WIDE), lambda i, j: (0, j)),
                  pl.BlockSpec((FF_WIDE, D), lambda i, j: (j, 0)),
                  pl.BlockSpec((1, D), lambda i, j: (0, 0))] + _token_specs(tm, grid_rank=2),
        out_specs=[pl.BlockSpec((tm, FF_WIDE), lambda i, j: (i, j)), pl.BlockSpec((tm, D), lambda i, j: (i, 0)),
                   pl.BlockSpec((D, tm), lambda i, j: (0, i)),
                   pl.BlockSpec((8, 128), lambda i, j: (0, 0)), pl.BlockSpec((8, D), lambda i, j: (0, 0))],
        out_shape=[jax.ShapeDtypeStruct((rows, D_FF), ACT_DTYPE), jax.ShapeDtypeStruct((rows, D), F32),
                   jax.ShapeDtypeStruct((D, rows), ACT_DTYPE),
                   jax.ShapeDtypeStruct((8, 128), F32), jax.ShapeDtypeStruct((8, D), F32)],
        scratch_shapes=[pltpu.VMEM((tm, D), F32)],
        compiler_params=_cp(("arbitrary", "arbitrary"), VMEM_WIDE_MB),
    )(f, h1, w1, w2, fnw, *([tgt] * nb))


def _ffn_bwd_act(dh2, a, w1, w2, h1, nfw, tm):
    rows = dh2.shape[0]
    nj = D_FF // FF_WIDE

    def body(dh2_ref, a_ref, w1_ref, w2_ref, h1_ref, nw_ref, da_ref, dh1_ref, gnf_ref, acc):
        i, j = pl.program_id(0), pl.program_id(1)

        @pl.when((i == 0) & (j == 0))
        def _():
            gnf_ref[...] = jnp.zeros_like(gnf_ref)

        @pl.when(j == 0)
        def _():
            acc[...] = jnp.zeros_like(acc)

        dz = _mm_nt(dh2_ref[...], w2_ref[...])
        da = dz * (2.0 * jnp.maximum(a_ref[...].astype(F32), 0.0))
        da_ref[...] = da.astype(ACT_DTYPE)
        acc[...] += _mm_nt(da, w1_ref[...])

        @pl.when(j == nj - 1)
        def _():
            h1 = h1_ref[...]
            rstd = lax.rsqrt(jnp.mean(h1 * h1, axis=-1, keepdims=True) + EPS)
            hn = h1 * rstd
            df = acc[...]
            gnf_ref[...] += jnp.broadcast_to(jnp.sum(df * hn, axis=0, keepdims=True), gnf_ref.shape)
            dfn = df * nw_ref[...]
            dh1_ref[...] = dh2_ref[...] + rstd * (dfn - hn * jnp.mean(dfn * hn, axis=-1, keepdims=True))

    return pl.pallas_call(
        body, name="ffn_bwd_act", grid=(rows // tm, nj),
        in_specs=[pl.BlockSpec((tm, D), lambda i, j: (i, 0)), pl.BlockSpec((tm, FF_WIDE), lambda i, j: (i, j)),
                  pl.BlockSpec((D, FF_WIDE), lambda i, j: (0, j)),
                  pl.BlockSpec((FF_WIDE, D), lambda i, j: (j, 0)),
                  pl.BlockSpec((tm, D), lambda i, j: (i, 0)), pl.BlockSpec((1, D), lambda i, j: (0, 0))],
        out_specs=[pl.BlockSpec((tm, FF_WIDE), lambda i, j: (i, j)), pl.BlockSpec((tm, D), lambda i, j: (i, 0)),
                   pl.BlockSpec((8, D), lambda i, j: (0, 0))],
        out_shape=[jax.ShapeDtypeStruct((rows, D_FF), ACT_DTYPE), jax.ShapeDtypeStruct((rows, D), F32),
                   jax.ShapeDtypeStruct((8, D), F32)],
        scratch_shapes=[pltpu.VMEM((tm, D), F32)],
        compiler_params=_cp(("arbitrary", "arbitrary"), VMEM_WIDE_MB),
    )(dh2, a, w1, w2, h1, nfw)


def _ffn_bwd_weights(ft, a, da, dh2t, tm):
    rows = a.shape[0]
    steps = rows // tm
    pair = 2 * FF_TILE

    def body(ft_ref, a_ref, da_ref, dh2t_ref, dw1_ref, dw2_ref, dw2t):
        i = pl.program_id(1)

        @pl.when(i == 0)
        def _():
            dw1_ref[...] = jnp.zeros_like(dw1_ref)
            dw2t[...] = jnp.zeros_like(dw2t)

        z = jnp.square(jnp.maximum(a_ref[...].astype(F32), 0.0))
        dw1 = _mm(ft_ref[...], da_ref[...])
        for core in range(2):
            dw1_ref[core] += dw1[:, core * FF_TILE:(core + 1) * FF_TILE]
        dw2t[...] += _mm(dh2t_ref[...], z)

        @pl.when(i == steps - 1)
        def _():
            for core in range(2):
                dw2_ref[core] = dw2t[:, core * FF_TILE:(core + 1) * FF_TILE].T

    return pl.pallas_call(
        body, name="ffn_bwd_weights", grid=(N_DEV // 2, steps),
        in_specs=[pl.BlockSpec((D, tm), lambda j, i: (0, i)), pl.BlockSpec((tm, pair), lambda j, i: (i, j)),
                  pl.BlockSpec((tm, pair), lambda j, i: (i, j)), pl.BlockSpec((D, tm), lambda j, i: (0, i))],
        out_specs=[pl.BlockSpec((2, None, D, FF_TILE), lambda j, i: (0, j, 0, 0)),
                   pl.BlockSpec((2, None, FF_TILE, D), lambda j, i: (0, j, 0, 0))],
        out_shape=[jax.ShapeDtypeStruct((2, 4, D, FF_TILE), F32), jax.ShapeDtypeStruct((2, 4, FF_TILE, D), F32)],
        scratch_shapes=[pltpu.VMEM((D, pair), F32)],
        compiler_params=_cp(("arbitrary", "arbitrary"), VMEM_WIDE_MB),
    )(ft, a, da, dh2t)


def _out_proj_bwd(dh1, og, osw, wout, tm, partials):
    rows = dh1.shape[0]
    steps = rows // tm
    ns = len(partials)

    def body(dh1_ref, og_ref, os_ref, w_ref, *rest):
        part_refs, rest = rest[:ns], rest[ns:]
        dog_ref, dos_ref, dw_ref = rest[:3]
        land_refs, (send_sems, recv_sems) = rest[3:3 + ns], rest[3 + ns:]
        i = pl.program_id(0)
        start, finish = _sibling_schedule(part_refs, land_refs, send_sems, recv_sems)

        @pl.when(i == 0)
        def _():
            dw_ref[...] = jnp.zeros_like(dw_ref)
            start()

        pl.when(i == steps - 1)(finish)

        dh1 = dh1_ref[...].astype(MXU_DTYPE)
        dog_ref[...] = _mm_nt(dh1, w_ref[0:512, :])
        dos_ref[...] = _mm_nt(dh1, w_ref[512:1024, :])
        for half, ref in enumerate((og_ref, os_ref)):
            dw = _mm_tn(ref[...], dh1)
            for blk in range(4):
                shard = half * 4 + blk
                dw_ref[shard % 2, shard // 2] += dw[blk * 128:(blk + 1) * 128, :]

    row = lambda w: pl.BlockSpec((tm, w), lambda i: (i, 0))
    outs = pl.pallas_call(
        body, name="out_proj_bwd", grid=(steps,),
        in_specs=[row(D), row(512), row(512), pl.BlockSpec((D, D), lambda i: (0, 0))] + [ANY] * ns,
        out_specs=[row(512), row(512), pl.BlockSpec((2, 4, 128, D), lambda i: (0, 0, 0, 0))] + [ANY] * ns,
        out_shape=[jax.ShapeDtypeStruct((rows, 512), F32), jax.ShapeDtypeStruct((rows, 512), F32),
                   jax.ShapeDtypeStruct((2, 4, 128, D), F32)] + _sibling_shapes(partials),
        scratch_shapes=_sibling_sems(ns),
        compiler_params=_cp(("arbitrary",), VMEM_TILE_MB),
    )(dh1, og, osw, wout, *partials)
    return outs[0], outs[1], outs[2], outs[3:]


def _swa_bwd(qr, kr, vr, osw, dos, sinks, jobs):
    rows = qr.shape[0]
    nblk = rows // SWA_BLOCK
    group = _swa_group(nblk)
    steps = nblk // group
    ns = jobs.n

    def body(q_ref, k0, kp, kc, v0, vp, vc, o_ref, do_ref, sink_ref, *rest):
        dq_ref, dk_ref, dv_ref, dsink_ref = rest[ns:ns + 4]
        start, finish = jobs.bind(rest[:ns], rest[ns + 4:2 * ns + 4], rest[2 * ns + 4:])
        step = pl.program_id(0)

        @pl.when(step == 0)
        def _():
            dk_ref[...] = jnp.zeros_like(dk_ref)
            dv_ref[...] = jnp.zeros_like(dv_ref)
            dsink_ref[...] = jnp.zeros_like(dsink_ref)
            start()

        pl.when(step == steps - 1)(finish)
        for g in range(group):
            block(step * group + g, g, q_ref, k0, kp, kc, v0, vp, vc, o_ref, do_ref, sink_ref,
                  dq_ref, dk_ref, dv_ref, dsink_ref)

    def block(n, g, q_ref, k0, kp, kc, v0, vp, vc, o_ref, do_ref, sink_ref, dq_ref, dk_ref, dv_ref, dsink_ref):
        rs = slice(g * SWA_BLOCK, (g + 1) * SWA_BLOCK)
        kall, vall = _swa_keys(k0, kp, kc, g), _swa_keys(v0, vp, vc, g)
        mask = _swa_mask(n)
        heads = range(SWA_HEADS)
        hs = [slice(h * SWA_HD, (h + 1) * SWA_HD) for h in heads]
        kv = [slice((h // SWA_GROUP) * SWA_HD, (h // SWA_GROUP + 1) * SWA_HD) for h in heads]
        sink = [sink_ref[0, h] for h in heads]
        qh = [q_ref[rs, hs[h]] for h in heads]
        doh = [do_ref[rs, hs[h]] for h in heads]
        s = [jnp.where(mask, _mm_nt(qh[h], kall[:, kv[h]]), NEG) for h in heads]
        dp = [_mm_nt(doh[h], vall[:, kv[h]]) for h in heads]
        delta = [jnp.sum(doh[h] * o_ref[rs, hs[h]].astype(F32), axis=-1, keepdims=True) for h in heads]
        m = [jnp.maximum(jnp.max(s[h], axis=-1, keepdims=True), sink[h]) for h in heads]
        e = [jnp.exp(s[h] - m[h]) for h in heads]
        inv = [1.0 / (jnp.sum(e[h], axis=-1, keepdims=True) + jnp.exp(sink[h] - m[h])) for h in heads]
        p = [e[h] * inv[h] for h in heads]
        ds = [p[h] * (dp[h] - delta[h]) for h in heads]
        dq = [_mm(ds[h], kall[:, kv[h]]) for h in heads]
        dkh = [_mm_tn(ds[h], qh[h]) for h in heads]
        dvh = [_mm_tn(p[h], doh[h]) for h in heads]
        for h in heads:
            dsink = -jnp.sum(jnp.exp(sink[h] - m[h]) * inv[h] * delta[h], axis=0, keepdims=True)
            dsink_ref[h:h + 1, :] += jnp.broadcast_to(dsink, (1, 128))
        dq_ref[rs, :] = jnp.concatenate(dq, axis=1).astype(ACT_DTYPE)
        group_sum = lambda parts, kvh: sum(parts[kvh * SWA_GROUP + 1:(kvh + 1) * SWA_GROUP], parts[kvh * SWA_GROUP])
        dk_all = jnp.concatenate([group_sum(dkh, kvh) for kvh in range(SWA_KV)], axis=1)
        dv_all = jnp.concatenate([group_sum(dvh, kvh) for kvh in range(SWA_KV)], axis=1)
        prev0 = pl.multiple_of(jnp.maximum(n - 1, 0) * SWA_BLOCK, SWA_BLOCK)
        cur0 = pl.multiple_of(n * SWA_BLOCK, SWA_BLOCK)
        for ref, val in ((dk_ref, dk_all), (dv_ref, dv_all)):
            ref[0:SWA_BLOCK, :] += val[0:SWA_BLOCK]
            ref[pl.ds(prev0, SWA_BLOCK), :] += val[SWA_BLOCK:2 * SWA_BLOCK]
            ref[pl.ds(cur0, SWA_BLOCK), :] += val[2 * SWA_BLOCK:]

    blk, first, prev = _swa_specs(group)
    whole = pl.BlockSpec((rows, 128), lambda n: (0, 0))
    outs = pl.pallas_call(
        body, name="swa_bwd", grid=(steps,),
        in_specs=[blk(512), first, prev, blk(128), first, prev, blk(128), blk(512), blk(512),
                  pl.BlockSpec(memory_space=pltpu.SMEM)] + [ANY] * ns,
        out_specs=[blk(512), whole, whole, pl.BlockSpec((8, 128), lambda n: (0, 0))] + [ANY] * ns,
        out_shape=[jax.ShapeDtypeStruct((rows, 512), ACT_DTYPE), jax.ShapeDtypeStruct((rows, 128), F32),
                   jax.ShapeDtypeStruct((rows, 128), F32), jax.ShapeDtypeStruct((8, 128), F32)] + jobs.out_shapes,
        scratch_shapes=jobs.sems,
        compiler_params=_cp(("arbitrary",), VMEM_TILE_MB),
    )(qr, kr, kr, kr, vr, vr, vr, osw, dos, sinks, *jobs.inputs)
    return outs[0], outs[1], outs[2], outs[3], jobs.split(outs[4:])


def _gla_bwd(proj, decay, dgate, oraw, states, dog, wg_p, gnw, jobs):
    rows = proj.shape[0]
    nc = rows // GLA_CHUNK
    group = _gla_group(nc, 5)
    steps, nrows = nc // group, group * GLA_CHUNK
    ns = jobs.n

    def body(q_ref, k_ref, v_ref, r_ref, lr_ref, b_ref, dgate_ref, oraw_ref, st_ref, dog_ref, wg_ref, gnw_ref, *rest):
        dq_ref, dk_ref, dv_ref, dr_ref, dlr_ref, dwg_ref, dbg_ref, dgnw_ref = rest[ns:ns + 8]
        dstate, db_scr = rest[2 * ns + 8:2 * ns + 10]
        start, finish = jobs.bind(rest[:ns], rest[ns + 8:2 * ns + 8], rest[2 * ns + 10:])
        t = pl.program_id(0)

        @pl.when(t == 0)
        def _():
            dstate[...] = jnp.zeros_like(dstate)
            dwg_ref[...] = jnp.zeros_like(dwg_ref)
            dbg_ref[...] = jnp.zeros_like(dbg_ref)
            dgnw_ref[...] = jnp.zeros_like(dgnw_ref)
            start()

        pl.when(t == steps - 1)(finish)

        lr, wg = lr_ref[...], wg_ref[...]
        b = b_ref[...]
        eb, enb = jnp.exp(b), jnp.exp(-b)
        scale = GLA_DK ** -0.5
        gq = q_ref[...] * scale * eb
        gk = k_ref[...] * enb
        v = v_ref[...]
        gnw_v = gnw_ref[...]
        tril = _tril64()
        is_last = lax.broadcasted_iota(jnp.int32, (GLA_CHUNK, 1), 0) == GLA_CHUNK - 1
        dgnw = jnp.zeros((1, GLA_DV), F32)
        pairs = [(h, gi) for h in range(GLA_HEADS) for gi in range(group)]
        rs = {gi: slice(gi * GLA_CHUNK, (gi + 1) * GLA_CHUNK) for gi in range(group)}
        s64 = {h: slice(h * GLA_DK, (h + 1) * GLA_DK) for h in range(GLA_HEADS)}
        s128 = {h: slice(h * GLA_DV, (h + 1) * GLA_DV) for h in range(GLA_HEADS)}
        qh = {(h, gi): gq[rs[gi], s64[h]] for h, gi in pairs}
        kh = {(h, gi): gk[rs[gi], s64[h]] for h, gi in pairs}
        vh = {(h, gi): v[rs[gi], s128[h]] for h, gi in pairs}
        ebl = {(h, gi): eb[(gi + 1) * GLA_CHUNK - 1:(gi + 1) * GLA_CHUNK, s64[h]] for h, gi in pairs}
        kl = {pr: kh[pr] * ebl[pr] for pr in pairs}
        st = {(h, gi): st_ref[gi, h] for h, gi in pairs}
        do = {}
        for h, gi in pairs:
            o, rh, dout = oraw_ref[rs[gi], s128[h]], r_ref[rs[gi], s128[h]], dog_ref[rs[gi], s128[h]]
            rstd = lax.rsqrt(jnp.mean(o * o, axis=-1, keepdims=True) + EPS)
            on = o * rstd
            sg = _sigmoid(rh)
            dr_ref[rs[gi], s128[h]] = (dout * (on * gnw_v) * (sg * (1.0 + rh * (1.0 - sg)))).astype(ACT_DTYPE)
            dy = dout * (rh * sg)
            dgnw = dgnw + jnp.sum(dy * on, axis=0, keepdims=True)
            don = dy * gnw_v
            do[h, gi] = rstd * (don - on * jnp.mean(don * on, axis=-1, keepdims=True))
        a = {pr: jnp.where(tril, _mm_nt(qh[pr], kh[pr]), 0.0) for pr in pairs}
        da = {pr: jnp.where(tril, _mm_nt(do[pr], vh[pr]), 0.0) for pr in pairs}
        dinc = {pr: _mm_tn(do[pr], qh[pr]) for pr in pairs}
        dgq = {pr: _mm(da[pr], kh[pr]) + _mm(do[pr], st[pr]) for pr in pairs}
        dgk = {pr: _mm_tn(da[pr], qh[pr]) for pr in pairs}
        dv_a = {pr: _mm_tn(a[pr], do[pr]) for pr in pairs}
        dsp = {}
        for h in range(GLA_HEADS):
            cur = dstate[h]
            for gi in reversed(range(group)):
                dsp[h, gi] = cur
                cur = cur * ebl[h, gi] + dinc[h, gi]
            dstate[h] = cur
        for h, gi in pairs:
            pr = (h, gi)
            dkl = _mm(vh[pr], dsp[pr])
            dv_ref[rs[gi], s128[h]] = (dv_a[pr] + _mm_nt(kl[pr], dsp[pr])).astype(ACT_DTYPE)
            debl = jnp.sum(dsp[pr] * st[pr], axis=0, keepdims=True)
            dq_ref[rs[gi], s64[h]] = (dgq[pr] * (scale * eb[rs[gi], s64[h]])).astype(ACT_DTYPE)
            dk_ref[rs[gi], s64[h]] = ((dgk[pr] + dkl * ebl[pr]) * enb[rs[gi], s64[h]]).astype(ACT_DTYPE)
            last = debl * ebl[pr] + jnp.sum(dkl * kl[pr], axis=0, keepdims=True)
            db_scr[rs[gi], s64[h]] = (dgq[pr] * qh[pr] - dgk[pr] * kh[pr] - dkl * kl[pr]
                                      + jnp.where(is_last, last, 0.0))
        dzg = _masked_sums(_chunk_masks(nrows)[1], db_scr[...]) * dgate_ref[...]
        dlr_ref[...] = _mm_nt(dzg, wg).astype(ACT_DTYPE)
        dwg_ref[...] += _mm_tn(lr, dzg)
        dbg_ref[...] += jnp.broadcast_to(jnp.sum(dzg, axis=0, keepdims=True), dbg_ref.shape)
        dgnw_ref[...] += jnp.broadcast_to(dgnw, dgnw_ref.shape)

    nb = lambda w, col: pl.BlockSpec((nrows, w), lambda t: (steps - 1 - t, col // w))
    const = lambda shape: pl.BlockSpec(shape, lambda t: (0,) * len(shape))
    outs = pl.pallas_call(
        body, name="gla_bwd", grid=(steps,),
        in_specs=[nb(256, C_GQ), nb(256, C_GK), nb(512, C_GV), nb(512, C_GR), nb(128, C_LR), nb(256, 0), nb(256, 0),
                  nb(512, 0),
                  pl.BlockSpec((group, GLA_HEADS, GLA_DV, GLA_DK), lambda t: (steps - 1 - t, 0, 0, 0)), nb(512, 0),
                  const((128, 256)), const((1, 128))] + [ANY] * ns,
        out_specs=[nb(256, 0), nb(256, 0), nb(512, 0), nb(512, 0), nb(128, 0),
                   const((128, 256)), const((8, 256)), const((8, 128))] + [ANY] * ns,
        out_shape=[jax.ShapeDtypeStruct((rows, 256), ACT_DTYPE), jax.ShapeDtypeStruct((rows, 256), ACT_DTYPE),
                   jax.ShapeDtypeStruct((rows, 512), ACT_DTYPE), jax.ShapeDtypeStruct((rows, 512), ACT_DTYPE),
                   jax.ShapeDtypeStruct((rows, 128), ACT_DTYPE), jax.ShapeDtypeStruct((128, 256), F32),
                   jax.ShapeDtypeStruct((8, 256), F32), jax.ShapeDtypeStruct((8, 128), F32)] + jobs.out_shapes,
        scratch_shapes=[pltpu.VMEM((GLA_HEADS, GLA_DV, GLA_DK), F32), pltpu.VMEM((nrows, 256), F32)] + jobs.sems,
        compiler_params=_cp(("arbitrary",)),
    )(proj, proj, proj, proj, proj, decay, dgate, oraw, states, dog, wg_p, gnw, *jobs.inputs)
    return outs[:8], jobs.split(outs[8:])


def _in_proj_bwd(x, lead, dh1, nw, win_p, dgv, dgr, dsq, dgq, dgk, dsk, dsv, dlr, angles, tm):
    seq = x.shape[0]
    rows = LEAD + seq
    nb = tm // LEAD
    steps = rows // tm

    def first_copy(scr, gx_ref, sem):
        return pltpu.make_async_copy(scr.at[pl.ds(LEAD, tm - LEAD)], gx_ref.at[pl.ds(0, tm - LEAD)], sem)

    def tile_copy(scr, gx_ref, sem, step):
        start = pl.multiple_of(jnp.maximum(step * tm - LEAD, 0), LEAD)
        return pltpu.make_async_copy(scr, gx_ref.at[pl.ds(start, tm)], sem)

    def body(*refs):
        x_refs, refs = refs[:nb], refs[nb:]
        (lead_ref, dh1_ref, nw_ref, w_ref, dgv_ref, dgr_ref, dsq_ref, dgq_ref, dgk_ref, dsk_ref, dsv_ref, dlr_ref,
         cs_ref, gx_ref, dlead_ref, dproj_ref, ut_ref, gnm_ref, scr, sem) = refs
        i = pl.program_id(0)

        @pl.when(i == 0)
        def _():
            gnm_ref[...] = jnp.zeros_like(gnm_ref)

        cos, sa, sb = _rope_tables(cs_ref[...])
        dsq_v = (_unrope(dsq_ref[...].astype(F32), cos, sa, sb) * (SWA_HD ** -0.5)).astype(MXU_DTYPE)
        dsk_v = _unrope(dsk_ref[...], cos, sa, sb).astype(MXU_DTYPE)
        dproj = jnp.concatenate(
            [dgv_ref[...].astype(MXU_DTYPE), dgr_ref[...].astype(MXU_DTYPE), dgq_ref[...].astype(MXU_DTYPE),
             dgk_ref[...].astype(MXU_DTYPE), dlr_ref[...].astype(MXU_DTYPE), dsq_v, dsk_v,
             dsv_ref[...].astype(MXU_DTYPE)],
            axis=1)
        dproj_ref[...] = dproj
        h = _h_tile(i, lead_ref, x_refs)
        rstd = lax.rsqrt(jnp.mean(h * h, axis=-1, keepdims=True) + EPS)
        hn = h * rstd
        nw_v = nw_ref[...]
        ut_ref[...] = (hn * nw_v).T.astype(ACT_DTYPE)
        du = _mm_nt(dproj, w_ref[...])
        gnm_ref[...] += jnp.broadcast_to(jnp.sum(du * hn, axis=0, keepdims=True), gnm_ref.shape)
        dun = du * nw_v
        dh0 = dh1_ref[...] + rstd * (dun - hn * jnp.mean(dun * hn, axis=-1, keepdims=True))

        if tm > LEAD:
            pl.when(i == 1)(lambda: first_copy(scr, gx_ref, sem).wait())
        pl.when(i > 1)(lambda: tile_copy(scr, gx_ref, sem, i).wait())
        scr[...] = dh0

        @pl.when(i == 0)
        def _():
            dlead_ref[...] = dh0[0:LEAD]
            if tm > LEAD:
                first_copy(scr, gx_ref, sem).start()
                if steps == 1:
                    first_copy(scr, gx_ref, sem).wait()

        @pl.when(i > 0)
        def _():
            tile_copy(scr, gx_ref, sem, i).start()

        if steps > 1:
            pl.when(i == steps - 1)(lambda: tile_copy(scr, gx_ref, sem, i).wait())

    row = lambda w: pl.BlockSpec((tm, w), lambda i: (i, 0))
    const = lambda shape: pl.BlockSpec(shape, lambda i: (0,) * len(shape))
    return pl.pallas_call(
        body, name="in_proj_bwd", grid=(steps,),
        in_specs=_token_specs(tm) + [const((LEAD, D)), row(D), const((1, D)), const((D, DINP)),
                                     row(512), row(512), row(512), row(256), row(256), row(128), row(128), row(128),
                                     row(ROPE_DIM)],
        out_specs=[ANY, const((LEAD, D)), row(DINP), pl.BlockSpec((D, tm), lambda i: (0, i)), const((8, D))],
        out_shape=[jax.ShapeDtypeStruct((seq, D), F32), jax.ShapeDtypeStruct((LEAD, D), F32),
                   jax.ShapeDtypeStruct((rows, DINP), ACT_DTYPE), jax.ShapeDtypeStruct((D, rows), ACT_DTYPE),
                   jax.ShapeDtypeStruct((8, D), F32)],
        scratch_shapes=[pltpu.VMEM((tm, D), F32), pltpu.SemaphoreType.DMA],
        compiler_params=_cp(("arbitrary",), VMEM_WIDE_MB),
    )(*([x] * nb), lead, dh1, nw, win_p, dgv, dgr, dsq, dgq, dgk, dsk, dsv, dlr, angles)


def _win_runs():
    groups = [(O_GQ, C_GQ), (O_GK, C_GK), (O_GV, C_GV), (O_GR, C_GR), (O_LR, C_LR), (O_SQ, C_SQ), (O_SK, C_SK),
              (O_SV, C_SV)]
    per = DIN // N_DEV
    runs = []
    for (o0, o1), c0 in groups:
        o = o0
        while o < o1:
            d = o // per
            end = min(o1, (d + 1) * per)
            runs.append((d, o - d * per, c0 + o - o0, end - o))
            o = end
    return runs


def _win_padded(g_in):
    tr = 128

    def body(g_ref, o_ref):
        o_ref[...] = jnp.zeros_like(o_ref)
        for d, s, c, w in _win_runs():
            o_ref[:, c:c + w] = g_ref[d, :, s:s + w]

    return pl.pallas_call(
        body, name="w_in_layout", grid=(D // tr,),
        in_specs=[pl.BlockSpec((N_DEV, tr, DIN // N_DEV), lambda i: (0, i, 0))],
        out_specs=pl.BlockSpec((tr, DINP), lambda i: (i, 0)),
        out_shape=jax.ShapeDtypeStruct((D, DINP), g_in.dtype),
        compiler_params=_cp(("arbitrary",)),
    )(g_in)


def _in_proj_bwd_weights(ut, dproj, tm, small):
    rows = dproj.shape[0]
    steps = rows // tm
    per = DIN // N_DEV

    def body(ut_ref, dp_ref, *rest):
        small_refs, (mine_ref, theirs_ref, total_ref, acc, stage, local_sems, send_sems, recv_sems) = rest[:9], rest[9:17]
        i = pl.program_id(0)
        start, finish = _small_sum_schedule(small_refs, total_ref, *rest[17:])
        x, y, c = _mesh_pos()

        @pl.when(i == 0)
        def _():
            acc[...] = jnp.zeros_like(acc)
            start()

        acc[...] += _mm(ut_ref[...], dp_ref[...])
        pl.when(i == steps - 1)(finish)

        def keep(slot, chip):
            return pltpu.make_async_copy(stage.at[slot], mine_ref.at[chip], local_sems.at[slot])

        def send(slot, chip):
            return pltpu.make_async_remote_copy(
                src_ref=stage.at[slot], dst_ref=theirs_ref.at[chip], send_sem=send_sems.at[slot],
                recv_sem=recv_sems.at[chip], device_id=(x, y, 1 - c), device_id_type=MESH)

        def drained(d):
            pl.when(c == d % 2)(keep(d % 2, d // 2).wait)
            pl.when(c != d % 2)(send(d % 2, d // 2).wait_send)

        @pl.when(i == steps - 1)
        def _():
            for d in range(N_DEV):
                slot, chip = d % 2, d // 2
                if d >= 2:
                    drained(d - 2)
                for owner, s, col, w in _win_runs():
                    if owner == d:
                        stage[slot, :, s:s + w] = acc[:, col:col + w]
                pl.when(c == slot)(keep(slot, chip).start)
                pl.when(c != slot)(send(slot, chip).start)
            drained(N_DEV - 2)
            drained(N_DEV - 1)
            for chip in range(4):
                send(0, chip).wait_recv()

    half = jax.ShapeDtypeStruct((4, D, per), F32)
    return pl.pallas_call(
        body, name="in_proj_bwd_weights", grid=(steps,),
        in_specs=[pl.BlockSpec((D, tm), lambda i: (0, i)), pl.BlockSpec((tm, DINP), lambda i: (i, 0))] + SMALL_SPECS,
        out_specs=[ANY, ANY, pl.BlockSpec((SMALL_ROWS, D), lambda i: (0, 0))],
        out_shape=[half, half, jax.ShapeDtypeStruct((SMALL_ROWS, D), F32)],
        scratch_shapes=[pltpu.VMEM((D, DINP), F32), pltpu.VMEM((2, D, per), F32), pltpu.SemaphoreType.DMA((2,)),
                        pltpu.SemaphoreType.DMA((2,)), pltpu.SemaphoreType.DMA((4,))] + _small_sum_scratch(),
        compiler_params=_cp(("arbitrary",), VMEM_WIDE_MB),
    )(ut, dproj, *small)


def _adamw(w, g, m, v):
    m = ADAM_B1 * m + (1.0 - ADAM_B1) * g
    v = ADAM_B2 * v + (1.0 - ADAM_B2) * jnp.square(g)
    m_hat = m / (1.0 - ADAM_B1 ** ADAM_STEP)
    v_hat = v / (1.0 - ADAM_B2 ** ADAM_STEP)
    delta = -ADAM_LR * (m_hat / (jnp.sqrt(v_hat) + ADAM_EPS) + ADAM_WD * w)
    return delta, m, v


ADAM_STEPS = 8


def _adamw_shards(items, name, jobs=None):
    jobs = jobs or _Jobs([])
    ns, nw = jobs.n, len(items)

    def body(*rest):
        ins, rest = rest[:5 * nw], rest[5 * nw:]
        job_ins, rest = rest[:ns], rest[ns:]
        outs, rest = rest[:4 * nw], rest[4 * nw:]
        start, finish = jobs.bind(job_ins, rest[:ns], rest[ns:])
        i = pl.program_id(0)
        pl.when(i == 0)(start)
        pl.when(i == ADAM_STEPS - 1)(finish)
        for k in range(nw):
            p_ref, own_ref, w_ref, m_ref, v_ref = ins[5 * k:5 * k + 5]
            g_ref, d_ref, nm_ref, nv_ref = outs[4 * k:4 * k + 4]
            g = ((p_ref[0].astype(F32) + p_ref[1].astype(F32)) + p_ref[2].astype(F32)) + own_ref[...]
            g_ref[...] = g
            d_ref[...], nm_ref[...], nv_ref[...] = _adamw(w_ref[...], g, m_ref[...], v_ref[...])

    in_specs, out_specs, out_shape, operands = [], [], [], []
    for parts, own, w, m, v in items:
        r, cdim = w.shape
        tr = r // ADAM_STEPS
        spec = pl.BlockSpec((tr, cdim), lambda i: (i, 0))
        in_specs += [pl.BlockSpec((3, tr, cdim), lambda i: (0, i, 0)), spec, spec, spec, spec]
        out_specs += [spec] * 4
        out_shape += [jax.ShapeDtypeStruct((r, cdim), F32)] * 4
        operands += [parts, own, w, m, v]
    outs = pl.pallas_call(
        body, name=name, grid=(ADAM_STEPS,),
        in_specs=in_specs + [ANY] * ns, out_specs=out_specs + [ANY] * ns, scratch_shapes=jobs.sems,
        out_shape=out_shape + jobs.out_shapes,
        compiler_params=_cp(("arbitrary",)),
    )(*operands, *jobs.inputs)
    return [outs[4 * k:4 * k + 4] for k in range(nw)], jobs.split(outs[4 * nw:])


def _adamw_small(items):
    n = len(items)

    def body(*refs):
        ins, outs = refs[:4 * n], refs[4 * n:]
        for k in range(n):
            w_ref, g_ref, m_ref, v_ref = ins[4 * k:4 * k + 4]
            d_ref, nm_ref, nv_ref = outs[3 * k:3 * k + 3]
            d_ref[...], nm_ref[...], nv_ref[...] = _adamw(w_ref[...], g_ref[...], m_ref[...], v_ref[...])

    vm = pl.BlockSpec(memory_space=pltpu.VMEM)
    shapes = [jax.ShapeDtypeStruct(w.shape, F32) for w, _, _, _ in items for _ in range(3)]
    outs = pl.pallas_call(body, name="adamw_small", in_specs=[vm] * (4 * n), out_specs=[vm] * (3 * n),
                          out_shape=shapes)(*[t for item in items for t in item])
    return [outs[3 * k:3 * k + 3] for k in range(n)]


def _pair_sums(where, mine, theirs, name):
    _, r, cdim = theirs.shape
    tr = 128 if r % 128 == 0 else r

    def body(where_ref, a_ref, b_ref, own_ref, wire_ref):
        chip = where_ref[1]
        own_ref[...] = a_ref[chip] + b_ref[chip]
        wire_ref[...] = (a_ref[...] + b_ref[...]).astype(WIRE_DTYPE)

    spec = pl.BlockSpec((4, tr, cdim), lambda i, s: (0, i, 0))
    mine_spec = spec if mine.ndim == 3 else pl.BlockSpec((None, 4, tr, cdim), lambda i, s: (s[0], 0, i, 0))
    return pl.pallas_call(
        body, name=name,
        grid_spec=pltpu.PrefetchScalarGridSpec(
            num_scalar_prefetch=1, grid=(r // tr,), in_specs=[mine_spec, spec],
            out_specs=[pl.BlockSpec((tr, cdim), lambda i, s: (i, 0)), spec]),
        out_shape=[jax.ShapeDtypeStruct((r, cdim), F32), jax.ShapeDtypeStruct(theirs.shape, WIRE_DTYPE)],
        compiler_params=_cp(("arbitrary",)))(where, mine, theirs)


def kernel(x, meta_tokens, norm_mix_w, w_in, w_gate_up, b_gate, gla_norm_w, sinks, w_out, norm_ff_w, w_ff1, w_ff2, final_norm_w, loss_target, m_meta_tokens, m_norm_mix_w, m_w_in, m_w_gate_up, m_b_gate, m_gla_norm_w, m_sinks, m_w_out, m_norm_ff_w, m_w_ff1, m_w_ff2, m_final_norm_w, v_meta_tokens, v_norm_mix_w, v_w_in, v_w_gate_up, v_b_gate, v_gla_norm_w, v_sinks, v_w_out, v_norm_ff_w, v_w_ff1, v_w_ff2, v_final_norm_w):
    seq = x.shape[1]
    rows = LEAD + seq
    tm = _row_tile(rows)
    tm_wide = WIDE_ROW_TILE if rows % WIDE_ROW_TILE == 0 else tm
    dev =4 * lax.axis_index("x") + 2 * lax.axis_index("y") + lax.axis_index("c")

    small_shard = jnp.concatenate([meta_tokens, w_gate_up[0], jnp.zeros((N_META, 96), F32)], axis=1)
    g_in, g_small = _all_gather([w_in[0].astype(WIRE_DTYPE), small_shard])
    later_shards = [w_out[0].astype(WIRE_DTYPE), w_ff1[0].astype(WIRE_DTYPE), w_ff2[0].astype(WIRE_DTYPE)]
    win_p = _win_padded(g_in)
    meta_full = jnp.transpose(g_small[:, :, 0:128], (1, 0, 2)).reshape(N_META, D)
    wg_full = jnp.transpose(g_small[:, :, 128:160], (1, 0, 2)).reshape(GLA_RANK, GLA_HEADS * GLA_DK)
    wg_p = jnp.concatenate([wg_full, jnp.zeros((128 - GLA_RANK, 256), F32)], axis=0)

    lead = jnp.concatenate([jnp.zeros((META0, D), F32), meta_full], axis=0)
    angles = _rope_angles(rows)
    proj, qr, kr, vr, (g_w1,) = _in_proj(x[0], lead, norm_mix_w, win_p, angles, tm, later_shards[1:2])
    oraw, og, states, decay, dgate, (g_out,) = _gla_fwd(proj, wg_p, b_gate, gla_norm_w, later_shards[0:1])
    osw, (g_w2,) = _swa_fwd(qr, kr, vr, sinks, later_shards[2:3])
    wout_full = g_out.reshape(D, D)
    w2_full = g_w2.reshape(D_FF, D)
    w1_full = jnp.transpose(g_w1, (1, 0, 2)).reshape(D, D_FF)
    h1, f, ft = _out_proj(x[0], lead, og, osw, wout_full, norm_ff_w, tm)
    a, dh2, dh2t, loss_p, gfn_p = _ffn_fwd(f, h1, w1_full, w2_full, loss_target[0], final_norm_w.reshape(1, D), tm)

    da, dh1, gnf_p = _ffn_bwd_act(dh2, a, w1_full, w2_full, h1, norm_ff_w, tm)
    dw1, dw2 = _ffn_bwd_weights(ft, a, da, dh2t, tm_wide)
    where = jnp.stack([lax.axis_index("c"), 2 * lax.axis_index("x") + lax.axis_index("y")]).astype(jnp.int32)
    dog, dos, dwout, theirs_ffn = _out_proj_bwd(dh1, og, osw, wout_full, tm, [dw1, dw2])
    pairs_ffn = [_pair_sums(where, p, q, "reduce_pair_%d" % (2 + k))
                 for k, (p, q) in enumerate(zip([dw1, dw2], theirs_ffn))]
    sums_ffn, wires_ffn = [p[0] for p in pairs_ffn], [p[1] for p in pairs_ffn]
    dsq, dsk, dsv, dsink_p, (parts_ffn, (theirs_wout,)) = _swa_bwd(
        qr, kr, vr, osw, dos, sinks, _Jobs([("chips", wires_ffn), ("sibling", [dwout])]))
    sum_wout, wire_wout = _pair_sums(where, dwout, theirs_wout, "reduce_pair_1")
    (dgq, dgk, dgv, dgr, dlr, dwg_p, dbg_p, dgnw_p), ((parts_wout,),) = _gla_bwd(
        proj, decay, dgate, oraw, states, dog, wg_p, gla_norm_w, _Jobs([("chips", [wire_wout])]))
    grad_x, dlead, dproj, ut, gnm_p = _in_proj_bwd(x[0], lead, dh1, norm_mix_w, win_p, dgv, dgr, dsq, dgq, dgk, dsk,
                                                   dsv, dlr, angles, tm)
    grad_x = grad_x[None]
    dwin_mine, dwin_theirs, total = _in_proj_bwd_weights(
        ut, dproj, tm_wide, [dlead, dwg_p, gnm_p, gnf_p, gfn_p, dbg_p, dgnw_p, loss_p, dsink_p])
    sum_win, sum_win_wire = _pair_sums(where, dwin_mine, dwin_theirs, "reduce_pair_0")

    g_meta = lax.dynamic_slice(total, (R_META, dev * 128), (N_META, 128))
    g_wg = lax.dynamic_slice(total, (R_WG, dev * 32), (GLA_RANK, 32))
    g_norm_mix, g_norm_ff = total[R_NORM_MIX:R_NORM_MIX + 1], total[R_NORM_FF:R_NORM_FF + 1]
    g_final_norm = total[R_FINAL:R_FINAL + 1]
    g_b_gate, g_gla_norm = total[R_B_GATE:R_B_GATE + 1, 0:256], total[R_GLA_NORM:R_GLA_NORM + 1, 0:128]
    g_sinks = total[R_SINKS:R_SINKS + SWA_HEADS, 0].reshape(1, SWA_HEADS)
    loss = total[R_LOSS, 0]

    ((g_wout, d_wout, nm_wout, nv_wout), (g_w1s, d_w1, nm_w1, nv_w1), (g_w2s, d_w2, nm_w2, nv_w2)), ((parts_win,),) = \
        _adamw_shards([(parts_wout, sum_wout, w_out[0], m_w_out[0], v_w_out[0]),
                       (parts_ffn[0], sums_ffn[0], w_ff1[0], m_w_ff1[0], v_w_ff1[0]),
                       (parts_ffn[1], sums_ffn[1], w_ff2[0], m_w_ff2[0], v_w_ff2[0])],
                      "adamw_w_out_ff", _Jobs([("chips", [sum_win_wire])]))
    ((g_win, d_win, nm_win, nv_win),), _ = _adamw_shards(
        [(parts_win, sum_win, w_in[0], m_w_in[0], v_w_in[0])], "adamw_w_in")

    names = ["meta", "wg", "norm_mix", "b_gate", "gla_norm", "sinks", "norm_ff", "final_norm"]
    ws = [meta_tokens, w_gate_up, norm_mix_w, b_gate, gla_norm_w, sinks, norm_ff_w, final_norm_w]
    gs = [g_meta, g_wg, g_norm_mix, g_b_gate, g_gla_norm, g_sinks, g_norm_ff, g_final_norm]
    ms = [m_meta_tokens, m_w_gate_up, m_norm_mix_w, m_b_gate, m_gla_norm_w, m_sinks, m_norm_ff_w, m_final_norm_w]
    vs = [v_meta_tokens, v_w_gate_up, v_norm_mix_w, v_b_gate, v_gla_norm_w, v_sinks, v_norm_ff_w, v_final_norm_w]
    flat = lambda t: t.reshape(-1, t.shape[-1])
    small_out = _adamw_small([(flat(w), flat(g), flat(m), flat(v)) for w, g, m, v in zip(ws, gs, ms, vs)])
    d_small = {n: small_out[k][0].reshape(ws[k].shape) for k, n in enumerate(names)}
    nm_small = {n: small_out[k][1].reshape(ws[k].shape) for k, n in enumerate(names)}
    nv_small = {n: small_out[k][2].reshape(ws[k].shape) for k, n in enumerate(names)}
    g_small_d = {n: g.reshape(ws[k].shape) for k, (n, g) in enumerate(zip(names, gs))}

    def ordered(big, small_d):
        win_v, wout_v, w1_v, w2_v = big
        return (small_d["meta"], small_d["norm_mix"], win_v[None], small_d["wg"], small_d["b_gate"],
                small_d["gla_norm"], small_d["sinks"], wout_v[None], small_d["norm_ff"], w1_v[None], w2_v[None],
                small_d["final_norm"])

    return (loss, grad_x,
            *ordered((g_win, g_wout, g_w1s, g_w2s), g_small_d),
            *ordered((d_win, d_wout, d_w1, d_w2), d_small),
            *ordered((nm_win, nm_wout, nm_w1, nm_w2), nm_small),
            *ordered((nv_win, nv_wout, nv_w1, nv_w2), nv_small))
```

```python
import functools

import jax
import jax.numpy as jnp
from jax import lax
from jax.experimental import pallas as pl
from jax.experimental.pallas import tpu as pltpu

F32 = jnp.float32
MXU_DTYPE = jnp.bfloat16
ACT_DTYPE = jnp.bfloat16
WIRE_DTYPE = jnp.bfloat16

D = 1024
N_META = 16
LEAD = 128
META0 = LEAD - N_META
EPS = 1e-5
GLA_HEADS, GLA_DK, GLA_DV, GLA_RANK, GLA_CHUNK = 4, 64, 128, 16, 64
GLA_TAU = 16.0
SWA_HEADS, SWA_KV, SWA_GROUP, SWA_HD, SWA_BLOCK = 8, 2, 4, 64, 128
ROPE_DIM, ROPE_THETA = 16, 500000.0
D_FF = 4096
N_DEV = 8
FF_TILE = D_FF // N_DEV
FF_WIDE = 2048
NEG = -1e30

C_GV, C_GR, C_GQ, C_GK, C_LR, C_SQ, C_SK, C_SV = 0, 512, 1024, 1280, 1536, 1664, 2176, 2304
DGLA = 1664
DINP = 2432
DIN = 2320
O_GQ, O_GK, O_GV, O_GR, O_LR, O_SQ, O_SK, O_SV = (0, 256), (256, 512), (512, 1024), (1024, 1536), (1536, 1552), (1552, 2064), (2064, 2192), (2192, 2320)

ADAM_LR, ADAM_B1, ADAM_B2, ADAM_EPS, ADAM_WD, ADAM_STEP = 0.001, 0.9, 0.999, 1e-08, 0.01, 10

MESH = pl.DeviceIdType.MESH
ANY = pl.BlockSpec(memory_space=pl.ANY)
VMEM_TILE_MB, VMEM_WIDE_MB = 48, 56


def _cp(sem=None, vmem_mb=None):
    kw = {}
    if sem is not None:
        kw["dimension_semantics"] = sem
    if vmem_mb is not None:
        kw["vmem_limit_bytes"] = vmem_mb << 20
    return pltpu.CompilerParams(**kw)


def _mm(a, b):
    return jnp.dot(a.astype(MXU_DTYPE), b.astype(MXU_DTYPE), preferred_element_type=F32)


def _mm_nt(a, b):
    return lax.dot_general(a.astype(MXU_DTYPE), b.astype(MXU_DTYPE), (((1,), (1,)), ((), ())),
                           preferred_element_type=F32)


def _mm_tn(a, b):
    return lax.dot_general(a.astype(MXU_DTYPE), b.astype(MXU_DTYPE), (((0,), (0,)), ((), ())),
                           preferred_element_type=F32)


def _masked_sums(mask, t):
    m = mask.astype(jnp.bfloat16)
    hi = t.astype(jnp.bfloat16)
    rest = t - hi.astype(F32)
    mid = rest.astype(jnp.bfloat16)
    low = (rest - mid.astype(F32)).astype(jnp.bfloat16)
    dot = lambda part: jnp.dot(m, part, preferred_element_type=F32)
    return dot(hi) + (dot(mid) + dot(low))


def _logsigmoid(z):
    return jnp.minimum(z, 0.0) - jnp.log(1.0 + jnp.exp(-jnp.abs(z)))


def _sigmoid(z):
    return 1.0 / (1.0 + jnp.exp(-z))


ROW_TILE, WIDE_ROW_TILE = 640, 1664


def _row_tile(rows, want=ROW_TILE):
    return want if rows % want == 0 else LEAD


def _mesh_pos():
    return lax.axis_index("x"), lax.axis_index("y"), lax.axis_index("c")


def _all_gather(shards):
    n = len(shards)

    def body(*refs):
        start, forward, finish = _gather_schedule(refs[:n], refs[n:2 * n], *refs[2 * n:])
        start()
        for j in range(3):
            forward(j)
        finish()

    gathered = pl.pallas_call(
        body, name="all_gather_weights",
        out_shape=_gathered_shapes(shards), in_specs=[ANY] * n, out_specs=[ANY] * n,
        scratch_shapes=_gather_sems(n),
    )(*shards)
    return _with_own_block(gathered, shards)


def _gathered_shapes(shards):
    return [jax.ShapeDtypeStruct((N_DEV,) + s.shape, s.dtype) for s in shards]


def _gather_sems(n):
    return [pltpu.SemaphoreType.DMA((7 * n,)), pltpu.SemaphoreType.DMA((7 * n,))] if n else []


def _place_gather(step, steps, shard_refs, gathered_refs, sems):
    if not shard_refs:
        return
    start, forward, finish = _gather_schedule(shard_refs, gathered_refs, *sems)
    pl.when(step == 0)(start)
    for j, at in enumerate((steps * 7 // 10, steps * 8 // 10, steps * 9 // 10)):
        pl.when(step == at)(functools.partial(forward, j))
    pl.when(step == steps - 1)(finish)


def _with_own_block(gathered, shards):
    dev = 4 * lax.axis_index("x") + 2 * lax.axis_index("y") + lax.axis_index("c")
    return [lax.dynamic_update_index_in_dim(g, s, dev, 0) for g, s in zip(gathered, shards)]


def _gather_schedule(ins, outs, send_sems, recv_sems):
    n = len(ins)
    x, y, c = _mesh_pos()
    me, sibling = (x, y, c), (x, y, 1 - c)
    chips = [(1 - x, y), (x, 1 - y), (1 - x, 1 - y)]

    def copy(a, k, block, to, src=None):
        dst = outs[a].at[4 * block[0] + 2 * block[1] + block[2]]
        return pltpu.make_async_remote_copy(
            src_ref=dst if src is None else src, dst_ref=dst,
            send_sem=send_sems.at[a * 7 + k], recv_sem=recv_sems.at[a * 7 + k],
            device_id=to, device_id_type=MESH)

    def first(a):
        return [copy(a, 0, me, sibling, src=ins[a])] + [copy(a, 1 + j, me, (*chip, c), src=ins[a])
                                                        for j, chip in enumerate(chips)]

    def start():
        for a in range(n):
            for cp in first(a):
                cp.start()

    def forward(j):
        for a in range(n):
            copy(a, 1 + j, (*chips[j], c), me).wait_recv()
            copy(a, 4 + j, (*chips[j], c), sibling).start()

    def finish():
        for a in range(n):
            copy(a, 0, sibling, me).wait_recv()
            for j, chip in enumerate(chips):
                copy(a, 4 + j, (*chip, 1 - c), me).wait_recv()
        for a in range(n):
            for cp in first(a) + [copy(a, 4 + j, (*chip, c), sibling) for j, chip in enumerate(chips)]:
                cp.wait_send()

    return start, forward, finish


def _sibling_shapes(gs):
    return [jax.ShapeDtypeStruct(g.shape[1:], g.dtype) for g in gs]


def _sibling_sems(n):
    return [pltpu.SemaphoreType.DMA((n,)), pltpu.SemaphoreType.DMA((n,))]


def _sibling_schedule(ins, land, send_sems, recv_sems):
    x, y, c = _mesh_pos()

    def copies():
        return [pltpu.make_async_remote_copy(
            src_ref=ins[a].at[1 - c], dst_ref=land[a], send_sem=send_sems.at[a], recv_sem=recv_sems.at[a],
            device_id=(x, y, 1 - c), device_id_type=MESH) for a in range(len(ins))]

    def start():
        for cp in copies():
            cp.start()

    def finish():
        for cp in copies():
            cp.wait_recv()
        for cp in copies():
            cp.wait_send()

    return start, finish


def _chips_shapes(ps):
    return [jax.ShapeDtypeStruct((3,) + p.shape[1:], p.dtype) for p in ps]


def _chips_sems(n):
    return [pltpu.SemaphoreType.DMA((3 * n,)), pltpu.SemaphoreType.DMA((3 * n,))]


def _chips_schedule(ins, land, send_sems, recv_sems):
    x, y, c = _mesh_pos()
    chips = [(1 - x, y), (x, 1 - y), (1 - x, 1 - y)]

    def copies():
        return [pltpu.make_async_remote_copy(
            src_ref=ins[a].at[2 * chip[0] + chip[1]], dst_ref=land[a].at[j],
            send_sem=send_sems.at[3 * a + j], recv_sem=recv_sems.at[3 * a + j],
            device_id=(*chip, c), device_id_type=MESH) for a in range(len(ins)) for j, chip in enumerate(chips)]

    def start():
        for cp in copies():
            cp.start()

    def finish():
        for cp in copies():
            cp.wait_recv()
        for cp in copies():
            cp.wait_send()

    return start, finish


class _Jobs:
    def __init__(self, jobs):
        self.jobs = jobs
        self.inputs = [a for _, arrs in jobs for a in arrs]
        self.out_shapes = [s for kind, arrs in jobs
                           for s in (_sibling_shapes(arrs) if kind == "sibling" else _chips_shapes(arrs))]
        self.sems = [s for kind, arrs in jobs
                     for s in (_sibling_sems(len(arrs)) if kind == "sibling" else _chips_sems(len(arrs)))]
        self.n = len(self.inputs)

    def bind(self, in_refs, out_refs, sem_refs):
        starts, finishes, at = [], [], 0
        for k, (kind, arrs) in enumerate(self.jobs):
            schedule = _sibling_schedule if kind == "sibling" else _chips_schedule
            start, finish = schedule(in_refs[at:at + len(arrs)], out_refs[at:at + len(arrs)],
                                     sem_refs[2 * k], sem_refs[2 * k + 1])
            starts.append(start)
            finishes.append(finish)
            at += len(arrs)

        def start_all():
            for f in starts:
                f()

        def finish_all():
            for f in finishes:
                f()

        return start_all, finish_all

    def split(self, outs):
        res, at = [], 0
        for _, arrs in self.jobs:
            res.append(list(outs[at:at + len(arrs)]))
            at += len(arrs)
        return res


R_META, R_WG, R_NORM_MIX, R_NORM_FF, R_FINAL, R_B_GATE, R_GLA_NORM, R_LOSS, R_SINKS, SMALL_ROWS = 0, 16, 32, 33, 34, 35, 36, 37, 40, 48


SMALL_SPECS = [pl.BlockSpec((LEAD, D), lambda i: (0, 0)), pl.BlockSpec((128, 256), lambda i: (0, 0)),
               pl.BlockSpec((8, D), lambda i: (0, 0)), pl.BlockSpec((8, D), lambda i: (0, 0)),
               pl.BlockSpec((8, D), lambda i: (0, 0)), pl.BlockSpec((8, 256), lambda i: (0, 0)),
               pl.BlockSpec((8, 128), lambda i: (0, 0)), pl.BlockSpec((8, 128), lambda i: (0, 0)),
               pl.BlockSpec((8, 128), lambda i: (0, 0))]


def _small_sum_scratch():
    return [pltpu.VMEM((SMALL_ROWS, D), F32), pltpu.VMEM((N_DEV, SMALL_ROWS, D), F32),
            pltpu.SemaphoreType.DMA((7,)), pltpu.SemaphoreType.DMA((7,))]


def _small_sum_schedule(small_refs, out_ref, p_ref, land, send_sems, recv_sems):
    dlead_ref, dwg_ref, gnm_ref, gnf_ref, gfn_ref, dbg_ref, dgnw_ref, loss_ref, dsink_ref = small_refs
    x, y, c = _mesh_pos()
    me = 4 * x + 2 * y + c

    def copies():
        res = []
        for k in range(1, N_DEV):
            bx, by, bc = (k >> 2) & 1, (k >> 1) & 1, k & 1
            peer = (1 - x if bx else x, 1 - y if by else y, 1 - c if bc else c)
            res.append(pltpu.make_async_remote_copy(
                src_ref=p_ref, dst_ref=land.at[me], send_sem=send_sems.at[k - 1], recv_sem=recv_sems.at[k - 1],
                device_id=peer, device_id_type=MESH))
        return res

    def start():
        p_ref[...] = jnp.zeros_like(p_ref)
        p_ref[R_META:R_META + N_META, :] = dlead_ref[META0:LEAD, :]
        p_ref[R_WG:R_WG + GLA_RANK, 0:256] = dwg_ref[0:GLA_RANK, :]
        p_ref[R_NORM_MIX:R_NORM_MIX + 1, :] = gnm_ref[0:1, :]
        p_ref[R_NORM_FF:R_NORM_FF + 1, :] = gnf_ref[0:1, :]
        p_ref[R_FINAL:R_FINAL + 1, :] = gfn_ref[0:1, :]
        p_ref[R_B_GATE:R_B_GATE + 1, 0:256] = dbg_ref[0:1, :]
        p_ref[R_GLA_NORM:R_GLA_NORM + 1, 0:128] = dgnw_ref[0:1, :]
        p_ref[R_LOSS:R_LOSS + 1, 0:128] = loss_ref[0:1, :]
        p_ref[R_SINKS:R_SINKS + SWA_HEADS, 0:128] = dsink_ref[...]
        land[me] = p_ref[...]
        for cp in copies():
            cp.start()

    def finish():
        for cp in copies():
            cp.wait_recv()
        for cp in copies():
            cp.wait_send()
        acc = land[0]
        for d in range(1, N_DEV):
            acc = acc + land[d]
        out_ref[...] = acc

    return start, finish


def _token_specs(tm, grid_rank=1):
    nb = tm // LEAD

    def spec(k):
        if grid_rank == 1:
            return pl.BlockSpec((LEAD, D), lambda i: (jnp.maximum(i * nb + k - 1, 0), 0))
        return pl.BlockSpec((LEAD, D), lambda i, j: (jnp.maximum(i * nb + k - 1, 0), 0))

    return [spec(k) for k in range(nb)]


def _h_tile(i, lead_ref, x_refs):
    first = jnp.where(i == 0, lead_ref[...], x_refs[0][...])
    return jnp.concatenate([first] + [r[...] for r in x_refs[1:]], axis=0)


def _in_proj(x, lead, nw, win_p, angles, tm, shards):
    rows = LEAD + x.shape[0]
    nb = tm // LEAD
    steps = rows // tm
    ns = len(shards)

    def body(*refs):
        x_refs, refs = refs[:nb], refs[nb:]
        lead_ref, nw_ref, w_ref, cs_ref = refs[:4]
        shard_refs, (o_ref, q_ref, k_ref, v_ref) = refs[4:4 + ns], refs[4 + ns:8 + ns]
        _place_gather(pl.program_id(0), steps, shard_refs, refs[8 + ns:8 + 2 * ns], refs[8 + 2 * ns:])
        h = _h_tile(pl.program_id(0), lead_ref, x_refs)
        rstd = lax.rsqrt(jnp.mean(h * h, axis=-1, keepdims=True) + EPS)
        u = (h * rstd * nw_ref[...]).astype(MXU_DTYPE)
        proj = jnp.dot(u, w_ref[...].astype(MXU_DTYPE), preferred_element_type=F32)
        o_ref[...] = proj[:, 0:DGLA]
        cos, sa, sb = _rope_tables(cs_ref[...])
        q_ref[...] = (_rope(proj[:, C_SQ:C_SK], cos, sa, sb) * (SWA_HD ** -0.5)).astype(ACT_DTYPE)
        k_ref[...] = _rope(proj[:, C_SK:C_SV], cos, sa, sb).astype(ACT_DTYPE)
        v_ref[...] = proj[:, C_SV:DINP].astype(ACT_DTYPE)

    row = lambda w: pl.BlockSpec((tm, w), lambda i: (i, 0))
    outs = pl.pallas_call(
        body, name="in_proj", grid=(steps,),
        in_specs=_token_specs(tm) + [pl.BlockSpec((LEAD, D), lambda i: (0, 0)), pl.BlockSpec((1, D), lambda i: (0, 0)),
                                     pl.BlockSpec((D, DINP), lambda i: (0, 0)), row(ROPE_DIM)]
        + [ANY] * ns,
        out_specs=[row(DGLA), row(512), row(128), row(128)] + [ANY] * ns,
        out_shape=[jax.ShapeDtypeStruct((rows, DGLA), F32), jax.ShapeDtypeStruct((rows, 512), ACT_DTYPE),
                   jax.ShapeDtypeStruct((rows, 128), ACT_DTYPE), jax.ShapeDtypeStruct((rows, 128), ACT_DTYPE)]
        + _gathered_shapes(shards),
        scratch_shapes=_gather_sems(ns),
        compiler_params=_cp(("arbitrary",), VMEM_WIDE_MB),
    )(*([x] * nb), lead, nw, win_p, angles, *shards)
    return outs[0], outs[1], outs[2], outs[3], _with_own_block(outs[4:], shards)


def _rope_angles(rows):
    pos = (jnp.arange(rows, dtype=jnp.int32) - META0).astype(F32)
    inv_freq = 1.0 / (ROPE_THETA ** (jnp.arange(0, ROPE_DIM, 2, dtype=F32) / ROPE_DIM))
    ang = pos[:, None] * inv_freq[None, :]
    return jnp.concatenate([jnp.cos(ang), jnp.sin(ang)], axis=1)


def _rope_tables(cs):
    shape = (2 * (ROPE_DIM // 2), 3 * 128)
    j = lax.broadcasted_iota(jnp.int32, shape, 0)
    col = lax.broadcasted_iota(jnp.int32, shape, 1)
    table, in_head = col // 128, col % SWA_HD
    match = (j % (ROPE_DIM // 2)) == (in_head % (ROPE_DIM // 2))
    is_cos = j < ROPE_DIM // 2
    first, second = in_head < ROPE_DIM // 2, (in_head >= ROPE_DIM // 2) & (in_head < ROPE_DIM)
    spread = (jnp.where(match & is_cos & (table == 0) & (first | second), 1.0, 0.0)
              + jnp.where(match & ~is_cos & (table == 1) & first, -1.0, 0.0)
              + jnp.where(match & ~is_cos & (table == 2) & second, 1.0, 0.0)).astype(jnp.bfloat16)
    hi = cs.astype(jnp.bfloat16)
    rest = cs - hi.astype(F32)
    mid = rest.astype(jnp.bfloat16)
    low = (rest - mid.astype(F32)).astype(jnp.bfloat16)
    dot = lambda part: jnp.dot(part, spread, preferred_element_type=F32)
    tabs = dot(hi) + (dot(mid) + dot(low))
    lane = lax.broadcasted_iota(jnp.int32, (1, 128), 1) % SWA_HD
    return tabs[:, 0:128] + jnp.where(lane >= ROPE_DIM, 1.0, 0.0), tabs[:, 128:256], tabs[:, 256:384]


def _rope(xv, cos, sa, sb):
    width = xv.shape[1]
    reps = width // 128
    if reps > 1:
        cos, sa, sb = (jnp.tile(t, (1, reps)) for t in (cos, sa, sb))
    return xv * cos + pltpu.roll(xv, width - 8, 1) * sa + pltpu.roll(xv, 8, 1) * sb


def _unrope(dy, cos, sa, sb):
    width = dy.shape[1]
    reps = width // 128
    if reps > 1:
        cos, sa, sb = (jnp.tile(t, (1, reps)) for t in (cos, sa, sb))
    return dy * cos + pltpu.roll(dy * sa, 8, 1) + pltpu.roll(dy * sb, width - 8, 1)


def _gla_group(nc, most):
    for g in (10, 5, 2):
        if g <= most and nc % g == 0:
            return g
    return 1


def _chunk_masks(nrows):
    ii = lax.broadcasted_iota(jnp.int32, (nrows, nrows), 0)
    jj = lax.broadcasted_iota(jnp.int32, (nrows, nrows), 1)
    same = (ii // GLA_CHUNK) == (jj // GLA_CHUNK)
    return same & (jj <= ii), same & (jj >= ii)


def _gla_gates(lr, wg, bg, first_row, at_or_before):
    nrows = lr.shape[0]
    zg = _mm(lr, wg) + bg
    live = first_row + lax.broadcasted_iota(jnp.int32, (nrows, 1), 0) >= META0
    g = jnp.where(live, _logsigmoid(zg) * (1.0 / GLA_TAU), 0.0)
    return _masked_sums(at_or_before, g), jnp.where(live, _sigmoid(-zg) * (1.0 / GLA_TAU), 0.0)


def _tril64():
    ii = lax.broadcasted_iota(jnp.int32, (GLA_CHUNK, GLA_CHUNK), 0)
    jj = lax.broadcasted_iota(jnp.int32, (GLA_CHUNK, GLA_CHUNK), 1)
    return jj <= ii


def _gla_fwd(proj, wg_p, bg, gnw, shards):
    rows = proj.shape[0]
    nc = rows // GLA_CHUNK
    group = _gla_group(nc, 10)
    steps, nrows = nc // group, group * GLA_CHUNK
    ns = len(shards)

    def body(q_ref, k_ref, v_ref, r_ref, lr_ref, lr_next_ref, wg_ref, bg_ref, gnw_ref, *rest):
        shard_refs, rest = rest[:ns], rest[ns:]
        oraw_ref, og_ref, st_ref, decay_ref, dgate_ref = rest[:5]
        gathered_refs, rest = rest[5:5 + ns], rest[5 + ns:]
        state, gates, mask = rest[:3]
        c = pl.program_id(0)

        @pl.when(c == 0)
        def _():
            state[...] = jnp.zeros_like(state)
            mask[...] = _chunk_masks(nrows)[0].astype(jnp.bfloat16)
            gates[0, 0], gates[0, 1] = _gla_gates(lr_ref[...], wg_ref[...], bg_ref[...], 0, mask[...])

        _place_gather(c, steps, shard_refs, gathered_refs, rest[3:])
        slot = c % 2
        b = gates[slot, 0]
        decay_ref[...] = b
        dgate_ref[...] = gates[slot, 1]
        gates[1 - slot, 0], gates[1 - slot, 1] = _gla_gates(lr_next_ref[...], wg_ref[...], bg_ref[...],
                                                            (c + 1) * nrows, mask[...])
        eb = jnp.exp(b)
        gq = q_ref[...] * (GLA_DK ** -0.5) * eb
        gk = k_ref[...] * jnp.exp(-b)
        v = v_ref[...]
        gnw_v = gnw_ref[...]
        tril = _tril64()
        pairs = [(h, gi) for h in range(GLA_HEADS) for gi in range(group)]
        rs = {gi: slice(gi * GLA_CHUNK, (gi + 1) * GLA_CHUNK) for gi in range(group)}
        s64 = {h: slice(h * GLA_DK, (h + 1) * GLA_DK) for h in range(GLA_HEADS)}
        s128 = {h: slice(h * GLA_DV, (h + 1) * GLA_DV) for h in range(GLA_HEADS)}
        qh = {(h, gi): gq[rs[gi], s64[h]] for h, gi in pairs}
        kh = {(h, gi): gk[rs[gi], s64[h]] for h, gi in pairs}
        vh = {(h, gi): v[rs[gi], s128[h]] for h, gi in pairs}
        ebl = {(h, gi): eb[(gi + 1) * GLA_CHUNK - 1:(gi + 1) * GLA_CHUNK, s64[h]] for h, gi in pairs}
        av = {pr: _mm(jnp.where(tril, _mm_nt(qh[pr], kh[pr]), 0.0), vh[pr]) for pr in pairs}
        inc = {pr: _mm_tn(vh[pr], kh[pr] * ebl[pr]) for pr in pairs}
        st = {}
        for h in range(GLA_HEADS):
            cur = state[h]
            for gi in range(group):
                st[h, gi] = cur
                st_ref[gi, h] = cur
                cur = cur * ebl[h, gi] + inc[h, gi]
            state[h] = cur
        for h, gi in pairs:
            o = av[h, gi] + _mm_nt(qh[h, gi], st[h, gi])
            oraw_ref[rs[gi], s128[h]] = o
            rstd = lax.rsqrt(jnp.mean(o * o, axis=-1, keepdims=True) + EPS)
            rh = r_ref[rs[gi], s128[h]]
            og_ref[rs[gi], s128[h]] = (o * rstd * gnw_v * (rh * _sigmoid(rh))).astype(ACT_DTYPE)

    nb = lambda w, col: pl.BlockSpec((nrows, w), lambda c: (c, col // w))
    const = lambda shape: pl.BlockSpec(shape, lambda c: (0,) * len(shape))
    outs = pl.pallas_call(
        body, name="gla_fwd", grid=(steps,),
        in_specs=[nb(256, C_GQ), nb(256, C_GK), nb(512, C_GV), nb(512, C_GR), nb(128, C_LR),
                  pl.BlockSpec((nrows, 128), lambda c: (jnp.minimum(c + 1, steps - 1), C_LR // 128)),
                  const((128, 256)), const((1, 256)), const((1, 128))] + [ANY] * ns,
        out_specs=[nb(512, 0), nb(512, 0),
                   pl.BlockSpec((group, GLA_HEADS, GLA_DV, GLA_DK), lambda c: (c, 0, 0, 0)),
                   nb(256, 0), nb(256, 0)] + [ANY] * ns,
        out_shape=[jax.ShapeDtypeStruct((rows, 512), F32), jax.ShapeDtypeStruct((rows, 512), ACT_DTYPE),
                   jax.ShapeDtypeStruct((nc, GLA_HEADS, GLA_DV, GLA_DK), F32),
                   jax.ShapeDtypeStruct((rows, 256), F32), jax.ShapeDtypeStruct((rows, 256), F32)]
        + _gathered_shapes(shards),
        scratch_shapes=[pltpu.VMEM((GLA_HEADS, GLA_DV, GLA_DK), F32), pltpu.VMEM((2, 2, nrows, 256), F32),
                        pltpu.VMEM((nrows, nrows), jnp.bfloat16)]
        + _gather_sems(ns),
        compiler_params=_cp(("arbitrary",)),
    )(proj, proj, proj, proj, proj, proj, wg_p, bg, gnw, *shards)
    return outs[0], outs[1], outs[2], outs[3], outs[4], _with_own_block(outs[5:], shards)


def _swa_mask(n):
    shape = (SWA_BLOCK, 3 * SWA_BLOCK)
    qi = lax.broadcasted_iota(jnp.int32, shape, 0)
    jj = lax.broadcasted_iota(jnp.int32, shape, 1)
    meta = (jj < SWA_BLOCK) & (jj >= META0) & ((n > 0) | (jj <= qi))
    prev = (jj >= SWA_BLOCK) & (jj < 2 * SWA_BLOCK) & (n >= 2) & (jj - SWA_BLOCK > qi)
    cur = (jj >= 2 * SWA_BLOCK) & (n >= 1) & (jj - 2 * SWA_BLOCK <= qi)
    return meta | prev | cur


def _swa_group(nblk):
    return 5 if nblk % 5 == 0 else 1


def _swa_specs(group):
    blk = lambda w: pl.BlockSpec((group * SWA_BLOCK, w), lambda n: (n, 0))
    first = pl.BlockSpec((SWA_BLOCK, 128), lambda n: (0, 0))
    prev = pl.BlockSpec((SWA_BLOCK, 128), lambda n: (jnp.maximum(n * group - 1, 0), 0))
    return blk, first, prev


def _swa_keys(first_ref, prev_ref, cur_ref, g):
    own = cur_ref[g * SWA_BLOCK:(g + 1) * SWA_BLOCK, :]
    before = prev_ref[...] if g == 0 else cur_ref[(g - 1) * SWA_BLOCK:g * SWA_BLOCK, :]
    return jnp.concatenate([first_ref[...], before, own], axis=0)


def _swa_fwd(qr, kr, vr, sinks, shards):
    rows = qr.shape[0]
    nblk = rows // SWA_BLOCK
    group = _swa_group(nblk)
    steps = nblk // group
    ns = len(shards)

    def body(q_ref, k0, kp, kc, v0, vp, vc, sink_ref, *rest):
        o_ref = rest[ns]
        _place_gather(pl.program_id(0), steps, rest[:ns], rest[ns + 1:2 * ns + 1], rest[2 * ns + 1:])
        for g in range(group):
            n = pl.program_id(0) * group + g
            rs = slice(g * SWA_BLOCK, (g + 1) * SWA_BLOCK)
            kall, vall = _swa_keys(k0, kp, kc, g), _swa_keys(v0, vp, vc, g)
            mask = _swa_mask(n)
            heads = range(SWA_HEADS)
            hs = [slice(h * SWA_HD, (h + 1) * SWA_HD) for h in heads]
            kv = [slice((h // SWA_GROUP) * SWA_HD, (h // SWA_GROUP + 1) * SWA_HD) for h in heads]
            s = [jnp.where(mask, _mm_nt(q_ref[rs, hs[h]], kall[:, kv[h]]), NEG) for h in heads]
            m = [jnp.maximum(jnp.max(s[h], axis=-1, keepdims=True), sink_ref[0, h]) for h in heads]
            p = [jnp.exp(s[h] - m[h]) for h in heads]
            den = [jnp.sum(p[h], axis=-1, keepdims=True) + jnp.exp(sink_ref[0, h] - m[h]) for h in heads]
            o = [_mm(p[h], vall[:, kv[h]]) for h in heads]
            for h in heads:
                o_ref[rs, hs[h]] = (o[h] / den[h]).astype(ACT_DTYPE)

    blk, first, prev = _swa_specs(group)
    outs = pl.pallas_call(
        body, name="swa_fwd", grid=(steps,),
        in_specs=[blk(512), first, prev, blk(128), first, prev, blk(128),
                  pl.BlockSpec(memory_space=pltpu.SMEM)] + [ANY] * ns,
        out_specs=[blk(512)] + [ANY] * ns,
        out_shape=[jax.ShapeDtypeStruct((rows, 512), ACT_DTYPE)] + _gathered_shapes(shards),
        scratch_shapes=_gather_sems(ns),
        compiler_params=_cp(("arbitrary",)),
    )(qr, kr, kr, kr, vr, vr, vr, sinks, *shards)
    return outs[0], _with_own_block(outs[1:], shards)


def _out_proj(x, lead, og, osw, wout, nfw, tm):
    rows = LEAD + x.shape[0]
    nb = tm // LEAD

    def body(*refs):
        x_refs, (lead_ref, og_ref, os_ref, w_ref, nw_ref, h1_ref, f_ref, ft_ref) = refs[:nb], refs[nb:]
        h0 = _h_tile(pl.program_id(0), lead_ref, x_refs)
        h1 = h0 + _mm(og_ref[...], w_ref[0:512, :]) + _mm(os_ref[...], w_ref[512:1024, :])
        h1_ref[...] = h1
        rstd = lax.rsqrt(jnp.mean(h1 * h1, axis=-1, keepdims=True) + EPS)
        f = h1 * rstd * nw_ref[...]
        f_ref[...] = f.astype(ACT_DTYPE)
        ft_ref[...] = f.T.astype(ACT_DTYPE)

    row = lambda w: pl.BlockSpec((tm, w), lambda i: (i, 0))
    return pl.pallas_call(
        body, name="out_proj", grid=(rows // tm,),
        in_specs=_token_specs(tm) + [pl.BlockSpec((LEAD, D), lambda i: (0, 0)), row(512), row(512),
                                     pl.BlockSpec((D, D), lambda i: (0, 0)), pl.BlockSpec((1, D), lambda i: (0, 0))],
        out_specs=[row(D), row(D), pl.BlockSpec((D, tm), lambda i: (0, i))],
        out_shape=[jax.ShapeDtypeStruct((rows, D), F32), jax.ShapeDtypeStruct((rows, D), ACT_DTYPE),
                   jax.ShapeDtypeStruct((D, rows), ACT_DTYPE)],
        compiler_params=_cp(("arbitrary",), VMEM_TILE_MB),
    )(*([x] * nb), lead, og, osw, wout, nfw)


def _ffn_fwd(f, h1, w1, w2, tgt, fnw, tm):
    rows = f.shape[0]
    nj = D_FF // FF_WIDE
    nb = tm // LEAD

    def body(f_ref, h1_ref, w1_ref, w2_ref, nw_ref, *rest):
        t_refs, (a_ref, dh2_ref, dh2t_ref, loss_ref, gfn_ref, acc) = rest[:nb], rest[nb:]
        i, j = pl.program_id(0), pl.program_id(1)

        @pl.when((i == 0) & (j == 0))
        def _():
            loss_ref[...] = jnp.zeros_like(loss_ref)
            gfn_ref[...] = jnp.zeros_like(gfn_ref)

        @pl.when(j == 0)
        def _():
            acc[...] = jnp.zeros_like(acc)

        a = _mm(f_ref[...], w1_ref[...])
        a_ref[...] = a.astype(ACT_DTYPE)
        z = jnp.square(jnp.maximum(a, 0.0))
        acc[...] += _mm(z, w2_ref[...])

        @pl.when(j == nj - 1)
        def _():
            h2 = h1_ref[...] + acc[...]
            rstd = lax.rsqrt(jnp.mean(h2 * h2, axis=-1, keepdims=True) + EPS)
            hn = h2 * rstd
            nw = nw_ref[...]
            row = i * tm + lax.broadcasted_iota(jnp.int32, (tm, 1), 0)
            target = jnp.concatenate([t[...] for t in t_refs], axis=0)
            err = jnp.where(row >= LEAD, hn * nw - target, 0.0)
            row_loss = jnp.sum(err * err, axis=-1, keepdims=True) * (1.0 / D)
            loss_ref[...] += jnp.broadcast_to(0.5 * jnp.sum(row_loss, axis=0, keepdims=True), loss_ref.shape)
            dy = err * (1.0 / D)
            gfn_ref[...] += jnp.broadcast_to(jnp.sum(dy * hn, axis=0, keepdims=True), gfn_ref.shape)
            dhn = dy * nw
            dh2 = rstd * (dhn - hn * jnp.mean(dhn * hn, axis=-1, keepdims=True))
            dh2_ref[...] = dh2
            dh2t_ref[...] = dh2.T.astype(ACT_DTYPE)

    return pl.pallas_call(
        body, name="ffn_fwd", grid=(rows // tm, nj),
        in_specs=[pl.BlockSpec((tm, D), lambda i, j: (i, 0)), pl.BlockSpec((tm, D), lambda i, j: (i, 0)),
                  pl.BlockSpec((D, FF_WIDE), lambda i, j: (0, j)),
                  pl.BlockSpec((FF_WIDE, D), lambda i, j: (j, 0)),
                  pl.BlockSpec((1, D), lambda i, j: (0, 0))] + _token_specs(tm, grid_rank=2),
        out_specs=[pl.BlockSpec((tm, FF_WIDE), lambda i, j: (i, j)), pl.BlockSpec((tm, D), lambda i, j: (i, 0)),
                   pl.BlockSpec((D, tm), lambda i, j: (0, i)),
                   pl.BlockSpec((8, 128), lambda i, j: (0, 0)), pl.BlockSpec((8, D), lambda i, j: (0, 0))],
        out_shape=[jax.ShapeDtypeStruct((rows, D_FF), ACT_DTYPE), jax.ShapeDtypeStruct((rows, D), F32),
                   jax.ShapeDtypeStruct((D, rows), ACT_DTYPE),
                   jax.ShapeDtypeStruct((8, 128), F32), jax.ShapeDtypeStruct((8, D), F32)],
        scratch_shapes=[pltpu.VMEM((tm, D), F32)],
        compiler_params=_cp(("arbitrary", "arbitrary"), VMEM_WIDE_MB),
    )(f, h1, w1, w2, fnw, *([tgt] * nb))


def _ffn_bwd_act(dh2, a, w1, w2, h1, nfw, tm):
    rows = dh2.shape[0]
    nj = D_FF // FF_WIDE

    def body(dh2_ref, a_ref, w1_ref, w2_ref, h1_ref, nw_ref, da_ref, dh1_ref, gnf_ref, acc):
        i, j = pl.program_id(0), pl.program_id(1)

        @pl.when((i == 0) & (j == 0))
        def _():
            gnf_ref[...] = jnp.zeros_like(gnf_ref)

        @pl.when(j == 0)
        def _():
            acc[...] = jnp.zeros_like(acc)

        dz = _mm_nt(dh2_ref[...], w2_ref[...])
        da = dz * (2.0 * jnp.maximum(a_ref[...].astype(F32), 0.0))
        da_ref[...] = da.astype(ACT_DTYPE)
        acc[...] += _mm_nt(da, w1_ref[...])

        @pl.when(j == nj - 1)
        def _():
            h1 = h1_ref[...]
            rstd = lax.rsqrt(jnp.mean(h1 * h1, axis=-1, keepdims=True) + EPS)
            hn = h1 * rstd
            df = acc[...]
            gnf_ref[...] += jnp.broadcast_to(jnp.sum(df * hn, axis=0, keepdims=True), gnf_ref.shape)
            dfn = df * nw_ref[...]
            dh1_ref[...] = dh2_ref[...] + rstd * (dfn - hn * jnp.mean(dfn * hn, axis=-1, keepdims=True))

    return pl.pallas_call(
        body, name="ffn_bwd_act", grid=(rows // tm, nj),
        in_specs=[pl.BlockSpec((tm, D), lambda i, j: (i, 0)), pl.BlockSpec((tm, FF_WIDE), lambda i, j: (i, j)),
                  pl.BlockSpec((D, FF_WIDE), lambda i, j: (0, j)),
                  pl.BlockSpec((FF_WIDE, D), lambda i, j: (j, 0)),
                  pl.BlockSpec((tm, D), lambda i, j: (i, 0)), pl.BlockSpec((1, D), lambda i, j: (0, 0))],
        out_specs=[pl.BlockSpec((tm, FF_WIDE), lambda i, j: (i, j)), pl.BlockSpec((tm, D), lambda i, j: (i, 0)),
                   pl.BlockSpec((8, D), lambda i, j: (0, 0))],
        out_shape=[jax.ShapeDtypeStruct((rows, D_FF), ACT_DTYPE), jax.ShapeDtypeStruct((rows, D), F32),
                   jax.ShapeDtypeStruct((8, D), F32)],
        scratch_shapes=[pltpu.VMEM((tm, D), F32)],
        compiler_params=_cp(("arbitrary", "arbitrary"), VMEM_WIDE_MB),
    )(dh2, a, w1, w2, h1, nfw)


def _ffn_bwd_weights(ft, a, da, dh2t, tm):
    rows = a.shape[0]
    steps = rows // tm
    pair = 2 * FF_TILE

    def body(ft_ref, a_ref, da_ref, dh2t_ref, dw1_ref, dw2_ref, dw2t):
        i = pl.program_id(1)

        @pl.when(i == 0)
        def _():
            dw1_ref[...] = jnp.zeros_like(dw1_ref)
            dw2t[...] = jnp.zeros_like(dw2t)

        z = jnp.square(jnp.maximum(a_ref[...].astype(F32), 0.0))
        dw1 = _mm(ft_ref[...], da_ref[...])
        for core in range(2):
            dw1_ref[core] += dw1[:, core * FF_TILE:(core + 1) * FF_TILE]
        dw2t[...] += _mm(dh2t_ref[...], z)

        @pl.when(i == steps - 1)
        def _():
            for core in range(2):
                dw2_ref[core] = dw2t[:, core * FF_TILE:(core + 1) * FF_TILE].T

    return pl.pallas_call(
        body, name="ffn_bwd_weights", grid=(N_DEV // 2, steps),
        in_specs=[pl.BlockSpec((D, tm), lambda j, i: (0, i)), pl.BlockSpec((tm, pair), lambda j, i: (i, j)),
                  pl.BlockSpec((tm, pair), lambda j, i: (i, j)), pl.BlockSpec((D, tm), lambda j, i: (0, i))],
        out_specs=[pl.BlockSpec((2, None, D, FF_TILE), lambda j, i: (0, j, 0, 0)),
                   pl.BlockSpec((2, None, FF_TILE, D), lambda j, i: (0, j, 0, 0))],
        out_shape=[jax.ShapeDtypeStruct((2, 4, D, FF_TILE), F32), jax.ShapeDtypeStruct((2, 4, FF_TILE, D), F32)],
        scratch_shapes=[pltpu.VMEM((D, pair), F32)],
        compiler_params=_cp(("arbitrary", "arbitrary"), VMEM_WIDE_MB),
    )(ft, a, da, dh2t)


def _out_proj_bwd(dh1, og, osw, wout, tm, partials):
    rows = dh1.shape[0]
    steps = rows // tm
    ns = len(partials)

    def body(dh1_ref, og_ref, os_ref, w_ref, *rest):
        part_refs, rest = rest[:ns], rest[ns:]
        dog_ref, dos_ref, dw_ref = rest[:3]
        land_refs, (send_sems, recv_sems) = rest[3:3 + ns], rest[3 + ns:]
        i = pl.program_id(0)
        start, finish = _sibling_schedule(part_refs, land_refs, send_sems, recv_sems)

        @pl.when(i == 0)
        def _():
            dw_ref[...] = jnp.zeros_like(dw_ref)
            start()

        pl.when(i == steps - 1)(finish)

        dh1 = dh1_ref[...].astype(MXU_DTYPE)
        dog_ref[...] = _mm_nt(dh1, w_ref[0:512, :])
        dos_ref[...] = _mm_nt(dh1, w_ref[512:1024, :])
        for half, ref in enumerate((og_ref, os_ref)):
            dw = _mm_tn(ref[...], dh1)
            for blk in range(4):
                shard = half * 4 + blk
                dw_ref[shard % 2, shard // 2] += dw[blk * 128:(blk + 1) * 128, :]

    row = lambda w: pl.BlockSpec((tm, w), lambda i: (i, 0))
    outs = pl.pallas_call(
        body, name="out_proj_bwd", grid=(steps,),
        in_specs=[row(D), row(512), row(512), pl.BlockSpec((D, D), lambda i: (0, 0))] + [ANY] * ns,
        out_specs=[row(512), row(512), pl.BlockSpec((2, 4, 128, D), lambda i: (0, 0, 0, 0))] + [ANY] * ns,
        out_shape=[jax.ShapeDtypeStruct((rows, 512), F32), jax.ShapeDtypeStruct((rows, 512), F32),
                   jax.ShapeDtypeStruct((2, 4, 128, D), F32)] + _sibling_shapes(partials),
        scratch_shapes=_sibling_sems(ns),
        compiler_params=_cp(("arbitrary",), VMEM_TILE_MB),
    )(dh1, og, osw, wout, *partials)
    return outs[0], outs[1], outs[2], outs[3:]


def _swa_bwd(qr, kr, vr, osw, dos, sinks, jobs):
    rows = qr.shape[0]
    nblk = rows // SWA_BLOCK
    group = _swa_group(nblk)
    steps = nblk // group
    ns = jobs.n

    def body(q_ref, k0, kp, kc, v0, vp, vc, o_ref, do_ref, sink_ref, *rest):
        dq_ref, dk_ref, dv_ref, dsink_ref = rest[ns:ns + 4]
        start, finish = jobs.bind(rest[:ns], rest[ns + 4:2 * ns + 4], rest[2 * ns + 4:])
        step = pl.program_id(0)

        @pl.when(step == 0)
        def _():
            dk_ref[...] = jnp.zeros_like(dk_ref)
            dv_ref[...] = jnp.zeros_like(dv_ref)
            dsink_ref[...] = jnp.zeros_like(dsink_ref)
            start()

        pl.when(step == steps - 1)(finish)
        for g in range(group):
            block(step * group + g, g, q_ref, k0, kp, kc, v0, vp, vc, o_ref, do_ref, sink_ref,
                  dq_ref, dk_ref, dv_ref, dsink_ref)

    def block(n, g, q_ref, k0, kp, kc, v0, vp, vc, o_ref, do_ref, sink_ref, dq_ref, dk_ref, dv_ref, dsink_ref):
        rs = slice(g * SWA_BLOCK, (g + 1) * SWA_BLOCK)
        kall, vall = _swa_keys(k0, kp, kc, g), _swa_keys(v0, vp, vc, g)
        mask = _swa_mask(n)
        heads = range(SWA_HEADS)
        hs = [slice(h * SWA_HD, (h + 1) * SWA_HD) for h in heads]
        kv = [slice((h // SWA_GROUP) * SWA_HD, (h // SWA_GROUP + 1) * SWA_HD) for h in heads]
        sink = [sink_ref[0, h] for h in heads]
        qh = [q_ref[rs, hs[h]] for h in heads]
        doh = [do_ref[rs, hs[h]] for h in heads]
        s = [jnp.where(mask, _mm_nt(qh[h], kall[:, kv[h]]), NEG) for h in heads]
        dp = [_mm_nt(doh[h], vall[:, kv[h]]) for h in heads]
        delta = [jnp.sum(doh[h] * o_ref[rs, hs[h]].astype(F32), axis=-1, keepdims=True) for h in heads]
        m = [jnp.maximum(jnp.max(s[h], axis=-1, keepdims=True), sink[h]) for h in heads]
        e = [jnp.exp(s[h] - m[h]) for h in heads]
        inv = [1.0 / (jnp.sum(e[h], axis=-1, keepdims=True) + jnp.exp(sink[h] - m[h])) for h in heads]
        p = [e[h] * inv[h] for h in heads]
        ds = [p[h] * (dp[h] - delta[h]) for h in heads]
        dq = [_mm(ds[h], kall[:, kv[h]]) for h in heads]
        dkh = [_mm_tn(ds[h], qh[h]) for h in heads]
        dvh = [_mm_tn(p[h], doh[h]) for h in heads]
        for h in heads:
            dsink = -jnp.sum(jnp.exp(sink[h] - m[h]) * inv[h] * delta[h], axis=0, keepdims=True)
            dsink_ref[h:h + 1, :] += jnp.broadcast_to(dsink, (1, 128))
        dq_ref[rs, :] = jnp.concatenate(dq, axis=1).astype(ACT_DTYPE)
        group_sum = lambda parts, kvh: sum(parts[kvh * SWA_GROUP + 1:(kvh + 1) * SWA_GROUP], parts[kvh * SWA_GROUP])
        dk_all = jnp.concatenate([group_sum(dkh, kvh) for kvh in range(SWA_KV)], axis=1)
        dv_all = jnp.concatenate([group_sum(dvh, kvh) for kvh in range(SWA_KV)], axis=1)
        prev0 = pl.multiple_of(jnp.maximum(n - 1, 0) * SWA_BLOCK, SWA_BLOCK)
        cur0 = pl.multiple_of(n * SWA_BLOCK, SWA_BLOCK)
        for ref, val in ((dk_ref, dk_all), (dv_ref, dv_all)):
            ref[0:SWA_BLOCK, :] += val[0:SWA_BLOCK]
            ref[pl.ds(prev0, SWA_BLOCK), :] += val[SWA_BLOCK:2 * SWA_BLOCK]
            ref[pl.ds(cur0, SWA_BLOCK), :] += val[2 * SWA_BLOCK:]

    blk, first, prev = _swa_specs(group)
    whole = pl.BlockSpec((rows, 128), lambda n: (0, 0))
    outs = pl.pallas_call(
        body, name="swa_bwd", grid=(steps,),
        in_specs=[blk(512), first, prev, blk(128), first, prev, blk(128), blk(512), blk(512),
                  pl.BlockSpec(memory_space=pltpu.SMEM)] + [ANY] * ns,
        out_specs=[blk(512), whole, whole, pl.BlockSpec((8, 128), lambda n: (0, 0))] + [ANY] * ns,
        out_shape=[jax.ShapeDtypeStruct((rows, 512), ACT_DTYPE), jax.ShapeDtypeStruct((rows, 128), F32),
                   jax.ShapeDtypeStruct((rows, 128), F32), jax.ShapeDtypeStruct((8, 128), F32)] + jobs.out_shapes,
        scratch_shapes=jobs.sems,
        compiler_params=_cp(("arbitrary",), VMEM_TILE_MB),
    )(qr, kr, kr, kr, vr, vr, vr, osw, dos, sinks, *jobs.inputs)
    return outs[0], outs[1], outs[2], outs[3], jobs.split(outs[4:])


def _gla_bwd(proj, decay, dgate, oraw, states, dog, wg_p, gnw, jobs):
    rows = proj.shape[0]
    nc = rows // GLA_CHUNK
    group = _gla_group(nc, 5)
    steps, nrows = nc // group, group * GLA_CHUNK
    ns = jobs.n

    def body(q_ref, k_ref, v_ref, r_ref, lr_ref, b_ref, dgate_ref, oraw_ref, st_ref, dog_ref, wg_ref, gnw_ref, *rest):
        dq_ref, dk_ref, dv_ref, dr_ref, dlr_ref, dwg_ref, dbg_ref, dgnw_ref = rest[ns:ns + 8]
        dstate, db_scr, mask = rest[2 * ns + 8:2 * ns + 11]
        start, finish = jobs.bind(rest[:ns], rest[ns + 8:2 * ns + 8], rest[2 * ns + 11:])
        t = pl.program_id(0)

        @pl.when(t == 0)
        def _():
            mask[...] = _chunk_masks(nrows)[1].astype(jnp.bfloat16)
            dstate[...] = jnp.zeros_like(dstate)
            dwg_ref[...] = jnp.zeros_like(dwg_ref)
            dbg_ref[...] = jnp.zeros_like(dbg_ref)
            dgnw_ref[...] = jnp.zeros_like(dgnw_ref)
            start()

        pl.when(t == steps - 1)(finish)

        lr, wg = lr_ref[...], wg_ref[...]
        b = b_ref[...]
        eb, enb = jnp.exp(b), jnp.exp(-b)
        scale = GLA_DK ** -0.5
        gq = q_ref[...] * scale * eb
        gk = k_ref[...] * enb
        v = v_ref[...]
        gnw_v = gnw_ref[...]
        tril = _tril64()
        is_last = lax.broadcasted_iota(jnp.int32, (GLA_CHUNK, 1), 0) == GLA_CHUNK - 1
        dgnw = jnp.zeros((1, GLA_DV), F32)
        pairs = [(h, gi) for h in range(GLA_HEADS) for gi in range(group)]
        rs = {gi: slice(gi * GLA_CHUNK, (gi + 1) * GLA_CHUNK) for gi in range(group)}
        s64 = {h: slice(h * GLA_DK, (h + 1) * GLA_DK) for h in range(GLA_HEADS)}
        s128 = {h: slice(h * GLA_DV, (h + 1) * GLA_DV) for h in range(GLA_HEADS)}
        qh = {(h, gi): gq[rs[gi], s64[h]] for h, gi in pairs}
        kh = {(h, gi): gk[rs[gi], s64[h]] for h, gi in pairs}
        vh = {(h, gi): v[rs[gi], s128[h]] for h, gi in pairs}
        ebl = {(h, gi): eb[(gi + 1) * GLA_CHUNK - 1:(gi + 1) * GLA_CHUNK, s64[h]] for h, gi in pairs}
        kl = {pr: kh[pr] * ebl[pr] for pr in pairs}
        st = {(h, gi): st_ref[gi, h] for h, gi in pairs}
        do = {}
        for h, gi in pairs:
            o, rh, dout = oraw_ref[rs[gi], s128[h]], r_ref[rs[gi], s128[h]], dog_ref[rs[gi], s128[h]]
            rstd = lax.rsqrt(jnp.mean(o * o, axis=-1, keepdims=True) + EPS)
            on = o * rstd
            sg = _sigmoid(rh)
            dr_ref[rs[gi], s128[h]] = (dout * (on * gnw_v) * (sg * (1.0 + rh * (1.0 - sg)))).astype(ACT_DTYPE)
            dy = dout * (rh * sg)
            dgnw = dgnw + jnp.sum(dy * on, axis=0, keepdims=True)
            don = dy * gnw_v
            do[h, gi] = rstd * (don - on * jnp.mean(don * on, axis=-1, keepdims=True))
        a = {pr: jnp.where(tril, _mm_nt(qh[pr], kh[pr]), 0.0) for pr in pairs}
        da = {pr: jnp.where(tril, _mm_nt(do[pr], vh[pr]), 0.0) for pr in pairs}
        dinc = {pr: _mm_tn(do[pr], qh[pr]) for pr in pairs}
        dgq = {pr: _mm(da[pr], kh[pr]) + _mm(do[pr], st[pr]) for pr in pairs}
        dgk = {pr: _mm_tn(da[pr], qh[pr]) for pr in pairs}
        dv_a = {pr: _mm_tn(a[pr], do[pr]) for pr in pairs}
        dsp = {}
        for h in range(GLA_HEADS):
            cur = dstate[h]
            for gi in reversed(range(group)):
                dsp[h, gi] = cur
                cur = cur * ebl[h, gi] + dinc[h, gi]
            dstate[h] = cur
        for h, gi in pairs:
            pr = (h, gi)
            dkl = _mm(vh[pr], dsp[pr])
            dv_ref[rs[gi], s128[h]] = (dv_a[pr] + _mm_nt(kl[pr], dsp[pr])).astype(ACT_DTYPE)
            debl = jnp.sum(dsp[pr] * st[pr], axis=0, keepdims=True)
            dq_ref[rs[gi], s64[h]] = (dgq[pr] * (scale * eb[rs[gi], s64[h]])).astype(ACT_DTYPE)
            dk_ref[rs[gi], s64[h]] = ((dgk[pr] + dkl * ebl[pr]) * enb[rs[gi], s64[h]]).astype(ACT_DTYPE)
            last = debl * ebl[pr] + jnp.sum(dkl * kl[pr], axis=0, keepdims=True)
            db_scr[rs[gi], s64[h]] = (dgq[pr] * qh[pr] - dgk[pr] * kh[pr] - dkl * kl[pr]
                                      + jnp.where(is_last, last, 0.0))
        dzg = _masked_sums(mask[...], db_scr[...]) * dgate_ref[...]
        dlr_ref[...] = _mm_nt(dzg, wg).astype(ACT_DTYPE)
        dwg_ref[...] += _mm_tn(lr, dzg)
        dbg_ref[...] += jnp.broadcast_to(jnp.sum(dzg, axis=0, keepdims=True), dbg_ref.shape)
        dgnw_ref[...] += jnp.broadcast_to(dgnw, dgnw_ref.shape)

    nb = lambda w, col: pl.BlockSpec((nrows, w), lambda t: (steps - 1 - t, col // w))
    const = lambda shape: pl.BlockSpec(shape, lambda t: (0,) * len(shape))
    outs = pl.pallas_call(
        body, name="gla_bwd", grid=(steps,),
        in_specs=[nb(256, C_GQ), nb(256, C_GK), nb(512, C_GV), nb(512, C_GR), nb(128, C_LR), nb(256, 0), nb(256, 0),
                  nb(512, 0),
                  pl.BlockSpec((group, GLA_HEADS, GLA_DV, GLA_DK), lambda t: (steps - 1 - t, 0, 0, 0)), nb(512, 0),
                  const((128, 256)), const((1, 128))] + [ANY] * ns,
        out_specs=[nb(256, 0), nb(256, 0), nb(512, 0), nb(512, 0), nb(128, 0),
                   const((128, 256)), const((8, 256)), const((8, 128))] + [ANY] * ns,
        out_shape=[jax.ShapeDtypeStruct((rows, 256), ACT_DTYPE), jax.ShapeDtypeStruct((rows, 256), ACT_DTYPE),
                   jax.ShapeDtypeStruct((rows, 512), ACT_DTYPE), jax.ShapeDtypeStruct((rows, 512), ACT_DTYPE),
                   jax.ShapeDtypeStruct((rows, 128), ACT_DTYPE), jax.ShapeDtypeStruct((128, 256), F32),
                   jax.ShapeDtypeStruct((8, 256), F32), jax.ShapeDtypeStruct((8, 128), F32)] + jobs.out_shapes,
        scratch_shapes=[pltpu.VMEM((GLA_HEADS, GLA_DV, GLA_DK), F32), pltpu.VMEM((nrows, 256), F32),
                        pltpu.VMEM((nrows, nrows), jnp.bfloat16)] + jobs.sems,
        compiler_params=_cp(("arbitrary",)),
    )(proj, proj, proj, proj, proj, decay, dgate, oraw, states, dog, wg_p, gnw, *jobs.inputs)
    return outs[:8], jobs.split(outs[8:])


def _in_proj_bwd(x, lead, dh1, nw, win_p, dgv, dgr, dsq, dgq, dgk, dsk, dsv, dlr, angles, tm):
    seq = x.shape[0]
    rows = LEAD + seq
    nb = tm // LEAD
    steps = rows // tm

    def first_copy(scr, gx_ref, sem):
        return pltpu.make_async_copy(scr.at[pl.ds(LEAD, tm - LEAD)], gx_ref.at[pl.ds(0, tm - LEAD)], sem)

    def tile_copy(scr, gx_ref, sem, step):
        start = pl.multiple_of(jnp.maximum(step * tm - LEAD, 0), LEAD)
        return pltpu.make_async_copy(scr, gx_ref.at[pl.ds(start, tm)], sem)

    def body(*refs):
        x_refs, refs = refs[:nb], refs[nb:]
        (lead_ref, dh1_ref, nw_ref, w_ref, dgv_ref, dgr_ref, dsq_ref, dgq_ref, dgk_ref, dsk_ref, dsv_ref, dlr_ref,
         cs_ref, gx_ref, dlead_ref, dproj_ref, ut_ref, gnm_ref, scr, sem) = refs
        i = pl.program_id(0)

        @pl.when(i == 0)
        def _():
            gnm_ref[...] = jnp.zeros_like(gnm_ref)

        cos, sa, sb = _rope_tables(cs_ref[...])
        dsq_v = (_unrope(dsq_ref[...].astype(F32), cos, sa, sb) * (SWA_HD ** -0.5)).astype(MXU_DTYPE)
        dsk_v = _unrope(dsk_ref[...], cos, sa, sb).astype(MXU_DTYPE)
        dproj = jnp.concatenate(
            [dgv_ref[...].astype(MXU_DTYPE), dgr_ref[...].astype(MXU_DTYPE), dgq_ref[...].astype(MXU_DTYPE),
             dgk_ref[...].astype(MXU_DTYPE), dlr_ref[...].astype(MXU_DTYPE), dsq_v, dsk_v,
             dsv_ref[...].astype(MXU_DTYPE)],
            axis=1)
        dproj_ref[...] = dproj
        h = _h_tile(i, lead_ref, x_refs)
        rstd = lax.rsqrt(jnp.mean(h * h, axis=-1, keepdims=True) + EPS)
        hn = h * rstd
        nw_v = nw_ref[...]
        ut_ref[...] = (hn * nw_v).T.astype(ACT_DTYPE)
        du = _mm_nt(dproj, w_ref[...])
        gnm_ref[...] += jnp.broadcast_to(jnp.sum(du * hn, axis=0, keepdims=True), gnm_ref.shape)
        dun = du * nw_v
        dh0 = dh1_ref[...] + rstd * (dun - hn * jnp.mean(dun * hn, axis=-1, keepdims=True))

        if tm > LEAD:
            pl.when(i == 1)(lambda: first_copy(scr, gx_ref, sem).wait())
        pl.when(i > 1)(lambda: tile_copy(scr, gx_ref, sem, i).wait())
        scr[...] = dh0

        @pl.when(i == 0)
        def _():
            dlead_ref[...] = dh0[0:LEAD]
            if tm > LEAD:
                first_copy(scr, gx_ref, sem).start()
                if steps == 1:
                    first_copy(scr, gx_ref, sem).wait()

        @pl.when(i > 0)
        def _():
            tile_copy(scr, gx_ref, sem, i).start()

        if steps > 1:
            pl.when(i == steps - 1)(lambda: tile_copy(scr, gx_ref, sem, i).wait())

    row = lambda w: pl.BlockSpec((tm, w), lambda i: (i, 0))
    const = lambda shape: pl.BlockSpec(shape, lambda i: (0,) * len(shape))
    return pl.pallas_call(
        body, name="in_proj_bwd", grid=(steps,),
        in_specs=_token_specs(tm) + [const((LEAD, D)), row(D), const((1, D)), const((D, DINP)),
                                     row(512), row(512), row(512), row(256), row(256), row(128), row(128), row(128),
                                     row(ROPE_DIM)],
        out_specs=[ANY, const((LEAD, D)), row(DINP), pl.BlockSpec((D, tm), lambda i: (0, i)), const((8, D))],
        out_shape=[jax.ShapeDtypeStruct((seq, D), F32), jax.ShapeDtypeStruct((LEAD, D), F32),
                   jax.ShapeDtypeStruct((rows, DINP), ACT_DTYPE), jax.ShapeDtypeStruct((D, rows), ACT_DTYPE),
                   jax.ShapeDtypeStruct((8, D), F32)],
        scratch_shapes=[pltpu.VMEM((tm, D), F32), pltpu.SemaphoreType.DMA],
        compiler_params=_cp(("arbitrary",), VMEM_WIDE_MB),
    )(*([x] * nb), lead, dh1, nw, win_p, dgv, dgr, dsq, dgq, dgk, dsk, dsv, dlr, angles)


def _win_runs():
    groups = [(O_GQ, C_GQ), (O_GK, C_GK), (O_GV, C_GV), (O_GR, C_GR), (O_LR, C_LR), (O_SQ, C_SQ), (O_SK, C_SK),
              (O_SV, C_SV)]
    per = DIN // N_DEV
    runs = []
    for (o0, o1), c0 in groups:
        o = o0
        while o < o1:
            d = o // per
            end = min(o1, (d + 1) * per)
            runs.append((d, o - d * per, c0 + o - o0, end - o))
            o = end
    return runs


def _win_padded(g_in):
    tr = 128

    def body(g_ref, o_ref):
        o_ref[...] = jnp.zeros_like(o_ref)
        for d, s, c, w in _win_runs():
            o_ref[:, c:c + w] = g_ref[d, :, s:s + w]

    return pl.pallas_call(
        body, name="w_in_layout", grid=(D // tr,),
        in_specs=[pl.BlockSpec((N_DEV, tr, DIN // N_DEV), lambda i: (0, i, 0))],
        out_specs=pl.BlockSpec((tr, DINP), lambda i: (i, 0)),
        out_shape=jax.ShapeDtypeStruct((D, DINP), g_in.dtype),
        compiler_params=_cp(("arbitrary",)),
    )(g_in)


def _in_proj_bwd_weights(ut, dproj, tm, small):
    rows = dproj.shape[0]
    steps = rows // tm
    per = DIN // N_DEV

    def body(ut_ref, dp_ref, *rest):
        small_refs, (mine_ref, theirs_ref, total_ref, acc, stage, local_sems, send_sems, recv_sems) = rest[:9], rest[9:17]
        i = pl.program_id(0)
        start, finish = _small_sum_schedule(small_refs, total_ref, *rest[17:])
        x, y, c = _mesh_pos()

        @pl.when(i == 0)
        def _():
            acc[...] = jnp.zeros_like(acc)
            start()

        acc[...] += _mm(ut_ref[...], dp_ref[...])
        pl.when(i == steps - 1)(finish)

        def keep(slot, chip):
            return pltpu.make_async_copy(stage.at[slot], mine_ref.at[chip], local_sems.at[slot])

        def send(slot, chip):
            return pltpu.make_async_remote_copy(
                src_ref=stage.at[slot], dst_ref=theirs_ref.at[chip], send_sem=send_sems.at[slot],
                recv_sem=recv_sems.at[chip], device_id=(x, y, 1 - c), device_id_type=MESH)

        def drained(d):
            pl.when(c == d % 2)(keep(d % 2, d // 2).wait)
            pl.when(c != d % 2)(send(d % 2, d // 2).wait_send)

        @pl.when(i == steps - 1)
        def _():
            for d in range(N_DEV):
                slot, chip = d % 2, d // 2
                if d >= 2:
                    drained(d - 2)
                for owner, s, col, w in _win_runs():
                    if owner == d:
                        stage[slot, :, s:s + w] = acc[:, col:col + w]
                pl.when(c == slot)(keep(slot, chip).start)
                pl.when(c != slot)(send(slot, chip).start)
            drained(N_DEV - 2)
            drained(N_DEV - 1)
            for chip in range(4):
                send(0, chip).wait_recv()

    half = jax.ShapeDtypeStruct((4, D, per), F32)
    return pl.pallas_call(
        body, name="in_proj_bwd_weights", grid=(steps,),
        in_specs=[pl.BlockSpec((D, tm), lambda i: (0, i)), pl.BlockSpec((tm, DINP), lambda i: (i, 0))] + SMALL_SPECS,
        out_specs=[ANY, ANY, pl.BlockSpec((SMALL_ROWS, D), lambda i: (0, 0))],
        out_shape=[half, half, jax.ShapeDtypeStruct((SMALL_ROWS, D), F32)],
        scratch_shapes=[pltpu.VMEM((D, DINP), F32), pltpu.VMEM((2, D, per), F32), pltpu.SemaphoreType.DMA((2,)),
                        pltpu.SemaphoreType.DMA((2,)), pltpu.SemaphoreType.DMA((4,))] + _small_sum_scratch(),
        compiler_params=_cp(("arbitrary",), VMEM_WIDE_MB),
    )(ut, dproj, *small)


def _adamw(w, g, m, v):
    m = ADAM_B1 * m + (1.0 - ADAM_B1) * g
    v = ADAM_B2 * v + (1.0 - ADAM_B2) * jnp.square(g)
    m_hat = m / (1.0 - ADAM_B1 ** ADAM_STEP)
    v_hat = v / (1.0 - ADAM_B2 ** ADAM_STEP)
    delta = -ADAM_LR * (m_hat / (jnp.sqrt(v_hat) + ADAM_EPS) + ADAM_WD * w)
    return delta, m, v


ADAM_STEPS = 8


def _adamw_shards(items, name, jobs=None):
    jobs = jobs or _Jobs([])
    ns, nw = jobs.n, len(items)

    def body(*rest):
        ins, rest = rest[:5 * nw], rest[5 * nw:]
        job_ins, rest = rest[:ns], rest[ns:]
        outs, rest = rest[:4 * nw], rest[4 * nw:]
        start, finish = jobs.bind(job_ins, rest[:ns], rest[ns:])
        i = pl.program_id(0)
        pl.when(i == 0)(start)
        pl.when(i == ADAM_STEPS - 1)(finish)
        for k in range(nw):
            p_ref, own_ref, w_ref, m_ref, v_ref = ins[5 * k:5 * k + 5]
            g_ref, d_ref, nm_ref, nv_ref = outs[4 * k:4 * k + 4]
            g = ((p_ref[0].astype(F32) + p_ref[1].astype(F32)) + p_ref[2].astype(F32)) + own_ref[...]
            g_ref[...] = g
            d_ref[...], nm_ref[...], nv_ref[...] = _adamw(w_ref[...], g, m_ref[...], v_ref[...])

    in_specs, out_specs, out_shape, operands = [], [], [], []
    for parts, own, w, m, v in items:
        r, cdim = w.shape
        tr = r // ADAM_STEPS
        spec = pl.BlockSpec((tr, cdim), lambda i: (i, 0))
        in_specs += [pl.BlockSpec((3, tr, cdim), lambda i: (0, i, 0)), spec, spec, spec, spec]
        out_specs += [spec] * 4
        out_shape += [jax.ShapeDtypeStruct((r, cdim), F32)] * 4
        operands += [parts, own, w, m, v]
    outs = pl.pallas_call(
        body, name=name, grid=(ADAM_STEPS,),
        in_specs=in_specs + [ANY] * ns, out_specs=out_specs + [ANY] * ns, scratch_shapes=jobs.sems,
        out_shape=out_shape + jobs.out_shapes,
        compiler_params=_cp(("arbitrary",)),
    )(*operands, *jobs.inputs)
    return [outs[4 * k:4 * k + 4] for k in range(nw)], jobs.split(outs[4 * nw:])


def _adamw_small(items):
    n = len(items)

    def body(*refs):
        ins, outs = refs[:4 * n], refs[4 * n:]
        for k in range(n):
            w_ref, g_ref, m_ref, v_ref = ins[4 * k:4 * k + 4]
            d_ref, nm_ref, nv_ref = outs[3 * k:3 * k + 3]
            d_ref[...], nm_ref[...], nv_ref[...] = _adamw(w_ref[...], g_ref[...], m_ref[...], v_ref[...])

    vm = pl.BlockSpec(memory_space=pltpu.VMEM)
    shapes = [jax.ShapeDtypeStruct(w.shape, F32) for w, _, _, _ in items for _ in range(3)]
    outs = pl.pallas_call(body, name="adamw_small", in_specs=[vm] * (4 * n), out_specs=[vm] * (3 * n),
                          out_shape=shapes)(*[t for item in items for t in item])
    return [outs[3 * k:3 * k + 3] for k in range(n)]


def _pair_sums(where, mine, theirs, name):
    _, r, cdim = theirs.shape
    tr = 128 if r % 128 == 0 else r

    def body(where_ref, a_ref, b_ref, own_ref, wire_ref):
        chip = where_ref[1]
        own_ref[...] = a_ref[chip] + b_ref[chip]
        wire_ref[...] = (a_ref[...] + b_ref[...]).astype(WIRE_DTYPE)

    spec = pl.BlockSpec((4, tr, cdim), lambda i, s: (0, i, 0))
    mine_spec = spec if mine.ndim == 3 else pl.BlockSpec((None, 4, tr, cdim), lambda i, s: (s[0], 0, i, 0))
    return pl.pallas_call(
        body, name=name,
        grid_spec=pltpu.PrefetchScalarGridSpec(
            num_scalar_prefetch=1, grid=(r // tr,), in_specs=[mine_spec, spec],
            out_specs=[pl.BlockSpec((tr, cdim), lambda i, s: (i, 0)), spec]),
        out_shape=[jax.ShapeDtypeStruct((r, cdim), F32), jax.ShapeDtypeStruct(theirs.shape, WIRE_DTYPE)],
        compiler_params=_cp(("arbitrary",)))(where, mine, theirs)


def kernel(x, meta_tokens, norm_mix_w, w_in, w_gate_up, b_gate, gla_norm_w, sinks, w_out, norm_ff_w, w_ff1, w_ff2, final_norm_w, loss_target, m_meta_tokens, m_norm_mix_w, m_w_in, m_w_gate_up, m_b_gate, m_gla_norm_w, m_sinks, m_w_out, m_norm_ff_w, m_w_ff1, m_w_ff2, m_final_norm_w, v_meta_tokens, v_norm_mix_w, v_w_in, v_w_gate_up, v_b_gate, v_gla_norm_w, v_sinks, v_w_out, v_norm_ff_w, v_w_ff1, v_w_ff2, v_final_norm_w):
    seq = x.shape[1]
    rows = LEAD + seq
    tm = _row_tile(rows)
    tm_wide = WIDE_ROW_TILE if rows % WIDE_ROW_TILE == 0 else tm
    dev =4 * lax.axis_index("x") + 2 * lax.axis_index("y") + lax.axis_index("c")

    small_shard = jnp.concatenate([meta_tokens, w_gate_up[0], jnp.zeros((N_META, 96), F32)], axis=1)
    g_in, g_small = _all_gather([w_in[0].astype(WIRE_DTYPE), small_shard])
    later_shards = [w_out[0].astype(WIRE_DTYPE), w_ff1[0].astype(WIRE_DTYPE), w_ff2[0].astype(WIRE_DTYPE)]
    win_p = _win_padded(g_in)
    meta_full = jnp.transpose(g_small[:, :, 0:128], (1, 0, 2)).reshape(N_META, D)
    wg_full = jnp.transpose(g_small[:, :, 128:160], (1, 0, 2)).reshape(GLA_RANK, GLA_HEADS * GLA_DK)
    wg_p = jnp.concatenate([wg_full, jnp.zeros((128 - GLA_RANK, 256), F32)], axis=0)

    lead = jnp.concatenate([jnp.zeros((META0, D), F32), meta_full], axis=0)
    angles = _rope_angles(rows)
    proj, qr, kr, vr, (g_w1,) = _in_proj(x[0], lead, norm_mix_w, win_p, angles, tm, later_shards[1:2])
    oraw, og, states, decay, dgate, (g_out,) = _gla_fwd(proj, wg_p, b_gate, gla_norm_w, later_shards[0:1])
    osw, (g_w2,) = _swa_fwd(qr, kr, vr, sinks, later_shards[2:3])
    wout_full = g_out.reshape(D, D)
    w2_full = g_w2.reshape(D_FF, D)
    w1_full = jnp.transpose(g_w1, (1, 0, 2)).reshape(D, D_FF)
    h1, f, ft = _out_proj(x[0], lead, og, osw, wout_full, norm_ff_w, tm)
    a, dh2, dh2t, loss_p, gfn_p = _ffn_fwd(f, h1, w1_full, w2_full, loss_target[0], final_norm_w.reshape(1, D), tm)

    da, dh1, gnf_p = _ffn_bwd_act(dh2, a, w1_full, w2_full, h1, norm_ff_w, tm)
    dw1, dw2 = _ffn_bwd_weights(ft, a, da, dh2t, tm_wide)
    where = jnp.stack([lax.axis_index("c"), 2 * lax.axis_index("x") + lax.axis_index("y")]).astype(jnp.int32)
    dog, dos, dwout, theirs_ffn = _out_proj_bwd(dh1, og, osw, wout_full, tm, [dw1, dw2])
    pairs_ffn = [_pair_sums(where, p, q, "reduce_pair_%d" % (2 + k))
                 for k, (p, q) in enumerate(zip([dw1, dw2], theirs_ffn))]
    sums_ffn, wires_ffn = [p[0] for p in pairs_ffn], [p[1] for p in pairs_ffn]
    dsq, dsk, dsv, dsink_p, (parts_ffn, (theirs_wout,)) = _swa_bwd(
        qr, kr, vr, osw, dos, sinks, _Jobs([("chips", wires_ffn), ("sibling", [dwout])]))
    sum_wout, wire_wout = _pair_sums(where, dwout, theirs_wout, "reduce_pair_1")
    (dgq, dgk, dgv, dgr, dlr, dwg_p, dbg_p, dgnw_p), ((parts_wout,),) = _gla_bwd(
        proj, decay, dgate, oraw, states, dog, wg_p, gla_norm_w, _Jobs([("chips", [wire_wout])]))
    grad_x, dlead, dproj, ut, gnm_p = _in_proj_bwd(x[0], lead, dh1, norm_mix_w, win_p, dgv, dgr, dsq, dgq, dgk, dsk,
                                                   dsv, dlr, angles, tm)
    grad_x = grad_x[None]
    dwin_mine, dwin_theirs, total = _in_proj_bwd_weights(
        ut, dproj, tm_wide, [dlead, dwg_p, gnm_p, gnf_p, gfn_p, dbg_p, dgnw_p, loss_p, dsink_p])
    sum_win, sum_win_wire = _pair_sums(where, dwin_mine, dwin_theirs, "reduce_pair_0")

    g_meta = lax.dynamic_slice(total, (R_META, dev * 128), (N_META, 128))
    g_wg = lax.dynamic_slice(total, (R_WG, dev * 32), (GLA_RANK, 32))
    g_norm_mix, g_norm_ff = total[R_NORM_MIX:R_NORM_MIX + 1], total[R_NORM_FF:R_NORM_FF + 1]
    g_final_norm = total[R_FINAL:R_FINAL + 1]
    g_b_gate, g_gla_norm = total[R_B_GATE:R_B_GATE + 1, 0:256], total[R_GLA_NORM:R_GLA_NORM + 1, 0:128]
    g_sinks = total[R_SINKS:R_SINKS + SWA_HEADS, 0].reshape(1, SWA_HEADS)
    loss = total[R_LOSS, 0]

    ((g_wout, d_wout, nm_wout, nv_wout), (g_w1s, d_w1, nm_w1, nv_w1), (g_w2s, d_w2, nm_w2, nv_w2)), ((parts_win,),) = \
        _adamw_shards([(parts_wout, sum_wout, w_out[0], m_w_out[0], v_w_out[0]),
                       (parts_ffn[0], sums_ffn[0], w_ff1[0], m_w_ff1[0], v_w_ff1[0]),
                       (parts_ffn[1], sums_ffn[1], w_ff2[0], m_w_ff2[0], v_w_ff2[0])],
                      "adamw_w_out_ff", _Jobs([("chips", [sum_win_wire])]))
    ((g_win, d_win, nm_win, nv_win),), _ = _adamw_shards(
        [(parts_win, sum_win, w_in[0], m_w_in[0], v_w_in[0])], "adamw_w_in")

    names = ["meta", "wg", "norm_mix", "b_gate", "gla_norm", "sinks", "norm_ff", "final_norm"]
    ws = [meta_tokens, w_gate_up, norm_mix_w, b_gate, gla_norm_w, sinks, norm_ff_w, final_norm_w]
    gs = [g_meta, g_wg, g_norm_mix, g_b_gate, g_gla_norm, g_sinks, g_norm_ff, g_final_norm]
    ms = [m_meta_tokens, m_w_gate_up, m_norm_mix_w, m_b_gate, m_gla_norm_w, m_sinks, m_norm_ff_w, m_final_norm_w]
    vs = [v_meta_tokens, v_w_gate_up, v_norm_mix_w, v_b_gate, v_gla_norm_w, v_sinks, v_norm_ff_w, v_final_norm_w]
    flat = lambda t: t.reshape(-1, t.shape[-1])
    small_out = _adamw_small([(flat(w), flat(g), flat(m), flat(v)) for w, g, m, v in zip(ws, gs, ms, vs)])
    d_small = {n: small_out[k][0].reshape(ws[k].shape) for k, n in enumerate(names)}
    nm_small = {n: small_out[k][1].reshape(ws[k].shape) for k, n in enumerate(names)}
    nv_small = {n: small_out[k][2].reshape(ws[k].shape) for k, n in enumerate(names)}
    g_small_d = {n: g.reshape(ws[k].shape) for k, (n, g) in enumerate(zip(names, gs))}

    def ordered(big, small_d):
        win_v, wout_v, w1_v, w2_v = big
        return (small_d["meta"], small_d["norm_mix"], win_v[None], small_d["wg"], small_d["b_gate"],
                small_d["gla_norm"], small_d["sinks"], wout_v[None], small_d["norm_ff"], w1_v[None], w2_v[None],
                small_d["final_norm"])

    return (loss, grad_x,
            *ordered((g_win, g_wout, g_w1s, g_w2s), g_small_d),
            *ordered((d_win, d_wout, d_w1, d_w2), d_small),
            *ordered((nm_win, nm_wout, nm_w1, nm_w2), nm_small),
            *ordered((nv_win, nv_wout, nv_w1, nv_w2), nv_small))
```

```python
import functools

import jax
import jax.numpy as jnp
from jax import lax
from jax.experimental import pallas as pl
from jax.experimental.pallas import tpu as pltpu

F32 = jnp.float32
MXU_DTYPE = jnp.bfloat16
ACT_DTYPE = jnp.bfloat16
WIRE_DTYPE = jnp.bfloat16

D = 1024
N_META = 16
LEAD = 128
META0 = LEAD - N_META
EPS = 1e-5
GLA_HEADS, GLA_DK, GLA_DV, GLA_RANK, GLA_CHUNK = 4, 64, 128, 16, 64
GLA_TAU = 16.0
SWA_HEADS, SWA_KV, SWA_GROUP, SWA_HD, SWA_BLOCK = 8, 2, 4, 64, 128
ROPE_DIM, ROPE_THETA = 16, 500000.0
D_FF = 4096
N_DEV = 8
FF_TILE = D_FF // N_DEV
FF_WIDE = 2048
NEG = -1e30

C_GV, C_GR, C_GQ, C_GK, C_LR, C_SQ, C_SK, C_SV = 0, 512, 1024, 1280, 1536, 1664, 2176, 2304
DGLA = 1664
DINP = 2432
DIN = 2320
O_GQ, O_GK, O_GV, O_GR, O_LR, O_SQ, O_SK, O_SV = (0, 256), (256, 512), (512, 1024), (1024, 1536), (1536, 1552), (1552, 2064), (2064, 2192), (2192, 2320)

ADAM_LR, ADAM_B1, ADAM_B2, ADAM_EPS, ADAM_WD, ADAM_STEP = 0.001, 0.9, 0.999, 1e-08, 0.01, 10

MESH = pl.DeviceIdType.MESH
ANY = pl.BlockSpec(memory_space=pl.ANY)
VMEM_TILE_MB, VMEM_WIDE_MB = 48, 56


def _cp(sem=None, vmem_mb=None):
    kw = {}
    if sem is not None:
        kw["dimension_semantics"] = sem
    if vmem_mb is not None:
        kw["vmem_limit_bytes"] = vmem_mb << 20
    return pltpu.CompilerParams(**kw)


def _mm(a, b):
    return jnp.dot(a.astype(MXU_DTYPE), b.astype(MXU_DTYPE), preferred_element_type=F32)


def _mm_nt(a, b):
    return lax.dot_general(a.astype(MXU_DTYPE), b.astype(MXU_DTYPE), (((1,), (1,)), ((), ())),
                           preferred_element_type=F32)


def _mm_tn(a, b):
    return lax.dot_general(a.astype(MXU_DTYPE), b.astype(MXU_DTYPE), (((0,), (0,)), ((), ())),
                           preferred_element_type=F32)


def _masked_sums(mask, t):
    m = mask.astype(jnp.bfloat16)
    hi = t.astype(jnp.bfloat16)
    rest = t - hi.astype(F32)
    mid = rest.astype(jnp.bfloat16)
    low = (rest - mid.astype(F32)).astype(jnp.bfloat16)
    dot = lambda part: jnp.dot(m, part, preferred_element_type=F32)
    return dot(hi) + (dot(mid) + dot(low))


def _logsigmoid(z):
    return jnp.minimum(z, 0.0) - jnp.log(1.0 + jnp.exp(-jnp.abs(z)))


def _sigmoid(z):
    return 1.0 / (1.0 + jnp.exp(-z))


ROW_TILE, WIDE_ROW_TILE = 640, 1664


def _row_tile(rows, want=ROW_TILE):
    return want if rows % want == 0 else LEAD


def _mesh_pos():
    return lax.axis_index("x"), lax.axis_index("y"), lax.axis_index("c")


def _all_gather(shards):
    n = len(shards)

    def body(*refs):
        start, forward, finish = _gather_schedule(refs[:n], refs[n:2 * n], *refs[2 * n:])
        start()
        for j in range(3):
            forward(j)
        finish()

    gathered = pl.pallas_call(
        body, name="all_gather_weights",
        out_shape=_gathered_shapes(shards), in_specs=[ANY] * n, out_specs=[ANY] * n,
        scratch_shapes=_gather_sems(n),
    )(*shards)
    return _with_own_block(gathered, shards)


def _gathered_shapes(shards):
    return [jax.ShapeDtypeStruct((N_DEV,) + s.shape, s.dtype) for s in shards]


def _gather_sems(n):
    return [pltpu.SemaphoreType.DMA((7 * n,)), pltpu.SemaphoreType.DMA((7 * n,))] if n else []


def _place_gather(step, steps, shard_refs, gathered_refs, sems):
    if not shard_refs:
        return
    start, forward, finish = _gather_schedule(shard_refs, gathered_refs, *sems)
    pl.when(step == 0)(start)
    for j, at in enumerate((steps * 7 // 10, steps * 8 // 10, steps * 9 // 10)):
        pl.when(step == at)(functools.partial(forward, j))
    pl.when(step == steps - 1)(finish)


def _with_own_block(gathered, shards):
    dev = 4 * lax.axis_index("x") + 2 * lax.axis_index("y") + lax.axis_index("c")
    return [lax.dynamic_update_index_in_dim(g, s, dev, 0) for g, s in zip(gathered, shards)]


def _gather_schedule(ins, outs, send_sems, recv_sems):
    n = len(ins)
    x, y, c = _mesh_pos()
    me, sibling = (x, y, c), (x, y, 1 - c)
    chips = [(1 - x, y), (x, 1 - y), (1 - x, 1 - y)]

    def copy(a, k, block, to, src=None):
        dst = outs[a].at[4 * block[0] + 2 * block[1] + block[2]]
        return pltpu.make_async_remote_copy(
            src_ref=dst if src is None else src, dst_ref=dst,
            send_sem=send_sems.at[a * 7 + k], recv_sem=recv_sems.at[a * 7 + k],
            device_id=to, device_id_type=MESH)

    def first(a):
        return [copy(a, 0, me, sibling, src=ins[a])] + [copy(a, 1 + j, me, (*chip, c), src=ins[a])
                                                        for j, chip in enumerate(chips)]

    def start():
        for a in range(n):
            for cp in first(a):
                cp.start()

    def forward(j):
        for a in range(n):
            copy(a, 1 + j, (*chips[j], c), me).wait_recv()
            copy(a, 4 + j, (*chips[j], c), sibling).start()

    def finish():
        for a in range(n):
            copy(a, 0, sibling, me).wait_recv()
            for j, chip in enumerate(chips):
                copy(a, 4 + j, (*chip, 1 - c), me).wait_recv()
        for a in range(n):
            for cp in first(a) + [copy(a, 4 + j, (*chip, c), sibling) for j, chip in enumerate(chips)]:
                cp.wait_send()

    return start, forward, finish


def _sibling_shapes(gs):
    return [jax.ShapeDtypeStruct(g.shape[1:], g.dtype) for g in gs]


def _sibling_sems(n):
    return [pltpu.SemaphoreType.DMA((n,)), pltpu.SemaphoreType.DMA((n,))]


def _sibling_schedule(ins, land, send_sems, recv_sems):
    x, y, c = _mesh_pos()

    def copies():
        return [pltpu.make_async_remote_copy(
            src_ref=ins[a].at[1 - c], dst_ref=land[a], send_sem=send_sems.at[a], recv_sem=recv_sems.at[a],
            device_id=(x, y, 1 - c), device_id_type=MESH) for a in range(len(ins))]

    def start():
        for cp in copies():
            cp.start()

    def finish():
        for cp in copies():
            cp.wait_recv()
        for cp in copies():
            cp.wait_send()

    return start, finish


def _chips_shapes(ps):
    return [jax.ShapeDtypeStruct((3,) + p.shape[1:], p.dtype) for p in ps]


def _chips_sems(n):
    return [pltpu.SemaphoreType.DMA((3 * n,)), pltpu.SemaphoreType.DMA((3 * n,))]


def _chips_schedule(ins, land, send_sems, recv_sems):
    x, y, c = _mesh_pos()
    chips = [(1 - x, y), (x, 1 - y), (1 - x, 1 - y)]

    def copies():
        return [pltpu.make_async_remote_copy(
            src_ref=ins[a].at[2 * chip[0] + chip[1]], dst_ref=land[a].at[j],
            send_sem=send_sems.at[3 * a + j], recv_sem=recv_sems.at[3 * a + j],
            device_id=(*chip, c), device_id_type=MESH) for a in range(len(ins)) for j, chip in enumerate(chips)]

    def start():
        for cp in copies():
            cp.start()

    def finish():
        for cp in copies():
            cp.wait_recv()
        for cp in copies():
            cp.wait_send()

    return start, finish


class _Jobs:
    def __init__(self, jobs):
        self.jobs = jobs
        self.inputs = [a for _, arrs in jobs for a in arrs]
        self.out_shapes = [s for kind, arrs in jobs
                           for s in (_sibling_shapes(arrs) if kind == "sibling" else _chips_shapes(arrs))]
        self.sems = [s for kind, arrs in jobs
                     for s in (_sibling_sems(len(arrs)) if kind == "sibling" else _chips_sems(len(arrs)))]
        self.n = len(self.inputs)

    def bind(self, in_refs, out_refs, sem_refs):
        starts, finishes, at = [], [], 0
        for k, (kind, arrs) in enumerate(self.jobs):
            schedule = _sibling_schedule if kind == "sibling" else _chips_schedule
            start, finish = schedule(in_refs[at:at + len(arrs)], out_refs[at:at + len(arrs)],
                                     sem_refs[2 * k], sem_refs[2 * k + 1])
            starts.append(start)
            finishes.append(finish)
            at += len(arrs)

        def start_all():
            for f in starts:
                f()

        def finish_all():
            for f in finishes:
                f()

        return start_all, finish_all

    def split(self, outs):
        res, at = [], 0
        for _, arrs in self.jobs:
            res.append(list(outs[at:at + len(arrs)]))
            at += len(arrs)
        return res


R_META, R_WG, R_NORM_MIX, R_NORM_FF, R_FINAL, R_B_GATE, R_GLA_NORM, R_LOSS, R_SINKS, SMALL_ROWS = 0, 16, 32, 33, 34, 35, 36, 37, 40, 48


SMALL_SPECS = [pl.BlockSpec((LEAD, D), lambda i: (0, 0)), pl.BlockSpec((128, 256), lambda i: (0, 0)),
               pl.BlockSpec((8, D), lambda i: (0, 0)), pl.BlockSpec((8, D), lambda i: (0, 0)),
               pl.BlockSpec((8, D), lambda i: (0, 0)), pl.BlockSpec((8, 256), lambda i: (0, 0)),
               pl.BlockSpec((8, 128), lambda i: (0, 0)), pl.BlockSpec((8, 128), lambda i: (0, 0)),
               pl.BlockSpec((8, 128), lambda i: (0, 0))]


def _small_sum_scratch():
    return [pltpu.VMEM((SMALL_ROWS, D), F32), pltpu.VMEM((N_DEV, SMALL_ROWS, D), F32),
            pltpu.SemaphoreType.DMA((7,)), pltpu.SemaphoreType.DMA((7,))]


def _small_sum_schedule(small_refs, out_ref, p_ref, land, send_sems, recv_sems):
    dlead_ref, dwg_ref, gnm_ref, gnf_ref, gfn_ref, dbg_ref, dgnw_ref, loss_ref, dsink_ref = small_refs
    x, y, c = _mesh_pos()
    me = 4 * x + 2 * y + c

    def copies():
        res = []
        for k in range(1, N_DEV):
            bx, by, bc = (k >> 2) & 1, (k >> 1) & 1, k & 1
            peer = (1 - x if bx else x, 1 - y if by else y, 1 - c if bc else c)
            res.append(pltpu.make_async_remote_copy(
                src_ref=p_ref, dst_ref=land.at[me], send_sem=send_sems.at[k - 1], recv_sem=recv_sems.at[k - 1],
                device_id=peer, device_id_type=MESH))
        return res

    def start():
        p_ref[...] = jnp.zeros_like(p_ref)
        p_ref[R_META:R_META + N_META, :] = dlead_ref[META0:LEAD, :]
        p_ref[R_WG:R_WG + GLA_RANK, 0:256] = dwg_ref[0:GLA_RANK, :]
        p_ref[R_NORM_MIX:R_NORM_MIX + 1, :] = gnm_ref[0:1, :]
        p_ref[R_NORM_FF:R_NORM_FF + 1, :] = gnf_ref[0:1, :]
        p_ref[R_FINAL:R_FINAL + 1, :] = gfn_ref[0:1, :]
        p_ref[R_B_GATE:R_B_GATE + 1, 0:256] = dbg_ref[0:1, :]
        p_ref[R_GLA_NORM:R_GLA_NORM + 1, 0:128] = dgnw_ref[0:1, :]
        p_ref[R_LOSS:R_LOSS + 1, 0:128] = loss_ref[0:1, :]
        p_ref[R_SINKS:R_SINKS + SWA_HEADS, 0:128] = dsink_ref[...]
        land[me] = p_ref[...]
        for cp in copies():
            cp.start()

    def finish():
        for cp in copies():
            cp.wait_recv()
        for cp in copies():
            cp.wait_send()
        acc = land[0]
        for d in range(1, N_DEV):
            acc = acc + land[d]
        out_ref[...] = acc

    return start, finish


def _token_specs(tm, grid_rank=1):
    nb = tm // LEAD

    def spec(k):
        if grid_rank == 1:
            return pl.BlockSpec((LEAD, D), lambda i: (jnp.maximum(i * nb + k - 1, 0), 0))
        return pl.BlockSpec((LEAD, D), lambda i, j: (jnp.maximum(i * nb + k - 1, 0), 0))

    return [spec(k) for k in range(nb)]


def _h_tile(i, lead_ref, x_refs):
    first = jnp.where(i == 0, lead_ref[...], x_refs[0][...])
    return jnp.concatenate([first] + [r[...] for r in x_refs[1:]], axis=0)


def _in_proj(x, lead, nw, win_p, angles, tm, shards):
    rows = LEAD + x.shape[0]
    nb = tm // LEAD
    steps = rows // tm
    ns = len(shards)

    def body(*refs):
        x_refs, refs = refs[:nb], refs[nb:]
        lead_ref, nw_ref, w_ref, cs_ref = refs[:4]
        shard_refs, (o_ref, q_ref, k_ref, v_ref) = refs[4:4 + ns], refs[4 + ns:8 + ns]
        _place_gather(pl.program_id(0), steps, shard_refs, refs[8 + ns:8 + 2 * ns], refs[8 + 2 * ns:])
        h = _h_tile(pl.program_id(0), lead_ref, x_refs)
        rstd = lax.rsqrt(jnp.mean(h * h, axis=-1, keepdims=True) + EPS)
        u = (h * rstd * nw_ref[...]).astype(MXU_DTYPE)
        proj = jnp.dot(u, w_ref[...].astype(MXU_DTYPE), preferred_element_type=F32)
        o_ref[...] = proj[:, 0:DGLA]
        cos, sa, sb = _rope_tables(cs_ref[...])
        q_ref[...] = (_rope(proj[:, C_SQ:C_SK], cos, sa, sb) * (SWA_HD ** -0.5)).astype(ACT_DTYPE)
        k_ref[...] = _rope(proj[:, C_SK:C_SV], cos, sa, sb).astype(ACT_DTYPE)
        v_ref[...] = proj[:, C_SV:DINP].astype(ACT_DTYPE)

    row = lambda w: pl.BlockSpec((tm, w), lambda i: (i, 0))
    outs = pl.pallas_call(
        body, name="in_proj", grid=(steps,),
        in_specs=_token_specs(tm) + [pl.BlockSpec((LEAD, D), lambda i: (0, 0)), pl.BlockSpec((1, D), lambda i: (0, 0)),
                                     pl.BlockSpec((D, DINP), lambda i: (0, 0)), row(ROPE_DIM)]
        + [ANY] * ns,
        out_specs=[row(DGLA), row(512), row(128), row(128)] + [ANY] * ns,
        out_shape=[jax.ShapeDtypeStruct((rows, DGLA), F32), jax.ShapeDtypeStruct((rows, 512), ACT_DTYPE),
                   jax.ShapeDtypeStruct((rows, 128), ACT_DTYPE), jax.ShapeDtypeStruct((rows, 128), ACT_DTYPE)]
        + _gathered_shapes(shards),
        scratch_shapes=_gather_sems(ns),
        compiler_params=_cp(("arbitrary",), VMEM_WIDE_MB),
    )(*([x] * nb), lead, nw, win_p, angles, *shards)
    return outs[0], outs[1], outs[2], outs[3], _with_own_block(outs[4:], shards)


def _rope_angles(rows):
    pos = (jnp.arange(rows, dtype=jnp.int32) - META0).astype(F32)
    inv_freq = 1.0 / (ROPE_THETA ** (jnp.arange(0, ROPE_DIM, 2, dtype=F32) / ROPE_DIM))
    ang = pos[:, None] * inv_freq[None, :]
    return jnp.concatenate([jnp.cos(ang), jnp.sin(ang)], axis=1)


def _rope_tables(cs):
    shape = (2 * (ROPE_DIM // 2), 3 * 128)
    j = lax.broadcasted_iota(jnp.int32, shape, 0)
    col = lax.broadcasted_iota(jnp.int32, shape, 1)
    table, in_head = col // 128, col % SWA_HD
    match = (j % (ROPE_DIM // 2)) == (in_head % (ROPE_DIM // 2))
    is_cos = j < ROPE_DIM // 2
    first, second = in_head < ROPE_DIM // 2, (in_head >= ROPE_DIM // 2) & (in_head < ROPE_DIM)
    spread = (jnp.where(match & is_cos & (table == 0) & (first | second), 1.0, 0.0)
              + jnp.where(match & ~is_cos & (table == 1) & first, -1.0, 0.0)
              + jnp.where(match & ~is_cos & (table == 2) & second, 1.0, 0.0)).astype(jnp.bfloat16)
    hi = cs.astype(jnp.bfloat16)
    rest = cs - hi.astype(F32)
    mid = rest.astype(jnp.bfloat16)
    low = (rest - mid.astype(F32)).astype(jnp.bfloat16)
    dot = lambda part: jnp.dot(part, spread, preferred_element_type=F32)
    tabs = dot(hi) + (dot(mid) + dot(low))
    lane = lax.broadcasted_iota(jnp.int32, (1, 128), 1) % SWA_HD
    return tabs[:, 0:128] + jnp.where(lane >= ROPE_DIM, 1.0, 0.0), tabs[:, 128:256], tabs[:, 256:384]


def _rope(xv, cos, sa, sb):
    width = xv.shape[1]
    reps = width // 128
    if reps > 1:
        cos, sa, sb = (jnp.tile(t, (1, reps)) for t in (cos, sa, sb))
    return xv * cos + pltpu.roll(xv, width - 8, 1) * sa + pltpu.roll(xv, 8, 1) * sb


def _unrope(dy, cos, sa, sb):
    width = dy.shape[1]
    reps = width // 128
    if reps > 1:
        cos, sa, sb = (jnp.tile(t, (1, reps)) for t in (cos, sa, sb))
    return dy * cos + pltpu.roll(dy * sa, 8, 1) + pltpu.roll(dy * sb, width - 8, 1)


def _gla_group(nc, most):
    for g in (10, 5, 2):
        if g <= most and nc % g == 0:
            return g
    return 1


def _chunk_masks(nrows):
    ii = lax.broadcasted_iota(jnp.int32, (nrows, nrows), 0)
    jj = lax.broadcasted_iota(jnp.int32, (nrows, nrows), 1)
    same = (ii // GLA_CHUNK) == (jj // GLA_CHUNK)
    return same & (jj <= ii), same & (jj >= ii)


def _gla_gates(lr, wg, bg, first_row):
    nrows = lr.shape[0]
    zg = _mm(lr, wg) + bg
    live = first_row + lax.broadcasted_iota(jnp.int32, (nrows, 1), 0) >= META0
    g = jnp.where(live, _logsigmoid(zg) * (1.0 / GLA_TAU), 0.0)
    return _masked_sums(_chunk_masks(nrows)[0], g), jnp.where(live, _sigmoid(-zg) * (1.0 / GLA_TAU), 0.0)


def _tril64():
    ii = lax.broadcasted_iota(jnp.int32, (GLA_CHUNK, GLA_CHUNK), 0)
    jj = lax.broadcasted_iota(jnp.int32, (GLA_CHUNK, GLA_CHUNK), 1)
    return jj <= ii


def _gla_fwd(proj, wg_p, bg, gnw, shards):
    rows = proj.shape[0]
    nc = rows // GLA_CHUNK
    group = _gla_group(nc, 10)
    steps, nrows = nc // group, group * GLA_CHUNK
    ns = len(shards)

    def body(q_ref, k_ref, v_ref, r_ref, lr_ref, lr_next_ref, wg_ref, bg_ref, gnw_ref, *rest):
        shard_refs, rest = rest[:ns], rest[ns:]
        oraw_ref, og_ref, st_ref, decay_ref, dgate_ref = rest[:5]
        gathered_refs, rest = rest[5:5 + ns], rest[5 + ns:]
        state, gates = rest[:2]
        c = pl.program_id(0)

        @pl.when(c == 0)
        def _():
            state[...] = jnp.zeros_like(state)
            gates[0, 0], gates[0, 1] = _gla_gates(lr_ref[...], wg_ref[...], bg_ref[...], 0)

        _place_gather(c, steps, shard_refs, gathered_refs, rest[2:])
        slot = c % 2
        b = gates[slot, 0]
        decay_ref[...] = b
        dgate_ref[...] = gates[slot, 1]
        gates[1 - slot, 0], gates[1 - slot, 1] = _gla_gates(lr_next_ref[...], wg_ref[...], bg_ref[...], (c + 1) * nrows)
        eb = jnp.exp(b)
        gq = q_ref[...] * (GLA_DK ** -0.5) * eb
        gk = k_ref[...] * jnp.exp(-b)
        v = v_ref[...]
        gnw_v = gnw_ref[...]
        tril = _tril64()
        pairs = [(h, gi) for h in range(GLA_HEADS) for gi in range(group)]
        rs = {gi: slice(gi * GLA_CHUNK, (gi + 1) * GLA_CHUNK) for gi in range(group)}
        s64 = {h: slice(h * GLA_DK, (h + 1) * GLA_DK) for h in range(GLA_HEADS)}
        s128 = {h: slice(h * GLA_DV, (h + 1) * GLA_DV) for h in range(GLA_HEADS)}
        qh = {(h, gi): gq[rs[gi], s64[h]] for h, gi in pairs}
        kh = {(h, gi): gk[rs[gi], s64[h]] for h, gi in pairs}
        vh = {(h, gi): v[rs[gi], s128[h]] for h, gi in pairs}
        ebl = {(h, gi): eb[(gi + 1) * GLA_CHUNK - 1:(gi + 1) * GLA_CHUNK, s64[h]] for h, gi in pairs}
        av = {pr: _mm(jnp.where(tril, _mm_nt(qh[pr], kh[pr]), 0.0), vh[pr]) for pr in pairs}
        inc = {pr: _mm_tn(vh[pr], kh[pr] * ebl[pr]) for pr in pairs}
        st = {}
        for h in range(GLA_HEADS):
            cur = state[h]
            for gi in range(group):
                st[h, gi] = cur
                st_ref[gi, h] = cur
                cur = cur * ebl[h, gi] + inc[h, gi]
            state[h] = cur
        for h, gi in pairs:
            o = av[h, gi] + _mm_nt(qh[h, gi], st[h, gi])
            oraw_ref[rs[gi], s128[h]] = o
            rstd = lax.rsqrt(jnp.mean(o * o, axis=-1, keepdims=True) + EPS)
            rh = r_ref[rs[gi], s128[h]]
            og_ref[rs[gi], s128[h]] = (o * rstd * gnw_v * (rh * _sigmoid(rh))).astype(ACT_DTYPE)

    nb = lambda w, col: pl.BlockSpec((nrows, w), lambda c: (c, col // w))
    const = lambda shape: pl.BlockSpec(shape, lambda c: (0,) * len(shape))
    outs = pl.pallas_call(
        body, name="gla_fwd", grid=(steps,),
        in_specs=[nb(256, C_GQ), nb(256, C_GK), nb(512, C_GV), nb(512, C_GR), nb(128, C_LR),
                  pl.BlockSpec((nrows, 128), lambda c: (jnp.minimum(c + 1, steps - 1), C_LR // 128)),
                  const((128, 256)), const((1, 256)), const((1, 128))] + [ANY] * ns,
        out_specs=[nb(512, 0), nb(512, 0),
                   pl.BlockSpec((group, GLA_HEADS, GLA_DV, GLA_DK), lambda c: (c, 0, 0, 0)),
                   nb(256, 0), nb(256, 0)] + [ANY] * ns,
        out_shape=[jax.ShapeDtypeStruct((rows, 512), F32), jax.ShapeDtypeStruct((rows, 512), ACT_DTYPE),
                   jax.ShapeDtypeStruct((nc, GLA_HEADS, GLA_DV, GLA_DK), F32),
                   jax.ShapeDtypeStruct((rows, 256), F32), jax.ShapeDtypeStruct((rows, 256), F32)]
        + _gathered_shapes(shards),
        scratch_shapes=[pltpu.VMEM((GLA_HEADS, GLA_DV, GLA_DK), F32), pltpu.VMEM((2, 2, nrows, 256), F32)]
        + _gather_sems(ns),
        compiler_params=_cp(("arbitrary",)),
    )(proj, proj, proj, proj, proj, proj, wg_p, bg, gnw, *shards)
    return outs[0], outs[1], outs[2], outs[3], outs[4], _with_own_block(outs[5:], shards)


def _swa_mask(n):
    shape = (SWA_GROUP * SWA_BLOCK, 3 * SWA_BLOCK)
    qi = lax.broadcasted_iota(jnp.int32, shape, 0) & (SWA_BLOCK - 1)
    jj = lax.broadcasted_iota(jnp.int32, shape, 1)
    meta = (jj < SWA_BLOCK) & (jj >= META0) & ((n > 0) | (jj <= qi))
    prev = (jj >= SWA_BLOCK) & (jj < 2 * SWA_BLOCK) & (n >= 2) & (jj - SWA_BLOCK > qi)
    cur = (jj >= 2 * SWA_BLOCK) & (n >= 1) & (jj - 2 * SWA_BLOCK <= qi)
    return meta | prev | cur


def _stack_heads(t, kvh):
    return jnp.concatenate([t[:, (kvh * SWA_GROUP + g) * SWA_HD:(kvh * SWA_GROUP + g + 1) * SWA_HD]
                            for g in range(SWA_GROUP)], axis=0)


def _stack_sinks(sink_ref, kvh):
    return jnp.concatenate([jnp.full((SWA_BLOCK, 1), sink_ref[0, kvh * SWA_GROUP + g], F32)
                            for g in range(SWA_GROUP)], axis=0)


def _swa_group(nblk):
    return 5 if nblk % 5 == 0 else 1


def _swa_specs(group):
    blk = lambda w: pl.BlockSpec((group * SWA_BLOCK, w), lambda n: (n, 0))
    first = pl.BlockSpec((SWA_BLOCK, 128), lambda n: (0, 0))
    prev = pl.BlockSpec((SWA_BLOCK, 128), lambda n: (jnp.maximum(n * group - 1, 0), 0))
    return blk, first, prev


def _swa_keys(first_ref, prev_ref, cur_ref, g):
    own = cur_ref[g * SWA_BLOCK:(g + 1) * SWA_BLOCK, :]
    before = prev_ref[...] if g == 0 else cur_ref[(g - 1) * SWA_BLOCK:g * SWA_BLOCK, :]
    return jnp.concatenate([first_ref[...], before, own], axis=0)


def _swa_fwd(qr, kr, vr, sinks, shards):
    rows = qr.shape[0]
    nblk = rows // SWA_BLOCK
    group = _swa_group(nblk)
    steps = nblk // group
    ns = len(shards)

    def body(q_ref, k0, kp, kc, v0, vp, vc, sink_ref, *rest):
        o_ref = rest[ns]
        _place_gather(pl.program_id(0), steps, rest[:ns], rest[ns + 1:2 * ns + 1], rest[2 * ns + 1:])
        for g in range(group):
            n = pl.program_id(0) * group + g
            rs = slice(g * SWA_BLOCK, (g + 1) * SWA_BLOCK)
            kall, vall = _swa_keys(k0, kp, kc, g), _swa_keys(v0, vp, vc, g)
            mask = _swa_mask(n)[0:SWA_BLOCK]
            heads = range(SWA_HEADS)
            hs = [slice(h * SWA_HD, (h + 1) * SWA_HD) for h in heads]
            kv = [slice((h // SWA_GROUP) * SWA_HD, (h // SWA_GROUP + 1) * SWA_HD) for h in heads]
            s = [jnp.where(mask, _mm_nt(q_ref[rs, hs[h]], kall[:, kv[h]]), NEG) for h in heads]
            m = [jnp.maximum(jnp.max(s[h], axis=-1, keepdims=True), sink_ref[0, h]) for h in heads]
            p = [jnp.exp(s[h] - m[h]) for h in heads]
            den = [jnp.sum(p[h], axis=-1, keepdims=True) + jnp.exp(sink_ref[0, h] - m[h]) for h in heads]
            o = [_mm(p[h], vall[:, kv[h]]) for h in heads]
            for h in heads:
                o_ref[rs, hs[h]] = (o[h] / den[h]).astype(ACT_DTYPE)

    blk, first, prev = _swa_specs(group)
    outs = pl.pallas_call(
        body, name="swa_fwd", grid=(steps,),
        in_specs=[blk(512), first, prev, blk(128), first, prev, blk(128),
                  pl.BlockSpec(memory_space=pltpu.SMEM)] + [ANY] * ns,
        out_specs=[blk(512)] + [ANY] * ns,
        out_shape=[jax.ShapeDtypeStruct((rows, 512), ACT_DTYPE)] + _gathered_shapes(shards),
        scratch_shapes=_gather_sems(ns),
        compiler_params=_cp(("arbitrary",)),
    )(qr, kr, kr, kr, vr, vr, vr, sinks, *shards)
    return outs[0], _with_own_block(outs[1:], shards)


def _out_proj(x, lead, og, osw, wout, nfw, tm):
    rows = LEAD + x.shape[0]
    nb = tm // LEAD

    def body(*refs):
        x_refs, (lead_ref, og_ref, os_ref, w_ref, nw_ref, h1_ref, f_ref, ft_ref) = refs[:nb], refs[nb:]
        h0 = _h_tile(pl.program_id(0), lead_ref, x_refs)
        h1 = h0 + _mm(og_ref[...], w_ref[0:512, :]) + _mm(os_ref[...], w_ref[512:1024, :])
        h1_ref[...] = h1
        rstd = lax.rsqrt(jnp.mean(h1 * h1, axis=-1, keepdims=True) + EPS)
        f = h1 * rstd * nw_ref[...]
        f_ref[...] = f.astype(ACT_DTYPE)
        ft_ref[...] = f.T.astype(ACT_DTYPE)

    row = lambda w: pl.BlockSpec((tm, w), lambda i: (i, 0))
    return pl.pallas_call(
        body, name="out_proj", grid=(rows // tm,),
        in_specs=_token_specs(tm) + [pl.BlockSpec((LEAD, D), lambda i: (0, 0)), row(512), row(512),
                                     pl.BlockSpec((D, D), lambda i: (0, 0)), pl.BlockSpec((1, D), lambda i: (0, 0))],
        out_specs=[row(D), row(D), pl.BlockSpec((D, tm), lambda i: (0, i))],
        out_shape=[jax.ShapeDtypeStruct((rows, D), F32), jax.ShapeDtypeStruct((rows, D), ACT_DTYPE),
                   jax.ShapeDtypeStruct((D, rows), ACT_DTYPE)],
        compiler_params=_cp(("arbitrary",), VMEM_TILE_MB),
    )(*([x] * nb), lead, og, osw, wout, nfw)


def _ffn_fwd(f, h1, w1, w2, tgt, fnw, tm):
    rows = f.shape[0]
    nj = D_FF // FF_WIDE
    nb = tm // LEAD

    def body(f_ref, h1_ref, w1_ref, w2_ref, nw_ref, *rest):
        t_refs, (a_ref, dh2_ref, dh2t_ref, loss_ref, gfn_ref, acc) = rest[:nb], rest[nb:]
        i, j = pl.program_id(0), pl.program_id(1)

        @pl.when((i == 0) & (j == 0))
        def _():
            loss_ref[...] = jnp.zeros_like(loss_ref)
            gfn_ref[...] = jnp.zeros_like(gfn_ref)

        @pl.when(j == 0)
        def _():
            acc[...] = jnp.zeros_like(acc)

        a = _mm(f_ref[...], w1_ref[...])
        a_ref[...] = a.astype(ACT_DTYPE)
        z = jnp.square(jnp.maximum(a, 0.0))
        acc[...] += _mm(z, w2_ref[...])

        @pl.when(j == nj - 1)
        def _():
            h2 = h1_ref[...] + acc[...]
            rstd = lax.rsqrt(jnp.mean(h2 * h2, axis=-1, keepdims=True) + EPS)
            hn = h2 * rstd
            nw = nw_ref[...]
            row = i * tm + lax.broadcasted_iota(jnp.int32, (tm, 1), 0)
            target = jnp.concatenate([t[...] for t in t_refs], axis=0)
            err = jnp.where(row >= LEAD, hn * nw - target, 0.0)
            row_loss = jnp.sum(err * err, axis=-1, keepdims=True) * (1.0 / D)
            loss_ref[...] += jnp.broadcast_to(0.5 * jnp.sum(row_loss, axis=0, keepdims=True), loss_ref.shape)
            dy = err * (1.0 / D)
            gfn_ref[...] += jnp.broadcast_to(jnp.sum(dy * hn, axis=0, keepdims=True), gfn_ref.shape)
            dhn = dy * nw
            dh2 = rstd * (dhn - hn * jnp.mean(dhn * hn, axis=-1, keepdims=True))
            dh2_ref[...] = dh2
            dh2t_ref[...] = dh2.T.astype(ACT_DTYPE)

    return pl.pallas_call(
        body, name="ffn_fwd", grid=(rows // tm, nj),
        in_specs=[pl.BlockSpec((tm, D), lambda i, j: (i, 0)), pl.BlockSpec((tm, D), lambda i, j: (i, 0)),
                  pl.BlockSpec((D, FF_WIDE), lambda i, j: (0, j)),
                  pl.BlockSpec((FF_WIDE, D), lambda i, j: (j, 0)),
                  pl.BlockSpec((1, D), lambda i, j: (0, 0))] + _token_specs(tm, grid_rank=2),
        out_specs=[pl.BlockSpec((tm, FF_WIDE), lambda i, j: (i, j)), pl.BlockSpec((tm, D), lambda i, j: (i, 0)),
                   pl.BlockSpec((D, tm), lambda i, j: (0, i)),
                   pl.BlockSpec((8, 128), lambda i, j: (0, 0)), pl.BlockSpec((8, D), lambda i, j: (0, 0))],
        out_shape=[jax.ShapeDtypeStruct((rows, D_FF), ACT_DTYPE), jax.ShapeDtypeStruct((rows, D), F32),
                   jax.ShapeDtypeStruct((D, rows), ACT_DTYPE),
                   jax.ShapeDtypeStruct((8, 128), F32), jax.ShapeDtypeStruct((8, D), F32)],
        scratch_shapes=[pltpu.VMEM((tm, D), F32)],
        compiler_params=_cp(("arbitrary", "arbitrary"), VMEM_WIDE_MB),
    )(f, h1, w1, w2, fnw, *([tgt] * nb))


def _ffn_bwd_act(dh2, a, w1, w2, h1, nfw, tm):
    rows = dh2.shape[0]
    nj = D_FF // FF_WIDE

    def body(dh2_ref, a_ref, w1_ref, w2_ref, h1_ref, nw_ref, da_ref, dh1_ref, gnf_ref, acc):
        i, j = pl.program_id(0), pl.program_id(1)

        @pl.when((i == 0) & (j == 0))
        def _():
            gnf_ref[...] = jnp.zeros_like(gnf_ref)

        @pl.when(j == 0)
        def _():
            acc[...] = jnp.zeros_like(acc)

        dz = _mm_nt(dh2_ref[...], w2_ref[...])
        da = dz * (2.0 * jnp.maximum(a_ref[...].astype(F32), 0.0))
        da_ref[...] = da.astype(ACT_DTYPE)
        acc[...] += _mm_nt(da, w1_ref[...])

        @pl.when(j == nj - 1)
        def _():
            h1 = h1_ref[...]
            rstd = lax.rsqrt(jnp.mean(h1 * h1, axis=-1, keepdims=True) + EPS)
            hn = h1 * rstd
            df = acc[...]
            gnf_ref[...] += jnp.broadcast_to(jnp.sum(df * hn, axis=0, keepdims=True), gnf_ref.shape)
            dfn = df * nw_ref[...]
            dh1_ref[...] = dh2_ref[...] + rstd * (dfn - hn * jnp.mean(dfn * hn, axis=-1, keepdims=True))

    return pl.pallas_call(
        body, name="ffn_bwd_act", grid=(rows // tm, nj),
        in_specs=[pl.BlockSpec((tm, D), lambda i, j: (i, 0)), pl.BlockSpec((tm, FF_WIDE), lambda i, j: (i, j)),
                  pl.BlockSpec((D, FF_WIDE), lambda i, j: (0, j)),
                  pl.BlockSpec((FF_WIDE, D), lambda i, j: (j, 0)),
                  pl.BlockSpec((tm, D), lambda i, j: (i, 0)), pl.BlockSpec((1, D), lambda i, j: (0, 0))],
        out_specs=[pl.BlockSpec((tm, FF_WIDE), lambda i, j: (i, j)), pl.BlockSpec((tm, D), lambda i, j: (i, 0)),
                   pl.BlockSpec((8, D), lambda i, j: (0, 0))],
        out_shape=[jax.ShapeDtypeStruct((rows, D_FF), ACT_DTYPE), jax.ShapeDtypeStruct((rows, D), F32),
                   jax.ShapeDtypeStruct((8, D), F32)],
        scratch_shapes=[pltpu.VMEM((tm, D), F32)],
        compiler_params=_cp(("arbitrary", "arbitrary"), VMEM_WIDE_MB),
    )(dh2, a, w1, w2, h1, nfw)


def _ffn_bwd_weights(ft, a, da, dh2t, tm):
    rows = a.shape[0]
    steps = rows // tm
    pair = 2 * FF_TILE

    def body(ft_ref, a_ref, da_ref, dh2t_ref, dw1_ref, dw2_ref, dw2t):
        i = pl.program_id(1)

        @pl.when(i == 0)
        def _():
            dw1_ref[...] = jnp.zeros_like(dw1_ref)
            dw2t[...] = jnp.zeros_like(dw2t)

        z = jnp.square(jnp.maximum(a_ref[...].astype(F32), 0.0))
        dw1 = _mm(ft_ref[...], da_ref[...])
        for core in range(2):
            dw1_ref[core] += dw1[:, core * FF_TILE:(core + 1) * FF_TILE]
        dw2t[...] += _mm(dh2t_ref[...], z)

        @pl.when(i == steps - 1)
        def _():
            for core in range(2):
                dw2_ref[core] = dw2t[:, core * FF_TILE:(core + 1) * FF_TILE].T

    return pl.pallas_call(
        body, name="ffn_bwd_weights", grid=(N_DEV // 2, steps),
        in_specs=[pl.BlockSpec((D, tm), lambda j, i: (0, i)), pl.BlockSpec((tm, pair), lambda j, i: (i, j)),
                  pl.BlockSpec((tm, pair), lambda j, i: (i, j)), pl.BlockSpec((D, tm), lambda j, i: (0, i))],
        out_specs=[pl.BlockSpec((2, None, D, FF_TILE), lambda j, i: (0, j, 0, 0)),
                   pl.BlockSpec((2, None, FF_TILE, D), lambda j, i: (0, j, 0, 0))],
        out_shape=[jax.ShapeDtypeStruct((2, 4, D, FF_TILE), F32), jax.ShapeDtypeStruct((2, 4, FF_TILE, D), F32)],
        scratch_shapes=[pltpu.VMEM((D, pair), F32)],
        compiler_params=_cp(("arbitrary", "arbitrary"), VMEM_WIDE_MB),
    )(ft, a, da, dh2t)


def _out_proj_bwd(dh1, og, osw, wout, tm, partials):
    rows = dh1.shape[0]
    steps = rows // tm
    ns = len(partials)

    def body(dh1_ref, og_ref, os_ref, w_ref, *rest):
        part_refs, rest = rest[:ns], rest[ns:]
        dog_ref, dos_ref, dw_ref = rest[:3]
        land_refs, (send_sems, recv_sems) = rest[3:3 + ns], rest[3 + ns:]
        i = pl.program_id(0)
        start, finish = _sibling_schedule(part_refs, land_refs, send_sems, recv_sems)

        @pl.when(i == 0)
        def _():
            dw_ref[...] = jnp.zeros_like(dw_ref)
            start()

        pl.when(i == steps - 1)(finish)

        dh1 = dh1_ref[...].astype(MXU_DTYPE)
        dog_ref[...] = _mm_nt(dh1, w_ref[0:512, :])
        dos_ref[...] = _mm_nt(dh1, w_ref[512:1024, :])
        for half, ref in enumerate((og_ref, os_ref)):
            dw = _mm_tn(ref[...], dh1)
            for blk in range(4):
                shard = half * 4 + blk
                dw_ref[shard % 2, shard // 2] += dw[blk * 128:(blk + 1) * 128, :]

    row = lambda w: pl.BlockSpec((tm, w), lambda i: (i, 0))
    outs = pl.pallas_call(
        body, name="out_proj_bwd", grid=(steps,),
        in_specs=[row(D), row(512), row(512), pl.BlockSpec((D, D), lambda i: (0, 0))] + [ANY] * ns,
        out_specs=[row(512), row(512), pl.BlockSpec((2, 4, 128, D), lambda i: (0, 0, 0, 0))] + [ANY] * ns,
        out_shape=[jax.ShapeDtypeStruct((rows, 512), F32), jax.ShapeDtypeStruct((rows, 512), F32),
                   jax.ShapeDtypeStruct((2, 4, 128, D), F32)] + _sibling_shapes(partials),
        scratch_shapes=_sibling_sems(ns),
        compiler_params=_cp(("arbitrary",), VMEM_TILE_MB),
    )(dh1, og, osw, wout, *partials)
    return outs[0], outs[1], outs[2], outs[3:]


def _swa_bwd(qr, kr, vr, osw, dos, sinks, jobs):
    rows = qr.shape[0]
    nblk = rows // SWA_BLOCK
    group = _swa_group(nblk)
    steps = nblk // group
    ns = jobs.n

    def body(q_ref, k0, kp, kc, v0, vp, vc, o_ref, do_ref, sink_ref, *rest):
        dq_ref, dk_ref, dv_ref, dsink_ref = rest[ns:ns + 4]
        start, finish = jobs.bind(rest[:ns], rest[ns + 4:2 * ns + 4], rest[2 * ns + 4:])
        step = pl.program_id(0)

        @pl.when(step == 0)
        def _():
            dk_ref[...] = jnp.zeros_like(dk_ref)
            dv_ref[...] = jnp.zeros_like(dv_ref)
            dsink_ref[...] = jnp.zeros_like(dsink_ref)
            start()

        pl.when(step == steps - 1)(finish)
        for g in range(group):
            block(step * group + g, g, q_ref, k0, kp, kc, v0, vp, vc, o_ref, do_ref, sink_ref,
                  dq_ref, dk_ref, dv_ref, dsink_ref)

    def block(n, g, q_ref, k0, kp, kc, v0, vp, vc, o_ref, do_ref, sink_ref, dq_ref, dk_ref, dv_ref, dsink_ref):
        rs = slice(g * SWA_BLOCK, (g + 1) * SWA_BLOCK)
        kall, vall = _swa_keys(k0, kp, kc, g), _swa_keys(v0, vp, vc, g)
        mask = _swa_mask(n)[0:SWA_BLOCK]
        heads = range(SWA_HEADS)
        hs = [slice(h * SWA_HD, (h + 1) * SWA_HD) for h in heads]
        kv = [slice((h // SWA_GROUP) * SWA_HD, (h // SWA_GROUP + 1) * SWA_HD) for h in heads]
        sink = [sink_ref[0, h] for h in heads]
        qh = [q_ref[rs, hs[h]] for h in heads]
        doh = [do_ref[rs, hs[h]] for h in heads]
        s = [jnp.where(mask, _mm_nt(qh[h], kall[:, kv[h]]), NEG) for h in heads]
        dp = [_mm_nt(doh[h], vall[:, kv[h]]) for h in heads]
        delta = [jnp.sum(doh[h] * o_ref[rs, hs[h]].astype(F32), axis=-1, keepdims=True) for h in heads]
        m = [jnp.maximum(jnp.max(s[h], axis=-1, keepdims=True), sink[h]) for h in heads]
        e = [jnp.exp(s[h] - m[h]) for h in heads]
        inv = [1.0 / (jnp.sum(e[h], axis=-1, keepdims=True) + jnp.exp(sink[h] - m[h])) for h in heads]
        p = [e[h] * inv[h] for h in heads]
        ds = [p[h] * (dp[h] - delta[h]) for h in heads]
        dq = [_mm(ds[h], kall[:, kv[h]]) for h in heads]
        dkh = [_mm_tn(ds[h], qh[h]) for h in heads]
        dvh = [_mm_tn(p[h], doh[h]) for h in heads]
        for h in heads:
            dsink = -jnp.sum(jnp.exp(sink[h] - m[h]) * inv[h] * delta[h], axis=0, keepdims=True)
            dsink_ref[h:h + 1, :] += jnp.broadcast_to(dsink, (1, 128))
        dq_ref[rs, :] = jnp.concatenate(dq, axis=1).astype(ACT_DTYPE)
        group_sum = lambda parts, kvh: sum(parts[kvh * SWA_GROUP + 1:(kvh + 1) * SWA_GROUP], parts[kvh * SWA_GROUP])
        dk_all = jnp.concatenate([group_sum(dkh, kvh) for kvh in range(SWA_KV)], axis=1)
        dv_all = jnp.concatenate([group_sum(dvh, kvh) for kvh in range(SWA_KV)], axis=1)
        prev0 = pl.multiple_of(jnp.maximum(n - 1, 0) * SWA_BLOCK, SWA_BLOCK)
        cur0 = pl.multiple_of(n * SWA_BLOCK, SWA_BLOCK)
        for ref, val in ((dk_ref, dk_all), (dv_ref, dv_all)):
            ref[0:SWA_BLOCK, :] += val[0:SWA_BLOCK]
            ref[pl.ds(prev0, SWA_BLOCK), :] += val[SWA_BLOCK:2 * SWA_BLOCK]
            ref[pl.ds(cur0, SWA_BLOCK), :] += val[2 * SWA_BLOCK:]

    blk, first, prev = _swa_specs(group)
    whole = pl.BlockSpec((rows, 128), lambda n: (0, 0))
    outs = pl.pallas_call(
        body, name="swa_bwd", grid=(steps,),
        in_specs=[blk(512), first, prev, blk(128), first, prev, blk(128), blk(512), blk(512),
                  pl.BlockSpec(memory_space=pltpu.SMEM)] + [ANY] * ns,
        out_specs=[blk(512), whole, whole, pl.BlockSpec((8, 128), lambda n: (0, 0))] + [ANY] * ns,
        out_shape=[jax.ShapeDtypeStruct((rows, 512), ACT_DTYPE), jax.ShapeDtypeStruct((rows, 128), F32),
                   jax.ShapeDtypeStruct((rows, 128), F32), jax.ShapeDtypeStruct((8, 128), F32)] + jobs.out_shapes,
        scratch_shapes=jobs.sems,
        compiler_params=_cp(("arbitrary",), VMEM_TILE_MB),
    )(qr, kr, kr, kr, vr, vr, vr, osw, dos, sinks, *jobs.inputs)
    return outs[0], outs[1], outs[2], outs[3], jobs.split(outs[4:])


def _gla_bwd(proj, decay, dgate, oraw, states, dog, wg_p, gnw, jobs):
    rows = proj.shape[0]
    nc = rows // GLA_CHUNK
    group = _gla_group(nc, 5)
    steps, nrows = nc // group, group * GLA_CHUNK
    ns = jobs.n

    def body(q_ref, k_ref, v_ref, r_ref, lr_ref, b_ref, dgate_ref, oraw_ref, st_ref, dog_ref, wg_ref, gnw_ref, *rest):
        dq_ref, dk_ref, dv_ref, dr_ref, dlr_ref, dwg_ref, dbg_ref, dgnw_ref = rest[ns:ns + 8]
        dstate, db_scr = rest[2 * ns + 8:2 * ns + 10]
        start, finish = jobs.bind(rest[:ns], rest[ns + 8:2 * ns + 8], rest[2 * ns + 10:])
        t = pl.program_id(0)

        @pl.when(t == 0)
        def _():
            dstate[...] = jnp.zeros_like(dstate)
            dwg_ref[...] = jnp.zeros_like(dwg_ref)
            dbg_ref[...] = jnp.zeros_like(dbg_ref)
            dgnw_ref[...] = jnp.zeros_like(dgnw_ref)
            start()

        pl.when(t == steps - 1)(finish)

        lr, wg = lr_ref[...], wg_ref[...]
        b = b_ref[...]
        eb, enb = jnp.exp(b), jnp.exp(-b)
        scale = GLA_DK ** -0.5
        gq = q_ref[...] * scale * eb
        gk = k_ref[...] * enb
        v = v_ref[...]
        gnw_v = gnw_ref[...]
        tril = _tril64()
        is_last = lax.broadcasted_iota(jnp.int32, (GLA_CHUNK, 1), 0) == GLA_CHUNK - 1
        dgnw = jnp.zeros((1, GLA_DV), F32)
        pairs = [(h, gi) for h in range(GLA_HEADS) for gi in range(group)]
        rs = {gi: slice(gi * GLA_CHUNK, (gi + 1) * GLA_CHUNK) for gi in range(group)}
        s64 = {h: slice(h * GLA_DK, (h + 1) * GLA_DK) for h in range(GLA_HEADS)}
        s128 = {h: slice(h * GLA_DV, (h + 1) * GLA_DV) for h in range(GLA_HEADS)}
        qh = {(h, gi): gq[rs[gi], s64[h]] for h, gi in pairs}
        kh = {(h, gi): gk[rs[gi], s64[h]] for h, gi in pairs}
        vh = {(h, gi): v[rs[gi], s128[h]] for h, gi in pairs}
        ebl = {(h, gi): eb[(gi + 1) * GLA_CHUNK - 1:(gi + 1) * GLA_CHUNK, s64[h]] for h, gi in pairs}
        kl = {pr: kh[pr] * ebl[pr] for pr in pairs}
        st = {(h, gi): st_ref[gi, h] for h, gi in pairs}
        do = {}
        for h, gi in pairs:
            o, rh, dout = oraw_ref[rs[gi], s128[h]], r_ref[rs[gi], s128[h]], dog_ref[rs[gi], s128[h]]
            rstd = lax.rsqrt(jnp.mean(o * o, axis=-1, keepdims=True) + EPS)
            on = o * rstd
            sg = _sigmoid(rh)
            dr_ref[rs[gi], s128[h]] = (dout * (on * gnw_v) * (sg * (1.0 + rh * (1.0 - sg)))).astype(ACT_DTYPE)
            dy = dout * (rh * sg)
            dgnw = dgnw + jnp.sum(dy * on, axis=0, keepdims=True)
            don = dy * gnw_v
            do[h, gi] = rstd * (don - on * jnp.mean(don * on, axis=-1, keepdims=True))
        a = {pr: jnp.where(tril, _mm_nt(qh[pr], kh[pr]), 0.0) for pr in pairs}
        da = {pr: jnp.where(tril, _mm_nt(do[pr], vh[pr]), 0.0) for pr in pairs}
        dinc = {pr: _mm_tn(do[pr], qh[pr]) for pr in pairs}
        dgq = {pr: _mm(da[pr], kh[pr]) + _mm(do[pr], st[pr]) for pr in pairs}
        dgk = {pr: _mm_tn(da[pr], qh[pr]) for pr in pairs}
        dv_a = {pr: _mm_tn(a[pr], do[pr]) for pr in pairs}
        dsp = {}
        for h in range(GLA_HEADS):
            cur = dstate[h]
            for gi in reversed(range(group)):
                dsp[h, gi] = cur
                cur = cur * ebl[h, gi] + dinc[h, gi]
            dstate[h] = cur
        for h, gi in pairs:
            pr = (h, gi)
            dkl = _mm(vh[pr], dsp[pr])
            dv_ref[rs[gi], s128[h]] = (dv_a[pr] + _mm_nt(kl[pr], dsp[pr])).astype(ACT_DTYPE)
            debl = jnp.sum(dsp[pr] * st[pr], axis=0, keepdims=True)
            dq_ref[rs[gi], s64[h]] = (dgq[pr] * (scale * eb[rs[gi], s64[h]])).astype(ACT_DTYPE)
            dk_ref[rs[gi], s64[h]] = ((dgk[pr] + dkl * ebl[pr]) * enb[rs[gi], s64[h]]).astype(ACT_DTYPE)
            last = debl * ebl[pr] + jnp.sum(dkl * kl[pr], axis=0, keepdims=True)
            db_scr[rs[gi], s64[h]] = (dgq[pr] * qh[pr] - dgk[pr] * kh[pr] - dkl * kl[pr]
                                      + jnp.where(is_last, last, 0.0))
        dzg = _masked_sums(_chunk_masks(nrows)[1], db_scr[...]) * dgate_ref[...]
        dlr_ref[...] = _mm_nt(dzg, wg).astype(ACT_DTYPE)
        dwg_ref[...] += _mm_tn(lr, dzg)
        dbg_ref[...] += jnp.broadcast_to(jnp.sum(dzg, axis=0, keepdims=True), dbg_ref.shape)
        dgnw_ref[...] += jnp.broadcast_to(dgnw, dgnw_ref.shape)

    nb = lambda w, col: pl.BlockSpec((nrows, w), lambda t: (steps - 1 - t, col // w))
    const = lambda shape: pl.BlockSpec(shape, lambda t: (0,) * len(shape))
    outs = pl.pallas_call(
        body, name="gla_bwd", grid=(steps,),
        in_specs=[nb(256, C_GQ), nb(256, C_GK), nb(512, C_GV), nb(512, C_GR), nb(128, C_LR), nb(256, 0), nb(256, 0),
                  nb(512, 0),
                  pl.BlockSpec((group, GLA_HEADS, GLA_DV, GLA_DK), lambda t: (steps - 1 - t, 0, 0, 0)), nb(512, 0),
                  const((128, 256)), const((1, 128))] + [ANY] * ns,
        out_specs=[nb(256, 0), nb(256, 0), nb(512, 0), nb(512, 0), nb(128, 0),
                   const((128, 256)), const((8, 256)), const((8, 128))] + [ANY] * ns,
        out_shape=[jax.ShapeDtypeStruct((rows, 256), ACT_DTYPE), jax.ShapeDtypeStruct((rows, 256), ACT_DTYPE),
                   jax.ShapeDtypeStruct((rows, 512), ACT_DTYPE), jax.ShapeDtypeStruct((rows, 512), ACT_DTYPE),
                   jax.ShapeDtypeStruct((rows, 128), ACT_DTYPE), jax.ShapeDtypeStruct((128, 256), F32),
                   jax.ShapeDtypeStruct((8, 256), F32), jax.ShapeDtypeStruct((8, 128), F32)] + jobs.out_shapes,
        scratch_shapes=[pltpu.VMEM((GLA_HEADS, GLA_DV, GLA_DK), F32), pltpu.VMEM((nrows, 256), F32)] + jobs.sems,
        compiler_params=_cp(("arbitrary",)),
    )(proj, proj, proj, proj, proj, decay, dgate, oraw, states, dog, wg_p, gnw, *jobs.inputs)
    return outs[:8], jobs.split(outs[8:])


def _in_proj_bwd(x, lead, dh1, nw, win_p, dgv, dgr, dsq, dgq, dgk, dsk, dsv, dlr, angles, tm):
    seq = x.shape[0]
    rows = LEAD + seq
    nb = tm // LEAD
    steps = rows // tm

    def first_copy(scr, gx_ref, sem):
        return pltpu.make_async_copy(scr.at[pl.ds(LEAD, tm - LEAD)], gx_ref.at[pl.ds(0, tm - LEAD)], sem)

    def tile_copy(scr, gx_ref, sem, step):
        start = pl.multiple_of(jnp.maximum(step * tm - LEAD, 0), LEAD)
        return pltpu.make_async_copy(scr, gx_ref.at[pl.ds(start, tm)], sem)

    def body(*refs):
        x_refs, refs = refs[:nb], refs[nb:]
        (lead_ref, dh1_ref, nw_ref, w_ref, dgv_ref, dgr_ref, dsq_ref, dgq_ref, dgk_ref, dsk_ref, dsv_ref, dlr_ref,
         cs_ref, gx_ref, dlead_ref, dproj_ref, ut_ref, gnm_ref, scr, sem) = refs
        i = pl.program_id(0)

        @pl.when(i == 0)
        def _():
            gnm_ref[...] = jnp.zeros_like(gnm_ref)

        cos, sa, sb = _rope_tables(cs_ref[...])
        dsq_v = (_unrope(dsq_ref[...].astype(F32), cos, sa, sb) * (SWA_HD ** -0.5)).astype(MXU_DTYPE)
        dsk_v = _unrope(dsk_ref[...], cos, sa, sb).astype(MXU_DTYPE)
        dproj = jnp.concatenate(
            [dgv_ref[...].astype(MXU_DTYPE), dgr_ref[...].astype(MXU_DTYPE), dgq_ref[...].astype(MXU_DTYPE),
             dgk_ref[...].astype(MXU_DTYPE), dlr_ref[...].astype(MXU_DTYPE), dsq_v, dsk_v,
             dsv_ref[...].astype(MXU_DTYPE)],
            axis=1)
        dproj_ref[...] = dproj
        h = _h_tile(i, lead_ref, x_refs)
        rstd = lax.rsqrt(jnp.mean(h * h, axis=-1, keepdims=True) + EPS)
        hn = h * rstd
        nw_v = nw_ref[...]
        ut_ref[...] = (hn * nw_v).T.astype(ACT_DTYPE)
        du = _mm_nt(dproj, w_ref[...])
        gnm_ref[...] += jnp.broadcast_to(jnp.sum(du * hn, axis=0, keepdims=True), gnm_ref.shape)
        dun = du * nw_v
        dh0 = dh1_ref[...] + rstd * (dun - hn * jnp.mean(dun * hn, axis=-1, keepdims=True))

        if tm > LEAD:
            pl.when(i == 1)(lambda: first_copy(scr, gx_ref, sem).wait())
        pl.when(i > 1)(lambda: tile_copy(scr, gx_ref, sem, i).wait())
        scr[...] = dh0

        @pl.when(i == 0)
        def _():
            dlead_ref[...] = dh0[0:LEAD]
            if tm > LEAD:
                first_copy(scr, gx_ref, sem).start()
                if steps == 1:
                    first_copy(scr, gx_ref, sem).wait()

        @pl.when(i > 0)
        def _():
            tile_copy(scr, gx_ref, sem, i).start()

        if steps > 1:
            pl.when(i == steps - 1)(lambda: tile_copy(scr, gx_ref, sem, i).wait())

    row = lambda w: pl.BlockSpec((tm, w), lambda i: (i, 0))
    const = lambda shape: pl.BlockSpec(shape, lambda i: (0,) * len(shape))
    return pl.pallas_call(
        body, name="in_proj_bwd", grid=(steps,),
        in_specs=_token_specs(tm) + [const((LEAD, D)), row(D), const((1, D)), const((D, DINP)),
                                     row(512), row(512), row(512), row(256), row(256), row(128), row(128), row(128),
                                     row(ROPE_DIM)],
        out_specs=[ANY, const((LEAD, D)), row(DINP), pl.BlockSpec((D, tm), lambda i: (0, i)), const((8, D))],
        out_shape=[jax.ShapeDtypeStruct((seq, D), F32), jax.ShapeDtypeStruct((LEAD, D), F32),
                   jax.ShapeDtypeStruct((rows, DINP), ACT_DTYPE), jax.ShapeDtypeStruct((D, rows), ACT_DTYPE),
                   jax.ShapeDtypeStruct((8, D), F32)],
        scratch_shapes=[pltpu.VMEM((tm, D), F32), pltpu.SemaphoreType.DMA],
        compiler_params=_cp(("arbitrary",), VMEM_WIDE_MB),
    )(*([x] * nb), lead, dh1, nw, win_p, dgv, dgr, dsq, dgq, dgk, dsk, dsv, dlr, angles)


def _win_runs():
    groups = [(O_GQ, C_GQ), (O_GK, C_GK), (O_GV, C_GV), (O_GR, C_GR), (O_LR, C_LR), (O_SQ, C_SQ), (O_SK, C_SK),
              (O_SV, C_SV)]
    per = DIN // N_DEV
    runs = []
    for (o0, o1), c0 in groups:
        o = o0
        while o < o1:
            d = o // per
            end = min(o1, (d + 1) * per)
            runs.append((d, o - d * per, c0 + o - o0, end - o))
            o = end
    return runs


def _win_padded(g_in):
    tr = 128

    def body(g_ref, o_ref):
        o_ref[...] = jnp.zeros_like(o_ref)
        for d, s, c, w in _win_runs():
            o_ref[:, c:c + w] = g_ref[d, :, s:s + w]

    return pl.pallas_call(
        body, name="w_in_layout", grid=(D // tr,),
        in_specs=[pl.BlockSpec((N_DEV, tr, DIN // N_DEV), lambda i: (0, i, 0))],
        out_specs=pl.BlockSpec((tr, DINP), lambda i: (i, 0)),
        out_shape=jax.ShapeDtypeStruct((D, DINP), g_in.dtype),
        compiler_params=_cp(("arbitrary",)),
    )(g_in)


def _in_proj_bwd_weights(ut, dproj, tm, small):
    rows = dproj.shape[0]
    steps = rows // tm
    per = DIN // N_DEV

    def body(ut_ref, dp_ref, *rest):
        small_refs, (mine_ref, theirs_ref, total_ref, acc, stage, local_sems, send_sems, recv_sems) = rest[:9], rest[9:17]
        i = pl.program_id(0)
        start, finish = _small_sum_schedule(small_refs, total_ref, *rest[17:])
        x, y, c = _mesh_pos()

        @pl.when(i == 0)
        def _():
            acc[...] = jnp.zeros_like(acc)
            start()

        acc[...] += _mm(ut_ref[...], dp_ref[...])
        pl.when(i == steps - 1)(finish)

        def keep(slot, chip):
            return pltpu.make_async_copy(stage.at[slot], mine_ref.at[chip], local_sems.at[slot])

        def send(slot, chip):
            return pltpu.make_async_remote_copy(
                src_ref=stage.at[slot], dst_ref=theirs_ref.at[chip], send_sem=send_sems.at[slot],
                recv_sem=recv_sems.at[chip], device_id=(x, y, 1 - c), device_id_type=MESH)

        def drained(d):
            pl.when(c == d % 2)(keep(d % 2, d // 2).wait)
            pl.when(c != d % 2)(send(d % 2, d // 2).wait_send)

        @pl.when(i == steps - 1)
        def _():
            for d in range(N_DEV):
                slot, chip = d % 2, d // 2
                if d >= 2:
                    drained(d - 2)
                for owner, s, col, w in _win_runs():
                    if owner == d:
                        stage[slot, :, s:s + w] = acc[:, col:col + w]
                pl.when(c == slot)(keep(slot, chip).start)
                pl.when(c != slot)(send(slot, chip).start)
            drained(N_DEV - 2)
            drained(N_DEV - 1)
            for chip in range(4):
                send(0, chip).wait_recv()

    half = jax.ShapeDtypeStruct((4, D, per), F32)
    return pl.pallas_call(
        body, name="in_proj_bwd_weights", grid=(steps,),
        in_specs=[pl.BlockSpec((D, tm), lambda i: (0, i)), pl.BlockSpec((tm, DINP), lambda i: (i, 0))] + SMALL_SPECS,
        out_specs=[ANY, ANY, pl.BlockSpec((SMALL_ROWS, D), lambda i: (0, 0))],
        out_shape=[half, half, jax.ShapeDtypeStruct((SMALL_ROWS, D), F32)],
        scratch_shapes=[pltpu.VMEM((D, DINP), F32), pltpu.VMEM((2, D, per), F32), pltpu.SemaphoreType.DMA((2,)),
                        pltpu.SemaphoreType.DMA((2,)), pltpu.SemaphoreType.DMA((4,))] + _small_sum_scratch(),
        compiler_params=_cp(("arbitrary",), VMEM_WIDE_MB),
    )(ut, dproj, *small)


def _adamw(w, g, m, v):
    m = ADAM_B1 * m + (1.0 - ADAM_B1) * g
    v = ADAM_B2 * v + (1.0 - ADAM_B2) * jnp.square(g)
    m_hat = m / (1.0 - ADAM_B1 ** ADAM_STEP)
    v_hat = v / (1.0 - ADAM_B2 ** ADAM_STEP)
    delta = -ADAM_LR * (m_hat / (jnp.sqrt(v_hat) + ADAM_EPS) + ADAM_WD * w)
    return delta, m, v


ADAM_STEPS = 8


def _adamw_shards(items, name, jobs=None):
    jobs = jobs or _Jobs([])
    ns, nw = jobs.n, len(items)

    def body(*rest):
        ins, rest = rest[:5 * nw], rest[5 * nw:]
        job_ins, rest = rest[:ns], rest[ns:]
        outs, rest = rest[:4 * nw], rest[4 * nw:]
        start, finish = jobs.bind(job_ins, rest[:ns], rest[ns:])
        i = pl.program_id(0)
        pl.when(i == 0)(start)
        pl.when(i == ADAM_STEPS - 1)(finish)
        for k in range(nw):
            p_ref, own_ref, w_ref, m_ref, v_ref = ins[5 * k:5 * k + 5]
            g_ref, d_ref, nm_ref, nv_ref = outs[4 * k:4 * k + 4]
            g = ((p_ref[0].astype(F32) + p_ref[1].astype(F32)) + p_ref[2].astype(F32)) + own_ref[...]
            g_ref[...] = g
            d_ref[...], nm_ref[...], nv_ref[...] = _adamw(w_ref[...], g, m_ref[...], v_ref[...])

    in_specs, out_specs, out_shape, operands = [], [], [], []
    for parts, own, w, m, v in items:
        r, cdim = w.shape
        tr = r // ADAM_STEPS
        spec = pl.BlockSpec((tr, cdim), lambda i: (i, 0))
        in_specs += [pl.BlockSpec((3, tr, cdim), lambda i: (0, i, 0)), spec, spec, spec, spec]
        out_specs += [spec] * 4
        out_shape += [jax.ShapeDtypeStruct((r, cdim), F32)] * 4
        operands += [parts, own, w, m, v]
    outs = pl.pallas_call(
        body, name=name, grid=(ADAM_STEPS,),
        in_specs=in_specs + [ANY] * ns, out_specs=out_specs + [ANY] * ns, scratch_shapes=jobs.sems,
        out_shape=out_shape + jobs.out_shapes,
        compiler_params=_cp(("arbitrary",)),
    )(*operands, *jobs.inputs)
    return [outs[4 * k:4 * k + 4] for k in range(nw)], jobs.split(outs[4 * nw:])


def _adamw_small(items):
    n = len(items)

    def body(*refs):
        ins, outs = refs[:4 * n], refs[4 * n:]
        for k in range(n):
            w_ref, g_ref, m_ref, v_ref = ins[4 * k:4 * k + 4]
            d_ref, nm_ref, nv_ref = outs[3 * k:3 * k + 3]
            d_ref[...], nm_ref[...], nv_ref[...] = _adamw(w_ref[...], g_ref[...], m_ref[...], v_ref[...])

    vm = pl.BlockSpec(memory_space=pltpu.VMEM)
    shapes = [jax.ShapeDtypeStruct(w.shape, F32) for w, _, _, _ in items for _ in range(3)]
    outs = pl.pallas_call(body, name="adamw_small", in_specs=[vm] * (4 * n), out_specs=[vm] * (3 * n),
                          out_shape=shapes)(*[t for item in items for t in item])
    return [outs[3 * k:3 * k + 3] for k in range(n)]


def _pair_sums(where, mine, theirs, name):
    _, r, cdim = theirs.shape
    tr = 128 if r % 128 == 0 else r

    def body(where_ref, a_ref, b_ref, own_ref, wire_ref):
        chip = where_ref[1]
        own_ref[...] = a_ref[chip] + b_ref[chip]
        wire_ref[...] = (a_ref[...] + b_ref[...]).astype(WIRE_DTYPE)

    spec = pl.BlockSpec((4, tr, cdim), lambda i, s: (0, i, 0))
    mine_spec = spec if mine.ndim == 3 else pl.BlockSpec((None, 4, tr, cdim), lambda i, s: (s[0], 0, i, 0))
    return pl.pallas_call(
        body, name=name,
        grid_spec=pltpu.PrefetchScalarGridSpec(
            num_scalar_prefetch=1, grid=(r // tr,), in_specs=[mine_spec, spec],
            out_specs=[pl.BlockSpec((tr, cdim), lambda i, s: (i, 0)), spec]),
        out_shape=[jax.ShapeDtypeStruct((r, cdim), F32), jax.ShapeDtypeStruct(theirs.shape, WIRE_DTYPE)],
        compiler_params=_cp(("arbitrary",)))(where, mine, theirs)


def kernel(x, meta_tokens, norm_mix_w, w_in, w_gate_up, b_gate, gla_norm_w, sinks, w_out, norm_ff_w, w_ff1, w_ff2, final_norm_w, loss_target, m_meta_tokens, m_norm_mix_w, m_w_in, m_w_gate_up, m_b_gate, m_gla_norm_w, m_sinks, m_w_out, m_norm_ff_w, m_w_ff1, m_w_ff2, m_final_norm_w, v_meta_tokens, v_norm_mix_w, v_w_in, v_w_gate_up, v_b_gate, v_gla_norm_w, v_sinks, v_w_out, v_norm_ff_w, v_w_ff1, v_w_ff2, v_final_norm_w):
    seq = x.shape[1]
    rows = LEAD + seq
    tm = _row_tile(rows)
    tm_wide = WIDE_ROW_TILE if rows % WIDE_ROW_TILE == 0 else tm
    dev =4 * lax.axis_index("x") + 2 * lax.axis_index("y") + lax.axis_index("c")

    small_shard = jnp.concatenate([meta_tokens, w_gate_up[0], jnp.zeros((N_META, 96), F32)], axis=1)
    g_in, g_small = _all_gather([w_in[0].astype(WIRE_DTYPE), small_shard])
    later_shards = [w_out[0].astype(WIRE_DTYPE), w_ff1[0].astype(WIRE_DTYPE), w_ff2[0].astype(WIRE_DTYPE)]
    win_p = _win_padded(g_in)
    meta_full = jnp.transpose(g_small[:, :, 0:128], (1, 0, 2)).reshape(N_META, D)
    wg_full = jnp.transpose(g_small[:, :, 128:160], (1, 0, 2)).reshape(GLA_RANK, GLA_HEADS * GLA_DK)
    wg_p = jnp.concatenate([wg_full, jnp.zeros((128 - GLA_RANK, 256), F32)], axis=0)

    lead = jnp.concatenate([jnp.zeros((META0, D), F32), meta_full], axis=0)
    angles = _rope_angles(rows)
    proj, qr, kr, vr, (g_w1,) = _in_proj(x[0], lead, norm_mix_w, win_p, angles, tm, later_shards[1:2])
    oraw, og, states, decay, dgate, (g_w2,) = _gla_fwd(proj, wg_p, b_gate, gla_norm_w, later_shards[2:3])
    osw, (g_out,) = _swa_fwd(qr, kr, vr, sinks, later_shards[0:1])
    wout_full = g_out.reshape(D, D)
    w2_full = g_w2.reshape(D_FF, D)
    w1_full = jnp.transpose(g_w1, (1, 0, 2)).reshape(D, D_FF)
    h1, f, ft = _out_proj(x[0], lead, og, osw, wout_full, norm_ff_w, tm)
    a, dh2, dh2t, loss_p, gfn_p = _ffn_fwd(f, h1, w1_full, w2_full, loss_target[0], final_norm_w.reshape(1, D), tm)

    da, dh1, gnf_p = _ffn_bwd_act(dh2, a, w1_full, w2_full, h1, norm_ff_w, tm)
    dw1, dw2 = _ffn_bwd_weights(ft, a, da, dh2t, tm_wide)
    where = jnp.stack([lax.axis_index("c"), 2 * lax.axis_index("x") + lax.axis_index("y")]).astype(jnp.int32)
    dog, dos, dwout, theirs_ffn = _out_proj_bwd(dh1, og, osw, wout_full, tm, [dw1, dw2])
    pairs_ffn = [_pair_sums(where, p, q, "reduce_pair_%d" % (2 + k))
                 for k, (p, q) in enumerate(zip([dw1, dw2], theirs_ffn))]
    sums_ffn, wires_ffn = [p[0] for p in pairs_ffn], [p[1] for p in pairs_ffn]
    dsq, dsk, dsv, dsink_p, (parts_ffn, (theirs_wout,)) = _swa_bwd(
        qr, kr, vr, osw, dos, sinks, _Jobs([("chips", wires_ffn), ("sibling", [dwout])]))
    sum_wout, wire_wout = _pair_sums(where, dwout, theirs_wout, "reduce_pair_1")
    (dgq, dgk, dgv, dgr, dlr, dwg_p, dbg_p, dgnw_p), ((parts_wout,),) = _gla_bwd(
        proj, decay, dgate, oraw, states, dog, wg_p, gla_norm_w, _Jobs([("chips", [wire_wout])]))
    grad_x, dlead, dproj, ut, gnm_p = _in_proj_bwd(x[0], lead, dh1, norm_mix_w, win_p, dgv, dgr, dsq, dgq, dgk, dsk,
                                                   dsv, dlr, angles, tm)
    grad_x = grad_x[None]
    dwin_mine, dwin_theirs, total = _in_proj_bwd_weights(
        ut, dproj, tm_wide, [dlead, dwg_p, gnm_p, gnf_p, gfn_p, dbg_p, dgnw_p, loss_p, dsink_p])
    sum_win, sum_win_wire = _pair_sums(where, dwin_mine, dwin_theirs, "reduce_pair_0")

    g_meta = lax.dynamic_slice(total, (R_META, dev * 128), (N_META, 128))
    g_wg = lax.dynamic_slice(total, (R_WG, dev * 32), (GLA_RANK, 32))
    g_norm_mix, g_norm_ff = total[R_NORM_MIX:R_NORM_MIX + 1], total[R_NORM_FF:R_NORM_FF + 1]
    g_final_norm = total[R_FINAL:R_FINAL + 1]
    g_b_gate, g_gla_norm = total[R_B_GATE:R_B_GATE + 1, 0:256], total[R_GLA_NORM:R_GLA_NORM + 1, 0:128]
    g_sinks = total[R_SINKS:R_SINKS + SWA_HEADS, 0].reshape(1, SWA_HEADS)
    loss = total[R_LOSS, 0]

    ((g_wout, d_wout, nm_wout, nv_wout), (g_w1s, d_w1, nm_w1, nv_w1), (g_w2s, d_w2, nm_w2, nv_w2)), ((parts_win,),) = \
        _adamw_shards([(parts_wout, sum_wout, w_out[0], m_w_out[0], v_w_out[0]),
                       (parts_ffn[0], sums_ffn[0], w_ff1[0], m_w_ff1[0], v_w_ff1[0]),
                       (parts_ffn[1], sums_ffn[1], w_ff2[0], m_w_ff2[0], v_w_ff2[0])],
                      "adamw_w_out_ff", _Jobs([("chips", [sum_win_wire])]))
    ((g_win, d_win, nm_win, nv_win),), _ = _adamw_shards(
        [(parts_win, sum_win, w_in[0], m_w_in[0], v_w_in[0])], "adamw_w_in")

    names = ["meta", "wg", "norm_mix", "b_gate", "gla_norm", "sinks", "norm_ff", "final_norm"]
    ws = [meta_tokens, w_gate_up, norm_mix_w, b_gate, gla_norm_w, sinks, norm_ff_w, final_norm_w]
    gs = [g_meta, g_wg, g_norm_mix, g_b_gate, g_gla_norm, g_sinks, g_norm_ff, g_final_norm]
    ms = [m_meta_tokens, m_w_gate_up, m_norm_mix_w, m_b_gate, m_gla_norm_w, m_sinks, m_norm_ff_w, m_final_norm_w]
    vs = [v_meta_tokens, v_w_gate_up, v_norm_mix_w, v_b_gate, v_gla_norm_w, v_sinks, v_norm_ff_w, v_final_norm_w]
    flat = lambda t: t.reshape(-1, t.shape[-1])
    small_out = _adamw_small([(flat(w), flat(g), flat(m), flat(v)) for w, g, m, v in zip(ws, gs, ms, vs)])
    d_small = {n: small_out[k][0].reshape(ws[k].shape) for k, n in enumerate(names)}
    nm_small = {n: small_out[k][1].reshape(ws[k].shape) for k, n in enumerate(names)}
    nv_small = {n: small_out[k][2].reshape(ws[k].shape) for k, n in enumerate(names)}
    g_small_d = {n: g.reshape(ws[k].shape) for k, (n, g) in enumerate(zip(names, gs))}

    def ordered(big, small_d):
        win_v, wout_v, w1_v, w2_v = big
        return (small_d["meta"], small_d["norm_mix"], win_v[None], small_d["wg"], small_d["b_gate"],
                small_d["gla_norm"], small_d["sinks"], wout_v[None], small_d["norm_ff"], w1_v[None], w2_v[None],
                small_d["final_norm"])

    return (loss, grad_x,
            *ordered((g_win, g_wout, g_w1s, g_w2s), g_small_d),
            *ordered((d_win, d_wout, d_w1, d_w2), d_small),
            *ordered((nm_win, nm_wout, nm_w1, nm_w2), nm_small),
            *ordered((nv_win, nv_wout, nv_w1, nv_w2), nv_small))
```

```python
import functools

import jax
import jax.numpy as jnp
from jax import lax
from jax.experimental import pallas as pl
from jax.experimental.pallas import tpu as pltpu

F32 = jnp.float32
MXU_DTYPE = jnp.bfloat16
ACT_DTYPE = jnp.bfloat16
WIRE_DTYPE = jnp.bfloat16

D = 1024
N_META = 16
LEAD = 128
META0 = LEAD - N_META
EPS = 1e-5
GLA_HEADS, GLA_DK, GLA_DV, GLA_RANK, GLA_CHUNK = 4, 64, 128, 16, 64
GLA_TAU = 16.0
SWA_HEADS, SWA_KV, SWA_GROUP, SWA_HD, SWA_BLOCK = 8, 2, 4, 64, 128
ROPE_DIM, ROPE_THETA = 16, 500000.0
D_FF = 4096
N_DEV = 8
FF_TILE = D_FF // N_DEV
FF_WIDE = 2048
NEG = -1e30

C_GV, C_GR, C_GQ, C_GK, C_LR, C_SQ, C_SK, C_SV = 0, 512, 1024, 1280, 1536, 1664, 2176, 2304
DGLA = 1664
DINP = 2432
DIN = 2320
O_GQ, O_GK, O_GV, O_GR, O_LR, O_SQ, O_SK, O_SV = (0, 256), (256, 512), (512, 1024), (1024, 1536), (1536, 1552), (1552, 2064), (2064, 2192), (2192, 2320)

ADAM_LR, ADAM_B1, ADAM_B2, ADAM_EPS, ADAM_WD, ADAM_STEP = 0.001, 0.9, 0.999, 1e-08, 0.01, 10

MESH = pl.DeviceIdType.MESH
LOCAL_COPY_PRIORITY = 1
ANY = pl.BlockSpec(memory_space=pl.ANY)
VMEM_TILE_MB, VMEM_WIDE_MB = 48, 56


def _cp(sem=None, vmem_mb=None):
    kw = {}
    if sem is not None:
        kw["dimension_semantics"] = sem
    if vmem_mb is not None:
        kw["vmem_limit_bytes"] = vmem_mb << 20
    return pltpu.CompilerParams(**kw)


def _mm(a, b):
    return jnp.dot(a.astype(MXU_DTYPE), b.astype(MXU_DTYPE), preferred_element_type=F32)


def _mm_nt(a, b):
    return lax.dot_general(a.astype(MXU_DTYPE), b.astype(MXU_DTYPE), (((1,), (1,)), ((), ())),
                           preferred_element_type=F32)


def _mm_tn(a, b):
    return lax.dot_general(a.astype(MXU_DTYPE), b.astype(MXU_DTYPE), (((0,), (0,)), ((), ())),
                           preferred_element_type=F32)


def _masked_sums(mask, t):
    m = mask.astype(jnp.bfloat16)
    hi = t.astype(jnp.bfloat16)
    rest = t - hi.astype(F32)
    mid = rest.astype(jnp.bfloat16)
    low = (rest - mid.astype(F32)).astype(jnp.bfloat16)
    dot = lambda part: jnp.dot(m, part, preferred_element_type=F32)
    return dot(hi) + (dot(mid) + dot(low))


def _logsigmoid(z):
    return jnp.minimum(z, 0.0) - jnp.log(1.0 + jnp.exp(-jnp.abs(z)))


def _sigmoid(z):
    return 1.0 / (1.0 + jnp.exp(-z))


ROW_TILE, WIDE_ROW_TILE = 640, 1664


def _row_tile(rows, want=ROW_TILE):
    return want if rows % want == 0 else LEAD


def _mesh_pos():
    return lax.axis_index("x"), lax.axis_index("y"), lax.axis_index("c")


def _all_gather(shards):
    n = len(shards)

    def body(*refs):
        start, forward, finish = _gather_schedule(refs[:n], refs[n:2 * n], *refs[2 * n:])
        start()
        for j in range(3):
            forward(j)
        finish()

    gathered = pl.pallas_call(
        body, name="all_gather_weights",
        out_shape=_gathered_shapes(shards), in_specs=[ANY] * n, out_specs=[ANY] * n,
        scratch_shapes=_gather_sems(n),
    )(*shards)
    return _with_own_block(gathered, shards)


def _gathered_shapes(shards):
    return [jax.ShapeDtypeStruct((N_DEV,) + s.shape, s.dtype) for s in shards]


def _gather_sems(n):
    return [pltpu.SemaphoreType.DMA((7 * n,)), pltpu.SemaphoreType.DMA((7 * n,))] if n else []


def _place_gather(step, steps, shard_refs, gathered_refs, sems):
    if not shard_refs:
        return
    start, forward, finish = _gather_schedule(shard_refs, gathered_refs, *sems)
    pl.when(step == 0)(start)
    for j, at in enumerate((steps * 7 // 10, steps * 8 // 10, steps * 9 // 10)):
        pl.when(step == at)(functools.partial(forward, j))
    pl.when(step == steps - 1)(finish)


def _with_own_block(gathered, shards):
    dev = 4 * lax.axis_index("x") + 2 * lax.axis_index("y") + lax.axis_index("c")
    return [lax.dynamic_update_index_in_dim(g, s, dev, 0) for g, s in zip(gathered, shards)]


def _gather_schedule(ins, outs, send_sems, recv_sems):
    n = len(ins)
    x, y, c = _mesh_pos()
    me, sibling = (x, y, c), (x, y, 1 - c)
    chips = [(1 - x, y), (x, 1 - y), (1 - x, 1 - y)]

    def copy(a, k, block, to, src=None):
        dst = outs[a].at[4 * block[0] + 2 * block[1] + block[2]]
        return pltpu.make_async_remote_copy(
            src_ref=dst if src is None else src, dst_ref=dst,
            send_sem=send_sems.at[a * 7 + k], recv_sem=recv_sems.at[a * 7 + k],
            device_id=to, device_id_type=MESH)

    def first(a):
        return [copy(a, 0, me, sibling, src=ins[a])] + [copy(a, 1 + j, me, (*chip, c), src=ins[a])
                                                        for j, chip in enumerate(chips)]

    def start():
        for a in range(n):
            for cp in first(a):
                cp.start()

    def forward(j):
        for a in range(n):
            copy(a, 1 + j, (*chips[j], c), me).wait_recv()
            copy(a, 4 + j, (*chips[j], c), sibling).start()

    def finish():
        for a in range(n):
            copy(a, 0, sibling, me).wait_recv()
            for j, chip in enumerate(chips):
                copy(a, 4 + j, (*chip, 1 - c), me).wait_recv()
        for a in range(n):
            for cp in first(a) + [copy(a, 4 + j, (*chip, c), sibling) for j, chip in enumerate(chips)]:
                cp.wait_send()

    return start, forward, finish


def _sibling_shapes(gs):
    return [jax.ShapeDtypeStruct(g.shape[1:], g.dtype) for g in gs]


def _sibling_sems(n):
    return [pltpu.SemaphoreType.DMA((n,)), pltpu.SemaphoreType.DMA((n,))]


def _sibling_schedule(ins, land, send_sems, recv_sems):
    x, y, c = _mesh_pos()

    def copies():
        return [pltpu.make_async_remote_copy(
            src_ref=ins[a].at[1 - c], dst_ref=land[a], send_sem=send_sems.at[a], recv_sem=recv_sems.at[a],
            device_id=(x, y, 1 - c), device_id_type=MESH) for a in range(len(ins))]

    def start():
        for cp in copies():
            cp.start()

    def finish():
        for cp in copies():
            cp.wait_recv()
        for cp in copies():
            cp.wait_send()

    return start, finish


def _chips_shapes(ps):
    return [jax.ShapeDtypeStruct((3,) + p.shape[1:], p.dtype) for p in ps]


def _chips_sems(n):
    return [pltpu.SemaphoreType.DMA((3 * n,)), pltpu.SemaphoreType.DMA((3 * n,))]


def _chips_schedule(ins, land, send_sems, recv_sems):
    x, y, c = _mesh_pos()
    chips = [(1 - x, y), (x, 1 - y), (1 - x, 1 - y)]

    def copies():
        return [pltpu.make_async_remote_copy(
            src_ref=ins[a].at[2 * chip[0] + chip[1]], dst_ref=land[a].at[j],
            send_sem=send_sems.at[3 * a + j], recv_sem=recv_sems.at[3 * a + j],
            device_id=(*chip, c), device_id_type=MESH) for a in range(len(ins)) for j, chip in enumerate(chips)]

    def start():
        for cp in copies():
            cp.start()

    def finish():
        for cp in copies():
            cp.wait_recv()
        for cp in copies():
            cp.wait_send()

    return start, finish


class _Jobs:
    def __init__(self, jobs):
        self.jobs = jobs
        self.inputs = [a for _, arrs in jobs for a in arrs]
        self.out_shapes = [s for kind, arrs in jobs
                           for s in (_sibling_shapes(arrs) if kind == "sibling" else _chips_shapes(arrs))]
        self.sems = [s for kind, arrs in jobs
                     for s in (_sibling_sems(len(arrs)) if kind == "sibling" else _chips_sems(len(arrs)))]
        self.n = len(self.inputs)

    def bind(self, in_refs, out_refs, sem_refs):
        starts, finishes, at = [], [], 0
        for k, (kind, arrs) in enumerate(self.jobs):
            schedule = _sibling_schedule if kind == "sibling" else _chips_schedule
            start, finish = schedule(in_refs[at:at + len(arrs)], out_refs[at:at + len(arrs)],
                                     sem_refs[2 * k], sem_refs[2 * k + 1])
            starts.append(start)
            finishes.append(finish)
            at += len(arrs)

        def start_all():
            for f in starts:
                f()

        def finish_all():
            for f in finishes:
                f()

        return start_all, finish_all

    def split(self, outs):
        res, at = [], 0
        for _, arrs in self.jobs:
            res.append(list(outs[at:at + len(arrs)]))
            at += len(arrs)
        return res


R_META, R_WG, R_NORM_MIX, R_NORM_FF, R_FINAL, R_B_GATE, R_GLA_NORM, R_LOSS, R_SINKS, SMALL_ROWS = 0, 16, 32, 33, 34, 35, 36, 37, 40, 48


SMALL_SPECS = [pl.BlockSpec((LEAD, D), lambda i: (0, 0)), pl.BlockSpec((128, 256), lambda i: (0, 0)),
               pl.BlockSpec((8, D), lambda i: (0, 0)), pl.BlockSpec((8, D), lambda i: (0, 0)),
               pl.BlockSpec((8, D), lambda i: (0, 0)), pl.BlockSpec((8, 256), lambda i: (0, 0)),
               pl.BlockSpec((8, 128), lambda i: (0, 0)), pl.BlockSpec((8, 128), lambda i: (0, 0)),
               pl.BlockSpec((8, 128), lambda i: (0, 0))]


def _small_sum_scratch():
    return [pltpu.VMEM((SMALL_ROWS, D), F32), pltpu.VMEM((N_DEV, SMALL_ROWS, D), F32),
            pltpu.SemaphoreType.DMA((7,)), pltpu.SemaphoreType.DMA((7,))]


def _small_sum_schedule(small_refs, out_ref, p_ref, land, send_sems, recv_sems):
    dlead_ref, dwg_ref, gnm_ref, gnf_ref, gfn_ref, dbg_ref, dgnw_ref, loss_ref, dsink_ref = small_refs
    x, y, c = _mesh_pos()
    me = 4 * x + 2 * y + c

    def copies():
        res = []
        for k in range(1, N_DEV):
            bx, by, bc = (k >> 2) & 1, (k >> 1) & 1, k & 1
            peer = (1 - x if bx else x, 1 - y if by else y, 1 - c if bc else c)
            res.append(pltpu.make_async_remote_copy(
                src_ref=p_ref, dst_ref=land.at[me], send_sem=send_sems.at[k - 1], recv_sem=recv_sems.at[k - 1],
                device_id=peer, device_id_type=MESH))
        return res

    def start():
        p_ref[...] = jnp.zeros_like(p_ref)
        p_ref[R_META:R_META + N_META, :] = dlead_ref[META0:LEAD, :]
        p_ref[R_WG:R_WG + GLA_RANK, 0:256] = dwg_ref[0:GLA_RANK, :]
        p_ref[R_NORM_MIX:R_NORM_MIX + 1, :] = gnm_ref[0:1, :]
        p_ref[R_NORM_FF:R_NORM_FF + 1, :] = gnf_ref[0:1, :]
        p_ref[R_FINAL:R_FINAL + 1, :] = gfn_ref[0:1, :]
        p_ref[R_B_GATE:R_B_GATE + 1, 0:256] = dbg_ref[0:1, :]
        p_ref[R_GLA_NORM:R_GLA_NORM + 1, 0:128] = dgnw_ref[0:1, :]
        p_ref[R_LOSS:R_LOSS + 1, 0:128] = loss_ref[0:1, :]
        p_ref[R_SINKS:R_SINKS + SWA_HEADS, 0:128] = dsink_ref[...]
        land[me] = p_ref[...]
        for cp in copies():
            cp.start()

    def finish():
        for cp in copies():
            cp.wait_recv()
        for cp in copies():
            cp.wait_send()
        acc = land[0]
        for d in range(1, N_DEV):
            acc = acc + land[d]
        out_ref[...] = acc

    return start, finish


def _token_specs(tm, grid_rank=1):
    nb = tm // LEAD

    def spec(k):
        if grid_rank == 1:
            return pl.BlockSpec((LEAD, D), lambda i: (jnp.maximum(i * nb + k - 1, 0), 0))
        return pl.BlockSpec((LEAD, D), lambda i, j: (jnp.maximum(i * nb + k - 1, 0), 0))

    return [spec(k) for k in range(nb)]


def _h_tile(i, lead_ref, x_refs):
    first = jnp.where(i == 0, lead_ref[...], x_refs[0][...])
    return jnp.concatenate([first] + [r[...] for r in x_refs[1:]], axis=0)


def _in_proj(x, lead, nw, win_p, angles, tm, shards):
    rows = LEAD + x.shape[0]
    nb = tm // LEAD
    steps = rows // tm
    ns = len(shards)

    def body(*refs):
        x_refs, refs = refs[:nb], refs[nb:]
        lead_ref, nw_ref, w_ref, cs_ref = refs[:4]
        shard_refs, (o_ref, q_ref, k_ref, v_ref) = refs[4:4 + ns], refs[4 + ns:8 + ns]
        _place_gather(pl.program_id(0), steps, shard_refs, refs[8 + ns:8 + 2 * ns], refs[8 + 2 * ns:])
        h = _h_tile(pl.program_id(0), lead_ref, x_refs)
        rstd = lax.rsqrt(jnp.mean(h * h, axis=-1, keepdims=True) + EPS)
        u = (h * rstd * nw_ref[...]).astype(MXU_DTYPE)
        proj = jnp.dot(u, w_ref[...].astype(MXU_DTYPE), preferred_element_type=F32)
        o_ref[...] = proj[:, 0:DGLA]
        cos, sa, sb = _rope_tables(cs_ref[...])
        q_ref[...] = (_rope(proj[:, C_SQ:C_SK], cos, sa, sb) * (SWA_HD ** -0.5)).astype(ACT_DTYPE)
        k_ref[...] = _rope(proj[:, C_SK:C_SV], cos, sa, sb).astype(ACT_DTYPE)
        v_ref[...] = proj[:, C_SV:DINP].astype(ACT_DTYPE)

    row = lambda w: pl.BlockSpec((tm, w), lambda i: (i, 0))
    outs = pl.pallas_call(
        body, name="in_proj", grid=(steps,),
        in_specs=_token_specs(tm) + [pl.BlockSpec((LEAD, D), lambda i: (0, 0)), pl.BlockSpec((1, D), lambda i: (0, 0)),
                                     pl.BlockSpec((D, DINP), lambda i: (0, 0)), row(ROPE_DIM)]
        + [ANY] * ns,
        out_specs=[row(DGLA), row(512), row(128), row(128)] + [ANY] * ns,
        out_shape=[jax.ShapeDtypeStruct((rows, DGLA), F32), jax.ShapeDtypeStruct((rows, 512), ACT_DTYPE),
                   jax.ShapeDtypeStruct((rows, 128), ACT_DTYPE), jax.ShapeDtypeStruct((rows, 128), ACT_DTYPE)]
        + _gathered_shapes(shards),
        scratch_shapes=_gather_sems(ns),
        compiler_params=_cp(("arbitrary",), VMEM_WIDE_MB),
    )(*([x] * nb), lead, nw, win_p, angles, *shards)
    return outs[0], outs[1], outs[2], outs[3], _with_own_block(outs[4:], shards)


def _rope_angles(rows):
    pos = (jnp.arange(rows, dtype=jnp.int32) - META0).astype(F32)
    inv_freq = 1.0 / (ROPE_THETA ** (jnp.arange(0, ROPE_DIM, 2, dtype=F32) / ROPE_DIM))
    ang = pos[:, None] * inv_freq[None, :]
    return jnp.concatenate([jnp.cos(ang), jnp.sin(ang)], axis=1)


def _rope_tables(cs):
    shape = (2 * (ROPE_DIM // 2), 3 * 128)
    j = lax.broadcasted_iota(jnp.int32, shape, 0)
    col = lax.broadcasted_iota(jnp.int32, shape, 1)
    table, in_head = col // 128, col % SWA_HD
    match = (j % (ROPE_DIM // 2)) == (in_head % (ROPE_DIM // 2))
    is_cos = j < ROPE_DIM // 2
    first, second = in_head < ROPE_DIM // 2, (in_head >= ROPE_DIM // 2) & (in_head < ROPE_DIM)
    spread = (jnp.where(match & is_cos & (table == 0) & (first | second), 1.0, 0.0)
              + jnp.where(match & ~is_cos & (table == 1) & first, -1.0, 0.0)
              + jnp.where(match & ~is_cos & (table == 2) & second, 1.0, 0.0)).astype(jnp.bfloat16)
    hi = cs.astype(jnp.bfloat16)
    rest = cs - hi.astype(F32)
    mid = rest.astype(jnp.bfloat16)
    low = (rest - mid.astype(F32)).astype(jnp.bfloat16)
    dot = lambda part: jnp.dot(part, spread, preferred_element_type=F32)
    tabs = dot(hi) + (dot(mid) + dot(low))
    lane = lax.broadcasted_iota(jnp.int32, (1, 128), 1) % SWA_HD
    return tabs[:, 0:128] + jnp.where(lane >= ROPE_DIM, 1.0, 0.0), tabs[:, 128:256], tabs[:, 256:384]


def _rope(xv, cos, sa, sb):
    width = xv.shape[1]
    reps = width // 128
    if reps > 1:
        cos, sa, sb = (jnp.tile(t, (1, reps)) for t in (cos, sa, sb))
    return xv * cos + pltpu.roll(xv, width - 8, 1) * sa + pltpu.roll(xv, 8, 1) * sb


def _unrope(dy, cos, sa, sb):
    width = dy.shape[1]
    reps = width // 128
    if reps > 1:
        cos, sa, sb = (jnp.tile(t, (1, reps)) for t in (cos, sa, sb))
    return dy * cos + pltpu.roll(dy * sa, 8, 1) + pltpu.roll(dy * sb, width - 8, 1)


def _gla_group(nc, most):
    for g in (10, 5, 2):
        if g <= most and nc % g == 0:
            return g
    return 1


def _chunk_masks(nrows):
    ii = lax.broadcasted_iota(jnp.int32, (nrows, nrows), 0)
    jj = lax.broadcasted_iota(jnp.int32, (nrows, nrows), 1)
    same = (ii // GLA_CHUNK) == (jj // GLA_CHUNK)
    return same & (jj <= ii), same & (jj >= ii)


def _gla_gates(lr, wg, bg, first_row):
    nrows = lr.shape[0]
    zg = _mm(lr, wg) + bg
    live = first_row + lax.broadcasted_iota(jnp.int32, (nrows, 1), 0) >= META0
    g = jnp.where(live, _logsigmoid(zg) * (1.0 / GLA_TAU), 0.0)
    return _masked_sums(_chunk_masks(nrows)[0], g), jnp.where(live, _sigmoid(-zg) * (1.0 / GLA_TAU), 0.0)


def _tril64():
    ii = lax.broadcasted_iota(jnp.int32, (GLA_CHUNK, GLA_CHUNK), 0)
    jj = lax.broadcasted_iota(jnp.int32, (GLA_CHUNK, GLA_CHUNK), 1)
    return jj <= ii


def _gla_fwd(proj, wg_p, bg, gnw, shards):
    rows = proj.shape[0]
    nc = rows // GLA_CHUNK
    group = _gla_group(nc, 10)
    steps, nrows = nc // group, group * GLA_CHUNK
    ns = len(shards)

    def body(q_ref, k_ref, v_ref, r_ref, lr_ref, lr_next_ref, wg_ref, bg_ref, gnw_ref, *rest):
        shard_refs, rest = rest[:ns], rest[ns:]
        oraw_ref, og_ref, st_ref, decay_ref, dgate_ref = rest[:5]
        gathered_refs, rest = rest[5:5 + ns], rest[5 + ns:]
        state, gates = rest[:2]
        c = pl.program_id(0)

        @pl.when(c == 0)
        def _():
            state[...] = jnp.zeros_like(state)
            gates[0, 0], gates[0, 1] = _gla_gates(lr_ref[...], wg_ref[...], bg_ref[...], 0)

        _place_gather(c, steps, shard_refs, gathered_refs, rest[2:])
        slot = c % 2
        b = gates[slot, 0]
        decay_ref[...] = b
        dgate_ref[...] = gates[slot, 1]
        gates[1 - slot, 0], gates[1 - slot, 1] = _gla_gates(lr_next_ref[...], wg_ref[...], bg_ref[...], (c + 1) * nrows)
        eb = jnp.exp(b)
        gq = q_ref[...] * (GLA_DK ** -0.5) * eb
        gk = k_ref[...] * jnp.exp(-b)
        v = v_ref[...]
        gnw_v = gnw_ref[...]
        tril = _tril64()
        pairs = [(h, gi) for h in range(GLA_HEADS) for gi in range(group)]
        rs = {gi: slice(gi * GLA_CHUNK, (gi + 1) * GLA_CHUNK) for gi in range(group)}
        s64 = {h: slice(h * GLA_DK, (h + 1) * GLA_DK) for h in range(GLA_HEADS)}
        s128 = {h: slice(h * GLA_DV, (h + 1) * GLA_DV) for h in range(GLA_HEADS)}
        qh = {(h, gi): gq[rs[gi], s64[h]] for h, gi in pairs}
        kh = {(h, gi): gk[rs[gi], s64[h]] for h, gi in pairs}
        vh = {(h, gi): v[rs[gi], s128[h]] for h, gi in pairs}
        ebl = {(h, gi): eb[(gi + 1) * GLA_CHUNK - 1:(gi + 1) * GLA_CHUNK, s64[h]] for h, gi in pairs}
        av = {pr: _mm(jnp.where(tril, _mm_nt(qh[pr], kh[pr]), 0.0), vh[pr]) for pr in pairs}
        inc = {pr: _mm_tn(vh[pr], kh[pr] * ebl[pr]) for pr in pairs}
        st = {}
        for h in range(GLA_HEADS):
            cur = state[h]
            for gi in range(group):
                st[h, gi] = cur
                st_ref[gi, h] = cur
                cur = cur * ebl[h, gi] + inc[h, gi]
            state[h] = cur
        for h, gi in pairs:
            o = av[h, gi] + _mm_nt(qh[h, gi], st[h, gi])
            oraw_ref[rs[gi], s128[h]] = o
            rstd = lax.rsqrt(jnp.mean(o * o, axis=-1, keepdims=True) + EPS)
            rh = r_ref[rs[gi], s128[h]]
            og_ref[rs[gi], s128[h]] = (o * rstd * gnw_v * (rh * _sigmoid(rh))).astype(ACT_DTYPE)

    nb = lambda w, col: pl.BlockSpec((nrows, w), lambda c: (c, col // w))
    const = lambda shape: pl.BlockSpec(shape, lambda c: (0,) * len(shape))
    outs = pl.pallas_call(
        body, name="gla_fwd", grid=(steps,),
        in_specs=[nb(256, C_GQ), nb(256, C_GK), nb(512, C_GV), nb(512, C_GR), nb(128, C_LR),
                  pl.BlockSpec((nrows, 128), lambda c: (jnp.minimum(c + 1, steps - 1), C_LR // 128)),
                  const((128, 256)), const((1, 256)), const((1, 128))] + [ANY] * ns,
        out_specs=[nb(512, 0), nb(512, 0),
                   pl.BlockSpec((group, GLA_HEADS, GLA_DV, GLA_DK), lambda c: (c, 0, 0, 0)),
                   nb(256, 0), nb(256, 0)] + [ANY] * ns,
        out_shape=[jax.ShapeDtypeStruct((rows, 512), F32), jax.ShapeDtypeStruct((rows, 512), ACT_DTYPE),
                   jax.ShapeDtypeStruct((nc, GLA_HEADS, GLA_DV, GLA_DK), F32),
                   jax.ShapeDtypeStruct((rows, 256), F32), jax.ShapeDtypeStruct((rows, 256), F32)]
        + _gathered_shapes(shards),
        scratch_shapes=[pltpu.VMEM((GLA_HEADS, GLA_DV, GLA_DK), F32), pltpu.VMEM((2, 2, nrows, 256), F32)]
        + _gather_sems(ns),
        compiler_params=_cp(("arbitrary",)),
    )(proj, proj, proj, proj, proj, proj, wg_p, bg, gnw, *shards)
    return outs[0], outs[1], outs[2], outs[3], outs[4], _with_own_block(outs[5:], shards)


def _swa_mask(n):
    shape = (SWA_GROUP * SWA_BLOCK, 3 * SWA_BLOCK)
    qi = lax.broadcasted_iota(jnp.int32, shape, 0) & (SWA_BLOCK - 1)
    jj = lax.broadcasted_iota(jnp.int32, shape, 1)
    meta = (jj < SWA_BLOCK) & (jj >= META0) & ((n > 0) | (jj <= qi))
    prev = (jj >= SWA_BLOCK) & (jj < 2 * SWA_BLOCK) & (n >= 2) & (jj - SWA_BLOCK > qi)
    cur = (jj >= 2 * SWA_BLOCK) & (n >= 1) & (jj - 2 * SWA_BLOCK <= qi)
    return meta | prev | cur


def _stack_heads(t, kvh):
    return jnp.concatenate([t[:, (kvh * SWA_GROUP + g) * SWA_HD:(kvh * SWA_GROUP + g + 1) * SWA_HD]
                            for g in range(SWA_GROUP)], axis=0)


def _stack_sinks(sink_ref, kvh):
    return jnp.concatenate([jnp.full((SWA_BLOCK, 1), sink_ref[0, kvh * SWA_GROUP + g], F32)
                            for g in range(SWA_GROUP)], axis=0)


def _swa_group(nblk):
    return 5 if nblk % 5 == 0 else 1


def _swa_specs(group):
    blk = lambda w: pl.BlockSpec((group * SWA_BLOCK, w), lambda n: (n, 0))
    first = pl.BlockSpec((SWA_BLOCK, 128), lambda n: (0, 0))
    prev = pl.BlockSpec((SWA_BLOCK, 128), lambda n: (jnp.maximum(n * group - 1, 0), 0))
    return blk, first, prev


def _swa_keys(first_ref, prev_ref, cur_ref, g):
    own = cur_ref[g * SWA_BLOCK:(g + 1) * SWA_BLOCK, :]
    before = prev_ref[...] if g == 0 else cur_ref[(g - 1) * SWA_BLOCK:g * SWA_BLOCK, :]
    return jnp.concatenate([first_ref[...], before, own], axis=0)


def _swa_fwd(qr, kr, vr, sinks, shards):
    rows = qr.shape[0]
    nblk = rows // SWA_BLOCK
    group = _swa_group(nblk)
    steps = nblk // group
    ns = len(shards)

    def body(q_ref, k0, kp, kc, v0, vp, vc, sink_ref, *rest):
        o_ref = rest[ns]
        _place_gather(pl.program_id(0), steps, rest[:ns], rest[ns + 1:2 * ns + 1], rest[2 * ns + 1:])
        for g in range(group):
            n = pl.program_id(0) * group + g
            rs = slice(g * SWA_BLOCK, (g + 1) * SWA_BLOCK)
            kall, vall = _swa_keys(k0, kp, kc, g), _swa_keys(v0, vp, vc, g)
            mask = _swa_mask(n)[0:SWA_BLOCK]
            heads = range(SWA_HEADS)
            hs = [slice(h * SWA_HD, (h + 1) * SWA_HD) for h in heads]
            kv = [slice((h // SWA_GROUP) * SWA_HD, (h // SWA_GROUP + 1) * SWA_HD) for h in heads]
            s = [jnp.where(mask, _mm_nt(q_ref[rs, hs[h]], kall[:, kv[h]]), NEG) for h in heads]
            m = [jnp.maximum(jnp.max(s[h], axis=-1, keepdims=True), sink_ref[0, h]) for h in heads]
            p = [jnp.exp(s[h] - m[h]) for h in heads]
            den = [jnp.sum(p[h], axis=-1, keepdims=True) + jnp.exp(sink_ref[0, h] - m[h]) for h in heads]
            o = [_mm(p[h], vall[:, kv[h]]) for h in heads]
            for h in heads:
                o_ref[rs, hs[h]] = (o[h] / den[h]).astype(ACT_DTYPE)

    blk, first, prev = _swa_specs(group)
    outs = pl.pallas_call(
        body, name="swa_fwd", grid=(steps,),
        in_specs=[blk(512), first, prev, blk(128), first, prev, blk(128),
                  pl.BlockSpec(memory_space=pltpu.SMEM)] + [ANY] * ns,
        out_specs=[blk(512)] + [ANY] * ns,
        out_shape=[jax.ShapeDtypeStruct((rows, 512), ACT_DTYPE)] + _gathered_shapes(shards),
        scratch_shapes=_gather_sems(ns),
        compiler_params=_cp(("arbitrary",)),
    )(qr, kr, kr, kr, vr, vr, vr, sinks, *shards)
    return outs[0], _with_own_block(outs[1:], shards)


def _out_proj(x, lead, og, osw, wout, nfw, tm):
    rows = LEAD + x.shape[0]
    nb = tm // LEAD

    def body(*refs):
        x_refs, (lead_ref, og_ref, os_ref, w_ref, nw_ref, h1_ref, f_ref, ft_ref) = refs[:nb], refs[nb:]
        h0 = _h_tile(pl.program_id(0), lead_ref, x_refs)
        h1 = h0 + _mm(og_ref[...], w_ref[0:512, :]) + _mm(os_ref[...], w_ref[512:1024, :])
        h1_ref[...] = h1
        rstd = lax.rsqrt(jnp.mean(h1 * h1, axis=-1, keepdims=True) + EPS)
        f = h1 * rstd * nw_ref[...]
        f_ref[...] = f.astype(ACT_DTYPE)
        ft_ref[...] = f.T.astype(ACT_DTYPE)

    row = lambda w: pl.BlockSpec((tm, w), lambda i: (i, 0))
    return pl.pallas_call(
        body, name="out_proj", grid=(rows // tm,),
        in_specs=_token_specs(tm) + [pl.BlockSpec((LEAD, D), lambda i: (0, 0)), row(512), row(512),
                                     pl.BlockSpec((D, D), lambda i: (0, 0)), pl.BlockSpec((1, D), lambda i: (0, 0))],
        out_specs=[row(D), row(D), pl.BlockSpec((D, tm), lambda i: (0, i))],
        out_shape=[jax.ShapeDtypeStruct((rows, D), F32), jax.ShapeDtypeStruct((rows, D), ACT_DTYPE),
                   jax.ShapeDtypeStruct((D, rows), ACT_DTYPE)],
        compiler_params=_cp(("arbitrary",), VMEM_TILE_MB),
    )(*([x] * nb), lead, og, osw, wout, nfw)


def _ffn_fwd(f, h1, w1, w2, tgt, fnw, tm):
    rows = f.shape[0]
    nj = D_FF // FF_WIDE
    nb = tm // LEAD

    def body(f_ref, h1_ref, w1_ref, w2_ref, nw_ref, *rest):
        t_refs, (a_ref, dh2_ref, dh2t_ref, loss_ref, gfn_ref, acc) = rest[:nb], rest[nb:]
        i, j = pl.program_id(0), pl.program_id(1)

        @pl.when((i == 0) & (j == 0))
        def _():
            loss_ref[...] = jnp.zeros_like(loss_ref)
            gfn_ref[...] = jnp.zeros_like(gfn_ref)

        @pl.when(j == 0)
        def _():
            acc[...] = jnp.zeros_like(acc)

        a = _mm(f_ref[...], w1_ref[...])
        a_ref[...] = a.astype(ACT_DTYPE)
        z = jnp.square(jnp.maximum(a, 0.0))
        acc[...] += _mm(z, w2_ref[...])

        @pl.when(j == nj - 1)
        def _():
            h2 = h1_ref[...] + acc[...]
            rstd = lax.rsqrt(jnp.mean(h2 * h2, axis=-1, keepdims=True) + EPS)
            hn = h2 * rstd
            nw = nw_ref[...]
            row = i * tm + lax.broadcasted_iota(jnp.int32, (tm, 1), 0)
            target = jnp.concatenate([t[...] for t in t_refs], axis=0)
            err = jnp.where(row >= LEAD, hn * nw - target, 0.0)
            row_loss = jnp.sum(err * err, axis=-1, keepdims=True) * (1.0 / D)
            loss_ref[...] += jnp.broadcast_to(0.5 * jnp.sum(row_loss, axis=0, keepdims=True), loss_ref.shape)
            dy = err * (1.0 / D)
            gfn_ref[...] += jnp.broadcast_to(jnp.sum(dy * hn, axis=0, keepdims=True), gfn_ref.shape)
            dhn = dy * nw
            dh2 = rstd * (dhn - hn * jnp.mean(dhn * hn, axis=-1, keepdims=True))
            dh2_ref[...] = dh2
            dh2t_ref[...] = dh2.T.astype(ACT_DTYPE)

    return pl.pallas_call(
        body, name="ffn_fwd", grid=(rows // tm, nj),
        in_specs=[pl.BlockSpec((tm, D), lambda i, j: (i, 0)), pl.BlockSpec((tm, D), lambda i, j: (i, 0)),
                  pl.BlockSpec((D, FF_WIDE), lambda i, j: (0, j)),
                  pl.BlockSpec((FF_WIDE, D), lambda i, j: (j, 0)),
                  pl.BlockSpec((1, D), lambda i, j: (0, 0))] + _token_specs(tm, grid_rank=2),
        out_specs=[pl.BlockSpec((tm, FF_WIDE), lambda i, j: (i, j)), pl.BlockSpec((tm, D), lambda i, j: (i, 0)),
                   pl.BlockSpec((D, tm), lambda i, j: (0, i)),
                   pl.BlockSpec((8, 128), lambda i, j: (0, 0)), pl.BlockSpec((8, D), lambda i, j: (0, 0))],
        out_shape=[jax.ShapeDtypeStruct((rows, D_FF), ACT_DTYPE), jax.ShapeDtypeStruct((rows, D), F32),
                   jax.ShapeDtypeStruct((D, rows), ACT_DTYPE),
                   jax.ShapeDtypeStruct((8, 128), F32), jax.ShapeDtypeStruct((8, D), F32)],
        scratch_shapes=[pltpu.VMEM((tm, D), F32)],
        compiler_params=_cp(("arbitrary", "arbitrary"), VMEM_WIDE_MB),
    )(f, h1, w1, w2, fnw, *([tgt] * nb))


def _ffn_bwd_act(dh2, a, w1, w2, h1, nfw, tm):
    rows = dh2.shape[0]
    nj = D_FF // FF_WIDE

    def body(dh2_ref, a_ref, w1_ref, w2_ref, h1_ref, nw_ref, da_ref, dh1_ref, gnf_ref, acc):
        i, j = pl.program_id(0), pl.program_id(1)

        @pl.when((i == 0) & (j == 0))
        def _():
            gnf_ref[...] = jnp.zeros_like(gnf_ref)

        @pl.when(j == 0)
        def _():
            acc[...] = jnp.zeros_like(acc)

        dz = _mm_nt(dh2_ref[...], w2_ref[...])
        da = dz * (2.0 * jnp.maximum(a_ref[...].astype(F32), 0.0))
        da_ref[...] = da.astype(ACT_DTYPE)
        acc[...] += _mm_nt(da, w1_ref[...])

        @pl.when(j == nj - 1)
        def _():
            h1 = h1_ref[...]
            rstd = lax.rsqrt(jnp.mean(h1 * h1, axis=-1, keepdims=True) + EPS)
            hn = h1 * rstd
            df = acc[...]
            gnf_ref[...] += jnp.broadcast_to(jnp.sum(df * hn, axis=0, keepdims=True), gnf_ref.shape)
            dfn = df * nw_ref[...]
            dh1_ref[...] = dh2_ref[...] + rstd * (dfn - hn * jnp.mean(dfn * hn, axis=-1, keepdims=True))

    return pl.pallas_call(
        body, name="ffn_bwd_act", grid=(rows // tm, nj),
        in_specs=[pl.BlockSpec((tm, D), lambda i, j: (i, 0)), pl.BlockSpec((tm, FF_WIDE), lambda i, j: (i, j)),
                  pl.BlockSpec((D, FF_WIDE), lambda i, j: (0, j)),
                  pl.BlockSpec((FF_WIDE, D), lambda i, j: (j, 0)),
                  pl.BlockSpec((tm, D), lambda i, j: (i, 0)), pl.BlockSpec((1, D), lambda i, j: (0, 0))],
        out_specs=[pl.BlockSpec((tm, FF_WIDE), lambda i, j: (i, j)), pl.BlockSpec((tm, D), lambda i, j: (i, 0)),
                   pl.BlockSpec((8, D), lambda i, j: (0, 0))],
        out_shape=[jax.ShapeDtypeStruct((rows, D_FF), ACT_DTYPE), jax.ShapeDtypeStruct((rows, D), F32),
                   jax.ShapeDtypeStruct((8, D), F32)],
        scratch_shapes=[pltpu.VMEM((tm, D), F32)],
        compiler_params=_cp(("arbitrary", "arbitrary"), VMEM_WIDE_MB),
    )(dh2, a, w1, w2, h1, nfw)


def _ffn_bwd_weights(ft, a, da, dh2t, tm):
    rows = a.shape[0]
    steps = rows // tm
    pair = 2 * FF_TILE

    def body(ft_ref, a_ref, da_ref, dh2t_ref, dw1_ref, dw2_ref, dw2t):
        i = pl.program_id(1)

        @pl.when(i == 0)
        def _():
            dw1_ref[...] = jnp.zeros_like(dw1_ref)
            dw2t[...] = jnp.zeros_like(dw2t)

        z = jnp.square(jnp.maximum(a_ref[...].astype(F32), 0.0))
        dw1 = _mm(ft_ref[...], da_ref[...])
        for core in range(2):
            dw1_ref[core] += dw1[:, core * FF_TILE:(core + 1) * FF_TILE]
        dw2t[...] += _mm(dh2t_ref[...], z)

        @pl.when(i == steps - 1)
        def _():
            for core in range(2):
                dw2_ref[core] = dw2t[:, core * FF_TILE:(core + 1) * FF_TILE].T

    return pl.pallas_call(
        body, name="ffn_bwd_weights", grid=(N_DEV // 2, steps),
        in_specs=[pl.BlockSpec((D, tm), lambda j, i: (0, i)), pl.BlockSpec((tm, pair), lambda j, i: (i, j)),
                  pl.BlockSpec((tm, pair), lambda j, i: (i, j)), pl.BlockSpec((D, tm), lambda j, i: (0, i))],
        out_specs=[pl.BlockSpec((2, None, D, FF_TILE), lambda j, i: (0, j, 0, 0)),
                   pl.BlockSpec((2, None, FF_TILE, D), lambda j, i: (0, j, 0, 0))],
        out_shape=[jax.ShapeDtypeStruct((2, 4, D, FF_TILE), F32), jax.ShapeDtypeStruct((2, 4, FF_TILE, D), F32)],
        scratch_shapes=[pltpu.VMEM((D, pair), F32)],
        compiler_params=_cp(("arbitrary", "arbitrary"), VMEM_WIDE_MB),
    )(ft, a, da, dh2t)


def _out_proj_bwd(dh1, og, osw, wout, tm, partials):
    rows = dh1.shape[0]
    steps = rows // tm
    ns = len(partials)

    def body(dh1_ref, og_ref, os_ref, w_ref, *rest):
        part_refs, rest = rest[:ns], rest[ns:]
        dog_ref, dos_ref, dw_ref = rest[:3]
        land_refs, (send_sems, recv_sems) = rest[3:3 + ns], rest[3 + ns:]
        i = pl.program_id(0)
        start, finish = _sibling_schedule(part_refs, land_refs, send_sems, recv_sems)

        @pl.when(i == 0)
        def _():
            dw_ref[...] = jnp.zeros_like(dw_ref)
            start()

        pl.when(i == steps - 1)(finish)

        dh1 = dh1_ref[...].astype(MXU_DTYPE)
        dog_ref[...] = _mm_nt(dh1, w_ref[0:512, :])
        dos_ref[...] = _mm_nt(dh1, w_ref[512:1024, :])
        for half, ref in enumerate((og_ref, os_ref)):
            dw = _mm_tn(ref[...], dh1)
            for blk in range(4):
                shard = half * 4 + blk
                dw_ref[shard % 2, shard // 2] += dw[blk * 128:(blk + 1) * 128, :]

    row = lambda w: pl.BlockSpec((tm, w), lambda i: (i, 0))
    outs = pl.pallas_call(
        body, name="out_proj_bwd", grid=(steps,),
        in_specs=[row(D), row(512), row(512), pl.BlockSpec((D, D), lambda i: (0, 0))] + [ANY] * ns,
        out_specs=[row(512), row(512), pl.BlockSpec((2, 4, 128, D), lambda i: (0, 0, 0, 0))] + [ANY] * ns,
        out_shape=[jax.ShapeDtypeStruct((rows, 512), F32), jax.ShapeDtypeStruct((rows, 512), F32),
                   jax.ShapeDtypeStruct((2, 4, 128, D), F32)] + _sibling_shapes(partials),
        scratch_shapes=_sibling_sems(ns),
        compiler_params=_cp(("arbitrary",), VMEM_TILE_MB),
    )(dh1, og, osw, wout, *partials)
    return outs[0], outs[1], outs[2], outs[3:]


def _swa_bwd(qr, kr, vr, osw, dos, sinks, jobs):
    rows = qr.shape[0]
    nblk = rows // SWA_BLOCK
    group = _swa_group(nblk)
    steps = nblk // group
    ns = jobs.n

    def body(q_ref, k0, kp, kc, v0, vp, vc, o_ref, do_ref, sink_ref, *rest):
        dq_ref, dk_ref, dv_ref, dsink_ref = rest[ns:ns + 4]
        start, finish = jobs.bind(rest[:ns], rest[ns + 4:2 * ns + 4], rest[2 * ns + 4:])
        step = pl.program_id(0)

        @pl.when(step == 0)
        def _():
            dk_ref[...] = jnp.zeros_like(dk_ref)
            dv_ref[...] = jnp.zeros_like(dv_ref)
            dsink_ref[...] = jnp.zeros_like(dsink_ref)
            start()

        pl.when(step == steps - 1)(finish)
        for g in range(group):
            block(step * group + g, g, q_ref, k0, kp, kc, v0, vp, vc, o_ref, do_ref, sink_ref,
                  dq_ref, dk_ref, dv_ref, dsink_ref)

    def block(n, g, q_ref, k0, kp, kc, v0, vp, vc, o_ref, do_ref, sink_ref, dq_ref, dk_ref, dv_ref, dsink_ref):
        rs = slice(g * SWA_BLOCK, (g + 1) * SWA_BLOCK)
        kall, vall = _swa_keys(k0, kp, kc, g), _swa_keys(v0, vp, vc, g)
        mask = _swa_mask(n)[0:SWA_BLOCK]
        heads = range(SWA_HEADS)
        hs = [slice(h * SWA_HD, (h + 1) * SWA_HD) for h in heads]
        kv = [slice((h // SWA_GROUP) * SWA_HD, (h // SWA_GROUP + 1) * SWA_HD) for h in heads]
        sink = [sink_ref[0, h] for h in heads]
        qh = [q_ref[rs, hs[h]] for h in heads]
        doh = [do_ref[rs, hs[h]] for h in heads]
        s = [jnp.where(mask, _mm_nt(qh[h], kall[:, kv[h]]), NEG) for h in heads]
        dp = [_mm_nt(doh[h], vall[:, kv[h]]) for h in heads]
        delta = [jnp.sum(doh[h] * o_ref[rs, hs[h]].astype(F32), axis=-1, keepdims=True) for h in heads]
        m = [jnp.maximum(jnp.max(s[h], axis=-1, keepdims=True), sink[h]) for h in heads]
        e = [jnp.exp(s[h] - m[h]) for h in heads]
        inv = [1.0 / (jnp.sum(e[h], axis=-1, keepdims=True) + jnp.exp(sink[h] - m[h])) for h in heads]
        p = [e[h] * inv[h] for h in heads]
        ds = [p[h] * (dp[h] - delta[h]) for h in heads]
        dq = [_mm(ds[h], kall[:, kv[h]]) for h in heads]
        dkh = [_mm_tn(ds[h], qh[h]) for h in heads]
        dvh = [_mm_tn(p[h], doh[h]) for h in heads]
        for h in heads:
            dsink = -jnp.sum(jnp.exp(sink[h] - m[h]) * inv[h] * delta[h], axis=0, keepdims=True)
            dsink_ref[h:h + 1, :] += jnp.broadcast_to(dsink, (1, 128))
        dq_ref[rs, :] = jnp.concatenate(dq, axis=1).astype(ACT_DTYPE)
        group_sum = lambda parts, kvh: sum(parts[kvh * SWA_GROUP + 1:(kvh + 1) * SWA_GROUP], parts[kvh * SWA_GROUP])
        dk_all = jnp.concatenate([group_sum(dkh, kvh) for kvh in range(SWA_KV)], axis=1)
        dv_all = jnp.concatenate([group_sum(dvh, kvh) for kvh in range(SWA_KV)], axis=1)
        prev0 = pl.multiple_of(jnp.maximum(n - 1, 0) * SWA_BLOCK, SWA_BLOCK)
        cur0 = pl.multiple_of(n * SWA_BLOCK, SWA_BLOCK)
        for ref, val in ((dk_ref, dk_all), (dv_ref, dv_all)):
            ref[0:SWA_BLOCK, :] += val[0:SWA_BLOCK]
            ref[pl.ds(prev0, SWA_BLOCK), :] += val[SWA_BLOCK:2 * SWA_BLOCK]
            ref[pl.ds(cur0, SWA_BLOCK), :] += val[2 * SWA_BLOCK:]

    blk, first, prev = _swa_specs(group)
    whole = pl.BlockSpec((rows, 128), lambda n: (0, 0))
    outs = pl.pallas_call(
        body, name="swa_bwd", grid=(steps,),
        in_specs=[blk(512), first, prev, blk(128), first, prev, blk(128), blk(512), blk(512),
                  pl.BlockSpec(memory_space=pltpu.SMEM)] + [ANY] * ns,
        out_specs=[blk(512), whole, whole, pl.BlockSpec((8, 128), lambda n: (0, 0))] + [ANY] * ns,
        out_shape=[jax.ShapeDtypeStruct((rows, 512), ACT_DTYPE), jax.ShapeDtypeStruct((rows, 128), F32),
                   jax.ShapeDtypeStruct((rows, 128), F32), jax.ShapeDtypeStruct((8, 128), F32)] + jobs.out_shapes,
        scratch_shapes=jobs.sems,
        compiler_params=_cp(("arbitrary",), VMEM_TILE_MB),
    )(qr, kr, kr, kr, vr, vr, vr, osw, dos, sinks, *jobs.inputs)
    return outs[0], outs[1], outs[2], outs[3], jobs.split(outs[4:])


def _gla_bwd(proj, decay, dgate, oraw, states, dog, wg_p, gnw, jobs):
    rows = proj.shape[0]
    nc = rows // GLA_CHUNK
    group = _gla_group(nc, 5)
    steps, nrows = nc // group, group * GLA_CHUNK
    ns = jobs.n

    def body(q_ref, k_ref, v_ref, r_ref, lr_ref, b_ref, dgate_ref, oraw_ref, st_ref, dog_ref, wg_ref, gnw_ref, *rest):
        dq_ref, dk_ref, dv_ref, dr_ref, dlr_ref, dwg_ref, dbg_ref, dgnw_ref = rest[ns:ns + 8]
        dstate, db_scr = rest[2 * ns + 8:2 * ns + 10]
        start, finish = jobs.bind(rest[:ns], rest[ns + 8:2 * ns + 8], rest[2 * ns + 10:])
        t = pl.program_id(0)

        @pl.when(t == 0)
        def _():
            dstate[...] = jnp.zeros_like(dstate)
            dwg_ref[...] = jnp.zeros_like(dwg_ref)
            dbg_ref[...] = jnp.zeros_like(dbg_ref)
            dgnw_ref[...] = jnp.zeros_like(dgnw_ref)
            start()

        pl.when(t == steps - 1)(finish)

        lr, wg = lr_ref[...], wg_ref[...]
        b = b_ref[...]
        eb, enb = jnp.exp(b), jnp.exp(-b)
        scale = GLA_DK ** -0.5
        gq = q_ref[...] * scale * eb
        gk = k_ref[...] * enb
        v = v_ref[...]
        gnw_v = gnw_ref[...]
        tril = _tril64()
        is_last = lax.broadcasted_iota(jnp.int32, (GLA_CHUNK, 1), 0) == GLA_CHUNK - 1
        dgnw = jnp.zeros((1, GLA_DV), F32)
        pairs = [(h, gi) for h in range(GLA_HEADS) for gi in range(group)]
        rs = {gi: slice(gi * GLA_CHUNK, (gi + 1) * GLA_CHUNK) for gi in range(group)}
        s64 = {h: slice(h * GLA_DK, (h + 1) * GLA_DK) for h in range(GLA_HEADS)}
        s128 = {h: slice(h * GLA_DV, (h + 1) * GLA_DV) for h in range(GLA_HEADS)}
        qh = {(h, gi): gq[rs[gi], s64[h]] for h, gi in pairs}
        kh = {(h, gi): gk[rs[gi], s64[h]] for h, gi in pairs}
        vh = {(h, gi): v[rs[gi], s128[h]] for h, gi in pairs}
        ebl = {(h, gi): eb[(gi + 1) * GLA_CHUNK - 1:(gi + 1) * GLA_CHUNK, s64[h]] for h, gi in pairs}
        kl = {pr: kh[pr] * ebl[pr] for pr in pairs}
        st = {(h, gi): st_ref[gi, h] for h, gi in pairs}
        do = {}
        for h, gi in pairs:
            o, rh, dout = oraw_ref[rs[gi], s128[h]], r_ref[rs[gi], s128[h]], dog_ref[rs[gi], s128[h]]
            rstd = lax.rsqrt(jnp.mean(o * o, axis=-1, keepdims=True) + EPS)
            on = o * rstd
            sg = _sigmoid(rh)
            dr_ref[rs[gi], s128[h]] = (dout * (on * gnw_v) * (sg * (1.0 + rh * (1.0 - sg)))).astype(ACT_DTYPE)
            dy = dout * (rh * sg)
            dgnw = dgnw + jnp.sum(dy * on, axis=0, keepdims=True)
            don = dy * gnw_v
            do[h, gi] = rstd * (don - on * jnp.mean(don * on, axis=-1, keepdims=True))
        a = {pr: jnp.where(tril, _mm_nt(qh[pr], kh[pr]), 0.0) for pr in pairs}
        da = {pr: jnp.where(tril, _mm_nt(do[pr], vh[pr]), 0.0) for pr in pairs}
        dinc = {pr: _mm_tn(do[pr], qh[pr]) for pr in pairs}
        dgq = {pr: _mm(da[pr], kh[pr]) + _mm(do[pr], st[pr]) for pr in pairs}
        dgk = {pr: _mm_tn(da[pr], qh[pr]) for pr in pairs}
        dv_a = {pr: _mm_tn(a[pr], do[pr]) for pr in pairs}
        dsp = {}
        for h in range(GLA_HEADS):
            cur = dstate[h]
            for gi in reversed(range(group)):
                dsp[h, gi] = cur
                cur = cur * ebl[h, gi] + dinc[h, gi]
            dstate[h] = cur
        for h, gi in pairs:
            pr = (h, gi)
            dkl = _mm(vh[pr], dsp[pr])
            dv_ref[rs[gi], s128[h]] = (dv_a[pr] + _mm_nt(kl[pr], dsp[pr])).astype(ACT_DTYPE)
            debl = jnp.sum(dsp[pr] * st[pr], axis=0, keepdims=True)
            dq_ref[rs[gi], s64[h]] = (dgq[pr] * (scale * eb[rs[gi], s64[h]])).astype(ACT_DTYPE)
            dk_ref[rs[gi], s64[h]] = ((dgk[pr] + dkl * ebl[pr]) * enb[rs[gi], s64[h]]).astype(ACT_DTYPE)
            last = debl * ebl[pr] + jnp.sum(dkl * kl[pr], axis=0, keepdims=True)
            db_scr[rs[gi], s64[h]] = (dgq[pr] * qh[pr] - dgk[pr] * kh[pr] - dkl * kl[pr]
                                      + jnp.where(is_last, last, 0.0))
        dzg = _masked_sums(_chunk_masks(nrows)[1], db_scr[...]) * dgate_ref[...]
        dlr_ref[...] = _mm_nt(dzg, wg).astype(ACT_DTYPE)
        dwg_ref[...] += _mm_tn(lr, dzg)
        dbg_ref[...] += jnp.broadcast_to(jnp.sum(dzg, axis=0, keepdims=True), dbg_ref.shape)
        dgnw_ref[...] += jnp.broadcast_to(dgnw, dgnw_ref.shape)

    nb = lambda w, col: pl.BlockSpec((nrows, w), lambda t: (steps - 1 - t, col // w))
    const = lambda shape: pl.BlockSpec(shape, lambda t: (0,) * len(shape))
    outs = pl.pallas_call(
        body, name="gla_bwd", grid=(steps,),
        in_specs=[nb(256, C_GQ), nb(256, C_GK), nb(512, C_GV), nb(512, C_GR), nb(128, C_LR), nb(256, 0), nb(256, 0),
                  nb(512, 0),
                  pl.BlockSpec((group, GLA_HEADS, GLA_DV, GLA_DK), lambda t: (steps - 1 - t, 0, 0, 0)), nb(512, 0),
                  const((128, 256)), const((1, 128))] + [ANY] * ns,
        out_specs=[nb(256, 0), nb(256, 0), nb(512, 0), nb(512, 0), nb(128, 0),
                   const((128, 256)), const((8, 256)), const((8, 128))] + [ANY] * ns,
        out_shape=[jax.ShapeDtypeStruct((rows, 256), ACT_DTYPE), jax.ShapeDtypeStruct((rows, 256), ACT_DTYPE),
                   jax.ShapeDtypeStruct((rows, 512), ACT_DTYPE), jax.ShapeDtypeStruct((rows, 512), ACT_DTYPE),
                   jax.ShapeDtypeStruct((rows, 128), ACT_DTYPE), jax.ShapeDtypeStruct((128, 256), F32),
                   jax.ShapeDtypeStruct((8, 256), F32), jax.ShapeDtypeStruct((8, 128), F32)] + jobs.out_shapes,
        scratch_shapes=[pltpu.VMEM((GLA_HEADS, GLA_DV, GLA_DK), F32), pltpu.VMEM((nrows, 256), F32)] + jobs.sems,
        compiler_params=_cp(("arbitrary",)),
    )(proj, proj, proj, proj, proj, decay, dgate, oraw, states, dog, wg_p, gnw, *jobs.inputs)
    return outs[:8], jobs.split(outs[8:])


def _in_proj_bwd(x, lead, dh1, nw, win_p, dgv, dgr, dsq, dgq, dgk, dsk, dsv, dlr, angles, tm):
    seq = x.shape[0]
    rows = LEAD + seq
    nb = tm // LEAD
    steps = rows // tm

    def first_copy(scr, gx_ref, sem):
        return pltpu.make_async_copy(scr.at[pl.ds(LEAD, tm - LEAD)], gx_ref.at[pl.ds(0, tm - LEAD)], sem)

    def tile_copy(scr, gx_ref, sem, step):
        start = pl.multiple_of(jnp.maximum(step * tm - LEAD, 0), LEAD)
        return pltpu.make_async_copy(scr, gx_ref.at[pl.ds(start, tm)], sem)

    def body(*refs):
        x_refs, refs = refs[:nb], refs[nb:]
        (lead_ref, dh1_ref, nw_ref, w_ref, dgv_ref, dgr_ref, dsq_ref, dgq_ref, dgk_ref, dsk_ref, dsv_ref, dlr_ref,
         cs_ref, gx_ref, dlead_ref, dproj_ref, ut_ref, gnm_ref, scr, sem) = refs
        i = pl.program_id(0)

        @pl.when(i == 0)
        def _():
            gnm_ref[...] = jnp.zeros_like(gnm_ref)

        cos, sa, sb = _rope_tables(cs_ref[...])
        dsq_v = (_unrope(dsq_ref[...].astype(F32), cos, sa, sb) * (SWA_HD ** -0.5)).astype(MXU_DTYPE)
        dsk_v = _unrope(dsk_ref[...], cos, sa, sb).astype(MXU_DTYPE)
        dproj = jnp.concatenate(
            [dgv_ref[...].astype(MXU_DTYPE), dgr_ref[...].astype(MXU_DTYPE), dgq_ref[...].astype(MXU_DTYPE),
             dgk_ref[...].astype(MXU_DTYPE), dlr_ref[...].astype(MXU_DTYPE), dsq_v, dsk_v,
             dsv_ref[...].astype(MXU_DTYPE)],
            axis=1)
        dproj_ref[...] = dproj
        h = _h_tile(i, lead_ref, x_refs)
        rstd = lax.rsqrt(jnp.mean(h * h, axis=-1, keepdims=True) + EPS)
        hn = h * rstd
        nw_v = nw_ref[...]
        ut_ref[...] = (hn * nw_v).T.astype(ACT_DTYPE)
        du = _mm_nt(dproj, w_ref[...])
        gnm_ref[...] += jnp.broadcast_to(jnp.sum(du * hn, axis=0, keepdims=True), gnm_ref.shape)
        dun = du * nw_v
        dh0 = dh1_ref[...] + rstd * (dun - hn * jnp.mean(dun * hn, axis=-1, keepdims=True))

        if tm > LEAD:
            pl.when(i == 1)(lambda: first_copy(scr, gx_ref, sem).wait())
        pl.when(i > 1)(lambda: tile_copy(scr, gx_ref, sem, i).wait())
        scr[...] = dh0

        @pl.when(i == 0)
        def _():
            dlead_ref[...] = dh0[0:LEAD]
            if tm > LEAD:
                first_copy(scr, gx_ref, sem).start(priority=LOCAL_COPY_PRIORITY)
                if steps == 1:
                    first_copy(scr, gx_ref, sem).wait()

        @pl.when(i > 0)
        def _():
            tile_copy(scr, gx_ref, sem, i).start(priority=LOCAL_COPY_PRIORITY)

        if steps > 1:
            pl.when(i == steps - 1)(lambda: tile_copy(scr, gx_ref, sem, i).wait())

    row = lambda w: pl.BlockSpec((tm, w), lambda i: (i, 0))
    const = lambda shape: pl.BlockSpec(shape, lambda i: (0,) * len(shape))
    return pl.pallas_call(
        body, name="in_proj_bwd", grid=(steps,),
        in_specs=_token_specs(tm) + [const((LEAD, D)), row(D), const((1, D)), const((D, DINP)),
                                     row(512), row(512), row(512), row(256), row(256), row(128), row(128), row(128),
                                     row(ROPE_DIM)],
        out_specs=[ANY, const((LEAD, D)), row(DINP), pl.BlockSpec((D, tm), lambda i: (0, i)), const((8, D))],
        out_shape=[jax.ShapeDtypeStruct((seq, D), F32), jax.ShapeDtypeStruct((LEAD, D), F32),
                   jax.ShapeDtypeStruct((rows, DINP), ACT_DTYPE), jax.ShapeDtypeStruct((D, rows), ACT_DTYPE),
                   jax.ShapeDtypeStruct((8, D), F32)],
        scratch_shapes=[pltpu.VMEM((tm, D), F32), pltpu.SemaphoreType.DMA],
        compiler_params=_cp(("arbitrary",), VMEM_WIDE_MB),
    )(*([x] * nb), lead, dh1, nw, win_p, dgv, dgr, dsq, dgq, dgk, dsk, dsv, dlr, angles)


def _win_runs():
    groups = [(O_GQ, C_GQ), (O_GK, C_GK), (O_GV, C_GV), (O_GR, C_GR), (O_LR, C_LR), (O_SQ, C_SQ), (O_SK, C_SK),
              (O_SV, C_SV)]
    per = DIN // N_DEV
    runs = []
    for (o0, o1), c0 in groups:
        o = o0
        while o < o1:
            d = o // per
            end = min(o1, (d + 1) * per)
            runs.append((d, o - d * per, c0 + o - o0, end - o))
            o = end
    return runs


def _win_padded(g_in):
    tr = 128

    def body(g_ref, o_ref):
        o_ref[...] = jnp.zeros_like(o_ref)
        for d, s, c, w in _win_runs():
            o_ref[:, c:c + w] = g_ref[d, :, s:s + w]

    return pl.pallas_call(
        body, name="w_in_layout", grid=(D // tr,),
        in_specs=[pl.BlockSpec((N_DEV, tr, DIN // N_DEV), lambda i: (0, i, 0))],
        out_specs=pl.BlockSpec((tr, DINP), lambda i: (i, 0)),
        out_shape=jax.ShapeDtypeStruct((D, DINP), g_in.dtype),
        compiler_params=_cp(("arbitrary",)),
    )(g_in)


def _in_proj_bwd_weights(ut, dproj, tm, small):
    rows = dproj.shape[0]
    steps = rows // tm
    per = DIN // N_DEV

    def body(ut_ref, dp_ref, *rest):
        small_refs, (mine_ref, theirs_ref, total_ref, acc, stage, local_sems, send_sems, recv_sems) = rest[:9], rest[9:17]
        i = pl.program_id(0)
        start, finish = _small_sum_schedule(small_refs, total_ref, *rest[17:])
        x, y, c = _mesh_pos()

        @pl.when(i == 0)
        def _():
            acc[...] = jnp.zeros_like(acc)
            start()

        acc[...] += _mm(ut_ref[...], dp_ref[...])
        pl.when(i == steps - 1)(finish)

        def keep(slot, chip):
            return pltpu.make_async_copy(stage.at[slot], mine_ref.at[chip], local_sems.at[slot])

        def send(slot, chip):
            return pltpu.make_async_remote_copy(
                src_ref=stage.at[slot], dst_ref=theirs_ref.at[chip], send_sem=send_sems.at[slot],
                recv_sem=recv_sems.at[chip], device_id=(x, y, 1 - c), device_id_type=MESH)

        def drained(d):
            pl.when(c == d % 2)(keep(d % 2, d // 2).wait)
            pl.when(c != d % 2)(send(d % 2, d // 2).wait_send)

        @pl.when(i == steps - 1)
        def _():
            for d in range(N_DEV):
                slot, chip = d % 2, d // 2
                if d >= 2:
                    drained(d - 2)
                for owner, s, col, w in _win_runs():
                    if owner == d:
                        stage[slot, :, s:s + w] = acc[:, col:col + w]
                pl.when(c == slot)(lambda: keep(slot, chip).start(priority=LOCAL_COPY_PRIORITY))
                pl.when(c != slot)(send(slot, chip).start)
            drained(N_DEV - 2)
            drained(N_DEV - 1)
            for chip in range(4):
                send(0, chip).wait_recv()

    half = jax.ShapeDtypeStruct((4, D, per), F32)
    return pl.pallas_call(
        body, name="in_proj_bwd_weights", grid=(steps,),
        in_specs=[pl.BlockSpec((D, tm), lambda i: (0, i)), pl.BlockSpec((tm, DINP), lambda i: (i, 0))] + SMALL_SPECS,
        out_specs=[ANY, ANY, pl.BlockSpec((SMALL_ROWS, D), lambda i: (0, 0))],
        out_shape=[half, half, jax.ShapeDtypeStruct((SMALL_ROWS, D), F32)],
        scratch_shapes=[pltpu.VMEM((D, DINP), F32), pltpu.VMEM((2, D, per), F32), pltpu.SemaphoreType.DMA((2,)),
                        pltpu.SemaphoreType.DMA((2,)), pltpu.SemaphoreType.DMA((4,))] + _small_sum_scratch(),
        compiler_params=_cp(("arbitrary",), VMEM_WIDE_MB),
    )(ut, dproj, *small)


def _adamw(w, g, m, v):
    m = ADAM_B1 * m + (1.0 - ADAM_B1) * g
    v = ADAM_B2 * v + (1.0 - ADAM_B2) * jnp.square(g)
    m_hat = m / (1.0 - ADAM_B1 ** ADAM_STEP)
    v_hat = v / (1.0 - ADAM_B2 ** ADAM_STEP)
    delta = -ADAM_LR * (m_hat / (jnp.sqrt(v_hat) + ADAM_EPS) + ADAM_WD * w)
    return delta, m, v


ADAM_STEPS = 8


def _adamw_shards(items, name, jobs=None):
    jobs = jobs or _Jobs([])
    ns, nw = jobs.n, len(items)

    def body(*rest):
        ins, rest = rest[:5 * nw], rest[5 * nw:]
        job_ins, rest = rest[:ns], rest[ns:]
        outs, rest = rest[:4 * nw], rest[4 * nw:]
        start, finish = jobs.bind(job_ins, rest[:ns], rest[ns:])
        i = pl.program_id(0)
        pl.when(i == 0)(start)
        pl.when(i == ADAM_STEPS - 1)(finish)
        for k in range(nw):
            p_ref, own_ref, w_ref, m_ref, v_ref = ins[5 * k:5 * k + 5]
            g_ref, d_ref, nm_ref, nv_ref = outs[4 * k:4 * k + 4]
            g = ((p_ref[0].astype(F32) + p_ref[1].astype(F32)) + p_ref[2].astype(F32)) + own_ref[...]
            g_ref[...] = g
            d_ref[...], nm_ref[...], nv_ref[...] = _adamw(w_ref[...], g, m_ref[...], v_ref[...])

    in_specs, out_specs, out_shape, operands = [], [], [], []
    for parts, own, w, m, v in items:
        r, cdim = w.shape
        tr = r // ADAM_STEPS
        spec = pl.BlockSpec((tr, cdim), lambda i: (i, 0))
        in_specs += [pl.BlockSpec((3, tr, cdim), lambda i: (0, i, 0)), spec, spec, spec, spec]
        out_specs += [spec] * 4
        out_shape += [jax.ShapeDtypeStruct((r, cdim), F32)] * 4
        operands += [parts, own, w, m, v]
    outs = pl.pallas_call(
        body, name=name, grid=(ADAM_STEPS,),
        in_specs=in_specs + [ANY] * ns, out_specs=out_specs + [ANY] * ns, scratch_shapes=jobs.sems,
        out_shape=out_shape + jobs.out_shapes,
        compiler_params=_cp(("arbitrary",)),
    )(*operands, *jobs.inputs)
    return [outs[4 * k:4 * k + 4] for k in range(nw)], jobs.split(outs[4 * nw:])


def _adamw_small(items):
    n = len(items)

    def body(*refs):
        ins, outs = refs[:4 * n], refs[4 * n:]
        for k in range(n):
            w_ref, g_ref, m_ref, v_ref = ins[4 * k:4 * k + 4]
            d_ref, nm_ref, nv_ref = outs[3 * k:3 * k + 3]
            d_ref[...], nm_ref[...], nv_ref[...] = _adamw(w_ref[...], g_ref[...], m_ref[...], v_ref[...])

    vm = pl.BlockSpec(memory_space=pltpu.VMEM)
    shapes = [jax.ShapeDtypeStruct(w.shape, F32) for w, _, _, _ in items for _ in range(3)]
    outs = pl.pallas_call(body, name="adamw_small", in_specs=[vm] * (4 * n), out_specs=[vm] * (3 * n),
                          out_shape=shapes)(*[t for item in items for t in item])
    return [outs[3 * k:3 * k + 3] for k in range(n)]


def _pair_sums(where, mine, theirs, name):
    _, r, cdim = theirs.shape
    tr = 128 if r % 128 == 0 else r

    def body(where_ref, a_ref, b_ref, own_ref, wire_ref):
        chip = where_ref[1]
        own_ref[...] = a_ref[chip] + b_ref[chip]
        wire_ref[...] = (a_ref[...] + b_ref[...]).astype(WIRE_DTYPE)

    spec = pl.BlockSpec((4, tr, cdim), lambda i, s: (0, i, 0))
    mine_spec = spec if mine.ndim == 3 else pl.BlockSpec((None, 4, tr, cdim), lambda i, s: (s[0], 0, i, 0))
    return pl.pallas_call(
        body, name=name,
        grid_spec=pltpu.PrefetchScalarGridSpec(
            num_scalar_prefetch=1, grid=(r // tr,), in_specs=[mine_spec, spec],
            out_specs=[pl.BlockSpec((tr, cdim), lambda i, s: (i, 0)), spec]),
        out_shape=[jax.ShapeDtypeStruct((r, cdim), F32), jax.ShapeDtypeStruct(theirs.shape, WIRE_DTYPE)],
        compiler_params=_cp(("arbitrary",)))(where, mine, theirs)


def kernel(x, meta_tokens, norm_mix_w, w_in, w_gate_up, b_gate, gla_norm_w, sinks, w_out, norm_ff_w, w_ff1, w_ff2, final_norm_w, loss_target, m_meta_tokens, m_norm_mix_w, m_w_in, m_w_gate_up, m_b_gate, m_gla_norm_w, m_sinks, m_w_out, m_norm_ff_w, m_w_ff1, m_w_ff2, m_final_norm_w, v_meta_tokens, v_norm_mix_w, v_w_in, v_w_gate_up, v_b_gate, v_gla_norm_w, v_sinks, v_w_out, v_norm_ff_w, v_w_ff1, v_w_ff2, v_final_norm_w):
    seq = x.shape[1]
    rows = LEAD + seq
    tm = _row_tile(rows)
    tm_wide = WIDE_ROW_TILE if rows % WIDE_ROW_TILE == 0 else tm
    dev =4 * lax.axis_index("x") + 2 * lax.axis_index("y") + lax.axis_index("c")

    small_shard = jnp.concatenate([meta_tokens, w_gate_up[0], jnp.zeros((N_META, 96), F32)], axis=1)
    g_in, g_small = _all_gather([w_in[0].astype(WIRE_DTYPE), small_shard])
    later_shards = [w_out[0].astype(WIRE_DTYPE), w_ff1[0].astype(WIRE_DTYPE), w_ff2[0].astype(WIRE_DTYPE)]
    win_p = _win_padded(g_in)
    meta_full = jnp.transpose(g_small[:, :, 0:128], (1, 0, 2)).reshape(N_META, D)
    wg_full = jnp.transpose(g_small[:, :, 128:160], (1, 0, 2)).reshape(GLA_RANK, GLA_HEADS * GLA_DK)
    wg_p = jnp.concatenate([wg_full, jnp.zeros((128 - GLA_RANK, 256), F32)], axis=0)

    lead = jnp.concatenate([jnp.zeros((META0, D), F32), meta_full], axis=0)
    angles = _rope_angles(rows)
    proj, qr, kr, vr, (g_w1,) = _in_proj(x[0], lead, norm_mix_w, win_p, angles, tm, later_shards[1:2])
    oraw, og, states, decay, dgate, (g_out,) = _gla_fwd(proj, wg_p, b_gate, gla_norm_w, later_shards[0:1])
    osw, (g_w2,) = _swa_fwd(qr, kr, vr, sinks, later_shards[2:3])
    wout_full = g_out.reshape(D, D)
    w2_full = g_w2.reshape(D_FF, D)
    w1_full = jnp.transpose(g_w1, (1, 0, 2)).reshape(D, D_FF)
    h1, f, ft = _out_proj(x[0], lead, og, osw, wout_full, norm_ff_w, tm)
    a, dh2, dh2t, loss_p, gfn_p = _ffn_fwd(f, h1, w1_full, w2_full, loss_target[0], final_norm_w.reshape(1, D), tm)

    da, dh1, gnf_p = _ffn_bwd_act(dh2, a, w1_full, w2_full, h1, norm_ff_w, tm)
    dw1, dw2 = _ffn_bwd_weights(ft, a, da, dh2t, tm_wide)
    where = jnp.stack([lax.axis_index("c"), 2 * lax.axis_index("x") + lax.axis_index("y")]).astype(jnp.int32)
    dog, dos, dwout, theirs_ffn = _out_proj_bwd(dh1, og, osw, wout_full, tm, [dw1, dw2])
    pairs_ffn = [_pair_sums(where, p, q, "reduce_pair_%d" % (2 + k))
                 for k, (p, q) in enumerate(zip([dw1, dw2], theirs_ffn))]
    sums_ffn, wires_ffn = [p[0] for p in pairs_ffn], [p[1] for p in pairs_ffn]
    dsq, dsk, dsv, dsink_p, (parts_ffn, (theirs_wout,)) = _swa_bwd(
        qr, kr, vr, osw, dos, sinks, _Jobs([("chips", wires_ffn), ("sibling", [dwout])]))
    sum_wout, wire_wout = _pair_sums(where, dwout, theirs_wout, "reduce_pair_1")
    (dgq, dgk, dgv, dgr, dlr, dwg_p, dbg_p, dgnw_p), ((parts_wout,),) = _gla_bwd(
        proj, decay, dgate, oraw, states, dog, wg_p, gla_norm_w, _Jobs([("chips", [wire_wout])]))
    grad_x, dlead, dproj, ut, gnm_p = _in_proj_bwd(x[0], lead, dh1, norm_mix_w, win_p, dgv, dgr, dsq, dgq, dgk, dsk,
                                                   dsv, dlr, angles, tm)
    grad_x = grad_x[None]
    dwin_mine, dwin_theirs, total = _in_proj_bwd_weights(
        ut, dproj, tm_wide, [dlead, dwg_p, gnm_p, gnf_p, gfn_p, dbg_p, dgnw_p, loss_p, dsink_p])
    sum_win, sum_win_wire = _pair_sums(where, dwin_mine, dwin_theirs, "reduce_pair_0")

    g_meta = lax.dynamic_slice(total, (R_META, dev * 128), (N_META, 128))
    g_wg = lax.dynamic_slice(total, (R_WG, dev * 32), (GLA_RANK, 32))
    g_norm_mix, g_norm_ff = total[R_NORM_MIX:R_NORM_MIX + 1], total[R_NORM_FF:R_NORM_FF + 1]
    g_final_norm = total[R_FINAL:R_FINAL + 1]
    g_b_gate, g_gla_norm = total[R_B_GATE:R_B_GATE + 1, 0:256], total[R_GLA_NORM:R_GLA_NORM + 1, 0:128]
    g_sinks = total[R_SINKS:R_SINKS + SWA_HEADS, 0].reshape(1, SWA_HEADS)
    loss = total[R_LOSS, 0]

    ((g_wout, d_wout, nm_wout, nv_wout), (g_w1s, d_w1, nm_w1, nv_w1), (g_w2s, d_w2, nm_w2, nv_w2)), ((parts_win,),) = \
        _adamw_shards([(parts_wout, sum_wout, w_out[0], m_w_out[0], v_w_out[0]),
                       (parts_ffn[0], sums_ffn[0], w_ff1[0], m_w_ff1[0], v_w_ff1[0]),
                       (parts_ffn[1], sums_ffn[1], w_ff2[0], m_w_ff2[0], v_w_ff2[0])],
                      "adamw_w_out_ff", _Jobs([("chips", [sum_win_wire])]))
    ((g_win, d_win, nm_win, nv_win),), _ = _adamw_shards(
        [(parts_win, sum_win, w_in[0], m_w_in[0], v_w_in[0])], "adamw_w_in")

    names = ["meta", "wg", "norm_mix", "b_gate", "gla_norm", "sinks", "norm_ff", "final_norm"]
    ws = [meta_tokens, w_gate_up, norm_mix_w, b_gate, gla_norm_w, sinks, norm_ff_w, final_norm_w]
    gs = [g_meta, g_wg, g_norm_mix, g_b_gate, g_gla_norm, g_sinks, g_norm_ff, g_final_norm]
    ms = [m_meta_tokens, m_w_gate_up, m_norm_mix_w, m_b_gate, m_gla_norm_w, m_sinks, m_norm_ff_w, m_final_norm_w]
    vs = [v_meta_tokens, v_w_gate_up, v_norm_mix_w, v_b_gate, v_gla_norm_w, v_sinks, v_norm_ff_w, v_final_norm_w]
    flat = lambda t: t.reshape(-1, t.shape[-1])
    small_out = _adamw_small([(flat(w), flat(g), flat(m), flat(v)) for w, g, m, v in zip(ws, gs, ms, vs)])
    d_small = {n: small_out[k][0].reshape(ws[k].shape) for k, n in enumerate(names)}
    nm_small = {n: small_out[k][1].reshape(ws[k].shape) for k, n in enumerate(names)}
    nv_small = {n: small_out[k][2].reshape(ws[k].shape) for k, n in enumerate(names)}
    g_small_d = {n: g.reshape(ws[k].shape) for k, (n, g) in enumerate(zip(names, gs))}

    def ordered(big, small_d):
        win_v, wout_v, w1_v, w2_v = big
        return (small_d["meta"], small_d["norm_mix"], win_v[None], small_d["wg"], small_d["b_gate"],
                small_d["gla_norm"], small_d["sinks"], wout_v[None], small_d["norm_ff"], w1_v[None], w2_v[None],
                small_d["final_norm"])

    return (loss, grad_x,
            *ordered((g_win, g_wout, g_w1s, g_w2s), g_small_d),
            *ordered((d_win, d_wout, d_w1, d_w2), d_small),
            *ordered((nm_win, nm_wout, nm_w1, nm_w2), nm_small),
            *ordered((nv_win, nv_wout, nv_w1, nv_w2), nv_small))
```

```python
import functools

import jax
import jax.numpy as jnp
from jax import lax
from jax.experimental import pallas as pl
from jax.experimental.pallas import tpu as pltpu

F32 = jnp.float32
MXU_DTYPE = jnp.bfloat16
ACT_DTYPE = jnp.bfloat16
WIRE_DTYPE = jnp.bfloat16

D = 1024
N_META = 16
LEAD = 128
META0 = LEAD - N_META
EPS = 1e-5
GLA_HEADS, GLA_DK, GLA_DV, GLA_RANK, GLA_CHUNK = 4, 64, 128, 16, 64
GLA_TAU = 16.0
SWA_HEADS, SWA_KV, SWA_GROUP, SWA_HD, SWA_BLOCK = 8, 2, 4, 64, 128
ROPE_DIM, ROPE_THETA = 16, 500000.0
D_FF = 4096
N_DEV = 8
FF_TILE = D_FF // N_DEV
FF_WIDE = 2048
NEG = -1e30

C_GV, C_GR, C_GQ, C_GK, C_LR, C_SQ, C_SK, C_SV = 0, 512, 1024, 1280, 1536, 1664, 2176, 2304
DGLA = 1664
DINP = 2432
DIN = 2320
O_GQ, O_GK, O_GV, O_GR, O_LR, O_SQ, O_SK, O_SV = (0, 256), (256, 512), (512, 1024), (1024, 1536), (1536, 1552), (1552, 2064), (2064, 2192), (2192, 2320)

ADAM_LR, ADAM_B1, ADAM_B2, ADAM_EPS, ADAM_WD, ADAM_STEP = 0.001, 0.9, 0.999, 1e-08, 0.01, 10

MESH = pl.DeviceIdType.MESH
ANY = pl.BlockSpec(memory_space=pl.ANY)
VMEM_TILE_MB, VMEM_WIDE_MB = 48, 56


def _cp(sem=None, vmem_mb=None):
    kw = {}
    if sem is not None:
        kw["dimension_semantics"] = sem
    if vmem_mb is not None:
        kw["vmem_limit_bytes"] = vmem_mb << 20
    return pltpu.CompilerParams(**kw)


def _mm(a, b):
    return jnp.dot(a.astype(MXU_DTYPE), b.astype(MXU_DTYPE), preferred_element_type=F32)


def _mm_nt(a, b):
    return lax.dot_general(a.astype(MXU_DTYPE), b.astype(MXU_DTYPE), (((1,), (1,)), ((), ())),
                           preferred_element_type=F32)


def _mm_tn(a, b):
    return lax.dot_general(a.astype(MXU_DTYPE), b.astype(MXU_DTYPE), (((0,), (0,)), ((), ())),
                           preferred_element_type=F32)


def _masked_sums(mask, t):
    m = mask.astype(jnp.bfloat16)
    hi = t.astype(jnp.bfloat16)
    rest = t - hi.astype(F32)
    mid = rest.astype(jnp.bfloat16)
    low = (rest - mid.astype(F32)).astype(jnp.bfloat16)
    dot = lambda part: jnp.dot(m, part, preferred_element_type=F32)
    return dot(hi) + (dot(mid) + dot(low))


def _logsigmoid(z):
    return jnp.minimum(z, 0.0) - jnp.log(1.0 + jnp.exp(-jnp.abs(z)))


def _sigmoid(z):
    return 1.0 / (1.0 + jnp.exp(-z))


ROW_TILE, WIDE_ROW_TILE = 640, 1664


def _row_tile(rows, want=ROW_TILE):
    return want if rows % want == 0 else LEAD


def _mesh_pos():
    return lax.axis_index("x"), lax.axis_index("y"), lax.axis_index("c")


def _all_gather(shards):
    n = len(shards)

    def body(*refs):
        start, forward, finish = _gather_schedule(refs[:n], refs[n:2 * n], *refs[2 * n:])
        start()
        for j in range(3):
            forward(j)
        finish()

    gathered = pl.pallas_call(
        body, name="all_gather_weights",
        out_shape=_gathered_shapes(shards), in_specs=[ANY] * n, out_specs=[ANY] * n,
        scratch_shapes=_gather_sems(n),
    )(*shards)
    return _with_own_block(gathered, shards)


def _gathered_shapes(shards):
    return [jax.ShapeDtypeStruct((N_DEV,) + s.shape, s.dtype) for s in shards]


def _gather_sems(n):
    return [pltpu.SemaphoreType.DMA((7 * n,)), pltpu.SemaphoreType.DMA((7 * n,))] if n else []


def _place_gather(step, steps, shard_refs, gathered_refs, sems):
    if not shard_refs:
        return
    start, forward, finish = _gather_schedule(shard_refs, gathered_refs, *sems)
    pl.when(step == 0)(start)
    for j, at in enumerate((steps * 7 // 10, steps * 8 // 10, steps * 9 // 10)):
        pl.when(step == at)(functools.partial(forward, j))
    pl.when(step == steps - 1)(finish)


def _with_own_block(gathered, shards):
    dev = 4 * lax.axis_index("x") + 2 * lax.axis_index("y") + lax.axis_index("c")
    return [lax.dynamic_update_index_in_dim(g, s, dev, 0) for g, s in zip(gathered, shards)]


def _gather_schedule(ins, outs, send_sems, recv_sems):
    n = len(ins)
    x, y, c = _mesh_pos()
    me, sibling = (x, y, c), (x, y, 1 - c)
    chips = [(1 - x, y), (x, 1 - y), (1 - x, 1 - y)]

    def copy(a, k, block, to, src=None):
        dst = outs[a].at[4 * block[0] + 2 * block[1] + block[2]]
        return pltpu.make_async_remote_copy(
            src_ref=dst if src is None else src, dst_ref=dst,
            send_sem=send_sems.at[a * 7 + k], recv_sem=recv_sems.at[a * 7 + k],
            device_id=to, device_id_type=MESH)

    def first(a):
        return [copy(a, 0, me, sibling, src=ins[a])] + [copy(a, 1 + j, me, (*chip, c), src=ins[a])
                                                        for j, chip in enumerate(chips)]

    def start():
        for a in range(n):
            for cp in first(a):
                cp.start()

    def forward(j):
        for a in range(n):
            copy(a, 1 + j, (*chips[j], c), me).wait_recv()
            copy(a, 4 + j, (*chips[j], c), sibling).start()

    def finish():
        for a in range(n):
            copy(a, 0, sibling, me).wait_recv()
            for j, chip in enumerate(chips):
                copy(a, 4 + j, (*chip, 1 - c), me).wait_recv()
        for a in range(n):
            for cp in first(a) + [copy(a, 4 + j, (*chip, c), sibling) for j, chip in enumerate(chips)]:
                cp.wait_send()

    return start, forward, finish


def _sibling_shapes(gs):
    return [jax.ShapeDtypeStruct(g.shape[1:], g.dtype) for g in gs]


def _sibling_sems(n):
    return [pltpu.SemaphoreType.DMA((n,)), pltpu.SemaphoreType.DMA((n,))]


def _sibling_schedule(ins, land, send_sems, recv_sems):
    x, y, c = _mesh_pos()

    def copies():
        return [pltpu.make_async_remote_copy(
            src_ref=ins[a].at[1 - c], dst_ref=land[a], send_sem=send_sems.at[a], recv_sem=recv_sems.at[a],
            device_id=(x, y, 1 - c), device_id_type=MESH) for a in range(len(ins))]

    def start():
        for cp in copies():
            cp.start()

    def finish():
        for cp in copies():
            cp.wait_recv()
        for cp in copies():
            cp.wait_send()

    return start, finish


def _chips_shapes(ps):
    return [jax.ShapeDtypeStruct((3,) + p.shape[1:], p.dtype) for p in ps]


def _chips_sems(n):
    return [pltpu.SemaphoreType.DMA((3 * n,)), pltpu.SemaphoreType.DMA((3 * n,))]


def _chips_schedule(ins, land, send_sems, recv_sems):
    x, y, c = _mesh_pos()
    chips = [(1 - x, y), (x, 1 - y), (1 - x, 1 - y)]

    def copies():
        return [pltpu.make_async_remote_copy(
            src_ref=ins[a].at[2 * chip[0] + chip[1]], dst_ref=land[a].at[j],
            send_sem=send_sems.at[3 * a + j], recv_sem=recv_sems.at[3 * a + j],
            device_id=(*chip, c), device_id_type=MESH) for a in range(len(ins)) for j, chip in enumerate(chips)]

    def start():
        for cp in copies():
            cp.start()

    def finish():
        for cp in copies():
            cp.wait_recv()
        for cp in copies():
            cp.wait_send()

    return start, finish


class _Jobs:
    def __init__(self, jobs):
        self.jobs = jobs
        self.inputs = [a for _, arrs in jobs for a in arrs]
        self.out_shapes = [s for kind, arrs in jobs
                           for s in (_sibling_shapes(arrs) if kind == "sibling" else _chips_shapes(arrs))]
        self.sems = [s for kind, arrs in jobs
                     for s in (_sibling_sems(len(arrs)) if kind == "sibling" else _chips_sems(len(arrs)))]
        self.n = len(self.inputs)

    def bind(self, in_refs, out_refs, sem_refs):
        starts, finishes, at = [], [], 0
        for k, (kind, arrs) in enumerate(self.jobs):
            schedule = _sibling_schedule if kind == "sibling" else _chips_schedule
            start, finish = schedule(in_refs[at:at + len(arrs)], out_refs[at:at + len(arrs)],
                                     sem_refs[2 * k], sem_refs[2 * k + 1])
            starts.append(start)
            finishes.append(finish)
            at += len(arrs)

        def start_all():
            for f in starts:
                f()

        def finish_all():
            for f in finishes:
                f()

        return start_all, finish_all

    def split(self, outs):
        res, at = [], 0
        for _, arrs in self.jobs:
            res.append(list(outs[at:at + len(arrs)]))
            at += len(arrs)
        return res


R_META, R_WG, R_NORM_MIX, R_NORM_FF, R_FINAL, R_B_GATE, R_GLA_NORM, R_LOSS, R_SINKS, SMALL_ROWS = 0, 16, 32, 33, 34, 35, 36, 37, 40, 48


SMALL_SPECS = [pl.BlockSpec((LEAD, D), lambda i: (0, 0)), pl.BlockSpec((128, 256), lambda i: (0, 0)),
               pl.BlockSpec((8, D), lambda i: (0, 0)), pl.BlockSpec((8, D), lambda i: (0, 0)),
               pl.BlockSpec((8, D), lambda i: (0, 0)), pl.BlockSpec((8, 256), lambda i: (0, 0)),
               pl.BlockSpec((8, 128), lambda i: (0, 0)), pl.BlockSpec((8, 128), lambda i: (0, 0)),
               pl.BlockSpec((8, 128), lambda i: (0, 0))]


def _small_sum_scratch():
    return [pltpu.VMEM((SMALL_ROWS, D), F32), pltpu.VMEM((N_DEV, SMALL_ROWS, D), F32),
            pltpu.SemaphoreType.DMA((7,)), pltpu.SemaphoreType.DMA((7,))]


def _small_sum_schedule(small_refs, out_ref, p_ref, land, send_sems, recv_sems):
    dlead_ref, dwg_ref, gnm_ref, gnf_ref, gfn_ref, dbg_ref, dgnw_ref, loss_ref, dsink_ref = small_refs
    x, y, c = _mesh_pos()
    me = 4 * x + 2 * y + c

    def copies():
        res = []
        for k in range(1, N_DEV):
            bx, by, bc = (k >> 2) & 1, (k >> 1) & 1, k & 1
            peer = (1 - x if bx else x, 1 - y if by else y, 1 - c if bc else c)
            res.append(pltpu.make_async_remote_copy(
                src_ref=p_ref, dst_ref=land.at[me], send_sem=send_sems.at[k - 1], recv_sem=recv_sems.at[k - 1],
                device_id=peer, device_id_type=MESH))
        return res

    def start():
        p_ref[...] = jnp.zeros_like(p_ref)
        p_ref[R_META:R_META + N_META, :] = dlead_ref[META0:LEAD, :]
        p_ref[R_WG:R_WG + GLA_RANK, 0:256] = dwg_ref[0:GLA_RANK, :]
        p_ref[R_NORM_MIX:R_NORM_MIX + 1, :] = gnm_ref[0:1, :]
        p_ref[R_NORM_FF:R_NORM_FF + 1, :] = gnf_ref[0:1, :]
        p_ref[R_FINAL:R_FINAL + 1, :] = gfn_ref[0:1, :]
        p_ref[R_B_GATE:R_B_GATE + 1, 0:256] = dbg_ref[0:1, :]
        p_ref[R_GLA_NORM:R_GLA_NORM + 1, 0:128] = dgnw_ref[0:1, :]
        p_ref[R_LOSS:R_LOSS + 1, 0:128] = loss_ref[0:1, :]
        p_ref[R_SINKS:R_SINKS + SWA_HEADS, 0:128] = dsink_ref[...]
        land[me] = p_ref[...]
        for cp in copies():
            cp.start()

    def finish():
        for cp in copies():
            cp.wait_recv()
        for cp in copies():
            cp.wait_send()
        acc = land[0]
        for d in range(1, N_DEV):
            acc = acc + land[d]
        out_ref[...] = acc

    return start, finish


def _token_specs(tm, grid_rank=1):
    nb = tm // LEAD

    def spec(k):
        if grid_rank == 1:
            return pl.BlockSpec((LEAD, D), lambda i: (jnp.maximum(i * nb + k - 1, 0), 0))
        return pl.BlockSpec((LEAD, D), lambda i, j: (jnp.maximum(i * nb + k - 1, 0), 0))

    return [spec(k) for k in range(nb)]


def _h_tile(i, lead_ref, x_refs):
    first = jnp.where(i == 0, lead_ref[...], x_refs[0][...])
    return jnp.concatenate([first] + [r[...] for r in x_refs[1:]], axis=0)


def _in_proj(x, lead, nw, win_p, angles, tm, shards):
    rows = LEAD + x.shape[0]
    nb = tm // LEAD
    steps = rows // tm
    ns = len(shards)

    def body(*refs):
        x_refs, refs = refs[:nb], refs[nb:]
        lead_ref, nw_ref, w_ref, cs_ref = refs[:4]
        shard_refs, (o_ref, q_ref, k_ref, v_ref) = refs[4:4 + ns], refs[4 + ns:8 + ns]
        _place_gather(pl.program_id(0), steps, shard_refs, refs[8 + ns:8 + 2 * ns], refs[8 + 2 * ns:])
        h = _h_tile(pl.program_id(0), lead_ref, x_refs)
        rstd = lax.rsqrt(jnp.mean(h * h, axis=-1, keepdims=True) + EPS)
        u = (h * rstd * nw_ref[...]).astype(MXU_DTYPE)
        proj = jnp.dot(u, w_ref[...].astype(MXU_DTYPE), preferred_element_type=F32)
        o_ref[...] = proj[:, 0:DGLA]
        cos, sa, sb = _rope_tables(cs_ref[...])
        q_ref[...] = (_rope(proj[:, C_SQ:C_SK], cos, sa, sb) * (SWA_HD ** -0.5)).astype(ACT_DTYPE)
        k_ref[...] = _rope(proj[:, C_SK:C_SV], cos, sa, sb).astype(ACT_DTYPE)
        v_ref[...] = proj[:, C_SV:DINP].astype(ACT_DTYPE)

    row = lambda w: pl.BlockSpec((tm, w), lambda i: (i, 0))
    outs = pl.pallas_call(
        body, name="in_proj", grid=(steps,),
        in_specs=_token_specs(tm) + [pl.BlockSpec((LEAD, D), lambda i: (0, 0)), pl.BlockSpec((1, D), lambda i: (0, 0)),
                                     pl.BlockSpec((D, DINP), lambda i: (0, 0)), row(ROPE_DIM)]
        + [ANY] * ns,
        out_specs=[row(DGLA), row(512), row(128), row(128)] + [ANY] * ns,
        out_shape=[jax.ShapeDtypeStruct((rows, DGLA), F32), jax.ShapeDtypeStruct((rows, 512), ACT_DTYPE),
                   jax.ShapeDtypeStruct((rows, 128), ACT_DTYPE), jax.ShapeDtypeStruct((rows, 128), ACT_DTYPE)]
        + _gathered_shapes(shards),
        scratch_shapes=_gather_sems(ns),
        compiler_params=_cp(("arbitrary",), VMEM_WIDE_MB),
    )(*([x] * nb), lead, nw, win_p, angles, *shards)
    return outs[0], outs[1], outs[2], outs[3], _with_own_block(outs[4:], shards)


def _rope_angles(rows):
    pos = (jnp.arange(rows, dtype=jnp.int32) - META0).astype(F32)
    inv_freq = 1.0 / (ROPE_THETA ** (jnp.arange(0, ROPE_DIM, 2, dtype=F32) / ROPE_DIM))
    ang = pos[:, None] * inv_freq[None, :]
    return jnp.concatenate([jnp.cos(ang), jnp.sin(ang)], axis=1)


def _rope_tables(cs):
    shape = (2 * (ROPE_DIM // 2), 3 * 128)
    j = lax.broadcasted_iota(jnp.int32, shape, 0)
    col = lax.broadcasted_iota(jnp.int32, shape, 1)
    table, in_head = col // 128, col % SWA_HD
    match = (j % (ROPE_DIM // 2)) == (in_head % (ROPE_DIM // 2))
    is_cos = j < ROPE_DIM // 2
    first, second = in_head < ROPE_DIM // 2, (in_head >= ROPE_DIM // 2) & (in_head < ROPE_DIM)
    spread = (jnp.where(match & is_cos & (table == 0) & (first | second), 1.0, 0.0)
              + jnp.where(match & ~is_cos & (table == 1) & first, -1.0, 0.0)
              + jnp.where(match & ~is_cos & (table == 2) & second, 1.0, 0.0)).astype(jnp.bfloat16)
    hi = cs.astype(jnp.bfloat16)
    rest = cs - hi.astype(F32)
    mid = rest.astype(jnp.bfloat16)
    low = (rest - mid.astype(F32)).astype(jnp.bfloat16)
    dot = lambda part: jnp.dot(part, spread, preferred_element_type=F32)
    tabs = dot(hi) + (dot(mid) + dot(low))
    lane = lax.broadcasted_iota(jnp.int32, (1, 128), 1) % SWA_HD
    return tabs[:, 0:128] + jnp.where(lane >= ROPE_DIM, 1.0, 0.0), tabs[:, 128:256], tabs[:, 256:384]


def _rope(xv, cos, sa, sb):
    width = xv.shape[1]
    reps = width // 128
    if reps > 1:
        cos, sa, sb = (jnp.tile(t, (1, reps)) for t in (cos, sa, sb))
    return xv * cos + pltpu.roll(xv, width - 8, 1) * sa + pltpu.roll(xv, 8, 1) * sb


def _unrope(dy, cos, sa, sb):
    width = dy.shape[1]
    reps = width // 128
    if reps > 1:
        cos, sa, sb = (jnp.tile(t, (1, reps)) for t in (cos, sa, sb))
    return dy * cos + pltpu.roll(dy * sa, 8, 1) + pltpu.roll(dy * sb, width - 8, 1)


def _gla_group(nc, most):
    for g in (10, 5, 2):
        if g <= most and nc % g == 0:
            return g
    return 1


def _chunk_masks(nrows):
    ii = lax.broadcasted_iota(jnp.int32, (nrows, nrows), 0)
    jj = lax.broadcasted_iota(jnp.int32, (nrows, nrows), 1)
    same = (ii // GLA_CHUNK) == (jj // GLA_CHUNK)
    return same & (jj <= ii), same & (jj >= ii)


def _gla_gates(lr, wg, bg, first_row):
    nrows = lr.shape[0]
    zg = _mm(lr, wg) + bg
    live = first_row + lax.broadcasted_iota(jnp.int32, (nrows, 1), 0) >= META0
    g = jnp.where(live, _logsigmoid(zg) * (1.0 / GLA_TAU), 0.0)
    return _masked_sums(_chunk_masks(nrows)[0], g), jnp.where(live, _sigmoid(-zg) * (1.0 / GLA_TAU), 0.0)


def _tril64():
    ii = lax.broadcasted_iota(jnp.int32, (GLA_CHUNK, GLA_CHUNK), 0)
    jj = lax.broadcasted_iota(jnp.int32, (GLA_CHUNK, GLA_CHUNK), 1)
    return jj <= ii


def _gla_fwd(proj, wg_p, bg, gnw, shards):
    rows = proj.shape[0]
    nc = rows // GLA_CHUNK
    group = _gla_group(nc, 10)
    steps, nrows = nc // group, group * GLA_CHUNK
    ns = len(shards)

    def body(q_ref, k_ref, v_ref, r_ref, lr_ref, lr_next_ref, wg_ref, bg_ref, gnw_ref, *rest):
        shard_refs, rest = rest[:ns], rest[ns:]
        oraw_ref, og_ref, st_ref, decay_ref, dgate_ref = rest[:5]
        gathered_refs, rest = rest[5:5 + ns], rest[5 + ns:]
        state, gates = rest[:2]
        c = pl.program_id(0)

        @pl.when(c == 0)
        def _():
            state[...] = jnp.zeros_like(state)
            gates[0, 0], gates[0, 1] = _gla_gates(lr_ref[...], wg_ref[...], bg_ref[...], 0)

        _place_gather(c, steps, shard_refs, gathered_refs, rest[2:])
        slot = c % 2
        b = gates[slot, 0]
        decay_ref[...] = b
        dgate_ref[...] = gates[slot, 1]
        gates[1 - slot, 0], gates[1 - slot, 1] = _gla_gates(lr_next_ref[...], wg_ref[...], bg_ref[...], (c + 1) * nrows)
        eb = jnp.exp(b)
        gq = q_ref[...] * (GLA_DK ** -0.5) * eb
        gk = k_ref[...] * jnp.exp(-b)
        v = v_ref[...]
        gnw_v = gnw_ref[...]
        tril = _tril64()
        pairs = [(h, gi) for h in range(GLA_HEADS) for gi in range(group)]
        rs = {gi: slice(gi * GLA_CHUNK, (gi + 1) * GLA_CHUNK) for gi in range(group)}
        s64 = {h: slice(h * GLA_DK, (h + 1) * GLA_DK) for h in range(GLA_HEADS)}
        s128 = {h: slice(h * GLA_DV, (h + 1) * GLA_DV) for h in range(GLA_HEADS)}
        qh = {(h, gi): gq[rs[gi], s64[h]] for h, gi in pairs}
        kh = {(h, gi): gk[rs[gi], s64[h]] for h, gi in pairs}
        vh = {(h, gi): v[rs[gi], s128[h]] for h, gi in pairs}
        ebl = {(h, gi): eb[(gi + 1) * GLA_CHUNK - 1:(gi + 1) * GLA_CHUNK, s64[h]] for h, gi in pairs}
        av = {pr: _mm(jnp.where(tril, _mm_nt(qh[pr], kh[pr]), 0.0), vh[pr]) for pr in pairs}
        inc = {pr: _mm_tn(vh[pr], kh[pr] * ebl[pr]) for pr in pairs}
        st = {}
        for h in range(GLA_HEADS):
            cur = state[h]
            for gi in range(group):
                st[h, gi] = cur
                st_ref[gi, h] = cur
                cur = cur * ebl[h, gi] + inc[h, gi]
            state[h] = cur
        for h, gi in pairs:
            o = av[h, gi] + _mm_nt(qh[h, gi], st[h, gi])
            oraw_ref[rs[gi], s128[h]] = o
            rstd = lax.rsqrt(jnp.mean(o * o, axis=-1, keepdims=True) + EPS)
            rh = r_ref[rs[gi], s128[h]]
            og_ref[rs[gi], s128[h]] = (o * rstd * gnw_v * (rh * _sigmoid(rh))).astype(ACT_DTYPE)

    nb = lambda w, col: pl.BlockSpec((nrows, w), lambda c: (c, col // w))
    const = lambda shape: pl.BlockSpec(shape, lambda c: (0,) * len(shape))
    outs = pl.pallas_call(
        body, name="gla_fwd", grid=(steps,),
        in_specs=[nb(256, C_GQ), nb(256, C_GK), nb(512, C_GV), nb(512, C_GR), nb(128, C_LR),
                  pl.BlockSpec((nrows, 128), lambda c: (jnp.minimum(c + 1, steps - 1), C_LR // 128)),
                  const((128, 256)), const((1, 256)), const((1, 128))] + [ANY] * ns,
        out_specs=[nb(512, 0), nb(512, 0),
                   pl.BlockSpec((group, GLA_HEADS, GLA_DV, GLA_DK), lambda c: (c, 0, 0, 0)),
                   nb(256, 0), nb(256, 0)] + [ANY] * ns,
        out_shape=[jax.ShapeDtypeStruct((rows, 512), F32), jax.ShapeDtypeStruct((rows, 512), ACT_DTYPE),
                   jax.ShapeDtypeStruct((nc, GLA_HEADS, GLA_DV, GLA_DK), F32),
                   jax.ShapeDtypeStruct((rows, 256), F32), jax.ShapeDtypeStruct((rows, 256), F32)]
        + _gathered_shapes(shards),
        scratch_shapes=[pltpu.VMEM((GLA_HEADS, GLA_DV, GLA_DK), F32), pltpu.VMEM((2, 2, nrows, 256), F32)]
        + _gather_sems(ns),
        compiler_params=_cp(("arbitrary",)),
    )(proj, proj, proj, proj, proj, proj, wg_p, bg, gnw, *shards)
    return outs[0], outs[1], outs[2], outs[3], outs[4], _with_own_block(outs[5:], shards)


def _swa_mask(n):
    shape = (SWA_GROUP * SWA_BLOCK, 3 * SWA_BLOCK)
    qi = lax.broadcasted_iota(jnp.int32, shape, 0) & (SWA_BLOCK - 1)
    jj = lax.broadcasted_iota(jnp.int32, shape, 1)
    meta = (jj < SWA_BLOCK) & (jj >= META0) & ((n > 0) | (jj <= qi))
    prev = (jj >= SWA_BLOCK) & (jj < 2 * SWA_BLOCK) & (n >= 2) & (jj - SWA_BLOCK > qi)
    cur = (jj >= 2 * SWA_BLOCK) & (n >= 1) & (jj - 2 * SWA_BLOCK <= qi)
    return meta | prev | cur


def _stack_heads(t, kvh):
    return jnp.concatenate([t[:, (kvh * SWA_GROUP + g) * SWA_HD:(kvh * SWA_GROUP + g + 1) * SWA_HD]
                            for g in range(SWA_GROUP)], axis=0)


def _stack_sinks(sink_ref, kvh):
    return jnp.concatenate([jnp.full((SWA_BLOCK, 1), sink_ref[0, kvh * SWA_GROUP + g], F32)
                            for g in range(SWA_GROUP)], axis=0)


def _swa_group(nblk):
    return 5 if nblk % 5 == 0 else 1


def _swa_specs(group):
    blk = lambda w: pl.BlockSpec((group * SWA_BLOCK, w), lambda n: (n, 0))
    first = pl.BlockSpec((SWA_BLOCK, 128), lambda n: (0, 0))
    prev = pl.BlockSpec((SWA_BLOCK, 128), lambda n: (jnp.maximum(n * group - 1, 0), 0))
    return blk, first, prev


def _swa_keys(first_ref, prev_ref, cur_ref, g):
    own = cur_ref[g * SWA_BLOCK:(g + 1) * SWA_BLOCK, :]
    before = prev_ref[...] if g == 0 else cur_ref[(g - 1) * SWA_BLOCK:g * SWA_BLOCK, :]
    return jnp.concatenate([first_ref[...], before, own], axis=0)


def _swa_fwd(qr, kr, vr, sinks, shards):
    rows = qr.shape[0]
    nblk = rows // SWA_BLOCK
    group = _swa_group(nblk)
    steps = nblk // group
    ns = len(shards)

    def body(q_ref, k0, kp, kc, v0, vp, vc, sink_ref, *rest):
        o_ref = rest[ns]
        _place_gather(pl.program_id(0), steps, rest[:ns], rest[ns + 1:2 * ns + 1], rest[2 * ns + 1:])
        for g in range(group):
            n = pl.program_id(0) * group + g
            rs = slice(g * SWA_BLOCK, (g + 1) * SWA_BLOCK)
            kall, vall = _swa_keys(k0, kp, kc, g), _swa_keys(v0, vp, vc, g)
            mask = _swa_mask(n)[0:SWA_BLOCK]
            heads = range(SWA_HEADS)
            hs = [slice(h * SWA_HD, (h + 1) * SWA_HD) for h in heads]
            kv = [slice((h // SWA_GROUP) * SWA_HD, (h // SWA_GROUP + 1) * SWA_HD) for h in heads]
            s = [jnp.where(mask, _mm_nt(q_ref[rs, hs[h]], kall[:, kv[h]]), NEG) for h in heads]
            m = [jnp.maximum(jnp.max(s[h], axis=-1, keepdims=True), sink_ref[0, h]) for h in heads]
            p = [jnp.exp(s[h] - m[h]) for h in heads]
            den = [jnp.sum(p[h], axis=-1, keepdims=True) + jnp.exp(sink_ref[0, h] - m[h]) for h in heads]
            o = [_mm(p[h], vall[:, kv[h]]) for h in heads]
            for h in heads:
                o_ref[rs, hs[h]] = (o[h] / den[h]).astype(ACT_DTYPE)

    blk, first, prev = _swa_specs(group)
    outs = pl.pallas_call(
        body, name="swa_fwd", grid=(steps,),
        in_specs=[blk(512), first, prev, blk(128), first, prev, blk(128),
                  pl.BlockSpec(memory_space=pltpu.SMEM)] + [ANY] * ns,
        out_specs=[blk(512)] + [ANY] * ns,
        out_shape=[jax.ShapeDtypeStruct((rows, 512), ACT_DTYPE)] + _gathered_shapes(shards),
        scratch_shapes=_gather_sems(ns),
        compiler_params=_cp(("arbitrary",)),
    )(qr, kr, kr, kr, vr, vr, vr, sinks, *shards)
    return outs[0], _with_own_block(outs[1:], shards)


def _out_proj(x, lead, og, osw, wout, nfw, tm):
    rows = LEAD + x.shape[0]
    nb = tm // LEAD

    def body(*refs):
        x_refs, (lead_ref, og_ref, os_ref, w_ref, nw_ref, h1_ref, f_ref, ft_ref) = refs[:nb], refs[nb:]
        h0 = _h_tile(pl.program_id(0), lead_ref, x_refs)
        h1 = h0 + _mm(og_ref[...], w_ref[0:512, :]) + _mm(os_ref[...], w_ref[512:1024, :])
        h1_ref[...] = h1
        rstd = lax.rsqrt(jnp.mean(h1 * h1, axis=-1, keepdims=True) + EPS)
        f = h1 * rstd * nw_ref[...]
        f_ref[...] = f.astype(ACT_DTYPE)
        ft_ref[...] = f.T.astype(ACT_DTYPE)

    row = lambda w: pl.BlockSpec((tm, w), lambda i: (i, 0))
    return pl.pallas_call(
        body, name="out_proj", grid=(rows // tm,),
        in_specs=_token_specs(tm) + [pl.BlockSpec((LEAD, D), lambda i: (0, 0)), row(512), row(512),
                                     pl.BlockSpec((D, D), lambda i: (0, 0)), pl.BlockSpec((1, D), lambda i: (0, 0))],
        out_specs=[row(D), row(D), pl.BlockSpec((D, tm), lambda i: (0, i))],
        out_shape=[jax.ShapeDtypeStruct((rows, D), F32), jax.ShapeDtypeStruct((rows, D), ACT_DTYPE),
                   jax.ShapeDtypeStruct((D, rows), ACT_DTYPE)],
        compiler_params=_cp(("arbitrary",), VMEM_TILE_MB),
    )(*([x] * nb), lead, og, osw, wout, nfw)


def _ffn_fwd(f, h1, w1, w2, tgt, fnw, tm):
    rows = f.shape[0]
    nj = D_FF // FF_WIDE
    nb = tm // LEAD

    def body(f_ref, h1_ref, w1_ref, w2_ref, nw_ref, *rest):
        t_refs, (a_ref, dh2_ref, dh2t_ref, loss_ref, gfn_ref, acc) = rest[:nb], rest[nb:]
        i, j = pl.program_id(0), pl.program_id(1)

        @pl.when((i == 0) & (j == 0))
        def _():
            loss_ref[...] = jnp.zeros_like(loss_ref)
            gfn_ref[...] = jnp.zeros_like(gfn_ref)

        @pl.when(j == 0)
        def _():
            acc[...] = jnp.zeros_like(acc)

        a = _mm(f_ref[...], w1_ref[...])
        a_ref[...] = a.astype(ACT_DTYPE)
        z = jnp.square(jnp.maximum(a, 0.0))
        acc[...] += _mm(z, w2_ref[...])

        @pl.when(j == nj - 1)
        def _():
            h2 = h1_ref[...] + acc[...]
            rstd = lax.rsqrt(jnp.mean(h2 * h2, axis=-1, keepdims=True) + EPS)
            hn = h2 * rstd
            nw = nw_ref[...]
            row = i * tm + lax.broadcasted_iota(jnp.int32, (tm, 1), 0)
            target = jnp.concatenate([t[...] for t in t_refs], axis=0)
            err = jnp.where(row >= LEAD, hn * nw - target, 0.0)
            row_loss = jnp.sum(err * err, axis=-1, keepdims=True) * (1.0 / D)
            loss_ref[...] += jnp.broadcast_to(0.5 * jnp.sum(row_loss, axis=0, keepdims=True), loss_ref.shape)
            dy = err * (1.0 / D)
            gfn_ref[...] += jnp.broadcast_to(jnp.sum(dy * hn, axis=0, keepdims=True), gfn_ref.shape)
            dhn = dy * nw
            dh2 = rstd * (dhn - hn * jnp.mean(dhn * hn, axis=-1, keepdims=True))
            dh2_ref[...] = dh2
            dh2t_ref[...] = dh2.T.astype(ACT_DTYPE)

    return pl.pallas_call(
        body, name="ffn_fwd", grid=(rows // tm, nj),
        in_specs=[pl.BlockSpec((tm, D), lambda i, j: (i, 0)), pl.BlockSpec((tm, D), lambda i, j: (i, 0)),
                  pl.BlockSpec((D, FF_WIDE), lambda i, j: (0, j)),
                  pl.BlockSpec((FF_WIDE, D), lambda i, j: (j, 0)),
                  pl.BlockSpec((1, D), lambda i, j: (0, 0))] + _token_specs(tm, grid_rank=2),
        out_specs=[pl.BlockSpec((tm, FF_WIDE), lambda i, j: (i, j)), pl.BlockSpec((tm, D), lambda i, j: (i, 0)),
                   pl.BlockSpec((D, tm), lambda i, j: (0, i)),
                   pl.BlockSpec((8, 128), lambda i, j: (0, 0)), pl.BlockSpec((8, D), lambda i, j: (0, 0))],
        out_shape=[jax.ShapeDtypeStruct((rows, D_FF), ACT_DTYPE), jax.ShapeDtypeStruct((rows, D), F32),
                   jax.ShapeDtypeStruct((D, rows), ACT_DTYPE),
                   jax.ShapeDtypeStruct((8, 128), F32), jax.ShapeDtypeStruct((8, D), F32)],
        scratch_shapes=[pltpu.VMEM((tm, D), F32)],
        compiler_params=_cp(("arbitrary", "arbitrary"), VMEM_WIDE_MB),
    )(f, h1, w1, w2, fnw, *([tgt] * nb))


def _ffn_bwd_act(dh2, a, w1, w2, h1, nfw, tm):
    rows = dh2.shape[0]
    nj = D_FF // FF_WIDE

    def body(dh2_ref, a_ref, w1_ref, w2_ref, h1_ref, nw_ref, da_ref, dh1_ref, gnf_ref, acc):
        i, j = pl.program_id(0), pl.program_id(1)

        @pl.when((i == 0) & (j == 0))
        def _():
            gnf_ref[...] = jnp.zeros_like(gnf_ref)

        @pl.when(j == 0)
        def _():
            acc[...] = jnp.zeros_like(acc)

        dz = _mm_nt(dh2_ref[...], w2_ref[...])
        da = dz * (2.0 * jnp.maximum(a_ref[...].astype(F32), 0.0))
        da_ref[...] = da.astype(ACT_DTYPE)
        acc[...] += _mm_nt(da, w1_ref[...])

        @pl.when(j == nj - 1)
        def _():
            h1 = h1_ref[...]
            rstd = lax.rsqrt(jnp.mean(h1 * h1, axis=-1, keepdims=True) + EPS)
            hn = h1 * rstd
            df = acc[...]
            gnf_ref[...] += jnp.broadcast_to(jnp.sum(df * hn, axis=0, keepdims=True), gnf_ref.shape)
            dfn = df * nw_ref[...]
            dh1_ref[...] = dh2_ref[...] + rstd * (dfn - hn * jnp.mean(dfn * hn, axis=-1, keepdims=True))

    return pl.pallas_call(
        body, name="ffn_bwd_act", grid=(rows // tm, nj),
        in_specs=[pl.BlockSpec((tm, D), lambda i, j: (i, 0)), pl.BlockSpec((tm, FF_WIDE), lambda i, j: (i, j)),
                  pl.BlockSpec((D, FF_WIDE), lambda i, j: (0, j)),
                  pl.BlockSpec((FF_WIDE, D), lambda i, j: (j, 0)),
                  pl.BlockSpec((tm, D), lambda i, j: (i, 0)), pl.BlockSpec((1, D), lambda i, j: (0, 0))],
        out_specs=[pl.BlockSpec((tm, FF_WIDE), lambda i, j: (i, j)), pl.BlockSpec((tm, D), lambda i, j: (i, 0)),
                   pl.BlockSpec((8, D), lambda i, j: (0, 0))],
        out_shape=[jax.ShapeDtypeStruct((rows, D_FF), ACT_DTYPE), jax.ShapeDtypeStruct((rows, D), F32),
                   jax.ShapeDtypeStruct((8, D), F32)],
        scratch_shapes=[pltpu.VMEM((tm, D), F32)],
        compiler_params=_cp(("arbitrary", "arbitrary"), VMEM_WIDE_MB),
    )(dh2, a, w1, w2, h1, nfw)


def _ffn_bwd_weights(ft, a, da, dh2t, tm):
    rows = a.shape[0]
    steps = rows // tm
    pair = 2 * FF_TILE

    def body(ft_ref, a_ref, da_ref, dh2t_ref, dw1_ref, dw2_ref, dw2t):
        i = pl.program_id(1)

        @pl.when(i == 0)
        def _():
            dw1_ref[...] = jnp.zeros_like(dw1_ref)
            dw2t[...] = jnp.zeros_like(dw2t)

        z = jnp.square(jnp.maximum(a_ref[...].astype(F32), 0.0))
        dw1 = _mm(ft_ref[...], da_ref[...])
        for core in range(2):
            dw1_ref[core] += dw1[:, core * FF_TILE:(core + 1) * FF_TILE]
        dw2t[...] += _mm(dh2t_ref[...], z)

        @pl.when(i == steps - 1)
        def _():
            for core in range(2):
                dw2_ref[core] = dw2t[:, core * FF_TILE:(core + 1) * FF_TILE].T

    return pl.pallas_call(
        body, name="ffn_bwd_weights", grid=(N_DEV // 2, steps),
        in_specs=[pl.BlockSpec((D, tm), lambda j, i: (0, i)), pl.BlockSpec((tm, pair), lambda j, i: (i, j)),
                  pl.BlockSpec((tm, pair), lambda j, i: (i, j)), pl.BlockSpec((D, tm), lambda j, i: (0, i))],
        out_specs=[pl.BlockSpec((2, None, D, FF_TILE), lambda j, i: (0, j, 0, 0)),
                   pl.BlockSpec((2, None, FF_TILE, D), lambda j, i: (0, j, 0, 0))],
        out_shape=[jax.ShapeDtypeStruct((2, 4, D, FF_TILE), F32), jax.ShapeDtypeStruct((2, 4, FF_TILE, D), F32)],
        scratch_shapes=[pltpu.VMEM((D, pair), F32)],
        compiler_params=_cp(("arbitrary", "arbitrary"), VMEM_WIDE_MB),
    )(ft, a, da, dh2t)


def _out_proj_bwd(dh1, og, osw, wout, tm, partials):
    rows = dh1.shape[0]
    steps = rows // tm
    ns = len(partials)

    def body(dh1_ref, og_ref, os_ref, w_ref, *rest):
        part_refs, rest = rest[:ns], rest[ns:]
        dog_ref, dos_ref, dw_ref = rest[:3]
        land_refs, (send_sems, recv_sems) = rest[3:3 + ns], rest[3 + ns:]
        i = pl.program_id(0)
        start, finish = _sibling_schedule(part_refs, land_refs, send_sems, recv_sems)

        @pl.when(i == 0)
        def _():
            dw_ref[...] = jnp.zeros_like(dw_ref)
            start()

        pl.when(i == steps - 1)(finish)

        dh1 = dh1_ref[...].astype(MXU_DTYPE)
        dog_ref[...] = _mm_nt(dh1, w_ref[0:512, :])
        dos_ref[...] = _mm_nt(dh1, w_ref[512:1024, :])
        for half, ref in enumerate((og_ref, os_ref)):
            dw = _mm_tn(ref[...], dh1)
            for blk in range(4):
                shard = half * 4 + blk
                dw_ref[shard % 2, shard // 2] += dw[blk * 128:(blk + 1) * 128, :]

    row = lambda w: pl.BlockSpec((tm, w), lambda i: (i, 0))
    outs = pl.pallas_call(
        body, name="out_proj_bwd", grid=(steps,),
        in_specs=[row(D), row(512), row(512), pl.BlockSpec((D, D), lambda i: (0, 0))] + [ANY] * ns,
        out_specs=[row(512), row(512), pl.BlockSpec((2, 4, 128, D), lambda i: (0, 0, 0, 0))] + [ANY] * ns,
        out_shape=[jax.ShapeDtypeStruct((rows, 512), F32), jax.ShapeDtypeStruct((rows, 512), F32),
                   jax.ShapeDtypeStruct((2, 4, 128, D), F32)] + _sibling_shapes(partials),
        scratch_shapes=_sibling_sems(ns),
        compiler_params=_cp(("arbitrary",), VMEM_TILE_MB),
    )(dh1, og, osw, wout, *partials)
    return outs[0], outs[1], outs[2], outs[3:]


def _swa_bwd(qr, kr, vr, osw, dos, sinks, jobs):
    rows = qr.shape[0]
    nblk = rows // SWA_BLOCK
    group = _swa_group(nblk)
    steps = nblk // group
    ns = jobs.n

    def body(q_ref, k0, kp, kc, v0, vp, vc, o_ref, do_ref, sink_ref, *rest):
        dq_ref, dk_ref, dv_ref, dsink_ref = rest[ns:ns + 4]
        start, finish = jobs.bind(rest[:ns], rest[ns + 4:2 * ns + 4], rest[2 * ns + 4:])
        step = pl.program_id(0)

        @pl.when(step == 0)
        def _():
            dk_ref[...] = jnp.zeros_like(dk_ref)
            dv_ref[...] = jnp.zeros_like(dv_ref)
            dsink_ref[...] = jnp.zeros_like(dsink_ref)
            start()

        pl.when(step == steps - 1)(finish)
        for g in range(group):
            block(step * group + g, g, q_ref, k0, kp, kc, v0, vp, vc, o_ref, do_ref, sink_ref,
                  dq_ref, dk_ref, dv_ref, dsink_ref)

    def block(n, g, q_ref, k0, kp, kc, v0, vp, vc, o_ref, do_ref, sink_ref, dq_ref, dk_ref, dv_ref, dsink_ref):
        rs = slice(g * SWA_BLOCK, (g + 1) * SWA_BLOCK)
        kall, vall = _swa_keys(k0, kp, kc, g), _swa_keys(v0, vp, vc, g)
        mask = _swa_mask(n)[0:SWA_BLOCK]
        heads = range(SWA_HEADS)
        hs = [slice(h * SWA_HD, (h + 1) * SWA_HD) for h in heads]
        kv = [slice((h // SWA_GROUP) * SWA_HD, (h // SWA_GROUP + 1) * SWA_HD) for h in heads]
        sink = [sink_ref[0, h] for h in heads]
        qh = [q_ref[rs, hs[h]] for h in heads]
        doh = [do_ref[rs, hs[h]] for h in heads]
        s = [jnp.where(mask, _mm_nt(qh[h], kall[:, kv[h]]), NEG) for h in heads]
        dp = [_mm_nt(doh[h], vall[:, kv[h]]) for h in heads]
        delta = [jnp.sum(doh[h] * o_ref[rs, hs[h]].astype(F32), axis=-1, keepdims=True) for h in heads]
        m = [jnp.maximum(jnp.max(s[h], axis=-1, keepdims=True), sink[h]) for h in heads]
        e = [jnp.exp(s[h] - m[h]) for h in heads]
        inv = [1.0 / (jnp.sum(e[h], axis=-1, keepdims=True) + jnp.exp(sink[h] - m[h])) for h in heads]
        p = [e[h] * inv[h] for h in heads]
        ds = [p[h] * (dp[h] - delta[h]) for h in heads]
        dq = [_mm(ds[h], kall[:, kv[h]]) for h in heads]
        dkh = [_mm_tn(ds[h], qh[h]) for h in heads]
        dvh = [_mm_tn(p[h], doh[h]) for h in heads]
        for h in heads:
            dsink = -jnp.sum(jnp.exp(sink[h] - m[h]) * inv[h] * delta[h], axis=0, keepdims=True)
            dsink_ref[h:h + 1, :] += jnp.broadcast_to(dsink, (1, 128))
        dq_ref[rs, :] = jnp.concatenate(dq, axis=1).astype(ACT_DTYPE)
        group_sum = lambda parts, kvh: sum(parts[kvh * SWA_GROUP + 1:(kvh + 1) * SWA_GROUP], parts[kvh * SWA_GROUP])
        dk_all = jnp.concatenate([group_sum(dkh, kvh) for kvh in range(SWA_KV)], axis=1)
        dv_all = jnp.concatenate([group_sum(dvh, kvh) for kvh in range(SWA_KV)], axis=1)
        prev0 = pl.multiple_of(jnp.maximum(n - 1, 0) * SWA_BLOCK, SWA_BLOCK)
        cur0 = pl.multiple_of(n * SWA_BLOCK, SWA_BLOCK)
        for ref, val in ((dk_ref, dk_all), (dv_ref, dv_all)):
            ref[0:SWA_BLOCK, :] += val[0:SWA_BLOCK]
            ref[pl.ds(prev0, SWA_BLOCK), :] += val[SWA_BLOCK:2 * SWA_BLOCK]
            ref[pl.ds(cur0, SWA_BLOCK), :] += val[2 * SWA_BLOCK:]

    blk, first, prev = _swa_specs(group)
    whole = pl.BlockSpec((rows, 128), lambda n: (0, 0))
    outs = pl.pallas_call(
        body, name="swa_bwd", grid=(steps,),
        in_specs=[blk(512), first, prev, blk(128), first, prev, blk(128), blk(512), blk(512),
                  pl.BlockSpec(memory_space=pltpu.SMEM)] + [ANY] * ns,
        out_specs=[blk(512), whole, whole, pl.BlockSpec((8, 128), lambda n: (0, 0))] + [ANY] * ns,
        out_shape=[jax.ShapeDtypeStruct((rows, 512), ACT_DTYPE), jax.ShapeDtypeStruct((rows, 128), F32),
                   jax.ShapeDtypeStruct((rows, 128), F32), jax.ShapeDtypeStruct((8, 128), F32)] + jobs.out_shapes,
        scratch_shapes=jobs.sems,
        compiler_params=_cp(("arbitrary",), VMEM_TILE_MB),
    )(qr, kr, kr, kr, vr, vr, vr, osw, dos, sinks, *jobs.inputs)
    return outs[0], outs[1], outs[2], outs[3], jobs.split(outs[4:])


def _gla_bwd(proj, decay, dgate, oraw, states, dog, wg_p, gnw, jobs):
    rows = proj.shape[0]
    nc = rows // GLA_CHUNK
    group = _gla_group(nc, 5)
    steps, nrows = nc // group, group * GLA_CHUNK
    ns = jobs.n

    def body(q_ref, k_ref, v_ref, r_ref, lr_ref, b_ref, dgate_ref, oraw_ref, st_ref, dog_ref, wg_ref, gnw_ref, *rest):
        dq_ref, dk_ref, dv_ref, dr_ref, dlr_ref, dwg_ref, dbg_ref, dgnw_ref = rest[ns:ns + 8]
        dstate, db_scr = rest[2 * ns + 8:2 * ns + 10]
        start, finish = jobs.bind(rest[:ns], rest[ns + 8:2 * ns + 8], rest[2 * ns + 10:])
        t = pl.program_id(0)

        @pl.when(t == 0)
        def _():
            dstate[...] = jnp.zeros_like(dstate)
            dwg_ref[...] = jnp.zeros_like(dwg_ref)
            dbg_ref[...] = jnp.zeros_like(dbg_ref)
            dgnw_ref[...] = jnp.zeros_like(dgnw_ref)
            start()

        pl.when(t == steps - 1)(finish)

        lr, wg = lr_ref[...], wg_ref[...]
        b = b_ref[...]
        eb, enb = jnp.exp(b), jnp.exp(-b)
        scale = GLA_DK ** -0.5
        gq = q_ref[...] * scale * eb
        gk = k_ref[...] * enb
        v = v_ref[...]
        gnw_v = gnw_ref[...]
        tril = _tril64()
        is_last = lax.broadcasted_iota(jnp.int32, (GLA_CHUNK, 1), 0) == GLA_CHUNK - 1
        dgnw = jnp.zeros((1, GLA_DV), F32)
        pairs = [(h, gi) for h in range(GLA_HEADS) for gi in range(group)]
        rs = {gi: slice(gi * GLA_CHUNK, (gi + 1) * GLA_CHUNK) for gi in range(group)}
        s64 = {h: slice(h * GLA_DK, (h + 1) * GLA_DK) for h in range(GLA_HEADS)}
        s128 = {h: slice(h * GLA_DV, (h + 1) * GLA_DV) for h in range(GLA_HEADS)}
        qh = {(h, gi): gq[rs[gi], s64[h]] for h, gi in pairs}
        kh = {(h, gi): gk[rs[gi], s64[h]] for h, gi in pairs}
        vh = {(h, gi): v[rs[gi], s128[h]] for h, gi in pairs}
        ebl = {(h, gi): eb[(gi + 1) * GLA_CHUNK - 1:(gi + 1) * GLA_CHUNK, s64[h]] for h, gi in pairs}
        kl = {pr: kh[pr] * ebl[pr] for pr in pairs}
        st = {(h, gi): st_ref[gi, h] for h, gi in pairs}
        do = {}
        for h, gi in pairs:
            o, rh, dout = oraw_ref[rs[gi], s128[h]], r_ref[rs[gi], s128[h]], dog_ref[rs[gi], s128[h]]
            rstd = lax.rsqrt(jnp.mean(o * o, axis=-1, keepdims=True) + EPS)
            on = o * rstd
            sg = _sigmoid(rh)
            dr_ref[rs[gi], s128[h]] = (dout * (on * gnw_v) * (sg * (1.0 + rh * (1.0 - sg)))).astype(ACT_DTYPE)
            dy = dout * (rh * sg)
            dgnw = dgnw + jnp.sum(dy * on, axis=0, keepdims=True)
            don = dy * gnw_v
            do[h, gi] = rstd * (don - on * jnp.mean(don * on, axis=-1, keepdims=True))
        a = {pr: jnp.where(tril, _mm_nt(qh[pr], kh[pr]), 0.0) for pr in pairs}
        da = {pr: jnp.where(tril, _mm_nt(do[pr], vh[pr]), 0.0) for pr in pairs}
        dinc = {pr: _mm_tn(do[pr], qh[pr]) for pr in pairs}
        dgq = {pr: _mm(da[pr], kh[pr]) + _mm(do[pr], st[pr]) for pr in pairs}
        dgk = {pr: _mm_tn(da[pr], qh[pr]) for pr in pairs}
        dv_a = {pr: _mm_tn(a[pr], do[pr]) for pr in pairs}
        dsp = {}
        for h in range(GLA_HEADS):
            cur = dstate[h]
            for gi in reversed(range(group)):
                dsp[h, gi] = cur
                cur = cur * ebl[h, gi] + dinc[h, gi]
            dstate[h] = cur
        for h, gi in pairs:
            pr = (h, gi)
            dkl = _mm(vh[pr], dsp[pr])
            dv_ref[rs[gi], s128[h]] = (dv_a[pr] + _mm_nt(kl[pr], dsp[pr])).astype(ACT_DTYPE)
            debl = jnp.sum(dsp[pr] * st[pr], axis=0, keepdims=True)
            dq_ref[rs[gi], s64[h]] = (dgq[pr] * (scale * eb[rs[gi], s64[h]])).astype(ACT_DTYPE)
            dk_ref[rs[gi], s64[h]] = ((dgk[pr] + dkl * ebl[pr]) * enb[rs[gi], s64[h]]).astype(ACT_DTYPE)
            last = debl * ebl[pr] + jnp.sum(dkl * kl[pr], axis=0, keepdims=True)
            db_scr[rs[gi], s64[h]] = (dgq[pr] * qh[pr] - dgk[pr] * kh[pr] - dkl * kl[pr]
                                      + jnp.where(is_last, last, 0.0))
        dzg = _masked_sums(_chunk_masks(nrows)[1], db_scr[...]) * dgate_ref[...]
        dlr_ref[...] = _mm_nt(dzg, wg).astype(ACT_DTYPE)
        dwg_ref[...] += _mm_tn(lr, dzg)
        dbg_ref[...] += jnp.broadcast_to(jnp.sum(dzg, axis=0, keepdims=True), dbg_ref.shape)
        dgnw_ref[...] += jnp.broadcast_to(dgnw, dgnw_ref.shape)

    nb = lambda w, col: pl.BlockSpec((nrows, w), lambda t: (steps - 1 - t, col // w))
    const = lambda shape: pl.BlockSpec(shape, lambda t: (0,) * len(shape))
    outs = pl.pallas_call(
        body, name="gla_bwd", grid=(steps,),
        in_specs=[nb(256, C_GQ), nb(256, C_GK), nb(512, C_GV), nb(512, C_GR), nb(128, C_LR), nb(256, 0), nb(256, 0),
                  nb(512, 0),
                  pl.BlockSpec((group, GLA_HEADS, GLA_DV, GLA_DK), lambda t: (steps - 1 - t, 0, 0, 0)), nb(512, 0),
                  const((128, 256)), const((1, 128))] + [ANY] * ns,
        out_specs=[nb(256, 0), nb(256, 0), nb(512, 0), nb(512, 0), nb(128, 0),
                   const((128, 256)), const((8, 256)), const((8, 128))] + [ANY] * ns,
        out_shape=[jax.ShapeDtypeStruct((rows, 256), ACT_DTYPE), jax.ShapeDtypeStruct((rows, 256), ACT_DTYPE),
                   jax.ShapeDtypeStruct((rows, 512), ACT_DTYPE), jax.ShapeDtypeStruct((rows, 512), ACT_DTYPE),
                   jax.ShapeDtypeStruct((rows, 128), ACT_DTYPE), jax.ShapeDtypeStruct((128, 256), F32),
                   jax.ShapeDtypeStruct((8, 256), F32), jax.ShapeDtypeStruct((8, 128), F32)] + jobs.out_shapes,
        scratch_shapes=[pltpu.VMEM((GLA_HEADS, GLA_DV, GLA_DK), F32), pltpu.VMEM((nrows, 256), F32)] + jobs.sems,
        compiler_params=_cp(("arbitrary",)),
    )(proj, proj, proj, proj, proj, decay, dgate, oraw, states, dog, wg_p, gnw, *jobs.inputs)
    return outs[:8], jobs.split(outs[8:])


def _in_proj_bwd(x, lead, dh1, nw, win_p, dgv, dgr, dsq, dgq, dgk, dsk, dsv, dlr, angles, tm):
    seq = x.shape[0]
    rows = LEAD + seq
    nb = tm // LEAD
    steps = rows // tm

    def first_copy(scr, gx_ref, sem):
        return pltpu.make_async_copy(scr.at[pl.ds(LEAD, tm - LEAD)], gx_ref.at[pl.ds(0, tm - LEAD)], sem)

    def tile_copy(scr, gx_ref, sem, step):
        start = pl.multiple_of(jnp.maximum(step * tm - LEAD, 0), LEAD)
        return pltpu.make_async_copy(scr, gx_ref.at[pl.ds(start, tm)], sem)

    def body(*refs):
        x_refs, refs = refs[:nb], refs[nb:]
        (lead_ref, dh1_ref, nw_ref, w_ref, dgv_ref, dgr_ref, dsq_ref, dgq_ref, dgk_ref, dsk_ref, dsv_ref, dlr_ref,
         cs_ref, gx_ref, dlead_ref, dproj_ref, ut_ref, gnm_ref, scr, sem) = refs
        i = pl.program_id(0)

        @pl.when(i == 0)
        def _():
            gnm_ref[...] = jnp.zeros_like(gnm_ref)

        cos, sa, sb = _rope_tables(cs_ref[...])
        dsq_v = (_unrope(dsq_ref[...].astype(F32), cos, sa, sb) * (SWA_HD ** -0.5)).astype(MXU_DTYPE)
        dsk_v = _unrope(dsk_ref[...], cos, sa, sb).astype(MXU_DTYPE)
        dproj = jnp.concatenate(
            [dgv_ref[...].astype(MXU_DTYPE), dgr_ref[...].astype(MXU_DTYPE), dgq_ref[...].astype(MXU_DTYPE),
             dgk_ref[...].astype(MXU_DTYPE), dlr_ref[...].astype(MXU_DTYPE), dsq_v, dsk_v,
             dsv_ref[...].astype(MXU_DTYPE)],
            axis=1)
        dproj_ref[...] = dproj
        h = _h_tile(i, lead_ref, x_refs)
        rstd = lax.rsqrt(jnp.mean(h * h, axis=-1, keepdims=True) + EPS)
        hn = h * rstd
        nw_v = nw_ref[...]
        ut_ref[...] = (hn * nw_v).T.astype(ACT_DTYPE)
        du = _mm_nt(dproj, w_ref[...])
        gnm_ref[...] += jnp.broadcast_to(jnp.sum(du * hn, axis=0, keepdims=True), gnm_ref.shape)
        dun = du * nw_v
        dh0 = dh1_ref[...] + rstd * (dun - hn * jnp.mean(dun * hn, axis=-1, keepdims=True))

        if tm > LEAD:
            pl.when(i == 1)(lambda: first_copy(scr, gx_ref, sem).wait())
        pl.when(i > 1)(lambda: tile_copy(scr, gx_ref, sem, i).wait())
        scr[...] = dh0

        @pl.when(i == 0)
        def _():
            dlead_ref[...] = dh0[0:LEAD]
            if tm > LEAD:
                first_copy(scr, gx_ref, sem).start()
                if steps == 1:
                    first_copy(scr, gx_ref, sem).wait()

        @pl.when(i > 0)
        def _():
            tile_copy(scr, gx_ref, sem, i).start()

        if steps > 1:
            pl.when(i == steps - 1)(lambda: tile_copy(scr, gx_ref, sem, i).wait())

    row = lambda w: pl.BlockSpec((tm, w), lambda i: (i, 0))
    const = lambda shape: pl.BlockSpec(shape, lambda i: (0,) * len(shape))
    return pl.pallas_call(
        body, name="in_proj_bwd", grid=(steps,),
        in_specs=_token_specs(tm) + [const((LEAD, D)), row(D), const((1, D)), const((D, DINP)),
                                     row(512), row(512), row(512), row(256), row(256), row(128), row(128), row(128),
                                     row(ROPE_DIM)],
        out_specs=[ANY, const((LEAD, D)), row(DINP), pl.BlockSpec((D, tm), lambda i: (0, i)), const((8, D))],
        out_shape=[jax.ShapeDtypeStruct((seq, D), F32), jax.ShapeDtypeStruct((LEAD, D), F32),
                   jax.ShapeDtypeStruct((rows, DINP), ACT_DTYPE), jax.ShapeDtypeStruct((D, rows), ACT_DTYPE),
                   jax.ShapeDtypeStruct((8, D), F32)],
        scratch_shapes=[pltpu.VMEM((tm, D), F32), pltpu.SemaphoreType.DMA],
        compiler_params=_cp(("arbitrary",), VMEM_WIDE_MB),
    )(*([x] * nb), lead, dh1, nw, win_p, dgv, dgr, dsq, dgq, dgk, dsk, dsv, dlr, angles)


def _win_runs():
    groups = [(O_GQ, C_GQ), (O_GK, C_GK), (O_GV, C_GV), (O_GR, C_GR), (O_LR, C_LR), (O_SQ, C_SQ), (O_SK, C_SK),
              (O_SV, C_SV)]
    per = DIN // N_DEV
    runs = []
    for (o0, o1), c0 in groups:
        o = o0
        while o < o1:
            d = o // per
            end = min(o1, (d + 1) * per)
            runs.append((d, o - d * per, c0 + o - o0, end - o))
            o = end
    return runs


def _win_padded(g_in):
    tr = 128

    def body(g_ref, o_ref):
        o_ref[...] = jnp.zeros_like(o_ref)
        for d, s, c, w in _win_runs():
            o_ref[:, c:c + w] = g_ref[d, :, s:s + w]

    return pl.pallas_call(
        body, name="w_in_layout", grid=(D // tr,),
        in_specs=[pl.BlockSpec((N_DEV, tr, DIN // N_DEV), lambda i: (0, i, 0))],
        out_specs=pl.BlockSpec((tr, DINP), lambda i: (i, 0)),
        out_shape=jax.ShapeDtypeStruct((D, DINP), g_in.dtype),
        compiler_params=_cp(("arbitrary",)),
    )(g_in)


def _in_proj_bwd_weights(ut, dproj, tm, small):
    rows = dproj.shape[0]
    steps = rows // tm
    per = DIN // N_DEV

    def body(ut_ref, dp_ref, *rest):
        small_refs, (mine_ref, theirs_ref, total_ref, acc, stage, local_sems, send_sems, recv_sems) = rest[:9], rest[9:17]
        i = pl.program_id(0)
        start, finish = _small_sum_schedule(small_refs, total_ref, *rest[17:])
        x, y, c = _mesh_pos()

        @pl.when(i == 0)
        def _():
            acc[...] = jnp.zeros_like(acc)
            start()

        acc[...] += _mm(ut_ref[...], dp_ref[...])
        pl.when(i == steps - 1)(finish)

        def keep(slot, chip):
            return pltpu.make_async_copy(stage.at[slot], mine_ref.at[chip], local_sems.at[slot])

        def send(slot, chip):
            return pltpu.make_async_remote_copy(
                src_ref=stage.at[slot], dst_ref=theirs_ref.at[chip], send_sem=send_sems.at[slot],
                recv_sem=recv_sems.at[chip], device_id=(x, y, 1 - c), device_id_type=MESH)

        def drained(d):
            pl.when(c == d % 2)(keep(d % 2, d // 2).wait)
            pl.when(c != d % 2)(send(d % 2, d // 2).wait_send)

        @pl.when(i == steps - 1)
        def _():
            for d in range(N_DEV):
                slot, chip = d % 2, d // 2
                if d >= 2:
                    drained(d - 2)
                for owner, s, col, w in _win_runs():
                    if owner == d:
                        stage[slot, :, s:s + w] = acc[:, col:col + w]
                pl.when(c == slot)(keep(slot, chip).start)
                pl.when(c != slot)(send(slot, chip).start)
            drained(N_DEV - 2)
            drained(N_DEV - 1)
            for chip in range(4):
                send(0, chip).wait_recv()

    half = jax.ShapeDtypeStruct((4, D, per), F32)
    return pl.pallas_call(
        body, name="in_proj_bwd_weights", grid=(steps,),
        in_specs=[pl.BlockSpec((D, tm), lambda i: (0, i)), pl.BlockSpec((tm, DINP), lambda i: (i, 0))] + SMALL_SPECS,
        out_specs=[ANY, ANY, pl.BlockSpec((SMALL_ROWS, D), lambda i: (0, 0))],
        out_shape=[half, half, jax.ShapeDtypeStruct((SMALL_ROWS, D), F32)],
        scratch_shapes=[pltpu.VMEM((D, DINP), F32), pltpu.VMEM((2, D, per), F32), pltpu.SemaphoreType.DMA((2,)),
                        pltpu.SemaphoreType.DMA((2,)), pltpu.SemaphoreType.DMA((4,))] + _small_sum_scratch(),
        compiler_params=_cp(("arbitrary",), VMEM_WIDE_MB),
    )(ut, dproj, *small)


def _adamw(w, g, m, v):
    m = ADAM_B1 * m + (1.0 - ADAM_B1) * g
    v = ADAM_B2 * v + (1.0 - ADAM_B2) * jnp.square(g)
    m_hat = m / (1.0 - ADAM_B1 ** ADAM_STEP)
    v_hat = v / (1.0 - ADAM_B2 ** ADAM_STEP)
    delta = -ADAM_LR * (m_hat / (jnp.sqrt(v_hat) + ADAM_EPS) + ADAM_WD * w)
    return delta, m, v


ADAM_STEPS = 8


def _adamw_shards(items, name, jobs=None):
    jobs = jobs or _Jobs([])
    ns, nw = jobs.n, len(items)

    def body(*rest):
        ins, rest = rest[:5 * nw], rest[5 * nw:]
        job_ins, rest = rest[:ns], rest[ns:]
        outs, rest = rest[:4 * nw], rest[4 * nw:]
        start, finish = jobs.bind(job_ins, rest[:ns], rest[ns:])
        i = pl.program_id(0)
        pl.when(i == 0)(start)
        pl.when(i == ADAM_STEPS - 1)(finish)
        for k in range(nw):
            p_ref, own_ref, w_ref, m_ref, v_ref = ins[5 * k:5 * k + 5]
            g_ref, d_ref, nm_ref, nv_ref = outs[4 * k:4 * k + 4]
            g = ((p_ref[0].astype(F32) + p_ref[1].astype(F32)) + p_ref[2].astype(F32)) + own_ref[...]
            g_ref[...] = g
            d_ref[...], nm_ref[...], nv_ref[...] = _adamw(w_ref[...], g, m_ref[...], v_ref[...])

    in_specs, out_specs, out_shape, operands = [], [], [], []
    for parts, own, w, m, v in items:
        r, cdim = w.shape
        tr = r // ADAM_STEPS
        spec = pl.BlockSpec((tr, cdim), lambda i: (i, 0))
        in_specs += [pl.BlockSpec((3, tr, cdim), lambda i: (0, i, 0)), spec, spec, spec, spec]
        out_specs += [spec] * 4
        out_shape += [jax.ShapeDtypeStruct((r, cdim), F32)] * 4
        operands += [parts, own, w, m, v]
    outs = pl.pallas_call(
        body, name=name, grid=(ADAM_STEPS,),
        in_specs=in_specs + [ANY] * ns, out_specs=out_specs + [ANY] * ns, scratch_shapes=jobs.sems,
        out_shape=out_shape + jobs.out_shapes,
        compiler_params=_cp(("arbitrary",)),
    )(*operands, *jobs.inputs)
    return [outs[4 * k:4 * k + 4] for k in range(nw)], jobs.split(outs[4 * nw:])


def _adamw_small(items):
    n = len(items)

    def body(*refs):
        ins, outs = refs[:4 * n], refs[4 * n:]
        for k in range(n):
            w_ref, g_ref, m_ref, v_ref = ins[4 * k:4 * k + 4]
            d_ref, nm_ref, nv_ref = outs[3 * k:3 * k + 3]
            d_ref[...], nm_ref[...], nv_ref[...] = _adamw(w_ref[...], g_ref[...], m_ref[...], v_ref[...])

    vm = pl.BlockSpec(memory_space=pltpu.VMEM)
    shapes = [jax.ShapeDtypeStruct(w.shape, F32) for w, _, _, _ in items for _ in range(3)]
    outs = pl.pallas_call(body, name="adamw_small", in_specs=[vm] * (4 * n), out_specs=[vm] * (3 * n),
                          out_shape=shapes)(*[t for item in items for t in item])
    return [outs[3 * k:3 * k + 3] for k in range(n)]


def _pair_sums(where, mine, theirs, name):
    _, r, cdim = theirs.shape
    tr = 128 if r % 128 == 0 else r

    def body(where_ref, a_ref, b_ref, own_ref, wire_ref):
        chip = where_ref[1]
        own_ref[...] = a_ref[chip] + b_ref[chip]
        wire_ref[...] = (a_ref[...] + b_ref[...]).astype(WIRE_DTYPE)

    spec = pl.BlockSpec((4, tr, cdim), lambda i, s: (0, i, 0))
    mine_spec = spec if mine.ndim == 3 else pl.BlockSpec((None, 4, tr, cdim), lambda i, s: (s[0], 0, i, 0))
    return pl.pallas_call(
        body, name=name,
        grid_spec=pltpu.PrefetchScalarGridSpec(
            num_scalar_prefetch=1, grid=(r // tr,), in_specs=[mine_spec, spec],
            out_specs=[pl.BlockSpec((tr, cdim), lambda i, s: (i, 0)), spec]),
        out_shape=[jax.ShapeDtypeStruct((r, cdim), F32), jax.ShapeDtypeStruct(theirs.shape, WIRE_DTYPE)],
        compiler_params=_cp(("arbitrary",)))(where, mine, theirs)


PAIR_STEPS = 8


def _pair_sums_fused(where, mines, theirs, name):
    n = len(mines)

    def body(where_ref, *refs):
        chip = where_ref[1]
        for k in range(n):
            a_ref, b_ref = refs[2 * k], refs[2 * k + 1]
            own_ref, wire_ref = refs[2 * n + 2 * k], refs[2 * n + 2 * k + 1]
            own_ref[...] = a_ref[chip] + b_ref[chip]
            wire_ref[...] = (a_ref[...] + b_ref[...]).astype(WIRE_DTYPE)

    in_specs, out_specs, out_shape = [], [], []
    for p, q in zip(mines, theirs):
        _, r, cdim = q.shape
        tr = r // PAIR_STEPS
        spec = pl.BlockSpec((4, tr, cdim), lambda i, s: (0, i, 0))
        mine_spec = spec if p.ndim == 3 else pl.BlockSpec((None, 4, tr, cdim), lambda i, s: (s[0], 0, i, 0))
        in_specs += [mine_spec, spec]
        out_specs += [pl.BlockSpec((tr, cdim), lambda i, s: (i, 0)), spec]
        out_shape += [jax.ShapeDtypeStruct((r, cdim), F32), jax.ShapeDtypeStruct(q.shape, WIRE_DTYPE)]
    outs = pl.pallas_call(
        body, name=name,
        grid_spec=pltpu.PrefetchScalarGridSpec(
            num_scalar_prefetch=1, grid=(PAIR_STEPS,), in_specs=in_specs, out_specs=out_specs),
        out_shape=out_shape,
        compiler_params=_cp(("arbitrary",)))(where, *[t for p, q in zip(mines, theirs) for t in (p, q)])
    return [outs[2 * k] for k in range(n)], [outs[2 * k + 1] for k in range(n)]


def kernel(x, meta_tokens, norm_mix_w, w_in, w_gate_up, b_gate, gla_norm_w, sinks, w_out, norm_ff_w, w_ff1, w_ff2, final_norm_w, loss_target, m_meta_tokens, m_norm_mix_w, m_w_in, m_w_gate_up, m_b_gate, m_gla_norm_w, m_sinks, m_w_out, m_norm_ff_w, m_w_ff1, m_w_ff2, m_final_norm_w, v_meta_tokens, v_norm_mix_w, v_w_in, v_w_gate_up, v_b_gate, v_gla_norm_w, v_sinks, v_w_out, v_norm_ff_w, v_w_ff1, v_w_ff2, v_final_norm_w):
    seq = x.shape[1]
    rows = LEAD + seq
    tm = _row_tile(rows)
    tm_wide = WIDE_ROW_TILE if rows % WIDE_ROW_TILE == 0 else tm
    dev =4 * lax.axis_index("x") + 2 * lax.axis_index("y") + lax.axis_index("c")

    small_shard = jnp.concatenate([meta_tokens, w_gate_up[0], jnp.zeros((N_META, 96), F32)], axis=1)
    g_in, g_small = _all_gather([w_in[0].astype(WIRE_DTYPE), small_shard])
    later_shards = [w_out[0].astype(WIRE_DTYPE), w_ff1[0].astype(WIRE_DTYPE), w_ff2[0].astype(WIRE_DTYPE)]
    win_p = _win_padded(g_in)
    meta_full = jnp.transpose(g_small[:, :, 0:128], (1, 0, 2)).reshape(N_META, D)
    wg_full = jnp.transpose(g_small[:, :, 128:160], (1, 0, 2)).reshape(GLA_RANK, GLA_HEADS * GLA_DK)
    wg_p = jnp.concatenate([wg_full, jnp.zeros((128 - GLA_RANK, 256), F32)], axis=0)

    lead = jnp.concatenate([jnp.zeros((META0, D), F32), meta_full], axis=0)
    angles = _rope_angles(rows)
    proj, qr, kr, vr, (g_w1,) = _in_proj(x[0], lead, norm_mix_w, win_p, angles, tm, later_shards[1:2])
    oraw, og, states, decay, dgate, (g_out,) = _gla_fwd(proj, wg_p, b_gate, gla_norm_w, later_shards[0:1])
    osw, (g_w2,) = _swa_fwd(qr, kr, vr, sinks, later_shards[2:3])
    wout_full = g_out.reshape(D, D)
    w2_full = g_w2.reshape(D_FF, D)
    w1_full = jnp.transpose(g_w1, (1, 0, 2)).reshape(D, D_FF)
    h1, f, ft = _out_proj(x[0], lead, og, osw, wout_full, norm_ff_w, tm)
    a, dh2, dh2t, loss_p, gfn_p = _ffn_fwd(f, h1, w1_full, w2_full, loss_target[0], final_norm_w.reshape(1, D), tm)

    da, dh1, gnf_p = _ffn_bwd_act(dh2, a, w1_full, w2_full, h1, norm_ff_w, tm)
    dw1, dw2 = _ffn_bwd_weights(ft, a, da, dh2t, tm_wide)
    where = jnp.stack([lax.axis_index("c"), 2 * lax.axis_index("x") + lax.axis_index("y")]).astype(jnp.int32)
    dog, dos, dwout, theirs_ffn = _out_proj_bwd(dh1, og, osw, wout_full, tm, [dw1, dw2])
    sums_ffn, wires_ffn = _pair_sums_fused(where, [dw1, dw2], theirs_ffn, "reduce_pair_ffn")
    dsq, dsk, dsv, dsink_p, (parts_ffn, (theirs_wout,)) = _swa_bwd(
        qr, kr, vr, osw, dos, sinks, _Jobs([("chips", wires_ffn), ("sibling", [dwout])]))
    sum_wout, wire_wout = _pair_sums(where, dwout, theirs_wout, "reduce_pair_1")
    (dgq, dgk, dgv, dgr, dlr, dwg_p, dbg_p, dgnw_p), ((parts_wout,),) = _gla_bwd(
        proj, decay, dgate, oraw, states, dog, wg_p, gla_norm_w, _Jobs([("chips", [wire_wout])]))
    grad_x, dlead, dproj, ut, gnm_p = _in_proj_bwd(x[0], lead, dh1, norm_mix_w, win_p, dgv, dgr, dsq, dgq, dgk, dsk,
                                                   dsv, dlr, angles, tm)
    grad_x = grad_x[None]
    dwin_mine, dwin_theirs, total = _in_proj_bwd_weights(
        ut, dproj, tm_wide, [dlead, dwg_p, gnm_p, gnf_p, gfn_p, dbg_p, dgnw_p, loss_p, dsink_p])
    sum_win, sum_win_wire = _pair_sums(where, dwin_mine, dwin_theirs, "reduce_pair_0")

    g_meta = lax.dynamic_slice(total, (R_META, dev * 128), (N_META, 128))
    g_wg = lax.dynamic_slice(total, (R_WG, dev * 32), (GLA_RANK, 32))
    g_norm_mix, g_norm_ff = total[R_NORM_MIX:R_NORM_MIX + 1], total[R_NORM_FF:R_NORM_FF + 1]
    g_final_norm = total[R_FINAL:R_FINAL + 1]
    g_b_gate, g_gla_norm = total[R_B_GATE:R_B_GATE + 1, 0:256], total[R_GLA_NORM:R_GLA_NORM + 1, 0:128]
    g_sinks = total[R_SINKS:R_SINKS + SWA_HEADS, 0].reshape(1, SWA_HEADS)
    loss = total[R_LOSS, 0]

    ((g_wout, d_wout, nm_wout, nv_wout), (g_w1s, d_w1, nm_w1, nv_w1), (g_w2s, d_w2, nm_w2, nv_w2)), ((parts_win,),) = \
        _adamw_shards([(parts_wout, sum_wout, w_out[0], m_w_out[0], v_w_out[0]),
                       (parts_ffn[0], sums_ffn[0], w_ff1[0], m_w_ff1[0], v_w_ff1[0]),
                       (parts_ffn[1], sums_ffn[1], w_ff2[0], m_w_ff2[0], v_w_ff2[0])],
                      "adamw_w_out_ff", _Jobs([("chips", [sum_win_wire])]))
    ((g_win, d_win, nm_win, nv_win),), _ = _adamw_shards(
        [(parts_win, sum_win, w_in[0], m_w_in[0], v_w_in[0])], "adamw_w_in")

    names = ["meta", "wg", "norm_mix", "b_gate", "gla_norm", "sinks", "norm_ff", "final_norm"]
    ws = [meta_tokens, w_gate_up, norm_mix_w, b_gate, gla_norm_w, sinks, norm_ff_w, final_norm_w]
    gs = [g_meta, g_wg, g_norm_mix, g_b_gate, g_gla_norm, g_sinks, g_norm_ff, g_final_norm]
    ms = [m_meta_tokens, m_w_gate_up, m_norm_mix_w, m_b_gate, m_gla_norm_w, m_sinks, m_norm_ff_w, m_final_norm_w]
    vs = [v_meta_tokens, v_w_gate_up, v_norm_mix_w, v_b_gate, v_gla_norm_w, v_sinks, v_norm_ff_w, v_final_norm_w]
    flat = lambda t: t.reshape(-1, t.shape[-1])
    small_out = _adamw_small([(flat(w), flat(g), flat(m), flat(v)) for w, g, m, v in zip(ws, gs, ms, vs)])
    d_small = {n: small_out[k][0].reshape(ws[k].shape) for k, n in enumerate(names)}
    nm_small = {n: small_out[k][1].reshape(ws[k].shape) for k, n in enumerate(names)}
    nv_small = {n: small_out[k][2].reshape(ws[k].shape) for k, n in enumerate(names)}
    g_small_d = {n: g.reshape(ws[k].shape) for k, (n, g) in enumerate(zip(names, gs))}

    def ordered(big, small_d):
        win_v, wout_v, w1_v, w2_v = big
        return (small_d["meta"], small_d["norm_mix"], win_v[None], small_d["wg"], small_d["b_gate"],
                small_d["gla_norm"], small_d["sinks"], wout_v[None], small_d["norm_ff"], w1_v[None], w2_v[None],
                small_d["final_norm"])

    return (loss, grad_x,
            *ordered((g_win, g_wout, g_w1s, g_w2s), g_small_d),
            *ordered((d_win, d_wout, d_w1, d_w2), d_small),
            *ordered((nm_win, nm_wout, nm_w1, nm_w2), nm_small),
            *ordered((nv_win, nv_wout, nv_w1, nv_w2), nv_small))
```
